```python
import math
import jax
import jax.numpy as jnp
from jax import lax
import numpy as np

D_MODEL = 1024
BATCH = 16
SEQ = 2048
DEPTH = 1

GRID_W = 64
CTX_LEN = 256
MIX_WIDTH = D_MODEL
RET_WIDTH = MIX_WIDTH // 2
RET_HEADS = 4
RET_HEAD_DIM = RET_WIDTH // RET_HEADS
RET_CHUNK = 128
RWKV_WIDTH = MIX_WIDTH - RET_WIDTH
RWKV_HEAD_DIM = 64
RWKV_HEADS = RWKV_WIDTH // RWKV_HEAD_DIM
DECAY_LORA = 64
AAA_LORA = 64
GATE_LORA = 128
D_FF = 4 * D_MODEL
ROPE_BASE = 10000.0
NORM_EPS = 1e-6
GN_EPS = 64e-5
W_DECAY_SCALE = math.exp(-0.5)
RET_COLS = 4 * RET_WIDTH
SHIFT_COLS = 3 * RWKV_WIDTH + DECAY_LORA + AAA_LORA + GATE_LORA
IN_COLS = RET_COLS + SHIFT_COLS

kernel_name = "hybrid_retention_rwkv7_dit_layer"


def rmsnorm(x, g):
    xf = x.astype(jnp.float32)
    y = xf * lax.rsqrt(jnp.mean(xf * xf, axis=-1, keepdims=True) + NORM_EPS)
    return (y * g.astype(jnp.float32)).astype(x.dtype)


def adaln_params(cvec, w_ada, b_ada):
    m = jax.nn.silu(cvec) @ w_ada + b_ada
    return jnp.split(m, 6, axis=-1)


def modulate(h, shift, scale):
    return h * (1 + scale) + shift


def flip_t(a):
    return jnp.flip(a, axis=1)


def split_heads(t, n_heads, head_dim):
    return t.reshape(t.shape[0], t.shape[1], n_heads, head_dim).astype(jnp.float32)


def rope_tables(rows, cols):
    half = RET_HEAD_DIM // 2
    inv = jnp.power(ROPE_BASE, -jnp.arange(0, half, 2, dtype=jnp.float32) / half)
    ang_r = rows.astype(jnp.float32)[:, None] * inv[None, :]
    ang_c = cols.astype(jnp.float32)[:, None] * inv[None, :]
    return (jnp.cos(ang_r), jnp.sin(ang_r), jnp.cos(ang_c), jnp.sin(ang_c))


def rotate_block(x, cos, sin):
    x1, x2 = jnp.split(x, 2, axis=-1)
    cos = cos[None, :, None, :]
    sin = sin[None, :, None, :]
    return jnp.concatenate([x1 * cos - x2 * sin, x1 * sin + x2 * cos], axis=-1)


def apply_rope_2d(x, tables):
    cr, sr, cc, sc = tables
    half = x.shape[-1] // 2
    return jnp.concatenate([rotate_block(x[..., :half], cr, sr), rotate_block(x[..., half:], cc, sc)], axis=-1)


def retention_scan(q, k, v, log_gamma, s0, inclusive):
    bsz, t_len, n_h, _ = q.shape
    dv = v.shape[-1]
    c = RET_CHUNK
    n_chunks = t_len // c

    def chunks(a):
        return a.reshape(bsz, n_chunks, c, n_h, a.shape[-1]).transpose(1, 0, 3, 2, 4)

    idx = jnp.arange(c, dtype=jnp.float32)
    dist = idx[:, None] - idx[None, :]
    mask = (dist >= 0) if inclusive else (dist > 0)
    lg = log_gamma[:, None, None]
    intra_decay = jnp.where(mask[None], jnp.exp(lg * jnp.maximum(dist, 0.0)[None]), 0.0)
    q_decay = jnp.exp(log_gamma[:, None] * (idx + 1.0)[None, :])
    k_decay = jnp.exp(log_gamma[:, None] * (c - 1.0 - idx)[None, :])
    chunk_decay = jnp.exp(log_gamma * c)

    def body(state, inp):
        qc, kc, vc = inp
        scores = jnp.einsum('bhid,bhjd->bhij', qc, kc) * intra_decay
        out = (jnp.einsum('bhij,bhjd->bhid', scores, vc)
               + jnp.einsum('bhid,bhde->bhie', qc * q_decay[..., None], state))
        state = (state * chunk_decay[:, None, None]
                 + jnp.einsum('bhjd,bhje->bhde', kc * k_decay[..., None], vc))
        return state, out

    s_final, outs = lax.scan(body, s0, (chunks(q), chunks(k), chunks(v)))
    out = outs.transpose(1, 0, 3, 2, 4).reshape(bsz, t_len, n_h, dv)
    return out, s_final


def retention_bidir(q, k, v, log_g, s0_fwd, s0_bwd):
    o_f, s_f = retention_scan(q, k, v, log_g[0], s0_fwd, True)
    o_b, s_b = retention_scan(flip_t(q), flip_t(k), flip_t(v), log_g[1], s0_bwd, False)
    return o_f + flip_t(o_b), s_f, s_b


def head_rms(o):
    o = o * lax.rsqrt(jnp.mean(o * o, axis=-1, keepdims=True) + NORM_EPS)
    return o.reshape(o.shape[0], o.shape[1], -1)


def token_shift(p, mu):
    prev = jnp.pad(p[:, :-1], ((0, 0), (1, 0), (0, 0)))
    nxt = jnp.pad(p[:, 1:], ((0, 0), (0, 1), (0, 0)))
    return p + mu[0] * (prev - p) + mu[1] * (nxt - p)


def rwkv_prepare(p, shift_mu, w0, w_up, a0, a_up, g_up, k_k, k_a):
    p = token_shift(p, shift_mu)
    w_ = RWKV_WIDTH
    r, k, v, wl, al, gl = jnp.split(
        p, [w_, 2 * w_, 3 * w_, 3 * w_ + DECAY_LORA, 3 * w_ + DECAY_LORA + AAA_LORA], axis=-1)
    kk = split_heads(k * k_k, RWKV_HEADS, RWKV_HEAD_DIM)
    kk = kk * lax.rsqrt(jnp.sum(kk * kk, axis=-1, keepdims=True) + 1e-12)
    dirs = []
    for d in range(2):
        w = jnp.exp(-W_DECAY_SCALE * jax.nn.sigmoid((w0[d] + jnp.tanh(wl) @ w_up[d]).astype(jnp.float32)))
        a = jax.nn.sigmoid((a0[d] + al @ a_up[d]).astype(jnp.float32))
        kt = k.astype(jnp.float32) * (1.0 + (a - 1.0) * k_a.astype(jnp.float32))
        dirs.append((split_heads(w, RWKV_HEADS, RWKV_HEAD_DIM),
                     split_heads(a, RWKV_HEADS, RWKV_HEAD_DIM),
                     split_heads(kt, RWKV_HEADS, RWKV_HEAD_DIM)))
    g = jax.nn.sigmoid(gl) @ g_up
    return (split_heads(r, RWKV_HEADS, RWKV_HEAD_DIM), split_heads(v, RWKV_HEADS, RWKV_HEAD_DIM), kk, dirs, g)


def rwkv7_scan(r, w, kk, a, kt, v, s0, inclusive):
    def update(state, w_t, kk_t, a_t, kt_t, v_t):
        removed = jnp.einsum('bhvk,bhk->bhv', state, kk_t)
        return (state * w_t[:, :, None, :]
                - removed[..., None] * (kk_t * a_t)[:, :, None, :]
                + v_t[..., None] * kt_t[:, :, None, :])

    def body(state, inp):
        r_t, w_t, kk_t, a_t, kt_t, v_t = inp
        if inclusive:
            state = update(state, w_t, kk_t, a_t, kt_t, v_t)
            y = jnp.einsum('bhvk,bhk->bhv', state, r_t)
        else:
            y = jnp.einsum('bhvk,bhk->bhv', state, r_t)
            state = update(state, w_t, kk_t, a_t, kt_t, v_t)
        return state, y

    xs = tuple(jnp.moveaxis(t, 1, 0) for t in (r, w, kk, a, kt, v))
    s_final, ys = lax.scan(body, s0, xs)
    return jnp.moveaxis(ys, 0, 1), s_final


def rwkv_bidir(r, v, kk, dirs, s0_fwd, s0_bwd):
    (w_f, a_f, kt_f), (w_b, a_b, kt_b) = dirs
    y_f, s_f = rwkv7_scan(r, w_f, kk, a_f, kt_f, v, s0_fwd, True)
    y_b, s_b = rwkv7_scan(flip_t(r), flip_t(w_b), flip_t(kk), flip_t(a_b), flip_t(kt_b), flip_t(v), s0_bwd, False)
    return y_f + flip_t(y_b), s_f, s_b


def merge_heads(o_ret, g_ret, y_rw, feat, r_k, ln_w, ln_b, w_out, dtype):
    r, v, _, dirs, g_rw = feat
    ret_out = head_rms(o_ret) * jax.nn.silu(g_ret.astype(jnp.float32))
    mean = jnp.mean(y_rw, axis=-1, keepdims=True)
    var = jnp.var(y_rw, axis=-1, keepdims=True)
    y_n = ((y_rw - mean) * lax.rsqrt(var + GN_EPS)).reshape(y_rw.shape[0], y_rw.shape[1], -1)
    y_n = y_n * ln_w.astype(jnp.float32) + ln_b.astype(jnp.float32)
    kt_f = dirs[0][2]
    rk = r_k.astype(jnp.float32).reshape(RWKV_HEADS, RWKV_HEAD_DIM)
    bonus = (jnp.sum(r * kt_f * rk, axis=-1, keepdims=True) * v).reshape(y_n.shape)
    rw_out = (y_n + bonus) * g_rw.astype(jnp.float32)
    return jnp.concatenate([ret_out, rw_out], axis=-1).astype(dtype) @ w_out


def token_mixers(hx, hc, rope, w_in, log_decay, shift_mu, w0, w_up, a0, a_up, g_up,
                 k_k, k_a, r_k, ln_w, ln_b, w_out, with_ctx_out):
    bsz = hx.shape[0]
    px = hx @ w_in
    pc = hc @ w_in

    log_g = -jnp.exp(log_decay.astype(jnp.float32))
    k_scale = RET_HEAD_DIM ** -0.5

    def ret_qkvg(p):
        q, k, v, g = jnp.split(p[..., :RET_COLS], 4, axis=-1)
        return (split_heads(q, RET_HEADS, RET_HEAD_DIM), split_heads(k, RET_HEADS, RET_HEAD_DIM) * k_scale,
                split_heads(v, RET_HEADS, RET_HEAD_DIM), g)

    qc, kc, vc, gc = ret_qkvg(pc)
    qx, kx, vx, gx = ret_qkvg(px)
    qx = apply_rope_2d(qx, rope)
    kx = apply_rope_2d(kx, rope)
    zeros_ret = jnp.zeros((bsz, RET_HEADS, RET_HEAD_DIM, RET_HEAD_DIM), jnp.float32)
    oc_ret, sf_ret, sb_ret = retention_bidir(qc, kc, vc, log_g, zeros_ret, zeros_ret)
    ox_ret, _, _ = retention_bidir(qx, kx, vx, log_g, sf_ret, sb_ret)

    feat_c = rwkv_prepare(pc[..., RET_COLS:], shift_mu, w0, w_up, a0, a_up, g_up, k_k, k_a)
    feat_x = rwkv_prepare(px[..., RET_COLS:], shift_mu, w0, w_up, a0, a_up, g_up, k_k, k_a)
    zeros_rw = jnp.zeros((bsz, RWKV_HEADS, RWKV_HEAD_DIM, RWKV_HEAD_DIM), jnp.float32)
    yc_rw, sf_rw, sb_rw = rwkv_bidir(feat_c[0], feat_c[1], feat_c[2], feat_c[3], zeros_rw, zeros_rw)
    yx_rw, _, _ = rwkv_bidir(feat_x[0], feat_x[1], feat_x[2], feat_x[3], sf_rw, sb_rw)

    out_x = merge_heads(ox_ret, gx, yx_rw, feat_x, r_k, ln_w, ln_b, w_out, hx.dtype)
    out_c = merge_heads(oc_ret, gc, yc_rw, feat_c, r_k, ln_w, ln_b, w_out, hc.dtype) if with_ctx_out else None
    return out_x, out_c


def squared_relu_mlp(h, w1, b1, w2, b2):
    return jnp.square(jax.nn.relu(h @ w1 + b1)) @ w2 + b2


def _fwd_setup_inputs(seed: int = 0) -> dict:
    key = jax.random.key(seed)
    ks = jax.random.split(key, 32)
    L, D, W = DEPTH, D_MODEL, RWKV_WIDTH

    def nrm(k, shape, s):
        return jax.random.normal(k, shape, jnp.float32) * s

    base_decay = jnp.log(-jnp.log(1.0 - jnp.power(2.0, -5.0 - jnp.arange(RET_HEADS, dtype=jnp.float32))))
    return {
        'x': nrm(ks[0], (BATCH, SEQ, D), 1.0),
        'c': nrm(ks[1], (BATCH, D), 1.0),
        'ctx': nrm(ks[2], (BATCH, CTX_LEN, D), 1.0),
        'c_ctx': nrm(ks[3], (D,), 1.0),
        'w_ada': nrm(ks[4], (L, D, 6 * D), D ** -0.5),
        'b_ada': nrm(ks[5], (L, 6 * D), 0.02),
        'norm1_g': 1.0 + nrm(ks[6], (L, D), 0.02),
        'norm2_g': 1.0 + nrm(ks[7], (L, D), 0.02),
        'w_in': nrm(ks[8], (L, D, IN_COLS), D ** -0.5),
        'ret_log_decay': base_decay + nrm(ks[9], (L, 2, RET_HEADS), 0.05),
        'rwkv_shift_mu': jax.random.uniform(ks[10], (L, 2, SHIFT_COLS), jnp.float32, 0.0, 0.5),
        'rwkv_w0': jax.random.uniform(ks[11], (L, 2, W), jnp.float32, -3.0, 1.0),
        'rwkv_w_up': nrm(ks[12], (L, 2, DECAY_LORA, W), 0.5 * DECAY_LORA ** -0.5),
        'rwkv_a0': nrm(ks[13], (L, 2, W), 0.5),
        'rwkv_a_up': nrm(ks[14], (L, 2, AAA_LORA, W), 0.5 * AAA_LORA ** -0.5),
        'rwkv_g_up': nrm(ks[15], (L, GATE_LORA, W), GATE_LORA ** -0.5),
        'rwkv_k_k': 0.85 + nrm(ks[16], (L, W), 0.05),
        'rwkv_k_a': 1.0 + nrm(ks[17], (L, W), 0.05),
        'rwkv_r_k': nrm(ks[18], (L, W), 0.1),
        'rwkv_ln_w': 1.0 + nrm(ks[19], (L, W), 0.02),
        'rwkv_ln_b': nrm(ks[20], (L, W), 0.02),
        'w_out': nrm(ks[21], (L, D, D), D ** -0.5),
        'w_ff1': nrm(ks[22], (L, D, D_FF), D ** -0.5),
        'b_ff1': nrm(ks[23], (L, D_FF), 0.02),
        'w_ff2': nrm(ks[24], (L, D_FF, D), D_FF ** -0.5),
        'b_ff2': nrm(ks[25], (L, D), 0.02),
        'final_g': 1.0 + nrm(ks[26], (D,), 0.02),
    }


def _fwd_reference(x, c, ctx, c_ctx, w_ada, b_ada, norm1_g, norm2_g, w_in, ret_log_decay,
              rwkv_shift_mu, rwkv_w0, rwkv_w_up, rwkv_a0, rwkv_a_up, rwkv_g_up,
              rwkv_k_k, rwkv_k_a, rwkv_r_k, rwkv_ln_w, rwkv_ln_b, w_out,
              w_ff1, b_ff1, w_ff2, b_ff2, final_g):
    n_tokens = x.shape[1]
    ROWS = n_tokens // GRID_W
    rows = jnp.repeat(jnp.arange(ROWS), GRID_W)
    cols = jnp.tile(jnp.arange(GRID_W), ROWS)
    rope = rope_tables(rows, cols)

    h_x, h_c = x, ctx
    for l in range(DEPTH):
        with_ctx = l < DEPTH - 1
        sh1, sc1, g1, sh2, sc2, g2 = [m[:, None, :] for m in adaln_params(c, w_ada[l], b_ada[l])]
        csh1, csc1, cg1, csh2, csc2, cg2 = adaln_params(c_ctx, w_ada[l], b_ada[l])
        nx = modulate(rmsnorm(h_x, norm1_g[l]), sh1, sc1)
        nc = modulate(rmsnorm(h_c, norm1_g[l]), csh1, csc1)
        mix_x, mix_c = token_mixers(nx, nc, rope, w_in[l], ret_log_decay[l], rwkv_shift_mu[l],
                                    rwkv_w0[l], rwkv_w_up[l], rwkv_a0[l], rwkv_a_up[l], rwkv_g_up[l],
                                    rwkv_k_k[l], rwkv_k_a[l], rwkv_r_k[l], rwkv_ln_w[l], rwkv_ln_b[l],
                                    w_out[l], with_ctx)
        h_x = h_x + g1 * mix_x
        h_x = h_x + g2 * squared_relu_mlp(modulate(rmsnorm(h_x, norm2_g[l]), sh2, sc2),
                                          w_ff1[l], b_ff1[l], w_ff2[l], b_ff2[l])
        if with_ctx:
            h_c = h_c + cg1 * mix_c
            h_c = h_c + cg2 * squared_relu_mlp(modulate(rmsnorm(h_c, norm2_g[l]), csh2, csc2),
                                              w_ff1[l], b_ff1[l], w_ff2[l], b_ff2[l])
    return rmsnorm(h_x, final_g)


import jax as _jax
import jax.numpy as _jnp

TWIN_FORMAT = 'train_step'
FWD_PARAMS = ['x', 'c', 'ctx', 'c_ctx', 'w_ada', 'b_ada', 'norm1_g', 'norm2_g', 'w_in', 'ret_log_decay', 'rwkv_shift_mu', 'rwkv_w0', 'rwkv_w_up', 'rwkv_a0', 'rwkv_a_up', 'rwkv_g_up', 'rwkv_k_k', 'rwkv_k_a', 'rwkv_r_k', 'rwkv_ln_w', 'rwkv_ln_b', 'w_out', 'w_ff1', 'b_ff1', 'w_ff2', 'b_ff2', 'final_g']
TWIN_WEIGHTS = ['c_ctx', 'w_ada', 'b_ada', 'norm1_g', 'norm2_g', 'w_in', 'ret_log_decay', 'rwkv_shift_mu', 'rwkv_w0', 'rwkv_w_up', 'rwkv_a0', 'rwkv_a_up', 'rwkv_g_up', 'rwkv_k_k', 'rwkv_k_a', 'rwkv_r_k', 'rwkv_ln_w', 'rwkv_ln_b', 'w_out', 'w_ff1', 'b_ff1', 'w_ff2', 'b_ff2', 'final_g']
TWIN_DIFF_INPUT = 'x'
TWIN_INPUTS = ['x', 'c', 'ctx', 'c_ctx', 'w_ada', 'b_ada', 'norm1_g', 'norm2_g', 'w_in', 'ret_log_decay', 'rwkv_shift_mu', 'rwkv_w0', 'rwkv_w_up', 'rwkv_a0', 'rwkv_a_up', 'rwkv_g_up', 'rwkv_k_k', 'rwkv_k_a', 'rwkv_r_k', 'rwkv_ln_w', 'rwkv_ln_b', 'w_out', 'w_ff1', 'b_ff1', 'w_ff2', 'b_ff2', 'final_g', 'loss_target', 'm_c_ctx', 'm_w_ada', 'm_b_ada', 'm_norm1_g', 'm_norm2_g', 'm_w_in', 'm_ret_log_decay', 'm_rwkv_shift_mu', 'm_rwkv_w0', 'm_rwkv_w_up', 'm_rwkv_a0', 'm_rwkv_a_up', 'm_rwkv_g_up', 'm_rwkv_k_k', 'm_rwkv_k_a', 'm_rwkv_r_k', 'm_rwkv_ln_w', 'm_rwkv_ln_b', 'm_w_out', 'm_w_ff1', 'm_b_ff1', 'm_w_ff2', 'm_b_ff2', 'm_final_g', 'v_c_ctx', 'v_w_ada', 'v_b_ada', 'v_norm1_g', 'v_norm2_g', 'v_w_in', 'v_ret_log_decay', 'v_rwkv_shift_mu', 'v_rwkv_w0', 'v_rwkv_w_up', 'v_rwkv_a0', 'v_rwkv_a_up', 'v_rwkv_g_up', 'v_rwkv_k_k', 'v_rwkv_k_a', 'v_rwkv_r_k', 'v_rwkv_ln_w', 'v_rwkv_ln_b', 'v_w_out', 'v_w_ff1', 'v_b_ff1', 'v_w_ff2', 'v_b_ff2', 'v_final_g']
TWIN_OUTPUTS = ['loss', 'grad_x', 'grad_c_ctx', 'grad_w_ada', 'grad_b_ada', 'grad_norm1_g', 'grad_norm2_g', 'grad_w_in', 'grad_ret_log_decay', 'grad_rwkv_shift_mu', 'grad_rwkv_w0', 'grad_rwkv_w_up', 'grad_rwkv_a0', 'grad_rwkv_a_up', 'grad_rwkv_g_up', 'grad_rwkv_k_k', 'grad_rwkv_k_a', 'grad_rwkv_r_k', 'grad_rwkv_ln_w', 'grad_rwkv_ln_b', 'grad_w_out', 'grad_w_ff1', 'grad_b_ff1', 'grad_w_ff2', 'grad_b_ff2', 'grad_final_g', 'delta_c_ctx', 'delta_w_ada', 'delta_b_ada', 'delta_norm1_g', 'delta_norm2_g', 'delta_w_in', 'delta_ret_log_decay', 'delta_rwkv_shift_mu', 'delta_rwkv_w0', 'delta_rwkv_w_up', 'delta_rwkv_a0', 'delta_rwkv_a_up', 'delta_rwkv_g_up', 'delta_rwkv_k_k', 'delta_rwkv_k_a', 'delta_rwkv_r_k', 'delta_rwkv_ln_w', 'delta_rwkv_ln_b', 'delta_w_out', 'delta_w_ff1', 'delta_b_ff1', 'delta_w_ff2', 'delta_b_ff2', 'delta_final_g', 'new_m_c_ctx', 'new_m_w_ada', 'new_m_b_ada', 'new_m_norm1_g', 'new_m_norm2_g', 'new_m_w_in', 'new_m_ret_log_decay', 'new_m_rwkv_shift_mu', 'new_m_rwkv_w0', 'new_m_rwkv_w_up', 'new_m_rwkv_a0', 'new_m_rwkv_a_up', 'new_m_rwkv_g_up', 'new_m_rwkv_k_k', 'new_m_rwkv_k_a', 'new_m_rwkv_r_k', 'new_m_rwkv_ln_w', 'new_m_rwkv_ln_b', 'new_m_w_out', 'new_m_w_ff1', 'new_m_b_ff1', 'new_m_w_ff2', 'new_m_b_ff2', 'new_m_final_g', 'new_v_c_ctx', 'new_v_w_ada', 'new_v_b_ada', 'new_v_norm1_g', 'new_v_norm2_g', 'new_v_w_in', 'new_v_ret_log_decay', 'new_v_rwkv_shift_mu', 'new_v_rwkv_w0', 'new_v_rwkv_w_up', 'new_v_rwkv_a0', 'new_v_rwkv_a_up', 'new_v_rwkv_g_up', 'new_v_rwkv_k_k', 'new_v_rwkv_k_a', 'new_v_rwkv_r_k', 'new_v_rwkv_ln_w', 'new_v_rwkv_ln_b', 'new_v_w_out', 'new_v_w_ff1', 'new_v_b_ff1', 'new_v_w_ff2', 'new_v_b_ff2', 'new_v_final_g']
TWIN_LEAF_KINDS = {'loss': 'loss', 'grad_x': 'grad_x', 'grad_c_ctx': 'grad_w', 'grad_w_ada': 'grad_w', 'grad_b_ada': 'grad_w', 'grad_norm1_g': 'grad_w', 'grad_norm2_g': 'grad_w', 'grad_w_in': 'grad_w', 'grad_ret_log_decay': 'grad_w', 'grad_rwkv_shift_mu': 'grad_w', 'grad_rwkv_w0': 'grad_w', 'grad_rwkv_w_up': 'grad_w', 'grad_rwkv_a0': 'grad_w', 'grad_rwkv_a_up': 'grad_w', 'grad_rwkv_g_up': 'grad_w', 'grad_rwkv_k_k': 'grad_w', 'grad_rwkv_k_a': 'grad_w', 'grad_rwkv_r_k': 'grad_w', 'grad_rwkv_ln_w': 'grad_w', 'grad_rwkv_ln_b': 'grad_w', 'grad_w_out': 'grad_w', 'grad_w_ff1': 'grad_w', 'grad_b_ff1': 'grad_w', 'grad_w_ff2': 'grad_w', 'grad_b_ff2': 'grad_w', 'grad_final_g': 'grad_w', 'delta_c_ctx': 'delta_w', 'delta_w_ada': 'delta_w', 'delta_b_ada': 'delta_w', 'delta_norm1_g': 'delta_w', 'delta_norm2_g': 'delta_w', 'delta_w_in': 'delta_w', 'delta_ret_log_decay': 'delta_w', 'delta_rwkv_shift_mu': 'delta_w', 'delta_rwkv_w0': 'delta_w', 'delta_rwkv_w_up': 'delta_w', 'delta_rwkv_a0': 'delta_w', 'delta_rwkv_a_up': 'delta_w', 'delta_rwkv_g_up': 'delta_w', 'delta_rwkv_k_k': 'delta_w', 'delta_rwkv_k_a': 'delta_w', 'delta_rwkv_r_k': 'delta_w', 'delta_rwkv_ln_w': 'delta_w', 'delta_rwkv_ln_b': 'delta_w', 'delta_w_out': 'delta_w', 'delta_w_ff1': 'delta_w', 'delta_b_ff1': 'delta_w', 'delta_w_ff2': 'delta_w', 'delta_b_ff2': 'delta_w', 'delta_final_g': 'delta_w', 'new_m_c_ctx': 'new_m', 'new_m_w_ada': 'new_m', 'new_m_b_ada': 'new_m', 'new_m_norm1_g': 'new_m', 'new_m_norm2_g': 'new_m', 'new_m_w_in': 'new_m', 'new_m_ret_log_decay': 'new_m', 'new_m_rwkv_shift_mu': 'new_m', 'new_m_rwkv_w0': 'new_m', 'new_m_rwkv_w_up': 'new_m', 'new_m_rwkv_a0': 'new_m', 'new_m_rwkv_a_up': 'new_m', 'new_m_rwkv_g_up': 'new_m', 'new_m_rwkv_k_k': 'new_m', 'new_m_rwkv_k_a': 'new_m', 'new_m_rwkv_r_k': 'new_m', 'new_m_rwkv_ln_w': 'new_m', 'new_m_rwkv_ln_b': 'new_m', 'new_m_w_out': 'new_m', 'new_m_w_ff1': 'new_m', 'new_m_b_ff1': 'new_m', 'new_m_w_ff2': 'new_m', 'new_m_b_ff2': 'new_m', 'new_m_final_g': 'new_m', 'new_v_c_ctx': 'new_v', 'new_v_w_ada': 'new_v', 'new_v_b_ada': 'new_v', 'new_v_norm1_g': 'new_v', 'new_v_norm2_g': 'new_v', 'new_v_w_in': 'new_v', 'new_v_ret_log_decay': 'new_v', 'new_v_rwkv_shift_mu': 'new_v', 'new_v_rwkv_w0': 'new_v', 'new_v_rwkv_w_up': 'new_v', 'new_v_rwkv_a0': 'new_v', 'new_v_rwkv_a_up': 'new_v', 'new_v_rwkv_g_up': 'new_v', 'new_v_rwkv_k_k': 'new_v', 'new_v_rwkv_k_a': 'new_v', 'new_v_rwkv_r_k': 'new_v', 'new_v_rwkv_ln_w': 'new_v', 'new_v_rwkv_ln_b': 'new_v', 'new_v_w_out': 'new_v', 'new_v_w_ff1': 'new_v', 'new_v_b_ff1': 'new_v', 'new_v_w_ff2': 'new_v', 'new_v_b_ff2': 'new_v', 'new_v_final_g': 'new_v'}


def _forward(args):
    return _fwd_reference(*[args[k] for k in FWD_PARAMS])


def _output_shape():
    out = _jax.eval_shape(lambda: _forward(_fwd_setup_inputs(0)))
    return out.shape, out.dtype

N_MICROBATCH = 1
ADAM_LR = 0.001
ADAM_B1 = 0.9
ADAM_B2 = 0.999
ADAM_EPS = 1e-08
ADAM_WD = 0.01
ADAM_STEP = 10
PER_EXAMPLE_BATCH_AXIS = {'x': 0, 'c': 0, 'ctx': 0, 'loss_target': 0}
SHARED_INPUTS = []
_WEIGHT_DTYPES = {'c_ctx': _jnp.float32, 'w_ada': _jnp.float32, 'b_ada': _jnp.float32, 'norm1_g': _jnp.float32, 'norm2_g': _jnp.float32, 'w_in': _jnp.float32, 'ret_log_decay': _jnp.float32, 'rwkv_shift_mu': _jnp.float32, 'rwkv_w0': _jnp.float32, 'rwkv_w_up': _jnp.float32, 'rwkv_a0': _jnp.float32, 'rwkv_a_up': _jnp.float32, 'rwkv_g_up': _jnp.float32, 'rwkv_k_k': _jnp.float32, 'rwkv_k_a': _jnp.float32, 'rwkv_r_k': _jnp.float32, 'rwkv_ln_w': _jnp.float32, 'rwkv_ln_b': _jnp.float32, 'w_out': _jnp.float32, 'w_ff1': _jnp.float32, 'b_ff1': _jnp.float32, 'w_ff2': _jnp.float32, 'b_ff2': _jnp.float32, 'final_g': _jnp.float32}
MOMENT_SCALE = {'c_ctx': 4.589727e-02, 'w_ada': 1.392536e-01, 'b_ada': 2.325898e-01, 'norm1_g': 1.274097e-01, 'norm2_g': 1.382546e-01, 'w_in': 9.622759e-02, 'ret_log_decay': 2.622666e-01, 'rwkv_shift_mu': 9.411213e-02, 'rwkv_w0': 2.976521e-02, 'rwkv_w_up': 1.230758e-02, 'rwkv_a0': 1.924888e-02, 'rwkv_a_up': 1.791989e-02, 'rwkv_g_up': 6.767507e-02, 'rwkv_k_k': 1.317650e-01, 'rwkv_k_a': 1.651980e-01, 'rwkv_r_k': 2.477601e-01, 'rwkv_ln_w': 6.752760e-02, 'rwkv_ln_b': 6.846554e-02, 'w_out': 7.394511e-02, 'w_ff1': 8.697312e-02, 'b_ff1': 9.275829e-02, 'w_ff2': 1.807443e-01, 'b_ff2': 1.352595e-01, 'final_g': 3.569578e+01}


def _to_microbatches(a, axis):
    t = _jnp.moveaxis(a, axis, 0)
    t = t.reshape((N_MICROBATCH, t.shape[0] // N_MICROBATCH) + t.shape[1:])
    return _jnp.moveaxis(t, 1, axis + 1)


def setup_inputs(seed: int = 0) -> dict:
    inp = _fwd_setup_inputs(seed)
    key = _jax.random.fold_in(_jax.random.key(seed), 7919)
    shape, _ = _output_shape()
    out = dict(inp)
    out["loss_target"] = _jax.random.normal(_jax.random.fold_in(key, 0), shape, _jnp.float32)
    for i, name in enumerate(TWIN_WEIGHTS):
        w = inp[name].astype(_jnp.float32)
        if MOMENT_SCALE is None:
            s = _jnp.sqrt(_jnp.mean(_jnp.square(w)) + 1e-30)
        else:
            s = MOMENT_SCALE[name]
        km, kv = _jax.random.split(_jax.random.fold_in(key, i + 1))
        out[name] = w
        out["m_" + name] = s * _jax.random.normal(km, w.shape, _jnp.float32)
        out["v_" + name] = (s * s) * _jax.random.uniform(kv, w.shape, _jnp.float32, 0.5, 1.5)
    if N_MICROBATCH > 1:
        for name, axis in PER_EXAMPLE_BATCH_AXIS.items():
            out[name] = _to_microbatches(out[name], axis)
    return {'x': out['x'], 'c': out['c'], 'ctx': out['ctx'], 'c_ctx': out['c_ctx'], 'w_ada': out['w_ada'], 'b_ada': out['b_ada'], 'norm1_g': out['norm1_g'], 'norm2_g': out['norm2_g'], 'w_in': out['w_in'], 'ret_log_decay': out['ret_log_decay'], 'rwkv_shift_mu': out['rwkv_shift_mu'], 'rwkv_w0': out['rwkv_w0'], 'rwkv_w_up': out['rwkv_w_up'], 'rwkv_a0': out['rwkv_a0'], 'rwkv_a_up': out['rwkv_a_up'], 'rwkv_g_up': out['rwkv_g_up'], 'rwkv_k_k': out['rwkv_k_k'], 'rwkv_k_a': out['rwkv_k_a'], 'rwkv_r_k': out['rwkv_r_k'], 'rwkv_ln_w': out['rwkv_ln_w'], 'rwkv_ln_b': out['rwkv_ln_b'], 'w_out': out['w_out'], 'w_ff1': out['w_ff1'], 'b_ff1': out['b_ff1'], 'w_ff2': out['w_ff2'], 'b_ff2': out['b_ff2'], 'final_g': out['final_g'], 'loss_target': out['loss_target'], 'm_c_ctx': out['m_c_ctx'], 'm_w_ada': out['m_w_ada'], 'm_b_ada': out['m_b_ada'], 'm_norm1_g': out['m_norm1_g'], 'm_norm2_g': out['m_norm2_g'], 'm_w_in': out['m_w_in'], 'm_ret_log_decay': out['m_ret_log_decay'], 'm_rwkv_shift_mu': out['m_rwkv_shift_mu'], 'm_rwkv_w0': out['m_rwkv_w0'], 'm_rwkv_w_up': out['m_rwkv_w_up'], 'm_rwkv_a0': out['m_rwkv_a0'], 'm_rwkv_a_up': out['m_rwkv_a_up'], 'm_rwkv_g_up': out['m_rwkv_g_up'], 'm_rwkv_k_k': out['m_rwkv_k_k'], 'm_rwkv_k_a': out['m_rwkv_k_a'], 'm_rwkv_r_k': out['m_rwkv_r_k'], 'm_rwkv_ln_w': out['m_rwkv_ln_w'], 'm_rwkv_ln_b': out['m_rwkv_ln_b'], 'm_w_out': out['m_w_out'], 'm_w_ff1': out['m_w_ff1'], 'm_b_ff1': out['m_b_ff1'], 'm_w_ff2': out['m_w_ff2'], 'm_b_ff2': out['m_b_ff2'], 'm_final_g': out['m_final_g'], 'v_c_ctx': out['v_c_ctx'], 'v_w_ada': out['v_w_ada'], 'v_b_ada': out['v_b_ada'], 'v_norm1_g': out['v_norm1_g'], 'v_norm2_g': out['v_norm2_g'], 'v_w_in': out['v_w_in'], 'v_ret_log_decay': out['v_ret_log_decay'], 'v_rwkv_shift_mu': out['v_rwkv_shift_mu'], 'v_rwkv_w0': out['v_rwkv_w0'], 'v_rwkv_w_up': out['v_rwkv_w_up'], 'v_rwkv_a0': out['v_rwkv_a0'], 'v_rwkv_a_up': out['v_rwkv_a_up'], 'v_rwkv_g_up': out['v_rwkv_g_up'], 'v_rwkv_k_k': out['v_rwkv_k_k'], 'v_rwkv_k_a': out['v_rwkv_k_a'], 'v_rwkv_r_k': out['v_rwkv_r_k'], 'v_rwkv_ln_w': out['v_rwkv_ln_w'], 'v_rwkv_ln_b': out['v_rwkv_ln_b'], 'v_w_out': out['v_w_out'], 'v_w_ff1': out['v_w_ff1'], 'v_b_ff1': out['v_b_ff1'], 'v_w_ff2': out['v_w_ff2'], 'v_b_ff2': out['v_b_ff2'], 'v_final_g': out['v_final_g']}


def _loss(weights, diff, rest, loss_target):
    with _jax.named_scope("forward"):
        args = {**rest, TWIN_DIFF_INPUT: diff, **{k: w.astype(_WEIGHT_DTYPES[k]) for k, w in weights.items()}}
        y = _forward(args)
    with _jax.named_scope("loss_head"):
        err = _jnp.square(y.astype(_jnp.float32) - loss_target)
        return 0.5 * _jnp.sum(_jnp.mean(err, axis=-1)) if err.ndim else 0.5 * err


def _adamw(w, g, m, v):
    m = ADAM_B1 * m + (1.0 - ADAM_B1) * g
    v = ADAM_B2 * v + (1.0 - ADAM_B2) * _jnp.square(g)
    m_hat = m / (1.0 - ADAM_B1 ** ADAM_STEP)
    v_hat = v / (1.0 - ADAM_B2 ** ADAM_STEP)
    delta = -ADAM_LR * (m_hat / (_jnp.sqrt(v_hat) + ADAM_EPS) + ADAM_WD * w)
    return delta, m, v


def reference(x, c, ctx, c_ctx, w_ada, b_ada, norm1_g, norm2_g, w_in, ret_log_decay, rwkv_shift_mu, rwkv_w0, rwkv_w_up, rwkv_a0, rwkv_a_up, rwkv_g_up, rwkv_k_k, rwkv_k_a, rwkv_r_k, rwkv_ln_w, rwkv_ln_b, w_out, w_ff1, b_ff1, w_ff2, b_ff2, final_g, loss_target, m_c_ctx, m_w_ada, m_b_ada, m_norm1_g, m_norm2_g, m_w_in, m_ret_log_decay, m_rwkv_shift_mu, m_rwkv_w0, m_rwkv_w_up, m_rwkv_a0, m_rwkv_a_up, m_rwkv_g_up, m_rwkv_k_k, m_rwkv_k_a, m_rwkv_r_k, m_rwkv_ln_w, m_rwkv_ln_b, m_w_out, m_w_ff1, m_b_ff1, m_w_ff2, m_b_ff2, m_final_g, v_c_ctx, v_w_ada, v_b_ada, v_norm1_g, v_norm2_g, v_w_in, v_ret_log_decay, v_rwkv_shift_mu, v_rwkv_w0, v_rwkv_w_up, v_rwkv_a0, v_rwkv_a_up, v_rwkv_g_up, v_rwkv_k_k, v_rwkv_k_a, v_rwkv_r_k, v_rwkv_ln_w, v_rwkv_ln_b, v_w_out, v_w_ff1, v_b_ff1, v_w_ff2, v_b_ff2, v_final_g):
    given = dict(x=x, c=c, ctx=ctx, c_ctx=c_ctx, w_ada=w_ada, b_ada=b_ada, norm1_g=norm1_g, norm2_g=norm2_g, w_in=w_in, ret_log_decay=ret_log_decay, rwkv_shift_mu=rwkv_shift_mu, rwkv_w0=rwkv_w0, rwkv_w_up=rwkv_w_up, rwkv_a0=rwkv_a0, rwkv_a_up=rwkv_a_up, rwkv_g_up=rwkv_g_up, rwkv_k_k=rwkv_k_k, rwkv_k_a=rwkv_k_a, rwkv_r_k=rwkv_r_k, rwkv_ln_w=rwkv_ln_w, rwkv_ln_b=rwkv_ln_b, w_out=w_out, w_ff1=w_ff1, b_ff1=b_ff1, w_ff2=w_ff2, b_ff2=b_ff2, final_g=final_g, loss_target=loss_target, m_c_ctx=m_c_ctx, m_w_ada=m_w_ada, m_b_ada=m_b_ada, m_norm1_g=m_norm1_g, m_norm2_g=m_norm2_g, m_w_in=m_w_in, m_ret_log_decay=m_ret_log_decay, m_rwkv_shift_mu=m_rwkv_shift_mu, m_rwkv_w0=m_rwkv_w0, m_rwkv_w_up=m_rwkv_w_up, m_rwkv_a0=m_rwkv_a0, m_rwkv_a_up=m_rwkv_a_up, m_rwkv_g_up=m_rwkv_g_up, m_rwkv_k_k=m_rwkv_k_k, m_rwkv_k_a=m_rwkv_k_a, m_rwkv_r_k=m_rwkv_r_k, m_rwkv_ln_w=m_rwkv_ln_w, m_rwkv_ln_b=m_rwkv_ln_b, m_w_out=m_w_out, m_w_ff1=m_w_ff1, m_b_ff1=m_b_ff1, m_w_ff2=m_w_ff2, m_b_ff2=m_b_ff2, m_final_g=m_final_g, v_c_ctx=v_c_ctx, v_w_ada=v_w_ada, v_b_ada=v_b_ada, v_norm1_g=v_norm1_g, v_norm2_g=v_norm2_g, v_w_in=v_w_in, v_ret_log_decay=v_ret_log_decay, v_rwkv_shift_mu=v_rwkv_shift_mu, v_rwkv_w0=v_rwkv_w0, v_rwkv_w_up=v_rwkv_w_up, v_rwkv_a0=v_rwkv_a0, v_rwkv_a_up=v_rwkv_a_up, v_rwkv_g_up=v_rwkv_g_up, v_rwkv_k_k=v_rwkv_k_k, v_rwkv_k_a=v_rwkv_k_a, v_rwkv_r_k=v_rwkv_r_k, v_rwkv_ln_w=v_rwkv_ln_w, v_rwkv_ln_b=v_rwkv_ln_b, v_w_out=v_w_out, v_w_ff1=v_w_ff1, v_b_ff1=v_b_ff1, v_w_ff2=v_w_ff2, v_b_ff2=v_b_ff2, v_final_g=v_final_g)
    weights = {n: given[n] for n in TWIN_WEIGHTS}
    shared = {n: given[n] for n in SHARED_INPUTS}
    per_example = {n: given[n] for n in ['x', 'c', 'ctx']}
    grad_fn = _jax.value_and_grad(_loss, argnums=(0, 1))

    def one_microbatch(ex, loss_target):
        ex = dict(ex)
        diff = ex.pop(TWIN_DIFF_INPUT)
        return grad_fn(weights, diff, {**shared, **ex}, loss_target)

    if N_MICROBATCH == 1:
        loss, (grad_w, grad_x) = one_microbatch(per_example, given["loss_target"])
    else:
        def body(carry, xs):
            loss_sum, grad_sum = carry
            l_k, (gw_k, gx_k) = one_microbatch(xs[0], xs[1])
            with _jax.named_scope("update"):
                return (loss_sum + l_k, _jax.tree.map(_jnp.add, grad_sum, gw_k)), gx_k

        init = (_jnp.zeros((), _jnp.float32), _jax.tree.map(_jnp.zeros_like, weights))
        (loss, grad_w), grad_x = _jax.lax.scan(body, init, (per_example, given["loss_target"]))
    with _jax.named_scope("update"):
        delta_w, new_m, new_v = {}, {}, {}
        for n in TWIN_WEIGHTS:
            delta_w[n], new_m[n], new_v[n] = _adamw(weights[n], grad_w[n], given["m_" + n], given["v_" + n])
    return (loss, grad_x, *[grad_w[n] for n in TWIN_WEIGHTS], *[delta_w[n] for n in TWIN_WEIGHTS],
            *[new_m[n] for n in TWIN_WEIGHTS], *[new_v[n] for n in TWIN_WEIGHTS])
```

```python
import functools
import math

import jax
import jax.numpy as jnp
from jax import lax
from jax.experimental import pallas as pl
from jax.experimental.pallas import tpu as pltpu

f32 = jnp.float32
MXU_DTYPE = jnp.bfloat16

D_MODEL = 1024
RET_W = 512
RET_HEADS = 4
RET_DH = 128
RET_CHUNK = 128
RW_W = 512
RW_N = 64
DECAY_LORA = 64
AAA_LORA = 64
GATE_LORA = 128
LORA_W = DECAY_LORA + AAA_LORA + GATE_LORA
D_FF = 4096
RET_COLS = 4 * RET_W
SHIFT_COLS = 3 * RW_W + LORA_W
IN_COLS = RET_COLS + SHIFT_COLS
GRID_W = 64
ROPE_BASE = 10000.0
NORM_EPS = 1e-6
GN_EPS = 64e-5
W_DECAY_SCALE = math.exp(-0.5)
ADAM_LR, ADAM_B1, ADAM_B2, ADAM_EPS, ADAM_WD, ADAM_STEP = 0.001, 0.9, 0.999, 1e-08, 0.01, 10

TOK_TILE = 256
SCAN_CHUNK = 16
N_DEV = 8
V7X_VMEM_BYTES = 64 * 1024 * 1024
VMEM_LIMIT = V7X_VMEM_BYTES * 7 // 8


def _cparams(sem):
    return pltpu.CompilerParams(dimension_semantics=sem, vmem_limit_bytes=VMEM_LIMIT)


def _tile(n, cap):
    best = None
    for t in range(128, min(n, cap) + 1, 128):
        if n % t == 0:
            best = t
    return best if best is not None else n


def matmul(a, b, mode, name, out_dtype=f32):
    if mode == "nn":
        (m, k), (k2, n) = a.shape, b.shape
    elif mode == "nt":
        (m, k), (n, k2) = a.shape, b.shape
    else:
        (k, m), (k2, n) = a.shape, b.shape
    assert k == k2, (a.shape, b.shape, mode)
    tm, tn, tk = _tile(m, 512), _tile(n, 768), _tile(k, 1024)
    nk = k // tk
    dims = {"nn": ((1,), (0,)), "nt": ((1,), (1,)), "tn": ((0,), (0,))}[mode]

    def body(a_ref, b_ref, o_ref, acc_ref):
        kk = pl.program_id(2)

        @pl.when(kk == 0)
        def _():
            acc_ref[...] = jnp.zeros_like(acc_ref)

        acc_ref[...] += lax.dot_general(a_ref[...].astype(MXU_DTYPE), b_ref[...].astype(MXU_DTYPE),
                                        (dims, ((), ())), preferred_element_type=f32)

        @pl.when(kk == nk - 1)
        def _():
            o_ref[...] = acc_ref[...].astype(o_ref.dtype)

    if mode == "nn":
        a_spec = pl.BlockSpec((tm, tk), lambda i, j, q: (i, q))
        b_spec = pl.BlockSpec((tk, tn), lambda i, j, q: (q, j))
    elif mode == "nt":
        a_spec = pl.BlockSpec((tm, tk), lambda i, j, q: (i, q))
        b_spec = pl.BlockSpec((tn, tk), lambda i, j, q: (j, q))
    else:
        a_spec = pl.BlockSpec((tk, tm), lambda i, j, q: (q, i))
        b_spec = pl.BlockSpec((tk, tn), lambda i, j, q: (q, j))
    return pl.pallas_call(
        body, grid=(m // tm, n // tn, nk), in_specs=[a_spec, b_spec],
        out_specs=pl.BlockSpec((tm, tn), lambda i, j, q: (i, j)),
        out_shape=jax.ShapeDtypeStruct((m, n), out_dtype),
        scratch_shapes=[pltpu.VMEM((tm, tn), f32)],
        compiler_params=_cparams(("parallel", "parallel", "arbitrary")), name=name)(a, b)


class Tiled:
    def __init__(self, arr, w=None, cidx=0, toff=0):
        self.arr, self.w, self.cidx, self.toff = arr, (arr.shape[-1] if w is None else w), cidx, toff

    def spec(self):
        cidx, toff = self.cidx, self.toff
        return pl.BlockSpec((None, TOK_TILE, self.w), lambda b, i: (b, i + toff, cidx))


class Seg:
    def __init__(self, arr, seg, first):
        self.arr, self.seg, self.first = arr, seg, first

    def spec(self):
        seg = self.seg
        return pl.BlockSpec((None, None, 1, self.arr.shape[-1]), lambda b, i: (b, seg(i), 0, 0))


class Glob:
    def __init__(self, arr):
        self.arr = arr

    def spec(self):
        return pl.BlockSpec(self.arr.shape, lambda b, i: (0,) * self.arr.ndim)


def ew_forward(fn, name, bsz, n_tiles, ins, outs):
    n_in = len(ins)

    def body(*refs):
        res = fn(*[r[...] for r in refs[:n_in]])
        for o_ref, o in zip(refs[n_in:], res):
            o_ref[...] = o.astype(o_ref.dtype)

    out_specs = [pl.BlockSpec((None, TOK_TILE, w), lambda b, i: (b, i, 0)) for w, _ in outs]
    out_shape = [jax.ShapeDtypeStruct((bsz, n_tiles * TOK_TILE, w), dt) for w, dt in outs]
    return pl.pallas_call(body, grid=(bsz, n_tiles), in_specs=[d.spec() for d in ins], out_specs=out_specs,
                          out_shape=out_shape, compiler_params=_cparams(("parallel", "parallel")), name=name)(
        *[d.arr for d in ins])


def ew_backward(fn, name, bsz, n_tiles, ins, cts, want, grad_dtypes=None):
    n_in, n_ct = len(ins), len(cts)
    diff = [k for k in range(n_in) if want[k]]
    grad_dtypes = grad_dtypes or {}

    def body(*refs):
        b, i = pl.program_id(0), pl.program_id(1)
        vals = [r[...] for r in refs[:n_in]]
        ct_vals = tuple(r[...].astype(f32) for r in refs[n_in:n_in + n_ct])
        g_refs = refs[n_in + n_ct:]

        def f(*dvals):
            full = list(vals)
            for k, v in zip(diff, dvals):
                full[k] = v
            return tuple(fn(*full))

        _, vjp = jax.vjp(f, *[vals[k] for k in diff])
        grads = vjp(ct_vals)
        for k, g_ref, g in zip(diff, g_refs, grads):
            d = ins[k]
            if isinstance(d, Tiled):
                g_ref[...] = g.astype(g_ref.dtype)
            else:
                zero = d.first(i) if isinstance(d, Seg) else jnp.logical_and(b == 0, i == 0)

                @pl.when(zero)
                def _(g_ref=g_ref):
                    g_ref[...] = jnp.zeros_like(g_ref)

                g_ref[...] += g

    out_specs, out_shape = [], []
    for k in diff:
        d = ins[k]
        if isinstance(d, Tiled):
            out_specs.append(pl.BlockSpec((None, TOK_TILE, d.w), lambda b, i: (b, i, 0)))
            out_shape.append(jax.ShapeDtypeStruct((bsz, n_tiles * TOK_TILE, d.w), grad_dtypes.get(k, f32)))
        else:
            out_specs.append(d.spec())
            out_shape.append(jax.ShapeDtypeStruct(d.arr.shape, f32))
    return pl.pallas_call(body, grid=(bsz, n_tiles), in_specs=[d.spec() for d in ins] + [c.spec() for c in cts],
                          out_specs=out_specs, out_shape=out_shape,
                          compiler_params=_cparams(("arbitrary", "arbitrary")), name=name)(
        *[d.arr for d in ins], *[c.arr for c in cts])


@jax.custom_vjp
def _mxu_dot(a, b):
    return jnp.dot(a.astype(MXU_DTYPE), b.astype(MXU_DTYPE), preferred_element_type=f32)


def _mxu_dot_fwd(a, b):
    return _mxu_dot(a, b), (a, b)


def _mxu_dot_bwd(res, ct):
    a, b = res
    ct = ct.astype(MXU_DTYPE)
    da = lax.dot_general(ct, b.astype(MXU_DTYPE), (((1,), (1,)), ((), ())), preferred_element_type=f32)
    db = lax.dot_general(a.astype(MXU_DTYPE), ct, (((0,), (0,)), ((), ())), preferred_element_type=f32)
    return da, db


_mxu_dot.defvjp(_mxu_dot_fwd, _mxu_dot_bwd)


def _split_dot_impl(x, ones_mat):
    hi = x.astype(MXU_DTYPE)
    lo = (x - hi.astype(f32)).astype(MXU_DTYPE)
    return jnp.dot(hi, ones_mat, preferred_element_type=f32) + jnp.dot(lo, ones_mat, preferred_element_type=f32)


@jax.custom_vjp
def _split_dot(x, ones_mat):
    return _split_dot_impl(x, ones_mat)


def _split_dot_fwd(x, ones_mat):
    return _split_dot_impl(x, ones_mat), ones_mat


def _split_dot_bwd(ones_mat, ct):
    return _split_dot_impl(ct, ones_mat), None


_split_dot.defvjp(_split_dot_fwd, _split_dot_bwd)


def _block_ones(n, group):
    idx = jnp.arange(n) // group
    return (idx[:, None] == idx[None, :]).astype(MXU_DTYPE)


def _rms(x, g):
    return x * lax.rsqrt(jnp.mean(x * x, axis=-1, keepdims=True) + NORM_EPS) * g


def fn_norm_mod(h, shift, scale, g):
    return (_rms(h, g) * (1.0 + scale) + shift,)


def fn_rwkv_prepare(ks, lora, w0_f, w0_b, a0_f, a0_b, w_up_f, w_up_b, a_up_f, a_up_b, g_up, k_k, k_a, ones64):
    kkr = ks * k_k
    kk = kkr * lax.rsqrt(_split_dot(kkr * kkr, ones64) + 1e-12)
    outs = [kk]
    th = jnp.tanh(lora)
    for w0, a0, w_up, a_up in ((w0_f, a0_f, w_up_f, a_up_f), (w0_b, a0_b, w_up_b, a_up_b)):
        w = jnp.exp(-W_DECAY_SCALE * jax.nn.sigmoid(w0 + _mxu_dot(th, w_up)))
        a = jax.nn.sigmoid(a0 + _mxu_dot(lora, a_up))
        kt = ks * (1.0 + (a - 1.0) * k_a)
        outs += [w, a * kk, kt]
    outs.append(_mxu_dot(jax.nn.sigmoid(lora), g_up))
    return tuple(outs)


def fn_merge(o_f, o_b, g_ret, y_f, y_b, r, kt_f, v, g_rw, r_k, ln_w, ln_b, ones64, ones128):
    o = o_f + o_b
    ret = o * lax.rsqrt(_split_dot(o * o, ones128) * (1.0 / RET_DH) + NORM_EPS) * (g_ret * jax.nn.sigmoid(g_ret))
    y = y_f + y_b
    mean = _split_dot(y, ones64) * (1.0 / RW_N)
    yc = y - mean
    var = _split_dot(yc * yc, ones64) * (1.0 / RW_N)
    y_n = yc * lax.rsqrt(var + GN_EPS) * ln_w + ln_b
    bonus = _split_dot(r * kt_f * r_k, ones64) * v
    return ret, (y_n + bonus) * g_rw


def fn_resid_norm_mod(x, mix, gate, shift, scale, g):
    h1 = x + gate * mix
    return h1, _rms(h1, g) * (1.0 + scale) + shift


def fn_relu2(u, b1):
    return (jnp.square(jnp.maximum(u + b1, 0.0)),)


def fn_loss(h1, f, tgt, gate, b2, g):
    y = _rms(h1 + gate * (f + b2), g)
    err = jnp.square(y - tgt)
    return 0.5 * jnp.sum(jnp.mean(err, axis=-1, keepdims=True), axis=0, keepdims=True)


def loss_and_grads(h1, f, tgt, gate, b2, g, bsz, n_tiles):
    def body(h1_ref, f_ref, t_ref, gate_ref, b2_ref, g_ref, loss_ref, dh1_ref, df_ref, dgate_ref, db2_ref, dg_ref):
        b, i = pl.program_id(0), pl.program_id(1)
        tgt_v = t_ref[...]
        loss, vjp = jax.vjp(lambda a, c, e, p, q: fn_loss(a, c, tgt_v, e, p, q),
                            h1_ref[...], f_ref[...], gate_ref[...], b2_ref[...], g_ref[...])
        dh1, df, dgate, db2, dg = vjp(jnp.ones((1, 1), f32))
        dh1_ref[...] = dh1
        df_ref[...] = df.astype(df_ref.dtype)

        @pl.when(i == 0)
        def _():
            dgate_ref[...] = jnp.zeros_like(dgate_ref)

        @pl.when(jnp.logical_and(b == 0, i == 0))
        def _():
            loss_ref[...] = jnp.zeros_like(loss_ref)
            db2_ref[...] = jnp.zeros_like(db2_ref)
            dg_ref[...] = jnp.zeros_like(dg_ref)

        dgate_ref[...] += dgate
        db2_ref[...] += db2
        dg_ref[...] += dg
        loss_ref[...] += jnp.broadcast_to(loss, loss_ref.shape)

    tile = pl.BlockSpec((None, TOK_TILE, D_MODEL), lambda b, i: (b, i, 0))
    row = pl.BlockSpec((1, D_MODEL), lambda b, i: (0, 0))
    seg = pl.BlockSpec((None, None, 1, D_MODEL), lambda b, i: (b, 0, 0, 0))
    t_tok = n_tiles * TOK_TILE
    return pl.pallas_call(
        body, grid=(bsz, n_tiles), in_specs=[tile, tile, tile, seg, row, row],
        out_specs=[pl.BlockSpec((1, 128), lambda b, i: (0, 0)), tile, tile, seg, row, row],
        out_shape=[jax.ShapeDtypeStruct((1, 128), f32), jax.ShapeDtypeStruct((bsz, t_tok, D_MODEL), f32),
                   jax.ShapeDtypeStruct((bsz, t_tok, D_MODEL), MXU_DTYPE),
                   jax.ShapeDtypeStruct((bsz, 1, 1, D_MODEL), f32),
                   jax.ShapeDtypeStruct((1, D_MODEL), f32), jax.ShapeDtypeStruct((1, D_MODEL), f32)],
        compiler_params=_cparams(("arbitrary", "arbitrary")), name="loss_and_grads")(h1, f, tgt, gate, b2, g)


SHIFT_BLOCK = 256
HALO_ROWS = 8


def _shift_specs(n_tok, col0):
    per_tile = TOK_TILE // HALO_ROWS
    last = n_tok // HALO_ROWS - 1
    tile = pl.BlockSpec((None, TOK_TILE, SHIFT_BLOCK), lambda j, b, i: (b, i, col0 + j))
    prev = pl.BlockSpec((None, HALO_ROWS, SHIFT_BLOCK),
                        lambda j, b, i: (b, jnp.maximum(i * per_tile - 1, 0), col0 + j))
    nxt = pl.BlockSpec((None, HALO_ROWS, SHIFT_BLOCK),
                       lambda j, b, i: (b, jnp.minimum((i + 1) * per_tile, last), col0 + j))
    return tile, prev, nxt


def _shifted(p, prev_ref, next_ref, is_first, is_last):
    row = lax.broadcasted_iota(jnp.int32, p.shape, 0)
    prev_row = jnp.where(is_first, 0.0, prev_ref[HALO_ROWS - 1:HALO_ROWS, :].astype(f32))
    next_row = jnp.where(is_last, 0.0, next_ref[0:1, :].astype(f32))
    prev = jnp.where(row == 0, prev_row, pltpu.roll(p, 1, axis=0))
    nxt = jnp.where(row == TOK_TILE - 1, next_row, pltpu.roll(p, TOK_TILE - 1, axis=0))
    return prev, nxt


def token_shift(px, mu, seg_first, seg_last):
    bsz, n_tok, _ = px.shape
    n_tiles = n_tok // TOK_TILE

    def body(p_ref, prev_ref, next_ref, mu_ref, o_ref):
        i = pl.program_id(2)
        p = p_ref[...]
        prev, nxt = _shifted(p, prev_ref, next_ref, seg_first(i), seg_last(i))
        o_ref[...] = p + mu_ref[0:1, :] * (prev - p) + mu_ref[1:2, :] * (nxt - p)

    tile, prev, nxt = _shift_specs(n_tok, RET_COLS // SHIFT_BLOCK)
    return pl.pallas_call(
        body, grid=(SHIFT_COLS // SHIFT_BLOCK, bsz, n_tiles),
        in_specs=[tile, prev, nxt, pl.BlockSpec((2, SHIFT_BLOCK), lambda j, b, i: (0, j))],
        out_specs=pl.BlockSpec((None, TOK_TILE, SHIFT_BLOCK), lambda j, b, i: (b, i, j)),
        out_shape=jax.ShapeDtypeStruct((bsz, n_tok, SHIFT_COLS), f32),
        compiler_params=_cparams(("parallel", "parallel", "parallel")), name="token_shift")(px, px, px, mu)


def token_shift_bwd(dps, px, mu, seg_first, seg_last):
    bsz, n_tok, _ = px.shape
    n_tiles = n_tok // TOK_TILE

    def body(d_ref, dprev_ref, dnext_ref, p_ref, prev_ref, next_ref, mu_ref, dp_ref, dmu_ref):
        b, i = pl.program_id(1), pl.program_id(2)
        first, last = seg_first(i), seg_last(i)
        d, p = d_ref[...], p_ref[...]
        d_prev, d_next = _shifted(d, dprev_ref, dnext_ref, first, last)
        p_prev, p_next = _shifted(p, prev_ref, next_ref, first, last)
        mu0, mu1 = mu_ref[0:1, :], mu_ref[1:2, :]
        dp_ref[...] = (d + mu0 * (d_next - d) + mu1 * (d_prev - d)).astype(dp_ref.dtype)

        @pl.when(jnp.logical_and(b == 0, i == 0))
        def _():
            dmu_ref[...] = jnp.zeros_like(dmu_ref)

        dmu_ref[0:1, :] += jnp.sum(d * (p_prev - p), axis=0, keepdims=True)
        dmu_ref[1:2, :] += jnp.sum(d * (p_next - p), axis=0, keepdims=True)

    dtile, dprev, dnext = _shift_specs(n_tok, 0)
    tile, prev, nxt = _shift_specs(n_tok, RET_COLS // SHIFT_BLOCK)
    mu_spec = pl.BlockSpec((2, SHIFT_BLOCK), lambda j, b, i: (0, j))
    return pl.pallas_call(
        body, grid=(SHIFT_COLS // SHIFT_BLOCK, bsz, n_tiles),
        in_specs=[dtile, dprev, dnext, tile, prev, nxt, mu_spec],
        out_specs=[pl.BlockSpec((None, TOK_TILE, SHIFT_BLOCK), lambda j, b, i: (b, i, j)), mu_spec],
        out_shape=[jax.ShapeDtypeStruct((bsz, n_tok, SHIFT_COLS), MXU_DTYPE),
                   jax.ShapeDtypeStruct((2, SHIFT_COLS), f32)],
        compiler_params=_cparams(("arbitrary", "arbitrary", "arbitrary")), name="token_shift_bwd")(
        dps, dps, dps, px, px, px, mu)


def _dg(a, b, ca, cb):
    return lax.dot_general(a.astype(MXU_DTYPE), b.astype(MXU_DTYPE), (((ca,), (cb,)), ((), ())),
                           preferred_element_type=f32)


@jax.custom_vjp
def _mm_nt(a, b):
    return _dg(a, b, 1, 1)


_mm_nt.defvjp(lambda a, b: (_dg(a, b, 1, 1), (a, b)),
              lambda res, ct: (_dg(ct, res[1], 1, 0), _dg(ct, res[0], 0, 0)))


@jax.custom_vjp
def _mm_tn(a, b):
    return _dg(a, b, 0, 0)


_mm_tn.defvjp(lambda a, b: (_dg(a, b, 0, 0), (a, b)),
              lambda res, ct: (_dg(res[1], ct, 1, 1), _dg(res[0], ct, 1, 0)))


def _ret_chunk(state, q_raw, k_raw, v, cos, sin, perm, ld_row, head, reverse):
    c = RET_CHUNK
    lane = lax.broadcasted_iota(jnp.int32, ld_row.shape, 1)
    lg = -jnp.exp(jnp.sum(jnp.where(lane == head, ld_row, 0.0), axis=-1, keepdims=True))
    rot = lambda t: t * cos + jnp.dot(t, perm, preferred_element_type=f32, precision=lax.Precision.HIGHEST) * sin
    q = rot(q_raw)
    k = rot(k_raw) * (RET_DH ** -0.5)
    ti = lax.broadcasted_iota(jnp.int32, (c, 1), 0).astype(f32)
    tj = lax.broadcasted_iota(jnp.int32, (1, c), 1).astype(f32)
    if not reverse:
        dist, mask, q_exp, k_exp = ti - tj, (ti - tj) >= 0, ti + 1.0, c - 1.0 - ti
    else:
        dist, mask, q_exp, k_exp = tj - ti, (tj - ti) > 0, c - ti, ti
    decay = jnp.where(mask, jnp.exp(lg * jnp.maximum(dist, 0.0)), 0.0)
    scores = _mm_nt(q, k) * decay
    out = _mxu_dot(scores, v) + _mxu_dot(q * jnp.exp(lg * q_exp), state)
    new_state = state * jnp.exp(lg * c) + _mm_tn(k * jnp.exp(lg * k_exp), v)
    return out, new_state


def _ret_specs(order):
    qkv = [pl.BlockSpec((None, RET_CHUNK, RET_W), functools.partial(lambda b, i, col: (b, order(i), col), col=col))
           for col in range(3)]
    tab = pl.BlockSpec((RET_CHUNK, RET_DH), lambda b, i: (order(i), 0))
    const = pl.BlockSpec((RET_DH, RET_DH), lambda b, i: (0, 0))
    ld = pl.BlockSpec((1, RET_DH), lambda b, i: (0, 0))
    return qkv, tab, const, ld


def retention_fwd(px, cos, sin, perm, ld_row, order, reverse, name):
    bsz, n_tok, _ = px.shape
    n_ch = n_tok // RET_CHUNK

    def body(q_ref, k_ref, v_ref, cos_ref, sin_ref, perm_ref, ld_ref, o_ref, sv_ref, st_ref):
        @pl.when(pl.program_id(1) == 0)
        def _():
            st_ref[...] = jnp.zeros_like(st_ref)

        for h in range(RET_HEADS):
            sl = slice(h * RET_DH, (h + 1) * RET_DH)
            s = st_ref[h]
            sv_ref[h] = s
            o, s_new = _ret_chunk(s, q_ref[:, sl], k_ref[:, sl], v_ref[:, sl], cos_ref[...], sin_ref[...],
                                  perm_ref[...], ld_ref[...], h, reverse)
            o_ref[:, sl] = o
            st_ref[h] = s_new

    qkv, tab, const, ld = _ret_specs(order)
    return pl.pallas_call(
        body, grid=(bsz, n_ch), in_specs=[*qkv, tab, tab, const, ld],
        out_specs=[pl.BlockSpec((None, RET_CHUNK, RET_W), lambda b, i: (b, order(i), 0)),
                   pl.BlockSpec((None, None, RET_HEADS, RET_DH, RET_DH), lambda b, i: (b, i, 0, 0, 0))],
        out_shape=[jax.ShapeDtypeStruct((bsz, n_tok, RET_W), f32),
                   jax.ShapeDtypeStruct((bsz, n_ch, RET_HEADS, RET_DH, RET_DH), f32)],
        scratch_shapes=[pltpu.VMEM((RET_HEADS, RET_DH, RET_DH), f32)],
        compiler_params=_cparams(("parallel", "arbitrary")), name=name)(px, px, px, cos, sin, perm, ld_row)


def retention_bwd(do, px, states, cos, sin, perm, ld_row, order, reverse, name):
    bsz, n_tok, _ = px.shape
    n_ch = n_tok // RET_CHUNK
    back = lambda i: order(n_ch - 1 - i)

    def body(do_ref, q_ref, k_ref, v_ref, sv_ref, cos_ref, sin_ref, perm_ref, ld_ref,
             dq_ref, dk_ref, dv_ref, dld_ref, dst_ref):
        b, i = pl.program_id(0), pl.program_id(1)

        @pl.when(i == 0)
        def _():
            dst_ref[...] = jnp.zeros_like(dst_ref)

        @pl.when(jnp.logical_and(b == 0, i == 0))
        def _():
            dld_ref[...] = jnp.zeros_like(dld_ref)

        cos_v, sin_v, perm_v = cos_ref[...], sin_ref[...], perm_ref[...]
        for h in range(RET_HEADS):
            sl = slice(h * RET_DH, (h + 1) * RET_DH)
            f = lambda s, q, k, v, ld, h=h: _ret_chunk(s, q, k, v, cos_v, sin_v, perm_v, ld, h, reverse)
            _, vjp = jax.vjp(f, sv_ref[h], q_ref[:, sl], k_ref[:, sl], v_ref[:, sl], ld_ref[...])
            ds, dq, dk, dv, dld = vjp((do_ref[:, sl], dst_ref[h]))
            dst_ref[h] = ds
            dq_ref[:, sl] = dq
            dk_ref[:, sl] = dk
            dv_ref[:, sl] = dv
            dld_ref[...] += dld

    qkv, tab, const, ld = _ret_specs(back)
    tok = pl.BlockSpec((None, RET_CHUNK, RET_W), lambda b, i: (b, back(i), 0))
    return pl.pallas_call(
        body, grid=(bsz, n_ch),
        in_specs=[tok, *qkv,
                  pl.BlockSpec((None, None, RET_HEADS, RET_DH, RET_DH), lambda b, i: (b, n_ch - 1 - i, 0, 0, 0)),
                  tab, tab, const, ld],
        out_specs=[tok, tok, tok, ld],
        out_shape=[jax.ShapeDtypeStruct((bsz, n_tok, RET_W), f32)] * 3 + [jax.ShapeDtypeStruct((1, RET_DH), f32)],
        scratch_shapes=[pltpu.VMEM((RET_HEADS, RET_DH, RET_DH), f32)],
        compiler_params=_cparams(("arbitrary", "arbitrary")), name=name)(
        do, px, px, px, states, cos, sin, perm, ld_row)


HALF_W = RW_W // 2


def _head_sum(x, ones):
    return jnp.concatenate([_split_dot_impl(x[:, :HALF_W], ones), _split_dot_impl(x[:, HALF_W:], ones)], axis=1)


def _stack(parts):
    return jnp.concatenate(parts, axis=0)


def _row(ref, b, t):
    return ref[b, pl.ds(t, 1), :]


def _to_columns(row_ref, dst_ref, eye, ones, bsz):
    for t in range(SCAN_CHUNK):
        cols = _head_sum(_stack([row_ref[b, t:t + 1, :] * eye for b in range(bsz)]), ones)
        for b in range(bsz):
            dst_ref[t, b] = cols[b * RW_N:(b + 1) * RW_N]


def _scan_specs(bsz, order):
    rows = lambda col=0: pl.BlockSpec((bsz, SCAN_CHUNK, RW_W), lambda i: (0, order(i), col))
    eye = pl.BlockSpec((RW_N, RW_W), lambda i: (0, 0))
    ones = pl.BlockSpec((HALF_W, HALF_W), lambda i: (0, 0))
    return rows, eye, ones


def _scan_update(sp, kk_t, w_t, b_t, kt_t, vcol, ones, bsz):
    removed = _head_sum(_stack([sp[b] * kk_t[b] for b in range(bsz)]), ones)
    rem = [removed[b * RW_N:(b + 1) * RW_N] for b in range(bsz)]
    new = [sp[b] * w_t[b] - rem[b] * b_t[b] + vcol[b] * kt_t[b] for b in range(bsz)]
    return rem, new


def rwkv_scan_fwd(rows_in, eye, ones, order, reverse, inclusive, name):
    bsz, n_tok, _ = rows_in[0][0].shape
    n_ch = n_tok // SCAN_CHUNK

    def body(r_ref, kk_ref, v_ref, w_ref, b_ref, kt_ref, eye_ref, ones_ref, y_ref, cs_ref,
             s_ref, vcol_ref, ybuf_ref):
        @pl.when(pl.program_id(0) == 0)
        def _():
            s_ref[...] = jnp.zeros_like(s_ref)

        cs_ref[...] = s_ref[...]
        eye_v, ones_v = eye_ref[...], ones_ref[...]
        _to_columns(v_ref, vcol_ref, eye_v, ones_v, bsz)

        def step(j, carry):
            t = SCAN_CHUNK - 1 - j if reverse else j
            sp = [s_ref[b] for b in range(bsz)]
            _, new = _scan_update(sp, [_row(kk_ref, b, t) for b in range(bsz)],
                                  [_row(w_ref, b, t) for b in range(bsz)], [_row(b_ref, b, t) for b in range(bsz)],
                                  [_row(kt_ref, b, t) for b in range(bsz)], [vcol_ref[t, b] for b in range(bsz)],
                                  ones_v, bsz)
            for b in range(bsz):
                s_ref[b] = new[b]
                ybuf_ref[t, b] = new[b] if inclusive else sp[b]
            return carry

        lax.fori_loop(0, SCAN_CHUNK, step, 0)
        for t in range(SCAN_CHUNK):
            ysum = _head_sum(_stack([ybuf_ref[t, b] * r_ref[b, t:t + 1, :] for b in range(bsz)]), ones_v)
            for b in range(bsz):
                y_ref[b, t:t + 1, :] = jnp.sum(ysum[b * RW_N:(b + 1) * RW_N] * eye_v, axis=0, keepdims=True)

    rows, eye_spec, ones_spec = _scan_specs(bsz, order)
    hist = pltpu.VMEM((SCAN_CHUNK, bsz, RW_N, RW_W), f32)
    return pl.pallas_call(
        body, grid=(n_ch,), in_specs=[rows(col) for _, col in rows_in] + [eye_spec, ones_spec],
        out_specs=[rows(), pl.BlockSpec((None, bsz, RW_N, RW_W), lambda i: (i, 0, 0, 0))],
        out_shape=[jax.ShapeDtypeStruct((bsz, n_tok, RW_W), f32),
                   jax.ShapeDtypeStruct((n_ch, bsz, RW_N, RW_W), f32)],
        scratch_shapes=[pltpu.VMEM((bsz, RW_N, RW_W), f32), hist, hist],
        compiler_params=_cparams(("arbitrary",)), name=name)(*[a for a, _ in rows_in], eye, ones)


def rwkv_scan_bwd(rows_in, starts, eye, ones, order, reverse, inclusive, name):
    bsz, n_tok, _ = rows_in[0][0].shape
    n_ch = n_tok // SCAN_CHUNK
    back = lambda i: order(n_ch - 1 - i)

    def body(dy_ref, r_ref, kk_ref, v_ref, w_ref, b_ref, kt_ref, cs_ref, eye_ref, ones_ref,
             dr_ref, dkk_ref, dv_ref, dw_ref, db_ref, dkt_ref,
             s_ref, ds_ref, vcol_ref, dycol_ref, hist_ref, rem_ref, dsh_ref, drem_ref):
        @pl.when(pl.program_id(0) == 0)
        def _():
            ds_ref[...] = jnp.zeros_like(ds_ref)

        eye_v, ones_v = eye_ref[...], ones_ref[...]
        _to_columns(v_ref, vcol_ref, eye_v, ones_v, bsz)
        _to_columns(dy_ref, dycol_ref, eye_v, ones_v, bsz)
        s_ref[...] = cs_ref[...]
        rng = range(bsz)

        def fstep(j, carry):
            t = SCAN_CHUNK - 1 - j if reverse else j
            sp = [s_ref[b] for b in rng]
            rem, new = _scan_update(sp, [_row(kk_ref, b, t) for b in rng], [_row(w_ref, b, t) for b in rng],
                                    [_row(b_ref, b, t) for b in rng], [_row(kt_ref, b, t) for b in rng],
                                    [vcol_ref[t, b] for b in rng], ones_v, bsz)
            for b in rng:
                hist_ref[t, b] = sp[b]
                rem_ref[t, b] = rem[b]
                s_ref[b] = new[b]
            return carry

        lax.fori_loop(0, SCAN_CHUNK, fstep, 0)

        def bstep(j, carry):
            t = j if reverse else SCAN_CHUNK - 1 - j
            ds = [ds_ref[b] for b in rng]
            read = [dycol_ref[t, b] * _row(r_ref, b, t) for b in rng]
            if inclusive:
                ds = [ds[b] + read[b] for b in rng]
            drem_all = -_head_sum(_stack([ds[b] * _row(b_ref, b, t) for b in rng]), ones_v)
            for b in rng:
                drem = drem_all[b * RW_N:(b + 1) * RW_N]
                dsh_ref[t, b] = ds[b]
                drem_ref[t, b] = drem
                dsp = ds[b] * _row(w_ref, b, t) + drem * _row(kk_ref, b, t)
                ds_ref[b] = dsp if inclusive else dsp + read[b]
            return carry

        lax.fori_loop(0, SCAN_CHUNK, bstep, 0)

        rsum = lambda z: jnp.sum(z, axis=0, keepdims=True)
        for t in range(SCAN_CHUNK):
            ts = slice(t, t + 1)
            dvc = _head_sum(_stack([dsh_ref[t, b] * kt_ref[b, ts, :] for b in rng]), ones_v)
            for b in rng:
                sp, rem, vc, ds = hist_ref[t, b], rem_ref[t, b], vcol_ref[t, b], dsh_ref[t, b]
                sy = sp * w_ref[b, ts, :] - rem * b_ref[b, ts, :] + vc * kt_ref[b, ts, :] if inclusive else sp
                dr_ref[b, ts, :] = rsum(sy * dycol_ref[t, b])
                dw_ref[b, ts, :] = rsum(ds * sp)
                db_ref[b, ts, :] = -rsum(ds * rem)
                dkt_ref[b, ts, :] = rsum(ds * vc)
                dkk_ref[b, ts, :] = rsum(sp * drem_ref[t, b])
                dv_ref[b, ts, :] = rsum(dvc[b * RW_N:(b + 1) * RW_N] * eye_v)

    rows, eye_spec, ones_spec = _scan_specs(bsz, back)
    hist = pltpu.VMEM((SCAN_CHUNK, bsz, RW_N, RW_W), f32)
    state = pltpu.VMEM((bsz, RW_N, RW_W), f32)
    return pl.pallas_call(
        body, grid=(n_ch,),
        in_specs=[rows(col) for _, col in rows_in]
        + [pl.BlockSpec((None, bsz, RW_N, RW_W), lambda i: (n_ch - 1 - i, 0, 0, 0)), eye_spec, ones_spec],
        out_specs=[rows()] * 6, out_shape=[jax.ShapeDtypeStruct((bsz, n_tok, RW_W), f32)] * 6,
        scratch_shapes=[state, state, hist, hist, hist, hist, hist, hist],
        compiler_params=_cparams(("arbitrary",)), name=name)(*[a for a, _ in rows_in], starts, eye, ones)


MOD_NAMES = ("shift1", "scale1", "gate1", "shift2", "scale2", "gate2")


def _rope_tables(t_ctx, t_x):
    quarter = RET_DH // 4
    pos = jnp.arange(t_x)
    inv = jnp.power(ROPE_BASE, -jnp.arange(0, 2 * quarter, 2, dtype=f32) / (2 * quarter))
    ang_r = (pos // GRID_W).astype(f32)[:, None] * inv[None, :]
    ang_c = (pos % GRID_W).astype(f32)[:, None] * inv[None, :]
    cos = jnp.concatenate([jnp.cos(ang_r)] * 2 + [jnp.cos(ang_c)] * 2, axis=1)
    sin = jnp.concatenate([-jnp.sin(ang_r), jnp.sin(ang_r), -jnp.sin(ang_c), jnp.sin(ang_c)], axis=1)
    cos = jnp.concatenate([jnp.ones((t_ctx, RET_DH), f32), cos], axis=0)
    sin = jnp.concatenate([jnp.zeros((t_ctx, RET_DH), f32), sin], axis=0)
    lane = jnp.arange(RET_DH)
    partner = jnp.where(lane % (2 * quarter) < quarter, lane + quarter, lane - quarter)
    perm = (lane[:, None] == partner[None, :]).astype(f32)
    return cos, sin, perm


def _pad_rows(w, lo, total):
    return jnp.pad(w, ((lo, total - lo - w.shape[0]), (0, 0)))


def layer_step(x, ctx, tgt, mod_x, mod_ctx, wt):
    bsz, t_x, _ = x.shape
    t_c = ctx.shape[1]
    t_all = t_c + t_x
    n_ct, n_xt = t_c // TOK_TILE, t_x // TOK_TILE
    n_t = n_ct + n_xt
    assert t_c % TOK_TILE == 0 and t_x % TOK_TILE == 0 and t_c % RET_CHUNK == 0

    seg = lambda i: (i >= n_ct).astype(jnp.int32)
    seg_first = lambda i: jnp.logical_or(i == 0, i == n_ct)
    seg_last = lambda i: jnp.logical_or(i == n_ct - 1, i == n_t - 1)
    mod_all = {n: jnp.stack([jnp.broadcast_to(mod_ctx[k], (bsz, D_MODEL)), mod_x[:, k]], axis=1)[:, :, None, :]
               for k, n in enumerate(MOD_NAMES)}
    mod_lat = {n: mod_x[:, k][:, None, None, :] for k, n in enumerate(MOD_NAMES)}
    both = lambda n: Seg(mod_all[n], seg, seg_first)
    lat = lambda n: Seg(mod_lat[n], lambda i: 0, lambda i: i == 0)
    flat = lambda a: a.reshape(-1, a.shape[-1])
    padc = lambda a: jnp.pad(a, ((0, 0), (t_c, 0), (0, 0)))

    def chunk_orders(n_ctx_chunks, n_chunks):
        fwd = lambda i: i
        bwd = lambda i: jnp.where(i < n_ctx_chunks, n_ctx_chunks - 1 - i, n_chunks + n_ctx_chunks - 1 - i)
        return fwd, bwd

    ones64, ones128 = _block_ones(RW_W, RW_N), _block_ones(RET_W, RET_DH)
    ones_half = _block_ones(HALF_W, RW_N)
    eye = jnp.tile(jnp.eye(RW_N, dtype=f32), (1, RW_W // RW_N))
    cos, sin, perm = _rope_tables(t_c, t_x)
    ld_rows = [jnp.pad(wt["ret_log_decay"][d][None, :], ((0, 0), (0, RET_DH - RET_HEADS))) for d in range(2)]
    w_up_pad = [_pad_rows(wt["rwkv_w_up"][d], 0, LORA_W) for d in range(2)]
    a_up_pad = [_pad_rows(wt["rwkv_a_up"][d], DECAY_LORA, LORA_W) for d in range(2)]
    g_up_pad = _pad_rows(wt["rwkv_g_up"], DECAY_LORA + AAA_LORA, LORA_W)
    row = lambda a, d: a[d][None, :]

    h = jnp.concatenate([ctx, x], axis=1)
    norm1_ins = lambda: [Tiled(h), both("shift1"), both("scale1"), Glob(wt["norm1_g"])]
    (n1,) = ew_forward(fn_norm_mod, "norm1", bsz, n_t, norm1_ins(), [(D_MODEL, MXU_DTYPE)])
    px = matmul(flat(n1), wt["w_in"], "nn", "proj_in").reshape(bsz, t_all, IN_COLS)
    ps = token_shift(px, wt["rwkv_shift_mu"], seg_first, seg_last)

    def prep_ins(toff=0):
        return [Tiled(ps, RW_W, 1), Tiled(ps, LORA_W, 3 * RW_W // LORA_W),
                Glob(row(wt["rwkv_w0"], 0)), Glob(row(wt["rwkv_w0"], 1)),
                Glob(row(wt["rwkv_a0"], 0)), Glob(row(wt["rwkv_a0"], 1)),
                Glob(w_up_pad[0]), Glob(w_up_pad[1]), Glob(a_up_pad[0]), Glob(a_up_pad[1]), Glob(g_up_pad),
                Glob(wt["rwkv_k_k"]), Glob(wt["rwkv_k_a"]), Glob(ones64)]

    kk, w_f, b_f, kt_f, w_b, b_b, kt_b, g_rw = ew_forward(fn_rwkv_prepare, "rwkv_prepare", bsz, n_t, prep_ins(),
                                                           [(RW_W, f32)] * 8)
    rw_order = chunk_orders(t_c // SCAN_CHUNK, t_all // SCAN_CHUNK)
    ret_order = chunk_orders(t_c // RET_CHUNK, t_all // RET_CHUNK)
    scan_rows = [[(ps, 0), (kk, 0), (ps, 2), (w_f, 0), (b_f, 0), (kt_f, 0)],
                 [(ps, 0), (kk, 0), (ps, 2), (w_b, 0), (b_b, 0), (kt_b, 0)]]
    dirs = [dict(reverse=False, inclusive=True), dict(reverse=True, inclusive=False)]
    y, starts, o, ret_states = [], [], [], []
    for d in range(2):
        y_d, s_d = rwkv_scan_fwd(scan_rows[d], eye, ones_half, rw_order[d], name=f"rwkv_scan_fwd{d}", **dirs[d])
        o_d, st_d = retention_fwd(px, cos, sin, perm, ld_rows[d], ret_order[d], dirs[d]["reverse"], f"retention_fwd{d}")
        y.append(y_d), starts.append(s_d), o.append(o_d), ret_states.append(st_d)

    def merge_ins():
        return [Tiled(o[0], toff=n_ct), Tiled(o[1], toff=n_ct), Tiled(px, RET_W, 3, n_ct),
                Tiled(y[0], toff=n_ct), Tiled(y[1], toff=n_ct), Tiled(ps, RW_W, 0, n_ct), Tiled(kt_f, toff=n_ct),
                Tiled(ps, RW_W, 2, n_ct), Tiled(g_rw, toff=n_ct),
                Glob(wt["rwkv_r_k"]), Glob(wt["rwkv_ln_w"]), Glob(wt["rwkv_ln_b"]), Glob(ones64), Glob(ones128)]

    ret_out, rw_out = ew_forward(fn_merge, "merge_heads", bsz, n_xt, merge_ins(), [(RET_W, MXU_DTYPE), (RW_W, MXU_DTYPE)])
    merged = jnp.concatenate([ret_out, rw_out], axis=-1)
    mix = matmul(flat(merged), wt["w_out"], "nn", "proj_out").reshape(bsz, t_x, D_MODEL)
    resid_ins = lambda: [Tiled(x), Tiled(mix), lat("gate1"), lat("shift2"), lat("scale2"), Glob(wt["norm2_g"])]
    h1, n2 = ew_forward(fn_resid_norm_mod, "resid_norm2", bsz, n_xt, resid_ins(), [(D_MODEL, f32), (D_MODEL, MXU_DTYPE)])
    u = matmul(flat(n2), wt["w_ff1"], "nn", "ff1").reshape(bsz, t_x, D_FF)
    relu_ins = lambda: [Tiled(u), Glob(wt["b_ff1"])]
    (act,) = ew_forward(fn_relu2, "relu2", bsz, n_xt, relu_ins(), [(D_FF, MXU_DTYPE)])
    ff = matmul(flat(act), wt["w_ff2"], "nn", "ff2").reshape(bsz, t_x, D_MODEL)

    g = {}
    loss, dh1, dff, dgate2, g["b_ff2"], g["final_g"] = loss_and_grads(
        h1, ff, tgt, mod_lat["gate2"], wt["b_ff2"], wt["final_g"], bsz, n_xt)
    dact = matmul(flat(dff), wt["w_ff2"], "nt", "ff2_dx").reshape(bsz, t_x, D_FF)
    g["w_ff2"] = matmul(flat(act), flat(dff), "tn", "ff2_dw")
    du, g["b_ff1"] = ew_backward(fn_relu2, "relu2_bwd", bsz, n_xt, relu_ins(), [Tiled(dact)], [True, True],
                                 {0: MXU_DTYPE})
    dn2 = matmul(flat(du), wt["w_ff1"], "nt", "ff1_dx").reshape(bsz, t_x, D_MODEL)
    g["w_ff1"] = matmul(flat(n2), flat(du), "tn", "ff1_dw")
    dx_res, dmix, dgate1, dshift2, dscale2, g["norm2_g"] = ew_backward(
        fn_resid_norm_mod, "resid_norm2_bwd", bsz, n_xt, resid_ins(), [Tiled(dh1), Tiled(dn2)], [True] * 6,
        {1: MXU_DTYPE})
    dmerged = matmul(flat(dmix), wt["w_out"], "nt", "proj_out_dx").reshape(bsz, t_x, D_MODEL)
    g["w_out"] = matmul(flat(merged), flat(dmix), "tn", "proj_out_dw")
    (do, dg_ret, dy, dr_m, dkt_m, dv_m, dg_rw, g["rwkv_r_k"], g["rwkv_ln_w"], g["rwkv_ln_b"]) = ew_backward(
        fn_merge, "merge_heads_bwd", bsz, n_xt, merge_ins(), [Tiled(dmerged, RET_W, 0), Tiled(dmerged, RW_W, 1)],
        [True, False, True, True, False, True, True, True, True, True, True, True, False, False])
    do, dy = padc(do), padc(dy)

    dqkv, dld, dscan = [], [], []
    for d in range(2):
        *dqkv_d, dld_d = retention_bwd(do, px, ret_states[d], cos, sin, perm, ld_rows[d], ret_order[d],
                                       dirs[d]["reverse"], f"retention_bwd{d}")
        dqkv.append(dqkv_d), dld.append(dld_d[0, :RET_HEADS])
        dscan.append(rwkv_scan_bwd([(dy, 0)] + scan_rows[d], starts[d], eye, ones_half, rw_order[d],
                                   name=f"rwkv_scan_bwd{d}", **dirs[d]))
    g["ret_log_decay"] = jnp.stack(dld)
    (dr_f, dkk_f, dv_f, dw_f, db_f, dkt_f), (dr_b, dkk_b, dv_b, dw_b, db_b, dkt_b) = dscan
    prep_cts = [dkk_f + dkk_b, dw_f, db_f, dkt_f + padc(dkt_m), dw_b, db_b, dkt_b, padc(dg_rw)]
    (dks, dlora, dw0_f, dw0_b, da0_f, da0_b, dwup_f, dwup_b, daup_f, daup_b, dgup, g["rwkv_k_k"],
     g["rwkv_k_a"]) = ew_backward(fn_rwkv_prepare, "rwkv_prepare_bwd", bsz, n_t, prep_ins(),
                                  [Tiled(c) for c in prep_cts], [True] * 13 + [False])
    g["rwkv_w0"] = jnp.concatenate([dw0_f, dw0_b], axis=0)
    g["rwkv_a0"] = jnp.concatenate([da0_f, da0_b], axis=0)
    g["rwkv_w_up"] = jnp.stack([dwup_f[:DECAY_LORA], dwup_b[:DECAY_LORA]])
    g["rwkv_a_up"] = jnp.stack([daup_f[DECAY_LORA:DECAY_LORA + AAA_LORA], daup_b[DECAY_LORA:DECAY_LORA + AAA_LORA]])
    g["rwkv_g_up"] = dgup[DECAY_LORA + AAA_LORA:]
    dps = jnp.concatenate([dr_f + dr_b + padc(dr_m), dks, dv_f + dv_b + padc(dv_m), dlora], axis=-1)
    dp_rw, g["rwkv_shift_mu"] = token_shift_bwd(dps, px, wt["rwkv_shift_mu"], seg_first, seg_last)
    dpx = jnp.concatenate([(dqkv[0][k] + dqkv[1][k]).astype(MXU_DTYPE) for k in range(3)]
                          + [padc(dg_ret).astype(MXU_DTYPE), dp_rw], axis=-1)
    dn1 = matmul(flat(dpx), wt["w_in"], "nt", "proj_in_dx").reshape(bsz, t_all, D_MODEL)
    g["w_in"] = matmul(flat(n1), flat(dpx), "tn", "proj_in_dw")
    dh, dshift1, dscale1, g["norm1_g"] = ew_backward(fn_norm_mod, "norm1_bwd", bsz, n_t, norm1_ins(), [Tiled(dn1)],
                                                     [True] * 4)
    grad_x = dh[:, t_c:] + dx_res
    zeros = jnp.zeros((D_MODEL,), f32)
    g["mod_x"] = jnp.stack([dshift1[:, 1, 0], dscale1[:, 1, 0], dgate1[:, 0, 0], dshift2[:, 0, 0], dscale2[:, 0, 0],
                            dgate2[:, 0, 0]], axis=1)
    g["mod_ctx"] = jnp.stack([dshift1[:, 0, 0].sum(0), dscale1[:, 0, 0].sum(0), zeros, zeros, zeros, zeros])
    return loss, grad_x, g


MESH_ID = pl.DeviceIdType.MESH
ALL_PEERS = [(dx, dy, dc) for dx in (0, 1) for dy in (0, 1) for dc in (0, 1)][1:]
SIBLING = [(0, 0, 1)]
CHIP_SLOTS = (0, 2, 4, 6)


def _mesh_pos():
    return lax.axis_index("x"), lax.axis_index("y"), lax.axis_index("c")


def _device_slot():
    x, y, c = _mesh_pos()
    return 4 * x + 2 * y + c


def exchange(arrs, gather, peers, name, by_core=False):
    n, n_peers = len(arrs), len(peers)
    n_slots = 2 if by_core else N_DEV
    slot = (lambda x, y, c: c) if by_core else (lambda x, y, c: 4 * x + 2 * y + c)

    def body(*refs):
        in_refs, out_refs = refs[:n], refs[n:2 * n]
        send_sems, recv_sems, local_sems = refs[2 * n:]
        x, y, c = _mesh_pos()
        me = slot(x, y, c)
        copies, locals_ = [], []
        for a in range(n):
            own = in_refs[a] if gather else in_refs[a].at[me]
            loc = pltpu.make_async_copy(own, out_refs[a].at[me], local_sems.at[a])
            loc.start()
            locals_.append(loc)
            for k, (dx, dy, dc) in enumerate(peers):
                peer = (1 - x if dx else x, 1 - y if dy else y, 1 - c if dc else c)
                src = in_refs[a] if gather else in_refs[a].at[slot(*peer)]
                sem = a * n_peers + k
                cp = pltpu.make_async_remote_copy(src_ref=src, dst_ref=out_refs[a].at[me], send_sem=send_sems.at[sem],
                                                  recv_sem=recv_sems.at[sem], device_id=peer, device_id_type=MESH_ID)
                cp.start()
                copies.append(cp)
        for cp in copies:
            cp.wait()
        for loc in locals_:
            loc.wait()

    any_spec = pl.BlockSpec(memory_space=pl.ANY)
    out_shape = [jax.ShapeDtypeStruct((n_slots,) + (a.shape if gather else a.shape[1:]), a.dtype) for a in arrs]
    res = pl.pallas_call(
        body, in_specs=[any_spec] * n, out_specs=[any_spec] * n, out_shape=out_shape,
        scratch_shapes=[pltpu.SemaphoreType.DMA((n * n_peers,)), pltpu.SemaphoreType.DMA((n * n_peers,)),
                        pltpu.SemaphoreType.DMA((n,))],
        name=name)(*arrs)
    return list(res)


def sum_slots(parts, slots, name):
    _, r, c = parts.shape
    tr = r
    for cand in (512, 256, 128, 64, 32, 16, 8):
        if r % cand == 0 and cand * c * 4 * len(slots) <= 8 * 1024 * 1024:
            tr = cand
            break

    def body(p_ref, o_ref):
        acc = p_ref[slots[0]]
        for s in slots[1:]:
            acc = acc + p_ref[s]
        o_ref[...] = acc

    return pl.pallas_call(body, grid=(r // tr,), in_specs=[pl.BlockSpec((parts.shape[0], tr, c), lambda i: (0, i, 0))],
                          out_specs=pl.BlockSpec((tr, c), lambda i: (i, 0)),
                          out_shape=jax.ShapeDtypeStruct((r, c), f32),
                          compiler_params=_cparams(("parallel",)), name=name)(parts)


def column_sum(a, name):
    def body(a_ref, o_ref):
        o_ref[...] = jnp.sum(a_ref[...], axis=0, keepdims=True)

    return pl.pallas_call(body, out_shape=jax.ShapeDtypeStruct((1, a.shape[1]), f32), name=name)(a)


def adamw(w, g, m, v, name):
    r, c = w.shape
    tr = r
    for cand in (256, 128, 64, 32, 16, 8):
        if r % cand == 0:
            tr = cand
            break

    def body(w_ref, g_ref, m_ref, v_ref, d_ref, mo_ref, vo_ref):
        gv = g_ref[...]
        m_new = ADAM_B1 * m_ref[...] + (1.0 - ADAM_B1) * gv
        v_new = ADAM_B2 * v_ref[...] + (1.0 - ADAM_B2) * jnp.square(gv)
        m_hat = m_new / (1.0 - ADAM_B1 ** ADAM_STEP)
        v_hat = v_new / (1.0 - ADAM_B2 ** ADAM_STEP)
        d_ref[...] = -ADAM_LR * (m_hat / (jnp.sqrt(v_hat) + ADAM_EPS) + ADAM_WD * w_ref[...])
        mo_ref[...] = m_new
        vo_ref[...] = v_new

    spec = pl.BlockSpec((tr, c), lambda i: (i, 0))
    return pl.pallas_call(body, grid=(r // tr,), in_specs=[spec] * 4, out_specs=[spec] * 3,
                          out_shape=[jax.ShapeDtypeStruct((r, c), f32)] * 3,
                          compiler_params=_cparams(("parallel",)), name=name)(w, g, m, v)


def adaln_fwd(c_rows, w, b):
    def body(c_ref, w_ref, b_ref, o_ref):
        cv = c_ref[...]
        o_ref[...] = _mxu_dot(cv * jax.nn.sigmoid(cv), w_ref[...]) + b_ref[...]

    return pl.pallas_call(body, out_shape=jax.ShapeDtypeStruct((c_rows.shape[0], w.shape[1]), f32),
                          compiler_params=pltpu.CompilerParams(vmem_limit_bytes=VMEM_LIMIT), name="adaln_fwd")(c_rows, w, b)


def adaln_bwd(c_rows, dm, w):
    def body(c_ref, dm_ref, w_ref, gw_ref, ds_ref):
        cv = c_ref[...]
        gw_ref[...] = _dg(cv * jax.nn.sigmoid(cv), dm_ref[...], 0, 0)
        ds_ref[...] = _dg(dm_ref[...], w_ref[...], 1, 1)

    return pl.pallas_call(body, out_shape=[jax.ShapeDtypeStruct(w.shape, f32),
                                           jax.ShapeDtypeStruct(c_rows.shape, f32)],
                          compiler_params=pltpu.CompilerParams(vmem_limit_bytes=VMEM_LIMIT), name="adaln_bwd")(c_rows, dm, w)


def c_ctx_grad(parts, c_ctx_row):
    def body(p_ref, c_ref, o_ref):
        total = p_ref[CHIP_SLOTS[0], 0:1, :]
        for s in CHIP_SLOTS[1:]:
            total = total + p_ref[s, 0:1, :]
        _, vjp = jax.vjp(jax.nn.silu, c_ref[...])
        o_ref[...] = vjp(total)[0]

    return pl.pallas_call(body, out_shape=jax.ShapeDtypeStruct((1, D_MODEL), f32), name="c_ctx_grad")(parts, c_ctx_row)


PACK_W = 1024
PACK_ROWS = 8


def _pack(arrs):
    pieces, layout, r0 = [], [], 0
    for a in arrs:
        size = math.prod(a.shape)
        rows = -(-size // (PACK_W * PACK_ROWS)) * PACK_ROWS
        pieces.append(jnp.pad(a.reshape(-1).astype(f32), (0, rows * PACK_W - size)).reshape(rows, PACK_W))
        layout.append((r0, rows, a.shape))
        r0 += rows
    return jnp.concatenate(pieces, axis=0), layout


def _unpack(pack, layout, lead=()):
    n_lead = len(lead)
    outs = []
    for r0, rows, shape in layout:
        piece = pack[(slice(None),) * n_lead + (slice(r0, r0 + rows),)].reshape(lead + (-1,))
        outs.append(piece[..., :math.prod(shape)].reshape(lead + tuple(shape)))
    return outs


W_NAMES = ("c_ctx", "w_ada", "b_ada", "norm1_g", "norm2_g", "w_in", "ret_log_decay", "rwkv_shift_mu", "rwkv_w0",
           "rwkv_w_up", "rwkv_a0", "rwkv_a_up", "rwkv_g_up", "rwkv_k_k", "rwkv_k_a", "rwkv_r_k", "rwkv_ln_w",
           "rwkv_ln_b", "w_out", "w_ff1", "b_ff1", "w_ff2", "b_ff2", "final_g")
COL_SHARDED = ("w_in", "w_ff1")
ROW_SHARDED = ("w_out", "w_ff2")
LAST_SHARDED = ("rwkv_shift_mu", "rwkv_w0", "rwkv_w_up", "rwkv_a0", "rwkv_a_up", "rwkv_g_up")
REPLICATED = ("c_ctx", "b_ada", "norm1_g", "norm2_g", "ret_log_decay", "rwkv_k_k", "rwkv_k_a", "rwkv_r_k",
              "rwkv_ln_w", "rwkv_ln_b", "b_ff1", "b_ff2", "final_g")
N_SHARDS = 4


def _train_step(a):
    x, c, ctx, tgt = a["x"], a["c"], a["ctx"], a["loss_target"]
    bsz = x.shape[0]
    mx, my, mc = _mesh_pos()
    shard = 2 * mx + my
    dev = _device_slot()

    (c_all,) = exchange([jnp.pad(c, ((0, PACK_ROWS - bsz), (0, 0)))], True, ALL_PEERS, "gather_c")
    n_ex = N_DEV * bsz
    c_rows = jnp.concatenate([c_all[:, :bsz].reshape(n_ex, D_MODEL), a["c_ctx"][None, :],
                              jnp.zeros((PACK_ROWS - 1, D_MODEL), f32)], axis=0)
    ada_cols = a["w_ada"].shape[-1]
    b_ada_cols = lax.dynamic_slice_in_dim(a["b_ada"], shard * ada_cols, ada_cols, axis=1)
    mod_cols = adaln_fwd(c_rows, a["w_ada"][0], b_ada_cols)

    halves, small_shards = [], [a[n][0] for n in LAST_SHARDED]
    for n in COL_SHARDED + ROW_SHARDED:
        w = a[n][0].astype(MXU_DTYPE)
        half = w.shape[0] // 2
        halves.append(lax.dynamic_slice_in_dim(w, mc * half, half, axis=0))
    small_pack, small_layout = _pack(small_shards)
    gathered = exchange([mod_cols] + halves + [small_pack], True, ALL_PEERS, "gather_weights")
    mod_all = jnp.stack([gathered[0][s] for s in CHIP_SLOTS], axis=1).reshape(c_rows.shape[0], -1)
    mod_x = lax.dynamic_slice_in_dim(mod_all, dev * bsz, bsz, axis=0).reshape(bsz, 6, D_MODEL)
    mod_ctx = mod_all[n_ex].reshape(6, D_MODEL)
    wt = {}
    for n, gth in zip(COL_SHARDED + ROW_SHARDED, gathered[1:5]):
        per_chip = gth.reshape(N_SHARDS, -1, gth.shape[-1])
        wt[n] = (per_chip.transpose(1, 0, 2).reshape(per_chip.shape[1], -1) if n in COL_SHARDED
                 else per_chip.reshape(-1, per_chip.shape[-1]))
    small_by_chip = _unpack(jnp.stack([gathered[5][s] for s in CHIP_SLOTS]), small_layout, (N_SHARDS,))
    for n, parts in zip(LAST_SHARDED, small_by_chip):
        wt[n] = jnp.concatenate([parts[s] for s in range(N_SHARDS)], axis=-1)
    for n in ("norm1_g", "norm2_g", "rwkv_k_k", "rwkv_k_a", "rwkv_r_k", "rwkv_ln_w", "rwkv_ln_b", "b_ff1", "b_ff2"):
        wt[n] = a[n]
    wt["ret_log_decay"] = a["ret_log_decay"][0]
    wt["final_g"] = a["final_g"][None, :]

    loss, grad_x, g = layer_step(x, ctx, tgt, mod_x, mod_ctx, wt)

    small_names = [n for n in REPLICATED if n not in ("c_ctx", "b_ada")] + list(LAST_SHARDED)
    g_pack, g_layout = _pack([jnp.pad(loss, ((0, 0), (0, PACK_W - loss.shape[1])))] + [g[n] for n in small_names]
                             + [g["mod_x"], g["mod_ctx"]])
    (g_packs,) = exchange([g_pack], True, ALL_PEERS, "gather_small_grads")
    g_sum = _unpack(sum_slots(g_packs, tuple(range(N_DEV)), "sum_small_grads"), g_layout)
    loss_total = g_sum[0][0, 0]
    grads = dict(zip(small_names, g_sum[1:1 + len(small_names)]))
    dmod_ctx = g_sum[-1].reshape(1, -1)
    dmod_x = _unpack(g_packs, g_layout, (N_DEV,))[-2].reshape(n_ex, -1)
    dmod = jnp.concatenate([dmod_x, dmod_ctx, jnp.zeros((PACK_ROWS - 1, dmod_x.shape[1]), f32)], axis=0)
    grads["b_ada"] = column_sum(dmod, "b_ada_grad")
    dmod_cols = lax.dynamic_slice_in_dim(dmod, shard * ada_cols, ada_cols, axis=1)
    grads["w_ada"], dsilu = adaln_bwd(c_rows, dmod_cols, a["w_ada"][0])

    blocks = [jnp.pad(dsilu[n_ex:n_ex + 1], ((0, PACK_ROWS - 1), (0, 0)))[None].repeat(N_DEV, axis=0)]
    for n in COL_SHARDED + ROW_SHARDED:
        gw = g[n]
        if n in COL_SHARDED:
            gw = gw.reshape(gw.shape[0], N_SHARDS, -1).transpose(1, 0, 2)
        blocks.append(gw.reshape(N_DEV, -1, gw.shape[-1]))
    received = exchange(blocks, False, ALL_PEERS, "scatter_big_grads")
    grads["c_ctx"] = c_ctx_grad(received[0], a["c_ctx"][None, :])
    half_sums = [sum_slots(p, tuple(range(N_DEV)), f"sum_{n}") for n, p in zip(COL_SHARDED + ROW_SHARDED, received[1:])]
    both_halves = exchange(half_sums, True, SIBLING, "swap_halves", by_core=True)
    for n, two in zip(COL_SHARDED + ROW_SHARDED, both_halves):
        grads[n] = two.reshape(-1, two.shape[-1])
    for n in LAST_SHARDED:
        width = a[n].shape[-1]
        grads[n] = lax.dynamic_slice_in_dim(grads[n], shard * width, width, axis=grads[n].ndim - 1)

    out_g, out_d, out_m, out_v = {}, {}, {}, {}
    for n in ("w_ada",) + COL_SHARDED + ROW_SHARDED:
        out_g[n] = grads[n].reshape(a[n].shape)
        two_d = lambda z: z.reshape(-1, z.shape[-1])
        d, m, v = adamw(two_d(a[n]), two_d(out_g[n]), two_d(a["m_" + n]), two_d(a["v_" + n]), f"adamw_{n}")
        out_d[n], out_m[n], out_v[n] = d.reshape(a[n].shape), m.reshape(a[n].shape), v.reshape(a[n].shape)
    rest = REPLICATED + LAST_SHARDED
    for n in rest:
        out_g[n] = grads[n].reshape(a[n].shape)
    packs = [_pack([src[n] for n in rest])[0] for src in
             ({n: a[n] for n in rest}, out_g, {n: a["m_" + n] for n in rest}, {n: a["v_" + n] for n in rest})]
    _, rest_layout = _pack([a[n] for n in rest])
    for dst, pack in zip((out_d, out_m, out_v), adamw(*packs, "adamw_small")):
        dst.update(zip(rest, _unpack(pack, rest_layout)))
    return (loss_total, grad_x, *[out_g[n] for n in W_NAMES], *[out_d[n] for n in W_NAMES],
            *[out_m[n] for n in W_NAMES], *[out_v[n] for n in W_NAMES])


def kernel(x, c, ctx, c_ctx, w_ada, b_ada, norm1_g, norm2_g, w_in, ret_log_decay, rwkv_shift_mu, rwkv_w0, rwkv_w_up, rwkv_a0, rwkv_a_up, rwkv_g_up, rwkv_k_k, rwkv_k_a, rwkv_r_k, rwkv_ln_w, rwkv_ln_b, w_out, w_ff1, b_ff1, w_ff2, b_ff2, final_g, loss_target, m_c_ctx, m_w_ada, m_b_ada, m_norm1_g, m_norm2_g, m_w_in, m_ret_log_decay, m_rwkv_shift_mu, m_rwkv_w0, m_rwkv_w_up, m_rwkv_a0, m_rwkv_a_up, m_rwkv_g_up, m_rwkv_k_k, m_rwkv_k_a, m_rwkv_r_k, m_rwkv_ln_w, m_rwkv_ln_b, m_w_out, m_w_ff1, m_b_ff1, m_w_ff2, m_b_ff2, m_final_g, v_c_ctx, v_w_ada, v_b_ada, v_norm1_g, v_norm2_g, v_w_in, v_ret_log_decay, v_rwkv_shift_mu, v_rwkv_w0, v_rwkv_w_up, v_rwkv_a0, v_rwkv_a_up, v_rwkv_g_up, v_rwkv_k_k, v_rwkv_k_a, v_rwkv_r_k, v_rwkv_ln_w, v_rwkv_ln_b, v_w_out, v_w_ff1, v_b_ff1, v_w_ff2, v_b_ff2, v_final_g):
    return _train_step(dict(locals()))
```

```python
import functools
import math

import jax
import jax.numpy as jnp
from jax import lax
from jax.experimental import pallas as pl
from jax.experimental.pallas import tpu as pltpu

f32 = jnp.float32
MXU_DTYPE = jnp.bfloat16

D_MODEL = 1024
RET_W = 512
RET_HEADS = 4
RET_DH = 128
RET_CHUNK = 128
RW_W = 512
RW_N = 64
DECAY_LORA = 64
AAA_LORA = 64
GATE_LORA = 128
LORA_W = DECAY_LORA + AAA_LORA + GATE_LORA
D_FF = 4096
RET_COLS = 4 * RET_W
SHIFT_COLS = 3 * RW_W + LORA_W
IN_COLS = RET_COLS + SHIFT_COLS
GRID_W = 64
ROPE_BASE = 10000.0
NORM_EPS = 1e-6
GN_EPS = 64e-5
W_DECAY_SCALE = math.exp(-0.5)
ADAM_LR, ADAM_B1, ADAM_B2, ADAM_EPS, ADAM_WD, ADAM_STEP = 0.001, 0.9, 0.999, 1e-08, 0.01, 10

TOK_TILE = 256
SCAN_CHUNK = 8
N_DEV = 8
V7X_VMEM_BYTES = 64 * 1024 * 1024
VMEM_LIMIT = V7X_VMEM_BYTES * 7 // 8


def _cparams(sem):
    return pltpu.CompilerParams(dimension_semantics=sem, vmem_limit_bytes=VMEM_LIMIT)


def _tile(n, cap):
    best = None
    for t in range(128, min(n, cap) + 1, 128):
        if n % t == 0:
            best = t
    return best if best is not None else n


def matmul(a, b, mode, name, out_dtype=f32):
    if mode == "nn":
        (m, k), (k2, n) = a.shape, b.shape
    elif mode == "nt":
        (m, k), (n, k2) = a.shape, b.shape
    else:
        (k, m), (k2, n) = a.shape, b.shape
    assert k == k2, (a.shape, b.shape, mode)
    tm, tn, tk = _tile(m, 512), _tile(n, 768), _tile(k, 1024)
    nk = k // tk
    dims = {"nn": ((1,), (0,)), "nt": ((1,), (1,)), "tn": ((0,), (0,))}[mode]

    def body(a_ref, b_ref, o_ref, acc_ref):
        kk = pl.program_id(2)

        @pl.when(kk == 0)
        def _():
            acc_ref[...] = jnp.zeros_like(acc_ref)

        acc_ref[...] += lax.dot_general(a_ref[...].astype(MXU_DTYPE), b_ref[...].astype(MXU_DTYPE),
                                        (dims, ((), ())), preferred_element_type=f32)

        @pl.when(kk == nk - 1)
        def _():
            o_ref[...] = acc_ref[...].astype(o_ref.dtype)

    if mode == "nn":
        a_spec = pl.BlockSpec((tm, tk), lambda i, j, q: (i, q))
        b_spec = pl.BlockSpec((tk, tn), lambda i, j, q: (q, j))
    elif mode == "nt":
        a_spec = pl.BlockSpec((tm, tk), lambda i, j, q: (i, q))
        b_spec = pl.BlockSpec((tn, tk), lambda i, j, q: (j, q))
    else:
        a_spec = pl.BlockSpec((tk, tm), lambda i, j, q: (q, i))
        b_spec = pl.BlockSpec((tk, tn), lambda i, j, q: (q, j))
    return pl.pallas_call(
        body, grid=(m // tm, n // tn, nk), in_specs=[a_spec, b_spec],
        out_specs=pl.BlockSpec((tm, tn), lambda i, j, q: (i, j)),
        out_shape=jax.ShapeDtypeStruct((m, n), out_dtype),
        scratch_shapes=[pltpu.VMEM((tm, tn), f32)],
        compiler_params=_cparams(("parallel", "parallel", "arbitrary")), name=name)(a, b)


class Tiled:
    def __init__(self, arr, w=None, cidx=0, toff=0):
        self.arr, self.w, self.cidx, self.toff = arr, (arr.shape[-1] if w is None else w), cidx, toff

    def spec(self):
        cidx, toff = self.cidx, self.toff
        return pl.BlockSpec((None, TOK_TILE, self.w), lambda b, i: (b, i + toff, cidx))


class Seg:
    def __init__(self, arr, seg, first):
        self.arr, self.seg, self.first = arr, seg, first

    def spec(self):
        seg = self.seg
        return pl.BlockSpec((None, None, 1, self.arr.shape[-1]), lambda b, i: (b, seg(i), 0, 0))


class Glob:
    def __init__(self, arr):
        self.arr = arr

    def spec(self):
        return pl.BlockSpec(self.arr.shape, lambda b, i: (0,) * self.arr.ndim)


def ew_forward(fn, name, bsz, n_tiles, ins, outs):
    n_in = len(ins)

    def body(*refs):
        res = fn(*[r[...] for r in refs[:n_in]])
        for o_ref, o in zip(refs[n_in:], res):
            o_ref[...] = o.astype(o_ref.dtype)

    out_specs = [pl.BlockSpec((None, TOK_TILE, w), lambda b, i: (b, i, 0)) for w, _ in outs]
    out_shape = [jax.ShapeDtypeStruct((bsz, n_tiles * TOK_TILE, w), dt) for w, dt in outs]
    return pl.pallas_call(body, grid=(bsz, n_tiles), in_specs=[d.spec() for d in ins], out_specs=out_specs,
                          out_shape=out_shape, compiler_params=_cparams(("parallel", "parallel")), name=name)(
        *[d.arr for d in ins])


def ew_backward(fn, name, bsz, n_tiles, ins, cts, want, grad_dtypes=None):
    n_in, n_ct = len(ins), len(cts)
    diff = [k for k in range(n_in) if want[k]]
    grad_dtypes = grad_dtypes or {}

    def body(*refs):
        b, i = pl.program_id(0), pl.program_id(1)
        vals = [r[...] for r in refs[:n_in]]
        ct_vals = tuple(r[...].astype(f32) for r in refs[n_in:n_in + n_ct])
        g_refs = refs[n_in + n_ct:]

        def f(*dvals):
            full = list(vals)
            for k, v in zip(diff, dvals):
                full[k] = v
            return tuple(fn(*full))

        _, vjp = jax.vjp(f, *[vals[k] for k in diff])
        grads = vjp(ct_vals)
        for k, g_ref, g in zip(diff, g_refs, grads):
            d = ins[k]
            if isinstance(d, Tiled):
                g_ref[...] = g.astype(g_ref.dtype)
            else:
                zero = d.first(i) if isinstance(d, Seg) else jnp.logical_and(b == 0, i == 0)

                @pl.when(zero)
                def _(g_ref=g_ref):
                    g_ref[...] = jnp.zeros_like(g_ref)

                g_ref[...] += g

    out_specs, out_shape = [], []
    for k in diff:
        d = ins[k]
        if isinstance(d, Tiled):
            out_specs.append(pl.BlockSpec((None, TOK_TILE, d.w), lambda b, i: (b, i, 0)))
            out_shape.append(jax.ShapeDtypeStruct((bsz, n_tiles * TOK_TILE, d.w), grad_dtypes.get(k, f32)))
        else:
            out_specs.append(d.spec())
            out_shape.append(jax.ShapeDtypeStruct(d.arr.shape, f32))
    return pl.pallas_call(body, grid=(bsz, n_tiles), in_specs=[d.spec() for d in ins] + [c.spec() for c in cts],
                          out_specs=out_specs, out_shape=out_shape,
                          compiler_params=_cparams(("arbitrary", "arbitrary")), name=name)(
        *[d.arr for d in ins], *[c.arr for c in cts])


@jax.custom_vjp
def _mxu_dot(a, b):
    return jnp.dot(a.astype(MXU_DTYPE), b.astype(MXU_DTYPE), preferred_element_type=f32)


def _mxu_dot_fwd(a, b):
    return _mxu_dot(a, b), (a, b)


def _mxu_dot_bwd(res, ct):
    a, b = res
    ct = ct.astype(MXU_DTYPE)
    da = lax.dot_general(ct, b.astype(MXU_DTYPE), (((1,), (1,)), ((), ())), preferred_element_type=f32)
    db = lax.dot_general(a.astype(MXU_DTYPE), ct, (((0,), (0,)), ((), ())), preferred_element_type=f32)
    return da, db


_mxu_dot.defvjp(_mxu_dot_fwd, _mxu_dot_bwd)


def _split_dot_impl(x, ones_mat):
    hi = x.astype(MXU_DTYPE)
    lo = (x - hi.astype(f32)).astype(MXU_DTYPE)
    return jnp.dot(hi, ones_mat, preferred_element_type=f32) + jnp.dot(lo, ones_mat, preferred_element_type=f32)


@jax.custom_vjp
def _split_dot(x, ones_mat):
    return _split_dot_impl(x, ones_mat)


def _split_dot_fwd(x, ones_mat):
    return _split_dot_impl(x, ones_mat), ones_mat


def _split_dot_bwd(ones_mat, ct):
    return _split_dot_impl(ct, ones_mat), None


_split_dot.defvjp(_split_dot_fwd, _split_dot_bwd)


def _block_ones(n, group):
    idx = jnp.arange(n) // group
    return (idx[:, None] == idx[None, :]).astype(MXU_DTYPE)


def _rms(x, g):
    return x * lax.rsqrt(jnp.mean(x * x, axis=-1, keepdims=True) + NORM_EPS) * g


def fn_norm_mod(h, shift, scale, g):
    return (_rms(h, g) * (1.0 + scale) + shift,)


def fn_rwkv_prepare(ks, lora, w0_f, w0_b, a0_f, a0_b, w_up_f, w_up_b, a_up_f, a_up_b, g_up, k_k, k_a, ones64):
    kkr = ks * k_k
    kk = kkr * lax.rsqrt(_split_dot(kkr * kkr, ones64) + 1e-12)
    outs = [kk]
    th = jnp.tanh(lora)
    for w0, a0, w_up, a_up in ((w0_f, a0_f, w_up_f, a_up_f), (w0_b, a0_b, w_up_b, a_up_b)):
        w = jnp.exp(-W_DECAY_SCALE * jax.nn.sigmoid(w0 + _mxu_dot(th, w_up)))
        a = jax.nn.sigmoid(a0 + _mxu_dot(lora, a_up))
        kt = ks * (1.0 + (a - 1.0) * k_a)
        outs += [w, a * kk, kt]
    outs.append(_mxu_dot(jax.nn.sigmoid(lora), g_up))
    return tuple(outs)


def fn_merge(o_f, o_b, g_ret, y_f, y_b, r, kt_f, v, g_rw, r_k, ln_w, ln_b, ones64, ones128):
    o = o_f + o_b
    ret = o * lax.rsqrt(_split_dot(o * o, ones128) * (1.0 / RET_DH) + NORM_EPS) * (g_ret * jax.nn.sigmoid(g_ret))
    y = y_f + y_b
    mean = _split_dot(y, ones64) * (1.0 / RW_N)
    yc = y - mean
    var = _split_dot(yc * yc, ones64) * (1.0 / RW_N)
    y_n = yc * lax.rsqrt(var + GN_EPS) * ln_w + ln_b
    bonus = _split_dot(r * kt_f * r_k, ones64) * v
    return ret, (y_n + bonus) * g_rw


def fn_resid_norm_mod(x, mix, gate, shift, scale, g):
    h1 = x + gate * mix
    return h1, _rms(h1, g) * (1.0 + scale) + shift


def fn_relu2(u, b1):
    return (jnp.square(jnp.maximum(u + b1, 0.0)),)


def fn_loss(h1, f, tgt, gate, b2, g):
    y = _rms(h1 + gate * (f + b2), g)
    err = jnp.square(y - tgt)
    return 0.5 * jnp.sum(jnp.mean(err, axis=-1, keepdims=True), axis=0, keepdims=True)


def loss_and_grads(h1, f, tgt, gate, b2, g, bsz, n_tiles):
    def body(h1_ref, f_ref, t_ref, gate_ref, b2_ref, g_ref, loss_ref, dh1_ref, df_ref, dgate_ref, db2_ref, dg_ref):
        b, i = pl.program_id(0), pl.program_id(1)
        tgt_v = t_ref[...]
        loss, vjp = jax.vjp(lambda a, c, e, p, q: fn_loss(a, c, tgt_v, e, p, q),
                            h1_ref[...], f_ref[...], gate_ref[...], b2_ref[...], g_ref[...])
        dh1, df, dgate, db2, dg = vjp(jnp.ones((1, 1), f32))
        dh1_ref[...] = dh1
        df_ref[...] = df.astype(df_ref.dtype)

        @pl.when(i == 0)
        def _():
            dgate_ref[...] = jnp.zeros_like(dgate_ref)

        @pl.when(jnp.logical_and(b == 0, i == 0))
        def _():
            loss_ref[...] = jnp.zeros_like(loss_ref)
            db2_ref[...] = jnp.zeros_like(db2_ref)
            dg_ref[...] = jnp.zeros_like(dg_ref)

        dgate_ref[...] += dgate
        db2_ref[...] += db2
        dg_ref[...] += dg
        loss_ref[...] += jnp.broadcast_to(loss, loss_ref.shape)

    tile = pl.BlockSpec((None, TOK_TILE, D_MODEL), lambda b, i: (b, i, 0))
    row = pl.BlockSpec((1, D_MODEL), lambda b, i: (0, 0))
    seg = pl.BlockSpec((None, None, 1, D_MODEL), lambda b, i: (b, 0, 0, 0))
    t_tok = n_tiles * TOK_TILE
    return pl.pallas_call(
        body, grid=(bsz, n_tiles), in_specs=[tile, tile, tile, seg, row, row],
        out_specs=[pl.BlockSpec((1, 128), lambda b, i: (0, 0)), tile, tile, seg, row, row],
        out_shape=[jax.ShapeDtypeStruct((1, 128), f32), jax.ShapeDtypeStruct((bsz, t_tok, D_MODEL), f32),
                   jax.ShapeDtypeStruct((bsz, t_tok, D_MODEL), MXU_DTYPE),
                   jax.ShapeDtypeStruct((bsz, 1, 1, D_MODEL), f32),
                   jax.ShapeDtypeStruct((1, D_MODEL), f32), jax.ShapeDtypeStruct((1, D_MODEL), f32)],
        compiler_params=_cparams(("arbitrary", "arbitrary")), name="loss_and_grads")(h1, f, tgt, gate, b2, g)


SHIFT_BLOCK = 256
HALO_ROWS = 8


def _shift_specs(n_tok, col0):
    per_tile = TOK_TILE // HALO_ROWS
    last = n_tok // HALO_ROWS - 1
    tile = pl.BlockSpec((None, TOK_TILE, SHIFT_BLOCK), lambda j, b, i: (b, i, col0 + j))
    prev = pl.BlockSpec((None, HALO_ROWS, SHIFT_BLOCK),
                        lambda j, b, i: (b, jnp.maximum(i * per_tile - 1, 0), col0 + j))
    nxt = pl.BlockSpec((None, HALO_ROWS, SHIFT_BLOCK),
                       lambda j, b, i: (b, jnp.minimum((i + 1) * per_tile, last), col0 + j))
    return tile, prev, nxt


def _shifted(p, prev_ref, next_ref, is_first, is_last):
    row = lax.broadcasted_iota(jnp.int32, p.shape, 0)
    prev_row = jnp.where(is_first, 0.0, prev_ref[HALO_ROWS - 1:HALO_ROWS, :].astype(f32))
    next_row = jnp.where(is_last, 0.0, next_ref[0:1, :].astype(f32))
    prev = jnp.where(row == 0, prev_row, pltpu.roll(p, 1, axis=0))
    nxt = jnp.where(row == TOK_TILE - 1, next_row, pltpu.roll(p, TOK_TILE - 1, axis=0))
    return prev, nxt


def token_shift(px, mu, seg_first, seg_last):
    bsz, n_tok, _ = px.shape
    n_tiles = n_tok // TOK_TILE

    def body(p_ref, prev_ref, next_ref, mu_ref, o_ref):
        i = pl.program_id(2)
        p = p_ref[...]
        prev, nxt = _shifted(p, prev_ref, next_ref, seg_first(i), seg_last(i))
        o_ref[...] = p + mu_ref[0:1, :] * (prev - p) + mu_ref[1:2, :] * (nxt - p)

    tile, prev, nxt = _shift_specs(n_tok, RET_COLS // SHIFT_BLOCK)
    return pl.pallas_call(
        body, grid=(SHIFT_COLS // SHIFT_BLOCK, bsz, n_tiles),
        in_specs=[tile, prev, nxt, pl.BlockSpec((2, SHIFT_BLOCK), lambda j, b, i: (0, j))],
        out_specs=pl.BlockSpec((None, TOK_TILE, SHIFT_BLOCK), lambda j, b, i: (b, i, j)),
        out_shape=jax.ShapeDtypeStruct((bsz, n_tok, SHIFT_COLS), f32),
        compiler_params=_cparams(("parallel", "parallel", "parallel")), name="token_shift")(px, px, px, mu)


def token_shift_bwd(dps, px, mu, seg_first, seg_last):
    bsz, n_tok, _ = px.shape
    n_tiles = n_tok // TOK_TILE

    def body(d_ref, dprev_ref, dnext_ref, p_ref, prev_ref, next_ref, mu_ref, dp_ref, dmu_ref):
        b, i = pl.program_id(1), pl.program_id(2)
        first, last = seg_first(i), seg_last(i)
        d, p = d_ref[...], p_ref[...]
        d_prev, d_next = _shifted(d, dprev_ref, dnext_ref, first, last)
        p_prev, p_next = _shifted(p, prev_ref, next_ref, first, last)
        mu0, mu1 = mu_ref[0:1, :], mu_ref[1:2, :]
        dp_ref[...] = (d + mu0 * (d_next - d) + mu1 * (d_prev - d)).astype(dp_ref.dtype)

        @pl.when(jnp.logical_and(b == 0, i == 0))
        def _():
            dmu_ref[...] = jnp.zeros_like(dmu_ref)

        dmu_ref[0:1, :] += jnp.sum(d * (p_prev - p), axis=0, keepdims=True)
        dmu_ref[1:2, :] += jnp.sum(d * (p_next - p), axis=0, keepdims=True)

    dtile, dprev, dnext = _shift_specs(n_tok, 0)
    tile, prev, nxt = _shift_specs(n_tok, RET_COLS // SHIFT_BLOCK)
    mu_spec = pl.BlockSpec((2, SHIFT_BLOCK), lambda j, b, i: (0, j))
    return pl.pallas_call(
        body, grid=(SHIFT_COLS // SHIFT_BLOCK, bsz, n_tiles),
        in_specs=[dtile, dprev, dnext, tile, prev, nxt, mu_spec],
        out_specs=[pl.BlockSpec((None, TOK_TILE, SHIFT_BLOCK), lambda j, b, i: (b, i, j)), mu_spec],
        out_shape=[jax.ShapeDtypeStruct((bsz, n_tok, SHIFT_COLS), MXU_DTYPE),
                   jax.ShapeDtypeStruct((2, SHIFT_COLS), f32)],
        compiler_params=_cparams(("arbitrary", "arbitrary", "arbitrary")), name="token_shift_bwd")(
        dps, dps, dps, px, px, px, mu)


def _dg(a, b, ca, cb):
    return lax.dot_general(a.astype(MXU_DTYPE), b.astype(MXU_DTYPE), (((ca,), (cb,)), ((), ())),
                           preferred_element_type=f32)


@jax.custom_vjp
def _mm_nt(a, b):
    return _dg(a, b, 1, 1)


_mm_nt.defvjp(lambda a, b: (_dg(a, b, 1, 1), (a, b)),
              lambda res, ct: (_dg(ct, res[1], 1, 0), _dg(ct, res[0], 0, 0)))


@jax.custom_vjp
def _mm_tn(a, b):
    return _dg(a, b, 0, 0)


_mm_tn.defvjp(lambda a, b: (_dg(a, b, 0, 0), (a, b)),
              lambda res, ct: (_dg(res[1], ct, 1, 1), _dg(res[0], ct, 1, 0)))


def _ret_chunk(state, q_raw, k_raw, v, cos, sin, perm, ld_row, head, reverse):
    c = RET_CHUNK
    lane = lax.broadcasted_iota(jnp.int32, ld_row.shape, 1)
    lg = -jnp.exp(jnp.sum(jnp.where(lane == head, ld_row, 0.0), axis=-1, keepdims=True))
    rot = lambda t: t * cos + jnp.dot(t, perm, preferred_element_type=f32, precision=lax.Precision.HIGHEST) * sin
    q = rot(q_raw)
    k = rot(k_raw) * (RET_DH ** -0.5)
    ti = lax.broadcasted_iota(jnp.int32, (c, 1), 0).astype(f32)
    tj = lax.broadcasted_iota(jnp.int32, (1, c), 1).astype(f32)
    if not reverse:
        dist, mask, q_exp, k_exp = ti - tj, (ti - tj) >= 0, ti + 1.0, c - 1.0 - ti
    else:
        dist, mask, q_exp, k_exp = tj - ti, (tj - ti) > 0, c - ti, ti
    decay = jnp.where(mask, jnp.exp(lg * jnp.maximum(dist, 0.0)), 0.0)
    scores = _mm_nt(q, k) * decay
    out = _mxu_dot(scores, v) + _mxu_dot(q * jnp.exp(lg * q_exp), state)
    new_state = state * jnp.exp(lg * c) + _mm_tn(k * jnp.exp(lg * k_exp), v)
    return out, new_state


def _ret_specs(order):
    qkv = [pl.BlockSpec((None, RET_CHUNK, RET_W), functools.partial(lambda b, i, col: (b, order(i), col), col=col))
           for col in range(3)]
    tab = pl.BlockSpec((RET_CHUNK, RET_DH), lambda b, i: (order(i), 0))
    const = pl.BlockSpec((RET_DH, RET_DH), lambda b, i: (0, 0))
    ld = pl.BlockSpec((1, RET_DH), lambda b, i: (0, 0))
    return qkv, tab, const, ld


def retention_fwd(px, cos, sin, perm, ld_row, order, reverse, name):
    bsz, n_tok, _ = px.shape
    n_ch = n_tok // RET_CHUNK

    def body(q_ref, k_ref, v_ref, cos_ref, sin_ref, perm_ref, ld_ref, o_ref, sv_ref, st_ref):
        @pl.when(pl.program_id(1) == 0)
        def _():
            st_ref[...] = jnp.zeros_like(st_ref)

        for h in range(RET_HEADS):
            sl = slice(h * RET_DH, (h + 1) * RET_DH)
            s = st_ref[h]
            sv_ref[h] = s
            o, s_new = _ret_chunk(s, q_ref[:, sl], k_ref[:, sl], v_ref[:, sl], cos_ref[...], sin_ref[...],
                                  perm_ref[...], ld_ref[...], h, reverse)
            o_ref[:, sl] = o
            st_ref[h] = s_new

    qkv, tab, const, ld = _ret_specs(order)
    return pl.pallas_call(
        body, grid=(bsz, n_ch), in_specs=[*qkv, tab, tab, const, ld],
        out_specs=[pl.BlockSpec((None, RET_CHUNK, RET_W), lambda b, i: (b, order(i), 0)),
                   pl.BlockSpec((None, None, RET_HEADS, RET_DH, RET_DH), lambda b, i: (b, i, 0, 0, 0))],
        out_shape=[jax.ShapeDtypeStruct((bsz, n_tok, RET_W), f32),
                   jax.ShapeDtypeStruct((bsz, n_ch, RET_HEADS, RET_DH, RET_DH), f32)],
        scratch_shapes=[pltpu.VMEM((RET_HEADS, RET_DH, RET_DH), f32)],
        compiler_params=_cparams(("parallel", "arbitrary")), name=name)(px, px, px, cos, sin, perm, ld_row)


def retention_bwd(do, px, states, cos, sin, perm, ld_row, order, reverse, name):
    bsz, n_tok, _ = px.shape
    n_ch = n_tok // RET_CHUNK
    back = lambda i: order(n_ch - 1 - i)

    def body(do_ref, q_ref, k_ref, v_ref, sv_ref, cos_ref, sin_ref, perm_ref, ld_ref,
             dq_ref, dk_ref, dv_ref, dld_ref, dst_ref):
        b, i = pl.program_id(0), pl.program_id(1)

        @pl.when(i == 0)
        def _():
            dst_ref[...] = jnp.zeros_like(dst_ref)

        @pl.when(jnp.logical_and(b == 0, i == 0))
        def _():
            dld_ref[...] = jnp.zeros_like(dld_ref)

        cos_v, sin_v, perm_v = cos_ref[...], sin_ref[...], perm_ref[...]
        for h in range(RET_HEADS):
            sl = slice(h * RET_DH, (h + 1) * RET_DH)
            f = lambda s, q, k, v, ld, h=h: _ret_chunk(s, q, k, v, cos_v, sin_v, perm_v, ld, h, reverse)
            _, vjp = jax.vjp(f, sv_ref[h], q_ref[:, sl], k_ref[:, sl], v_ref[:, sl], ld_ref[...])
            ds, dq, dk, dv, dld = vjp((do_ref[:, sl], dst_ref[h]))
            dst_ref[h] = ds
            dq_ref[:, sl] = dq
            dk_ref[:, sl] = dk
            dv_ref[:, sl] = dv
            dld_ref[...] += dld

    qkv, tab, const, ld = _ret_specs(back)
    tok = pl.BlockSpec((None, RET_CHUNK, RET_W), lambda b, i: (b, back(i), 0))
    return pl.pallas_call(
        body, grid=(bsz, n_ch),
        in_specs=[tok, *qkv,
                  pl.BlockSpec((None, None, RET_HEADS, RET_DH, RET_DH), lambda b, i: (b, n_ch - 1 - i, 0, 0, 0)),
                  tab, tab, const, ld],
        out_specs=[tok, tok, tok, ld],
        out_shape=[jax.ShapeDtypeStruct((bsz, n_tok, RET_W), f32)] * 3 + [jax.ShapeDtypeStruct((1, RET_DH), f32)],
        scratch_shapes=[pltpu.VMEM((RET_HEADS, RET_DH, RET_DH), f32)],
        compiler_params=_cparams(("arbitrary", "arbitrary")), name=name)(
        do, px, px, px, states, cos, sin, perm, ld_row)


HALF_W = RW_W // 2


def _head_sum(x, ones):
    xm = x.astype(MXU_DTYPE)
    return jnp.concatenate([jnp.dot(xm[:, :HALF_W], ones, preferred_element_type=f32),
                            jnp.dot(xm[:, HALF_W:], ones, preferred_element_type=f32)], axis=1)


def _stack(parts):
    return jnp.concatenate(parts, axis=0)


def _row(ref, b, t):
    return ref[b, pl.ds(t, 1), :]


SCAN_GROUP = 4
SCAN_DIRS = ((False, True), (True, False))


def _grouped_head_sum(buf_ref, ones, bsz):
    rows = SCAN_GROUP * bsz * RW_N

    def group(gi, carry):
        sl = pl.ds(gi * SCAN_GROUP, SCAN_GROUP)
        buf_ref[sl] = _head_sum(buf_ref[sl].reshape(rows, RW_W), ones).reshape(SCAN_GROUP, bsz, RW_N, RW_W)
        return carry

    lax.fori_loop(0, SCAN_CHUNK // SCAN_GROUP, group, 0)


def _to_columns(row_ref, dst_ref, eye, ones, bsz):
    for t in range(SCAN_CHUNK):
        for b in range(bsz):
            dst_ref[t, b] = row_ref[b, t:t + 1, :] * eye
    _grouped_head_sum(dst_ref, ones, bsz)


def _to_rows(buf_ref, row_ref, eye, ones, bsz):
    _grouped_head_sum(buf_ref, ones, bsz)
    for t in range(SCAN_CHUNK):
        for b in range(bsz):
            row_ref[b, t:t + 1, :] = jnp.sum(buf_ref[t, b] * eye, axis=0, keepdims=True)


def _scan_specs(bsz, order):
    rows = lambda col=0: pl.BlockSpec((bsz, SCAN_CHUNK, RW_W), lambda i: (0, order(i), col))
    eye = pl.BlockSpec((RW_N, RW_W), lambda i: (0, 0))
    ones = pl.BlockSpec((HALF_W, HALF_W), lambda i: (0, 0))
    return rows, eye, ones


def _scan_update(sp, kk_t, w_t, b_t, kt_t, vcol, ones, bsz):
    removed = _head_sum(_stack([sp[b] * kk_t[b] for b in range(bsz)]), ones)
    rem = [removed[b * RW_N:(b + 1) * RW_N] for b in range(bsz)]
    new = [sp[b] * w_t[b] - rem[b] * b_t[b] + vcol[b] * kt_t[b] for b in range(bsz)]
    return rem, new


N_ROWS_FWD = 6
N_ROWS_BWD = 7


def rwkv_scan_fwd(rows_in, eye, ones, orders, name):
    bsz, n_tok, _ = rows_in[0][0][0].shape
    n_ch = n_tok // SCAN_CHUNK
    rng = range(bsz)

    def body(*refs):
        rows = [refs[:N_ROWS_FWD], refs[N_ROWS_FWD:2 * N_ROWS_FWD]]
        eye_ref, ones_ref, y0, y1, cs0, cs1, s0, s1, vc0, vc1, yb0, yb1 = refs[2 * N_ROWS_FWD:]
        y_refs, cs_refs, s_refs, vcol_refs, ybuf_refs = (y0, y1), (cs0, cs1), (s0, s1), (vc0, vc1), (yb0, yb1)
        eye_v, ones_v = eye_ref[...], ones_ref[...]
        for d in range(2):
            @pl.when(pl.program_id(0) == 0)
            def _(d=d):
                s_refs[d][...] = jnp.zeros_like(s_refs[d])

            cs_refs[d][...] = s_refs[d][...]
            _to_columns(rows[d][2], vcol_refs[d], eye_v, ones_v, bsz)

        def step(j, carry):
            for d, (reverse, inclusive) in enumerate(SCAN_DIRS):
                _, kk_ref, _, w_ref, b_ref, kt_ref = rows[d]
                t = SCAN_CHUNK - 1 - j if reverse else j
                sp = [s_refs[d][b] for b in rng]
                _, new = _scan_update(sp, [_row(kk_ref, b, t) for b in rng], [_row(w_ref, b, t) for b in rng],
                                      [_row(b_ref, b, t) for b in rng], [_row(kt_ref, b, t) for b in rng],
                                      [vcol_refs[d][t, b] for b in rng], ones_v, bsz)
                for b in rng:
                    s_refs[d][b] = new[b]
                    ybuf_refs[d][t, b] = (new[b] if inclusive else sp[b]) * _row(rows[d][0], b, t)
            return carry

        lax.fori_loop(0, SCAN_CHUNK, step, 0)
        for d in range(2):
            _to_rows(ybuf_refs[d], y_refs[d], eye_v, ones_v, bsz)

    specs = [_scan_specs(bsz, orders[d]) for d in range(2)]
    eye_spec, ones_spec = specs[0][1], specs[0][2]
    hist = pltpu.VMEM((SCAN_CHUNK, bsz, RW_N, RW_W), f32)
    state = pltpu.VMEM((bsz, RW_N, RW_W), f32)
    start_spec = pl.BlockSpec((None, bsz, RW_N, RW_W), lambda i: (i, 0, 0, 0))
    return pl.pallas_call(
        body, grid=(n_ch,),
        in_specs=[specs[d][0](col) for d in range(2) for _, col in rows_in[d]] + [eye_spec, ones_spec],
        out_specs=[specs[0][0](), specs[1][0](), start_spec, start_spec],
        out_shape=[jax.ShapeDtypeStruct((bsz, n_tok, RW_W), f32)] * 2
        + [jax.ShapeDtypeStruct((n_ch, bsz, RW_N, RW_W), f32)] * 2,
        scratch_shapes=[state, state, hist, hist, hist, hist],
        compiler_params=_cparams(("arbitrary",)), name=name)(*[a for d in range(2) for a, _ in rows_in[d]], eye, ones)


def rwkv_scan_bwd(rows_in, starts, eye, ones, orders, name):
    bsz, n_tok, _ = rows_in[0][0][0].shape
    n_ch = n_tok // SCAN_CHUNK
    backs = [functools.partial(lambda i, order: order(n_ch - 1 - i), order=orders[d]) for d in range(2)]
    rng = range(bsz)
    n_out, n_scr = 6, 8

    def body(*refs):
        rows = [refs[:N_ROWS_BWD], refs[N_ROWS_BWD:2 * N_ROWS_BWD]]
        rest = refs[2 * N_ROWS_BWD:]
        cs_refs, (eye_ref, ones_ref) = rest[:2], rest[2:4]
        outs = [rest[4:4 + n_out], rest[4 + n_out:4 + 2 * n_out]]
        scr = [rest[4 + 2 * n_out:4 + 2 * n_out + n_scr], rest[4 + 2 * n_out + n_scr:]]
        eye_v, ones_v = eye_ref[...], ones_ref[...]
        for d in range(2):
            s_ref, ds_ref, vcol_ref, dycol_ref = scr[d][:4]

            @pl.when(pl.program_id(0) == 0)
            def _(ds_ref=ds_ref):
                ds_ref[...] = jnp.zeros_like(ds_ref)

            _to_columns(rows[d][3], vcol_ref, eye_v, ones_v, bsz)
            _to_columns(rows[d][0], dycol_ref, eye_v, ones_v, bsz)
            s_ref[...] = cs_refs[d][...]

        def fstep(j, carry):
            for d, (reverse, _) in enumerate(SCAN_DIRS):
                _, _, kk_ref, _, w_ref, b_ref, kt_ref = rows[d]
                s_ref, _, vcol_ref, _, hist_ref, rem_ref, _, _ = scr[d]
                t = SCAN_CHUNK - 1 - j if reverse else j
                sp = [s_ref[b] for b in rng]
                rem, new = _scan_update(sp, [_row(kk_ref, b, t) for b in rng], [_row(w_ref, b, t) for b in rng],
                                        [_row(b_ref, b, t) for b in rng], [_row(kt_ref, b, t) for b in rng],
                                        [vcol_ref[t, b] for b in rng], ones_v, bsz)
                for b in rng:
                    hist_ref[t, b] = sp[b]
                    rem_ref[t, b] = rem[b]
                    s_ref[b] = new[b]
            return carry

        lax.fori_loop(0, SCAN_CHUNK, fstep, 0)

        def bstep(j, carry):
            for d, (reverse, inclusive) in enumerate(SCAN_DIRS):
                _, r_ref, kk_ref, _, w_ref, b_ref, _ = rows[d]
                _, ds_ref, _, dycol_ref, _, _, dsh_ref, drem_ref = scr[d]
                t = j if reverse else SCAN_CHUNK - 1 - j
                ds = [ds_ref[b] for b in rng]
                read = [dycol_ref[t, b] * _row(r_ref, b, t) for b in rng]
                if inclusive:
                    ds = [ds[b] + read[b] for b in rng]
                drem_all = -_head_sum(_stack([ds[b] * _row(b_ref, b, t) for b in rng]), ones_v)
                for b in rng:
                    drem = drem_all[b * RW_N:(b + 1) * RW_N]
                    dsh_ref[t, b] = ds[b]
                    drem_ref[t, b] = drem
                    dsp = ds[b] * _row(w_ref, b, t) + drem * _row(kk_ref, b, t)
                    ds_ref[b] = dsp if inclusive else dsp + read[b]
            return carry

        lax.fori_loop(0, SCAN_CHUNK, bstep, 0)

        rsum = lambda z: jnp.sum(z, axis=0, keepdims=True)
        for d, (reverse, inclusive) in enumerate(SCAN_DIRS):
            kt_ref = rows[d][6]
            dr_ref, dkk_ref, dv_ref, dw_ref, db_ref, dkt_ref = outs[d]
            s_ref, _, vcol_ref, dycol_ref, hist_ref, rem_ref, dsh_ref, drem_ref = scr[d]
            for t in range(SCAN_CHUNK):
                ts = slice(t, t + 1)
                after = t - 1 if reverse else t + 1
                for b in rng:
                    sp, ds = hist_ref[t, b], dsh_ref[t, b]
                    if not inclusive:
                        seen = sp
                    else:
                        seen = hist_ref[after, b] if 0 <= after < SCAN_CHUNK else s_ref[b]
                    dr_ref[b, ts, :] = rsum(seen * dycol_ref[t, b])
                    dw_ref[b, ts, :] = rsum(ds * sp)
                    db_ref[b, ts, :] = -rsum(ds * rem_ref[t, b])
                    dkt_ref[b, ts, :] = rsum(ds * vcol_ref[t, b])
                    dkk_ref[b, ts, :] = rsum(sp * drem_ref[t, b])
                    dycol_ref[t, b] = ds * kt_ref[b, ts, :]
            _to_rows(dycol_ref, dv_ref, eye_v, ones_v, bsz)

    specs = [_scan_specs(bsz, backs[d]) for d in range(2)]
    eye_spec, ones_spec = specs[0][1], specs[0][2]
    hist = pltpu.VMEM((SCAN_CHUNK, bsz, RW_N, RW_W), f32)
    state = pltpu.VMEM((bsz, RW_N, RW_W), f32)
    start_spec = pl.BlockSpec((None, bsz, RW_N, RW_W), lambda i: (n_ch - 1 - i, 0, 0, 0))
    return pl.pallas_call(
        body, grid=(n_ch,),
        in_specs=[specs[d][0](col) for d in range(2) for _, col in rows_in[d]]
        + [start_spec, start_spec, eye_spec, ones_spec],
        out_specs=[specs[d][0]() for d in range(2) for _ in range(n_out)],
        out_shape=[jax.ShapeDtypeStruct((bsz, n_tok, RW_W), f32)] * (2 * n_out),
        scratch_shapes=[state, state, hist, hist, hist, hist, hist, hist] * 2,
        compiler_params=_cparams(("arbitrary",)), name=name)(
        *[a for d in range(2) for a, _ in rows_in[d]], *starts, eye, ones)


MOD_NAMES = ("shift1", "scale1", "gate1", "shift2", "scale2", "gate2")


def _rope_tables(t_ctx, t_x):
    quarter = RET_DH // 4
    pos = jnp.arange(t_x)
    inv = jnp.power(ROPE_BASE, -jnp.arange(0, 2 * quarter, 2, dtype=f32) / (2 * quarter))
    ang_r = (pos // GRID_W).astype(f32)[:, None] * inv[None, :]
    ang_c = (pos % GRID_W).astype(f32)[:, None] * inv[None, :]
    cos = jnp.concatenate([jnp.cos(ang_r)] * 2 + [jnp.cos(ang_c)] * 2, axis=1)
    sin = jnp.concatenate([-jnp.sin(ang_r), jnp.sin(ang_r), -jnp.sin(ang_c), jnp.sin(ang_c)], axis=1)
    cos = jnp.concatenate([jnp.ones((t_ctx, RET_DH), f32), cos], axis=0)
    sin = jnp.concatenate([jnp.zeros((t_ctx, RET_DH), f32), sin], axis=0)
    lane = jnp.arange(RET_DH)
    partner = jnp.where(lane % (2 * quarter) < quarter, lane + quarter, lane - quarter)
    perm = (lane[:, None] == partner[None, :]).astype(f32)
    return cos, sin, perm


def _pad_rows(w, lo, total):
    return jnp.pad(w, ((lo, total - lo - w.shape[0]), (0, 0)))


def layer_step(x, ctx, tgt, mod_x, mod_ctx, wt):
    bsz, t_x, _ = x.shape
    t_c = ctx.shape[1]
    t_all = t_c + t_x
    n_ct, n_xt = t_c // TOK_TILE, t_x // TOK_TILE
    n_t = n_ct + n_xt
    assert t_c % TOK_TILE == 0 and t_x % TOK_TILE == 0 and t_c % RET_CHUNK == 0

    seg = lambda i: (i >= n_ct).astype(jnp.int32)
    seg_first = lambda i: jnp.logical_or(i == 0, i == n_ct)
    seg_last = lambda i: jnp.logical_or(i == n_ct - 1, i == n_t - 1)
    mod_all = {n: jnp.stack([jnp.broadcast_to(mod_ctx[k], (bsz, D_MODEL)), mod_x[:, k]], axis=1)[:, :, None, :]
               for k, n in enumerate(MOD_NAMES)}
    mod_lat = {n: mod_x[:, k][:, None, None, :] for k, n in enumerate(MOD_NAMES)}
    both = lambda n: Seg(mod_all[n], seg, seg_first)
    lat = lambda n: Seg(mod_lat[n], lambda i: 0, lambda i: i == 0)
    flat = lambda a: a.reshape(-1, a.shape[-1])
    padc = lambda a: jnp.pad(a, ((0, 0), (t_c, 0), (0, 0)))

    def chunk_orders(n_ctx_chunks, n_chunks):
        fwd = lambda i: i
        bwd = lambda i: jnp.where(i < n_ctx_chunks, n_ctx_chunks - 1 - i, n_chunks + n_ctx_chunks - 1 - i)
        return fwd, bwd

    ones64, ones128 = _block_ones(RW_W, RW_N), _block_ones(RET_W, RET_DH)
    ones_half = _block_ones(HALF_W, RW_N)
    eye = jnp.tile(jnp.eye(RW_N, dtype=f32), (1, RW_W // RW_N))
    cos, sin, perm = _rope_tables(t_c, t_x)
    ld_rows = [jnp.pad(wt["ret_log_decay"][d][None, :], ((0, 0), (0, RET_DH - RET_HEADS))) for d in range(2)]
    w_up_pad = [_pad_rows(wt["rwkv_w_up"][d], 0, LORA_W) for d in range(2)]
    a_up_pad = [_pad_rows(wt["rwkv_a_up"][d], DECAY_LORA, LORA_W) for d in range(2)]
    g_up_pad = _pad_rows(wt["rwkv_g_up"], DECAY_LORA + AAA_LORA, LORA_W)
    row = lambda a, d: a[d][None, :]

    h = jnp.concatenate([ctx, x], axis=1)
    norm1_ins = lambda: [Tiled(h), both("shift1"), both("scale1"), Glob(wt["norm1_g"])]
    (n1,) = ew_forward(fn_norm_mod, "norm1", bsz, n_t, norm1_ins(), [(D_MODEL, MXU_DTYPE)])
    px = matmul(flat(n1), wt["w_in"], "nn", "proj_in").reshape(bsz, t_all, IN_COLS)
    ps = token_shift(px, wt["rwkv_shift_mu"], seg_first, seg_last)

    def prep_ins(toff=0):
        return [Tiled(ps, RW_W, 1), Tiled(ps, LORA_W, 3 * RW_W // LORA_W),
                Glob(row(wt["rwkv_w0"], 0)), Glob(row(wt["rwkv_w0"], 1)),
                Glob(row(wt["rwkv_a0"], 0)), Glob(row(wt["rwkv_a0"], 1)),
                Glob(w_up_pad[0]), Glob(w_up_pad[1]), Glob(a_up_pad[0]), Glob(a_up_pad[1]), Glob(g_up_pad),
                Glob(wt["rwkv_k_k"]), Glob(wt["rwkv_k_a"]), Glob(ones64)]

    kk, w_f, b_f, kt_f, w_b, b_b, kt_b, g_rw = ew_forward(fn_rwkv_prepare, "rwkv_prepare", bsz, n_t, prep_ins(),
                                                           [(RW_W, f32)] * 8)
    rw_order = chunk_orders(t_c // SCAN_CHUNK, t_all // SCAN_CHUNK)
    ret_order = chunk_orders(t_c // RET_CHUNK, t_all // RET_CHUNK)
    scan_rows = [[(ps, 0), (kk, 0), (ps, 2), (w_f, 0), (b_f, 0), (kt_f, 0)],
                 [(ps, 0), (kk, 0), (ps, 2), (w_b, 0), (b_b, 0), (kt_b, 0)]]
    *y, start_f, start_b = rwkv_scan_fwd(scan_rows, eye, ones_half, rw_order, "rwkv_scan_fwd")
    o, ret_states = [], []
    for d in range(2):
        o_d, st_d = retention_fwd(px, cos, sin, perm, ld_rows[d], ret_order[d], SCAN_DIRS[d][0], f"retention_fwd{d}")
        o.append(o_d), ret_states.append(st_d)

    def merge_ins():
        return [Tiled(o[0], toff=n_ct), Tiled(o[1], toff=n_ct), Tiled(px, RET_W, 3, n_ct),
                Tiled(y[0], toff=n_ct), Tiled(y[1], toff=n_ct), Tiled(ps, RW_W, 0, n_ct), Tiled(kt_f, toff=n_ct),
                Tiled(ps, RW_W, 2, n_ct), Tiled(g_rw, toff=n_ct),
                Glob(wt["rwkv_r_k"]), Glob(wt["rwkv_ln_w"]), Glob(wt["rwkv_ln_b"]), Glob(ones64), Glob(ones128)]

    ret_out, rw_out = ew_forward(fn_merge, "merge_heads", bsz, n_xt, merge_ins(), [(RET_W, MXU_DTYPE), (RW_W, MXU_DTYPE)])
    merged = jnp.concatenate([ret_out, rw_out], axis=-1)
    mix = matmul(flat(merged), wt["w_out"], "nn", "proj_out").reshape(bsz, t_x, D_MODEL)
    resid_ins = lambda: [Tiled(x), Tiled(mix), lat("gate1"), lat("shift2"), lat("scale2"), Glob(wt["norm2_g"])]
    h1, n2 = ew_forward(fn_resid_norm_mod, "resid_norm2", bsz, n_xt, resid_ins(), [(D_MODEL, f32), (D_MODEL, MXU_DTYPE)])
    u = matmul(flat(n2), wt["w_ff1"], "nn", "ff1").reshape(bsz, t_x, D_FF)
    relu_ins = lambda: [Tiled(u), Glob(wt["b_ff1"])]
    (act,) = ew_forward(fn_relu2, "relu2", bsz, n_xt, relu_ins(), [(D_FF, MXU_DTYPE)])
    ff = matmul(flat(act), wt["w_ff2"], "nn", "ff2").reshape(bsz, t_x, D_MODEL)

    g = {}
    loss, dh1, dff, dgate2, g["b_ff2"], g["final_g"] = loss_and_grads(
        h1, ff, tgt, mod_lat["gate2"], wt["b_ff2"], wt["final_g"], bsz, n_xt)
    dact = matmul(flat(dff), wt["w_ff2"], "nt", "ff2_dx").reshape(bsz, t_x, D_FF)
    g["w_ff2"] = matmul(flat(act), flat(dff), "tn", "ff2_dw")
    du, g["b_ff1"] = ew_backward(fn_relu2, "relu2_bwd", bsz, n_xt, relu_ins(), [Tiled(dact)], [True, True],
                                 {0: MXU_DTYPE})
    dn2 = matmul(flat(du), wt["w_ff1"], "nt", "ff1_dx").reshape(bsz, t_x, D_MODEL)
    g["w_ff1"] = matmul(flat(n2), flat(du), "tn", "ff1_dw")
    dx_res, dmix, dgate1, dshift2, dscale2, g["norm2_g"] = ew_backward(
        fn_resid_norm_mod, "resid_norm2_bwd", bsz, n_xt, resid_ins(), [Tiled(dh1), Tiled(dn2)], [True] * 6,
        {1: MXU_DTYPE})
    dmerged = matmul(flat(dmix), wt["w_out"], "nt", "proj_out_dx").reshape(bsz, t_x, D_MODEL)
    g["w_out"] = matmul(flat(merged), flat(dmix), "tn", "proj_out_dw")
    (do, dg_ret, dy, dr_m, dkt_m, dv_m, dg_rw, g["rwkv_r_k"], g["rwkv_ln_w"], g["rwkv_ln_b"]) = ew_backward(
        fn_merge, "merge_heads_bwd", bsz, n_xt, merge_ins(), [Tiled(dmerged, RET_W, 0), Tiled(dmerged, RW_W, 1)],
        [True, False, True, True, False, True, True, True, True, True, True, True, False, False])
    do, dy = padc(do), padc(dy)

    dqkv, dld = [], []
    for d in range(2):
        *dqkv_d, dld_d = retention_bwd(do, px, ret_states[d], cos, sin, perm, ld_rows[d], ret_order[d],
                                       SCAN_DIRS[d][0], f"retention_bwd{d}")
        dqkv.append(dqkv_d), dld.append(dld_d[0, :RET_HEADS])
    g["ret_log_decay"] = jnp.stack(dld)
    (dr_f, dkk_f, dv_f, dw_f, db_f, dkt_f, dr_b, dkk_b, dv_b, dw_b, db_b, dkt_b) = rwkv_scan_bwd(
        [[(dy, 0)] + scan_rows[d] for d in range(2)], (start_f, start_b), eye, ones_half, rw_order, "rwkv_scan_bwd")
    prep_cts = [dkk_f + dkk_b, dw_f, db_f, dkt_f + padc(dkt_m), dw_b, db_b, dkt_b, padc(dg_rw)]
    (dks, dlora, dw0_f, dw0_b, da0_f, da0_b, dwup_f, dwup_b, daup_f, daup_b, dgup, g["rwkv_k_k"],
     g["rwkv_k_a"]) = ew_backward(fn_rwkv_prepare, "rwkv_prepare_bwd", bsz, n_t, prep_ins(),
                                  [Tiled(c) for c in prep_cts], [True] * 13 + [False])
    g["rwkv_w0"] = jnp.concatenate([dw0_f, dw0_b], axis=0)
    g["rwkv_a0"] = jnp.concatenate([da0_f, da0_b], axis=0)
    g["rwkv_w_up"] = jnp.stack([dwup_f[:DECAY_LORA], dwup_b[:DECAY_LORA]])
    g["rwkv_a_up"] = jnp.stack([daup_f[DECAY_LORA:DECAY_LORA + AAA_LORA], daup_b[DECAY_LORA:DECAY_LORA + AAA_LORA]])
    g["rwkv_g_up"] = dgup[DECAY_LORA + AAA_LORA:]
    dps = jnp.concatenate([dr_f + dr_b + padc(dr_m), dks, dv_f + dv_b + padc(dv_m), dlora], axis=-1)
    dp_rw, g["rwkv_shift_mu"] = token_shift_bwd(dps, px, wt["rwkv_shift_mu"], seg_first, seg_last)
    dpx = jnp.concatenate([(dqkv[0][k] + dqkv[1][k]).astype(MXU_DTYPE) for k in range(3)]
                          + [padc(dg_ret).astype(MXU_DTYPE), dp_rw], axis=-1)
    dn1 = matmul(flat(dpx), wt["w_in"], "nt", "proj_in_dx").reshape(bsz, t_all, D_MODEL)
    g["w_in"] = matmul(flat(n1), flat(dpx), "tn", "proj_in_dw")
    dh, dshift1, dscale1, g["norm1_g"] = ew_backward(fn_norm_mod, "norm1_bwd", bsz, n_t, norm1_ins(), [Tiled(dn1)],
                                                     [True] * 4)
    grad_x = dh[:, t_c:] + dx_res
    zeros = jnp.zeros((D_MODEL,), f32)
    g["mod_x"] = jnp.stack([dshift1[:, 1, 0], dscale1[:, 1, 0], dgate1[:, 0, 0], dshift2[:, 0, 0], dscale2[:, 0, 0],
                            dgate2[:, 0, 0]], axis=1)
    g["mod_ctx"] = jnp.stack([dshift1[:, 0, 0].sum(0), dscale1[:, 0, 0].sum(0), zeros, zeros, zeros, zeros])
    return loss, grad_x, g


MESH_ID = pl.DeviceIdType.MESH
ALL_PEERS = [(dx, dy, dc) for dx in (0, 1) for dy in (0, 1) for dc in (0, 1)][1:]
SIBLING = [(0, 0, 1)]
CHIP_SLOTS = (0, 2, 4, 6)


def _mesh_pos():
    return lax.axis_index("x"), lax.axis_index("y"), lax.axis_index("c")


def _device_slot():
    x, y, c = _mesh_pos()
    return 4 * x + 2 * y + c


def exchange(arrs, gather, peers, name, by_core=False, pieces=1):
    n, n_peers = len(arrs), len(peers)
    n_slots = 2 if by_core else N_DEV
    slot = (lambda x, y, c: c) if by_core else (lambda x, y, c: 4 * x + 2 * y + c)
    block_rows = [a.shape[0] if gather else a.shape[1] for a in arrs]
    assert all(r % pieces == 0 for r in block_rows), (block_rows, pieces)

    def body(*refs):
        in_refs, out_refs = refs[:n], refs[n:2 * n]
        send_sems, recv_sems, local_sems = refs[2 * n:]
        x, y, c = _mesh_pos()
        me = slot(x, y, c)
        copies, locals_ = [], []
        for a in range(n):
            own = in_refs[a] if gather else in_refs[a].at[me]
            loc = pltpu.make_async_copy(own, out_refs[a].at[me], local_sems.at[a])
            loc.start()
            locals_.append(loc)
            for k, (dx, dy, dc) in enumerate(peers):
                peer = (1 - x if dx else x, 1 - y if dy else y, 1 - c if dc else c)
                src = in_refs[a] if gather else in_refs[a].at[slot(*peer)]
                for q in range(pieces):
                    part = pl.ds(q * (block_rows[a] // pieces), block_rows[a] // pieces)
                    sem = (a * n_peers + k) * pieces + q
                    cp = pltpu.make_async_remote_copy(
                        src_ref=src.at[part], dst_ref=out_refs[a].at[me, part], send_sem=send_sems.at[sem],
                        recv_sem=recv_sems.at[sem], device_id=peer, device_id_type=MESH_ID)
                    cp.start()
                    copies.append(cp)
        for cp in copies:
            cp.wait()
        for loc in locals_:
            loc.wait()

    any_spec = pl.BlockSpec(memory_space=pl.ANY)
    out_shape = [jax.ShapeDtypeStruct((n_slots,) + (a.shape if gather else a.shape[1:]), a.dtype) for a in arrs]
    n_sems = n * n_peers * pieces
    res = pl.pallas_call(
        body, in_specs=[any_spec] * n, out_specs=[any_spec] * n, out_shape=out_shape,
        scratch_shapes=[pltpu.SemaphoreType.DMA((n_sems,)), pltpu.SemaphoreType.DMA((n_sems,)),
                        pltpu.SemaphoreType.DMA((n,))],
        name=name)(*arrs)
    return list(res)


def sum_slots(parts, slots, name):
    _, r, c = parts.shape
    tr = r
    for cand in (512, 256, 128, 64, 32, 16, 8):
        if r % cand == 0 and cand * c * 4 * len(slots) <= 8 * 1024 * 1024:
            tr = cand
            break

    def body(p_ref, o_ref):
        acc = p_ref[slots[0]].astype(f32)
        for s in slots[1:]:
            acc = acc + p_ref[s].astype(f32)
        o_ref[...] = acc

    return pl.pallas_call(body, grid=(r // tr,), in_specs=[pl.BlockSpec((parts.shape[0], tr, c), lambda i: (0, i, 0))],
                          out_specs=pl.BlockSpec((tr, c), lambda i: (i, 0)),
                          out_shape=jax.ShapeDtypeStruct((r, c), f32),
                          compiler_params=_cparams(("parallel",)), name=name)(parts)


def column_sum(a, name):
    def body(a_ref, o_ref):
        o_ref[...] = jnp.sum(a_ref[...], axis=0, keepdims=True)

    return pl.pallas_call(body, out_shape=jax.ShapeDtypeStruct((1, a.shape[1]), f32), name=name)(a)


def adamw(w, g, m, v, name):
    r, c = w.shape
    tr = r
    for cand in (256, 128, 64, 32, 16, 8):
        if r % cand == 0:
            tr = cand
            break

    def body(w_ref, g_ref, m_ref, v_ref, d_ref, mo_ref, vo_ref):
        gv = g_ref[...]
        m_new = ADAM_B1 * m_ref[...] + (1.0 - ADAM_B1) * gv
        v_new = ADAM_B2 * v_ref[...] + (1.0 - ADAM_B2) * jnp.square(gv)
        m_hat = m_new / (1.0 - ADAM_B1 ** ADAM_STEP)
        v_hat = v_new / (1.0 - ADAM_B2 ** ADAM_STEP)
        d_ref[...] = -ADAM_LR * (m_hat / (jnp.sqrt(v_hat) + ADAM_EPS) + ADAM_WD * w_ref[...])
        mo_ref[...] = m_new
        vo_ref[...] = v_new

    spec = pl.BlockSpec((tr, c), lambda i: (i, 0))
    return pl.pallas_call(body, grid=(r // tr,), in_specs=[spec] * 4, out_specs=[spec] * 3,
                          out_shape=[jax.ShapeDtypeStruct((r, c), f32)] * 3,
                          compiler_params=_cparams(("parallel",)), name=name)(w, g, m, v)


def adaln_fwd(c_rows, w, b):
    def body(c_ref, w_ref, b_ref, o_ref):
        cv = c_ref[...]
        o_ref[...] = _mxu_dot(cv * jax.nn.sigmoid(cv), w_ref[...]) + b_ref[...]

    return pl.pallas_call(body, out_shape=jax.ShapeDtypeStruct((c_rows.shape[0], w.shape[1]), f32),
                          compiler_params=pltpu.CompilerParams(vmem_limit_bytes=VMEM_LIMIT), name="adaln_fwd")(c_rows, w, b)


def adaln_bwd(c_rows, dm, w):
    def body(c_ref, dm_ref, w_ref, gw_ref, ds_ref):
        cv = c_ref[...]
        gw_ref[...] = _dg(cv * jax.nn.sigmoid(cv), dm_ref[...], 0, 0)
        ds_ref[...] = _dg(dm_ref[...], w_ref[...], 1, 1)

    return pl.pallas_call(body, out_shape=[jax.ShapeDtypeStruct(w.shape, f32),
                                           jax.ShapeDtypeStruct(c_rows.shape, f32)],
                          compiler_params=pltpu.CompilerParams(vmem_limit_bytes=VMEM_LIMIT), name="adaln_bwd")(c_rows, dm, w)


def c_ctx_grad(parts, c_ctx_row):
    def body(p_ref, c_ref, o_ref):
        total = p_ref[CHIP_SLOTS[0], 0:1, :]
        for s in CHIP_SLOTS[1:]:
            total = total + p_ref[s, 0:1, :]
        _, vjp = jax.vjp(jax.nn.silu, c_ref[...])
        o_ref[...] = vjp(total)[0]

    return pl.pallas_call(body, out_shape=jax.ShapeDtypeStruct((1, D_MODEL), f32), name="c_ctx_grad")(parts, c_ctx_row)


PACK_W = 1024
PACK_ROWS = 8


def _pack(arrs):
    pieces, layout, r0 = [], [], 0
    for a in arrs:
        size = math.prod(a.shape)
        rows = -(-size // (PACK_W * PACK_ROWS)) * PACK_ROWS
        pieces.append(jnp.pad(a.reshape(-1).astype(f32), (0, rows * PACK_W - size)).reshape(rows, PACK_W))
        layout.append((r0, rows, a.shape))
        r0 += rows
    return jnp.concatenate(pieces, axis=0), layout


def _unpack(pack, layout, lead=()):
    n_lead = len(lead)
    outs = []
    for r0, rows, shape in layout:
        piece = pack[(slice(None),) * n_lead + (slice(r0, r0 + rows),)].reshape(lead + (-1,))
        outs.append(piece[..., :math.prod(shape)].reshape(lead + tuple(shape)))
    return outs


W_NAMES = ("c_ctx", "w_ada", "b_ada", "norm1_g", "norm2_g", "w_in", "ret_log_decay", "rwkv_shift_mu", "rwkv_w0",
           "rwkv_w_up", "rwkv_a0", "rwkv_a_up", "rwkv_g_up", "rwkv_k_k", "rwkv_k_a", "rwkv_r_k", "rwkv_ln_w",
           "rwkv_ln_b", "w_out", "w_ff1", "b_ff1", "w_ff2", "b_ff2", "final_g")
COL_SHARDED = ("w_in", "w_ff1")
ROW_SHARDED = ("w_out", "w_ff2")
LAST_SHARDED = ("rwkv_shift_mu", "rwkv_w0", "rwkv_w_up", "rwkv_a0", "rwkv_a_up", "rwkv_g_up")
REPLICATED = ("c_ctx", "b_ada", "norm1_g", "norm2_g", "ret_log_decay", "rwkv_k_k", "rwkv_k_a", "rwkv_r_k",
              "rwkv_ln_w", "rwkv_ln_b", "b_ff1", "b_ff2", "final_g")
N_SHARDS = 4


def _train_step(a):
    x, c, ctx, tgt = a["x"], a["c"], a["ctx"], a["loss_target"]
    bsz = x.shape[0]
    mx, my, mc = _mesh_pos()
    shard = 2 * mx + my
    dev = _device_slot()

    (c_all,) = exchange([jnp.pad(c, ((0, PACK_ROWS - bsz), (0, 0)))], True, ALL_PEERS, "gather_c")
    n_ex = N_DEV * bsz
    c_rows = jnp.concatenate([c_all[:, :bsz].reshape(n_ex, D_MODEL), a["c_ctx"][None, :],
                              jnp.zeros((PACK_ROWS - 1, D_MODEL), f32)], axis=0)
    ada_cols = a["w_ada"].shape[-1]
    b_ada_cols = lax.dynamic_slice_in_dim(a["b_ada"], shard * ada_cols, ada_cols, axis=1)
    mod_cols = adaln_fwd(c_rows, a["w_ada"][0], b_ada_cols)

    halves, small_shards = [], [a[n][0] for n in LAST_SHARDED]
    for n in COL_SHARDED + ROW_SHARDED:
        w = a[n][0].astype(MXU_DTYPE)
        half = w.shape[0] // 2
        halves.append(lax.dynamic_slice_in_dim(w, mc * half, half, axis=0))
    small_pack, small_layout = _pack(small_shards)
    gathered = exchange([mod_cols] + halves + [small_pack], True, ALL_PEERS, "gather_weights")
    mod_all = jnp.stack([gathered[0][s] for s in CHIP_SLOTS], axis=1).reshape(c_rows.shape[0], -1)
    mod_x = lax.dynamic_slice_in_dim(mod_all, dev * bsz, bsz, axis=0).reshape(bsz, 6, D_MODEL)
    mod_ctx = mod_all[n_ex].reshape(6, D_MODEL)
    wt = {}
    for n, gth in zip(COL_SHARDED + ROW_SHARDED, gathered[1:5]):
        per_chip = gth.reshape(N_SHARDS, -1, gth.shape[-1])
        wt[n] = (per_chip.transpose(1, 0, 2).reshape(per_chip.shape[1], -1) if n in COL_SHARDED
                 else per_chip.reshape(-1, per_chip.shape[-1]))
    small_by_chip = _unpack(jnp.stack([gathered[5][s] for s in CHIP_SLOTS]), small_layout, (N_SHARDS,))
    for n, parts in zip(LAST_SHARDED, small_by_chip):
        wt[n] = jnp.concatenate([parts[s] for s in range(N_SHARDS)], axis=-1)
    for n in ("norm1_g", "norm2_g", "rwkv_k_k", "rwkv_k_a", "rwkv_r_k", "rwkv_ln_w", "rwkv_ln_b", "b_ff1", "b_ff2"):
        wt[n] = a[n]
    wt["ret_log_decay"] = a["ret_log_decay"][0]
    wt["final_g"] = a["final_g"][None, :]

    loss, grad_x, g = layer_step(x, ctx, tgt, mod_x, mod_ctx, wt)

    small_names = [n for n in REPLICATED if n not in ("c_ctx", "b_ada")] + list(LAST_SHARDED)
    g_pack, g_layout = _pack([jnp.pad(loss, ((0, 0), (0, PACK_W - loss.shape[1])))] + [g[n] for n in small_names]
                             + [g["mod_x"], g["mod_ctx"]])
    (g_packs,) = exchange([g_pack], True, ALL_PEERS, "gather_small_grads")
    g_sum = _unpack(sum_slots(g_packs, tuple(range(N_DEV)), "sum_small_grads"), g_layout)
    loss_total = g_sum[0][0, 0]
    grads = dict(zip(small_names, g_sum[1:1 + len(small_names)]))
    dmod_ctx = g_sum[-1].reshape(1, -1)
    dmod_x = _unpack(g_packs, g_layout, (N_DEV,))[-2].reshape(n_ex, -1)
    dmod = jnp.concatenate([dmod_x, dmod_ctx, jnp.zeros((PACK_ROWS - 1, dmod_x.shape[1]), f32)], axis=0)
    grads["b_ada"] = column_sum(dmod, "b_ada_grad")
    dmod_cols = lax.dynamic_slice_in_dim(dmod, shard * ada_cols, ada_cols, axis=1)
    grads["w_ada"], dsilu = adaln_bwd(c_rows, dmod_cols, a["w_ada"][0])

    blocks = [jnp.pad(dsilu[n_ex:n_ex + 1], ((0, PACK_ROWS - 1), (0, 0)))[None].repeat(N_DEV, axis=0)]
    for n in COL_SHARDED + ROW_SHARDED:
        gw = g[n]
        if n in COL_SHARDED:
            gw = gw.reshape(gw.shape[0], N_SHARDS, -1).transpose(1, 0, 2)
        blocks.append(gw.reshape(N_DEV, -1, gw.shape[-1]).astype(MXU_DTYPE))
    received = exchange(blocks, False, ALL_PEERS, "scatter_big_grads")
    grads["c_ctx"] = c_ctx_grad(received[0], a["c_ctx"][None, :])
    half_sums = [sum_slots(p, tuple(range(N_DEV)), f"sum_{n}") for n, p in zip(COL_SHARDED + ROW_SHARDED, received[1:])]
    both_halves = exchange(half_sums, True, SIBLING, "swap_halves", by_core=True, pieces=4)
    for n, two in zip(COL_SHARDED + ROW_SHARDED, both_halves):
        grads[n] = two.reshape(-1, two.shape[-1])
    for n in LAST_SHARDED:
        width = a[n].shape[-1]
        grads[n] = lax.dynamic_slice_in_dim(grads[n], shard * width, width, axis=grads[n].ndim - 1)

    out_g, out_d, out_m, out_v = {}, {}, {}, {}
    for n in ("w_ada",) + COL_SHARDED + ROW_SHARDED:
        out_g[n] = grads[n].reshape(a[n].shape)
        two_d = lambda z: z.reshape(-1, z.shape[-1])
        d, m, v = adamw(two_d(a[n]), two_d(out_g[n]), two_d(a["m_" + n]), two_d(a["v_" + n]), f"adamw_{n}")
        out_d[n], out_m[n], out_v[n] = d.reshape(a[n].shape), m.reshape(a[n].shape), v.reshape(a[n].shape)
    rest = REPLICATED + LAST_SHARDED
    for n in rest:
        out_g[n] = grads[n].reshape(a[n].shape)
    packs = [_pack([src[n] for n in rest])[0] for src in
             ({n: a[n] for n in rest}, out_g, {n: a["m_" + n] for n in rest}, {n: a["v_" + n] for n in rest})]
    _, rest_layout = _pack([a[n] for n in rest])
    for dst, pack in zip((out_d, out_m, out_v), adamw(*packs, "adamw_small")):
        dst.update(zip(rest, _unpack(pack, rest_layout)))
    return (loss_total, grad_x, *[out_g[n] for n in W_NAMES], *[out_d[n] for n in W_NAMES],
            *[out_m[n] for n in W_NAMES], *[out_v[n] for n in W_NAMES])


def kernel(x, c, ctx, c_ctx, w_ada, b_ada, norm1_g, norm2_g, w_in, ret_log_decay, rwkv_shift_mu, rwkv_w0, rwkv_w_up, rwkv_a0, rwkv_a_up, rwkv_g_up, rwkv_k_k, rwkv_k_a, rwkv_r_k, rwkv_ln_w, rwkv_ln_b, w_out, w_ff1, b_ff1, w_ff2, b_ff2, final_g, loss_target, m_c_ctx, m_w_ada, m_b_ada, m_norm1_g, m_norm2_g, m_w_in, m_ret_log_decay, m_rwkv_shift_mu, m_rwkv_w0, m_rwkv_w_up, m_rwkv_a0, m_rwkv_a_up, m_rwkv_g_up, m_rwkv_k_k, m_rwkv_k_a, m_rwkv_r_k, m_rwkv_ln_w, m_rwkv_ln_b, m_w_out, m_w_ff1, m_b_ff1, m_w_ff2, m_b_ff2, m_final_g, v_c_ctx, v_w_ada, v_b_ada, v_norm1_g, v_norm2_g, v_w_in, v_ret_log_decay, v_rwkv_shift_mu, v_rwkv_w0, v_rwkv_w_up, v_rwkv_a0, v_rwkv_a_up, v_rwkv_g_up, v_rwkv_k_k, v_rwkv_k_a, v_rwkv_r_k, v_rwkv_ln_w, v_rwkv_ln_b, v_w_out, v_w_ff1, v_b_ff1, v_w_ff2, v_b_ff2, v_final_g):
    return _train_step(dict(locals()))
```

```python
import functools
import math

import jax
import jax.numpy as jnp
from jax import lax
from jax.experimental import pallas as pl
from jax.experimental.pallas import tpu as pltpu

f32 = jnp.float32
MXU_DTYPE = jnp.bfloat16

D_MODEL = 1024
RET_W = 512
RET_HEADS = 4
RET_DH = 128
RET_CHUNK = 128
RW_W = 512
RW_N = 64
DECAY_LORA = 64
AAA_LORA = 64
GATE_LORA = 128
LORA_W = DECAY_LORA + AAA_LORA + GATE_LORA
D_FF = 4096
RET_COLS = 4 * RET_W
SHIFT_COLS = 3 * RW_W + LORA_W
IN_COLS = RET_COLS + SHIFT_COLS
GRID_W = 64
ROPE_BASE = 10000.0
NORM_EPS = 1e-6
GN_EPS = 64e-5
W_DECAY_SCALE = math.exp(-0.5)
ADAM_LR, ADAM_B1, ADAM_B2, ADAM_EPS, ADAM_WD, ADAM_STEP = 0.001, 0.9, 0.999, 1e-08, 0.01, 10

TOK_TILE = 256
SCAN_CHUNK = 8
N_DEV = 8
V7X_VMEM_BYTES = 64 * 1024 * 1024
VMEM_LIMIT = V7X_VMEM_BYTES * 7 // 8


def _cparams(sem):
    return pltpu.CompilerParams(dimension_semantics=sem, vmem_limit_bytes=VMEM_LIMIT)


def _tile(n, cap):
    best = None
    for t in range(128, min(n, cap) + 1, 128):
        if n % t == 0:
            best = t
    return best if best is not None else n


def matmul(a, b, mode, name, out_dtype=f32):
    if mode == "nn":
        (m, k), (k2, n) = a.shape, b.shape
    elif mode == "nt":
        (m, k), (n, k2) = a.shape, b.shape
    else:
        (k, m), (k2, n) = a.shape, b.shape
    assert k == k2, (a.shape, b.shape, mode)
    tm, tn, tk = _tile(m, 512), _tile(n, 768), _tile(k, 1024)
    nk = k // tk
    dims = {"nn": ((1,), (0,)), "nt": ((1,), (1,)), "tn": ((0,), (0,))}[mode]

    def body(a_ref, b_ref, o_ref, acc_ref):
        kk = pl.program_id(2)

        @pl.when(kk == 0)
        def _():
            acc_ref[...] = jnp.zeros_like(acc_ref)

        acc_ref[...] += lax.dot_general(a_ref[...].astype(MXU_DTYPE), b_ref[...].astype(MXU_DTYPE),
                                        (dims, ((), ())), preferred_element_type=f32)

        @pl.when(kk == nk - 1)
        def _():
            o_ref[...] = acc_ref[...].astype(o_ref.dtype)

    if mode == "nn":
        a_spec = pl.BlockSpec((tm, tk), lambda i, j, q: (i, q))
        b_spec = pl.BlockSpec((tk, tn), lambda i, j, q: (q, j))
    elif mode == "nt":
        a_spec = pl.BlockSpec((tm, tk), lambda i, j, q: (i, q))
        b_spec = pl.BlockSpec((tn, tk), lambda i, j, q: (j, q))
    else:
        a_spec = pl.BlockSpec((tk, tm), lambda i, j, q: (q, i))
        b_spec = pl.BlockSpec((tk, tn), lambda i, j, q: (q, j))
    return pl.pallas_call(
        body, grid=(m // tm, n // tn, nk), in_specs=[a_spec, b_spec],
        out_specs=pl.BlockSpec((tm, tn), lambda i, j, q: (i, j)),
        out_shape=jax.ShapeDtypeStruct((m, n), out_dtype),
        scratch_shapes=[pltpu.VMEM((tm, tn), f32)],
        compiler_params=_cparams(("parallel", "parallel", "arbitrary")), name=name)(a, b)


class Tiled:
    def __init__(self, arr, w=None, cidx=0, toff=0):
        self.arr, self.w, self.cidx, self.toff = arr, (arr.shape[-1] if w is None else w), cidx, toff

    def spec(self):
        cidx, toff = self.cidx, self.toff
        return pl.BlockSpec((None, TOK_TILE, self.w), lambda b, i: (b, i + toff, cidx))


class Seg:
    def __init__(self, arr, seg, first):
        self.arr, self.seg, self.first = arr, seg, first

    def spec(self):
        seg = self.seg
        return pl.BlockSpec((None, None, 1, self.arr.shape[-1]), lambda b, i: (b, seg(i), 0, 0))


class Glob:
    def __init__(self, arr):
        self.arr = arr

    def spec(self):
        return pl.BlockSpec(self.arr.shape, lambda b, i: (0,) * self.arr.ndim)


def ew_forward(fn, name, bsz, n_tiles, ins, outs):
    n_in = len(ins)

    def body(*refs):
        res = fn(*[r[...] for r in refs[:n_in]])
        for o_ref, o in zip(refs[n_in:], res):
            o_ref[...] = o.astype(o_ref.dtype)

    out_specs = [pl.BlockSpec((None, TOK_TILE, w), lambda b, i: (b, i, 0)) for w, _ in outs]
    out_shape = [jax.ShapeDtypeStruct((bsz, n_tiles * TOK_TILE, w), dt) for w, dt in outs]
    return pl.pallas_call(body, grid=(bsz, n_tiles), in_specs=[d.spec() for d in ins], out_specs=out_specs,
                          out_shape=out_shape, compiler_params=_cparams(("parallel", "parallel")), name=name)(
        *[d.arr for d in ins])


def ew_backward(fn, name, bsz, n_tiles, ins, cts, want, grad_dtypes=None):
    n_in, n_ct = len(ins), len(cts)
    diff = [k for k in range(n_in) if want[k]]
    grad_dtypes = grad_dtypes or {}

    def body(*refs):
        b, i = pl.program_id(0), pl.program_id(1)
        vals = [r[...] for r in refs[:n_in]]
        ct_vals = tuple(r[...].astype(f32) for r in refs[n_in:n_in + n_ct])
        g_refs = refs[n_in + n_ct:]

        def f(*dvals):
            full = list(vals)
            for k, v in zip(diff, dvals):
                full[k] = v
            return tuple(fn(*full))

        _, vjp = jax.vjp(f, *[vals[k] for k in diff])
        grads = vjp(ct_vals)
        for k, g_ref, g in zip(diff, g_refs, grads):
            d = ins[k]
            if isinstance(d, Tiled):
                g_ref[...] = g.astype(g_ref.dtype)
            else:
                zero = d.first(i) if isinstance(d, Seg) else jnp.logical_and(b == 0, i == 0)

                @pl.when(zero)
                def _(g_ref=g_ref):
                    g_ref[...] = jnp.zeros_like(g_ref)

                g_ref[...] += g

    out_specs, out_shape = [], []
    for k in diff:
        d = ins[k]
        if isinstance(d, Tiled):
            out_specs.append(pl.BlockSpec((None, TOK_TILE, d.w), lambda b, i: (b, i, 0)))
            out_shape.append(jax.ShapeDtypeStruct((bsz, n_tiles * TOK_TILE, d.w), grad_dtypes.get(k, f32)))
        else:
            out_specs.append(d.spec())
            out_shape.append(jax.ShapeDtypeStruct(d.arr.shape, f32))
    return pl.pallas_call(body, grid=(bsz, n_tiles), in_specs=[d.spec() for d in ins] + [c.spec() for c in cts],
                          out_specs=out_specs, out_shape=out_shape,
                          compiler_params=_cparams(("arbitrary", "arbitrary")), name=name)(
        *[d.arr for d in ins], *[c.arr for c in cts])


@jax.custom_vjp
def _mxu_dot(a, b):
    return jnp.dot(a.astype(MXU_DTYPE), b.astype(MXU_DTYPE), preferred_element_type=f32)


def _mxu_dot_fwd(a, b):
    return _mxu_dot(a, b), (a, b)


def _mxu_dot_bwd(res, ct):
    a, b = res
    ct = ct.astype(MXU_DTYPE)
    da = lax.dot_general(ct, b.astype(MXU_DTYPE), (((1,), (1,)), ((), ())), preferred_element_type=f32)
    db = lax.dot_general(a.astype(MXU_DTYPE), ct, (((0,), (0,)), ((), ())), preferred_element_type=f32)
    return da, db


_mxu_dot.defvjp(_mxu_dot_fwd, _mxu_dot_bwd)


def _split_dot_impl(x, ones_mat):
    hi = x.astype(MXU_DTYPE)
    lo = (x - hi.astype(f32)).astype(MXU_DTYPE)
    return jnp.dot(hi, ones_mat, preferred_element_type=f32) + jnp.dot(lo, ones_mat, preferred_element_type=f32)


@jax.custom_vjp
def _split_dot(x, ones_mat):
    return _split_dot_impl(x, ones_mat)


def _split_dot_fwd(x, ones_mat):
    return _split_dot_impl(x, ones_mat), ones_mat


def _split_dot_bwd(ones_mat, ct):
    return _split_dot_impl(ct, ones_mat), None


_split_dot.defvjp(_split_dot_fwd, _split_dot_bwd)


def _block_ones(n, group):
    idx = jnp.arange(n) // group
    return (idx[:, None] == idx[None, :]).astype(MXU_DTYPE)


def _rms(x, g):
    return x * lax.rsqrt(jnp.mean(x * x, axis=-1, keepdims=True) + NORM_EPS) * g


def fn_norm_mod(h, shift, scale, g):
    return (_rms(h, g) * (1.0 + scale) + shift,)


def fn_rwkv_prepare(ks, lora, w0_f, w0_b, a0_f, a0_b, w_up_f, w_up_b, a_up_f, a_up_b, g_up, k_k, k_a, ones64):
    kkr = ks * k_k
    kk = kkr * lax.rsqrt(_split_dot(kkr * kkr, ones64) + 1e-12)
    outs = [kk]
    th = jnp.tanh(lora)
    for w0, a0, w_up, a_up in ((w0_f, a0_f, w_up_f, a_up_f), (w0_b, a0_b, w_up_b, a_up_b)):
        w = jnp.exp(-W_DECAY_SCALE * jax.nn.sigmoid(w0 + _mxu_dot(th, w_up)))
        a = jax.nn.sigmoid(a0 + _mxu_dot(lora, a_up))
        kt = ks * (1.0 + (a - 1.0) * k_a)
        outs += [w, a * kk, kt]
    outs.append(_mxu_dot(jax.nn.sigmoid(lora), g_up))
    return tuple(outs)


def fn_merge(o_f, o_b, g_ret, y_f, y_b, r, kt_f, v, g_rw, r_k, ln_w, ln_b, ones64, ones128):
    o = o_f + o_b
    ret = o * lax.rsqrt(_split_dot(o * o, ones128) * (1.0 / RET_DH) + NORM_EPS) * (g_ret * jax.nn.sigmoid(g_ret))
    y = y_f + y_b
    mean = _split_dot(y, ones64) * (1.0 / RW_N)
    yc = y - mean
    var = _split_dot(yc * yc, ones64) * (1.0 / RW_N)
    y_n = yc * lax.rsqrt(var + GN_EPS) * ln_w + ln_b
    bonus = _split_dot(r * kt_f * r_k, ones64) * v
    return ret, (y_n + bonus) * g_rw


def fn_resid_norm_mod(x, mix, gate, shift, scale, g):
    h1 = x + gate * mix
    return h1, _rms(h1, g) * (1.0 + scale) + shift


def fn_relu2(u, b1):
    return (jnp.square(jnp.maximum(u + b1, 0.0)),)


def fn_loss(h1, f, tgt, gate, b2, g):
    y = _rms(h1 + gate * (f + b2), g)
    err = jnp.square(y - tgt)
    return 0.5 * jnp.sum(jnp.mean(err, axis=-1, keepdims=True), axis=0, keepdims=True)


def loss_and_grads(h1, f, tgt, gate, b2, g, bsz, n_tiles):
    def body(h1_ref, f_ref, t_ref, gate_ref, b2_ref, g_ref, loss_ref, dh1_ref, df_ref, dgate_ref, db2_ref, dg_ref):
        b, i = pl.program_id(0), pl.program_id(1)
        tgt_v = t_ref[...]
        loss, vjp = jax.vjp(lambda a, c, e, p, q: fn_loss(a, c, tgt_v, e, p, q),
                            h1_ref[...], f_ref[...], gate_ref[...], b2_ref[...], g_ref[...])
        dh1, df, dgate, db2, dg = vjp(jnp.ones((1, 1), f32))
        dh1_ref[...] = dh1
        df_ref[...] = df.astype(df_ref.dtype)

        @pl.when(i == 0)
        def _():
            dgate_ref[...] = jnp.zeros_like(dgate_ref)

        @pl.when(jnp.logical_and(b == 0, i == 0))
        def _():
            loss_ref[...] = jnp.zeros_like(loss_ref)
            db2_ref[...] = jnp.zeros_like(db2_ref)
            dg_ref[...] = jnp.zeros_like(dg_ref)

        dgate_ref[...] += dgate
        db2_ref[...] += db2
        dg_ref[...] += dg
        loss_ref[...] += jnp.broadcast_to(loss, loss_ref.shape)

    tile = pl.BlockSpec((None, TOK_TILE, D_MODEL), lambda b, i: (b, i, 0))
    row = pl.BlockSpec((1, D_MODEL), lambda b, i: (0, 0))
    seg = pl.BlockSpec((None, None, 1, D_MODEL), lambda b, i: (b, 0, 0, 0))
    t_tok = n_tiles * TOK_TILE
    return pl.pallas_call(
        body, grid=(bsz, n_tiles), in_specs=[tile, tile, tile, seg, row, row],
        out_specs=[pl.BlockSpec((1, 128), lambda b, i: (0, 0)), tile, tile, seg, row, row],
        out_shape=[jax.ShapeDtypeStruct((1, 128), f32), jax.ShapeDtypeStruct((bsz, t_tok, D_MODEL), f32),
                   jax.ShapeDtypeStruct((bsz, t_tok, D_MODEL), MXU_DTYPE),
                   jax.ShapeDtypeStruct((bsz, 1, 1, D_MODEL), f32),
                   jax.ShapeDtypeStruct((1, D_MODEL), f32), jax.ShapeDtypeStruct((1, D_MODEL), f32)],
        compiler_params=_cparams(("arbitrary", "arbitrary")), name="loss_and_grads")(h1, f, tgt, gate, b2, g)


SHIFT_BLOCK = 256
HALO_ROWS = 8


def _shift_specs(n_tok, col0):
    per_tile = TOK_TILE // HALO_ROWS
    last = n_tok // HALO_ROWS - 1
    tile = pl.BlockSpec((None, TOK_TILE, SHIFT_BLOCK), lambda j, b, i: (b, i, col0 + j))
    prev = pl.BlockSpec((None, HALO_ROWS, SHIFT_BLOCK),
                        lambda j, b, i: (b, jnp.maximum(i * per_tile - 1, 0), col0 + j))
    nxt = pl.BlockSpec((None, HALO_ROWS, SHIFT_BLOCK),
                       lambda j, b, i: (b, jnp.minimum((i + 1) * per_tile, last), col0 + j))
    return tile, prev, nxt


def _shifted(p, prev_ref, next_ref, is_first, is_last):
    row = lax.broadcasted_iota(jnp.int32, p.shape, 0)
    prev_row = jnp.where(is_first, 0.0, prev_ref[HALO_ROWS - 1:HALO_ROWS, :].astype(f32))
    next_row = jnp.where(is_last, 0.0, next_ref[0:1, :].astype(f32))
    prev = jnp.where(row == 0, prev_row, pltpu.roll(p, 1, axis=0))
    nxt = jnp.where(row == TOK_TILE - 1, next_row, pltpu.roll(p, TOK_TILE - 1, axis=0))
    return prev, nxt


def token_shift(px, mu, seg_first, seg_last):
    bsz, n_tok, _ = px.shape
    n_tiles = n_tok // TOK_TILE

    def body(p_ref, prev_ref, next_ref, mu_ref, o_ref):
        i = pl.program_id(2)
        p = p_ref[...]
        prev, nxt = _shifted(p, prev_ref, next_ref, seg_first(i), seg_last(i))
        o_ref[...] = p + mu_ref[0:1, :] * (prev - p) + mu_ref[1:2, :] * (nxt - p)

    tile, prev, nxt = _shift_specs(n_tok, RET_COLS // SHIFT_BLOCK)
    return pl.pallas_call(
        body, grid=(SHIFT_COLS // SHIFT_BLOCK, bsz, n_tiles),
        in_specs=[tile, prev, nxt, pl.BlockSpec((2, SHIFT_BLOCK), lambda j, b, i: (0, j))],
        out_specs=pl.BlockSpec((None, TOK_TILE, SHIFT_BLOCK), lambda j, b, i: (b, i, j)),
        out_shape=jax.ShapeDtypeStruct((bsz, n_tok, SHIFT_COLS), f32),
        compiler_params=_cparams(("parallel", "parallel", "parallel")), name="token_shift")(px, px, px, mu)


def token_shift_bwd(dps, px, mu, seg_first, seg_last):
    bsz, n_tok, _ = px.shape
    n_tiles = n_tok // TOK_TILE

    def body(d_ref, dprev_ref, dnext_ref, p_ref, prev_ref, next_ref, mu_ref, dp_ref, dmu_ref):
        b, i = pl.program_id(1), pl.program_id(2)
        first, last = seg_first(i), seg_last(i)
        d, p = d_ref[...], p_ref[...]
        d_prev, d_next = _shifted(d, dprev_ref, dnext_ref, first, last)
        p_prev, p_next = _shifted(p, prev_ref, next_ref, first, last)
        mu0, mu1 = mu_ref[0:1, :], mu_ref[1:2, :]
        dp_ref[...] = (d + mu0 * (d_next - d) + mu1 * (d_prev - d)).astype(dp_ref.dtype)

        @pl.when(jnp.logical_and(b == 0, i == 0))
        def _():
            dmu_ref[...] = jnp.zeros_like(dmu_ref)

        dmu_ref[0:1, :] += jnp.sum(d * (p_prev - p), axis=0, keepdims=True)
        dmu_ref[1:2, :] += jnp.sum(d * (p_next - p), axis=0, keepdims=True)

    dtile, dprev, dnext = _shift_specs(n_tok, 0)
    tile, prev, nxt = _shift_specs(n_tok, RET_COLS // SHIFT_BLOCK)
    mu_spec = pl.BlockSpec((2, SHIFT_BLOCK), lambda j, b, i: (0, j))
    return pl.pallas_call(
        body, grid=(SHIFT_COLS // SHIFT_BLOCK, bsz, n_tiles),
        in_specs=[dtile, dprev, dnext, tile, prev, nxt, mu_spec],
        out_specs=[pl.BlockSpec((None, TOK_TILE, SHIFT_BLOCK), lambda j, b, i: (b, i, j)), mu_spec],
        out_shape=[jax.ShapeDtypeStruct((bsz, n_tok, SHIFT_COLS), MXU_DTYPE),
                   jax.ShapeDtypeStruct((2, SHIFT_COLS), f32)],
        compiler_params=_cparams(("arbitrary", "arbitrary", "arbitrary")), name="token_shift_bwd")(
        dps, dps, dps, px, px, px, mu)


def _dg(a, b, ca, cb):
    return lax.dot_general(a.astype(MXU_DTYPE), b.astype(MXU_DTYPE), (((ca,), (cb,)), ((), ())),
                           preferred_element_type=f32)


@jax.custom_vjp
def _mm_nt(a, b):
    return _dg(a, b, 1, 1)


_mm_nt.defvjp(lambda a, b: (_dg(a, b, 1, 1), (a, b)),
              lambda res, ct: (_dg(ct, res[1], 1, 0), _dg(ct, res[0], 0, 0)))


@jax.custom_vjp
def _mm_tn(a, b):
    return _dg(a, b, 0, 0)


_mm_tn.defvjp(lambda a, b: (_dg(a, b, 0, 0), (a, b)),
              lambda res, ct: (_dg(res[1], ct, 1, 1), _dg(res[0], ct, 1, 0)))


def _ret_chunk(state, q_raw, k_raw, v, cos, sin, perm, ld_row, head, reverse):
    c = RET_CHUNK
    lane = lax.broadcasted_iota(jnp.int32, ld_row.shape, 1)
    lg = -jnp.exp(jnp.sum(jnp.where(lane == head, ld_row, 0.0), axis=-1, keepdims=True))
    rot = lambda t: t * cos + jnp.dot(t, perm, preferred_element_type=f32, precision=lax.Precision.HIGHEST) * sin
    q = rot(q_raw)
    k = rot(k_raw) * (RET_DH ** -0.5)
    ti = lax.broadcasted_iota(jnp.int32, (c, 1), 0).astype(f32)
    tj = lax.broadcasted_iota(jnp.int32, (1, c), 1).astype(f32)
    if not reverse:
        dist, mask, q_exp, k_exp = ti - tj, (ti - tj) >= 0, ti + 1.0, c - 1.0 - ti
    else:
        dist, mask, q_exp, k_exp = tj - ti, (tj - ti) > 0, c - ti, ti
    decay = jnp.where(mask, jnp.exp(lg * jnp.maximum(dist, 0.0)), 0.0)
    scores = _mm_nt(q, k) * decay
    out = _mxu_dot(scores, v) + _mxu_dot(q * jnp.exp(lg * q_exp), state)
    new_state = state * jnp.exp(lg * c) + _mm_tn(k * jnp.exp(lg * k_exp), v)
    return out, new_state


def _ret_specs(order):
    qkv = [pl.BlockSpec((None, RET_CHUNK, RET_W), functools.partial(lambda b, i, col: (b, order(i), col), col=col))
           for col in range(3)]
    tab = pl.BlockSpec((RET_CHUNK, RET_DH), lambda b, i: (order(i), 0))
    const = pl.BlockSpec((RET_DH, RET_DH), lambda b, i: (0, 0))
    ld = pl.BlockSpec((1, RET_DH), lambda b, i: (0, 0))
    return qkv, tab, const, ld


def retention_fwd(px, cos, sin, perm, ld_row, order, reverse, name):
    bsz, n_tok, _ = px.shape
    n_ch = n_tok // RET_CHUNK

    def body(q_ref, k_ref, v_ref, cos_ref, sin_ref, perm_ref, ld_ref, o_ref, sv_ref, st_ref):
        @pl.when(pl.program_id(1) == 0)
        def _():
            st_ref[...] = jnp.zeros_like(st_ref)

        for h in range(RET_HEADS):
            sl = slice(h * RET_DH, (h + 1) * RET_DH)
            s = st_ref[h]
            sv_ref[h] = s
            o, s_new = _ret_chunk(s, q_ref[:, sl], k_ref[:, sl], v_ref[:, sl], cos_ref[...], sin_ref[...],
                                  perm_ref[...], ld_ref[...], h, reverse)
            o_ref[:, sl] = o
            st_ref[h] = s_new

    qkv, tab, const, ld = _ret_specs(order)
    return pl.pallas_call(
        body, grid=(bsz, n_ch), in_specs=[*qkv, tab, tab, const, ld],
        out_specs=[pl.BlockSpec((None, RET_CHUNK, RET_W), lambda b, i: (b, order(i), 0)),
                   pl.BlockSpec((None, None, RET_HEADS, RET_DH, RET_DH), lambda b, i: (b, i, 0, 0, 0))],
        out_shape=[jax.ShapeDtypeStruct((bsz, n_tok, RET_W), f32),
                   jax.ShapeDtypeStruct((bsz, n_ch, RET_HEADS, RET_DH, RET_DH), f32)],
        scratch_shapes=[pltpu.VMEM((RET_HEADS, RET_DH, RET_DH), f32)],
        compiler_params=_cparams(("parallel", "arbitrary")), name=name)(px, px, px, cos, sin, perm, ld_row)


def retention_bwd(do, px, states, cos, sin, perm, ld_row, order, reverse, name):
    bsz, n_tok, _ = px.shape
    n_ch = n_tok // RET_CHUNK
    back = lambda i: order(n_ch - 1 - i)

    def body(do_ref, q_ref, k_ref, v_ref, sv_ref, cos_ref, sin_ref, perm_ref, ld_ref,
             dq_ref, dk_ref, dv_ref, dld_ref, dst_ref):
        b, i = pl.program_id(0), pl.program_id(1)

        @pl.when(i == 0)
        def _():
            dst_ref[...] = jnp.zeros_like(dst_ref)

        @pl.when(jnp.logical_and(b == 0, i == 0))
        def _():
            dld_ref[...] = jnp.zeros_like(dld_ref)

        cos_v, sin_v, perm_v = cos_ref[...], sin_ref[...], perm_ref[...]
        for h in range(RET_HEADS):
            sl = slice(h * RET_DH, (h + 1) * RET_DH)
            f = lambda s, q, k, v, ld, h=h: _ret_chunk(s, q, k, v, cos_v, sin_v, perm_v, ld, h, reverse)
            _, vjp = jax.vjp(f, sv_ref[h], q_ref[:, sl], k_ref[:, sl], v_ref[:, sl], ld_ref[...])
            ds, dq, dk, dv, dld = vjp((do_ref[:, sl], dst_ref[h]))
            dst_ref[h] = ds
            dq_ref[:, sl] = dq
            dk_ref[:, sl] = dk
            dv_ref[:, sl] = dv
            dld_ref[...] += dld

    qkv, tab, const, ld = _ret_specs(back)
    tok = pl.BlockSpec((None, RET_CHUNK, RET_W), lambda b, i: (b, back(i), 0))
    return pl.pallas_call(
        body, grid=(bsz, n_ch),
        in_specs=[tok, *qkv,
                  pl.BlockSpec((None, None, RET_HEADS, RET_DH, RET_DH), lambda b, i: (b, n_ch - 1 - i, 0, 0, 0)),
                  tab, tab, const, ld],
        out_specs=[tok, tok, tok, ld],
        out_shape=[jax.ShapeDtypeStruct((bsz, n_tok, RET_W), f32)] * 3 + [jax.ShapeDtypeStruct((1, RET_DH), f32)],
        scratch_shapes=[pltpu.VMEM((RET_HEADS, RET_DH, RET_DH), f32)],
        compiler_params=_cparams(("arbitrary", "arbitrary")), name=name)(
        do, px, px, px, states, cos, sin, perm, ld_row)


HALF_W = RW_W // 2


def _head_sum(x, ones):
    xm = x.astype(MXU_DTYPE)
    return jnp.concatenate([jnp.dot(xm[:, :HALF_W], ones, preferred_element_type=f32),
                            jnp.dot(xm[:, HALF_W:], ones, preferred_element_type=f32)], axis=1)


def _stack(parts):
    return jnp.concatenate(parts, axis=0)


def _row(ref, b, t):
    return ref[b, pl.ds(t, 1), :]


SCAN_DIRS = ((False, True), (True, False))
RW_HEADS = RW_W // RW_N
HEAD_ROWS_PAD = 16


def _head_rows(row, mask):
    return jnp.broadcast_to(row, mask.shape) * mask


def _outer(per_value, row, mask_pad):
    return lax.dot_general(per_value.astype(MXU_DTYPE), _head_rows(row, mask_pad).astype(MXU_DTYPE),
                           (((0,), (0,)), ((), ())), preferred_element_type=f32)


def _read(states, rows, mask):
    lhs = _stack([_head_rows(r, mask) for r in rows])
    return lax.dot_general(lhs.astype(MXU_DTYPE), _stack(states).astype(MXU_DTYPE), (((1,), (1,)), ((), ())),
                           preferred_element_type=f32)


def _row_from_heads(per_value, state, mask_pad):
    full = jnp.dot(per_value.astype(MXU_DTYPE), state.astype(MXU_DTYPE), preferred_element_type=f32)
    return jnp.sum(full * mask_pad, axis=0, keepdims=True)


def _scan_specs(bsz, order):
    rows = lambda col=0: pl.BlockSpec((bsz, SCAN_CHUNK, RW_W), lambda i: (0, order(i), col))
    per_value = pl.BlockSpec((bsz, SCAN_CHUNK, HEAD_ROWS_PAD, RW_N), lambda i: (0, order(i), 0, 0))
    raw = pl.BlockSpec((SCAN_CHUNK, RW_HEADS * bsz, RW_N * bsz), lambda i: (order(i), 0, 0))
    return rows, per_value, raw


def _removed(sp, kk_t, ones, bsz):
    removed = _head_sum(_stack([sp[b] * kk_t[b] for b in range(bsz)]), ones)
    return [removed[b * RW_N:(b + 1) * RW_N] for b in range(bsz)]


def _advance(sp, rem, w_t, b_t, vk, bsz):
    return [sp[b] * w_t[b] - rem[b] * b_t[b] + vk[b] for b in range(bsz)]


def heads_to_rows(a):
    b, t, _ = a.shape
    return jnp.pad(a.reshape(b, t, RW_HEADS, RW_N), ((0, 0), (0, 0), (0, HEAD_ROWS_PAD - RW_HEADS), (0, 0)))


def read_out_rows(raw, bsz):
    t = raw.shape[0]
    r5 = raw.reshape(t, bsz, RW_HEADS, bsz, RW_N)
    return jnp.stack([r5[:, b, :, b, :] for b in range(bsz)]).reshape(bsz, t, RW_W)


N_ROWS_FWD = 5
N_ROWS_BWD = 5


def _scan_consts(bsz):
    head = (jnp.arange(RW_W)[None, :] // RW_N == jnp.arange(RW_HEADS)[:, None]).astype(f32)
    return head, jnp.pad(head, ((0, HEAD_ROWS_PAD - RW_HEADS), (0, 0))), _block_ones(HALF_W, RW_N)


def _const_specs(consts):
    return [pl.BlockSpec(c.shape, lambda i: (0, 0)) for c in consts]


def rwkv_scan_fwd(rows_in, v_heads, orders, name):
    bsz, n_tok, _ = rows_in[0][0][0].shape
    n_ch = n_tok // SCAN_CHUNK
    rng = range(bsz)
    consts = _scan_consts(bsz)

    def body(*refs):
        rows = [refs[:N_ROWS_FWD], refs[N_ROWS_FWD:2 * N_ROWS_FWD]]
        v0, v1, head_ref, pad_ref, ones_ref, y0, y1, cs0, cs1, s0, s1, late_ref = refs[2 * N_ROWS_FWD:]
        v_refs, y_refs, cs_refs, s_refs = (v0, v1), (y0, y1), (cs0, cs1), (s0, s1)
        head_v, pad_v, ones_v = head_ref[...], pad_ref[...], ones_ref[...]
        for d in range(2):
            @pl.when(pl.program_id(0) == 0)
            def _(d=d):
                s_refs[d][...] = jnp.zeros_like(s_refs[d])

            cs_refs[d][...] = s_refs[d][...]

        def step(j, carry):
            ts = [SCAN_CHUNK - 1 - j if reverse else j for reverse, _ in SCAN_DIRS]
            sps = [[s_refs[d][b] for b in rng] for d in range(2)]
            rems = [_removed(sps[d], [_row(rows[d][1], b, ts[d]) for b in rng], ones_v, bsz) for d in range(2)]
            vks = [[_outer(v_refs[d][b, ts[d]], _row(rows[d][4], b, ts[d]), pad_v) for b in rng] for d in range(2)]
            for d, (reverse, inclusive) in enumerate(SCAN_DIRS):
                r_ref = rows[d][0]
                if inclusive:
                    before = jnp.maximum(j - 1, 0)
                    late_ref[j] = _read(sps[d], [_row(r_ref, b, before) for b in rng], head_v)
                else:
                    y_refs[d][ts[d]] = _read(sps[d], [_row(r_ref, b, ts[d]) for b in rng], head_v)
            for d in range(2):
                new = _advance(sps[d], rems[d], [_row(rows[d][2], b, ts[d]) for b in rng],
                               [_row(rows[d][3], b, ts[d]) for b in rng], vks[d], bsz)
                for b in rng:
                    s_refs[d][b] = new[b]
            return carry

        lax.fori_loop(0, SCAN_CHUNK, step, 0)
        for d, (reverse, inclusive) in enumerate(SCAN_DIRS):
            if inclusive:
                assert not reverse
                last = SCAN_CHUNK - 1
                late_ref[SCAN_CHUNK] = _read([s_refs[d][b] for b in rng], [rows[d][0][b, last:last + 1, :] for b in rng],
                                             head_v)
                for t in range(SCAN_CHUNK):
                    y_refs[d][t] = late_ref[t + 1]

    specs = [_scan_specs(bsz, orders[d]) for d in range(2)]
    state = pltpu.VMEM((bsz, RW_N, RW_W), f32)
    late = pltpu.VMEM((SCAN_CHUNK + 1, RW_HEADS * bsz, RW_N * bsz), f32)
    start_spec = pl.BlockSpec((None, bsz, RW_N, RW_W), lambda i: (i, 0, 0, 0))
    return pl.pallas_call(
        body, grid=(n_ch,),
        in_specs=[specs[d][0](col) for d in range(2) for _, col in rows_in[d]] + [specs[0][1], specs[1][1]]
        + _const_specs(consts),
        out_specs=[specs[0][2], specs[1][2], start_spec, start_spec],
        out_shape=[jax.ShapeDtypeStruct((n_tok, RW_HEADS * bsz, RW_N * bsz), f32)] * 2
        + [jax.ShapeDtypeStruct((n_ch, bsz, RW_N, RW_W), f32)] * 2,
        scratch_shapes=[state, state, late],
        compiler_params=_cparams(("arbitrary",)), name=name)(
        *[a for d in range(2) for a, _ in rows_in[d]], v_heads, v_heads, *consts)


def rwkv_scan_bwd(rows_in, v_heads, dy_heads, starts, orders, name):
    bsz, n_tok, _ = rows_in[0][0][0].shape
    n_ch = n_tok // SCAN_CHUNK
    backs = [functools.partial(lambda i, order: order(n_ch - 1 - i), order=orders[d]) for d in range(2)]
    rng = range(bsz)
    consts = _scan_consts(bsz)
    n_out, n_scr = 6, 7

    def body(*refs):
        rows = [refs[:N_ROWS_BWD], refs[N_ROWS_BWD:2 * N_ROWS_BWD]]
        rest = refs[2 * N_ROWS_BWD:]
        v_refs, dy_refs, cs_refs, (head_ref, pad_ref, ones_ref) = rest[0:2], rest[2:4], rest[4:6], rest[6:9]
        outs = [rest[9:9 + n_out], rest[9 + n_out:9 + 2 * n_out]]
        scr = [rest[9 + 2 * n_out:9 + 2 * n_out + n_scr], rest[9 + 2 * n_out + n_scr:]]
        head_v, pad_v, ones_v = head_ref[...], pad_ref[...], ones_ref[...]
        for d in range(2):
            s_ref, ds_ref = scr[d][:2]

            @pl.when(pl.program_id(0) == 0)
            def _(ds_ref=ds_ref):
                ds_ref[...] = jnp.zeros_like(ds_ref)

            s_ref[...] = cs_refs[d][...]

        def fstep(j, carry):
            ts = [SCAN_CHUNK - 1 - j if reverse else j for reverse, _ in SCAN_DIRS]
            sps = [[scr[d][0][b] for b in rng] for d in range(2)]
            rems = [_removed(sps[d], [_row(rows[d][1], b, ts[d]) for b in rng], ones_v, bsz) for d in range(2)]
            vks = [[_outer(v_refs[d][b, ts[d]], _row(rows[d][4], b, ts[d]), pad_v) for b in rng] for d in range(2)]
            for d in range(2):
                s_ref, _, hist_ref, rem_ref, _, _, _ = scr[d]
                new = _advance(sps[d], rems[d], [_row(rows[d][2], b, ts[d]) for b in rng],
                               [_row(rows[d][3], b, ts[d]) for b in rng], vks[d], bsz)
                for b in rng:
                    hist_ref[ts[d], b] = sps[d][b]
                    rem_ref[ts[d], b] = rems[d][b]
                    s_ref[b] = new[b]
            return carry

        lax.fori_loop(0, SCAN_CHUNK, fstep, 0)

        def step_of(j, reverse):
            return j if reverse else SCAN_CHUNK - 1 - j

        for d, (reverse, _) in enumerate(SCAN_DIRS):
            t0 = step_of(0, reverse)
            for b in rng:
                scr[d][6][b] = _outer(dy_refs[d][b, t0], rows[d][0][b, t0:t0 + 1, :], pad_v)

        def bstep(j, carry):
            ts = [step_of(j, reverse) for reverse, _ in SCAN_DIRS]
            reads = [[scr[d][6][b] for b in rng] for d in range(2)]
            dss = []
            for d, (_, inclusive) in enumerate(SCAN_DIRS):
                ds = [scr[d][1][b] for b in rng]
                dss.append([ds[b] + reads[d][b] for b in rng] if inclusive else ds)
            drems = [_removed(dss[d], [-_row(rows[d][3], b, ts[d]) for b in rng], ones_v, bsz) for d in range(2)]
            for d, (reverse, _) in enumerate(SCAN_DIRS):
                t_next = step_of(jnp.minimum(j + 1, SCAN_CHUNK - 1), reverse)
                for b in rng:
                    scr[d][6][b] = _outer(dy_refs[d][b, t_next], _row(rows[d][0], b, t_next), pad_v)
                outs[d][5][ts[d]] = _read(dss[d], [_row(rows[d][4], b, ts[d]) for b in rng], head_v)
            for d, (_, inclusive) in enumerate(SCAN_DIRS):
                _, kk_ref, w_ref, _, _ = rows[d]
                _, ds_ref, _, _, dsh_ref, drem_ref, _ = scr[d]
                for b in rng:
                    dsh_ref[ts[d], b] = dss[d][b]
                    drem_ref[ts[d], b] = drems[d][b]
                    dsp = dss[d][b] * _row(w_ref, b, ts[d]) + drems[d][b] * _row(kk_ref, b, ts[d])
                    ds_ref[b] = dsp if inclusive else dsp + reads[d][b]
            return carry

        lax.fori_loop(0, SCAN_CHUNK, bstep, 0)

        rsum = lambda z: jnp.sum(z, axis=0, keepdims=True)
        for d, (reverse, inclusive) in enumerate(SCAN_DIRS):
            dr_ref, dkk_ref, dw_ref, db_ref, dkt_ref, _ = outs[d]
            s_ref, _, hist_ref, rem_ref, dsh_ref, drem_ref, _ = scr[d]
            for t in range(SCAN_CHUNK):
                ts = slice(t, t + 1)
                after = t - 1 if reverse else t + 1
                for b in rng:
                    sp, ds = hist_ref[t, b], dsh_ref[t, b]
                    if not inclusive:
                        seen = sp
                    else:
                        seen = hist_ref[after, b] if 0 <= after < SCAN_CHUNK else s_ref[b]
                    dr_ref[b, ts, :] = _row_from_heads(dy_refs[d][b, t], seen, pad_v)
                    dkt_ref[b, ts, :] = _row_from_heads(v_refs[d][b, t], ds, pad_v)
                    dw_ref[b, ts, :] = rsum(ds * sp)
                    db_ref[b, ts, :] = -rsum(ds * rem_ref[t, b])
                    dkk_ref[b, ts, :] = rsum(sp * drem_ref[t, b])

    specs = [_scan_specs(bsz, backs[d]) for d in range(2)]
    hist = pltpu.VMEM((SCAN_CHUNK, bsz, RW_N, RW_W), f32)
    state = pltpu.VMEM((bsz, RW_N, RW_W), f32)
    start_spec = pl.BlockSpec((None, bsz, RW_N, RW_W), lambda i: (n_ch - 1 - i, 0, 0, 0))
    row_shape = jax.ShapeDtypeStruct((bsz, n_tok, RW_W), f32)
    raw_shape = jax.ShapeDtypeStruct((n_tok, RW_HEADS * bsz, RW_N * bsz), f32)
    return pl.pallas_call(
        body, grid=(n_ch,),
        in_specs=[specs[d][0](col) for d in range(2) for _, col in rows_in[d]]
        + [specs[0][1], specs[1][1]] * 2 + [start_spec, start_spec] + _const_specs(consts),
        out_specs=[spec for d in range(2) for spec in [specs[d][0]()] * 5 + [specs[d][2]]],
        out_shape=([row_shape] * 5 + [raw_shape]) * 2,
        scratch_shapes=[state, state, hist, hist, hist, hist, state] * 2,
        compiler_params=_cparams(("arbitrary",)), name=name)(
        *[a for d in range(2) for a, _ in rows_in[d]], v_heads, v_heads, dy_heads, dy_heads, *starts, *consts)


MOD_NAMES = ("shift1", "scale1", "gate1", "shift2", "scale2", "gate2")


def _rope_tables(t_ctx, t_x):
    quarter = RET_DH // 4
    pos = jnp.arange(t_x)
    inv = jnp.power(ROPE_BASE, -jnp.arange(0, 2 * quarter, 2, dtype=f32) / (2 * quarter))
    ang_r = (pos // GRID_W).astype(f32)[:, None] * inv[None, :]
    ang_c = (pos % GRID_W).astype(f32)[:, None] * inv[None, :]
    cos = jnp.concatenate([jnp.cos(ang_r)] * 2 + [jnp.cos(ang_c)] * 2, axis=1)
    sin = jnp.concatenate([-jnp.sin(ang_r), jnp.sin(ang_r), -jnp.sin(ang_c), jnp.sin(ang_c)], axis=1)
    cos = jnp.concatenate([jnp.ones((t_ctx, RET_DH), f32), cos], axis=0)
    sin = jnp.concatenate([jnp.zeros((t_ctx, RET_DH), f32), sin], axis=0)
    lane = jnp.arange(RET_DH)
    partner = jnp.where(lane % (2 * quarter) < quarter, lane + quarter, lane - quarter)
    perm = (lane[:, None] == partner[None, :]).astype(f32)
    return cos, sin, perm


def _pad_rows(w, lo, total):
    return jnp.pad(w, ((lo, total - lo - w.shape[0]), (0, 0)))


def layer_step(x, ctx, tgt, mod_x, mod_ctx, wt):
    bsz, t_x, _ = x.shape
    t_c = ctx.shape[1]
    t_all = t_c + t_x
    n_ct, n_xt = t_c // TOK_TILE, t_x // TOK_TILE
    n_t = n_ct + n_xt
    assert t_c % TOK_TILE == 0 and t_x % TOK_TILE == 0 and t_c % RET_CHUNK == 0

    seg = lambda i: (i >= n_ct).astype(jnp.int32)
    seg_first = lambda i: jnp.logical_or(i == 0, i == n_ct)
    seg_last = lambda i: jnp.logical_or(i == n_ct - 1, i == n_t - 1)
    mod_all = {n: jnp.stack([jnp.broadcast_to(mod_ctx[k], (bsz, D_MODEL)), mod_x[:, k]], axis=1)[:, :, None, :]
               for k, n in enumerate(MOD_NAMES)}
    mod_lat = {n: mod_x[:, k][:, None, None, :] for k, n in enumerate(MOD_NAMES)}
    both = lambda n: Seg(mod_all[n], seg, seg_first)
    lat = lambda n: Seg(mod_lat[n], lambda i: 0, lambda i: i == 0)
    flat = lambda a: a.reshape(-1, a.shape[-1])
    padc = lambda a: jnp.pad(a, ((0, 0), (t_c, 0), (0, 0)))

    def chunk_orders(n_ctx_chunks, n_chunks):
        fwd = lambda i: i
        bwd = lambda i: jnp.where(i < n_ctx_chunks, n_ctx_chunks - 1 - i, n_chunks + n_ctx_chunks - 1 - i)
        return fwd, bwd

    ones64, ones128 = _block_ones(RW_W, RW_N), _block_ones(RET_W, RET_DH)
    cos, sin, perm = _rope_tables(t_c, t_x)
    ld_rows = [jnp.pad(wt["ret_log_decay"][d][None, :], ((0, 0), (0, RET_DH - RET_HEADS))) for d in range(2)]
    w_up_pad = [_pad_rows(wt["rwkv_w_up"][d], 0, LORA_W) for d in range(2)]
    a_up_pad = [_pad_rows(wt["rwkv_a_up"][d], DECAY_LORA, LORA_W) for d in range(2)]
    g_up_pad = _pad_rows(wt["rwkv_g_up"], DECAY_LORA + AAA_LORA, LORA_W)
    row = lambda a, d: a[d][None, :]

    h = jnp.concatenate([ctx, x], axis=1)
    norm1_ins = lambda: [Tiled(h), both("shift1"), both("scale1"), Glob(wt["norm1_g"])]
    (n1,) = ew_forward(fn_norm_mod, "norm1", bsz, n_t, norm1_ins(), [(D_MODEL, MXU_DTYPE)])
    px = matmul(flat(n1), wt["w_in"], "nn", "proj_in").reshape(bsz, t_all, IN_COLS)
    ps = token_shift(px, wt["rwkv_shift_mu"], seg_first, seg_last)

    def prep_ins(toff=0):
        return [Tiled(ps, RW_W, 1), Tiled(ps, LORA_W, 3 * RW_W // LORA_W),
                Glob(row(wt["rwkv_w0"], 0)), Glob(row(wt["rwkv_w0"], 1)),
                Glob(row(wt["rwkv_a0"], 0)), Glob(row(wt["rwkv_a0"], 1)),
                Glob(w_up_pad[0]), Glob(w_up_pad[1]), Glob(a_up_pad[0]), Glob(a_up_pad[1]), Glob(g_up_pad),
                Glob(wt["rwkv_k_k"]), Glob(wt["rwkv_k_a"]), Glob(ones64)]

    kk, w_f, b_f, kt_f, w_b, b_b, kt_b, g_rw = ew_forward(fn_rwkv_prepare, "rwkv_prepare", bsz, n_t, prep_ins(),
                                                           [(RW_W, f32)] * 8)
    rw_order = chunk_orders(t_c // SCAN_CHUNK, t_all // SCAN_CHUNK)
    ret_order = chunk_orders(t_c // RET_CHUNK, t_all // RET_CHUNK)
    scan_rows = [[(ps, 0), (kk, 0), (w_f, 0), (b_f, 0), (kt_f, 0)], [(ps, 0), (kk, 0), (w_b, 0), (b_b, 0), (kt_b, 0)]]
    v_heads = heads_to_rows(ps[..., 2 * RW_W:3 * RW_W])
    y_raw_f, y_raw_b, start_f, start_b = rwkv_scan_fwd(scan_rows, v_heads, rw_order, "rwkv_scan_fwd")
    y = [read_out_rows(y_raw_f, bsz), read_out_rows(y_raw_b, bsz)]
    o, ret_states = [], []
    for d in range(2):
        o_d, st_d = retention_fwd(px, cos, sin, perm, ld_rows[d], ret_order[d], SCAN_DIRS[d][0], f"retention_fwd{d}")
        o.append(o_d), ret_states.append(st_d)

    def merge_ins():
        return [Tiled(o[0], toff=n_ct), Tiled(o[1], toff=n_ct), Tiled(px, RET_W, 3, n_ct),
                Tiled(y[0], toff=n_ct), Tiled(y[1], toff=n_ct), Tiled(ps, RW_W, 0, n_ct), Tiled(kt_f, toff=n_ct),
                Tiled(ps, RW_W, 2, n_ct), Tiled(g_rw, toff=n_ct),
                Glob(wt["rwkv_r_k"]), Glob(wt["rwkv_ln_w"]), Glob(wt["rwkv_ln_b"]), Glob(ones64), Glob(ones128)]

    ret_out, rw_out = ew_forward(fn_merge, "merge_heads", bsz, n_xt, merge_ins(), [(RET_W, MXU_DTYPE), (RW_W, MXU_DTYPE)])
    merged = jnp.concatenate([ret_out, rw_out], axis=-1)
    mix = matmul(flat(merged), wt["w_out"], "nn", "proj_out").reshape(bsz, t_x, D_MODEL)
    resid_ins = lambda: [Tiled(x), Tiled(mix), lat("gate1"), lat("shift2"), lat("scale2"), Glob(wt["norm2_g"])]
    h1, n2 = ew_forward(fn_resid_norm_mod, "resid_norm2", bsz, n_xt, resid_ins(), [(D_MODEL, f32), (D_MODEL, MXU_DTYPE)])
    u = matmul(flat(n2), wt["w_ff1"], "nn", "ff1").reshape(bsz, t_x, D_FF)
    relu_ins = lambda: [Tiled(u), Glob(wt["b_ff1"])]
    (act,) = ew_forward(fn_relu2, "relu2", bsz, n_xt, relu_ins(), [(D_FF, MXU_DTYPE)])
    ff = matmul(flat(act), wt["w_ff2"], "nn", "ff2").reshape(bsz, t_x, D_MODEL)

    g = {}
    loss, dh1, dff, dgate2, g["b_ff2"], g["final_g"] = loss_and_grads(
        h1, ff, tgt, mod_lat["gate2"], wt["b_ff2"], wt["final_g"], bsz, n_xt)
    dact = matmul(flat(dff), wt["w_ff2"], "nt", "ff2_dx").reshape(bsz, t_x, D_FF)
    g["w_ff2"] = matmul(flat(act), flat(dff), "tn", "ff2_dw")
    du, g["b_ff1"] = ew_backward(fn_relu2, "relu2_bwd", bsz, n_xt, relu_ins(), [Tiled(dact)], [True, True],
                                 {0: MXU_DTYPE})
    dn2 = matmul(flat(du), wt["w_ff1"], "nt", "ff1_dx").reshape(bsz, t_x, D_MODEL)
    g["w_ff1"] = matmul(flat(n2), flat(du), "tn", "ff1_dw")
    dx_res, dmix, dgate1, dshift2, dscale2, g["norm2_g"] = ew_backward(
        fn_resid_norm_mod, "resid_norm2_bwd", bsz, n_xt, resid_ins(), [Tiled(dh1), Tiled(dn2)], [True] * 6,
        {1: MXU_DTYPE})
    dmerged = matmul(flat(dmix), wt["w_out"], "nt", "proj_out_dx").reshape(bsz, t_x, D_MODEL)
    g["w_out"] = matmul(flat(merged), flat(dmix), "tn", "proj_out_dw")
    (do, dg_ret, dy, dr_m, dkt_m, dv_m, dg_rw, g["rwkv_r_k"], g["rwkv_ln_w"], g["rwkv_ln_b"]) = ew_backward(
        fn_merge, "merge_heads_bwd", bsz, n_xt, merge_ins(), [Tiled(dmerged, RET_W, 0), Tiled(dmerged, RW_W, 1)],
        [True, False, True, True, False, True, True, True, True, True, True, True, False, False])
    do, dy = padc(do), padc(dy)

    dqkv, dld = [], []
    for d in range(2):
        *dqkv_d, dld_d = retention_bwd(do, px, ret_states[d], cos, sin, perm, ld_rows[d], ret_order[d],
                                       SCAN_DIRS[d][0], f"retention_bwd{d}")
        dqkv.append(dqkv_d), dld.append(dld_d[0, :RET_HEADS])
    g["ret_log_decay"] = jnp.stack(dld)
    (dr_f, dkk_f, dw_f, db_f, dkt_f, dv_raw_f, dr_b, dkk_b, dw_b, db_b, dkt_b, dv_raw_b) = rwkv_scan_bwd(
        scan_rows, v_heads, heads_to_rows(dy), (start_f, start_b), rw_order, "rwkv_scan_bwd")
    dv_f, dv_b = read_out_rows(dv_raw_f, bsz), read_out_rows(dv_raw_b, bsz)
    prep_cts = [dkk_f + dkk_b, dw_f, db_f, dkt_f + padc(dkt_m), dw_b, db_b, dkt_b, padc(dg_rw)]
    (dks, dlora, dw0_f, dw0_b, da0_f, da0_b, dwup_f, dwup_b, daup_f, daup_b, dgup, g["rwkv_k_k"],
     g["rwkv_k_a"]) = ew_backward(fn_rwkv_prepare, "rwkv_prepare_bwd", bsz, n_t, prep_ins(),
                                  [Tiled(c) for c in prep_cts], [True] * 13 + [False])
    g["rwkv_w0"] = jnp.concatenate([dw0_f, dw0_b], axis=0)
    g["rwkv_a0"] = jnp.concatenate([da0_f, da0_b], axis=0)
    g["rwkv_w_up"] = jnp.stack([dwup_f[:DECAY_LORA], dwup_b[:DECAY_LORA]])
    g["rwkv_a_up"] = jnp.stack([daup_f[DECAY_LORA:DECAY_LORA + AAA_LORA], daup_b[DECAY_LORA:DECAY_LORA + AAA_LORA]])
    g["rwkv_g_up"] = dgup[DECAY_LORA + AAA_LORA:]
    dps = jnp.concatenate([dr_f + dr_b + padc(dr_m), dks, dv_f + dv_b + padc(dv_m), dlora], axis=-1)
    dp_rw, g["rwkv_shift_mu"] = token_shift_bwd(dps, px, wt["rwkv_shift_mu"], seg_first, seg_last)
    dpx = jnp.concatenate([(dqkv[0][k] + dqkv[1][k]).astype(MXU_DTYPE) for k in range(3)]
                          + [padc(dg_ret).astype(MXU_DTYPE), dp_rw], axis=-1)
    dn1 = matmul(flat(dpx), wt["w_in"], "nt", "proj_in_dx").reshape(bsz, t_all, D_MODEL)
    g["w_in"] = matmul(flat(n1), flat(dpx), "tn", "proj_in_dw")
    dh, dshift1, dscale1, g["norm1_g"] = ew_backward(fn_norm_mod, "norm1_bwd", bsz, n_t, norm1_ins(), [Tiled(dn1)],
                                                     [True] * 4)
    grad_x = dh[:, t_c:] + dx_res
    zeros = jnp.zeros((D_MODEL,), f32)
    g["mod_x"] = jnp.stack([dshift1[:, 1, 0], dscale1[:, 1, 0], dgate1[:, 0, 0], dshift2[:, 0, 0], dscale2[:, 0, 0],
                            dgate2[:, 0, 0]], axis=1)
    g["mod_ctx"] = jnp.stack([dshift1[:, 0, 0].sum(0), dscale1[:, 0, 0].sum(0), zeros, zeros, zeros, zeros])
    return loss, grad_x, g


MESH_ID = pl.DeviceIdType.MESH
ALL_PEERS = [(dx, dy, dc) for dx in (0, 1) for dy in (0, 1) for dc in (0, 1)][1:]
SIBLING = [(0, 0, 1)]
CHIP_SLOTS = (0, 2, 4, 6)


def _mesh_pos():
    return lax.axis_index("x"), lax.axis_index("y"), lax.axis_index("c")


def _device_slot():
    x, y, c = _mesh_pos()
    return 4 * x + 2 * y + c


def sibling_swap(arrs, name, pieces=1):
    n = len(arrs)
    assert all(a.shape[0] % pieces == 0 for a in arrs)

    def body(*refs):
        in_refs, out_refs = refs[:n], refs[n:2 * n]
        send_sems, recv_sems = refs[2 * n:]
        x, y, c = _mesh_pos()
        copies = []
        for a in range(n):
            rows = arrs[a].shape[0] // pieces
            for q in range(pieces):
                part = pl.ds(q * rows, rows)
                cp = pltpu.make_async_remote_copy(
                    src_ref=in_refs[a].at[part], dst_ref=out_refs[a].at[part], send_sem=send_sems.at[a * pieces + q],
                    recv_sem=recv_sems.at[a * pieces + q], device_id=(x, y, 1 - c), device_id_type=MESH_ID)
                cp.start()
                copies.append(cp)
        for cp in copies:
            cp.wait()

    any_spec = pl.BlockSpec(memory_space=pl.ANY)
    res = pl.pallas_call(
        body, in_specs=[any_spec] * n, out_specs=[any_spec] * n,
        out_shape=[jax.ShapeDtypeStruct(a.shape, a.dtype) for a in arrs],
        scratch_shapes=[pltpu.SemaphoreType.DMA((n * pieces,)), pltpu.SemaphoreType.DMA((n * pieces,))],
        name=name)(*arrs)
    return list(res)


def exchange(arrs, gather, peers, name, pieces=1):
    n, n_peers = len(arrs), len(peers)
    n_slots = N_DEV
    slot = lambda x, y, c: 4 * x + 2 * y + c
    block_rows = [a.shape[0] if gather else a.shape[1] for a in arrs]
    assert all(r % pieces == 0 for r in block_rows), (block_rows, pieces)

    def body(*refs):
        in_refs, out_refs = refs[:n], refs[n:2 * n]
        send_sems, recv_sems, local_sems = refs[2 * n:]
        x, y, c = _mesh_pos()
        me = slot(x, y, c)
        copies, locals_ = [], []
        for a in range(n):
            own = in_refs[a] if gather else in_refs[a].at[me]
            loc = pltpu.make_async_copy(own, out_refs[a].at[me], local_sems.at[a])
            loc.start()
            locals_.append(loc)
            for k, (dx, dy, dc) in enumerate(peers):
                peer = (1 - x if dx else x, 1 - y if dy else y, 1 - c if dc else c)
                src = in_refs[a] if gather else in_refs[a].at[slot(*peer)]
                for q in range(pieces):
                    part = pl.ds(q * (block_rows[a] // pieces), block_rows[a] // pieces)
                    sem = (a * n_peers + k) * pieces + q
                    cp = pltpu.make_async_remote_copy(
                        src_ref=src.at[part], dst_ref=out_refs[a].at[me, part], send_sem=send_sems.at[sem],
                        recv_sem=recv_sems.at[sem], device_id=peer, device_id_type=MESH_ID)
                    cp.start()
                    copies.append(cp)
        for cp in copies:
            cp.wait()
        for loc in locals_:
            loc.wait()

    any_spec = pl.BlockSpec(memory_space=pl.ANY)
    out_shape = [jax.ShapeDtypeStruct((n_slots,) + (a.shape if gather else a.shape[1:]), a.dtype) for a in arrs]
    n_sems = n * n_peers * pieces
    res = pl.pallas_call(
        body, in_specs=[any_spec] * n, out_specs=[any_spec] * n, out_shape=out_shape,
        scratch_shapes=[pltpu.SemaphoreType.DMA((n_sems,)), pltpu.SemaphoreType.DMA((n_sems,)),
                        pltpu.SemaphoreType.DMA((n,))],
        name=name)(*arrs)
    return list(res)


def sum_slots(parts, slots, name):
    _, r, c = parts.shape
    tr = r
    for cand in (512, 256, 128, 64, 32, 16, 8):
        if r % cand == 0 and cand * c * 4 * len(slots) <= 8 * 1024 * 1024:
            tr = cand
            break

    def body(p_ref, o_ref):
        acc = p_ref[slots[0]].astype(f32)
        for s in slots[1:]:
            acc = acc + p_ref[s].astype(f32)
        o_ref[...] = acc

    return pl.pallas_call(body, grid=(r // tr,), in_specs=[pl.BlockSpec((parts.shape[0], tr, c), lambda i: (0, i, 0))],
                          out_specs=pl.BlockSpec((tr, c), lambda i: (i, 0)),
                          out_shape=jax.ShapeDtypeStruct((r, c), f32),
                          compiler_params=_cparams(("parallel",)), name=name)(parts)


def column_sum(a, name):
    def body(a_ref, o_ref):
        o_ref[...] = jnp.sum(a_ref[...], axis=0, keepdims=True)

    return pl.pallas_call(body, out_shape=jax.ShapeDtypeStruct((1, a.shape[1]), f32), name=name)(a)


def adamw(w, g, m, v, name):
    r, c = w.shape
    tr = r
    for cand in (256, 128, 64, 32, 16, 8):
        if r % cand == 0:
            tr = cand
            break

    def body(w_ref, g_ref, m_ref, v_ref, d_ref, mo_ref, vo_ref):
        gv = g_ref[...]
        m_new = ADAM_B1 * m_ref[...] + (1.0 - ADAM_B1) * gv
        v_new = ADAM_B2 * v_ref[...] + (1.0 - ADAM_B2) * jnp.square(gv)
        m_hat = m_new / (1.0 - ADAM_B1 ** ADAM_STEP)
        v_hat = v_new / (1.0 - ADAM_B2 ** ADAM_STEP)
        d_ref[...] = -ADAM_LR * (m_hat / (jnp.sqrt(v_hat) + ADAM_EPS) + ADAM_WD * w_ref[...])
        mo_ref[...] = m_new
        vo_ref[...] = v_new

    spec = pl.BlockSpec((tr, c), lambda i: (i, 0))
    return pl.pallas_call(body, grid=(r // tr,), in_specs=[spec] * 4, out_specs=[spec] * 3,
                          out_shape=[jax.ShapeDtypeStruct((r, c), f32)] * 3,
                          compiler_params=_cparams(("parallel",)), name=name)(w, g, m, v)


def adaln_fwd(c_rows, w, b):
    def body(c_ref, w_ref, b_ref, o_ref):
        cv = c_ref[...]
        o_ref[...] = _mxu_dot(cv * jax.nn.sigmoid(cv), w_ref[...]) + b_ref[...]

    return pl.pallas_call(body, out_shape=jax.ShapeDtypeStruct((c_rows.shape[0], w.shape[1]), f32),
                          compiler_params=pltpu.CompilerParams(vmem_limit_bytes=VMEM_LIMIT), name="adaln_fwd")(c_rows, w, b)


def adaln_bwd(c_rows, dm, w):
    def body(c_ref, dm_ref, w_ref, gw_ref, ds_ref):
        cv = c_ref[...]
        gw_ref[...] = _dg(cv * jax.nn.sigmoid(cv), dm_ref[...], 0, 0)
        ds_ref[...] = _dg(dm_ref[...], w_ref[...], 1, 1)

    return pl.pallas_call(body, out_shape=[jax.ShapeDtypeStruct(w.shape, f32),
                                           jax.ShapeDtypeStruct(c_rows.shape, f32)],
                          compiler_params=pltpu.CompilerParams(vmem_limit_bytes=VMEM_LIMIT), name="adaln_bwd")(c_rows, dm, w)


def c_ctx_grad(parts, c_ctx_row):
    def body(p_ref, c_ref, o_ref):
        total = p_ref[CHIP_SLOTS[0], 0:1, :]
        for s in CHIP_SLOTS[1:]:
            total = total + p_ref[s, 0:1, :]
        _, vjp = jax.vjp(jax.nn.silu, c_ref[...])
        o_ref[...] = vjp(total)[0]

    return pl.pallas_call(body, out_shape=jax.ShapeDtypeStruct((1, D_MODEL), f32), name="c_ctx_grad")(parts, c_ctx_row)


PACK_W = 1024
PACK_ROWS = 8


def _pack(arrs):
    pieces, layout, r0 = [], [], 0
    for a in arrs:
        size = math.prod(a.shape)
        rows = -(-size // (PACK_W * PACK_ROWS)) * PACK_ROWS
        pieces.append(jnp.pad(a.reshape(-1).astype(f32), (0, rows * PACK_W - size)).reshape(rows, PACK_W))
        layout.append((r0, rows, a.shape))
        r0 += rows
    return jnp.concatenate(pieces, axis=0), layout


def _unpack(pack, layout, lead=()):
    n_lead = len(lead)
    outs = []
    for r0, rows, shape in layout:
        piece = pack[(slice(None),) * n_lead + (slice(r0, r0 + rows),)].reshape(lead + (-1,))
        outs.append(piece[..., :math.prod(shape)].reshape(lead + tuple(shape)))
    return outs


W_NAMES = ("c_ctx", "w_ada", "b_ada", "norm1_g", "norm2_g", "w_in", "ret_log_decay", "rwkv_shift_mu", "rwkv_w0",
           "rwkv_w_up", "rwkv_a0", "rwkv_a_up", "rwkv_g_up", "rwkv_k_k", "rwkv_k_a", "rwkv_r_k", "rwkv_ln_w",
           "rwkv_ln_b", "w_out", "w_ff1", "b_ff1", "w_ff2", "b_ff2", "final_g")
COL_SHARDED = ("w_in", "w_ff1")
ROW_SHARDED = ("w_out", "w_ff2")
LAST_SHARDED = ("rwkv_shift_mu", "rwkv_w0", "rwkv_w_up", "rwkv_a0", "rwkv_a_up", "rwkv_g_up")
REPLICATED = ("c_ctx", "b_ada", "norm1_g", "norm2_g", "ret_log_decay", "rwkv_k_k", "rwkv_k_a", "rwkv_r_k",
              "rwkv_ln_w", "rwkv_ln_b", "b_ff1", "b_ff2", "final_g")
N_SHARDS = 4


def _train_step(a):
    x, c, ctx, tgt = a["x"], a["c"], a["ctx"], a["loss_target"]
    bsz = x.shape[0]
    mx, my, mc = _mesh_pos()
    shard = 2 * mx + my
    dev = _device_slot()

    (c_all,) = exchange([jnp.pad(c, ((0, PACK_ROWS - bsz), (0, 0)))], True, ALL_PEERS, "gather_c")
    n_ex = N_DEV * bsz
    c_rows = jnp.concatenate([c_all[:, :bsz].reshape(n_ex, D_MODEL), a["c_ctx"][None, :],
                              jnp.zeros((PACK_ROWS - 1, D_MODEL), f32)], axis=0)
    ada_cols = a["w_ada"].shape[-1]
    b_ada_cols = lax.dynamic_slice_in_dim(a["b_ada"], shard * ada_cols, ada_cols, axis=1)
    mod_cols = adaln_fwd(c_rows, a["w_ada"][0], b_ada_cols)

    halves, small_shards = [], [a[n][0] for n in LAST_SHARDED]
    for n in COL_SHARDED + ROW_SHARDED:
        w = a[n][0].astype(MXU_DTYPE)
        half = w.shape[0] // 2
        halves.append(lax.dynamic_slice_in_dim(w, mc * half, half, axis=0))
    small_pack, small_layout = _pack(small_shards)
    gathered = exchange([mod_cols] + halves + [small_pack], True, ALL_PEERS, "gather_weights")
    mod_all = jnp.stack([gathered[0][s] for s in CHIP_SLOTS], axis=1).reshape(c_rows.shape[0], -1)
    mod_x = lax.dynamic_slice_in_dim(mod_all, dev * bsz, bsz, axis=0).reshape(bsz, 6, D_MODEL)
    mod_ctx = mod_all[n_ex].reshape(6, D_MODEL)
    wt = {}
    for n, gth in zip(COL_SHARDED + ROW_SHARDED, gathered[1:5]):
        per_chip = gth.reshape(N_SHARDS, -1, gth.shape[-1])
        wt[n] = (per_chip.transpose(1, 0, 2).reshape(per_chip.shape[1], -1) if n in COL_SHARDED
                 else per_chip.reshape(-1, per_chip.shape[-1]))
    small_by_chip = _unpack(jnp.stack([gathered[5][s] for s in CHIP_SLOTS]), small_layout, (N_SHARDS,))
    for n, parts in zip(LAST_SHARDED, small_by_chip):
        wt[n] = jnp.concatenate([parts[s] for s in range(N_SHARDS)], axis=-1)
    for n in ("norm1_g", "norm2_g", "rwkv_k_k", "rwkv_k_a", "rwkv_r_k", "rwkv_ln_w", "rwkv_ln_b", "b_ff1", "b_ff2"):
        wt[n] = a[n]
    wt["ret_log_decay"] = a["ret_log_decay"][0]
    wt["final_g"] = a["final_g"][None, :]

    loss, grad_x, g = layer_step(x, ctx, tgt, mod_x, mod_ctx, wt)

    small_names = [n for n in REPLICATED if n not in ("c_ctx", "b_ada")] + list(LAST_SHARDED)
    g_pack, g_layout = _pack([jnp.pad(loss, ((0, 0), (0, PACK_W - loss.shape[1])))] + [g[n] for n in small_names]
                             + [g["mod_x"], g["mod_ctx"]])
    (g_packs,) = exchange([g_pack], True, ALL_PEERS, "gather_small_grads")
    g_sum = _unpack(sum_slots(g_packs, tuple(range(N_DEV)), "sum_small_grads"), g_layout)
    loss_total = g_sum[0][0, 0]
    grads = dict(zip(small_names, g_sum[1:1 + len(small_names)]))
    dmod_ctx = g_sum[-1].reshape(1, -1)
    dmod_x = _unpack(g_packs, g_layout, (N_DEV,))[-2].reshape(n_ex, -1)
    dmod = jnp.concatenate([dmod_x, dmod_ctx, jnp.zeros((PACK_ROWS - 1, dmod_x.shape[1]), f32)], axis=0)
    grads["b_ada"] = column_sum(dmod, "b_ada_grad")
    dmod_cols = lax.dynamic_slice_in_dim(dmod, shard * ada_cols, ada_cols, axis=1)
    grads["w_ada"], dsilu = adaln_bwd(c_rows, dmod_cols, a["w_ada"][0])

    blocks = [jnp.pad(dsilu[n_ex:n_ex + 1], ((0, PACK_ROWS - 1), (0, 0)))[None].repeat(N_DEV, axis=0)]
    for n in COL_SHARDED + ROW_SHARDED:
        gw = g[n]
        if n in COL_SHARDED:
            gw = gw.reshape(gw.shape[0], N_SHARDS, -1).transpose(1, 0, 2)
        blocks.append(gw.reshape(N_DEV, -1, gw.shape[-1]).astype(MXU_DTYPE))
    received = exchange(blocks, False, ALL_PEERS, "scatter_big_grads")
    grads["c_ctx"] = c_ctx_grad(received[0], a["c_ctx"][None, :])
    half_sums = [sum_slots(p, tuple(range(N_DEV)), f"sum_{n}") for n, p in zip(COL_SHARDED + ROW_SHARDED, received[1:])]
    other_halves = sibling_swap(half_sums, "swap_halves", pieces=4)
    for n, mine, other in zip(COL_SHARDED + ROW_SHARDED, half_sums, other_halves):
        rows = mine.shape[0]
        whole = jnp.zeros((2 * rows, mine.shape[1]), f32)
        whole = lax.dynamic_update_slice_in_dim(whole, mine, mc * rows, axis=0)
        grads[n] = lax.dynamic_update_slice_in_dim(whole, other, (1 - mc) * rows, axis=0)
    for n in LAST_SHARDED:
        width = a[n].shape[-1]
        grads[n] = lax.dynamic_slice_in_dim(grads[n], shard * width, width, axis=grads[n].ndim - 1)

    out_g, out_d, out_m, out_v = {}, {}, {}, {}
    for n in ("w_ada",) + COL_SHARDED + ROW_SHARDED:
        out_g[n] = grads[n].reshape(a[n].shape)
        two_d = lambda z: z.reshape(-1, z.shape[-1])
        d, m, v = adamw(two_d(a[n]), two_d(out_g[n]), two_d(a["m_" + n]), two_d(a["v_" + n]), f"adamw_{n}")
        out_d[n], out_m[n], out_v[n] = d.reshape(a[n].shape), m.reshape(a[n].shape), v.reshape(a[n].shape)
    rest = REPLICATED + LAST_SHARDED
    for n in rest:
        out_g[n] = grads[n].reshape(a[n].shape)
    packs = [_pack([src[n] for n in rest])[0] for src in
             ({n: a[n] for n in rest}, out_g, {n: a["m_" + n] for n in rest}, {n: a["v_" + n] for n in rest})]
    _, rest_layout = _pack([a[n] for n in rest])
    for dst, pack in zip((out_d, out_m, out_v), adamw(*packs, "adamw_small")):
        dst.update(zip(rest, _unpack(pack, rest_layout)))
    return (loss_total, grad_x, *[out_g[n] for n in W_NAMES], *[out_d[n] for n in W_NAMES],
            *[out_m[n] for n in W_NAMES], *[out_v[n] for n in W_NAMES])


def kernel(x, c, ctx, c_ctx, w_ada, b_ada, norm1_g, norm2_g, w_in, ret_log_decay, rwkv_shift_mu, rwkv_w0, rwkv_w_up, rwkv_a0, rwkv_a_up, rwkv_g_up, rwkv_k_k, rwkv_k_a, rwkv_r_k, rwkv_ln_w, rwkv_ln_b, w_out, w_ff1, b_ff1, w_ff2, b_ff2, final_g, loss_target, m_c_ctx, m_w_ada, m_b_ada, m_norm1_g, m_norm2_g, m_w_in, m_ret_log_decay, m_rwkv_shift_mu, m_rwkv_w0, m_rwkv_w_up, m_rwkv_a0, m_rwkv_a_up, m_rwkv_g_up, m_rwkv_k_k, m_rwkv_k_a, m_rwkv_r_k, m_rwkv_ln_w, m_rwkv_ln_b, m_w_out, m_w_ff1, m_b_ff1, m_w_ff2, m_b_ff2, m_final_g, v_c_ctx, v_w_ada, v_b_ada, v_norm1_g, v_norm2_g, v_w_in, v_ret_log_decay, v_rwkv_shift_mu, v_rwkv_w0, v_rwkv_w_up, v_rwkv_a0, v_rwkv_a_up, v_rwkv_g_up, v_rwkv_k_k, v_rwkv_k_a, v_rwkv_r_k, v_rwkv_ln_w, v_rwkv_ln_b, v_w_out, v_w_ff1, v_b_ff1, v_w_ff2, v_b_ff2, v_final_g):
    return _train_step(dict(locals()))
```

```python
import functools
import math

import jax
import jax.numpy as jnp
from jax import lax
from jax.experimental import pallas as pl
from jax.experimental.pallas import tpu as pltpu

f32 = jnp.float32
MXU_DTYPE = jnp.bfloat16

D_MODEL = 1024
RET_W = 512
RET_HEADS = 4
RET_DH = 128
RET_CHUNK = 128
RW_W = 512
RW_N = 64
DECAY_LORA = 64
AAA_LORA = 64
GATE_LORA = 128
LORA_W = DECAY_LORA + AAA_LORA + GATE_LORA
D_FF = 4096
RET_COLS = 4 * RET_W
SHIFT_COLS = 3 * RW_W + LORA_W
IN_COLS = RET_COLS + SHIFT_COLS
GRID_W = 64
ROPE_BASE = 10000.0
NORM_EPS = 1e-6
GN_EPS = 64e-5
W_DECAY_SCALE = math.exp(-0.5)
ADAM_LR, ADAM_B1, ADAM_B2, ADAM_EPS, ADAM_WD, ADAM_STEP = 0.001, 0.9, 0.999, 1e-08, 0.01, 10

TOK_TILE = 256
SCAN_CHUNK = 8
SCAN_UNROLL = 2
N_DEV = 8
V7X_VMEM_BYTES = 64 * 1024 * 1024
VMEM_LIMIT = V7X_VMEM_BYTES * 7 // 8


def _cparams(sem):
    return pltpu.CompilerParams(dimension_semantics=sem, vmem_limit_bytes=VMEM_LIMIT)


def _tile(n, cap):
    best = None
    for t in range(128, min(n, cap) + 1, 128):
        if n % t == 0:
            best = t
    return best if best is not None else n


def matmul(a, b, mode, name, out_dtype=f32):
    if mode == "nn":
        (m, k), (k2, n) = a.shape, b.shape
    elif mode == "nt":
        (m, k), (n, k2) = a.shape, b.shape
    else:
        (k, m), (k2, n) = a.shape, b.shape
    assert k == k2, (a.shape, b.shape, mode)
    tm, tn, tk = _tile(m, 512), _tile(n, 768), _tile(k, 1024)
    nk = k // tk
    dims = {"nn": ((1,), (0,)), "nt": ((1,), (1,)), "tn": ((0,), (0,))}[mode]

    def body(a_ref, b_ref, o_ref, acc_ref):
        kk = pl.program_id(2)

        @pl.when(kk == 0)
        def _():
            acc_ref[...] = jnp.zeros_like(acc_ref)

        acc_ref[...] += lax.dot_general(a_ref[...].astype(MXU_DTYPE), b_ref[...].astype(MXU_DTYPE),
                                        (dims, ((), ())), preferred_element_type=f32)

        @pl.when(kk == nk - 1)
        def _():
            o_ref[...] = acc_ref[...].astype(o_ref.dtype)

    if mode == "nn":
        a_spec = pl.BlockSpec((tm, tk), lambda i, j, q: (i, q))
        b_spec = pl.BlockSpec((tk, tn), lambda i, j, q: (q, j))
    elif mode == "nt":
        a_spec = pl.BlockSpec((tm, tk), lambda i, j, q: (i, q))
        b_spec = pl.BlockSpec((tn, tk), lambda i, j, q: (j, q))
    else:
        a_spec = pl.BlockSpec((tk, tm), lambda i, j, q: (q, i))
        b_spec = pl.BlockSpec((tk, tn), lambda i, j, q: (q, j))
    return pl.pallas_call(
        body, grid=(m // tm, n // tn, nk), in_specs=[a_spec, b_spec],
        out_specs=pl.BlockSpec((tm, tn), lambda i, j, q: (i, j)),
        out_shape=jax.ShapeDtypeStruct((m, n), out_dtype),
        scratch_shapes=[pltpu.VMEM((tm, tn), f32)],
        compiler_params=_cparams(("parallel", "parallel", "arbitrary")), name=name)(a, b)


class Tiled:
    def __init__(self, arr, w=None, cidx=0, toff=0):
        self.arr, self.w, self.cidx, self.toff = arr, (arr.shape[-1] if w is None else w), cidx, toff

    def spec(self):
        cidx, toff = self.cidx, self.toff
        return pl.BlockSpec((None, TOK_TILE, self.w), lambda b, i: (b, i + toff, cidx))


class Seg:
    def __init__(self, arr, seg, first):
        self.arr, self.seg, self.first = arr, seg, first

    def spec(self):
        seg = self.seg
        return pl.BlockSpec((None, None, 1, self.arr.shape[-1]), lambda b, i: (b, seg(i), 0, 0))


class Glob:
    def __init__(self, arr):
        self.arr = arr

    def spec(self):
        return pl.BlockSpec(self.arr.shape, lambda b, i: (0,) * self.arr.ndim)


def ew_forward(fn, name, bsz, n_tiles, ins, outs):
    n_in = len(ins)

    def body(*refs):
        res = fn(*[r[...] for r in refs[:n_in]])
        for o_ref, o in zip(refs[n_in:], res):
            o_ref[...] = o.astype(o_ref.dtype)

    out_specs = [pl.BlockSpec((None, TOK_TILE, w), lambda b, i: (b, i, 0)) for w, _ in outs]
    out_shape = [jax.ShapeDtypeStruct((bsz, n_tiles * TOK_TILE, w), dt) for w, dt in outs]
    return pl.pallas_call(body, grid=(bsz, n_tiles), in_specs=[d.spec() for d in ins], out_specs=out_specs,
                          out_shape=out_shape, compiler_params=_cparams(("parallel", "parallel")), name=name)(
        *[d.arr for d in ins])


def ew_backward(fn, name, bsz, n_tiles, ins, cts, want, grad_dtypes=None):
    n_in, n_ct = len(ins), len(cts)
    diff = [k for k in range(n_in) if want[k]]
    grad_dtypes = grad_dtypes or {}

    def body(*refs):
        b, i = pl.program_id(0), pl.program_id(1)
        vals = [r[...] for r in refs[:n_in]]
        ct_vals = tuple(r[...].astype(f32) for r in refs[n_in:n_in + n_ct])
        g_refs = refs[n_in + n_ct:]

        def f(*dvals):
            full = list(vals)
            for k, v in zip(diff, dvals):
                full[k] = v
            return tuple(fn(*full))

        _, vjp = jax.vjp(f, *[vals[k] for k in diff])
        grads = vjp(ct_vals)
        for k, g_ref, g in zip(diff, g_refs, grads):
            d = ins[k]
            if isinstance(d, Tiled):
                g_ref[...] = g.astype(g_ref.dtype)
            else:
                zero = d.first(i) if isinstance(d, Seg) else jnp.logical_and(b == 0, i == 0)

                @pl.when(zero)
                def _(g_ref=g_ref):
                    g_ref[...] = jnp.zeros_like(g_ref)

                g_ref[...] += g

    out_specs, out_shape = [], []
    for k in diff:
        d = ins[k]
        if isinstance(d, Tiled):
            out_specs.append(pl.BlockSpec((None, TOK_TILE, d.w), lambda b, i: (b, i, 0)))
            out_shape.append(jax.ShapeDtypeStruct((bsz, n_tiles * TOK_TILE, d.w), grad_dtypes.get(k, f32)))
        else:
            out_specs.append(d.spec())
            out_shape.append(jax.ShapeDtypeStruct(d.arr.shape, f32))
    return pl.pallas_call(body, grid=(bsz, n_tiles), in_specs=[d.spec() for d in ins] + [c.spec() for c in cts],
                          out_specs=out_specs, out_shape=out_shape,
                          compiler_params=_cparams(("arbitrary", "arbitrary")), name=name)(
        *[d.arr for d in ins], *[c.arr for c in cts])


@jax.custom_vjp
def _mxu_dot(a, b):
    return jnp.dot(a.astype(MXU_DTYPE), b.astype(MXU_DTYPE), preferred_element_type=f32)


def _mxu_dot_fwd(a, b):
    return _mxu_dot(a, b), (a, b)


def _mxu_dot_bwd(res, ct):
    a, b = res
    ct = ct.astype(MXU_DTYPE)
    da = lax.dot_general(ct, b.astype(MXU_DTYPE), (((1,), (1,)), ((), ())), preferred_element_type=f32)
    db = lax.dot_general(a.astype(MXU_DTYPE), ct, (((0,), (0,)), ((), ())), preferred_element_type=f32)
    return da, db


_mxu_dot.defvjp(_mxu_dot_fwd, _mxu_dot_bwd)


def _split_dot_impl(x, ones_mat):
    hi = x.astype(MXU_DTYPE)
    lo = (x - hi.astype(f32)).astype(MXU_DTYPE)
    return jnp.dot(hi, ones_mat, preferred_element_type=f32) + jnp.dot(lo, ones_mat, preferred_element_type=f32)


@jax.custom_vjp
def _split_dot(x, ones_mat):
    return _split_dot_impl(x, ones_mat)


def _split_dot_fwd(x, ones_mat):
    return _split_dot_impl(x, ones_mat), ones_mat


def _split_dot_bwd(ones_mat, ct):
    return _split_dot_impl(ct, ones_mat), None


_split_dot.defvjp(_split_dot_fwd, _split_dot_bwd)


def _block_ones(n, group):
    idx = jnp.arange(n) // group
    return (idx[:, None] == idx[None, :]).astype(MXU_DTYPE)


def _rms(x, g):
    return x * lax.rsqrt(jnp.mean(x * x, axis=-1, keepdims=True) + NORM_EPS) * g


def fn_norm_mod(h, shift, scale, g):
    return (_rms(h, g) * (1.0 + scale) + shift,)


def fn_rwkv_prepare(ks, lora, w0_f, w0_b, a0_f, a0_b, w_up_f, w_up_b, a_up_f, a_up_b, g_up, k_k, k_a, ones64):
    kkr = ks * k_k
    kk = kkr * lax.rsqrt(_split_dot(kkr * kkr, ones64) + 1e-12)
    outs = [kk]
    th = jnp.tanh(lora)
    for w0, a0, w_up, a_up in ((w0_f, a0_f, w_up_f, a_up_f), (w0_b, a0_b, w_up_b, a_up_b)):
        w = jnp.exp(-W_DECAY_SCALE * jax.nn.sigmoid(w0 + _mxu_dot(th, w_up)))
        a = jax.nn.sigmoid(a0 + _mxu_dot(lora, a_up))
        kt = ks * (1.0 + (a - 1.0) * k_a)
        outs += [w, a * kk, kt]
    outs.append(_mxu_dot(jax.nn.sigmoid(lora), g_up))
    return tuple(outs)


def fn_merge(o_f, o_b, g_ret, y_f, y_b, r, kt_f, v, g_rw, r_k, ln_w, ln_b, ones64, ones128):
    o = o_f + o_b
    ret = o * lax.rsqrt(_split_dot(o * o, ones128) * (1.0 / RET_DH) + NORM_EPS) * (g_ret * jax.nn.sigmoid(g_ret))
    y = y_f + y_b
    mean = _split_dot(y, ones64) * (1.0 / RW_N)
    yc = y - mean
    var = _split_dot(yc * yc, ones64) * (1.0 / RW_N)
    y_n = yc * lax.rsqrt(var + GN_EPS) * ln_w + ln_b
    bonus = _split_dot(r * kt_f * r_k, ones64) * v
    return ret, (y_n + bonus) * g_rw


def fn_resid_norm_mod(x, mix, gate, shift, scale, g):
    h1 = x + gate * mix
    return h1, _rms(h1, g) * (1.0 + scale) + shift


def fn_relu2(u, b1):
    return (jnp.square(jnp.maximum(u + b1, 0.0)),)


def fn_loss(h1, f, tgt, gate, b2, g):
    y = _rms(h1 + gate * (f + b2), g)
    err = jnp.square(y - tgt)
    return 0.5 * jnp.sum(jnp.mean(err, axis=-1, keepdims=True), axis=0, keepdims=True)


def loss_and_grads(h1, f, tgt, gate, b2, g, bsz, n_tiles):
    def body(h1_ref, f_ref, t_ref, gate_ref, b2_ref, g_ref, loss_ref, dh1_ref, df_ref, dgate_ref, db2_ref, dg_ref):
        b, i = pl.program_id(0), pl.program_id(1)
        tgt_v = t_ref[...]
        loss, vjp = jax.vjp(lambda a, c, e, p, q: fn_loss(a, c, tgt_v, e, p, q),
                            h1_ref[...], f_ref[...], gate_ref[...], b2_ref[...], g_ref[...])
        dh1, df, dgate, db2, dg = vjp(jnp.ones((1, 1), f32))
        dh1_ref[...] = dh1
        df_ref[...] = df.astype(df_ref.dtype)

        @pl.when(i == 0)
        def _():
            dgate_ref[...] = jnp.zeros_like(dgate_ref)

        @pl.when(jnp.logical_and(b == 0, i == 0))
        def _():
            loss_ref[...] = jnp.zeros_like(loss_ref)
            db2_ref[...] = jnp.zeros_like(db2_ref)
            dg_ref[...] = jnp.zeros_like(dg_ref)

        dgate_ref[...] += dgate
        db2_ref[...] += db2
        dg_ref[...] += dg
        loss_ref[...] += jnp.broadcast_to(loss, loss_ref.shape)

    tile = pl.BlockSpec((None, TOK_TILE, D_MODEL), lambda b, i: (b, i, 0))
    row = pl.BlockSpec((1, D_MODEL), lambda b, i: (0, 0))
    seg = pl.BlockSpec((None, None, 1, D_MODEL), lambda b, i: (b, 0, 0, 0))
    t_tok = n_tiles * TOK_TILE
    return pl.pallas_call(
        body, grid=(bsz, n_tiles), in_specs=[tile, tile, tile, seg, row, row],
        out_specs=[pl.BlockSpec((1, 128), lambda b, i: (0, 0)), tile, tile, seg, row, row],
        out_shape=[jax.ShapeDtypeStruct((1, 128), f32), jax.ShapeDtypeStruct((bsz, t_tok, D_MODEL), f32),
                   jax.ShapeDtypeStruct((bsz, t_tok, D_MODEL), MXU_DTYPE),
                   jax.ShapeDtypeStruct((bsz, 1, 1, D_MODEL), f32),
                   jax.ShapeDtypeStruct((1, D_MODEL), f32), jax.ShapeDtypeStruct((1, D_MODEL), f32)],
        compiler_params=_cparams(("arbitrary", "arbitrary")), name="loss_and_grads")(h1, f, tgt, gate, b2, g)


SHIFT_BLOCK = 256
HALO_ROWS = 8


def _shift_specs(n_tok, col0):
    per_tile = TOK_TILE // HALO_ROWS
    last = n_tok // HALO_ROWS - 1
    tile = pl.BlockSpec((None, TOK_TILE, SHIFT_BLOCK), lambda j, b, i: (b, i, col0 + j))
    prev = pl.BlockSpec((None, HALO_ROWS, SHIFT_BLOCK),
                        lambda j, b, i: (b, jnp.maximum(i * per_tile - 1, 0), col0 + j))
    nxt = pl.BlockSpec((None, HALO_ROWS, SHIFT_BLOCK),
                       lambda j, b, i: (b, jnp.minimum((i + 1) * per_tile, last), col0 + j))
    return tile, prev, nxt


def _shifted(p, prev_ref, next_ref, is_first, is_last):
    row = lax.broadcasted_iota(jnp.int32, p.shape, 0)
    prev_row = jnp.where(is_first, 0.0, prev_ref[HALO_ROWS - 1:HALO_ROWS, :].astype(f32))
    next_row = jnp.where(is_last, 0.0, next_ref[0:1, :].astype(f32))
    prev = jnp.where(row == 0, prev_row, pltpu.roll(p, 1, axis=0))
    nxt = jnp.where(row == TOK_TILE - 1, next_row, pltpu.roll(p, TOK_TILE - 1, axis=0))
    return prev, nxt


def token_shift(px, mu, seg_first, seg_last):
    bsz, n_tok, _ = px.shape
    n_tiles = n_tok // TOK_TILE

    def body(p_ref, prev_ref, next_ref, mu_ref, o_ref):
        i = pl.program_id(2)
        p = p_ref[...]
        prev, nxt = _shifted(p, prev_ref, next_ref, seg_first(i), seg_last(i))
        o_ref[...] = p + mu_ref[0:1, :] * (prev - p) + mu_ref[1:2, :] * (nxt - p)

    tile, prev, nxt = _shift_specs(n_tok, RET_COLS // SHIFT_BLOCK)
    return pl.pallas_call(
        body, grid=(SHIFT_COLS // SHIFT_BLOCK, bsz, n_tiles),
        in_specs=[tile, prev, nxt, pl.BlockSpec((2, SHIFT_BLOCK), lambda j, b, i: (0, j))],
        out_specs=pl.BlockSpec((None, TOK_TILE, SHIFT_BLOCK), lambda j, b, i: (b, i, j)),
        out_shape=jax.ShapeDtypeStruct((bsz, n_tok, SHIFT_COLS), f32),
        compiler_params=_cparams(("parallel", "parallel", "parallel")), name="token_shift")(px, px, px, mu)


def token_shift_bwd(dps, px, mu, seg_first, seg_last):
    bsz, n_tok, _ = px.shape
    n_tiles = n_tok // TOK_TILE

    def body(d_ref, dprev_ref, dnext_ref, p_ref, prev_ref, next_ref, mu_ref, dp_ref, dmu_ref):
        b, i = pl.program_id(1), pl.program_id(2)
        first, last = seg_first(i), seg_last(i)
        d, p = d_ref[...], p_ref[...]
        d_prev, d_next = _shifted(d, dprev_ref, dnext_ref, first, last)
        p_prev, p_next = _shifted(p, prev_ref, next_ref, first, last)
        mu0, mu1 = mu_ref[0:1, :], mu_ref[1:2, :]
        dp_ref[...] = (d + mu0 * (d_next - d) + mu1 * (d_prev - d)).astype(dp_ref.dtype)

        @pl.when(jnp.logical_and(b == 0, i == 0))
        def _():
            dmu_ref[...] = jnp.zeros_like(dmu_ref)

        dmu_ref[0:1, :] += jnp.sum(d * (p_prev - p), axis=0, keepdims=True)
        dmu_ref[1:2, :] += jnp.sum(d * (p_next - p), axis=0, keepdims=True)

    dtile, dprev, dnext = _shift_specs(n_tok, 0)
    tile, prev, nxt = _shift_specs(n_tok, RET_COLS // SHIFT_BLOCK)
    mu_spec = pl.BlockSpec((2, SHIFT_BLOCK), lambda j, b, i: (0, j))
    return pl.pallas_call(
        body, grid=(SHIFT_COLS // SHIFT_BLOCK, bsz, n_tiles),
        in_specs=[dtile, dprev, dnext, tile, prev, nxt, mu_spec],
        out_specs=[pl.BlockSpec((None, TOK_TILE, SHIFT_BLOCK), lambda j, b, i: (b, i, j)), mu_spec],
        out_shape=[jax.ShapeDtypeStruct((bsz, n_tok, SHIFT_COLS), MXU_DTYPE),
                   jax.ShapeDtypeStruct((2, SHIFT_COLS), f32)],
        compiler_params=_cparams(("arbitrary", "arbitrary", "arbitrary")), name="token_shift_bwd")(
        dps, dps, dps, px, px, px, mu)


def _dg(a, b, ca, cb):
    return lax.dot_general(a.astype(MXU_DTYPE), b.astype(MXU_DTYPE), (((ca,), (cb,)), ((), ())),
                           preferred_element_type=f32)


@jax.custom_vjp
def _mm_nt(a, b):
    return _dg(a, b, 1, 1)


_mm_nt.defvjp(lambda a, b: (_dg(a, b, 1, 1), (a, b)),
              lambda res, ct: (_dg(ct, res[1], 1, 0), _dg(ct, res[0], 0, 0)))


@jax.custom_vjp
def _mm_tn(a, b):
    return _dg(a, b, 0, 0)


_mm_tn.defvjp(lambda a, b: (_dg(a, b, 0, 0), (a, b)),
              lambda res, ct: (_dg(res[1], ct, 1, 1), _dg(res[0], ct, 1, 0)))


def _ret_chunk(state, q_raw, k_raw, v, cos, sin, perm, ld_row, head, reverse):
    c = RET_CHUNK
    lane = lax.broadcasted_iota(jnp.int32, ld_row.shape, 1)
    lg = -jnp.exp(jnp.sum(jnp.where(lane == head, ld_row, 0.0), axis=-1, keepdims=True))
    rot = lambda t: t * cos + jnp.dot(t, perm, preferred_element_type=f32, precision=lax.Precision.HIGHEST) * sin
    q = rot(q_raw)
    k = rot(k_raw) * (RET_DH ** -0.5)
    ti = lax.broadcasted_iota(jnp.int32, (c, 1), 0).astype(f32)
    tj = lax.broadcasted_iota(jnp.int32, (1, c), 1).astype(f32)
    if not reverse:
        dist, mask, q_exp, k_exp = ti - tj, (ti - tj) >= 0, ti + 1.0, c - 1.0 - ti
    else:
        dist, mask, q_exp, k_exp = tj - ti, (tj - ti) > 0, c - ti, ti
    decay = jnp.where(mask, jnp.exp(lg * jnp.maximum(dist, 0.0)), 0.0)
    scores = _mm_nt(q, k) * decay
    out = _mxu_dot(scores, v) + _mxu_dot(q * jnp.exp(lg * q_exp), state)
    new_state = state * jnp.exp(lg * c) + _mm_tn(k * jnp.exp(lg * k_exp), v)
    return out, new_state


def _ret_specs(order):
    qkv = [pl.BlockSpec((None, RET_CHUNK, RET_W), functools.partial(lambda b, i, col: (b, order(i), col), col=col))
           for col in range(3)]
    tab = pl.BlockSpec((RET_CHUNK, RET_DH), lambda b, i: (order(i), 0))
    const = pl.BlockSpec((RET_DH, RET_DH), lambda b, i: (0, 0))
    ld = pl.BlockSpec((1, RET_DH), lambda b, i: (0, 0))
    return qkv, tab, const, ld


def retention_fwd(px, cos, sin, perm, ld_row, order, reverse, name):
    bsz, n_tok, _ = px.shape
    n_ch = n_tok // RET_CHUNK

    def body(q_ref, k_ref, v_ref, cos_ref, sin_ref, perm_ref, ld_ref, o_ref, sv_ref, st_ref):
        @pl.when(pl.program_id(1) == 0)
        def _():
            st_ref[...] = jnp.zeros_like(st_ref)

        for h in range(RET_HEADS):
            sl = slice(h * RET_DH, (h + 1) * RET_DH)
            s = st_ref[h]
            sv_ref[h] = s
            o, s_new = _ret_chunk(s, q_ref[:, sl], k_ref[:, sl], v_ref[:, sl], cos_ref[...], sin_ref[...],
                                  perm_ref[...], ld_ref[...], h, reverse)
            o_ref[:, sl] = o
            st_ref[h] = s_new

    qkv, tab, const, ld = _ret_specs(order)
    return pl.pallas_call(
        body, grid=(bsz, n_ch), in_specs=[*qkv, tab, tab, const, ld],
        out_specs=[pl.BlockSpec((None, RET_CHUNK, RET_W), lambda b, i: (b, order(i), 0)),
                   pl.BlockSpec((None, None, RET_HEADS, RET_DH, RET_DH), lambda b, i: (b, i, 0, 0, 0))],
        out_shape=[jax.ShapeDtypeStruct((bsz, n_tok, RET_W), f32),
                   jax.ShapeDtypeStruct((bsz, n_ch, RET_HEADS, RET_DH, RET_DH), f32)],
        scratch_shapes=[pltpu.VMEM((RET_HEADS, RET_DH, RET_DH), f32)],
        compiler_params=_cparams(("parallel", "arbitrary")), name=name)(px, px, px, cos, sin, perm, ld_row)


def retention_bwd(do, px, states, cos, sin, perm, ld_row, order, reverse, name):
    bsz, n_tok, _ = px.shape
    n_ch = n_tok // RET_CHUNK
    back = lambda i: order(n_ch - 1 - i)

    def body(do_ref, q_ref, k_ref, v_ref, sv_ref, cos_ref, sin_ref, perm_ref, ld_ref,
             dq_ref, dk_ref, dv_ref, dld_ref, dst_ref):
        b, i = pl.program_id(0), pl.program_id(1)

        @pl.when(i == 0)
        def _():
            dst_ref[...] = jnp.zeros_like(dst_ref)

        @pl.when(jnp.logical_and(b == 0, i == 0))
        def _():
            dld_ref[...] = jnp.zeros_like(dld_ref)

        cos_v, sin_v, perm_v = cos_ref[...], sin_ref[...], perm_ref[...]
        for h in range(RET_HEADS):
            sl = slice(h * RET_DH, (h + 1) * RET_DH)
            f = lambda s, q, k, v, ld, h=h: _ret_chunk(s, q, k, v, cos_v, sin_v, perm_v, ld, h, reverse)
            _, vjp = jax.vjp(f, sv_ref[h], q_ref[:, sl], k_ref[:, sl], v_ref[:, sl], ld_ref[...])
            ds, dq, dk, dv, dld = vjp((do_ref[:, sl], dst_ref[h]))
            dst_ref[h] = ds
            dq_ref[:, sl] = dq
            dk_ref[:, sl] = dk
            dv_ref[:, sl] = dv
            dld_ref[...] += dld

    qkv, tab, const, ld = _ret_specs(back)
    tok = pl.BlockSpec((None, RET_CHUNK, RET_W), lambda b, i: (b, back(i), 0))
    return pl.pallas_call(
        body, grid=(bsz, n_ch),
        in_specs=[tok, *qkv,
                  pl.BlockSpec((None, None, RET_HEADS, RET_DH, RET_DH), lambda b, i: (b, n_ch - 1 - i, 0, 0, 0)),
                  tab, tab, const, ld],
        out_specs=[tok, tok, tok, ld],
        out_shape=[jax.ShapeDtypeStruct((bsz, n_tok, RET_W), f32)] * 3 + [jax.ShapeDtypeStruct((1, RET_DH), f32)],
        scratch_shapes=[pltpu.VMEM((RET_HEADS, RET_DH, RET_DH), f32)],
        compiler_params=_cparams(("arbitrary", "arbitrary")), name=name)(
        do, px, px, px, states, cos, sin, perm, ld_row)


HALF_W = RW_W // 2


def _head_sum(x, ones):
    xm = x.astype(MXU_DTYPE)
    return jnp.concatenate([jnp.dot(xm[:, :HALF_W], ones, preferred_element_type=f32),
                            jnp.dot(xm[:, HALF_W:], ones, preferred_element_type=f32)], axis=1)


def _stack(parts):
    return jnp.concatenate(parts, axis=0)


def _row(ref, b, t):
    return ref[b, pl.ds(t, 1), :]


SCAN_DIRS = ((False, True), (True, False))
RW_HEADS = RW_W // RW_N
HEAD_ROWS_PAD = 16


def _head_rows(row, mask):
    return jnp.broadcast_to(row, mask.shape) * mask


def _outer(per_value, row, mask_pad):
    return lax.dot_general(per_value.astype(MXU_DTYPE), _head_rows(row, mask_pad).astype(MXU_DTYPE),
                           (((0,), (0,)), ((), ())), preferred_element_type=f32)


def _read(states, rows, mask):
    lhs = _stack([_head_rows(r, mask) for r in rows])
    return lax.dot_general(lhs.astype(MXU_DTYPE), _stack(states).astype(MXU_DTYPE), (((1,), (1,)), ((), ())),
                           preferred_element_type=f32)


def _row_from_heads(per_value, state, mask_pad):
    full = jnp.dot(per_value.astype(MXU_DTYPE), state.astype(MXU_DTYPE), preferred_element_type=f32)
    return jnp.sum(full * mask_pad, axis=0, keepdims=True)


def _scan_specs(bsz, order):
    rows = lambda col=0: pl.BlockSpec((bsz, SCAN_CHUNK, RW_W), lambda i: (0, order(i), col))
    per_value = pl.BlockSpec((bsz, SCAN_CHUNK, HEAD_ROWS_PAD, RW_N), lambda i: (0, order(i), 0, 0))
    raw = pl.BlockSpec((SCAN_CHUNK, RW_HEADS * bsz, RW_N * bsz), lambda i: (order(i), 0, 0))
    return rows, per_value, raw


def _removed(sp, kk_t, ones, bsz):
    removed = _head_sum(_stack([sp[b] * kk_t[b] for b in range(bsz)]), ones)
    return [removed[b * RW_N:(b + 1) * RW_N] for b in range(bsz)]


def _advance(sp, rem, w_t, b_t, vk, bsz):
    return [sp[b] * w_t[b] - rem[b] * b_t[b] + vk[b] for b in range(bsz)]


def heads_to_rows(a):
    b, t, _ = a.shape
    return jnp.pad(a.reshape(b, t, RW_HEADS, RW_N), ((0, 0), (0, 0), (0, HEAD_ROWS_PAD - RW_HEADS), (0, 0)))


def read_out_rows(raw, bsz):
    t = raw.shape[0]
    r5 = raw.reshape(t, bsz, RW_HEADS, bsz, RW_N)
    return jnp.stack([r5[:, b, :, b, :] for b in range(bsz)]).reshape(bsz, t, RW_W)


N_ROWS_FWD = 5
N_ROWS_BWD = 5


def _scan_consts(bsz):
    head = (jnp.arange(RW_W)[None, :] // RW_N == jnp.arange(RW_HEADS)[:, None]).astype(f32)
    return head, jnp.pad(head, ((0, HEAD_ROWS_PAD - RW_HEADS), (0, 0))), _block_ones(HALF_W, RW_N)


def _const_specs(consts):
    return [pl.BlockSpec(c.shape, lambda i: (0, 0)) for c in consts]


def rwkv_scan_fwd(rows_in, v_heads, orders, name):
    bsz, n_tok, _ = rows_in[0][0][0].shape
    n_ch = n_tok // SCAN_CHUNK
    rng = range(bsz)
    consts = _scan_consts(bsz)

    def body(*refs):
        rows = [refs[:N_ROWS_FWD], refs[N_ROWS_FWD:2 * N_ROWS_FWD]]
        v0, v1, head_ref, pad_ref, ones_ref, y0, y1, cs0, cs1, s0, s1, late_ref = refs[2 * N_ROWS_FWD:]
        v_refs, y_refs, cs_refs, s_refs = (v0, v1), (y0, y1), (cs0, cs1), (s0, s1)
        head_v, pad_v, ones_v = head_ref[...], pad_ref[...], ones_ref[...]
        for d in range(2):
            @pl.when(pl.program_id(0) == 0)
            def _(d=d):
                s_refs[d][...] = jnp.zeros_like(s_refs[d])

            cs_refs[d][...] = s_refs[d][...]

        def step(j, carry):
            ts = [SCAN_CHUNK - 1 - j if reverse else j for reverse, _ in SCAN_DIRS]
            sps = [[s_refs[d][b] for b in rng] for d in range(2)]
            rems = [_removed(sps[d], [_row(rows[d][1], b, ts[d]) for b in rng], ones_v, bsz) for d in range(2)]
            vks = [[_outer(v_refs[d][b, ts[d]], _row(rows[d][4], b, ts[d]), pad_v) for b in rng] for d in range(2)]
            for d, (reverse, inclusive) in enumerate(SCAN_DIRS):
                r_ref = rows[d][0]
                if inclusive:
                    before = jnp.maximum(j - 1, 0)
                    late_ref[j] = _read(sps[d], [_row(r_ref, b, before) for b in rng], head_v)
                else:
                    y_refs[d][ts[d]] = _read(sps[d], [_row(r_ref, b, ts[d]) for b in rng], head_v)
            for d in range(2):
                new = _advance(sps[d], rems[d], [_row(rows[d][2], b, ts[d]) for b in rng],
                               [_row(rows[d][3], b, ts[d]) for b in rng], vks[d], bsz)
                for b in rng:
                    s_refs[d][b] = new[b]
            return carry

        lax.fori_loop(0, SCAN_CHUNK, step, 0, unroll=SCAN_UNROLL)
        for d, (reverse, inclusive) in enumerate(SCAN_DIRS):
            if inclusive:
                assert not reverse
                last = SCAN_CHUNK - 1
                late_ref[SCAN_CHUNK] = _read([s_refs[d][b] for b in rng], [rows[d][0][b, last:last + 1, :] for b in rng],
                                             head_v)
                for t in range(SCAN_CHUNK):
                    y_refs[d][t] = late_ref[t + 1]

    specs = [_scan_specs(bsz, orders[d]) for d in range(2)]
    state = pltpu.VMEM((bsz, RW_N, RW_W), f32)
    late = pltpu.VMEM((SCAN_CHUNK + 1, RW_HEADS * bsz, RW_N * bsz), f32)
    start_spec = pl.BlockSpec((None, bsz, RW_N, RW_W), lambda i: (i, 0, 0, 0))
    return pl.pallas_call(
        body, grid=(n_ch,),
        in_specs=[specs[d][0](col) for d in range(2) for _, col in rows_in[d]] + [specs[0][1], specs[1][1]]
        + _const_specs(consts),
        out_specs=[specs[0][2], specs[1][2], start_spec, start_spec],
        out_shape=[jax.ShapeDtypeStruct((n_tok, RW_HEADS * bsz, RW_N * bsz), f32)] * 2
        + [jax.ShapeDtypeStruct((n_ch, bsz, RW_N, RW_W), f32)] * 2,
        scratch_shapes=[state, state, late],
        compiler_params=_cparams(("arbitrary",)), name=name)(
        *[a for d in range(2) for a, _ in rows_in[d]], v_heads, v_heads, *consts)


def rwkv_scan_bwd(rows_in, v_heads, dy_heads, starts, orders, name):
    bsz, n_tok, _ = rows_in[0][0][0].shape
    n_ch = n_tok // SCAN_CHUNK
    backs = [functools.partial(lambda i, order: order(n_ch - 1 - i), order=orders[d]) for d in range(2)]
    rng = range(bsz)
    consts = _scan_consts(bsz)
    n_out, n_scr = 6, 7

    def body(*refs):
        rows = [refs[:N_ROWS_BWD], refs[N_ROWS_BWD:2 * N_ROWS_BWD]]
        rest = refs[2 * N_ROWS_BWD:]
        v_refs, dy_refs, cs_refs, (head_ref, pad_ref, ones_ref) = rest[0:2], rest[2:4], rest[4:6], rest[6:9]
        outs = [rest[9:9 + n_out], rest[9 + n_out:9 + 2 * n_out]]
        scr = [rest[9 + 2 * n_out:9 + 2 * n_out + n_scr], rest[9 + 2 * n_out + n_scr:]]
        head_v, pad_v, ones_v = head_ref[...], pad_ref[...], ones_ref[...]
        for d in range(2):
            s_ref, ds_ref = scr[d][:2]

            @pl.when(pl.program_id(0) == 0)
            def _(ds_ref=ds_ref):
                ds_ref[...] = jnp.zeros_like(ds_ref)

            s_ref[...] = cs_refs[d][...]

        def fstep(j, carry):
            ts = [SCAN_CHUNK - 1 - j if reverse else j for reverse, _ in SCAN_DIRS]
            sps = [[scr[d][0][b] for b in rng] for d in range(2)]
            rems = [_removed(sps[d], [_row(rows[d][1], b, ts[d]) for b in rng], ones_v, bsz) for d in range(2)]
            vks = [[_outer(v_refs[d][b, ts[d]], _row(rows[d][4], b, ts[d]), pad_v) for b in rng] for d in range(2)]
            for d in range(2):
                s_ref, _, hist_ref, rem_ref, _, _, _ = scr[d]
                new = _advance(sps[d], rems[d], [_row(rows[d][2], b, ts[d]) for b in rng],
                               [_row(rows[d][3], b, ts[d]) for b in rng], vks[d], bsz)
                for b in rng:
                    hist_ref[ts[d], b] = sps[d][b]
                    rem_ref[ts[d], b] = rems[d][b]
                    s_ref[b] = new[b]
            return carry

        lax.fori_loop(0, SCAN_CHUNK, fstep, 0, unroll=SCAN_UNROLL)

        def step_of(j, reverse):
            return j if reverse else SCAN_CHUNK - 1 - j

        for d, (reverse, _) in enumerate(SCAN_DIRS):
            t0 = step_of(0, reverse)
            for b in rng:
                scr[d][6][b] = _outer(dy_refs[d][b, t0], rows[d][0][b, t0:t0 + 1, :], pad_v)

        def bstep(j, carry):
            ts = [step_of(j, reverse) for reverse, _ in SCAN_DIRS]
            reads = [[scr[d][6][b] for b in rng] for d in range(2)]
            dss = []
            for d, (_, inclusive) in enumerate(SCAN_DIRS):
                ds = [scr[d][1][b] for b in rng]
                dss.append([ds[b] + reads[d][b] for b in rng] if inclusive else ds)
            drems = [_removed(dss[d], [-_row(rows[d][3], b, ts[d]) for b in rng], ones_v, bsz) for d in range(2)]
            for d, (reverse, _) in enumerate(SCAN_DIRS):
                t_next = step_of(jnp.minimum(j + 1, SCAN_CHUNK - 1), reverse)
                for b in rng:
                    scr[d][6][b] = _outer(dy_refs[d][b, t_next], _row(rows[d][0], b, t_next), pad_v)
                outs[d][5][ts[d]] = _read(dss[d], [_row(rows[d][4], b, ts[d]) for b in rng], head_v)
            for d, (_, inclusive) in enumerate(SCAN_DIRS):
                _, kk_ref, w_ref, _, _ = rows[d]
                _, ds_ref, _, _, dsh_ref, drem_ref, _ = scr[d]
                for b in rng:
                    dsh_ref[ts[d], b] = dss[d][b]
                    drem_ref[ts[d], b] = drems[d][b]
                    dsp = dss[d][b] * _row(w_ref, b, ts[d]) + drems[d][b] * _row(kk_ref, b, ts[d])
                    ds_ref[b] = dsp if inclusive else dsp + reads[d][b]
            return carry

        lax.fori_loop(0, SCAN_CHUNK, bstep, 0, unroll=SCAN_UNROLL)

        rsum = lambda z: jnp.sum(z, axis=0, keepdims=True)
        for d, (reverse, inclusive) in enumerate(SCAN_DIRS):
            dr_ref, dkk_ref, dw_ref, db_ref, dkt_ref, _ = outs[d]
            s_ref, _, hist_ref, rem_ref, dsh_ref, drem_ref, _ = scr[d]
            for t in range(SCAN_CHUNK):
                ts = slice(t, t + 1)
                after = t - 1 if reverse else t + 1
                for b in rng:
                    sp, ds = hist_ref[t, b], dsh_ref[t, b]
                    if not inclusive:
                        seen = sp
                    else:
                        seen = hist_ref[after, b] if 0 <= after < SCAN_CHUNK else s_ref[b]
                    dr_ref[b, ts, :] = _row_from_heads(dy_refs[d][b, t], seen, pad_v)
                    dkt_ref[b, ts, :] = _row_from_heads(v_refs[d][b, t], ds, pad_v)
                    dw_ref[b, ts, :] = rsum(ds * sp)
                    db_ref[b, ts, :] = -rsum(ds * rem_ref[t, b])
                    dkk_ref[b, ts, :] = rsum(sp * drem_ref[t, b])

    specs = [_scan_specs(bsz, backs[d]) for d in range(2)]
    hist = pltpu.VMEM((SCAN_CHUNK, bsz, RW_N, RW_W), f32)
    state = pltpu.VMEM((bsz, RW_N, RW_W), f32)
    start_spec = pl.BlockSpec((None, bsz, RW_N, RW_W), lambda i: (n_ch - 1 - i, 0, 0, 0))
    row_shape = jax.ShapeDtypeStruct((bsz, n_tok, RW_W), f32)
    raw_shape = jax.ShapeDtypeStruct((n_tok, RW_HEADS * bsz, RW_N * bsz), f32)
    return pl.pallas_call(
        body, grid=(n_ch,),
        in_specs=[specs[d][0](col) for d in range(2) for _, col in rows_in[d]]
        + [specs[0][1], specs[1][1]] * 2 + [start_spec, start_spec] + _const_specs(consts),
        out_specs=[spec for d in range(2) for spec in [specs[d][0]()] * 5 + [specs[d][2]]],
        out_shape=([row_shape] * 5 + [raw_shape]) * 2,
        scratch_shapes=[state, state, hist, hist, hist, hist, state] * 2,
        compiler_params=_cparams(("arbitrary",)), name=name)(
        *[a for d in range(2) for a, _ in rows_in[d]], v_heads, v_heads, dy_heads, dy_heads, *starts, *consts)


MOD_NAMES = ("shift1", "scale1", "gate1", "shift2", "scale2", "gate2")


def _rope_tables(t_ctx, t_x):
    quarter = RET_DH // 4
    pos = jnp.arange(t_x)
    inv = jnp.power(ROPE_BASE, -jnp.arange(0, 2 * quarter, 2, dtype=f32) / (2 * quarter))
    ang_r = (pos // GRID_W).astype(f32)[:, None] * inv[None, :]
    ang_c = (pos % GRID_W).astype(f32)[:, None] * inv[None, :]
    cos = jnp.concatenate([jnp.cos(ang_r)] * 2 + [jnp.cos(ang_c)] * 2, axis=1)
    sin = jnp.concatenate([-jnp.sin(ang_r), jnp.sin(ang_r), -jnp.sin(ang_c), jnp.sin(ang_c)], axis=1)
    cos = jnp.concatenate([jnp.ones((t_ctx, RET_DH), f32), cos], axis=0)
    sin = jnp.concatenate([jnp.zeros((t_ctx, RET_DH), f32), sin], axis=0)
    lane = jnp.arange(RET_DH)
    partner = jnp.where(lane % (2 * quarter) < quarter, lane + quarter, lane - quarter)
    perm = (lane[:, None] == partner[None, :]).astype(f32)
    return cos, sin, perm


def _pad_rows(w, lo, total):
    return jnp.pad(w, ((lo, total - lo - w.shape[0]), (0, 0)))


def layer_step(x, ctx, tgt, mod_x, mod_ctx, wt):
    bsz, t_x, _ = x.shape
    t_c = ctx.shape[1]
    t_all = t_c + t_x
    n_ct, n_xt = t_c // TOK_TILE, t_x // TOK_TILE
    n_t = n_ct + n_xt
    assert t_c % TOK_TILE == 0 and t_x % TOK_TILE == 0 and t_c % RET_CHUNK == 0

    seg = lambda i: (i >= n_ct).astype(jnp.int32)
    seg_first = lambda i: jnp.logical_or(i == 0, i == n_ct)
    seg_last = lambda i: jnp.logical_or(i == n_ct - 1, i == n_t - 1)
    mod_all = {n: jnp.stack([jnp.broadcast_to(mod_ctx[k], (bsz, D_MODEL)), mod_x[:, k]], axis=1)[:, :, None, :]
               for k, n in enumerate(MOD_NAMES)}
    mod_lat = {n: mod_x[:, k][:, None, None, :] for k, n in enumerate(MOD_NAMES)}
    both = lambda n: Seg(mod_all[n], seg, seg_first)
    lat = lambda n: Seg(mod_lat[n], lambda i: 0, lambda i: i == 0)
    flat = lambda a: a.reshape(-1, a.shape[-1])
    padc = lambda a: jnp.pad(a, ((0, 0), (t_c, 0), (0, 0)))

    def chunk_orders(n_ctx_chunks, n_chunks):
        fwd = lambda i: i
        bwd = lambda i: jnp.where(i < n_ctx_chunks, n_ctx_chunks - 1 - i, n_chunks + n_ctx_chunks - 1 - i)
        return fwd, bwd

    ones64, ones128 = _block_ones(RW_W, RW_N), _block_ones(RET_W, RET_DH)
    cos, sin, perm = _rope_tables(t_c, t_x)
    ld_rows = [jnp.pad(wt["ret_log_decay"][d][None, :], ((0, 0), (0, RET_DH - RET_HEADS))) for d in range(2)]
    w_up_pad = [_pad_rows(wt["rwkv_w_up"][d], 0, LORA_W) for d in range(2)]
    a_up_pad = [_pad_rows(wt["rwkv_a_up"][d], DECAY_LORA, LORA_W) for d in range(2)]
    g_up_pad = _pad_rows(wt["rwkv_g_up"], DECAY_LORA + AAA_LORA, LORA_W)
    row = lambda a, d: a[d][None, :]

    h = jnp.concatenate([ctx, x], axis=1)
    norm1_ins = lambda: [Tiled(h), both("shift1"), both("scale1"), Glob(wt["norm1_g"])]
    (n1,) = ew_forward(fn_norm_mod, "norm1", bsz, n_t, norm1_ins(), [(D_MODEL, MXU_DTYPE)])
    px = matmul(flat(n1), wt["w_in"], "nn", "proj_in").reshape(bsz, t_all, IN_COLS)
    ps = token_shift(px, wt["rwkv_shift_mu"], seg_first, seg_last)

    def prep_ins(toff=0):
        return [Tiled(ps, RW_W, 1), Tiled(ps, LORA_W, 3 * RW_W // LORA_W),
                Glob(row(wt["rwkv_w0"], 0)), Glob(row(wt["rwkv_w0"], 1)),
                Glob(row(wt["rwkv_a0"], 0)), Glob(row(wt["rwkv_a0"], 1)),
                Glob(w_up_pad[0]), Glob(w_up_pad[1]), Glob(a_up_pad[0]), Glob(a_up_pad[1]), Glob(g_up_pad),
                Glob(wt["rwkv_k_k"]), Glob(wt["rwkv_k_a"]), Glob(ones64)]

    kk, w_f, b_f, kt_f, w_b, b_b, kt_b, g_rw = ew_forward(fn_rwkv_prepare, "rwkv_prepare", bsz, n_t, prep_ins(),
                                                           [(RW_W, f32)] * 8)
    rw_order = chunk_orders(t_c // SCAN_CHUNK, t_all // SCAN_CHUNK)
    ret_order = chunk_orders(t_c // RET_CHUNK, t_all // RET_CHUNK)
    scan_rows = [[(ps, 0), (kk, 0), (w_f, 0), (b_f, 0), (kt_f, 0)], [(ps, 0), (kk, 0), (w_b, 0), (b_b, 0), (kt_b, 0)]]
    v_heads = heads_to_rows(ps[..., 2 * RW_W:3 * RW_W])
    y_raw_f, y_raw_b, start_f, start_b = rwkv_scan_fwd(scan_rows, v_heads, rw_order, "rwkv_scan_fwd")
    y = [read_out_rows(y_raw_f, bsz), read_out_rows(y_raw_b, bsz)]
    o, ret_states = [], []
    for d in range(2):
        o_d, st_d = retention_fwd(px, cos, sin, perm, ld_rows[d], ret_order[d], SCAN_DIRS[d][0], f"retention_fwd{d}")
        o.append(o_d), ret_states.append(st_d)

    def merge_ins():
        return [Tiled(o[0], toff=n_ct), Tiled(o[1], toff=n_ct), Tiled(px, RET_W, 3, n_ct),
                Tiled(y[0], toff=n_ct), Tiled(y[1], toff=n_ct), Tiled(ps, RW_W, 0, n_ct), Tiled(kt_f, toff=n_ct),
                Tiled(ps, RW_W, 2, n_ct), Tiled(g_rw, toff=n_ct),
                Glob(wt["rwkv_r_k"]), Glob(wt["rwkv_ln_w"]), Glob(wt["rwkv_ln_b"]), Glob(ones64), Glob(ones128)]

    ret_out, rw_out = ew_forward(fn_merge, "merge_heads", bsz, n_xt, merge_ins(), [(RET_W, MXU_DTYPE), (RW_W, MXU_DTYPE)])
    merged = jnp.concatenate([ret_out, rw_out], axis=-1)
    mix = matmul(flat(merged), wt["w_out"], "nn", "proj_out").reshape(bsz, t_x, D_MODEL)
    resid_ins = lambda: [Tiled(x), Tiled(mix), lat("gate1"), lat("shift2"), lat("scale2"), Glob(wt["norm2_g"])]
    h1, n2 = ew_forward(fn_resid_norm_mod, "resid_norm2", bsz, n_xt, resid_ins(), [(D_MODEL, f32), (D_MODEL, MXU_DTYPE)])
    u = matmul(flat(n2), wt["w_ff1"], "nn", "ff1").reshape(bsz, t_x, D_FF)
    relu_ins = lambda: [Tiled(u), Glob(wt["b_ff1"])]
    (act,) = ew_forward(fn_relu2, "relu2", bsz, n_xt, relu_ins(), [(D_FF, MXU_DTYPE)])
    ff = matmul(flat(act), wt["w_ff2"], "nn", "ff2").reshape(bsz, t_x, D_MODEL)

    g = {}
    loss, dh1, dff, dgate2, g["b_ff2"], g["final_g"] = loss_and_grads(
        h1, ff, tgt, mod_lat["gate2"], wt["b_ff2"], wt["final_g"], bsz, n_xt)
    dact = matmul(flat(dff), wt["w_ff2"], "nt", "ff2_dx").reshape(bsz, t_x, D_FF)
    g["w_ff2"] = matmul(flat(act), flat(dff), "tn", "ff2_dw")
    du, g["b_ff1"] = ew_backward(fn_relu2, "relu2_bwd", bsz, n_xt, relu_ins(), [Tiled(dact)], [True, True],
                                 {0: MXU_DTYPE})
    dn2 = matmul(flat(du), wt["w_ff1"], "nt", "ff1_dx").reshape(bsz, t_x, D_MODEL)
    g["w_ff1"] = matmul(flat(n2), flat(du), "tn", "ff1_dw")
    dx_res, dmix, dgate1, dshift2, dscale2, g["norm2_g"] = ew_backward(
        fn_resid_norm_mod, "resid_norm2_bwd", bsz, n_xt, resid_ins(), [Tiled(dh1), Tiled(dn2)], [True] * 6,
        {1: MXU_DTYPE})
    dmerged = matmul(flat(dmix), wt["w_out"], "nt", "proj_out_dx").reshape(bsz, t_x, D_MODEL)
    g["w_out"] = matmul(flat(merged), flat(dmix), "tn", "proj_out_dw")
    (do, dg_ret, dy, dr_m, dkt_m, dv_m, dg_rw, g["rwkv_r_k"], g["rwkv_ln_w"], g["rwkv_ln_b"]) = ew_backward(
        fn_merge, "merge_heads_bwd", bsz, n_xt, merge_ins(), [Tiled(dmerged, RET_W, 0), Tiled(dmerged, RW_W, 1)],
        [True, False, True, True, False, True, True, True, True, True, True, True, False, False])
    do, dy = padc(do), padc(dy)

    dqkv, dld = [], []
    for d in range(2):
        *dqkv_d, dld_d = retention_bwd(do, px, ret_states[d], cos, sin, perm, ld_rows[d], ret_order[d],
                                       SCAN_DIRS[d][0], f"retention_bwd{d}")
        dqkv.append(dqkv_d), dld.append(dld_d[0, :RET_HEADS])
    g["ret_log_decay"] = jnp.stack(dld)
    (dr_f, dkk_f, dw_f, db_f, dkt_f, dv_raw_f, dr_b, dkk_b, dw_b, db_b, dkt_b, dv_raw_b) = rwkv_scan_bwd(
        scan_rows, v_heads, heads_to_rows(dy), (start_f, start_b), rw_order, "rwkv_scan_bwd")
    dv_f, dv_b = read_out_rows(dv_raw_f, bsz), read_out_rows(dv_raw_b, bsz)
    prep_cts = [dkk_f + dkk_b, dw_f, db_f, dkt_f + padc(dkt_m), dw_b, db_b, dkt_b, padc(dg_rw)]
    (dks, dlora, dw0_f, dw0_b, da0_f, da0_b, dwup_f, dwup_b, daup_f, daup_b, dgup, g["rwkv_k_k"],
     g["rwkv_k_a"]) = ew_backward(fn_rwkv_prepare, "rwkv_prepare_bwd", bsz, n_t, prep_ins(),
                                  [Tiled(c) for c in prep_cts], [True] * 13 + [False])
    g["rwkv_w0"] = jnp.concatenate([dw0_f, dw0_b], axis=0)
    g["rwkv_a0"] = jnp.concatenate([da0_f, da0_b], axis=0)
    g["rwkv_w_up"] = jnp.stack([dwup_f[:DECAY_LORA], dwup_b[:DECAY_LORA]])
    g["rwkv_a_up"] = jnp.stack([daup_f[DECAY_LORA:DECAY_LORA + AAA_LORA], daup_b[DECAY_LORA:DECAY_LORA + AAA_LORA]])
    g["rwkv_g_up"] = dgup[DECAY_LORA + AAA_LORA:]
    dps = jnp.concatenate([dr_f + dr_b + padc(dr_m), dks, dv_f + dv_b + padc(dv_m), dlora], axis=-1)
    dp_rw, g["rwkv_shift_mu"] = token_shift_bwd(dps, px, wt["rwkv_shift_mu"], seg_first, seg_last)
    dpx = jnp.concatenate([(dqkv[0][k] + dqkv[1][k]).astype(MXU_DTYPE) for k in range(3)]
                          + [padc(dg_ret).astype(MXU_DTYPE), dp_rw], axis=-1)
    dn1 = matmul(flat(dpx), wt["w_in"], "nt", "proj_in_dx").reshape(bsz, t_all, D_MODEL)
    g["w_in"] = matmul(flat(n1), flat(dpx), "tn", "proj_in_dw")
    dh, dshift1, dscale1, g["norm1_g"] = ew_backward(fn_norm_mod, "norm1_bwd", bsz, n_t, norm1_ins(), [Tiled(dn1)],
                                                     [True] * 4)
    grad_x = dh[:, t_c:] + dx_res
    zeros = jnp.zeros((D_MODEL,), f32)
    g["mod_x"] = jnp.stack([dshift1[:, 1, 0], dscale1[:, 1, 0], dgate1[:, 0, 0], dshift2[:, 0, 0], dscale2[:, 0, 0],
                            dgate2[:, 0, 0]], axis=1)
    g["mod_ctx"] = jnp.stack([dshift1[:, 0, 0].sum(0), dscale1[:, 0, 0].sum(0), zeros, zeros, zeros, zeros])
    return loss, grad_x, g


MESH_ID = pl.DeviceIdType.MESH
ALL_PEERS = [(dx, dy, dc) for dx in (0, 1) for dy in (0, 1) for dc in (0, 1)][1:]
SIBLING = [(0, 0, 1)]
CHIP_SLOTS = (0, 2, 4, 6)


def _mesh_pos():
    return lax.axis_index("x"), lax.axis_index("y"), lax.axis_index("c")


def _device_slot():
    x, y, c = _mesh_pos()
    return 4 * x + 2 * y + c


def sibling_swap(arrs, name, pieces=1):
    n = len(arrs)
    assert all(a.shape[0] % pieces == 0 for a in arrs)

    def body(*refs):
        in_refs, out_refs = refs[:n], refs[n:2 * n]
        send_sems, recv_sems = refs[2 * n:]
        x, y, c = _mesh_pos()
        copies = []
        for a in range(n):
            rows = arrs[a].shape[0] // pieces
            for q in range(pieces):
                part = pl.ds(q * rows, rows)
                cp = pltpu.make_async_remote_copy(
                    src_ref=in_refs[a].at[part], dst_ref=out_refs[a].at[part], send_sem=send_sems.at[a * pieces + q],
                    recv_sem=recv_sems.at[a * pieces + q], device_id=(x, y, 1 - c), device_id_type=MESH_ID)
                cp.start()
                copies.append(cp)
        for cp in copies:
            cp.wait()

    any_spec = pl.BlockSpec(memory_space=pl.ANY)
    res = pl.pallas_call(
        body, in_specs=[any_spec] * n, out_specs=[any_spec] * n,
        out_shape=[jax.ShapeDtypeStruct(a.shape, a.dtype) for a in arrs],
        scratch_shapes=[pltpu.SemaphoreType.DMA((n * pieces,)), pltpu.SemaphoreType.DMA((n * pieces,))],
        name=name)(*arrs)
    return list(res)


def exchange(arrs, gather, peers, name, pieces=1):
    n, n_peers = len(arrs), len(peers)
    n_slots = N_DEV
    slot = lambda x, y, c: 4 * x + 2 * y + c
    block_rows = [a.shape[0] if gather else a.shape[1] for a in arrs]
    assert all(r % pieces == 0 for r in block_rows), (block_rows, pieces)

    def body(*refs):
        in_refs, out_refs = refs[:n], refs[n:2 * n]
        send_sems, recv_sems, local_sems = refs[2 * n:]
        x, y, c = _mesh_pos()
        me = slot(x, y, c)
        copies, locals_ = [], []
        for a in range(n):
            own = in_refs[a] if gather else in_refs[a].at[me]
            loc = pltpu.make_async_copy(own, out_refs[a].at[me], local_sems.at[a])
            loc.start()
            locals_.append(loc)
            for k, (dx, dy, dc) in enumerate(peers):
                peer = (1 - x if dx else x, 1 - y if dy else y, 1 - c if dc else c)
                src = in_refs[a] if gather else in_refs[a].at[slot(*peer)]
                for q in range(pieces):
                    part = pl.ds(q * (block_rows[a] // pieces), block_rows[a] // pieces)
                    sem = (a * n_peers + k) * pieces + q
                    cp = pltpu.make_async_remote_copy(
                        src_ref=src.at[part], dst_ref=out_refs[a].at[me, part], send_sem=send_sems.at[sem],
                        recv_sem=recv_sems.at[sem], device_id=peer, device_id_type=MESH_ID)
                    cp.start()
                    copies.append(cp)
        for cp in copies:
            cp.wait()
        for loc in locals_:
            loc.wait()

    any_spec = pl.BlockSpec(memory_space=pl.ANY)
    out_shape = [jax.ShapeDtypeStruct((n_slots,) + (a.shape if gather else a.shape[1:]), a.dtype) for a in arrs]
    n_sems = n * n_peers * pieces
    res = pl.pallas_call(
        body, in_specs=[any_spec] * n, out_specs=[any_spec] * n, out_shape=out_shape,
        scratch_shapes=[pltpu.SemaphoreType.DMA((n_sems,)), pltpu.SemaphoreType.DMA((n_sems,)),
                        pltpu.SemaphoreType.DMA((n,))],
        name=name)(*arrs)
    return list(res)


def gather_two_level(arrs, name):
    n = len(arrs)
    per = 7

    def body(*refs):
        in_refs, out_refs = refs[:n], refs[n:2 * n]
        send_sems, recv_sems = refs[2 * n:]
        x, y, c = _mesh_pos()
        me, sibling = (x, y, c), (x, y, 1 - c)
        chips = [(1 - x, y), (x, 1 - y), (1 - x, 1 - y)]

        def copy(a, k, block, to, src=None):
            rows = out_refs[a].at[4 * block[0] + 2 * block[1] + block[2]]
            return pltpu.make_async_remote_copy(src_ref=rows if src is None else src, dst_ref=rows,
                                                send_sem=send_sems.at[a * per + k], recv_sem=recv_sems.at[a * per + k],
                                                device_id=to, device_id_type=MESH_ID)

        first, passed = [], []
        for a in range(n):
            first.append(copy(a, 0, me, sibling, src=in_refs[a]))
            first += [copy(a, 1 + j, me, (*chip, c), src=in_refs[a]) for j, chip in enumerate(chips)]
        for cp in first:
            cp.start()
        for a in range(n):
            for j, chip in enumerate(chips):
                copy(a, 1 + j, (*chip, c), me).wait_recv()
                fwd = copy(a, 4 + j, (*chip, c), sibling)
                fwd.start()
                passed.append(fwd)
        for a in range(n):
            copy(a, 0, sibling, me).wait_recv()
            for j, chip in enumerate(chips):
                copy(a, 4 + j, (*chip, 1 - c), me).wait_recv()
        for cp in first + passed:
            cp.wait_send()

    any_spec = pl.BlockSpec(memory_space=pl.ANY)
    res = pl.pallas_call(
        body, in_specs=[any_spec] * n, out_specs=[any_spec] * n,
        out_shape=[jax.ShapeDtypeStruct((N_DEV,) + a.shape, a.dtype) for a in arrs],
        scratch_shapes=[pltpu.SemaphoreType.DMA((n * per,)), pltpu.SemaphoreType.DMA((n * per,))],
        name=name)(*arrs)
    return list(res)


def sum_slots(parts, slots, name):
    _, r, c = parts.shape
    tr = r
    for cand in (512, 256, 128, 64, 32, 16, 8):
        if r % cand == 0 and cand * c * 4 * len(slots) <= 8 * 1024 * 1024:
            tr = cand
            break

    def body(p_ref, o_ref):
        acc = p_ref[slots[0]].astype(f32)
        for s in slots[1:]:
            acc = acc + p_ref[s].astype(f32)
        o_ref[...] = acc

    return pl.pallas_call(body, grid=(r // tr,), in_specs=[pl.BlockSpec((parts.shape[0], tr, c), lambda i: (0, i, 0))],
                          out_specs=pl.BlockSpec((tr, c), lambda i: (i, 0)),
                          out_shape=jax.ShapeDtypeStruct((r, c), f32),
                          compiler_params=_cparams(("parallel",)), name=name)(parts)


def column_sum(a, name):
    def body(a_ref, o_ref):
        o_ref[...] = jnp.sum(a_ref[...], axis=0, keepdims=True)

    return pl.pallas_call(body, out_shape=jax.ShapeDtypeStruct((1, a.shape[1]), f32), name=name)(a)


def adamw(w, g, m, v, name):
    r, c = w.shape
    tr = r
    for cand in (256, 128, 64, 32, 16, 8):
        if r % cand == 0:
            tr = cand
            break

    def body(w_ref, g_ref, m_ref, v_ref, d_ref, mo_ref, vo_ref):
        gv = g_ref[...]
        m_new = ADAM_B1 * m_ref[...] + (1.0 - ADAM_B1) * gv
        v_new = ADAM_B2 * v_ref[...] + (1.0 - ADAM_B2) * jnp.square(gv)
        m_hat = m_new / (1.0 - ADAM_B1 ** ADAM_STEP)
        v_hat = v_new / (1.0 - ADAM_B2 ** ADAM_STEP)
        d_ref[...] = -ADAM_LR * (m_hat / (jnp.sqrt(v_hat) + ADAM_EPS) + ADAM_WD * w_ref[...])
        mo_ref[...] = m_new
        vo_ref[...] = v_new

    spec = pl.BlockSpec((tr, c), lambda i: (i, 0))
    return pl.pallas_call(body, grid=(r // tr,), in_specs=[spec] * 4, out_specs=[spec] * 3,
                          out_shape=[jax.ShapeDtypeStruct((r, c), f32)] * 3,
                          compiler_params=_cparams(("parallel",)), name=name)(w, g, m, v)


def adaln_fwd(c_rows, w, b):
    def body(c_ref, w_ref, b_ref, o_ref):
        cv = c_ref[...]
        o_ref[...] = _mxu_dot(cv * jax.nn.sigmoid(cv), w_ref[...]) + b_ref[...]

    return pl.pallas_call(body, out_shape=jax.ShapeDtypeStruct((c_rows.shape[0], w.shape[1]), f32),
                          compiler_params=pltpu.CompilerParams(vmem_limit_bytes=VMEM_LIMIT), name="adaln_fwd")(c_rows, w, b)


def adaln_bwd(c_rows, dm, w):
    def body(c_ref, dm_ref, w_ref, gw_ref, ds_ref):
        cv = c_ref[...]
        gw_ref[...] = _dg(cv * jax.nn.sigmoid(cv), dm_ref[...], 0, 0)
        ds_ref[...] = _dg(dm_ref[...], w_ref[...], 1, 1)

    return pl.pallas_call(body, out_shape=[jax.ShapeDtypeStruct(w.shape, f32),
                                           jax.ShapeDtypeStruct(c_rows.shape, f32)],
                          compiler_params=pltpu.CompilerParams(vmem_limit_bytes=VMEM_LIMIT), name="adaln_bwd")(c_rows, dm, w)


def c_ctx_grad(parts, c_ctx_row):
    def body(p_ref, c_ref, o_ref):
        total = p_ref[CHIP_SLOTS[0], 0:1, :]
        for s in CHIP_SLOTS[1:]:
            total = total + p_ref[s, 0:1, :]
        _, vjp = jax.vjp(jax.nn.silu, c_ref[...])
        o_ref[...] = vjp(total)[0]

    return pl.pallas_call(body, out_shape=jax.ShapeDtypeStruct((1, D_MODEL), f32), name="c_ctx_grad")(parts, c_ctx_row)


PACK_W = 1024
PACK_ROWS = 8


def _pack(arrs):
    pieces, layout, r0 = [], [], 0
    for a in arrs:
        size = math.prod(a.shape)
        rows = -(-size // (PACK_W * PACK_ROWS)) * PACK_ROWS
        pieces.append(jnp.pad(a.reshape(-1).astype(f32), (0, rows * PACK_W - size)).reshape(rows, PACK_W))
        layout.append((r0, rows, a.shape))
        r0 += rows
    return jnp.concatenate(pieces, axis=0), layout


def _unpack(pack, layout, lead=()):
    n_lead = len(lead)
    outs = []
    for r0, rows, shape in layout:
        piece = pack[(slice(None),) * n_lead + (slice(r0, r0 + rows),)].reshape(lead + (-1,))
        outs.append(piece[..., :math.prod(shape)].reshape(lead + tuple(shape)))
    return outs


W_NAMES = ("c_ctx", "w_ada", "b_ada", "norm1_g", "norm2_g", "w_in", "ret_log_decay", "rwkv_shift_mu", "rwkv_w0",
           "rwkv_w_up", "rwkv_a0", "rwkv_a_up", "rwkv_g_up", "rwkv_k_k", "rwkv_k_a", "rwkv_r_k", "rwkv_ln_w",
           "rwkv_ln_b", "w_out", "w_ff1", "b_ff1", "w_ff2", "b_ff2", "final_g")
COL_SHARDED = ("w_in", "w_ff1")
ROW_SHARDED = ("w_out", "w_ff2")
LAST_SHARDED = ("rwkv_shift_mu", "rwkv_w0", "rwkv_w_up", "rwkv_a0", "rwkv_a_up", "rwkv_g_up")
REPLICATED = ("c_ctx", "b_ada", "norm1_g", "norm2_g", "ret_log_decay", "rwkv_k_k", "rwkv_k_a", "rwkv_r_k",
              "rwkv_ln_w", "rwkv_ln_b", "b_ff1", "b_ff2", "final_g")
N_SHARDS = 4


def _train_step(a):
    x, c, ctx, tgt = a["x"], a["c"], a["ctx"], a["loss_target"]
    bsz = x.shape[0]
    mx, my, mc = _mesh_pos()
    shard = 2 * mx + my
    dev = _device_slot()

    (c_all,) = exchange([jnp.pad(c, ((0, PACK_ROWS - bsz), (0, 0)))], True, ALL_PEERS, "gather_c")
    n_ex = N_DEV * bsz
    c_rows = jnp.concatenate([c_all[:, :bsz].reshape(n_ex, D_MODEL), a["c_ctx"][None, :],
                              jnp.zeros((PACK_ROWS - 1, D_MODEL), f32)], axis=0)
    ada_cols = a["w_ada"].shape[-1]
    b_ada_cols = lax.dynamic_slice_in_dim(a["b_ada"], shard * ada_cols, ada_cols, axis=1)
    mod_cols = adaln_fwd(c_rows, a["w_ada"][0], b_ada_cols)

    halves, small_shards = [], [a[n][0] for n in LAST_SHARDED]
    for n in COL_SHARDED + ROW_SHARDED:
        w = a[n][0].astype(MXU_DTYPE)
        half = w.shape[0] // 2
        halves.append(lax.dynamic_slice_in_dim(w, mc * half, half, axis=0))
    small_pack, small_layout = _pack(small_shards)
    own_blocks = [mod_cols] + halves + [small_pack]
    gathered = [lax.dynamic_update_index_in_dim(got, own, dev, 0)
                for got, own in zip(gather_two_level(own_blocks, "gather_weights"), own_blocks)]
    mod_all = jnp.stack([gathered[0][s] for s in CHIP_SLOTS], axis=1).reshape(c_rows.shape[0], -1)
    mod_x = lax.dynamic_slice_in_dim(mod_all, dev * bsz, bsz, axis=0).reshape(bsz, 6, D_MODEL)
    mod_ctx = mod_all[n_ex].reshape(6, D_MODEL)
    wt = {}
    for n, gth in zip(COL_SHARDED + ROW_SHARDED, gathered[1:5]):
        per_chip = gth.reshape(N_SHARDS, -1, gth.shape[-1])
        wt[n] = (per_chip.transpose(1, 0, 2).reshape(per_chip.shape[1], -1) if n in COL_SHARDED
                 else per_chip.reshape(-1, per_chip.shape[-1]))
    small_by_chip = _unpack(jnp.stack([gathered[5][s] for s in CHIP_SLOTS]), small_layout, (N_SHARDS,))
    for n, parts in zip(LAST_SHARDED, small_by_chip):
        wt[n] = jnp.concatenate([parts[s] for s in range(N_SHARDS)], axis=-1)
    for n in ("norm1_g", "norm2_g", "rwkv_k_k", "rwkv_k_a", "rwkv_r_k", "rwkv_ln_w", "rwkv_ln_b", "b_ff1", "b_ff2"):
        wt[n] = a[n]
    wt["ret_log_decay"] = a["ret_log_decay"][0]
    wt["final_g"] = a["final_g"][None, :]

    loss, grad_x, g = layer_step(x, ctx, tgt, mod_x, mod_ctx, wt)

    small_names = [n for n in REPLICATED if n not in ("c_ctx", "b_ada")]
    g_pack, g_layout = _pack([jnp.pad(loss, ((0, 0), (0, PACK_W - loss.shape[1])))] + [g[n] for n in small_names]
                             + [g["mod_x"], g["mod_ctx"]])
    (g_packs,) = gather_two_level([g_pack], "gather_small_grads")
    g_packs = lax.dynamic_update_index_in_dim(g_packs, g_pack, dev, 0)
    g_sum = _unpack(sum_slots(g_packs, tuple(range(N_DEV)), "sum_small_grads"), g_layout)
    loss_total = g_sum[0][0, 0]
    grads = dict(zip(small_names, g_sum[1:1 + len(small_names)]))
    dmod_ctx = g_sum[-1].reshape(1, -1)
    dmod_x = _unpack(g_packs, g_layout, (N_DEV,))[-2].reshape(n_ex, -1)
    dmod = jnp.concatenate([dmod_x, dmod_ctx, jnp.zeros((PACK_ROWS - 1, dmod_x.shape[1]), f32)], axis=0)
    grads["b_ada"] = column_sum(dmod, "b_ada_grad")
    dmod_cols = lax.dynamic_slice_in_dim(dmod, shard * ada_cols, ada_cols, axis=1)
    grads["w_ada"], dsilu = adaln_bwd(c_rows, dmod_cols, a["w_ada"][0])

    blocks = [jnp.pad(dsilu[n_ex:n_ex + 1], ((0, PACK_ROWS - 1), (0, 0)))[None].repeat(N_DEV, axis=0)]
    for n in COL_SHARDED + ROW_SHARDED:
        gw = g[n]
        if n in COL_SHARDED:
            gw = gw.reshape(gw.shape[0], N_SHARDS, -1).transpose(1, 0, 2)
        blocks.append(gw.reshape(N_DEV, -1, gw.shape[-1]).astype(MXU_DTYPE))
    shard_packs = []
    for s in range(N_SHARDS):
        pieces_s = [lax.slice_in_dim(g[n], s * a[n].shape[-1], (s + 1) * a[n].shape[-1], axis=g[n].ndim - 1)
                    for n in LAST_SHARDED]
        pack_s, shard_layout = _pack(pieces_s)
        shard_packs.append(jnp.pad(pack_s, ((0, -pack_s.shape[0] % (2 * PACK_ROWS)), (0, 0))))
    blocks.append(jnp.stack(shard_packs).reshape(N_DEV, -1, PACK_W))
    scattered = COL_SHARDED + ROW_SHARDED + ("small_shards",)
    received = exchange(blocks, False, ALL_PEERS, "scatter_big_grads")
    grads["c_ctx"] = c_ctx_grad(received[0], a["c_ctx"][None, :])
    half_sums = [sum_slots(p, tuple(range(N_DEV)), f"sum_{n}") for n, p in zip(scattered, received[1:])]
    other_halves = sibling_swap(half_sums, "swap_halves")
    for n, mine, other in zip(scattered, half_sums, other_halves):
        rows = mine.shape[0]
        whole = jnp.zeros((2 * rows, mine.shape[1]), f32)
        whole = lax.dynamic_update_slice_in_dim(whole, mine, mc * rows, axis=0)
        grads[n] = lax.dynamic_update_slice_in_dim(whole, other, (1 - mc) * rows, axis=0)
    grads.update(zip(LAST_SHARDED, _unpack(grads.pop("small_shards"), shard_layout)))

    out_g, out_d, out_m, out_v = {}, {}, {}, {}
    for n in ("w_ada",) + COL_SHARDED + ROW_SHARDED:
        out_g[n] = grads[n].reshape(a[n].shape)
        two_d = lambda z: z.reshape(-1, z.shape[-1])
        d, m, v = adamw(two_d(a[n]), two_d(out_g[n]), two_d(a["m_" + n]), two_d(a["v_" + n]), f"adamw_{n}")
        out_d[n], out_m[n], out_v[n] = d.reshape(a[n].shape), m.reshape(a[n].shape), v.reshape(a[n].shape)
    rest = REPLICATED + LAST_SHARDED
    for n in rest:
        out_g[n] = grads[n].reshape(a[n].shape)
    packs = [_pack([src[n] for n in rest])[0] for src in
             ({n: a[n] for n in rest}, out_g, {n: a["m_" + n] for n in rest}, {n: a["v_" + n] for n in rest})]
    _, rest_layout = _pack([a[n] for n in rest])
    for dst, pack in zip((out_d, out_m, out_v), adamw(*packs, "adamw_small")):
        dst.update(zip(rest, _unpack(pack, rest_layout)))
    return (loss_total, grad_x, *[out_g[n] for n in W_NAMES], *[out_d[n] for n in W_NAMES],
            *[out_m[n] for n in W_NAMES], *[out_v[n] for n in W_NAMES])


def kernel(x, c, ctx, c_ctx, w_ada, b_ada, norm1_g, norm2_g, w_in, ret_log_decay, rwkv_shift_mu, rwkv_w0, rwkv_w_up, rwkv_a0, rwkv_a_up, rwkv_g_up, rwkv_k_k, rwkv_k_a, rwkv_r_k, rwkv_ln_w, rwkv_ln_b, w_out, w_ff1, b_ff1, w_ff2, b_ff2, final_g, loss_target, m_c_ctx, m_w_ada, m_b_ada, m_norm1_g, m_norm2_g, m_w_in, m_ret_log_decay, m_rwkv_shift_mu, m_rwkv_w0, m_rwkv_w_up, m_rwkv_a0, m_rwkv_a_up, m_rwkv_g_up, m_rwkv_k_k, m_rwkv_k_a, m_rwkv_r_k, m_rwkv_ln_w, m_rwkv_ln_b, m_w_out, m_w_ff1, m_b_ff1, m_w_ff2, m_b_ff2, m_final_g, v_c_ctx, v_w_ada, v_b_ada, v_norm1_g, v_norm2_g, v_w_in, v_ret_log_decay, v_rwkv_shift_mu, v_rwkv_w0, v_rwkv_w_up, v_rwkv_a0, v_rwkv_a_up, v_rwkv_g_up, v_rwkv_k_k, v_rwkv_k_a, v_rwkv_r_k, v_rwkv_ln_w, v_rwkv_ln_b, v_w_out, v_w_ff1, v_b_ff1, v_w_ff2, v_b_ff2, v_final_g):
    return _train_step(dict(locals()))
```

```python
import functools
import math

import jax
import jax.numpy as jnp
from jax import lax
from jax.experimental import pallas as pl
from jax.experimental.pallas import tpu as pltpu

f32 = jnp.float32
MXU_DTYPE = jnp.bfloat16

D_MODEL = 1024
RET_W = 512
RET_HEADS = 4
RET_DH = 128
RET_CHUNK = 128
RW_W = 512
RW_N = 64
DECAY_LORA = 64
AAA_LORA = 64
GATE_LORA = 128
LORA_W = DECAY_LORA + AAA_LORA + GATE_LORA
D_FF = 4096
RET_COLS = 4 * RET_W
SHIFT_COLS = 3 * RW_W + LORA_W
IN_COLS = RET_COLS + SHIFT_COLS
GRID_W = 64
ROPE_BASE = 10000.0
NORM_EPS = 1e-6
GN_EPS = 64e-5
W_DECAY_SCALE = math.exp(-0.5)
ADAM_LR, ADAM_B1, ADAM_B2, ADAM_EPS, ADAM_WD, ADAM_STEP = 0.001, 0.9, 0.999, 1e-08, 0.01, 10

TOK_TILE = 256
MATMUL_TILE = 1024
SCAN_CHUNK = 8
SCAN_UNROLL = 2
N_DEV = 8
V7X_VMEM_BYTES = 64 * 1024 * 1024
VMEM_LIMIT = V7X_VMEM_BYTES * 7 // 8


def _cparams(sem):
    return pltpu.CompilerParams(dimension_semantics=sem, vmem_limit_bytes=VMEM_LIMIT)


def _tile(n, cap):
    best = None
    for t in range(128, min(n, cap) + 1, 128):
        if n % t == 0:
            best = t
    return best if best is not None else n


def matmul(a, b, mode, name, out_dtype=f32):
    if mode == "nn":
        (m, k), (k2, n) = a.shape, b.shape
    elif mode == "nt":
        (m, k), (n, k2) = a.shape, b.shape
    else:
        (k, m), (k2, n) = a.shape, b.shape
    assert k == k2, (a.shape, b.shape, mode)
    tm, tn, tk = _tile(m, MATMUL_TILE), _tile(n, MATMUL_TILE), _tile(k, MATMUL_TILE)
    nk = k // tk
    dims = {"nn": ((1,), (0,)), "nt": ((1,), (1,)), "tn": ((0,), (0,))}[mode]

    def body(a_ref, b_ref, o_ref, acc_ref):
        kk = pl.program_id(2)

        @pl.when(kk == 0)
        def _():
            acc_ref[...] = jnp.zeros_like(acc_ref)

        acc_ref[...] += lax.dot_general(a_ref[...].astype(MXU_DTYPE), b_ref[...].astype(MXU_DTYPE),
                                        (dims, ((), ())), preferred_element_type=f32)

        @pl.when(kk == nk - 1)
        def _():
            o_ref[...] = acc_ref[...].astype(o_ref.dtype)

    if mode == "nn":
        a_spec = pl.BlockSpec((tm, tk), lambda i, j, q: (i, q))
        b_spec = pl.BlockSpec((tk, tn), lambda i, j, q: (q, j))
    elif mode == "nt":
        a_spec = pl.BlockSpec((tm, tk), lambda i, j, q: (i, q))
        b_spec = pl.BlockSpec((tn, tk), lambda i, j, q: (j, q))
    else:
        a_spec = pl.BlockSpec((tk, tm), lambda i, j, q: (q, i))
        b_spec = pl.BlockSpec((tk, tn), lambda i, j, q: (q, j))
    return pl.pallas_call(
        body, grid=(m // tm, n // tn, nk), in_specs=[a_spec, b_spec],
        out_specs=pl.BlockSpec((tm, tn), lambda i, j, q: (i, j)),
        out_shape=jax.ShapeDtypeStruct((m, n), out_dtype),
        scratch_shapes=[pltpu.VMEM((tm, tn), f32)],
        compiler_params=_cparams(("parallel", "parallel", "arbitrary")), name=name)(a, b)


class Tiled:
    def __init__(self, arr, w=None, cidx=0, toff=0):
        self.arr, self.w, self.cidx, self.toff = arr, (arr.shape[-1] if w is None else w), cidx, toff

    def spec(self):
        cidx, toff = self.cidx, self.toff
        return pl.BlockSpec((None, TOK_TILE, self.w), lambda b, i: (b, i + toff, cidx))


class Seg:
    def __init__(self, arr, seg, first):
        self.arr, self.seg, self.first = arr, seg, first

    def spec(self):
        seg = self.seg
        return pl.BlockSpec((None, None, 1, self.arr.shape[-1]), lambda b, i: (b, seg(i), 0, 0))


class Glob:
    def __init__(self, arr):
        self.arr = arr

    def spec(self):
        return pl.BlockSpec(self.arr.shape, lambda b, i: (0,) * self.arr.ndim)


def ew_forward(fn, name, bsz, n_tiles, ins, outs):
    n_in = len(ins)

    def body(*refs):
        res = fn(*[r[...] for r in refs[:n_in]])
        for o_ref, o in zip(refs[n_in:], res):
            o_ref[...] = o.astype(o_ref.dtype)

    out_specs = [pl.BlockSpec((None, TOK_TILE, w), lambda b, i: (b, i, 0)) for w, _ in outs]
    out_shape = [jax.ShapeDtypeStruct((bsz, n_tiles * TOK_TILE, w), dt) for w, dt in outs]
    return pl.pallas_call(body, grid=(bsz, n_tiles), in_specs=[d.spec() for d in ins], out_specs=out_specs,
                          out_shape=out_shape, compiler_params=_cparams(("parallel", "parallel")), name=name)(
        *[d.arr for d in ins])


def ew_backward(fn, name, bsz, n_tiles, ins, cts, want, grad_dtypes=None):
    n_in, n_ct = len(ins), len(cts)
    diff = [k for k in range(n_in) if want[k]]
    grad_dtypes = grad_dtypes or {}

    def body(*refs):
        b, i = pl.program_id(0), pl.program_id(1)
        vals = [r[...] for r in refs[:n_in]]
        ct_vals = tuple(r[...].astype(f32) for r in refs[n_in:n_in + n_ct])
        g_refs = refs[n_in + n_ct:]

        def f(*dvals):
            full = list(vals)
            for k, v in zip(diff, dvals):
                full[k] = v
            return tuple(fn(*full))

        _, vjp = jax.vjp(f, *[vals[k] for k in diff])
        grads = vjp(ct_vals)
        for k, g_ref, g in zip(diff, g_refs, grads):
            d = ins[k]
            if isinstance(d, Tiled):
                g_ref[...] = g.astype(g_ref.dtype)
            else:
                zero = d.first(i) if isinstance(d, Seg) else jnp.logical_and(b == 0, i == 0)

                @pl.when(zero)
                def _(g_ref=g_ref):
                    g_ref[...] = jnp.zeros_like(g_ref)

                g_ref[...] += g

    out_specs, out_shape = [], []
    for k in diff:
        d = ins[k]
        if isinstance(d, Tiled):
            out_specs.append(pl.BlockSpec((None, TOK_TILE, d.w), lambda b, i: (b, i, 0)))
            out_shape.append(jax.ShapeDtypeStruct((bsz, n_tiles * TOK_TILE, d.w), grad_dtypes.get(k, f32)))
        else:
            out_specs.append(d.spec())
            out_shape.append(jax.ShapeDtypeStruct(d.arr.shape, f32))
    return pl.pallas_call(body, grid=(bsz, n_tiles), in_specs=[d.spec() for d in ins] + [c.spec() for c in cts],
                          out_specs=out_specs, out_shape=out_shape,
                          compiler_params=_cparams(("arbitrary", "arbitrary")), name=name)(
        *[d.arr for d in ins], *[c.arr for c in cts])


@jax.custom_vjp
def _mxu_dot(a, b):
    return jnp.dot(a.astype(MXU_DTYPE), b.astype(MXU_DTYPE), preferred_element_type=f32)


def _mxu_dot_fwd(a, b):
    return _mxu_dot(a, b), (a, b)


def _mxu_dot_bwd(res, ct):
    a, b = res
    ct = ct.astype(MXU_DTYPE)
    da = lax.dot_general(ct, b.astype(MXU_DTYPE), (((1,), (1,)), ((), ())), preferred_element_type=f32)
    db = lax.dot_general(a.astype(MXU_DTYPE), ct, (((0,), (0,)), ((), ())), preferred_element_type=f32)
    return da, db


_mxu_dot.defvjp(_mxu_dot_fwd, _mxu_dot_bwd)


def _split_dot_impl(x, ones_mat):
    hi = x.astype(MXU_DTYPE)
    lo = (x - hi.astype(f32)).astype(MXU_DTYPE)
    return jnp.dot(hi, ones_mat, preferred_element_type=f32) + jnp.dot(lo, ones_mat, preferred_element_type=f32)


@jax.custom_vjp
def _split_dot(x, ones_mat):
    return _split_dot_impl(x, ones_mat)


def _split_dot_fwd(x, ones_mat):
    return _split_dot_impl(x, ones_mat), ones_mat


def _split_dot_bwd(ones_mat, ct):
    return _split_dot_impl(ct, ones_mat), None


_split_dot.defvjp(_split_dot_fwd, _split_dot_bwd)


def _block_ones(n, group):
    idx = jnp.arange(n) // group
    return (idx[:, None] == idx[None, :]).astype(MXU_DTYPE)


def _rms(x, g):
    return x * lax.rsqrt(jnp.mean(x * x, axis=-1, keepdims=True) + NORM_EPS) * g


def fn_norm_mod(h, shift, scale, g):
    return (_rms(h, g) * (1.0 + scale) + shift,)


def fn_rwkv_prepare(ks, lora, w0_f, w0_b, a0_f, a0_b, w_up_f, w_up_b, a_up_f, a_up_b, g_up, k_k, k_a, ones64):
    kkr = ks * k_k
    kk = kkr * lax.rsqrt(_split_dot(kkr * kkr, ones64) + 1e-12)
    outs = [kk]
    th = jnp.tanh(lora)
    for w0, a0, w_up, a_up in ((w0_f, a0_f, w_up_f, a_up_f), (w0_b, a0_b, w_up_b, a_up_b)):
        w = jnp.exp(-W_DECAY_SCALE * jax.nn.sigmoid(w0 + _mxu_dot(th, w_up)))
        a = jax.nn.sigmoid(a0 + _mxu_dot(lora, a_up))
        kt = ks * (1.0 + (a - 1.0) * k_a)
        outs += [w, a * kk, kt]
    outs.append(_mxu_dot(jax.nn.sigmoid(lora), g_up))
    return tuple(outs)


def fn_merge(o_f, o_b, g_ret, y_f, y_b, r, kt_f, v, g_rw, r_k, ln_w, ln_b, ones64, ones128):
    o = o_f + o_b
    ret = o * lax.rsqrt(_split_dot(o * o, ones128) * (1.0 / RET_DH) + NORM_EPS) * (g_ret * jax.nn.sigmoid(g_ret))
    y = y_f + y_b
    mean = _split_dot(y, ones64) * (1.0 / RW_N)
    yc = y - mean
    var = _split_dot(yc * yc, ones64) * (1.0 / RW_N)
    y_n = yc * lax.rsqrt(var + GN_EPS) * ln_w + ln_b
    bonus = _split_dot(r * kt_f * r_k, ones64) * v
    return ret, (y_n + bonus) * g_rw


def fn_resid_norm_mod(x, mix, gate, shift, scale, g):
    h1 = x + gate * mix
    return h1, _rms(h1, g) * (1.0 + scale) + shift


def fn_relu2(u, b1):
    return (jnp.square(jnp.maximum(u + b1, 0.0)),)


def fn_loss(h1, f, tgt, gate, b2, g):
    y = _rms(h1 + gate * (f + b2), g)
    err = jnp.square(y - tgt)
    return 0.5 * jnp.sum(jnp.mean(err, axis=-1, keepdims=True), axis=0, keepdims=True)


def loss_and_grads(h1, f, tgt, gate, b2, g, bsz, n_tiles):
    def body(h1_ref, f_ref, t_ref, gate_ref, b2_ref, g_ref, loss_ref, dh1_ref, df_ref, dgate_ref, db2_ref, dg_ref):
        b, i = pl.program_id(0), pl.program_id(1)
        tgt_v = t_ref[...]
        loss, vjp = jax.vjp(lambda a, c, e, p, q: fn_loss(a, c, tgt_v, e, p, q),
                            h1_ref[...], f_ref[...], gate_ref[...], b2_ref[...], g_ref[...])
        dh1, df, dgate, db2, dg = vjp(jnp.ones((1, 1), f32))
        dh1_ref[...] = dh1
        df_ref[...] = df.astype(df_ref.dtype)

        @pl.when(i == 0)
        def _():
            dgate_ref[...] = jnp.zeros_like(dgate_ref)

        @pl.when(jnp.logical_and(b == 0, i == 0))
        def _():
            loss_ref[...] = jnp.zeros_like(loss_ref)
            db2_ref[...] = jnp.zeros_like(db2_ref)
            dg_ref[...] = jnp.zeros_like(dg_ref)

        dgate_ref[...] += dgate
        db2_ref[...] += db2
        dg_ref[...] += dg
        loss_ref[...] += jnp.broadcast_to(loss, loss_ref.shape)

    tile = pl.BlockSpec((None, TOK_TILE, D_MODEL), lambda b, i: (b, i, 0))
    row = pl.BlockSpec((1, D_MODEL), lambda b, i: (0, 0))
    seg = pl.BlockSpec((None, None, 1, D_MODEL), lambda b, i: (b, 0, 0, 0))
    t_tok = n_tiles * TOK_TILE
    return pl.pallas_call(
        body, grid=(bsz, n_tiles), in_specs=[tile, tile, tile, seg, row, row],
        out_specs=[pl.BlockSpec((1, 128), lambda b, i: (0, 0)), tile, tile, seg, row, row],
        out_shape=[jax.ShapeDtypeStruct((1, 128), f32), jax.ShapeDtypeStruct((bsz, t_tok, D_MODEL), f32),
                   jax.ShapeDtypeStruct((bsz, t_tok, D_MODEL), MXU_DTYPE),
                   jax.ShapeDtypeStruct((bsz, 1, 1, D_MODEL), f32),
                   jax.ShapeDtypeStruct((1, D_MODEL), f32), jax.ShapeDtypeStruct((1, D_MODEL), f32)],
        compiler_params=_cparams(("arbitrary", "arbitrary")), name="loss_and_grads")(h1, f, tgt, gate, b2, g)


SHIFT_BLOCK = SHIFT_COLS
HALO_ROWS = 8


def _shift_specs(n_tok, col0):
    per_tile = TOK_TILE // HALO_ROWS
    last = n_tok // HALO_ROWS - 1
    tile = pl.BlockSpec((None, TOK_TILE, SHIFT_BLOCK), lambda j, b, i: (b, i, col0 + j))
    prev = pl.BlockSpec((None, HALO_ROWS, SHIFT_BLOCK),
                        lambda j, b, i: (b, jnp.maximum(i * per_tile - 1, 0), col0 + j))
    nxt = pl.BlockSpec((None, HALO_ROWS, SHIFT_BLOCK),
                       lambda j, b, i: (b, jnp.minimum((i + 1) * per_tile, last), col0 + j))
    return tile, prev, nxt


def _shifted(p, prev_ref, next_ref, is_first, is_last):
    row = lax.broadcasted_iota(jnp.int32, p.shape, 0)
    prev_row = jnp.where(is_first, 0.0, prev_ref[HALO_ROWS - 1:HALO_ROWS, :].astype(f32))
    next_row = jnp.where(is_last, 0.0, next_ref[0:1, :].astype(f32))
    prev = jnp.where(row == 0, prev_row, pltpu.roll(p, 1, axis=0))
    nxt = jnp.where(row == TOK_TILE - 1, next_row, pltpu.roll(p, TOK_TILE - 1, axis=0))
    return prev, nxt


def token_shift(px, mu, seg_first, seg_last):
    bsz, n_tok, _ = px.shape
    n_tiles = n_tok // TOK_TILE

    def body(p_ref, prev_ref, next_ref, mu_ref, o_ref):
        i = pl.program_id(2)
        p = p_ref[...]
        prev, nxt = _shifted(p, prev_ref, next_ref, seg_first(i), seg_last(i))
        o_ref[...] = p + mu_ref[0:1, :] * (prev - p) + mu_ref[1:2, :] * (nxt - p)

    tile, prev, nxt = _shift_specs(n_tok, 0)
    return pl.pallas_call(
        body, grid=(SHIFT_COLS // SHIFT_BLOCK, bsz, n_tiles),
        in_specs=[tile, prev, nxt, pl.BlockSpec((2, SHIFT_BLOCK), lambda j, b, i: (0, j))],
        out_specs=pl.BlockSpec((None, TOK_TILE, SHIFT_BLOCK), lambda j, b, i: (b, i, j)),
        out_shape=jax.ShapeDtypeStruct((bsz, n_tok, SHIFT_COLS), f32),
        compiler_params=_cparams(("parallel", "parallel", "parallel")), name="token_shift")(px, px, px, mu)


def token_shift_bwd(dps, px, mu, seg_first, seg_last):
    bsz, n_tok, _ = px.shape
    n_tiles = n_tok // TOK_TILE

    def body(d_ref, dprev_ref, dnext_ref, p_ref, prev_ref, next_ref, mu_ref, dp_ref, dmu_ref):
        b, i = pl.program_id(1), pl.program_id(2)
        first, last = seg_first(i), seg_last(i)
        d, p = d_ref[...], p_ref[...]
        d_prev, d_next = _shifted(d, dprev_ref, dnext_ref, first, last)
        p_prev, p_next = _shifted(p, prev_ref, next_ref, first, last)
        mu0, mu1 = mu_ref[0:1, :], mu_ref[1:2, :]
        dp_ref[...] = (d + mu0 * (d_next - d) + mu1 * (d_prev - d)).astype(dp_ref.dtype)

        @pl.when(jnp.logical_and(b == 0, i == 0))
        def _():
            dmu_ref[...] = jnp.zeros_like(dmu_ref)

        dmu_ref[0:1, :] += jnp.sum(d * (p_prev - p), axis=0, keepdims=True)
        dmu_ref[1:2, :] += jnp.sum(d * (p_next - p), axis=0, keepdims=True)

    dtile, dprev, dnext = _shift_specs(n_tok, 0)
    tile, prev, nxt = _shift_specs(n_tok, 0)
    mu_spec = pl.BlockSpec((2, SHIFT_BLOCK), lambda j, b, i: (0, j))
    return pl.pallas_call(
        body, grid=(SHIFT_COLS // SHIFT_BLOCK, bsz, n_tiles),
        in_specs=[dtile, dprev, dnext, tile, prev, nxt, mu_spec],
        out_specs=[pl.BlockSpec((None, TOK_TILE, SHIFT_BLOCK), lambda j, b, i: (b, i, j)), mu_spec],
        out_shape=[jax.ShapeDtypeStruct((bsz, n_tok, SHIFT_COLS), MXU_DTYPE),
                   jax.ShapeDtypeStruct((2, SHIFT_COLS), f32)],
        compiler_params=_cparams(("arbitrary", "arbitrary", "arbitrary")), name="token_shift_bwd")(
        dps, dps, dps, px, px, px, mu)


def _dg(a, b, ca, cb):
    return lax.dot_general(a.astype(MXU_DTYPE), b.astype(MXU_DTYPE), (((ca,), (cb,)), ((), ())),
                           preferred_element_type=f32)


@jax.custom_vjp
def _mm_nt(a, b):
    return _dg(a, b, 1, 1)


_mm_nt.defvjp(lambda a, b: (_dg(a, b, 1, 1), (a, b)),
              lambda res, ct: (_dg(ct, res[1], 1, 0), _dg(ct, res[0], 0, 0)))


@jax.custom_vjp
def _mm_tn(a, b):
    return _dg(a, b, 0, 0)


_mm_tn.defvjp(lambda a, b: (_dg(a, b, 0, 0), (a, b)),
              lambda res, ct: (_dg(res[1], ct, 1, 1), _dg(res[0], ct, 1, 0)))


def _ret_chunk(state, q_raw, k_raw, v, cos, sin, perm, ld_row, head, reverse):
    c = RET_CHUNK
    lane = lax.broadcasted_iota(jnp.int32, ld_row.shape, 1)
    lg = -jnp.exp(jnp.sum(jnp.where(lane == head, ld_row, 0.0), axis=-1, keepdims=True))
    rot = lambda t: t * cos + jnp.dot(t, perm, preferred_element_type=f32, precision=lax.Precision.HIGHEST) * sin
    q = rot(q_raw)
    k = rot(k_raw) * (RET_DH ** -0.5)
    ti = lax.broadcasted_iota(jnp.int32, (c, 1), 0).astype(f32)
    tj = lax.broadcasted_iota(jnp.int32, (1, c), 1).astype(f32)
    if not reverse:
        dist, mask, q_exp, k_exp = ti - tj, (ti - tj) >= 0, ti + 1.0, c - 1.0 - ti
    else:
        dist, mask, q_exp, k_exp = tj - ti, (tj - ti) > 0, c - ti, ti
    decay = jnp.where(mask, jnp.exp(lg * jnp.maximum(dist, 0.0)), 0.0)
    scores = _mm_nt(q, k) * decay
    out = _mxu_dot(scores, v) + _mxu_dot(q * jnp.exp(lg * q_exp), state)
    new_state = state * jnp.exp(lg * c) + _mm_tn(k * jnp.exp(lg * k_exp), v)
    return out, new_state


def _ret_specs(order):
    qkv = [pl.BlockSpec((None, RET_CHUNK, RET_W), functools.partial(lambda b, i, col: (b, order(i), col), col=col))
           for col in range(3)]
    tab = pl.BlockSpec((RET_CHUNK, RET_DH), lambda b, i: (order(i), 0))
    const = pl.BlockSpec((RET_DH, RET_DH), lambda b, i: (0, 0))
    ld = pl.BlockSpec((1, RET_DH), lambda b, i: (0, 0))
    return qkv, tab, const, ld


def retention_fwd(px, cos, sin, perm, ld_row, order, reverse, name):
    bsz, n_tok, _ = px.shape
    n_ch = n_tok // RET_CHUNK

    def body(q_ref, k_ref, v_ref, cos_ref, sin_ref, perm_ref, ld_ref, o_ref, sv_ref, st_ref):
        @pl.when(pl.program_id(1) == 0)
        def _():
            st_ref[...] = jnp.zeros_like(st_ref)

        for h in range(RET_HEADS):
            sl = slice(h * RET_DH, (h + 1) * RET_DH)
            s = st_ref[h]
            sv_ref[h] = s
            o, s_new = _ret_chunk(s, q_ref[:, sl], k_ref[:, sl], v_ref[:, sl], cos_ref[...], sin_ref[...],
                                  perm_ref[...], ld_ref[...], h, reverse)
            o_ref[:, sl] = o
            st_ref[h] = s_new

    qkv, tab, const, ld = _ret_specs(order)
    return pl.pallas_call(
        body, grid=(bsz, n_ch), in_specs=[*qkv, tab, tab, const, ld],
        out_specs=[pl.BlockSpec((None, RET_CHUNK, RET_W), lambda b, i: (b, order(i), 0)),
                   pl.BlockSpec((None, None, RET_HEADS, RET_DH, RET_DH), lambda b, i: (b, i, 0, 0, 0))],
        out_shape=[jax.ShapeDtypeStruct((bsz, n_tok, RET_W), f32),
                   jax.ShapeDtypeStruct((bsz, n_ch, RET_HEADS, RET_DH, RET_DH), f32)],
        scratch_shapes=[pltpu.VMEM((RET_HEADS, RET_DH, RET_DH), f32)],
        compiler_params=_cparams(("parallel", "arbitrary")), name=name)(px, px, px, cos, sin, perm, ld_row)


def retention_bwd(do, px, states, cos, sin, perm, ld_row, order, reverse, name):
    bsz, n_tok, _ = px.shape
    n_ch = n_tok // RET_CHUNK
    back = lambda i: order(n_ch - 1 - i)

    def body(do_ref, q_ref, k_ref, v_ref, sv_ref, cos_ref, sin_ref, perm_ref, ld_ref,
             dq_ref, dk_ref, dv_ref, dld_ref, dst_ref):
        b, i = pl.program_id(0), pl.program_id(1)

        @pl.when(i == 0)
        def _():
            dst_ref[...] = jnp.zeros_like(dst_ref)

        @pl.when(jnp.logical_and(b == 0, i == 0))
        def _():
            dld_ref[...] = jnp.zeros_like(dld_ref)

        cos_v, sin_v, perm_v = cos_ref[...], sin_ref[...], perm_ref[...]
        for h in range(RET_HEADS):
            sl = slice(h * RET_DH, (h + 1) * RET_DH)
            f = lambda s, q, k, v, ld, h=h: _ret_chunk(s, q, k, v, cos_v, sin_v, perm_v, ld, h, reverse)
            _, vjp = jax.vjp(f, sv_ref[h], q_ref[:, sl], k_ref[:, sl], v_ref[:, sl], ld_ref[...])
            ds, dq, dk, dv, dld = vjp((do_ref[:, sl], dst_ref[h]))
            dst_ref[h] = ds
            dq_ref[:, sl] = dq
            dk_ref[:, sl] = dk
            dv_ref[:, sl] = dv
            dld_ref[...] += dld

    qkv, tab, const, ld = _ret_specs(back)
    tok = pl.BlockSpec((None, RET_CHUNK, RET_W), lambda b, i: (b, back(i), 0))
    return pl.pallas_call(
        body, grid=(bsz, n_ch),
        in_specs=[tok, *qkv,
                  pl.BlockSpec((None, None, RET_HEADS, RET_DH, RET_DH), lambda b, i: (b, n_ch - 1 - i, 0, 0, 0)),
                  tab, tab, const, ld],
        out_specs=[tok, tok, tok, ld],
        out_shape=[jax.ShapeDtypeStruct((bsz, n_tok, RET_W), f32)] * 3 + [jax.ShapeDtypeStruct((1, RET_DH), f32)],
        scratch_shapes=[pltpu.VMEM((RET_HEADS, RET_DH, RET_DH), f32)],
        compiler_params=_cparams(("arbitrary", "arbitrary")), name=name)(
        do, px, px, px, states, cos, sin, perm, ld_row)


HALF_W = RW_W // 2


def _head_sum(x, ones):
    xm = x.astype(MXU_DTYPE)
    return jnp.concatenate([jnp.dot(xm[:, :HALF_W], ones, preferred_element_type=f32),
                            jnp.dot(xm[:, HALF_W:], ones, preferred_element_type=f32)], axis=1)


def _stack(parts):
    return jnp.concatenate(parts, axis=0)


def _row(ref, b, t):
    return ref[b, pl.ds(t, 1), :]


SCAN_DIRS = ((False, True), (True, False))
RW_HEADS = RW_W // RW_N
HEAD_ROWS_PAD = 16


def _head_rows(row, mask):
    return jnp.broadcast_to(row, mask.shape) * mask


def _outer(per_value, row, mask_pad):
    return lax.dot_general(per_value.astype(MXU_DTYPE), _head_rows(row, mask_pad).astype(MXU_DTYPE),
                           (((0,), (0,)), ((), ())), preferred_element_type=f32)


def _read(states, rows, mask):
    lhs = _stack([_head_rows(r, mask) for r in rows])
    return lax.dot_general(lhs.astype(MXU_DTYPE), _stack(states).astype(MXU_DTYPE), (((1,), (1,)), ((), ())),
                           preferred_element_type=f32)


def _row_from_heads(per_value, state, mask_pad):
    full = jnp.dot(per_value.astype(MXU_DTYPE), state.astype(MXU_DTYPE), preferred_element_type=f32)
    return jnp.sum(full * mask_pad, axis=0, keepdims=True)


def _scan_specs(bsz, order):
    rows = lambda col=0: pl.BlockSpec((bsz, SCAN_CHUNK, RW_W), lambda i: (0, order(i), col))
    per_value = pl.BlockSpec((bsz, SCAN_CHUNK, HEAD_ROWS_PAD, RW_N), lambda i: (0, order(i), 0, 0))
    raw = pl.BlockSpec((SCAN_CHUNK, RW_HEADS * bsz, RW_N * bsz), lambda i: (order(i), 0, 0))
    return rows, per_value, raw


def _removed(sp, kk_t, ones, bsz):
    removed = _head_sum(_stack([sp[b] * kk_t[b] for b in range(bsz)]), ones)
    return [removed[b * RW_N:(b + 1) * RW_N] for b in range(bsz)]


def _advance(sp, rem, w_t, b_t, vk, bsz):
    return [sp[b] * w_t[b] - rem[b] * b_t[b] + vk[b] for b in range(bsz)]


def heads_to_rows(a):
    b, t, _ = a.shape
    return jnp.pad(a.reshape(b, t, RW_HEADS, RW_N), ((0, 0), (0, 0), (0, HEAD_ROWS_PAD - RW_HEADS), (0, 0)))


def read_out_rows(raw, bsz):
    t = raw.shape[0]
    r5 = raw.reshape(t, bsz, RW_HEADS, bsz, RW_N)
    return jnp.stack([r5[:, b, :, b, :] for b in range(bsz)]).reshape(bsz, t, RW_W)


N_ROWS_FWD = 5
N_ROWS_BWD = 5


def _scan_consts(bsz):
    head = (jnp.arange(RW_W)[None, :] // RW_N == jnp.arange(RW_HEADS)[:, None]).astype(f32)
    return head, jnp.pad(head, ((0, HEAD_ROWS_PAD - RW_HEADS), (0, 0))), _block_ones(HALF_W, RW_N)


def _const_specs(consts):
    return [pl.BlockSpec(c.shape, lambda i: (0, 0)) for c in consts]


def rwkv_scan_fwd(rows_in, v_heads, orders, name):
    bsz, n_tok, _ = rows_in[0][0][0].shape
    n_ch = n_tok // SCAN_CHUNK
    rng = range(bsz)
    consts = _scan_consts(bsz)

    def body(*refs):
        rows = [refs[:N_ROWS_FWD], refs[N_ROWS_FWD:2 * N_ROWS_FWD]]
        v0, v1, head_ref, pad_ref, ones_ref, y0, y1, cs0, cs1, s0, s1, late_ref = refs[2 * N_ROWS_FWD:]
        v_refs, y_refs, cs_refs, s_refs = (v0, v1), (y0, y1), (cs0, cs1), (s0, s1)
        head_v, pad_v, ones_v = head_ref[...], pad_ref[...], ones_ref[...]
        for d in range(2):
            @pl.when(pl.program_id(0) == 0)
            def _(d=d):
                s_refs[d][...] = jnp.zeros_like(s_refs[d])

            cs_refs[d][...] = s_refs[d][...]

        def step(j, carry):
            ts = [SCAN_CHUNK - 1 - j if reverse else j for reverse, _ in SCAN_DIRS]
            sps = [[s_refs[d][b] for b in rng] for d in range(2)]
            rems = [_removed(sps[d], [_row(rows[d][1], b, ts[d]) for b in rng], ones_v, bsz) for d in range(2)]
            vks = [[_outer(v_refs[d][b, ts[d]], _row(rows[d][4], b, ts[d]), pad_v) for b in rng] for d in range(2)]
            for d, (reverse, inclusive) in enumerate(SCAN_DIRS):
                r_ref = rows[d][0]
                if inclusive:
                    before = jnp.maximum(j - 1, 0)
                    late_ref[j] = _read(sps[d], [_row(r_ref, b, before) for b in rng], head_v)
                else:
                    y_refs[d][ts[d]] = _read(sps[d], [_row(r_ref, b, ts[d]) for b in rng], head_v)
            for d in range(2):
                new = _advance(sps[d], rems[d], [_row(rows[d][2], b, ts[d]) for b in rng],
                               [_row(rows[d][3], b, ts[d]) for b in rng], vks[d], bsz)
                for b in rng:
                    s_refs[d][b] = new[b]
            return carry

        lax.fori_loop(0, SCAN_CHUNK, step, 0, unroll=SCAN_UNROLL)
        for d, (reverse, inclusive) in enumerate(SCAN_DIRS):
            if inclusive:
                assert not reverse
                last = SCAN_CHUNK - 1
                late_ref[SCAN_CHUNK] = _read([s_refs[d][b] for b in rng], [rows[d][0][b, last:last + 1, :] for b in rng],
                                             head_v)
                for t in range(SCAN_CHUNK):
                    y_refs[d][t] = late_ref[t + 1]

    specs = [_scan_specs(bsz, orders[d]) for d in range(2)]
    state = pltpu.VMEM((bsz, RW_N, RW_W), f32)
    late = pltpu.VMEM((SCAN_CHUNK + 1, RW_HEADS * bsz, RW_N * bsz), f32)
    start_spec = pl.BlockSpec((None, bsz, RW_N, RW_W), lambda i: (i, 0, 0, 0))
    return pl.pallas_call(
        body, grid=(n_ch,),
        in_specs=[specs[d][0](col) for d in range(2) for _, col in rows_in[d]] + [specs[0][1], specs[1][1]]
        + _const_specs(consts),
        out_specs=[specs[0][2], specs[1][2], start_spec, start_spec],
        out_shape=[jax.ShapeDtypeStruct((n_tok, RW_HEADS * bsz, RW_N * bsz), f32)] * 2
        + [jax.ShapeDtypeStruct((n_ch, bsz, RW_N, RW_W), f32)] * 2,
        scratch_shapes=[state, state, late],
        compiler_params=_cparams(("arbitrary",)), name=name)(
        *[a for d in range(2) for a, _ in rows_in[d]], v_heads, v_heads, *consts)


def rwkv_scan_bwd(rows_in, v_heads, dy_heads, starts, orders, name):
    bsz, n_tok, _ = rows_in[0][0][0].shape
    n_ch = n_tok // SCAN_CHUNK
    backs = [functools.partial(lambda i, order: order(n_ch - 1 - i), order=orders[d]) for d in range(2)]
    rng = range(bsz)
    consts = _scan_consts(bsz)
    n_out, n_scr = 6, 7

    def body(*refs):
        rows = [refs[:N_ROWS_BWD], refs[N_ROWS_BWD:2 * N_ROWS_BWD]]
        rest = refs[2 * N_ROWS_BWD:]
        v_refs, dy_refs, cs_refs, (head_ref, pad_ref, ones_ref) = rest[0:2], rest[2:4], rest[4:6], rest[6:9]
        outs = [rest[9:9 + n_out], rest[9 + n_out:9 + 2 * n_out]]
        scr = [rest[9 + 2 * n_out:9 + 2 * n_out + n_scr], rest[9 + 2 * n_out + n_scr:]]
        head_v, pad_v, ones_v = head_ref[...], pad_ref[...], ones_ref[...]
        for d in range(2):
            s_ref, ds_ref = scr[d][:2]

            @pl.when(pl.program_id(0) == 0)
            def _(ds_ref=ds_ref):
                ds_ref[...] = jnp.zeros_like(ds_ref)

            s_ref[...] = cs_refs[d][...]

        def fstep(j, carry):
            ts = [SCAN_CHUNK - 1 - j if reverse else j for reverse, _ in SCAN_DIRS]
            sps = [[scr[d][0][b] for b in rng] for d in range(2)]
            rems = [_removed(sps[d], [_row(rows[d][1], b, ts[d]) for b in rng], ones_v, bsz) for d in range(2)]
            vks = [[_outer(v_refs[d][b, ts[d]], _row(rows[d][4], b, ts[d]), pad_v) for b in rng] for d in range(2)]
            for d in range(2):
                s_ref, _, hist_ref, rem_ref, _, _, _ = scr[d]
                new = _advance(sps[d], rems[d], [_row(rows[d][2], b, ts[d]) for b in rng],
                               [_row(rows[d][3], b, ts[d]) for b in rng], vks[d], bsz)
                for b in rng:
                    hist_ref[ts[d], b] = sps[d][b]
                    rem_ref[ts[d], b] = rems[d][b]
                    s_ref[b] = new[b]
            return carry

        lax.fori_loop(0, SCAN_CHUNK, fstep, 0, unroll=SCAN_UNROLL)

        def step_of(j, reverse):
            return j if reverse else SCAN_CHUNK - 1 - j

        for d, (reverse, _) in enumerate(SCAN_DIRS):
            t0 = step_of(0, reverse)
            for b in rng:
                scr[d][6][b] = _outer(dy_refs[d][b, t0], rows[d][0][b, t0:t0 + 1, :], pad_v)

        def bstep(j, carry):
            ts = [step_of(j, reverse) for reverse, _ in SCAN_DIRS]
            reads = [[scr[d][6][b] for b in rng] for d in range(2)]
            dss = []
            for d, (_, inclusive) in enumerate(SCAN_DIRS):
                ds = [scr[d][1][b] for b in rng]
                dss.append([ds[b] + reads[d][b] for b in rng] if inclusive else ds)
            drems = [_removed(dss[d], [-_row(rows[d][3], b, ts[d]) for b in rng], ones_v, bsz) for d in range(2)]
            for d, (reverse, _) in enumerate(SCAN_DIRS):
                t_next = step_of(jnp.minimum(j + 1, SCAN_CHUNK - 1), reverse)
                for b in rng:
                    scr[d][6][b] = _outer(dy_refs[d][b, t_next], _row(rows[d][0], b, t_next), pad_v)
                outs[d][5][ts[d]] = _read(dss[d], [_row(rows[d][4], b, ts[d]) for b in rng], head_v)
            for d, (_, inclusive) in enumerate(SCAN_DIRS):
                _, kk_ref, w_ref, _, _ = rows[d]
                _, ds_ref, _, _, dsh_ref, drem_ref, _ = scr[d]
                for b in rng:
                    dsh_ref[ts[d], b] = dss[d][b]
                    drem_ref[ts[d], b] = drems[d][b]
                    dsp = dss[d][b] * _row(w_ref, b, ts[d]) + drems[d][b] * _row(kk_ref, b, ts[d])
                    ds_ref[b] = dsp if inclusive else dsp + reads[d][b]
            return carry

        lax.fori_loop(0, SCAN_CHUNK, bstep, 0, unroll=SCAN_UNROLL)

        rsum = lambda z: jnp.sum(z, axis=0, keepdims=True)
        for d, (reverse, inclusive) in enumerate(SCAN_DIRS):
            dr_ref, dkk_ref, dw_ref, db_ref, dkt_ref, _ = outs[d]
            s_ref, _, hist_ref, rem_ref, dsh_ref, drem_ref, _ = scr[d]
            for t in range(SCAN_CHUNK):
                ts = slice(t, t + 1)
                after = t - 1 if reverse else t + 1
                for b in rng:
                    sp, ds = hist_ref[t, b], dsh_ref[t, b]
                    if not inclusive:
                        seen = sp
                    else:
                        seen = hist_ref[after, b] if 0 <= after < SCAN_CHUNK else s_ref[b]
                    dr_ref[b, ts, :] = _row_from_heads(dy_refs[d][b, t], seen, pad_v)
                    dkt_ref[b, ts, :] = _row_from_heads(v_refs[d][b, t], ds, pad_v)
                    dw_ref[b, ts, :] = rsum(ds * sp)
                    db_ref[b, ts, :] = -rsum(ds * rem_ref[t, b])
                    dkk_ref[b, ts, :] = rsum(sp * drem_ref[t, b])

    specs = [_scan_specs(bsz, backs[d]) for d in range(2)]
    hist = pltpu.VMEM((SCAN_CHUNK, bsz, RW_N, RW_W), f32)
    state = pltpu.VMEM((bsz, RW_N, RW_W), f32)
    start_spec = pl.BlockSpec((None, bsz, RW_N, RW_W), lambda i: (n_ch - 1 - i, 0, 0, 0))
    row_shape = jax.ShapeDtypeStruct((bsz, n_tok, RW_W), f32)
    raw_shape = jax.ShapeDtypeStruct((n_tok, RW_HEADS * bsz, RW_N * bsz), f32)
    return pl.pallas_call(
        body, grid=(n_ch,),
        in_specs=[specs[d][0](col) for d in range(2) for _, col in rows_in[d]]
        + [specs[0][1], specs[1][1]] * 2 + [start_spec, start_spec] + _const_specs(consts),
        out_specs=[spec for d in range(2) for spec in [specs[d][0]()] * 5 + [specs[d][2]]],
        out_shape=([row_shape] * 5 + [raw_shape]) * 2,
        scratch_shapes=[state, state, hist, hist, hist, hist, state] * 2,
        compiler_params=_cparams(("arbitrary",)), name=name)(
        *[a for d in range(2) for a, _ in rows_in[d]], v_heads, v_heads, dy_heads, dy_heads, *starts, *consts)


MOD_NAMES = ("shift1", "scale1", "gate1", "shift2", "scale2", "gate2")


def _rope_tables(t_ctx, t_x):
    quarter = RET_DH // 4
    pos = jnp.arange(t_x)
    inv = jnp.power(ROPE_BASE, -jnp.arange(0, 2 * quarter, 2, dtype=f32) / (2 * quarter))
    ang_r = (pos // GRID_W).astype(f32)[:, None] * inv[None, :]
    ang_c = (pos % GRID_W).astype(f32)[:, None] * inv[None, :]
    cos = jnp.concatenate([jnp.cos(ang_r)] * 2 + [jnp.cos(ang_c)] * 2, axis=1)
    sin = jnp.concatenate([-jnp.sin(ang_r), jnp.sin(ang_r), -jnp.sin(ang_c), jnp.sin(ang_c)], axis=1)
    cos = jnp.concatenate([jnp.ones((t_ctx, RET_DH), f32), cos], axis=0)
    sin = jnp.concatenate([jnp.zeros((t_ctx, RET_DH), f32), sin], axis=0)
    lane = jnp.arange(RET_DH)
    partner = jnp.where(lane % (2 * quarter) < quarter, lane + quarter, lane - quarter)
    perm = (lane[:, None] == partner[None, :]).astype(f32)
    return cos, sin, perm


def _pad_rows(w, lo, total):
    return jnp.pad(w, ((lo, total - lo - w.shape[0]), (0, 0)))


def layer_step(x, ctx, tgt, mod_x, mod_ctx, wt):
    bsz, t_x, _ = x.shape
    t_c = ctx.shape[1]
    t_all = t_c + t_x
    n_ct, n_xt = t_c // TOK_TILE, t_x // TOK_TILE
    n_t = n_ct + n_xt
    assert t_c % TOK_TILE == 0 and t_x % TOK_TILE == 0 and t_c % RET_CHUNK == 0

    seg = lambda i: (i >= n_ct).astype(jnp.int32)
    seg_first = lambda i: jnp.logical_or(i == 0, i == n_ct)
    seg_last = lambda i: jnp.logical_or(i == n_ct - 1, i == n_t - 1)
    mod_all = {n: jnp.stack([jnp.broadcast_to(mod_ctx[k], (bsz, D_MODEL)), mod_x[:, k]], axis=1)[:, :, None, :]
               for k, n in enumerate(MOD_NAMES)}
    mod_lat = {n: mod_x[:, k][:, None, None, :] for k, n in enumerate(MOD_NAMES)}
    both = lambda n: Seg(mod_all[n], seg, seg_first)
    lat = lambda n: Seg(mod_lat[n], lambda i: 0, lambda i: i == 0)
    flat = lambda a: a.reshape(-1, a.shape[-1])
    padc = lambda a: jnp.pad(a, ((0, 0), (t_c, 0), (0, 0)))

    def chunk_orders(n_ctx_chunks, n_chunks):
        fwd = lambda i: i
        bwd = lambda i: jnp.where(i < n_ctx_chunks, n_ctx_chunks - 1 - i, n_chunks + n_ctx_chunks - 1 - i)
        return fwd, bwd

    ones64, ones128 = _block_ones(RW_W, RW_N), _block_ones(RET_W, RET_DH)
    cos, sin, perm = _rope_tables(t_c, t_x)
    ld_rows = [jnp.pad(wt["ret_log_decay"][d][None, :], ((0, 0), (0, RET_DH - RET_HEADS))) for d in range(2)]
    w_up_pad = [_pad_rows(wt["rwkv_w_up"][d], 0, LORA_W) for d in range(2)]
    a_up_pad = [_pad_rows(wt["rwkv_a_up"][d], DECAY_LORA, LORA_W) for d in range(2)]
    g_up_pad = _pad_rows(wt["rwkv_g_up"], DECAY_LORA + AAA_LORA, LORA_W)
    row = lambda a, d: a[d][None, :]

    h = jnp.concatenate([ctx, x], axis=1)
    norm1_ins = lambda: [Tiled(h), both("shift1"), both("scale1"), Glob(wt["norm1_g"])]
    (n1,) = ew_forward(fn_norm_mod, "norm1", bsz, n_t, norm1_ins(), [(D_MODEL, MXU_DTYPE)])
    px = matmul(flat(n1), wt["w_in"], "nn", "proj_in").reshape(bsz, t_all, IN_COLS)
    px_rw = px[..., RET_COLS:]
    ps = token_shift(px_rw, wt["rwkv_shift_mu"], seg_first, seg_last)

    def prep_ins(toff=0):
        return [Tiled(ps, RW_W, 1), Tiled(ps, LORA_W, 3 * RW_W // LORA_W),
                Glob(row(wt["rwkv_w0"], 0)), Glob(row(wt["rwkv_w0"], 1)),
                Glob(row(wt["rwkv_a0"], 0)), Glob(row(wt["rwkv_a0"], 1)),
                Glob(w_up_pad[0]), Glob(w_up_pad[1]), Glob(a_up_pad[0]), Glob(a_up_pad[1]), Glob(g_up_pad),
                Glob(wt["rwkv_k_k"]), Glob(wt["rwkv_k_a"]), Glob(ones64)]

    kk, w_f, b_f, kt_f, w_b, b_b, kt_b, g_rw = ew_forward(fn_rwkv_prepare, "rwkv_prepare", bsz, n_t, prep_ins(),
                                                           [(RW_W, f32)] * 8)
    rw_order = chunk_orders(t_c // SCAN_CHUNK, t_all // SCAN_CHUNK)
    ret_order = chunk_orders(t_c // RET_CHUNK, t_all // RET_CHUNK)
    scan_rows = [[(ps, 0), (kk, 0), (w_f, 0), (b_f, 0), (kt_f, 0)], [(ps, 0), (kk, 0), (w_b, 0), (b_b, 0), (kt_b, 0)]]
    v_heads = heads_to_rows(ps[..., 2 * RW_W:3 * RW_W])
    y_raw_f, y_raw_b, start_f, start_b = rwkv_scan_fwd(scan_rows, v_heads, rw_order, "rwkv_scan_fwd")
    y = [read_out_rows(y_raw_f, bsz), read_out_rows(y_raw_b, bsz)]
    o, ret_states = [], []
    for d in range(2):
        o_d, st_d = retention_fwd(px, cos, sin, perm, ld_rows[d], ret_order[d], SCAN_DIRS[d][0], f"retention_fwd{d}")
        o.append(o_d), ret_states.append(st_d)

    def merge_ins():
        return [Tiled(o[0], toff=n_ct), Tiled(o[1], toff=n_ct), Tiled(px, RET_W, 3, n_ct),
                Tiled(y[0], toff=n_ct), Tiled(y[1], toff=n_ct), Tiled(ps, RW_W, 0, n_ct), Tiled(kt_f, toff=n_ct),
                Tiled(ps, RW_W, 2, n_ct), Tiled(g_rw, toff=n_ct),
                Glob(wt["rwkv_r_k"]), Glob(wt["rwkv_ln_w"]), Glob(wt["rwkv_ln_b"]), Glob(ones64), Glob(ones128)]

    ret_out, rw_out = ew_forward(fn_merge, "merge_heads", bsz, n_xt, merge_ins(), [(RET_W, MXU_DTYPE), (RW_W, MXU_DTYPE)])
    merged = jnp.concatenate([ret_out, rw_out], axis=-1)
    mix = matmul(flat(merged), wt["w_out"], "nn", "proj_out").reshape(bsz, t_x, D_MODEL)
    resid_ins = lambda: [Tiled(x), Tiled(mix), lat("gate1"), lat("shift2"), lat("scale2"), Glob(wt["norm2_g"])]
    h1, n2 = ew_forward(fn_resid_norm_mod, "resid_norm2", bsz, n_xt, resid_ins(), [(D_MODEL, f32), (D_MODEL, MXU_DTYPE)])
    u = matmul(flat(n2), wt["w_ff1"], "nn", "ff1").reshape(bsz, t_x, D_FF)
    relu_ins = lambda: [Tiled(u), Glob(wt["b_ff1"])]
    (act,) = ew_forward(fn_relu2, "relu2", bsz, n_xt, relu_ins(), [(D_FF, MXU_DTYPE)])
    ff = matmul(flat(act), wt["w_ff2"], "nn", "ff2").reshape(bsz, t_x, D_MODEL)

    g = {}
    loss, dh1, dff, dgate2, g["b_ff2"], g["final_g"] = loss_and_grads(
        h1, ff, tgt, mod_lat["gate2"], wt["b_ff2"], wt["final_g"], bsz, n_xt)
    dact = matmul(flat(dff), wt["w_ff2"], "nt", "ff2_dx", MXU_DTYPE).reshape(bsz, t_x, D_FF)
    g["w_ff2"] = matmul(flat(act), flat(dff), "tn", "ff2_dw")
    du, g["b_ff1"] = ew_backward(fn_relu2, "relu2_bwd", bsz, n_xt, relu_ins(), [Tiled(dact)], [True, True],
                                 {0: MXU_DTYPE})
    dn2 = matmul(flat(du), wt["w_ff1"], "nt", "ff1_dx").reshape(bsz, t_x, D_MODEL)
    g["w_ff1"] = matmul(flat(n2), flat(du), "tn", "ff1_dw")
    dx_res, dmix, dgate1, dshift2, dscale2, g["norm2_g"] = ew_backward(
        fn_resid_norm_mod, "resid_norm2_bwd", bsz, n_xt, resid_ins(), [Tiled(dh1), Tiled(dn2)], [True] * 6,
        {1: MXU_DTYPE})
    dmerged = matmul(flat(dmix), wt["w_out"], "nt", "proj_out_dx").reshape(bsz, t_x, D_MODEL)
    g["w_out"] = matmul(flat(merged), flat(dmix), "tn", "proj_out_dw")
    (do, dg_ret, dy, dr_m, dkt_m, dv_m, dg_rw, g["rwkv_r_k"], g["rwkv_ln_w"], g["rwkv_ln_b"]) = ew_backward(
        fn_merge, "merge_heads_bwd", bsz, n_xt, merge_ins(), [Tiled(dmerged, RET_W, 0), Tiled(dmerged, RW_W, 1)],
        [True, False, True, True, False, True, True, True, True, True, True, True, False, False])
    do, dy = padc(do), padc(dy)

    dqkv, dld = [], []
    for d in range(2):
        *dqkv_d, dld_d = retention_bwd(do, px, ret_states[d], cos, sin, perm, ld_rows[d], ret_order[d],
                                       SCAN_DIRS[d][0], f"retention_bwd{d}")
        dqkv.append(dqkv_d), dld.append(dld_d[0, :RET_HEADS])
    g["ret_log_decay"] = jnp.stack(dld)
    (dr_f, dkk_f, dw_f, db_f, dkt_f, dv_raw_f, dr_b, dkk_b, dw_b, db_b, dkt_b, dv_raw_b) = rwkv_scan_bwd(
        scan_rows, v_heads, heads_to_rows(dy), (start_f, start_b), rw_order, "rwkv_scan_bwd")
    dv_f, dv_b = read_out_rows(dv_raw_f, bsz), read_out_rows(dv_raw_b, bsz)
    prep_cts = [dkk_f + dkk_b, dw_f, db_f, dkt_f + padc(dkt_m), dw_b, db_b, dkt_b, padc(dg_rw)]
    (dks, dlora, dw0_f, dw0_b, da0_f, da0_b, dwup_f, dwup_b, daup_f, daup_b, dgup, g["rwkv_k_k"],
     g["rwkv_k_a"]) = ew_backward(fn_rwkv_prepare, "rwkv_prepare_bwd", bsz, n_t, prep_ins(),
                                  [Tiled(c) for c in prep_cts], [True] * 13 + [False])
    g["rwkv_w0"] = jnp.concatenate([dw0_f, dw0_b], axis=0)
    g["rwkv_a0"] = jnp.concatenate([da0_f, da0_b], axis=0)
    g["rwkv_w_up"] = jnp.stack([dwup_f[:DECAY_LORA], dwup_b[:DECAY_LORA]])
    g["rwkv_a_up"] = jnp.stack([daup_f[DECAY_LORA:DECAY_LORA + AAA_LORA], daup_b[DECAY_LORA:DECAY_LORA + AAA_LORA]])
    g["rwkv_g_up"] = dgup[DECAY_LORA + AAA_LORA:]
    dps = jnp.concatenate([dr_f + dr_b + padc(dr_m), dks, dv_f + dv_b + padc(dv_m), dlora], axis=-1)
    dp_rw, g["rwkv_shift_mu"] = token_shift_bwd(dps, px_rw, wt["rwkv_shift_mu"], seg_first, seg_last)
    dpx = jnp.concatenate([(dqkv[0][k] + dqkv[1][k]).astype(MXU_DTYPE) for k in range(3)]
                          + [padc(dg_ret).astype(MXU_DTYPE), dp_rw], axis=-1)
    dn1 = matmul(flat(dpx), wt["w_in"], "nt", "proj_in_dx").reshape(bsz, t_all, D_MODEL)
    g["w_in"] = matmul(flat(n1), flat(dpx), "tn", "proj_in_dw")
    dh, dshift1, dscale1, g["norm1_g"] = ew_backward(fn_norm_mod, "norm1_bwd", bsz, n_t, norm1_ins(), [Tiled(dn1)],
                                                     [True] * 4)
    grad_x = dh[:, t_c:] + dx_res
    zeros = jnp.zeros((D_MODEL,), f32)
    g["mod_x"] = jnp.stack([dshift1[:, 1, 0], dscale1[:, 1, 0], dgate1[:, 0, 0], dshift2[:, 0, 0], dscale2[:, 0, 0],
                            dgate2[:, 0, 0]], axis=1)
    g["mod_ctx"] = jnp.stack([dshift1[:, 0, 0].sum(0), dscale1[:, 0, 0].sum(0), zeros, zeros, zeros, zeros])
    return loss, grad_x, g


MESH_ID = pl.DeviceIdType.MESH
ALL_PEERS = [(dx, dy, dc) for dx in (0, 1) for dy in (0, 1) for dc in (0, 1)][1:]
CHIP_PEERS = [(1, 0, 0), (0, 1, 0), (1, 1, 0)]
CHIP_SLOTS = (0, 2, 4, 6)


def _mesh_pos():
    return lax.axis_index("x"), lax.axis_index("y"), lax.axis_index("c")


def _device_slot():
    x, y, c = _mesh_pos()
    return 4 * x + 2 * y + c


def sibling_swap(arrs, name, pieces=1):
    n = len(arrs)
    assert all(a.shape[0] % pieces == 0 for a in arrs)

    def body(*refs):
        in_refs, out_refs = refs[:n], refs[n:2 * n]
        send_sems, recv_sems = refs[2 * n:]
        x, y, c = _mesh_pos()
        copies = []
        for a in range(n):
            rows = arrs[a].shape[0] // pieces
            for q in range(pieces):
                part = pl.ds(q * rows, rows)
                cp = pltpu.make_async_remote_copy(
                    src_ref=in_refs[a].at[part], dst_ref=out_refs[a].at[part], send_sem=send_sems.at[a * pieces + q],
                    recv_sem=recv_sems.at[a * pieces + q], device_id=(x, y, 1 - c), device_id_type=MESH_ID)
                cp.start()
                copies.append(cp)
        for cp in copies:
            cp.wait()

    any_spec = pl.BlockSpec(memory_space=pl.ANY)
    res = pl.pallas_call(
        body, in_specs=[any_spec] * n, out_specs=[any_spec] * n,
        out_shape=[jax.ShapeDtypeStruct(a.shape, a.dtype) for a in arrs],
        scratch_shapes=[pltpu.SemaphoreType.DMA((n * pieces,)), pltpu.SemaphoreType.DMA((n * pieces,))],
        name=name)(*arrs)
    return list(res)


def exchange(arrs, gather, peers, name, pieces=1, by_chip=False, own=True):
    n, n_peers = len(arrs), len(peers)
    n_slots = N_SHARDS if by_chip else N_DEV
    slot = (lambda x, y, c: 2 * x + y) if by_chip else (lambda x, y, c: 4 * x + 2 * y + c)
    block_rows = [a.shape[0] if gather else a.shape[1] for a in arrs]
    assert all(r % pieces == 0 for r in block_rows), (block_rows, pieces)

    def body(*refs):
        in_refs, out_refs = refs[:n], refs[n:2 * n]
        send_sems, recv_sems, local_sems = refs[2 * n:]
        x, y, c = _mesh_pos()
        me = slot(x, y, c)
        copies, locals_ = [], []
        for a in range(n):
            if own:
                mine = in_refs[a] if gather else in_refs[a].at[me]
                loc = pltpu.make_async_copy(mine, out_refs[a].at[me], local_sems.at[a])
                loc.start()
                locals_.append(loc)
            for k, (dx, dy, dc) in enumerate(peers):
                peer = (1 - x if dx else x, 1 - y if dy else y, 1 - c if dc else c)
                src = in_refs[a] if gather else in_refs[a].at[slot(*peer)]
                for q in range(pieces):
                    part = pl.ds(q * (block_rows[a] // pieces), block_rows[a] // pieces)
                    sem = (a * n_peers + k) * pieces + q
                    cp = pltpu.make_async_remote_copy(
                        src_ref=src.at[part], dst_ref=out_refs[a].at[me, part], send_sem=send_sems.at[sem],
                        recv_sem=recv_sems.at[sem], device_id=peer, device_id_type=MESH_ID)
                    cp.start()
                    copies.append(cp)
        for cp in copies:
            cp.wait()
        for loc in locals_:
            loc.wait()

    any_spec = pl.BlockSpec(memory_space=pl.ANY)
    out_shape = [jax.ShapeDtypeStruct((n_slots,) + (a.shape if gather else a.shape[1:]), a.dtype) for a in arrs]
    n_sems = n * n_peers * pieces
    res = pl.pallas_call(
        body, in_specs=[any_spec] * n, out_specs=[any_spec] * n, out_shape=out_shape,
        scratch_shapes=[pltpu.SemaphoreType.DMA((n_sems,)), pltpu.SemaphoreType.DMA((n_sems,)),
                        pltpu.SemaphoreType.DMA((n,))],
        name=name)(*arrs)
    return list(res)


def add_arrays(parts, name, out_dtype=f32):
    r, c = parts[0].shape
    tr = r
    for cand in (512, 256, 128, 64, 32, 16):
        if r % cand == 0:
            tr = cand
            break

    def body(*refs):
        acc = refs[0][...].astype(f32)
        for p_ref in refs[1:-1]:
            acc = acc + p_ref[...].astype(f32)
        refs[-1][...] = acc.astype(out_dtype)

    spec = pl.BlockSpec((tr, c), lambda i: (i, 0))
    return pl.pallas_call(body, grid=(r // tr,), in_specs=[spec] * len(parts), out_specs=spec,
                          out_shape=jax.ShapeDtypeStruct((r, c), out_dtype),
                          compiler_params=_cparams(("parallel",)), name=name)(*parts)


def gather_two_level(arrs, name):
    n = len(arrs)
    per = 7

    def body(*refs):
        in_refs, out_refs = refs[:n], refs[n:2 * n]
        send_sems, recv_sems = refs[2 * n:]
        x, y, c = _mesh_pos()
        me, sibling = (x, y, c), (x, y, 1 - c)
        chips = [(1 - x, y), (x, 1 - y), (1 - x, 1 - y)]

        def copy(a, k, block, to, src=None):
            rows = out_refs[a].at[4 * block[0] + 2 * block[1] + block[2]]
            return pltpu.make_async_remote_copy(src_ref=rows if src is None else src, dst_ref=rows,
                                                send_sem=send_sems.at[a * per + k], recv_sem=recv_sems.at[a * per + k],
                                                device_id=to, device_id_type=MESH_ID)

        first, passed = [], []
        for a in range(n):
            first.append(copy(a, 0, me, sibling, src=in_refs[a]))
            first += [copy(a, 1 + j, me, (*chip, c), src=in_refs[a]) for j, chip in enumerate(chips)]
        for cp in first:
            cp.start()
        for a in range(n):
            for j, chip in enumerate(chips):
                copy(a, 1 + j, (*chip, c), me).wait_recv()
                fwd = copy(a, 4 + j, (*chip, c), sibling)
                fwd.start()
                passed.append(fwd)
        for a in range(n):
            copy(a, 0, sibling, me).wait_recv()
            for j, chip in enumerate(chips):
                copy(a, 4 + j, (*chip, 1 - c), me).wait_recv()
        for cp in first + passed:
            cp.wait_send()

    any_spec = pl.BlockSpec(memory_space=pl.ANY)
    res = pl.pallas_call(
        body, in_specs=[any_spec] * n, out_specs=[any_spec] * n,
        out_shape=[jax.ShapeDtypeStruct((N_DEV,) + a.shape, a.dtype) for a in arrs],
        scratch_shapes=[pltpu.SemaphoreType.DMA((n * per,)), pltpu.SemaphoreType.DMA((n * per,))],
        name=name)(*arrs)
    return list(res)


def sum_slots(parts, slots, name):
    _, r, c = parts.shape
    tr = r
    for cand in (512, 256, 128, 64, 32, 16, 8):
        if r % cand == 0 and cand * c * 4 * len(slots) <= 8 * 1024 * 1024:
            tr = cand
            break

    def body(p_ref, o_ref):
        acc = p_ref[slots[0]].astype(f32)
        for s in slots[1:]:
            acc = acc + p_ref[s].astype(f32)
        o_ref[...] = acc

    return pl.pallas_call(body, grid=(r // tr,), in_specs=[pl.BlockSpec((parts.shape[0], tr, c), lambda i: (0, i, 0))],
                          out_specs=pl.BlockSpec((tr, c), lambda i: (i, 0)),
                          out_shape=jax.ShapeDtypeStruct((r, c), f32),
                          compiler_params=_cparams(("parallel",)), name=name)(parts)


def column_sum(a, name):
    def body(a_ref, o_ref):
        o_ref[...] = jnp.sum(a_ref[...], axis=0, keepdims=True)

    return pl.pallas_call(body, out_shape=jax.ShapeDtypeStruct((1, a.shape[1]), f32), name=name)(a)


def adamw(w, g, m, v, name):
    r, c = w.shape
    tr = r
    for cand in (256, 128, 64, 32, 16, 8):
        if r % cand == 0:
            tr = cand
            break

    def body(w_ref, g_ref, m_ref, v_ref, d_ref, mo_ref, vo_ref):
        gv = g_ref[...]
        m_new = ADAM_B1 * m_ref[...] + (1.0 - ADAM_B1) * gv
        v_new = ADAM_B2 * v_ref[...] + (1.0 - ADAM_B2) * jnp.square(gv)
        m_hat = m_new / (1.0 - ADAM_B1 ** ADAM_STEP)
        v_hat = v_new / (1.0 - ADAM_B2 ** ADAM_STEP)
        d_ref[...] = -ADAM_LR * (m_hat / (jnp.sqrt(v_hat) + ADAM_EPS) + ADAM_WD * w_ref[...])
        mo_ref[...] = m_new
        vo_ref[...] = v_new

    spec = pl.BlockSpec((tr, c), lambda i: (i, 0))
    return pl.pallas_call(body, grid=(r // tr,), in_specs=[spec] * 4, out_specs=[spec] * 3,
                          out_shape=[jax.ShapeDtypeStruct((r, c), f32)] * 3,
                          compiler_params=_cparams(("parallel",)), name=name)(w, g, m, v)


def adaln_fwd(c_rows, w, b):
    def body(c_ref, w_ref, b_ref, o_ref):
        cv = c_ref[...]
        o_ref[...] = _mxu_dot(cv * jax.nn.sigmoid(cv), w_ref[...]) + b_ref[...]

    return pl.pallas_call(body, out_shape=jax.ShapeDtypeStruct((c_rows.shape[0], w.shape[1]), f32),
                          compiler_params=pltpu.CompilerParams(vmem_limit_bytes=VMEM_LIMIT), name="adaln_fwd")(c_rows, w, b)


def adaln_bwd(c_rows, dm, w):
    def body(c_ref, dm_ref, w_ref, gw_ref, ds_ref):
        cv = c_ref[...]
        gw_ref[...] = _dg(cv * jax.nn.sigmoid(cv), dm_ref[...], 0, 0)
        ds_ref[...] = _dg(dm_ref[...], w_ref[...], 1, 1)

    return pl.pallas_call(body, out_shape=[jax.ShapeDtypeStruct(w.shape, f32),
                                           jax.ShapeDtypeStruct(c_rows.shape, f32)],
                          compiler_params=pltpu.CompilerParams(vmem_limit_bytes=VMEM_LIMIT), name="adaln_bwd")(c_rows, dm, w)


def c_ctx_grad(parts, c_ctx_row):
    def body(p_ref, c_ref, o_ref):
        total = p_ref[0, 0:1, :]
        for s in range(1, N_SHARDS):
            total = total + p_ref[s, 0:1, :]
        _, vjp = jax.vjp(jax.nn.silu, c_ref[...])
        o_ref[...] = vjp(total)[0]

    return pl.pallas_call(body, out_shape=jax.ShapeDtypeStruct((1, D_MODEL), f32), name="c_ctx_grad")(parts, c_ctx_row)


PACK_W = 1024
PACK_ROWS = 8


def _pack(arrs):
    pieces, layout, r0 = [], [], 0
    for a in arrs:
        size = math.prod(a.shape)
        rows = -(-size // (PACK_W * PACK_ROWS)) * PACK_ROWS
        pieces.append(jnp.pad(a.reshape(-1).astype(f32), (0, rows * PACK_W - size)).reshape(rows, PACK_W))
        layout.append((r0, rows, a.shape))
        r0 += rows
    return jnp.concatenate(pieces, axis=0), layout


def _unpack(pack, layout, lead=()):
    n_lead = len(lead)
    outs = []
    for r0, rows, shape in layout:
        piece = pack[(slice(None),) * n_lead + (slice(r0, r0 + rows),)].reshape(lead + (-1,))
        outs.append(piece[..., :math.prod(shape)].reshape(lead + tuple(shape)))
    return outs


W_NAMES = ("c_ctx", "w_ada", "b_ada", "norm1_g", "norm2_g", "w_in", "ret_log_decay", "rwkv_shift_mu", "rwkv_w0",
           "rwkv_w_up", "rwkv_a0", "rwkv_a_up", "rwkv_g_up", "rwkv_k_k", "rwkv_k_a", "rwkv_r_k", "rwkv_ln_w",
           "rwkv_ln_b", "w_out", "w_ff1", "b_ff1", "w_ff2", "b_ff2", "final_g")
COL_SHARDED = ("w_in", "w_ff1")
ROW_SHARDED = ("w_out", "w_ff2")
LAST_SHARDED = ("rwkv_shift_mu", "rwkv_w0", "rwkv_w_up", "rwkv_a0", "rwkv_a_up", "rwkv_g_up")
REPLICATED = ("c_ctx", "b_ada", "norm1_g", "norm2_g", "ret_log_decay", "rwkv_k_k", "rwkv_k_a", "rwkv_r_k",
              "rwkv_ln_w", "rwkv_ln_b", "b_ff1", "b_ff2", "final_g")
N_SHARDS = 4


def _train_step(a):
    x, c, ctx, tgt = a["x"], a["c"], a["ctx"], a["loss_target"]
    bsz = x.shape[0]
    mx, my, mc = _mesh_pos()
    shard = 2 * mx + my
    dev = _device_slot()

    (c_all,) = exchange([jnp.pad(c, ((0, PACK_ROWS - bsz), (0, 0)))], True, ALL_PEERS, "gather_c")
    n_ex = N_DEV * bsz
    c_rows = jnp.concatenate([c_all[:, :bsz].reshape(n_ex, D_MODEL), a["c_ctx"][None, :],
                              jnp.zeros((PACK_ROWS - 1, D_MODEL), f32)], axis=0)
    ada_cols = a["w_ada"].shape[-1]
    b_ada_cols = lax.dynamic_slice_in_dim(a["b_ada"], shard * ada_cols, ada_cols, axis=1)
    mod_cols = adaln_fwd(c_rows, a["w_ada"][0], b_ada_cols)

    halves, small_shards = [], [a[n][0] for n in LAST_SHARDED]
    for n in COL_SHARDED + ROW_SHARDED:
        w = a[n][0].astype(MXU_DTYPE)
        half = w.shape[0] // 2
        halves.append(lax.dynamic_slice_in_dim(w, mc * half, half, axis=0))
    small_pack, small_layout = _pack(small_shards)
    own_blocks = [mod_cols] + halves + [small_pack]
    gathered = [lax.dynamic_update_index_in_dim(got, own, dev, 0)
                for got, own in zip(gather_two_level(own_blocks, "gather_weights"), own_blocks)]
    mod_all = jnp.stack([gathered[0][s] for s in CHIP_SLOTS], axis=1).reshape(c_rows.shape[0], -1)
    mod_x = lax.dynamic_slice_in_dim(mod_all, dev * bsz, bsz, axis=0).reshape(bsz, 6, D_MODEL)
    mod_ctx = mod_all[n_ex].reshape(6, D_MODEL)
    wt = {}
    for n, gth in zip(COL_SHARDED + ROW_SHARDED, gathered[1:5]):
        per_chip = gth.reshape(N_SHARDS, -1, gth.shape[-1])
        wt[n] = (per_chip.transpose(1, 0, 2).reshape(per_chip.shape[1], -1) if n in COL_SHARDED
                 else per_chip.reshape(-1, per_chip.shape[-1]))
    small_by_chip = _unpack(jnp.stack([gathered[5][s] for s in CHIP_SLOTS]), small_layout, (N_SHARDS,))
    for n, parts in zip(LAST_SHARDED, small_by_chip):
        wt[n] = jnp.concatenate([parts[s] for s in range(N_SHARDS)], axis=-1)
    for n in ("norm1_g", "norm2_g", "rwkv_k_k", "rwkv_k_a", "rwkv_r_k", "rwkv_ln_w", "rwkv_ln_b", "b_ff1", "b_ff2"):
        wt[n] = a[n]
    wt["ret_log_decay"] = a["ret_log_decay"][0]
    wt["final_g"] = a["final_g"][None, :]

    loss, grad_x, g = layer_step(x, ctx, tgt, mod_x, mod_ctx, wt)

    small_names = [n for n in REPLICATED if n not in ("c_ctx", "b_ada")]
    g_pack, g_layout = _pack([jnp.pad(loss, ((0, 0), (0, PACK_W - loss.shape[1])))] + [g[n] for n in small_names]
                             + [g["mod_x"], g["mod_ctx"]])
    (g_packs,) = gather_two_level([g_pack], "gather_small_grads")
    g_packs = lax.dynamic_update_index_in_dim(g_packs, g_pack, dev, 0)
    g_sum = _unpack(sum_slots(g_packs, tuple(range(N_DEV)), "sum_small_grads"), g_layout)
    loss_total = g_sum[0][0, 0]
    grads = dict(zip(small_names, g_sum[1:1 + len(small_names)]))
    dmod_ctx = g_sum[-1].reshape(1, -1)
    dmod_x = _unpack(g_packs, g_layout, (N_DEV,))[-2].reshape(n_ex, -1)
    dmod = jnp.concatenate([dmod_x, dmod_ctx, jnp.zeros((PACK_ROWS - 1, dmod_x.shape[1]), f32)], axis=0)
    grads["b_ada"] = column_sum(dmod, "b_ada_grad")
    dmod_cols = lax.dynamic_slice_in_dim(dmod, shard * ada_cols, ada_cols, axis=1)
    grads["w_ada"], dsilu = adaln_bwd(c_rows, dmod_cols, a["w_ada"][0])

    blocks = []
    for n in COL_SHARDED + ROW_SHARDED:
        gw = g[n]
        if n in COL_SHARDED:
            gw = gw.reshape(gw.shape[0], N_SHARDS, -1).transpose(1, 0, 2)
        blocks.append(gw.reshape(N_DEV, -1, gw.shape[-1]).astype(MXU_DTYPE))
    shard_packs = []
    for s in range(N_SHARDS):
        pieces_s = [lax.slice_in_dim(g[n], s * a[n].shape[-1], (s + 1) * a[n].shape[-1], axis=g[n].ndim - 1)
                    for n in LAST_SHARDED]
        pack_s, shard_layout = _pack(pieces_s)
        shard_packs.append(jnp.pad(pack_s, ((0, -pack_s.shape[0] % (2 * PACK_ROWS)), (0, 0))))
    blocks.append(jnp.stack(shard_packs).reshape(N_DEV, -1, PACK_W))
    scattered = COL_SHARDED + ROW_SHARDED + ("small_shards",)
    halves_of = lambda blk, core: lax.dynamic_index_in_dim(
        blk.reshape(N_SHARDS, 2, *blk.shape[1:]), core, axis=1, keepdims=False).reshape(-1, blk.shape[-1])
    from_sibling = sibling_swap([halves_of(blk, 1 - mc) for blk in blocks], "prereduce_swap")
    chip_sums = [add_arrays([halves_of(blk, mc), got], f"prereduce_{n}", blk.dtype).reshape(N_SHARDS, -1, blk.shape[-1])
                 for n, blk, got in zip(scattered, blocks, from_sibling)]
    dsilu_rows = jnp.broadcast_to(jnp.pad(dsilu[n_ex:n_ex + 1], ((0, PACK_ROWS - 1), (0, 0)))[None],
                                  (N_SHARDS, PACK_ROWS, D_MODEL))
    to_chips = [dsilu_rows] + chip_sums
    received = exchange(to_chips, False, CHIP_PEERS, "scatter_big_grads", by_chip=True, own=False)
    received = [lax.dynamic_update_index_in_dim(got, lax.dynamic_index_in_dim(sent, shard, 0, keepdims=False), shard, 0)
                for got, sent in zip(received, to_chips)]
    grads["c_ctx"] = c_ctx_grad(received[0], a["c_ctx"][None, :])
    half_sums = [sum_slots(p, tuple(range(N_SHARDS)), f"sum_{n}") for n, p in zip(scattered, received[1:])]
    other_halves = sibling_swap(half_sums, "swap_halves")
    for n, mine, other in zip(scattered, half_sums, other_halves):
        rows = mine.shape[0]
        whole = jnp.zeros((2 * rows, mine.shape[1]), f32)
        whole = lax.dynamic_update_slice_in_dim(whole, mine, mc * rows, axis=0)
        grads[n] = lax.dynamic_update_slice_in_dim(whole, other, (1 - mc) * rows, axis=0)
    grads.update(zip(LAST_SHARDED, _unpack(grads.pop("small_shards"), shard_layout)))

    out_g, out_d, out_m, out_v = {}, {}, {}, {}
    for n in ("w_ada",) + COL_SHARDED + ROW_SHARDED:
        out_g[n] = grads[n].reshape(a[n].shape)
        two_d = lambda z: z.reshape(-1, z.shape[-1])
        d, m, v = adamw(two_d(a[n]), two_d(out_g[n]), two_d(a["m_" + n]), two_d(a["v_" + n]), f"adamw_{n}")
        out_d[n], out_m[n], out_v[n] = d.reshape(a[n].shape), m.reshape(a[n].shape), v.reshape(a[n].shape)
    rest = REPLICATED + LAST_SHARDED
    for n in rest:
        out_g[n] = grads[n].reshape(a[n].shape)
    packs = [_pack([src[n] for n in rest])[0] for src in
             ({n: a[n] for n in rest}, out_g, {n: a["m_" + n] for n in rest}, {n: a["v_" + n] for n in rest})]
    _, rest_layout = _pack([a[n] for n in rest])
    for dst, pack in zip((out_d, out_m, out_v), adamw(*packs, "adamw_small")):
        dst.update(zip(rest, _unpack(pack, rest_layout)))
    return (loss_total, grad_x, *[out_g[n] for n in W_NAMES], *[out_d[n] for n in W_NAMES],
            *[out_m[n] for n in W_NAMES], *[out_v[n] for n in W_NAMES])


def kernel(x, c, ctx, c_ctx, w_ada, b_ada, norm1_g, norm2_g, w_in, ret_log_decay, rwkv_shift_mu, rwkv_w0, rwkv_w_up, rwkv_a0, rwkv_a_up, rwkv_g_up, rwkv_k_k, rwkv_k_a, rwkv_r_k, rwkv_ln_w, rwkv_ln_b, w_out, w_ff1, b_ff1, w_ff2, b_ff2, final_g, loss_target, m_c_ctx, m_w_ada, m_b_ada, m_norm1_g, m_norm2_g, m_w_in, m_ret_log_decay, m_rwkv_shift_mu, m_rwkv_w0, m_rwkv_w_up, m_rwkv_a0, m_rwkv_a_up, m_rwkv_g_up, m_rwkv_k_k, m_rwkv_k_a, m_rwkv_r_k, m_rwkv_ln_w, m_rwkv_ln_b, m_w_out, m_w_ff1, m_b_ff1, m_w_ff2, m_b_ff2, m_final_g, v_c_ctx, v_w_ada, v_b_ada, v_norm1_g, v_norm2_g, v_w_in, v_ret_log_decay, v_rwkv_shift_mu, v_rwkv_w0, v_rwkv_w_up, v_rwkv_a0, v_rwkv_a_up, v_rwkv_g_up, v_rwkv_k_k, v_rwkv_k_a, v_rwkv_r_k, v_rwkv_ln_w, v_rwkv_ln_b, v_w_out, v_w_ff1, v_b_ff1, v_w_ff2, v_b_ff2, v_final_g):
    return _train_step(dict(locals()))
```

```python
import functools
import math

import jax
import jax.numpy as jnp
from jax import lax
from jax.experimental import pallas as pl
from jax.experimental.pallas import tpu as pltpu

f32 = jnp.float32
MXU_DTYPE = jnp.bfloat16

D_MODEL = 1024
RET_W = 512
RET_HEADS = 4
RET_DH = 128
RET_CHUNK = 128
RW_W = 512
RW_N = 64
DECAY_LORA = 64
AAA_LORA = 64
GATE_LORA = 128
LORA_W = DECAY_LORA + AAA_LORA + GATE_LORA
D_FF = 4096
RET_COLS = 4 * RET_W
SHIFT_COLS = 3 * RW_W + LORA_W
IN_COLS = RET_COLS + SHIFT_COLS
GRID_W = 64
ROPE_BASE = 10000.0
NORM_EPS = 1e-6
GN_EPS = 64e-5
W_DECAY_SCALE = math.exp(-0.5)
ADAM_LR, ADAM_B1, ADAM_B2, ADAM_EPS, ADAM_WD, ADAM_STEP = 0.001, 0.9, 0.999, 1e-08, 0.01, 10

TOK_TILE = 256
MATMUL_TILE = 1024
SCAN_CHUNK = 8
SCAN_UNROLL = 2
N_DEV = 8
V7X_VMEM_BYTES = 64 * 1024 * 1024
VMEM_LIMIT = V7X_VMEM_BYTES * 7 // 8


def _cparams(sem):
    return pltpu.CompilerParams(dimension_semantics=sem, vmem_limit_bytes=VMEM_LIMIT)


def _tile(n, cap):
    best = None
    for t in range(128, min(n, cap) + 1, 128):
        if n % t == 0:
            best = t
    return best if best is not None else n


def matmul(a, b, mode, name, out_dtype=f32, bias=None, finish=None):
    if mode == "nn":
        (m, k), (k2, n) = a.shape, b.shape
    elif mode == "nt":
        (m, k), (n, k2) = a.shape, b.shape
    else:
        (k, m), (k2, n) = a.shape, b.shape
    assert k == k2, (a.shape, b.shape, mode)
    tm, tn, tk = _tile(m, MATMUL_TILE), _tile(n, MATMUL_TILE), _tile(k, MATMUL_TILE)
    nk = k // tk
    dims = {"nn": ((1,), (0,)), "nt": ((1,), (1,)), "tn": ((0,), (0,))}[mode]

    def body(a_ref, b_ref, *rest):
        o_ref, acc_ref = rest[-2:]
        kk = pl.program_id(2)

        @pl.when(kk == 0)
        def _():
            acc_ref[...] = jnp.zeros_like(acc_ref)

        acc_ref[...] += lax.dot_general(a_ref[...].astype(MXU_DTYPE), b_ref[...].astype(MXU_DTYPE),
                                        (dims, ((), ())), preferred_element_type=f32)

        @pl.when(kk == nk - 1)
        def _():
            res = acc_ref[...]
            if bias is not None:
                res = res + rest[0][...]
            if finish is not None:
                res = finish(res)
            o_ref[...] = res.astype(o_ref.dtype)

    if mode == "nn":
        a_spec = pl.BlockSpec((tm, tk), lambda i, j, q: (i, q))
        b_spec = pl.BlockSpec((tk, tn), lambda i, j, q: (q, j))
    elif mode == "nt":
        a_spec = pl.BlockSpec((tm, tk), lambda i, j, q: (i, q))
        b_spec = pl.BlockSpec((tn, tk), lambda i, j, q: (j, q))
    else:
        a_spec = pl.BlockSpec((tk, tm), lambda i, j, q: (q, i))
        b_spec = pl.BlockSpec((tk, tn), lambda i, j, q: (q, j))
    extra_specs = [] if bias is None else [pl.BlockSpec((1, tn), lambda i, j, q: (0, j))]
    extra = [] if bias is None else [bias]
    return pl.pallas_call(
        body, grid=(m // tm, n // tn, nk), in_specs=[a_spec, b_spec] + extra_specs,
        out_specs=pl.BlockSpec((tm, tn), lambda i, j, q: (i, j)),
        out_shape=jax.ShapeDtypeStruct((m, n), out_dtype),
        scratch_shapes=[pltpu.VMEM((tm, tn), f32)],
        compiler_params=_cparams(("parallel", "parallel", "arbitrary")), name=name)(a, b, *extra)


class Tiled:
    def __init__(self, arr, w=None, cidx=0, toff=0):
        self.arr, self.w, self.cidx, self.toff = arr, (arr.shape[-1] if w is None else w), cidx, toff

    def spec(self):
        cidx, toff = self.cidx, self.toff
        return pl.BlockSpec((None, TOK_TILE, self.w), lambda b, i: (b, jnp.maximum(i + toff, 0), cidx))


class Seg:
    def __init__(self, arr, seg, first):
        self.arr, self.seg, self.first = arr, seg, first

    def spec(self):
        seg = self.seg
        return pl.BlockSpec((None, None, 1, self.arr.shape[-1]), lambda b, i: (b, seg(i), 0, 0))


class Glob:
    def __init__(self, arr):
        self.arr = arr

    def spec(self):
        return pl.BlockSpec(self.arr.shape, lambda b, i: (0,) * self.arr.ndim)


def ew_forward(fn, name, bsz, n_tiles, ins, outs):
    n_in = len(ins)

    def body(*refs):
        res = fn(*[r[...] for r in refs[:n_in]])
        for o_ref, o in zip(refs[n_in:], res):
            o_ref[...] = o.astype(o_ref.dtype)

    out_specs = [pl.BlockSpec((None, TOK_TILE, w), lambda b, i: (b, i, 0)) for w, _ in outs]
    out_shape = [jax.ShapeDtypeStruct((bsz, n_tiles * TOK_TILE, w), dt) for w, dt in outs]
    return pl.pallas_call(body, grid=(bsz, n_tiles), in_specs=[d.spec() for d in ins], out_specs=out_specs,
                          out_shape=out_shape, compiler_params=_cparams(("parallel", "parallel")), name=name)(
        *[d.arr for d in ins])


def ew_backward(fn, name, bsz, n_tiles, ins, cts, want, grad_dtypes=None, lead=0):
    n_in, n_ct = len(ins), len(cts)
    diff = [k for k in range(n_in) if want[k]]
    grad_dtypes = grad_dtypes or {}
    assert lead == 0 or not any(isinstance(ins[k], Seg) for k in diff)

    def body(*refs):
        b, i = pl.program_id(0), pl.program_id(1)
        g_refs = refs[n_in + n_ct:]

        def tile_grads():
            vals = [r[...] for r in refs[:n_in]]
            ct_vals = tuple(r[...].astype(f32) for r in refs[n_in:n_in + n_ct])

            def f(*dvals):
                full = list(vals)
                for k, v in zip(diff, dvals):
                    full[k] = v
                return tuple(fn(*full))

            _, vjp = jax.vjp(f, *[vals[k] for k in diff])
            grads = vjp(ct_vals)
            for k, g_ref, g in zip(diff, g_refs, grads):
                d = ins[k]
                if isinstance(d, Tiled):
                    g_ref[...] = g.astype(g_ref.dtype)
                else:
                    zero = d.first(i) if isinstance(d, Seg) else jnp.logical_and(b == 0, i == lead)

                    @pl.when(zero)
                    def _(g_ref=g_ref):
                        g_ref[...] = jnp.zeros_like(g_ref)

                    g_ref[...] += g

        if lead == 0:
            tile_grads()
        else:
            pl.when(i >= lead)(tile_grads)

            @pl.when(i < lead)
            def _():
                for k, g_ref in zip(diff, g_refs):
                    if isinstance(ins[k], Tiled):
                        g_ref[...] = jnp.zeros_like(g_ref)

    out_specs, out_shape = [], []
    for k in diff:
        d = ins[k]
        if isinstance(d, Tiled):
            out_specs.append(pl.BlockSpec((None, TOK_TILE, d.w), lambda b, i: (b, i, 0)))
            out_shape.append(jax.ShapeDtypeStruct((bsz, (n_tiles + lead) * TOK_TILE, d.w), grad_dtypes.get(k, f32)))
        else:
            out_specs.append(d.spec())
            out_shape.append(jax.ShapeDtypeStruct(d.arr.shape, f32))
    return pl.pallas_call(body, grid=(bsz, n_tiles + lead),
                          in_specs=[d.spec() for d in ins] + [c.spec() for c in cts],
                          out_specs=out_specs, out_shape=out_shape,
                          compiler_params=_cparams(("arbitrary", "arbitrary")), name=name)(
        *[d.arr for d in ins], *[c.arr for c in cts])


@jax.custom_vjp
def _mxu_dot(a, b):
    return jnp.dot(a.astype(MXU_DTYPE), b.astype(MXU_DTYPE), preferred_element_type=f32)


def _mxu_dot_fwd(a, b):
    return _mxu_dot(a, b), (a, b)


def _mxu_dot_bwd(res, ct):
    a, b = res
    ct = ct.astype(MXU_DTYPE)
    da = lax.dot_general(ct, b.astype(MXU_DTYPE), (((1,), (1,)), ((), ())), preferred_element_type=f32)
    db = lax.dot_general(a.astype(MXU_DTYPE), ct, (((0,), (0,)), ((), ())), preferred_element_type=f32)
    return da, db


_mxu_dot.defvjp(_mxu_dot_fwd, _mxu_dot_bwd)


def _split_dot_impl(x, ones_mat):
    hi = x.astype(MXU_DTYPE)
    lo = (x - hi.astype(f32)).astype(MXU_DTYPE)
    return jnp.dot(hi, ones_mat, preferred_element_type=f32) + jnp.dot(lo, ones_mat, preferred_element_type=f32)


@jax.custom_vjp
def _split_dot(x, ones_mat):
    return _split_dot_impl(x, ones_mat)


def _split_dot_fwd(x, ones_mat):
    return _split_dot_impl(x, ones_mat), ones_mat


def _split_dot_bwd(ones_mat, ct):
    return _split_dot_impl(ct, ones_mat), None


_split_dot.defvjp(_split_dot_fwd, _split_dot_bwd)


def _block_ones(n, group):
    idx = jnp.arange(n) // group
    return (idx[:, None] == idx[None, :]).astype(MXU_DTYPE)


def _rms(x, g):
    return x * lax.rsqrt(jnp.mean(x * x, axis=-1, keepdims=True) + NORM_EPS) * g


def fn_norm_mod(h, shift, scale, g):
    return (_rms(h, g) * (1.0 + scale) + shift,)


def fn_rwkv_prepare(ks, lora, w0_f, w0_b, a0_f, a0_b, w_up_f, w_up_b, a_up_f, a_up_b, g_up, k_k, k_a, ones64):
    kkr = ks * k_k
    kk = kkr * lax.rsqrt(_split_dot(kkr * kkr, ones64) + 1e-12)
    outs = [kk]
    th = jnp.tanh(lora)
    for w0, a0, w_up, a_up in ((w0_f, a0_f, w_up_f, a_up_f), (w0_b, a0_b, w_up_b, a_up_b)):
        w = jnp.exp(-W_DECAY_SCALE * jax.nn.sigmoid(w0 + _mxu_dot(th, w_up)))
        a = jax.nn.sigmoid(a0 + _mxu_dot(lora, a_up))
        kt = ks * (1.0 + (a - 1.0) * k_a)
        outs += [w, a * kk, kt]
    outs.append(_mxu_dot(jax.nn.sigmoid(lora), g_up))
    return tuple(outs)


def fn_merge(o_f, o_b, g_ret, y_f, y_b, r, kt_f, v, g_rw, r_k, ln_w, ln_b, ones64, ones128):
    o = o_f + o_b
    ret = o * lax.rsqrt(_split_dot(o * o, ones128) * (1.0 / RET_DH) + NORM_EPS) * (g_ret * jax.nn.sigmoid(g_ret))
    y = y_f + y_b
    mean = _split_dot(y, ones64) * (1.0 / RW_N)
    yc = y - mean
    var = _split_dot(yc * yc, ones64) * (1.0 / RW_N)
    y_n = yc * lax.rsqrt(var + GN_EPS) * ln_w + ln_b
    bonus = _split_dot(r * kt_f * r_k, ones64) * v
    return ret, (y_n + bonus) * g_rw


def fn_resid_norm_mod(x, mix, gate, shift, scale, g):
    h1 = x + gate * mix
    return h1, _rms(h1, g) * (1.0 + scale) + shift


def relu2(z):
    return jnp.square(jnp.maximum(z, 0.0))


def relu2_backward(act, dact, name):
    bsz, n_tok, width = act.shape

    def body(a_ref, d_ref, du_ref, db_ref):
        du = d_ref[...].astype(f32) * (2.0 * jnp.sqrt(a_ref[...].astype(f32)))
        du_ref[...] = du.astype(du_ref.dtype)

        @pl.when(jnp.logical_and(pl.program_id(0) == 0, pl.program_id(1) == 0))
        def _():
            db_ref[...] = jnp.zeros_like(db_ref)

        db_ref[...] += jnp.sum(du, axis=0, keepdims=True)

    tile = pl.BlockSpec((None, TOK_TILE, width), lambda b, i: (b, i, 0))
    row = pl.BlockSpec((1, width), lambda b, i: (0, 0))
    return pl.pallas_call(body, grid=(bsz, n_tok // TOK_TILE), in_specs=[tile, tile], out_specs=[tile, row],
                          out_shape=[jax.ShapeDtypeStruct(act.shape, MXU_DTYPE), jax.ShapeDtypeStruct((1, width), f32)],
                          compiler_params=_cparams(("arbitrary", "arbitrary")), name=name)(act, dact)


def fn_loss(h1, f, tgt, gate, b2, g):
    y = _rms(h1 + gate * (f + b2), g)
    err = jnp.square(y - tgt)
    return 0.5 * jnp.sum(jnp.mean(err, axis=-1, keepdims=True), axis=0, keepdims=True)


def loss_and_grads(h1, f, tgt, gate, b2, g, bsz, n_tiles):
    def body(h1_ref, f_ref, t_ref, gate_ref, b2_ref, g_ref, loss_ref, dh1_ref, df_ref, dgate_ref, db2_ref, dg_ref):
        b, i = pl.program_id(0), pl.program_id(1)
        tgt_v = t_ref[...]
        loss, vjp = jax.vjp(lambda a, c, e, p, q: fn_loss(a, c, tgt_v, e, p, q),
                            h1_ref[...], f_ref[...], gate_ref[...], b2_ref[...], g_ref[...])
        dh1, df, dgate, db2, dg = vjp(jnp.ones((1, 1), f32))
        dh1_ref[...] = dh1
        df_ref[...] = df.astype(df_ref.dtype)

        @pl.when(i == 0)
        def _():
            dgate_ref[...] = jnp.zeros_like(dgate_ref)

        @pl.when(jnp.logical_and(b == 0, i == 0))
        def _():
            loss_ref[...] = jnp.zeros_like(loss_ref)
            db2_ref[...] = jnp.zeros_like(db2_ref)
            dg_ref[...] = jnp.zeros_like(dg_ref)

        dgate_ref[...] += dgate
        db2_ref[...] += db2
        dg_ref[...] += dg
        loss_ref[...] += jnp.broadcast_to(loss, loss_ref.shape)

    tile = pl.BlockSpec((None, TOK_TILE, D_MODEL), lambda b, i: (b, i, 0))
    row = pl.BlockSpec((1, D_MODEL), lambda b, i: (0, 0))
    seg = pl.BlockSpec((None, None, 1, D_MODEL), lambda b, i: (b, 0, 0, 0))
    t_tok = n_tiles * TOK_TILE
    return pl.pallas_call(
        body, grid=(bsz, n_tiles), in_specs=[tile, tile, tile, seg, row, row],
        out_specs=[pl.BlockSpec((1, 128), lambda b, i: (0, 0)), tile, tile, seg, row, row],
        out_shape=[jax.ShapeDtypeStruct((1, 128), f32), jax.ShapeDtypeStruct((bsz, t_tok, D_MODEL), f32),
                   jax.ShapeDtypeStruct((bsz, t_tok, D_MODEL), MXU_DTYPE),
                   jax.ShapeDtypeStruct((bsz, 1, 1, D_MODEL), f32),
                   jax.ShapeDtypeStruct((1, D_MODEL), f32), jax.ShapeDtypeStruct((1, D_MODEL), f32)],
        compiler_params=_cparams(("arbitrary", "arbitrary")), name="loss_and_grads")(h1, f, tgt, gate, b2, g)


SHIFT_BLOCK = SHIFT_COLS
HALO_ROWS = 8


def _shift_specs(n_tok, col0):
    per_tile = TOK_TILE // HALO_ROWS
    last = n_tok // HALO_ROWS - 1
    tile = pl.BlockSpec((None, TOK_TILE, SHIFT_BLOCK), lambda j, b, i: (b, i, col0 + j))
    prev = pl.BlockSpec((None, HALO_ROWS, SHIFT_BLOCK),
                        lambda j, b, i: (b, jnp.maximum(i * per_tile - 1, 0), col0 + j))
    nxt = pl.BlockSpec((None, HALO_ROWS, SHIFT_BLOCK),
                       lambda j, b, i: (b, jnp.minimum((i + 1) * per_tile, last), col0 + j))
    return tile, prev, nxt


def _shifted(p, prev_ref, next_ref, is_first, is_last):
    row = lax.broadcasted_iota(jnp.int32, p.shape, 0)
    prev_row = jnp.where(is_first, 0.0, prev_ref[HALO_ROWS - 1:HALO_ROWS, :].astype(f32))
    next_row = jnp.where(is_last, 0.0, next_ref[0:1, :].astype(f32))
    prev = jnp.where(row == 0, prev_row, pltpu.roll(p, 1, axis=0))
    nxt = jnp.where(row == TOK_TILE - 1, next_row, pltpu.roll(p, TOK_TILE - 1, axis=0))
    return prev, nxt


def token_shift(px, mu, seg_first, seg_last):
    bsz, n_tok, _ = px.shape
    n_tiles = n_tok // TOK_TILE

    def body(p_ref, prev_ref, next_ref, mu_ref, o_ref):
        i = pl.program_id(2)
        p = p_ref[...]
        prev, nxt = _shifted(p, prev_ref, next_ref, seg_first(i), seg_last(i))
        o_ref[...] = p + mu_ref[0:1, :] * (prev - p) + mu_ref[1:2, :] * (nxt - p)

    tile, prev, nxt = _shift_specs(n_tok, 0)
    return pl.pallas_call(
        body, grid=(SHIFT_COLS // SHIFT_BLOCK, bsz, n_tiles),
        in_specs=[tile, prev, nxt, pl.BlockSpec((2, SHIFT_BLOCK), lambda j, b, i: (0, j))],
        out_specs=pl.BlockSpec((None, TOK_TILE, SHIFT_BLOCK), lambda j, b, i: (b, i, j)),
        out_shape=jax.ShapeDtypeStruct((bsz, n_tok, SHIFT_COLS), f32),
        compiler_params=_cparams(("parallel", "parallel", "parallel")), name="token_shift")(px, px, px, mu)


def token_shift_bwd(dps, px, mu, seg_first, seg_last):
    bsz, n_tok, _ = px.shape
    n_tiles = n_tok // TOK_TILE

    def body(d_ref, dprev_ref, dnext_ref, p_ref, prev_ref, next_ref, mu_ref, dp_ref, dmu_ref):
        b, i = pl.program_id(1), pl.program_id(2)
        first, last = seg_first(i), seg_last(i)
        d, p = d_ref[...], p_ref[...]
        d_prev, d_next = _shifted(d, dprev_ref, dnext_ref, first, last)
        p_prev, p_next = _shifted(p, prev_ref, next_ref, first, last)
        mu0, mu1 = mu_ref[0:1, :], mu_ref[1:2, :]
        dp_ref[...] = (d + mu0 * (d_next - d) + mu1 * (d_prev - d)).astype(dp_ref.dtype)

        @pl.when(jnp.logical_and(b == 0, i == 0))
        def _():
            dmu_ref[...] = jnp.zeros_like(dmu_ref)

        dmu_ref[0:1, :] += jnp.sum(d * (p_prev - p), axis=0, keepdims=True)
        dmu_ref[1:2, :] += jnp.sum(d * (p_next - p), axis=0, keepdims=True)

    dtile, dprev, dnext = _shift_specs(n_tok, 0)
    tile, prev, nxt = _shift_specs(n_tok, 0)
    mu_spec = pl.BlockSpec((2, SHIFT_BLOCK), lambda j, b, i: (0, j))
    return pl.pallas_call(
        body, grid=(SHIFT_COLS // SHIFT_BLOCK, bsz, n_tiles),
        in_specs=[dtile, dprev, dnext, tile, prev, nxt, mu_spec],
        out_specs=[pl.BlockSpec((None, TOK_TILE, SHIFT_BLOCK), lambda j, b, i: (b, i, j)), mu_spec],
        out_shape=[jax.ShapeDtypeStruct((bsz, n_tok, SHIFT_COLS), MXU_DTYPE),
                   jax.ShapeDtypeStruct((2, SHIFT_COLS), f32)],
        compiler_params=_cparams(("arbitrary", "arbitrary", "arbitrary")), name="token_shift_bwd")(
        dps, dps, dps, px, px, px, mu)


def _dg(a, b, ca, cb):
    return lax.dot_general(a.astype(MXU_DTYPE), b.astype(MXU_DTYPE), (((ca,), (cb,)), ((), ())),
                           preferred_element_type=f32)


@jax.custom_vjp
def _mm_nt(a, b):
    return _dg(a, b, 1, 1)


_mm_nt.defvjp(lambda a, b: (_dg(a, b, 1, 1), (a, b)),
              lambda res, ct: (_dg(ct, res[1], 1, 0), _dg(ct, res[0], 0, 0)))


@jax.custom_vjp
def _mm_tn(a, b):
    return _dg(a, b, 0, 0)


_mm_tn.defvjp(lambda a, b: (_dg(a, b, 0, 0), (a, b)),
              lambda res, ct: (_dg(res[1], ct, 1, 1), _dg(res[0], ct, 1, 0)))


ROTARY_PAIR = RET_DH // 4


def _swap_pairs_impl(t):
    lane = lax.broadcasted_iota(jnp.int32, t.shape, 1)
    return jnp.where(lane % (2 * ROTARY_PAIR) < ROTARY_PAIR, pltpu.roll(t, RET_DH - ROTARY_PAIR, axis=1),
                     pltpu.roll(t, ROTARY_PAIR, axis=1))


@jax.custom_vjp
def _swap_pairs(t):
    return _swap_pairs_impl(t)


_swap_pairs.defvjp(lambda t: (_swap_pairs_impl(t), None), lambda _, ct: (_swap_pairs_impl(ct),))


def _ret_chunk(state, q_raw, k_raw, v, cos, sin, ld_row, head, reverse):
    c = RET_CHUNK
    lane = lax.broadcasted_iota(jnp.int32, ld_row.shape, 1)
    lg = -jnp.exp(jnp.sum(jnp.where(lane == head, ld_row, 0.0), axis=-1, keepdims=True))
    rot = lambda t: t * cos + _swap_pairs(t) * sin
    q = rot(q_raw)
    k = rot(k_raw) * (RET_DH ** -0.5)
    ti = lax.broadcasted_iota(jnp.int32, (c, 1), 0).astype(f32)
    tj = lax.broadcasted_iota(jnp.int32, (1, c), 1).astype(f32)
    if not reverse:
        dist, mask, q_exp, k_exp = ti - tj, (ti - tj) >= 0, ti + 1.0, c - 1.0 - ti
    else:
        dist, mask, q_exp, k_exp = tj - ti, (tj - ti) > 0, c - ti, ti
    decay = jnp.where(mask, jnp.exp(lg * jnp.maximum(dist, 0.0)), 0.0)
    scores = _mm_nt(q, k) * decay
    out = _mxu_dot(scores, v) + _mxu_dot(q * jnp.exp(lg * q_exp), state)
    new_state = state * jnp.exp(lg * c) + _mm_tn(k * jnp.exp(lg * k_exp), v)
    return out, new_state


def _ret_specs(order):
    qkv = [pl.BlockSpec((None, RET_CHUNK, RET_W), functools.partial(lambda b, i, col: (b, order(i), col), col=col))
           for col in range(3)]
    tab = pl.BlockSpec((RET_CHUNK, RET_DH), lambda b, i: (order(i), 0))
    ld = pl.BlockSpec((1, RET_DH), lambda b, i: (0, 0))
    return qkv, tab, ld


def retention_fwd(px, cos, sin, ld_row, order, reverse, name):
    bsz, n_tok, _ = px.shape
    n_ch = n_tok // RET_CHUNK

    def body(q_ref, k_ref, v_ref, cos_ref, sin_ref, ld_ref, o_ref, sv_ref, st_ref):
        @pl.when(pl.program_id(1) == 0)
        def _():
            st_ref[...] = jnp.zeros_like(st_ref)

        for h in range(RET_HEADS):
            sl = slice(h * RET_DH, (h + 1) * RET_DH)
            s = st_ref[h]
            sv_ref[h] = s
            o, s_new = _ret_chunk(s, q_ref[:, sl], k_ref[:, sl], v_ref[:, sl], cos_ref[...], sin_ref[...],
                                  ld_ref[...], h, reverse)
            o_ref[:, sl] = o
            st_ref[h] = s_new

    qkv, tab, ld = _ret_specs(order)
    return pl.pallas_call(
        body, grid=(bsz, n_ch), in_specs=[*qkv, tab, tab, ld],
        out_specs=[pl.BlockSpec((None, RET_CHUNK, RET_W), lambda b, i: (b, order(i), 0)),
                   pl.BlockSpec((None, None, RET_HEADS, RET_DH, RET_DH), lambda b, i: (b, i, 0, 0, 0))],
        out_shape=[jax.ShapeDtypeStruct((bsz, n_tok, RET_W), f32),
                   jax.ShapeDtypeStruct((bsz, n_ch, RET_HEADS, RET_DH, RET_DH), f32)],
        scratch_shapes=[pltpu.VMEM((RET_HEADS, RET_DH, RET_DH), f32)],
        compiler_params=_cparams(("parallel", "arbitrary")), name=name)(px, px, px, cos, sin, ld_row)


def retention_bwd(do, px, states, cos, sin, ld_row, order, reverse, name):
    bsz, n_tok, _ = px.shape
    n_ch = n_tok // RET_CHUNK
    back = lambda i: order(n_ch - 1 - i)

    def body(do_ref, q_ref, k_ref, v_ref, sv_ref, cos_ref, sin_ref, ld_ref,
             dq_ref, dk_ref, dv_ref, dld_ref, dst_ref):
        b, i = pl.program_id(0), pl.program_id(1)

        @pl.when(i == 0)
        def _():
            dst_ref[...] = jnp.zeros_like(dst_ref)

        @pl.when(jnp.logical_and(b == 0, i == 0))
        def _():
            dld_ref[...] = jnp.zeros_like(dld_ref)

        cos_v, sin_v = cos_ref[...], sin_ref[...]
        for h in range(RET_HEADS):
            sl = slice(h * RET_DH, (h + 1) * RET_DH)
            f = lambda s, q, k, v, ld, h=h: _ret_chunk(s, q, k, v, cos_v, sin_v, ld, h, reverse)
            _, vjp = jax.vjp(f, sv_ref[h], q_ref[:, sl], k_ref[:, sl], v_ref[:, sl], ld_ref[...])
            ds, dq, dk, dv, dld = vjp((do_ref[:, sl], dst_ref[h]))
            dst_ref[h] = ds
            dq_ref[:, sl] = dq
            dk_ref[:, sl] = dk
            dv_ref[:, sl] = dv
            dld_ref[...] += dld

    qkv, tab, ld = _ret_specs(back)
    tok = pl.BlockSpec((None, RET_CHUNK, RET_W), lambda b, i: (b, back(i), 0))
    return pl.pallas_call(
        body, grid=(bsz, n_ch),
        in_specs=[tok, *qkv,
                  pl.BlockSpec((None, None, RET_HEADS, RET_DH, RET_DH), lambda b, i: (b, n_ch - 1 - i, 0, 0, 0)),
                  tab, tab, ld],
        out_specs=[tok, tok, tok, ld],
        out_shape=[jax.ShapeDtypeStruct((bsz, n_tok, RET_W), f32)] * 3 + [jax.ShapeDtypeStruct((1, RET_DH), f32)],
        scratch_shapes=[pltpu.VMEM((RET_HEADS, RET_DH, RET_DH), f32)],
        compiler_params=_cparams(("arbitrary", "arbitrary")), name=name)(
        do, px, px, px, states, cos, sin, ld_row)


HALF_W = RW_W // 2


def _head_sum(x, ones):
    xm = x.astype(MXU_DTYPE)
    return jnp.concatenate([jnp.dot(xm[:, :HALF_W], ones, preferred_element_type=f32),
                            jnp.dot(xm[:, HALF_W:], ones, preferred_element_type=f32)], axis=1)


def _stack(parts):
    return jnp.concatenate(parts, axis=0)


def _row(ref, b, t):
    return ref[b, pl.ds(t, 1), :]


SCAN_DIRS = ((False, True), (True, False))
RW_HEADS = RW_W // RW_N
HEAD_ROWS_PAD = 16


def _head_rows(row, mask):
    return jnp.broadcast_to(row, mask.shape) * mask


def _outer(per_value, row, mask_pad):
    return lax.dot_general(per_value.astype(MXU_DTYPE), _head_rows(row, mask_pad).astype(MXU_DTYPE),
                           (((0,), (0,)), ((), ())), preferred_element_type=f32)


def _read(states, rows, mask):
    lhs = _stack([_head_rows(r, mask) for r in rows])
    return lax.dot_general(lhs.astype(MXU_DTYPE), _stack(states).astype(MXU_DTYPE), (((1,), (1,)), ((), ())),
                           preferred_element_type=f32)


def _row_from_heads(per_value, state, mask_pad):
    full = jnp.dot(per_value.astype(MXU_DTYPE), state.astype(MXU_DTYPE), preferred_element_type=f32)
    return jnp.sum(full * mask_pad, axis=0, keepdims=True)


def _scan_specs(bsz, order):
    rows = lambda col=0: pl.BlockSpec((bsz, SCAN_CHUNK, RW_W), lambda i: (0, order(i), col))
    per_value = pl.BlockSpec((bsz, SCAN_CHUNK, HEAD_ROWS_PAD, RW_N), lambda i: (0, order(i), 0, 0))
    raw = pl.BlockSpec((SCAN_CHUNK, RW_HEADS * bsz, RW_N * bsz), lambda i: (order(i), 0, 0))
    return rows, per_value, raw


def _removed(sp, kk_t, ones, bsz):
    removed = _head_sum(_stack([sp[b] * kk_t[b] for b in range(bsz)]), ones)
    return [removed[b * RW_N:(b + 1) * RW_N] for b in range(bsz)]


def _advance(sp, rem, w_t, b_t, vk, bsz):
    return [sp[b] * w_t[b] - rem[b] * b_t[b] + vk[b] for b in range(bsz)]


def heads_to_rows(a):
    b, t, _ = a.shape
    return jnp.pad(a.reshape(b, t, RW_HEADS, RW_N), ((0, 0), (0, 0), (0, HEAD_ROWS_PAD - RW_HEADS), (0, 0)))


def read_out_rows(raw, bsz):
    t = raw.shape[0]
    r5 = raw.reshape(t, bsz, RW_HEADS, bsz, RW_N)
    return jnp.stack([r5[:, b, :, b, :] for b in range(bsz)]).reshape(bsz, t, RW_W)


N_ROWS_FWD = 5
N_ROWS_BWD = 5


def _scan_consts(bsz):
    head = (jnp.arange(RW_W)[None, :] // RW_N == jnp.arange(RW_HEADS)[:, None]).astype(f32)
    return head, jnp.pad(head, ((0, HEAD_ROWS_PAD - RW_HEADS), (0, 0))), _block_ones(HALF_W, RW_N)


def _const_specs(consts):
    return [pl.BlockSpec(c.shape, lambda i: (0, 0)) for c in consts]


def rwkv_scan_fwd(rows_in, v_heads, orders, name):
    bsz, n_tok, _ = rows_in[0][0][0].shape
    n_ch = n_tok // SCAN_CHUNK
    rng = range(bsz)
    consts = _scan_consts(bsz)

    def body(*refs):
        rows = [refs[:N_ROWS_FWD], refs[N_ROWS_FWD:2 * N_ROWS_FWD]]
        v0, v1, head_ref, pad_ref, ones_ref, y0, y1, cs0, cs1, s0, s1, late_ref = refs[2 * N_ROWS_FWD:]
        v_refs, y_refs, cs_refs, s_refs = (v0, v1), (y0, y1), (cs0, cs1), (s0, s1)
        head_v, pad_v, ones_v = head_ref[...], pad_ref[...], ones_ref[...]
        for d in range(2):
            @pl.when(pl.program_id(0) == 0)
            def _(d=d):
                s_refs[d][...] = jnp.zeros_like(s_refs[d])

            cs_refs[d][...] = s_refs[d][...]

        def step(j, carry):
            ts = [SCAN_CHUNK - 1 - j if reverse else j for reverse, _ in SCAN_DIRS]
            sps = [[s_refs[d][b] for b in rng] for d in range(2)]
            rems = [_removed(sps[d], [_row(rows[d][1], b, ts[d]) for b in rng], ones_v, bsz) for d in range(2)]
            vks = [[_outer(v_refs[d][b, ts[d]], _row(rows[d][4], b, ts[d]), pad_v) for b in rng] for d in range(2)]
            for d, (reverse, inclusive) in enumerate(SCAN_DIRS):
                r_ref = rows[d][0]
                if inclusive:
                    before = jnp.maximum(j - 1, 0)
                    late_ref[j] = _read(sps[d], [_row(r_ref, b, before) for b in rng], head_v)
                else:
                    y_refs[d][ts[d]] = _read(sps[d], [_row(r_ref, b, ts[d]) for b in rng], head_v)
            for d in range(2):
                new = _advance(sps[d], rems[d], [_row(rows[d][2], b, ts[d]) for b in rng],
                               [_row(rows[d][3], b, ts[d]) for b in rng], vks[d], bsz)
                for b in rng:
                    s_refs[d][b] = new[b]
            return carry

        lax.fori_loop(0, SCAN_CHUNK, step, 0, unroll=SCAN_UNROLL)
        for d, (reverse, inclusive) in enumerate(SCAN_DIRS):
            if inclusive:
                assert not reverse
                last = SCAN_CHUNK - 1
                late_ref[SCAN_CHUNK] = _read([s_refs[d][b] for b in rng], [rows[d][0][b, last:last + 1, :] for b in rng],
                                             head_v)
                for t in range(SCAN_CHUNK):
                    y_refs[d][t] = late_ref[t + 1]

    specs = [_scan_specs(bsz, orders[d]) for d in range(2)]
    state = pltpu.VMEM((bsz, RW_N, RW_W), f32)
    late = pltpu.VMEM((SCAN_CHUNK + 1, RW_HEADS * bsz, RW_N * bsz), f32)
    start_spec = pl.BlockSpec((None, bsz, RW_N, RW_W), lambda i: (i, 0, 0, 0))
    return pl.pallas_call(
        body, grid=(n_ch,),
        in_specs=[specs[d][0](col) for d in range(2) for _, col in rows_in[d]] + [specs[0][1], specs[1][1]]
        + _const_specs(consts),
        out_specs=[specs[0][2], specs[1][2], start_spec, start_spec],
        out_shape=[jax.ShapeDtypeStruct((n_tok, RW_HEADS * bsz, RW_N * bsz), f32)] * 2
        + [jax.ShapeDtypeStruct((n_ch, bsz, RW_N, RW_W), f32)] * 2,
        scratch_shapes=[state, state, late],
        compiler_params=_cparams(("arbitrary",)), name=name)(
        *[a for d in range(2) for a, _ in rows_in[d]], v_heads, v_heads, *consts)


def rwkv_scan_bwd(rows_in, v_heads, dy_heads, starts, orders, name):
    bsz, n_tok, _ = rows_in[0][0][0].shape
    n_ch = n_tok // SCAN_CHUNK
    backs = [functools.partial(lambda i, order: order(n_ch - 1 - i), order=orders[d]) for d in range(2)]
    rng = range(bsz)
    consts = _scan_consts(bsz)
    n_out, n_scr = 6, 7

    def body(*refs):
        rows = [refs[:N_ROWS_BWD], refs[N_ROWS_BWD:2 * N_ROWS_BWD]]
        rest = refs[2 * N_ROWS_BWD:]
        v_refs, dy_refs, cs_refs, (head_ref, pad_ref, ones_ref) = rest[0:2], rest[2:4], rest[4:6], rest[6:9]
        outs = [rest[9:9 + n_out], rest[9 + n_out:9 + 2 * n_out]]
        scr = [rest[9 + 2 * n_out:9 + 2 * n_out + n_scr], rest[9 + 2 * n_out + n_scr:]]
        head_v, pad_v, ones_v = head_ref[...], pad_ref[...], ones_ref[...]
        for d in range(2):
            s_ref, ds_ref = scr[d][:2]

            @pl.when(pl.program_id(0) == 0)
            def _(ds_ref=ds_ref):
                ds_ref[...] = jnp.zeros_like(ds_ref)

            s_ref[...] = cs_refs[d][...]

        def fstep(j, carry):
            ts = [SCAN_CHUNK - 1 - j if reverse else j for reverse, _ in SCAN_DIRS]
            sps = [[scr[d][0][b] for b in rng] for d in range(2)]
            rems = [_removed(sps[d], [_row(rows[d][1], b, ts[d]) for b in rng], ones_v, bsz) for d in range(2)]
            vks = [[_outer(v_refs[d][b, ts[d]], _row(rows[d][4], b, ts[d]), pad_v) for b in rng] for d in range(2)]
            for d in range(2):
                s_ref, _, hist_ref, rem_ref, _, _, _ = scr[d]
                new = _advance(sps[d], rems[d], [_row(rows[d][2], b, ts[d]) for b in rng],
                               [_row(rows[d][3], b, ts[d]) for b in rng], vks[d], bsz)
                for b in rng:
                    hist_ref[ts[d], b] = sps[d][b]
                    rem_ref[ts[d], b] = rems[d][b]
                    s_ref[b] = new[b]
            return carry

        lax.fori_loop(0, SCAN_CHUNK, fstep, 0, unroll=SCAN_UNROLL)

        def step_of(j, reverse):
            return j if reverse else SCAN_CHUNK - 1 - j

        for d, (reverse, _) in enumerate(SCAN_DIRS):
            t0 = step_of(0, reverse)
            for b in rng:
                scr[d][6][b] = _outer(dy_refs[d][b, t0], rows[d][0][b, t0:t0 + 1, :], pad_v)

        def bstep(j, carry):
            ts = [step_of(j, reverse) for reverse, _ in SCAN_DIRS]
            reads = [[scr[d][6][b] for b in rng] for d in range(2)]
            dss = []
            for d, (_, inclusive) in enumerate(SCAN_DIRS):
                ds = [scr[d][1][b] for b in rng]
                dss.append([ds[b] + reads[d][b] for b in rng] if inclusive else ds)
            drems = [_removed(dss[d], [-_row(rows[d][3], b, ts[d]) for b in rng], ones_v, bsz) for d in range(2)]
            for d, (reverse, _) in enumerate(SCAN_DIRS):
                t_next = step_of(jnp.minimum(j + 1, SCAN_CHUNK - 1), reverse)
                for b in rng:
                    scr[d][6][b] = _outer(dy_refs[d][b, t_next], _row(rows[d][0], b, t_next), pad_v)
                outs[d][5][ts[d]] = _read(dss[d], [_row(rows[d][4], b, ts[d]) for b in rng], head_v)
            for d, (_, inclusive) in enumerate(SCAN_DIRS):
                _, kk_ref, w_ref, _, _ = rows[d]
                _, ds_ref, _, _, dsh_ref, drem_ref, _ = scr[d]
                for b in rng:
                    dsh_ref[ts[d], b] = dss[d][b]
                    drem_ref[ts[d], b] = drems[d][b]
                    dsp = dss[d][b] * _row(w_ref, b, ts[d]) + drems[d][b] * _row(kk_ref, b, ts[d])
                    ds_ref[b] = dsp if inclusive else dsp + reads[d][b]
            return carry

        lax.fori_loop(0, SCAN_CHUNK, bstep, 0, unroll=SCAN_UNROLL)

        rsum = lambda z: jnp.sum(z, axis=0, keepdims=True)
        for d, (reverse, inclusive) in enumerate(SCAN_DIRS):
            dr_ref, dkk_ref, dw_ref, db_ref, dkt_ref, _ = outs[d]
            s_ref, _, hist_ref, rem_ref, dsh_ref, drem_ref, _ = scr[d]
            for t in range(SCAN_CHUNK):
                ts = slice(t, t + 1)
                after = t - 1 if reverse else t + 1
                for b in rng:
                    sp, ds = hist_ref[t, b], dsh_ref[t, b]
                    if not inclusive:
                        seen = sp
                    else:
                        seen = hist_ref[after, b] if 0 <= after < SCAN_CHUNK else s_ref[b]
                    dr_ref[b, ts, :] = _row_from_heads(dy_refs[d][b, t], seen, pad_v)
                    dkt_ref[b, ts, :] = _row_from_heads(v_refs[d][b, t], ds, pad_v)
                    dw_ref[b, ts, :] = rsum(ds * sp)
                    db_ref[b, ts, :] = -rsum(ds * rem_ref[t, b])
                    dkk_ref[b, ts, :] = rsum(sp * drem_ref[t, b])

    specs = [_scan_specs(bsz, backs[d]) for d in range(2)]
    hist = pltpu.VMEM((SCAN_CHUNK, bsz, RW_N, RW_W), f32)
    state = pltpu.VMEM((bsz, RW_N, RW_W), f32)
    start_spec = pl.BlockSpec((None, bsz, RW_N, RW_W), lambda i: (n_ch - 1 - i, 0, 0, 0))
    row_shape = jax.ShapeDtypeStruct((bsz, n_tok, RW_W), f32)
    raw_shape = jax.ShapeDtypeStruct((n_tok, RW_HEADS * bsz, RW_N * bsz), f32)
    return pl.pallas_call(
        body, grid=(n_ch,),
        in_specs=[specs[d][0](col) for d in range(2) for _, col in rows_in[d]]
        + [specs[0][1], specs[1][1]] * 2 + [start_spec, start_spec] + _const_specs(consts),
        out_specs=[spec for d in range(2) for spec in [specs[d][0]()] * 5 + [specs[d][2]]],
        out_shape=([row_shape] * 5 + [raw_shape]) * 2,
        scratch_shapes=[state, state, hist, hist, hist, hist, state] * 2,
        compiler_params=_cparams(("arbitrary",)), name=name)(
        *[a for d in range(2) for a, _ in rows_in[d]], v_heads, v_heads, dy_heads, dy_heads, *starts, *consts)


MOD_NAMES = ("shift1", "scale1", "gate1", "shift2", "scale2", "gate2")


def _rope_tables(t_ctx, t_x):
    quarter = RET_DH // 4
    pos = jnp.arange(t_x)
    inv = jnp.power(ROPE_BASE, -jnp.arange(0, 2 * quarter, 2, dtype=f32) / (2 * quarter))
    ang_r = (pos // GRID_W).astype(f32)[:, None] * inv[None, :]
    ang_c = (pos % GRID_W).astype(f32)[:, None] * inv[None, :]
    cos = jnp.concatenate([jnp.cos(ang_r)] * 2 + [jnp.cos(ang_c)] * 2, axis=1)
    sin = jnp.concatenate([-jnp.sin(ang_r), jnp.sin(ang_r), -jnp.sin(ang_c), jnp.sin(ang_c)], axis=1)
    cos = jnp.concatenate([jnp.ones((t_ctx, RET_DH), f32), cos], axis=0)
    sin = jnp.concatenate([jnp.zeros((t_ctx, RET_DH), f32), sin], axis=0)
    return cos, sin


def _pad_rows(w, lo, total):
    return jnp.pad(w, ((lo, total - lo - w.shape[0]), (0, 0)))


def layer_step(x, ctx, tgt, mod_x, mod_ctx, wt):
    bsz, t_x, _ = x.shape
    t_c = ctx.shape[1]
    t_all = t_c + t_x
    n_ct, n_xt = t_c // TOK_TILE, t_x // TOK_TILE
    n_t = n_ct + n_xt
    assert t_c % TOK_TILE == 0 and t_x % TOK_TILE == 0 and t_c % RET_CHUNK == 0

    seg = lambda i: (i >= n_ct).astype(jnp.int32)
    seg_first = lambda i: jnp.logical_or(i == 0, i == n_ct)
    seg_last = lambda i: jnp.logical_or(i == n_ct - 1, i == n_t - 1)
    mod_all = {n: jnp.stack([jnp.broadcast_to(mod_ctx[k], (bsz, D_MODEL)), mod_x[:, k]], axis=1)[:, :, None, :]
               for k, n in enumerate(MOD_NAMES)}
    mod_lat = {n: mod_x[:, k][:, None, None, :] for k, n in enumerate(MOD_NAMES)}
    both = lambda n: Seg(mod_all[n], seg, seg_first)
    lat = lambda n: Seg(mod_lat[n], lambda i: 0, lambda i: i == 0)
    flat = lambda a: a.reshape(-1, a.shape[-1])

    def chunk_orders(n_ctx_chunks, n_chunks):
        fwd = lambda i: i
        bwd = lambda i: jnp.where(i < n_ctx_chunks, n_ctx_chunks - 1 - i, n_chunks + n_ctx_chunks - 1 - i)
        return fwd, bwd

    ones64, ones128 = _block_ones(RW_W, RW_N), _block_ones(RET_W, RET_DH)
    cos, sin = _rope_tables(t_c, t_x)
    ld_rows = [jnp.pad(wt["ret_log_decay"][d][None, :], ((0, 0), (0, RET_DH - RET_HEADS))) for d in range(2)]
    w_up_pad = [_pad_rows(wt["rwkv_w_up"][d], 0, LORA_W) for d in range(2)]
    a_up_pad = [_pad_rows(wt["rwkv_a_up"][d], DECAY_LORA, LORA_W) for d in range(2)]
    g_up_pad = _pad_rows(wt["rwkv_g_up"], DECAY_LORA + AAA_LORA, LORA_W)
    row = lambda a, d: a[d][None, :]

    h = jnp.concatenate([ctx, x], axis=1)
    norm1_ins = lambda: [Tiled(h), both("shift1"), both("scale1"), Glob(wt["norm1_g"])]
    (n1,) = ew_forward(fn_norm_mod, "norm1", bsz, n_t, norm1_ins(), [(D_MODEL, MXU_DTYPE)])
    px = matmul(flat(n1), wt["w_in"], "nn", "proj_in").reshape(bsz, t_all, IN_COLS)
    px_rw = px[..., RET_COLS:]
    ps = token_shift(px_rw, wt["rwkv_shift_mu"], seg_first, seg_last)

    def prep_ins(toff=0):
        return [Tiled(ps, RW_W, 1), Tiled(ps, LORA_W, 3 * RW_W // LORA_W),
                Glob(row(wt["rwkv_w0"], 0)), Glob(row(wt["rwkv_w0"], 1)),
                Glob(row(wt["rwkv_a0"], 0)), Glob(row(wt["rwkv_a0"], 1)),
                Glob(w_up_pad[0]), Glob(w_up_pad[1]), Glob(a_up_pad[0]), Glob(a_up_pad[1]), Glob(g_up_pad),
                Glob(wt["rwkv_k_k"]), Glob(wt["rwkv_k_a"]), Glob(ones64)]

    kk, w_f, b_f, kt_f, w_b, b_b, kt_b, g_rw = ew_forward(fn_rwkv_prepare, "rwkv_prepare", bsz, n_t, prep_ins(),
                                                           [(RW_W, f32)] * 8)
    rw_order = chunk_orders(t_c // SCAN_CHUNK, t_all // SCAN_CHUNK)
    ret_order = chunk_orders(t_c // RET_CHUNK, t_all // RET_CHUNK)
    scan_rows = [[(ps, 0), (kk, 0), (w_f, 0), (b_f, 0), (kt_f, 0)], [(ps, 0), (kk, 0), (w_b, 0), (b_b, 0), (kt_b, 0)]]
    v_heads = heads_to_rows(ps[..., 2 * RW_W:3 * RW_W])
    y_raw_f, y_raw_b, start_f, start_b = rwkv_scan_fwd(scan_rows, v_heads, rw_order, "rwkv_scan_fwd")
    y = [read_out_rows(y_raw_f, bsz), read_out_rows(y_raw_b, bsz)]
    o, ret_states = [], []
    for d in range(2):
        o_d, st_d = retention_fwd(px, cos, sin, ld_rows[d], ret_order[d], SCAN_DIRS[d][0], f"retention_fwd{d}")
        o.append(o_d), ret_states.append(st_d)

    def merge_ins(toff):
        return [Tiled(o[0], toff=toff), Tiled(o[1], toff=toff), Tiled(px, RET_W, 3, toff),
                Tiled(y[0], toff=toff), Tiled(y[1], toff=toff), Tiled(ps, RW_W, 0, toff), Tiled(kt_f, toff=toff),
                Tiled(ps, RW_W, 2, toff), Tiled(g_rw, toff=toff),
                Glob(wt["rwkv_r_k"]), Glob(wt["rwkv_ln_w"]), Glob(wt["rwkv_ln_b"]), Glob(ones64), Glob(ones128)]

    ret_out, rw_out = ew_forward(fn_merge, "merge_heads", bsz, n_xt, merge_ins(n_ct),
                                 [(RET_W, MXU_DTYPE), (RW_W, MXU_DTYPE)])
    merged = jnp.concatenate([ret_out, rw_out], axis=-1)
    mix = matmul(flat(merged), wt["w_out"], "nn", "proj_out").reshape(bsz, t_x, D_MODEL)
    resid_ins = lambda: [Tiled(x), Tiled(mix), lat("gate1"), lat("shift2"), lat("scale2"), Glob(wt["norm2_g"])]
    h1, n2 = ew_forward(fn_resid_norm_mod, "resid_norm2", bsz, n_xt, resid_ins(), [(D_MODEL, f32), (D_MODEL, MXU_DTYPE)])
    act = matmul(flat(n2), wt["w_ff1"], "nn", "ff1", MXU_DTYPE, wt["b_ff1"], relu2).reshape(bsz, t_x, D_FF)
    ff = matmul(flat(act), wt["w_ff2"], "nn", "ff2").reshape(bsz, t_x, D_MODEL)

    g = {}
    loss, dh1, dff, dgate2, g["b_ff2"], g["final_g"] = loss_and_grads(
        h1, ff, tgt, mod_lat["gate2"], wt["b_ff2"], wt["final_g"], bsz, n_xt)
    dact = matmul(flat(dff), wt["w_ff2"], "nt", "ff2_dx", MXU_DTYPE).reshape(bsz, t_x, D_FF)
    g["w_ff2"] = matmul(flat(act), flat(dff), "tn", "ff2_dw")
    du, g["b_ff1"] = relu2_backward(act, dact, "relu2_bwd")
    dn2 = matmul(flat(du), wt["w_ff1"], "nt", "ff1_dx").reshape(bsz, t_x, D_MODEL)
    g["w_ff1"] = matmul(flat(n2), flat(du), "tn", "ff1_dw")
    dx_res, dmix, dgate1, dshift2, dscale2, g["norm2_g"] = ew_backward(
        fn_resid_norm_mod, "resid_norm2_bwd", bsz, n_xt, resid_ins(), [Tiled(dh1), Tiled(dn2)], [True] * 6,
        {1: MXU_DTYPE})
    dmerged = matmul(flat(dmix), wt["w_out"], "nt", "proj_out_dx").reshape(bsz, t_x, D_MODEL)
    g["w_out"] = matmul(flat(merged), flat(dmix), "tn", "proj_out_dw")
    (do, dg_ret, dy, dr_m, dkt_m, dv_m, dg_rw, g["rwkv_r_k"], g["rwkv_ln_w"], g["rwkv_ln_b"]) = ew_backward(
        fn_merge, "merge_heads_bwd", bsz, n_xt, merge_ins(0),
        [Tiled(dmerged, RET_W, 0, -n_ct), Tiled(dmerged, RW_W, 1, -n_ct)],
        [True, False, True, True, False, True, True, True, True, True, True, True, False, False], lead=n_ct)

    dqkv, dld = [], []
    for d in range(2):
        *dqkv_d, dld_d = retention_bwd(do, px, ret_states[d], cos, sin, ld_rows[d], ret_order[d],
                                       SCAN_DIRS[d][0], f"retention_bwd{d}")
        dqkv.append(dqkv_d), dld.append(dld_d[0, :RET_HEADS])
    g["ret_log_decay"] = jnp.stack(dld)
    (dr_f, dkk_f, dw_f, db_f, dkt_f, dv_raw_f, dr_b, dkk_b, dw_b, db_b, dkt_b, dv_raw_b) = rwkv_scan_bwd(
        scan_rows, v_heads, heads_to_rows(dy), (start_f, start_b), rw_order, "rwkv_scan_bwd")
    dv_f, dv_b = read_out_rows(dv_raw_f, bsz), read_out_rows(dv_raw_b, bsz)
    prep_cts = [dkk_f + dkk_b, dw_f, db_f, dkt_f + dkt_m, dw_b, db_b, dkt_b, dg_rw]
    (dks, dlora, dw0_f, dw0_b, da0_f, da0_b, dwup_f, dwup_b, daup_f, daup_b, dgup, g["rwkv_k_k"],
     g["rwkv_k_a"]) = ew_backward(fn_rwkv_prepare, "rwkv_prepare_bwd", bsz, n_t, prep_ins(),
                                  [Tiled(c) for c in prep_cts], [True] * 13 + [False])
    g["rwkv_w0"] = jnp.concatenate([dw0_f, dw0_b], axis=0)
    g["rwkv_a0"] = jnp.concatenate([da0_f, da0_b], axis=0)
    g["rwkv_w_up"] = jnp.stack([dwup_f[:DECAY_LORA], dwup_b[:DECAY_LORA]])
    g["rwkv_a_up"] = jnp.stack([daup_f[DECAY_LORA:DECAY_LORA + AAA_LORA], daup_b[DECAY_LORA:DECAY_LORA + AAA_LORA]])
    g["rwkv_g_up"] = dgup[DECAY_LORA + AAA_LORA:]
    dps = jnp.concatenate([dr_f + dr_b + dr_m, dks, dv_f + dv_b + dv_m, dlora], axis=-1)
    dp_rw, g["rwkv_shift_mu"] = token_shift_bwd(dps, px_rw, wt["rwkv_shift_mu"], seg_first, seg_last)
    dpx = jnp.concatenate([(dqkv[0][k] + dqkv[1][k]).astype(MXU_DTYPE) for k in range(3)]
                          + [dg_ret.astype(MXU_DTYPE), dp_rw], axis=-1)
    dn1 = matmul(flat(dpx), wt["w_in"], "nt", "proj_in_dx").reshape(bsz, t_all, D_MODEL)
    g["w_in"] = matmul(flat(n1), flat(dpx), "tn", "proj_in_dw")
    dh, dshift1, dscale1, g["norm1_g"] = ew_backward(fn_norm_mod, "norm1_bwd", bsz, n_t, norm1_ins(), [Tiled(dn1)],
                                                     [True] * 4)
    grad_x = dh[:, t_c:] + dx_res
    zeros = jnp.zeros((D_MODEL,), f32)
    g["mod_x"] = jnp.stack([dshift1[:, 1, 0], dscale1[:, 1, 0], dgate1[:, 0, 0], dshift2[:, 0, 0], dscale2[:, 0, 0],
                            dgate2[:, 0, 0]], axis=1)
    g["mod_ctx"] = jnp.stack([dshift1[:, 0, 0].sum(0), dscale1[:, 0, 0].sum(0), zeros, zeros, zeros, zeros])
    return loss, grad_x, g


MESH_ID = pl.DeviceIdType.MESH
ALL_PEERS = [(dx, dy, dc) for dx in (0, 1) for dy in (0, 1) for dc in (0, 1)][1:]
CHIP_PEERS = [(1, 0, 0), (0, 1, 0), (1, 1, 0)]
CHIP_SLOTS = (0, 2, 4, 6)


def _mesh_pos():
    return lax.axis_index("x"), lax.axis_index("y"), lax.axis_index("c")


def _device_slot():
    x, y, c = _mesh_pos()
    return 4 * x + 2 * y + c


def sibling_swap(arrs, name, pieces=1):
    n = len(arrs)
    assert all(a.shape[0] % pieces == 0 for a in arrs)

    def body(*refs):
        in_refs, out_refs = refs[:n], refs[n:2 * n]
        send_sems, recv_sems = refs[2 * n:]
        x, y, c = _mesh_pos()
        copies = []
        for a in range(n):
            rows = arrs[a].shape[0] // pieces
            for q in range(pieces):
                part = pl.ds(q * rows, rows)
                cp = pltpu.make_async_remote_copy(
                    src_ref=in_refs[a].at[part], dst_ref=out_refs[a].at[part], send_sem=send_sems.at[a * pieces + q],
                    recv_sem=recv_sems.at[a * pieces + q], device_id=(x, y, 1 - c), device_id_type=MESH_ID)
                cp.start()
                copies.append(cp)
        for cp in copies:
            cp.wait()

    any_spec = pl.BlockSpec(memory_space=pl.ANY)
    res = pl.pallas_call(
        body, in_specs=[any_spec] * n, out_specs=[any_spec] * n,
        out_shape=[jax.ShapeDtypeStruct(a.shape, a.dtype) for a in arrs],
        scratch_shapes=[pltpu.SemaphoreType.DMA((n * pieces,)), pltpu.SemaphoreType.DMA((n * pieces,))],
        name=name)(*arrs)
    return list(res)


def exchange(arrs, gather, peers, name, pieces=1, by_chip=False, own=True):
    n, n_peers = len(arrs), len(peers)
    n_slots = N_SHARDS if by_chip else N_DEV
    slot = (lambda x, y, c: 2 * x + y) if by_chip else (lambda x, y, c: 4 * x + 2 * y + c)
    block_rows = [a.shape[0] if gather else a.shape[1] for a in arrs]
    assert all(r % pieces == 0 for r in block_rows), (block_rows, pieces)

    def body(*refs):
        in_refs, out_refs = refs[:n], refs[n:2 * n]
        send_sems, recv_sems, local_sems = refs[2 * n:]
        x, y, c = _mesh_pos()
        me = slot(x, y, c)
        copies, locals_ = [], []
        for a in range(n):
            if own:
                mine = in_refs[a] if gather else in_refs[a].at[me]
                loc = pltpu.make_async_copy(mine, out_refs[a].at[me], local_sems.at[a])
                loc.start()
                locals_.append(loc)
            for k, (dx, dy, dc) in enumerate(peers):
                peer = (1 - x if dx else x, 1 - y if dy else y, 1 - c if dc else c)
                src = in_refs[a] if gather else in_refs[a].at[slot(*peer)]
                for q in range(pieces):
                    part = pl.ds(q * (block_rows[a] // pieces), block_rows[a] // pieces)
                    sem = (a * n_peers + k) * pieces + q
                    cp = pltpu.make_async_remote_copy(
                        src_ref=src.at[part], dst_ref=out_refs[a].at[me, part], send_sem=send_sems.at[sem],
                        recv_sem=recv_sems.at[sem], device_id=peer, device_id_type=MESH_ID)
                    cp.start()
                    copies.append(cp)
        for cp in copies:
            cp.wait()
        for loc in locals_:
            loc.wait()

    any_spec = pl.BlockSpec(memory_space=pl.ANY)
    out_shape = [jax.ShapeDtypeStruct((n_slots,) + (a.shape if gather else a.shape[1:]), a.dtype) for a in arrs]
    n_sems = n * n_peers * pieces
    res = pl.pallas_call(
        body, in_specs=[any_spec] * n, out_specs=[any_spec] * n, out_shape=out_shape,
        scratch_shapes=[pltpu.SemaphoreType.DMA((n_sems,)), pltpu.SemaphoreType.DMA((n_sems,)),
                        pltpu.SemaphoreType.DMA((n,))],
        name=name)(*arrs)
    return list(res)


def add_arrays(parts, name, out_dtype=f32):
    r, c = parts[0].shape
    tr = r
    for cand in (512, 256, 128, 64, 32, 16):
        if r % cand == 0:
            tr = cand
            break

    def body(*refs):
        acc = refs[0][...].astype(f32)
        for p_ref in refs[1:-1]:
            acc = acc + p_ref[...].astype(f32)
        refs[-1][...] = acc.astype(out_dtype)

    spec = pl.BlockSpec((tr, c), lambda i: (i, 0))
    return pl.pallas_call(body, grid=(r // tr,), in_specs=[spec] * len(parts), out_specs=spec,
                          out_shape=jax.ShapeDtypeStruct((r, c), out_dtype),
                          compiler_params=_cparams(("parallel",)), name=name)(*parts)


def gather_two_level(arrs, name):
    n = len(arrs)
    per = 7

    def body(*refs):
        in_refs, out_refs = refs[:n], refs[n:2 * n]
        send_sems, recv_sems = refs[2 * n:]
        x, y, c = _mesh_pos()
        me, sibling = (x, y, c), (x, y, 1 - c)
        chips = [(1 - x, y), (x, 1 - y), (1 - x, 1 - y)]

        def copy(a, k, block, to, src=None):
            rows = out_refs[a].at[4 * block[0] + 2 * block[1] + block[2]]
            return pltpu.make_async_remote_copy(src_ref=rows if src is None else src, dst_ref=rows,
                                                send_sem=send_sems.at[a * per + k], recv_sem=recv_sems.at[a * per + k],
                                                device_id=to, device_id_type=MESH_ID)

        first, passed = [], []
        for a in range(n):
            first.append(copy(a, 0, me, sibling, src=in_refs[a]))
            first += [copy(a, 1 + j, me, (*chip, c), src=in_refs[a]) for j, chip in enumerate(chips)]
        for cp in first:
            cp.start()
        for a in range(n):
            for j, chip in enumerate(chips):
                copy(a, 1 + j, (*chip, c), me).wait_recv()
                fwd = copy(a, 4 + j, (*chip, c), sibling)
                fwd.start()
                passed.append(fwd)
        for a in range(n):
            copy(a, 0, sibling, me).wait_recv()
            for j, chip in enumerate(chips):
                copy(a, 4 + j, (*chip, 1 - c), me).wait_recv()
        for cp in first + passed:
            cp.wait_send()

    any_spec = pl.BlockSpec(memory_space=pl.ANY)
    res = pl.pallas_call(
        body, in_specs=[any_spec] * n, out_specs=[any_spec] * n,
        out_shape=[jax.ShapeDtypeStruct((N_DEV,) + a.shape, a.dtype) for a in arrs],
        scratch_shapes=[pltpu.SemaphoreType.DMA((n * per,)), pltpu.SemaphoreType.DMA((n * per,))],
        name=name)(*arrs)
    return list(res)


def sum_slots(parts, slots, name):
    _, r, c = parts.shape
    tr = r
    for cand in (512, 256, 128, 64, 32, 16, 8):
        if r % cand == 0 and cand * c * 4 * len(slots) <= 8 * 1024 * 1024:
            tr = cand
            break

    def body(p_ref, o_ref):
        acc = p_ref[slots[0]].astype(f32)
        for s in slots[1:]:
            acc = acc + p_ref[s].astype(f32)
        o_ref[...] = acc

    return pl.pallas_call(body, grid=(r // tr,), in_specs=[pl.BlockSpec((parts.shape[0], tr, c), lambda i: (0, i, 0))],
                          out_specs=pl.BlockSpec((tr, c), lambda i: (i, 0)),
                          out_shape=jax.ShapeDtypeStruct((r, c), f32),
                          compiler_params=_cparams(("parallel",)), name=name)(parts)


def column_sum(a, name):
    def body(a_ref, o_ref):
        o_ref[...] = jnp.sum(a_ref[...], axis=0, keepdims=True)

    return pl.pallas_call(body, out_shape=jax.ShapeDtypeStruct((1, a.shape[1]), f32), name=name)(a)


def adamw(w, g, m, v, name):
    r, c = w.shape
    tr = r
    for cand in (256, 128, 64, 32, 16, 8):
        if r % cand == 0:
            tr = cand
            break

    def body(w_ref, g_ref, m_ref, v_ref, d_ref, mo_ref, vo_ref):
        gv = g_ref[...]
        m_new = ADAM_B1 * m_ref[...] + (1.0 - ADAM_B1) * gv
        v_new = ADAM_B2 * v_ref[...] + (1.0 - ADAM_B2) * jnp.square(gv)
        m_hat = m_new / (1.0 - ADAM_B1 ** ADAM_STEP)
        v_hat = v_new / (1.0 - ADAM_B2 ** ADAM_STEP)
        d_ref[...] = -ADAM_LR * (m_hat / (jnp.sqrt(v_hat) + ADAM_EPS) + ADAM_WD * w_ref[...])
        mo_ref[...] = m_new
        vo_ref[...] = v_new

    spec = pl.BlockSpec((tr, c), lambda i: (i, 0))
    return pl.pallas_call(body, grid=(r // tr,), in_specs=[spec] * 4, out_specs=[spec] * 3,
                          out_shape=[jax.ShapeDtypeStruct((r, c), f32)] * 3,
                          compiler_params=_cparams(("parallel",)), name=name)(w, g, m, v)


def adaln_fwd(c_rows, w, b):
    def body(c_ref, w_ref, b_ref, o_ref):
        cv = c_ref[...]
        o_ref[...] = _mxu_dot(cv * jax.nn.sigmoid(cv), w_ref[...]) + b_ref[...]

    return pl.pallas_call(body, out_shape=jax.ShapeDtypeStruct((c_rows.shape[0], w.shape[1]), f32),
                          compiler_params=pltpu.CompilerParams(vmem_limit_bytes=VMEM_LIMIT), name="adaln_fwd")(c_rows, w, b)


def adaln_bwd(c_rows, dm, w):
    def body(c_ref, dm_ref, w_ref, gw_ref, ds_ref):
        cv = c_ref[...]
        gw_ref[...] = _dg(cv * jax.nn.sigmoid(cv), dm_ref[...], 0, 0)
        ds_ref[...] = _dg(dm_ref[...], w_ref[...], 1, 1)

    return pl.pallas_call(body, out_shape=[jax.ShapeDtypeStruct(w.shape, f32),
                                           jax.ShapeDtypeStruct(c_rows.shape, f32)],
                          compiler_params=pltpu.CompilerParams(vmem_limit_bytes=VMEM_LIMIT), name="adaln_bwd")(c_rows, dm, w)


def c_ctx_grad(parts, c_ctx_row):
    def body(p_ref, c_ref, o_ref):
        total = p_ref[0, 0:1, :]
        for s in range(1, N_SHARDS):
            total = total + p_ref[s, 0:1, :]
        _, vjp = jax.vjp(jax.nn.silu, c_ref[...])
        o_ref[...] = vjp(total)[0]

    return pl.pallas_call(body, out_shape=jax.ShapeDtypeStruct((1, D_MODEL), f32), name="c_ctx_grad")(parts, c_ctx_row)


PACK_W = 1024
PACK_ROWS = 8


def _pack(arrs):
    pieces, layout, r0 = [], [], 0
    for a in arrs:
        size = math.prod(a.shape)
        rows = -(-size // (PACK_W * PACK_ROWS)) * PACK_ROWS
        pieces.append(jnp.pad(a.reshape(-1).astype(f32), (0, rows * PACK_W - size)).reshape(rows, PACK_W))
        layout.append((r0, rows, a.shape))
        r0 += rows
    return jnp.concatenate(pieces, axis=0), layout


def _unpack(pack, layout, lead=()):
    n_lead = len(lead)
    outs = []
    for r0, rows, shape in layout:
        piece = pack[(slice(None),) * n_lead + (slice(r0, r0 + rows),)].reshape(lead + (-1,))
        outs.append(piece[..., :math.prod(shape)].reshape(lead + tuple(shape)))
    return outs


W_NAMES = ("c_ctx", "w_ada", "b_ada", "norm1_g", "norm2_g", "w_in", "ret_log_decay", "rwkv_shift_mu", "rwkv_w0",
           "rwkv_w_up", "rwkv_a0", "rwkv_a_up", "rwkv_g_up", "rwkv_k_k", "rwkv_k_a", "rwkv_r_k", "rwkv_ln_w",
           "rwkv_ln_b", "w_out", "w_ff1", "b_ff1", "w_ff2", "b_ff2", "final_g")
COL_SHARDED = ("w_in", "w_ff1")
ROW_SHARDED = ("w_out", "w_ff2")
LAST_SHARDED = ("rwkv_shift_mu", "rwkv_w0", "rwkv_w_up", "rwkv_a0", "rwkv_a_up", "rwkv_g_up")
REPLICATED = ("c_ctx", "b_ada", "norm1_g", "norm2_g", "ret_log_decay", "rwkv_k_k", "rwkv_k_a", "rwkv_r_k",
              "rwkv_ln_w", "rwkv_ln_b", "b_ff1", "b_ff2", "final_g")
N_SHARDS = 4


def _train_step(a):
    x, c, ctx, tgt = a["x"], a["c"], a["ctx"], a["loss_target"]
    bsz = x.shape[0]
    mx, my, mc = _mesh_pos()
    shard = 2 * mx + my
    dev = _device_slot()

    (c_all,) = exchange([jnp.pad(c, ((0, PACK_ROWS - bsz), (0, 0)))], True, ALL_PEERS, "gather_c")
    n_ex = N_DEV * bsz
    c_rows = jnp.concatenate([c_all[:, :bsz].reshape(n_ex, D_MODEL), a["c_ctx"][None, :],
                              jnp.zeros((PACK_ROWS - 1, D_MODEL), f32)], axis=0)
    ada_cols = a["w_ada"].shape[-1]
    b_ada_cols = lax.dynamic_slice_in_dim(a["b_ada"], shard * ada_cols, ada_cols, axis=1)
    mod_cols = adaln_fwd(c_rows, a["w_ada"][0], b_ada_cols)

    halves, small_shards = [], [a[n][0] for n in LAST_SHARDED]
    for n in COL_SHARDED + ROW_SHARDED:
        w = a[n][0].astype(MXU_DTYPE)
        half = w.shape[0] // 2
        halves.append(lax.dynamic_slice_in_dim(w, mc * half, half, axis=0))
    small_pack, small_layout = _pack(small_shards)
    own_blocks = [mod_cols] + halves + [small_pack]
    gathered = [lax.dynamic_update_index_in_dim(got, own, dev, 0)
                for got, own in zip(gather_two_level(own_blocks, "gather_weights"), own_blocks)]
    mod_all = jnp.stack([gathered[0][s] for s in CHIP_SLOTS], axis=1).reshape(c_rows.shape[0], -1)
    mod_x = lax.dynamic_slice_in_dim(mod_all, dev * bsz, bsz, axis=0).reshape(bsz, 6, D_MODEL)
    mod_ctx = mod_all[n_ex].reshape(6, D_MODEL)
    wt = {}
    for n, gth in zip(COL_SHARDED + ROW_SHARDED, gathered[1:5]):
        per_chip = gth.reshape(N_SHARDS, -1, gth.shape[-1])
        wt[n] = (per_chip.transpose(1, 0, 2).reshape(per_chip.shape[1], -1) if n in COL_SHARDED
                 else per_chip.reshape(-1, per_chip.shape[-1]))
    small_by_chip = _unpack(jnp.stack([gathered[5][s] for s in CHIP_SLOTS]), small_layout, (N_SHARDS,))
    for n, parts in zip(LAST_SHARDED, small_by_chip):
        wt[n] = jnp.concatenate([parts[s] for s in range(N_SHARDS)], axis=-1)
    for n in ("norm1_g", "norm2_g", "rwkv_k_k", "rwkv_k_a", "rwkv_r_k", "rwkv_ln_w", "rwkv_ln_b", "b_ff1", "b_ff2"):
        wt[n] = a[n]
    wt["ret_log_decay"] = a["ret_log_decay"][0]
    wt["final_g"] = a["final_g"][None, :]

    loss, grad_x, g = layer_step(x, ctx, tgt, mod_x, mod_ctx, wt)

    small_names = [n for n in REPLICATED if n not in ("c_ctx", "b_ada")]
    g_pack, g_layout = _pack([jnp.pad(loss, ((0, 0), (0, PACK_W - loss.shape[1])))] + [g[n] for n in small_names]
                             + [g["mod_x"], g["mod_ctx"]])
    (g_packs,) = gather_two_level([g_pack], "gather_small_grads")
    g_packs = lax.dynamic_update_index_in_dim(g_packs, g_pack, dev, 0)
    g_sum = _unpack(sum_slots(g_packs, tuple(range(N_DEV)), "sum_small_grads"), g_layout)
    loss_total = g_sum[0][0, 0]
    grads = dict(zip(small_names, g_sum[1:1 + len(small_names)]))
    dmod_ctx = g_sum[-1].reshape(1, -1)
    dmod_x = _unpack(g_packs, g_layout, (N_DEV,))[-2].reshape(n_ex, -1)
    dmod = jnp.concatenate([dmod_x, dmod_ctx, jnp.zeros((PACK_ROWS - 1, dmod_x.shape[1]), f32)], axis=0)
    grads["b_ada"] = column_sum(dmod, "b_ada_grad")
    dmod_cols = lax.dynamic_slice_in_dim(dmod, shard * ada_cols, ada_cols, axis=1)
    grads["w_ada"], dsilu = adaln_bwd(c_rows, dmod_cols, a["w_ada"][0])

    blocks = []
    for n in COL_SHARDED + ROW_SHARDED:
        gw = g[n]
        if n in COL_SHARDED:
            gw = gw.reshape(gw.shape[0], N_SHARDS, -1).transpose(1, 0, 2)
        blocks.append(gw.reshape(N_DEV, -1, gw.shape[-1]).astype(MXU_DTYPE))
    shard_packs = []
    for s in range(N_SHARDS):
        pieces_s = [lax.slice_in_dim(g[n], s * a[n].shape[-1], (s + 1) * a[n].shape[-1], axis=g[n].ndim - 1)
                    for n in LAST_SHARDED]
        pack_s, shard_layout = _pack(pieces_s)
        shard_packs.append(jnp.pad(pack_s, ((0, -pack_s.shape[0] % (2 * PACK_ROWS)), (0, 0))))
    blocks.append(jnp.stack(shard_packs).reshape(N_DEV, -1, PACK_W))
    scattered = COL_SHARDED + ROW_SHARDED + ("small_shards",)
    halves_of = lambda blk, core: lax.dynamic_index_in_dim(
        blk.reshape(N_SHARDS, 2, *blk.shape[1:]), core, axis=1, keepdims=False).reshape(-1, blk.shape[-1])
    from_sibling = sibling_swap([halves_of(blk, 1 - mc) for blk in blocks], "prereduce_swap")
    chip_sums = [add_arrays([halves_of(blk, mc), got], f"prereduce_{n}", blk.dtype).reshape(N_SHARDS, -1, blk.shape[-1])
                 for n, blk, got in zip(scattered, blocks, from_sibling)]
    dsilu_rows = jnp.broadcast_to(jnp.pad(dsilu[n_ex:n_ex + 1], ((0, PACK_ROWS - 1), (0, 0)))[None],
                                  (N_SHARDS, PACK_ROWS, D_MODEL))
    to_chips = [dsilu_rows] + chip_sums
    received = exchange(to_chips, False, CHIP_PEERS, "scatter_big_grads", by_chip=True, own=False)
    received = [lax.dynamic_update_index_in_dim(got, lax.dynamic_index_in_dim(sent, shard, 0, keepdims=False), shard, 0)
                for got, sent in zip(received, to_chips)]
    grads["c_ctx"] = c_ctx_grad(received[0], a["c_ctx"][None, :])
    half_sums = [sum_slots(p, tuple(range(N_SHARDS)), f"sum_{n}") for n, p in zip(scattered, received[1:])]
    other_halves = sibling_swap(half_sums, "swap_halves")
    for n, mine, other in zip(scattered, half_sums, other_halves):
        rows = mine.shape[0]
        whole = jnp.zeros((2 * rows, mine.shape[1]), f32)
        whole = lax.dynamic_update_slice_in_dim(whole, mine, mc * rows, axis=0)
        grads[n] = lax.dynamic_update_slice_in_dim(whole, other, (1 - mc) * rows, axis=0)
    grads.update(zip(LAST_SHARDED, _unpack(grads.pop("small_shards"), shard_layout)))

    out_g, out_d, out_m, out_v = {}, {}, {}, {}
    for n in ("w_ada",) + COL_SHARDED + ROW_SHARDED:
        out_g[n] = grads[n].reshape(a[n].shape)
        two_d = lambda z: z.reshape(-1, z.shape[-1])
        d, m, v = adamw(two_d(a[n]), two_d(out_g[n]), two_d(a["m_" + n]), two_d(a["v_" + n]), f"adamw_{n}")
        out_d[n], out_m[n], out_v[n] = d.reshape(a[n].shape), m.reshape(a[n].shape), v.reshape(a[n].shape)
    rest = REPLICATED + LAST_SHARDED
    for n in rest:
        out_g[n] = grads[n].reshape(a[n].shape)
    packs = [_pack([src[n] for n in rest])[0] for src in
             ({n: a[n] for n in rest}, out_g, {n: a["m_" + n] for n in rest}, {n: a["v_" + n] for n in rest})]
    _, rest_layout = _pack([a[n] for n in rest])
    for dst, pack in zip((out_d, out_m, out_v), adamw(*packs, "adamw_small")):
        dst.update(zip(rest, _unpack(pack, rest_layout)))
    return (loss_total, grad_x, *[out_g[n] for n in W_NAMES], *[out_d[n] for n in W_NAMES],
            *[out_m[n] for n in W_NAMES], *[out_v[n] for n in W_NAMES])


def kernel(x, c, ctx, c_ctx, w_ada, b_ada, norm1_g, norm2_g, w_in, ret_log_decay, rwkv_shift_mu, rwkv_w0, rwkv_w_up, rwkv_a0, rwkv_a_up, rwkv_g_up, rwkv_k_k, rwkv_k_a, rwkv_r_k, rwkv_ln_w, rwkv_ln_b, w_out, w_ff1, b_ff1, w_ff2, b_ff2, final_g, loss_target, m_c_ctx, m_w_ada, m_b_ada, m_norm1_g, m_norm2_g, m_w_in, m_ret_log_decay, m_rwkv_shift_mu, m_rwkv_w0, m_rwkv_w_up, m_rwkv_a0, m_rwkv_a_up, m_rwkv_g_up, m_rwkv_k_k, m_rwkv_k_a, m_rwkv_r_k, m_rwkv_ln_w, m_rwkv_ln_b, m_w_out, m_w_ff1, m_b_ff1, m_w_ff2, m_b_ff2, m_final_g, v_c_ctx, v_w_ada, v_b_ada, v_norm1_g, v_norm2_g, v_w_in, v_ret_log_decay, v_rwkv_shift_mu, v_rwkv_w0, v_rwkv_w_up, v_rwkv_a0, v_rwkv_a_up, v_rwkv_g_up, v_rwkv_k_k, v_rwkv_k_a, v_rwkv_r_k, v_rwkv_ln_w, v_rwkv_ln_b, v_w_out, v_w_ff1, v_b_ff1, v_w_ff2, v_b_ff2, v_final_g):
    return _train_step(dict(locals()))
```

```python
import functools
import math

import jax
import jax.numpy as jnp
from jax import lax
from jax.experimental import pallas as pl
from jax.experimental.pallas import tpu as pltpu

f32 = jnp.float32
MXU_DTYPE = jnp.bfloat16

D_MODEL = 1024
RET_W = 512
RET_HEADS = 4
RET_DH = 128
RET_CHUNK = 128
RW_W = 512
RW_N = 64
DECAY_LORA = 64
AAA_LORA = 64
GATE_LORA = 128
LORA_W = DECAY_LORA + AAA_LORA + GATE_LORA
D_FF = 4096
RET_COLS = 4 * RET_W
SHIFT_COLS = 3 * RW_W + LORA_W
IN_COLS = RET_COLS + SHIFT_COLS
GRID_W = 64
ROPE_BASE = 10000.0
NORM_EPS = 1e-6
GN_EPS = 64e-5
W_DECAY_SCALE = math.exp(-0.5)
ADAM_LR, ADAM_B1, ADAM_B2, ADAM_EPS, ADAM_WD, ADAM_STEP = 0.001, 0.9, 0.999, 1e-08, 0.01, 10

TOK_TILE = 256
MATMUL_TILE = 1024
SCAN_CHUNK = 16
SCAN_UNROLL = SCAN_CHUNK
N_DEV = 8
V7X_VMEM_BYTES = 64 * 1024 * 1024
VMEM_LIMIT = V7X_VMEM_BYTES * 7 // 8


def _cparams(sem):
    return pltpu.CompilerParams(dimension_semantics=sem, vmem_limit_bytes=VMEM_LIMIT)


def _tile(n, cap):
    best = None
    for t in range(128, min(n, cap) + 1, 128):
        if n % t == 0:
            best = t
    return best if best is not None else n


def matmul(a, b, mode, name, out_dtype=f32, bias=None, finish=None):
    if mode == "nn":
        (m, k), (k2, n) = a.shape, b.shape
    elif mode == "nt":
        (m, k), (n, k2) = a.shape, b.shape
    else:
        (k, m), (k2, n) = a.shape, b.shape
    assert k == k2, (a.shape, b.shape, mode)
    tm, tn, tk = _tile(m, MATMUL_TILE), _tile(n, MATMUL_TILE), _tile(k, MATMUL_TILE)
    nk = k // tk
    dims = {"nn": ((1,), (0,)), "nt": ((1,), (1,)), "tn": ((0,), (0,))}[mode]

    def body(a_ref, b_ref, *rest):
        o_ref, acc_ref = rest[-2:]
        kk = pl.program_id(2)

        @pl.when(kk == 0)
        def _():
            acc_ref[...] = jnp.zeros_like(acc_ref)

        acc_ref[...] += lax.dot_general(a_ref[...].astype(MXU_DTYPE), b_ref[...].astype(MXU_DTYPE),
                                        (dims, ((), ())), preferred_element_type=f32)

        @pl.when(kk == nk - 1)
        def _():
            res = acc_ref[...]
            if bias is not None:
                res = res + rest[0][...]
            if finish is not None:
                res = finish(res)
            o_ref[...] = res.astype(o_ref.dtype)

    if mode == "nn":
        a_spec = pl.BlockSpec((tm, tk), lambda i, j, q: (i, q))
        b_spec = pl.BlockSpec((tk, tn), lambda i, j, q: (q, j))
    elif mode == "nt":
        a_spec = pl.BlockSpec((tm, tk), lambda i, j, q: (i, q))
        b_spec = pl.BlockSpec((tn, tk), lambda i, j, q: (j, q))
    else:
        a_spec = pl.BlockSpec((tk, tm), lambda i, j, q: (q, i))
        b_spec = pl.BlockSpec((tk, tn), lambda i, j, q: (q, j))
    extra_specs = [] if bias is None else [pl.BlockSpec((1, tn), lambda i, j, q: (0, j))]
    extra = [] if bias is None else [bias]
    return pl.pallas_call(
        body, grid=(m // tm, n // tn, nk), in_specs=[a_spec, b_spec] + extra_specs,
        out_specs=pl.BlockSpec((tm, tn), lambda i, j, q: (i, j)),
        out_shape=jax.ShapeDtypeStruct((m, n), out_dtype),
        scratch_shapes=[pltpu.VMEM((tm, tn), f32)],
        compiler_params=_cparams(("parallel", "parallel", "arbitrary")), name=name)(a, b, *extra)


class Tiled:
    def __init__(self, arr, w=None, cidx=0, toff=0):
        self.arr, self.w, self.cidx, self.toff = arr, (arr.shape[-1] if w is None else w), cidx, toff

    def spec(self):
        cidx, toff = self.cidx, self.toff
        return pl.BlockSpec((None, TOK_TILE, self.w), lambda b, i: (b, jnp.maximum(i + toff, 0), cidx))


class Seg:
    def __init__(self, arr, seg, first):
        self.arr, self.seg, self.first = arr, seg, first

    def spec(self):
        seg = self.seg
        return pl.BlockSpec((None, None, 1, self.arr.shape[-1]), lambda b, i: (b, seg(i), 0, 0))


class Glob:
    def __init__(self, arr):
        self.arr = arr

    def spec(self):
        return pl.BlockSpec(self.arr.shape, lambda b, i: (0,) * self.arr.ndim)


def ew_forward(fn, name, bsz, n_tiles, ins, outs):
    n_in = len(ins)

    def body(*refs):
        res = fn(*[r[...] for r in refs[:n_in]])
        for o_ref, o in zip(refs[n_in:], res):
            o_ref[...] = o.astype(o_ref.dtype)

    out_specs = [pl.BlockSpec((None, TOK_TILE, w), lambda b, i: (b, i, 0)) for w, _ in outs]
    out_shape = [jax.ShapeDtypeStruct((bsz, n_tiles * TOK_TILE, w), dt) for w, dt in outs]
    return pl.pallas_call(body, grid=(bsz, n_tiles), in_specs=[d.spec() for d in ins], out_specs=out_specs,
                          out_shape=out_shape, compiler_params=_cparams(("parallel", "parallel")), name=name)(
        *[d.arr for d in ins])


def ew_backward(fn, name, bsz, n_tiles, ins, cts, want, grad_dtypes=None, lead=0):
    n_in, n_ct = len(ins), len(cts)
    diff = [k for k in range(n_in) if want[k]]
    grad_dtypes = grad_dtypes or {}
    assert lead == 0 or not any(isinstance(ins[k], Seg) for k in diff)

    def body(*refs):
        b, i = pl.program_id(0), pl.program_id(1)
        g_refs = refs[n_in + n_ct:]

        def tile_grads():
            vals = [r[...] for r in refs[:n_in]]
            ct_vals = tuple(r[...].astype(f32) for r in refs[n_in:n_in + n_ct])

            def f(*dvals):
                full = list(vals)
                for k, v in zip(diff, dvals):
                    full[k] = v
                return tuple(fn(*full))

            _, vjp = jax.vjp(f, *[vals[k] for k in diff])
            grads = vjp(ct_vals)
            for k, g_ref, g in zip(diff, g_refs, grads):
                d = ins[k]
                if isinstance(d, Tiled):
                    g_ref[...] = g.astype(g_ref.dtype)
                else:
                    zero = d.first(i) if isinstance(d, Seg) else jnp.logical_and(b == 0, i == lead)

                    @pl.when(zero)
                    def _(g_ref=g_ref):
                        g_ref[...] = jnp.zeros_like(g_ref)

                    g_ref[...] += g

        if lead == 0:
            tile_grads()
        else:
            pl.when(i >= lead)(tile_grads)

            @pl.when(i < lead)
            def _():
                for k, g_ref in zip(diff, g_refs):
                    if isinstance(ins[k], Tiled):
                        g_ref[...] = jnp.zeros_like(g_ref)

    out_specs, out_shape = [], []
    for k in diff:
        d = ins[k]
        if isinstance(d, Tiled):
            out_specs.append(pl.BlockSpec((None, TOK_TILE, d.w), lambda b, i: (b, i, 0)))
            out_shape.append(jax.ShapeDtypeStruct((bsz, (n_tiles + lead) * TOK_TILE, d.w), grad_dtypes.get(k, f32)))
        else:
            out_specs.append(d.spec())
            out_shape.append(jax.ShapeDtypeStruct(d.arr.shape, f32))
    return pl.pallas_call(body, grid=(bsz, n_tiles + lead),
                          in_specs=[d.spec() for d in ins] + [c.spec() for c in cts],
                          out_specs=out_specs, out_shape=out_shape,
                          compiler_params=_cparams(("arbitrary", "arbitrary")), name=name)(
        *[d.arr for d in ins], *[c.arr for c in cts])


@jax.custom_vjp
def _mxu_dot(a, b):
    return jnp.dot(a.astype(MXU_DTYPE), b.astype(MXU_DTYPE), preferred_element_type=f32)


def _mxu_dot_fwd(a, b):
    return _mxu_dot(a, b), (a, b)


def _mxu_dot_bwd(res, ct):
    a, b = res
    ct = ct.astype(MXU_DTYPE)
    da = lax.dot_general(ct, b.astype(MXU_DTYPE), (((1,), (1,)), ((), ())), preferred_element_type=f32)
    db = lax.dot_general(a.astype(MXU_DTYPE), ct, (((0,), (0,)), ((), ())), preferred_element_type=f32)
    return da, db


_mxu_dot.defvjp(_mxu_dot_fwd, _mxu_dot_bwd)


def _split_dot_impl(x, ones_mat):
    hi = x.astype(MXU_DTYPE)
    lo = (x - hi.astype(f32)).astype(MXU_DTYPE)
    return jnp.dot(hi, ones_mat, preferred_element_type=f32) + jnp.dot(lo, ones_mat, preferred_element_type=f32)


@jax.custom_vjp
def _split_dot(x, ones_mat):
    return _split_dot_impl(x, ones_mat)


def _split_dot_fwd(x, ones_mat):
    return _split_dot_impl(x, ones_mat), ones_mat


def _split_dot_bwd(ones_mat, ct):
    return _split_dot_impl(ct, ones_mat), None


_split_dot.defvjp(_split_dot_fwd, _split_dot_bwd)


def _block_ones(n, group):
    idx = jnp.arange(n) // group
    return (idx[:, None] == idx[None, :]).astype(MXU_DTYPE)


def _rms(x, g):
    return x * lax.rsqrt(jnp.mean(x * x, axis=-1, keepdims=True) + NORM_EPS) * g


def fn_norm_mod(h, shift, scale, g):
    return (_rms(h, g) * (1.0 + scale) + shift,)


def fn_rwkv_prepare(ks, lora, w0_f, w0_b, a0_f, a0_b, w_up_f, w_up_b, a_up_f, a_up_b, g_up, k_k, k_a, ones64):
    kkr = ks * k_k
    kk = kkr * lax.rsqrt(_split_dot(kkr * kkr, ones64) + 1e-12)
    outs = [kk]
    th = jnp.tanh(lora)
    for w0, a0, w_up, a_up in ((w0_f, a0_f, w_up_f, a_up_f), (w0_b, a0_b, w_up_b, a_up_b)):
        w = jnp.exp(-W_DECAY_SCALE * jax.nn.sigmoid(w0 + _mxu_dot(th, w_up)))
        a = jax.nn.sigmoid(a0 + _mxu_dot(lora, a_up))
        kt = ks * (1.0 + (a - 1.0) * k_a)
        outs += [w, a * kk, kt]
    outs.append(_mxu_dot(jax.nn.sigmoid(lora), g_up))
    return tuple(outs)


def fn_merge(o_f, o_b, g_ret, y_f, y_b, r, kt_f, v, g_rw, r_k, ln_w, ln_b, ones64, ones128):
    o = o_f + o_b
    ret = o * lax.rsqrt(_split_dot(o * o, ones128) * (1.0 / RET_DH) + NORM_EPS) * (g_ret * jax.nn.sigmoid(g_ret))
    y = y_f + y_b
    mean = _split_dot(y, ones64) * (1.0 / RW_N)
    yc = y - mean
    var = _split_dot(yc * yc, ones64) * (1.0 / RW_N)
    y_n = yc * lax.rsqrt(var + GN_EPS) * ln_w + ln_b
    bonus = _split_dot(r * kt_f * r_k, ones64) * v
    return ret, (y_n + bonus) * g_rw


def fn_resid_norm_mod(x, mix, gate, shift, scale, g):
    h1 = x + gate * mix
    return h1, _rms(h1, g) * (1.0 + scale) + shift


def relu2(z):
    return jnp.square(jnp.maximum(z, 0.0))


def relu2_backward(act, dact, name):
    bsz, n_tok, width = act.shape

    def body(a_ref, d_ref, du_ref, db_ref):
        du = d_ref[...].astype(f32) * (2.0 * jnp.sqrt(a_ref[...].astype(f32)))
        du_ref[...] = du.astype(du_ref.dtype)

        @pl.when(jnp.logical_and(pl.program_id(0) == 0, pl.program_id(1) == 0))
        def _():
            db_ref[...] = jnp.zeros_like(db_ref)

        db_ref[...] += jnp.sum(du, axis=0, keepdims=True)

    tile = pl.BlockSpec((None, TOK_TILE, width), lambda b, i: (b, i, 0))
    row = pl.BlockSpec((1, width), lambda b, i: (0, 0))
    return pl.pallas_call(body, grid=(bsz, n_tok // TOK_TILE), in_specs=[tile, tile], out_specs=[tile, row],
                          out_shape=[jax.ShapeDtypeStruct(act.shape, MXU_DTYPE), jax.ShapeDtypeStruct((1, width), f32)],
                          compiler_params=_cparams(("arbitrary", "arbitrary")), name=name)(act, dact)


def fn_loss(h1, f, tgt, gate, b2, g):
    y = _rms(h1 + gate * (f + b2), g)
    err = jnp.square(y - tgt)
    return 0.5 * jnp.sum(jnp.mean(err, axis=-1, keepdims=True), axis=0, keepdims=True)


def loss_and_grads(h1, f, tgt, gate, b2, g, bsz, n_tiles):
    def body(h1_ref, f_ref, t_ref, gate_ref, b2_ref, g_ref, loss_ref, dh1_ref, df_ref, dgate_ref, db2_ref, dg_ref):
        b, i = pl.program_id(0), pl.program_id(1)
        tgt_v = t_ref[...]
        loss, vjp = jax.vjp(lambda a, c, e, p, q: fn_loss(a, c, tgt_v, e, p, q),
                            h1_ref[...], f_ref[...], gate_ref[...], b2_ref[...], g_ref[...])
        dh1, df, dgate, db2, dg = vjp(jnp.ones((1, 1), f32))
        dh1_ref[...] = dh1
        df_ref[...] = df.astype(df_ref.dtype)

        @pl.when(i == 0)
        def _():
            dgate_ref[...] = jnp.zeros_like(dgate_ref)

        @pl.when(jnp.logical_and(b == 0, i == 0))
        def _():
            loss_ref[...] = jnp.zeros_like(loss_ref)
            db2_ref[...] = jnp.zeros_like(db2_ref)
            dg_ref[...] = jnp.zeros_like(dg_ref)

        dgate_ref[...] += dgate
        db2_ref[...] += db2
        dg_ref[...] += dg
        loss_ref[...] += jnp.broadcast_to(loss, loss_ref.shape)

    tile = pl.BlockSpec((None, TOK_TILE, D_MODEL), lambda b, i: (b, i, 0))
    row = pl.BlockSpec((1, D_MODEL), lambda b, i: (0, 0))
    seg = pl.BlockSpec((None, None, 1, D_MODEL), lambda b, i: (b, 0, 0, 0))
    t_tok = n_tiles * TOK_TILE
    return pl.pallas_call(
        body, grid=(bsz, n_tiles), in_specs=[tile, tile, tile, seg, row, row],
        out_specs=[pl.BlockSpec((1, 128), lambda b, i: (0, 0)), tile, tile, seg, row, row],
        out_shape=[jax.ShapeDtypeStruct((1, 128), f32), jax.ShapeDtypeStruct((bsz, t_tok, D_MODEL), f32),
                   jax.ShapeDtypeStruct((bsz, t_tok, D_MODEL), MXU_DTYPE),
                   jax.ShapeDtypeStruct((bsz, 1, 1, D_MODEL), f32),
                   jax.ShapeDtypeStruct((1, D_MODEL), f32), jax.ShapeDtypeStruct((1, D_MODEL), f32)],
        compiler_params=_cparams(("arbitrary", "arbitrary")), name="loss_and_grads")(h1, f, tgt, gate, b2, g)


SHIFT_BLOCK = SHIFT_COLS
HALO_ROWS = 8


def _shift_specs(n_tok, col0):
    per_tile = TOK_TILE // HALO_ROWS
    last = n_tok // HALO_ROWS - 1
    tile = pl.BlockSpec((None, TOK_TILE, SHIFT_BLOCK), lambda j, b, i: (b, i, col0 + j))
    prev = pl.BlockSpec((None, HALO_ROWS, SHIFT_BLOCK),
                        lambda j, b, i: (b, jnp.maximum(i * per_tile - 1, 0), col0 + j))
    nxt = pl.BlockSpec((None, HALO_ROWS, SHIFT_BLOCK),
                       lambda j, b, i: (b, jnp.minimum((i + 1) * per_tile, last), col0 + j))
    return tile, prev, nxt


def _shifted(p, prev_ref, next_ref, is_first, is_last):
    row = lax.broadcasted_iota(jnp.int32, p.shape, 0)
    prev_row = jnp.where(is_first, 0.0, prev_ref[HALO_ROWS - 1:HALO_ROWS, :].astype(f32))
    next_row = jnp.where(is_last, 0.0, next_ref[0:1, :].astype(f32))
    prev = jnp.where(row == 0, prev_row, pltpu.roll(p, 1, axis=0))
    nxt = jnp.where(row == TOK_TILE - 1, next_row, pltpu.roll(p, TOK_TILE - 1, axis=0))
    return prev, nxt


def token_shift(px, mu, seg_first, seg_last):
    bsz, n_tok, _ = px.shape
    n_tiles = n_tok // TOK_TILE

    def body(p_ref, prev_ref, next_ref, mu_ref, o_ref):
        i = pl.program_id(2)
        p = p_ref[...]
        prev, nxt = _shifted(p, prev_ref, next_ref, seg_first(i), seg_last(i))
        o_ref[...] = p + mu_ref[0:1, :] * (prev - p) + mu_ref[1:2, :] * (nxt - p)

    tile, prev, nxt = _shift_specs(n_tok, 0)
    return pl.pallas_call(
        body, grid=(SHIFT_COLS // SHIFT_BLOCK, bsz, n_tiles),
        in_specs=[tile, prev, nxt, pl.BlockSpec((2, SHIFT_BLOCK), lambda j, b, i: (0, j))],
        out_specs=pl.BlockSpec((None, TOK_TILE, SHIFT_BLOCK), lambda j, b, i: (b, i, j)),
        out_shape=jax.ShapeDtypeStruct((bsz, n_tok, SHIFT_COLS), f32),
        compiler_params=_cparams(("parallel", "parallel", "parallel")), name="token_shift")(px, px, px, mu)


def token_shift_bwd(dps, px, mu, seg_first, seg_last):
    bsz, n_tok, _ = px.shape
    n_tiles = n_tok // TOK_TILE

    def body(d_ref, dprev_ref, dnext_ref, p_ref, prev_ref, next_ref, mu_ref, dp_ref, dmu_ref):
        b, i = pl.program_id(1), pl.program_id(2)
        first, last = seg_first(i), seg_last(i)
        d, p = d_ref[...], p_ref[...]
        d_prev, d_next = _shifted(d, dprev_ref, dnext_ref, first, last)
        p_prev, p_next = _shifted(p, prev_ref, next_ref, first, last)
        mu0, mu1 = mu_ref[0:1, :], mu_ref[1:2, :]
        dp_ref[...] = (d + mu0 * (d_next - d) + mu1 * (d_prev - d)).astype(dp_ref.dtype)

        @pl.when(jnp.logical_and(b == 0, i == 0))
        def _():
            dmu_ref[...] = jnp.zeros_like(dmu_ref)

        dmu_ref[0:1, :] += jnp.sum(d * (p_prev - p), axis=0, keepdims=True)
        dmu_ref[1:2, :] += jnp.sum(d * (p_next - p), axis=0, keepdims=True)

    dtile, dprev, dnext = _shift_specs(n_tok, 0)
    tile, prev, nxt = _shift_specs(n_tok, 0)
    mu_spec = pl.BlockSpec((2, SHIFT_BLOCK), lambda j, b, i: (0, j))
    return pl.pallas_call(
        body, grid=(SHIFT_COLS // SHIFT_BLOCK, bsz, n_tiles),
        in_specs=[dtile, dprev, dnext, tile, prev, nxt, mu_spec],
        out_specs=[pl.BlockSpec((None, TOK_TILE, SHIFT_BLOCK), lambda j, b, i: (b, i, j)), mu_spec],
        out_shape=[jax.ShapeDtypeStruct((bsz, n_tok, SHIFT_COLS), MXU_DTYPE),
                   jax.ShapeDtypeStruct((2, SHIFT_COLS), f32)],
        compiler_params=_cparams(("arbitrary", "arbitrary", "arbitrary")), name="token_shift_bwd")(
        dps, dps, dps, px, px, px, mu)


def _dg(a, b, ca, cb):
    return lax.dot_general(a.astype(MXU_DTYPE), b.astype(MXU_DTYPE), (((ca,), (cb,)), ((), ())),
                           preferred_element_type=f32)


@jax.custom_vjp
def _mm_nt(a, b):
    return _dg(a, b, 1, 1)


_mm_nt.defvjp(lambda a, b: (_dg(a, b, 1, 1), (a, b)),
              lambda res, ct: (_dg(ct, res[1], 1, 0), _dg(ct, res[0], 0, 0)))


@jax.custom_vjp
def _mm_tn(a, b):
    return _dg(a, b, 0, 0)


_mm_tn.defvjp(lambda a, b: (_dg(a, b, 0, 0), (a, b)),
              lambda res, ct: (_dg(res[1], ct, 1, 1), _dg(res[0], ct, 1, 0)))


ROTARY_PAIR = RET_DH // 4


def _swap_pairs_impl(t):
    lane = lax.broadcasted_iota(jnp.int32, t.shape, 1)
    return jnp.where(lane % (2 * ROTARY_PAIR) < ROTARY_PAIR, pltpu.roll(t, RET_DH - ROTARY_PAIR, axis=1),
                     pltpu.roll(t, ROTARY_PAIR, axis=1))


@jax.custom_vjp
def _swap_pairs(t):
    return _swap_pairs_impl(t)


_swap_pairs.defvjp(lambda t: (_swap_pairs_impl(t), None), lambda _, ct: (_swap_pairs_impl(ct),))


def _ret_chunk(state, q_raw, k_raw, v, cos, sin, ld_row, head, reverse):
    c = RET_CHUNK
    lane = lax.broadcasted_iota(jnp.int32, ld_row.shape, 1)
    lg = -jnp.exp(jnp.sum(jnp.where(lane == head, ld_row, 0.0), axis=-1, keepdims=True))
    rot = lambda t: t * cos + _swap_pairs(t) * sin
    q = rot(q_raw)
    k = rot(k_raw) * (RET_DH ** -0.5)
    ti = lax.broadcasted_iota(jnp.int32, (c, 1), 0).astype(f32)
    tj = lax.broadcasted_iota(jnp.int32, (1, c), 1).astype(f32)
    if not reverse:
        dist, mask, q_exp, k_exp = ti - tj, (ti - tj) >= 0, ti + 1.0, c - 1.0 - ti
    else:
        dist, mask, q_exp, k_exp = tj - ti, (tj - ti) > 0, c - ti, ti
    decay = jnp.where(mask, jnp.exp(lg * jnp.maximum(dist, 0.0)), 0.0)
    scores = _mm_nt(q, k) * decay
    out = _mxu_dot(scores, v) + _mxu_dot(q * jnp.exp(lg * q_exp), state)
    new_state = state * jnp.exp(lg * c) + _mm_tn(k * jnp.exp(lg * k_exp), v)
    return out, new_state


def _ret_specs(order):
    qkv = [pl.BlockSpec((None, RET_CHUNK, RET_W), functools.partial(lambda b, i, col: (b, order(i), col), col=col))
           for col in range(3)]
    tab = pl.BlockSpec((RET_CHUNK, RET_DH), lambda b, i: (order(i), 0))
    ld = pl.BlockSpec((1, RET_DH), lambda b, i: (0, 0))
    return qkv, tab, ld


def retention_fwd(px, cos, sin, ld_row, order, reverse, name):
    bsz, n_tok, _ = px.shape
    n_ch = n_tok // RET_CHUNK

    def body(q_ref, k_ref, v_ref, cos_ref, sin_ref, ld_ref, o_ref, sv_ref, st_ref):
        @pl.when(pl.program_id(1) == 0)
        def _():
            st_ref[...] = jnp.zeros_like(st_ref)

        for h in range(RET_HEADS):
            sl = slice(h * RET_DH, (h + 1) * RET_DH)
            s = st_ref[h]
            sv_ref[h] = s
            o, s_new = _ret_chunk(s, q_ref[:, sl], k_ref[:, sl], v_ref[:, sl], cos_ref[...], sin_ref[...],
                                  ld_ref[...], h, reverse)
            o_ref[:, sl] = o
            st_ref[h] = s_new

    qkv, tab, ld = _ret_specs(order)
    return pl.pallas_call(
        body, grid=(bsz, n_ch), in_specs=[*qkv, tab, tab, ld],
        out_specs=[pl.BlockSpec((None, RET_CHUNK, RET_W), lambda b, i: (b, order(i), 0)),
                   pl.BlockSpec((None, None, RET_HEADS, RET_DH, RET_DH), lambda b, i: (b, i, 0, 0, 0))],
        out_shape=[jax.ShapeDtypeStruct((bsz, n_tok, RET_W), f32),
                   jax.ShapeDtypeStruct((bsz, n_ch, RET_HEADS, RET_DH, RET_DH), f32)],
        scratch_shapes=[pltpu.VMEM((RET_HEADS, RET_DH, RET_DH), f32)],
        compiler_params=_cparams(("parallel", "arbitrary")), name=name)(px, px, px, cos, sin, ld_row)


def retention_bwd(do, px, states, cos, sin, ld_row, order, reverse, name):
    bsz, n_tok, _ = px.shape
    n_ch = n_tok // RET_CHUNK
    back = lambda i: order(n_ch - 1 - i)

    def body(do_ref, q_ref, k_ref, v_ref, sv_ref, cos_ref, sin_ref, ld_ref,
             dq_ref, dk_ref, dv_ref, dld_ref, dst_ref):
        b, i = pl.program_id(0), pl.program_id(1)

        @pl.when(i == 0)
        def _():
            dst_ref[...] = jnp.zeros_like(dst_ref)

        @pl.when(jnp.logical_and(b == 0, i == 0))
        def _():
            dld_ref[...] = jnp.zeros_like(dld_ref)

        cos_v, sin_v = cos_ref[...], sin_ref[...]
        for h in range(RET_HEADS):
            sl = slice(h * RET_DH, (h + 1) * RET_DH)
            f = lambda s, q, k, v, ld, h=h: _ret_chunk(s, q, k, v, cos_v, sin_v, ld, h, reverse)
            _, vjp = jax.vjp(f, sv_ref[h], q_ref[:, sl], k_ref[:, sl], v_ref[:, sl], ld_ref[...])
            ds, dq, dk, dv, dld = vjp((do_ref[:, sl], dst_ref[h]))
            dst_ref[h] = ds
            dq_ref[:, sl] = dq
            dk_ref[:, sl] = dk
            dv_ref[:, sl] = dv
            dld_ref[...] += dld

    qkv, tab, ld = _ret_specs(back)
    tok = pl.BlockSpec((None, RET_CHUNK, RET_W), lambda b, i: (b, back(i), 0))
    return pl.pallas_call(
        body, grid=(bsz, n_ch),
        in_specs=[tok, *qkv,
                  pl.BlockSpec((None, None, RET_HEADS, RET_DH, RET_DH), lambda b, i: (b, n_ch - 1 - i, 0, 0, 0)),
                  tab, tab, ld],
        out_specs=[tok, tok, tok, ld],
        out_shape=[jax.ShapeDtypeStruct((bsz, n_tok, RET_W), f32)] * 3 + [jax.ShapeDtypeStruct((1, RET_DH), f32)],
        scratch_shapes=[pltpu.VMEM((RET_HEADS, RET_DH, RET_DH), f32)],
        compiler_params=_cparams(("arbitrary", "arbitrary")), name=name)(
        do, px, px, px, states, cos, sin, ld_row)


HALF_W = RW_W // 2


def _head_sum(x, ones):
    xm = x.astype(MXU_DTYPE)
    return jnp.concatenate([jnp.dot(xm[:, :HALF_W], ones, preferred_element_type=f32),
                            jnp.dot(xm[:, HALF_W:], ones, preferred_element_type=f32)], axis=1)


def _stack(parts):
    return jnp.concatenate(parts, axis=0)


def _row(ref, b, t):
    return ref[b, pl.ds(t, 1), :]


SCAN_DIRS = ((False, True), (True, False))
RW_HEADS = RW_W // RW_N
HEAD_ROWS_PAD = 16


def _head_rows(row, mask):
    return jnp.broadcast_to(row, mask.shape) * mask


def _outer(per_value, row, mask_pad):
    return lax.dot_general(per_value.astype(MXU_DTYPE), _head_rows(row, mask_pad).astype(MXU_DTYPE),
                           (((0,), (0,)), ((), ())), preferred_element_type=f32)


def _read(states, rows, mask):
    lhs = _stack([_head_rows(r, mask) for r in rows])
    return lax.dot_general(lhs.astype(MXU_DTYPE), _stack(states).astype(MXU_DTYPE), (((1,), (1,)), ((), ())),
                           preferred_element_type=f32)


def _row_from_heads(per_value, state, mask_pad):
    full = jnp.dot(per_value.astype(MXU_DTYPE), state.astype(MXU_DTYPE), preferred_element_type=f32)
    return jnp.sum(full * mask_pad, axis=0, keepdims=True)


def _scan_specs(bsz, order):
    rows = lambda col=0: pl.BlockSpec((bsz, SCAN_CHUNK, RW_W), lambda i: (0, order(i), col))
    per_value = pl.BlockSpec((bsz, SCAN_CHUNK, HEAD_ROWS_PAD, RW_N), lambda i: (0, order(i), 0, 0))
    raw = pl.BlockSpec((SCAN_CHUNK, RW_HEADS * bsz, RW_N * bsz), lambda i: (order(i), 0, 0))
    return rows, per_value, raw


def _removed(sp, kk_t, ones, bsz):
    removed = _head_sum(_stack([sp[b] * kk_t[b] for b in range(bsz)]), ones)
    return [removed[b * RW_N:(b + 1) * RW_N] for b in range(bsz)]


def _advance(sp, rem, w_t, b_t, vk, bsz):
    return [sp[b] * w_t[b] - rem[b] * b_t[b] + vk[b] for b in range(bsz)]


def heads_to_rows(a):
    b, t, _ = a.shape
    return jnp.pad(a.reshape(b, t, RW_HEADS, RW_N), ((0, 0), (0, 0), (0, HEAD_ROWS_PAD - RW_HEADS), (0, 0)))


def read_out_rows(raw, bsz):
    t = raw.shape[0]
    r5 = raw.reshape(t, bsz, RW_HEADS, bsz, RW_N)
    return jnp.stack([r5[:, b, :, b, :] for b in range(bsz)]).reshape(bsz, t, RW_W)


N_ROWS_FWD = 5
N_ROWS_BWD = 5


def _scan_consts(bsz):
    head = (jnp.arange(RW_W)[None, :] // RW_N == jnp.arange(RW_HEADS)[:, None]).astype(f32)
    return head, jnp.pad(head, ((0, HEAD_ROWS_PAD - RW_HEADS), (0, 0))), _block_ones(HALF_W, RW_N)


def _const_specs(consts):
    return [pl.BlockSpec(c.shape, lambda i: (0, 0)) for c in consts]


def rwkv_scan_fwd(rows_in, v_heads, orders, name):
    bsz, n_tok, _ = rows_in[0][0][0].shape
    n_ch = n_tok // SCAN_CHUNK
    rng = range(bsz)
    consts = _scan_consts(bsz)

    def body(*refs):
        rows = [refs[:N_ROWS_FWD], refs[N_ROWS_FWD:2 * N_ROWS_FWD]]
        v0, v1, head_ref, pad_ref, ones_ref, y0, y1, cs0, cs1, s0, s1, late_ref = refs[2 * N_ROWS_FWD:]
        v_refs, y_refs, cs_refs, s_refs = (v0, v1), (y0, y1), (cs0, cs1), (s0, s1)
        head_v, pad_v, ones_v = head_ref[...], pad_ref[...], ones_ref[...]
        for d in range(2):
            @pl.when(pl.program_id(0) == 0)
            def _(d=d):
                s_refs[d][...] = jnp.zeros_like(s_refs[d])

            cs_refs[d][...] = s_refs[d][...]

        def step(j, carry):
            ts = [SCAN_CHUNK - 1 - j if reverse else j for reverse, _ in SCAN_DIRS]
            sps = [[s_refs[d][b] for b in rng] for d in range(2)]
            rems = [_removed(sps[d], [_row(rows[d][1], b, ts[d]) for b in rng], ones_v, bsz) for d in range(2)]
            vks = [[_outer(v_refs[d][b, ts[d]], _row(rows[d][4], b, ts[d]), pad_v) for b in rng] for d in range(2)]
            for d, (reverse, inclusive) in enumerate(SCAN_DIRS):
                r_ref = rows[d][0]
                if inclusive:
                    before = jnp.maximum(j - 1, 0)
                    late_ref[j] = _read(sps[d], [_row(r_ref, b, before) for b in rng], head_v)
                else:
                    y_refs[d][ts[d]] = _read(sps[d], [_row(r_ref, b, ts[d]) for b in rng], head_v)
            for d in range(2):
                new = _advance(sps[d], rems[d], [_row(rows[d][2], b, ts[d]) for b in rng],
                               [_row(rows[d][3], b, ts[d]) for b in rng], vks[d], bsz)
                for b in rng:
                    s_refs[d][b] = new[b]
            return carry

        lax.fori_loop(0, SCAN_CHUNK, step, 0, unroll=SCAN_UNROLL)
        for d, (reverse, inclusive) in enumerate(SCAN_DIRS):
            if inclusive:
                assert not reverse
                last = SCAN_CHUNK - 1
                late_ref[SCAN_CHUNK] = _read([s_refs[d][b] for b in rng], [rows[d][0][b, last:last + 1, :] for b in rng],
                                             head_v)
                for t in range(SCAN_CHUNK):
                    y_refs[d][t] = late_ref[t + 1]

    specs = [_scan_specs(bsz, orders[d]) for d in range(2)]
    state = pltpu.VMEM((bsz, RW_N, RW_W), f32)
    late = pltpu.VMEM((SCAN_CHUNK + 1, RW_HEADS * bsz, RW_N * bsz), f32)
    start_spec = pl.BlockSpec((None, bsz, RW_N, RW_W), lambda i: (i, 0, 0, 0))
    return pl.pallas_call(
        body, grid=(n_ch,),
        in_specs=[specs[d][0](col) for d in range(2) for _, col in rows_in[d]] + [specs[0][1], specs[1][1]]
        + _const_specs(consts),
        out_specs=[specs[0][2], specs[1][2], start_spec, start_spec],
        out_shape=[jax.ShapeDtypeStruct((n_tok, RW_HEADS * bsz, RW_N * bsz), f32)] * 2
        + [jax.ShapeDtypeStruct((n_ch, bsz, RW_N, RW_W), f32)] * 2,
        scratch_shapes=[state, state, late],
        compiler_params=_cparams(("arbitrary",)), name=name)(
        *[a for d in range(2) for a, _ in rows_in[d]], v_heads, v_heads, *consts)


def rwkv_scan_bwd(rows_in, v_heads, dy_heads, starts, orders, name):
    bsz, n_tok, _ = rows_in[0][0][0].shape
    n_ch = n_tok // SCAN_CHUNK
    backs = [functools.partial(lambda i, order: order(n_ch - 1 - i), order=orders[d]) for d in range(2)]
    rng = range(bsz)
    consts = _scan_consts(bsz)
    n_out, n_scr = 6, 7

    def body(*refs):
        rows = [refs[:N_ROWS_BWD], refs[N_ROWS_BWD:2 * N_ROWS_BWD]]
        rest = refs[2 * N_ROWS_BWD:]
        v_refs, dy_refs, cs_refs, (head_ref, pad_ref, ones_ref) = rest[0:2], rest[2:4], rest[4:6], rest[6:9]
        outs = [rest[9:9 + n_out], rest[9 + n_out:9 + 2 * n_out]]
        scr = [rest[9 + 2 * n_out:9 + 2 * n_out + n_scr], rest[9 + 2 * n_out + n_scr:]]
        head_v, pad_v, ones_v = head_ref[...], pad_ref[...], ones_ref[...]
        for d in range(2):
            s_ref, ds_ref = scr[d][:2]

            @pl.when(pl.program_id(0) == 0)
            def _(ds_ref=ds_ref):
                ds_ref[...] = jnp.zeros_like(ds_ref)

            s_ref[...] = cs_refs[d][...]

        def fstep(j, carry):
            ts = [SCAN_CHUNK - 1 - j if reverse else j for reverse, _ in SCAN_DIRS]
            sps = [[scr[d][0][b] for b in rng] for d in range(2)]
            rems = [_removed(sps[d], [_row(rows[d][1], b, ts[d]) for b in rng], ones_v, bsz) for d in range(2)]
            vks = [[_outer(v_refs[d][b, ts[d]], _row(rows[d][4], b, ts[d]), pad_v) for b in rng] for d in range(2)]
            for d in range(2):
                s_ref, _, hist_ref, rem_ref, _, _, _ = scr[d]
                new = _advance(sps[d], rems[d], [_row(rows[d][2], b, ts[d]) for b in rng],
                               [_row(rows[d][3], b, ts[d]) for b in rng], vks[d], bsz)
                for b in rng:
                    hist_ref[ts[d], b] = sps[d][b]
                    rem_ref[ts[d], b] = rems[d][b]
                    s_ref[b] = new[b]
            return carry

        lax.fori_loop(0, SCAN_CHUNK, fstep, 0, unroll=SCAN_UNROLL)

        def step_of(j, reverse):
            return j if reverse else SCAN_CHUNK - 1 - j

        for d, (reverse, _) in enumerate(SCAN_DIRS):
            t0 = step_of(0, reverse)
            for b in rng:
                scr[d][6][b] = _outer(dy_refs[d][b, t0], rows[d][0][b, t0:t0 + 1, :], pad_v)

        def bstep(j, carry):
            ts = [step_of(j, reverse) for reverse, _ in SCAN_DIRS]
            reads = [[scr[d][6][b] for b in rng] for d in range(2)]
            dss = []
            for d, (_, inclusive) in enumerate(SCAN_DIRS):
                ds = [scr[d][1][b] for b in rng]
                dss.append([ds[b] + reads[d][b] for b in rng] if inclusive else ds)
            drems = [_removed(dss[d], [-_row(rows[d][3], b, ts[d]) for b in rng], ones_v, bsz) for d in range(2)]
            for d, (reverse, _) in enumerate(SCAN_DIRS):
                t_next = step_of(jnp.minimum(j + 1, SCAN_CHUNK - 1), reverse)
                for b in rng:
                    scr[d][6][b] = _outer(dy_refs[d][b, t_next], _row(rows[d][0], b, t_next), pad_v)
                outs[d][5][ts[d]] = _read(dss[d], [_row(rows[d][4], b, ts[d]) for b in rng], head_v)
            for d, (_, inclusive) in enumerate(SCAN_DIRS):
                _, kk_ref, w_ref, _, _ = rows[d]
                _, ds_ref, _, _, dsh_ref, drem_ref, _ = scr[d]
                for b in rng:
                    dsh_ref[ts[d], b] = dss[d][b]
                    drem_ref[ts[d], b] = drems[d][b]
                    dsp = dss[d][b] * _row(w_ref, b, ts[d]) + drems[d][b] * _row(kk_ref, b, ts[d])
                    ds_ref[b] = dsp if inclusive else dsp + reads[d][b]
            return carry

        lax.fori_loop(0, SCAN_CHUNK, bstep, 0, unroll=SCAN_UNROLL)

        rsum = lambda z: jnp.sum(z, axis=0, keepdims=True)
        for d, (reverse, inclusive) in enumerate(SCAN_DIRS):
            dr_ref, dkk_ref, dw_ref, db_ref, dkt_ref, _ = outs[d]
            s_ref, _, hist_ref, rem_ref, dsh_ref, drem_ref, _ = scr[d]
            for t in range(SCAN_CHUNK):
                ts = slice(t, t + 1)
                after = t - 1 if reverse else t + 1
                for b in rng:
                    sp, ds = hist_ref[t, b], dsh_ref[t, b]
                    if not inclusive:
                        seen = sp
                    else:
                        seen = hist_ref[after, b] if 0 <= after < SCAN_CHUNK else s_ref[b]
                    dr_ref[b, ts, :] = _row_from_heads(dy_refs[d][b, t], seen, pad_v)
                    dkt_ref[b, ts, :] = _row_from_heads(v_refs[d][b, t], ds, pad_v)
                    dw_ref[b, ts, :] = rsum(ds * sp)
                    db_ref[b, ts, :] = -rsum(ds * rem_ref[t, b])
                    dkk_ref[b, ts, :] = rsum(sp * drem_ref[t, b])

    specs = [_scan_specs(bsz, backs[d]) for d in range(2)]
    hist = pltpu.VMEM((SCAN_CHUNK, bsz, RW_N, RW_W), f32)
    state = pltpu.VMEM((bsz, RW_N, RW_W), f32)
    start_spec = pl.BlockSpec((None, bsz, RW_N, RW_W), lambda i: (n_ch - 1 - i, 0, 0, 0))
    row_shape = jax.ShapeDtypeStruct((bsz, n_tok, RW_W), f32)
    raw_shape = jax.ShapeDtypeStruct((n_tok, RW_HEADS * bsz, RW_N * bsz), f32)
    return pl.pallas_call(
        body, grid=(n_ch,),
        in_specs=[specs[d][0](col) for d in range(2) for _, col in rows_in[d]]
        + [specs[0][1], specs[1][1]] * 2 + [start_spec, start_spec] + _const_specs(consts),
        out_specs=[spec for d in range(2) for spec in [specs[d][0]()] * 5 + [specs[d][2]]],
        out_shape=([row_shape] * 5 + [raw_shape]) * 2,
        scratch_shapes=[state, state, hist, hist, hist, hist, state] * 2,
        compiler_params=_cparams(("arbitrary",)), name=name)(
        *[a for d in range(2) for a, _ in rows_in[d]], v_heads, v_heads, dy_heads, dy_heads, *starts, *consts)


MOD_NAMES = ("shift1", "scale1", "gate1", "shift2", "scale2", "gate2")


def _rope_tables(t_ctx, t_x):
    quarter = RET_DH // 4
    pos = jnp.arange(t_x)
    inv = jnp.power(ROPE_BASE, -jnp.arange(0, 2 * quarter, 2, dtype=f32) / (2 * quarter))
    ang_r = (pos // GRID_W).astype(f32)[:, None] * inv[None, :]
    ang_c = (pos % GRID_W).astype(f32)[:, None] * inv[None, :]
    cos = jnp.concatenate([jnp.cos(ang_r)] * 2 + [jnp.cos(ang_c)] * 2, axis=1)
    sin = jnp.concatenate([-jnp.sin(ang_r), jnp.sin(ang_r), -jnp.sin(ang_c), jnp.sin(ang_c)], axis=1)
    cos = jnp.concatenate([jnp.ones((t_ctx, RET_DH), f32), cos], axis=0)
    sin = jnp.concatenate([jnp.zeros((t_ctx, RET_DH), f32), sin], axis=0)
    return cos, sin


def _pad_rows(w, lo, total):
    return jnp.pad(w, ((lo, total - lo - w.shape[0]), (0, 0)))


def layer_step(x, ctx, tgt, mod_x, mod_ctx, wt):
    bsz, t_x, _ = x.shape
    t_c = ctx.shape[1]
    t_all = t_c + t_x
    n_ct, n_xt = t_c // TOK_TILE, t_x // TOK_TILE
    n_t = n_ct + n_xt
    assert t_c % TOK_TILE == 0 and t_x % TOK_TILE == 0 and t_c % RET_CHUNK == 0

    seg = lambda i: (i >= n_ct).astype(jnp.int32)
    seg_first = lambda i: jnp.logical_or(i == 0, i == n_ct)
    seg_last = lambda i: jnp.logical_or(i == n_ct - 1, i == n_t - 1)
    mod_all = {n: jnp.stack([jnp.broadcast_to(mod_ctx[k], (bsz, D_MODEL)), mod_x[:, k]], axis=1)[:, :, None, :]
               for k, n in enumerate(MOD_NAMES)}
    mod_lat = {n: mod_x[:, k][:, None, None, :] for k, n in enumerate(MOD_NAMES)}
    both = lambda n: Seg(mod_all[n], seg, seg_first)
    lat = lambda n: Seg(mod_lat[n], lambda i: 0, lambda i: i == 0)
    flat = lambda a: a.reshape(-1, a.shape[-1])

    def chunk_orders(n_ctx_chunks, n_chunks):
        fwd = lambda i: i
        bwd = lambda i: jnp.where(i < n_ctx_chunks, n_ctx_chunks - 1 - i, n_chunks + n_ctx_chunks - 1 - i)
        return fwd, bwd

    ones64, ones128 = _block_ones(RW_W, RW_N), _block_ones(RET_W, RET_DH)
    cos, sin = _rope_tables(t_c, t_x)
    ld_rows = [jnp.pad(wt["ret_log_decay"][d][None, :], ((0, 0), (0, RET_DH - RET_HEADS))) for d in range(2)]
    w_up_pad = [_pad_rows(wt["rwkv_w_up"][d], 0, LORA_W) for d in range(2)]
    a_up_pad = [_pad_rows(wt["rwkv_a_up"][d], DECAY_LORA, LORA_W) for d in range(2)]
    g_up_pad = _pad_rows(wt["rwkv_g_up"], DECAY_LORA + AAA_LORA, LORA_W)
    row = lambda a, d: a[d][None, :]

    h = jnp.concatenate([ctx, x], axis=1)
    norm1_ins = lambda: [Tiled(h), both("shift1"), both("scale1"), Glob(wt["norm1_g"])]
    (n1,) = ew_forward(fn_norm_mod, "norm1", bsz, n_t, norm1_ins(), [(D_MODEL, MXU_DTYPE)])
    px = matmul(flat(n1), wt["w_in"], "nn", "proj_in").reshape(bsz, t_all, IN_COLS)
    px_rw = px[..., RET_COLS:]
    ps = token_shift(px_rw, wt["rwkv_shift_mu"], seg_first, seg_last)

    def prep_ins(toff=0):
        return [Tiled(ps, RW_W, 1), Tiled(ps, LORA_W, 3 * RW_W // LORA_W),
                Glob(row(wt["rwkv_w0"], 0)), Glob(row(wt["rwkv_w0"], 1)),
                Glob(row(wt["rwkv_a0"], 0)), Glob(row(wt["rwkv_a0"], 1)),
                Glob(w_up_pad[0]), Glob(w_up_pad[1]), Glob(a_up_pad[0]), Glob(a_up_pad[1]), Glob(g_up_pad),
                Glob(wt["rwkv_k_k"]), Glob(wt["rwkv_k_a"]), Glob(ones64)]

    kk, w_f, b_f, kt_f, w_b, b_b, kt_b, g_rw = ew_forward(fn_rwkv_prepare, "rwkv_prepare", bsz, n_t, prep_ins(),
                                                           [(RW_W, f32)] * 8)
    rw_order = chunk_orders(t_c // SCAN_CHUNK, t_all // SCAN_CHUNK)
    ret_order = chunk_orders(t_c // RET_CHUNK, t_all // RET_CHUNK)
    scan_rows = [[(ps, 0), (kk, 0), (w_f, 0), (b_f, 0), (kt_f, 0)], [(ps, 0), (kk, 0), (w_b, 0), (b_b, 0), (kt_b, 0)]]
    v_heads = heads_to_rows(ps[..., 2 * RW_W:3 * RW_W])
    y_raw_f, y_raw_b, start_f, start_b = rwkv_scan_fwd(scan_rows, v_heads, rw_order, "rwkv_scan_fwd")
    y = [read_out_rows(y_raw_f, bsz), read_out_rows(y_raw_b, bsz)]
    o, ret_states = [], []
    for d in range(2):
        o_d, st_d = retention_fwd(px, cos, sin, ld_rows[d], ret_order[d], SCAN_DIRS[d][0], f"retention_fwd{d}")
        o.append(o_d), ret_states.append(st_d)

    def merge_ins(toff):
        return [Tiled(o[0], toff=toff), Tiled(o[1], toff=toff), Tiled(px, RET_W, 3, toff),
                Tiled(y[0], toff=toff), Tiled(y[1], toff=toff), Tiled(ps, RW_W, 0, toff), Tiled(kt_f, toff=toff),
                Tiled(ps, RW_W, 2, toff), Tiled(g_rw, toff=toff),
                Glob(wt["rwkv_r_k"]), Glob(wt["rwkv_ln_w"]), Glob(wt["rwkv_ln_b"]), Glob(ones64), Glob(ones128)]

    ret_out, rw_out = ew_forward(fn_merge, "merge_heads", bsz, n_xt, merge_ins(n_ct),
                                 [(RET_W, MXU_DTYPE), (RW_W, MXU_DTYPE)])
    merged = jnp.concatenate([ret_out, rw_out], axis=-1)
    mix = matmul(flat(merged), wt["w_out"], "nn", "proj_out").reshape(bsz, t_x, D_MODEL)
    resid_ins = lambda: [Tiled(x), Tiled(mix), lat("gate1"), lat("shift2"), lat("scale2"), Glob(wt["norm2_g"])]
    h1, n2 = ew_forward(fn_resid_norm_mod, "resid_norm2", bsz, n_xt, resid_ins(), [(D_MODEL, f32), (D_MODEL, MXU_DTYPE)])
    act = matmul(flat(n2), wt["w_ff1"], "nn", "ff1", MXU_DTYPE, wt["b_ff1"], relu2).reshape(bsz, t_x, D_FF)
    ff = matmul(flat(act), wt["w_ff2"], "nn", "ff2").reshape(bsz, t_x, D_MODEL)

    g = {}
    loss, dh1, dff, dgate2, g["b_ff2"], g["final_g"] = loss_and_grads(
        h1, ff, tgt, mod_lat["gate2"], wt["b_ff2"], wt["final_g"], bsz, n_xt)
    dact = matmul(flat(dff), wt["w_ff2"], "nt", "ff2_dx", MXU_DTYPE).reshape(bsz, t_x, D_FF)
    g["w_ff2"] = matmul(flat(act), flat(dff), "tn", "ff2_dw")
    du, g["b_ff1"] = relu2_backward(act, dact, "relu2_bwd")
    dn2 = matmul(flat(du), wt["w_ff1"], "nt", "ff1_dx").reshape(bsz, t_x, D_MODEL)
    g["w_ff1"] = matmul(flat(n2), flat(du), "tn", "ff1_dw")
    dx_res, dmix, dgate1, dshift2, dscale2, g["norm2_g"] = ew_backward(
        fn_resid_norm_mod, "resid_norm2_bwd", bsz, n_xt, resid_ins(), [Tiled(dh1), Tiled(dn2)], [True] * 6,
        {1: MXU_DTYPE})
    dmerged = matmul(flat(dmix), wt["w_out"], "nt", "proj_out_dx").reshape(bsz, t_x, D_MODEL)
    g["w_out"] = matmul(flat(merged), flat(dmix), "tn", "proj_out_dw")
    (do, dg_ret, dy, dr_m, dkt_m, dv_m, dg_rw, g["rwkv_r_k"], g["rwkv_ln_w"], g["rwkv_ln_b"]) = ew_backward(
        fn_merge, "merge_heads_bwd", bsz, n_xt, merge_ins(0),
        [Tiled(dmerged, RET_W, 0, -n_ct), Tiled(dmerged, RW_W, 1, -n_ct)],
        [True, False, True, True, False, True, True, True, True, True, True, True, False, False], lead=n_ct)

    dqkv, dld = [], []
    for d in range(2):
        *dqkv_d, dld_d = retention_bwd(do, px, ret_states[d], cos, sin, ld_rows[d], ret_order[d],
                                       SCAN_DIRS[d][0], f"retention_bwd{d}")
        dqkv.append(dqkv_d), dld.append(dld_d[0, :RET_HEADS])
    g["ret_log_decay"] = jnp.stack(dld)
    (dr_f, dkk_f, dw_f, db_f, dkt_f, dv_raw_f, dr_b, dkk_b, dw_b, db_b, dkt_b, dv_raw_b) = rwkv_scan_bwd(
        scan_rows, v_heads, heads_to_rows(dy), (start_f, start_b), rw_order, "rwkv_scan_bwd")
    dv_f, dv_b = read_out_rows(dv_raw_f, bsz), read_out_rows(dv_raw_b, bsz)
    prep_cts = [dkk_f + dkk_b, dw_f, db_f, dkt_f + dkt_m, dw_b, db_b, dkt_b, dg_rw]
    (dks, dlora, dw0_f, dw0_b, da0_f, da0_b, dwup_f, dwup_b, daup_f, daup_b, dgup, g["rwkv_k_k"],
     g["rwkv_k_a"]) = ew_backward(fn_rwkv_prepare, "rwkv_prepare_bwd", bsz, n_t, prep_ins(),
                                  [Tiled(c) for c in prep_cts], [True] * 13 + [False])
    g["rwkv_w0"] = jnp.concatenate([dw0_f, dw0_b], axis=0)
    g["rwkv_a0"] = jnp.concatenate([da0_f, da0_b], axis=0)
    g["rwkv_w_up"] = jnp.stack([dwup_f[:DECAY_LORA], dwup_b[:DECAY_LORA]])
    g["rwkv_a_up"] = jnp.stack([daup_f[DECAY_LORA:DECAY_LORA + AAA_LORA], daup_b[DECAY_LORA:DECAY_LORA + AAA_LORA]])
    g["rwkv_g_up"] = dgup[DECAY_LORA + AAA_LORA:]
    dps = jnp.concatenate([dr_f + dr_b + dr_m, dks, dv_f + dv_b + dv_m, dlora], axis=-1)
    dp_rw, g["rwkv_shift_mu"] = token_shift_bwd(dps, px_rw, wt["rwkv_shift_mu"], seg_first, seg_last)
    dpx = jnp.concatenate([(dqkv[0][k] + dqkv[1][k]).astype(MXU_DTYPE) for k in range(3)]
                          + [dg_ret.astype(MXU_DTYPE), dp_rw], axis=-1)
    dn1 = matmul(flat(dpx), wt["w_in"], "nt", "proj_in_dx").reshape(bsz, t_all, D_MODEL)
    g["w_in"] = matmul(flat(n1), flat(dpx), "tn", "proj_in_dw")
    dh, dshift1, dscale1, g["norm1_g"] = ew_backward(fn_norm_mod, "norm1_bwd", bsz, n_t, norm1_ins(), [Tiled(dn1)],
                                                     [True] * 4)
    grad_x = dh[:, t_c:] + dx_res
    zeros = jnp.zeros((D_MODEL,), f32)
    g["mod_x"] = jnp.stack([dshift1[:, 1, 0], dscale1[:, 1, 0], dgate1[:, 0, 0], dshift2[:, 0, 0], dscale2[:, 0, 0],
                            dgate2[:, 0, 0]], axis=1)
    g["mod_ctx"] = jnp.stack([dshift1[:, 0, 0].sum(0), dscale1[:, 0, 0].sum(0), zeros, zeros, zeros, zeros])
    return loss, grad_x, g


MESH_ID = pl.DeviceIdType.MESH
ALL_PEERS = [(dx, dy, dc) for dx in (0, 1) for dy in (0, 1) for dc in (0, 1)][1:]
CHIP_PEERS = [(1, 0, 0), (0, 1, 0), (1, 1, 0)]
CHIP_SLOTS = (0, 2, 4, 6)


def _mesh_pos():
    return lax.axis_index("x"), lax.axis_index("y"), lax.axis_index("c")


def _device_slot():
    x, y, c = _mesh_pos()
    return 4 * x + 2 * y + c


def sibling_swap(arrs, name, pieces=1):
    n = len(arrs)
    assert all(a.shape[0] % pieces == 0 for a in arrs)

    def body(*refs):
        in_refs, out_refs = refs[:n], refs[n:2 * n]
        send_sems, recv_sems = refs[2 * n:]
        x, y, c = _mesh_pos()
        copies = []
        for a in range(n):
            rows = arrs[a].shape[0] // pieces
            for q in range(pieces):
                part = pl.ds(q * rows, rows)
                cp = pltpu.make_async_remote_copy(
                    src_ref=in_refs[a].at[part], dst_ref=out_refs[a].at[part], send_sem=send_sems.at[a * pieces + q],
                    recv_sem=recv_sems.at[a * pieces + q], device_id=(x, y, 1 - c), device_id_type=MESH_ID)
                cp.start()
                copies.append(cp)
        for cp in copies:
            cp.wait()

    any_spec = pl.BlockSpec(memory_space=pl.ANY)
    res = pl.pallas_call(
        body, in_specs=[any_spec] * n, out_specs=[any_spec] * n,
        out_shape=[jax.ShapeDtypeStruct(a.shape, a.dtype) for a in arrs],
        scratch_shapes=[pltpu.SemaphoreType.DMA((n * pieces,)), pltpu.SemaphoreType.DMA((n * pieces,))],
        name=name)(*arrs)
    return list(res)


def exchange(arrs, gather, peers, name, pieces=1, by_chip=False, own=True):
    n, n_peers = len(arrs), len(peers)
    n_slots = N_SHARDS if by_chip else N_DEV
    slot = (lambda x, y, c: 2 * x + y) if by_chip else (lambda x, y, c: 4 * x + 2 * y + c)
    block_rows = [a.shape[0] if gather else a.shape[1] for a in arrs]
    assert all(r % pieces == 0 for r in block_rows), (block_rows, pieces)

    def body(*refs):
        in_refs, out_refs = refs[:n], refs[n:2 * n]
        send_sems, recv_sems, local_sems = refs[2 * n:]
        x, y, c = _mesh_pos()
        me = slot(x, y, c)
        copies, locals_ = [], []
        for a in range(n):
            if own:
                mine = in_refs[a] if gather else in_refs[a].at[me]
                loc = pltpu.make_async_copy(mine, out_refs[a].at[me], local_sems.at[a])
                loc.start()
                locals_.append(loc)
            for k, (dx, dy, dc) in enumerate(peers):
                peer = (1 - x if dx else x, 1 - y if dy else y, 1 - c if dc else c)
                src = in_refs[a] if gather else in_refs[a].at[slot(*peer)]
                for q in range(pieces):
                    part = pl.ds(q * (block_rows[a] // pieces), block_rows[a] // pieces)
                    sem = (a * n_peers + k) * pieces + q
                    cp = pltpu.make_async_remote_copy(
                        src_ref=src.at[part], dst_ref=out_refs[a].at[me, part], send_sem=send_sems.at[sem],
                        recv_sem=recv_sems.at[sem], device_id=peer, device_id_type=MESH_ID)
                    cp.start()
                    copies.append(cp)
        for cp in copies:
            cp.wait()
        for loc in locals_:
            loc.wait()

    any_spec = pl.BlockSpec(memory_space=pl.ANY)
    out_shape = [jax.ShapeDtypeStruct((n_slots,) + (a.shape if gather else a.shape[1:]), a.dtype) for a in arrs]
    n_sems = n * n_peers * pieces
    res = pl.pallas_call(
        body, in_specs=[any_spec] * n, out_specs=[any_spec] * n, out_shape=out_shape,
        scratch_shapes=[pltpu.SemaphoreType.DMA((n_sems,)), pltpu.SemaphoreType.DMA((n_sems,)),
                        pltpu.SemaphoreType.DMA((n,))],
        name=name)(*arrs)
    return list(res)


def add_arrays(parts, name, out_dtype=f32):
    r, c = parts[0].shape
    tr = r
    for cand in (512, 256, 128, 64, 32, 16):
        if r % cand == 0:
            tr = cand
            break

    def body(*refs):
        acc = refs[0][...].astype(f32)
        for p_ref in refs[1:-1]:
            acc = acc + p_ref[...].astype(f32)
        refs[-1][...] = acc.astype(out_dtype)

    spec = pl.BlockSpec((tr, c), lambda i: (i, 0))
    return pl.pallas_call(body, grid=(r // tr,), in_specs=[spec] * len(parts), out_specs=spec,
                          out_shape=jax.ShapeDtypeStruct((r, c), out_dtype),
                          compiler_params=_cparams(("parallel",)), name=name)(*parts)


def gather_two_level(arrs, name):
    n = len(arrs)
    per = 7

    def body(*refs):
        in_refs, out_refs = refs[:n], refs[n:2 * n]
        send_sems, recv_sems = refs[2 * n:]
        x, y, c = _mesh_pos()
        me, sibling = (x, y, c), (x, y, 1 - c)
        chips = [(1 - x, y), (x, 1 - y), (1 - x, 1 - y)]

        def copy(a, k, block, to, src=None):
            rows = out_refs[a].at[4 * block[0] + 2 * block[1] + block[2]]
            return pltpu.make_async_remote_copy(src_ref=rows if src is None else src, dst_ref=rows,
                                                send_sem=send_sems.at[a * per + k], recv_sem=recv_sems.at[a * per + k],
                                                device_id=to, device_id_type=MESH_ID)

        first, passed = [], []
        for a in range(n):
            first.append(copy(a, 0, me, sibling, src=in_refs[a]))
            first += [copy(a, 1 + j, me, (*chip, c), src=in_refs[a]) for j, chip in enumerate(chips)]
        for cp in first:
            cp.start()
        for a in range(n):
            for j, chip in enumerate(chips):
                copy(a, 1 + j, (*chip, c), me).wait_recv()
                fwd = copy(a, 4 + j, (*chip, c), sibling)
                fwd.start()
                passed.append(fwd)
        for a in range(n):
            copy(a, 0, sibling, me).wait_recv()
            for j, chip in enumerate(chips):
                copy(a, 4 + j, (*chip, 1 - c), me).wait_recv()
        for cp in first + passed:
            cp.wait_send()

    any_spec = pl.BlockSpec(memory_space=pl.ANY)
    res = pl.pallas_call(
        body, in_specs=[any_spec] * n, out_specs=[any_spec] * n,
        out_shape=[jax.ShapeDtypeStruct((N_DEV,) + a.shape, a.dtype) for a in arrs],
        scratch_shapes=[pltpu.SemaphoreType.DMA((n * per,)), pltpu.SemaphoreType.DMA((n * per,))],
        name=name)(*arrs)
    return list(res)


def sum_slots(parts, slots, name):
    _, r, c = parts.shape
    tr = r
    for cand in (512, 256, 128, 64, 32, 16, 8):
        if r % cand == 0 and cand * c * 4 * len(slots) <= 8 * 1024 * 1024:
            tr = cand
            break

    def body(p_ref, o_ref):
        acc = p_ref[slots[0]].astype(f32)
        for s in slots[1:]:
            acc = acc + p_ref[s].astype(f32)
        o_ref[...] = acc

    return pl.pallas_call(body, grid=(r // tr,), in_specs=[pl.BlockSpec((parts.shape[0], tr, c), lambda i: (0, i, 0))],
                          out_specs=pl.BlockSpec((tr, c), lambda i: (i, 0)),
                          out_shape=jax.ShapeDtypeStruct((r, c), f32),
                          compiler_params=_cparams(("parallel",)), name=name)(parts)


def column_sum(a, name):
    def body(a_ref, o_ref):
        o_ref[...] = jnp.sum(a_ref[...], axis=0, keepdims=True)

    return pl.pallas_call(body, out_shape=jax.ShapeDtypeStruct((1, a.shape[1]), f32), name=name)(a)


def adamw(w, g, m, v, name):
    r, c = w.shape
    tr = r
    for cand in (256, 128, 64, 32, 16, 8):
        if r % cand == 0:
            tr = cand
            break

    def body(w_ref, g_ref, m_ref, v_ref, d_ref, mo_ref, vo_ref):
        gv = g_ref[...]
        m_new = ADAM_B1 * m_ref[...] + (1.0 - ADAM_B1) * gv
        v_new = ADAM_B2 * v_ref[...] + (1.0 - ADAM_B2) * jnp.square(gv)
        m_hat = m_new / (1.0 - ADAM_B1 ** ADAM_STEP)
        v_hat = v_new / (1.0 - ADAM_B2 ** ADAM_STEP)
        d_ref[...] = -ADAM_LR * (m_hat / (jnp.sqrt(v_hat) + ADAM_EPS) + ADAM_WD * w_ref[...])
        mo_ref[...] = m_new
        vo_ref[...] = v_new

    spec = pl.BlockSpec((tr, c), lambda i: (i, 0))
    return pl.pallas_call(body, grid=(r // tr,), in_specs=[spec] * 4, out_specs=[spec] * 3,
                          out_shape=[jax.ShapeDtypeStruct((r, c), f32)] * 3,
                          compiler_params=_cparams(("parallel",)), name=name)(w, g, m, v)


def adaln_fwd(c_rows, w, b):
    def body(c_ref, w_ref, b_ref, o_ref):
        cv = c_ref[...]
        o_ref[...] = _mxu_dot(cv * jax.nn.sigmoid(cv), w_ref[...]) + b_ref[...]

    return pl.pallas_call(body, out_shape=jax.ShapeDtypeStruct((c_rows.shape[0], w.shape[1]), f32),
                          compiler_params=pltpu.CompilerParams(vmem_limit_bytes=VMEM_LIMIT), name="adaln_fwd")(c_rows, w, b)


def adaln_bwd(c_rows, dm, w):
    def body(c_ref, dm_ref, w_ref, gw_ref, ds_ref):
        cv = c_ref[...]
        gw_ref[...] = _dg(cv * jax.nn.sigmoid(cv), dm_ref[...], 0, 0)
        ds_ref[...] = _dg(dm_ref[...], w_ref[...], 1, 1)

    return pl.pallas_call(body, out_shape=[jax.ShapeDtypeStruct(w.shape, f32),
                                           jax.ShapeDtypeStruct(c_rows.shape, f32)],
                          compiler_params=pltpu.CompilerParams(vmem_limit_bytes=VMEM_LIMIT), name="adaln_bwd")(c_rows, dm, w)


def c_ctx_grad(parts, c_ctx_row):
    def body(p_ref, c_ref, o_ref):
        total = p_ref[0, 0:1, :]
        for s in range(1, N_SHARDS):
            total = total + p_ref[s, 0:1, :]
        _, vjp = jax.vjp(jax.nn.silu, c_ref[...])
        o_ref[...] = vjp(total)[0]

    return pl.pallas_call(body, out_shape=jax.ShapeDtypeStruct((1, D_MODEL), f32), name="c_ctx_grad")(parts, c_ctx_row)


PACK_W = 1024
PACK_ROWS = 8


def _pack(arrs):
    pieces, layout, r0 = [], [], 0
    for a in arrs:
        size = math.prod(a.shape)
        rows = -(-size // (PACK_W * PACK_ROWS)) * PACK_ROWS
        pieces.append(jnp.pad(a.reshape(-1).astype(f32), (0, rows * PACK_W - size)).reshape(rows, PACK_W))
        layout.append((r0, rows, a.shape))
        r0 += rows
    return jnp.concatenate(pieces, axis=0), layout


def _unpack(pack, layout, lead=()):
    n_lead = len(lead)
    outs = []
    for r0, rows, shape in layout:
        piece = pack[(slice(None),) * n_lead + (slice(r0, r0 + rows),)].reshape(lead + (-1,))
        outs.append(piece[..., :math.prod(shape)].reshape(lead + tuple(shape)))
    return outs


W_NAMES = ("c_ctx", "w_ada", "b_ada", "norm1_g", "norm2_g", "w_in", "ret_log_decay", "rwkv_shift_mu", "rwkv_w0",
           "rwkv_w_up", "rwkv_a0", "rwkv_a_up", "rwkv_g_up", "rwkv_k_k", "rwkv_k_a", "rwkv_r_k", "rwkv_ln_w",
           "rwkv_ln_b", "w_out", "w_ff1", "b_ff1", "w_ff2", "b_ff2", "final_g")
COL_SHARDED = ("w_in", "w_ff1")
ROW_SHARDED = ("w_out", "w_ff2")
LAST_SHARDED = ("rwkv_shift_mu", "rwkv_w0", "rwkv_w_up", "rwkv_a0", "rwkv_a_up", "rwkv_g_up")
REPLICATED = ("c_ctx", "b_ada", "norm1_g", "norm2_g", "ret_log_decay", "rwkv_k_k", "rwkv_k_a", "rwkv_r_k",
              "rwkv_ln_w", "rwkv_ln_b", "b_ff1", "b_ff2", "final_g")
N_SHARDS = 4


def _train_step(a):
    x, c, ctx, tgt = a["x"], a["c"], a["ctx"], a["loss_target"]
    bsz = x.shape[0]
    mx, my, mc = _mesh_pos()
    shard = 2 * mx + my
    dev = _device_slot()

    (c_all,) = exchange([jnp.pad(c, ((0, PACK_ROWS - bsz), (0, 0)))], True, ALL_PEERS, "gather_c")
    n_ex = N_DEV * bsz
    c_rows = jnp.concatenate([c_all[:, :bsz].reshape(n_ex, D_MODEL), a["c_ctx"][None, :],
                              jnp.zeros((PACK_ROWS - 1, D_MODEL), f32)], axis=0)
    ada_cols = a["w_ada"].shape[-1]
    b_ada_cols = lax.dynamic_slice_in_dim(a["b_ada"], shard * ada_cols, ada_cols, axis=1)
    mod_cols = adaln_fwd(c_rows, a["w_ada"][0], b_ada_cols)

    halves, small_shards = [], [a[n][0] for n in LAST_SHARDED]
    for n in COL_SHARDED + ROW_SHARDED:
        w = a[n][0].astype(MXU_DTYPE)
        half = w.shape[0] // 2
        halves.append(lax.dynamic_slice_in_dim(w, mc * half, half, axis=0))
    small_pack, small_layout = _pack(small_shards)
    own_blocks = [mod_cols] + halves + [small_pack]
    gathered = [lax.dynamic_update_index_in_dim(got, own, dev, 0)
                for got, own in zip(gather_two_level(own_blocks, "gather_weights"), own_blocks)]
    mod_all = jnp.stack([gathered[0][s] for s in CHIP_SLOTS], axis=1).reshape(c_rows.shape[0], -1)
    mod_x = lax.dynamic_slice_in_dim(mod_all, dev * bsz, bsz, axis=0).reshape(bsz, 6, D_MODEL)
    mod_ctx = mod_all[n_ex].reshape(6, D_MODEL)
    wt = {}
    for n, gth in zip(COL_SHARDED + ROW_SHARDED, gathered[1:5]):
        per_chip = gth.reshape(N_SHARDS, -1, gth.shape[-1])
        wt[n] = (per_chip.transpose(1, 0, 2).reshape(per_chip.shape[1], -1) if n in COL_SHARDED
                 else per_chip.reshape(-1, per_chip.shape[-1]))
    small_by_chip = _unpack(jnp.stack([gathered[5][s] for s in CHIP_SLOTS]), small_layout, (N_SHARDS,))
    for n, parts in zip(LAST_SHARDED, small_by_chip):
        wt[n] = jnp.concatenate([parts[s] for s in range(N_SHARDS)], axis=-1)
    for n in ("norm1_g", "norm2_g", "rwkv_k_k", "rwkv_k_a", "rwkv_r_k", "rwkv_ln_w", "rwkv_ln_b", "b_ff1", "b_ff2"):
        wt[n] = a[n]
    wt["ret_log_decay"] = a["ret_log_decay"][0]
    wt["final_g"] = a["final_g"][None, :]

    loss, grad_x, g = layer_step(x, ctx, tgt, mod_x, mod_ctx, wt)

    small_names = [n for n in REPLICATED if n not in ("c_ctx", "b_ada")]
    g_pack, g_layout = _pack([jnp.pad(loss, ((0, 0), (0, PACK_W - loss.shape[1])))] + [g[n] for n in small_names]
                             + [g["mod_x"], g["mod_ctx"]])
    (g_packs,) = gather_two_level([g_pack], "gather_small_grads")
    g_packs = lax.dynamic_update_index_in_dim(g_packs, g_pack, dev, 0)
    g_sum = _unpack(sum_slots(g_packs, tuple(range(N_DEV)), "sum_small_grads"), g_layout)
    loss_total = g_sum[0][0, 0]
    grads = dict(zip(small_names, g_sum[1:1 + len(small_names)]))
    dmod_ctx = g_sum[-1].reshape(1, -1)
    dmod_x = _unpack(g_packs, g_layout, (N_DEV,))[-2].reshape(n_ex, -1)
    dmod = jnp.concatenate([dmod_x, dmod_ctx, jnp.zeros((PACK_ROWS - 1, dmod_x.shape[1]), f32)], axis=0)
    grads["b_ada"] = column_sum(dmod, "b_ada_grad")
    dmod_cols = lax.dynamic_slice_in_dim(dmod, shard * ada_cols, ada_cols, axis=1)
    grads["w_ada"], dsilu = adaln_bwd(c_rows, dmod_cols, a["w_ada"][0])

    blocks = []
    for n in COL_SHARDED + ROW_SHARDED:
        gw = g[n]
        if n in COL_SHARDED:
            gw = gw.reshape(gw.shape[0], N_SHARDS, -1).transpose(1, 0, 2)
        blocks.append(gw.reshape(N_DEV, -1, gw.shape[-1]).astype(MXU_DTYPE))
    shard_packs = []
    for s in range(N_SHARDS):
        pieces_s = [lax.slice_in_dim(g[n], s * a[n].shape[-1], (s + 1) * a[n].shape[-1], axis=g[n].ndim - 1)
                    for n in LAST_SHARDED]
        pack_s, shard_layout = _pack(pieces_s)
        shard_packs.append(jnp.pad(pack_s, ((0, -pack_s.shape[0] % (2 * PACK_ROWS)), (0, 0))))
    blocks.append(jnp.stack(shard_packs).reshape(N_DEV, -1, PACK_W))
    scattered = COL_SHARDED + ROW_SHARDED + ("small_shards",)
    halves_of = lambda blk, core: lax.dynamic_index_in_dim(
        blk.reshape(N_SHARDS, 2, *blk.shape[1:]), core, axis=1, keepdims=False).reshape(-1, blk.shape[-1])
    from_sibling = sibling_swap([halves_of(blk, 1 - mc) for blk in blocks], "prereduce_swap")
    chip_sums = [add_arrays([halves_of(blk, mc), got], f"prereduce_{n}", blk.dtype).reshape(N_SHARDS, -1, blk.shape[-1])
                 for n, blk, got in zip(scattered, blocks, from_sibling)]
    dsilu_rows = jnp.broadcast_to(jnp.pad(dsilu[n_ex:n_ex + 1], ((0, PACK_ROWS - 1), (0, 0)))[None],
                                  (N_SHARDS, PACK_ROWS, D_MODEL))
    to_chips = [dsilu_rows] + chip_sums
    received = exchange(to_chips, False, CHIP_PEERS, "scatter_big_grads", by_chip=True, own=False)
    received = [lax.dynamic_update_index_in_dim(got, lax.dynamic_index_in_dim(sent, shard, 0, keepdims=False), shard, 0)
                for got, sent in zip(received, to_chips)]
    grads["c_ctx"] = c_ctx_grad(received[0], a["c_ctx"][None, :])
    half_sums = [sum_slots(p, tuple(range(N_SHARDS)), f"sum_{n}") for n, p in zip(scattered, received[1:])]
    other_halves = sibling_swap(half_sums, "swap_halves")
    for n, mine, other in zip(scattered, half_sums, other_halves):
        rows = mine.shape[0]
        whole = jnp.zeros((2 * rows, mine.shape[1]), f32)
        whole = lax.dynamic_update_slice_in_dim(whole, mine, mc * rows, axis=0)
        grads[n] = lax.dynamic_update_slice_in_dim(whole, other, (1 - mc) * rows, axis=0)
    grads.update(zip(LAST_SHARDED, _unpack(grads.pop("small_shards"), shard_layout)))

    out_g, out_d, out_m, out_v = {}, {}, {}, {}
    for n in ("w_ada",) + COL_SHARDED + ROW_SHARDED:
        out_g[n] = grads[n].reshape(a[n].shape)
        two_d = lambda z: z.reshape(-1, z.shape[-1])
        d, m, v = adamw(two_d(a[n]), two_d(out_g[n]), two_d(a["m_" + n]), two_d(a["v_" + n]), f"adamw_{n}")
        out_d[n], out_m[n], out_v[n] = d.reshape(a[n].shape), m.reshape(a[n].shape), v.reshape(a[n].shape)
    rest = REPLICATED + LAST_SHARDED
    for n in rest:
        out_g[n] = grads[n].reshape(a[n].shape)
    packs = [_pack([src[n] for n in rest])[0] for src in
             ({n: a[n] for n in rest}, out_g, {n: a["m_" + n] for n in rest}, {n: a["v_" + n] for n in rest})]
    _, rest_layout = _pack([a[n] for n in rest])
    for dst, pack in zip((out_d, out_m, out_v), adamw(*packs, "adamw_small")):
        dst.update(zip(rest, _unpack(pack, rest_layout)))
    return (loss_total, grad_x, *[out_g[n] for n in W_NAMES], *[out_d[n] for n in W_NAMES],
            *[out_m[n] for n in W_NAMES], *[out_v[n] for n in W_NAMES])


def kernel(x, c, ctx, c_ctx, w_ada, b_ada, norm1_g, norm2_g, w_in, ret_log_decay, rwkv_shift_mu, rwkv_w0, rwkv_w_up, rwkv_a0, rwkv_a_up, rwkv_g_up, rwkv_k_k, rwkv_k_a, rwkv_r_k, rwkv_ln_w, rwkv_ln_b, w_out, w_ff1, b_ff1, w_ff2, b_ff2, final_g, loss_target, m_c_ctx, m_w_ada, m_b_ada, m_norm1_g, m_norm2_g, m_w_in, m_ret_log_decay, m_rwkv_shift_mu, m_rwkv_w0, m_rwkv_w_up, m_rwkv_a0, m_rwkv_a_up, m_rwkv_g_up, m_rwkv_k_k, m_rwkv_k_a, m_rwkv_r_k, m_rwkv_ln_w, m_rwkv_ln_b, m_w_out, m_w_ff1, m_b_ff1, m_w_ff2, m_b_ff2, m_final_g, v_c_ctx, v_w_ada, v_b_ada, v_norm1_g, v_norm2_g, v_w_in, v_ret_log_decay, v_rwkv_shift_mu, v_rwkv_w0, v_rwkv_w_up, v_rwkv_a0, v_rwkv_a_up, v_rwkv_g_up, v_rwkv_k_k, v_rwkv_k_a, v_rwkv_r_k, v_rwkv_ln_w, v_rwkv_ln_b, v_w_out, v_w_ff1, v_b_ff1, v_w_ff2, v_b_ff2, v_final_g):
    return _train_step(dict(locals()))
```

```python
import functools
import math

import jax
import jax.numpy as jnp
from jax import lax
from jax.experimental import pallas as pl
from jax.experimental.pallas import tpu as pltpu

f32 = jnp.float32
MXU_DTYPE = jnp.bfloat16

D_MODEL = 1024
RET_W = 512
RET_HEADS = 4
RET_DH = 128
RET_CHUNK = 128
RW_W = 512
RW_N = 64
DECAY_LORA = 64
AAA_LORA = 64
GATE_LORA = 128
LORA_W = DECAY_LORA + AAA_LORA + GATE_LORA
D_FF = 4096
RET_COLS = 4 * RET_W
SHIFT_COLS = 3 * RW_W + LORA_W
IN_COLS = RET_COLS + SHIFT_COLS
GRID_W = 64
ROPE_BASE = 10000.0
NORM_EPS = 1e-6
GN_EPS = 64e-5
W_DECAY_SCALE = math.exp(-0.5)
ADAM_LR, ADAM_B1, ADAM_B2, ADAM_EPS, ADAM_WD, ADAM_STEP = 0.001, 0.9, 0.999, 1e-08, 0.01, 10

TOK_TILE = 256
MATMUL_TILE = 1024
SCAN_CHUNK = 16
SCAN_UNROLL = SCAN_CHUNK
N_DEV = 8
V7X_VMEM_BYTES = 64 * 1024 * 1024
VMEM_LIMIT = V7X_VMEM_BYTES * 7 // 8


def _cparams(sem):
    return pltpu.CompilerParams(dimension_semantics=sem, vmem_limit_bytes=VMEM_LIMIT)


def _tile(n, cap):
    best = None
    for t in range(128, min(n, cap) + 1, 128):
        if n % t == 0:
            best = t
    return best if best is not None else n


def matmul(a, b, mode, name, out_dtype=f32, bias=None, finish=None):
    if mode == "nn":
        (m, k), (k2, n) = a.shape, b.shape
    elif mode == "nt":
        (m, k), (n, k2) = a.shape, b.shape
    else:
        (k, m), (k2, n) = a.shape, b.shape
    assert k == k2, (a.shape, b.shape, mode)
    tm, tn, tk = _tile(m, MATMUL_TILE), _tile(n, MATMUL_TILE), _tile(k, MATMUL_TILE)
    nk = k // tk
    dims = {"nn": ((1,), (0,)), "nt": ((1,), (1,)), "tn": ((0,), (0,))}[mode]

    def body(a_ref, b_ref, *rest):
        o_ref, acc_ref = rest[-2:]
        kk = pl.program_id(2)

        @pl.when(kk == 0)
        def _():
            acc_ref[...] = jnp.zeros_like(acc_ref)

        acc_ref[...] += lax.dot_general(a_ref[...].astype(MXU_DTYPE), b_ref[...].astype(MXU_DTYPE),
                                        (dims, ((), ())), preferred_element_type=f32)

        @pl.when(kk == nk - 1)
        def _():
            res = acc_ref[...]
            if bias is not None:
                res = res + rest[0][...]
            if finish is not None:
                res = finish(res)
            o_ref[...] = res.astype(o_ref.dtype)

    if mode == "nn":
        a_spec = pl.BlockSpec((tm, tk), lambda i, j, q: (i, q))
        b_spec = pl.BlockSpec((tk, tn), lambda i, j, q: (q, j))
    elif mode == "nt":
        a_spec = pl.BlockSpec((tm, tk), lambda i, j, q: (i, q))
        b_spec = pl.BlockSpec((tn, tk), lambda i, j, q: (j, q))
    else:
        a_spec = pl.BlockSpec((tk, tm), lambda i, j, q: (q, i))
        b_spec = pl.BlockSpec((tk, tn), lambda i, j, q: (q, j))
    extra_specs = [] if bias is None else [pl.BlockSpec((1, tn), lambda i, j, q: (0, j))]
    extra = [] if bias is None else [bias]
    return pl.pallas_call(
        body, grid=(m // tm, n // tn, nk), in_specs=[a_spec, b_spec] + extra_specs,
        out_specs=pl.BlockSpec((tm, tn), lambda i, j, q: (i, j)),
        out_shape=jax.ShapeDtypeStruct((m, n), out_dtype),
        scratch_shapes=[pltpu.VMEM((tm, tn), f32)],
        compiler_params=_cparams(("parallel", "parallel", "arbitrary")), name=name)(a, b, *extra)


class Tiled:
    def __init__(self, arr, w=None, cidx=0, toff=0):
        self.arr, self.w, self.cidx, self.toff = arr, (arr.shape[-1] if w is None else w), cidx, toff

    def spec(self):
        cidx, toff = self.cidx, self.toff
        return pl.BlockSpec((None, TOK_TILE, self.w), lambda b, i: (b, jnp.maximum(i + toff, 0), cidx))


class Seg:
    def __init__(self, arr, seg, first):
        self.arr, self.seg, self.first = arr, seg, first

    def spec(self):
        seg = self.seg
        return pl.BlockSpec((None, None, 1, self.arr.shape[-1]), lambda b, i: (b, seg(i), 0, 0))


class Glob:
    def __init__(self, arr):
        self.arr = arr

    def spec(self):
        return pl.BlockSpec(self.arr.shape, lambda b, i: (0,) * self.arr.ndim)


def ew_forward(fn, name, bsz, n_tiles, ins, outs):
    n_in = len(ins)

    def body(*refs):
        res = fn(*[r[...] for r in refs[:n_in]])
        for o_ref, o in zip(refs[n_in:], res):
            o_ref[...] = o.astype(o_ref.dtype)

    out_specs = [pl.BlockSpec((None, TOK_TILE, w), lambda b, i: (b, i, 0)) for w, _ in outs]
    out_shape = [jax.ShapeDtypeStruct((bsz, n_tiles * TOK_TILE, w), dt) for w, dt in outs]
    return pl.pallas_call(body, grid=(bsz, n_tiles), in_specs=[d.spec() for d in ins], out_specs=out_specs,
                          out_shape=out_shape, compiler_params=_cparams(("parallel", "parallel")), name=name)(
        *[d.arr for d in ins])


def ew_backward(fn, name, bsz, n_tiles, ins, cts, want, grad_dtypes=None, lead=0):
    n_in, n_ct = len(ins), len(cts)
    diff = [k for k in range(n_in) if want[k]]
    grad_dtypes = grad_dtypes or {}
    assert lead == 0 or not any(isinstance(ins[k], Seg) for k in diff)

    def body(*refs):
        b, i = pl.program_id(0), pl.program_id(1)
        g_refs = refs[n_in + n_ct:]

        def tile_grads():
            vals = [r[...] for r in refs[:n_in]]
            ct_vals = tuple(r[...].astype(f32) for r in refs[n_in:n_in + n_ct])

            def f(*dvals):
                full = list(vals)
                for k, v in zip(diff, dvals):
                    full[k] = v
                return tuple(fn(*full))

            _, vjp = jax.vjp(f, *[vals[k] for k in diff])
            grads = vjp(ct_vals)
            for k, g_ref, g in zip(diff, g_refs, grads):
                d = ins[k]
                if isinstance(d, Tiled):
                    g_ref[...] = g.astype(g_ref.dtype)
                else:
                    zero = d.first(i) if isinstance(d, Seg) else jnp.logical_and(b == 0, i == lead)

                    @pl.when(zero)
                    def _(g_ref=g_ref):
                        g_ref[...] = jnp.zeros_like(g_ref)

                    g_ref[...] += g

        if lead == 0:
            tile_grads()
        else:
            pl.when(i >= lead)(tile_grads)

            @pl.when(i < lead)
            def _():
                for k, g_ref in zip(diff, g_refs):
                    if isinstance(ins[k], Tiled):
                        g_ref[...] = jnp.zeros_like(g_ref)

    out_specs, out_shape = [], []
    for k in diff:
        d = ins[k]
        if isinstance(d, Tiled):
            out_specs.append(pl.BlockSpec((None, TOK_TILE, d.w), lambda b, i: (b, i, 0)))
            out_shape.append(jax.ShapeDtypeStruct((bsz, (n_tiles + lead) * TOK_TILE, d.w), grad_dtypes.get(k, f32)))
        else:
            out_specs.append(d.spec())
            out_shape.append(jax.ShapeDtypeStruct(d.arr.shape, f32))
    return pl.pallas_call(body, grid=(bsz, n_tiles + lead),
                          in_specs=[d.spec() for d in ins] + [c.spec() for c in cts],
                          out_specs=out_specs, out_shape=out_shape,
                          compiler_params=_cparams(("arbitrary", "arbitrary")), name=name)(
        *[d.arr for d in ins], *[c.arr for c in cts])


@jax.custom_vjp
def _mxu_dot(a, b):
    return jnp.dot(a.astype(MXU_DTYPE), b.astype(MXU_DTYPE), preferred_element_type=f32)


def _mxu_dot_fwd(a, b):
    return _mxu_dot(a, b), (a, b)


def _mxu_dot_bwd(res, ct):
    a, b = res
    ct = ct.astype(MXU_DTYPE)
    da = lax.dot_general(ct, b.astype(MXU_DTYPE), (((1,), (1,)), ((), ())), preferred_element_type=f32)
    db = lax.dot_general(a.astype(MXU_DTYPE), ct, (((0,), (0,)), ((), ())), preferred_element_type=f32)
    return da, db


_mxu_dot.defvjp(_mxu_dot_fwd, _mxu_dot_bwd)


def _split_dot_impl(x, ones_mat):
    hi = x.astype(MXU_DTYPE)
    lo = (x - hi.astype(f32)).astype(MXU_DTYPE)
    return jnp.dot(hi, ones_mat, preferred_element_type=f32) + jnp.dot(lo, ones_mat, preferred_element_type=f32)


@jax.custom_vjp
def _split_dot(x, ones_mat):
    return _split_dot_impl(x, ones_mat)


def _split_dot_fwd(x, ones_mat):
    return _split_dot_impl(x, ones_mat), ones_mat


def _split_dot_bwd(ones_mat, ct):
    return _split_dot_impl(ct, ones_mat), None


_split_dot.defvjp(_split_dot_fwd, _split_dot_bwd)


def _block_ones(n, group):
    idx = jnp.arange(n) // group
    return (idx[:, None] == idx[None, :]).astype(MXU_DTYPE)


def _rms(x, g):
    return x * lax.rsqrt(jnp.mean(x * x, axis=-1, keepdims=True) + NORM_EPS) * g


def fn_norm_mod(h, shift, scale, g):
    return (_rms(h, g) * (1.0 + scale) + shift,)


def fn_rwkv_prepare(ks, lora, w0_f, w0_b, a0_f, a0_b, w_up_f, w_up_b, a_up_f, a_up_b, g_up, k_k, k_a, ones64):
    kkr = ks * k_k
    kk = kkr * lax.rsqrt(_split_dot(kkr * kkr, ones64) + 1e-12)
    outs = [kk]
    th = jnp.tanh(lora)
    for w0, a0, w_up, a_up in ((w0_f, a0_f, w_up_f, a_up_f), (w0_b, a0_b, w_up_b, a_up_b)):
        w = jnp.exp(-W_DECAY_SCALE * jax.nn.sigmoid(w0 + _mxu_dot(th, w_up)))
        a = jax.nn.sigmoid(a0 + _mxu_dot(lora, a_up))
        kt = ks * (1.0 + (a - 1.0) * k_a)
        outs += [w, a * kk, kt]
    outs.append(_mxu_dot(jax.nn.sigmoid(lora), g_up))
    return tuple(outs)


def fn_merge(o_f, o_b, g_ret, y_f, y_b, r, kt_f, v, g_rw, r_k, ln_w, ln_b, ones64, ones128):
    o = o_f + o_b
    ret = o * lax.rsqrt(_split_dot(o * o, ones128) * (1.0 / RET_DH) + NORM_EPS) * (g_ret * jax.nn.sigmoid(g_ret))
    y = y_f + y_b
    mean = _split_dot(y, ones64) * (1.0 / RW_N)
    yc = y - mean
    var = _split_dot(yc * yc, ones64) * (1.0 / RW_N)
    y_n = yc * lax.rsqrt(var + GN_EPS) * ln_w + ln_b
    bonus = _split_dot(r * kt_f * r_k, ones64) * v
    return ret, (y_n + bonus) * g_rw


def fn_resid_norm_mod(x, mix, gate, shift, scale, g):
    h1 = x + gate * mix
    return h1, _rms(h1, g) * (1.0 + scale) + shift


def relu2(z):
    return jnp.square(jnp.maximum(z, 0.0))


def relu2_backward(act, dact, name):
    bsz, n_tok, width = act.shape

    def body(a_ref, d_ref, du_ref, db_ref):
        du = d_ref[...].astype(f32) * (2.0 * jnp.sqrt(a_ref[...].astype(f32)))
        du_ref[...] = du.astype(du_ref.dtype)

        @pl.when(jnp.logical_and(pl.program_id(0) == 0, pl.program_id(1) == 0))
        def _():
            db_ref[...] = jnp.zeros_like(db_ref)

        db_ref[...] += jnp.sum(du, axis=0, keepdims=True)

    tile = pl.BlockSpec((None, TOK_TILE, width), lambda b, i: (b, i, 0))
    row = pl.BlockSpec((1, width), lambda b, i: (0, 0))
    return pl.pallas_call(body, grid=(bsz, n_tok // TOK_TILE), in_specs=[tile, tile], out_specs=[tile, row],
                          out_shape=[jax.ShapeDtypeStruct(act.shape, MXU_DTYPE), jax.ShapeDtypeStruct((1, width), f32)],
                          compiler_params=_cparams(("arbitrary", "arbitrary")), name=name)(act, dact)


def fn_loss(h1, f, tgt, gate, b2, g):
    y = _rms(h1 + gate * (f + b2), g)
    err = jnp.square(y - tgt)
    return 0.5 * jnp.sum(jnp.mean(err, axis=-1, keepdims=True), axis=0, keepdims=True)


def loss_and_grads(h1, f, tgt, gate, b2, g, bsz, n_tiles):
    def body(h1_ref, f_ref, t_ref, gate_ref, b2_ref, g_ref, loss_ref, dh1_ref, df_ref, dgate_ref, db2_ref, dg_ref):
        b, i = pl.program_id(0), pl.program_id(1)
        tgt_v = t_ref[...]
        loss, vjp = jax.vjp(lambda a, c, e, p, q: fn_loss(a, c, tgt_v, e, p, q),
                            h1_ref[...], f_ref[...], gate_ref[...], b2_ref[...], g_ref[...])
        dh1, df, dgate, db2, dg = vjp(jnp.ones((1, 1), f32))
        dh1_ref[...] = dh1
        df_ref[...] = df.astype(df_ref.dtype)

        @pl.when(i == 0)
        def _():
            dgate_ref[...] = jnp.zeros_like(dgate_ref)

        @pl.when(jnp.logical_and(b == 0, i == 0))
        def _():
            loss_ref[...] = jnp.zeros_like(loss_ref)
            db2_ref[...] = jnp.zeros_like(db2_ref)
            dg_ref[...] = jnp.zeros_like(dg_ref)

        dgate_ref[...] += dgate
        db2_ref[...] += db2
        dg_ref[...] += dg
        loss_ref[...] += jnp.broadcast_to(loss, loss_ref.shape)

    tile = pl.BlockSpec((None, TOK_TILE, D_MODEL), lambda b, i: (b, i, 0))
    row = pl.BlockSpec((1, D_MODEL), lambda b, i: (0, 0))
    seg = pl.BlockSpec((None, None, 1, D_MODEL), lambda b, i: (b, 0, 0, 0))
    t_tok = n_tiles * TOK_TILE
    return pl.pallas_call(
        body, grid=(bsz, n_tiles), in_specs=[tile, tile, tile, seg, row, row],
        out_specs=[pl.BlockSpec((1, 128), lambda b, i: (0, 0)), tile, tile, seg, row, row],
        out_shape=[jax.ShapeDtypeStruct((1, 128), f32), jax.ShapeDtypeStruct((bsz, t_tok, D_MODEL), f32),
                   jax.ShapeDtypeStruct((bsz, t_tok, D_MODEL), MXU_DTYPE),
                   jax.ShapeDtypeStruct((bsz, 1, 1, D_MODEL), f32),
                   jax.ShapeDtypeStruct((1, D_MODEL), f32), jax.ShapeDtypeStruct((1, D_MODEL), f32)],
        compiler_params=_cparams(("arbitrary", "arbitrary")), name="loss_and_grads")(h1, f, tgt, gate, b2, g)


SHIFT_BLOCK = SHIFT_COLS
HALO_ROWS = 8


def _shift_specs(n_tok, col0):
    per_tile = TOK_TILE // HALO_ROWS
    last = n_tok // HALO_ROWS - 1
    tile = pl.BlockSpec((None, TOK_TILE, SHIFT_BLOCK), lambda j, b, i: (b, i, col0 + j))
    prev = pl.BlockSpec((None, HALO_ROWS, SHIFT_BLOCK),
                        lambda j, b, i: (b, jnp.maximum(i * per_tile - 1, 0), col0 + j))
    nxt = pl.BlockSpec((None, HALO_ROWS, SHIFT_BLOCK),
                       lambda j, b, i: (b, jnp.minimum((i + 1) * per_tile, last), col0 + j))
    return tile, prev, nxt


def _shifted(p, prev_ref, next_ref, is_first, is_last):
    row = lax.broadcasted_iota(jnp.int32, p.shape, 0)
    prev_row = jnp.where(is_first, 0.0, prev_ref[HALO_ROWS - 1:HALO_ROWS, :].astype(f32))
    next_row = jnp.where(is_last, 0.0, next_ref[0:1, :].astype(f32))
    prev = jnp.where(row == 0, prev_row, pltpu.roll(p, 1, axis=0))
    nxt = jnp.where(row == TOK_TILE - 1, next_row, pltpu.roll(p, TOK_TILE - 1, axis=0))
    return prev, nxt


def token_shift(px, mu, seg_first, seg_last):
    bsz, n_tok, _ = px.shape
    n_tiles = n_tok // TOK_TILE

    def body(p_ref, prev_ref, next_ref, mu_ref, o_ref):
        i = pl.program_id(2)
        p = p_ref[...]
        prev, nxt = _shifted(p, prev_ref, next_ref, seg_first(i), seg_last(i))
        o_ref[...] = p + mu_ref[0:1, :] * (prev - p) + mu_ref[1:2, :] * (nxt - p)

    tile, prev, nxt = _shift_specs(n_tok, 0)
    return pl.pallas_call(
        body, grid=(SHIFT_COLS // SHIFT_BLOCK, bsz, n_tiles),
        in_specs=[tile, prev, nxt, pl.BlockSpec((2, SHIFT_BLOCK), lambda j, b, i: (0, j))],
        out_specs=pl.BlockSpec((None, TOK_TILE, SHIFT_BLOCK), lambda j, b, i: (b, i, j)),
        out_shape=jax.ShapeDtypeStruct((bsz, n_tok, SHIFT_COLS), f32),
        compiler_params=_cparams(("parallel", "parallel", "parallel")), name="token_shift")(px, px, px, mu)


def token_shift_bwd(dps, px, mu, seg_first, seg_last):
    bsz, n_tok, _ = px.shape
    n_tiles = n_tok // TOK_TILE

    def body(d_ref, dprev_ref, dnext_ref, p_ref, prev_ref, next_ref, mu_ref, dp_ref, dmu_ref):
        b, i = pl.program_id(1), pl.program_id(2)
        first, last = seg_first(i), seg_last(i)
        d, p = d_ref[...], p_ref[...]
        d_prev, d_next = _shifted(d, dprev_ref, dnext_ref, first, last)
        p_prev, p_next = _shifted(p, prev_ref, next_ref, first, last)
        mu0, mu1 = mu_ref[0:1, :], mu_ref[1:2, :]
        dp_ref[...] = (d + mu0 * (d_next - d) + mu1 * (d_prev - d)).astype(dp_ref.dtype)

        @pl.when(jnp.logical_and(b == 0, i == 0))
        def _():
            dmu_ref[...] = jnp.zeros_like(dmu_ref)

        dmu_ref[0:1, :] += jnp.sum(d * (p_prev - p), axis=0, keepdims=True)
        dmu_ref[1:2, :] += jnp.sum(d * (p_next - p), axis=0, keepdims=True)

    dtile, dprev, dnext = _shift_specs(n_tok, 0)
    tile, prev, nxt = _shift_specs(n_tok, 0)
    mu_spec = pl.BlockSpec((2, SHIFT_BLOCK), lambda j, b, i: (0, j))
    return pl.pallas_call(
        body, grid=(SHIFT_COLS // SHIFT_BLOCK, bsz, n_tiles),
        in_specs=[dtile, dprev, dnext, tile, prev, nxt, mu_spec],
        out_specs=[pl.BlockSpec((None, TOK_TILE, SHIFT_BLOCK), lambda j, b, i: (b, i, j)), mu_spec],
        out_shape=[jax.ShapeDtypeStruct((bsz, n_tok, SHIFT_COLS), MXU_DTYPE),
                   jax.ShapeDtypeStruct((2, SHIFT_COLS), f32)],
        compiler_params=_cparams(("arbitrary", "arbitrary", "arbitrary")), name="token_shift_bwd")(
        dps, dps, dps, px, px, px, mu)


def _dg(a, b, ca, cb):
    return lax.dot_general(a.astype(MXU_DTYPE), b.astype(MXU_DTYPE), (((ca,), (cb,)), ((), ())),
                           preferred_element_type=f32)


@jax.custom_vjp
def _mm_nt(a, b):
    return _dg(a, b, 1, 1)


_mm_nt.defvjp(lambda a, b: (_dg(a, b, 1, 1), (a, b)),
              lambda res, ct: (_dg(ct, res[1], 1, 0), _dg(ct, res[0], 0, 0)))


@jax.custom_vjp
def _mm_tn(a, b):
    return _dg(a, b, 0, 0)


_mm_tn.defvjp(lambda a, b: (_dg(a, b, 0, 0), (a, b)),
              lambda res, ct: (_dg(res[1], ct, 1, 1), _dg(res[0], ct, 1, 0)))


ROTARY_PAIR = RET_DH // 4


def _swap_pairs_impl(t):
    lane = lax.broadcasted_iota(jnp.int32, t.shape, 1)
    return jnp.where(lane % (2 * ROTARY_PAIR) < ROTARY_PAIR, pltpu.roll(t, RET_DH - ROTARY_PAIR, axis=1),
                     pltpu.roll(t, ROTARY_PAIR, axis=1))


@jax.custom_vjp
def _swap_pairs(t):
    return _swap_pairs_impl(t)


_swap_pairs.defvjp(lambda t: (_swap_pairs_impl(t), None), lambda _, ct: (_swap_pairs_impl(ct),))


def _ret_chunk(state, q_raw, k_raw, v, cos, sin, ld_row, head, reverse):
    c = RET_CHUNK
    lane = lax.broadcasted_iota(jnp.int32, ld_row.shape, 1)
    lg = -jnp.exp(jnp.sum(jnp.where(lane == head, ld_row, 0.0), axis=-1, keepdims=True))
    rot = lambda t: t * cos + _swap_pairs(t) * sin
    q = rot(q_raw)
    k = rot(k_raw) * (RET_DH ** -0.5)
    ti = lax.broadcasted_iota(jnp.int32, (c, 1), 0).astype(f32)
    tj = lax.broadcasted_iota(jnp.int32, (1, c), 1).astype(f32)
    if not reverse:
        dist, mask, q_exp, k_exp = ti - tj, (ti - tj) >= 0, ti + 1.0, c - 1.0 - ti
    else:
        dist, mask, q_exp, k_exp = tj - ti, (tj - ti) > 0, c - ti, ti
    decay = jnp.where(mask, jnp.exp(lg * jnp.maximum(dist, 0.0)), 0.0)
    scores = _mm_nt(q, k) * decay
    out = _mxu_dot(scores, v) + _mxu_dot(q * jnp.exp(lg * q_exp), state)
    new_state = state * jnp.exp(lg * c) + _mm_tn(k * jnp.exp(lg * k_exp), v)
    return out, new_state


def _ret_specs(order):
    qkv = [pl.BlockSpec((None, RET_CHUNK, RET_W), functools.partial(lambda b, i, col: (b, order(i), col), col=col))
           for col in range(3)]
    tab = pl.BlockSpec((RET_CHUNK, RET_DH), lambda b, i: (order(i), 0))
    ld = pl.BlockSpec((1, RET_DH), lambda b, i: (0, 0))
    return qkv, tab, ld


def retention_fwd(px, cos, sin, ld_row, order, reverse, name):
    bsz, n_tok, _ = px.shape
    n_ch = n_tok // RET_CHUNK

    def body(q_ref, k_ref, v_ref, cos_ref, sin_ref, ld_ref, o_ref, sv_ref, st_ref):
        @pl.when(pl.program_id(1) == 0)
        def _():
            st_ref[...] = jnp.zeros_like(st_ref)

        for h in range(RET_HEADS):
            sl = slice(h * RET_DH, (h + 1) * RET_DH)
            s = st_ref[h]
            sv_ref[h] = s
            o, s_new = _ret_chunk(s, q_ref[:, sl], k_ref[:, sl], v_ref[:, sl], cos_ref[...], sin_ref[...],
                                  ld_ref[...], h, reverse)
            o_ref[:, sl] = o
            st_ref[h] = s_new

    qkv, tab, ld = _ret_specs(order)
    return pl.pallas_call(
        body, grid=(bsz, n_ch), in_specs=[*qkv, tab, tab, ld],
        out_specs=[pl.BlockSpec((None, RET_CHUNK, RET_W), lambda b, i: (b, order(i), 0)),
                   pl.BlockSpec((None, None, RET_HEADS, RET_DH, RET_DH), lambda b, i: (b, i, 0, 0, 0))],
        out_shape=[jax.ShapeDtypeStruct((bsz, n_tok, RET_W), f32),
                   jax.ShapeDtypeStruct((bsz, n_ch, RET_HEADS, RET_DH, RET_DH), f32)],
        scratch_shapes=[pltpu.VMEM((RET_HEADS, RET_DH, RET_DH), f32)],
        compiler_params=_cparams(("parallel", "arbitrary")), name=name)(px, px, px, cos, sin, ld_row)


def retention_bwd(do, px, states, cos, sin, ld_row, order, reverse, name):
    bsz, n_tok, _ = px.shape
    n_ch = n_tok // RET_CHUNK
    back = lambda i: order(n_ch - 1 - i)

    def body(do_ref, q_ref, k_ref, v_ref, sv_ref, cos_ref, sin_ref, ld_ref,
             dq_ref, dk_ref, dv_ref, dld_ref, dst_ref):
        b, i = pl.program_id(0), pl.program_id(1)

        @pl.when(i == 0)
        def _():
            dst_ref[...] = jnp.zeros_like(dst_ref)

        @pl.when(jnp.logical_and(b == 0, i == 0))
        def _():
            dld_ref[...] = jnp.zeros_like(dld_ref)

        cos_v, sin_v = cos_ref[...], sin_ref[...]
        for h in range(RET_HEADS):
            sl = slice(h * RET_DH, (h + 1) * RET_DH)
            f = lambda s, q, k, v, ld, h=h: _ret_chunk(s, q, k, v, cos_v, sin_v, ld, h, reverse)
            _, vjp = jax.vjp(f, sv_ref[h], q_ref[:, sl], k_ref[:, sl], v_ref[:, sl], ld_ref[...])
            ds, dq, dk, dv, dld = vjp((do_ref[:, sl], dst_ref[h]))
            dst_ref[h] = ds
            dq_ref[:, sl] = dq
            dk_ref[:, sl] = dk
            dv_ref[:, sl] = dv
            dld_ref[...] += dld

    qkv, tab, ld = _ret_specs(back)
    tok = pl.BlockSpec((None, RET_CHUNK, RET_W), lambda b, i: (b, back(i), 0))
    return pl.pallas_call(
        body, grid=(bsz, n_ch),
        in_specs=[tok, *qkv,
                  pl.BlockSpec((None, None, RET_HEADS, RET_DH, RET_DH), lambda b, i: (b, n_ch - 1 - i, 0, 0, 0)),
                  tab, tab, ld],
        out_specs=[tok, tok, tok, ld],
        out_shape=[jax.ShapeDtypeStruct((bsz, n_tok, RET_W), f32)] * 3 + [jax.ShapeDtypeStruct((1, RET_DH), f32)],
        scratch_shapes=[pltpu.VMEM((RET_HEADS, RET_DH, RET_DH), f32)],
        compiler_params=_cparams(("arbitrary", "arbitrary")), name=name)(
        do, px, px, px, states, cos, sin, ld_row)


HALF_W = RW_W // 2


def _head_sum(x, ones):
    xm = x.astype(MXU_DTYPE)
    return jnp.concatenate([jnp.dot(xm[:, :HALF_W], ones, preferred_element_type=f32),
                            jnp.dot(xm[:, HALF_W:], ones, preferred_element_type=f32)], axis=1)


def _stack(parts):
    return jnp.concatenate(parts, axis=0)


def _row(ref, b, t):
    return ref[b, pl.ds(t, 1), :]


SCAN_DIRS = ((False, True), (True, False))
RW_HEADS = RW_W // RW_N
HEAD_ROWS_PAD = 16


def _head_rows(row, mask):
    return jnp.broadcast_to(row, mask.shape) * mask


def _outer(per_value, row, mask_pad):
    return lax.dot_general(per_value.astype(MXU_DTYPE), _head_rows(row, mask_pad).astype(MXU_DTYPE),
                           (((0,), (0,)), ((), ())), preferred_element_type=f32)


def _read(states, rows, mask):
    lhs = _stack([_head_rows(r, mask) for r in rows])
    return lax.dot_general(lhs.astype(MXU_DTYPE), _stack(states).astype(MXU_DTYPE), (((1,), (1,)), ((), ())),
                           preferred_element_type=f32)


def _row_from_heads(per_value, state, mask_pad):
    full = jnp.dot(per_value.astype(MXU_DTYPE), state.astype(MXU_DTYPE), preferred_element_type=f32)
    return jnp.sum(full * mask_pad, axis=0, keepdims=True)


def _scan_specs(bsz, order):
    rows = lambda col=0: pl.BlockSpec((bsz, SCAN_CHUNK, RW_W), lambda i: (0, order(i), col))
    per_value = pl.BlockSpec((bsz, SCAN_CHUNK, HEAD_ROWS_PAD, RW_N), lambda i: (0, order(i), 0, 0))
    return rows, per_value


def _removed(sp, kk_t, ones, bsz):
    removed = _head_sum(_stack([sp[b] * kk_t[b] for b in range(bsz)]), ones)
    return [removed[b * RW_N:(b + 1) * RW_N] for b in range(bsz)]


def _advance(sp, rem, w_t, b_t, vk, bsz):
    return [sp[b] * w_t[b] - rem[b] * b_t[b] + vk[b] for b in range(bsz)]


def heads_to_rows(a):
    b, t, _ = a.shape
    return jnp.pad(a.reshape(b, t, RW_HEADS, RW_N), ((0, 0), (0, 0), (0, HEAD_ROWS_PAD - RW_HEADS), (0, 0)))


def _blocks_to_rows(raw_ref, first, row_ref, bsz):
    steps = pl.ds(first, SCAN_CHUNK)
    for b in range(bsz):
        for h in range(RW_HEADS):
            row_ref[b, :, h * RW_N:(h + 1) * RW_N] = raw_ref[steps, RW_HEADS * b + h, RW_N * b:RW_N * (b + 1)]


N_ROWS_FWD = 5
N_ROWS_BWD = 5


def _scan_consts(bsz):
    head = (jnp.arange(RW_W)[None, :] // RW_N == jnp.arange(RW_HEADS)[:, None]).astype(f32)
    return head, jnp.pad(head, ((0, HEAD_ROWS_PAD - RW_HEADS), (0, 0))), _block_ones(HALF_W, RW_N)


def _const_specs(consts):
    return [pl.BlockSpec(c.shape, lambda i: (0, 0)) for c in consts]


def rwkv_scan_fwd(rows_in, v_heads, orders, name):
    bsz, n_tok, _ = rows_in[0][0][0].shape
    n_ch = n_tok // SCAN_CHUNK
    rng = range(bsz)
    consts = _scan_consts(bsz)

    def body(*refs):
        rows = [refs[:N_ROWS_FWD], refs[N_ROWS_FWD:2 * N_ROWS_FWD]]
        v0, v1, head_ref, pad_ref, ones_ref, y0, y1, cs0, cs1, s0, s1, late_ref, raw_ref = refs[2 * N_ROWS_FWD:]
        v_refs, y_refs, cs_refs, s_refs = (v0, v1), (y0, y1), (cs0, cs1), (s0, s1)
        head_v, pad_v, ones_v = head_ref[...], pad_ref[...], ones_ref[...]
        for d in range(2):
            @pl.when(pl.program_id(0) == 0)
            def _(d=d):
                s_refs[d][...] = jnp.zeros_like(s_refs[d])

            cs_refs[d][...] = s_refs[d][...]

        def step(j, carry):
            ts = [SCAN_CHUNK - 1 - j if reverse else j for reverse, _ in SCAN_DIRS]
            sps = [[s_refs[d][b] for b in rng] for d in range(2)]
            rems = [_removed(sps[d], [_row(rows[d][1], b, ts[d]) for b in rng], ones_v, bsz) for d in range(2)]
            vks = [[_outer(v_refs[d][b, ts[d]], _row(rows[d][4], b, ts[d]), pad_v) for b in rng] for d in range(2)]
            for d, (reverse, inclusive) in enumerate(SCAN_DIRS):
                r_ref = rows[d][0]
                if inclusive:
                    before = jnp.maximum(j - 1, 0)
                    late_ref[j] = _read(sps[d], [_row(r_ref, b, before) for b in rng], head_v)
                else:
                    raw_ref[ts[d]] = _read(sps[d], [_row(r_ref, b, ts[d]) for b in rng], head_v)
            for d in range(2):
                new = _advance(sps[d], rems[d], [_row(rows[d][2], b, ts[d]) for b in rng],
                               [_row(rows[d][3], b, ts[d]) for b in rng], vks[d], bsz)
                for b in rng:
                    s_refs[d][b] = new[b]
            return carry

        lax.fori_loop(0, SCAN_CHUNK, step, 0, unroll=SCAN_UNROLL)
        for d, (reverse, inclusive) in enumerate(SCAN_DIRS):
            if inclusive:
                assert not reverse
                last = SCAN_CHUNK - 1
                late_ref[SCAN_CHUNK] = _read([s_refs[d][b] for b in rng], [rows[d][0][b, last:last + 1, :] for b in rng],
                                             head_v)
                _blocks_to_rows(late_ref, 1, y_refs[d], bsz)
            else:
                _blocks_to_rows(raw_ref, 0, y_refs[d], bsz)

    specs = [_scan_specs(bsz, orders[d]) for d in range(2)]
    state = pltpu.VMEM((bsz, RW_N, RW_W), f32)
    late = pltpu.VMEM((SCAN_CHUNK + 1, RW_HEADS * bsz, RW_N * bsz), f32)
    raw = pltpu.VMEM((SCAN_CHUNK, RW_HEADS * bsz, RW_N * bsz), f32)
    start_spec = pl.BlockSpec((None, bsz, RW_N, RW_W), lambda i: (i, 0, 0, 0))
    return pl.pallas_call(
        body, grid=(n_ch,),
        in_specs=[specs[d][0](col) for d in range(2) for _, col in rows_in[d]] + [specs[0][1], specs[1][1]]
        + _const_specs(consts),
        out_specs=[specs[0][0](), specs[1][0](), start_spec, start_spec],
        out_shape=[jax.ShapeDtypeStruct((bsz, n_tok, RW_W), f32)] * 2
        + [jax.ShapeDtypeStruct((n_ch, bsz, RW_N, RW_W), f32)] * 2,
        scratch_shapes=[state, state, late, raw],
        compiler_params=_cparams(("arbitrary",)), name=name)(
        *[a for d in range(2) for a, _ in rows_in[d]], v_heads, v_heads, *consts)


def rwkv_scan_bwd(rows_in, v_heads, dy_heads, starts, orders, name):
    bsz, n_tok, _ = rows_in[0][0][0].shape
    n_ch = n_tok // SCAN_CHUNK
    backs = [functools.partial(lambda i, order: order(n_ch - 1 - i), order=orders[d]) for d in range(2)]
    rng = range(bsz)
    consts = _scan_consts(bsz)
    n_out, n_scr = 6, 8

    def body(*refs):
        rows = [refs[:N_ROWS_BWD], refs[N_ROWS_BWD:2 * N_ROWS_BWD]]
        rest = refs[2 * N_ROWS_BWD:]
        v_refs, dy_refs, cs_refs, (head_ref, pad_ref, ones_ref) = rest[0:2], rest[2:4], rest[4:6], rest[6:9]
        outs = [rest[9:9 + n_out], rest[9 + n_out:9 + 2 * n_out]]
        scr = [rest[9 + 2 * n_out:9 + 2 * n_out + n_scr], rest[9 + 2 * n_out + n_scr:]]
        head_v, pad_v, ones_v = head_ref[...], pad_ref[...], ones_ref[...]
        for d in range(2):
            s_ref, ds_ref = scr[d][:2]

            @pl.when(pl.program_id(0) == 0)
            def _(ds_ref=ds_ref):
                ds_ref[...] = jnp.zeros_like(ds_ref)

            s_ref[...] = cs_refs[d][...]

        def fstep(j, carry):
            ts = [SCAN_CHUNK - 1 - j if reverse else j for reverse, _ in SCAN_DIRS]
            sps = [[scr[d][0][b] for b in rng] for d in range(2)]
            rems = [_removed(sps[d], [_row(rows[d][1], b, ts[d]) for b in rng], ones_v, bsz) for d in range(2)]
            vks = [[_outer(v_refs[d][b, ts[d]], _row(rows[d][4], b, ts[d]), pad_v) for b in rng] for d in range(2)]
            for d in range(2):
                s_ref, _, hist_ref, rem_ref = scr[d][:4]
                new = _advance(sps[d], rems[d], [_row(rows[d][2], b, ts[d]) for b in rng],
                               [_row(rows[d][3], b, ts[d]) for b in rng], vks[d], bsz)
                for b in rng:
                    hist_ref[ts[d], b] = sps[d][b]
                    rem_ref[ts[d], b] = rems[d][b]
                    s_ref[b] = new[b]
            return carry

        lax.fori_loop(0, SCAN_CHUNK, fstep, 0, unroll=SCAN_UNROLL)

        def step_of(j, reverse):
            return j if reverse else SCAN_CHUNK - 1 - j

        for d, (reverse, _) in enumerate(SCAN_DIRS):
            t0 = step_of(0, reverse)
            for b in rng:
                scr[d][6][b] = _outer(dy_refs[d][b, t0], rows[d][0][b, t0:t0 + 1, :], pad_v)

        def bstep(j, carry):
            ts = [step_of(j, reverse) for reverse, _ in SCAN_DIRS]
            reads = [[scr[d][6][b] for b in rng] for d in range(2)]
            dss = []
            for d, (_, inclusive) in enumerate(SCAN_DIRS):
                ds = [scr[d][1][b] for b in rng]
                dss.append([ds[b] + reads[d][b] for b in rng] if inclusive else ds)
            drems = [_removed(dss[d], [-_row(rows[d][3], b, ts[d]) for b in rng], ones_v, bsz) for d in range(2)]
            for d, (reverse, _) in enumerate(SCAN_DIRS):
                t_next = step_of(jnp.minimum(j + 1, SCAN_CHUNK - 1), reverse)
                for b in rng:
                    scr[d][6][b] = _outer(dy_refs[d][b, t_next], _row(rows[d][0], b, t_next), pad_v)
                scr[d][7][ts[d]] = _read(dss[d], [_row(rows[d][4], b, ts[d]) for b in rng], head_v)
            for d, (_, inclusive) in enumerate(SCAN_DIRS):
                _, kk_ref, w_ref, _, _ = rows[d]
                _, ds_ref, _, _, dsh_ref, drem_ref = scr[d][:6]
                for b in rng:
                    dsh_ref[ts[d], b] = dss[d][b]
                    drem_ref[ts[d], b] = drems[d][b]
                    dsp = dss[d][b] * _row(w_ref, b, ts[d]) + drems[d][b] * _row(kk_ref, b, ts[d])
                    ds_ref[b] = dsp if inclusive else dsp + reads[d][b]
            return carry

        lax.fori_loop(0, SCAN_CHUNK, bstep, 0, unroll=SCAN_UNROLL)

        rsum = lambda z: jnp.sum(z, axis=0, keepdims=True)
        for d, (reverse, inclusive) in enumerate(SCAN_DIRS):
            dr_ref, dkk_ref, dw_ref, db_ref, dkt_ref, dv_ref = outs[d]
            s_ref, _, hist_ref, rem_ref, dsh_ref, drem_ref, _, dv_raw_ref = scr[d]
            _blocks_to_rows(dv_raw_ref, 0, dv_ref, bsz)
            for t in range(SCAN_CHUNK):
                ts = slice(t, t + 1)
                after = t - 1 if reverse else t + 1
                for b in rng:
                    sp, ds = hist_ref[t, b], dsh_ref[t, b]
                    if not inclusive:
                        seen = sp
                    else:
                        seen = hist_ref[after, b] if 0 <= after < SCAN_CHUNK else s_ref[b]
                    dr_ref[b, ts, :] = _row_from_heads(dy_refs[d][b, t], seen, pad_v)
                    dkt_ref[b, ts, :] = _row_from_heads(v_refs[d][b, t], ds, pad_v)
                    dw_ref[b, ts, :] = rsum(ds * sp)
                    db_ref[b, ts, :] = -rsum(ds * rem_ref[t, b])
                    dkk_ref[b, ts, :] = rsum(sp * drem_ref[t, b])

    specs = [_scan_specs(bsz, backs[d]) for d in range(2)]
    hist = pltpu.VMEM((SCAN_CHUNK, bsz, RW_N, RW_W), f32)
    state = pltpu.VMEM((bsz, RW_N, RW_W), f32)
    start_spec = pl.BlockSpec((None, bsz, RW_N, RW_W), lambda i: (n_ch - 1 - i, 0, 0, 0))
    raw = pltpu.VMEM((SCAN_CHUNK, RW_HEADS * bsz, RW_N * bsz), f32)
    return pl.pallas_call(
        body, grid=(n_ch,),
        in_specs=[specs[d][0](col) for d in range(2) for _, col in rows_in[d]]
        + [specs[0][1], specs[1][1]] * 2 + [start_spec, start_spec] + _const_specs(consts),
        out_specs=[specs[d][0]() for d in range(2) for _ in range(n_out)],
        out_shape=[jax.ShapeDtypeStruct((bsz, n_tok, RW_W), f32)] * (2 * n_out),
        scratch_shapes=[state, state, hist, hist, hist, hist, state, raw] * 2,
        compiler_params=_cparams(("arbitrary",)), name=name)(
        *[a for d in range(2) for a, _ in rows_in[d]], v_heads, v_heads, dy_heads, dy_heads, *starts, *consts)


MOD_NAMES = ("shift1", "scale1", "gate1", "shift2", "scale2", "gate2")


def _rope_tables(t_ctx, t_x):
    quarter = RET_DH // 4
    pos = jnp.arange(t_x)
    inv = jnp.power(ROPE_BASE, -jnp.arange(0, 2 * quarter, 2, dtype=f32) / (2 * quarter))
    ang_r = (pos // GRID_W).astype(f32)[:, None] * inv[None, :]
    ang_c = (pos % GRID_W).astype(f32)[:, None] * inv[None, :]
    cos = jnp.concatenate([jnp.cos(ang_r)] * 2 + [jnp.cos(ang_c)] * 2, axis=1)
    sin = jnp.concatenate([-jnp.sin(ang_r), jnp.sin(ang_r), -jnp.sin(ang_c), jnp.sin(ang_c)], axis=1)
    cos = jnp.concatenate([jnp.ones((t_ctx, RET_DH), f32), cos], axis=0)
    sin = jnp.concatenate([jnp.zeros((t_ctx, RET_DH), f32), sin], axis=0)
    return cos, sin


def _pad_rows(w, lo, total):
    return jnp.pad(w, ((lo, total - lo - w.shape[0]), (0, 0)))


def layer_step(x, ctx, tgt, mod_x, mod_ctx, wt):
    bsz, t_x, _ = x.shape
    t_c = ctx.shape[1]
    t_all = t_c + t_x
    n_ct, n_xt = t_c // TOK_TILE, t_x // TOK_TILE
    n_t = n_ct + n_xt
    assert t_c % TOK_TILE == 0 and t_x % TOK_TILE == 0 and t_c % RET_CHUNK == 0

    seg = lambda i: (i >= n_ct).astype(jnp.int32)
    seg_first = lambda i: jnp.logical_or(i == 0, i == n_ct)
    seg_last = lambda i: jnp.logical_or(i == n_ct - 1, i == n_t - 1)
    mod_all = {n: jnp.stack([jnp.broadcast_to(mod_ctx[k], (bsz, D_MODEL)), mod_x[:, k]], axis=1)[:, :, None, :]
               for k, n in enumerate(MOD_NAMES)}
    mod_lat = {n: mod_x[:, k][:, None, None, :] for k, n in enumerate(MOD_NAMES)}
    both = lambda n: Seg(mod_all[n], seg, seg_first)
    lat = lambda n: Seg(mod_lat[n], lambda i: 0, lambda i: i == 0)
    flat = lambda a: a.reshape(-1, a.shape[-1])

    def chunk_orders(n_ctx_chunks, n_chunks):
        fwd = lambda i: i
        bwd = lambda i: jnp.where(i < n_ctx_chunks, n_ctx_chunks - 1 - i, n_chunks + n_ctx_chunks - 1 - i)
        return fwd, bwd

    ones64, ones128 = _block_ones(RW_W, RW_N), _block_ones(RET_W, RET_DH)
    cos, sin = _rope_tables(t_c, t_x)
    ld_rows = [jnp.pad(wt["ret_log_decay"][d][None, :], ((0, 0), (0, RET_DH - RET_HEADS))) for d in range(2)]
    w_up_pad = [_pad_rows(wt["rwkv_w_up"][d], 0, LORA_W) for d in range(2)]
    a_up_pad = [_pad_rows(wt["rwkv_a_up"][d], DECAY_LORA, LORA_W) for d in range(2)]
    g_up_pad = _pad_rows(wt["rwkv_g_up"], DECAY_LORA + AAA_LORA, LORA_W)
    row = lambda a, d: a[d][None, :]

    h = jnp.concatenate([ctx, x], axis=1)
    norm1_ins = lambda: [Tiled(h), both("shift1"), both("scale1"), Glob(wt["norm1_g"])]
    (n1,) = ew_forward(fn_norm_mod, "norm1", bsz, n_t, norm1_ins(), [(D_MODEL, MXU_DTYPE)])
    px = matmul(flat(n1), wt["w_in"], "nn", "proj_in").reshape(bsz, t_all, IN_COLS)
    px_rw = px[..., RET_COLS:]
    ps = token_shift(px_rw, wt["rwkv_shift_mu"], seg_first, seg_last)

    def prep_ins(toff=0):
        return [Tiled(ps, RW_W, 1), Tiled(ps, LORA_W, 3 * RW_W // LORA_W),
                Glob(row(wt["rwkv_w0"], 0)), Glob(row(wt["rwkv_w0"], 1)),
                Glob(row(wt["rwkv_a0"], 0)), Glob(row(wt["rwkv_a0"], 1)),
                Glob(w_up_pad[0]), Glob(w_up_pad[1]), Glob(a_up_pad[0]), Glob(a_up_pad[1]), Glob(g_up_pad),
                Glob(wt["rwkv_k_k"]), Glob(wt["rwkv_k_a"]), Glob(ones64)]

    kk, w_f, b_f, kt_f, w_b, b_b, kt_b, g_rw = ew_forward(fn_rwkv_prepare, "rwkv_prepare", bsz, n_t, prep_ins(),
                                                           [(RW_W, f32)] * 8)
    rw_order = chunk_orders(t_c // SCAN_CHUNK, t_all // SCAN_CHUNK)
    ret_order = chunk_orders(t_c // RET_CHUNK, t_all // RET_CHUNK)
    scan_rows = [[(ps, 0), (kk, 0), (w_f, 0), (b_f, 0), (kt_f, 0)], [(ps, 0), (kk, 0), (w_b, 0), (b_b, 0), (kt_b, 0)]]
    v_heads = heads_to_rows(ps[..., 2 * RW_W:3 * RW_W])
    *y, start_f, start_b = rwkv_scan_fwd(scan_rows, v_heads, rw_order, "rwkv_scan_fwd")
    o, ret_states = [], []
    for d in range(2):
        o_d, st_d = retention_fwd(px, cos, sin, ld_rows[d], ret_order[d], SCAN_DIRS[d][0], f"retention_fwd{d}")
        o.append(o_d), ret_states.append(st_d)

    def merge_ins(toff):
        return [Tiled(o[0], toff=toff), Tiled(o[1], toff=toff), Tiled(px, RET_W, 3, toff),
                Tiled(y[0], toff=toff), Tiled(y[1], toff=toff), Tiled(ps, RW_W, 0, toff), Tiled(kt_f, toff=toff),
                Tiled(ps, RW_W, 2, toff), Tiled(g_rw, toff=toff),
                Glob(wt["rwkv_r_k"]), Glob(wt["rwkv_ln_w"]), Glob(wt["rwkv_ln_b"]), Glob(ones64), Glob(ones128)]

    ret_out, rw_out = ew_forward(fn_merge, "merge_heads", bsz, n_xt, merge_ins(n_ct),
                                 [(RET_W, MXU_DTYPE), (RW_W, MXU_DTYPE)])
    merged = jnp.concatenate([ret_out, rw_out], axis=-1)
    mix = matmul(flat(merged), wt["w_out"], "nn", "proj_out").reshape(bsz, t_x, D_MODEL)
    resid_ins = lambda: [Tiled(x), Tiled(mix), lat("gate1"), lat("shift2"), lat("scale2"), Glob(wt["norm2_g"])]
    h1, n2 = ew_forward(fn_resid_norm_mod, "resid_norm2", bsz, n_xt, resid_ins(), [(D_MODEL, f32), (D_MODEL, MXU_DTYPE)])
    act = matmul(flat(n2), wt["w_ff1"], "nn", "ff1", MXU_DTYPE, wt["b_ff1"], relu2).reshape(bsz, t_x, D_FF)
    ff = matmul(flat(act), wt["w_ff2"], "nn", "ff2").reshape(bsz, t_x, D_MODEL)

    g = {}
    loss, dh1, dff, dgate2, g["b_ff2"], g["final_g"] = loss_and_grads(
        h1, ff, tgt, mod_lat["gate2"], wt["b_ff2"], wt["final_g"], bsz, n_xt)
    dact = matmul(flat(dff), wt["w_ff2"], "nt", "ff2_dx", MXU_DTYPE).reshape(bsz, t_x, D_FF)
    g["w_ff2"] = matmul(flat(act), flat(dff), "tn", "ff2_dw")
    du, g["b_ff1"] = relu2_backward(act, dact, "relu2_bwd")
    dn2 = matmul(flat(du), wt["w_ff1"], "nt", "ff1_dx").reshape(bsz, t_x, D_MODEL)
    g["w_ff1"] = matmul(flat(n2), flat(du), "tn", "ff1_dw")
    dx_res, dmix, dgate1, dshift2, dscale2, g["norm2_g"] = ew_backward(
        fn_resid_norm_mod, "resid_norm2_bwd", bsz, n_xt, resid_ins(), [Tiled(dh1), Tiled(dn2)], [True] * 6,
        {1: MXU_DTYPE})
    dmerged = matmul(flat(dmix), wt["w_out"], "nt", "proj_out_dx").reshape(bsz, t_x, D_MODEL)
    g["w_out"] = matmul(flat(merged), flat(dmix), "tn", "proj_out_dw")
    (do, dg_ret, dy, dr_m, dkt_m, dv_m, dg_rw, g["rwkv_r_k"], g["rwkv_ln_w"], g["rwkv_ln_b"]) = ew_backward(
        fn_merge, "merge_heads_bwd", bsz, n_xt, merge_ins(0),
        [Tiled(dmerged, RET_W, 0, -n_ct), Tiled(dmerged, RW_W, 1, -n_ct)],
        [True, False, True, True, False, True, True, True, True, True, True, True, False, False], lead=n_ct)

    dqkv, dld = [], []
    for d in range(2):
        *dqkv_d, dld_d = retention_bwd(do, px, ret_states[d], cos, sin, ld_rows[d], ret_order[d],
                                       SCAN_DIRS[d][0], f"retention_bwd{d}")
        dqkv.append(dqkv_d), dld.append(dld_d[0, :RET_HEADS])
    g["ret_log_decay"] = jnp.stack(dld)
    (dr_f, dkk_f, dw_f, db_f, dkt_f, dv_f, dr_b, dkk_b, dw_b, db_b, dkt_b, dv_b) = rwkv_scan_bwd(
        scan_rows, v_heads, heads_to_rows(dy), (start_f, start_b), rw_order, "rwkv_scan_bwd")
    prep_cts = [dkk_f + dkk_b, dw_f, db_f, dkt_f + dkt_m, dw_b, db_b, dkt_b, dg_rw]
    (dks, dlora, dw0_f, dw0_b, da0_f, da0_b, dwup_f, dwup_b, daup_f, daup_b, dgup, g["rwkv_k_k"],
     g["rwkv_k_a"]) = ew_backward(fn_rwkv_prepare, "rwkv_prepare_bwd", bsz, n_t, prep_ins(),
                                  [Tiled(c) for c in prep_cts], [True] * 13 + [False])
    g["rwkv_w0"] = jnp.concatenate([dw0_f, dw0_b], axis=0)
    g["rwkv_a0"] = jnp.concatenate([da0_f, da0_b], axis=0)
    g["rwkv_w_up"] = jnp.stack([dwup_f[:DECAY_LORA], dwup_b[:DECAY_LORA]])
    g["rwkv_a_up"] = jnp.stack([daup_f[DECAY_LORA:DECAY_LORA + AAA_LORA], daup_b[DECAY_LORA:DECAY_LORA + AAA_LORA]])
    g["rwkv_g_up"] = dgup[DECAY_LORA + AAA_LORA:]
    dps = jnp.concatenate([dr_f + dr_b + dr_m, dks, dv_f + dv_b + dv_m, dlora], axis=-1)
    dp_rw, g["rwkv_shift_mu"] = token_shift_bwd(dps, px_rw, wt["rwkv_shift_mu"], seg_first, seg_last)
    dpx = jnp.concatenate([(dqkv[0][k] + dqkv[1][k]).astype(MXU_DTYPE) for k in range(3)]
                          + [dg_ret.astype(MXU_DTYPE), dp_rw], axis=-1)
    dn1 = matmul(flat(dpx), wt["w_in"], "nt", "proj_in_dx").reshape(bsz, t_all, D_MODEL)
    g["w_in"] = matmul(flat(n1), flat(dpx), "tn", "proj_in_dw")
    dh, dshift1, dscale1, g["norm1_g"] = ew_backward(fn_norm_mod, "norm1_bwd", bsz, n_t, norm1_ins(), [Tiled(dn1)],
                                                     [True] * 4)
    grad_x = dh[:, t_c:] + dx_res
    zeros = jnp.zeros((D_MODEL,), f32)
    g["mod_x"] = jnp.stack([dshift1[:, 1, 0], dscale1[:, 1, 0], dgate1[:, 0, 0], dshift2[:, 0, 0], dscale2[:, 0, 0],
                            dgate2[:, 0, 0]], axis=1)
    g["mod_ctx"] = jnp.stack([dshift1[:, 0, 0].sum(0), dscale1[:, 0, 0].sum(0), zeros, zeros, zeros, zeros])
    return loss, grad_x, g


MESH_ID = pl.DeviceIdType.MESH
ALL_PEERS = [(dx, dy, dc) for dx in (0, 1) for dy in (0, 1) for dc in (0, 1)][1:]
CHIP_PEERS = [(1, 0, 0), (0, 1, 0), (1, 1, 0)]
CHIP_SLOTS = (0, 2, 4, 6)


def _mesh_pos():
    return lax.axis_index("x"), lax.axis_index("y"), lax.axis_index("c")


def _device_slot():
    x, y, c = _mesh_pos()
    return 4 * x + 2 * y + c


def sibling_swap(arrs, name, pieces=1):
    n = len(arrs)
    assert all(a.shape[0] % pieces == 0 for a in arrs)

    def body(*refs):
        in_refs, out_refs = refs[:n], refs[n:2 * n]
        send_sems, recv_sems = refs[2 * n:]
        x, y, c = _mesh_pos()
        copies = []
        for a in range(n):
            rows = arrs[a].shape[0] // pieces
            for q in range(pieces):
                part = pl.ds(q * rows, rows)
                cp = pltpu.make_async_remote_copy(
                    src_ref=in_refs[a].at[part], dst_ref=out_refs[a].at[part], send_sem=send_sems.at[a * pieces + q],
                    recv_sem=recv_sems.at[a * pieces + q], device_id=(x, y, 1 - c), device_id_type=MESH_ID)
                cp.start()
                copies.append(cp)
        for cp in copies:
            cp.wait()

    any_spec = pl.BlockSpec(memory_space=pl.ANY)
    res = pl.pallas_call(
        body, in_specs=[any_spec] * n, out_specs=[any_spec] * n,
        out_shape=[jax.ShapeDtypeStruct(a.shape, a.dtype) for a in arrs],
        scratch_shapes=[pltpu.SemaphoreType.DMA((n * pieces,)), pltpu.SemaphoreType.DMA((n * pieces,))],
        name=name)(*arrs)
    return list(res)


def exchange(arrs, gather, peers, name, pieces=1, by_chip=False, own=True):
    n, n_peers = len(arrs), len(peers)
    n_slots = N_SHARDS if by_chip else N_DEV
    slot = (lambda x, y, c: 2 * x + y) if by_chip else (lambda x, y, c: 4 * x + 2 * y + c)
    block_rows = [a.shape[0] if gather else a.shape[1] for a in arrs]
    assert all(r % pieces == 0 for r in block_rows), (block_rows, pieces)

    def body(*refs):
        in_refs, out_refs = refs[:n], refs[n:2 * n]
        send_sems, recv_sems, local_sems = refs[2 * n:]
        x, y, c = _mesh_pos()
        me = slot(x, y, c)
        copies, locals_ = [], []
        for a in range(n):
            if own:
                mine = in_refs[a] if gather else in_refs[a].at[me]
                loc = pltpu.make_async_copy(mine, out_refs[a].at[me], local_sems.at[a])
                loc.start()
                locals_.append(loc)
            for k, (dx, dy, dc) in enumerate(peers):
                peer = (1 - x if dx else x, 1 - y if dy else y, 1 - c if dc else c)
                src = in_refs[a] if gather else in_refs[a].at[slot(*peer)]
                for q in range(pieces):
                    part = pl.ds(q * (block_rows[a] // pieces), block_rows[a] // pieces)
                    sem = (a * n_peers + k) * pieces + q
                    cp = pltpu.make_async_remote_copy(
                        src_ref=src.at[part], dst_ref=out_refs[a].at[me, part], send_sem=send_sems.at[sem],
                        recv_sem=recv_sems.at[sem], device_id=peer, device_id_type=MESH_ID)
                    cp.start()
                    copies.append(cp)
        for cp in copies:
            cp.wait()
        for loc in locals_:
            loc.wait()

    any_spec = pl.BlockSpec(memory_space=pl.ANY)
    out_shape = [jax.ShapeDtypeStruct((n_slots,) + (a.shape if gather else a.shape[1:]), a.dtype) for a in arrs]
    n_sems = n * n_peers * pieces
    res = pl.pallas_call(
        body, in_specs=[any_spec] * n, out_specs=[any_spec] * n, out_shape=out_shape,
        scratch_shapes=[pltpu.SemaphoreType.DMA((n_sems,)), pltpu.SemaphoreType.DMA((n_sems,)),
                        pltpu.SemaphoreType.DMA((n,))],
        name=name)(*arrs)
    return list(res)


def add_arrays(parts, name, out_dtype=f32):
    r, c = parts[0].shape
    tr = r
    for cand in (512, 256, 128, 64, 32, 16):
        if r % cand == 0:
            tr = cand
            break

    def body(*refs):
        acc = refs[0][...].astype(f32)
        for p_ref in refs[1:-1]:
            acc = acc + p_ref[...].astype(f32)
        refs[-1][...] = acc.astype(out_dtype)

    spec = pl.BlockSpec((tr, c), lambda i: (i, 0))
    return pl.pallas_call(body, grid=(r // tr,), in_specs=[spec] * len(parts), out_specs=spec,
                          out_shape=jax.ShapeDtypeStruct((r, c), out_dtype),
                          compiler_params=_cparams(("parallel",)), name=name)(*parts)


def gather_two_level(arrs, name):
    n = len(arrs)
    per = 7

    def body(*refs):
        in_refs, out_refs = refs[:n], refs[n:2 * n]
        send_sems, recv_sems = refs[2 * n:]
        x, y, c = _mesh_pos()
        me, sibling = (x, y, c), (x, y, 1 - c)
        chips = [(1 - x, y), (x, 1 - y), (1 - x, 1 - y)]

        def copy(a, k, block, to, src=None):
            rows = out_refs[a].at[4 * block[0] + 2 * block[1] + block[2]]
            return pltpu.make_async_remote_copy(src_ref=rows if src is None else src, dst_ref=rows,
                                                send_sem=send_sems.at[a * per + k], recv_sem=recv_sems.at[a * per + k],
                                                device_id=to, device_id_type=MESH_ID)

        first, passed = [], []
        for a in range(n):
            first.append(copy(a, 0, me, sibling, src=in_refs[a]))
            first += [copy(a, 1 + j, me, (*chip, c), src=in_refs[a]) for j, chip in enumerate(chips)]
        for cp in first:
            cp.start()
        for a in range(n):
            for j, chip in enumerate(chips):
                copy(a, 1 + j, (*chip, c), me).wait_recv()
                fwd = copy(a, 4 + j, (*chip, c), sibling)
                fwd.start()
                passed.append(fwd)
        for a in range(n):
            copy(a, 0, sibling, me).wait_recv()
            for j, chip in enumerate(chips):
                copy(a, 4 + j, (*chip, 1 - c), me).wait_recv()
        for cp in first + passed:
            cp.wait_send()

    any_spec = pl.BlockSpec(memory_space=pl.ANY)
    res = pl.pallas_call(
        body, in_specs=[any_spec] * n, out_specs=[any_spec] * n,
        out_shape=[jax.ShapeDtypeStruct((N_DEV,) + a.shape, a.dtype) for a in arrs],
        scratch_shapes=[pltpu.SemaphoreType.DMA((n * per,)), pltpu.SemaphoreType.DMA((n * per,))],
        name=name)(*arrs)
    return list(res)


def sum_slots(parts, slots, name):
    _, r, c = parts.shape
    tr = r
    for cand in (512, 256, 128, 64, 32, 16, 8):
        if r % cand == 0 and cand * c * 4 * len(slots) <= 8 * 1024 * 1024:
            tr = cand
            break

    def body(p_ref, o_ref):
        acc = p_ref[slots[0]].astype(f32)
        for s in slots[1:]:
            acc = acc + p_ref[s].astype(f32)
        o_ref[...] = acc

    return pl.pallas_call(body, grid=(r // tr,), in_specs=[pl.BlockSpec((parts.shape[0], tr, c), lambda i: (0, i, 0))],
                          out_specs=pl.BlockSpec((tr, c), lambda i: (i, 0)),
                          out_shape=jax.ShapeDtypeStruct((r, c), f32),
                          compiler_params=_cparams(("parallel",)), name=name)(parts)


def column_sum(a, name):
    def body(a_ref, o_ref):
        o_ref[...] = jnp.sum(a_ref[...], axis=0, keepdims=True)

    return pl.pallas_call(body, out_shape=jax.ShapeDtypeStruct((1, a.shape[1]), f32), name=name)(a)


def adamw(w, g, m, v, name):
    r, c = w.shape
    tr = r
    for cand in (256, 128, 64, 32, 16, 8):
        if r % cand == 0:
            tr = cand
            break

    def body(w_ref, g_ref, m_ref, v_ref, d_ref, mo_ref, vo_ref):
        gv = g_ref[...]
        m_new = ADAM_B1 * m_ref[...] + (1.0 - ADAM_B1) * gv
        v_new = ADAM_B2 * v_ref[...] + (1.0 - ADAM_B2) * jnp.square(gv)
        m_hat = m_new / (1.0 - ADAM_B1 ** ADAM_STEP)
        v_hat = v_new / (1.0 - ADAM_B2 ** ADAM_STEP)
        d_ref[...] = -ADAM_LR * (m_hat / (jnp.sqrt(v_hat) + ADAM_EPS) + ADAM_WD * w_ref[...])
        mo_ref[...] = m_new
        vo_ref[...] = v_new

    spec = pl.BlockSpec((tr, c), lambda i: (i, 0))
    return pl.pallas_call(body, grid=(r // tr,), in_specs=[spec] * 4, out_specs=[spec] * 3,
                          out_shape=[jax.ShapeDtypeStruct((r, c), f32)] * 3,
                          compiler_params=_cparams(("parallel",)), name=name)(w, g, m, v)


def adaln_fwd(c_rows, w, b):
    def body(c_ref, w_ref, b_ref, o_ref):
        cv = c_ref[...]
        o_ref[...] = _mxu_dot(cv * jax.nn.sigmoid(cv), w_ref[...]) + b_ref[...]

    return pl.pallas_call(body, out_shape=jax.ShapeDtypeStruct((c_rows.shape[0], w.shape[1]), f32),
                          compiler_params=pltpu.CompilerParams(vmem_limit_bytes=VMEM_LIMIT), name="adaln_fwd")(c_rows, w, b)


def adaln_bwd(c_rows, dm, w):
    def body(c_ref, dm_ref, w_ref, gw_ref, ds_ref):
        cv = c_ref[...]
        gw_ref[...] = _dg(cv * jax.nn.sigmoid(cv), dm_ref[...], 0, 0)
        ds_ref[...] = _dg(dm_ref[...], w_ref[...], 1, 1)

    return pl.pallas_call(body, out_shape=[jax.ShapeDtypeStruct(w.shape, f32),
                                           jax.ShapeDtypeStruct(c_rows.shape, f32)],
                          compiler_params=pltpu.CompilerParams(vmem_limit_bytes=VMEM_LIMIT), name="adaln_bwd")(c_rows, dm, w)


def c_ctx_grad(parts, c_ctx_row):
    def body(p_ref, c_ref, o_ref):
        total = p_ref[0, 0:1, :]
        for s in range(1, N_SHARDS):
            total = total + p_ref[s, 0:1, :]
        _, vjp = jax.vjp(jax.nn.silu, c_ref[...])
        o_ref[...] = vjp(total)[0]

    return pl.pallas_call(body, out_shape=jax.ShapeDtypeStruct((1, D_MODEL), f32), name="c_ctx_grad")(parts, c_ctx_row)


PACK_W = 1024
PACK_ROWS = 8


def _pack(arrs):
    pieces, layout, r0 = [], [], 0
    for a in arrs:
        size = math.prod(a.shape)
        rows = -(-size // (PACK_W * PACK_ROWS)) * PACK_ROWS
        pieces.append(jnp.pad(a.reshape(-1).astype(f32), (0, rows * PACK_W - size)).reshape(rows, PACK_W))
        layout.append((r0, rows, a.shape))
        r0 += rows
    return jnp.concatenate(pieces, axis=0), layout


def _unpack(pack, layout, lead=()):
    n_lead = len(lead)
    outs = []
    for r0, rows, shape in layout:
        piece = pack[(slice(None),) * n_lead + (slice(r0, r0 + rows),)].reshape(lead + (-1,))
        outs.append(piece[..., :math.prod(shape)].reshape(lead + tuple(shape)))
    return outs


W_NAMES = ("c_ctx", "w_ada", "b_ada", "norm1_g", "norm2_g", "w_in", "ret_log_decay", "rwkv_shift_mu", "rwkv_w0",
           "rwkv_w_up", "rwkv_a0", "rwkv_a_up", "rwkv_g_up", "rwkv_k_k", "rwkv_k_a", "rwkv_r_k", "rwkv_ln_w",
           "rwkv_ln_b", "w_out", "w_ff1", "b_ff1", "w_ff2", "b_ff2", "final_g")
COL_SHARDED = ("w_in", "w_ff1")
ROW_SHARDED = ("w_out", "w_ff2")
LAST_SHARDED = ("rwkv_shift_mu", "rwkv_w0", "rwkv_w_up", "rwkv_a0", "rwkv_a_up", "rwkv_g_up")
REPLICATED = ("c_ctx", "b_ada", "norm1_g", "norm2_g", "ret_log_decay", "rwkv_k_k", "rwkv_k_a", "rwkv_r_k",
              "rwkv_ln_w", "rwkv_ln_b", "b_ff1", "b_ff2", "final_g")
N_SHARDS = 4


def _train_step(a):
    x, c, ctx, tgt = a["x"], a["c"], a["ctx"], a["loss_target"]
    bsz = x.shape[0]
    mx, my, mc = _mesh_pos()
    shard = 2 * mx + my
    dev = _device_slot()

    (c_all,) = exchange([jnp.pad(c, ((0, PACK_ROWS - bsz), (0, 0)))], True, ALL_PEERS, "gather_c")
    n_ex = N_DEV * bsz
    c_rows = jnp.concatenate([c_all[:, :bsz].reshape(n_ex, D_MODEL), a["c_ctx"][None, :],
                              jnp.zeros((PACK_ROWS - 1, D_MODEL), f32)], axis=0)
    ada_cols = a["w_ada"].shape[-1]
    b_ada_cols = lax.dynamic_slice_in_dim(a["b_ada"], shard * ada_cols, ada_cols, axis=1)
    mod_cols = adaln_fwd(c_rows, a["w_ada"][0], b_ada_cols)

    halves, small_shards = [], [a[n][0] for n in LAST_SHARDED]
    for n in COL_SHARDED + ROW_SHARDED:
        w = a[n][0].astype(MXU_DTYPE)
        half = w.shape[0] // 2
        halves.append(lax.dynamic_slice_in_dim(w, mc * half, half, axis=0))
    small_pack, small_layout = _pack(small_shards)
    own_blocks = [mod_cols] + halves + [small_pack]
    gathered = [lax.dynamic_update_index_in_dim(got, own, dev, 0)
                for got, own in zip(gather_two_level(own_blocks, "gather_weights"), own_blocks)]
    mod_all = jnp.stack([gathered[0][s] for s in CHIP_SLOTS], axis=1).reshape(c_rows.shape[0], -1)
    mod_x = lax.dynamic_slice_in_dim(mod_all, dev * bsz, bsz, axis=0).reshape(bsz, 6, D_MODEL)
    mod_ctx = mod_all[n_ex].reshape(6, D_MODEL)
    wt = {}
    for n, gth in zip(COL_SHARDED + ROW_SHARDED, gathered[1:5]):
        per_chip = gth.reshape(N_SHARDS, -1, gth.shape[-1])
        wt[n] = (per_chip.transpose(1, 0, 2).reshape(per_chip.shape[1], -1) if n in COL_SHARDED
                 else per_chip.reshape(-1, per_chip.shape[-1]))
    small_by_chip = _unpack(jnp.stack([gathered[5][s] for s in CHIP_SLOTS]), small_layout, (N_SHARDS,))
    for n, parts in zip(LAST_SHARDED, small_by_chip):
        wt[n] = jnp.concatenate([parts[s] for s in range(N_SHARDS)], axis=-1)
    for n in ("norm1_g", "norm2_g", "rwkv_k_k", "rwkv_k_a", "rwkv_r_k", "rwkv_ln_w", "rwkv_ln_b", "b_ff1", "b_ff2"):
        wt[n] = a[n]
    wt["ret_log_decay"] = a["ret_log_decay"][0]
    wt["final_g"] = a["final_g"][None, :]

    loss, grad_x, g = layer_step(x, ctx, tgt, mod_x, mod_ctx, wt)

    small_names = [n for n in REPLICATED if n not in ("c_ctx", "b_ada")]
    g_pack, g_layout = _pack([jnp.pad(loss, ((0, 0), (0, PACK_W - loss.shape[1])))] + [g[n] for n in small_names]
                             + [g["mod_x"], g["mod_ctx"]])
    (g_packs,) = gather_two_level([g_pack], "gather_small_grads")
    g_packs = lax.dynamic_update_index_in_dim(g_packs, g_pack, dev, 0)
    g_sum = _unpack(sum_slots(g_packs, tuple(range(N_DEV)), "sum_small_grads"), g_layout)
    loss_total = g_sum[0][0, 0]
    grads = dict(zip(small_names, g_sum[1:1 + len(small_names)]))
    dmod_ctx = g_sum[-1].reshape(1, -1)
    dmod_x = _unpack(g_packs, g_layout, (N_DEV,))[-2].reshape(n_ex, -1)
    dmod = jnp.concatenate([dmod_x, dmod_ctx, jnp.zeros((PACK_ROWS - 1, dmod_x.shape[1]), f32)], axis=0)
    grads["b_ada"] = column_sum(dmod, "b_ada_grad")
    dmod_cols = lax.dynamic_slice_in_dim(dmod, shard * ada_cols, ada_cols, axis=1)
    grads["w_ada"], dsilu = adaln_bwd(c_rows, dmod_cols, a["w_ada"][0])

    blocks = []
    for n in COL_SHARDED + ROW_SHARDED:
        gw = g[n]
        if n in COL_SHARDED:
            gw = gw.reshape(gw.shape[0], N_SHARDS, -1).transpose(1, 0, 2)
        blocks.append(gw.reshape(N_DEV, -1, gw.shape[-1]).astype(MXU_DTYPE))
    shard_packs = []
    for s in range(N_SHARDS):
        pieces_s = [lax.slice_in_dim(g[n], s * a[n].shape[-1], (s + 1) * a[n].shape[-1], axis=g[n].ndim - 1)
                    for n in LAST_SHARDED]
        pack_s, shard_layout = _pack(pieces_s)
        shard_packs.append(jnp.pad(pack_s, ((0, -pack_s.shape[0] % (2 * PACK_ROWS)), (0, 0))))
    blocks.append(jnp.stack(shard_packs).reshape(N_DEV, -1, PACK_W))
    scattered = COL_SHARDED + ROW_SHARDED + ("small_shards",)
    halves_of = lambda blk, core: lax.dynamic_index_in_dim(
        blk.reshape(N_SHARDS, 2, *blk.shape[1:]), core, axis=1, keepdims=False).reshape(-1, blk.shape[-1])
    from_sibling = sibling_swap([halves_of(blk, 1 - mc) for blk in blocks], "prereduce_swap")
    chip_sums = [add_arrays([halves_of(blk, mc), got], f"prereduce_{n}", blk.dtype).reshape(N_SHARDS, -1, blk.shape[-1])
                 for n, blk, got in zip(scattered, blocks, from_sibling)]
    dsilu_rows = jnp.broadcast_to(jnp.pad(dsilu[n_ex:n_ex + 1], ((0, PACK_ROWS - 1), (0, 0)))[None],
                                  (N_SHARDS, PACK_ROWS, D_MODEL))
    to_chips = [dsilu_rows] + chip_sums
    received = exchange(to_chips, False, CHIP_PEERS, "scatter_big_grads", by_chip=True, own=False)
    received = [lax.dynamic_update_index_in_dim(got, lax.dynamic_index_in_dim(sent, shard, 0, keepdims=False), shard, 0)
                for got, sent in zip(received, to_chips)]
    grads["c_ctx"] = c_ctx_grad(received[0], a["c_ctx"][None, :])
    half_sums = [sum_slots(p, tuple(range(N_SHARDS)), f"sum_{n}") for n, p in zip(scattered, received[1:])]
    other_halves = sibling_swap(half_sums, "swap_halves")
    for n, mine, other in zip(scattered, half_sums, other_halves):
        rows = mine.shape[0]
        whole = jnp.zeros((2 * rows, mine.shape[1]), f32)
        whole = lax.dynamic_update_slice_in_dim(whole, mine, mc * rows, axis=0)
        grads[n] = lax.dynamic_update_slice_in_dim(whole, other, (1 - mc) * rows, axis=0)
    grads.update(zip(LAST_SHARDED, _unpack(grads.pop("small_shards"), shard_layout)))

    out_g, out_d, out_m, out_v = {}, {}, {}, {}
    for n in ("w_ada",) + COL_SHARDED + ROW_SHARDED:
        out_g[n] = grads[n].reshape(a[n].shape)
        two_d = lambda z: z.reshape(-1, z.shape[-1])
        d, m, v = adamw(two_d(a[n]), two_d(out_g[n]), two_d(a["m_" + n]), two_d(a["v_" + n]), f"adamw_{n}")
        out_d[n], out_m[n], out_v[n] = d.reshape(a[n].shape), m.reshape(a[n].shape), v.reshape(a[n].shape)
    rest = REPLICATED + LAST_SHARDED
    for n in rest:
        out_g[n] = grads[n].reshape(a[n].shape)
    packs = [_pack([src[n] for n in rest])[0] for src in
             ({n: a[n] for n in rest}, out_g, {n: a["m_" + n] for n in rest}, {n: a["v_" + n] for n in rest})]
    _, rest_layout = _pack([a[n] for n in rest])
    for dst, pack in zip((out_d, out_m, out_v), adamw(*packs, "adamw_small")):
        dst.update(zip(rest, _unpack(pack, rest_layout)))
    return (loss_total, grad_x, *[out_g[n] for n in W_NAMES], *[out_d[n] for n in W_NAMES],
            *[out_m[n] for n in W_NAMES], *[out_v[n] for n in W_NAMES])


def kernel(x, c, ctx, c_ctx, w_ada, b_ada, norm1_g, norm2_g, w_in, ret_log_decay, rwkv_shift_mu, rwkv_w0, rwkv_w_up, rwkv_a0, rwkv_a_up, rwkv_g_up, rwkv_k_k, rwkv_k_a, rwkv_r_k, rwkv_ln_w, rwkv_ln_b, w_out, w_ff1, b_ff1, w_ff2, b_ff2, final_g, loss_target, m_c_ctx, m_w_ada, m_b_ada, m_norm1_g, m_norm2_g, m_w_in, m_ret_log_decay, m_rwkv_shift_mu, m_rwkv_w0, m_rwkv_w_up, m_rwkv_a0, m_rwkv_a_up, m_rwkv_g_up, m_rwkv_k_k, m_rwkv_k_a, m_rwkv_r_k, m_rwkv_ln_w, m_rwkv_ln_b, m_w_out, m_w_ff1, m_b_ff1, m_w_ff2, m_b_ff2, m_final_g, v_c_ctx, v_w_ada, v_b_ada, v_norm1_g, v_norm2_g, v_w_in, v_ret_log_decay, v_rwkv_shift_mu, v_rwkv_w0, v_rwkv_w_up, v_rwkv_a0, v_rwkv_a_up, v_rwkv_g_up, v_rwkv_k_k, v_rwkv_k_a, v_rwkv_r_k, v_rwkv_ln_w, v_rwkv_ln_b, v_w_out, v_w_ff1, v_b_ff1, v_w_ff2, v_b_ff2, v_final_g):
    return _train_step(dict(locals()))
```

```python
import functools
import math

import jax
import jax.numpy as jnp
from jax import lax
from jax.experimental import pallas as pl
from jax.experimental.pallas import tpu as pltpu

f32 = jnp.float32
MXU_DTYPE = jnp.bfloat16

D_MODEL = 1024
RET_W = 512
RET_HEADS = 4
RET_DH = 128
RET_CHUNK = 128
RW_W = 512
RW_N = 64
DECAY_LORA = 64
AAA_LORA = 64
GATE_LORA = 128
LORA_W = DECAY_LORA + AAA_LORA + GATE_LORA
D_FF = 4096
RET_COLS = 4 * RET_W
SHIFT_COLS = 3 * RW_W + LORA_W
IN_COLS = RET_COLS + SHIFT_COLS
GRID_W = 64
ROPE_BASE = 10000.0
NORM_EPS = 1e-6
GN_EPS = 64e-5
W_DECAY_SCALE = math.exp(-0.5)
ADAM_LR, ADAM_B1, ADAM_B2, ADAM_EPS, ADAM_WD, ADAM_STEP = 0.001, 0.9, 0.999, 1e-08, 0.01, 10

TOK_TILE = 256
MATMUL_TILE = 1024
SCAN_CHUNK = 16
SCAN_UNROLL = SCAN_CHUNK
N_DEV = 8
V7X_VMEM_BYTES = 64 * 1024 * 1024
VMEM_LIMIT = V7X_VMEM_BYTES * 7 // 8


def _cparams(sem):
    return pltpu.CompilerParams(dimension_semantics=sem, vmem_limit_bytes=VMEM_LIMIT)


def _tile(n, cap):
    best = None
    for t in range(128, min(n, cap) + 1, 128):
        if n % t == 0:
            best = t
    return best if best is not None else n


def matmul(a, b, mode, name, out_dtype=f32, bias=None, finish=None):
    if mode == "nn":
        (m, k), (k2, n) = a.shape, b.shape
    elif mode == "nt":
        (m, k), (n, k2) = a.shape, b.shape
    else:
        (k, m), (k2, n) = a.shape, b.shape
    assert k == k2, (a.shape, b.shape, mode)
    tm, tn, tk = _tile(m, MATMUL_TILE), _tile(n, MATMUL_TILE), _tile(k, MATMUL_TILE)
    nk = k // tk
    dims = {"nn": ((1,), (0,)), "nt": ((1,), (1,)), "tn": ((0,), (0,))}[mode]

    def body(a_ref, b_ref, *rest):
        o_ref, acc_ref = rest[-2:]
        kk = pl.program_id(2)

        @pl.when(kk == 0)
        def _():
            acc_ref[...] = jnp.zeros_like(acc_ref)

        acc_ref[...] += lax.dot_general(a_ref[...].astype(MXU_DTYPE), b_ref[...].astype(MXU_DTYPE),
                                        (dims, ((), ())), preferred_element_type=f32)

        @pl.when(kk == nk - 1)
        def _():
            res = acc_ref[...]
            if bias is not None:
                res = res + rest[0][...]
            if finish is not None:
                res = finish(res)
            o_ref[...] = res.astype(o_ref.dtype)

    if mode == "nn":
        a_spec = pl.BlockSpec((tm, tk), lambda i, j, q: (i, q))
        b_spec = pl.BlockSpec((tk, tn), lambda i, j, q: (q, j))
    elif mode == "nt":
        a_spec = pl.BlockSpec((tm, tk), lambda i, j, q: (i, q))
        b_spec = pl.BlockSpec((tn, tk), lambda i, j, q: (j, q))
    else:
        a_spec = pl.BlockSpec((tk, tm), lambda i, j, q: (q, i))
        b_spec = pl.BlockSpec((tk, tn), lambda i, j, q: (q, j))
    extra_specs = [] if bias is None else [pl.BlockSpec((1, tn), lambda i, j, q: (0, j))]
    extra = [] if bias is None else [bias]
    return pl.pallas_call(
        body, grid=(m // tm, n // tn, nk), in_specs=[a_spec, b_spec] + extra_specs,
        out_specs=pl.BlockSpec((tm, tn), lambda i, j, q: (i, j)),
        out_shape=jax.ShapeDtypeStruct((m, n), out_dtype),
        scratch_shapes=[pltpu.VMEM((tm, tn), f32)],
        compiler_params=_cparams(("parallel", "parallel", "arbitrary")), name=name)(a, b, *extra)


class Tiled:
    def __init__(self, arr, w=None, cidx=0, toff=0):
        self.arr, self.w, self.cidx, self.toff = arr, (arr.shape[-1] if w is None else w), cidx, toff

    def spec(self):
        cidx, toff = self.cidx, self.toff
        return pl.BlockSpec((None, TOK_TILE, self.w), lambda b, i: (b, jnp.maximum(i + toff, 0), cidx))


class Seg:
    def __init__(self, arr, seg, first):
        self.arr, self.seg, self.first = arr, seg, first

    def spec(self):
        seg = self.seg
        return pl.BlockSpec((None, None, 1, self.arr.shape[-1]), lambda b, i: (b, seg(i), 0, 0))


class Glob:
    def __init__(self, arr):
        self.arr = arr

    def spec(self):
        return pl.BlockSpec(self.arr.shape, lambda b, i: (0,) * self.arr.ndim)


def ew_forward(fn, name, bsz, n_tiles, ins, outs):
    n_in = len(ins)

    def body(*refs):
        res = fn(*[r[...] for r in refs[:n_in]])
        for o_ref, o in zip(refs[n_in:], res):
            o_ref[...] = o.astype(o_ref.dtype)

    out_specs = [pl.BlockSpec((None, TOK_TILE, w), lambda b, i: (b, i, 0)) for w, _ in outs]
    out_shape = [jax.ShapeDtypeStruct((bsz, n_tiles * TOK_TILE, w), dt) for w, dt in outs]
    return pl.pallas_call(body, grid=(bsz, n_tiles), in_specs=[d.spec() for d in ins], out_specs=out_specs,
                          out_shape=out_shape, compiler_params=_cparams(("parallel", "parallel")), name=name)(
        *[d.arr for d in ins])


def ew_backward(fn, name, bsz, n_tiles, ins, cts, want, grad_dtypes=None, lead=0):
    n_in, n_ct = len(ins), len(cts)
    diff = [k for k in range(n_in) if want[k]]
    grad_dtypes = grad_dtypes or {}
    assert lead == 0 or not any(isinstance(ins[k], Seg) for k in diff)

    def body(*refs):
        b, i = pl.program_id(0), pl.program_id(1)
        g_refs = refs[n_in + n_ct:]

        def tile_grads():
            vals = [r[...] for r in refs[:n_in]]
            ct_vals = tuple(r[...].astype(f32) for r in refs[n_in:n_in + n_ct])

            def f(*dvals):
                full = list(vals)
                for k, v in zip(diff, dvals):
                    full[k] = v
                return tuple(fn(*full))

            _, vjp = jax.vjp(f, *[vals[k] for k in diff])
            grads = vjp(ct_vals)
            for k, g_ref, g in zip(diff, g_refs, grads):
                d = ins[k]
                if isinstance(d, Tiled):
                    g_ref[...] = g.astype(g_ref.dtype)
                else:
                    zero = d.first(i) if isinstance(d, Seg) else jnp.logical_and(b == 0, i == lead)

                    @pl.when(zero)
                    def _(g_ref=g_ref):
                        g_ref[...] = jnp.zeros_like(g_ref)

                    g_ref[...] += g

        if lead == 0:
            tile_grads()
        else:
            pl.when(i >= lead)(tile_grads)

            @pl.when(i < lead)
            def _():
                for k, g_ref in zip(diff, g_refs):
                    if isinstance(ins[k], Tiled):
                        g_ref[...] = jnp.zeros_like(g_ref)

    out_specs, out_shape = [], []
    for k in diff:
        d = ins[k]
        if isinstance(d, Tiled):
            out_specs.append(pl.BlockSpec((None, TOK_TILE, d.w), lambda b, i: (b, i, 0)))
            out_shape.append(jax.ShapeDtypeStruct((bsz, (n_tiles + lead) * TOK_TILE, d.w), grad_dtypes.get(k, f32)))
        else:
            out_specs.append(d.spec())
            out_shape.append(jax.ShapeDtypeStruct(d.arr.shape, f32))
    return pl.pallas_call(body, grid=(bsz, n_tiles + lead),
                          in_specs=[d.spec() for d in ins] + [c.spec() for c in cts],
                          out_specs=out_specs, out_shape=out_shape,
                          compiler_params=_cparams(("arbitrary", "arbitrary")), name=name)(
        *[d.arr for d in ins], *[c.arr for c in cts])


@jax.custom_vjp
def _mxu_dot(a, b):
    return jnp.dot(a.astype(MXU_DTYPE), b.astype(MXU_DTYPE), preferred_element_type=f32)


def _mxu_dot_fwd(a, b):
    return _mxu_dot(a, b), (a, b)


def _mxu_dot_bwd(res, ct):
    a, b = res
    ct = ct.astype(MXU_DTYPE)
    da = lax.dot_general(ct, b.astype(MXU_DTYPE), (((1,), (1,)), ((), ())), preferred_element_type=f32)
    db = lax.dot_general(a.astype(MXU_DTYPE), ct, (((0,), (0,)), ((), ())), preferred_element_type=f32)
    return da, db


_mxu_dot.defvjp(_mxu_dot_fwd, _mxu_dot_bwd)


def _split_dot_impl(x, ones_mat):
    hi = x.astype(MXU_DTYPE)
    lo = (x - hi.astype(f32)).astype(MXU_DTYPE)
    return jnp.dot(hi, ones_mat, preferred_element_type=f32) + jnp.dot(lo, ones_mat, preferred_element_type=f32)


@jax.custom_vjp
def _split_dot(x, ones_mat):
    return _split_dot_impl(x, ones_mat)


def _split_dot_fwd(x, ones_mat):
    return _split_dot_impl(x, ones_mat), ones_mat


def _split_dot_bwd(ones_mat, ct):
    return _split_dot_impl(ct, ones_mat), None


_split_dot.defvjp(_split_dot_fwd, _split_dot_bwd)


def _block_ones(n, group):
    idx = jnp.arange(n) // group
    return (idx[:, None] == idx[None, :]).astype(MXU_DTYPE)


def _rms(x, g):
    return x * lax.rsqrt(jnp.mean(x * x, axis=-1, keepdims=True) + NORM_EPS) * g


def fn_norm_mod(h, shift, scale, g):
    return (_rms(h, g) * (1.0 + scale) + shift,)


def fn_rwkv_prepare(ks, lora, w0_f, w0_b, a0_f, a0_b, w_up_f, w_up_b, a_up_f, a_up_b, g_up, k_k, k_a, ones64):
    kkr = ks * k_k
    kk = kkr * lax.rsqrt(_split_dot(kkr * kkr, ones64) + 1e-12)
    outs = [kk]
    th = jnp.tanh(lora)
    for w0, a0, w_up, a_up in ((w0_f, a0_f, w_up_f, a_up_f), (w0_b, a0_b, w_up_b, a_up_b)):
        w = jnp.exp(-W_DECAY_SCALE * jax.nn.sigmoid(w0 + _mxu_dot(th, w_up)))
        a = jax.nn.sigmoid(a0 + _mxu_dot(lora, a_up))
        kt = ks * (1.0 + (a - 1.0) * k_a)
        outs += [w, a * kk, kt]
    outs.append(_mxu_dot(jax.nn.sigmoid(lora), g_up))
    return tuple(outs)


def fn_merge(o_f, o_b, g_ret, y_f, y_b, r, kt_f, v, g_rw, r_k, ln_w, ln_b, ones64, ones128):
    o = o_f + o_b
    ret = o * lax.rsqrt(_split_dot(o * o, ones128) * (1.0 / RET_DH) + NORM_EPS) * (g_ret * jax.nn.sigmoid(g_ret))
    y = y_f + y_b
    mean = _split_dot(y, ones64) * (1.0 / RW_N)
    yc = y - mean
    var = _split_dot(yc * yc, ones64) * (1.0 / RW_N)
    y_n = yc * lax.rsqrt(var + GN_EPS) * ln_w + ln_b
    bonus = _split_dot(r * kt_f * r_k, ones64) * v
    return ret, (y_n + bonus) * g_rw


def fn_resid_norm_mod(x, mix, gate, shift, scale, g):
    h1 = x + gate * mix
    return h1, _rms(h1, g) * (1.0 + scale) + shift


def relu2(z):
    return jnp.square(jnp.maximum(z, 0.0))


def relu2_backward(act, dact, name):
    bsz, n_tok, width = act.shape

    def body(a_ref, d_ref, du_ref, db_ref):
        du = d_ref[...].astype(f32) * (2.0 * jnp.sqrt(a_ref[...].astype(f32)))
        du_ref[...] = du.astype(du_ref.dtype)

        @pl.when(jnp.logical_and(pl.program_id(0) == 0, pl.program_id(1) == 0))
        def _():
            db_ref[...] = jnp.zeros_like(db_ref)

        db_ref[...] += jnp.sum(du, axis=0, keepdims=True)

    tile = pl.BlockSpec((None, TOK_TILE, width), lambda b, i: (b, i, 0))
    row = pl.BlockSpec((1, width), lambda b, i: (0, 0))
    return pl.pallas_call(body, grid=(bsz, n_tok // TOK_TILE), in_specs=[tile, tile], out_specs=[tile, row],
                          out_shape=[jax.ShapeDtypeStruct(act.shape, MXU_DTYPE), jax.ShapeDtypeStruct((1, width), f32)],
                          compiler_params=_cparams(("arbitrary", "arbitrary")), name=name)(act, dact)


def fn_loss(h1, f, tgt, gate, b2, g):
    y = _rms(h1 + gate * (f + b2), g)
    err = jnp.square(y - tgt)
    return 0.5 * jnp.sum(jnp.mean(err, axis=-1, keepdims=True), axis=0, keepdims=True)


def loss_and_grads(h1, f, tgt, gate, b2, g, bsz, n_tiles):
    def body(h1_ref, f_ref, t_ref, gate_ref, b2_ref, g_ref, loss_ref, dh1_ref, df_ref, dgate_ref, db2_ref, dg_ref):
        b, i = pl.program_id(0), pl.program_id(1)
        tgt_v = t_ref[...]
        loss, vjp = jax.vjp(lambda a, c, e, p, q: fn_loss(a, c, tgt_v, e, p, q),
                            h1_ref[...], f_ref[...], gate_ref[...], b2_ref[...], g_ref[...])
        dh1, df, dgate, db2, dg = vjp(jnp.ones((1, 1), f32))
        dh1_ref[...] = dh1
        df_ref[...] = df.astype(df_ref.dtype)

        @pl.when(i == 0)
        def _():
            dgate_ref[...] = jnp.zeros_like(dgate_ref)

        @pl.when(jnp.logical_and(b == 0, i == 0))
        def _():
            loss_ref[...] = jnp.zeros_like(loss_ref)
            db2_ref[...] = jnp.zeros_like(db2_ref)
            dg_ref[...] = jnp.zeros_like(dg_ref)

        dgate_ref[...] += dgate
        db2_ref[...] += db2
        dg_ref[...] += dg
        loss_ref[...] += jnp.broadcast_to(loss, loss_ref.shape)

    tile = pl.BlockSpec((None, TOK_TILE, D_MODEL), lambda b, i: (b, i, 0))
    row = pl.BlockSpec((1, D_MODEL), lambda b, i: (0, 0))
    seg = pl.BlockSpec((None, None, 1, D_MODEL), lambda b, i: (b, 0, 0, 0))
    t_tok = n_tiles * TOK_TILE
    return pl.pallas_call(
        body, grid=(bsz, n_tiles), in_specs=[tile, tile, tile, seg, row, row],
        out_specs=[pl.BlockSpec((1, 128), lambda b, i: (0, 0)), tile, tile, seg, row, row],
        out_shape=[jax.ShapeDtypeStruct((1, 128), f32), jax.ShapeDtypeStruct((bsz, t_tok, D_MODEL), f32),
                   jax.ShapeDtypeStruct((bsz, t_tok, D_MODEL), MXU_DTYPE),
                   jax.ShapeDtypeStruct((bsz, 1, 1, D_MODEL), f32),
                   jax.ShapeDtypeStruct((1, D_MODEL), f32), jax.ShapeDtypeStruct((1, D_MODEL), f32)],
        compiler_params=_cparams(("arbitrary", "arbitrary")), name="loss_and_grads")(h1, f, tgt, gate, b2, g)


SHIFT_BLOCK = SHIFT_COLS
HALO_ROWS = 8


def _shift_specs(n_tok, col0):
    per_tile = TOK_TILE // HALO_ROWS
    last = n_tok // HALO_ROWS - 1
    tile = pl.BlockSpec((None, TOK_TILE, SHIFT_BLOCK), lambda j, b, i: (b, i, col0 + j))
    prev = pl.BlockSpec((None, HALO_ROWS, SHIFT_BLOCK),
                        lambda j, b, i: (b, jnp.maximum(i * per_tile - 1, 0), col0 + j))
    nxt = pl.BlockSpec((None, HALO_ROWS, SHIFT_BLOCK),
                       lambda j, b, i: (b, jnp.minimum((i + 1) * per_tile, last), col0 + j))
    return tile, prev, nxt


def _shifted(p, prev_ref, next_ref, is_first, is_last):
    row = lax.broadcasted_iota(jnp.int32, p.shape, 0)
    prev_row = jnp.where(is_first, 0.0, prev_ref[HALO_ROWS - 1:HALO_ROWS, :].astype(f32))
    next_row = jnp.where(is_last, 0.0, next_ref[0:1, :].astype(f32))
    prev = jnp.where(row == 0, prev_row, pltpu.roll(p, 1, axis=0))
    nxt = jnp.where(row == TOK_TILE - 1, next_row, pltpu.roll(p, TOK_TILE - 1, axis=0))
    return prev, nxt


def token_shift(px, mu, seg_first, seg_last):
    bsz, n_tok, _ = px.shape
    n_tiles = n_tok // TOK_TILE

    def body(p_ref, prev_ref, next_ref, mu_ref, o_ref):
        i = pl.program_id(2)
        p = p_ref[...]
        prev, nxt = _shifted(p, prev_ref, next_ref, seg_first(i), seg_last(i))
        o_ref[...] = p + mu_ref[0:1, :] * (prev - p) + mu_ref[1:2, :] * (nxt - p)

    tile, prev, nxt = _shift_specs(n_tok, 0)
    return pl.pallas_call(
        body, grid=(SHIFT_COLS // SHIFT_BLOCK, bsz, n_tiles),
        in_specs=[tile, prev, nxt, pl.BlockSpec((2, SHIFT_BLOCK), lambda j, b, i: (0, j))],
        out_specs=pl.BlockSpec((None, TOK_TILE, SHIFT_BLOCK), lambda j, b, i: (b, i, j)),
        out_shape=jax.ShapeDtypeStruct((bsz, n_tok, SHIFT_COLS), f32),
        compiler_params=_cparams(("parallel", "parallel", "parallel")), name="token_shift")(px, px, px, mu)


def token_shift_bwd(dps, px, mu, seg_first, seg_last):
    bsz, n_tok, _ = px.shape
    n_tiles = n_tok // TOK_TILE

    def body(d_ref, dprev_ref, dnext_ref, p_ref, prev_ref, next_ref, mu_ref, dp_ref, dmu_ref):
        b, i = pl.program_id(1), pl.program_id(2)
        first, last = seg_first(i), seg_last(i)
        d, p = d_ref[...], p_ref[...]
        d_prev, d_next = _shifted(d, dprev_ref, dnext_ref, first, last)
        p_prev, p_next = _shifted(p, prev_ref, next_ref, first, last)
        mu0, mu1 = mu_ref[0:1, :], mu_ref[1:2, :]
        dp_ref[...] = (d + mu0 * (d_next - d) + mu1 * (d_prev - d)).astype(dp_ref.dtype)

        @pl.when(jnp.logical_and(b == 0, i == 0))
        def _():
            dmu_ref[...] = jnp.zeros_like(dmu_ref)

        dmu_ref[0:1, :] += jnp.sum(d * (p_prev - p), axis=0, keepdims=True)
        dmu_ref[1:2, :] += jnp.sum(d * (p_next - p), axis=0, keepdims=True)

    dtile, dprev, dnext = _shift_specs(n_tok, 0)
    tile, prev, nxt = _shift_specs(n_tok, 0)
    mu_spec = pl.BlockSpec((2, SHIFT_BLOCK), lambda j, b, i: (0, j))
    return pl.pallas_call(
        body, grid=(SHIFT_COLS // SHIFT_BLOCK, bsz, n_tiles),
        in_specs=[dtile, dprev, dnext, tile, prev, nxt, mu_spec],
        out_specs=[pl.BlockSpec((None, TOK_TILE, SHIFT_BLOCK), lambda j, b, i: (b, i, j)), mu_spec],
        out_shape=[jax.ShapeDtypeStruct((bsz, n_tok, SHIFT_COLS), MXU_DTYPE),
                   jax.ShapeDtypeStruct((2, SHIFT_COLS), f32)],
        compiler_params=_cparams(("arbitrary", "arbitrary", "arbitrary")), name="token_shift_bwd")(
        dps, dps, dps, px, px, px, mu)


def _dg(a, b, ca, cb):
    return lax.dot_general(a.astype(MXU_DTYPE), b.astype(MXU_DTYPE), (((ca,), (cb,)), ((), ())),
                           preferred_element_type=f32)


@jax.custom_vjp
def _mm_nt(a, b):
    return _dg(a, b, 1, 1)


_mm_nt.defvjp(lambda a, b: (_dg(a, b, 1, 1), (a, b)),
              lambda res, ct: (_dg(ct, res[1], 1, 0), _dg(ct, res[0], 0, 0)))


@jax.custom_vjp
def _mm_tn(a, b):
    return _dg(a, b, 0, 0)


_mm_tn.defvjp(lambda a, b: (_dg(a, b, 0, 0), (a, b)),
              lambda res, ct: (_dg(res[1], ct, 1, 1), _dg(res[0], ct, 1, 0)))


ROTARY_PAIR = RET_DH // 4


def _swap_pairs_impl(t):
    lane = lax.broadcasted_iota(jnp.int32, t.shape, 1)
    return jnp.where(lane % (2 * ROTARY_PAIR) < ROTARY_PAIR, pltpu.roll(t, RET_DH - ROTARY_PAIR, axis=1),
                     pltpu.roll(t, ROTARY_PAIR, axis=1))


@jax.custom_vjp
def _swap_pairs(t):
    return _swap_pairs_impl(t)


_swap_pairs.defvjp(lambda t: (_swap_pairs_impl(t), None), lambda _, ct: (_swap_pairs_impl(ct),))


def _ret_chunk(state, q_raw, k_raw, v, cos, sin, ld_row, head, reverse):
    c = RET_CHUNK
    lane = lax.broadcasted_iota(jnp.int32, ld_row.shape, 1)
    lg = -jnp.exp(jnp.sum(jnp.where(lane == head, ld_row, 0.0), axis=-1, keepdims=True))
    rot = lambda t: t * cos + _swap_pairs(t) * sin
    q = rot(q_raw)
    k = rot(k_raw) * (RET_DH ** -0.5)
    ti = lax.broadcasted_iota(jnp.int32, (c, 1), 0).astype(f32)
    tj = lax.broadcasted_iota(jnp.int32, (1, c), 1).astype(f32)
    if not reverse:
        dist, mask, q_exp, k_exp = ti - tj, (ti - tj) >= 0, ti + 1.0, c - 1.0 - ti
    else:
        dist, mask, q_exp, k_exp = tj - ti, (tj - ti) > 0, c - ti, ti
    decay = jnp.where(mask, jnp.exp(lg * jnp.maximum(dist, 0.0)), 0.0)
    scores = _mm_nt(q, k) * decay
    out = _mxu_dot(scores, v) + _mxu_dot(q * jnp.exp(lg * q_exp), state)
    new_state = state * jnp.exp(lg * c) + _mm_tn(k * jnp.exp(lg * k_exp), v)
    return out, new_state


def _ret_specs(order):
    qkv = [pl.BlockSpec((None, RET_CHUNK, RET_W), functools.partial(lambda b, i, col: (b, order(i), col), col=col))
           for col in range(3)]
    tab = pl.BlockSpec((RET_CHUNK, RET_DH), lambda b, i: (order(i), 0))
    ld = pl.BlockSpec((1, RET_DH), lambda b, i: (0, 0))
    return qkv, tab, ld


def retention_fwd(px, cos, sin, ld_row, order, reverse, name):
    bsz, n_tok, _ = px.shape
    n_ch = n_tok // RET_CHUNK

    def body(q_ref, k_ref, v_ref, cos_ref, sin_ref, ld_ref, o_ref, sv_ref, st_ref):
        @pl.when(pl.program_id(1) == 0)
        def _():
            st_ref[...] = jnp.zeros_like(st_ref)

        for h in range(RET_HEADS):
            sl = slice(h * RET_DH, (h + 1) * RET_DH)
            s = st_ref[h]
            sv_ref[h] = s
            o, s_new = _ret_chunk(s, q_ref[:, sl], k_ref[:, sl], v_ref[:, sl], cos_ref[...], sin_ref[...],
                                  ld_ref[...], h, reverse)
            o_ref[:, sl] = o
            st_ref[h] = s_new

    qkv, tab, ld = _ret_specs(order)
    return pl.pallas_call(
        body, grid=(bsz, n_ch), in_specs=[*qkv, tab, tab, ld],
        out_specs=[pl.BlockSpec((None, RET_CHUNK, RET_W), lambda b, i: (b, order(i), 0)),
                   pl.BlockSpec((None, None, RET_HEADS, RET_DH, RET_DH), lambda b, i: (b, i, 0, 0, 0))],
        out_shape=[jax.ShapeDtypeStruct((bsz, n_tok, RET_W), f32),
                   jax.ShapeDtypeStruct((bsz, n_ch, RET_HEADS, RET_DH, RET_DH), f32)],
        scratch_shapes=[pltpu.VMEM((RET_HEADS, RET_DH, RET_DH), f32)],
        compiler_params=_cparams(("parallel", "arbitrary")), name=name)(px, px, px, cos, sin, ld_row)


def retention_bwd(do, px, states, cos, sin, ld_row, order, reverse, name):
    bsz, n_tok, _ = px.shape
    n_ch = n_tok // RET_CHUNK
    back = lambda i: order(n_ch - 1 - i)

    def body(do_ref, q_ref, k_ref, v_ref, sv_ref, cos_ref, sin_ref, ld_ref,
             dq_ref, dk_ref, dv_ref, dld_ref, dst_ref):
        b, i = pl.program_id(0), pl.program_id(1)

        @pl.when(i == 0)
        def _():
            dst_ref[...] = jnp.zeros_like(dst_ref)

        @pl.when(jnp.logical_and(b == 0, i == 0))
        def _():
            dld_ref[...] = jnp.zeros_like(dld_ref)

        cos_v, sin_v = cos_ref[...], sin_ref[...]
        for h in range(RET_HEADS):
            sl = slice(h * RET_DH, (h + 1) * RET_DH)
            f = lambda s, q, k, v, ld, h=h: _ret_chunk(s, q, k, v, cos_v, sin_v, ld, h, reverse)
            _, vjp = jax.vjp(f, sv_ref[h], q_ref[:, sl], k_ref[:, sl], v_ref[:, sl], ld_ref[...])
            ds, dq, dk, dv, dld = vjp((do_ref[:, sl], dst_ref[h]))
            dst_ref[h] = ds
            dq_ref[:, sl] = dq
            dk_ref[:, sl] = dk
            dv_ref[:, sl] = dv
            dld_ref[...] += dld

    qkv, tab, ld = _ret_specs(back)
    tok = pl.BlockSpec((None, RET_CHUNK, RET_W), lambda b, i: (b, back(i), 0))
    return pl.pallas_call(
        body, grid=(bsz, n_ch),
        in_specs=[tok, *qkv,
                  pl.BlockSpec((None, None, RET_HEADS, RET_DH, RET_DH), lambda b, i: (b, n_ch - 1 - i, 0, 0, 0)),
                  tab, tab, ld],
        out_specs=[tok, tok, tok, ld],
        out_shape=[jax.ShapeDtypeStruct((bsz, n_tok, RET_W), f32)] * 3 + [jax.ShapeDtypeStruct((1, RET_DH), f32)],
        scratch_shapes=[pltpu.VMEM((RET_HEADS, RET_DH, RET_DH), f32)],
        compiler_params=_cparams(("arbitrary", "arbitrary")), name=name)(
        do, px, px, px, states, cos, sin, ld_row)


HALF_W = RW_W // 2


def _head_sum(x, ones):
    xm = x.astype(MXU_DTYPE)
    return jnp.concatenate([jnp.dot(xm[:, :HALF_W], ones, preferred_element_type=f32),
                            jnp.dot(xm[:, HALF_W:], ones, preferred_element_type=f32)], axis=1)


def _stack(parts):
    return jnp.concatenate(parts, axis=0)


def _row(ref, b, t):
    return ref[b, pl.ds(t, 1), :]


SCAN_DIRS = ((False, True), (True, False))
RW_HEADS = RW_W // RW_N
HEAD_ROWS_PAD = 16


def _head_rows(row, mask):
    return jnp.broadcast_to(row, mask.shape) * mask


def _outer(per_value, row, mask_pad):
    return lax.dot_general(per_value.astype(MXU_DTYPE), _head_rows(row, mask_pad).astype(MXU_DTYPE),
                           (((0,), (0,)), ((), ())), preferred_element_type=f32)


def _read(states, rows, mask):
    lhs = _stack([_head_rows(r, mask) for r in rows])
    return lax.dot_general(lhs.astype(MXU_DTYPE), _stack(states).astype(MXU_DTYPE), (((1,), (1,)), ((), ())),
                           preferred_element_type=f32)


def _row_from_heads(per_value, state, mask_pad):
    full = jnp.dot(per_value.astype(MXU_DTYPE), state.astype(MXU_DTYPE), preferred_element_type=f32)
    return jnp.sum(full * mask_pad, axis=0, keepdims=True)


def _scan_specs(bsz, order):
    rows = lambda col=0: pl.BlockSpec((bsz, SCAN_CHUNK, RW_W), lambda i: (0, order(i), col))
    per_value = pl.BlockSpec((bsz, SCAN_CHUNK, HEAD_ROWS_PAD, RW_N), lambda i: (0, order(i), 0, 0))
    return rows, per_value


def _removed(sp, kk_t, ones, bsz):
    removed = _head_sum(_stack([sp[b] * kk_t[b] for b in range(bsz)]), ones)
    return [removed[b * RW_N:(b + 1) * RW_N] for b in range(bsz)]


def _advance(sp, rem, w_t, b_t, vk, bsz):
    return [sp[b] * w_t[b] - rem[b] * b_t[b] + vk[b] for b in range(bsz)]


def heads_to_rows(a):
    b, t, _ = a.shape
    return jnp.pad(a.reshape(b, t, RW_HEADS, RW_N), ((0, 0), (0, 0), (0, HEAD_ROWS_PAD - RW_HEADS), (0, 0)))


def _blocks_to_rows(raw_ref, first, row_ref, bsz):
    steps = pl.ds(first, SCAN_CHUNK)
    for b in range(bsz):
        for h in range(RW_HEADS):
            row_ref[b, :, h * RW_N:(h + 1) * RW_N] = raw_ref[steps, RW_HEADS * b + h, RW_N * b:RW_N * (b + 1)]


N_ROWS_FWD = 5
N_ROWS_BWD = 5


def _scan_consts(bsz):
    head = (jnp.arange(RW_W)[None, :] // RW_N == jnp.arange(RW_HEADS)[:, None]).astype(f32)
    return head, jnp.pad(head, ((0, HEAD_ROWS_PAD - RW_HEADS), (0, 0))), _block_ones(HALF_W, RW_N)


def _const_specs(consts):
    return [pl.BlockSpec(c.shape, lambda i: (0, 0)) for c in consts]


def rwkv_scan_fwd(rows_in, v_heads, orders, name):
    bsz, n_tok, _ = rows_in[0][0][0].shape
    n_ch = n_tok // SCAN_CHUNK
    rng = range(bsz)
    consts = _scan_consts(bsz)

    def body(*refs):
        rows = [refs[:N_ROWS_FWD], refs[N_ROWS_FWD:2 * N_ROWS_FWD]]
        v0, v1, head_ref, pad_ref, ones_ref, y0, y1, cs0, cs1, s0, s1, late_ref, raw_ref = refs[2 * N_ROWS_FWD:]
        v_refs, y_refs, cs_refs, s_refs = (v0, v1), (y0, y1), (cs0, cs1), (s0, s1)
        head_v, pad_v, ones_v = head_ref[...], pad_ref[...], ones_ref[...]
        for d in range(2):
            @pl.when(pl.program_id(0) == 0)
            def _(d=d):
                s_refs[d][...] = jnp.zeros_like(s_refs[d])

            cs_refs[d][...] = s_refs[d][...]

        def step(j, carry):
            ts = [SCAN_CHUNK - 1 - j if reverse else j for reverse, _ in SCAN_DIRS]
            sps = [[s_refs[d][b] for b in rng] for d in range(2)]
            rems = [_removed(sps[d], [_row(rows[d][1], b, ts[d]) for b in rng], ones_v, bsz) for d in range(2)]
            vks = [[_outer(v_refs[d][b, ts[d]], _row(rows[d][4], b, ts[d]), pad_v) for b in rng] for d in range(2)]
            for d, (reverse, inclusive) in enumerate(SCAN_DIRS):
                r_ref = rows[d][0]
                if inclusive:
                    before = jnp.maximum(j - 1, 0)
                    late_ref[j] = _read(sps[d], [_row(r_ref, b, before) for b in rng], head_v)
                else:
                    raw_ref[ts[d]] = _read(sps[d], [_row(r_ref, b, ts[d]) for b in rng], head_v)
            for d in range(2):
                new = _advance(sps[d], rems[d], [_row(rows[d][2], b, ts[d]) for b in rng],
                               [_row(rows[d][3], b, ts[d]) for b in rng], vks[d], bsz)
                for b in rng:
                    s_refs[d][b] = new[b]
            return carry

        lax.fori_loop(0, SCAN_CHUNK, step, 0, unroll=SCAN_UNROLL)
        for d, (reverse, inclusive) in enumerate(SCAN_DIRS):
            if inclusive:
                assert not reverse
                last = SCAN_CHUNK - 1
                late_ref[SCAN_CHUNK] = _read([s_refs[d][b] for b in rng], [rows[d][0][b, last:last + 1, :] for b in rng],
                                             head_v)
                _blocks_to_rows(late_ref, 1, y_refs[d], bsz)
            else:
                _blocks_to_rows(raw_ref, 0, y_refs[d], bsz)

    specs = [_scan_specs(bsz, orders[d]) for d in range(2)]
    state = pltpu.VMEM((bsz, RW_N, RW_W), f32)
    late = pltpu.VMEM((SCAN_CHUNK + 1, RW_HEADS * bsz, RW_N * bsz), f32)
    raw = pltpu.VMEM((SCAN_CHUNK, RW_HEADS * bsz, RW_N * bsz), f32)
    start_spec = pl.BlockSpec((None, bsz, RW_N, RW_W), lambda i: (i, 0, 0, 0))
    return pl.pallas_call(
        body, grid=(n_ch,),
        in_specs=[specs[d][0](col) for d in range(2) for _, col in rows_in[d]] + [specs[0][1], specs[1][1]]
        + _const_specs(consts),
        out_specs=[specs[0][0](), specs[1][0](), start_spec, start_spec],
        out_shape=[jax.ShapeDtypeStruct((bsz, n_tok, RW_W), f32)] * 2
        + [jax.ShapeDtypeStruct((n_ch, bsz, RW_N, RW_W), f32)] * 2,
        scratch_shapes=[state, state, late, raw],
        compiler_params=_cparams(("arbitrary",)), name=name)(
        *[a for d in range(2) for a, _ in rows_in[d]], v_heads, v_heads, *consts)


def rwkv_scan_bwd(rows_in, v_heads, dy_heads, starts, orders, name):
    bsz, n_tok, _ = rows_in[0][0][0].shape
    n_ch = n_tok // SCAN_CHUNK
    backs = [functools.partial(lambda i, order: order(n_ch - 1 - i), order=orders[d]) for d in range(2)]
    rng = range(bsz)
    consts = _scan_consts(bsz)
    n_out, n_scr = 6, 8

    def body(*refs):
        rows = [refs[:N_ROWS_BWD], refs[N_ROWS_BWD:2 * N_ROWS_BWD]]
        rest = refs[2 * N_ROWS_BWD:]
        v_refs, dy_refs, cs_refs, (head_ref, pad_ref, ones_ref) = rest[0:2], rest[2:4], rest[4:6], rest[6:9]
        outs = [rest[9:9 + n_out], rest[9 + n_out:9 + 2 * n_out]]
        scr = [rest[9 + 2 * n_out:9 + 2 * n_out + n_scr], rest[9 + 2 * n_out + n_scr:]]
        head_v, pad_v, ones_v = head_ref[...], pad_ref[...], ones_ref[...]
        for d in range(2):
            s_ref, ds_ref = scr[d][:2]

            @pl.when(pl.program_id(0) == 0)
            def _(ds_ref=ds_ref):
                ds_ref[...] = jnp.zeros_like(ds_ref)

            s_ref[...] = cs_refs[d][...]

        def fstep(j, carry):
            ts = [SCAN_CHUNK - 1 - j if reverse else j for reverse, _ in SCAN_DIRS]
            sps = [[scr[d][0][b] for b in rng] for d in range(2)]
            rems = [_removed(sps[d], [_row(rows[d][1], b, ts[d]) for b in rng], ones_v, bsz) for d in range(2)]
            vks = [[_outer(v_refs[d][b, ts[d]], _row(rows[d][4], b, ts[d]), pad_v) for b in rng] for d in range(2)]
            for d in range(2):
                s_ref, _, hist_ref, rem_ref = scr[d][:4]
                new = _advance(sps[d], rems[d], [_row(rows[d][2], b, ts[d]) for b in rng],
                               [_row(rows[d][3], b, ts[d]) for b in rng], vks[d], bsz)
                for b in rng:
                    hist_ref[ts[d], b] = sps[d][b]
                    rem_ref[ts[d], b] = rems[d][b]
                    s_ref[b] = new[b]
            return carry

        lax.fori_loop(0, SCAN_CHUNK, fstep, 0, unroll=SCAN_UNROLL)

        def step_of(j, reverse):
            return j if reverse else SCAN_CHUNK - 1 - j

        for d, (reverse, _) in enumerate(SCAN_DIRS):
            t0 = step_of(0, reverse)
            for b in rng:
                scr[d][6][b] = _outer(dy_refs[d][b, t0], rows[d][0][b, t0:t0 + 1, :], pad_v)

        def bstep(j, carry):
            ts = [step_of(j, reverse) for reverse, _ in SCAN_DIRS]
            reads = [[scr[d][6][b] for b in rng] for d in range(2)]
            dss = []
            for d, (_, inclusive) in enumerate(SCAN_DIRS):
                ds = [scr[d][1][b] for b in rng]
                dss.append([ds[b] + reads[d][b] for b in rng] if inclusive else ds)
            drems = [_removed(dss[d], [-_row(rows[d][3], b, ts[d]) for b in rng], ones_v, bsz) for d in range(2)]
            for d, (reverse, _) in enumerate(SCAN_DIRS):
                t_next = step_of(jnp.minimum(j + 1, SCAN_CHUNK - 1), reverse)
                for b in rng:
                    scr[d][6][b] = _outer(dy_refs[d][b, t_next], _row(rows[d][0], b, t_next), pad_v)
                scr[d][7][ts[d]] = _read(dss[d], [_row(rows[d][4], b, ts[d]) for b in rng], head_v)
            for d, (_, inclusive) in enumerate(SCAN_DIRS):
                _, kk_ref, w_ref, _, _ = rows[d]
                _, ds_ref, _, _, dsh_ref, drem_ref = scr[d][:6]
                for b in rng:
                    dsh_ref[ts[d], b] = dss[d][b]
                    drem_ref[ts[d], b] = drems[d][b]
                    dsp = dss[d][b] * _row(w_ref, b, ts[d]) + drems[d][b] * _row(kk_ref, b, ts[d])
                    ds_ref[b] = dsp if inclusive else dsp + reads[d][b]
            return carry

        lax.fori_loop(0, SCAN_CHUNK, bstep, 0, unroll=SCAN_UNROLL)

        rsum = lambda z: jnp.sum(z, axis=0, keepdims=True)
        for d, (reverse, inclusive) in enumerate(SCAN_DIRS):
            dr_ref, dkk_ref, dw_ref, db_ref, dkt_ref, dv_ref = outs[d]
            s_ref, _, hist_ref, rem_ref, dsh_ref, drem_ref, _, dv_raw_ref = scr[d]
            _blocks_to_rows(dv_raw_ref, 0, dv_ref, bsz)
            for t in range(SCAN_CHUNK):
                ts = slice(t, t + 1)
                after = t - 1 if reverse else t + 1
                for b in rng:
                    sp, ds = hist_ref[t, b], dsh_ref[t, b]
                    if not inclusive:
                        seen = sp
                    else:
                        seen = hist_ref[after, b] if 0 <= after < SCAN_CHUNK else s_ref[b]
                    dr_ref[b, ts, :] = _row_from_heads(dy_refs[d][b, t], seen, pad_v)
                    dkt_ref[b, ts, :] = _row_from_heads(v_refs[d][b, t], ds, pad_v)
                    dw_ref[b, ts, :] = rsum(ds * sp)
                    db_ref[b, ts, :] = -rsum(ds * rem_ref[t, b])
                    dkk_ref[b, ts, :] = rsum(sp * drem_ref[t, b])

    specs = [_scan_specs(bsz, backs[d]) for d in range(2)]
    hist = pltpu.VMEM((SCAN_CHUNK, bsz, RW_N, RW_W), f32)
    state = pltpu.VMEM((bsz, RW_N, RW_W), f32)
    start_spec = pl.BlockSpec((None, bsz, RW_N, RW_W), lambda i: (n_ch - 1 - i, 0, 0, 0))
    raw = pltpu.VMEM((SCAN_CHUNK, RW_HEADS * bsz, RW_N * bsz), f32)
    return pl.pallas_call(
        body, grid=(n_ch,),
        in_specs=[specs[d][0](col) for d in range(2) for _, col in rows_in[d]]
        + [specs[0][1], specs[1][1]] * 2 + [start_spec, start_spec] + _const_specs(consts),
        out_specs=[specs[d][0]() for d in range(2) for _ in range(n_out)],
        out_shape=[jax.ShapeDtypeStruct((bsz, n_tok, RW_W), f32)] * (2 * n_out),
        scratch_shapes=[state, state, hist, hist, hist, hist, state, raw] * 2,
        compiler_params=_cparams(("arbitrary",)), name=name)(
        *[a for d in range(2) for a, _ in rows_in[d]], v_heads, v_heads, dy_heads, dy_heads, *starts, *consts)


MOD_NAMES = ("shift1", "scale1", "gate1", "shift2", "scale2", "gate2")


def _rope_tables(t_ctx, t_x):
    quarter = RET_DH // 4
    pos = jnp.arange(t_x)
    inv = jnp.power(ROPE_BASE, -jnp.arange(0, 2 * quarter, 2, dtype=f32) / (2 * quarter))
    ang_r = (pos // GRID_W).astype(f32)[:, None] * inv[None, :]
    ang_c = (pos % GRID_W).astype(f32)[:, None] * inv[None, :]
    cos = jnp.concatenate([jnp.cos(ang_r)] * 2 + [jnp.cos(ang_c)] * 2, axis=1)
    sin = jnp.concatenate([-jnp.sin(ang_r), jnp.sin(ang_r), -jnp.sin(ang_c), jnp.sin(ang_c)], axis=1)
    cos = jnp.concatenate([jnp.ones((t_ctx, RET_DH), f32), cos], axis=0)
    sin = jnp.concatenate([jnp.zeros((t_ctx, RET_DH), f32), sin], axis=0)
    return cos, sin


def _pad_rows(w, lo, total):
    return jnp.pad(w, ((lo, total - lo - w.shape[0]), (0, 0)))


LATE_WEIGHTS = ("w_out", "w_ff1", "w_ff2")


def layer_step(x, ctx, tgt, mod_x, mod_ctx, wt, late_weights=None, early_grads=None):
    bsz, t_x, _ = x.shape
    t_c = ctx.shape[1]
    t_all = t_c + t_x
    n_ct, n_xt = t_c // TOK_TILE, t_x // TOK_TILE
    n_t = n_ct + n_xt
    assert t_c % TOK_TILE == 0 and t_x % TOK_TILE == 0 and t_c % RET_CHUNK == 0

    seg = lambda i: (i >= n_ct).astype(jnp.int32)
    seg_first = lambda i: jnp.logical_or(i == 0, i == n_ct)
    seg_last = lambda i: jnp.logical_or(i == n_ct - 1, i == n_t - 1)
    mod_all = {n: jnp.stack([jnp.broadcast_to(mod_ctx[k], (bsz, D_MODEL)), mod_x[:, k]], axis=1)[:, :, None, :]
               for k, n in enumerate(MOD_NAMES)}
    mod_lat = {n: mod_x[:, k][:, None, None, :] for k, n in enumerate(MOD_NAMES)}
    both = lambda n: Seg(mod_all[n], seg, seg_first)
    lat = lambda n: Seg(mod_lat[n], lambda i: 0, lambda i: i == 0)
    flat = lambda a: a.reshape(-1, a.shape[-1])

    def chunk_orders(n_ctx_chunks, n_chunks):
        fwd = lambda i: i
        bwd = lambda i: jnp.where(i < n_ctx_chunks, n_ctx_chunks - 1 - i, n_chunks + n_ctx_chunks - 1 - i)
        return fwd, bwd

    ones64, ones128 = _block_ones(RW_W, RW_N), _block_ones(RET_W, RET_DH)
    cos, sin = _rope_tables(t_c, t_x)
    ld_rows = [jnp.pad(wt["ret_log_decay"][d][None, :], ((0, 0), (0, RET_DH - RET_HEADS))) for d in range(2)]
    w_up_pad = [_pad_rows(wt["rwkv_w_up"][d], 0, LORA_W) for d in range(2)]
    a_up_pad = [_pad_rows(wt["rwkv_a_up"][d], DECAY_LORA, LORA_W) for d in range(2)]
    g_up_pad = _pad_rows(wt["rwkv_g_up"], DECAY_LORA + AAA_LORA, LORA_W)
    row = lambda a, d: a[d][None, :]

    h = jnp.concatenate([ctx, x], axis=1)
    norm1_ins = lambda: [Tiled(h), both("shift1"), both("scale1"), Glob(wt["norm1_g"])]
    (n1,) = ew_forward(fn_norm_mod, "norm1", bsz, n_t, norm1_ins(), [(D_MODEL, MXU_DTYPE)])
    px = matmul(flat(n1), wt["w_in"], "nn", "proj_in").reshape(bsz, t_all, IN_COLS)
    px_rw = px[..., RET_COLS:]
    ps = token_shift(px_rw, wt["rwkv_shift_mu"], seg_first, seg_last)

    def prep_ins(toff=0):
        return [Tiled(ps, RW_W, 1), Tiled(ps, LORA_W, 3 * RW_W // LORA_W),
                Glob(row(wt["rwkv_w0"], 0)), Glob(row(wt["rwkv_w0"], 1)),
                Glob(row(wt["rwkv_a0"], 0)), Glob(row(wt["rwkv_a0"], 1)),
                Glob(w_up_pad[0]), Glob(w_up_pad[1]), Glob(a_up_pad[0]), Glob(a_up_pad[1]), Glob(g_up_pad),
                Glob(wt["rwkv_k_k"]), Glob(wt["rwkv_k_a"]), Glob(ones64)]

    kk, w_f, b_f, kt_f, w_b, b_b, kt_b, g_rw = ew_forward(fn_rwkv_prepare, "rwkv_prepare", bsz, n_t, prep_ins(),
                                                           [(RW_W, f32)] * 8)
    rw_order = chunk_orders(t_c // SCAN_CHUNK, t_all // SCAN_CHUNK)
    ret_order = chunk_orders(t_c // RET_CHUNK, t_all // RET_CHUNK)
    scan_rows = [[(ps, 0), (kk, 0), (w_f, 0), (b_f, 0), (kt_f, 0)], [(ps, 0), (kk, 0), (w_b, 0), (b_b, 0), (kt_b, 0)]]
    v_heads = heads_to_rows(ps[..., 2 * RW_W:3 * RW_W])
    *y, start_f, start_b = rwkv_scan_fwd(scan_rows, v_heads, rw_order, "rwkv_scan_fwd")
    o, ret_states = [], []
    for d in range(2):
        o_d, st_d = retention_fwd(px, cos, sin, ld_rows[d], ret_order[d], SCAN_DIRS[d][0], f"retention_fwd{d}")
        o.append(o_d), ret_states.append(st_d)

    def merge_ins(toff):
        return [Tiled(o[0], toff=toff), Tiled(o[1], toff=toff), Tiled(px, RET_W, 3, toff),
                Tiled(y[0], toff=toff), Tiled(y[1], toff=toff), Tiled(ps, RW_W, 0, toff), Tiled(kt_f, toff=toff),
                Tiled(ps, RW_W, 2, toff), Tiled(g_rw, toff=toff),
                Glob(wt["rwkv_r_k"]), Glob(wt["rwkv_ln_w"]), Glob(wt["rwkv_ln_b"]), Glob(ones64), Glob(ones128)]

    ret_out, rw_out = ew_forward(fn_merge, "merge_heads", bsz, n_xt, merge_ins(n_ct),
                                 [(RET_W, MXU_DTYPE), (RW_W, MXU_DTYPE)])
    merged = jnp.concatenate([ret_out, rw_out], axis=-1)
    if late_weights is not None:
        wt = {**wt, **late_weights(merged)}
    mix = matmul(flat(merged), wt["w_out"], "nn", "proj_out").reshape(bsz, t_x, D_MODEL)
    resid_ins = lambda: [Tiled(x), Tiled(mix), lat("gate1"), lat("shift2"), lat("scale2"), Glob(wt["norm2_g"])]
    h1, n2 = ew_forward(fn_resid_norm_mod, "resid_norm2", bsz, n_xt, resid_ins(), [(D_MODEL, f32), (D_MODEL, MXU_DTYPE)])
    act = matmul(flat(n2), wt["w_ff1"], "nn", "ff1", MXU_DTYPE, wt["b_ff1"], relu2).reshape(bsz, t_x, D_FF)
    ff = matmul(flat(act), wt["w_ff2"], "nn", "ff2").reshape(bsz, t_x, D_MODEL)

    g = {}
    loss, dh1, dff, dgate2, g["b_ff2"], g["final_g"] = loss_and_grads(
        h1, ff, tgt, mod_lat["gate2"], wt["b_ff2"], wt["final_g"], bsz, n_xt)
    dact = matmul(flat(dff), wt["w_ff2"], "nt", "ff2_dx", MXU_DTYPE).reshape(bsz, t_x, D_FF)
    g["w_ff2"] = matmul(flat(act), flat(dff), "tn", "ff2_dw")
    du, g["b_ff1"] = relu2_backward(act, dact, "relu2_bwd")
    dn2 = matmul(flat(du), wt["w_ff1"], "nt", "ff1_dx").reshape(bsz, t_x, D_MODEL)
    g["w_ff1"] = matmul(flat(n2), flat(du), "tn", "ff1_dw")
    dx_res, dmix, dgate1, dshift2, dscale2, g["norm2_g"] = ew_backward(
        fn_resid_norm_mod, "resid_norm2_bwd", bsz, n_xt, resid_ins(), [Tiled(dh1), Tiled(dn2)], [True] * 6,
        {1: MXU_DTYPE})
    dmerged = matmul(flat(dmix), wt["w_out"], "nt", "proj_out_dx").reshape(bsz, t_x, D_MODEL)
    g["w_out"] = matmul(flat(merged), flat(dmix), "tn", "proj_out_dw")
    if early_grads is not None:
        token = early_grads({n: g.pop(n) for n in LATE_WEIGHTS})
        wt = {**wt, "rwkv_r_k": wt["rwkv_r_k"] + token[:1, :1]}
    (do, dg_ret, dy, dr_m, dkt_m, dv_m, dg_rw, g["rwkv_r_k"], g["rwkv_ln_w"], g["rwkv_ln_b"]) = ew_backward(
        fn_merge, "merge_heads_bwd", bsz, n_xt, merge_ins(0),
        [Tiled(dmerged, RET_W, 0, -n_ct), Tiled(dmerged, RW_W, 1, -n_ct)],
        [True, False, True, True, False, True, True, True, True, True, True, True, False, False], lead=n_ct)

    dqkv, dld = [], []
    for d in range(2):
        *dqkv_d, dld_d = retention_bwd(do, px, ret_states[d], cos, sin, ld_rows[d], ret_order[d],
                                       SCAN_DIRS[d][0], f"retention_bwd{d}")
        dqkv.append(dqkv_d), dld.append(dld_d[0, :RET_HEADS])
    g["ret_log_decay"] = jnp.stack(dld)
    (dr_f, dkk_f, dw_f, db_f, dkt_f, dv_f, dr_b, dkk_b, dw_b, db_b, dkt_b, dv_b) = rwkv_scan_bwd(
        scan_rows, v_heads, heads_to_rows(dy), (start_f, start_b), rw_order, "rwkv_scan_bwd")
    prep_cts = [dkk_f + dkk_b, dw_f, db_f, dkt_f + dkt_m, dw_b, db_b, dkt_b, dg_rw]
    (dks, dlora, dw0_f, dw0_b, da0_f, da0_b, dwup_f, dwup_b, daup_f, daup_b, dgup, g["rwkv_k_k"],
     g["rwkv_k_a"]) = ew_backward(fn_rwkv_prepare, "rwkv_prepare_bwd", bsz, n_t, prep_ins(),
                                  [Tiled(c) for c in prep_cts], [True] * 13 + [False])
    g["rwkv_w0"] = jnp.concatenate([dw0_f, dw0_b], axis=0)
    g["rwkv_a0"] = jnp.concatenate([da0_f, da0_b], axis=0)
    g["rwkv_w_up"] = jnp.stack([dwup_f[:DECAY_LORA], dwup_b[:DECAY_LORA]])
    g["rwkv_a_up"] = jnp.stack([daup_f[DECAY_LORA:DECAY_LORA + AAA_LORA], daup_b[DECAY_LORA:DECAY_LORA + AAA_LORA]])
    g["rwkv_g_up"] = dgup[DECAY_LORA + AAA_LORA:]
    dps = jnp.concatenate([dr_f + dr_b + dr_m, dks, dv_f + dv_b + dv_m, dlora], axis=-1)
    dp_rw, g["rwkv_shift_mu"] = token_shift_bwd(dps, px_rw, wt["rwkv_shift_mu"], seg_first, seg_last)
    dpx = jnp.concatenate([(dqkv[0][k] + dqkv[1][k]).astype(MXU_DTYPE) for k in range(3)]
                          + [dg_ret.astype(MXU_DTYPE), dp_rw], axis=-1)
    dn1 = matmul(flat(dpx), wt["w_in"], "nt", "proj_in_dx").reshape(bsz, t_all, D_MODEL)
    g["w_in"] = matmul(flat(n1), flat(dpx), "tn", "proj_in_dw")
    dh, dshift1, dscale1, g["norm1_g"] = ew_backward(fn_norm_mod, "norm1_bwd", bsz, n_t, norm1_ins(), [Tiled(dn1)],
                                                     [True] * 4)
    grad_x = dh[:, t_c:] + dx_res
    zeros = jnp.zeros((D_MODEL,), f32)
    g["mod_x"] = jnp.stack([dshift1[:, 1, 0], dscale1[:, 1, 0], dgate1[:, 0, 0], dshift2[:, 0, 0], dscale2[:, 0, 0],
                            dgate2[:, 0, 0]], axis=1)
    g["mod_ctx"] = jnp.stack([dshift1[:, 0, 0].sum(0), dscale1[:, 0, 0].sum(0), zeros, zeros, zeros, zeros])
    return loss, grad_x, g


MESH_ID = pl.DeviceIdType.MESH
ALL_PEERS = [(dx, dy, dc) for dx in (0, 1) for dy in (0, 1) for dc in (0, 1)][1:]
CHIP_PEERS = [(1, 0, 0), (0, 1, 0), (1, 1, 0)]
CHIP_SLOTS = (0, 2, 4, 6)


def _mesh_pos():
    return lax.axis_index("x"), lax.axis_index("y"), lax.axis_index("c")


def _device_slot():
    x, y, c = _mesh_pos()
    return 4 * x + 2 * y + c


def sibling_swap(arrs, name, pieces=1):
    n = len(arrs)
    assert all(a.shape[0] % pieces == 0 for a in arrs)

    def body(*refs):
        in_refs, out_refs = refs[:n], refs[n:2 * n]
        send_sems, recv_sems = refs[2 * n:]
        x, y, c = _mesh_pos()
        copies = []
        for a in range(n):
            rows = arrs[a].shape[0] // pieces
            for q in range(pieces):
                part = pl.ds(q * rows, rows)
                cp = pltpu.make_async_remote_copy(
                    src_ref=in_refs[a].at[part], dst_ref=out_refs[a].at[part], send_sem=send_sems.at[a * pieces + q],
                    recv_sem=recv_sems.at[a * pieces + q], device_id=(x, y, 1 - c), device_id_type=MESH_ID)
                cp.start()
                copies.append(cp)
        for cp in copies:
            cp.wait()

    any_spec = pl.BlockSpec(memory_space=pl.ANY)
    res = pl.pallas_call(
        body, in_specs=[any_spec] * n, out_specs=[any_spec] * n,
        out_shape=[jax.ShapeDtypeStruct(a.shape, a.dtype) for a in arrs],
        scratch_shapes=[pltpu.SemaphoreType.DMA((n * pieces,)), pltpu.SemaphoreType.DMA((n * pieces,))],
        name=name)(*arrs)
    return list(res)


def exchange(arrs, gather, peers, name, pieces=1, by_chip=False, own=True):
    n, n_peers = len(arrs), len(peers)
    n_slots = N_SHARDS if by_chip else N_DEV
    slot = (lambda x, y, c: 2 * x + y) if by_chip else (lambda x, y, c: 4 * x + 2 * y + c)
    block_rows = [a.shape[0] if gather else a.shape[1] for a in arrs]
    assert all(r % pieces == 0 for r in block_rows), (block_rows, pieces)

    def body(*refs):
        in_refs, out_refs = refs[:n], refs[n:2 * n]
        send_sems, recv_sems, local_sems = refs[2 * n:]
        x, y, c = _mesh_pos()
        me = slot(x, y, c)
        copies, locals_ = [], []
        for a in range(n):
            if own:
                mine = in_refs[a] if gather else in_refs[a].at[me]
                loc = pltpu.make_async_copy(mine, out_refs[a].at[me], local_sems.at[a])
                loc.start()
                locals_.append(loc)
            for k, (dx, dy, dc) in enumerate(peers):
                peer = (1 - x if dx else x, 1 - y if dy else y, 1 - c if dc else c)
                src = in_refs[a] if gather else in_refs[a].at[slot(*peer)]
                for q in range(pieces):
                    part = pl.ds(q * (block_rows[a] // pieces), block_rows[a] // pieces)
                    sem = (a * n_peers + k) * pieces + q
                    cp = pltpu.make_async_remote_copy(
                        src_ref=src.at[part], dst_ref=out_refs[a].at[me, part], send_sem=send_sems.at[sem],
                        recv_sem=recv_sems.at[sem], device_id=peer, device_id_type=MESH_ID)
                    cp.start()
                    copies.append(cp)
        for cp in copies:
            cp.wait()
        for loc in locals_:
            loc.wait()

    any_spec = pl.BlockSpec(memory_space=pl.ANY)
    out_shape = [jax.ShapeDtypeStruct((n_slots,) + (a.shape if gather else a.shape[1:]), a.dtype) for a in arrs]
    n_sems = n * n_peers * pieces
    res = pl.pallas_call(
        body, in_specs=[any_spec] * n, out_specs=[any_spec] * n, out_shape=out_shape,
        scratch_shapes=[pltpu.SemaphoreType.DMA((n_sems,)), pltpu.SemaphoreType.DMA((n_sems,)),
                        pltpu.SemaphoreType.DMA((n,))],
        name=name)(*arrs)
    return list(res)


HBM_SPEC = pl.BlockSpec(memory_space=pltpu.HBM)
SEM_SPEC = pl.BlockSpec(memory_space=pltpu.SEMAPHORE)
DATAFLOW = pltpu.SideEffectType.DATAFLOW_SIDE_EFFECTING


def _peer_copies(src_refs, land_refs, send_sems, recv_sems, gather):
    x, y, c = _mesh_pos()
    me = 4 * x + 2 * y + c
    copies = []
    for a, (src_ref, land_ref) in enumerate(zip(src_refs, land_refs)):
        for k, (dx, dy, dc) in enumerate(ALL_PEERS):
            peer = (1 - x if dx else x, 1 - y if dy else y, 1 - c if dc else c)
            src = src_ref if gather else src_ref.at[4 * peer[0] + 2 * peer[1] + peer[2]]
            sem = a * len(ALL_PEERS) + k
            copies.append(pltpu.make_async_remote_copy(src_ref=src, dst_ref=land_ref.at[me], send_sem=send_sems.at[sem],
                                                       recv_sem=recv_sems.at[sem], device_id=peer,
                                                       device_id_type=MESH_ID))
    return copies


def exchange_start(arrs, gather, name):
    n = len(arrs)
    lands = [lax.empty((N_DEV,) + (a.shape if gather else a.shape[1:]), a.dtype) for a in arrs]

    def body(*refs):
        for cp in _peer_copies(refs[:n], refs[n:2 * n], refs[2 * n], refs[2 * n + 1], gather):
            cp.start()
        refs[-1][...] = jnp.zeros_like(refs[-1])

    sems = pltpu.SemaphoreType.DMA((n * len(ALL_PEERS),))
    hbm = [pltpu.HBM(a.shape, a.dtype) for a in arrs + lands]
    res = pl.pallas_call(
        body, name=name, out_shape=(sems, sems, *hbm, jax.ShapeDtypeStruct((8, 128), f32)),
        in_specs=[HBM_SPEC] * (2 * n),
        out_specs=(SEM_SPEC, SEM_SPEC, *[HBM_SPEC] * (2 * n), pl.BlockSpec(memory_space=pltpu.VMEM)),
        input_output_aliases={i: 2 + i for i in range(2 * n)},
        compiler_params=pltpu.CompilerParams(has_side_effects=DATAFLOW))(
        *[pltpu.with_memory_space_constraint(a, pltpu.HBM) for a in arrs + lands])
    return res[0], res[1], list(res[2:2 + n]), list(res[2 + n:2 + 2 * n]), res[-1]


def exchange_wait(started, after, gather, name):
    send_sems, recv_sems, srcs, lands, _ = started
    n = len(srcs)

    def body(*refs):
        for cp in _peer_copies(refs[:n], refs[n:2 * n], refs[2 * n], refs[2 * n + 1], gather):
            cp.wait_send()
            cp.wait_recv()

    res = pl.pallas_call(
        body, name=name, out_shape=tuple(pltpu.HBM(a.shape, a.dtype) for a in srcs + lands),
        in_specs=[HBM_SPEC] * (2 * n) + [SEM_SPEC, SEM_SPEC, pl.BlockSpec(memory_space=pl.ANY)],
        out_specs=tuple([HBM_SPEC] * (2 * n)), input_output_aliases={i: i for i in range(2 * n)},
        compiler_params=pltpu.CompilerParams(has_side_effects=DATAFLOW))(*srcs, *lands, send_sems, recv_sems, after)
    return list(res[n:])


def add_arrays(parts, name, out_dtype=f32):
    r, c = parts[0].shape
    tr = r
    for cand in (512, 256, 128, 64, 32, 16):
        if r % cand == 0:
            tr = cand
            break

    def body(*refs):
        acc = refs[0][...].astype(f32)
        for p_ref in refs[1:-1]:
            acc = acc + p_ref[...].astype(f32)
        refs[-1][...] = acc.astype(out_dtype)

    spec = pl.BlockSpec((tr, c), lambda i: (i, 0))
    return pl.pallas_call(body, grid=(r // tr,), in_specs=[spec] * len(parts), out_specs=spec,
                          out_shape=jax.ShapeDtypeStruct((r, c), out_dtype),
                          compiler_params=_cparams(("parallel",)), name=name)(*parts)


def gather_two_level(arrs, name):
    n = len(arrs)
    per = 7

    def body(*refs):
        in_refs, out_refs = refs[:n], refs[n:2 * n]
        send_sems, recv_sems = refs[2 * n:]
        x, y, c = _mesh_pos()
        me, sibling = (x, y, c), (x, y, 1 - c)
        chips = [(1 - x, y), (x, 1 - y), (1 - x, 1 - y)]

        def copy(a, k, block, to, src=None):
            rows = out_refs[a].at[4 * block[0] + 2 * block[1] + block[2]]
            return pltpu.make_async_remote_copy(src_ref=rows if src is None else src, dst_ref=rows,
                                                send_sem=send_sems.at[a * per + k], recv_sem=recv_sems.at[a * per + k],
                                                device_id=to, device_id_type=MESH_ID)

        first, passed = [], []
        for a in range(n):
            first.append(copy(a, 0, me, sibling, src=in_refs[a]))
            first += [copy(a, 1 + j, me, (*chip, c), src=in_refs[a]) for j, chip in enumerate(chips)]
        for cp in first:
            cp.start()
        for a in range(n):
            for j, chip in enumerate(chips):
                copy(a, 1 + j, (*chip, c), me).wait_recv()
                fwd = copy(a, 4 + j, (*chip, c), sibling)
                fwd.start()
                passed.append(fwd)
        for a in range(n):
            copy(a, 0, sibling, me).wait_recv()
            for j, chip in enumerate(chips):
                copy(a, 4 + j, (*chip, 1 - c), me).wait_recv()
        for cp in first + passed:
            cp.wait_send()

    any_spec = pl.BlockSpec(memory_space=pl.ANY)
    res = pl.pallas_call(
        body, in_specs=[any_spec] * n, out_specs=[any_spec] * n,
        out_shape=[jax.ShapeDtypeStruct((N_DEV,) + a.shape, a.dtype) for a in arrs],
        scratch_shapes=[pltpu.SemaphoreType.DMA((n * per,)), pltpu.SemaphoreType.DMA((n * per,))],
        name=name)(*arrs)
    return list(res)


def sum_slots(parts, slots, name):
    _, r, c = parts.shape
    tr = r
    for cand in (512, 256, 128, 64, 32, 16, 8):
        if r % cand == 0 and cand * c * 4 * len(slots) <= 8 * 1024 * 1024:
            tr = cand
            break

    def body(p_ref, o_ref):
        acc = p_ref[slots[0]].astype(f32)
        for s in slots[1:]:
            acc = acc + p_ref[s].astype(f32)
        o_ref[...] = acc

    return pl.pallas_call(body, grid=(r // tr,), in_specs=[pl.BlockSpec((parts.shape[0], tr, c), lambda i: (0, i, 0))],
                          out_specs=pl.BlockSpec((tr, c), lambda i: (i, 0)),
                          out_shape=jax.ShapeDtypeStruct((r, c), f32),
                          compiler_params=_cparams(("parallel",)), name=name)(parts)


def column_sum(a, name):
    def body(a_ref, o_ref):
        o_ref[...] = jnp.sum(a_ref[...], axis=0, keepdims=True)

    return pl.pallas_call(body, out_shape=jax.ShapeDtypeStruct((1, a.shape[1]), f32), name=name)(a)


def adamw(w, g, m, v, name):
    r, c = w.shape
    tr = r
    for cand in (256, 128, 64, 32, 16, 8):
        if r % cand == 0:
            tr = cand
            break

    def body(w_ref, g_ref, m_ref, v_ref, d_ref, mo_ref, vo_ref):
        gv = g_ref[...]
        m_new = ADAM_B1 * m_ref[...] + (1.0 - ADAM_B1) * gv
        v_new = ADAM_B2 * v_ref[...] + (1.0 - ADAM_B2) * jnp.square(gv)
        m_hat = m_new / (1.0 - ADAM_B1 ** ADAM_STEP)
        v_hat = v_new / (1.0 - ADAM_B2 ** ADAM_STEP)
        d_ref[...] = -ADAM_LR * (m_hat / (jnp.sqrt(v_hat) + ADAM_EPS) + ADAM_WD * w_ref[...])
        mo_ref[...] = m_new
        vo_ref[...] = v_new

    spec = pl.BlockSpec((tr, c), lambda i: (i, 0))
    return pl.pallas_call(body, grid=(r // tr,), in_specs=[spec] * 4, out_specs=[spec] * 3,
                          out_shape=[jax.ShapeDtypeStruct((r, c), f32)] * 3,
                          compiler_params=_cparams(("parallel",)), name=name)(w, g, m, v)


def adaln_fwd(c_rows, w, b):
    def body(c_ref, w_ref, b_ref, o_ref):
        cv = c_ref[...]
        o_ref[...] = _mxu_dot(cv * jax.nn.sigmoid(cv), w_ref[...]) + b_ref[...]

    return pl.pallas_call(body, out_shape=jax.ShapeDtypeStruct((c_rows.shape[0], w.shape[1]), f32),
                          compiler_params=pltpu.CompilerParams(vmem_limit_bytes=VMEM_LIMIT), name="adaln_fwd")(c_rows, w, b)


def adaln_bwd(c_rows, dm, w):
    def body(c_ref, dm_ref, w_ref, gw_ref, ds_ref):
        cv = c_ref[...]
        gw_ref[...] = _dg(cv * jax.nn.sigmoid(cv), dm_ref[...], 0, 0)
        ds_ref[...] = _dg(dm_ref[...], w_ref[...], 1, 1)

    return pl.pallas_call(body, out_shape=[jax.ShapeDtypeStruct(w.shape, f32),
                                           jax.ShapeDtypeStruct(c_rows.shape, f32)],
                          compiler_params=pltpu.CompilerParams(vmem_limit_bytes=VMEM_LIMIT), name="adaln_bwd")(c_rows, dm, w)


def c_ctx_grad(parts, c_ctx_row):
    def body(p_ref, c_ref, o_ref):
        total = p_ref[0, 0:1, :]
        for s in range(1, N_SHARDS):
            total = total + p_ref[s, 0:1, :]
        _, vjp = jax.vjp(jax.nn.silu, c_ref[...])
        o_ref[...] = vjp(total)[0]

    return pl.pallas_call(body, out_shape=jax.ShapeDtypeStruct((1, D_MODEL), f32), name="c_ctx_grad")(parts, c_ctx_row)


PACK_W = 1024
PACK_ROWS = 8


def _pack(arrs):
    pieces, layout, r0 = [], [], 0
    for a in arrs:
        size = math.prod(a.shape)
        rows = -(-size // (PACK_W * PACK_ROWS)) * PACK_ROWS
        pieces.append(jnp.pad(a.reshape(-1).astype(f32), (0, rows * PACK_W - size)).reshape(rows, PACK_W))
        layout.append((r0, rows, a.shape))
        r0 += rows
    return jnp.concatenate(pieces, axis=0), layout


def _unpack(pack, layout, lead=()):
    n_lead = len(lead)
    outs = []
    for r0, rows, shape in layout:
        piece = pack[(slice(None),) * n_lead + (slice(r0, r0 + rows),)].reshape(lead + (-1,))
        outs.append(piece[..., :math.prod(shape)].reshape(lead + tuple(shape)))
    return outs


W_NAMES = ("c_ctx", "w_ada", "b_ada", "norm1_g", "norm2_g", "w_in", "ret_log_decay", "rwkv_shift_mu", "rwkv_w0",
           "rwkv_w_up", "rwkv_a0", "rwkv_a_up", "rwkv_g_up", "rwkv_k_k", "rwkv_k_a", "rwkv_r_k", "rwkv_ln_w",
           "rwkv_ln_b", "w_out", "w_ff1", "b_ff1", "w_ff2", "b_ff2", "final_g")
COL_SHARDED = ("w_in", "w_ff1")
ROW_SHARDED = ("w_out", "w_ff2")
LAST_SHARDED = ("rwkv_shift_mu", "rwkv_w0", "rwkv_w_up", "rwkv_a0", "rwkv_a_up", "rwkv_g_up")
REPLICATED = ("c_ctx", "b_ada", "norm1_g", "norm2_g", "ret_log_decay", "rwkv_k_k", "rwkv_k_a", "rwkv_r_k",
              "rwkv_ln_w", "rwkv_ln_b", "b_ff1", "b_ff2", "final_g")
N_SHARDS = 4


def _train_step(a):
    x, c, ctx, tgt = a["x"], a["c"], a["ctx"], a["loss_target"]
    bsz = x.shape[0]
    mx, my, mc = _mesh_pos()
    shard = 2 * mx + my
    dev = _device_slot()

    (c_all,) = exchange([jnp.pad(c, ((0, PACK_ROWS - bsz), (0, 0)))], True, ALL_PEERS, "gather_c")
    n_ex = N_DEV * bsz
    c_rows = jnp.concatenate([c_all[:, :bsz].reshape(n_ex, D_MODEL), a["c_ctx"][None, :],
                              jnp.zeros((PACK_ROWS - 1, D_MODEL), f32)], axis=0)
    ada_cols = a["w_ada"].shape[-1]
    b_ada_cols = lax.dynamic_slice_in_dim(a["b_ada"], shard * ada_cols, ada_cols, axis=1)
    mod_cols = adaln_fwd(c_rows, a["w_ada"][0], b_ada_cols)

    def own_half(n):
        w = a[n][0].astype(MXU_DTYPE)
        return lax.dynamic_slice_in_dim(w, mc * (w.shape[0] // 2), w.shape[0] // 2, axis=0)

    def whole_weight(n, gth, own):
        per_chip = lax.dynamic_update_index_in_dim(gth, own, dev, 0).reshape(N_SHARDS, -1, gth.shape[-1])
        return (per_chip.transpose(1, 0, 2).reshape(per_chip.shape[1], -1) if n in COL_SHARDED
                else per_chip.reshape(-1, per_chip.shape[-1]))

    small_pack, small_layout = _pack([a[n][0] for n in LAST_SHARDED])
    own_blocks = [mod_cols, own_half("w_in"), small_pack]
    gathered = gather_two_level(own_blocks, "gather_weights")
    late_own = [own_half(n) for n in LATE_WEIGHTS]
    late_started = exchange_start(late_own, True, "gather_late_start")
    mod_own = lax.dynamic_update_index_in_dim(gathered[0], mod_cols, dev, 0)
    mod_all = jnp.stack([mod_own[s] for s in CHIP_SLOTS], axis=1).reshape(c_rows.shape[0], -1)
    mod_all = mod_all + late_started[-1][0, 0]
    mod_x = lax.dynamic_slice_in_dim(mod_all, dev * bsz, bsz, axis=0).reshape(bsz, 6, D_MODEL)
    mod_ctx = mod_all[n_ex].reshape(6, D_MODEL)
    wt = {"w_in": whole_weight("w_in", gathered[1], own_blocks[1])}

    def late_weights(after):
        lands = exchange_wait(late_started, after, True, "gather_late_wait")
        return {n: whole_weight(n, land, own) for n, land, own in zip(LATE_WEIGHTS, lands, late_own)}

    def grad_blocks(n, gw):
        if n in COL_SHARDED:
            gw = gw.reshape(gw.shape[0], N_SHARDS, -1).transpose(1, 0, 2)
        return gw.reshape(N_DEV, -1, gw.shape[-1]).astype(MXU_DTYPE)

    late_sent = {}

    def early_grads(late_g):
        late_sent["blocks"] = [grad_blocks(n, late_g[n]) for n in LATE_WEIGHTS]
        late_sent["started"] = exchange_start(late_sent["blocks"], False, "scatter_late_start")
        return late_sent["started"][-1]

    small_own = lax.dynamic_update_index_in_dim(gathered[2], small_pack, dev, 0)
    small_by_chip = _unpack(jnp.stack([small_own[s] for s in CHIP_SLOTS]), small_layout, (N_SHARDS,))
    for n, parts in zip(LAST_SHARDED, small_by_chip):
        wt[n] = jnp.concatenate([parts[s] for s in range(N_SHARDS)], axis=-1)
    for n in ("norm1_g", "norm2_g", "rwkv_k_k", "rwkv_k_a", "rwkv_r_k", "rwkv_ln_w", "rwkv_ln_b", "b_ff1", "b_ff2"):
        wt[n] = a[n]
    wt["ret_log_decay"] = a["ret_log_decay"][0]
    wt["final_g"] = a["final_g"][None, :]

    loss, grad_x, g = layer_step(x, ctx, tgt, mod_x, mod_ctx, wt, late_weights, early_grads)

    small_names = [n for n in REPLICATED if n not in ("c_ctx", "b_ada")]
    g_pack, g_layout = _pack([jnp.pad(loss, ((0, 0), (0, PACK_W - loss.shape[1])))] + [g[n] for n in small_names]
                             + [g["mod_x"], g["mod_ctx"]])
    (g_packs,) = gather_two_level([g_pack], "gather_small_grads")
    g_packs = lax.dynamic_update_index_in_dim(g_packs, g_pack, dev, 0)
    g_sum = _unpack(sum_slots(g_packs, tuple(range(N_DEV)), "sum_small_grads"), g_layout)
    loss_total = g_sum[0][0, 0]
    grads = dict(zip(small_names, g_sum[1:1 + len(small_names)]))
    dmod_ctx = g_sum[-1].reshape(1, -1)
    dmod_x = _unpack(g_packs, g_layout, (N_DEV,))[-2].reshape(n_ex, -1)
    dmod = jnp.concatenate([dmod_x, dmod_ctx, jnp.zeros((PACK_ROWS - 1, dmod_x.shape[1]), f32)], axis=0)
    grads["b_ada"] = column_sum(dmod, "b_ada_grad")
    dmod_cols = lax.dynamic_slice_in_dim(dmod, shard * ada_cols, ada_cols, axis=1)
    grads["w_ada"], dsilu = adaln_bwd(c_rows, dmod_cols, a["w_ada"][0])

    blocks = [grad_blocks("w_in", g["w_in"])]
    shard_packs = []
    for s in range(N_SHARDS):
        pieces_s = [lax.slice_in_dim(g[n], s * a[n].shape[-1], (s + 1) * a[n].shape[-1], axis=g[n].ndim - 1)
                    for n in LAST_SHARDED]
        pack_s, shard_layout = _pack(pieces_s)
        shard_packs.append(jnp.pad(pack_s, ((0, -pack_s.shape[0] % (2 * PACK_ROWS)), (0, 0))))
    blocks.append(jnp.stack(shard_packs).reshape(N_DEV, -1, PACK_W))
    scattered = ("w_in", "small_shards")
    halves_of = lambda blk, core: lax.dynamic_index_in_dim(
        blk.reshape(N_SHARDS, 2, *blk.shape[1:]), core, axis=1, keepdims=False).reshape(-1, blk.shape[-1])
    from_sibling = sibling_swap([halves_of(blk, 1 - mc) for blk in blocks], "prereduce_swap")
    chip_sums = [add_arrays([halves_of(blk, mc), got], f"prereduce_{n}", blk.dtype).reshape(N_SHARDS, -1, blk.shape[-1])
                 for n, blk, got in zip(scattered, blocks, from_sibling)]
    dsilu_rows = jnp.broadcast_to(jnp.pad(dsilu[n_ex:n_ex + 1], ((0, PACK_ROWS - 1), (0, 0)))[None],
                                  (N_SHARDS, PACK_ROWS, D_MODEL))
    to_chips = [dsilu_rows] + chip_sums
    received = exchange(to_chips, False, CHIP_PEERS, "scatter_big_grads", by_chip=True, own=False)
    received = [lax.dynamic_update_index_in_dim(got, lax.dynamic_index_in_dim(sent, shard, 0, keepdims=False), shard, 0)
                for got, sent in zip(received, to_chips)]
    grads["c_ctx"] = c_ctx_grad(received[0], a["c_ctx"][None, :])
    half_sums = [sum_slots(p, tuple(range(N_SHARDS)), f"sum_{n}") for n, p in zip(scattered, received[1:])]
    late_lands = exchange_wait(late_sent["started"], half_sums[0], False, "scatter_late_wait")
    for n, land, sent in zip(LATE_WEIGHTS, late_lands, late_sent["blocks"]):
        land = lax.dynamic_update_index_in_dim(land, lax.dynamic_index_in_dim(sent, dev, 0, keepdims=False), dev, 0)
        half_sums.append(sum_slots(land, tuple(range(N_DEV)), f"sum_{n}"))
    scattered = scattered + LATE_WEIGHTS
    other_halves = sibling_swap(half_sums, "swap_halves")
    for n, mine, other in zip(scattered, half_sums, other_halves):
        rows = mine.shape[0]
        whole = jnp.zeros((2 * rows, mine.shape[1]), f32)
        whole = lax.dynamic_update_slice_in_dim(whole, mine, mc * rows, axis=0)
        grads[n] = lax.dynamic_update_slice_in_dim(whole, other, (1 - mc) * rows, axis=0)
    grads.update(zip(LAST_SHARDED, _unpack(grads.pop("small_shards"), shard_layout)))

    out_g, out_d, out_m, out_v = {}, {}, {}, {}
    for n in ("w_ada",) + COL_SHARDED + ROW_SHARDED:
        out_g[n] = grads[n].reshape(a[n].shape)
        two_d = lambda z: z.reshape(-1, z.shape[-1])
        d, m, v = adamw(two_d(a[n]), two_d(out_g[n]), two_d(a["m_" + n]), two_d(a["v_" + n]), f"adamw_{n}")
        out_d[n], out_m[n], out_v[n] = d.reshape(a[n].shape), m.reshape(a[n].shape), v.reshape(a[n].shape)
    rest = REPLICATED + LAST_SHARDED
    for n in rest:
        out_g[n] = grads[n].reshape(a[n].shape)
    packs = [_pack([src[n] for n in rest])[0] for src in
             ({n: a[n] for n in rest}, out_g, {n: a["m_" + n] for n in rest}, {n: a["v_" + n] for n in rest})]
    _, rest_layout = _pack([a[n] for n in rest])
    for dst, pack in zip((out_d, out_m, out_v), adamw(*packs, "adamw_small")):
        dst.update(zip(rest, _unpack(pack, rest_layout)))
    return (loss_total, grad_x, *[out_g[n] for n in W_NAMES], *[out_d[n] for n in W_NAMES],
            *[out_m[n] for n in W_NAMES], *[out_v[n] for n in W_NAMES])


def kernel(x, c, ctx, c_ctx, w_ada, b_ada, norm1_g, norm2_g, w_in, ret_log_decay, rwkv_shift_mu, rwkv_w0, rwkv_w_up, rwkv_a0, rwkv_a_up, rwkv_g_up, rwkv_k_k, rwkv_k_a, rwkv_r_k, rwkv_ln_w, rwkv_ln_b, w_out, w_ff1, b_ff1, w_ff2, b_ff2, final_g, loss_target, m_c_ctx, m_w_ada, m_b_ada, m_norm1_g, m_norm2_g, m_w_in, m_ret_log_decay, m_rwkv_shift_mu, m_rwkv_w0, m_rwkv_w_up, m_rwkv_a0, m_rwkv_a_up, m_rwkv_g_up, m_rwkv_k_k, m_rwkv_k_a, m_rwkv_r_k, m_rwkv_ln_w, m_rwkv_ln_b, m_w_out, m_w_ff1, m_b_ff1, m_w_ff2, m_b_ff2, m_final_g, v_c_ctx, v_w_ada, v_b_ada, v_norm1_g, v_norm2_g, v_w_in, v_ret_log_decay, v_rwkv_shift_mu, v_rwkv_w0, v_rwkv_w_up, v_rwkv_a0, v_rwkv_a_up, v_rwkv_g_up, v_rwkv_k_k, v_rwkv_k_a, v_rwkv_r_k, v_rwkv_ln_w, v_rwkv_ln_b, v_w_out, v_w_ff1, v_b_ff1, v_w_ff2, v_b_ff2, v_final_g):
    return _train_step(dict(locals()))
```

```python
import functools
import math

import jax
import jax.numpy as jnp
from jax import lax
from jax.experimental import pallas as pl
from jax.experimental.pallas import tpu as pltpu

f32 = jnp.float32
MXU_DTYPE = jnp.bfloat16

D_MODEL = 1024
RET_W = 512
RET_HEADS = 4
RET_DH = 128
RET_CHUNK = 128
RW_W = 512
RW_N = 64
DECAY_LORA = 64
AAA_LORA = 64
GATE_LORA = 128
LORA_W = DECAY_LORA + AAA_LORA + GATE_LORA
D_FF = 4096
RET_COLS = 4 * RET_W
SHIFT_COLS = 3 * RW_W + LORA_W
IN_COLS = RET_COLS + SHIFT_COLS
GRID_W = 64
ROPE_BASE = 10000.0
NORM_EPS = 1e-6
GN_EPS = 64e-5
W_DECAY_SCALE = math.exp(-0.5)
ADAM_LR, ADAM_B1, ADAM_B2, ADAM_EPS, ADAM_WD, ADAM_STEP = 0.001, 0.9, 0.999, 1e-08, 0.01, 10

TOK_TILE = 256
MATMUL_TILE = 1024
SCAN_CHUNK = 16
SCAN_UNROLL = SCAN_CHUNK
N_DEV = 8
V7X_VMEM_BYTES = 64 * 1024 * 1024
VMEM_LIMIT = V7X_VMEM_BYTES * 7 // 8


def _cparams(sem):
    return pltpu.CompilerParams(dimension_semantics=sem, vmem_limit_bytes=VMEM_LIMIT)


def _tile(n, cap):
    best = None
    for t in range(128, min(n, cap) + 1, 128):
        if n % t == 0:
            best = t
    return best if best is not None else n


def matmul(a, b, mode, name, out_dtype=f32, bias=None, finish=None):
    if mode == "nn":
        (m, k), (k2, n) = a.shape, b.shape
    elif mode == "nt":
        (m, k), (n, k2) = a.shape, b.shape
    else:
        (k, m), (k2, n) = a.shape, b.shape
    assert k == k2, (a.shape, b.shape, mode)
    tm, tn, tk = _tile(m, MATMUL_TILE), _tile(n, MATMUL_TILE), _tile(k, MATMUL_TILE)
    nk = k // tk
    dims = {"nn": ((1,), (0,)), "nt": ((1,), (1,)), "tn": ((0,), (0,))}[mode]

    def body(a_ref, b_ref, *rest):
        o_ref, acc_ref = rest[-2:]
        kk = pl.program_id(2)

        @pl.when(kk == 0)
        def _():
            acc_ref[...] = jnp.zeros_like(acc_ref)

        acc_ref[...] += lax.dot_general(a_ref[...].astype(MXU_DTYPE), b_ref[...].astype(MXU_DTYPE),
                                        (dims, ((), ())), preferred_element_type=f32)

        @pl.when(kk == nk - 1)
        def _():
            res = acc_ref[...]
            if bias is not None:
                res = res + rest[0][...]
            if finish is not None:
                res = finish(res)
            o_ref[...] = res.astype(o_ref.dtype)

    if mode == "nn":
        a_spec = pl.BlockSpec((tm, tk), lambda i, j, q: (i, q))
        b_spec = pl.BlockSpec((tk, tn), lambda i, j, q: (q, j))
    elif mode == "nt":
        a_spec = pl.BlockSpec((tm, tk), lambda i, j, q: (i, q))
        b_spec = pl.BlockSpec((tn, tk), lambda i, j, q: (j, q))
    else:
        a_spec = pl.BlockSpec((tk, tm), lambda i, j, q: (q, i))
        b_spec = pl.BlockSpec((tk, tn), lambda i, j, q: (q, j))
    extra_specs = [] if bias is None else [pl.BlockSpec((1, tn), lambda i, j, q: (0, j))]
    extra = [] if bias is None else [bias]
    return pl.pallas_call(
        body, grid=(m // tm, n // tn, nk), in_specs=[a_spec, b_spec] + extra_specs,
        out_specs=pl.BlockSpec((tm, tn), lambda i, j, q: (i, j)),
        out_shape=jax.ShapeDtypeStruct((m, n), out_dtype),
        scratch_shapes=[pltpu.VMEM((tm, tn), f32)],
        compiler_params=_cparams(("parallel", "parallel", "arbitrary")), name=name)(a, b, *extra)


class Tiled:
    def __init__(self, arr, w=None, cidx=0, toff=0):
        self.arr, self.w, self.cidx, self.toff = arr, (arr.shape[-1] if w is None else w), cidx, toff

    def spec(self):
        cidx, toff = self.cidx, self.toff
        return pl.BlockSpec((None, TOK_TILE, self.w), lambda b, i: (b, jnp.maximum(i + toff, 0), cidx))


class Seg:
    def __init__(self, arr, seg, first):
        self.arr, self.seg, self.first = arr, seg, first

    def spec(self):
        seg = self.seg
        return pl.BlockSpec((None, None, 1, self.arr.shape[-1]), lambda b, i: (b, seg(i), 0, 0))


class Glob:
    def __init__(self, arr):
        self.arr = arr

    def spec(self):
        return pl.BlockSpec(self.arr.shape, lambda b, i: (0,) * self.arr.ndim)


def ew_forward(fn, name, bsz, n_tiles, ins, outs):
    n_in = len(ins)

    def body(*refs):
        res = fn(*[r[...] for r in refs[:n_in]])
        for o_ref, o in zip(refs[n_in:], res):
            o_ref[...] = o.astype(o_ref.dtype)

    out_specs = [pl.BlockSpec((None, TOK_TILE, w), lambda b, i: (b, i, 0)) for w, _ in outs]
    out_shape = [jax.ShapeDtypeStruct((bsz, n_tiles * TOK_TILE, w), dt) for w, dt in outs]
    return pl.pallas_call(body, grid=(bsz, n_tiles), in_specs=[d.spec() for d in ins], out_specs=out_specs,
                          out_shape=out_shape, compiler_params=_cparams(("parallel", "parallel")), name=name)(
        *[d.arr for d in ins])


def ew_backward(fn, name, bsz, n_tiles, ins, cts, want, grad_dtypes=None, lead=0):
    n_in, n_ct = len(ins), len(cts)
    diff = [k for k in range(n_in) if want[k]]
    grad_dtypes = grad_dtypes or {}
    assert lead == 0 or not any(isinstance(ins[k], Seg) for k in diff)

    def body(*refs):
        b, i = pl.program_id(0), pl.program_id(1)
        g_refs = refs[n_in + n_ct:]

        def tile_grads():
            vals = [r[...] for r in refs[:n_in]]
            ct_vals = tuple(r[...].astype(f32) for r in refs[n_in:n_in + n_ct])

            def f(*dvals):
                full = list(vals)
                for k, v in zip(diff, dvals):
                    full[k] = v
                return tuple(fn(*full))

            _, vjp = jax.vjp(f, *[vals[k] for k in diff])
            grads = vjp(ct_vals)
            for k, g_ref, g in zip(diff, g_refs, grads):
                d = ins[k]
                if isinstance(d, Tiled):
                    g_ref[...] = g.astype(g_ref.dtype)
                else:
                    zero = d.first(i) if isinstance(d, Seg) else jnp.logical_and(b == 0, i == lead)

                    @pl.when(zero)
                    def _(g_ref=g_ref):
                        g_ref[...] = jnp.zeros_like(g_ref)

                    g_ref[...] += g

        if lead == 0:
            tile_grads()
        else:
            pl.when(i >= lead)(tile_grads)

            @pl.when(i < lead)
            def _():
                for k, g_ref in zip(diff, g_refs):
                    if isinstance(ins[k], Tiled):
                        g_ref[...] = jnp.zeros_like(g_ref)

    out_specs, out_shape = [], []
    for k in diff:
        d = ins[k]
        if isinstance(d, Tiled):
            out_specs.append(pl.BlockSpec((None, TOK_TILE, d.w), lambda b, i: (b, i, 0)))
            out_shape.append(jax.ShapeDtypeStruct((bsz, (n_tiles + lead) * TOK_TILE, d.w), grad_dtypes.get(k, f32)))
        else:
            out_specs.append(d.spec())
            out_shape.append(jax.ShapeDtypeStruct(d.arr.shape, f32))
    return pl.pallas_call(body, grid=(bsz, n_tiles + lead),
                          in_specs=[d.spec() for d in ins] + [c.spec() for c in cts],
                          out_specs=out_specs, out_shape=out_shape,
                          compiler_params=_cparams(("arbitrary", "arbitrary")), name=name)(
        *[d.arr for d in ins], *[c.arr for c in cts])


@jax.custom_vjp
def _mxu_dot(a, b):
    return jnp.dot(a.astype(MXU_DTYPE), b.astype(MXU_DTYPE), preferred_element_type=f32)


def _mxu_dot_fwd(a, b):
    return _mxu_dot(a, b), (a, b)


def _mxu_dot_bwd(res, ct):
    a, b = res
    ct = ct.astype(MXU_DTYPE)
    da = lax.dot_general(ct, b.astype(MXU_DTYPE), (((1,), (1,)), ((), ())), preferred_element_type=f32)
    db = lax.dot_general(a.astype(MXU_DTYPE), ct, (((0,), (0,)), ((), ())), preferred_element_type=f32)
    return da, db


_mxu_dot.defvjp(_mxu_dot_fwd, _mxu_dot_bwd)


def _split_dot_impl(x, ones_mat):
    hi = x.astype(MXU_DTYPE)
    lo = (x - hi.astype(f32)).astype(MXU_DTYPE)
    return jnp.dot(hi, ones_mat, preferred_element_type=f32) + jnp.dot(lo, ones_mat, preferred_element_type=f32)


@jax.custom_vjp
def _split_dot(x, ones_mat):
    return _split_dot_impl(x, ones_mat)


def _split_dot_fwd(x, ones_mat):
    return _split_dot_impl(x, ones_mat), ones_mat


def _split_dot_bwd(ones_mat, ct):
    return _split_dot_impl(ct, ones_mat), None


_split_dot.defvjp(_split_dot_fwd, _split_dot_bwd)


def _block_ones(n, group):
    idx = jnp.arange(n) // group
    return (idx[:, None] == idx[None, :]).astype(MXU_DTYPE)


def _rms(x, g):
    return x * lax.rsqrt(jnp.mean(x * x, axis=-1, keepdims=True) + NORM_EPS) * g


def fn_norm_mod(h, shift, scale, g):
    return (_rms(h, g) * (1.0 + scale) + shift,)


def fn_rwkv_prepare(ks, lora, w0_f, w0_b, a0_f, a0_b, w_up_f, w_up_b, a_up_f, a_up_b, g_up, k_k, k_a, ones64):
    kkr = ks * k_k
    kk = kkr * lax.rsqrt(_split_dot(kkr * kkr, ones64) + 1e-12)
    outs = [kk]
    th = jnp.tanh(lora)
    for w0, a0, w_up, a_up in ((w0_f, a0_f, w_up_f, a_up_f), (w0_b, a0_b, w_up_b, a_up_b)):
        w = jnp.exp(-W_DECAY_SCALE * jax.nn.sigmoid(w0 + _mxu_dot(th, w_up)))
        a = jax.nn.sigmoid(a0 + _mxu_dot(lora, a_up))
        kt = ks * (1.0 + (a - 1.0) * k_a)
        outs += [w, a * kk, kt]
    outs.append(_mxu_dot(jax.nn.sigmoid(lora), g_up))
    return tuple(outs)


def fn_merge(o_f, o_b, g_ret, y_f, y_b, r, kt_f, v, g_rw, r_k, ln_w, ln_b, ones64, ones128):
    o = o_f + o_b
    ret = o * lax.rsqrt(_split_dot(o * o, ones128) * (1.0 / RET_DH) + NORM_EPS) * (g_ret * jax.nn.sigmoid(g_ret))
    y = y_f + y_b
    mean = _split_dot(y, ones64) * (1.0 / RW_N)
    yc = y - mean
    var = _split_dot(yc * yc, ones64) * (1.0 / RW_N)
    y_n = yc * lax.rsqrt(var + GN_EPS) * ln_w + ln_b
    bonus = _split_dot(r * kt_f * r_k, ones64) * v
    return ret, (y_n + bonus) * g_rw


def fn_resid_norm_mod(x, mix, gate, shift, scale, g):
    h1 = x + gate * mix
    return h1, _rms(h1, g) * (1.0 + scale) + shift


def relu2(z):
    return jnp.square(jnp.maximum(z, 0.0))


def relu2_backward(act, dact, name):
    bsz, n_tok, width = act.shape

    def body(a_ref, d_ref, du_ref, db_ref):
        du = d_ref[...].astype(f32) * (2.0 * jnp.sqrt(a_ref[...].astype(f32)))
        du_ref[...] = du.astype(du_ref.dtype)

        @pl.when(jnp.logical_and(pl.program_id(0) == 0, pl.program_id(1) == 0))
        def _():
            db_ref[...] = jnp.zeros_like(db_ref)

        db_ref[...] += jnp.sum(du, axis=0, keepdims=True)

    tile = pl.BlockSpec((None, TOK_TILE, width), lambda b, i: (b, i, 0))
    row = pl.BlockSpec((1, width), lambda b, i: (0, 0))
    return pl.pallas_call(body, grid=(bsz, n_tok // TOK_TILE), in_specs=[tile, tile], out_specs=[tile, row],
                          out_shape=[jax.ShapeDtypeStruct(act.shape, MXU_DTYPE), jax.ShapeDtypeStruct((1, width), f32)],
                          compiler_params=_cparams(("arbitrary", "arbitrary")), name=name)(act, dact)


def fn_loss(h1, f, tgt, gate, b2, g):
    y = _rms(h1 + gate * (f + b2), g)
    err = jnp.square(y - tgt)
    return 0.5 * jnp.sum(jnp.mean(err, axis=-1, keepdims=True), axis=0, keepdims=True)


def loss_and_grads(h1, f, tgt, gate, b2, g, bsz, n_tiles):
    def body(h1_ref, f_ref, t_ref, gate_ref, b2_ref, g_ref, loss_ref, dh1_ref, df_ref, dgate_ref, db2_ref, dg_ref):
        b, i = pl.program_id(0), pl.program_id(1)
        tgt_v = t_ref[...]
        loss, vjp = jax.vjp(lambda a, c, e, p, q: fn_loss(a, c, tgt_v, e, p, q),
                            h1_ref[...], f_ref[...], gate_ref[...], b2_ref[...], g_ref[...])
        dh1, df, dgate, db2, dg = vjp(jnp.ones((1, 1), f32))
        dh1_ref[...] = dh1
        df_ref[...] = df.astype(df_ref.dtype)

        @pl.when(i == 0)
        def _():
            dgate_ref[...] = jnp.zeros_like(dgate_ref)

        @pl.when(jnp.logical_and(b == 0, i == 0))
        def _():
            loss_ref[...] = jnp.zeros_like(loss_ref)
            db2_ref[...] = jnp.zeros_like(db2_ref)
            dg_ref[...] = jnp.zeros_like(dg_ref)

        dgate_ref[...] += dgate
        db2_ref[...] += db2
        dg_ref[...] += dg
        loss_ref[...] += jnp.broadcast_to(loss, loss_ref.shape)

    tile = pl.BlockSpec((None, TOK_TILE, D_MODEL), lambda b, i: (b, i, 0))
    row = pl.BlockSpec((1, D_MODEL), lambda b, i: (0, 0))
    seg = pl.BlockSpec((None, None, 1, D_MODEL), lambda b, i: (b, 0, 0, 0))
    t_tok = n_tiles * TOK_TILE
    return pl.pallas_call(
        body, grid=(bsz, n_tiles), in_specs=[tile, tile, tile, seg, row, row],
        out_specs=[pl.BlockSpec((1, 128), lambda b, i: (0, 0)), tile, tile, seg, row, row],
        out_shape=[jax.ShapeDtypeStruct((1, 128), f32), jax.ShapeDtypeStruct((bsz, t_tok, D_MODEL), f32),
                   jax.ShapeDtypeStruct((bsz, t_tok, D_MODEL), MXU_DTYPE),
                   jax.ShapeDtypeStruct((bsz, 1, 1, D_MODEL), f32),
                   jax.ShapeDtypeStruct((1, D_MODEL), f32), jax.ShapeDtypeStruct((1, D_MODEL), f32)],
        compiler_params=_cparams(("arbitrary", "arbitrary")), name="loss_and_grads")(h1, f, tgt, gate, b2, g)


SHIFT_BLOCK = SHIFT_COLS
HALO_ROWS = 8


def _shift_specs(n_tok, col0):
    per_tile = TOK_TILE // HALO_ROWS
    last = n_tok // HALO_ROWS - 1
    tile = pl.BlockSpec((None, TOK_TILE, SHIFT_BLOCK), lambda j, b, i: (b, i, col0 + j))
    prev = pl.BlockSpec((None, HALO_ROWS, SHIFT_BLOCK),
                        lambda j, b, i: (b, jnp.maximum(i * per_tile - 1, 0), col0 + j))
    nxt = pl.BlockSpec((None, HALO_ROWS, SHIFT_BLOCK),
                       lambda j, b, i: (b, jnp.minimum((i + 1) * per_tile, last), col0 + j))
    return tile, prev, nxt


def _shifted(p, prev_ref, next_ref, is_first, is_last):
    row = lax.broadcasted_iota(jnp.int32, p.shape, 0)
    prev_row = jnp.where(is_first, 0.0, prev_ref[HALO_ROWS - 1:HALO_ROWS, :].astype(f32))
    next_row = jnp.where(is_last, 0.0, next_ref[0:1, :].astype(f32))
    prev = jnp.where(row == 0, prev_row, pltpu.roll(p, 1, axis=0))
    nxt = jnp.where(row == TOK_TILE - 1, next_row, pltpu.roll(p, TOK_TILE - 1, axis=0))
    return prev, nxt


def token_shift(px, mu, seg_first, seg_last):
    bsz, n_tok, _ = px.shape
    n_tiles = n_tok // TOK_TILE

    def body(p_ref, prev_ref, next_ref, mu_ref, o_ref):
        i = pl.program_id(2)
        p = p_ref[...]
        prev, nxt = _shifted(p, prev_ref, next_ref, seg_first(i), seg_last(i))
        o_ref[...] = p + mu_ref[0:1, :] * (prev - p) + mu_ref[1:2, :] * (nxt - p)

    tile, prev, nxt = _shift_specs(n_tok, 0)
    return pl.pallas_call(
        body, grid=(SHIFT_COLS // SHIFT_BLOCK, bsz, n_tiles),
        in_specs=[tile, prev, nxt, pl.BlockSpec((2, SHIFT_BLOCK), lambda j, b, i: (0, j))],
        out_specs=pl.BlockSpec((None, TOK_TILE, SHIFT_BLOCK), lambda j, b, i: (b, i, j)),
        out_shape=jax.ShapeDtypeStruct((bsz, n_tok, SHIFT_COLS), f32),
        compiler_params=_cparams(("parallel", "parallel", "parallel")), name="token_shift")(px, px, px, mu)


def token_shift_bwd(dps, px, mu, seg_first, seg_last):
    bsz, n_tok, _ = px.shape
    n_tiles = n_tok // TOK_TILE

    def body(d_ref, dprev_ref, dnext_ref, p_ref, prev_ref, next_ref, mu_ref, dp_ref, dmu_ref):
        b, i = pl.program_id(1), pl.program_id(2)
        first, last = seg_first(i), seg_last(i)
        d, p = d_ref[...], p_ref[...]
        d_prev, d_next = _shifted(d, dprev_ref, dnext_ref, first, last)
        p_prev, p_next = _shifted(p, prev_ref, next_ref, first, last)
        mu0, mu1 = mu_ref[0:1, :], mu_ref[1:2, :]
        dp_ref[...] = (d + mu0 * (d_next - d) + mu1 * (d_prev - d)).astype(dp_ref.dtype)

        @pl.when(jnp.logical_and(b == 0, i == 0))
        def _():
            dmu_ref[...] = jnp.zeros_like(dmu_ref)

        dmu_ref[0:1, :] += jnp.sum(d * (p_prev - p), axis=0, keepdims=True)
        dmu_ref[1:2, :] += jnp.sum(d * (p_next - p), axis=0, keepdims=True)

    dtile, dprev, dnext = _shift_specs(n_tok, 0)
    tile, prev, nxt = _shift_specs(n_tok, 0)
    mu_spec = pl.BlockSpec((2, SHIFT_BLOCK), lambda j, b, i: (0, j))
    return pl.pallas_call(
        body, grid=(SHIFT_COLS // SHIFT_BLOCK, bsz, n_tiles),
        in_specs=[dtile, dprev, dnext, tile, prev, nxt, mu_spec],
        out_specs=[pl.BlockSpec((None, TOK_TILE, SHIFT_BLOCK), lambda j, b, i: (b, i, j)), mu_spec],
        out_shape=[jax.ShapeDtypeStruct((bsz, n_tok, SHIFT_COLS), MXU_DTYPE),
                   jax.ShapeDtypeStruct((2, SHIFT_COLS), f32)],
        compiler_params=_cparams(("arbitrary", "arbitrary", "arbitrary")), name="token_shift_bwd")(
        dps, dps, dps, px, px, px, mu)


def _dg(a, b, ca, cb):
    return lax.dot_general(a.astype(MXU_DTYPE), b.astype(MXU_DTYPE), (((ca,), (cb,)), ((), ())),
                           preferred_element_type=f32)


@jax.custom_vjp
def _mm_nt(a, b):
    return _dg(a, b, 1, 1)


_mm_nt.defvjp(lambda a, b: (_dg(a, b, 1, 1), (a, b)),
              lambda res, ct: (_dg(ct, res[1], 1, 0), _dg(ct, res[0], 0, 0)))


@jax.custom_vjp
def _mm_tn(a, b):
    return _dg(a, b, 0, 0)


_mm_tn.defvjp(lambda a, b: (_dg(a, b, 0, 0), (a, b)),
              lambda res, ct: (_dg(res[1], ct, 1, 1), _dg(res[0], ct, 1, 0)))


ROTARY_PAIR = RET_DH // 4


def _swap_pairs_impl(t):
    lane = lax.broadcasted_iota(jnp.int32, t.shape, 1)
    return jnp.where(lane % (2 * ROTARY_PAIR) < ROTARY_PAIR, pltpu.roll(t, RET_DH - ROTARY_PAIR, axis=1),
                     pltpu.roll(t, ROTARY_PAIR, axis=1))


@jax.custom_vjp
def _swap_pairs(t):
    return _swap_pairs_impl(t)


_swap_pairs.defvjp(lambda t: (_swap_pairs_impl(t), None), lambda _, ct: (_swap_pairs_impl(ct),))


def _ret_chunk(state, q_raw, k_raw, v, cos, sin, ld_row, head, reverse):
    c = RET_CHUNK
    lane = lax.broadcasted_iota(jnp.int32, ld_row.shape, 1)
    lg = -jnp.exp(jnp.sum(jnp.where(lane == head, ld_row, 0.0), axis=-1, keepdims=True))
    rot = lambda t: t * cos + _swap_pairs(t) * sin
    q = rot(q_raw)
    k = rot(k_raw) * (RET_DH ** -0.5)
    ti = lax.broadcasted_iota(jnp.int32, (c, 1), 0).astype(f32)
    tj = lax.broadcasted_iota(jnp.int32, (1, c), 1).astype(f32)
    if not reverse:
        dist, mask, q_exp, k_exp = ti - tj, (ti - tj) >= 0, ti + 1.0, c - 1.0 - ti
    else:
        dist, mask, q_exp, k_exp = tj - ti, (tj - ti) > 0, c - ti, ti
    decay = jnp.where(mask, jnp.exp(lg * jnp.maximum(dist, 0.0)), 0.0)
    scores = _mm_nt(q, k) * decay
    out = _mxu_dot(scores, v) + _mxu_dot(q * jnp.exp(lg * q_exp), state)
    new_state = state * jnp.exp(lg * c) + _mm_tn(k * jnp.exp(lg * k_exp), v)
    return out, new_state


def _ret_specs(bsz, order):
    tok = lambda col=0: pl.BlockSpec((bsz, RET_CHUNK, RET_W), lambda i: (0, order(i), col))
    tab = pl.BlockSpec((RET_CHUNK, RET_DH), lambda i: (order(i), 0))
    ld = pl.BlockSpec((1, RET_DH), lambda i: (0, 0))
    return tok, tab, ld


def retention_fwd(px, cos, sin, ld_row, order, reverse, name):
    bsz, n_tok, _ = px.shape
    n_ch = n_tok // RET_CHUNK

    def body(q_ref, k_ref, v_ref, cos_ref, sin_ref, ld_ref, o_ref, sv_ref, st_ref):
        @pl.when(pl.program_id(0) == 0)
        def _():
            st_ref[...] = jnp.zeros_like(st_ref)

        for b in range(bsz):
            for h in range(RET_HEADS):
                sl = slice(h * RET_DH, (h + 1) * RET_DH)
                s = st_ref[b, h]
                sv_ref[b, h] = s
                o, s_new = _ret_chunk(s, q_ref[b, :, sl], k_ref[b, :, sl], v_ref[b, :, sl], cos_ref[...], sin_ref[...],
                                      ld_ref[...], h, reverse)
                o_ref[b, :, sl] = o
                st_ref[b, h] = s_new

    tok, tab, ld = _ret_specs(bsz, order)
    return pl.pallas_call(
        body, grid=(n_ch,), in_specs=[tok(0), tok(1), tok(2), tab, tab, ld],
        out_specs=[tok(), pl.BlockSpec((bsz, None, RET_HEADS, RET_DH, RET_DH), lambda i: (0, i, 0, 0, 0))],
        out_shape=[jax.ShapeDtypeStruct((bsz, n_tok, RET_W), f32),
                   jax.ShapeDtypeStruct((bsz, n_ch, RET_HEADS, RET_DH, RET_DH), f32)],
        scratch_shapes=[pltpu.VMEM((bsz, RET_HEADS, RET_DH, RET_DH), f32)],
        compiler_params=_cparams(("arbitrary",)), name=name)(px, px, px, cos, sin, ld_row)


def retention_bwd(do, px, states, cos, sin, ld_row, order, reverse, name):
    bsz, n_tok, _ = px.shape
    n_ch = n_tok // RET_CHUNK
    back = lambda i: order(n_ch - 1 - i)

    def body(do_ref, q_ref, k_ref, v_ref, sv_ref, cos_ref, sin_ref, ld_ref,
             dq_ref, dk_ref, dv_ref, dld_ref, dst_ref):
        @pl.when(pl.program_id(0) == 0)
        def _():
            dst_ref[...] = jnp.zeros_like(dst_ref)
            dld_ref[...] = jnp.zeros_like(dld_ref)

        cos_v, sin_v = cos_ref[...], sin_ref[...]
        for b in range(bsz):
            for h in range(RET_HEADS):
                sl = slice(h * RET_DH, (h + 1) * RET_DH)
                f = lambda s, q, k, v, ld, h=h: _ret_chunk(s, q, k, v, cos_v, sin_v, ld, h, reverse)
                _, vjp = jax.vjp(f, sv_ref[b, h], q_ref[b, :, sl], k_ref[b, :, sl], v_ref[b, :, sl], ld_ref[...])
                ds, dq, dk, dv, dld = vjp((do_ref[b, :, sl], dst_ref[b, h]))
                dst_ref[b, h] = ds
                dq_ref[b, :, sl] = dq
                dk_ref[b, :, sl] = dk
                dv_ref[b, :, sl] = dv
                dld_ref[...] += dld

    tok, tab, ld = _ret_specs(bsz, back)
    return pl.pallas_call(
        body, grid=(n_ch,),
        in_specs=[tok(), tok(0), tok(1), tok(2),
                  pl.BlockSpec((bsz, None, RET_HEADS, RET_DH, RET_DH), lambda i: (0, n_ch - 1 - i, 0, 0, 0)),
                  tab, tab, ld],
        out_specs=[tok(), tok(), tok(), ld],
        out_shape=[jax.ShapeDtypeStruct((bsz, n_tok, RET_W), f32)] * 3 + [jax.ShapeDtypeStruct((1, RET_DH), f32)],
        scratch_shapes=[pltpu.VMEM((bsz, RET_HEADS, RET_DH, RET_DH), f32)],
        compiler_params=_cparams(("arbitrary",)), name=name)(
        do, px, px, px, states, cos, sin, ld_row)


HALF_W = RW_W // 2


def _head_sum(x, ones):
    xm = x.astype(MXU_DTYPE)
    return jnp.concatenate([jnp.dot(xm[:, :HALF_W], ones, preferred_element_type=f32),
                            jnp.dot(xm[:, HALF_W:], ones, preferred_element_type=f32)], axis=1)


def _stack(parts):
    return jnp.concatenate(parts, axis=0)


def _row(ref, b, t):
    return ref[b, pl.ds(t, 1), :]


SCAN_DIRS = ((False, True), (True, False))
RW_HEADS = RW_W // RW_N
HEAD_ROWS_PAD = 16


def _head_rows(row, mask):
    return jnp.broadcast_to(row, mask.shape) * mask


def _outer(per_value, row, mask_pad):
    return lax.dot_general(per_value.astype(MXU_DTYPE), _head_rows(row, mask_pad).astype(MXU_DTYPE),
                           (((0,), (0,)), ((), ())), preferred_element_type=f32)


def _read(states, rows, mask):
    lhs = _stack([_head_rows(r, mask) for r in rows])
    return lax.dot_general(lhs.astype(MXU_DTYPE), _stack(states).astype(MXU_DTYPE), (((1,), (1,)), ((), ())),
                           preferred_element_type=f32)


def _row_from_heads(per_value, state, mask_pad):
    full = jnp.dot(per_value.astype(MXU_DTYPE), state.astype(MXU_DTYPE), preferred_element_type=f32)
    return jnp.sum(full * mask_pad, axis=0, keepdims=True)


def _scan_specs(bsz, order):
    rows = lambda col=0: pl.BlockSpec((bsz, SCAN_CHUNK, RW_W), lambda i: (0, order(i), col))
    per_value = pl.BlockSpec((bsz, SCAN_CHUNK, HEAD_ROWS_PAD, RW_N), lambda i: (0, order(i), 0, 0))
    return rows, per_value


def _removed(sp, kk_t, ones, bsz):
    removed = _head_sum(_stack([sp[b] * kk_t[b] for b in range(bsz)]), ones)
    return [removed[b * RW_N:(b + 1) * RW_N] for b in range(bsz)]


def _advance(sp, rem, w_t, b_t, vk, bsz):
    return [sp[b] * w_t[b] - rem[b] * b_t[b] + vk[b] for b in range(bsz)]


def heads_to_rows(a):
    b, t, _ = a.shape
    return jnp.pad(a.astype(MXU_DTYPE).reshape(b, t, RW_HEADS, RW_N),
                   ((0, 0), (0, 0), (0, HEAD_ROWS_PAD - RW_HEADS), (0, 0)))


def _blocks_to_rows(raw_ref, first, row_ref, bsz):
    steps = pl.ds(first, SCAN_CHUNK)
    for b in range(bsz):
        for h in range(RW_HEADS):
            row_ref[b, :, h * RW_N:(h + 1) * RW_N] = raw_ref[steps, RW_HEADS * b + h, RW_N * b:RW_N * (b + 1)]


N_ROWS_FWD = 5
N_ROWS_BWD = 5


def _scan_consts(bsz):
    head = (jnp.arange(RW_W)[None, :] // RW_N == jnp.arange(RW_HEADS)[:, None]).astype(f32)
    return head, jnp.pad(head, ((0, HEAD_ROWS_PAD - RW_HEADS), (0, 0))), _block_ones(HALF_W, RW_N)


def _const_specs(consts):
    return [pl.BlockSpec(c.shape, lambda i: (0, 0)) for c in consts]


def rwkv_scan_fwd(rows_in, v_heads, orders, name):
    bsz, n_tok, _ = rows_in[0][0][0].shape
    n_ch = n_tok // SCAN_CHUNK
    rng = range(bsz)
    consts = _scan_consts(bsz)

    def body(*refs):
        rows = [refs[:N_ROWS_FWD], refs[N_ROWS_FWD:2 * N_ROWS_FWD]]
        v0, v1, head_ref, pad_ref, ones_ref, y0, y1, cs0, cs1, s0, s1, late_ref, raw_ref = refs[2 * N_ROWS_FWD:]
        v_refs, y_refs, cs_refs, s_refs = (v0, v1), (y0, y1), (cs0, cs1), (s0, s1)
        head_v, pad_v, ones_v = head_ref[...], pad_ref[...], ones_ref[...]
        for d in range(2):
            @pl.when(pl.program_id(0) == 0)
            def _(d=d):
                s_refs[d][...] = jnp.zeros_like(s_refs[d])

            cs_refs[d][...] = s_refs[d][...]

        def step(j, carry):
            ts = [SCAN_CHUNK - 1 - j if reverse else j for reverse, _ in SCAN_DIRS]
            sps = [[s_refs[d][b] for b in rng] for d in range(2)]
            rems = [_removed(sps[d], [_row(rows[d][1], b, ts[d]) for b in rng], ones_v, bsz) for d in range(2)]
            vks = [[_outer(v_refs[d][b, ts[d]], _row(rows[d][4], b, ts[d]), pad_v) for b in rng] for d in range(2)]
            for d, (reverse, inclusive) in enumerate(SCAN_DIRS):
                r_ref = rows[d][0]
                if inclusive:
                    before = jnp.maximum(j - 1, 0)
                    late_ref[j] = _read(sps[d], [_row(r_ref, b, before) for b in rng], head_v)
                else:
                    raw_ref[ts[d]] = _read(sps[d], [_row(r_ref, b, ts[d]) for b in rng], head_v)
            for d in range(2):
                new = _advance(sps[d], rems[d], [_row(rows[d][2], b, ts[d]) for b in rng],
                               [_row(rows[d][3], b, ts[d]) for b in rng], vks[d], bsz)
                for b in rng:
                    s_refs[d][b] = new[b]
            return carry

        lax.fori_loop(0, SCAN_CHUNK, step, 0, unroll=SCAN_UNROLL)
        for d, (reverse, inclusive) in enumerate(SCAN_DIRS):
            if inclusive:
                assert not reverse
                last = SCAN_CHUNK - 1
                late_ref[SCAN_CHUNK] = _read([s_refs[d][b] for b in rng], [rows[d][0][b, last:last + 1, :] for b in rng],
                                             head_v)
                _blocks_to_rows(late_ref, 1, y_refs[d], bsz)
            else:
                _blocks_to_rows(raw_ref, 0, y_refs[d], bsz)

    specs = [_scan_specs(bsz, orders[d]) for d in range(2)]
    state = pltpu.VMEM((bsz, RW_N, RW_W), f32)
    late = pltpu.VMEM((SCAN_CHUNK + 1, RW_HEADS * bsz, RW_N * bsz), f32)
    raw = pltpu.VMEM((SCAN_CHUNK, RW_HEADS * bsz, RW_N * bsz), f32)
    start_spec = pl.BlockSpec((None, bsz, RW_N, RW_W), lambda i: (i, 0, 0, 0))
    return pl.pallas_call(
        body, grid=(n_ch,),
        in_specs=[specs[d][0](col) for d in range(2) for _, col in rows_in[d]] + [specs[0][1], specs[1][1]]
        + _const_specs(consts),
        out_specs=[specs[0][0](), specs[1][0](), start_spec, start_spec],
        out_shape=[jax.ShapeDtypeStruct((bsz, n_tok, RW_W), f32)] * 2
        + [jax.ShapeDtypeStruct((n_ch, bsz, RW_N, RW_W), f32)] * 2,
        scratch_shapes=[state, state, late, raw],
        compiler_params=_cparams(("arbitrary",)), name=name)(
        *[a for d in range(2) for a, _ in rows_in[d]], v_heads, v_heads, *consts)


def rwkv_scan_bwd(rows_in, v_heads, dy_heads, starts, orders, name):
    bsz, n_tok, _ = rows_in[0][0][0].shape
    n_ch = n_tok // SCAN_CHUNK
    backs = [functools.partial(lambda i, order: order(n_ch - 1 - i), order=orders[d]) for d in range(2)]
    rng = range(bsz)
    consts = _scan_consts(bsz)
    n_out, n_scr = 6, 8

    def body(*refs):
        rows = [refs[:N_ROWS_BWD], refs[N_ROWS_BWD:2 * N_ROWS_BWD]]
        rest = refs[2 * N_ROWS_BWD:]
        v_refs, dy_refs, cs_refs, (head_ref, pad_ref, ones_ref) = rest[0:2], rest[2:4], rest[4:6], rest[6:9]
        outs = [rest[9:9 + n_out], rest[9 + n_out:9 + 2 * n_out]]
        scr = [rest[9 + 2 * n_out:9 + 2 * n_out + n_scr], rest[9 + 2 * n_out + n_scr:]]
        head_v, pad_v, ones_v = head_ref[...], pad_ref[...], ones_ref[...]
        for d in range(2):
            s_ref, ds_ref = scr[d][:2]

            @pl.when(pl.program_id(0) == 0)
            def _(ds_ref=ds_ref):
                ds_ref[...] = jnp.zeros_like(ds_ref)

            s_ref[...] = cs_refs[d][...]

        def fstep(j, carry):
            ts = [SCAN_CHUNK - 1 - j if reverse else j for reverse, _ in SCAN_DIRS]
            sps = [[scr[d][0][b] for b in rng] for d in range(2)]
            rems = [_removed(sps[d], [_row(rows[d][1], b, ts[d]) for b in rng], ones_v, bsz) for d in range(2)]
            vks = [[_outer(v_refs[d][b, ts[d]], _row(rows[d][4], b, ts[d]), pad_v) for b in rng] for d in range(2)]
            for d in range(2):
                s_ref, _, hist_ref, rem_ref = scr[d][:4]
                new = _advance(sps[d], rems[d], [_row(rows[d][2], b, ts[d]) for b in rng],
                               [_row(rows[d][3], b, ts[d]) for b in rng], vks[d], bsz)
                for b in rng:
                    hist_ref[ts[d], b] = sps[d][b]
                    rem_ref[ts[d], b] = rems[d][b]
                    s_ref[b] = new[b]
            return carry

        lax.fori_loop(0, SCAN_CHUNK, fstep, 0, unroll=SCAN_UNROLL)

        def step_of(j, reverse):
            return j if reverse else SCAN_CHUNK - 1 - j

        for d, (reverse, _) in enumerate(SCAN_DIRS):
            t0 = step_of(0, reverse)
            for b in rng:
                scr[d][6][b] = _outer(dy_refs[d][b, t0], rows[d][0][b, t0:t0 + 1, :], pad_v)

        def bstep(j, carry):
            ts = [step_of(j, reverse) for reverse, _ in SCAN_DIRS]
            reads = [[scr[d][6][b] for b in rng] for d in range(2)]
            dss = []
            for d, (_, inclusive) in enumerate(SCAN_DIRS):
                ds = [scr[d][1][b] for b in rng]
                dss.append([ds[b] + reads[d][b] for b in rng] if inclusive else ds)
            drems = [_removed(dss[d], [-_row(rows[d][3], b, ts[d]) for b in rng], ones_v, bsz) for d in range(2)]
            for d, (reverse, _) in enumerate(SCAN_DIRS):
                t_next = step_of(jnp.minimum(j + 1, SCAN_CHUNK - 1), reverse)
                for b in rng:
                    scr[d][6][b] = _outer(dy_refs[d][b, t_next], _row(rows[d][0], b, t_next), pad_v)
                scr[d][7][ts[d]] = _read(dss[d], [_row(rows[d][4], b, ts[d]) for b in rng], head_v)
            for d, (_, inclusive) in enumerate(SCAN_DIRS):
                _, kk_ref, w_ref, _, _ = rows[d]
                _, ds_ref, _, _, dsh_ref, drem_ref = scr[d][:6]
                for b in rng:
                    dsh_ref[ts[d], b] = dss[d][b]
                    drem_ref[ts[d], b] = drems[d][b]
                    dsp = dss[d][b] * _row(w_ref, b, ts[d]) + drems[d][b] * _row(kk_ref, b, ts[d])
                    ds_ref[b] = dsp if inclusive else dsp + reads[d][b]
            return carry

        lax.fori_loop(0, SCAN_CHUNK, bstep, 0, unroll=SCAN_UNROLL)

        rsum = lambda z: jnp.sum(z, axis=0, keepdims=True)
        for d, (reverse, inclusive) in enumerate(SCAN_DIRS):
            dr_ref, dkk_ref, dw_ref, db_ref, dkt_ref, dv_ref = outs[d]
            s_ref, _, hist_ref, rem_ref, dsh_ref, drem_ref, _, dv_raw_ref = scr[d]
            _blocks_to_rows(dv_raw_ref, 0, dv_ref, bsz)
            for t in range(SCAN_CHUNK):
                ts = slice(t, t + 1)
                after = t - 1 if reverse else t + 1
                for b in rng:
                    sp, ds = hist_ref[t, b], dsh_ref[t, b]
                    if not inclusive:
                        seen = sp
                    else:
                        seen = hist_ref[after, b] if 0 <= after < SCAN_CHUNK else s_ref[b]
                    dr_ref[b, ts, :] = _row_from_heads(dy_refs[d][b, t], seen, pad_v)
                    dkt_ref[b, ts, :] = _row_from_heads(v_refs[d][b, t], ds, pad_v)
                    dw_ref[b, ts, :] = rsum(ds * sp)
                    db_ref[b, ts, :] = -rsum(ds * rem_ref[t, b])
                    dkk_ref[b, ts, :] = rsum(sp * drem_ref[t, b])

    specs = [_scan_specs(bsz, backs[d]) for d in range(2)]
    hist = pltpu.VMEM((SCAN_CHUNK, bsz, RW_N, RW_W), f32)
    state = pltpu.VMEM((bsz, RW_N, RW_W), f32)
    start_spec = pl.BlockSpec((None, bsz, RW_N, RW_W), lambda i: (n_ch - 1 - i, 0, 0, 0))
    raw = pltpu.VMEM((SCAN_CHUNK, RW_HEADS * bsz, RW_N * bsz), f32)
    return pl.pallas_call(
        body, grid=(n_ch,),
        in_specs=[specs[d][0](col) for d in range(2) for _, col in rows_in[d]]
        + [specs[0][1], specs[1][1]] * 2 + [start_spec, start_spec] + _const_specs(consts),
        out_specs=[specs[d][0]() for d in range(2) for _ in range(n_out)],
        out_shape=[jax.ShapeDtypeStruct((bsz, n_tok, RW_W), f32)] * (2 * n_out),
        scratch_shapes=[state, state, hist, hist, hist, hist, state, raw] * 2,
        compiler_params=_cparams(("arbitrary",)), name=name)(
        *[a for d in range(2) for a, _ in rows_in[d]], v_heads, v_heads, dy_heads, dy_heads, *starts, *consts)


MOD_NAMES = ("shift1", "scale1", "gate1", "shift2", "scale2", "gate2")


def _rope_tables(t_ctx, t_x):
    quarter = RET_DH // 4
    pos = jnp.arange(t_x)
    inv = jnp.power(ROPE_BASE, -jnp.arange(0, 2 * quarter, 2, dtype=f32) / (2 * quarter))
    ang_r = (pos // GRID_W).astype(f32)[:, None] * inv[None, :]
    ang_c = (pos % GRID_W).astype(f32)[:, None] * inv[None, :]
    cos = jnp.concatenate([jnp.cos(ang_r)] * 2 + [jnp.cos(ang_c)] * 2, axis=1)
    sin = jnp.concatenate([-jnp.sin(ang_r), jnp.sin(ang_r), -jnp.sin(ang_c), jnp.sin(ang_c)], axis=1)
    cos = jnp.concatenate([jnp.ones((t_ctx, RET_DH), f32), cos], axis=0)
    sin = jnp.concatenate([jnp.zeros((t_ctx, RET_DH), f32), sin], axis=0)
    return cos, sin


def _pad_rows(w, lo, total):
    return jnp.pad(w, ((lo, total - lo - w.shape[0]), (0, 0)))


LATE_WEIGHTS = ("w_out", "w_ff1", "w_ff2")


def layer_step(x, ctx, tgt, mod_x, mod_ctx, wt, late_weights=None, early_grads=None):
    bsz, t_x, _ = x.shape
    t_c = ctx.shape[1]
    t_all = t_c + t_x
    n_ct, n_xt = t_c // TOK_TILE, t_x // TOK_TILE
    n_t = n_ct + n_xt
    assert t_c % TOK_TILE == 0 and t_x % TOK_TILE == 0 and t_c % RET_CHUNK == 0

    seg = lambda i: (i >= n_ct).astype(jnp.int32)
    seg_first = lambda i: jnp.logical_or(i == 0, i == n_ct)
    seg_last = lambda i: jnp.logical_or(i == n_ct - 1, i == n_t - 1)
    mod_all = {n: jnp.stack([jnp.broadcast_to(mod_ctx[k], (bsz, D_MODEL)), mod_x[:, k]], axis=1)[:, :, None, :]
               for k, n in enumerate(MOD_NAMES)}
    mod_lat = {n: mod_x[:, k][:, None, None, :] for k, n in enumerate(MOD_NAMES)}
    both = lambda n: Seg(mod_all[n], seg, seg_first)
    lat = lambda n: Seg(mod_lat[n], lambda i: 0, lambda i: i == 0)
    flat = lambda a: a.reshape(-1, a.shape[-1])

    def chunk_orders(n_ctx_chunks, n_chunks):
        fwd = lambda i: i
        bwd = lambda i: jnp.where(i < n_ctx_chunks, n_ctx_chunks - 1 - i, n_chunks + n_ctx_chunks - 1 - i)
        return fwd, bwd

    ones64, ones128 = _block_ones(RW_W, RW_N), _block_ones(RET_W, RET_DH)
    cos, sin = _rope_tables(t_c, t_x)
    ld_rows = [jnp.pad(wt["ret_log_decay"][d][None, :], ((0, 0), (0, RET_DH - RET_HEADS))) for d in range(2)]
    w_up_pad = [_pad_rows(wt["rwkv_w_up"][d], 0, LORA_W) for d in range(2)]
    a_up_pad = [_pad_rows(wt["rwkv_a_up"][d], DECAY_LORA, LORA_W) for d in range(2)]
    g_up_pad = _pad_rows(wt["rwkv_g_up"], DECAY_LORA + AAA_LORA, LORA_W)
    row = lambda a, d: a[d][None, :]

    h = jnp.concatenate([ctx, x], axis=1)
    norm1_ins = lambda: [Tiled(h), both("shift1"), both("scale1"), Glob(wt["norm1_g"])]
    (n1,) = ew_forward(fn_norm_mod, "norm1", bsz, n_t, norm1_ins(), [(D_MODEL, MXU_DTYPE)])
    px = matmul(flat(n1), wt["w_in"], "nn", "proj_in").reshape(bsz, t_all, IN_COLS)
    px_rw = px[..., RET_COLS:]
    ps = token_shift(px_rw, wt["rwkv_shift_mu"], seg_first, seg_last)

    def prep_ins(toff=0):
        return [Tiled(ps, RW_W, 1), Tiled(ps, LORA_W, 3 * RW_W // LORA_W),
                Glob(row(wt["rwkv_w0"], 0)), Glob(row(wt["rwkv_w0"], 1)),
                Glob(row(wt["rwkv_a0"], 0)), Glob(row(wt["rwkv_a0"], 1)),
                Glob(w_up_pad[0]), Glob(w_up_pad[1]), Glob(a_up_pad[0]), Glob(a_up_pad[1]), Glob(g_up_pad),
                Glob(wt["rwkv_k_k"]), Glob(wt["rwkv_k_a"]), Glob(ones64)]

    kk, w_f, b_f, kt_f, w_b, b_b, kt_b, g_rw = ew_forward(fn_rwkv_prepare, "rwkv_prepare", bsz, n_t, prep_ins(),
                                                           [(RW_W, f32)] * 8)
    rw_order = chunk_orders(t_c // SCAN_CHUNK, t_all // SCAN_CHUNK)
    ret_order = chunk_orders(t_c // RET_CHUNK, t_all // RET_CHUNK)
    scan_rows = [[(ps, 0), (kk, 0), (w_f, 0), (b_f, 0), (kt_f, 0)], [(ps, 0), (kk, 0), (w_b, 0), (b_b, 0), (kt_b, 0)]]
    v_heads = heads_to_rows(ps[..., 2 * RW_W:3 * RW_W])
    *y, start_f, start_b = rwkv_scan_fwd(scan_rows, v_heads, rw_order, "rwkv_scan_fwd")
    o, ret_states = [], []
    for d in range(2):
        o_d, st_d = retention_fwd(px, cos, sin, ld_rows[d], ret_order[d], SCAN_DIRS[d][0], f"retention_fwd{d}")
        o.append(o_d), ret_states.append(st_d)

    def merge_ins(toff):
        return [Tiled(o[0], toff=toff), Tiled(o[1], toff=toff), Tiled(px, RET_W, 3, toff),
                Tiled(y[0], toff=toff), Tiled(y[1], toff=toff), Tiled(ps, RW_W, 0, toff), Tiled(kt_f, toff=toff),
                Tiled(ps, RW_W, 2, toff), Tiled(g_rw, toff=toff),
                Glob(wt["rwkv_r_k"]), Glob(wt["rwkv_ln_w"]), Glob(wt["rwkv_ln_b"]), Glob(ones64), Glob(ones128)]

    ret_out, rw_out = ew_forward(fn_merge, "merge_heads", bsz, n_xt, merge_ins(n_ct),
                                 [(RET_W, MXU_DTYPE), (RW_W, MXU_DTYPE)])
    merged = jnp.concatenate([ret_out, rw_out], axis=-1)
    if late_weights is not None:
        wt = {**wt, **late_weights(merged)}
    mix = matmul(flat(merged), wt["w_out"], "nn", "proj_out").reshape(bsz, t_x, D_MODEL)
    resid_ins = lambda: [Tiled(x), Tiled(mix), lat("gate1"), lat("shift2"), lat("scale2"), Glob(wt["norm2_g"])]
    h1, n2 = ew_forward(fn_resid_norm_mod, "resid_norm2", bsz, n_xt, resid_ins(), [(D_MODEL, f32), (D_MODEL, MXU_DTYPE)])
    act = matmul(flat(n2), wt["w_ff1"], "nn", "ff1", MXU_DTYPE, wt["b_ff1"], relu2).reshape(bsz, t_x, D_FF)
    ff = matmul(flat(act), wt["w_ff2"], "nn", "ff2").reshape(bsz, t_x, D_MODEL)

    g = {}
    loss, dh1, dff, dgate2, g["b_ff2"], g["final_g"] = loss_and_grads(
        h1, ff, tgt, mod_lat["gate2"], wt["b_ff2"], wt["final_g"], bsz, n_xt)
    dact = matmul(flat(dff), wt["w_ff2"], "nt", "ff2_dx", MXU_DTYPE).reshape(bsz, t_x, D_FF)
    g["w_ff2"] = matmul(flat(act), flat(dff), "tn", "ff2_dw")
    du, g["b_ff1"] = relu2_backward(act, dact, "relu2_bwd")
    dn2 = matmul(flat(du), wt["w_ff1"], "nt", "ff1_dx").reshape(bsz, t_x, D_MODEL)
    g["w_ff1"] = matmul(flat(n2), flat(du), "tn", "ff1_dw")
    dx_res, dmix, dgate1, dshift2, dscale2, g["norm2_g"] = ew_backward(
        fn_resid_norm_mod, "resid_norm2_bwd", bsz, n_xt, resid_ins(), [Tiled(dh1), Tiled(dn2)], [True] * 6,
        {1: MXU_DTYPE})
    dmerged = matmul(flat(dmix), wt["w_out"], "nt", "proj_out_dx").reshape(bsz, t_x, D_MODEL)
    g["w_out"] = matmul(flat(merged), flat(dmix), "tn", "proj_out_dw")
    if early_grads is not None:
        token = early_grads({n: g.pop(n) for n in LATE_WEIGHTS})
        wt = {**wt, "rwkv_r_k": wt["rwkv_r_k"] + token[:1, :1]}
    (do, dg_ret, dy, dr_m, dkt_m, dv_m, dg_rw, g["rwkv_r_k"], g["rwkv_ln_w"], g["rwkv_ln_b"]) = ew_backward(
        fn_merge, "merge_heads_bwd", bsz, n_xt, merge_ins(0),
        [Tiled(dmerged, RET_W, 0, -n_ct), Tiled(dmerged, RW_W, 1, -n_ct)],
        [True, False, True, True, False, True, True, True, True, True, True, True, False, False], lead=n_ct)

    dqkv, dld = [], []
    for d in range(2):
        *dqkv_d, dld_d = retention_bwd(do, px, ret_states[d], cos, sin, ld_rows[d], ret_order[d],
                                       SCAN_DIRS[d][0], f"retention_bwd{d}")
        dqkv.append(dqkv_d), dld.append(dld_d[0, :RET_HEADS])
    g["ret_log_decay"] = jnp.stack(dld)
    (dr_f, dkk_f, dw_f, db_f, dkt_f, dv_f, dr_b, dkk_b, dw_b, db_b, dkt_b, dv_b) = rwkv_scan_bwd(
        scan_rows, v_heads, heads_to_rows(dy), (start_f, start_b), rw_order, "rwkv_scan_bwd")
    prep_cts = [dkk_f + dkk_b, dw_f, db_f, dkt_f + dkt_m, dw_b, db_b, dkt_b, dg_rw]
    (dks, dlora, dw0_f, dw0_b, da0_f, da0_b, dwup_f, dwup_b, daup_f, daup_b, dgup, g["rwkv_k_k"],
     g["rwkv_k_a"]) = ew_backward(fn_rwkv_prepare, "rwkv_prepare_bwd", bsz, n_t, prep_ins(),
                                  [Tiled(c) for c in prep_cts], [True] * 13 + [False])
    g["rwkv_w0"] = jnp.concatenate([dw0_f, dw0_b], axis=0)
    g["rwkv_a0"] = jnp.concatenate([da0_f, da0_b], axis=0)
    g["rwkv_w_up"] = jnp.stack([dwup_f[:DECAY_LORA], dwup_b[:DECAY_LORA]])
    g["rwkv_a_up"] = jnp.stack([daup_f[DECAY_LORA:DECAY_LORA + AAA_LORA], daup_b[DECAY_LORA:DECAY_LORA + AAA_LORA]])
    g["rwkv_g_up"] = dgup[DECAY_LORA + AAA_LORA:]
    dps = jnp.concatenate([dr_f + dr_b + dr_m, dks, dv_f + dv_b + dv_m, dlora], axis=-1)
    dp_rw, g["rwkv_shift_mu"] = token_shift_bwd(dps, px_rw, wt["rwkv_shift_mu"], seg_first, seg_last)
    dpx = jnp.concatenate([(dqkv[0][k] + dqkv[1][k]).astype(MXU_DTYPE) for k in range(3)]
                          + [dg_ret.astype(MXU_DTYPE), dp_rw], axis=-1)
    dn1 = matmul(flat(dpx), wt["w_in"], "nt", "proj_in_dx").reshape(bsz, t_all, D_MODEL)
    g["w_in"] = matmul(flat(n1), flat(dpx), "tn", "proj_in_dw")
    dh, dshift1, dscale1, g["norm1_g"] = ew_backward(fn_norm_mod, "norm1_bwd", bsz, n_t, norm1_ins(), [Tiled(dn1)],
                                                     [True] * 4)
    grad_x = dh[:, t_c:] + dx_res
    zeros = jnp.zeros((D_MODEL,), f32)
    g["mod_x"] = jnp.stack([dshift1[:, 1, 0], dscale1[:, 1, 0], dgate1[:, 0, 0], dshift2[:, 0, 0], dscale2[:, 0, 0],
                            dgate2[:, 0, 0]], axis=1)
    g["mod_ctx"] = jnp.stack([dshift1[:, 0, 0].sum(0), dscale1[:, 0, 0].sum(0), zeros, zeros, zeros, zeros])
    return loss, grad_x, g


MESH_ID = pl.DeviceIdType.MESH
ALL_PEERS = [(dx, dy, dc) for dx in (0, 1) for dy in (0, 1) for dc in (0, 1)][1:]
CHIP_PEERS = [(1, 0, 0), (0, 1, 0), (1, 1, 0)]
CHIP_SLOTS = (0, 2, 4, 6)


def _mesh_pos():
    return lax.axis_index("x"), lax.axis_index("y"), lax.axis_index("c")


def _device_slot():
    x, y, c = _mesh_pos()
    return 4 * x + 2 * y + c


def sibling_swap(arrs, name, pieces=1):
    n = len(arrs)
    assert all(a.shape[0] % pieces == 0 for a in arrs)

    def body(*refs):
        in_refs, out_refs = refs[:n], refs[n:2 * n]
        send_sems, recv_sems = refs[2 * n:]
        x, y, c = _mesh_pos()
        copies = []
        for a in range(n):
            rows = arrs[a].shape[0] // pieces
            for q in range(pieces):
                part = pl.ds(q * rows, rows)
                cp = pltpu.make_async_remote_copy(
                    src_ref=in_refs[a].at[part], dst_ref=out_refs[a].at[part], send_sem=send_sems.at[a * pieces + q],
                    recv_sem=recv_sems.at[a * pieces + q], device_id=(x, y, 1 - c), device_id_type=MESH_ID)
                cp.start()
                copies.append(cp)
        for cp in copies:
            cp.wait()

    any_spec = pl.BlockSpec(memory_space=pl.ANY)
    res = pl.pallas_call(
        body, in_specs=[any_spec] * n, out_specs=[any_spec] * n,
        out_shape=[jax.ShapeDtypeStruct(a.shape, a.dtype) for a in arrs],
        scratch_shapes=[pltpu.SemaphoreType.DMA((n * pieces,)), pltpu.SemaphoreType.DMA((n * pieces,))],
        name=name)(*arrs)
    return list(res)


def exchange(arrs, gather, peers, name, pieces=1, by_chip=False, own=True):
    n, n_peers = len(arrs), len(peers)
    n_slots = N_SHARDS if by_chip else N_DEV
    slot = (lambda x, y, c: 2 * x + y) if by_chip else (lambda x, y, c: 4 * x + 2 * y + c)
    block_rows = [a.shape[0] if gather else a.shape[1] for a in arrs]
    assert all(r % pieces == 0 for r in block_rows), (block_rows, pieces)

    def body(*refs):
        in_refs, out_refs = refs[:n], refs[n:2 * n]
        send_sems, recv_sems, local_sems = refs[2 * n:]
        x, y, c = _mesh_pos()
        me = slot(x, y, c)
        copies, locals_ = [], []
        for a in range(n):
            if own:
                mine = in_refs[a] if gather else in_refs[a].at[me]
                loc = pltpu.make_async_copy(mine, out_refs[a].at[me], local_sems.at[a])
                loc.start()
                locals_.append(loc)
            for k, (dx, dy, dc) in enumerate(peers):
                peer = (1 - x if dx else x, 1 - y if dy else y, 1 - c if dc else c)
                src = in_refs[a] if gather else in_refs[a].at[slot(*peer)]
                for q in range(pieces):
                    part = pl.ds(q * (block_rows[a] // pieces), block_rows[a] // pieces)
                    sem = (a * n_peers + k) * pieces + q
                    cp = pltpu.make_async_remote_copy(
                        src_ref=src.at[part], dst_ref=out_refs[a].at[me, part], send_sem=send_sems.at[sem],
                        recv_sem=recv_sems.at[sem], device_id=peer, device_id_type=MESH_ID)
                    cp.start()
                    copies.append(cp)
        for cp in copies:
            cp.wait()
        for loc in locals_:
            loc.wait()

    any_spec = pl.BlockSpec(memory_space=pl.ANY)
    out_shape = [jax.ShapeDtypeStruct((n_slots,) + (a.shape if gather else a.shape[1:]), a.dtype) for a in arrs]
    n_sems = n * n_peers * pieces
    res = pl.pallas_call(
        body, in_specs=[any_spec] * n, out_specs=[any_spec] * n, out_shape=out_shape,
        scratch_shapes=[pltpu.SemaphoreType.DMA((n_sems,)), pltpu.SemaphoreType.DMA((n_sems,)),
                        pltpu.SemaphoreType.DMA((n,))],
        name=name)(*arrs)
    return list(res)


HBM_SPEC = pl.BlockSpec(memory_space=pltpu.HBM)
SEM_SPEC = pl.BlockSpec(memory_space=pltpu.SEMAPHORE)
DATAFLOW = pltpu.SideEffectType.DATAFLOW_SIDE_EFFECTING


def _peer_copies(src_refs, land_refs, send_sems, recv_sems, gather):
    x, y, c = _mesh_pos()
    me = 4 * x + 2 * y + c
    copies = []
    for a, (src_ref, land_ref) in enumerate(zip(src_refs, land_refs)):
        for k, (dx, dy, dc) in enumerate(ALL_PEERS):
            peer = (1 - x if dx else x, 1 - y if dy else y, 1 - c if dc else c)
            src = src_ref if gather else src_ref.at[4 * peer[0] + 2 * peer[1] + peer[2]]
            sem = a * len(ALL_PEERS) + k
            copies.append(pltpu.make_async_remote_copy(src_ref=src, dst_ref=land_ref.at[me], send_sem=send_sems.at[sem],
                                                       recv_sem=recv_sems.at[sem], device_id=peer,
                                                       device_id_type=MESH_ID))
    return copies


def exchange_start(arrs, gather, name):
    n = len(arrs)
    lands = [lax.empty((N_DEV,) + (a.shape if gather else a.shape[1:]), a.dtype) for a in arrs]

    def body(*refs):
        for cp in _peer_copies(refs[:n], refs[n:2 * n], refs[2 * n], refs[2 * n + 1], gather):
            cp.start()
        refs[-1][...] = jnp.zeros_like(refs[-1])

    sems = pltpu.SemaphoreType.DMA((n * len(ALL_PEERS),))
    hbm = [pltpu.HBM(a.shape, a.dtype) for a in arrs + lands]
    res = pl.pallas_call(
        body, name=name, out_shape=(sems, sems, *hbm, jax.ShapeDtypeStruct((8, 128), f32)),
        in_specs=[HBM_SPEC] * (2 * n),
        out_specs=(SEM_SPEC, SEM_SPEC, *[HBM_SPEC] * (2 * n), pl.BlockSpec(memory_space=pltpu.VMEM)),
        input_output_aliases={i: 2 + i for i in range(2 * n)},
        compiler_params=pltpu.CompilerParams(has_side_effects=DATAFLOW))(
        *[pltpu.with_memory_space_constraint(a, pltpu.HBM) for a in arrs + lands])
    return res[0], res[1], list(res[2:2 + n]), list(res[2 + n:2 + 2 * n]), res[-1]


def exchange_wait(started, after, gather, name):
    send_sems, recv_sems, srcs, lands, _ = started
    n = len(srcs)

    def body(*refs):
        for cp in _peer_copies(refs[:n], refs[n:2 * n], refs[2 * n], refs[2 * n + 1], gather):
            cp.wait_send()
            cp.wait_recv()

    res = pl.pallas_call(
        body, name=name, out_shape=tuple(pltpu.HBM(a.shape, a.dtype) for a in srcs + lands),
        in_specs=[HBM_SPEC] * (2 * n) + [SEM_SPEC, SEM_SPEC, pl.BlockSpec(memory_space=pl.ANY)],
        out_specs=tuple([HBM_SPEC] * (2 * n)), input_output_aliases={i: i for i in range(2 * n)},
        compiler_params=pltpu.CompilerParams(has_side_effects=DATAFLOW))(*srcs, *lands, send_sems, recv_sems, after)
    return list(res[n:])


def add_arrays(parts, name, out_dtype=f32):
    r, c = parts[0].shape
    tr = r
    for cand in (512, 256, 128, 64, 32, 16):
        if r % cand == 0:
            tr = cand
            break

    def body(*refs):
        acc = refs[0][...].astype(f32)
        for p_ref in refs[1:-1]:
            acc = acc + p_ref[...].astype(f32)
        refs[-1][...] = acc.astype(out_dtype)

    spec = pl.BlockSpec((tr, c), lambda i: (i, 0))
    return pl.pallas_call(body, grid=(r // tr,), in_specs=[spec] * len(parts), out_specs=spec,
                          out_shape=jax.ShapeDtypeStruct((r, c), out_dtype),
                          compiler_params=_cparams(("parallel",)), name=name)(*parts)


def gather_two_level(arrs, name):
    n = len(arrs)
    per = 7

    def body(*refs):
        in_refs, out_refs = refs[:n], refs[n:2 * n]
        send_sems, recv_sems = refs[2 * n:]
        x, y, c = _mesh_pos()
        me, sibling = (x, y, c), (x, y, 1 - c)
        chips = [(1 - x, y), (x, 1 - y), (1 - x, 1 - y)]

        def copy(a, k, block, to, src=None):
            rows = out_refs[a].at[4 * block[0] + 2 * block[1] + block[2]]
            return pltpu.make_async_remote_copy(src_ref=rows if src is None else src, dst_ref=rows,
                                                send_sem=send_sems.at[a * per + k], recv_sem=recv_sems.at[a * per + k],
                                                device_id=to, device_id_type=MESH_ID)

        first, passed = [], []
        for a in range(n):
            first.append(copy(a, 0, me, sibling, src=in_refs[a]))
            first += [copy(a, 1 + j, me, (*chip, c), src=in_refs[a]) for j, chip in enumerate(chips)]
        for cp in first:
            cp.start()
        for a in range(n):
            for j, chip in enumerate(chips):
                copy(a, 1 + j, (*chip, c), me).wait_recv()
                fwd = copy(a, 4 + j, (*chip, c), sibling)
                fwd.start()
                passed.append(fwd)
        for a in range(n):
            copy(a, 0, sibling, me).wait_recv()
            for j, chip in enumerate(chips):
                copy(a, 4 + j, (*chip, 1 - c), me).wait_recv()
        for cp in first + passed:
            cp.wait_send()

    any_spec = pl.BlockSpec(memory_space=pl.ANY)
    res = pl.pallas_call(
        body, in_specs=[any_spec] * n, out_specs=[any_spec] * n,
        out_shape=[jax.ShapeDtypeStruct((N_DEV,) + a.shape, a.dtype) for a in arrs],
        scratch_shapes=[pltpu.SemaphoreType.DMA((n * per,)), pltpu.SemaphoreType.DMA((n * per,))],
        name=name)(*arrs)
    return list(res)


def sum_slots(parts, slots, name):
    _, r, c = parts.shape
    tr = r
    for cand in (512, 256, 128, 64, 32, 16, 8):
        if r % cand == 0 and cand * c * 4 * len(slots) <= 8 * 1024 * 1024:
            tr = cand
            break

    def body(p_ref, o_ref):
        acc = p_ref[slots[0]].astype(f32)
        for s in slots[1:]:
            acc = acc + p_ref[s].astype(f32)
        o_ref[...] = acc

    return pl.pallas_call(body, grid=(r // tr,), in_specs=[pl.BlockSpec((parts.shape[0], tr, c), lambda i: (0, i, 0))],
                          out_specs=pl.BlockSpec((tr, c), lambda i: (i, 0)),
                          out_shape=jax.ShapeDtypeStruct((r, c), f32),
                          compiler_params=_cparams(("parallel",)), name=name)(parts)


def column_sum(a, name):
    def body(a_ref, o_ref):
        o_ref[...] = jnp.sum(a_ref[...], axis=0, keepdims=True)

    return pl.pallas_call(body, out_shape=jax.ShapeDtypeStruct((1, a.shape[1]), f32), name=name)(a)


def adamw(w, g, m, v, name):
    r, c = w.shape
    tr = r
    for cand in (256, 128, 64, 32, 16, 8):
        if r % cand == 0:
            tr = cand
            break

    def body(w_ref, g_ref, m_ref, v_ref, d_ref, mo_ref, vo_ref):
        gv = g_ref[...]
        m_new = ADAM_B1 * m_ref[...] + (1.0 - ADAM_B1) * gv
        v_new = ADAM_B2 * v_ref[...] + (1.0 - ADAM_B2) * jnp.square(gv)
        m_hat = m_new / (1.0 - ADAM_B1 ** ADAM_STEP)
        v_hat = v_new / (1.0 - ADAM_B2 ** ADAM_STEP)
        d_ref[...] = -ADAM_LR * (m_hat / (jnp.sqrt(v_hat) + ADAM_EPS) + ADAM_WD * w_ref[...])
        mo_ref[...] = m_new
        vo_ref[...] = v_new

    spec = pl.BlockSpec((tr, c), lambda i: (i, 0))
    return pl.pallas_call(body, grid=(r // tr,), in_specs=[spec] * 4, out_specs=[spec] * 3,
                          out_shape=[jax.ShapeDtypeStruct((r, c), f32)] * 3,
                          compiler_params=_cparams(("parallel",)), name=name)(w, g, m, v)


def adaln_fwd(c_rows, w, b):
    def body(c_ref, w_ref, b_ref, o_ref):
        cv = c_ref[...]
        o_ref[...] = _mxu_dot(cv * jax.nn.sigmoid(cv), w_ref[...]) + b_ref[...]

    return pl.pallas_call(body, out_shape=jax.ShapeDtypeStruct((c_rows.shape[0], w.shape[1]), f32),
                          compiler_params=pltpu.CompilerParams(vmem_limit_bytes=VMEM_LIMIT), name="adaln_fwd")(c_rows, w, b)


def adaln_bwd(c_rows, dm, w):
    def body(c_ref, dm_ref, w_ref, gw_ref, ds_ref):
        cv = c_ref[...]
        gw_ref[...] = _dg(cv * jax.nn.sigmoid(cv), dm_ref[...], 0, 0)
        ds_ref[...] = _dg(dm_ref[...], w_ref[...], 1, 1)

    return pl.pallas_call(body, out_shape=[jax.ShapeDtypeStruct(w.shape, f32),
                                           jax.ShapeDtypeStruct(c_rows.shape, f32)],
                          compiler_params=pltpu.CompilerParams(vmem_limit_bytes=VMEM_LIMIT), name="adaln_bwd")(c_rows, dm, w)


def c_ctx_grad(parts, c_ctx_row):
    def body(p_ref, c_ref, o_ref):
        total = p_ref[0, 0:1, :]
        for s in range(1, N_SHARDS):
            total = total + p_ref[s, 0:1, :]
        _, vjp = jax.vjp(jax.nn.silu, c_ref[...])
        o_ref[...] = vjp(total)[0]

    return pl.pallas_call(body, out_shape=jax.ShapeDtypeStruct((1, D_MODEL), f32), name="c_ctx_grad")(parts, c_ctx_row)


PACK_W = 1024
PACK_ROWS = 8


def _pack(arrs):
    pieces, layout, r0 = [], [], 0
    for a in arrs:
        size = math.prod(a.shape)
        rows = -(-size // (PACK_W * PACK_ROWS)) * PACK_ROWS
        pieces.append(jnp.pad(a.reshape(-1).astype(f32), (0, rows * PACK_W - size)).reshape(rows, PACK_W))
        layout.append((r0, rows, a.shape))
        r0 += rows
    return jnp.concatenate(pieces, axis=0), layout


def _unpack(pack, layout, lead=()):
    n_lead = len(lead)
    outs = []
    for r0, rows, shape in layout:
        piece = pack[(slice(None),) * n_lead + (slice(r0, r0 + rows),)].reshape(lead + (-1,))
        outs.append(piece[..., :math.prod(shape)].reshape(lead + tuple(shape)))
    return outs


W_NAMES = ("c_ctx", "w_ada", "b_ada", "norm1_g", "norm2_g", "w_in", "ret_log_decay", "rwkv_shift_mu", "rwkv_w0",
           "rwkv_w_up", "rwkv_a0", "rwkv_a_up", "rwkv_g_up", "rwkv_k_k", "rwkv_k_a", "rwkv_r_k", "rwkv_ln_w",
           "rwkv_ln_b", "w_out", "w_ff1", "b_ff1", "w_ff2", "b_ff2", "final_g")
COL_SHARDED = ("w_in", "w_ff1")
ROW_SHARDED = ("w_out", "w_ff2")
LAST_SHARDED = ("rwkv_shift_mu", "rwkv_w0", "rwkv_w_up", "rwkv_a0", "rwkv_a_up", "rwkv_g_up")
REPLICATED = ("c_ctx", "b_ada", "norm1_g", "norm2_g", "ret_log_decay", "rwkv_k_k", "rwkv_k_a", "rwkv_r_k",
              "rwkv_ln_w", "rwkv_ln_b", "b_ff1", "b_ff2", "final_g")
N_SHARDS = 4


def _train_step(a):
    x, c, ctx, tgt = a["x"], a["c"], a["ctx"], a["loss_target"]
    bsz = x.shape[0]
    mx, my, mc = _mesh_pos()
    shard = 2 * mx + my
    dev = _device_slot()

    (c_all,) = exchange([jnp.pad(c, ((0, PACK_ROWS - bsz), (0, 0)))], True, ALL_PEERS, "gather_c")
    n_ex = N_DEV * bsz
    c_rows = jnp.concatenate([c_all[:, :bsz].reshape(n_ex, D_MODEL), a["c_ctx"][None, :],
                              jnp.zeros((PACK_ROWS - 1, D_MODEL), f32)], axis=0)
    ada_cols = a["w_ada"].shape[-1]
    b_ada_cols = lax.dynamic_slice_in_dim(a["b_ada"], shard * ada_cols, ada_cols, axis=1)
    mod_cols = adaln_fwd(c_rows, a["w_ada"][0], b_ada_cols)

    def own_half(n):
        w = a[n][0].astype(MXU_DTYPE)
        return lax.dynamic_slice_in_dim(w, mc * (w.shape[0] // 2), w.shape[0] // 2, axis=0)

    def whole_weight(n, gth, own):
        per_chip = lax.dynamic_update_index_in_dim(gth, own, dev, 0).reshape(N_SHARDS, -1, gth.shape[-1])
        return (per_chip.transpose(1, 0, 2).reshape(per_chip.shape[1], -1) if n in COL_SHARDED
                else per_chip.reshape(-1, per_chip.shape[-1]))

    small_pack, small_layout = _pack([a[n][0] for n in LAST_SHARDED])
    own_blocks = [mod_cols, own_half("w_in"), small_pack]
    gathered = gather_two_level(own_blocks, "gather_weights")
    late_own = [own_half(n) for n in LATE_WEIGHTS]
    late_started = exchange_start(late_own, True, "gather_late_start")
    mod_own = lax.dynamic_update_index_in_dim(gathered[0], mod_cols, dev, 0)
    mod_all = jnp.stack([mod_own[s] for s in CHIP_SLOTS], axis=1).reshape(c_rows.shape[0], -1)
    mod_all = mod_all + late_started[-1][0, 0]
    mod_x = lax.dynamic_slice_in_dim(mod_all, dev * bsz, bsz, axis=0).reshape(bsz, 6, D_MODEL)
    mod_ctx = mod_all[n_ex].reshape(6, D_MODEL)
    wt = {"w_in": whole_weight("w_in", gathered[1], own_blocks[1])}

    def late_weights(after):
        lands = exchange_wait(late_started, after, True, "gather_late_wait")
        return {n: whole_weight(n, land, own) for n, land, own in zip(LATE_WEIGHTS, lands, late_own)}

    def grad_blocks(n, gw):
        if n in COL_SHARDED:
            gw = gw.reshape(gw.shape[0], N_SHARDS, -1).transpose(1, 0, 2)
        return gw.reshape(N_DEV, -1, gw.shape[-1]).astype(MXU_DTYPE)

    late_sent = {}

    def early_grads(late_g):
        late_sent["blocks"] = [grad_blocks(n, late_g[n]) for n in LATE_WEIGHTS]
        late_sent["started"] = exchange_start(late_sent["blocks"], False, "scatter_late_start")
        return late_sent["started"][-1]

    small_own = lax.dynamic_update_index_in_dim(gathered[2], small_pack, dev, 0)
    small_by_chip = _unpack(jnp.stack([small_own[s] for s in CHIP_SLOTS]), small_layout, (N_SHARDS,))
    for n, parts in zip(LAST_SHARDED, small_by_chip):
        wt[n] = jnp.concatenate([parts[s] for s in range(N_SHARDS)], axis=-1)
    for n in ("norm1_g", "norm2_g", "rwkv_k_k", "rwkv_k_a", "rwkv_r_k", "rwkv_ln_w", "rwkv_ln_b", "b_ff1", "b_ff2"):
        wt[n] = a[n]
    wt["ret_log_decay"] = a["ret_log_decay"][0]
    wt["final_g"] = a["final_g"][None, :]

    loss, grad_x, g = layer_step(x, ctx, tgt, mod_x, mod_ctx, wt, late_weights, early_grads)

    small_names = [n for n in REPLICATED if n not in ("c_ctx", "b_ada")]
    g_pack, g_layout = _pack([jnp.pad(loss, ((0, 0), (0, PACK_W - loss.shape[1])))] + [g[n] for n in small_names]
                             + [g["mod_x"], g["mod_ctx"]])
    (g_packs,) = gather_two_level([g_pack], "gather_small_grads")
    g_packs = lax.dynamic_update_index_in_dim(g_packs, g_pack, dev, 0)
    g_sum = _unpack(sum_slots(g_packs, tuple(range(N_DEV)), "sum_small_grads"), g_layout)
    loss_total = g_sum[0][0, 0]
    grads = dict(zip(small_names, g_sum[1:1 + len(small_names)]))
    dmod_ctx = g_sum[-1].reshape(1, -1)
    dmod_x = _unpack(g_packs, g_layout, (N_DEV,))[-2].reshape(n_ex, -1)
    dmod = jnp.concatenate([dmod_x, dmod_ctx, jnp.zeros((PACK_ROWS - 1, dmod_x.shape[1]), f32)], axis=0)
    grads["b_ada"] = column_sum(dmod, "b_ada_grad")
    dmod_cols = lax.dynamic_slice_in_dim(dmod, shard * ada_cols, ada_cols, axis=1)
    grads["w_ada"], dsilu = adaln_bwd(c_rows, dmod_cols, a["w_ada"][0])

    blocks = [grad_blocks("w_in", g["w_in"])]
    shard_packs = []
    for s in range(N_SHARDS):
        pieces_s = [lax.slice_in_dim(g[n], s * a[n].shape[-1], (s + 1) * a[n].shape[-1], axis=g[n].ndim - 1)
                    for n in LAST_SHARDED]
        pack_s, shard_layout = _pack(pieces_s)
        shard_packs.append(jnp.pad(pack_s, ((0, -pack_s.shape[0] % (2 * PACK_ROWS)), (0, 0))))
    blocks.append(jnp.stack(shard_packs).reshape(N_DEV, -1, PACK_W))
    scattered = ("w_in", "small_shards")
    halves_of = lambda blk, core: lax.dynamic_index_in_dim(
        blk.reshape(N_SHARDS, 2, *blk.shape[1:]), core, axis=1, keepdims=False).reshape(-1, blk.shape[-1])
    from_sibling = sibling_swap([halves_of(blk, 1 - mc) for blk in blocks], "prereduce_swap")
    chip_sums = [add_arrays([halves_of(blk, mc), got], f"prereduce_{n}", blk.dtype).reshape(N_SHARDS, -1, blk.shape[-1])
                 for n, blk, got in zip(scattered, blocks, from_sibling)]
    dsilu_rows = jnp.broadcast_to(jnp.pad(dsilu[n_ex:n_ex + 1], ((0, PACK_ROWS - 1), (0, 0)))[None],
                                  (N_SHARDS, PACK_ROWS, D_MODEL))
    to_chips = [dsilu_rows] + chip_sums
    received = exchange(to_chips, False, CHIP_PEERS, "scatter_big_grads", by_chip=True, own=False)
    received = [lax.dynamic_update_index_in_dim(got, lax.dynamic_index_in_dim(sent, shard, 0, keepdims=False), shard, 0)
                for got, sent in zip(received, to_chips)]
    grads["c_ctx"] = c_ctx_grad(received[0], a["c_ctx"][None, :])
    half_sums = [sum_slots(p, tuple(range(N_SHARDS)), f"sum_{n}") for n, p in zip(scattered, received[1:])]
    late_lands = exchange_wait(late_sent["started"], half_sums[0], False, "scatter_late_wait")
    for n, land, sent in zip(LATE_WEIGHTS, late_lands, late_sent["blocks"]):
        land = lax.dynamic_update_index_in_dim(land, lax.dynamic_index_in_dim(sent, dev, 0, keepdims=False), dev, 0)
        half_sums.append(sum_slots(land, tuple(range(N_DEV)), f"sum_{n}"))
    scattered = scattered + LATE_WEIGHTS
    other_halves = sibling_swap(half_sums, "swap_halves")
    for n, mine, other in zip(scattered, half_sums, other_halves):
        rows = mine.shape[0]
        whole = jnp.zeros((2 * rows, mine.shape[1]), f32)
        whole = lax.dynamic_update_slice_in_dim(whole, mine, mc * rows, axis=0)
        grads[n] = lax.dynamic_update_slice_in_dim(whole, other, (1 - mc) * rows, axis=0)
    grads.update(zip(LAST_SHARDED, _unpack(grads.pop("small_shards"), shard_layout)))

    out_g, out_d, out_m, out_v = {}, {}, {}, {}
    for n in ("w_ada",) + COL_SHARDED + ROW_SHARDED:
        out_g[n] = grads[n].reshape(a[n].shape)
        two_d = lambda z: z.reshape(-1, z.shape[-1])
        d, m, v = adamw(two_d(a[n]), two_d(out_g[n]), two_d(a["m_" + n]), two_d(a["v_" + n]), f"adamw_{n}")
        out_d[n], out_m[n], out_v[n] = d.reshape(a[n].shape), m.reshape(a[n].shape), v.reshape(a[n].shape)
    rest = REPLICATED + LAST_SHARDED
    for n in rest:
        out_g[n] = grads[n].reshape(a[n].shape)
    packs = [_pack([src[n] for n in rest])[0] for src in
             ({n: a[n] for n in rest}, out_g, {n: a["m_" + n] for n in rest}, {n: a["v_" + n] for n in rest})]
    _, rest_layout = _pack([a[n] for n in rest])
    for dst, pack in zip((out_d, out_m, out_v), adamw(*packs, "adamw_small")):
        dst.update(zip(rest, _unpack(pack, rest_layout)))
    return (loss_total, grad_x, *[out_g[n] for n in W_NAMES], *[out_d[n] for n in W_NAMES],
            *[out_m[n] for n in W_NAMES], *[out_v[n] for n in W_NAMES])


def kernel(x, c, ctx, c_ctx, w_ada, b_ada, norm1_g, norm2_g, w_in, ret_log_decay, rwkv_shift_mu, rwkv_w0, rwkv_w_up, rwkv_a0, rwkv_a_up, rwkv_g_up, rwkv_k_k, rwkv_k_a, rwkv_r_k, rwkv_ln_w, rwkv_ln_b, w_out, w_ff1, b_ff1, w_ff2, b_ff2, final_g, loss_target, m_c_ctx, m_w_ada, m_b_ada, m_norm1_g, m_norm2_g, m_w_in, m_ret_log_decay, m_rwkv_shift_mu, m_rwkv_w0, m_rwkv_w_up, m_rwkv_a0, m_rwkv_a_up, m_rwkv_g_up, m_rwkv_k_k, m_rwkv_k_a, m_rwkv_r_k, m_rwkv_ln_w, m_rwkv_ln_b, m_w_out, m_w_ff1, m_b_ff1, m_w_ff2, m_b_ff2, m_final_g, v_c_ctx, v_w_ada, v_b_ada, v_norm1_g, v_norm2_g, v_w_in, v_ret_log_decay, v_rwkv_shift_mu, v_rwkv_w0, v_rwkv_w_up, v_rwkv_a0, v_rwkv_a_up, v_rwkv_g_up, v_rwkv_k_k, v_rwkv_k_a, v_rwkv_r_k, v_rwkv_ln_w, v_rwkv_ln_b, v_w_out, v_w_ff1, v_b_ff1, v_w_ff2, v_b_ff2, v_final_g):
    return _train_step(dict(locals()))
```

```python
import functools
import math

import jax
import jax.numpy as jnp
from jax import lax
from jax.experimental import pallas as pl
from jax.experimental.pallas import tpu as pltpu

f32 = jnp.float32
MXU_DTYPE = jnp.bfloat16

D_MODEL = 1024
RET_W = 512
RET_HEADS = 4
RET_DH = 128
RET_CHUNK = 128
RW_W = 512
RW_N = 64
DECAY_LORA = 64
AAA_LORA = 64
GATE_LORA = 128
LORA_W = DECAY_LORA + AAA_LORA + GATE_LORA
D_FF = 4096
RET_COLS = 4 * RET_W
SHIFT_COLS = 3 * RW_W + LORA_W
IN_COLS = RET_COLS + SHIFT_COLS
GRID_W = 64
ROPE_BASE = 10000.0
NORM_EPS = 1e-6
GN_EPS = 64e-5
W_DECAY_SCALE = math.exp(-0.5)
ADAM_LR, ADAM_B1, ADAM_B2, ADAM_EPS, ADAM_WD, ADAM_STEP = 0.001, 0.9, 0.999, 1e-08, 0.01, 10

TOK_TILE = 256
MATMUL_TILE = 1024
SCAN_CHUNK = 16
SCAN_UNROLL = SCAN_CHUNK
N_DEV = 8
V7X_VMEM_BYTES = 64 * 1024 * 1024
VMEM_LIMIT = V7X_VMEM_BYTES * 7 // 8


def _cparams(sem):
    return pltpu.CompilerParams(dimension_semantics=sem, vmem_limit_bytes=VMEM_LIMIT)


def _tile(n, cap):
    best = None
    for t in range(128, min(n, cap) + 1, 128):
        if n % t == 0:
            best = t
    return best if best is not None else n


def matmul(a, b, mode, name, out_dtype=f32, bias=None, finish=None):
    if mode == "nn":
        (m, k), (k2, n) = a.shape, b.shape
    elif mode == "nt":
        (m, k), (n, k2) = a.shape, b.shape
    else:
        (k, m), (k2, n) = a.shape, b.shape
    assert k == k2, (a.shape, b.shape, mode)
    tm, tn, tk = _tile(m, MATMUL_TILE), _tile(n, MATMUL_TILE), _tile(k, MATMUL_TILE)
    nk = k // tk
    dims = {"nn": ((1,), (0,)), "nt": ((1,), (1,)), "tn": ((0,), (0,))}[mode]

    def body(a_ref, b_ref, *rest):
        o_ref, acc_ref = rest[-2:]
        kk = pl.program_id(2)

        @pl.when(kk == 0)
        def _():
            acc_ref[...] = jnp.zeros_like(acc_ref)

        acc_ref[...] += lax.dot_general(a_ref[...].astype(MXU_DTYPE), b_ref[...].astype(MXU_DTYPE),
                                        (dims, ((), ())), preferred_element_type=f32)

        @pl.when(kk == nk - 1)
        def _():
            res = acc_ref[...]
            if bias is not None:
                res = res + rest[0][...]
            if finish is not None:
                res = finish(res)
            o_ref[...] = res.astype(o_ref.dtype)

    if mode == "nn":
        a_spec = pl.BlockSpec((tm, tk), lambda i, j, q: (i, q))
        b_spec = pl.BlockSpec((tk, tn), lambda i, j, q: (q, j))
    elif mode == "nt":
        a_spec = pl.BlockSpec((tm, tk), lambda i, j, q: (i, q))
        b_spec = pl.BlockSpec((tn, tk), lambda i, j, q: (j, q))
    else:
        a_spec = pl.BlockSpec((tk, tm), lambda i, j, q: (q, i))
        b_spec = pl.BlockSpec((tk, tn), lambda i, j, q: (q, j))
    extra_specs = [] if bias is None else [pl.BlockSpec((1, tn), lambda i, j, q: (0, j))]
    extra = [] if bias is None else [bias]
    return pl.pallas_call(
        body, grid=(m // tm, n // tn, nk), in_specs=[a_spec, b_spec] + extra_specs,
        out_specs=pl.BlockSpec((tm, tn), lambda i, j, q: (i, j)),
        out_shape=jax.ShapeDtypeStruct((m, n), out_dtype),
        scratch_shapes=[pltpu.VMEM((tm, tn), f32)],
        compiler_params=_cparams(("parallel", "parallel", "arbitrary")), name=name)(a, b, *extra)


class Tiled:
    def __init__(self, arr, w=None, cidx=0, toff=0):
        self.arr, self.w, self.cidx, self.toff = arr, (arr.shape[-1] if w is None else w), cidx, toff

    def spec(self):
        cidx, toff = self.cidx, self.toff
        return pl.BlockSpec((None, TOK_TILE, self.w), lambda b, i: (b, jnp.maximum(i + toff, 0), cidx))


class Seg:
    def __init__(self, arr, seg, first):
        self.arr, self.seg, self.first = arr, seg, first

    def spec(self):
        seg = self.seg
        return pl.BlockSpec((None, None, 1, self.arr.shape[-1]), lambda b, i: (b, seg(i), 0, 0))


class Glob:
    def __init__(self, arr):
        self.arr = arr

    def spec(self):
        return pl.BlockSpec(self.arr.shape, lambda b, i: (0,) * self.arr.ndim)


def ew_forward(fn, name, bsz, n_tiles, ins, outs):
    n_in = len(ins)

    def body(*refs):
        res = fn(*[r[...] for r in refs[:n_in]])
        for o_ref, o in zip(refs[n_in:], res):
            o_ref[...] = o.astype(o_ref.dtype)

    out_specs = [pl.BlockSpec((None, TOK_TILE, w), lambda b, i: (b, i, 0)) for w, _ in outs]
    out_shape = [jax.ShapeDtypeStruct((bsz, n_tiles * TOK_TILE, w), dt) for w, dt in outs]
    return pl.pallas_call(body, grid=(bsz, n_tiles), in_specs=[d.spec() for d in ins], out_specs=out_specs,
                          out_shape=out_shape, compiler_params=_cparams(("parallel", "parallel")), name=name)(
        *[d.arr for d in ins])


def ew_backward(fn, name, bsz, n_tiles, ins, cts, want, grad_dtypes=None, lead=0):
    n_in, n_ct = len(ins), len(cts)
    diff = [k for k in range(n_in) if want[k]]
    grad_dtypes = grad_dtypes or {}
    assert lead == 0 or not any(isinstance(ins[k], Seg) for k in diff)

    def body(*refs):
        b, i = pl.program_id(0), pl.program_id(1)
        g_refs = refs[n_in + n_ct:]

        def tile_grads():
            vals = [r[...] for r in refs[:n_in]]
            ct_vals = tuple(r[...].astype(f32) for r in refs[n_in:n_in + n_ct])

            def f(*dvals):
                full = list(vals)
                for k, v in zip(diff, dvals):
                    full[k] = v
                return tuple(fn(*full))

            _, vjp = jax.vjp(f, *[vals[k] for k in diff])
            grads = vjp(ct_vals)
            for k, g_ref, g in zip(diff, g_refs, grads):
                d = ins[k]
                if isinstance(d, Tiled):
                    g_ref[...] = g.astype(g_ref.dtype)
                else:
                    zero = d.first(i) if isinstance(d, Seg) else jnp.logical_and(b == 0, i == lead)

                    @pl.when(zero)
                    def _(g_ref=g_ref):
                        g_ref[...] = jnp.zeros_like(g_ref)

                    g_ref[...] += g

        if lead == 0:
            tile_grads()
        else:
            pl.when(i >= lead)(tile_grads)

            @pl.when(i < lead)
            def _():
                for k, g_ref in zip(diff, g_refs):
                    if isinstance(ins[k], Tiled):
                        g_ref[...] = jnp.zeros_like(g_ref)

    out_specs, out_shape = [], []
    for k in diff:
        d = ins[k]
        if isinstance(d, Tiled):
            out_specs.append(pl.BlockSpec((None, TOK_TILE, d.w), lambda b, i: (b, i, 0)))
            out_shape.append(jax.ShapeDtypeStruct((bsz, (n_tiles + lead) * TOK_TILE, d.w), grad_dtypes.get(k, f32)))
        else:
            out_specs.append(d.spec())
            out_shape.append(jax.ShapeDtypeStruct(d.arr.shape, f32))
    return pl.pallas_call(body, grid=(bsz, n_tiles + lead),
                          in_specs=[d.spec() for d in ins] + [c.spec() for c in cts],
                          out_specs=out_specs, out_shape=out_shape,
                          compiler_params=_cparams(("arbitrary", "arbitrary")), name=name)(
        *[d.arr for d in ins], *[c.arr for c in cts])


@jax.custom_vjp
def _mxu_dot(a, b):
    return jnp.dot(a.astype(MXU_DTYPE), b.astype(MXU_DTYPE), preferred_element_type=f32)


def _mxu_dot_fwd(a, b):
    return _mxu_dot(a, b), (a, b)


def _mxu_dot_bwd(res, ct):
    a, b = res
    ct = ct.astype(MXU_DTYPE)
    da = lax.dot_general(ct, b.astype(MXU_DTYPE), (((1,), (1,)), ((), ())), preferred_element_type=f32)
    db = lax.dot_general(a.astype(MXU_DTYPE), ct, (((0,), (0,)), ((), ())), preferred_element_type=f32)
    return da, db


_mxu_dot.defvjp(_mxu_dot_fwd, _mxu_dot_bwd)


def _split_dot_impl(x, ones_mat):
    hi = x.astype(MXU_DTYPE)
    lo = (x - hi.astype(f32)).astype(MXU_DTYPE)
    return jnp.dot(hi, ones_mat, preferred_element_type=f32) + jnp.dot(lo, ones_mat, preferred_element_type=f32)


@jax.custom_vjp
def _split_dot(x, ones_mat):
    return _split_dot_impl(x, ones_mat)


def _split_dot_fwd(x, ones_mat):
    return _split_dot_impl(x, ones_mat), ones_mat


def _split_dot_bwd(ones_mat, ct):
    return _split_dot_impl(ct, ones_mat), None


_split_dot.defvjp(_split_dot_fwd, _split_dot_bwd)


def _block_ones(n, group):
    idx = jnp.arange(n) // group
    return (idx[:, None] == idx[None, :]).astype(MXU_DTYPE)


def _rms(x, g):
    return x * lax.rsqrt(jnp.mean(x * x, axis=-1, keepdims=True) + NORM_EPS) * g


def fn_norm_mod(h, shift, scale, g):
    return (_rms(h, g) * (1.0 + scale) + shift,)


def fn_rwkv_prepare(ks, lora, w0_f, w0_b, a0_f, a0_b, w_up_f, w_up_b, a_up_f, a_up_b, g_up, k_k, k_a, ones64):
    kkr = ks * k_k
    kk = kkr * lax.rsqrt(_split_dot(kkr * kkr, ones64) + 1e-12)
    outs = [kk]
    th = jnp.tanh(lora)
    for w0, a0, w_up, a_up in ((w0_f, a0_f, w_up_f, a_up_f), (w0_b, a0_b, w_up_b, a_up_b)):
        w = jnp.exp(-W_DECAY_SCALE * jax.nn.sigmoid(w0 + _mxu_dot(th, w_up)))
        a = jax.nn.sigmoid(a0 + _mxu_dot(lora, a_up))
        kt = ks * (1.0 + (a - 1.0) * k_a)
        outs += [w, a * kk, kt]
    outs.append(_mxu_dot(jax.nn.sigmoid(lora), g_up))
    return tuple(outs)


def fn_merge(o_f, o_b, g_ret, y_f, y_b, r, kt_f, v, g_rw, r_k, ln_w, ln_b, ones64, ones128):
    o = o_f + o_b
    ret = o * lax.rsqrt(_split_dot(o * o, ones128) * (1.0 / RET_DH) + NORM_EPS) * (g_ret * jax.nn.sigmoid(g_ret))
    y = y_f + y_b
    mean = _split_dot(y, ones64) * (1.0 / RW_N)
    yc = y - mean
    var = _split_dot(yc * yc, ones64) * (1.0 / RW_N)
    y_n = yc * lax.rsqrt(var + GN_EPS) * ln_w + ln_b
    bonus = _split_dot(r * kt_f * r_k, ones64) * v
    return ret, (y_n + bonus) * g_rw


def fn_resid_norm_mod(x, mix, gate, shift, scale, g):
    h1 = x + gate * mix
    return h1, _rms(h1, g) * (1.0 + scale) + shift


def relu2(z):
    return jnp.square(jnp.maximum(z, 0.0))


def relu2_backward(act, dact, name):
    bsz, n_tok, width = act.shape

    def body(a_ref, d_ref, du_ref, db_ref):
        du = d_ref[...].astype(f32) * (2.0 * jnp.sqrt(a_ref[...].astype(f32)))
        du_ref[...] = du.astype(du_ref.dtype)

        @pl.when(jnp.logical_and(pl.program_id(0) == 0, pl.program_id(1) == 0))
        def _():
            db_ref[...] = jnp.zeros_like(db_ref)

        db_ref[...] += jnp.sum(du, axis=0, keepdims=True)

    tile = pl.BlockSpec((None, TOK_TILE, width), lambda b, i: (b, i, 0))
    row = pl.BlockSpec((1, width), lambda b, i: (0, 0))
    return pl.pallas_call(body, grid=(bsz, n_tok // TOK_TILE), in_specs=[tile, tile], out_specs=[tile, row],
                          out_shape=[jax.ShapeDtypeStruct(act.shape, MXU_DTYPE), jax.ShapeDtypeStruct((1, width), f32)],
                          compiler_params=_cparams(("arbitrary", "arbitrary")), name=name)(act, dact)


def fn_loss(h1, f, tgt, gate, b2, g):
    y = _rms(h1 + gate * (f + b2), g)
    err = jnp.square(y - tgt)
    return 0.5 * jnp.sum(jnp.mean(err, axis=-1, keepdims=True), axis=0, keepdims=True)


def loss_and_grads(h1, f, tgt, gate, b2, g, bsz, n_tiles):
    def body(h1_ref, f_ref, t_ref, gate_ref, b2_ref, g_ref, loss_ref, dh1_ref, df_ref, dgate_ref, db2_ref, dg_ref):
        b, i = pl.program_id(0), pl.program_id(1)
        tgt_v = t_ref[...]
        loss, vjp = jax.vjp(lambda a, c, e, p, q: fn_loss(a, c, tgt_v, e, p, q),
                            h1_ref[...], f_ref[...], gate_ref[...], b2_ref[...], g_ref[...])
        dh1, df, dgate, db2, dg = vjp(jnp.ones((1, 1), f32))
        dh1_ref[...] = dh1
        df_ref[...] = df.astype(df_ref.dtype)

        @pl.when(i == 0)
        def _():
            dgate_ref[...] = jnp.zeros_like(dgate_ref)

        @pl.when(jnp.logical_and(b == 0, i == 0))
        def _():
            loss_ref[...] = jnp.zeros_like(loss_ref)
            db2_ref[...] = jnp.zeros_like(db2_ref)
            dg_ref[...] = jnp.zeros_like(dg_ref)

        dgate_ref[...] += dgate
        db2_ref[...] += db2
        dg_ref[...] += dg
        loss_ref[...] += jnp.broadcast_to(loss, loss_ref.shape)

    tile = pl.BlockSpec((None, TOK_TILE, D_MODEL), lambda b, i: (b, i, 0))
    row = pl.BlockSpec((1, D_MODEL), lambda b, i: (0, 0))
    seg = pl.BlockSpec((None, None, 1, D_MODEL), lambda b, i: (b, 0, 0, 0))
    t_tok = n_tiles * TOK_TILE
    return pl.pallas_call(
        body, grid=(bsz, n_tiles), in_specs=[tile, tile, tile, seg, row, row],
        out_specs=[pl.BlockSpec((1, 128), lambda b, i: (0, 0)), tile, tile, seg, row, row],
        out_shape=[jax.ShapeDtypeStruct((1, 128), f32), jax.ShapeDtypeStruct((bsz, t_tok, D_MODEL), f32),
                   jax.ShapeDtypeStruct((bsz, t_tok, D_MODEL), MXU_DTYPE),
                   jax.ShapeDtypeStruct((bsz, 1, 1, D_MODEL), f32),
                   jax.ShapeDtypeStruct((1, D_MODEL), f32), jax.ShapeDtypeStruct((1, D_MODEL), f32)],
        compiler_params=_cparams(("arbitrary", "arbitrary")), name="loss_and_grads")(h1, f, tgt, gate, b2, g)


SHIFT_BLOCK = SHIFT_COLS
HALO_ROWS = 8


def _shift_specs(n_tok, col0):
    per_tile = TOK_TILE // HALO_ROWS
    last = n_tok // HALO_ROWS - 1
    tile = pl.BlockSpec((None, TOK_TILE, SHIFT_BLOCK), lambda j, b, i: (b, i, col0 + j))
    prev = pl.BlockSpec((None, HALO_ROWS, SHIFT_BLOCK),
                        lambda j, b, i: (b, jnp.maximum(i * per_tile - 1, 0), col0 + j))
    nxt = pl.BlockSpec((None, HALO_ROWS, SHIFT_BLOCK),
                       lambda j, b, i: (b, jnp.minimum((i + 1) * per_tile, last), col0 + j))
    return tile, prev, nxt


def _shifted(p, prev_ref, next_ref, is_first, is_last):
    row = lax.broadcasted_iota(jnp.int32, p.shape, 0)
    prev_row = jnp.where(is_first, 0.0, prev_ref[HALO_ROWS - 1:HALO_ROWS, :].astype(f32))
    next_row = jnp.where(is_last, 0.0, next_ref[0:1, :].astype(f32))
    prev = jnp.where(row == 0, prev_row, pltpu.roll(p, 1, axis=0))
    nxt = jnp.where(row == TOK_TILE - 1, next_row, pltpu.roll(p, TOK_TILE - 1, axis=0))
    return prev, nxt


def token_shift(px, mu, seg_first, seg_last):
    bsz, n_tok, _ = px.shape
    n_tiles = n_tok // TOK_TILE

    def body(p_ref, prev_ref, next_ref, mu_ref, o_ref):
        i = pl.program_id(2)
        p = p_ref[...]
        prev, nxt = _shifted(p, prev_ref, next_ref, seg_first(i), seg_last(i))
        o_ref[...] = p + mu_ref[0:1, :] * (prev - p) + mu_ref[1:2, :] * (nxt - p)

    tile, prev, nxt = _shift_specs(n_tok, 0)
    return pl.pallas_call(
        body, grid=(SHIFT_COLS // SHIFT_BLOCK, bsz, n_tiles),
        in_specs=[tile, prev, nxt, pl.BlockSpec((2, SHIFT_BLOCK), lambda j, b, i: (0, j))],
        out_specs=pl.BlockSpec((None, TOK_TILE, SHIFT_BLOCK), lambda j, b, i: (b, i, j)),
        out_shape=jax.ShapeDtypeStruct((bsz, n_tok, SHIFT_COLS), f32),
        compiler_params=_cparams(("parallel", "parallel", "parallel")), name="token_shift")(px, px, px, mu)


def token_shift_bwd(dps, px, mu, seg_first, seg_last):
    bsz, n_tok, _ = px.shape
    n_tiles = n_tok // TOK_TILE

    def body(d_ref, dprev_ref, dnext_ref, p_ref, prev_ref, next_ref, mu_ref, dp_ref, dmu_ref):
        b, i = pl.program_id(1), pl.program_id(2)
        first, last = seg_first(i), seg_last(i)
        d, p = d_ref[...], p_ref[...]
        d_prev, d_next = _shifted(d, dprev_ref, dnext_ref, first, last)
        p_prev, p_next = _shifted(p, prev_ref, next_ref, first, last)
        mu0, mu1 = mu_ref[0:1, :], mu_ref[1:2, :]
        dp_ref[...] = (d + mu0 * (d_next - d) + mu1 * (d_prev - d)).astype(dp_ref.dtype)

        @pl.when(jnp.logical_and(b == 0, i == 0))
        def _():
            dmu_ref[...] = jnp.zeros_like(dmu_ref)

        dmu_ref[0:1, :] += jnp.sum(d * (p_prev - p), axis=0, keepdims=True)
        dmu_ref[1:2, :] += jnp.sum(d * (p_next - p), axis=0, keepdims=True)

    dtile, dprev, dnext = _shift_specs(n_tok, 0)
    tile, prev, nxt = _shift_specs(n_tok, 0)
    mu_spec = pl.BlockSpec((2, SHIFT_BLOCK), lambda j, b, i: (0, j))
    return pl.pallas_call(
        body, grid=(SHIFT_COLS // SHIFT_BLOCK, bsz, n_tiles),
        in_specs=[dtile, dprev, dnext, tile, prev, nxt, mu_spec],
        out_specs=[pl.BlockSpec((None, TOK_TILE, SHIFT_BLOCK), lambda j, b, i: (b, i, j)), mu_spec],
        out_shape=[jax.ShapeDtypeStruct((bsz, n_tok, SHIFT_COLS), MXU_DTYPE),
                   jax.ShapeDtypeStruct((2, SHIFT_COLS), f32)],
        compiler_params=_cparams(("arbitrary", "arbitrary", "arbitrary")), name="token_shift_bwd")(
        dps, dps, dps, px, px, px, mu)


def _dg(a, b, ca, cb):
    return lax.dot_general(a.astype(MXU_DTYPE), b.astype(MXU_DTYPE), (((ca,), (cb,)), ((), ())),
                           preferred_element_type=f32)


@jax.custom_vjp
def _mm_nt(a, b):
    return _dg(a, b, 1, 1)


_mm_nt.defvjp(lambda a, b: (_dg(a, b, 1, 1), (a, b)),
              lambda res, ct: (_dg(ct, res[1], 1, 0), _dg(ct, res[0], 0, 0)))


@jax.custom_vjp
def _mm_tn(a, b):
    return _dg(a, b, 0, 0)


_mm_tn.defvjp(lambda a, b: (_dg(a, b, 0, 0), (a, b)),
              lambda res, ct: (_dg(res[1], ct, 1, 1), _dg(res[0], ct, 1, 0)))


ROTARY_PAIR = RET_DH // 4


def _swap_pairs_impl(t):
    lane = lax.broadcasted_iota(jnp.int32, t.shape, 1)
    return jnp.where(lane % (2 * ROTARY_PAIR) < ROTARY_PAIR, pltpu.roll(t, RET_DH - ROTARY_PAIR, axis=1),
                     pltpu.roll(t, ROTARY_PAIR, axis=1))


@jax.custom_vjp
def _swap_pairs(t):
    return _swap_pairs_impl(t)


_swap_pairs.defvjp(lambda t: (_swap_pairs_impl(t), None), lambda _, ct: (_swap_pairs_impl(ct),))


def _ret_chunk(state, q_raw, k_raw, v, cos, sin, ld_row, head, reverse):
    c = RET_CHUNK
    lane = lax.broadcasted_iota(jnp.int32, ld_row.shape, 1)
    lg = -jnp.exp(jnp.sum(jnp.where(lane == head, ld_row, 0.0), axis=-1, keepdims=True))
    rot = lambda t: t * cos + _swap_pairs(t) * sin
    q = rot(q_raw)
    k = rot(k_raw) * (RET_DH ** -0.5)
    ti = lax.broadcasted_iota(jnp.int32, (c, 1), 0).astype(f32)
    tj = lax.broadcasted_iota(jnp.int32, (1, c), 1).astype(f32)
    if not reverse:
        dist, mask, q_exp, k_exp = ti - tj, (ti - tj) >= 0, ti + 1.0, c - 1.0 - ti
    else:
        dist, mask, q_exp, k_exp = tj - ti, (tj - ti) > 0, c - ti, ti
    decay = jnp.where(mask, jnp.exp(lg * jnp.maximum(dist, 0.0)), 0.0)
    scores = _mm_nt(q, k) * decay
    out = _mxu_dot(scores, v) + _mxu_dot(q * jnp.exp(lg * q_exp), state)
    new_state = state * jnp.exp(lg * c) + _mm_tn(k * jnp.exp(lg * k_exp), v)
    return out, new_state


def _ret_specs(bsz, order):
    tok = lambda col=0: pl.BlockSpec((bsz, RET_CHUNK, RET_W), lambda i: (0, order(i), col))
    tab = pl.BlockSpec((RET_CHUNK, RET_DH), lambda i: (order(i), 0))
    ld = pl.BlockSpec((1, RET_DH), lambda i: (0, 0))
    return tok, tab, ld


def retention_fwd(px, cos, sin, ld_row, order, reverse, name):
    bsz, n_tok, _ = px.shape
    n_ch = n_tok // RET_CHUNK

    def body(q_ref, k_ref, v_ref, cos_ref, sin_ref, ld_ref, o_ref, sv_ref, st_ref):
        @pl.when(pl.program_id(0) == 0)
        def _():
            st_ref[...] = jnp.zeros_like(st_ref)

        for b in range(bsz):
            for h in range(RET_HEADS):
                sl = slice(h * RET_DH, (h + 1) * RET_DH)
                s = st_ref[b, h]
                sv_ref[b, h] = s
                o, s_new = _ret_chunk(s, q_ref[b, :, sl], k_ref[b, :, sl], v_ref[b, :, sl], cos_ref[...], sin_ref[...],
                                      ld_ref[...], h, reverse)
                o_ref[b, :, sl] = o
                st_ref[b, h] = s_new

    tok, tab, ld = _ret_specs(bsz, order)
    return pl.pallas_call(
        body, grid=(n_ch,), in_specs=[tok(0), tok(1), tok(2), tab, tab, ld],
        out_specs=[tok(), pl.BlockSpec((bsz, None, RET_HEADS, RET_DH, RET_DH), lambda i: (0, i, 0, 0, 0))],
        out_shape=[jax.ShapeDtypeStruct((bsz, n_tok, RET_W), f32),
                   jax.ShapeDtypeStruct((bsz, n_ch, RET_HEADS, RET_DH, RET_DH), f32)],
        scratch_shapes=[pltpu.VMEM((bsz, RET_HEADS, RET_DH, RET_DH), f32)],
        compiler_params=_cparams(("arbitrary",)), name=name)(px, px, px, cos, sin, ld_row)


def retention_bwd(do, px, states, cos, sin, ld_row, order, reverse, name):
    bsz, n_tok, _ = px.shape
    n_ch = n_tok // RET_CHUNK
    back = lambda i: order(n_ch - 1 - i)

    def body(do_ref, q_ref, k_ref, v_ref, sv_ref, cos_ref, sin_ref, ld_ref,
             dq_ref, dk_ref, dv_ref, dld_ref, dst_ref):
        @pl.when(pl.program_id(0) == 0)
        def _():
            dst_ref[...] = jnp.zeros_like(dst_ref)
            dld_ref[...] = jnp.zeros_like(dld_ref)

        cos_v, sin_v = cos_ref[...], sin_ref[...]
        for b in range(bsz):
            for h in range(RET_HEADS):
                sl = slice(h * RET_DH, (h + 1) * RET_DH)
                f = lambda s, q, k, v, ld, h=h: _ret_chunk(s, q, k, v, cos_v, sin_v, ld, h, reverse)
                _, vjp = jax.vjp(f, sv_ref[b, h], q_ref[b, :, sl], k_ref[b, :, sl], v_ref[b, :, sl], ld_ref[...])
                ds, dq, dk, dv, dld = vjp((do_ref[b, :, sl], dst_ref[b, h]))
                dst_ref[b, h] = ds
                dq_ref[b, :, sl] = dq
                dk_ref[b, :, sl] = dk
                dv_ref[b, :, sl] = dv
                dld_ref[...] += dld

    tok, tab, ld = _ret_specs(bsz, back)
    return pl.pallas_call(
        body, grid=(n_ch,),
        in_specs=[tok(), tok(0), tok(1), tok(2),
                  pl.BlockSpec((bsz, None, RET_HEADS, RET_DH, RET_DH), lambda i: (0, n_ch - 1 - i, 0, 0, 0)),
                  tab, tab, ld],
        out_specs=[tok(), tok(), tok(), ld],
        out_shape=[jax.ShapeDtypeStruct((bsz, n_tok, RET_W), f32)] * 3 + [jax.ShapeDtypeStruct((1, RET_DH), f32)],
        scratch_shapes=[pltpu.VMEM((bsz, RET_HEADS, RET_DH, RET_DH), f32)],
        compiler_params=_cparams(("arbitrary",)), name=name)(
        do, px, px, px, states, cos, sin, ld_row)


HALF_W = RW_W // 2


def _head_sum(x, ones):
    xm = x.astype(MXU_DTYPE)
    return jnp.concatenate([jnp.dot(xm[:, :HALF_W], ones, preferred_element_type=f32),
                            jnp.dot(xm[:, HALF_W:], ones, preferred_element_type=f32)], axis=1)


def _stack(parts):
    return jnp.concatenate(parts, axis=0)


def _row(ref, b, t):
    return ref[b, pl.ds(t, 1), :]


SCAN_DIRS = ((False, True), (True, False))
RW_HEADS = RW_W // RW_N
HEAD_ROWS_PAD = 16


def _head_rows(row, mask):
    return jnp.broadcast_to(row, mask.shape) * mask


def _outer(per_value, row, mask_pad):
    return lax.dot_general(per_value.astype(MXU_DTYPE), _head_rows(row, mask_pad).astype(MXU_DTYPE),
                           (((0,), (0,)), ((), ())), preferred_element_type=f32)


def _read(states, rows, mask):
    lhs = _stack([_head_rows(r, mask) for r in rows])
    return lax.dot_general(lhs.astype(MXU_DTYPE), _stack(states).astype(MXU_DTYPE), (((1,), (1,)), ((), ())),
                           preferred_element_type=f32)


def _row_from_heads(per_value, state, mask_pad):
    full = jnp.dot(per_value.astype(MXU_DTYPE), state.astype(MXU_DTYPE), preferred_element_type=f32)
    return jnp.sum(full * mask_pad, axis=0, keepdims=True)


def _scan_specs(bsz, order):
    rows = lambda col=0: pl.BlockSpec((bsz, SCAN_CHUNK, RW_W), lambda i: (0, order(i), col))
    per_value = pl.BlockSpec((bsz, SCAN_CHUNK, HEAD_ROWS_PAD, RW_N), lambda i: (0, order(i), 0, 0))
    return rows, per_value


def _mxu_operands(states):
    return [s.astype(MXU_DTYPE) for s in states]


def _removed(states_m, kk_t, ones, bsz):
    removed = _head_sum(_stack([states_m[b] * kk_t[b].astype(MXU_DTYPE) for b in range(bsz)]), ones)
    return [removed[b * RW_N:(b + 1) * RW_N] for b in range(bsz)]


def _advance(sp, rem, w_t, b_t, vk, bsz):
    return [sp[b] * w_t[b] - rem[b] * b_t[b] + vk[b] for b in range(bsz)]


def heads_to_rows(a):
    b, t, _ = a.shape
    return jnp.pad(a.astype(MXU_DTYPE).reshape(b, t, RW_HEADS, RW_N),
                   ((0, 0), (0, 0), (0, HEAD_ROWS_PAD - RW_HEADS), (0, 0)))


def _blocks_to_rows(raw_ref, first, row_ref, bsz):
    steps = pl.ds(first, SCAN_CHUNK)
    for b in range(bsz):
        for h in range(RW_HEADS):
            row_ref[b, :, h * RW_N:(h + 1) * RW_N] = raw_ref[steps, RW_HEADS * b + h, RW_N * b:RW_N * (b + 1)]


N_ROWS_FWD = 5
N_ROWS_BWD = 5


def _scan_consts(bsz):
    head = (jnp.arange(RW_W)[None, :] // RW_N == jnp.arange(RW_HEADS)[:, None]).astype(f32)
    return head, jnp.pad(head, ((0, HEAD_ROWS_PAD - RW_HEADS), (0, 0))), _block_ones(HALF_W, RW_N)


def _const_specs(consts):
    return [pl.BlockSpec(c.shape, lambda i: (0, 0)) for c in consts]


def rwkv_scan_fwd(rows_in, v_heads, orders, name):
    bsz, n_tok, _ = rows_in[0][0][0].shape
    n_ch = n_tok // SCAN_CHUNK
    rng = range(bsz)
    consts = _scan_consts(bsz)

    def body(*refs):
        rows = [refs[:N_ROWS_FWD], refs[N_ROWS_FWD:2 * N_ROWS_FWD]]
        v0, v1, head_ref, pad_ref, ones_ref, y0, y1, cs0, cs1, s0, s1, late_ref, raw_ref = refs[2 * N_ROWS_FWD:]
        v_refs, y_refs, cs_refs, s_refs = (v0, v1), (y0, y1), (cs0, cs1), (s0, s1)
        head_v, pad_v, ones_v = head_ref[...], pad_ref[...], ones_ref[...]
        for d in range(2):
            @pl.when(pl.program_id(0) == 0)
            def _(d=d):
                s_refs[d][...] = jnp.zeros_like(s_refs[d])

            cs_refs[d][...] = s_refs[d][...]

        def step(j, carry):
            ts = [SCAN_CHUNK - 1 - j if reverse else j for reverse, _ in SCAN_DIRS]
            sps = [[s_refs[d][b] for b in rng] for d in range(2)]
            sps_m = [_mxu_operands(sps[d]) for d in range(2)]
            rems = [_removed(sps_m[d], [_row(rows[d][1], b, ts[d]) for b in rng], ones_v, bsz) for d in range(2)]
            vks = [[_outer(v_refs[d][b, ts[d]], _row(rows[d][4], b, ts[d]), pad_v) for b in rng] for d in range(2)]
            for d, (reverse, inclusive) in enumerate(SCAN_DIRS):
                r_ref = rows[d][0]
                if inclusive:
                    before = jnp.maximum(j - 1, 0)
                    late_ref[j] = _read(sps_m[d], [_row(r_ref, b, before) for b in rng], head_v)
                else:
                    raw_ref[ts[d]] = _read(sps_m[d], [_row(r_ref, b, ts[d]) for b in rng], head_v)
            for d in range(2):
                new = _advance(sps[d], rems[d], [_row(rows[d][2], b, ts[d]) for b in rng],
                               [_row(rows[d][3], b, ts[d]) for b in rng], vks[d], bsz)
                for b in rng:
                    s_refs[d][b] = new[b]
            return carry

        lax.fori_loop(0, SCAN_CHUNK, step, 0, unroll=SCAN_UNROLL)
        for d, (reverse, inclusive) in enumerate(SCAN_DIRS):
            if inclusive:
                assert not reverse
                last = SCAN_CHUNK - 1
                late_ref[SCAN_CHUNK] = _read(_mxu_operands([s_refs[d][b] for b in rng]),
                                             [rows[d][0][b, last:last + 1, :] for b in rng], head_v)
                _blocks_to_rows(late_ref, 1, y_refs[d], bsz)
            else:
                _blocks_to_rows(raw_ref, 0, y_refs[d], bsz)

    specs = [_scan_specs(bsz, orders[d]) for d in range(2)]
    state = pltpu.VMEM((bsz, RW_N, RW_W), f32)
    late = pltpu.VMEM((SCAN_CHUNK + 1, RW_HEADS * bsz, RW_N * bsz), f32)
    raw = pltpu.VMEM((SCAN_CHUNK, RW_HEADS * bsz, RW_N * bsz), f32)
    start_spec = pl.BlockSpec((None, bsz, RW_N, RW_W), lambda i: (i, 0, 0, 0))
    return pl.pallas_call(
        body, grid=(n_ch,),
        in_specs=[specs[d][0](col) for d in range(2) for _, col in rows_in[d]] + [specs[0][1], specs[1][1]]
        + _const_specs(consts),
        out_specs=[specs[0][0](), specs[1][0](), start_spec, start_spec],
        out_shape=[jax.ShapeDtypeStruct((bsz, n_tok, RW_W), f32)] * 2
        + [jax.ShapeDtypeStruct((n_ch, bsz, RW_N, RW_W), f32)] * 2,
        scratch_shapes=[state, state, late, raw],
        compiler_params=_cparams(("arbitrary",)), name=name)(
        *[a for d in range(2) for a, _ in rows_in[d]], v_heads, v_heads, *consts)


def rwkv_scan_bwd(rows_in, v_heads, dy_heads, starts, orders, name):
    bsz, n_tok, _ = rows_in[0][0][0].shape
    n_ch = n_tok // SCAN_CHUNK
    backs = [functools.partial(lambda i, order: order(n_ch - 1 - i), order=orders[d]) for d in range(2)]
    rng = range(bsz)
    consts = _scan_consts(bsz)
    n_out, n_scr = 6, 8

    def body(*refs):
        rows = [refs[:N_ROWS_BWD], refs[N_ROWS_BWD:2 * N_ROWS_BWD]]
        rest = refs[2 * N_ROWS_BWD:]
        v_refs, dy_refs, cs_refs, (head_ref, pad_ref, ones_ref) = rest[0:2], rest[2:4], rest[4:6], rest[6:9]
        outs = [rest[9:9 + n_out], rest[9 + n_out:9 + 2 * n_out]]
        scr = [rest[9 + 2 * n_out:9 + 2 * n_out + n_scr], rest[9 + 2 * n_out + n_scr:]]
        head_v, pad_v, ones_v = head_ref[...], pad_ref[...], ones_ref[...]
        for d in range(2):
            s_ref, ds_ref = scr[d][:2]

            @pl.when(pl.program_id(0) == 0)
            def _(ds_ref=ds_ref):
                ds_ref[...] = jnp.zeros_like(ds_ref)

            s_ref[...] = cs_refs[d][...]

        def fstep(j, carry):
            ts = [SCAN_CHUNK - 1 - j if reverse else j for reverse, _ in SCAN_DIRS]
            sps = [[scr[d][0][b] for b in rng] for d in range(2)]
            rems = [_removed(_mxu_operands(sps[d]), [_row(rows[d][1], b, ts[d]) for b in rng], ones_v, bsz)
                    for d in range(2)]
            vks = [[_outer(v_refs[d][b, ts[d]], _row(rows[d][4], b, ts[d]), pad_v) for b in rng] for d in range(2)]
            for d in range(2):
                s_ref, _, hist_ref, rem_ref = scr[d][:4]
                new = _advance(sps[d], rems[d], [_row(rows[d][2], b, ts[d]) for b in rng],
                               [_row(rows[d][3], b, ts[d]) for b in rng], vks[d], bsz)
                for b in rng:
                    hist_ref[ts[d], b] = sps[d][b]
                    rem_ref[ts[d], b] = rems[d][b]
                    s_ref[b] = new[b]
            return carry

        lax.fori_loop(0, SCAN_CHUNK, fstep, 0, unroll=SCAN_UNROLL)

        def step_of(j, reverse):
            return j if reverse else SCAN_CHUNK - 1 - j

        for d, (reverse, _) in enumerate(SCAN_DIRS):
            t0 = step_of(0, reverse)
            for b in rng:
                scr[d][6][b] = _outer(dy_refs[d][b, t0], rows[d][0][b, t0:t0 + 1, :], pad_v)

        def bstep(j, carry):
            ts = [step_of(j, reverse) for reverse, _ in SCAN_DIRS]
            reads = [[scr[d][6][b] for b in rng] for d in range(2)]
            dss = []
            for d, (_, inclusive) in enumerate(SCAN_DIRS):
                ds = [scr[d][1][b] for b in rng]
                dss.append([ds[b] + reads[d][b] for b in rng] if inclusive else ds)
            dss_m = [_mxu_operands(dss[d]) for d in range(2)]
            drems = [_removed(dss_m[d], [-_row(rows[d][3], b, ts[d]) for b in rng], ones_v, bsz) for d in range(2)]
            for d, (reverse, _) in enumerate(SCAN_DIRS):
                t_next = step_of(jnp.minimum(j + 1, SCAN_CHUNK - 1), reverse)
                for b in rng:
                    scr[d][6][b] = _outer(dy_refs[d][b, t_next], _row(rows[d][0], b, t_next), pad_v)
                scr[d][7][ts[d]] = _read(dss_m[d], [_row(rows[d][4], b, ts[d]) for b in rng], head_v)
            for d, (_, inclusive) in enumerate(SCAN_DIRS):
                _, kk_ref, w_ref, _, _ = rows[d]
                _, ds_ref, _, _, dsh_ref, drem_ref = scr[d][:6]
                for b in rng:
                    dsh_ref[ts[d], b] = dss[d][b]
                    drem_ref[ts[d], b] = drems[d][b]
                    dsp = dss[d][b] * _row(w_ref, b, ts[d]) + drems[d][b] * _row(kk_ref, b, ts[d])
                    ds_ref[b] = dsp if inclusive else dsp + reads[d][b]
            return carry

        lax.fori_loop(0, SCAN_CHUNK, bstep, 0, unroll=SCAN_UNROLL)

        rsum = lambda z: jnp.sum(z, axis=0, keepdims=True)
        for d, (reverse, inclusive) in enumerate(SCAN_DIRS):
            dr_ref, dkk_ref, dw_ref, db_ref, dkt_ref, dv_ref = outs[d]
            s_ref, _, hist_ref, rem_ref, dsh_ref, drem_ref, _, dv_raw_ref = scr[d]
            _blocks_to_rows(dv_raw_ref, 0, dv_ref, bsz)
            for t in range(SCAN_CHUNK):
                ts = slice(t, t + 1)
                after = t - 1 if reverse else t + 1
                for b in rng:
                    sp, ds = hist_ref[t, b], dsh_ref[t, b]
                    if not inclusive:
                        seen = sp
                    else:
                        seen = hist_ref[after, b] if 0 <= after < SCAN_CHUNK else s_ref[b]
                    dr_ref[b, ts, :] = _row_from_heads(dy_refs[d][b, t], seen, pad_v)
                    dkt_ref[b, ts, :] = _row_from_heads(v_refs[d][b, t], ds, pad_v)
                    dw_ref[b, ts, :] = rsum(ds * sp)
                    db_ref[b, ts, :] = -rsum(ds * rem_ref[t, b])
                    dkk_ref[b, ts, :] = rsum(sp * drem_ref[t, b])

    specs = [_scan_specs(bsz, backs[d]) for d in range(2)]
    hist = pltpu.VMEM((SCAN_CHUNK, bsz, RW_N, RW_W), f32)
    state = pltpu.VMEM((bsz, RW_N, RW_W), f32)
    start_spec = pl.BlockSpec((None, bsz, RW_N, RW_W), lambda i: (n_ch - 1 - i, 0, 0, 0))
    raw = pltpu.VMEM((SCAN_CHUNK, RW_HEADS * bsz, RW_N * bsz), f32)
    return pl.pallas_call(
        body, grid=(n_ch,),
        in_specs=[specs[d][0](col) for d in range(2) for _, col in rows_in[d]]
        + [specs[0][1], specs[1][1]] * 2 + [start_spec, start_spec] + _const_specs(consts),
        out_specs=[specs[d][0]() for d in range(2) for _ in range(n_out)],
        out_shape=[jax.ShapeDtypeStruct((bsz, n_tok, RW_W), f32)] * (2 * n_out),
        scratch_shapes=[state, state, hist, hist, hist, hist, state, raw] * 2,
        compiler_params=_cparams(("arbitrary",)), name=name)(
        *[a for d in range(2) for a, _ in rows_in[d]], v_heads, v_heads, dy_heads, dy_heads, *starts, *consts)


MOD_NAMES = ("shift1", "scale1", "gate1", "shift2", "scale2", "gate2")


def _rope_tables(t_ctx, t_x):
    quarter = RET_DH // 4
    pos = jnp.arange(t_x)
    inv = jnp.power(ROPE_BASE, -jnp.arange(0, 2 * quarter, 2, dtype=f32) / (2 * quarter))
    ang_r = (pos // GRID_W).astype(f32)[:, None] * inv[None, :]
    ang_c = (pos % GRID_W).astype(f32)[:, None] * inv[None, :]
    cos = jnp.concatenate([jnp.cos(ang_r)] * 2 + [jnp.cos(ang_c)] * 2, axis=1)
    sin = jnp.concatenate([-jnp.sin(ang_r), jnp.sin(ang_r), -jnp.sin(ang_c), jnp.sin(ang_c)], axis=1)
    cos = jnp.concatenate([jnp.ones((t_ctx, RET_DH), f32), cos], axis=0)
    sin = jnp.concatenate([jnp.zeros((t_ctx, RET_DH), f32), sin], axis=0)
    return cos, sin


def _pad_rows(w, lo, total):
    return jnp.pad(w, ((lo, total - lo - w.shape[0]), (0, 0)))


LATE_WEIGHTS = ("w_out", "w_ff1", "w_ff2")


def layer_step(x, ctx, tgt, mod_x, mod_ctx, wt, late_weights=None, early_grads=None):
    bsz, t_x, _ = x.shape
    t_c = ctx.shape[1]
    t_all = t_c + t_x
    n_ct, n_xt = t_c // TOK_TILE, t_x // TOK_TILE
    n_t = n_ct + n_xt
    assert t_c % TOK_TILE == 0 and t_x % TOK_TILE == 0 and t_c % RET_CHUNK == 0

    seg = lambda i: (i >= n_ct).astype(jnp.int32)
    seg_first = lambda i: jnp.logical_or(i == 0, i == n_ct)
    seg_last = lambda i: jnp.logical_or(i == n_ct - 1, i == n_t - 1)
    mod_all = {n: jnp.stack([jnp.broadcast_to(mod_ctx[k], (bsz, D_MODEL)), mod_x[:, k]], axis=1)[:, :, None, :]
               for k, n in enumerate(MOD_NAMES)}
    mod_lat = {n: mod_x[:, k][:, None, None, :] for k, n in enumerate(MOD_NAMES)}
    both = lambda n: Seg(mod_all[n], seg, seg_first)
    lat = lambda n: Seg(mod_lat[n], lambda i: 0, lambda i: i == 0)
    flat = lambda a: a.reshape(-1, a.shape[-1])

    def chunk_orders(n_ctx_chunks, n_chunks):
        fwd = lambda i: i
        bwd = lambda i: jnp.where(i < n_ctx_chunks, n_ctx_chunks - 1 - i, n_chunks + n_ctx_chunks - 1 - i)
        return fwd, bwd

    ones64, ones128 = _block_ones(RW_W, RW_N), _block_ones(RET_W, RET_DH)
    cos, sin = _rope_tables(t_c, t_x)
    ld_rows = [jnp.pad(wt["ret_log_decay"][d][None, :], ((0, 0), (0, RET_DH - RET_HEADS))) for d in range(2)]
    w_up_pad = [_pad_rows(wt["rwkv_w_up"][d], 0, LORA_W) for d in range(2)]
    a_up_pad = [_pad_rows(wt["rwkv_a_up"][d], DECAY_LORA, LORA_W) for d in range(2)]
    g_up_pad = _pad_rows(wt["rwkv_g_up"], DECAY_LORA + AAA_LORA, LORA_W)
    row = lambda a, d: a[d][None, :]

    h = jnp.concatenate([ctx, x], axis=1)
    norm1_ins = lambda: [Tiled(h), both("shift1"), both("scale1"), Glob(wt["norm1_g"])]
    (n1,) = ew_forward(fn_norm_mod, "norm1", bsz, n_t, norm1_ins(), [(D_MODEL, MXU_DTYPE)])
    px = matmul(flat(n1), wt["w_in"], "nn", "proj_in").reshape(bsz, t_all, IN_COLS)
    px_rw = px[..., RET_COLS:]
    ps = token_shift(px_rw, wt["rwkv_shift_mu"], seg_first, seg_last)

    def prep_ins(toff=0):
        return [Tiled(ps, RW_W, 1), Tiled(ps, LORA_W, 3 * RW_W // LORA_W),
                Glob(row(wt["rwkv_w0"], 0)), Glob(row(wt["rwkv_w0"], 1)),
                Glob(row(wt["rwkv_a0"], 0)), Glob(row(wt["rwkv_a0"], 1)),
                Glob(w_up_pad[0]), Glob(w_up_pad[1]), Glob(a_up_pad[0]), Glob(a_up_pad[1]), Glob(g_up_pad),
                Glob(wt["rwkv_k_k"]), Glob(wt["rwkv_k_a"]), Glob(ones64)]

    kk, w_f, b_f, kt_f, w_b, b_b, kt_b, g_rw = ew_forward(fn_rwkv_prepare, "rwkv_prepare", bsz, n_t, prep_ins(),
                                                           [(RW_W, f32)] * 8)
    rw_order = chunk_orders(t_c // SCAN_CHUNK, t_all // SCAN_CHUNK)
    ret_order = chunk_orders(t_c // RET_CHUNK, t_all // RET_CHUNK)
    scan_rows = [[(ps, 0), (kk, 0), (w_f, 0), (b_f, 0), (kt_f, 0)], [(ps, 0), (kk, 0), (w_b, 0), (b_b, 0), (kt_b, 0)]]
    v_heads = heads_to_rows(ps[..., 2 * RW_W:3 * RW_W])
    *y, start_f, start_b = rwkv_scan_fwd(scan_rows, v_heads, rw_order, "rwkv_scan_fwd")
    o, ret_states = [], []
    for d in range(2):
        o_d, st_d = retention_fwd(px, cos, sin, ld_rows[d], ret_order[d], SCAN_DIRS[d][0], f"retention_fwd{d}")
        o.append(o_d), ret_states.append(st_d)

    def merge_ins(toff):
        return [Tiled(o[0], toff=toff), Tiled(o[1], toff=toff), Tiled(px, RET_W, 3, toff),
                Tiled(y[0], toff=toff), Tiled(y[1], toff=toff), Tiled(ps, RW_W, 0, toff), Tiled(kt_f, toff=toff),
                Tiled(ps, RW_W, 2, toff), Tiled(g_rw, toff=toff),
                Glob(wt["rwkv_r_k"]), Glob(wt["rwkv_ln_w"]), Glob(wt["rwkv_ln_b"]), Glob(ones64), Glob(ones128)]

    ret_out, rw_out = ew_forward(fn_merge, "merge_heads", bsz, n_xt, merge_ins(n_ct),
                                 [(RET_W, MXU_DTYPE), (RW_W, MXU_DTYPE)])
    merged = jnp.concatenate([ret_out, rw_out], axis=-1)
    if late_weights is not None:
        wt = {**wt, **late_weights(merged)}
    mix = matmul(flat(merged), wt["w_out"], "nn", "proj_out").reshape(bsz, t_x, D_MODEL)
    resid_ins = lambda: [Tiled(x), Tiled(mix), lat("gate1"), lat("shift2"), lat("scale2"), Glob(wt["norm2_g"])]
    h1, n2 = ew_forward(fn_resid_norm_mod, "resid_norm2", bsz, n_xt, resid_ins(), [(D_MODEL, f32), (D_MODEL, MXU_DTYPE)])
    act = matmul(flat(n2), wt["w_ff1"], "nn", "ff1", MXU_DTYPE, wt["b_ff1"], relu2).reshape(bsz, t_x, D_FF)
    ff = matmul(flat(act), wt["w_ff2"], "nn", "ff2").reshape(bsz, t_x, D_MODEL)

    g = {}
    loss, dh1, dff, dgate2, g["b_ff2"], g["final_g"] = loss_and_grads(
        h1, ff, tgt, mod_lat["gate2"], wt["b_ff2"], wt["final_g"], bsz, n_xt)
    dact = matmul(flat(dff), wt["w_ff2"], "nt", "ff2_dx", MXU_DTYPE).reshape(bsz, t_x, D_FF)
    g["w_ff2"] = matmul(flat(act), flat(dff), "tn", "ff2_dw")
    du, g["b_ff1"] = relu2_backward(act, dact, "relu2_bwd")
    dn2 = matmul(flat(du), wt["w_ff1"], "nt", "ff1_dx").reshape(bsz, t_x, D_MODEL)
    g["w_ff1"] = matmul(flat(n2), flat(du), "tn", "ff1_dw")
    dx_res, dmix, dgate1, dshift2, dscale2, g["norm2_g"] = ew_backward(
        fn_resid_norm_mod, "resid_norm2_bwd", bsz, n_xt, resid_ins(), [Tiled(dh1), Tiled(dn2)], [True] * 6,
        {1: MXU_DTYPE})
    dmerged = matmul(flat(dmix), wt["w_out"], "nt", "proj_out_dx").reshape(bsz, t_x, D_MODEL)
    g["w_out"] = matmul(flat(merged), flat(dmix), "tn", "proj_out_dw")
    if early_grads is not None:
        token = early_grads({n: g.pop(n) for n in LATE_WEIGHTS})
        wt = {**wt, "rwkv_r_k": wt["rwkv_r_k"] + token[:1, :1]}
    (do, dg_ret, dy, dr_m, dkt_m, dv_m, dg_rw, g["rwkv_r_k"], g["rwkv_ln_w"], g["rwkv_ln_b"]) = ew_backward(
        fn_merge, "merge_heads_bwd", bsz, n_xt, merge_ins(0),
        [Tiled(dmerged, RET_W, 0, -n_ct), Tiled(dmerged, RW_W, 1, -n_ct)],
        [True, False, True, True, False, True, True, True, True, True, True, True, False, False], lead=n_ct)

    dqkv, dld = [], []
    for d in range(2):
        *dqkv_d, dld_d = retention_bwd(do, px, ret_states[d], cos, sin, ld_rows[d], ret_order[d],
                                       SCAN_DIRS[d][0], f"retention_bwd{d}")
        dqkv.append(dqkv_d), dld.append(dld_d[0, :RET_HEADS])
    g["ret_log_decay"] = jnp.stack(dld)
    (dr_f, dkk_f, dw_f, db_f, dkt_f, dv_f, dr_b, dkk_b, dw_b, db_b, dkt_b, dv_b) = rwkv_scan_bwd(
        scan_rows, v_heads, heads_to_rows(dy), (start_f, start_b), rw_order, "rwkv_scan_bwd")
    prep_cts = [dkk_f + dkk_b, dw_f, db_f, dkt_f + dkt_m, dw_b, db_b, dkt_b, dg_rw]
    (dks, dlora, dw0_f, dw0_b, da0_f, da0_b, dwup_f, dwup_b, daup_f, daup_b, dgup, g["rwkv_k_k"],
     g["rwkv_k_a"]) = ew_backward(fn_rwkv_prepare, "rwkv_prepare_bwd", bsz, n_t, prep_ins(),
                                  [Tiled(c) for c in prep_cts], [True] * 13 + [False])
    g["rwkv_w0"] = jnp.concatenate([dw0_f, dw0_b], axis=0)
    g["rwkv_a0"] = jnp.concatenate([da0_f, da0_b], axis=0)
    g["rwkv_w_up"] = jnp.stack([dwup_f[:DECAY_LORA], dwup_b[:DECAY_LORA]])
    g["rwkv_a_up"] = jnp.stack([daup_f[DECAY_LORA:DECAY_LORA + AAA_LORA], daup_b[DECAY_LORA:DECAY_LORA + AAA_LORA]])
    g["rwkv_g_up"] = dgup[DECAY_LORA + AAA_LORA:]
    dps = jnp.concatenate([dr_f + dr_b + dr_m, dks, dv_f + dv_b + dv_m, dlora], axis=-1)
    dp_rw, g["rwkv_shift_mu"] = token_shift_bwd(dps, px_rw, wt["rwkv_shift_mu"], seg_first, seg_last)
    dpx = jnp.concatenate([(dqkv[0][k] + dqkv[1][k]).astype(MXU_DTYPE) for k in range(3)]
                          + [dg_ret.astype(MXU_DTYPE), dp_rw], axis=-1)
    dn1 = matmul(flat(dpx), wt["w_in"], "nt", "proj_in_dx").reshape(bsz, t_all, D_MODEL)
    g["w_in"] = matmul(flat(n1), flat(dpx), "tn", "proj_in_dw")
    dh, dshift1, dscale1, g["norm1_g"] = ew_backward(fn_norm_mod, "norm1_bwd", bsz, n_t, norm1_ins(), [Tiled(dn1)],
                                                     [True] * 4)
    grad_x = dh[:, t_c:] + dx_res
    zeros = jnp.zeros((D_MODEL,), f32)
    g["mod_x"] = jnp.stack([dshift1[:, 1, 0], dscale1[:, 1, 0], dgate1[:, 0, 0], dshift2[:, 0, 0], dscale2[:, 0, 0],
                            dgate2[:, 0, 0]], axis=1)
    g["mod_ctx"] = jnp.stack([dshift1[:, 0, 0].sum(0), dscale1[:, 0, 0].sum(0), zeros, zeros, zeros, zeros])
    return loss, grad_x, g


MESH_ID = pl.DeviceIdType.MESH
ALL_PEERS = [(dx, dy, dc) for dx in (0, 1) for dy in (0, 1) for dc in (0, 1)][1:]
CHIP_PEERS = [(1, 0, 0), (0, 1, 0), (1, 1, 0)]
CHIP_SLOTS = (0, 2, 4, 6)


def _mesh_pos():
    return lax.axis_index("x"), lax.axis_index("y"), lax.axis_index("c")


def _device_slot():
    x, y, c = _mesh_pos()
    return 4 * x + 2 * y + c


def sibling_swap(arrs, name):
    n = len(arrs)

    def body(*refs):
        in_refs, out_refs = refs[:n], refs[n:2 * n]
        send_sems, recv_sems = refs[2 * n:]
        x, y, c = _mesh_pos()
        copies = [pltpu.make_async_remote_copy(src_ref=in_refs[a], dst_ref=out_refs[a], send_sem=send_sems.at[a],
                                               recv_sem=recv_sems.at[a], device_id=(x, y, 1 - c),
                                               device_id_type=MESH_ID) for a in range(n)]
        for cp in copies:
            cp.start()
        for cp in copies:
            cp.wait()

    any_spec = pl.BlockSpec(memory_space=pl.ANY)
    res = pl.pallas_call(
        body, in_specs=[any_spec] * n, out_specs=[any_spec] * n,
        out_shape=[jax.ShapeDtypeStruct(a.shape, a.dtype) for a in arrs],
        scratch_shapes=[pltpu.SemaphoreType.DMA((n,)), pltpu.SemaphoreType.DMA((n,))],
        name=name)(*arrs)
    return list(res)


def exchange(arrs, gather, peers, name, by_chip=False, own=True):
    n, n_peers = len(arrs), len(peers)
    n_slots = N_SHARDS if by_chip else N_DEV
    slot = (lambda x, y, c: 2 * x + y) if by_chip else (lambda x, y, c: 4 * x + 2 * y + c)

    def body(*refs):
        in_refs, out_refs = refs[:n], refs[n:2 * n]
        send_sems, recv_sems, local_sems = refs[2 * n:]
        x, y, c = _mesh_pos()
        me = slot(x, y, c)
        copies, locals_ = [], []
        for a in range(n):
            if own:
                mine = in_refs[a] if gather else in_refs[a].at[me]
                loc = pltpu.make_async_copy(mine, out_refs[a].at[me], local_sems.at[a])
                loc.start()
                locals_.append(loc)
            for k, (dx, dy, dc) in enumerate(peers):
                peer = (1 - x if dx else x, 1 - y if dy else y, 1 - c if dc else c)
                src = in_refs[a] if gather else in_refs[a].at[slot(*peer)]
                sem = a * n_peers + k
                cp = pltpu.make_async_remote_copy(src_ref=src, dst_ref=out_refs[a].at[me], send_sem=send_sems.at[sem],
                                                  recv_sem=recv_sems.at[sem], device_id=peer, device_id_type=MESH_ID)
                cp.start()
                copies.append(cp)
        for cp in copies:
            cp.wait()
        for loc in locals_:
            loc.wait()

    any_spec = pl.BlockSpec(memory_space=pl.ANY)
    out_shape = [jax.ShapeDtypeStruct((n_slots,) + (a.shape if gather else a.shape[1:]), a.dtype) for a in arrs]
    n_sems = n * n_peers
    res = pl.pallas_call(
        body, in_specs=[any_spec] * n, out_specs=[any_spec] * n, out_shape=out_shape,
        scratch_shapes=[pltpu.SemaphoreType.DMA((n_sems,)), pltpu.SemaphoreType.DMA((n_sems,)),
                        pltpu.SemaphoreType.DMA((n,))],
        name=name)(*arrs)
    return list(res)


HBM_SPEC = pl.BlockSpec(memory_space=pltpu.HBM)
SEM_SPEC = pl.BlockSpec(memory_space=pltpu.SEMAPHORE)
DATAFLOW = pltpu.SideEffectType.DATAFLOW_SIDE_EFFECTING


def _peer_copies(src_refs, land_refs, send_sems, recv_sems, gather):
    x, y, c = _mesh_pos()
    me = 4 * x + 2 * y + c
    copies = []
    for a, (src_ref, land_ref) in enumerate(zip(src_refs, land_refs)):
        for k, (dx, dy, dc) in enumerate(ALL_PEERS):
            peer = (1 - x if dx else x, 1 - y if dy else y, 1 - c if dc else c)
            src = src_ref if gather else src_ref.at[4 * peer[0] + 2 * peer[1] + peer[2]]
            sem = a * len(ALL_PEERS) + k
            copies.append(pltpu.make_async_remote_copy(src_ref=src, dst_ref=land_ref.at[me], send_sem=send_sems.at[sem],
                                                       recv_sem=recv_sems.at[sem], device_id=peer,
                                                       device_id_type=MESH_ID))
    return copies


def exchange_start(arrs, gather, name):
    n = len(arrs)
    lands = [lax.empty((N_DEV,) + (a.shape if gather else a.shape[1:]), a.dtype) for a in arrs]

    def body(*refs):
        for cp in _peer_copies(refs[:n], refs[n:2 * n], refs[2 * n], refs[2 * n + 1], gather):
            cp.start()
        refs[-1][...] = jnp.zeros_like(refs[-1])

    sems = pltpu.SemaphoreType.DMA((n * len(ALL_PEERS),))
    hbm = [pltpu.HBM(a.shape, a.dtype) for a in arrs + lands]
    res = pl.pallas_call(
        body, name=name, out_shape=(sems, sems, *hbm, jax.ShapeDtypeStruct((8, 128), f32)),
        in_specs=[HBM_SPEC] * (2 * n),
        out_specs=(SEM_SPEC, SEM_SPEC, *[HBM_SPEC] * (2 * n), pl.BlockSpec(memory_space=pltpu.VMEM)),
        input_output_aliases={i: 2 + i for i in range(2 * n)},
        compiler_params=pltpu.CompilerParams(has_side_effects=DATAFLOW))(
        *[pltpu.with_memory_space_constraint(a, pltpu.HBM) for a in arrs + lands])
    return res[0], res[1], list(res[2:2 + n]), list(res[2 + n:2 + 2 * n]), res[-1]


def exchange_wait(started, after, gather, name):
    send_sems, recv_sems, srcs, lands, _ = started
    n = len(srcs)

    def body(*refs):
        for cp in _peer_copies(refs[:n], refs[n:2 * n], refs[2 * n], refs[2 * n + 1], gather):
            cp.wait_send()
            cp.wait_recv()

    res = pl.pallas_call(
        body, name=name, out_shape=tuple(pltpu.HBM(a.shape, a.dtype) for a in srcs + lands),
        in_specs=[HBM_SPEC] * (2 * n) + [SEM_SPEC, SEM_SPEC, pl.BlockSpec(memory_space=pl.ANY)],
        out_specs=tuple([HBM_SPEC] * (2 * n)), input_output_aliases={i: i for i in range(2 * n)},
        compiler_params=pltpu.CompilerParams(has_side_effects=DATAFLOW))(*srcs, *lands, send_sems, recv_sems, after)
    return list(res[n:])


def add_arrays(parts, name, out_dtype=f32):
    r, c = parts[0].shape
    tr = r
    for cand in (512, 256, 128, 64, 32, 16):
        if r % cand == 0:
            tr = cand
            break

    def body(*refs):
        acc = refs[0][...].astype(f32)
        for p_ref in refs[1:-1]:
            acc = acc + p_ref[...].astype(f32)
        refs[-1][...] = acc.astype(out_dtype)

    spec = pl.BlockSpec((tr, c), lambda i: (i, 0))
    return pl.pallas_call(body, grid=(r // tr,), in_specs=[spec] * len(parts), out_specs=spec,
                          out_shape=jax.ShapeDtypeStruct((r, c), out_dtype),
                          compiler_params=_cparams(("parallel",)), name=name)(*parts)


def gather_two_level(arrs, name):
    n = len(arrs)
    per = 7

    def body(*refs):
        in_refs, out_refs = refs[:n], refs[n:2 * n]
        send_sems, recv_sems = refs[2 * n:]
        x, y, c = _mesh_pos()
        me, sibling = (x, y, c), (x, y, 1 - c)
        chips = [(1 - x, y), (x, 1 - y), (1 - x, 1 - y)]

        def copy(a, k, block, to, src=None):
            rows = out_refs[a].at[4 * block[0] + 2 * block[1] + block[2]]
            return pltpu.make_async_remote_copy(src_ref=rows if src is None else src, dst_ref=rows,
                                                send_sem=send_sems.at[a * per + k], recv_sem=recv_sems.at[a * per + k],
                                                device_id=to, device_id_type=MESH_ID)

        first, passed = [], []
        for a in range(n):
            first.append(copy(a, 0, me, sibling, src=in_refs[a]))
            first += [copy(a, 1 + j, me, (*chip, c), src=in_refs[a]) for j, chip in enumerate(chips)]
        for cp in first:
            cp.start()
        for a in range(n):
            for j, chip in enumerate(chips):
                copy(a, 1 + j, (*chip, c), me).wait_recv()
                fwd = copy(a, 4 + j, (*chip, c), sibling)
                fwd.start()
                passed.append(fwd)
        for a in range(n):
            copy(a, 0, sibling, me).wait_recv()
            for j, chip in enumerate(chips):
                copy(a, 4 + j, (*chip, 1 - c), me).wait_recv()
        for cp in first + passed:
            cp.wait_send()

    any_spec = pl.BlockSpec(memory_space=pl.ANY)
    res = pl.pallas_call(
        body, in_specs=[any_spec] * n, out_specs=[any_spec] * n,
        out_shape=[jax.ShapeDtypeStruct((N_DEV,) + a.shape, a.dtype) for a in arrs],
        scratch_shapes=[pltpu.SemaphoreType.DMA((n * per,)), pltpu.SemaphoreType.DMA((n * per,))],
        name=name)(*arrs)
    return list(res)


def sum_slots(parts, slots, name):
    _, r, c = parts.shape
    tr = r
    for cand in (512, 256, 128, 64, 32, 16, 8):
        if r % cand == 0 and cand * c * 4 * len(slots) <= 8 * 1024 * 1024:
            tr = cand
            break

    def body(p_ref, o_ref):
        acc = p_ref[slots[0]].astype(f32)
        for s in slots[1:]:
            acc = acc + p_ref[s].astype(f32)
        o_ref[...] = acc

    return pl.pallas_call(body, grid=(r // tr,), in_specs=[pl.BlockSpec((parts.shape[0], tr, c), lambda i: (0, i, 0))],
                          out_specs=pl.BlockSpec((tr, c), lambda i: (i, 0)),
                          out_shape=jax.ShapeDtypeStruct((r, c), f32),
                          compiler_params=_cparams(("parallel",)), name=name)(parts)


def column_sum(a, name):
    def body(a_ref, o_ref):
        o_ref[...] = jnp.sum(a_ref[...], axis=0, keepdims=True)

    return pl.pallas_call(body, out_shape=jax.ShapeDtypeStruct((1, a.shape[1]), f32), name=name)(a)


def adamw(w, g, m, v, name):
    r, c = w.shape
    tr = r
    for cand in (256, 128, 64, 32, 16, 8):
        if r % cand == 0:
            tr = cand
            break

    def body(w_ref, g_ref, m_ref, v_ref, d_ref, mo_ref, vo_ref):
        gv = g_ref[...]
        m_new = ADAM_B1 * m_ref[...] + (1.0 - ADAM_B1) * gv
        v_new = ADAM_B2 * v_ref[...] + (1.0 - ADAM_B2) * jnp.square(gv)
        m_hat = m_new / (1.0 - ADAM_B1 ** ADAM_STEP)
        v_hat = v_new / (1.0 - ADAM_B2 ** ADAM_STEP)
        d_ref[...] = -ADAM_LR * (m_hat / (jnp.sqrt(v_hat) + ADAM_EPS) + ADAM_WD * w_ref[...])
        mo_ref[...] = m_new
        vo_ref[...] = v_new

    spec = pl.BlockSpec((tr, c), lambda i: (i, 0))
    return pl.pallas_call(body, grid=(r // tr,), in_specs=[spec] * 4, out_specs=[spec] * 3,
                          out_shape=[jax.ShapeDtypeStruct((r, c), f32)] * 3,
                          compiler_params=_cparams(("parallel",)), name=name)(w, g, m, v)


def adaln_fwd(c_rows, w, b):
    def body(c_ref, w_ref, b_ref, o_ref):
        cv = c_ref[...]
        o_ref[...] = _mxu_dot(cv * jax.nn.sigmoid(cv), w_ref[...]) + b_ref[...]

    return pl.pallas_call(body, out_shape=jax.ShapeDtypeStruct((c_rows.shape[0], w.shape[1]), f32),
                          compiler_params=pltpu.CompilerParams(vmem_limit_bytes=VMEM_LIMIT), name="adaln_fwd")(c_rows, w, b)


def adaln_bwd(c_rows, dm, w):
    def body(c_ref, dm_ref, w_ref, gw_ref, ds_ref):
        cv = c_ref[...]
        gw_ref[...] = _dg(cv * jax.nn.sigmoid(cv), dm_ref[...], 0, 0)
        ds_ref[...] = _dg(dm_ref[...], w_ref[...], 1, 1)

    return pl.pallas_call(body, out_shape=[jax.ShapeDtypeStruct(w.shape, f32),
                                           jax.ShapeDtypeStruct(c_rows.shape, f32)],
                          compiler_params=pltpu.CompilerParams(vmem_limit_bytes=VMEM_LIMIT), name="adaln_bwd")(c_rows, dm, w)


def c_ctx_grad(parts, c_ctx_row):
    def body(p_ref, c_ref, o_ref):
        total = p_ref[0, 0:1, :]
        for s in range(1, N_SHARDS):
            total = total + p_ref[s, 0:1, :]
        _, vjp = jax.vjp(jax.nn.silu, c_ref[...])
        o_ref[...] = vjp(total)[0]

    return pl.pallas_call(body, out_shape=jax.ShapeDtypeStruct((1, D_MODEL), f32), name="c_ctx_grad")(parts, c_ctx_row)


PACK_W = 1024
PACK_ROWS = 8


def _pack(arrs):
    pieces, layout, r0 = [], [], 0
    for a in arrs:
        size = math.prod(a.shape)
        rows = -(-size // (PACK_W * PACK_ROWS)) * PACK_ROWS
        pieces.append(jnp.pad(a.reshape(-1).astype(f32), (0, rows * PACK_W - size)).reshape(rows, PACK_W))
        layout.append((r0, rows, a.shape))
        r0 += rows
    return jnp.concatenate(pieces, axis=0), layout


def _unpack(pack, layout, lead=()):
    n_lead = len(lead)
    outs = []
    for r0, rows, shape in layout:
        piece = pack[(slice(None),) * n_lead + (slice(r0, r0 + rows),)].reshape(lead + (-1,))
        outs.append(piece[..., :math.prod(shape)].reshape(lead + tuple(shape)))
    return outs


W_NAMES = ("c_ctx", "w_ada", "b_ada", "norm1_g", "norm2_g", "w_in", "ret_log_decay", "rwkv_shift_mu", "rwkv_w0",
           "rwkv_w_up", "rwkv_a0", "rwkv_a_up", "rwkv_g_up", "rwkv_k_k", "rwkv_k_a", "rwkv_r_k", "rwkv_ln_w",
           "rwkv_ln_b", "w_out", "w_ff1", "b_ff1", "w_ff2", "b_ff2", "final_g")
COL_SHARDED = ("w_in", "w_ff1")
ROW_SHARDED = ("w_out", "w_ff2")
LAST_SHARDED = ("rwkv_shift_mu", "rwkv_w0", "rwkv_w_up", "rwkv_a0", "rwkv_a_up", "rwkv_g_up")
REPLICATED = ("c_ctx", "b_ada", "norm1_g", "norm2_g", "ret_log_decay", "rwkv_k_k", "rwkv_k_a", "rwkv_r_k",
              "rwkv_ln_w", "rwkv_ln_b", "b_ff1", "b_ff2", "final_g")
N_SHARDS = 4


def _train_step(a):
    x, c, ctx, tgt = a["x"], a["c"], a["ctx"], a["loss_target"]
    bsz = x.shape[0]
    mx, my, mc = _mesh_pos()
    shard = 2 * mx + my
    dev = _device_slot()

    (c_all,) = exchange([jnp.pad(c, ((0, PACK_ROWS - bsz), (0, 0)))], True, ALL_PEERS, "gather_c")
    n_ex = N_DEV * bsz
    c_rows = jnp.concatenate([c_all[:, :bsz].reshape(n_ex, D_MODEL), a["c_ctx"][None, :],
                              jnp.zeros((PACK_ROWS - 1, D_MODEL), f32)], axis=0)
    ada_cols = a["w_ada"].shape[-1]
    b_ada_cols = lax.dynamic_slice_in_dim(a["b_ada"], shard * ada_cols, ada_cols, axis=1)
    mod_cols = adaln_fwd(c_rows, a["w_ada"][0], b_ada_cols)

    def own_half(n):
        w = a[n][0].astype(MXU_DTYPE)
        return lax.dynamic_slice_in_dim(w, mc * (w.shape[0] // 2), w.shape[0] // 2, axis=0)

    def whole_weight(n, gth, own):
        per_chip = lax.dynamic_update_index_in_dim(gth, own, dev, 0).reshape(N_SHARDS, -1, gth.shape[-1])
        return (per_chip.transpose(1, 0, 2).reshape(per_chip.shape[1], -1) if n in COL_SHARDED
                else per_chip.reshape(-1, per_chip.shape[-1]))

    small_pack, small_layout = _pack([a[n][0] for n in LAST_SHARDED])
    own_blocks = [mod_cols, own_half("w_in"), small_pack]
    gathered = gather_two_level(own_blocks, "gather_weights")
    late_own = [own_half(n) for n in LATE_WEIGHTS]
    late_started = exchange_start(late_own, True, "gather_late_start")
    mod_own = lax.dynamic_update_index_in_dim(gathered[0], mod_cols, dev, 0)
    mod_all = jnp.stack([mod_own[s] for s in CHIP_SLOTS], axis=1).reshape(c_rows.shape[0], -1)
    mod_all = mod_all + late_started[-1][0, 0]
    mod_x = lax.dynamic_slice_in_dim(mod_all, dev * bsz, bsz, axis=0).reshape(bsz, 6, D_MODEL)
    mod_ctx = mod_all[n_ex].reshape(6, D_MODEL)
    wt = {"w_in": whole_weight("w_in", gathered[1], own_blocks[1])}

    def late_weights(after):
        lands = exchange_wait(late_started, after, True, "gather_late_wait")
        return {n: whole_weight(n, land, own) for n, land, own in zip(LATE_WEIGHTS, lands, late_own)}

    def grad_blocks(n, gw):
        if n in COL_SHARDED:
            gw = gw.reshape(gw.shape[0], N_SHARDS, -1).transpose(1, 0, 2)
        return gw.reshape(N_DEV, -1, gw.shape[-1]).astype(MXU_DTYPE)

    late_sent = {}

    def early_grads(late_g):
        late_sent["blocks"] = [grad_blocks(n, late_g[n]) for n in LATE_WEIGHTS]
        late_sent["started"] = exchange_start(late_sent["blocks"], False, "scatter_late_start")
        return late_sent["started"][-1]

    small_own = lax.dynamic_update_index_in_dim(gathered[2], small_pack, dev, 0)
    small_by_chip = _unpack(jnp.stack([small_own[s] for s in CHIP_SLOTS]), small_layout, (N_SHARDS,))
    for n, parts in zip(LAST_SHARDED, small_by_chip):
        wt[n] = jnp.concatenate([parts[s] for s in range(N_SHARDS)], axis=-1)
    for n in ("norm1_g", "norm2_g", "rwkv_k_k", "rwkv_k_a", "rwkv_r_k", "rwkv_ln_w", "rwkv_ln_b", "b_ff1", "b_ff2"):
        wt[n] = a[n]
    wt["ret_log_decay"] = a["ret_log_decay"][0]
    wt["final_g"] = a["final_g"][None, :]

    loss, grad_x, g = layer_step(x, ctx, tgt, mod_x, mod_ctx, wt, late_weights, early_grads)

    small_names = [n for n in REPLICATED if n not in ("c_ctx", "b_ada")]
    g_pack, g_layout = _pack([jnp.pad(loss, ((0, 0), (0, PACK_W - loss.shape[1])))] + [g[n] for n in small_names]
                             + [g["mod_x"], g["mod_ctx"]])
    (g_packs,) = gather_two_level([g_pack], "gather_small_grads")
    g_packs = lax.dynamic_update_index_in_dim(g_packs, g_pack, dev, 0)
    g_sum = _unpack(sum_slots(g_packs, tuple(range(N_DEV)), "sum_small_grads"), g_layout)
    loss_total = g_sum[0][0, 0]
    grads = dict(zip(small_names, g_sum[1:1 + len(small_names)]))
    dmod_ctx = g_sum[-1].reshape(1, -1)
    dmod_x = _unpack(g_packs, g_layout, (N_DEV,))[-2].reshape(n_ex, -1)
    dmod = jnp.concatenate([dmod_x, dmod_ctx, jnp.zeros((PACK_ROWS - 1, dmod_x.shape[1]), f32)], axis=0)
    grads["b_ada"] = column_sum(dmod, "b_ada_grad")
    dmod_cols = lax.dynamic_slice_in_dim(dmod, shard * ada_cols, ada_cols, axis=1)
    grads["w_ada"], dsilu = adaln_bwd(c_rows, dmod_cols, a["w_ada"][0])

    blocks = [grad_blocks("w_in", g["w_in"])]
    shard_packs = []
    for s in range(N_SHARDS):
        pieces_s = [lax.slice_in_dim(g[n], s * a[n].shape[-1], (s + 1) * a[n].shape[-1], axis=g[n].ndim - 1)
                    for n in LAST_SHARDED]
        pack_s, shard_layout = _pack(pieces_s)
        shard_packs.append(jnp.pad(pack_s, ((0, -pack_s.shape[0] % (2 * PACK_ROWS)), (0, 0))))
    blocks.append(jnp.stack(shard_packs).reshape(N_DEV, -1, PACK_W))
    scattered = ("w_in", "small_shards")
    halves_of = lambda blk, core: lax.dynamic_index_in_dim(
        blk.reshape(N_SHARDS, 2, *blk.shape[1:]), core, axis=1, keepdims=False).reshape(-1, blk.shape[-1])
    from_sibling = sibling_swap([halves_of(blk, 1 - mc) for blk in blocks], "prereduce_swap")
    chip_sums = [add_arrays([halves_of(blk, mc), got], f"prereduce_{n}", blk.dtype).reshape(N_SHARDS, -1, blk.shape[-1])
                 for n, blk, got in zip(scattered, blocks, from_sibling)]
    dsilu_rows = jnp.broadcast_to(jnp.pad(dsilu[n_ex:n_ex + 1], ((0, PACK_ROWS - 1), (0, 0)))[None],
                                  (N_SHARDS, PACK_ROWS, D_MODEL))
    to_chips = [dsilu_rows] + chip_sums
    received = exchange(to_chips, False, CHIP_PEERS, "scatter_big_grads", by_chip=True, own=False)
    received = [lax.dynamic_update_index_in_dim(got, lax.dynamic_index_in_dim(sent, shard, 0, keepdims=False), shard, 0)
                for got, sent in zip(received, to_chips)]
    grads["c_ctx"] = c_ctx_grad(received[0], a["c_ctx"][None, :])
    half_sums = [sum_slots(p, tuple(range(N_SHARDS)), f"sum_{n}") for n, p in zip(scattered, received[1:])]
    late_lands = exchange_wait(late_sent["started"], half_sums[0], False, "scatter_late_wait")
    for n, land, sent in zip(LATE_WEIGHTS, late_lands, late_sent["blocks"]):
        land = lax.dynamic_update_index_in_dim(land, lax.dynamic_index_in_dim(sent, dev, 0, keepdims=False), dev, 0)
        half_sums.append(sum_slots(land, tuple(range(N_DEV)), f"sum_{n}"))
    scattered = scattered + LATE_WEIGHTS
    other_halves = sibling_swap(half_sums, "swap_halves")
    for n, mine, other in zip(scattered, half_sums, other_halves):
        rows = mine.shape[0]
        whole = jnp.zeros((2 * rows, mine.shape[1]), f32)
        whole = lax.dynamic_update_slice_in_dim(whole, mine, mc * rows, axis=0)
        grads[n] = lax.dynamic_update_slice_in_dim(whole, other, (1 - mc) * rows, axis=0)
    grads.update(zip(LAST_SHARDED, _unpack(grads.pop("small_shards"), shard_layout)))

    out_g, out_d, out_m, out_v = {}, {}, {}, {}
    for n in ("w_ada",) + COL_SHARDED + ROW_SHARDED:
        out_g[n] = grads[n].reshape(a[n].shape)
        two_d = lambda z: z.reshape(-1, z.shape[-1])
        d, m, v = adamw(two_d(a[n]), two_d(out_g[n]), two_d(a["m_" + n]), two_d(a["v_" + n]), f"adamw_{n}")
        out_d[n], out_m[n], out_v[n] = d.reshape(a[n].shape), m.reshape(a[n].shape), v.reshape(a[n].shape)
    rest = REPLICATED + LAST_SHARDED
    for n in rest:
        out_g[n] = grads[n].reshape(a[n].shape)
    packs = [_pack([src[n] for n in rest])[0] for src in
             ({n: a[n] for n in rest}, out_g, {n: a["m_" + n] for n in rest}, {n: a["v_" + n] for n in rest})]
    _, rest_layout = _pack([a[n] for n in rest])
    for dst, pack in zip((out_d, out_m, out_v), adamw(*packs, "adamw_small")):
        dst.update(zip(rest, _unpack(pack, rest_layout)))
    return (loss_total, grad_x, *[out_g[n] for n in W_NAMES], *[out_d[n] for n in W_NAMES],
            *[out_m[n] for n in W_NAMES], *[out_v[n] for n in W_NAMES])


def kernel(x, c, ctx, c_ctx, w_ada, b_ada, norm1_g, norm2_g, w_in, ret_log_decay, rwkv_shift_mu, rwkv_w0, rwkv_w_up, rwkv_a0, rwkv_a_up, rwkv_g_up, rwkv_k_k, rwkv_k_a, rwkv_r_k, rwkv_ln_w, rwkv_ln_b, w_out, w_ff1, b_ff1, w_ff2, b_ff2, final_g, loss_target, m_c_ctx, m_w_ada, m_b_ada, m_norm1_g, m_norm2_g, m_w_in, m_ret_log_decay, m_rwkv_shift_mu, m_rwkv_w0, m_rwkv_w_up, m_rwkv_a0, m_rwkv_a_up, m_rwkv_g_up, m_rwkv_k_k, m_rwkv_k_a, m_rwkv_r_k, m_rwkv_ln_w, m_rwkv_ln_b, m_w_out, m_w_ff1, m_b_ff1, m_w_ff2, m_b_ff2, m_final_g, v_c_ctx, v_w_ada, v_b_ada, v_norm1_g, v_norm2_g, v_w_in, v_ret_log_decay, v_rwkv_shift_mu, v_rwkv_w0, v_rwkv_w_up, v_rwkv_a0, v_rwkv_a_up, v_rwkv_g_up, v_rwkv_k_k, v_rwkv_k_a, v_rwkv_r_k, v_rwkv_ln_w, v_rwkv_ln_b, v_w_out, v_w_ff1, v_b_ff1, v_w_ff2, v_b_ff2, v_final_g):
    return _train_step(dict(locals()))
```

```python
import functools
import math

import jax
import jax.numpy as jnp
from jax import lax
from jax.experimental import pallas as pl
from jax.experimental.pallas import tpu as pltpu

f32 = jnp.float32
MXU_DTYPE = jnp.bfloat16

D_MODEL = 1024
RET_W = 512
RET_HEADS = 4
RET_DH = 128
RET_CHUNK = 128
RW_W = 512
RW_N = 64
DECAY_LORA = 64
AAA_LORA = 64
GATE_LORA = 128
LORA_W = DECAY_LORA + AAA_LORA + GATE_LORA
D_FF = 4096
RET_COLS = 4 * RET_W
SHIFT_COLS = 3 * RW_W + LORA_W
IN_COLS = RET_COLS + SHIFT_COLS
GRID_W = 64
ROPE_BASE = 10000.0
NORM_EPS = 1e-6
GN_EPS = 64e-5
W_DECAY_SCALE = math.exp(-0.5)
ADAM_LR, ADAM_B1, ADAM_B2, ADAM_EPS, ADAM_WD, ADAM_STEP = 0.001, 0.9, 0.999, 1e-08, 0.01, 10

TOK_TILE = 256
MATMUL_TILE = 1024
SCAN_CHUNK = 16
SCAN_UNROLL = SCAN_CHUNK
N_DEV = 8
V7X_VMEM_BYTES = 64 * 1024 * 1024
VMEM_LIMIT = V7X_VMEM_BYTES * 7 // 8


def _cparams(sem):
    return pltpu.CompilerParams(dimension_semantics=sem, vmem_limit_bytes=VMEM_LIMIT)


def _tile(n, cap):
    best = None
    for t in range(128, min(n, cap) + 1, 128):
        if n % t == 0:
            best = t
    return best if best is not None else n


def matmul(a, b, mode, name, out_dtype=f32, bias=None, finish=None):
    if mode == "nn":
        (m, k), (k2, n) = a.shape, b.shape
    elif mode == "nt":
        (m, k), (n, k2) = a.shape, b.shape
    else:
        (k, m), (k2, n) = a.shape, b.shape
    assert k == k2, (a.shape, b.shape, mode)
    tm, tn, tk = _tile(m, MATMUL_TILE), _tile(n, MATMUL_TILE), _tile(k, MATMUL_TILE)
    nk = k // tk
    dims = {"nn": ((1,), (0,)), "nt": ((1,), (1,)), "tn": ((0,), (0,))}[mode]

    def body(a_ref, b_ref, *rest):
        o_ref, acc_ref = rest[-2:]
        kk = pl.program_id(2)

        @pl.when(kk == 0)
        def _():
            acc_ref[...] = jnp.zeros_like(acc_ref)

        acc_ref[...] += lax.dot_general(a_ref[...].astype(MXU_DTYPE), b_ref[...].astype(MXU_DTYPE),
                                        (dims, ((), ())), preferred_element_type=f32)

        @pl.when(kk == nk - 1)
        def _():
            res = acc_ref[...]
            if bias is not None:
                res = res + rest[0][...]
            if finish is not None:
                res = finish(res)
            o_ref[...] = res.astype(o_ref.dtype)

    if mode == "nn":
        a_spec = pl.BlockSpec((tm, tk), lambda i, j, q: (i, q))
        b_spec = pl.BlockSpec((tk, tn), lambda i, j, q: (q, j))
    elif mode == "nt":
        a_spec = pl.BlockSpec((tm, tk), lambda i, j, q: (i, q))
        b_spec = pl.BlockSpec((tn, tk), lambda i, j, q: (j, q))
    else:
        a_spec = pl.BlockSpec((tk, tm), lambda i, j, q: (q, i))
        b_spec = pl.BlockSpec((tk, tn), lambda i, j, q: (q, j))
    extra_specs = [] if bias is None else [pl.BlockSpec((1, tn), lambda i, j, q: (0, j))]
    extra = [] if bias is None else [bias]
    return pl.pallas_call(
        body, grid=(m // tm, n // tn, nk), in_specs=[a_spec, b_spec] + extra_specs,
        out_specs=pl.BlockSpec((tm, tn), lambda i, j, q: (i, j)),
        out_shape=jax.ShapeDtypeStruct((m, n), out_dtype),
        scratch_shapes=[pltpu.VMEM((tm, tn), f32)],
        compiler_params=_cparams(("parallel", "parallel", "arbitrary")), name=name)(a, b, *extra)


class Tiled:
    def __init__(self, arr, w=None, cidx=0, toff=0):
        self.arr, self.w, self.cidx, self.toff = arr, (arr.shape[-1] if w is None else w), cidx, toff

    def spec(self):
        cidx, toff = self.cidx, self.toff
        return pl.BlockSpec((None, TOK_TILE, self.w), lambda b, i: (b, jnp.maximum(i + toff, 0), cidx))


class Seg:
    def __init__(self, arr, seg, first):
        self.arr, self.seg, self.first = arr, seg, first

    def spec(self):
        seg = self.seg
        return pl.BlockSpec((None, None, 1, self.arr.shape[-1]), lambda b, i: (b, seg(i), 0, 0))


class Glob:
    def __init__(self, arr):
        self.arr = arr

    def spec(self):
        return pl.BlockSpec(self.arr.shape, lambda b, i: (0,) * self.arr.ndim)


def ew_forward(fn, name, bsz, n_tiles, ins, outs):
    n_in = len(ins)

    def body(*refs):
        res = fn(*[r[...] for r in refs[:n_in]])
        for o_ref, o in zip(refs[n_in:], res):
            o_ref[...] = o.astype(o_ref.dtype)

    out_specs = [pl.BlockSpec((None, TOK_TILE, w), lambda b, i: (b, i, 0)) for w, _ in outs]
    out_shape = [jax.ShapeDtypeStruct((bsz, n_tiles * TOK_TILE, w), dt) for w, dt in outs]
    return pl.pallas_call(body, grid=(bsz, n_tiles), in_specs=[d.spec() for d in ins], out_specs=out_specs,
                          out_shape=out_shape, compiler_params=_cparams(("parallel", "parallel")), name=name)(
        *[d.arr for d in ins])


def ew_backward(fn, name, bsz, n_tiles, ins, cts, want, grad_dtypes=None, lead=0):
    n_in, n_ct = len(ins), len(cts)
    diff = [k for k in range(n_in) if want[k]]
    grad_dtypes = grad_dtypes or {}
    assert lead == 0 or not any(isinstance(ins[k], Seg) for k in diff)

    def body(*refs):
        b, i = pl.program_id(0), pl.program_id(1)
        g_refs = refs[n_in + n_ct:]

        def tile_grads():
            vals = [r[...] for r in refs[:n_in]]
            ct_vals = tuple(r[...].astype(f32) for r in refs[n_in:n_in + n_ct])

            def f(*dvals):
                full = list(vals)
                for k, v in zip(diff, dvals):
                    full[k] = v
                return tuple(fn(*full))

            _, vjp = jax.vjp(f, *[vals[k] for k in diff])
            grads = vjp(ct_vals)
            for k, g_ref, g in zip(diff, g_refs, grads):
                d = ins[k]
                if isinstance(d, Tiled):
                    g_ref[...] = g.astype(g_ref.dtype)
                else:
                    zero = d.first(i) if isinstance(d, Seg) else jnp.logical_and(b == 0, i == lead)

                    @pl.when(zero)
                    def _(g_ref=g_ref):
                        g_ref[...] = jnp.zeros_like(g_ref)

                    g_ref[...] += g

        if lead == 0:
            tile_grads()
        else:
            pl.when(i >= lead)(tile_grads)

            @pl.when(i < lead)
            def _():
                for k, g_ref in zip(diff, g_refs):
                    if isinstance(ins[k], Tiled):
                        g_ref[...] = jnp.zeros_like(g_ref)

    out_specs, out_shape = [], []
    for k in diff:
        d = ins[k]
        if isinstance(d, Tiled):
            out_specs.append(pl.BlockSpec((None, TOK_TILE, d.w), lambda b, i: (b, i, 0)))
            out_shape.append(jax.ShapeDtypeStruct((bsz, (n_tiles + lead) * TOK_TILE, d.w), grad_dtypes.get(k, f32)))
        else:
            out_specs.append(d.spec())
            out_shape.append(jax.ShapeDtypeStruct(d.arr.shape, f32))
    return pl.pallas_call(body, grid=(bsz, n_tiles + lead),
                          in_specs=[d.spec() for d in ins] + [c.spec() for c in cts],
                          out_specs=out_specs, out_shape=out_shape,
                          compiler_params=_cparams(("arbitrary", "arbitrary")), name=name)(
        *[d.arr for d in ins], *[c.arr for c in cts])


@jax.custom_vjp
def _mxu_dot(a, b):
    return jnp.dot(a.astype(MXU_DTYPE), b.astype(MXU_DTYPE), preferred_element_type=f32)


def _mxu_dot_fwd(a, b):
    return _mxu_dot(a, b), (a, b)


def _mxu_dot_bwd(res, ct):
    a, b = res
    ct = ct.astype(MXU_DTYPE)
    da = lax.dot_general(ct, b.astype(MXU_DTYPE), (((1,), (1,)), ((), ())), preferred_element_type=f32)
    db = lax.dot_general(a.astype(MXU_DTYPE), ct, (((0,), (0,)), ((), ())), preferred_element_type=f32)
    return da, db


_mxu_dot.defvjp(_mxu_dot_fwd, _mxu_dot_bwd)


def _split_dot_impl(x, ones_mat):
    hi = x.astype(MXU_DTYPE)
    lo = (x - hi.astype(f32)).astype(MXU_DTYPE)
    return jnp.dot(hi, ones_mat, preferred_element_type=f32) + jnp.dot(lo, ones_mat, preferred_element_type=f32)


@jax.custom_vjp
def _split_dot(x, ones_mat):
    return _split_dot_impl(x, ones_mat)


def _split_dot_fwd(x, ones_mat):
    return _split_dot_impl(x, ones_mat), ones_mat


def _split_dot_bwd(ones_mat, ct):
    return _split_dot_impl(ct, ones_mat), None


_split_dot.defvjp(_split_dot_fwd, _split_dot_bwd)


def _block_ones(n, group):
    idx = jnp.arange(n) // group
    return (idx[:, None] == idx[None, :]).astype(MXU_DTYPE)


def _rms(x, g):
    return x * lax.rsqrt(jnp.mean(x * x, axis=-1, keepdims=True) + NORM_EPS) * g


def fn_norm_mod(h, shift, scale, g):
    return (_rms(h, g) * (1.0 + scale) + shift,)


def fn_rwkv_prepare(ks, lora, w0_f, w0_b, a0_f, a0_b, w_up_f, w_up_b, a_up_f, a_up_b, g_up, k_k, k_a, ones64):
    kkr = ks * k_k
    kk = kkr * lax.rsqrt(_split_dot(kkr * kkr, ones64) + 1e-12)
    outs = [kk]
    th = jnp.tanh(lora)
    for w0, a0, w_up, a_up in ((w0_f, a0_f, w_up_f, a_up_f), (w0_b, a0_b, w_up_b, a_up_b)):
        w = jnp.exp(-W_DECAY_SCALE * jax.nn.sigmoid(w0 + _mxu_dot(th, w_up)))
        a = jax.nn.sigmoid(a0 + _mxu_dot(lora, a_up))
        kt = ks * (1.0 + (a - 1.0) * k_a)
        outs += [w, a * kk, kt]
    outs.append(_mxu_dot(jax.nn.sigmoid(lora), g_up))
    return tuple(outs)


def fn_merge(o_f, o_b, g_ret, y_f, y_b, r, kt_f, v, g_rw, r_k, ln_w, ln_b, ones64, ones128):
    o = o_f + o_b
    ret = o * lax.rsqrt(_split_dot(o * o, ones128) * (1.0 / RET_DH) + NORM_EPS) * (g_ret * jax.nn.sigmoid(g_ret))
    y = y_f + y_b
    mean = _split_dot(y, ones64) * (1.0 / RW_N)
    yc = y - mean
    var = _split_dot(yc * yc, ones64) * (1.0 / RW_N)
    y_n = yc * lax.rsqrt(var + GN_EPS) * ln_w + ln_b
    bonus = _split_dot(r * kt_f * r_k, ones64) * v
    return ret, (y_n + bonus) * g_rw


def fn_resid_norm_mod(x, mix, gate, shift, scale, g):
    h1 = x + gate * mix
    return h1, _rms(h1, g) * (1.0 + scale) + shift


def relu2(z):
    return jnp.square(jnp.maximum(z, 0.0))


def relu2_backward(act, dact, name):
    bsz, n_tok, width = act.shape

    def body(a_ref, d_ref, du_ref, db_ref):
        du = d_ref[...].astype(f32) * (2.0 * jnp.sqrt(a_ref[...].astype(f32)))
        du_ref[...] = du.astype(du_ref.dtype)

        @pl.when(jnp.logical_and(pl.program_id(0) == 0, pl.program_id(1) == 0))
        def _():
            db_ref[...] = jnp.zeros_like(db_ref)

        db_ref[...] += jnp.sum(du, axis=0, keepdims=True)

    tile = pl.BlockSpec((None, TOK_TILE, width), lambda b, i: (b, i, 0))
    row = pl.BlockSpec((1, width), lambda b, i: (0, 0))
    return pl.pallas_call(body, grid=(bsz, n_tok // TOK_TILE), in_specs=[tile, tile], out_specs=[tile, row],
                          out_shape=[jax.ShapeDtypeStruct(act.shape, MXU_DTYPE), jax.ShapeDtypeStruct((1, width), f32)],
                          compiler_params=_cparams(("arbitrary", "arbitrary")), name=name)(act, dact)


def fn_loss(h1, f, tgt, gate, b2, g):
    y = _rms(h1 + gate * (f + b2), g)
    err = jnp.square(y - tgt)
    return 0.5 * jnp.sum(jnp.mean(err, axis=-1, keepdims=True), axis=0, keepdims=True)


def loss_and_grads(h1, f, tgt, gate, b2, g, bsz, n_tiles):
    def body(h1_ref, f_ref, t_ref, gate_ref, b2_ref, g_ref, loss_ref, dh1_ref, df_ref, dgate_ref, db2_ref, dg_ref):
        b, i = pl.program_id(0), pl.program_id(1)
        tgt_v = t_ref[...]
        loss, vjp = jax.vjp(lambda a, c, e, p, q: fn_loss(a, c, tgt_v, e, p, q),
                            h1_ref[...], f_ref[...], gate_ref[...], b2_ref[...], g_ref[...])
        dh1, df, dgate, db2, dg = vjp(jnp.ones((1, 1), f32))
        dh1_ref[...] = dh1
        df_ref[...] = df.astype(df_ref.dtype)

        @pl.when(i == 0)
        def _():
            dgate_ref[...] = jnp.zeros_like(dgate_ref)

        @pl.when(jnp.logical_and(b == 0, i == 0))
        def _():
            loss_ref[...] = jnp.zeros_like(loss_ref)
            db2_ref[...] = jnp.zeros_like(db2_ref)
            dg_ref[...] = jnp.zeros_like(dg_ref)

        dgate_ref[...] += dgate
        db2_ref[...] += db2
        dg_ref[...] += dg
        loss_ref[...] += jnp.broadcast_to(loss, loss_ref.shape)

    tile = pl.BlockSpec((None, TOK_TILE, D_MODEL), lambda b, i: (b, i, 0))
    row = pl.BlockSpec((1, D_MODEL), lambda b, i: (0, 0))
    seg = pl.BlockSpec((None, None, 1, D_MODEL), lambda b, i: (b, 0, 0, 0))
    t_tok = n_tiles * TOK_TILE
    return pl.pallas_call(
        body, grid=(bsz, n_tiles), in_specs=[tile, tile, tile, seg, row, row],
        out_specs=[pl.BlockSpec((1, 128), lambda b, i: (0, 0)), tile, tile, seg, row, row],
        out_shape=[jax.ShapeDtypeStruct((1, 128), f32), jax.ShapeDtypeStruct((bsz, t_tok, D_MODEL), f32),
                   jax.ShapeDtypeStruct((bsz, t_tok, D_MODEL), MXU_DTYPE),
                   jax.ShapeDtypeStruct((bsz, 1, 1, D_MODEL), f32),
                   jax.ShapeDtypeStruct((1, D_MODEL), f32), jax.ShapeDtypeStruct((1, D_MODEL), f32)],
        compiler_params=_cparams(("arbitrary", "arbitrary")), name="loss_and_grads")(h1, f, tgt, gate, b2, g)


SHIFT_BLOCK = SHIFT_COLS
HALO_ROWS = 8


def _shift_specs(n_tok, col0):
    per_tile = TOK_TILE // HALO_ROWS
    last = n_tok // HALO_ROWS - 1
    tile = pl.BlockSpec((None, TOK_TILE, SHIFT_BLOCK), lambda j, b, i: (b, i, col0 + j))
    prev = pl.BlockSpec((None, HALO_ROWS, SHIFT_BLOCK),
                        lambda j, b, i: (b, jnp.maximum(i * per_tile - 1, 0), col0 + j))
    nxt = pl.BlockSpec((None, HALO_ROWS, SHIFT_BLOCK),
                       lambda j, b, i: (b, jnp.minimum((i + 1) * per_tile, last), col0 + j))
    return tile, prev, nxt


def _shifted(p, prev_ref, next_ref, is_first, is_last):
    row = lax.broadcasted_iota(jnp.int32, p.shape, 0)
    prev_row = jnp.where(is_first, 0.0, prev_ref[HALO_ROWS - 1:HALO_ROWS, :].astype(f32))
    next_row = jnp.where(is_last, 0.0, next_ref[0:1, :].astype(f32))
    prev = jnp.where(row == 0, prev_row, pltpu.roll(p, 1, axis=0))
    nxt = jnp.where(row == TOK_TILE - 1, next_row, pltpu.roll(p, TOK_TILE - 1, axis=0))
    return prev, nxt


def token_shift(px, mu, seg_first, seg_last):
    bsz, n_tok, _ = px.shape
    n_tiles = n_tok // TOK_TILE

    def body(p_ref, prev_ref, next_ref, mu_ref, o_ref):
        i = pl.program_id(2)
        p = p_ref[...]
        prev, nxt = _shifted(p, prev_ref, next_ref, seg_first(i), seg_last(i))
        o_ref[...] = p + mu_ref[0:1, :] * (prev - p) + mu_ref[1:2, :] * (nxt - p)

    tile, prev, nxt = _shift_specs(n_tok, 0)
    return pl.pallas_call(
        body, grid=(SHIFT_COLS // SHIFT_BLOCK, bsz, n_tiles),
        in_specs=[tile, prev, nxt, pl.BlockSpec((2, SHIFT_BLOCK), lambda j, b, i: (0, j))],
        out_specs=pl.BlockSpec((None, TOK_TILE, SHIFT_BLOCK), lambda j, b, i: (b, i, j)),
        out_shape=jax.ShapeDtypeStruct((bsz, n_tok, SHIFT_COLS), f32),
        compiler_params=_cparams(("parallel", "parallel", "parallel")), name="token_shift")(px, px, px, mu)


def token_shift_bwd(dps, px, mu, seg_first, seg_last):
    bsz, n_tok, _ = px.shape
    n_tiles = n_tok // TOK_TILE

    def body(d_ref, dprev_ref, dnext_ref, p_ref, prev_ref, next_ref, mu_ref, dp_ref, dmu_ref):
        b, i = pl.program_id(1), pl.program_id(2)
        first, last = seg_first(i), seg_last(i)
        d, p = d_ref[...], p_ref[...]
        d_prev, d_next = _shifted(d, dprev_ref, dnext_ref, first, last)
        p_prev, p_next = _shifted(p, prev_ref, next_ref, first, last)
        mu0, mu1 = mu_ref[0:1, :], mu_ref[1:2, :]
        dp_ref[...] = (d + mu0 * (d_next - d) + mu1 * (d_prev - d)).astype(dp_ref.dtype)

        @pl.when(jnp.logical_and(b == 0, i == 0))
        def _():
            dmu_ref[...] = jnp.zeros_like(dmu_ref)

        dmu_ref[0:1, :] += jnp.sum(d * (p_prev - p), axis=0, keepdims=True)
        dmu_ref[1:2, :] += jnp.sum(d * (p_next - p), axis=0, keepdims=True)

    dtile, dprev, dnext = _shift_specs(n_tok, 0)
    tile, prev, nxt = _shift_specs(n_tok, 0)
    mu_spec = pl.BlockSpec((2, SHIFT_BLOCK), lambda j, b, i: (0, j))
    return pl.pallas_call(
        body, grid=(SHIFT_COLS // SHIFT_BLOCK, bsz, n_tiles),
        in_specs=[dtile, dprev, dnext, tile, prev, nxt, mu_spec],
        out_specs=[pl.BlockSpec((None, TOK_TILE, SHIFT_BLOCK), lambda j, b, i: (b, i, j)), mu_spec],
        out_shape=[jax.ShapeDtypeStruct((bsz, n_tok, SHIFT_COLS), MXU_DTYPE),
                   jax.ShapeDtypeStruct((2, SHIFT_COLS), f32)],
        compiler_params=_cparams(("arbitrary", "arbitrary", "arbitrary")), name="token_shift_bwd")(
        dps, dps, dps, px, px, px, mu)


def _dg(a, b, ca, cb):
    return lax.dot_general(a.astype(MXU_DTYPE), b.astype(MXU_DTYPE), (((ca,), (cb,)), ((), ())),
                           preferred_element_type=f32)


@jax.custom_vjp
def _mm_nt(a, b):
    return _dg(a, b, 1, 1)


_mm_nt.defvjp(lambda a, b: (_dg(a, b, 1, 1), (a, b)),
              lambda res, ct: (_dg(ct, res[1], 1, 0), _dg(ct, res[0], 0, 0)))


@jax.custom_vjp
def _mm_tn(a, b):
    return _dg(a, b, 0, 0)


_mm_tn.defvjp(lambda a, b: (_dg(a, b, 0, 0), (a, b)),
              lambda res, ct: (_dg(res[1], ct, 1, 1), _dg(res[0], ct, 1, 0)))


ROTARY_PAIR = RET_DH // 4


def _swap_pairs_impl(t):
    lane = lax.broadcasted_iota(jnp.int32, t.shape, 1)
    return jnp.where(lane % (2 * ROTARY_PAIR) < ROTARY_PAIR, pltpu.roll(t, RET_DH - ROTARY_PAIR, axis=1),
                     pltpu.roll(t, ROTARY_PAIR, axis=1))


@jax.custom_vjp
def _swap_pairs(t):
    return _swap_pairs_impl(t)


_swap_pairs.defvjp(lambda t: (_swap_pairs_impl(t), None), lambda _, ct: (_swap_pairs_impl(ct),))


def _ret_chunk(state, q_raw, k_raw, v, cos, sin, ld_row, head, reverse):
    c = RET_CHUNK
    lane = lax.broadcasted_iota(jnp.int32, ld_row.shape, 1)
    lg = -jnp.exp(jnp.sum(jnp.where(lane == head, ld_row, 0.0), axis=-1, keepdims=True))
    rot = lambda t: t * cos + _swap_pairs(t) * sin
    q = rot(q_raw)
    k = rot(k_raw) * (RET_DH ** -0.5)
    ti = lax.broadcasted_iota(jnp.int32, (c, 1), 0).astype(f32)
    tj = lax.broadcasted_iota(jnp.int32, (1, c), 1).astype(f32)
    if not reverse:
        dist, mask, q_exp, k_exp = ti - tj, (ti - tj) >= 0, ti + 1.0, c - 1.0 - ti
    else:
        dist, mask, q_exp, k_exp = tj - ti, (tj - ti) > 0, c - ti, ti
    decay = jnp.where(mask, jnp.exp(lg * jnp.maximum(dist, 0.0)), 0.0)
    scores = _mm_nt(q, k) * decay
    out = _mxu_dot(scores, v) + _mxu_dot(q * jnp.exp(lg * q_exp), state)
    new_state = state * jnp.exp(lg * c) + _mm_tn(k * jnp.exp(lg * k_exp), v)
    return out, new_state


def _ret_specs(bsz, order):
    tok = lambda col=0: pl.BlockSpec((bsz, RET_CHUNK, RET_W), lambda i: (0, order(i), col))
    tab = pl.BlockSpec((RET_CHUNK, RET_DH), lambda i: (order(i), 0))
    ld = pl.BlockSpec((1, RET_DH), lambda i: (0, 0))
    return tok, tab, ld


def retention_fwd(px, cos, sin, ld_row, order, reverse, name):
    bsz, n_tok, _ = px.shape
    n_ch = n_tok // RET_CHUNK

    def body(q_ref, k_ref, v_ref, cos_ref, sin_ref, ld_ref, o_ref, sv_ref, st_ref):
        @pl.when(pl.program_id(0) == 0)
        def _():
            st_ref[...] = jnp.zeros_like(st_ref)

        for b in range(bsz):
            for h in range(RET_HEADS):
                sl = slice(h * RET_DH, (h + 1) * RET_DH)
                s = st_ref[b, h]
                sv_ref[b, h] = s
                o, s_new = _ret_chunk(s, q_ref[b, :, sl], k_ref[b, :, sl], v_ref[b, :, sl], cos_ref[...], sin_ref[...],
                                      ld_ref[...], h, reverse)
                o_ref[b, :, sl] = o
                st_ref[b, h] = s_new

    tok, tab, ld = _ret_specs(bsz, order)
    return pl.pallas_call(
        body, grid=(n_ch,), in_specs=[tok(0), tok(1), tok(2), tab, tab, ld],
        out_specs=[tok(), pl.BlockSpec((bsz, None, RET_HEADS, RET_DH, RET_DH), lambda i: (0, i, 0, 0, 0))],
        out_shape=[jax.ShapeDtypeStruct((bsz, n_tok, RET_W), f32),
                   jax.ShapeDtypeStruct((bsz, n_ch, RET_HEADS, RET_DH, RET_DH), f32)],
        scratch_shapes=[pltpu.VMEM((bsz, RET_HEADS, RET_DH, RET_DH), f32)],
        compiler_params=_cparams(("arbitrary",)), name=name)(px, px, px, cos, sin, ld_row)


def retention_bwd(do, px, states, cos, sin, ld_row, order, reverse, name):
    bsz, n_tok, _ = px.shape
    n_ch = n_tok // RET_CHUNK
    back = lambda i: order(n_ch - 1 - i)

    def body(do_ref, q_ref, k_ref, v_ref, sv_ref, cos_ref, sin_ref, ld_ref,
             dq_ref, dk_ref, dv_ref, dld_ref, dst_ref):
        @pl.when(pl.program_id(0) == 0)
        def _():
            dst_ref[...] = jnp.zeros_like(dst_ref)
            dld_ref[...] = jnp.zeros_like(dld_ref)

        cos_v, sin_v = cos_ref[...], sin_ref[...]
        for b in range(bsz):
            for h in range(RET_HEADS):
                sl = slice(h * RET_DH, (h + 1) * RET_DH)
                f = lambda s, q, k, v, ld, h=h: _ret_chunk(s, q, k, v, cos_v, sin_v, ld, h, reverse)
                _, vjp = jax.vjp(f, sv_ref[b, h], q_ref[b, :, sl], k_ref[b, :, sl], v_ref[b, :, sl], ld_ref[...])
                ds, dq, dk, dv, dld = vjp((do_ref[b, :, sl], dst_ref[b, h]))
                dst_ref[b, h] = ds
                dq_ref[b, :, sl] = dq
                dk_ref[b, :, sl] = dk
                dv_ref[b, :, sl] = dv
                dld_ref[...] += dld

    tok, tab, ld = _ret_specs(bsz, back)
    return pl.pallas_call(
        body, grid=(n_ch,),
        in_specs=[tok(), tok(0), tok(1), tok(2),
                  pl.BlockSpec((bsz, None, RET_HEADS, RET_DH, RET_DH), lambda i: (0, n_ch - 1 - i, 0, 0, 0)),
                  tab, tab, ld],
        out_specs=[tok(), tok(), tok(), ld],
        out_shape=[jax.ShapeDtypeStruct((bsz, n_tok, RET_W), f32)] * 3 + [jax.ShapeDtypeStruct((1, RET_DH), f32)],
        scratch_shapes=[pltpu.VMEM((bsz, RET_HEADS, RET_DH, RET_DH), f32)],
        compiler_params=_cparams(("arbitrary",)), name=name)(
        do, px, px, px, states, cos, sin, ld_row)


HALF_W = RW_W // 2


def _head_sum(x, ones):
    xm = x.astype(MXU_DTYPE)
    return jnp.concatenate([jnp.dot(xm[:, :HALF_W], ones, preferred_element_type=f32),
                            jnp.dot(xm[:, HALF_W:], ones, preferred_element_type=f32)], axis=1)


def _stack(parts):
    return jnp.concatenate(parts, axis=0)


def _row(ref, b, t):
    return ref[b, pl.ds(t, 1), :]


SCAN_DIRS = ((False, True), (True, False))
RW_HEADS = RW_W // RW_N
HEAD_ROWS_PAD = 16


def _head_rows(row, mask):
    return jnp.broadcast_to(row, mask.shape) * mask


def _outer(per_value, row, mask_pad):
    return lax.dot_general(per_value.astype(MXU_DTYPE), _head_rows(row, mask_pad).astype(MXU_DTYPE),
                           (((0,), (0,)), ((), ())), preferred_element_type=f32)


def _read(states, rows, mask):
    lhs = _stack([_head_rows(r, mask) for r in rows])
    return lax.dot_general(lhs.astype(MXU_DTYPE), _stack(states).astype(MXU_DTYPE), (((1,), (1,)), ((), ())),
                           preferred_element_type=f32)


def _row_from_heads(per_value, state, mask_pad):
    full = jnp.dot(per_value.astype(MXU_DTYPE), state.astype(MXU_DTYPE), preferred_element_type=f32)
    return jnp.sum(full * mask_pad, axis=0, keepdims=True)


def _scan_specs(bsz, order):
    rows = lambda col=0: pl.BlockSpec((bsz, SCAN_CHUNK, RW_W), lambda i: (0, order(i), col))
    per_value = pl.BlockSpec((bsz, SCAN_CHUNK, HEAD_ROWS_PAD, RW_N), lambda i: (0, order(i), 0, 0))
    states = pl.BlockSpec((SCAN_CHUNK, bsz, RW_N, RW_W), lambda i: (order(i), 0, 0, 0))
    return rows, per_value, states


def _mxu_operands(states):
    return [s.astype(MXU_DTYPE) for s in states]


def _removed(states_m, kk_t, ones, bsz):
    removed = _head_sum(_stack([states_m[b] * kk_t[b].astype(MXU_DTYPE) for b in range(bsz)]), ones)
    return [removed[b * RW_N:(b + 1) * RW_N] for b in range(bsz)]


def _advance(sp, rem, w_t, b_t, vk, bsz):
    return [sp[b] * w_t[b] - rem[b] * b_t[b] + vk[b] for b in range(bsz)]


def heads_to_rows(a):
    b, t, _ = a.shape
    return jnp.pad(a.astype(MXU_DTYPE).reshape(b, t, RW_HEADS, RW_N),
                   ((0, 0), (0, 0), (0, HEAD_ROWS_PAD - RW_HEADS), (0, 0)))


def _blocks_to_rows(raw_ref, first, row_ref, bsz):
    steps = pl.ds(first, SCAN_CHUNK)
    for b in range(bsz):
        for h in range(RW_HEADS):
            row_ref[b, :, h * RW_N:(h + 1) * RW_N] = raw_ref[steps, RW_HEADS * b + h, RW_N * b:RW_N * (b + 1)]


N_ROWS_FWD = 5
N_ROWS_BWD = 5


def _scan_consts(bsz):
    head = (jnp.arange(RW_W)[None, :] // RW_N == jnp.arange(RW_HEADS)[:, None]).astype(f32)
    return head, jnp.pad(head, ((0, HEAD_ROWS_PAD - RW_HEADS), (0, 0))), _block_ones(HALF_W, RW_N)


def _const_specs(consts):
    return [pl.BlockSpec(c.shape, lambda i: (0, 0)) for c in consts]


def rwkv_scan_fwd(rows_in, v_heads, orders, name):
    bsz, n_tok, _ = rows_in[0][0][0].shape
    n_ch = n_tok // SCAN_CHUNK
    rng = range(bsz)
    consts = _scan_consts(bsz)

    def body(*refs):
        rows = [refs[:N_ROWS_FWD], refs[N_ROWS_FWD:2 * N_ROWS_FWD]]
        (v0, v1, head_ref, pad_ref, ones_ref, y0, y1, h0, h1, f0, f1, s0, s1, late_ref,
         raw_ref) = refs[2 * N_ROWS_FWD:]
        v_refs, y_refs, hist_refs, final_refs, s_refs = (v0, v1), (y0, y1), (h0, h1), (f0, f1), (s0, s1)
        head_v, pad_v, ones_v = head_ref[...], pad_ref[...], ones_ref[...]
        for d in range(2):
            @pl.when(pl.program_id(0) == 0)
            def _(d=d):
                s_refs[d][...] = jnp.zeros_like(s_refs[d])

        def step(j, carry):
            ts = [SCAN_CHUNK - 1 - j if reverse else j for reverse, _ in SCAN_DIRS]
            sps = [[s_refs[d][b] for b in rng] for d in range(2)]
            sps_m = [_mxu_operands(sps[d]) for d in range(2)]
            rems = [_removed(sps_m[d], [_row(rows[d][1], b, ts[d]) for b in rng], ones_v, bsz) for d in range(2)]
            vks = [[_outer(v_refs[d][b, ts[d]], _row(rows[d][4], b, ts[d]), pad_v) for b in rng] for d in range(2)]
            for d, (reverse, inclusive) in enumerate(SCAN_DIRS):
                r_ref = rows[d][0]
                if inclusive:
                    before = jnp.maximum(j - 1, 0)
                    late_ref[j] = _read(sps_m[d], [_row(r_ref, b, before) for b in rng], head_v)
                else:
                    raw_ref[ts[d]] = _read(sps_m[d], [_row(r_ref, b, ts[d]) for b in rng], head_v)
            for d in range(2):
                new = _advance(sps[d], rems[d], [_row(rows[d][2], b, ts[d]) for b in rng],
                               [_row(rows[d][3], b, ts[d]) for b in rng], vks[d], bsz)
                for b in rng:
                    hist_refs[d][ts[d], b] = sps_m[d][b]
                    s_refs[d][b] = new[b]
            return carry

        lax.fori_loop(0, SCAN_CHUNK, step, 0, unroll=SCAN_UNROLL)
        for d, (reverse, inclusive) in enumerate(SCAN_DIRS):
            final_refs[d][...] = s_refs[d][...]
            if inclusive:
                assert not reverse
                last = SCAN_CHUNK - 1
                late_ref[SCAN_CHUNK] = _read(_mxu_operands([s_refs[d][b] for b in rng]),
                                             [rows[d][0][b, last:last + 1, :] for b in rng], head_v)
                _blocks_to_rows(late_ref, 1, y_refs[d], bsz)
            else:
                _blocks_to_rows(raw_ref, 0, y_refs[d], bsz)

    specs = [_scan_specs(bsz, orders[d]) for d in range(2)]
    state = pltpu.VMEM((bsz, RW_N, RW_W), f32)
    late = pltpu.VMEM((SCAN_CHUNK + 1, RW_HEADS * bsz, RW_N * bsz), f32)
    raw = pltpu.VMEM((SCAN_CHUNK, RW_HEADS * bsz, RW_N * bsz), f32)
    final_spec = pl.BlockSpec((bsz, RW_N, RW_W), lambda i: (0, 0, 0))
    return pl.pallas_call(
        body, grid=(n_ch,),
        in_specs=[specs[d][0](col) for d in range(2) for _, col in rows_in[d]] + [specs[0][1], specs[1][1]]
        + _const_specs(consts),
        out_specs=[specs[0][0](), specs[1][0](), specs[0][2], specs[1][2], final_spec, final_spec],
        out_shape=[jax.ShapeDtypeStruct((bsz, n_tok, RW_W), f32)] * 2
        + [jax.ShapeDtypeStruct((n_tok, bsz, RW_N, RW_W), MXU_DTYPE)] * 2
        + [jax.ShapeDtypeStruct((bsz, RW_N, RW_W), f32)] * 2,
        scratch_shapes=[state, state, late, raw],
        compiler_params=_cparams(("arbitrary",)), name=name)(
        *[a for d in range(2) for a, _ in rows_in[d]], v_heads, v_heads, *consts)


def rwkv_scan_bwd(rows_in, v_heads, dy_heads, hists, finals, orders, name):
    bsz, n_tok, _ = rows_in[0][0][0].shape
    n_ch = n_tok // SCAN_CHUNK
    backs = [functools.partial(lambda i, order: order(n_ch - 1 - i), order=orders[d]) for d in range(2)]
    rng = range(bsz)
    consts = _scan_consts(bsz)
    n_out, n_scr = 6, 6

    def body(*refs):
        rows = [refs[:N_ROWS_BWD], refs[N_ROWS_BWD:2 * N_ROWS_BWD]]
        rest = refs[2 * N_ROWS_BWD:]
        v_refs, dy_refs, hist_refs, final_refs = rest[0:2], rest[2:4], rest[4:6], rest[6:8]
        head_ref, pad_ref, ones_ref = rest[8:11]
        outs = [rest[11:11 + n_out], rest[11 + n_out:11 + 2 * n_out]]
        scr = [rest[11 + 2 * n_out:11 + 2 * n_out + n_scr], rest[11 + 2 * n_out + n_scr:]]
        head_v, pad_v, ones_v = head_ref[...], pad_ref[...], ones_ref[...]
        for d in range(2):
            @pl.when(pl.program_id(0) == 0)
            def _(d=d):
                scr[d][1][...] = jnp.zeros_like(scr[d][1])
                scr[d][0][...] = final_refs[d][...]

        def step_of(j, reverse):
            return j if reverse else SCAN_CHUNK - 1 - j

        for d, (reverse, _) in enumerate(SCAN_DIRS):
            t0 = step_of(0, reverse)
            for b in rng:
                scr[d][4][b] = _outer(dy_refs[d][b, t0], rows[d][0][b, t0:t0 + 1, :], pad_v)

        def bstep(j, carry):
            ts = [step_of(j, reverse) for reverse, _ in SCAN_DIRS]
            reads = [[scr[d][4][b] for b in rng] for d in range(2)]
            dss = []
            for d, (_, inclusive) in enumerate(SCAN_DIRS):
                ds = [scr[d][1][b] for b in rng]
                dss.append([ds[b] + reads[d][b] for b in rng] if inclusive else ds)
            dss_m = [_mxu_operands(dss[d]) for d in range(2)]
            drems = [_removed(dss_m[d], [-_row(rows[d][3], b, ts[d]) for b in rng], ones_v, bsz) for d in range(2)]
            for d, (reverse, _) in enumerate(SCAN_DIRS):
                t_next = step_of(jnp.minimum(j + 1, SCAN_CHUNK - 1), reverse)
                for b in rng:
                    scr[d][4][b] = _outer(dy_refs[d][b, t_next], _row(rows[d][0], b, t_next), pad_v)
                scr[d][5][ts[d]] = _read(dss_m[d], [_row(rows[d][4], b, ts[d]) for b in rng], head_v)
            for d, (_, inclusive) in enumerate(SCAN_DIRS):
                _, kk_ref, w_ref, _, _ = rows[d]
                _, ds_ref, dsh_ref, drem_ref = scr[d][:4]
                for b in rng:
                    dsh_ref[ts[d], b] = dss[d][b]
                    drem_ref[ts[d], b] = drems[d][b]
                    dsp = dss[d][b] * _row(w_ref, b, ts[d]) + drems[d][b] * _row(kk_ref, b, ts[d])
                    ds_ref[b] = dsp if inclusive else dsp + reads[d][b]
            return carry

        lax.fori_loop(0, SCAN_CHUNK, bstep, 0, unroll=SCAN_UNROLL)

        rsum = lambda z: jnp.sum(z, axis=0, keepdims=True)
        for d, (reverse, inclusive) in enumerate(SCAN_DIRS):
            dr_ref, dkk_ref, dw_ref, db_ref, dkt_ref, dv_ref = outs[d]
            after_ref, _, dsh_ref, drem_ref, _, dv_raw_ref = scr[d]
            hist_ref, kk_ref = hist_refs[d], rows[d][1]
            _blocks_to_rows(dv_raw_ref, 0, dv_ref, bsz)
            for t in range(SCAN_CHUNK):
                ts = slice(t, t + 1)
                after = t - 1 if reverse else t + 1
                rems = _removed([hist_ref[t, b] for b in rng], [kk_ref[b, ts, :] for b in rng], ones_v, bsz)
                for b in rng:
                    sp_m, ds = hist_ref[t, b], dsh_ref[t, b]
                    sp = sp_m.astype(f32)
                    if not inclusive:
                        seen = sp_m
                    else:
                        seen = hist_ref[after, b] if 0 <= after < SCAN_CHUNK else after_ref[b]
                    dr_ref[b, ts, :] = _row_from_heads(dy_refs[d][b, t], seen, pad_v)
                    dkt_ref[b, ts, :] = _row_from_heads(v_refs[d][b, t], ds, pad_v)
                    dw_ref[b, ts, :] = rsum(ds * sp)
                    db_ref[b, ts, :] = -rsum(ds * rems[b])
                    dkk_ref[b, ts, :] = rsum(sp * drem_ref[t, b])
            if inclusive:
                first = SCAN_CHUNK - 1 if reverse else 0
                for b in rng:
                    after_ref[b] = hist_ref[first, b].astype(f32)

    specs = [_scan_specs(bsz, backs[d]) for d in range(2)]
    hist = pltpu.VMEM((SCAN_CHUNK, bsz, RW_N, RW_W), f32)
    state = pltpu.VMEM((bsz, RW_N, RW_W), f32)
    final_spec = pl.BlockSpec((bsz, RW_N, RW_W), lambda i: (0, 0, 0))
    raw = pltpu.VMEM((SCAN_CHUNK, RW_HEADS * bsz, RW_N * bsz), f32)
    return pl.pallas_call(
        body, grid=(n_ch,),
        in_specs=[specs[d][0](col) for d in range(2) for _, col in rows_in[d]]
        + [specs[0][1], specs[1][1]] * 2 + [specs[0][2], specs[1][2], final_spec, final_spec] + _const_specs(consts),
        out_specs=[specs[d][0]() for d in range(2) for _ in range(n_out)],
        out_shape=[jax.ShapeDtypeStruct((bsz, n_tok, RW_W), f32)] * (2 * n_out),
        scratch_shapes=[state, state, hist, hist, state, raw] * 2,
        compiler_params=_cparams(("arbitrary",)), name=name)(
        *[a for d in range(2) for a, _ in rows_in[d]], v_heads, v_heads, dy_heads, dy_heads, *hists, *finals, *consts)


MOD_NAMES = ("shift1", "scale1", "gate1", "shift2", "scale2", "gate2")


def _rope_tables(t_ctx, t_x):
    quarter = RET_DH // 4
    pos = jnp.arange(t_x)
    inv = jnp.power(ROPE_BASE, -jnp.arange(0, 2 * quarter, 2, dtype=f32) / (2 * quarter))
    ang_r = (pos // GRID_W).astype(f32)[:, None] * inv[None, :]
    ang_c = (pos % GRID_W).astype(f32)[:, None] * inv[None, :]
    cos = jnp.concatenate([jnp.cos(ang_r)] * 2 + [jnp.cos(ang_c)] * 2, axis=1)
    sin = jnp.concatenate([-jnp.sin(ang_r), jnp.sin(ang_r), -jnp.sin(ang_c), jnp.sin(ang_c)], axis=1)
    cos = jnp.concatenate([jnp.ones((t_ctx, RET_DH), f32), cos], axis=0)
    sin = jnp.concatenate([jnp.zeros((t_ctx, RET_DH), f32), sin], axis=0)
    return cos, sin


def _pad_rows(w, lo, total):
    return jnp.pad(w, ((lo, total - lo - w.shape[0]), (0, 0)))


LATE_WEIGHTS = ("w_out", "w_ff1", "w_ff2")


def layer_step(x, ctx, tgt, mod_x, mod_ctx, wt, late_weights=None, early_grads=None):
    bsz, t_x, _ = x.shape
    t_c = ctx.shape[1]
    t_all = t_c + t_x
    n_ct, n_xt = t_c // TOK_TILE, t_x // TOK_TILE
    n_t = n_ct + n_xt
    assert t_c % TOK_TILE == 0 and t_x % TOK_TILE == 0 and t_c % RET_CHUNK == 0

    seg = lambda i: (i >= n_ct).astype(jnp.int32)
    seg_first = lambda i: jnp.logical_or(i == 0, i == n_ct)
    seg_last = lambda i: jnp.logical_or(i == n_ct - 1, i == n_t - 1)
    mod_all = {n: jnp.stack([jnp.broadcast_to(mod_ctx[k], (bsz, D_MODEL)), mod_x[:, k]], axis=1)[:, :, None, :]
               for k, n in enumerate(MOD_NAMES)}
    mod_lat = {n: mod_x[:, k][:, None, None, :] for k, n in enumerate(MOD_NAMES)}
    both = lambda n: Seg(mod_all[n], seg, seg_first)
    lat = lambda n: Seg(mod_lat[n], lambda i: 0, lambda i: i == 0)
    flat = lambda a: a.reshape(-1, a.shape[-1])

    def chunk_orders(n_ctx_chunks, n_chunks):
        fwd = lambda i: i
        bwd = lambda i: jnp.where(i < n_ctx_chunks, n_ctx_chunks - 1 - i, n_chunks + n_ctx_chunks - 1 - i)
        return fwd, bwd

    ones64, ones128 = _block_ones(RW_W, RW_N), _block_ones(RET_W, RET_DH)
    cos, sin = _rope_tables(t_c, t_x)
    ld_rows = [jnp.pad(wt["ret_log_decay"][d][None, :], ((0, 0), (0, RET_DH - RET_HEADS))) for d in range(2)]
    w_up_pad = [_pad_rows(wt["rwkv_w_up"][d], 0, LORA_W) for d in range(2)]
    a_up_pad = [_pad_rows(wt["rwkv_a_up"][d], DECAY_LORA, LORA_W) for d in range(2)]
    g_up_pad = _pad_rows(wt["rwkv_g_up"], DECAY_LORA + AAA_LORA, LORA_W)
    row = lambda a, d: a[d][None, :]

    h = jnp.concatenate([ctx, x], axis=1)
    norm1_ins = lambda: [Tiled(h), both("shift1"), both("scale1"), Glob(wt["norm1_g"])]
    (n1,) = ew_forward(fn_norm_mod, "norm1", bsz, n_t, norm1_ins(), [(D_MODEL, MXU_DTYPE)])
    px = matmul(flat(n1), wt["w_in"], "nn", "proj_in").reshape(bsz, t_all, IN_COLS)
    px_rw = px[..., RET_COLS:]
    ps = token_shift(px_rw, wt["rwkv_shift_mu"], seg_first, seg_last)

    def prep_ins(toff=0):
        return [Tiled(ps, RW_W, 1), Tiled(ps, LORA_W, 3 * RW_W // LORA_W),
                Glob(row(wt["rwkv_w0"], 0)), Glob(row(wt["rwkv_w0"], 1)),
                Glob(row(wt["rwkv_a0"], 0)), Glob(row(wt["rwkv_a0"], 1)),
                Glob(w_up_pad[0]), Glob(w_up_pad[1]), Glob(a_up_pad[0]), Glob(a_up_pad[1]), Glob(g_up_pad),
                Glob(wt["rwkv_k_k"]), Glob(wt["rwkv_k_a"]), Glob(ones64)]

    kk, w_f, b_f, kt_f, w_b, b_b, kt_b, g_rw = ew_forward(fn_rwkv_prepare, "rwkv_prepare", bsz, n_t, prep_ins(),
                                                           [(RW_W, f32)] * 8)
    rw_order = chunk_orders(t_c // SCAN_CHUNK, t_all // SCAN_CHUNK)
    ret_order = chunk_orders(t_c // RET_CHUNK, t_all // RET_CHUNK)
    scan_rows = [[(ps, 0), (kk, 0), (w_f, 0), (b_f, 0), (kt_f, 0)], [(ps, 0), (kk, 0), (w_b, 0), (b_b, 0), (kt_b, 0)]]
    v_heads = heads_to_rows(ps[..., 2 * RW_W:3 * RW_W])
    y_f, y_b, *kept_states = rwkv_scan_fwd(scan_rows, v_heads, rw_order, "rwkv_scan_fwd")
    y = [y_f, y_b]
    o, ret_states = [], []
    for d in range(2):
        o_d, st_d = retention_fwd(px, cos, sin, ld_rows[d], ret_order[d], SCAN_DIRS[d][0], f"retention_fwd{d}")
        o.append(o_d), ret_states.append(st_d)

    def merge_ins(toff):
        return [Tiled(o[0], toff=toff), Tiled(o[1], toff=toff), Tiled(px, RET_W, 3, toff),
                Tiled(y[0], toff=toff), Tiled(y[1], toff=toff), Tiled(ps, RW_W, 0, toff), Tiled(kt_f, toff=toff),
                Tiled(ps, RW_W, 2, toff), Tiled(g_rw, toff=toff),
                Glob(wt["rwkv_r_k"]), Glob(wt["rwkv_ln_w"]), Glob(wt["rwkv_ln_b"]), Glob(ones64), Glob(ones128)]

    ret_out, rw_out = ew_forward(fn_merge, "merge_heads", bsz, n_xt, merge_ins(n_ct),
                                 [(RET_W, MXU_DTYPE), (RW_W, MXU_DTYPE)])
    merged = jnp.concatenate([ret_out, rw_out], axis=-1)
    if late_weights is not None:
        wt = {**wt, **late_weights(merged)}
    mix = matmul(flat(merged), wt["w_out"], "nn", "proj_out").reshape(bsz, t_x, D_MODEL)
    resid_ins = lambda: [Tiled(x), Tiled(mix), lat("gate1"), lat("shift2"), lat("scale2"), Glob(wt["norm2_g"])]
    h1, n2 = ew_forward(fn_resid_norm_mod, "resid_norm2", bsz, n_xt, resid_ins(), [(D_MODEL, f32), (D_MODEL, MXU_DTYPE)])
    act = matmul(flat(n2), wt["w_ff1"], "nn", "ff1", MXU_DTYPE, wt["b_ff1"], relu2).reshape(bsz, t_x, D_FF)
    ff = matmul(flat(act), wt["w_ff2"], "nn", "ff2").reshape(bsz, t_x, D_MODEL)

    g = {}
    loss, dh1, dff, dgate2, g["b_ff2"], g["final_g"] = loss_and_grads(
        h1, ff, tgt, mod_lat["gate2"], wt["b_ff2"], wt["final_g"], bsz, n_xt)
    dact = matmul(flat(dff), wt["w_ff2"], "nt", "ff2_dx", MXU_DTYPE).reshape(bsz, t_x, D_FF)
    g["w_ff2"] = matmul(flat(act), flat(dff), "tn", "ff2_dw")
    du, g["b_ff1"] = relu2_backward(act, dact, "relu2_bwd")
    dn2 = matmul(flat(du), wt["w_ff1"], "nt", "ff1_dx").reshape(bsz, t_x, D_MODEL)
    g["w_ff1"] = matmul(flat(n2), flat(du), "tn", "ff1_dw")
    dx_res, dmix, dgate1, dshift2, dscale2, g["norm2_g"] = ew_backward(
        fn_resid_norm_mod, "resid_norm2_bwd", bsz, n_xt, resid_ins(), [Tiled(dh1), Tiled(dn2)], [True] * 6,
        {1: MXU_DTYPE})
    dmerged = matmul(flat(dmix), wt["w_out"], "nt", "proj_out_dx").reshape(bsz, t_x, D_MODEL)
    g["w_out"] = matmul(flat(merged), flat(dmix), "tn", "proj_out_dw")
    if early_grads is not None:
        token = early_grads({n: g.pop(n) for n in LATE_WEIGHTS})
        wt = {**wt, "rwkv_r_k": wt["rwkv_r_k"] + token[:1, :1]}
    (do, dg_ret, dy, dr_m, dkt_m, dv_m, dg_rw, g["rwkv_r_k"], g["rwkv_ln_w"], g["rwkv_ln_b"]) = ew_backward(
        fn_merge, "merge_heads_bwd", bsz, n_xt, merge_ins(0),
        [Tiled(dmerged, RET_W, 0, -n_ct), Tiled(dmerged, RW_W, 1, -n_ct)],
        [True, False, True, True, False, True, True, True, True, True, True, True, False, False], lead=n_ct)

    dqkv, dld = [], []
    for d in range(2):
        *dqkv_d, dld_d = retention_bwd(do, px, ret_states[d], cos, sin, ld_rows[d], ret_order[d],
                                       SCAN_DIRS[d][0], f"retention_bwd{d}")
        dqkv.append(dqkv_d), dld.append(dld_d[0, :RET_HEADS])
    g["ret_log_decay"] = jnp.stack(dld)
    (dr_f, dkk_f, dw_f, db_f, dkt_f, dv_f, dr_b, dkk_b, dw_b, db_b, dkt_b, dv_b) = rwkv_scan_bwd(
        scan_rows, v_heads, heads_to_rows(dy), kept_states[:2], kept_states[2:], rw_order, "rwkv_scan_bwd")
    prep_cts = [dkk_f + dkk_b, dw_f, db_f, dkt_f + dkt_m, dw_b, db_b, dkt_b, dg_rw]
    (dks, dlora, dw0_f, dw0_b, da0_f, da0_b, dwup_f, dwup_b, daup_f, daup_b, dgup, g["rwkv_k_k"],
     g["rwkv_k_a"]) = ew_backward(fn_rwkv_prepare, "rwkv_prepare_bwd", bsz, n_t, prep_ins(),
                                  [Tiled(c) for c in prep_cts], [True] * 13 + [False])
    g["rwkv_w0"] = jnp.concatenate([dw0_f, dw0_b], axis=0)
    g["rwkv_a0"] = jnp.concatenate([da0_f, da0_b], axis=0)
    g["rwkv_w_up"] = jnp.stack([dwup_f[:DECAY_LORA], dwup_b[:DECAY_LORA]])
    g["rwkv_a_up"] = jnp.stack([daup_f[DECAY_LORA:DECAY_LORA + AAA_LORA], daup_b[DECAY_LORA:DECAY_LORA + AAA_LORA]])
    g["rwkv_g_up"] = dgup[DECAY_LORA + AAA_LORA:]
    dps = jnp.concatenate([dr_f + dr_b + dr_m, dks, dv_f + dv_b + dv_m, dlora], axis=-1)
    dp_rw, g["rwkv_shift_mu"] = token_shift_bwd(dps, px_rw, wt["rwkv_shift_mu"], seg_first, seg_last)
    dpx = jnp.concatenate([(dqkv[0][k] + dqkv[1][k]).astype(MXU_DTYPE) for k in range(3)]
                          + [dg_ret.astype(MXU_DTYPE), dp_rw], axis=-1)
    dn1 = matmul(flat(dpx), wt["w_in"], "nt", "proj_in_dx").reshape(bsz, t_all, D_MODEL)
    g["w_in"] = matmul(flat(n1), flat(dpx), "tn", "proj_in_dw")
    dh, dshift1, dscale1, g["norm1_g"] = ew_backward(fn_norm_mod, "norm1_bwd", bsz, n_t, norm1_ins(), [Tiled(dn1)],
                                                     [True] * 4)
    grad_x = dh[:, t_c:] + dx_res
    zeros = jnp.zeros((D_MODEL,), f32)
    g["mod_x"] = jnp.stack([dshift1[:, 1, 0], dscale1[:, 1, 0], dgate1[:, 0, 0], dshift2[:, 0, 0], dscale2[:, 0, 0],
                            dgate2[:, 0, 0]], axis=1)
    g["mod_ctx"] = jnp.stack([dshift1[:, 0, 0].sum(0), dscale1[:, 0, 0].sum(0), zeros, zeros, zeros, zeros])
    return loss, grad_x, g


MESH_ID = pl.DeviceIdType.MESH
ALL_PEERS = [(dx, dy, dc) for dx in (0, 1) for dy in (0, 1) for dc in (0, 1)][1:]
CHIP_PEERS = [(1, 0, 0), (0, 1, 0), (1, 1, 0)]
CHIP_SLOTS = (0, 2, 4, 6)


def _mesh_pos():
    return lax.axis_index("x"), lax.axis_index("y"), lax.axis_index("c")


def _device_slot():
    x, y, c = _mesh_pos()
    return 4 * x + 2 * y + c


def sibling_swap(arrs, name):
    n = len(arrs)

    def body(*refs):
        in_refs, out_refs = refs[:n], refs[n:2 * n]
        send_sems, recv_sems = refs[2 * n:]
        x, y, c = _mesh_pos()
        copies = [pltpu.make_async_remote_copy(src_ref=in_refs[a], dst_ref=out_refs[a], send_sem=send_sems.at[a],
                                               recv_sem=recv_sems.at[a], device_id=(x, y, 1 - c),
                                               device_id_type=MESH_ID) for a in range(n)]
        for cp in copies:
            cp.start()
        for cp in copies:
            cp.wait()

    any_spec = pl.BlockSpec(memory_space=pl.ANY)
    res = pl.pallas_call(
        body, in_specs=[any_spec] * n, out_specs=[any_spec] * n,
        out_shape=[jax.ShapeDtypeStruct(a.shape, a.dtype) for a in arrs],
        scratch_shapes=[pltpu.SemaphoreType.DMA((n,)), pltpu.SemaphoreType.DMA((n,))],
        name=name)(*arrs)
    return list(res)


def exchange(arrs, gather, peers, name, by_chip=False, own=True):
    n, n_peers = len(arrs), len(peers)
    n_slots = N_SHARDS if by_chip else N_DEV
    slot = (lambda x, y, c: 2 * x + y) if by_chip else (lambda x, y, c: 4 * x + 2 * y + c)

    def body(*refs):
        in_refs, out_refs = refs[:n], refs[n:2 * n]
        send_sems, recv_sems, local_sems = refs[2 * n:]
        x, y, c = _mesh_pos()
        me = slot(x, y, c)
        copies, locals_ = [], []
        for a in range(n):
            if own:
                mine = in_refs[a] if gather else in_refs[a].at[me]
                loc = pltpu.make_async_copy(mine, out_refs[a].at[me], local_sems.at[a])
                loc.start()
                locals_.append(loc)
            for k, (dx, dy, dc) in enumerate(peers):
                peer = (1 - x if dx else x, 1 - y if dy else y, 1 - c if dc else c)
                src = in_refs[a] if gather else in_refs[a].at[slot(*peer)]
                sem = a * n_peers + k
                cp = pltpu.make_async_remote_copy(src_ref=src, dst_ref=out_refs[a].at[me], send_sem=send_sems.at[sem],
                                                  recv_sem=recv_sems.at[sem], device_id=peer, device_id_type=MESH_ID)
                cp.start()
                copies.append(cp)
        for cp in copies:
            cp.wait()
        for loc in locals_:
            loc.wait()

    any_spec = pl.BlockSpec(memory_space=pl.ANY)
    out_shape = [jax.ShapeDtypeStruct((n_slots,) + (a.shape if gather else a.shape[1:]), a.dtype) for a in arrs]
    n_sems = n * n_peers
    res = pl.pallas_call(
        body, in_specs=[any_spec] * n, out_specs=[any_spec] * n, out_shape=out_shape,
        scratch_shapes=[pltpu.SemaphoreType.DMA((n_sems,)), pltpu.SemaphoreType.DMA((n_sems,)),
                        pltpu.SemaphoreType.DMA((n,))],
        name=name)(*arrs)
    return list(res)


HBM_SPEC = pl.BlockSpec(memory_space=pltpu.HBM)
SEM_SPEC = pl.BlockSpec(memory_space=pltpu.SEMAPHORE)
DATAFLOW = pltpu.SideEffectType.DATAFLOW_SIDE_EFFECTING


def _peer_copies(src_refs, land_refs, send_sems, recv_sems, gather):
    x, y, c = _mesh_pos()
    me = 4 * x + 2 * y + c
    copies = []
    for a, (src_ref, land_ref) in enumerate(zip(src_refs, land_refs)):
        for k, (dx, dy, dc) in enumerate(ALL_PEERS):
            peer = (1 - x if dx else x, 1 - y if dy else y, 1 - c if dc else c)
            src = src_ref if gather else src_ref.at[4 * peer[0] + 2 * peer[1] + peer[2]]
            sem = a * len(ALL_PEERS) + k
            copies.append(pltpu.make_async_remote_copy(src_ref=src, dst_ref=land_ref.at[me], send_sem=send_sems.at[sem],
                                                       recv_sem=recv_sems.at[sem], device_id=peer,
                                                       device_id_type=MESH_ID))
    return copies


def exchange_start(arrs, gather, name):
    n = len(arrs)
    lands = [lax.empty((N_DEV,) + (a.shape if gather else a.shape[1:]), a.dtype) for a in arrs]

    def body(*refs):
        for cp in _peer_copies(refs[:n], refs[n:2 * n], refs[2 * n], refs[2 * n + 1], gather):
            cp.start()
        refs[-1][...] = jnp.zeros_like(refs[-1])

    sems = pltpu.SemaphoreType.DMA((n * len(ALL_PEERS),))
    hbm = [pltpu.HBM(a.shape, a.dtype) for a in arrs + lands]
    res = pl.pallas_call(
        body, name=name, out_shape=(sems, sems, *hbm, jax.ShapeDtypeStruct((8, 128), f32)),
        in_specs=[HBM_SPEC] * (2 * n),
        out_specs=(SEM_SPEC, SEM_SPEC, *[HBM_SPEC] * (2 * n), pl.BlockSpec(memory_space=pltpu.VMEM)),
        input_output_aliases={i: 2 + i for i in range(2 * n)},
        compiler_params=pltpu.CompilerParams(has_side_effects=DATAFLOW))(
        *[pltpu.with_memory_space_constraint(a, pltpu.HBM) for a in arrs + lands])
    return res[0], res[1], list(res[2:2 + n]), list(res[2 + n:2 + 2 * n]), res[-1]


def exchange_wait(started, after, gather, name):
    send_sems, recv_sems, srcs, lands, _ = started
    n = len(srcs)

    def body(*refs):
        for cp in _peer_copies(refs[:n], refs[n:2 * n], refs[2 * n], refs[2 * n + 1], gather):
            cp.wait_send()
            cp.wait_recv()

    res = pl.pallas_call(
        body, name=name, out_shape=tuple(pltpu.HBM(a.shape, a.dtype) for a in srcs + lands),
        in_specs=[HBM_SPEC] * (2 * n) + [SEM_SPEC, SEM_SPEC, pl.BlockSpec(memory_space=pl.ANY)],
        out_specs=tuple([HBM_SPEC] * (2 * n)), input_output_aliases={i: i for i in range(2 * n)},
        compiler_params=pltpu.CompilerParams(has_side_effects=DATAFLOW))(*srcs, *lands, send_sems, recv_sems, after)
    return list(res[n:])


def add_arrays(parts, name, out_dtype=f32):
    r, c = parts[0].shape
    tr = r
    for cand in (512, 256, 128, 64, 32, 16):
        if r % cand == 0:
            tr = cand
            break

    def body(*refs):
        acc = refs[0][...].astype(f32)
        for p_ref in refs[1:-1]:
            acc = acc + p_ref[...].astype(f32)
        refs[-1][...] = acc.astype(out_dtype)

    spec = pl.BlockSpec((tr, c), lambda i: (i, 0))
    return pl.pallas_call(body, grid=(r // tr,), in_specs=[spec] * len(parts), out_specs=spec,
                          out_shape=jax.ShapeDtypeStruct((r, c), out_dtype),
                          compiler_params=_cparams(("parallel",)), name=name)(*parts)


def gather_two_level(arrs, name):
    n = len(arrs)
    per = 7

    def body(*refs):
        in_refs, out_refs = refs[:n], refs[n:2 * n]
        send_sems, recv_sems = refs[2 * n:]
        x, y, c = _mesh_pos()
        me, sibling = (x, y, c), (x, y, 1 - c)
        chips = [(1 - x, y), (x, 1 - y), (1 - x, 1 - y)]

        def copy(a, k, block, to, src=None):
            rows = out_refs[a].at[4 * block[0] + 2 * block[1] + block[2]]
            return pltpu.make_async_remote_copy(src_ref=rows if src is None else src, dst_ref=rows,
                                                send_sem=send_sems.at[a * per + k], recv_sem=recv_sems.at[a * per + k],
                                                device_id=to, device_id_type=MESH_ID)

        first, passed = [], []
        for a in range(n):
            first.append(copy(a, 0, me, sibling, src=in_refs[a]))
            first += [copy(a, 1 + j, me, (*chip, c), src=in_refs[a]) for j, chip in enumerate(chips)]
        for cp in first:
            cp.start()
        for a in range(n):
            for j, chip in enumerate(chips):
                copy(a, 1 + j, (*chip, c), me).wait_recv()
                fwd = copy(a, 4 + j, (*chip, c), sibling)
                fwd.start()
                passed.append(fwd)
        for a in range(n):
            copy(a, 0, sibling, me).wait_recv()
            for j, chip in enumerate(chips):
                copy(a, 4 + j, (*chip, 1 - c), me).wait_recv()
        for cp in first + passed:
            cp.wait_send()

    any_spec = pl.BlockSpec(memory_space=pl.ANY)
    res = pl.pallas_call(
        body, in_specs=[any_spec] * n, out_specs=[any_spec] * n,
        out_shape=[jax.ShapeDtypeStruct((N_DEV,) + a.shape, a.dtype) for a in arrs],
        scratch_shapes=[pltpu.SemaphoreType.DMA((n * per,)), pltpu.SemaphoreType.DMA((n * per,))],
        name=name)(*arrs)
    return list(res)


def sum_slots(parts, slots, name):
    _, r, c = parts.shape
    tr = r
    for cand in (512, 256, 128, 64, 32, 16, 8):
        if r % cand == 0 and cand * c * 4 * len(slots) <= 8 * 1024 * 1024:
            tr = cand
            break

    def body(p_ref, o_ref):
        acc = p_ref[slots[0]].astype(f32)
        for s in slots[1:]:
            acc = acc + p_ref[s].astype(f32)
        o_ref[...] = acc

    return pl.pallas_call(body, grid=(r // tr,), in_specs=[pl.BlockSpec((parts.shape[0], tr, c), lambda i: (0, i, 0))],
                          out_specs=pl.BlockSpec((tr, c), lambda i: (i, 0)),
                          out_shape=jax.ShapeDtypeStruct((r, c), f32),
                          compiler_params=_cparams(("parallel",)), name=name)(parts)


def column_sum(a, name):
    def body(a_ref, o_ref):
        o_ref[...] = jnp.sum(a_ref[...], axis=0, keepdims=True)

    return pl.pallas_call(body, out_shape=jax.ShapeDtypeStruct((1, a.shape[1]), f32), name=name)(a)


def adamw(w, g, m, v, name):
    r, c = w.shape
    tr = r
    for cand in (256, 128, 64, 32, 16, 8):
        if r % cand == 0:
            tr = cand
            break

    def body(w_ref, g_ref, m_ref, v_ref, d_ref, mo_ref, vo_ref):
        gv = g_ref[...]
        m_new = ADAM_B1 * m_ref[...] + (1.0 - ADAM_B1) * gv
        v_new = ADAM_B2 * v_ref[...] + (1.0 - ADAM_B2) * jnp.square(gv)
        m_hat = m_new / (1.0 - ADAM_B1 ** ADAM_STEP)
        v_hat = v_new / (1.0 - ADAM_B2 ** ADAM_STEP)
        d_ref[...] = -ADAM_LR * (m_hat / (jnp.sqrt(v_hat) + ADAM_EPS) + ADAM_WD * w_ref[...])
        mo_ref[...] = m_new
        vo_ref[...] = v_new

    spec = pl.BlockSpec((tr, c), lambda i: (i, 0))
    return pl.pallas_call(body, grid=(r // tr,), in_specs=[spec] * 4, out_specs=[spec] * 3,
                          out_shape=[jax.ShapeDtypeStruct((r, c), f32)] * 3,
                          compiler_params=_cparams(("parallel",)), name=name)(w, g, m, v)


def adaln_fwd(c_rows, w, b):
    def body(c_ref, w_ref, b_ref, o_ref):
        cv = c_ref[...]
        o_ref[...] = _mxu_dot(cv * jax.nn.sigmoid(cv), w_ref[...]) + b_ref[...]

    return pl.pallas_call(body, out_shape=jax.ShapeDtypeStruct((c_rows.shape[0], w.shape[1]), f32),
                          compiler_params=pltpu.CompilerParams(vmem_limit_bytes=VMEM_LIMIT), name="adaln_fwd")(c_rows, w, b)


def adaln_bwd(c_rows, dm, w):
    def body(c_ref, dm_ref, w_ref, gw_ref, ds_ref):
        cv = c_ref[...]
        gw_ref[...] = _dg(cv * jax.nn.sigmoid(cv), dm_ref[...], 0, 0)
        ds_ref[...] = _dg(dm_ref[...], w_ref[...], 1, 1)

    return pl.pallas_call(body, out_shape=[jax.ShapeDtypeStruct(w.shape, f32),
                                           jax.ShapeDtypeStruct(c_rows.shape, f32)],
                          compiler_params=pltpu.CompilerParams(vmem_limit_bytes=VMEM_LIMIT), name="adaln_bwd")(c_rows, dm, w)


def c_ctx_grad(parts, c_ctx_row):
    def body(p_ref, c_ref, o_ref):
        total = p_ref[0, 0:1, :]
        for s in range(1, N_SHARDS):
            total = total + p_ref[s, 0:1, :]
        _, vjp = jax.vjp(jax.nn.silu, c_ref[...])
        o_ref[...] = vjp(total)[0]

    return pl.pallas_call(body, out_shape=jax.ShapeDtypeStruct((1, D_MODEL), f32), name="c_ctx_grad")(parts, c_ctx_row)


PACK_W = 1024
PACK_ROWS = 8


def _pack(arrs):
    pieces, layout, r0 = [], [], 0
    for a in arrs:
        size = math.prod(a.shape)
        rows = -(-size // (PACK_W * PACK_ROWS)) * PACK_ROWS
        pieces.append(jnp.pad(a.reshape(-1).astype(f32), (0, rows * PACK_W - size)).reshape(rows, PACK_W))
        layout.append((r0, rows, a.shape))
        r0 += rows
    return jnp.concatenate(pieces, axis=0), layout


def _unpack(pack, layout, lead=()):
    n_lead = len(lead)
    outs = []
    for r0, rows, shape in layout:
        piece = pack[(slice(None),) * n_lead + (slice(r0, r0 + rows),)].reshape(lead + (-1,))
        outs.append(piece[..., :math.prod(shape)].reshape(lead + tuple(shape)))
    return outs


W_NAMES = ("c_ctx", "w_ada", "b_ada", "norm1_g", "norm2_g", "w_in", "ret_log_decay", "rwkv_shift_mu", "rwkv_w0",
           "rwkv_w_up", "rwkv_a0", "rwkv_a_up", "rwkv_g_up", "rwkv_k_k", "rwkv_k_a", "rwkv_r_k", "rwkv_ln_w",
           "rwkv_ln_b", "w_out", "w_ff1", "b_ff1", "w_ff2", "b_ff2", "final_g")
COL_SHARDED = ("w_in", "w_ff1")
ROW_SHARDED = ("w_out", "w_ff2")
LAST_SHARDED = ("rwkv_shift_mu", "rwkv_w0", "rwkv_w_up", "rwkv_a0", "rwkv_a_up", "rwkv_g_up")
REPLICATED = ("c_ctx", "b_ada", "norm1_g", "norm2_g", "ret_log_decay", "rwkv_k_k", "rwkv_k_a", "rwkv_r_k",
              "rwkv_ln_w", "rwkv_ln_b", "b_ff1", "b_ff2", "final_g")
N_SHARDS = 4


def _train_step(a):
    x, c, ctx, tgt = a["x"], a["c"], a["ctx"], a["loss_target"]
    bsz = x.shape[0]
    mx, my, mc = _mesh_pos()
    shard = 2 * mx + my
    dev = _device_slot()

    (c_all,) = exchange([jnp.pad(c, ((0, PACK_ROWS - bsz), (0, 0)))], True, ALL_PEERS, "gather_c")
    n_ex = N_DEV * bsz
    c_rows = jnp.concatenate([c_all[:, :bsz].reshape(n_ex, D_MODEL), a["c_ctx"][None, :],
                              jnp.zeros((PACK_ROWS - 1, D_MODEL), f32)], axis=0)
    ada_cols = a["w_ada"].shape[-1]
    b_ada_cols = lax.dynamic_slice_in_dim(a["b_ada"], shard * ada_cols, ada_cols, axis=1)
    mod_cols = adaln_fwd(c_rows, a["w_ada"][0], b_ada_cols)

    def own_half(n):
        w = a[n][0].astype(MXU_DTYPE)
        return lax.dynamic_slice_in_dim(w, mc * (w.shape[0] // 2), w.shape[0] // 2, axis=0)

    def whole_weight(n, gth, own):
        per_chip = lax.dynamic_update_index_in_dim(gth, own, dev, 0).reshape(N_SHARDS, -1, gth.shape[-1])
        return (per_chip.transpose(1, 0, 2).reshape(per_chip.shape[1], -1) if n in COL_SHARDED
                else per_chip.reshape(-1, per_chip.shape[-1]))

    small_pack, small_layout = _pack([a[n][0] for n in LAST_SHARDED])
    own_blocks = [mod_cols, own_half("w_in"), small_pack]
    gathered = gather_two_level(own_blocks, "gather_weights")
    late_own = [own_half(n) for n in LATE_WEIGHTS]
    late_started = exchange_start(late_own, True, "gather_late_start")
    mod_own = lax.dynamic_update_index_in_dim(gathered[0], mod_cols, dev, 0)
    mod_all = jnp.stack([mod_own[s] for s in CHIP_SLOTS], axis=1).reshape(c_rows.shape[0], -1)
    mod_all = mod_all + late_started[-1][0, 0]
    mod_x = lax.dynamic_slice_in_dim(mod_all, dev * bsz, bsz, axis=0).reshape(bsz, 6, D_MODEL)
    mod_ctx = mod_all[n_ex].reshape(6, D_MODEL)
    wt = {"w_in": whole_weight("w_in", gathered[1], own_blocks[1])}

    def late_weights(after):
        lands = exchange_wait(late_started, after, True, "gather_late_wait")
        return {n: whole_weight(n, land, own) for n, land, own in zip(LATE_WEIGHTS, lands, late_own)}

    def grad_blocks(n, gw):
        if n in COL_SHARDED:
            gw = gw.reshape(gw.shape[0], N_SHARDS, -1).transpose(1, 0, 2)
        return gw.reshape(N_DEV, -1, gw.shape[-1]).astype(MXU_DTYPE)

    late_sent = {}

    def early_grads(late_g):
        late_sent["blocks"] = [grad_blocks(n, late_g[n]) for n in LATE_WEIGHTS]
        late_sent["started"] = exchange_start(late_sent["blocks"], False, "scatter_late_start")
        return late_sent["started"][-1]

    small_own = lax.dynamic_update_index_in_dim(gathered[2], small_pack, dev, 0)
    small_by_chip = _unpack(jnp.stack([small_own[s] for s in CHIP_SLOTS]), small_layout, (N_SHARDS,))
    for n, parts in zip(LAST_SHARDED, small_by_chip):
        wt[n] = jnp.concatenate([parts[s] for s in range(N_SHARDS)], axis=-1)
    for n in ("norm1_g", "norm2_g", "rwkv_k_k", "rwkv_k_a", "rwkv_r_k", "rwkv_ln_w", "rwkv_ln_b", "b_ff1", "b_ff2"):
        wt[n] = a[n]
    wt["ret_log_decay"] = a["ret_log_decay"][0]
    wt["final_g"] = a["final_g"][None, :]

    loss, grad_x, g = layer_step(x, ctx, tgt, mod_x, mod_ctx, wt, late_weights, early_grads)

    small_names = [n for n in REPLICATED if n not in ("c_ctx", "b_ada")]
    g_pack, g_layout = _pack([jnp.pad(loss, ((0, 0), (0, PACK_W - loss.shape[1])))] + [g[n] for n in small_names]
                             + [g["mod_x"], g["mod_ctx"]])
    (g_packs,) = gather_two_level([g_pack], "gather_small_grads")
    g_packs = lax.dynamic_update_index_in_dim(g_packs, g_pack, dev, 0)
    g_sum = _unpack(sum_slots(g_packs, tuple(range(N_DEV)), "sum_small_grads"), g_layout)
    loss_total = g_sum[0][0, 0]
    grads = dict(zip(small_names, g_sum[1:1 + len(small_names)]))
    dmod_ctx = g_sum[-1].reshape(1, -1)
    dmod_x = _unpack(g_packs, g_layout, (N_DEV,))[-2].reshape(n_ex, -1)
    dmod = jnp.concatenate([dmod_x, dmod_ctx, jnp.zeros((PACK_ROWS - 1, dmod_x.shape[1]), f32)], axis=0)
    grads["b_ada"] = column_sum(dmod, "b_ada_grad")
    dmod_cols = lax.dynamic_slice_in_dim(dmod, shard * ada_cols, ada_cols, axis=1)
    grads["w_ada"], dsilu = adaln_bwd(c_rows, dmod_cols, a["w_ada"][0])

    blocks = [grad_blocks("w_in", g["w_in"])]
    shard_packs = []
    for s in range(N_SHARDS):
        pieces_s = [lax.slice_in_dim(g[n], s * a[n].shape[-1], (s + 1) * a[n].shape[-1], axis=g[n].ndim - 1)
                    for n in LAST_SHARDED]
        pack_s, shard_layout = _pack(pieces_s)
        shard_packs.append(jnp.pad(pack_s, ((0, -pack_s.shape[0] % (2 * PACK_ROWS)), (0, 0))))
    blocks.append(jnp.stack(shard_packs).reshape(N_DEV, -1, PACK_W))
    scattered = ("w_in", "small_shards")
    halves_of = lambda blk, core: lax.dynamic_index_in_dim(
        blk.reshape(N_SHARDS, 2, *blk.shape[1:]), core, axis=1, keepdims=False).reshape(-1, blk.shape[-1])
    from_sibling = sibling_swap([halves_of(blk, 1 - mc) for blk in blocks], "prereduce_swap")
    chip_sums = [add_arrays([halves_of(blk, mc), got], f"prereduce_{n}", blk.dtype).reshape(N_SHARDS, -1, blk.shape[-1])
                 for n, blk, got in zip(scattered, blocks, from_sibling)]
    dsilu_rows = jnp.broadcast_to(jnp.pad(dsilu[n_ex:n_ex + 1], ((0, PACK_ROWS - 1), (0, 0)))[None],
                                  (N_SHARDS, PACK_ROWS, D_MODEL))
    to_chips = [dsilu_rows] + chip_sums
    received = exchange(to_chips, False, CHIP_PEERS, "scatter_big_grads", by_chip=True, own=False)
    received = [lax.dynamic_update_index_in_dim(got, lax.dynamic_index_in_dim(sent, shard, 0, keepdims=False), shard, 0)
                for got, sent in zip(received, to_chips)]
    grads["c_ctx"] = c_ctx_grad(received[0], a["c_ctx"][None, :])
    half_sums = [sum_slots(p, tuple(range(N_SHARDS)), f"sum_{n}") for n, p in zip(scattered, received[1:])]
    late_lands = exchange_wait(late_sent["started"], half_sums[0], False, "scatter_late_wait")
    for n, land, sent in zip(LATE_WEIGHTS, late_lands, late_sent["blocks"]):
        land = lax.dynamic_update_index_in_dim(land, lax.dynamic_index_in_dim(sent, dev, 0, keepdims=False), dev, 0)
        half_sums.append(sum_slots(land, tuple(range(N_DEV)), f"sum_{n}"))
    scattered = scattered + LATE_WEIGHTS
    other_halves = sibling_swap(half_sums, "swap_halves")
    for n, mine, other in zip(scattered, half_sums, other_halves):
        rows = mine.shape[0]
        whole = jnp.zeros((2 * rows, mine.shape[1]), f32)
        whole = lax.dynamic_update_slice_in_dim(whole, mine, mc * rows, axis=0)
        grads[n] = lax.dynamic_update_slice_in_dim(whole, other, (1 - mc) * rows, axis=0)
    grads.update(zip(LAST_SHARDED, _unpack(grads.pop("small_shards"), shard_layout)))

    out_g, out_d, out_m, out_v = {}, {}, {}, {}
    for n in ("w_ada",) + COL_SHARDED + ROW_SHARDED:
        out_g[n] = grads[n].reshape(a[n].shape)
        two_d = lambda z: z.reshape(-1, z.shape[-1])
        d, m, v = adamw(two_d(a[n]), two_d(out_g[n]), two_d(a["m_" + n]), two_d(a["v_" + n]), f"adamw_{n}")
        out_d[n], out_m[n], out_v[n] = d.reshape(a[n].shape), m.reshape(a[n].shape), v.reshape(a[n].shape)
    rest = REPLICATED + LAST_SHARDED
    for n in rest:
        out_g[n] = grads[n].reshape(a[n].shape)
    packs = [_pack([src[n] for n in rest])[0] for src in
             ({n: a[n] for n in rest}, out_g, {n: a["m_" + n] for n in rest}, {n: a["v_" + n] for n in rest})]
    _, rest_layout = _pack([a[n] for n in rest])
    for dst, pack in zip((out_d, out_m, out_v), adamw(*packs, "adamw_small")):
        dst.update(zip(rest, _unpack(pack, rest_layout)))
    return (loss_total, grad_x, *[out_g[n] for n in W_NAMES], *[out_d[n] for n in W_NAMES],
            *[out_m[n] for n in W_NAMES], *[out_v[n] for n in W_NAMES])


def kernel(x, c, ctx, c_ctx, w_ada, b_ada, norm1_g, norm2_g, w_in, ret_log_decay, rwkv_shift_mu, rwkv_w0, rwkv_w_up, rwkv_a0, rwkv_a_up, rwkv_g_up, rwkv_k_k, rwkv_k_a, rwkv_r_k, rwkv_ln_w, rwkv_ln_b, w_out, w_ff1, b_ff1, w_ff2, b_ff2, final_g, loss_target, m_c_ctx, m_w_ada, m_b_ada, m_norm1_g, m_norm2_g, m_w_in, m_ret_log_decay, m_rwkv_shift_mu, m_rwkv_w0, m_rwkv_w_up, m_rwkv_a0, m_rwkv_a_up, m_rwkv_g_up, m_rwkv_k_k, m_rwkv_k_a, m_rwkv_r_k, m_rwkv_ln_w, m_rwkv_ln_b, m_w_out, m_w_ff1, m_b_ff1, m_w_ff2, m_b_ff2, m_final_g, v_c_ctx, v_w_ada, v_b_ada, v_norm1_g, v_norm2_g, v_w_in, v_ret_log_decay, v_rwkv_shift_mu, v_rwkv_w0, v_rwkv_w_up, v_rwkv_a0, v_rwkv_a_up, v_rwkv_g_up, v_rwkv_k_k, v_rwkv_k_a, v_rwkv_r_k, v_rwkv_ln_w, v_rwkv_ln_b, v_w_out, v_w_ff1, v_b_ff1, v_w_ff2, v_b_ff2, v_final_g):
    return _train_step(dict(locals()))
```

```python
import functools
import math

import jax
import jax.numpy as jnp
from jax import lax
from jax.experimental import pallas as pl
from jax.experimental.pallas import tpu as pltpu

f32 = jnp.float32
MXU_DTYPE = jnp.bfloat16

D_MODEL = 1024
RET_W = 512
RET_HEADS = 4
RET_DH = 128
RET_CHUNK = 128
RW_W = 512
RW_N = 64
DECAY_LORA = 64
AAA_LORA = 64
GATE_LORA = 128
LORA_W = DECAY_LORA + AAA_LORA + GATE_LORA
D_FF = 4096
RET_COLS = 4 * RET_W
SHIFT_COLS = 3 * RW_W + LORA_W
IN_COLS = RET_COLS + SHIFT_COLS
GRID_W = 64
ROPE_BASE = 10000.0
NORM_EPS = 1e-6
GN_EPS = 64e-5
W_DECAY_SCALE = math.exp(-0.5)
ADAM_LR, ADAM_B1, ADAM_B2, ADAM_EPS, ADAM_WD, ADAM_STEP = 0.001, 0.9, 0.999, 1e-08, 0.01, 10

TOK_TILE = 256
MATMUL_TILE = 1024
SCAN_CHUNK = 16
SCAN_UNROLL = SCAN_CHUNK
N_DEV = 8
V7X_VMEM_BYTES = 64 * 1024 * 1024
VMEM_LIMIT = V7X_VMEM_BYTES * 7 // 8


def _cparams(sem):
    return pltpu.CompilerParams(dimension_semantics=sem, vmem_limit_bytes=VMEM_LIMIT)


def _tile(n, cap):
    best = None
    for t in range(128, min(n, cap) + 1, 128):
        if n % t == 0:
            best = t
    return best if best is not None else n


def matmul(a, b, mode, name, out_dtype=f32, bias=None, finish=None):
    if mode == "nn":
        (m, k), (k2, n) = a.shape, b.shape
    elif mode == "nt":
        (m, k), (n, k2) = a.shape, b.shape
    else:
        (k, m), (k2, n) = a.shape, b.shape
    assert k == k2, (a.shape, b.shape, mode)
    tm, tn, tk = _tile(m, MATMUL_TILE), _tile(n, MATMUL_TILE), _tile(k, MATMUL_TILE)
    nk = k // tk
    dims = {"nn": ((1,), (0,)), "nt": ((1,), (1,)), "tn": ((0,), (0,))}[mode]

    def body(a_ref, b_ref, *rest):
        o_ref, acc_ref = rest[-2:]
        kk = pl.program_id(2)

        @pl.when(kk == 0)
        def _():
            acc_ref[...] = jnp.zeros_like(acc_ref)

        acc_ref[...] += lax.dot_general(a_ref[...].astype(MXU_DTYPE), b_ref[...].astype(MXU_DTYPE),
                                        (dims, ((), ())), preferred_element_type=f32)

        @pl.when(kk == nk - 1)
        def _():
            res = acc_ref[...]
            if bias is not None:
                res = res + rest[0][...]
            if finish is not None:
                res = finish(res)
            o_ref[...] = res.astype(o_ref.dtype)

    if mode == "nn":
        a_spec = pl.BlockSpec((tm, tk), lambda i, j, q: (i, q))
        b_spec = pl.BlockSpec((tk, tn), lambda i, j, q: (q, j))
    elif mode == "nt":
        a_spec = pl.BlockSpec((tm, tk), lambda i, j, q: (i, q))
        b_spec = pl.BlockSpec((tn, tk), lambda i, j, q: (j, q))
    else:
        a_spec = pl.BlockSpec((tk, tm), lambda i, j, q: (q, i))
        b_spec = pl.BlockSpec((tk, tn), lambda i, j, q: (q, j))
    extra_specs = [] if bias is None else [pl.BlockSpec((1, tn), lambda i, j, q: (0, j))]
    extra = [] if bias is None else [bias]
    return pl.pallas_call(
        body, grid=(m // tm, n // tn, nk), in_specs=[a_spec, b_spec] + extra_specs,
        out_specs=pl.BlockSpec((tm, tn), lambda i, j, q: (i, j)),
        out_shape=jax.ShapeDtypeStruct((m, n), out_dtype),
        scratch_shapes=[pltpu.VMEM((tm, tn), f32)],
        compiler_params=_cparams(("parallel", "parallel", "arbitrary")), name=name)(a, b, *extra)


class Tiled:
    def __init__(self, arr, w=None, cidx=0, toff=0):
        self.arr, self.w, self.cidx, self.toff = arr, (arr.shape[-1] if w is None else w), cidx, toff

    def spec(self):
        cidx, toff = self.cidx, self.toff
        return pl.BlockSpec((None, TOK_TILE, self.w), lambda b, i: (b, jnp.maximum(i + toff, 0), cidx))


class Seg:
    def __init__(self, arr, seg, first):
        self.arr, self.seg, self.first = arr, seg, first

    def spec(self):
        seg = self.seg
        return pl.BlockSpec((None, None, 1, self.arr.shape[-1]), lambda b, i: (b, seg(i), 0, 0))


class Glob:
    def __init__(self, arr):
        self.arr = arr

    def spec(self):
        return pl.BlockSpec(self.arr.shape, lambda b, i: (0,) * self.arr.ndim)


def ew_forward(fn, name, bsz, n_tiles, ins, outs):
    n_in = len(ins)

    def body(*refs):
        res = fn(*[r[...] for r in refs[:n_in]])
        for o_ref, o in zip(refs[n_in:], res):
            o_ref[...] = o.astype(o_ref.dtype)

    out_specs = [pl.BlockSpec((None, TOK_TILE, w), lambda b, i: (b, i, 0)) for w, _ in outs]
    out_shape = [jax.ShapeDtypeStruct((bsz, n_tiles * TOK_TILE, w), dt) for w, dt in outs]
    return pl.pallas_call(body, grid=(bsz, n_tiles), in_specs=[d.spec() for d in ins], out_specs=out_specs,
                          out_shape=out_shape, compiler_params=_cparams(("parallel", "parallel")), name=name)(
        *[d.arr for d in ins])


def ew_backward(fn, name, bsz, n_tiles, ins, cts, want, grad_dtypes=None, lead=0):
    n_in, n_ct = len(ins), len(cts)
    diff = [k for k in range(n_in) if want[k]]
    grad_dtypes = grad_dtypes or {}
    assert lead == 0 or not any(isinstance(ins[k], Seg) for k in diff)

    def body(*refs):
        b, i = pl.program_id(0), pl.program_id(1)
        g_refs = refs[n_in + n_ct:]

        def tile_grads():
            vals = [r[...] for r in refs[:n_in]]
            ct_vals = tuple(r[...].astype(f32) for r in refs[n_in:n_in + n_ct])

            def f(*dvals):
                full = list(vals)
                for k, v in zip(diff, dvals):
                    full[k] = v
                return tuple(fn(*full))

            _, vjp = jax.vjp(f, *[vals[k] for k in diff])
            grads = vjp(ct_vals)
            for k, g_ref, g in zip(diff, g_refs, grads):
                d = ins[k]
                if isinstance(d, Tiled):
                    g_ref[...] = g.astype(g_ref.dtype)
                else:
                    zero = d.first(i) if isinstance(d, Seg) else jnp.logical_and(b == 0, i == lead)

                    @pl.when(zero)
                    def _(g_ref=g_ref):
                        g_ref[...] = jnp.zeros_like(g_ref)

                    g_ref[...] += g

        if lead == 0:
            tile_grads()
        else:
            pl.when(i >= lead)(tile_grads)

            @pl.when(i < lead)
            def _():
                for k, g_ref in zip(diff, g_refs):
                    if isinstance(ins[k], Tiled):
                        g_ref[...] = jnp.zeros_like(g_ref)

    out_specs, out_shape = [], []
    for k in diff:
        d = ins[k]
        if isinstance(d, Tiled):
            out_specs.append(pl.BlockSpec((None, TOK_TILE, d.w), lambda b, i: (b, i, 0)))
            out_shape.append(jax.ShapeDtypeStruct((bsz, (n_tiles + lead) * TOK_TILE, d.w), grad_dtypes.get(k, f32)))
        else:
            out_specs.append(d.spec())
            out_shape.append(jax.ShapeDtypeStruct(d.arr.shape, f32))
    return pl.pallas_call(body, grid=(bsz, n_tiles + lead),
                          in_specs=[d.spec() for d in ins] + [c.spec() for c in cts],
                          out_specs=out_specs, out_shape=out_shape,
                          compiler_params=_cparams(("arbitrary", "arbitrary")), name=name)(
        *[d.arr for d in ins], *[c.arr for c in cts])


@jax.custom_vjp
def _mxu_dot(a, b):
    return jnp.dot(a.astype(MXU_DTYPE), b.astype(MXU_DTYPE), preferred_element_type=f32)


def _mxu_dot_fwd(a, b):
    return _mxu_dot(a, b), (a, b)


def _mxu_dot_bwd(res, ct):
    a, b = res
    ct = ct.astype(MXU_DTYPE)
    da = lax.dot_general(ct, b.astype(MXU_DTYPE), (((1,), (1,)), ((), ())), preferred_element_type=f32)
    db = lax.dot_general(a.astype(MXU_DTYPE), ct, (((0,), (0,)), ((), ())), preferred_element_type=f32)
    return da, db


_mxu_dot.defvjp(_mxu_dot_fwd, _mxu_dot_bwd)


def _split_dot_impl(x, ones_mat):
    hi = x.astype(MXU_DTYPE)
    lo = (x - hi.astype(f32)).astype(MXU_DTYPE)
    return jnp.dot(hi, ones_mat, preferred_element_type=f32) + jnp.dot(lo, ones_mat, preferred_element_type=f32)


@jax.custom_vjp
def _split_dot(x, ones_mat):
    return _split_dot_impl(x, ones_mat)


def _split_dot_fwd(x, ones_mat):
    return _split_dot_impl(x, ones_mat), ones_mat


def _split_dot_bwd(ones_mat, ct):
    return _split_dot_impl(ct, ones_mat), None


_split_dot.defvjp(_split_dot_fwd, _split_dot_bwd)


def _block_ones(n, group):
    idx = jnp.arange(n) // group
    return (idx[:, None] == idx[None, :]).astype(MXU_DTYPE)


def _rms(x, g):
    return x * lax.rsqrt(jnp.mean(x * x, axis=-1, keepdims=True) + NORM_EPS) * g


def fn_norm_mod(h, shift, scale, g):
    return (_rms(h, g) * (1.0 + scale) + shift,)


def fn_rwkv_prepare(ks, lora, w0_f, w0_b, a0_f, a0_b, w_up_f, w_up_b, a_up_f, a_up_b, g_up, k_k, k_a, ones64):
    kkr = ks * k_k
    kk = kkr * lax.rsqrt(_split_dot(kkr * kkr, ones64) + 1e-12)
    outs = [kk]
    th = jnp.tanh(lora)
    for w0, a0, w_up, a_up in ((w0_f, a0_f, w_up_f, a_up_f), (w0_b, a0_b, w_up_b, a_up_b)):
        w = jnp.exp(-W_DECAY_SCALE * jax.nn.sigmoid(w0 + _mxu_dot(th, w_up)))
        a = jax.nn.sigmoid(a0 + _mxu_dot(lora, a_up))
        kt = ks * (1.0 + (a - 1.0) * k_a)
        outs += [w, a * kk, kt]
    outs.append(_mxu_dot(jax.nn.sigmoid(lora), g_up))
    return tuple(outs)


def fn_merge(o_f, o_b, g_ret, y_f, y_b, r, kt_f, v, g_rw, r_k, ln_w, ln_b, ones64, ones128):
    o = o_f + o_b
    ret = o * lax.rsqrt(_split_dot(o * o, ones128) * (1.0 / RET_DH) + NORM_EPS) * (g_ret * jax.nn.sigmoid(g_ret))
    y = y_f + y_b
    mean = _split_dot(y, ones64) * (1.0 / RW_N)
    yc = y - mean
    var = _split_dot(yc * yc, ones64) * (1.0 / RW_N)
    y_n = yc * lax.rsqrt(var + GN_EPS) * ln_w + ln_b
    bonus = _split_dot(r * kt_f * r_k, ones64) * v
    return ret, (y_n + bonus) * g_rw


def fn_resid_norm_mod(x, mix, gate, shift, scale, g):
    h1 = x + gate * mix
    return h1, _rms(h1, g) * (1.0 + scale) + shift


def relu2(z):
    return jnp.square(jnp.maximum(z, 0.0))


def relu2_backward(act, dact, name):
    bsz, n_tok, width = act.shape

    def body(a_ref, d_ref, du_ref, db_ref):
        du = d_ref[...].astype(f32) * (2.0 * jnp.sqrt(a_ref[...].astype(f32)))
        du_ref[...] = du.astype(du_ref.dtype)

        @pl.when(jnp.logical_and(pl.program_id(0) == 0, pl.program_id(1) == 0))
        def _():
            db_ref[...] = jnp.zeros_like(db_ref)

        db_ref[...] += jnp.sum(du, axis=0, keepdims=True)

    tile = pl.BlockSpec((None, TOK_TILE, width), lambda b, i: (b, i, 0))
    row = pl.BlockSpec((1, width), lambda b, i: (0, 0))
    return pl.pallas_call(body, grid=(bsz, n_tok // TOK_TILE), in_specs=[tile, tile], out_specs=[tile, row],
                          out_shape=[jax.ShapeDtypeStruct(act.shape, MXU_DTYPE), jax.ShapeDtypeStruct((1, width), f32)],
                          compiler_params=_cparams(("arbitrary", "arbitrary")), name=name)(act, dact)


def fn_loss(h1, f, tgt, gate, b2, g):
    y = _rms(h1 + gate * (f + b2), g)
    err = jnp.square(y - tgt)
    return 0.5 * jnp.sum(jnp.mean(err, axis=-1, keepdims=True), axis=0, keepdims=True)


def loss_and_grads(h1, f, tgt, gate, b2, g, bsz, n_tiles):
    def body(h1_ref, f_ref, t_ref, gate_ref, b2_ref, g_ref, loss_ref, dh1_ref, df_ref, dgate_ref, db2_ref, dg_ref):
        b, i = pl.program_id(0), pl.program_id(1)
        tgt_v = t_ref[...]
        loss, vjp = jax.vjp(lambda a, c, e, p, q: fn_loss(a, c, tgt_v, e, p, q),
                            h1_ref[...], f_ref[...], gate_ref[...], b2_ref[...], g_ref[...])
        dh1, df, dgate, db2, dg = vjp(jnp.ones((1, 1), f32))
        dh1_ref[...] = dh1
        df_ref[...] = df.astype(df_ref.dtype)

        @pl.when(i == 0)
        def _():
            dgate_ref[...] = jnp.zeros_like(dgate_ref)

        @pl.when(jnp.logical_and(b == 0, i == 0))
        def _():
            loss_ref[...] = jnp.zeros_like(loss_ref)
            db2_ref[...] = jnp.zeros_like(db2_ref)
            dg_ref[...] = jnp.zeros_like(dg_ref)

        dgate_ref[...] += dgate
        db2_ref[...] += db2
        dg_ref[...] += dg
        loss_ref[...] += jnp.broadcast_to(loss, loss_ref.shape)

    tile = pl.BlockSpec((None, TOK_TILE, D_MODEL), lambda b, i: (b, i, 0))
    row = pl.BlockSpec((1, D_MODEL), lambda b, i: (0, 0))
    seg = pl.BlockSpec((None, None, 1, D_MODEL), lambda b, i: (b, 0, 0, 0))
    t_tok = n_tiles * TOK_TILE
    return pl.pallas_call(
        body, grid=(bsz, n_tiles), in_specs=[tile, tile, tile, seg, row, row],
        out_specs=[pl.BlockSpec((1, 128), lambda b, i: (0, 0)), tile, tile, seg, row, row],
        out_shape=[jax.ShapeDtypeStruct((1, 128), f32), jax.ShapeDtypeStruct((bsz, t_tok, D_MODEL), f32),
                   jax.ShapeDtypeStruct((bsz, t_tok, D_MODEL), MXU_DTYPE),
                   jax.ShapeDtypeStruct((bsz, 1, 1, D_MODEL), f32),
                   jax.ShapeDtypeStruct((1, D_MODEL), f32), jax.ShapeDtypeStruct((1, D_MODEL), f32)],
        compiler_params=_cparams(("arbitrary", "arbitrary")), name="loss_and_grads")(h1, f, tgt, gate, b2, g)


SHIFT_BLOCK = SHIFT_COLS
HALO_ROWS = 8


def _shift_specs(n_tok, col0):
    per_tile = TOK_TILE // HALO_ROWS
    last = n_tok // HALO_ROWS - 1
    tile = pl.BlockSpec((None, TOK_TILE, SHIFT_BLOCK), lambda j, b, i: (b, i, col0 + j))
    prev = pl.BlockSpec((None, HALO_ROWS, SHIFT_BLOCK),
                        lambda j, b, i: (b, jnp.maximum(i * per_tile - 1, 0), col0 + j))
    nxt = pl.BlockSpec((None, HALO_ROWS, SHIFT_BLOCK),
                       lambda j, b, i: (b, jnp.minimum((i + 1) * per_tile, last), col0 + j))
    return tile, prev, nxt


def _shifted(p, prev_ref, next_ref, is_first, is_last):
    row = lax.broadcasted_iota(jnp.int32, p.shape, 0)
    prev_row = jnp.where(is_first, 0.0, prev_ref[HALO_ROWS - 1:HALO_ROWS, :].astype(f32))
    next_row = jnp.where(is_last, 0.0, next_ref[0:1, :].astype(f32))
    prev = jnp.where(row == 0, prev_row, pltpu.roll(p, 1, axis=0))
    nxt = jnp.where(row == TOK_TILE - 1, next_row, pltpu.roll(p, TOK_TILE - 1, axis=0))
    return prev, nxt


def token_shift(px, mu, seg_first, seg_last):
    bsz, n_tok, _ = px.shape
    n_tiles = n_tok // TOK_TILE

    def body(p_ref, prev_ref, next_ref, mu_ref, o_ref):
        i = pl.program_id(2)
        p = p_ref[...]
        prev, nxt = _shifted(p, prev_ref, next_ref, seg_first(i), seg_last(i))
        o_ref[...] = p + mu_ref[0:1, :] * (prev - p) + mu_ref[1:2, :] * (nxt - p)

    tile, prev, nxt = _shift_specs(n_tok, 0)
    return pl.pallas_call(
        body, grid=(SHIFT_COLS // SHIFT_BLOCK, bsz, n_tiles),
        in_specs=[tile, prev, nxt, pl.BlockSpec((2, SHIFT_BLOCK), lambda j, b, i: (0, j))],
        out_specs=pl.BlockSpec((None, TOK_TILE, SHIFT_BLOCK), lambda j, b, i: (b, i, j)),
        out_shape=jax.ShapeDtypeStruct((bsz, n_tok, SHIFT_COLS), f32),
        compiler_params=_cparams(("parallel", "parallel", "parallel")), name="token_shift")(px, px, px, mu)


def token_shift_bwd(dps, px, mu, seg_first, seg_last):
    bsz, n_tok, _ = px.shape
    n_tiles = n_tok // TOK_TILE

    def body(d_ref, dprev_ref, dnext_ref, p_ref, prev_ref, next_ref, mu_ref, dp_ref, dmu_ref):
        b, i = pl.program_id(1), pl.program_id(2)
        first, last = seg_first(i), seg_last(i)
        d, p = d_ref[...], p_ref[...]
        d_prev, d_next = _shifted(d, dprev_ref, dnext_ref, first, last)
        p_prev, p_next = _shifted(p, prev_ref, next_ref, first, last)
        mu0, mu1 = mu_ref[0:1, :], mu_ref[1:2, :]
        dp_ref[...] = (d + mu0 * (d_next - d) + mu1 * (d_prev - d)).astype(dp_ref.dtype)

        @pl.when(jnp.logical_and(b == 0, i == 0))
        def _():
            dmu_ref[...] = jnp.zeros_like(dmu_ref)

        dmu_ref[0:1, :] += jnp.sum(d * (p_prev - p), axis=0, keepdims=True)
        dmu_ref[1:2, :] += jnp.sum(d * (p_next - p), axis=0, keepdims=True)

    dtile, dprev, dnext = _shift_specs(n_tok, 0)
    tile, prev, nxt = _shift_specs(n_tok, 0)
    mu_spec = pl.BlockSpec((2, SHIFT_BLOCK), lambda j, b, i: (0, j))
    return pl.pallas_call(
        body, grid=(SHIFT_COLS // SHIFT_BLOCK, bsz, n_tiles),
        in_specs=[dtile, dprev, dnext, tile, prev, nxt, mu_spec],
        out_specs=[pl.BlockSpec((None, TOK_TILE, SHIFT_BLOCK), lambda j, b, i: (b, i, j)), mu_spec],
        out_shape=[jax.ShapeDtypeStruct((bsz, n_tok, SHIFT_COLS), MXU_DTYPE),
                   jax.ShapeDtypeStruct((2, SHIFT_COLS), f32)],
        compiler_params=_cparams(("arbitrary", "arbitrary", "arbitrary")), name="token_shift_bwd")(
        dps, dps, dps, px, px, px, mu)


def _dg(a, b, ca, cb):
    return lax.dot_general(a.astype(MXU_DTYPE), b.astype(MXU_DTYPE), (((ca,), (cb,)), ((), ())),
                           preferred_element_type=f32)


@jax.custom_vjp
def _mm_nt(a, b):
    return _dg(a, b, 1, 1)


_mm_nt.defvjp(lambda a, b: (_dg(a, b, 1, 1), (a, b)),
              lambda res, ct: (_dg(ct, res[1], 1, 0), _dg(ct, res[0], 0, 0)))


@jax.custom_vjp
def _mm_tn(a, b):
    return _dg(a, b, 0, 0)


_mm_tn.defvjp(lambda a, b: (_dg(a, b, 0, 0), (a, b)),
              lambda res, ct: (_dg(res[1], ct, 1, 1), _dg(res[0], ct, 1, 0)))


ROTARY_PAIR = RET_DH // 4


def _swap_pairs_impl(t):
    lane = lax.broadcasted_iota(jnp.int32, t.shape, 1)
    return jnp.where(lane % (2 * ROTARY_PAIR) < ROTARY_PAIR, pltpu.roll(t, RET_DH - ROTARY_PAIR, axis=1),
                     pltpu.roll(t, ROTARY_PAIR, axis=1))


@jax.custom_vjp
def _swap_pairs(t):
    return _swap_pairs_impl(t)


_swap_pairs.defvjp(lambda t: (_swap_pairs_impl(t), None), lambda _, ct: (_swap_pairs_impl(ct),))


def _ret_chunk(state, q_raw, k_raw, v, cos, sin, ld_row, head, reverse):
    c = RET_CHUNK
    lane = lax.broadcasted_iota(jnp.int32, ld_row.shape, 1)
    lg = -jnp.exp(jnp.sum(jnp.where(lane == head, ld_row, 0.0), axis=-1, keepdims=True))
    rot = lambda t: t * cos + _swap_pairs(t) * sin
    q = rot(q_raw)
    k = rot(k_raw) * (RET_DH ** -0.5)
    ti = lax.broadcasted_iota(jnp.int32, (c, 1), 0).astype(f32)
    tj = lax.broadcasted_iota(jnp.int32, (1, c), 1).astype(f32)
    if not reverse:
        dist, mask, q_exp, k_exp = ti - tj, (ti - tj) >= 0, ti + 1.0, c - 1.0 - ti
    else:
        dist, mask, q_exp, k_exp = tj - ti, (tj - ti) > 0, c - ti, ti
    decay = jnp.where(mask, jnp.exp(lg * jnp.maximum(dist, 0.0)), 0.0)
    scores = _mm_nt(q, k) * decay
    out = _mxu_dot(scores, v) + _mxu_dot(q * jnp.exp(lg * q_exp), state)
    new_state = state * jnp.exp(lg * c) + _mm_tn(k * jnp.exp(lg * k_exp), v)
    return out, new_state


def _ret_specs(bsz, order):
    tok = lambda col=0: pl.BlockSpec((bsz, RET_CHUNK, RET_W), lambda i: (0, order(i), col))
    tab = pl.BlockSpec((RET_CHUNK, RET_DH), lambda i: (order(i), 0))
    ld = pl.BlockSpec((1, RET_DH), lambda i: (0, 0))
    return tok, tab, ld


def retention_fwd(px, cos, sin, ld_row, order, reverse, name):
    bsz, n_tok, _ = px.shape
    n_ch = n_tok // RET_CHUNK

    def body(q_ref, k_ref, v_ref, cos_ref, sin_ref, ld_ref, o_ref, sv_ref, st_ref):
        @pl.when(pl.program_id(0) == 0)
        def _():
            st_ref[...] = jnp.zeros_like(st_ref)

        for b in range(bsz):
            for h in range(RET_HEADS):
                sl = slice(h * RET_DH, (h + 1) * RET_DH)
                s = st_ref[b, h]
                sv_ref[b, h] = s
                o, s_new = _ret_chunk(s, q_ref[b, :, sl], k_ref[b, :, sl], v_ref[b, :, sl], cos_ref[...], sin_ref[...],
                                      ld_ref[...], h, reverse)
                o_ref[b, :, sl] = o
                st_ref[b, h] = s_new

    tok, tab, ld = _ret_specs(bsz, order)
    return pl.pallas_call(
        body, grid=(n_ch,), in_specs=[tok(0), tok(1), tok(2), tab, tab, ld],
        out_specs=[tok(), pl.BlockSpec((bsz, None, RET_HEADS, RET_DH, RET_DH), lambda i: (0, i, 0, 0, 0))],
        out_shape=[jax.ShapeDtypeStruct((bsz, n_tok, RET_W), f32),
                   jax.ShapeDtypeStruct((bsz, n_ch, RET_HEADS, RET_DH, RET_DH), f32)],
        scratch_shapes=[pltpu.VMEM((bsz, RET_HEADS, RET_DH, RET_DH), f32)],
        compiler_params=_cparams(("arbitrary",)), name=name)(px, px, px, cos, sin, ld_row)


def retention_bwd(do, px, states, cos, sin, ld_row, order, reverse, name):
    bsz, n_tok, _ = px.shape
    n_ch = n_tok // RET_CHUNK
    back = lambda i: order(n_ch - 1 - i)

    def body(do_ref, q_ref, k_ref, v_ref, sv_ref, cos_ref, sin_ref, ld_ref,
             dq_ref, dk_ref, dv_ref, dld_ref, dst_ref):
        @pl.when(pl.program_id(0) == 0)
        def _():
            dst_ref[...] = jnp.zeros_like(dst_ref)
            dld_ref[...] = jnp.zeros_like(dld_ref)

        cos_v, sin_v = cos_ref[...], sin_ref[...]
        for b in range(bsz):
            for h in range(RET_HEADS):
                sl = slice(h * RET_DH, (h + 1) * RET_DH)
                f = lambda s, q, k, v, ld, h=h: _ret_chunk(s, q, k, v, cos_v, sin_v, ld, h, reverse)
                _, vjp = jax.vjp(f, sv_ref[b, h], q_ref[b, :, sl], k_ref[b, :, sl], v_ref[b, :, sl], ld_ref[...])
                ds, dq, dk, dv, dld = vjp((do_ref[b, :, sl], dst_ref[b, h]))
                dst_ref[b, h] = ds
                dq_ref[b, :, sl] = dq
                dk_ref[b, :, sl] = dk
                dv_ref[b, :, sl] = dv
                dld_ref[...] += dld

    tok, tab, ld = _ret_specs(bsz, back)
    return pl.pallas_call(
        body, grid=(n_ch,),
        in_specs=[tok(), tok(0), tok(1), tok(2),
                  pl.BlockSpec((bsz, None, RET_HEADS, RET_DH, RET_DH), lambda i: (0, n_ch - 1 - i, 0, 0, 0)),
                  tab, tab, ld],
        out_specs=[tok(), tok(), tok(), ld],
        out_shape=[jax.ShapeDtypeStruct((bsz, n_tok, RET_W), f32)] * 3 + [jax.ShapeDtypeStruct((1, RET_DH), f32)],
        scratch_shapes=[pltpu.VMEM((bsz, RET_HEADS, RET_DH, RET_DH), f32)],
        compiler_params=_cparams(("arbitrary",)), name=name)(
        do, px, px, px, states, cos, sin, ld_row)


HALF_W = RW_W // 2


def _head_sum(x, ones):
    xm = x.astype(MXU_DTYPE)
    return jnp.concatenate([jnp.dot(xm[:, :HALF_W], ones, preferred_element_type=f32),
                            jnp.dot(xm[:, HALF_W:], ones, preferred_element_type=f32)], axis=1)


def _stack(parts):
    return jnp.concatenate(parts, axis=0)


def _row(ref, b, t):
    return ref[b, pl.ds(t, 1), :]


SCAN_DIRS = ((False, True), (True, False))
RW_HEADS = RW_W // RW_N
HEAD_ROWS_PAD = 16


def _head_rows(row, mask):
    return jnp.broadcast_to(row, mask.shape) * mask


def _outer(per_value, row, mask_pad):
    return lax.dot_general(per_value.astype(MXU_DTYPE), _head_rows(row, mask_pad).astype(MXU_DTYPE),
                           (((0,), (0,)), ((), ())), preferred_element_type=f32)


def _read(states, rows, mask, more_rows=()):
    lhs = _stack([_head_rows(r, mask) for r in list(rows) + list(more_rows)])
    return lax.dot_general(lhs.astype(MXU_DTYPE), _stack(states).astype(MXU_DTYPE), (((1,), (1,)), ((), ())),
                           preferred_element_type=f32)


def _own_block(raw, b):
    lanes = raw[:, RW_N * b:RW_N * (b + 1)]
    turned = _stack([lanes[RW_HEADS * b:], lanes[:RW_HEADS * b]]) if b else lanes
    if turned.shape[0] < HEAD_ROWS_PAD:
        turned = _stack([turned, jnp.zeros((HEAD_ROWS_PAD - turned.shape[0], RW_N), f32)])
    return turned[:HEAD_ROWS_PAD]


def _row_from_heads(per_value, state, mask_pad):
    full = jnp.dot(per_value.astype(MXU_DTYPE), state.astype(MXU_DTYPE), preferred_element_type=f32)
    return jnp.sum(full * mask_pad, axis=0, keepdims=True)


def _scan_specs(bsz, order):
    rows = lambda col=0: pl.BlockSpec((bsz, SCAN_CHUNK, RW_W), lambda i: (0, order(i), col))
    per_value = pl.BlockSpec((bsz, SCAN_CHUNK, HEAD_ROWS_PAD, RW_N), lambda i: (0, order(i), 0, 0))
    states = pl.BlockSpec((SCAN_CHUNK, bsz, RW_N, RW_W), lambda i: (order(i), 0, 0, 0))
    blocks = pl.BlockSpec((SCAN_CHUNK, RW_HEADS * bsz, RW_N * bsz), lambda i: (order(i), 0, 0))
    return rows, per_value, states, blocks


def _mxu_operands(states):
    return [s.astype(MXU_DTYPE) for s in states]


def _removed(states_m, kk_t, ones, bsz):
    removed = _head_sum(_stack([states_m[b] * kk_t[b].astype(MXU_DTYPE) for b in range(bsz)]), ones)
    return [removed[b * RW_N:(b + 1) * RW_N] for b in range(bsz)]


def _advance(sp, rem, w_t, b_t, vk, bsz):
    return [sp[b] * w_t[b] - rem[b] * b_t[b] + vk[b] for b in range(bsz)]


def heads_to_rows(a):
    b, t, _ = a.shape
    return jnp.pad(a.astype(MXU_DTYPE).reshape(b, t, RW_HEADS, RW_N),
                   ((0, 0), (0, 0), (0, HEAD_ROWS_PAD - RW_HEADS), (0, 0)))


def _blocks_to_rows(raw_ref, first, row_ref, bsz):
    steps = pl.ds(first, SCAN_CHUNK)
    for b in range(bsz):
        for h in range(RW_HEADS):
            row_ref[b, :, h * RW_N:(h + 1) * RW_N] = raw_ref[steps, RW_HEADS * b + h, RW_N * b:RW_N * (b + 1)]


N_ROWS_FWD = 5
N_ROWS_BWD = 5


def _scan_consts(bsz):
    head = (jnp.arange(RW_W)[None, :] // RW_N == jnp.arange(RW_HEADS)[:, None]).astype(f32)
    return head, jnp.pad(head, ((0, HEAD_ROWS_PAD - RW_HEADS), (0, 0))), _block_ones(HALF_W, RW_N)


def _const_specs(consts):
    return [pl.BlockSpec(c.shape, lambda i: (0, 0)) for c in consts]


def rwkv_scan_fwd(rows_in, v_heads, orders, name):
    bsz, n_tok, _ = rows_in[0][0][0].shape
    n_ch = n_tok // SCAN_CHUNK
    rng = range(bsz)
    consts = _scan_consts(bsz)

    def body(*refs):
        rows = [refs[:N_ROWS_FWD], refs[N_ROWS_FWD:2 * N_ROWS_FWD]]
        (v0, v1, head_ref, pad_ref, ones_ref, y0, y1, h0, h1, f0, f1, m0, m1, s0, s1, late_ref,
         raw_ref) = refs[2 * N_ROWS_FWD:]
        v_refs, y_refs, hist_refs, final_refs, s_refs = (v0, v1), (y0, y1), (h0, h1), (f0, f1), (s0, s1)
        removed_refs = (m0, m1)
        n_blk = RW_HEADS * bsz
        head_v, pad_v, ones_v = head_ref[...], pad_ref[...], ones_ref[...]
        for d in range(2):
            @pl.when(pl.program_id(0) == 0)
            def _(d=d):
                s_refs[d][...] = jnp.zeros_like(s_refs[d])

        def step(j, carry):
            ts = [SCAN_CHUNK - 1 - j if reverse else j for reverse, _ in SCAN_DIRS]
            sps = [[s_refs[d][b] for b in rng] for d in range(2)]
            sps_m = [_mxu_operands(sps[d]) for d in range(2)]
            rems = [_removed(sps_m[d], [_row(rows[d][1], b, ts[d]) for b in rng], ones_v, bsz) for d in range(2)]
            vks = [[_outer(v_refs[d][b, ts[d]], _row(rows[d][4], b, ts[d]), pad_v) for b in rng] for d in range(2)]
            for d, (reverse, inclusive) in enumerate(SCAN_DIRS):
                r_ref = rows[d][0]
                read_at = jnp.maximum(j - 1, 0) if inclusive else ts[d]
                both = _read(sps_m[d], [_row(r_ref, b, read_at) for b in rng], head_v,
                             [_row(rows[d][1], b, ts[d]) for b in rng])
                if inclusive:
                    late_ref[j] = both[:n_blk]
                else:
                    raw_ref[ts[d]] = both[:n_blk]
                removed_refs[d][ts[d]] = both[n_blk:]
            for d in range(2):
                new = _advance(sps[d], rems[d], [_row(rows[d][2], b, ts[d]) for b in rng],
                               [_row(rows[d][3], b, ts[d]) for b in rng], vks[d], bsz)
                for b in rng:
                    hist_refs[d][ts[d], b] = sps_m[d][b]
                    s_refs[d][b] = new[b]
            return carry

        lax.fori_loop(0, SCAN_CHUNK, step, 0, unroll=SCAN_UNROLL)
        for d, (reverse, inclusive) in enumerate(SCAN_DIRS):
            final_refs[d][...] = s_refs[d][...]
            if inclusive:
                assert not reverse
                last = SCAN_CHUNK - 1
                late_ref[SCAN_CHUNK] = _read(_mxu_operands([s_refs[d][b] for b in rng]),
                                             [rows[d][0][b, last:last + 1, :] for b in rng], head_v)
                _blocks_to_rows(late_ref, 1, y_refs[d], bsz)
            else:
                _blocks_to_rows(raw_ref, 0, y_refs[d], bsz)

    specs = [_scan_specs(bsz, orders[d]) for d in range(2)]
    state = pltpu.VMEM((bsz, RW_N, RW_W), f32)
    late = pltpu.VMEM((SCAN_CHUNK + 1, RW_HEADS * bsz, RW_N * bsz), f32)
    raw = pltpu.VMEM((SCAN_CHUNK, RW_HEADS * bsz, RW_N * bsz), f32)
    final_spec = pl.BlockSpec((bsz, RW_N, RW_W), lambda i: (0, 0, 0))
    return pl.pallas_call(
        body, grid=(n_ch,),
        in_specs=[specs[d][0](col) for d in range(2) for _, col in rows_in[d]] + [specs[0][1], specs[1][1]]
        + _const_specs(consts),
        out_specs=[specs[0][0](), specs[1][0](), specs[0][2], specs[1][2], final_spec, final_spec,
                   specs[0][3], specs[1][3]],
        out_shape=[jax.ShapeDtypeStruct((bsz, n_tok, RW_W), f32)] * 2
        + [jax.ShapeDtypeStruct((n_tok, bsz, RW_N, RW_W), MXU_DTYPE)] * 2
        + [jax.ShapeDtypeStruct((bsz, RW_N, RW_W), f32)] * 2
        + [jax.ShapeDtypeStruct((n_tok, RW_HEADS * bsz, RW_N * bsz), f32)] * 2,
        scratch_shapes=[state, state, late, raw],
        compiler_params=_cparams(("arbitrary",)), name=name)(
        *[a for d in range(2) for a, _ in rows_in[d]], v_heads, v_heads, *consts)


def rwkv_scan_bwd(rows_in, v_heads, dy_heads, hists, finals, removed, orders, name):
    bsz, n_tok, _ = rows_in[0][0][0].shape
    n_ch = n_tok // SCAN_CHUNK
    backs = [functools.partial(lambda i, order: order(n_ch - 1 - i), order=orders[d]) for d in range(2)]
    rng = range(bsz)
    consts = _scan_consts(bsz)
    n_out, n_scr = 6, 6

    def body(*refs):
        rows = [refs[:N_ROWS_BWD], refs[N_ROWS_BWD:2 * N_ROWS_BWD]]
        rest = refs[2 * N_ROWS_BWD:]
        v_refs, dy_refs, hist_refs, final_refs, removed_refs = rest[0:2], rest[2:4], rest[4:6], rest[6:8], rest[8:10]
        head_ref, pad_ref, ones_ref = rest[10:13]
        outs = [rest[13:13 + n_out], rest[13 + n_out:13 + 2 * n_out]]
        scr = [rest[13 + 2 * n_out:13 + 2 * n_out + n_scr], rest[13 + 2 * n_out + n_scr:]]
        n_blk = RW_HEADS * bsz
        head_v, pad_v, ones_v = head_ref[...], pad_ref[...], ones_ref[...]
        for d in range(2):
            @pl.when(pl.program_id(0) == 0)
            def _(d=d):
                scr[d][1][...] = jnp.zeros_like(scr[d][1])
                scr[d][0][...] = final_refs[d][...]

        def step_of(j, reverse):
            return j if reverse else SCAN_CHUNK - 1 - j

        for d, (reverse, _) in enumerate(SCAN_DIRS):
            t0 = step_of(0, reverse)
            for b in rng:
                scr[d][3][b] = _outer(dy_refs[d][b, t0], rows[d][0][b, t0:t0 + 1, :], pad_v)

        def bstep(j, carry):
            ts = [step_of(j, reverse) for reverse, _ in SCAN_DIRS]
            reads = [[scr[d][3][b] for b in rng] for d in range(2)]
            dss = []
            for d, (_, inclusive) in enumerate(SCAN_DIRS):
                ds = [scr[d][1][b] for b in rng]
                dss.append([ds[b] + reads[d][b] for b in rng] if inclusive else ds)
            dss_m = [_mxu_operands(dss[d]) for d in range(2)]
            drems = [_removed(dss_m[d], [-_row(rows[d][3], b, ts[d]) for b in rng], ones_v, bsz) for d in range(2)]
            for d, (reverse, _) in enumerate(SCAN_DIRS):
                t_next = step_of(jnp.minimum(j + 1, SCAN_CHUNK - 1), reverse)
                for b in rng:
                    scr[d][3][b] = _outer(dy_refs[d][b, t_next], _row(rows[d][0], b, t_next), pad_v)
                both = _read(dss_m[d], [_row(rows[d][4], b, ts[d]) for b in rng], head_v,
                             [-_row(rows[d][3], b, ts[d]) for b in rng])
                scr[d][4][ts[d]] = both[:n_blk]
                scr[d][5][ts[d]] = both[n_blk:]
            for d, (_, inclusive) in enumerate(SCAN_DIRS):
                _, kk_ref, w_ref, _, _ = rows[d]
                _, ds_ref, dsh_ref = scr[d][:3]
                for b in rng:
                    dsh_ref[ts[d], b] = dss[d][b]
                    dsp = dss[d][b] * _row(w_ref, b, ts[d]) + drems[d][b] * _row(kk_ref, b, ts[d])
                    ds_ref[b] = dsp if inclusive else dsp + reads[d][b]
            return carry

        lax.fori_loop(0, SCAN_CHUNK, bstep, 0, unroll=SCAN_UNROLL)

        rsum = lambda z: jnp.sum(z, axis=0, keepdims=True)
        for d, (reverse, inclusive) in enumerate(SCAN_DIRS):
            dr_ref, dkk_ref, dw_ref, db_ref, dkt_ref, dv_ref = outs[d]
            after_ref, _, dsh_ref, _, dv_raw_ref, dremt_ref = scr[d]
            hist_ref, removed_ref = hist_refs[d], removed_refs[d]
            _blocks_to_rows(dv_raw_ref, 0, dv_ref, bsz)
            for t in range(SCAN_CHUNK):
                ts = slice(t, t + 1)
                after = t - 1 if reverse else t + 1
                for b in rng:
                    sp_m, ds = hist_ref[t, b], dsh_ref[t, b]
                    sp = sp_m.astype(f32)
                    if not inclusive:
                        seen = sp_m
                    else:
                        seen = hist_ref[after, b] if 0 <= after < SCAN_CHUNK else after_ref[b]
                    dr_ref[b, ts, :] = _row_from_heads(dy_refs[d][b, t], seen, pad_v)
                    dkt_ref[b, ts, :] = _row_from_heads(v_refs[d][b, t], ds, pad_v)
                    dw_ref[b, ts, :] = rsum(ds * sp)
                    db_ref[b, ts, :] = -_row_from_heads(_own_block(removed_ref[t], b), ds, pad_v)
                    dkk_ref[b, ts, :] = _row_from_heads(_own_block(dremt_ref[t], b), sp_m, pad_v)
            if inclusive:
                first = SCAN_CHUNK - 1 if reverse else 0
                for b in rng:
                    after_ref[b] = hist_ref[first, b].astype(f32)

    specs = [_scan_specs(bsz, backs[d]) for d in range(2)]
    hist = pltpu.VMEM((SCAN_CHUNK, bsz, RW_N, RW_W), f32)
    state = pltpu.VMEM((bsz, RW_N, RW_W), f32)
    final_spec = pl.BlockSpec((bsz, RW_N, RW_W), lambda i: (0, 0, 0))
    raw = pltpu.VMEM((SCAN_CHUNK, RW_HEADS * bsz, RW_N * bsz), f32)
    return pl.pallas_call(
        body, grid=(n_ch,),
        in_specs=[specs[d][0](col) for d in range(2) for _, col in rows_in[d]]
        + [specs[0][1], specs[1][1]] * 2 + [specs[0][2], specs[1][2], final_spec, final_spec, specs[0][3], specs[1][3]]
        + _const_specs(consts),
        out_specs=[specs[d][0]() for d in range(2) for _ in range(n_out)],
        out_shape=[jax.ShapeDtypeStruct((bsz, n_tok, RW_W), f32)] * (2 * n_out),
        scratch_shapes=[state, state, hist, state, raw, raw] * 2,
        compiler_params=_cparams(("arbitrary",)), name=name)(
        *[a for d in range(2) for a, _ in rows_in[d]], v_heads, v_heads, dy_heads, dy_heads, *hists, *finals, *removed, *consts)


MOD_NAMES = ("shift1", "scale1", "gate1", "shift2", "scale2", "gate2")


def _rope_tables(t_ctx, t_x):
    quarter = RET_DH // 4
    pos = jnp.arange(t_x)
    inv = jnp.power(ROPE_BASE, -jnp.arange(0, 2 * quarter, 2, dtype=f32) / (2 * quarter))
    ang_r = (pos // GRID_W).astype(f32)[:, None] * inv[None, :]
    ang_c = (pos % GRID_W).astype(f32)[:, None] * inv[None, :]
    cos = jnp.concatenate([jnp.cos(ang_r)] * 2 + [jnp.cos(ang_c)] * 2, axis=1)
    sin = jnp.concatenate([-jnp.sin(ang_r), jnp.sin(ang_r), -jnp.sin(ang_c), jnp.sin(ang_c)], axis=1)
    cos = jnp.concatenate([jnp.ones((t_ctx, RET_DH), f32), cos], axis=0)
    sin = jnp.concatenate([jnp.zeros((t_ctx, RET_DH), f32), sin], axis=0)
    return cos, sin


def _pad_rows(w, lo, total):
    return jnp.pad(w, ((lo, total - lo - w.shape[0]), (0, 0)))


LATE_WEIGHTS = ("w_out", "w_ff1", "w_ff2")


def layer_step(x, ctx, tgt, mod_x, mod_ctx, wt, late_weights=None, early_grads=None):
    bsz, t_x, _ = x.shape
    t_c = ctx.shape[1]
    t_all = t_c + t_x
    n_ct, n_xt = t_c // TOK_TILE, t_x // TOK_TILE
    n_t = n_ct + n_xt
    assert t_c % TOK_TILE == 0 and t_x % TOK_TILE == 0 and t_c % RET_CHUNK == 0

    seg = lambda i: (i >= n_ct).astype(jnp.int32)
    seg_first = lambda i: jnp.logical_or(i == 0, i == n_ct)
    seg_last = lambda i: jnp.logical_or(i == n_ct - 1, i == n_t - 1)
    mod_all = {n: jnp.stack([jnp.broadcast_to(mod_ctx[k], (bsz, D_MODEL)), mod_x[:, k]], axis=1)[:, :, None, :]
               for k, n in enumerate(MOD_NAMES)}
    mod_lat = {n: mod_x[:, k][:, None, None, :] for k, n in enumerate(MOD_NAMES)}
    both = lambda n: Seg(mod_all[n], seg, seg_first)
    lat = lambda n: Seg(mod_lat[n], lambda i: 0, lambda i: i == 0)
    flat = lambda a: a.reshape(-1, a.shape[-1])

    def chunk_orders(n_ctx_chunks, n_chunks):
        fwd = lambda i: i
        bwd = lambda i: jnp.where(i < n_ctx_chunks, n_ctx_chunks - 1 - i, n_chunks + n_ctx_chunks - 1 - i)
        return fwd, bwd

    ones64, ones128 = _block_ones(RW_W, RW_N), _block_ones(RET_W, RET_DH)
    cos, sin = _rope_tables(t_c, t_x)
    ld_rows = [jnp.pad(wt["ret_log_decay"][d][None, :], ((0, 0), (0, RET_DH - RET_HEADS))) for d in range(2)]
    w_up_pad = [_pad_rows(wt["rwkv_w_up"][d], 0, LORA_W) for d in range(2)]
    a_up_pad = [_pad_rows(wt["rwkv_a_up"][d], DECAY_LORA, LORA_W) for d in range(2)]
    g_up_pad = _pad_rows(wt["rwkv_g_up"], DECAY_LORA + AAA_LORA, LORA_W)
    row = lambda a, d: a[d][None, :]

    h = jnp.concatenate([ctx, x], axis=1)
    norm1_ins = lambda: [Tiled(h), both("shift1"), both("scale1"), Glob(wt["norm1_g"])]
    (n1,) = ew_forward(fn_norm_mod, "norm1", bsz, n_t, norm1_ins(), [(D_MODEL, MXU_DTYPE)])
    px = matmul(flat(n1), wt["w_in"], "nn", "proj_in").reshape(bsz, t_all, IN_COLS)
    px_rw = px[..., RET_COLS:]
    ps = token_shift(px_rw, wt["rwkv_shift_mu"], seg_first, seg_last)

    def prep_ins(toff=0):
        return [Tiled(ps, RW_W, 1), Tiled(ps, LORA_W, 3 * RW_W // LORA_W),
                Glob(row(wt["rwkv_w0"], 0)), Glob(row(wt["rwkv_w0"], 1)),
                Glob(row(wt["rwkv_a0"], 0)), Glob(row(wt["rwkv_a0"], 1)),
                Glob(w_up_pad[0]), Glob(w_up_pad[1]), Glob(a_up_pad[0]), Glob(a_up_pad[1]), Glob(g_up_pad),
                Glob(wt["rwkv_k_k"]), Glob(wt["rwkv_k_a"]), Glob(ones64)]

    kk, w_f, b_f, kt_f, w_b, b_b, kt_b, g_rw = ew_forward(fn_rwkv_prepare, "rwkv_prepare", bsz, n_t, prep_ins(),
                                                           [(RW_W, f32)] * 8)
    rw_order = chunk_orders(t_c // SCAN_CHUNK, t_all // SCAN_CHUNK)
    ret_order = chunk_orders(t_c // RET_CHUNK, t_all // RET_CHUNK)
    scan_rows = [[(ps, 0), (kk, 0), (w_f, 0), (b_f, 0), (kt_f, 0)], [(ps, 0), (kk, 0), (w_b, 0), (b_b, 0), (kt_b, 0)]]
    v_heads = heads_to_rows(ps[..., 2 * RW_W:3 * RW_W])
    y_f, y_b, *kept_states = rwkv_scan_fwd(scan_rows, v_heads, rw_order, "rwkv_scan_fwd")
    y = [y_f, y_b]
    o, ret_states = [], []
    for d in range(2):
        o_d, st_d = retention_fwd(px, cos, sin, ld_rows[d], ret_order[d], SCAN_DIRS[d][0], f"retention_fwd{d}")
        o.append(o_d), ret_states.append(st_d)

    def merge_ins(toff):
        return [Tiled(o[0], toff=toff), Tiled(o[1], toff=toff), Tiled(px, RET_W, 3, toff),
                Tiled(y[0], toff=toff), Tiled(y[1], toff=toff), Tiled(ps, RW_W, 0, toff), Tiled(kt_f, toff=toff),
                Tiled(ps, RW_W, 2, toff), Tiled(g_rw, toff=toff),
                Glob(wt["rwkv_r_k"]), Glob(wt["rwkv_ln_w"]), Glob(wt["rwkv_ln_b"]), Glob(ones64), Glob(ones128)]

    ret_out, rw_out = ew_forward(fn_merge, "merge_heads", bsz, n_xt, merge_ins(n_ct),
                                 [(RET_W, MXU_DTYPE), (RW_W, MXU_DTYPE)])
    merged = jnp.concatenate([ret_out, rw_out], axis=-1)
    if late_weights is not None:
        wt = {**wt, **late_weights(merged)}
    mix = matmul(flat(merged), wt["w_out"], "nn", "proj_out").reshape(bsz, t_x, D_MODEL)
    resid_ins = lambda: [Tiled(x), Tiled(mix), lat("gate1"), lat("shift2"), lat("scale2"), Glob(wt["norm2_g"])]
    h1, n2 = ew_forward(fn_resid_norm_mod, "resid_norm2", bsz, n_xt, resid_ins(), [(D_MODEL, f32), (D_MODEL, MXU_DTYPE)])
    act = matmul(flat(n2), wt["w_ff1"], "nn", "ff1", MXU_DTYPE, wt["b_ff1"], relu2).reshape(bsz, t_x, D_FF)
    ff = matmul(flat(act), wt["w_ff2"], "nn", "ff2").reshape(bsz, t_x, D_MODEL)

    g = {}
    loss, dh1, dff, dgate2, g["b_ff2"], g["final_g"] = loss_and_grads(
        h1, ff, tgt, mod_lat["gate2"], wt["b_ff2"], wt["final_g"], bsz, n_xt)
    dact = matmul(flat(dff), wt["w_ff2"], "nt", "ff2_dx", MXU_DTYPE).reshape(bsz, t_x, D_FF)
    g["w_ff2"] = matmul(flat(act), flat(dff), "tn", "ff2_dw")
    du, g["b_ff1"] = relu2_backward(act, dact, "relu2_bwd")
    dn2 = matmul(flat(du), wt["w_ff1"], "nt", "ff1_dx").reshape(bsz, t_x, D_MODEL)
    g["w_ff1"] = matmul(flat(n2), flat(du), "tn", "ff1_dw")
    dx_res, dmix, dgate1, dshift2, dscale2, g["norm2_g"] = ew_backward(
        fn_resid_norm_mod, "resid_norm2_bwd", bsz, n_xt, resid_ins(), [Tiled(dh1), Tiled(dn2)], [True] * 6,
        {1: MXU_DTYPE})
    dmerged = matmul(flat(dmix), wt["w_out"], "nt", "proj_out_dx").reshape(bsz, t_x, D_MODEL)
    g["w_out"] = matmul(flat(merged), flat(dmix), "tn", "proj_out_dw")
    if early_grads is not None:
        token = early_grads({n: g.pop(n) for n in LATE_WEIGHTS})
        wt = {**wt, "rwkv_r_k": wt["rwkv_r_k"] + token[:1, :1]}
    (do, dg_ret, dy, dr_m, dkt_m, dv_m, dg_rw, g["rwkv_r_k"], g["rwkv_ln_w"], g["rwkv_ln_b"]) = ew_backward(
        fn_merge, "merge_heads_bwd", bsz, n_xt, merge_ins(0),
        [Tiled(dmerged, RET_W, 0, -n_ct), Tiled(dmerged, RW_W, 1, -n_ct)],
        [True, False, True, True, False, True, True, True, True, True, True, True, False, False], lead=n_ct)

    dqkv, dld = [], []
    for d in range(2):
        *dqkv_d, dld_d = retention_bwd(do, px, ret_states[d], cos, sin, ld_rows[d], ret_order[d],
                                       SCAN_DIRS[d][0], f"retention_bwd{d}")
        dqkv.append(dqkv_d), dld.append(dld_d[0, :RET_HEADS])
    g["ret_log_decay"] = jnp.stack(dld)
    (dr_f, dkk_f, dw_f, db_f, dkt_f, dv_f, dr_b, dkk_b, dw_b, db_b, dkt_b, dv_b) = rwkv_scan_bwd(
        scan_rows, v_heads, heads_to_rows(dy), kept_states[:2], kept_states[2:4], kept_states[4:], rw_order, "rwkv_scan_bwd")
    prep_cts = [dkk_f + dkk_b, dw_f, db_f, dkt_f + dkt_m, dw_b, db_b, dkt_b, dg_rw]
    (dks, dlora, dw0_f, dw0_b, da0_f, da0_b, dwup_f, dwup_b, daup_f, daup_b, dgup, g["rwkv_k_k"],
     g["rwkv_k_a"]) = ew_backward(fn_rwkv_prepare, "rwkv_prepare_bwd", bsz, n_t, prep_ins(),
                                  [Tiled(c) for c in prep_cts], [True] * 13 + [False])
    g["rwkv_w0"] = jnp.concatenate([dw0_f, dw0_b], axis=0)
    g["rwkv_a0"] = jnp.concatenate([da0_f, da0_b], axis=0)
    g["rwkv_w_up"] = jnp.stack([dwup_f[:DECAY_LORA], dwup_b[:DECAY_LORA]])
    g["rwkv_a_up"] = jnp.stack([daup_f[DECAY_LORA:DECAY_LORA + AAA_LORA], daup_b[DECAY_LORA:DECAY_LORA + AAA_LORA]])
    g["rwkv_g_up"] = dgup[DECAY_LORA + AAA_LORA:]
    dps = jnp.concatenate([dr_f + dr_b + dr_m, dks, dv_f + dv_b + dv_m, dlora], axis=-1)
    dp_rw, g["rwkv_shift_mu"] = token_shift_bwd(dps, px_rw, wt["rwkv_shift_mu"], seg_first, seg_last)
    dpx = jnp.concatenate([(dqkv[0][k] + dqkv[1][k]).astype(MXU_DTYPE) for k in range(3)]
                          + [dg_ret.astype(MXU_DTYPE), dp_rw], axis=-1)
    dn1 = matmul(flat(dpx), wt["w_in"], "nt", "proj_in_dx").reshape(bsz, t_all, D_MODEL)
    g["w_in"] = matmul(flat(n1), flat(dpx), "tn", "proj_in_dw")
    dh, dshift1, dscale1, g["norm1_g"] = ew_backward(fn_norm_mod, "norm1_bwd", bsz, n_t, norm1_ins(), [Tiled(dn1)],
                                                     [True] * 4)
    grad_x = dh[:, t_c:] + dx_res
    zeros = jnp.zeros((D_MODEL,), f32)
    g["mod_x"] = jnp.stack([dshift1[:, 1, 0], dscale1[:, 1, 0], dgate1[:, 0, 0], dshift2[:, 0, 0], dscale2[:, 0, 0],
                            dgate2[:, 0, 0]], axis=1)
    g["mod_ctx"] = jnp.stack([dshift1[:, 0, 0].sum(0), dscale1[:, 0, 0].sum(0), zeros, zeros, zeros, zeros])
    return loss, grad_x, g


MESH_ID = pl.DeviceIdType.MESH
ALL_PEERS = [(dx, dy, dc) for dx in (0, 1) for dy in (0, 1) for dc in (0, 1)][1:]
CHIP_PEERS = [(1, 0, 0), (0, 1, 0), (1, 1, 0)]
CHIP_SLOTS = (0, 2, 4, 6)


def _mesh_pos():
    return lax.axis_index("x"), lax.axis_index("y"), lax.axis_index("c")


def _device_slot():
    x, y, c = _mesh_pos()
    return 4 * x + 2 * y + c


def sibling_swap(arrs, name):
    n = len(arrs)

    def body(*refs):
        in_refs, out_refs = refs[:n], refs[n:2 * n]
        send_sems, recv_sems = refs[2 * n:]
        x, y, c = _mesh_pos()
        copies = [pltpu.make_async_remote_copy(src_ref=in_refs[a], dst_ref=out_refs[a], send_sem=send_sems.at[a],
                                               recv_sem=recv_sems.at[a], device_id=(x, y, 1 - c),
                                               device_id_type=MESH_ID) for a in range(n)]
        for cp in copies:
            cp.start()
        for cp in copies:
            cp.wait()

    any_spec = pl.BlockSpec(memory_space=pl.ANY)
    res = pl.pallas_call(
        body, in_specs=[any_spec] * n, out_specs=[any_spec] * n,
        out_shape=[jax.ShapeDtypeStruct(a.shape, a.dtype) for a in arrs],
        scratch_shapes=[pltpu.SemaphoreType.DMA((n,)), pltpu.SemaphoreType.DMA((n,))],
        name=name)(*arrs)
    return list(res)


def exchange(arrs, gather, peers, name, by_chip=False, own=True):
    n, n_peers = len(arrs), len(peers)
    n_slots = N_SHARDS if by_chip else N_DEV
    slot = (lambda x, y, c: 2 * x + y) if by_chip else (lambda x, y, c: 4 * x + 2 * y + c)

    def body(*refs):
        in_refs, out_refs = refs[:n], refs[n:2 * n]
        send_sems, recv_sems, local_sems = refs[2 * n:]
        x, y, c = _mesh_pos()
        me = slot(x, y, c)
        copies, locals_ = [], []
        for a in range(n):
            if own:
                mine = in_refs[a] if gather else in_refs[a].at[me]
                loc = pltpu.make_async_copy(mine, out_refs[a].at[me], local_sems.at[a])
                loc.start()
                locals_.append(loc)
            for k, (dx, dy, dc) in enumerate(peers):
                peer = (1 - x if dx else x, 1 - y if dy else y, 1 - c if dc else c)
                src = in_refs[a] if gather else in_refs[a].at[slot(*peer)]
                sem = a * n_peers + k
                cp = pltpu.make_async_remote_copy(src_ref=src, dst_ref=out_refs[a].at[me], send_sem=send_sems.at[sem],
                                                  recv_sem=recv_sems.at[sem], device_id=peer, device_id_type=MESH_ID)
                cp.start()
                copies.append(cp)
        for cp in copies:
            cp.wait()
        for loc in locals_:
            loc.wait()

    any_spec = pl.BlockSpec(memory_space=pl.ANY)
    out_shape = [jax.ShapeDtypeStruct((n_slots,) + (a.shape if gather else a.shape[1:]), a.dtype) for a in arrs]
    n_sems = n * n_peers
    res = pl.pallas_call(
        body, in_specs=[any_spec] * n, out_specs=[any_spec] * n, out_shape=out_shape,
        scratch_shapes=[pltpu.SemaphoreType.DMA((n_sems,)), pltpu.SemaphoreType.DMA((n_sems,)),
                        pltpu.SemaphoreType.DMA((n,))],
        name=name)(*arrs)
    return list(res)


HBM_SPEC = pl.BlockSpec(memory_space=pltpu.HBM)
SEM_SPEC = pl.BlockSpec(memory_space=pltpu.SEMAPHORE)
DATAFLOW = pltpu.SideEffectType.DATAFLOW_SIDE_EFFECTING


def _peer_copies(src_refs, land_refs, send_sems, recv_sems, gather):
    x, y, c = _mesh_pos()
    me = 4 * x + 2 * y + c
    copies = []
    for a, (src_ref, land_ref) in enumerate(zip(src_refs, land_refs)):
        for k, (dx, dy, dc) in enumerate(ALL_PEERS):
            peer = (1 - x if dx else x, 1 - y if dy else y, 1 - c if dc else c)
            src = src_ref if gather else src_ref.at[4 * peer[0] + 2 * peer[1] + peer[2]]
            sem = a * len(ALL_PEERS) + k
            copies.append(pltpu.make_async_remote_copy(src_ref=src, dst_ref=land_ref.at[me], send_sem=send_sems.at[sem],
                                                       recv_sem=recv_sems.at[sem], device_id=peer,
                                                       device_id_type=MESH_ID))
    return copies


def exchange_start(arrs, gather, name):
    n = len(arrs)
    lands = [lax.empty((N_DEV,) + (a.shape if gather else a.shape[1:]), a.dtype) for a in arrs]

    def body(*refs):
        for cp in _peer_copies(refs[:n], refs[n:2 * n], refs[2 * n], refs[2 * n + 1], gather):
            cp.start()
        refs[-1][...] = jnp.zeros_like(refs[-1])

    sems = pltpu.SemaphoreType.DMA((n * len(ALL_PEERS),))
    hbm = [pltpu.HBM(a.shape, a.dtype) for a in arrs + lands]
    res = pl.pallas_call(
        body, name=name, out_shape=(sems, sems, *hbm, jax.ShapeDtypeStruct((8, 128), f32)),
        in_specs=[HBM_SPEC] * (2 * n),
        out_specs=(SEM_SPEC, SEM_SPEC, *[HBM_SPEC] * (2 * n), pl.BlockSpec(memory_space=pltpu.VMEM)),
        input_output_aliases={i: 2 + i for i in range(2 * n)},
        compiler_params=pltpu.CompilerParams(has_side_effects=DATAFLOW))(
        *[pltpu.with_memory_space_constraint(a, pltpu.HBM) for a in arrs + lands])
    return res[0], res[1], list(res[2:2 + n]), list(res[2 + n:2 + 2 * n]), res[-1]


def exchange_wait(started, after, gather, name):
    send_sems, recv_sems, srcs, lands, _ = started
    n = len(srcs)

    def body(*refs):
        for cp in _peer_copies(refs[:n], refs[n:2 * n], refs[2 * n], refs[2 * n + 1], gather):
            cp.wait_send()
            cp.wait_recv()

    res = pl.pallas_call(
        body, name=name, out_shape=tuple(pltpu.HBM(a.shape, a.dtype) for a in srcs + lands),
        in_specs=[HBM_SPEC] * (2 * n) + [SEM_SPEC, SEM_SPEC, pl.BlockSpec(memory_space=pl.ANY)],
        out_specs=tuple([HBM_SPEC] * (2 * n)), input_output_aliases={i: i for i in range(2 * n)},
        compiler_params=pltpu.CompilerParams(has_side_effects=DATAFLOW))(*srcs, *lands, send_sems, recv_sems, after)
    return list(res[n:])


def add_arrays(parts, name, out_dtype=f32):
    r, c = parts[0].shape
    tr = r
    for cand in (512, 256, 128, 64, 32, 16):
        if r % cand == 0:
            tr = cand
            break

    def body(*refs):
        acc = refs[0][...].astype(f32)
        for p_ref in refs[1:-1]:
            acc = acc + p_ref[...].astype(f32)
        refs[-1][...] = acc.astype(out_dtype)

    spec = pl.BlockSpec((tr, c), lambda i: (i, 0))
    return pl.pallas_call(body, grid=(r // tr,), in_specs=[spec] * len(parts), out_specs=spec,
                          out_shape=jax.ShapeDtypeStruct((r, c), out_dtype),
                          compiler_params=_cparams(("parallel",)), name=name)(*parts)


def gather_two_level(arrs, name):
    n = len(arrs)
    per = 7

    def body(*refs):
        in_refs, out_refs = refs[:n], refs[n:2 * n]
        send_sems, recv_sems = refs[2 * n:]
        x, y, c = _mesh_pos()
        me, sibling = (x, y, c), (x, y, 1 - c)
        chips = [(1 - x, y), (x, 1 - y), (1 - x, 1 - y)]

        def copy(a, k, block, to, src=None):
            rows = out_refs[a].at[4 * block[0] + 2 * block[1] + block[2]]
            return pltpu.make_async_remote_copy(src_ref=rows if src is None else src, dst_ref=rows,
                                                send_sem=send_sems.at[a * per + k], recv_sem=recv_sems.at[a * per + k],
                                                device_id=to, device_id_type=MESH_ID)

        first, passed = [], []
        for a in range(n):
            first.append(copy(a, 0, me, sibling, src=in_refs[a]))
            first += [copy(a, 1 + j, me, (*chip, c), src=in_refs[a]) for j, chip in enumerate(chips)]
        for cp in first:
            cp.start()
        for a in range(n):
            for j, chip in enumerate(chips):
                copy(a, 1 + j, (*chip, c), me).wait_recv()
                fwd = copy(a, 4 + j, (*chip, c), sibling)
                fwd.start()
                passed.append(fwd)
        for a in range(n):
            copy(a, 0, sibling, me).wait_recv()
            for j, chip in enumerate(chips):
                copy(a, 4 + j, (*chip, 1 - c), me).wait_recv()
        for cp in first + passed:
            cp.wait_send()

    any_spec = pl.BlockSpec(memory_space=pl.ANY)
    res = pl.pallas_call(
        body, in_specs=[any_spec] * n, out_specs=[any_spec] * n,
        out_shape=[jax.ShapeDtypeStruct((N_DEV,) + a.shape, a.dtype) for a in arrs],
        scratch_shapes=[pltpu.SemaphoreType.DMA((n * per,)), pltpu.SemaphoreType.DMA((n * per,))],
        name=name)(*arrs)
    return list(res)


def sum_slots(parts, slots, name):
    _, r, c = parts.shape
    tr = r
    for cand in (512, 256, 128, 64, 32, 16, 8):
        if r % cand == 0 and cand * c * 4 * len(slots) <= 8 * 1024 * 1024:
            tr = cand
            break

    def body(p_ref, o_ref):
        acc = p_ref[slots[0]].astype(f32)
        for s in slots[1:]:
            acc = acc + p_ref[s].astype(f32)
        o_ref[...] = acc

    return pl.pallas_call(body, grid=(r // tr,), in_specs=[pl.BlockSpec((parts.shape[0], tr, c), lambda i: (0, i, 0))],
                          out_specs=pl.BlockSpec((tr, c), lambda i: (i, 0)),
                          out_shape=jax.ShapeDtypeStruct((r, c), f32),
                          compiler_params=_cparams(("parallel",)), name=name)(parts)


def column_sum(a, name):
    def body(a_ref, o_ref):
        o_ref[...] = jnp.sum(a_ref[...], axis=0, keepdims=True)

    return pl.pallas_call(body, out_shape=jax.ShapeDtypeStruct((1, a.shape[1]), f32), name=name)(a)


def adamw(w, g, m, v, name):
    r, c = w.shape
    tr = r
    for cand in (256, 128, 64, 32, 16, 8):
        if r % cand == 0:
            tr = cand
            break

    def body(w_ref, g_ref, m_ref, v_ref, d_ref, mo_ref, vo_ref):
        gv = g_ref[...]
        m_new = ADAM_B1 * m_ref[...] + (1.0 - ADAM_B1) * gv
        v_new = ADAM_B2 * v_ref[...] + (1.0 - ADAM_B2) * jnp.square(gv)
        m_hat = m_new / (1.0 - ADAM_B1 ** ADAM_STEP)
        v_hat = v_new / (1.0 - ADAM_B2 ** ADAM_STEP)
        d_ref[...] = -ADAM_LR * (m_hat / (jnp.sqrt(v_hat) + ADAM_EPS) + ADAM_WD * w_ref[...])
        mo_ref[...] = m_new
        vo_ref[...] = v_new

    spec = pl.BlockSpec((tr, c), lambda i: (i, 0))
    return pl.pallas_call(body, grid=(r // tr,), in_specs=[spec] * 4, out_specs=[spec] * 3,
                          out_shape=[jax.ShapeDtypeStruct((r, c), f32)] * 3,
                          compiler_params=_cparams(("parallel",)), name=name)(w, g, m, v)


def adaln_fwd(c_rows, w, b):
    def body(c_ref, w_ref, b_ref, o_ref):
        cv = c_ref[...]
        o_ref[...] = _mxu_dot(cv * jax.nn.sigmoid(cv), w_ref[...]) + b_ref[...]

    return pl.pallas_call(body, out_shape=jax.ShapeDtypeStruct((c_rows.shape[0], w.shape[1]), f32),
                          compiler_params=pltpu.CompilerParams(vmem_limit_bytes=VMEM_LIMIT), name="adaln_fwd")(c_rows, w, b)


def adaln_bwd(c_rows, dm, w):
    def body(c_ref, dm_ref, w_ref, gw_ref, ds_ref):
        cv = c_ref[...]
        gw_ref[...] = _dg(cv * jax.nn.sigmoid(cv), dm_ref[...], 0, 0)
        ds_ref[...] = _dg(dm_ref[...], w_ref[...], 1, 1)

    return pl.pallas_call(body, out_shape=[jax.ShapeDtypeStruct(w.shape, f32),
                                           jax.ShapeDtypeStruct(c_rows.shape, f32)],
                          compiler_params=pltpu.CompilerParams(vmem_limit_bytes=VMEM_LIMIT), name="adaln_bwd")(c_rows, dm, w)


def c_ctx_grad(parts, c_ctx_row):
    def body(p_ref, c_ref, o_ref):
        total = p_ref[0, 0:1, :]
        for s in range(1, N_SHARDS):
            total = total + p_ref[s, 0:1, :]
        _, vjp = jax.vjp(jax.nn.silu, c_ref[...])
        o_ref[...] = vjp(total)[0]

    return pl.pallas_call(body, out_shape=jax.ShapeDtypeStruct((1, D_MODEL), f32), name="c_ctx_grad")(parts, c_ctx_row)


PACK_W = 1024
PACK_ROWS = 8


def _pack(arrs):
    pieces, layout, r0 = [], [], 0
    for a in arrs:
        size = math.prod(a.shape)
        rows = -(-size // (PACK_W * PACK_ROWS)) * PACK_ROWS
        pieces.append(jnp.pad(a.reshape(-1).astype(f32), (0, rows * PACK_W - size)).reshape(rows, PACK_W))
        layout.append((r0, rows, a.shape))
        r0 += rows
    return jnp.concatenate(pieces, axis=0), layout


def _unpack(pack, layout, lead=()):
    n_lead = len(lead)
    outs = []
    for r0, rows, shape in layout:
        piece = pack[(slice(None),) * n_lead + (slice(r0, r0 + rows),)].reshape(lead + (-1,))
        outs.append(piece[..., :math.prod(shape)].reshape(lead + tuple(shape)))
    return outs


W_NAMES = ("c_ctx", "w_ada", "b_ada", "norm1_g", "norm2_g", "w_in", "ret_log_decay", "rwkv_shift_mu", "rwkv_w0",
           "rwkv_w_up", "rwkv_a0", "rwkv_a_up", "rwkv_g_up", "rwkv_k_k", "rwkv_k_a", "rwkv_r_k", "rwkv_ln_w",
           "rwkv_ln_b", "w_out", "w_ff1", "b_ff1", "w_ff2", "b_ff2", "final_g")
COL_SHARDED = ("w_in", "w_ff1")
ROW_SHARDED = ("w_out", "w_ff2")
LAST_SHARDED = ("rwkv_shift_mu", "rwkv_w0", "rwkv_w_up", "rwkv_a0", "rwkv_a_up", "rwkv_g_up")
REPLICATED = ("c_ctx", "b_ada", "norm1_g", "norm2_g", "ret_log_decay", "rwkv_k_k", "rwkv_k_a", "rwkv_r_k",
              "rwkv_ln_w", "rwkv_ln_b", "b_ff1", "b_ff2", "final_g")
N_SHARDS = 4


def _train_step(a):
    x, c, ctx, tgt = a["x"], a["c"], a["ctx"], a["loss_target"]
    bsz = x.shape[0]
    mx, my, mc = _mesh_pos()
    shard = 2 * mx + my
    dev = _device_slot()

    (c_all,) = exchange([jnp.pad(c, ((0, PACK_ROWS - bsz), (0, 0)))], True, ALL_PEERS, "gather_c")
    n_ex = N_DEV * bsz
    c_rows = jnp.concatenate([c_all[:, :bsz].reshape(n_ex, D_MODEL), a["c_ctx"][None, :],
                              jnp.zeros((PACK_ROWS - 1, D_MODEL), f32)], axis=0)
    ada_cols = a["w_ada"].shape[-1]
    b_ada_cols = lax.dynamic_slice_in_dim(a["b_ada"], shard * ada_cols, ada_cols, axis=1)
    mod_cols = adaln_fwd(c_rows, a["w_ada"][0], b_ada_cols)

    def own_half(n):
        w = a[n][0].astype(MXU_DTYPE)
        return lax.dynamic_slice_in_dim(w, mc * (w.shape[0] // 2), w.shape[0] // 2, axis=0)

    def whole_weight(n, gth, own):
        per_chip = lax.dynamic_update_index_in_dim(gth, own, dev, 0).reshape(N_SHARDS, -1, gth.shape[-1])
        return (per_chip.transpose(1, 0, 2).reshape(per_chip.shape[1], -1) if n in COL_SHARDED
                else per_chip.reshape(-1, per_chip.shape[-1]))

    small_pack, small_layout = _pack([a[n][0] for n in LAST_SHARDED])
    own_blocks = [mod_cols, own_half("w_in"), small_pack]
    gathered = gather_two_level(own_blocks, "gather_weights")
    late_own = [own_half(n) for n in LATE_WEIGHTS]
    late_started = exchange_start(late_own, True, "gather_late_start")
    mod_own = lax.dynamic_update_index_in_dim(gathered[0], mod_cols, dev, 0)
    mod_all = jnp.stack([mod_own[s] for s in CHIP_SLOTS], axis=1).reshape(c_rows.shape[0], -1)
    mod_all = mod_all + late_started[-1][0, 0]
    mod_x = lax.dynamic_slice_in_dim(mod_all, dev * bsz, bsz, axis=0).reshape(bsz, 6, D_MODEL)
    mod_ctx = mod_all[n_ex].reshape(6, D_MODEL)
    wt = {"w_in": whole_weight("w_in", gathered[1], own_blocks[1])}

    def late_weights(after):
        lands = exchange_wait(late_started, after, True, "gather_late_wait")
        return {n: whole_weight(n, land, own) for n, land, own in zip(LATE_WEIGHTS, lands, late_own)}

    def grad_blocks(n, gw):
        if n in COL_SHARDED:
            gw = gw.reshape(gw.shape[0], N_SHARDS, -1).transpose(1, 0, 2)
        return gw.reshape(N_DEV, -1, gw.shape[-1]).astype(MXU_DTYPE)

    late_sent = {}

    def early_grads(late_g):
        late_sent["blocks"] = [grad_blocks(n, late_g[n]) for n in LATE_WEIGHTS]
        late_sent["started"] = exchange_start(late_sent["blocks"], False, "scatter_late_start")
        return late_sent["started"][-1]

    small_own = lax.dynamic_update_index_in_dim(gathered[2], small_pack, dev, 0)
    small_by_chip = _unpack(jnp.stack([small_own[s] for s in CHIP_SLOTS]), small_layout, (N_SHARDS,))
    for n, parts in zip(LAST_SHARDED, small_by_chip):
        wt[n] = jnp.concatenate([parts[s] for s in range(N_SHARDS)], axis=-1)
    for n in ("norm1_g", "norm2_g", "rwkv_k_k", "rwkv_k_a", "rwkv_r_k", "rwkv_ln_w", "rwkv_ln_b", "b_ff1", "b_ff2"):
        wt[n] = a[n]
    wt["ret_log_decay"] = a["ret_log_decay"][0]
    wt["final_g"] = a["final_g"][None, :]

    loss, grad_x, g = layer_step(x, ctx, tgt, mod_x, mod_ctx, wt, late_weights, early_grads)

    small_names = [n for n in REPLICATED if n not in ("c_ctx", "b_ada")]
    g_pack, g_layout = _pack([jnp.pad(loss, ((0, 0), (0, PACK_W - loss.shape[1])))] + [g[n] for n in small_names]
                             + [g["mod_x"], g["mod_ctx"]])
    (g_packs,) = gather_two_level([g_pack], "gather_small_grads")
    g_packs = lax.dynamic_update_index_in_dim(g_packs, g_pack, dev, 0)
    g_sum = _unpack(sum_slots(g_packs, tuple(range(N_DEV)), "sum_small_grads"), g_layout)
    loss_total = g_sum[0][0, 0]
    grads = dict(zip(small_names, g_sum[1:1 + len(small_names)]))
    dmod_ctx = g_sum[-1].reshape(1, -1)
    dmod_x = _unpack(g_packs, g_layout, (N_DEV,))[-2].reshape(n_ex, -1)
    dmod = jnp.concatenate([dmod_x, dmod_ctx, jnp.zeros((PACK_ROWS - 1, dmod_x.shape[1]), f32)], axis=0)
    grads["b_ada"] = column_sum(dmod, "b_ada_grad")
    dmod_cols = lax.dynamic_slice_in_dim(dmod, shard * ada_cols, ada_cols, axis=1)
    grads["w_ada"], dsilu = adaln_bwd(c_rows, dmod_cols, a["w_ada"][0])

    blocks = [grad_blocks("w_in", g["w_in"])]
    shard_packs = []
    for s in range(N_SHARDS):
        pieces_s = [lax.slice_in_dim(g[n], s * a[n].shape[-1], (s + 1) * a[n].shape[-1], axis=g[n].ndim - 1)
                    for n in LAST_SHARDED]
        pack_s, shard_layout = _pack(pieces_s)
        shard_packs.append(jnp.pad(pack_s, ((0, -pack_s.shape[0] % (2 * PACK_ROWS)), (0, 0))))
    blocks.append(jnp.stack(shard_packs).reshape(N_DEV, -1, PACK_W))
    scattered = ("w_in", "small_shards")
    halves_of = lambda blk, core: lax.dynamic_index_in_dim(
        blk.reshape(N_SHARDS, 2, *blk.shape[1:]), core, axis=1, keepdims=False).reshape(-1, blk.shape[-1])
    from_sibling = sibling_swap([halves_of(blk, 1 - mc) for blk in blocks], "prereduce_swap")
    chip_sums = [add_arrays([halves_of(blk, mc), got], f"prereduce_{n}", blk.dtype).reshape(N_SHARDS, -1, blk.shape[-1])
                 for n, blk, got in zip(scattered, blocks, from_sibling)]
    dsilu_rows = jnp.broadcast_to(jnp.pad(dsilu[n_ex:n_ex + 1], ((0, PACK_ROWS - 1), (0, 0)))[None],
                                  (N_SHARDS, PACK_ROWS, D_MODEL))
    to_chips = [dsilu_rows] + chip_sums
    received = exchange(to_chips, False, CHIP_PEERS, "scatter_big_grads", by_chip=True, own=False)
    received = [lax.dynamic_update_index_in_dim(got, lax.dynamic_index_in_dim(sent, shard, 0, keepdims=False), shard, 0)
                for got, sent in zip(received, to_chips)]
    grads["c_ctx"] = c_ctx_grad(received[0], a["c_ctx"][None, :])
    half_sums = [sum_slots(p, tuple(range(N_SHARDS)), f"sum_{n}") for n, p in zip(scattered, received[1:])]
    late_lands = exchange_wait(late_sent["started"], half_sums[0], False, "scatter_late_wait")
    for n, land, sent in zip(LATE_WEIGHTS, late_lands, late_sent["blocks"]):
        land = lax.dynamic_update_index_in_dim(land, lax.dynamic_index_in_dim(sent, dev, 0, keepdims=False), dev, 0)
        half_sums.append(sum_slots(land, tuple(range(N_DEV)), f"sum_{n}"))
    scattered = scattered + LATE_WEIGHTS
    other_halves = sibling_swap(half_sums, "swap_halves")
    for n, mine, other in zip(scattered, half_sums, other_halves):
        rows = mine.shape[0]
        whole = jnp.zeros((2 * rows, mine.shape[1]), f32)
        whole = lax.dynamic_update_slice_in_dim(whole, mine, mc * rows, axis=0)
        grads[n] = lax.dynamic_update_slice_in_dim(whole, other, (1 - mc) * rows, axis=0)
    grads.update(zip(LAST_SHARDED, _unpack(grads.pop("small_shards"), shard_layout)))

    out_g, out_d, out_m, out_v = {}, {}, {}, {}
    for n in ("w_ada",) + COL_SHARDED + ROW_SHARDED:
        out_g[n] = grads[n].reshape(a[n].shape)
        two_d = lambda z: z.reshape(-1, z.shape[-1])
        d, m, v = adamw(two_d(a[n]), two_d(out_g[n]), two_d(a["m_" + n]), two_d(a["v_" + n]), f"adamw_{n}")
        out_d[n], out_m[n], out_v[n] = d.reshape(a[n].shape), m.reshape(a[n].shape), v.reshape(a[n].shape)
    rest = REPLICATED + LAST_SHARDED
    for n in rest:
        out_g[n] = grads[n].reshape(a[n].shape)
    packs = [_pack([src[n] for n in rest])[0] for src in
             ({n: a[n] for n in rest}, out_g, {n: a["m_" + n] for n in rest}, {n: a["v_" + n] for n in rest})]
    _, rest_layout = _pack([a[n] for n in rest])
    for dst, pack in zip((out_d, out_m, out_v), adamw(*packs, "adamw_small")):
        dst.update(zip(rest, _unpack(pack, rest_layout)))
    return (loss_total, grad_x, *[out_g[n] for n in W_NAMES], *[out_d[n] for n in W_NAMES],
            *[out_m[n] for n in W_NAMES], *[out_v[n] for n in W_NAMES])


def kernel(x, c, ctx, c_ctx, w_ada, b_ada, norm1_g, norm2_g, w_in, ret_log_decay, rwkv_shift_mu, rwkv_w0, rwkv_w_up, rwkv_a0, rwkv_a_up, rwkv_g_up, rwkv_k_k, rwkv_k_a, rwkv_r_k, rwkv_ln_w, rwkv_ln_b, w_out, w_ff1, b_ff1, w_ff2, b_ff2, final_g, loss_target, m_c_ctx, m_w_ada, m_b_ada, m_norm1_g, m_norm2_g, m_w_in, m_ret_log_decay, m_rwkv_shift_mu, m_rwkv_w0, m_rwkv_w_up, m_rwkv_a0, m_rwkv_a_up, m_rwkv_g_up, m_rwkv_k_k, m_rwkv_k_a, m_rwkv_r_k, m_rwkv_ln_w, m_rwkv_ln_b, m_w_out, m_w_ff1, m_b_ff1, m_w_ff2, m_b_ff2, m_final_g, v_c_ctx, v_w_ada, v_b_ada, v_norm1_g, v_norm2_g, v_w_in, v_ret_log_decay, v_rwkv_shift_mu, v_rwkv_w0, v_rwkv_w_up, v_rwkv_a0, v_rwkv_a_up, v_rwkv_g_up, v_rwkv_k_k, v_rwkv_k_a, v_rwkv_r_k, v_rwkv_ln_w, v_rwkv_ln_b, v_w_out, v_w_ff1, v_b_ff1, v_w_ff2, v_b_ff2, v_final_g):
    return _train_step(dict(locals()))
```

```python
import functools
import math

import jax
import jax.numpy as jnp
from jax import lax
from jax.experimental import pallas as pl
from jax.experimental.pallas import tpu as pltpu

f32 = jnp.float32
MXU_DTYPE = jnp.bfloat16

D_MODEL = 1024
RET_W = 512
RET_HEADS = 4
RET_DH = 128
RET_CHUNK = 128
RW_W = 512
RW_N = 64
DECAY_LORA = 64
AAA_LORA = 64
GATE_LORA = 128
LORA_W = DECAY_LORA + AAA_LORA + GATE_LORA
D_FF = 4096
RET_COLS = 4 * RET_W
SHIFT_COLS = 3 * RW_W + LORA_W
IN_COLS = RET_COLS + SHIFT_COLS
GRID_W = 64
ROPE_BASE = 10000.0
NORM_EPS = 1e-6
GN_EPS = 64e-5
W_DECAY_SCALE = math.exp(-0.5)
ADAM_LR, ADAM_B1, ADAM_B2, ADAM_EPS, ADAM_WD, ADAM_STEP = 0.001, 0.9, 0.999, 1e-08, 0.01, 10

TOK_TILE = 256
MATMUL_TILE = 1024
SCAN_CHUNK = 32
SCAN_UNROLL = SCAN_CHUNK
N_DEV = 8
V7X_VMEM_BYTES = 64 * 1024 * 1024
VMEM_LIMIT = V7X_VMEM_BYTES * 7 // 8


def _cparams(sem):
    return pltpu.CompilerParams(dimension_semantics=sem, vmem_limit_bytes=VMEM_LIMIT)


def _tile(n, cap):
    best = None
    for t in range(128, min(n, cap) + 1, 128):
        if n % t == 0:
            best = t
    return best if best is not None else n


def matmul(a, b, mode, name, out_dtype=f32, bias=None, finish=None):
    if mode == "nn":
        (m, k), (k2, n) = a.shape, b.shape
    elif mode == "nt":
        (m, k), (n, k2) = a.shape, b.shape
    else:
        (k, m), (k2, n) = a.shape, b.shape
    assert k == k2, (a.shape, b.shape, mode)
    tm, tn, tk = _tile(m, MATMUL_TILE), _tile(n, MATMUL_TILE), _tile(k, MATMUL_TILE)
    nk = k // tk
    dims = {"nn": ((1,), (0,)), "nt": ((1,), (1,)), "tn": ((0,), (0,))}[mode]

    def body(a_ref, b_ref, *rest):
        o_ref, acc_ref = rest[-2:]
        kk = pl.program_id(2)

        @pl.when(kk == 0)
        def _():
            acc_ref[...] = jnp.zeros_like(acc_ref)

        acc_ref[...] += lax.dot_general(a_ref[...].astype(MXU_DTYPE), b_ref[...].astype(MXU_DTYPE),
                                        (dims, ((), ())), preferred_element_type=f32)

        @pl.when(kk == nk - 1)
        def _():
            res = acc_ref[...]
            if bias is not None:
                res = res + rest[0][...]
            if finish is not None:
                res = finish(res)
            o_ref[...] = res.astype(o_ref.dtype)

    if mode == "nn":
        a_spec = pl.BlockSpec((tm, tk), lambda i, j, q: (i, q))
        b_spec = pl.BlockSpec((tk, tn), lambda i, j, q: (q, j))
    elif mode == "nt":
        a_spec = pl.BlockSpec((tm, tk), lambda i, j, q: (i, q))
        b_spec = pl.BlockSpec((tn, tk), lambda i, j, q: (j, q))
    else:
        a_spec = pl.BlockSpec((tk, tm), lambda i, j, q: (q, i))
        b_spec = pl.BlockSpec((tk, tn), lambda i, j, q: (q, j))
    extra_specs = [] if bias is None else [pl.BlockSpec((1, tn), lambda i, j, q: (0, j))]
    extra = [] if bias is None else [bias]
    return pl.pallas_call(
        body, grid=(m // tm, n // tn, nk), in_specs=[a_spec, b_spec] + extra_specs,
        out_specs=pl.BlockSpec((tm, tn), lambda i, j, q: (i, j)),
        out_shape=jax.ShapeDtypeStruct((m, n), out_dtype),
        scratch_shapes=[pltpu.VMEM((tm, tn), f32)],
        compiler_params=_cparams(("parallel", "parallel", "arbitrary")), name=name)(a, b, *extra)


class Tiled:
    def __init__(self, arr, w=None, cidx=0, toff=0):
        self.arr, self.w, self.cidx, self.toff = arr, (arr.shape[-1] if w is None else w), cidx, toff

    def spec(self):
        cidx, toff = self.cidx, self.toff
        return pl.BlockSpec((None, TOK_TILE, self.w), lambda b, i: (b, jnp.maximum(i + toff, 0), cidx))


class Seg:
    def __init__(self, arr, seg, first):
        self.arr, self.seg, self.first = arr, seg, first

    def spec(self):
        seg = self.seg
        return pl.BlockSpec((None, None, 1, self.arr.shape[-1]), lambda b, i: (b, seg(i), 0, 0))


class Glob:
    def __init__(self, arr):
        self.arr = arr

    def spec(self):
        return pl.BlockSpec(self.arr.shape, lambda b, i: (0,) * self.arr.ndim)


def ew_forward(fn, name, bsz, n_tiles, ins, outs):
    n_in = len(ins)

    def body(*refs):
        res = fn(*[r[...] for r in refs[:n_in]])
        for o_ref, o in zip(refs[n_in:], res):
            o_ref[...] = o.astype(o_ref.dtype)

    out_specs = [pl.BlockSpec((None, TOK_TILE, w), lambda b, i: (b, i, 0)) for w, _ in outs]
    out_shape = [jax.ShapeDtypeStruct((bsz, n_tiles * TOK_TILE, w), dt) for w, dt in outs]
    return pl.pallas_call(body, grid=(bsz, n_tiles), in_specs=[d.spec() for d in ins], out_specs=out_specs,
                          out_shape=out_shape, compiler_params=_cparams(("parallel", "parallel")), name=name)(
        *[d.arr for d in ins])


def ew_backward(fn, name, bsz, n_tiles, ins, cts, want, grad_dtypes=None, lead=0):
    n_in, n_ct = len(ins), len(cts)
    diff = [k for k in range(n_in) if want[k]]
    grad_dtypes = grad_dtypes or {}
    assert lead == 0 or not any(isinstance(ins[k], Seg) for k in diff)

    def body(*refs):
        b, i = pl.program_id(0), pl.program_id(1)
        g_refs = refs[n_in + n_ct:]

        def tile_grads():
            vals = [r[...] for r in refs[:n_in]]
            ct_vals = tuple(r[...].astype(f32) for r in refs[n_in:n_in + n_ct])

            def f(*dvals):
                full = list(vals)
                for k, v in zip(diff, dvals):
                    full[k] = v
                return tuple(fn(*full))

            _, vjp = jax.vjp(f, *[vals[k] for k in diff])
            grads = vjp(ct_vals)
            for k, g_ref, g in zip(diff, g_refs, grads):
                d = ins[k]
                if isinstance(d, Tiled):
                    g_ref[...] = g.astype(g_ref.dtype)
                else:
                    zero = d.first(i) if isinstance(d, Seg) else jnp.logical_and(b == 0, i == lead)

                    @pl.when(zero)
                    def _(g_ref=g_ref):
                        g_ref[...] = jnp.zeros_like(g_ref)

                    g_ref[...] += g

        if lead == 0:
            tile_grads()
        else:
            pl.when(i >= lead)(tile_grads)

            @pl.when(i < lead)
            def _():
                for k, g_ref in zip(diff, g_refs):
                    if isinstance(ins[k], Tiled):
                        g_ref[...] = jnp.zeros_like(g_ref)

    out_specs, out_shape = [], []
    for k in diff:
        d = ins[k]
        if isinstance(d, Tiled):
            out_specs.append(pl.BlockSpec((None, TOK_TILE, d.w), lambda b, i: (b, i, 0)))
            out_shape.append(jax.ShapeDtypeStruct((bsz, (n_tiles + lead) * TOK_TILE, d.w), grad_dtypes.get(k, f32)))
        else:
            out_specs.append(d.spec())
            out_shape.append(jax.ShapeDtypeStruct(d.arr.shape, f32))
    return pl.pallas_call(body, grid=(bsz, n_tiles + lead),
                          in_specs=[d.spec() for d in ins] + [c.spec() for c in cts],
                          out_specs=out_specs, out_shape=out_shape,
                          compiler_params=_cparams(("arbitrary", "arbitrary")), name=name)(
        *[d.arr for d in ins], *[c.arr for c in cts])


@jax.custom_vjp
def _mxu_dot(a, b):
    return jnp.dot(a.astype(MXU_DTYPE), b.astype(MXU_DTYPE), preferred_element_type=f32)


def _mxu_dot_fwd(a, b):
    return _mxu_dot(a, b), (a, b)


def _mxu_dot_bwd(res, ct):
    a, b = res
    ct = ct.astype(MXU_DTYPE)
    da = lax.dot_general(ct, b.astype(MXU_DTYPE), (((1,), (1,)), ((), ())), preferred_element_type=f32)
    db = lax.dot_general(a.astype(MXU_DTYPE), ct, (((0,), (0,)), ((), ())), preferred_element_type=f32)
    return da, db


_mxu_dot.defvjp(_mxu_dot_fwd, _mxu_dot_bwd)


def _split_dot_impl(x, ones_mat):
    hi = x.astype(MXU_DTYPE)
    lo = (x - hi.astype(f32)).astype(MXU_DTYPE)
    return jnp.dot(hi, ones_mat, preferred_element_type=f32) + jnp.dot(lo, ones_mat, preferred_element_type=f32)


@jax.custom_vjp
def _split_dot(x, ones_mat):
    return _split_dot_impl(x, ones_mat)


def _split_dot_fwd(x, ones_mat):
    return _split_dot_impl(x, ones_mat), ones_mat


def _split_dot_bwd(ones_mat, ct):
    return _split_dot_impl(ct, ones_mat), None


_split_dot.defvjp(_split_dot_fwd, _split_dot_bwd)


def _block_ones(n, group):
    idx = jnp.arange(n) // group
    return (idx[:, None] == idx[None, :]).astype(MXU_DTYPE)


def _rms(x, g):
    return x * lax.rsqrt(jnp.mean(x * x, axis=-1, keepdims=True) + NORM_EPS) * g


def fn_norm_mod(h, shift, scale, g):
    return (_rms(h, g) * (1.0 + scale) + shift,)


def fn_rwkv_prepare(ks, lora, w0_f, w0_b, a0_f, a0_b, w_up_f, w_up_b, a_up_f, a_up_b, g_up, k_k, k_a, ones64):
    kkr = ks * k_k
    kk = kkr * lax.rsqrt(_split_dot(kkr * kkr, ones64) + 1e-12)
    outs = [kk]
    th = jnp.tanh(lora)
    for w0, a0, w_up, a_up in ((w0_f, a0_f, w_up_f, a_up_f), (w0_b, a0_b, w_up_b, a_up_b)):
        w = jnp.exp(-W_DECAY_SCALE * jax.nn.sigmoid(w0 + _mxu_dot(th, w_up)))
        a = jax.nn.sigmoid(a0 + _mxu_dot(lora, a_up))
        kt = ks * (1.0 + (a - 1.0) * k_a)
        outs += [w, a * kk, kt]
    outs.append(_mxu_dot(jax.nn.sigmoid(lora), g_up))
    return tuple(outs)


def fn_merge(o_f, o_b, g_ret, y_f, y_b, r, kt_f, v, g_rw, r_k, ln_w, ln_b, ones64, ones128):
    o = o_f + o_b
    ret = o * lax.rsqrt(_split_dot(o * o, ones128) * (1.0 / RET_DH) + NORM_EPS) * (g_ret * jax.nn.sigmoid(g_ret))
    y = y_f + y_b
    mean = _split_dot(y, ones64) * (1.0 / RW_N)
    yc = y - mean
    var = _split_dot(yc * yc, ones64) * (1.0 / RW_N)
    y_n = yc * lax.rsqrt(var + GN_EPS) * ln_w + ln_b
    bonus = _split_dot(r * kt_f * r_k, ones64) * v
    return ret, (y_n + bonus) * g_rw


def fn_resid_norm_mod(x, mix, gate, shift, scale, g):
    h1 = x + gate * mix
    return h1, _rms(h1, g) * (1.0 + scale) + shift


def relu2(z):
    return jnp.square(jnp.maximum(z, 0.0))


def relu2_backward(act, dact, name):
    bsz, n_tok, width = act.shape

    def body(a_ref, d_ref, du_ref, db_ref):
        du = d_ref[...].astype(f32) * (2.0 * jnp.sqrt(a_ref[...].astype(f32)))
        du_ref[...] = du.astype(du_ref.dtype)

        @pl.when(jnp.logical_and(pl.program_id(0) == 0, pl.program_id(1) == 0))
        def _():
            db_ref[...] = jnp.zeros_like(db_ref)

        db_ref[...] += jnp.sum(du, axis=0, keepdims=True)

    tile = pl.BlockSpec((None, TOK_TILE, width), lambda b, i: (b, i, 0))
    row = pl.BlockSpec((1, width), lambda b, i: (0, 0))
    return pl.pallas_call(body, grid=(bsz, n_tok // TOK_TILE), in_specs=[tile, tile], out_specs=[tile, row],
                          out_shape=[jax.ShapeDtypeStruct(act.shape, MXU_DTYPE), jax.ShapeDtypeStruct((1, width), f32)],
                          compiler_params=_cparams(("arbitrary", "arbitrary")), name=name)(act, dact)


def fn_loss(h1, f, tgt, gate, b2, g):
    y = _rms(h1 + gate * (f + b2), g)
    err = jnp.square(y - tgt)
    return 0.5 * jnp.sum(jnp.mean(err, axis=-1, keepdims=True), axis=0, keepdims=True)


def loss_and_grads(h1, f, tgt, gate, b2, g, bsz, n_tiles):
    def body(h1_ref, f_ref, t_ref, gate_ref, b2_ref, g_ref, loss_ref, dh1_ref, df_ref, dgate_ref, db2_ref, dg_ref):
        b, i = pl.program_id(0), pl.program_id(1)
        tgt_v = t_ref[...]
        loss, vjp = jax.vjp(lambda a, c, e, p, q: fn_loss(a, c, tgt_v, e, p, q),
                            h1_ref[...], f_ref[...], gate_ref[...], b2_ref[...], g_ref[...])
        dh1, df, dgate, db2, dg = vjp(jnp.ones((1, 1), f32))
        dh1_ref[...] = dh1
        df_ref[...] = df.astype(df_ref.dtype)

        @pl.when(i == 0)
        def _():
            dgate_ref[...] = jnp.zeros_like(dgate_ref)

        @pl.when(jnp.logical_and(b == 0, i == 0))
        def _():
            loss_ref[...] = jnp.zeros_like(loss_ref)
            db2_ref[...] = jnp.zeros_like(db2_ref)
            dg_ref[...] = jnp.zeros_like(dg_ref)

        dgate_ref[...] += dgate
        db2_ref[...] += db2
        dg_ref[...] += dg
        loss_ref[...] += jnp.broadcast_to(loss, loss_ref.shape)

    tile = pl.BlockSpec((None, TOK_TILE, D_MODEL), lambda b, i: (b, i, 0))
    row = pl.BlockSpec((1, D_MODEL), lambda b, i: (0, 0))
    seg = pl.BlockSpec((None, None, 1, D_MODEL), lambda b, i: (b, 0, 0, 0))
    t_tok = n_tiles * TOK_TILE
    return pl.pallas_call(
        body, grid=(bsz, n_tiles), in_specs=[tile, tile, tile, seg, row, row],
        out_specs=[pl.BlockSpec((1, 128), lambda b, i: (0, 0)), tile, tile, seg, row, row],
        out_shape=[jax.ShapeDtypeStruct((1, 128), f32), jax.ShapeDtypeStruct((bsz, t_tok, D_MODEL), f32),
                   jax.ShapeDtypeStruct((bsz, t_tok, D_MODEL), MXU_DTYPE),
                   jax.ShapeDtypeStruct((bsz, 1, 1, D_MODEL), f32),
                   jax.ShapeDtypeStruct((1, D_MODEL), f32), jax.ShapeDtypeStruct((1, D_MODEL), f32)],
        compiler_params=_cparams(("arbitrary", "arbitrary")), name="loss_and_grads")(h1, f, tgt, gate, b2, g)


SHIFT_BLOCK = SHIFT_COLS
HALO_ROWS = 8


def _shift_specs(n_tok, col0):
    per_tile = TOK_TILE // HALO_ROWS
    last = n_tok // HALO_ROWS - 1
    tile = pl.BlockSpec((None, TOK_TILE, SHIFT_BLOCK), lambda j, b, i: (b, i, col0 + j))
    prev = pl.BlockSpec((None, HALO_ROWS, SHIFT_BLOCK),
                        lambda j, b, i: (b, jnp.maximum(i * per_tile - 1, 0), col0 + j))
    nxt = pl.BlockSpec((None, HALO_ROWS, SHIFT_BLOCK),
                       lambda j, b, i: (b, jnp.minimum((i + 1) * per_tile, last), col0 + j))
    return tile, prev, nxt


def _shifted(p, prev_ref, next_ref, is_first, is_last):
    row = lax.broadcasted_iota(jnp.int32, p.shape, 0)
    prev_row = jnp.where(is_first, 0.0, prev_ref[HALO_ROWS - 1:HALO_ROWS, :].astype(f32))
    next_row = jnp.where(is_last, 0.0, next_ref[0:1, :].astype(f32))
    prev = jnp.where(row == 0, prev_row, pltpu.roll(p, 1, axis=0))
    nxt = jnp.where(row == TOK_TILE - 1, next_row, pltpu.roll(p, TOK_TILE - 1, axis=0))
    return prev, nxt


def token_shift(px, mu, seg_first, seg_last):
    bsz, n_tok, _ = px.shape
    n_tiles = n_tok // TOK_TILE

    def body(p_ref, prev_ref, next_ref, mu_ref, o_ref):
        i = pl.program_id(2)
        p = p_ref[...]
        prev, nxt = _shifted(p, prev_ref, next_ref, seg_first(i), seg_last(i))
        o_ref[...] = p + mu_ref[0:1, :] * (prev - p) + mu_ref[1:2, :] * (nxt - p)

    tile, prev, nxt = _shift_specs(n_tok, 0)
    return pl.pallas_call(
        body, grid=(SHIFT_COLS // SHIFT_BLOCK, bsz, n_tiles),
        in_specs=[tile, prev, nxt, pl.BlockSpec((2, SHIFT_BLOCK), lambda j, b, i: (0, j))],
        out_specs=pl.BlockSpec((None, TOK_TILE, SHIFT_BLOCK), lambda j, b, i: (b, i, j)),
        out_shape=jax.ShapeDtypeStruct((bsz, n_tok, SHIFT_COLS), f32),
        compiler_params=_cparams(("parallel", "parallel", "parallel")), name="token_shift")(px, px, px, mu)


def token_shift_bwd(dps, px, mu, seg_first, seg_last):
    bsz, n_tok, _ = px.shape
    n_tiles = n_tok // TOK_TILE

    def body(d_ref, dprev_ref, dnext_ref, p_ref, prev_ref, next_ref, mu_ref, dp_ref, dmu_ref):
        b, i = pl.program_id(1), pl.program_id(2)
        first, last = seg_first(i), seg_last(i)
        d, p = d_ref[...], p_ref[...]
        d_prev, d_next = _shifted(d, dprev_ref, dnext_ref, first, last)
        p_prev, p_next = _shifted(p, prev_ref, next_ref, first, last)
        mu0, mu1 = mu_ref[0:1, :], mu_ref[1:2, :]
        dp_ref[...] = (d + mu0 * (d_next - d) + mu1 * (d_prev - d)).astype(dp_ref.dtype)

        @pl.when(jnp.logical_and(b == 0, i == 0))
        def _():
            dmu_ref[...] = jnp.zeros_like(dmu_ref)

        dmu_ref[0:1, :] += jnp.sum(d * (p_prev - p), axis=0, keepdims=True)
        dmu_ref[1:2, :] += jnp.sum(d * (p_next - p), axis=0, keepdims=True)

    dtile, dprev, dnext = _shift_specs(n_tok, 0)
    tile, prev, nxt = _shift_specs(n_tok, 0)
    mu_spec = pl.BlockSpec((2, SHIFT_BLOCK), lambda j, b, i: (0, j))
    return pl.pallas_call(
        body, grid=(SHIFT_COLS // SHIFT_BLOCK, bsz, n_tiles),
        in_specs=[dtile, dprev, dnext, tile, prev, nxt, mu_spec],
        out_specs=[pl.BlockSpec((None, TOK_TILE, SHIFT_BLOCK), lambda j, b, i: (b, i, j)), mu_spec],
        out_shape=[jax.ShapeDtypeStruct((bsz, n_tok, SHIFT_COLS), MXU_DTYPE),
                   jax.ShapeDtypeStruct((2, SHIFT_COLS), f32)],
        compiler_params=_cparams(("arbitrary", "arbitrary", "arbitrary")), name="token_shift_bwd")(
        dps, dps, dps, px, px, px, mu)


def _dg(a, b, ca, cb):
    return lax.dot_general(a.astype(MXU_DTYPE), b.astype(MXU_DTYPE), (((ca,), (cb,)), ((), ())),
                           preferred_element_type=f32)


@jax.custom_vjp
def _mm_nt(a, b):
    return _dg(a, b, 1, 1)


_mm_nt.defvjp(lambda a, b: (_dg(a, b, 1, 1), (a, b)),
              lambda res, ct: (_dg(ct, res[1], 1, 0), _dg(ct, res[0], 0, 0)))


@jax.custom_vjp
def _mm_tn(a, b):
    return _dg(a, b, 0, 0)


_mm_tn.defvjp(lambda a, b: (_dg(a, b, 0, 0), (a, b)),
              lambda res, ct: (_dg(res[1], ct, 1, 1), _dg(res[0], ct, 1, 0)))


ROTARY_PAIR = RET_DH // 4


def _swap_pairs_impl(t):
    lane = lax.broadcasted_iota(jnp.int32, t.shape, 1)
    return jnp.where(lane % (2 * ROTARY_PAIR) < ROTARY_PAIR, pltpu.roll(t, RET_DH - ROTARY_PAIR, axis=1),
                     pltpu.roll(t, ROTARY_PAIR, axis=1))


@jax.custom_vjp
def _swap_pairs(t):
    return _swap_pairs_impl(t)


_swap_pairs.defvjp(lambda t: (_swap_pairs_impl(t), None), lambda _, ct: (_swap_pairs_impl(ct),))


def _ret_chunk(state, q_raw, k_raw, v, cos, sin, ld_row, head, reverse):
    c = RET_CHUNK
    lane = lax.broadcasted_iota(jnp.int32, ld_row.shape, 1)
    lg = -jnp.exp(jnp.sum(jnp.where(lane == head, ld_row, 0.0), axis=-1, keepdims=True))
    rot = lambda t: t * cos + _swap_pairs(t) * sin
    q = rot(q_raw)
    k = rot(k_raw) * (RET_DH ** -0.5)
    ti = lax.broadcasted_iota(jnp.int32, (c, 1), 0).astype(f32)
    tj = lax.broadcasted_iota(jnp.int32, (1, c), 1).astype(f32)
    if not reverse:
        dist, mask, q_exp, k_exp = ti - tj, (ti - tj) >= 0, ti + 1.0, c - 1.0 - ti
    else:
        dist, mask, q_exp, k_exp = tj - ti, (tj - ti) > 0, c - ti, ti
    decay = jnp.where(mask, jnp.exp(lg * jnp.maximum(dist, 0.0)), 0.0)
    scores = _mm_nt(q, k) * decay
    out = _mxu_dot(scores, v) + _mxu_dot(q * jnp.exp(lg * q_exp), state)
    new_state = state * jnp.exp(lg * c) + _mm_tn(k * jnp.exp(lg * k_exp), v)
    return out, new_state


def _ret_specs(bsz, order):
    tok = lambda col=0: pl.BlockSpec((bsz, RET_CHUNK, RET_W), lambda i: (0, order(i), col))
    tab = pl.BlockSpec((RET_CHUNK, RET_DH), lambda i: (order(i), 0))
    ld = pl.BlockSpec((1, RET_DH), lambda i: (0, 0))
    return tok, tab, ld


def retention_fwd(px, cos, sin, ld_row, order, reverse, name):
    bsz, n_tok, _ = px.shape
    n_ch = n_tok // RET_CHUNK

    def body(q_ref, k_ref, v_ref, cos_ref, sin_ref, ld_ref, o_ref, sv_ref, st_ref):
        @pl.when(pl.program_id(0) == 0)
        def _():
            st_ref[...] = jnp.zeros_like(st_ref)

        for b in range(bsz):
            for h in range(RET_HEADS):
                sl = slice(h * RET_DH, (h + 1) * RET_DH)
                s = st_ref[b, h]
                sv_ref[b, h] = s
                o, s_new = _ret_chunk(s, q_ref[b, :, sl], k_ref[b, :, sl], v_ref[b, :, sl], cos_ref[...], sin_ref[...],
                                      ld_ref[...], h, reverse)
                o_ref[b, :, sl] = o
                st_ref[b, h] = s_new

    tok, tab, ld = _ret_specs(bsz, order)
    return pl.pallas_call(
        body, grid=(n_ch,), in_specs=[tok(0), tok(1), tok(2), tab, tab, ld],
        out_specs=[tok(), pl.BlockSpec((bsz, None, RET_HEADS, RET_DH, RET_DH), lambda i: (0, i, 0, 0, 0))],
        out_shape=[jax.ShapeDtypeStruct((bsz, n_tok, RET_W), f32),
                   jax.ShapeDtypeStruct((bsz, n_ch, RET_HEADS, RET_DH, RET_DH), f32)],
        scratch_shapes=[pltpu.VMEM((bsz, RET_HEADS, RET_DH, RET_DH), f32)],
        compiler_params=_cparams(("arbitrary",)), name=name)(px, px, px, cos, sin, ld_row)


def retention_bwd(do, px, states, cos, sin, ld_row, order, reverse, name):
    bsz, n_tok, _ = px.shape
    n_ch = n_tok // RET_CHUNK
    back = lambda i: order(n_ch - 1 - i)

    def body(do_ref, q_ref, k_ref, v_ref, sv_ref, cos_ref, sin_ref, ld_ref,
             dq_ref, dk_ref, dv_ref, dld_ref, dst_ref):
        @pl.when(pl.program_id(0) == 0)
        def _():
            dst_ref[...] = jnp.zeros_like(dst_ref)
            dld_ref[...] = jnp.zeros_like(dld_ref)

        cos_v, sin_v = cos_ref[...], sin_ref[...]
        for b in range(bsz):
            for h in range(RET_HEADS):
                sl = slice(h * RET_DH, (h + 1) * RET_DH)
                f = lambda s, q, k, v, ld, h=h: _ret_chunk(s, q, k, v, cos_v, sin_v, ld, h, reverse)
                _, vjp = jax.vjp(f, sv_ref[b, h], q_ref[b, :, sl], k_ref[b, :, sl], v_ref[b, :, sl], ld_ref[...])
                ds, dq, dk, dv, dld = vjp((do_ref[b, :, sl], dst_ref[b, h]))
                dst_ref[b, h] = ds
                dq_ref[b, :, sl] = dq
                dk_ref[b, :, sl] = dk
                dv_ref[b, :, sl] = dv
                dld_ref[...] += dld

    tok, tab, ld = _ret_specs(bsz, back)
    return pl.pallas_call(
        body, grid=(n_ch,),
        in_specs=[tok(), tok(0), tok(1), tok(2),
                  pl.BlockSpec((bsz, None, RET_HEADS, RET_DH, RET_DH), lambda i: (0, n_ch - 1 - i, 0, 0, 0)),
                  tab, tab, ld],
        out_specs=[tok(), tok(), tok(), ld],
        out_shape=[jax.ShapeDtypeStruct((bsz, n_tok, RET_W), f32)] * 3 + [jax.ShapeDtypeStruct((1, RET_DH), f32)],
        scratch_shapes=[pltpu.VMEM((bsz, RET_HEADS, RET_DH, RET_DH), f32)],
        compiler_params=_cparams(("arbitrary",)), name=name)(
        do, px, px, px, states, cos, sin, ld_row)


HALF_W = RW_W // 2


def _head_sum(x, ones):
    xm = x.astype(MXU_DTYPE)
    return jnp.concatenate([jnp.dot(xm[:, :HALF_W], ones, preferred_element_type=f32),
                            jnp.dot(xm[:, HALF_W:], ones, preferred_element_type=f32)], axis=1)


def _stack(parts):
    return jnp.concatenate(parts, axis=0)


def _row(ref, b, t):
    return ref[b, pl.ds(t, 1), :]


SCAN_DIRS = ((False, True), (True, False))
RW_HEADS = RW_W // RW_N
HEAD_ROWS_PAD = 16


def _head_rows(row, mask):
    return jnp.broadcast_to(row, mask.shape) * mask


def _outer(per_value, row, mask_pad):
    return lax.dot_general(per_value.astype(MXU_DTYPE), _head_rows(row, mask_pad).astype(MXU_DTYPE),
                           (((0,), (0,)), ((), ())), preferred_element_type=f32)


def _read(states, rows, mask, more_rows=()):
    lhs = _stack([_head_rows(r, mask) for r in list(rows) + list(more_rows)])
    return lax.dot_general(lhs.astype(MXU_DTYPE), _stack(states).astype(MXU_DTYPE), (((1,), (1,)), ((), ())),
                           preferred_element_type=f32)


def _own_block(raw, b):
    lanes = raw[:, RW_N * b:RW_N * (b + 1)]
    turned = _stack([lanes[RW_HEADS * b:], lanes[:RW_HEADS * b]]) if b else lanes
    if turned.shape[0] < HEAD_ROWS_PAD:
        turned = _stack([turned, jnp.zeros((HEAD_ROWS_PAD - turned.shape[0], RW_N), f32)])
    return turned[:HEAD_ROWS_PAD]


def _row_from_heads(per_value, state, mask_pad):
    full = jnp.dot(per_value.astype(MXU_DTYPE), state.astype(MXU_DTYPE), preferred_element_type=f32)
    return jnp.sum(full * mask_pad, axis=0, keepdims=True)


def _scan_specs(bsz, order):
    rows = lambda col=0: pl.BlockSpec((bsz, SCAN_CHUNK, RW_W), lambda i: (0, order(i), col))
    per_value = pl.BlockSpec((bsz, SCAN_CHUNK, HEAD_ROWS_PAD, RW_N), lambda i: (0, order(i), 0, 0))
    states = pl.BlockSpec((SCAN_CHUNK, bsz, RW_N, RW_W), lambda i: (order(i), 0, 0, 0))
    blocks = pl.BlockSpec((SCAN_CHUNK, RW_HEADS * bsz, RW_N * bsz), lambda i: (order(i), 0, 0))
    return rows, per_value, states, blocks


def _mxu_operands(states):
    return [s.astype(MXU_DTYPE) for s in states]


def _removed(states_m, kk_t, ones, bsz):
    removed = _head_sum(_stack([states_m[b] * kk_t[b].astype(MXU_DTYPE) for b in range(bsz)]), ones)
    return [removed[b * RW_N:(b + 1) * RW_N] for b in range(bsz)]


def _advance(sp, rem, w_t, b_t, vk, bsz):
    return [sp[b] * w_t[b] - rem[b] * b_t[b] + vk[b] for b in range(bsz)]


def heads_to_rows(a):
    b, t, _ = a.shape
    return jnp.pad(a.astype(MXU_DTYPE).reshape(b, t, RW_HEADS, RW_N),
                   ((0, 0), (0, 0), (0, HEAD_ROWS_PAD - RW_HEADS), (0, 0)))


def _blocks_to_rows(raw_ref, first, row_ref, bsz):
    steps = pl.ds(first, SCAN_CHUNK)
    for b in range(bsz):
        for h in range(RW_HEADS):
            row_ref[b, :, h * RW_N:(h + 1) * RW_N] = raw_ref[steps, RW_HEADS * b + h, RW_N * b:RW_N * (b + 1)]


N_ROWS_FWD = 5
N_ROWS_BWD = 5


def _scan_consts(bsz):
    head = (jnp.arange(RW_W)[None, :] // RW_N == jnp.arange(RW_HEADS)[:, None]).astype(f32)
    return head, jnp.pad(head, ((0, HEAD_ROWS_PAD - RW_HEADS), (0, 0))), _block_ones(HALF_W, RW_N)


def _const_specs(consts):
    return [pl.BlockSpec(c.shape, lambda i: (0, 0)) for c in consts]


def rwkv_scan_fwd(rows_in, v_heads, orders, name):
    bsz, n_tok, _ = rows_in[0][0][0].shape
    n_ch = n_tok // SCAN_CHUNK
    rng = range(bsz)
    consts = _scan_consts(bsz)

    def body(*refs):
        rows = [refs[:N_ROWS_FWD], refs[N_ROWS_FWD:2 * N_ROWS_FWD]]
        (v0, v1, head_ref, pad_ref, ones_ref, y0, y1, h0, h1, f0, f1, m0, m1, s0, s1, late_ref,
         raw_ref) = refs[2 * N_ROWS_FWD:]
        v_refs, y_refs, hist_refs, final_refs, s_refs = (v0, v1), (y0, y1), (h0, h1), (f0, f1), (s0, s1)
        removed_refs = (m0, m1)
        n_blk = RW_HEADS * bsz
        head_v, pad_v, ones_v = head_ref[...], pad_ref[...], ones_ref[...]
        for d in range(2):
            @pl.when(pl.program_id(0) == 0)
            def _(d=d):
                s_refs[d][...] = jnp.zeros_like(s_refs[d])

        def step(j, carry):
            ts = [SCAN_CHUNK - 1 - j if reverse else j for reverse, _ in SCAN_DIRS]
            sps = [[s_refs[d][b] for b in rng] for d in range(2)]
            sps_m = [_mxu_operands(sps[d]) for d in range(2)]
            vks = [[_outer(v_refs[d][b, ts[d]], _row(rows[d][4], b, ts[d]), pad_v) for b in rng] for d in range(2)]
            rems = [_removed(sps_m[d], [_row(rows[d][1], b, ts[d]) for b in rng], ones_v, bsz) for d in range(2)]
            for d, (reverse, inclusive) in enumerate(SCAN_DIRS):
                r_ref = rows[d][0]
                read_at = jnp.maximum(j - 1, 0) if inclusive else ts[d]
                both = _read(sps_m[d], [_row(r_ref, b, read_at) for b in rng], head_v,
                             [_row(rows[d][1], b, ts[d]) for b in rng])
                if inclusive:
                    late_ref[j] = both[:n_blk]
                else:
                    raw_ref[ts[d]] = both[:n_blk]
                removed_refs[d][ts[d]] = both[n_blk:]
            for d in range(2):
                new = _advance(sps[d], rems[d], [_row(rows[d][2], b, ts[d]) for b in rng],
                               [_row(rows[d][3], b, ts[d]) for b in rng], vks[d], bsz)
                for b in rng:
                    hist_refs[d][ts[d], b] = sps_m[d][b]
                    s_refs[d][b] = new[b]
            return carry

        lax.fori_loop(0, SCAN_CHUNK, step, 0, unroll=SCAN_UNROLL)
        for d, (reverse, inclusive) in enumerate(SCAN_DIRS):
            final_refs[d][...] = s_refs[d][...]
            if inclusive:
                assert not reverse
                last = SCAN_CHUNK - 1
                late_ref[SCAN_CHUNK] = _read(_mxu_operands([s_refs[d][b] for b in rng]),
                                             [rows[d][0][b, last:last + 1, :] for b in rng], head_v)
                _blocks_to_rows(late_ref, 1, y_refs[d], bsz)
            else:
                _blocks_to_rows(raw_ref, 0, y_refs[d], bsz)

    specs = [_scan_specs(bsz, orders[d]) for d in range(2)]
    state = pltpu.VMEM((bsz, RW_N, RW_W), f32)
    late = pltpu.VMEM((SCAN_CHUNK + 1, RW_HEADS * bsz, RW_N * bsz), f32)
    raw = pltpu.VMEM((SCAN_CHUNK, RW_HEADS * bsz, RW_N * bsz), f32)
    final_spec = pl.BlockSpec((bsz, RW_N, RW_W), lambda i: (0, 0, 0))
    return pl.pallas_call(
        body, grid=(n_ch,),
        in_specs=[specs[d][0](col) for d in range(2) for _, col in rows_in[d]] + [specs[0][1], specs[1][1]]
        + _const_specs(consts),
        out_specs=[specs[0][0](), specs[1][0](), specs[0][2], specs[1][2], final_spec, final_spec,
                   specs[0][3], specs[1][3]],
        out_shape=[jax.ShapeDtypeStruct((bsz, n_tok, RW_W), f32)] * 2
        + [jax.ShapeDtypeStruct((n_tok, bsz, RW_N, RW_W), MXU_DTYPE)] * 2
        + [jax.ShapeDtypeStruct((bsz, RW_N, RW_W), f32)] * 2
        + [jax.ShapeDtypeStruct((n_tok, RW_HEADS * bsz, RW_N * bsz), f32)] * 2,
        scratch_shapes=[state, state, late, raw],
        compiler_params=_cparams(("arbitrary",)), name=name)(
        *[a for d in range(2) for a, _ in rows_in[d]], v_heads, v_heads, *consts)


def rwkv_scan_bwd(rows_in, v_heads, dy_heads, hists, finals, removed, orders, name):
    bsz, n_tok, _ = rows_in[0][0][0].shape
    n_ch = n_tok // SCAN_CHUNK
    backs = [functools.partial(lambda i, order: order(n_ch - 1 - i), order=orders[d]) for d in range(2)]
    rng = range(bsz)
    consts = _scan_consts(bsz)
    n_out, n_scr = 6, 6

    def body(*refs):
        rows = [refs[:N_ROWS_BWD], refs[N_ROWS_BWD:2 * N_ROWS_BWD]]
        rest = refs[2 * N_ROWS_BWD:]
        v_refs, dy_refs, hist_refs, final_refs, removed_refs = rest[0:2], rest[2:4], rest[4:6], rest[6:8], rest[8:10]
        head_ref, pad_ref, ones_ref = rest[10:13]
        outs = [rest[13:13 + n_out], rest[13 + n_out:13 + 2 * n_out]]
        scr = [rest[13 + 2 * n_out:13 + 2 * n_out + n_scr], rest[13 + 2 * n_out + n_scr:]]
        n_blk = RW_HEADS * bsz
        head_v, pad_v, ones_v = head_ref[...], pad_ref[...], ones_ref[...]
        for d in range(2):
            @pl.when(pl.program_id(0) == 0)
            def _(d=d):
                scr[d][1][...] = jnp.zeros_like(scr[d][1])
                scr[d][0][...] = final_refs[d][...]

        def step_of(j, reverse):
            return j if reverse else SCAN_CHUNK - 1 - j

        for d, (reverse, _) in enumerate(SCAN_DIRS):
            t0 = step_of(0, reverse)
            for b in rng:
                scr[d][3][b] = _outer(dy_refs[d][b, t0], rows[d][0][b, t0:t0 + 1, :], pad_v)

        def bstep(j, carry):
            ts = [step_of(j, reverse) for reverse, _ in SCAN_DIRS]
            reads = [[scr[d][3][b] for b in rng] for d in range(2)]
            dss = []
            for d, (_, inclusive) in enumerate(SCAN_DIRS):
                ds = [scr[d][1][b] for b in rng]
                dss.append([ds[b] + reads[d][b] for b in rng] if inclusive else ds)
            dss_m = [_mxu_operands(dss[d]) for d in range(2)]
            nexts = []
            for d, (reverse, _) in enumerate(SCAN_DIRS):
                t_next = step_of(jnp.minimum(j + 1, SCAN_CHUNK - 1), reverse)
                nexts.append([_outer(dy_refs[d][b, t_next], _row(rows[d][0], b, t_next), pad_v) for b in rng])
            drems = [_removed(dss_m[d], [-_row(rows[d][3], b, ts[d]) for b in rng], ones_v, bsz) for d in range(2)]
            for d in range(2):
                for b in rng:
                    scr[d][3][b] = nexts[d][b]
                both = _read(dss_m[d], [_row(rows[d][4], b, ts[d]) for b in rng], head_v,
                             [-_row(rows[d][3], b, ts[d]) for b in rng])
                scr[d][4][ts[d]] = both[:n_blk]
                scr[d][5][ts[d]] = both[n_blk:]
            for d, (_, inclusive) in enumerate(SCAN_DIRS):
                _, kk_ref, w_ref, _, _ = rows[d]
                _, ds_ref, dsh_ref = scr[d][:3]
                for b in rng:
                    dsh_ref[ts[d], b] = dss[d][b]
                    dsp = dss[d][b] * _row(w_ref, b, ts[d]) + drems[d][b] * _row(kk_ref, b, ts[d])
                    ds_ref[b] = dsp if inclusive else dsp + reads[d][b]
            return carry

        lax.fori_loop(0, SCAN_CHUNK, bstep, 0, unroll=SCAN_UNROLL)

        rsum = lambda z: jnp.sum(z, axis=0, keepdims=True)
        for d, (reverse, inclusive) in enumerate(SCAN_DIRS):
            dr_ref, dkk_ref, dw_ref, db_ref, dkt_ref, dv_ref = outs[d]
            after_ref, _, dsh_ref, _, dv_raw_ref, dremt_ref = scr[d]
            hist_ref, removed_ref = hist_refs[d], removed_refs[d]
            _blocks_to_rows(dv_raw_ref, 0, dv_ref, bsz)
            for t in range(SCAN_CHUNK):
                ts = slice(t, t + 1)
                after = t - 1 if reverse else t + 1
                for b in rng:
                    sp_m, ds = hist_ref[t, b], dsh_ref[t, b]
                    sp = sp_m.astype(f32)
                    if not inclusive:
                        seen = sp_m
                    else:
                        seen = hist_ref[after, b] if 0 <= after < SCAN_CHUNK else after_ref[b]
                    dr_ref[b, ts, :] = _row_from_heads(dy_refs[d][b, t], seen, pad_v)
                    dkt_ref[b, ts, :] = _row_from_heads(v_refs[d][b, t], ds, pad_v)
                    dw_ref[b, ts, :] = rsum(ds * sp)
                    db_ref[b, ts, :] = -_row_from_heads(_own_block(removed_ref[t], b), ds, pad_v)
                    dkk_ref[b, ts, :] = _row_from_heads(_own_block(dremt_ref[t], b), sp_m, pad_v)
            if inclusive:
                first = SCAN_CHUNK - 1 if reverse else 0
                for b in rng:
                    after_ref[b] = hist_ref[first, b].astype(f32)

    specs = [_scan_specs(bsz, backs[d]) for d in range(2)]
    hist = pltpu.VMEM((SCAN_CHUNK, bsz, RW_N, RW_W), f32)
    state = pltpu.VMEM((bsz, RW_N, RW_W), f32)
    final_spec = pl.BlockSpec((bsz, RW_N, RW_W), lambda i: (0, 0, 0))
    raw = pltpu.VMEM((SCAN_CHUNK, RW_HEADS * bsz, RW_N * bsz), f32)
    return pl.pallas_call(
        body, grid=(n_ch,),
        in_specs=[specs[d][0](col) for d in range(2) for _, col in rows_in[d]]
        + [specs[0][1], specs[1][1]] * 2 + [specs[0][2], specs[1][2], final_spec, final_spec, specs[0][3], specs[1][3]]
        + _const_specs(consts),
        out_specs=[specs[d][0]() for d in range(2) for _ in range(n_out)],
        out_shape=[jax.ShapeDtypeStruct((bsz, n_tok, RW_W), f32)] * (2 * n_out),
        scratch_shapes=[state, state, hist, state, raw, raw] * 2,
        compiler_params=_cparams(("arbitrary",)), name=name)(
        *[a for d in range(2) for a, _ in rows_in[d]], v_heads, v_heads, dy_heads, dy_heads, *hists, *finals, *removed, *consts)


MOD_NAMES = ("shift1", "scale1", "gate1", "shift2", "scale2", "gate2")


def _rope_tables(t_ctx, t_x):
    quarter = RET_DH // 4
    pos = jnp.arange(t_x)
    inv = jnp.power(ROPE_BASE, -jnp.arange(0, 2 * quarter, 2, dtype=f32) / (2 * quarter))
    ang_r = (pos // GRID_W).astype(f32)[:, None] * inv[None, :]
    ang_c = (pos % GRID_W).astype(f32)[:, None] * inv[None, :]
    cos = jnp.concatenate([jnp.cos(ang_r)] * 2 + [jnp.cos(ang_c)] * 2, axis=1)
    sin = jnp.concatenate([-jnp.sin(ang_r), jnp.sin(ang_r), -jnp.sin(ang_c), jnp.sin(ang_c)], axis=1)
    cos = jnp.concatenate([jnp.ones((t_ctx, RET_DH), f32), cos], axis=0)
    sin = jnp.concatenate([jnp.zeros((t_ctx, RET_DH), f32), sin], axis=0)
    return cos, sin


def _pad_rows(w, lo, total):
    return jnp.pad(w, ((lo, total - lo - w.shape[0]), (0, 0)))


LATE_WEIGHTS = ("w_out", "w_ff1", "w_ff2")


def layer_step(x, ctx, tgt, mod_x, mod_ctx, wt, late_weights=None, early_grads=None):
    bsz, t_x, _ = x.shape
    t_c = ctx.shape[1]
    t_all = t_c + t_x
    n_ct, n_xt = t_c // TOK_TILE, t_x // TOK_TILE
    n_t = n_ct + n_xt
    assert t_c % TOK_TILE == 0 and t_x % TOK_TILE == 0 and t_c % RET_CHUNK == 0

    seg = lambda i: (i >= n_ct).astype(jnp.int32)
    seg_first = lambda i: jnp.logical_or(i == 0, i == n_ct)
    seg_last = lambda i: jnp.logical_or(i == n_ct - 1, i == n_t - 1)
    mod_all = {n: jnp.stack([jnp.broadcast_to(mod_ctx[k], (bsz, D_MODEL)), mod_x[:, k]], axis=1)[:, :, None, :]
               for k, n in enumerate(MOD_NAMES)}
    mod_lat = {n: mod_x[:, k][:, None, None, :] for k, n in enumerate(MOD_NAMES)}
    both = lambda n: Seg(mod_all[n], seg, seg_first)
    lat = lambda n: Seg(mod_lat[n], lambda i: 0, lambda i: i == 0)
    flat = lambda a: a.reshape(-1, a.shape[-1])

    def chunk_orders(n_ctx_chunks, n_chunks):
        fwd = lambda i: i
        bwd = lambda i: jnp.where(i < n_ctx_chunks, n_ctx_chunks - 1 - i, n_chunks + n_ctx_chunks - 1 - i)
        return fwd, bwd

    ones64, ones128 = _block_ones(RW_W, RW_N), _block_ones(RET_W, RET_DH)
    cos, sin = _rope_tables(t_c, t_x)
    ld_rows = [jnp.pad(wt["ret_log_decay"][d][None, :], ((0, 0), (0, RET_DH - RET_HEADS))) for d in range(2)]
    w_up_pad = [_pad_rows(wt["rwkv_w_up"][d], 0, LORA_W) for d in range(2)]
    a_up_pad = [_pad_rows(wt["rwkv_a_up"][d], DECAY_LORA, LORA_W) for d in range(2)]
    g_up_pad = _pad_rows(wt["rwkv_g_up"], DECAY_LORA + AAA_LORA, LORA_W)
    row = lambda a, d: a[d][None, :]

    h = jnp.concatenate([ctx, x], axis=1)
    norm1_ins = lambda: [Tiled(h), both("shift1"), both("scale1"), Glob(wt["norm1_g"])]
    (n1,) = ew_forward(fn_norm_mod, "norm1", bsz, n_t, norm1_ins(), [(D_MODEL, MXU_DTYPE)])
    px = matmul(flat(n1), wt["w_in"], "nn", "proj_in").reshape(bsz, t_all, IN_COLS)
    px_rw = px[..., RET_COLS:]
    ps = token_shift(px_rw, wt["rwkv_shift_mu"], seg_first, seg_last)

    def prep_ins(toff=0):
        return [Tiled(ps, RW_W, 1), Tiled(ps, LORA_W, 3 * RW_W // LORA_W),
                Glob(row(wt["rwkv_w0"], 0)), Glob(row(wt["rwkv_w0"], 1)),
                Glob(row(wt["rwkv_a0"], 0)), Glob(row(wt["rwkv_a0"], 1)),
                Glob(w_up_pad[0]), Glob(w_up_pad[1]), Glob(a_up_pad[0]), Glob(a_up_pad[1]), Glob(g_up_pad),
                Glob(wt["rwkv_k_k"]), Glob(wt["rwkv_k_a"]), Glob(ones64)]

    kk, w_f, b_f, kt_f, w_b, b_b, kt_b, g_rw = ew_forward(fn_rwkv_prepare, "rwkv_prepare", bsz, n_t, prep_ins(),
                                                           [(RW_W, f32)] * 8)
    rw_order = chunk_orders(t_c // SCAN_CHUNK, t_all // SCAN_CHUNK)
    ret_order = chunk_orders(t_c // RET_CHUNK, t_all // RET_CHUNK)
    scan_rows = [[(ps, 0), (kk, 0), (w_f, 0), (b_f, 0), (kt_f, 0)], [(ps, 0), (kk, 0), (w_b, 0), (b_b, 0), (kt_b, 0)]]
    v_heads = heads_to_rows(ps[..., 2 * RW_W:3 * RW_W])
    y_f, y_b, *kept_states = rwkv_scan_fwd(scan_rows, v_heads, rw_order, "rwkv_scan_fwd")
    y = [y_f, y_b]
    o, ret_states = [], []
    for d in range(2):
        o_d, st_d = retention_fwd(px, cos, sin, ld_rows[d], ret_order[d], SCAN_DIRS[d][0], f"retention_fwd{d}")
        o.append(o_d), ret_states.append(st_d)

    def merge_ins(toff):
        return [Tiled(o[0], toff=toff), Tiled(o[1], toff=toff), Tiled(px, RET_W, 3, toff),
                Tiled(y[0], toff=toff), Tiled(y[1], toff=toff), Tiled(ps, RW_W, 0, toff), Tiled(kt_f, toff=toff),
                Tiled(ps, RW_W, 2, toff), Tiled(g_rw, toff=toff),
                Glob(wt["rwkv_r_k"]), Glob(wt["rwkv_ln_w"]), Glob(wt["rwkv_ln_b"]), Glob(ones64), Glob(ones128)]

    ret_out, rw_out = ew_forward(fn_merge, "merge_heads", bsz, n_xt, merge_ins(n_ct),
                                 [(RET_W, MXU_DTYPE), (RW_W, MXU_DTYPE)])
    merged = jnp.concatenate([ret_out, rw_out], axis=-1)
    if late_weights is not None:
        wt = {**wt, **late_weights(merged)}
    mix = matmul(flat(merged), wt["w_out"], "nn", "proj_out").reshape(bsz, t_x, D_MODEL)
    resid_ins = lambda: [Tiled(x), Tiled(mix), lat("gate1"), lat("shift2"), lat("scale2"), Glob(wt["norm2_g"])]
    h1, n2 = ew_forward(fn_resid_norm_mod, "resid_norm2", bsz, n_xt, resid_ins(), [(D_MODEL, f32), (D_MODEL, MXU_DTYPE)])
    act = matmul(flat(n2), wt["w_ff1"], "nn", "ff1", MXU_DTYPE, wt["b_ff1"], relu2).reshape(bsz, t_x, D_FF)
    ff = matmul(flat(act), wt["w_ff2"], "nn", "ff2").reshape(bsz, t_x, D_MODEL)

    g = {}
    loss, dh1, dff, dgate2, g["b_ff2"], g["final_g"] = loss_and_grads(
        h1, ff, tgt, mod_lat["gate2"], wt["b_ff2"], wt["final_g"], bsz, n_xt)
    dact = matmul(flat(dff), wt["w_ff2"], "nt", "ff2_dx", MXU_DTYPE).reshape(bsz, t_x, D_FF)
    g["w_ff2"] = matmul(flat(act), flat(dff), "tn", "ff2_dw")
    du, g["b_ff1"] = relu2_backward(act, dact, "relu2_bwd")
    dn2 = matmul(flat(du), wt["w_ff1"], "nt", "ff1_dx").reshape(bsz, t_x, D_MODEL)
    g["w_ff1"] = matmul(flat(n2), flat(du), "tn", "ff1_dw")
    dx_res, dmix, dgate1, dshift2, dscale2, g["norm2_g"] = ew_backward(
        fn_resid_norm_mod, "resid_norm2_bwd", bsz, n_xt, resid_ins(), [Tiled(dh1), Tiled(dn2)], [True] * 6,
        {1: MXU_DTYPE})
    dmerged = matmul(flat(dmix), wt["w_out"], "nt", "proj_out_dx").reshape(bsz, t_x, D_MODEL)
    g["w_out"] = matmul(flat(merged), flat(dmix), "tn", "proj_out_dw")
    if early_grads is not None:
        token = early_grads({n: g.pop(n) for n in LATE_WEIGHTS})
        wt = {**wt, "rwkv_r_k": wt["rwkv_r_k"] + token[:1, :1]}
    (do, dg_ret, dy, dr_m, dkt_m, dv_m, dg_rw, g["rwkv_r_k"], g["rwkv_ln_w"], g["rwkv_ln_b"]) = ew_backward(
        fn_merge, "merge_heads_bwd", bsz, n_xt, merge_ins(0),
        [Tiled(dmerged, RET_W, 0, -n_ct), Tiled(dmerged, RW_W, 1, -n_ct)],
        [True, False, True, True, False, True, True, True, True, True, True, True, False, False], lead=n_ct)

    dqkv, dld = [], []
    for d in range(2):
        *dqkv_d, dld_d = retention_bwd(do, px, ret_states[d], cos, sin, ld_rows[d], ret_order[d],
                                       SCAN_DIRS[d][0], f"retention_bwd{d}")
        dqkv.append(dqkv_d), dld.append(dld_d[0, :RET_HEADS])
    g["ret_log_decay"] = jnp.stack(dld)
    (dr_f, dkk_f, dw_f, db_f, dkt_f, dv_f, dr_b, dkk_b, dw_b, db_b, dkt_b, dv_b) = rwkv_scan_bwd(
        scan_rows, v_heads, heads_to_rows(dy), kept_states[:2], kept_states[2:4], kept_states[4:], rw_order, "rwkv_scan_bwd")
    prep_cts = [dkk_f + dkk_b, dw_f, db_f, dkt_f + dkt_m, dw_b, db_b, dkt_b, dg_rw]
    (dks, dlora, dw0_f, dw0_b, da0_f, da0_b, dwup_f, dwup_b, daup_f, daup_b, dgup, g["rwkv_k_k"],
     g["rwkv_k_a"]) = ew_backward(fn_rwkv_prepare, "rwkv_prepare_bwd", bsz, n_t, prep_ins(),
                                  [Tiled(c) for c in prep_cts], [True] * 13 + [False])
    g["rwkv_w0"] = jnp.concatenate([dw0_f, dw0_b], axis=0)
    g["rwkv_a0"] = jnp.concatenate([da0_f, da0_b], axis=0)
    g["rwkv_w_up"] = jnp.stack([dwup_f[:DECAY_LORA], dwup_b[:DECAY_LORA]])
    g["rwkv_a_up"] = jnp.stack([daup_f[DECAY_LORA:DECAY_LORA + AAA_LORA], daup_b[DECAY_LORA:DECAY_LORA + AAA_LORA]])
    g["rwkv_g_up"] = dgup[DECAY_LORA + AAA_LORA:]
    dps = jnp.concatenate([dr_f + dr_b + dr_m, dks, dv_f + dv_b + dv_m, dlora], axis=-1)
    dp_rw, g["rwkv_shift_mu"] = token_shift_bwd(dps, px_rw, wt["rwkv_shift_mu"], seg_first, seg_last)
    dpx = jnp.concatenate([(dqkv[0][k] + dqkv[1][k]).astype(MXU_DTYPE) for k in range(3)]
                          + [dg_ret.astype(MXU_DTYPE), dp_rw], axis=-1)
    dn1 = matmul(flat(dpx), wt["w_in"], "nt", "proj_in_dx").reshape(bsz, t_all, D_MODEL)
    g["w_in"] = matmul(flat(n1), flat(dpx), "tn", "proj_in_dw")
    dh, dshift1, dscale1, g["norm1_g"] = ew_backward(fn_norm_mod, "norm1_bwd", bsz, n_t, norm1_ins(), [Tiled(dn1)],
                                                     [True] * 4)
    grad_x = dh[:, t_c:] + dx_res
    zeros = jnp.zeros((D_MODEL,), f32)
    g["mod_x"] = jnp.stack([dshift1[:, 1, 0], dscale1[:, 1, 0], dgate1[:, 0, 0], dshift2[:, 0, 0], dscale2[:, 0, 0],
                            dgate2[:, 0, 0]], axis=1)
    g["mod_ctx"] = jnp.stack([dshift1[:, 0, 0].sum(0), dscale1[:, 0, 0].sum(0), zeros, zeros, zeros, zeros])
    return loss, grad_x, g


MESH_ID = pl.DeviceIdType.MESH
ALL_PEERS = [(dx, dy, dc) for dx in (0, 1) for dy in (0, 1) for dc in (0, 1)][1:]
CHIP_PEERS = [(1, 0, 0), (0, 1, 0), (1, 1, 0)]
CHIP_SLOTS = (0, 2, 4, 6)


def _mesh_pos():
    return lax.axis_index("x"), lax.axis_index("y"), lax.axis_index("c")


def _device_slot():
    x, y, c = _mesh_pos()
    return 4 * x + 2 * y + c


def sibling_swap(arrs, name):
    n = len(arrs)

    def body(*refs):
        in_refs, out_refs = refs[:n], refs[n:2 * n]
        send_sems, recv_sems = refs[2 * n:]
        x, y, c = _mesh_pos()
        copies = [pltpu.make_async_remote_copy(src_ref=in_refs[a], dst_ref=out_refs[a], send_sem=send_sems.at[a],
                                               recv_sem=recv_sems.at[a], device_id=(x, y, 1 - c),
                                               device_id_type=MESH_ID) for a in range(n)]
        for cp in copies:
            cp.start()
        for cp in copies:
            cp.wait()

    any_spec = pl.BlockSpec(memory_space=pl.ANY)
    res = pl.pallas_call(
        body, in_specs=[any_spec] * n, out_specs=[any_spec] * n,
        out_shape=[jax.ShapeDtypeStruct(a.shape, a.dtype) for a in arrs],
        scratch_shapes=[pltpu.SemaphoreType.DMA((n,)), pltpu.SemaphoreType.DMA((n,))],
        name=name)(*arrs)
    return list(res)


def exchange(arrs, gather, peers, name, by_chip=False, own=True):
    n, n_peers = len(arrs), len(peers)
    n_slots = N_SHARDS if by_chip else N_DEV
    slot = (lambda x, y, c: 2 * x + y) if by_chip else (lambda x, y, c: 4 * x + 2 * y + c)

    def body(*refs):
        in_refs, out_refs = refs[:n], refs[n:2 * n]
        send_sems, recv_sems, local_sems = refs[2 * n:]
        x, y, c = _mesh_pos()
        me = slot(x, y, c)
        copies, locals_ = [], []
        for a in range(n):
            if own:
                mine = in_refs[a] if gather else in_refs[a].at[me]
                loc = pltpu.make_async_copy(mine, out_refs[a].at[me], local_sems.at[a])
                loc.start()
                locals_.append(loc)
            for k, (dx, dy, dc) in enumerate(peers):
                peer = (1 - x if dx else x, 1 - y if dy else y, 1 - c if dc else c)
                src = in_refs[a] if gather else in_refs[a].at[slot(*peer)]
                sem = a * n_peers + k
                cp = pltpu.make_async_remote_copy(src_ref=src, dst_ref=out_refs[a].at[me], send_sem=send_sems.at[sem],
                                                  recv_sem=recv_sems.at[sem], device_id=peer, device_id_type=MESH_ID)
                cp.start()
                copies.append(cp)
        for cp in copies:
            cp.wait()
        for loc in locals_:
            loc.wait()

    any_spec = pl.BlockSpec(memory_space=pl.ANY)
    out_shape = [jax.ShapeDtypeStruct((n_slots,) + (a.shape if gather else a.shape[1:]), a.dtype) for a in arrs]
    n_sems = n * n_peers
    res = pl.pallas_call(
        body, in_specs=[any_spec] * n, out_specs=[any_spec] * n, out_shape=out_shape,
        scratch_shapes=[pltpu.SemaphoreType.DMA((n_sems,)), pltpu.SemaphoreType.DMA((n_sems,)),
                        pltpu.SemaphoreType.DMA((n,))],
        name=name)(*arrs)
    return list(res)


HBM_SPEC = pl.BlockSpec(memory_space=pltpu.HBM)
SEM_SPEC = pl.BlockSpec(memory_space=pltpu.SEMAPHORE)
DATAFLOW = pltpu.SideEffectType.DATAFLOW_SIDE_EFFECTING


def _peer_copies(src_refs, land_refs, send_sems, recv_sems, gather):
    x, y, c = _mesh_pos()
    me = 4 * x + 2 * y + c
    copies = []
    for a, (src_ref, land_ref) in enumerate(zip(src_refs, land_refs)):
        for k, (dx, dy, dc) in enumerate(ALL_PEERS):
            peer = (1 - x if dx else x, 1 - y if dy else y, 1 - c if dc else c)
            src = src_ref if gather else src_ref.at[4 * peer[0] + 2 * peer[1] + peer[2]]
            sem = a * len(ALL_PEERS) + k
            copies.append(pltpu.make_async_remote_copy(src_ref=src, dst_ref=land_ref.at[me], send_sem=send_sems.at[sem],
                                                       recv_sem=recv_sems.at[sem], device_id=peer,
                                                       device_id_type=MESH_ID))
    return copies


def exchange_start(arrs, gather, name):
    n = len(arrs)
    lands = [lax.empty((N_DEV,) + (a.shape if gather else a.shape[1:]), a.dtype) for a in arrs]

    def body(*refs):
        for cp in _peer_copies(refs[:n], refs[n:2 * n], refs[2 * n], refs[2 * n + 1], gather):
            cp.start()
        refs[-1][...] = jnp.zeros_like(refs[-1])

    sems = pltpu.SemaphoreType.DMA((n * len(ALL_PEERS),))
    hbm = [pltpu.HBM(a.shape, a.dtype) for a in arrs + lands]
    res = pl.pallas_call(
        body, name=name, out_shape=(sems, sems, *hbm, jax.ShapeDtypeStruct((8, 128), f32)),
        in_specs=[HBM_SPEC] * (2 * n),
        out_specs=(SEM_SPEC, SEM_SPEC, *[HBM_SPEC] * (2 * n), pl.BlockSpec(memory_space=pltpu.VMEM)),
        input_output_aliases={i: 2 + i for i in range(2 * n)},
        compiler_params=pltpu.CompilerParams(has_side_effects=DATAFLOW))(
        *[pltpu.with_memory_space_constraint(a, pltpu.HBM) for a in arrs + lands])
    return res[0], res[1], list(res[2:2 + n]), list(res[2 + n:2 + 2 * n]), res[-1]


def exchange_wait(started, after, gather, name):
    send_sems, recv_sems, srcs, lands, _ = started
    n = len(srcs)

    def body(*refs):
        for cp in _peer_copies(refs[:n], refs[n:2 * n], refs[2 * n], refs[2 * n + 1], gather):
            cp.wait_send()
            cp.wait_recv()

    res = pl.pallas_call(
        body, name=name, out_shape=tuple(pltpu.HBM(a.shape, a.dtype) for a in srcs + lands),
        in_specs=[HBM_SPEC] * (2 * n) + [SEM_SPEC, SEM_SPEC, pl.BlockSpec(memory_space=pl.ANY)],
        out_specs=tuple([HBM_SPEC] * (2 * n)), input_output_aliases={i: i for i in range(2 * n)},
        compiler_params=pltpu.CompilerParams(has_side_effects=DATAFLOW))(*srcs, *lands, send_sems, recv_sems, after)
    return list(res[n:])


def add_arrays(parts, name, out_dtype=f32):
    r, c = parts[0].shape
    tr = r
    for cand in (512, 256, 128, 64, 32, 16):
        if r % cand == 0:
            tr = cand
            break

    def body(*refs):
        acc = refs[0][...].astype(f32)
        for p_ref in refs[1:-1]:
            acc = acc + p_ref[...].astype(f32)
        refs[-1][...] = acc.astype(out_dtype)

    spec = pl.BlockSpec((tr, c), lambda i: (i, 0))
    return pl.pallas_call(body, grid=(r // tr,), in_specs=[spec] * len(parts), out_specs=spec,
                          out_shape=jax.ShapeDtypeStruct((r, c), out_dtype),
                          compiler_params=_cparams(("parallel",)), name=name)(*parts)


def gather_two_level(arrs, name):
    n = len(arrs)
    per = 7

    def body(*refs):
        in_refs, out_refs = refs[:n], refs[n:2 * n]
        send_sems, recv_sems = refs[2 * n:]
        x, y, c = _mesh_pos()
        me, sibling = (x, y, c), (x, y, 1 - c)
        chips = [(1 - x, y), (x, 1 - y), (1 - x, 1 - y)]

        def copy(a, k, block, to, src=None):
            rows = out_refs[a].at[4 * block[0] + 2 * block[1] + block[2]]
            return pltpu.make_async_remote_copy(src_ref=rows if src is None else src, dst_ref=rows,
                                                send_sem=send_sems.at[a * per + k], recv_sem=recv_sems.at[a * per + k],
                                                device_id=to, device_id_type=MESH_ID)

        first, passed = [], []
        for a in range(n):
            first.append(copy(a, 0, me, sibling, src=in_refs[a]))
            first += [copy(a, 1 + j, me, (*chip, c), src=in_refs[a]) for j, chip in enumerate(chips)]
        for cp in first:
            cp.start()
        for a in range(n):
            for j, chip in enumerate(chips):
                copy(a, 1 + j, (*chip, c), me).wait_recv()
                fwd = copy(a, 4 + j, (*chip, c), sibling)
                fwd.start()
                passed.append(fwd)
        for a in range(n):
            copy(a, 0, sibling, me).wait_recv()
            for j, chip in enumerate(chips):
                copy(a, 4 + j, (*chip, 1 - c), me).wait_recv()
        for cp in first + passed:
            cp.wait_send()

    any_spec = pl.BlockSpec(memory_space=pl.ANY)
    res = pl.pallas_call(
        body, in_specs=[any_spec] * n, out_specs=[any_spec] * n,
        out_shape=[jax.ShapeDtypeStruct((N_DEV,) + a.shape, a.dtype) for a in arrs],
        scratch_shapes=[pltpu.SemaphoreType.DMA((n * per,)), pltpu.SemaphoreType.DMA((n * per,))],
        name=name)(*arrs)
    return list(res)


def sum_slots(parts, slots, name):
    _, r, c = parts.shape
    tr = r
    for cand in (512, 256, 128, 64, 32, 16, 8):
        if r % cand == 0 and cand * c * 4 * len(slots) <= 8 * 1024 * 1024:
            tr = cand
            break

    def body(p_ref, o_ref):
        acc = p_ref[slots[0]].astype(f32)
        for s in slots[1:]:
            acc = acc + p_ref[s].astype(f32)
        o_ref[...] = acc

    return pl.pallas_call(body, grid=(r // tr,), in_specs=[pl.BlockSpec((parts.shape[0], tr, c), lambda i: (0, i, 0))],
                          out_specs=pl.BlockSpec((tr, c), lambda i: (i, 0)),
                          out_shape=jax.ShapeDtypeStruct((r, c), f32),
                          compiler_params=_cparams(("parallel",)), name=name)(parts)


def column_sum(a, name):
    def body(a_ref, o_ref):
        o_ref[...] = jnp.sum(a_ref[...], axis=0, keepdims=True)

    return pl.pallas_call(body, out_shape=jax.ShapeDtypeStruct((1, a.shape[1]), f32), name=name)(a)


def adamw(w, g, m, v, name):
    r, c = w.shape
    tr = r
    for cand in (256, 128, 64, 32, 16, 8):
        if r % cand == 0:
            tr = cand
            break

    def body(w_ref, g_ref, m_ref, v_ref, d_ref, mo_ref, vo_ref):
        gv = g_ref[...]
        m_new = ADAM_B1 * m_ref[...] + (1.0 - ADAM_B1) * gv
        v_new = ADAM_B2 * v_ref[...] + (1.0 - ADAM_B2) * jnp.square(gv)
        m_hat = m_new / (1.0 - ADAM_B1 ** ADAM_STEP)
        v_hat = v_new / (1.0 - ADAM_B2 ** ADAM_STEP)
        d_ref[...] = -ADAM_LR * (m_hat / (jnp.sqrt(v_hat) + ADAM_EPS) + ADAM_WD * w_ref[...])
        mo_ref[...] = m_new
        vo_ref[...] = v_new

    spec = pl.BlockSpec((tr, c), lambda i: (i, 0))
    return pl.pallas_call(body, grid=(r // tr,), in_specs=[spec] * 4, out_specs=[spec] * 3,
                          out_shape=[jax.ShapeDtypeStruct((r, c), f32)] * 3,
                          compiler_params=_cparams(("parallel",)), name=name)(w, g, m, v)


def adaln_fwd(c_rows, w, b):
    def body(c_ref, w_ref, b_ref, o_ref):
        cv = c_ref[...]
        o_ref[...] = _mxu_dot(cv * jax.nn.sigmoid(cv), w_ref[...]) + b_ref[...]

    return pl.pallas_call(body, out_shape=jax.ShapeDtypeStruct((c_rows.shape[0], w.shape[1]), f32),
                          compiler_params=pltpu.CompilerParams(vmem_limit_bytes=VMEM_LIMIT), name="adaln_fwd")(c_rows, w, b)


def adaln_bwd(c_rows, dm, w):
    def body(c_ref, dm_ref, w_ref, gw_ref, ds_ref):
        cv = c_ref[...]
        gw_ref[...] = _dg(cv * jax.nn.sigmoid(cv), dm_ref[...], 0, 0)
        ds_ref[...] = _dg(dm_ref[...], w_ref[...], 1, 1)

    return pl.pallas_call(body, out_shape=[jax.ShapeDtypeStruct(w.shape, f32),
                                           jax.ShapeDtypeStruct(c_rows.shape, f32)],
                          compiler_params=pltpu.CompilerParams(vmem_limit_bytes=VMEM_LIMIT), name="adaln_bwd")(c_rows, dm, w)


def c_ctx_grad(parts, c_ctx_row):
    def body(p_ref, c_ref, o_ref):
        total = p_ref[0, 0:1, :]
        for s in range(1, N_SHARDS):
            total = total + p_ref[s, 0:1, :]
        _, vjp = jax.vjp(jax.nn.silu, c_ref[...])
        o_ref[...] = vjp(total)[0]

    return pl.pallas_call(body, out_shape=jax.ShapeDtypeStruct((1, D_MODEL), f32), name="c_ctx_grad")(parts, c_ctx_row)


PACK_W = 1024
PACK_ROWS = 8


def _pack(arrs):
    pieces, layout, r0 = [], [], 0
    for a in arrs:
        size = math.prod(a.shape)
        rows = -(-size // (PACK_W * PACK_ROWS)) * PACK_ROWS
        pieces.append(jnp.pad(a.reshape(-1).astype(f32), (0, rows * PACK_W - size)).reshape(rows, PACK_W))
        layout.append((r0, rows, a.shape))
        r0 += rows
    return jnp.concatenate(pieces, axis=0), layout


def _unpack(pack, layout, lead=()):
    n_lead = len(lead)
    outs = []
    for r0, rows, shape in layout:
        piece = pack[(slice(None),) * n_lead + (slice(r0, r0 + rows),)].reshape(lead + (-1,))
        outs.append(piece[..., :math.prod(shape)].reshape(lead + tuple(shape)))
    return outs


W_NAMES = ("c_ctx", "w_ada", "b_ada", "norm1_g", "norm2_g", "w_in", "ret_log_decay", "rwkv_shift_mu", "rwkv_w0",
           "rwkv_w_up", "rwkv_a0", "rwkv_a_up", "rwkv_g_up", "rwkv_k_k", "rwkv_k_a", "rwkv_r_k", "rwkv_ln_w",
           "rwkv_ln_b", "w_out", "w_ff1", "b_ff1", "w_ff2", "b_ff2", "final_g")
COL_SHARDED = ("w_in", "w_ff1")
ROW_SHARDED = ("w_out", "w_ff2")
LAST_SHARDED = ("rwkv_shift_mu", "rwkv_w0", "rwkv_w_up", "rwkv_a0", "rwkv_a_up", "rwkv_g_up")
REPLICATED = ("c_ctx", "b_ada", "norm1_g", "norm2_g", "ret_log_decay", "rwkv_k_k", "rwkv_k_a", "rwkv_r_k",
              "rwkv_ln_w", "rwkv_ln_b", "b_ff1", "b_ff2", "final_g")
N_SHARDS = 4


def _train_step(a):
    x, c, ctx, tgt = a["x"], a["c"], a["ctx"], a["loss_target"]
    bsz = x.shape[0]
    mx, my, mc = _mesh_pos()
    shard = 2 * mx + my
    dev = _device_slot()

    (c_all,) = exchange([jnp.pad(c, ((0, PACK_ROWS - bsz), (0, 0)))], True, ALL_PEERS, "gather_c")
    n_ex = N_DEV * bsz
    c_rows = jnp.concatenate([c_all[:, :bsz].reshape(n_ex, D_MODEL), a["c_ctx"][None, :],
                              jnp.zeros((PACK_ROWS - 1, D_MODEL), f32)], axis=0)
    ada_cols = a["w_ada"].shape[-1]
    b_ada_cols = lax.dynamic_slice_in_dim(a["b_ada"], shard * ada_cols, ada_cols, axis=1)
    mod_cols = adaln_fwd(c_rows, a["w_ada"][0], b_ada_cols)

    def own_half(n):
        w = a[n][0].astype(MXU_DTYPE)
        return lax.dynamic_slice_in_dim(w, mc * (w.shape[0] // 2), w.shape[0] // 2, axis=0)

    def whole_weight(n, gth, own):
        per_chip = lax.dynamic_update_index_in_dim(gth, own, dev, 0).reshape(N_SHARDS, -1, gth.shape[-1])
        return (per_chip.transpose(1, 0, 2).reshape(per_chip.shape[1], -1) if n in COL_SHARDED
                else per_chip.reshape(-1, per_chip.shape[-1]))

    small_pack, small_layout = _pack([a[n][0] for n in LAST_SHARDED])
    own_blocks = [mod_cols, own_half("w_in"), small_pack]
    gathered = gather_two_level(own_blocks, "gather_weights")
    late_own = [own_half(n) for n in LATE_WEIGHTS]
    late_started = exchange_start(late_own, True, "gather_late_start")
    mod_own = lax.dynamic_update_index_in_dim(gathered[0], mod_cols, dev, 0)
    mod_all = jnp.stack([mod_own[s] for s in CHIP_SLOTS], axis=1).reshape(c_rows.shape[0], -1)
    mod_all = mod_all + late_started[-1][0, 0]
    mod_x = lax.dynamic_slice_in_dim(mod_all, dev * bsz, bsz, axis=0).reshape(bsz, 6, D_MODEL)
    mod_ctx = mod_all[n_ex].reshape(6, D_MODEL)
    wt = {"w_in": whole_weight("w_in", gathered[1], own_blocks[1])}

    def late_weights(after):
        lands = exchange_wait(late_started, after, True, "gather_late_wait")
        return {n: whole_weight(n, land, own) for n, land, own in zip(LATE_WEIGHTS, lands, late_own)}

    def grad_blocks(n, gw):
        if n in COL_SHARDED:
            gw = gw.reshape(gw.shape[0], N_SHARDS, -1).transpose(1, 0, 2)
        return gw.reshape(N_DEV, -1, gw.shape[-1]).astype(MXU_DTYPE)

    late_sent = {}

    def early_grads(late_g):
        late_sent["blocks"] = [grad_blocks(n, late_g[n]) for n in LATE_WEIGHTS]
        late_sent["started"] = exchange_start(late_sent["blocks"], False, "scatter_late_start")
        return late_sent["started"][-1]

    small_own = lax.dynamic_update_index_in_dim(gathered[2], small_pack, dev, 0)
    small_by_chip = _unpack(jnp.stack([small_own[s] for s in CHIP_SLOTS]), small_layout, (N_SHARDS,))
    for n, parts in zip(LAST_SHARDED, small_by_chip):
        wt[n] = jnp.concatenate([parts[s] for s in range(N_SHARDS)], axis=-1)
    for n in ("norm1_g", "norm2_g", "rwkv_k_k", "rwkv_k_a", "rwkv_r_k", "rwkv_ln_w", "rwkv_ln_b", "b_ff1", "b_ff2"):
        wt[n] = a[n]
    wt["ret_log_decay"] = a["ret_log_decay"][0]
    wt["final_g"] = a["final_g"][None, :]

    loss, grad_x, g = layer_step(x, ctx, tgt, mod_x, mod_ctx, wt, late_weights, early_grads)

    small_names = [n for n in REPLICATED if n not in ("c_ctx", "b_ada")]
    g_pack, g_layout = _pack([jnp.pad(loss, ((0, 0), (0, PACK_W - loss.shape[1])))] + [g[n] for n in small_names]
                             + [g["mod_x"], g["mod_ctx"]])
    (g_packs,) = gather_two_level([g_pack], "gather_small_grads")
    g_packs = lax.dynamic_update_index_in_dim(g_packs, g_pack, dev, 0)
    g_sum = _unpack(sum_slots(g_packs, tuple(range(N_DEV)), "sum_small_grads"), g_layout)
    loss_total = g_sum[0][0, 0]
    grads = dict(zip(small_names, g_sum[1:1 + len(small_names)]))
    dmod_ctx = g_sum[-1].reshape(1, -1)
    dmod_x = _unpack(g_packs, g_layout, (N_DEV,))[-2].reshape(n_ex, -1)
    dmod = jnp.concatenate([dmod_x, dmod_ctx, jnp.zeros((PACK_ROWS - 1, dmod_x.shape[1]), f32)], axis=0)
    grads["b_ada"] = column_sum(dmod, "b_ada_grad")
    dmod_cols = lax.dynamic_slice_in_dim(dmod, shard * ada_cols, ada_cols, axis=1)
    grads["w_ada"], dsilu = adaln_bwd(c_rows, dmod_cols, a["w_ada"][0])

    blocks = [grad_blocks("w_in", g["w_in"])]
    shard_packs = []
    for s in range(N_SHARDS):
        pieces_s = [lax.slice_in_dim(g[n], s * a[n].shape[-1], (s + 1) * a[n].shape[-1], axis=g[n].ndim - 1)
                    for n in LAST_SHARDED]
        pack_s, shard_layout = _pack(pieces_s)
        shard_packs.append(jnp.pad(pack_s, ((0, -pack_s.shape[0] % (2 * PACK_ROWS)), (0, 0))))
    blocks.append(jnp.stack(shard_packs).reshape(N_DEV, -1, PACK_W))
    scattered = ("w_in", "small_shards")
    halves_of = lambda blk, core: lax.dynamic_index_in_dim(
        blk.reshape(N_SHARDS, 2, *blk.shape[1:]), core, axis=1, keepdims=False).reshape(-1, blk.shape[-1])
    from_sibling = sibling_swap([halves_of(blk, 1 - mc) for blk in blocks], "prereduce_swap")
    chip_sums = [add_arrays([halves_of(blk, mc), got], f"prereduce_{n}", blk.dtype).reshape(N_SHARDS, -1, blk.shape[-1])
                 for n, blk, got in zip(scattered, blocks, from_sibling)]
    dsilu_rows = jnp.broadcast_to(jnp.pad(dsilu[n_ex:n_ex + 1], ((0, PACK_ROWS - 1), (0, 0)))[None],
                                  (N_SHARDS, PACK_ROWS, D_MODEL))
    to_chips = [dsilu_rows] + chip_sums
    received = exchange(to_chips, False, CHIP_PEERS, "scatter_big_grads", by_chip=True, own=False)
    received = [lax.dynamic_update_index_in_dim(got, lax.dynamic_index_in_dim(sent, shard, 0, keepdims=False), shard, 0)
                for got, sent in zip(received, to_chips)]
    grads["c_ctx"] = c_ctx_grad(received[0], a["c_ctx"][None, :])
    half_sums = [sum_slots(p, tuple(range(N_SHARDS)), f"sum_{n}") for n, p in zip(scattered, received[1:])]
    late_lands = exchange_wait(late_sent["started"], half_sums[0], False, "scatter_late_wait")
    for n, land, sent in zip(LATE_WEIGHTS, late_lands, late_sent["blocks"]):
        land = lax.dynamic_update_index_in_dim(land, lax.dynamic_index_in_dim(sent, dev, 0, keepdims=False), dev, 0)
        half_sums.append(sum_slots(land, tuple(range(N_DEV)), f"sum_{n}"))
    scattered = scattered + LATE_WEIGHTS
    other_halves = sibling_swap(half_sums, "swap_halves")
    for n, mine, other in zip(scattered, half_sums, other_halves):
        rows = mine.shape[0]
        whole = jnp.zeros((2 * rows, mine.shape[1]), f32)
        whole = lax.dynamic_update_slice_in_dim(whole, mine, mc * rows, axis=0)
        grads[n] = lax.dynamic_update_slice_in_dim(whole, other, (1 - mc) * rows, axis=0)
    grads.update(zip(LAST_SHARDED, _unpack(grads.pop("small_shards"), shard_layout)))

    out_g, out_d, out_m, out_v = {}, {}, {}, {}
    for n in ("w_ada",) + COL_SHARDED + ROW_SHARDED:
        out_g[n] = grads[n].reshape(a[n].shape)
        two_d = lambda z: z.reshape(-1, z.shape[-1])
        d, m, v = adamw(two_d(a[n]), two_d(out_g[n]), two_d(a["m_" + n]), two_d(a["v_" + n]), f"adamw_{n}")
        out_d[n], out_m[n], out_v[n] = d.reshape(a[n].shape), m.reshape(a[n].shape), v.reshape(a[n].shape)
    rest = REPLICATED + LAST_SHARDED
    for n in rest:
        out_g[n] = grads[n].reshape(a[n].shape)
    packs = [_pack([src[n] for n in rest])[0] for src in
             ({n: a[n] for n in rest}, out_g, {n: a["m_" + n] for n in rest}, {n: a["v_" + n] for n in rest})]
    _, rest_layout = _pack([a[n] for n in rest])
    for dst, pack in zip((out_d, out_m, out_v), adamw(*packs, "adamw_small")):
        dst.update(zip(rest, _unpack(pack, rest_layout)))
    return (loss_total, grad_x, *[out_g[n] for n in W_NAMES], *[out_d[n] for n in W_NAMES],
            *[out_m[n] for n in W_NAMES], *[out_v[n] for n in W_NAMES])


def kernel(x, c, ctx, c_ctx, w_ada, b_ada, norm1_g, norm2_g, w_in, ret_log_decay, rwkv_shift_mu, rwkv_w0, rwkv_w_up, rwkv_a0, rwkv_a_up, rwkv_g_up, rwkv_k_k, rwkv_k_a, rwkv_r_k, rwkv_ln_w, rwkv_ln_b, w_out, w_ff1, b_ff1, w_ff2, b_ff2, final_g, loss_target, m_c_ctx, m_w_ada, m_b_ada, m_norm1_g, m_norm2_g, m_w_in, m_ret_log_decay, m_rwkv_shift_mu, m_rwkv_w0, m_rwkv_w_up, m_rwkv_a0, m_rwkv_a_up, m_rwkv_g_up, m_rwkv_k_k, m_rwkv_k_a, m_rwkv_r_k, m_rwkv_ln_w, m_rwkv_ln_b, m_w_out, m_w_ff1, m_b_ff1, m_w_ff2, m_b_ff2, m_final_g, v_c_ctx, v_w_ada, v_b_ada, v_norm1_g, v_norm2_g, v_w_in, v_ret_log_decay, v_rwkv_shift_mu, v_rwkv_w0, v_rwkv_w_up, v_rwkv_a0, v_rwkv_a_up, v_rwkv_g_up, v_rwkv_k_k, v_rwkv_k_a, v_rwkv_r_k, v_rwkv_ln_w, v_rwkv_ln_b, v_w_out, v_w_ff1, v_b_ff1, v_w_ff2, v_b_ff2, v_final_g):
    return _train_step(dict(locals()))
```

```python
import functools
import math

import jax
import jax.numpy as jnp
from jax import lax
from jax.experimental import pallas as pl
from jax.experimental.pallas import tpu as pltpu

f32 = jnp.float32
MXU_DTYPE = jnp.bfloat16

D_MODEL = 1024
RET_W = 512
RET_HEADS = 4
RET_DH = 128
RET_CHUNK = 128
RW_W = 512
RW_N = 64
DECAY_LORA = 64
AAA_LORA = 64
GATE_LORA = 128
LORA_W = DECAY_LORA + AAA_LORA + GATE_LORA
D_FF = 4096
RET_COLS = 4 * RET_W
SHIFT_COLS = 3 * RW_W + LORA_W
IN_COLS = RET_COLS + SHIFT_COLS
GRID_W = 64
ROPE_BASE = 10000.0
NORM_EPS = 1e-6
GN_EPS = 64e-5
W_DECAY_SCALE = math.exp(-0.5)
ADAM_LR, ADAM_B1, ADAM_B2, ADAM_EPS, ADAM_WD, ADAM_STEP = 0.001, 0.9, 0.999, 1e-08, 0.01, 10

TOK_TILE = 256
MATMUL_TILE = 1024
SCAN_CHUNK = 32
SCAN_UNROLL = SCAN_CHUNK
N_DEV = 8
V7X_VMEM_BYTES = 64 * 1024 * 1024
VMEM_LIMIT = V7X_VMEM_BYTES * 7 // 8


def _cparams(sem):
    return pltpu.CompilerParams(dimension_semantics=sem, vmem_limit_bytes=VMEM_LIMIT)


def _tile(n, cap):
    best = None
    for t in range(128, min(n, cap) + 1, 128):
        if n % t == 0:
            best = t
    return best if best is not None else n


def matmul(a, b, mode, name, out_dtype=f32, bias=None, finish=None):
    if mode == "nn":
        (m, k), (k2, n) = a.shape, b.shape
    elif mode == "nt":
        (m, k), (n, k2) = a.shape, b.shape
    else:
        (k, m), (k2, n) = a.shape, b.shape
    assert k == k2, (a.shape, b.shape, mode)
    tm, tn, tk = _tile(m, MATMUL_TILE), _tile(n, MATMUL_TILE), _tile(k, MATMUL_TILE)
    nk = k // tk
    dims = {"nn": ((1,), (0,)), "nt": ((1,), (1,)), "tn": ((0,), (0,))}[mode]

    def body(a_ref, b_ref, *rest):
        o_ref, acc_ref = rest[-2:]
        kk = pl.program_id(2)

        @pl.when(kk == 0)
        def _():
            acc_ref[...] = jnp.zeros_like(acc_ref)

        acc_ref[...] += lax.dot_general(a_ref[...].astype(MXU_DTYPE), b_ref[...].astype(MXU_DTYPE),
                                        (dims, ((), ())), preferred_element_type=f32)

        @pl.when(kk == nk - 1)
        def _():
            res = acc_ref[...]
            if bias is not None:
                res = res + rest[0][...]
            if finish is not None:
                res = finish(res)
            o_ref[...] = res.astype(o_ref.dtype)

    if mode == "nn":
        a_spec = pl.BlockSpec((tm, tk), lambda i, j, q: (i, q))
        b_spec = pl.BlockSpec((tk, tn), lambda i, j, q: (q, j))
    elif mode == "nt":
        a_spec = pl.BlockSpec((tm, tk), lambda i, j, q: (i, q))
        b_spec = pl.BlockSpec((tn, tk), lambda i, j, q: (j, q))
    else:
        a_spec = pl.BlockSpec((tk, tm), lambda i, j, q: (q, i))
        b_spec = pl.BlockSpec((tk, tn), lambda i, j, q: (q, j))
    extra_specs = [] if bias is None else [pl.BlockSpec((1, tn), lambda i, j, q: (0, j))]
    extra = [] if bias is None else [bias]
    return pl.pallas_call(
        body, grid=(m // tm, n // tn, nk), in_specs=[a_spec, b_spec] + extra_specs,
        out_specs=pl.BlockSpec((tm, tn), lambda i, j, q: (i, j)),
        out_shape=jax.ShapeDtypeStruct((m, n), out_dtype),
        scratch_shapes=[pltpu.VMEM((tm, tn), f32)],
        compiler_params=_cparams(("parallel", "parallel", "arbitrary")), name=name)(a, b, *extra)


class Tiled:
    def __init__(self, arr, w=None, cidx=0, toff=0):
        self.arr, self.w, self.cidx, self.toff = arr, (arr.shape[-1] if w is None else w), cidx, toff

    def spec(self):
        cidx, toff = self.cidx, self.toff
        return pl.BlockSpec((None, TOK_TILE, self.w), lambda b, i: (b, jnp.maximum(i + toff, 0), cidx))


class Seg:
    def __init__(self, arr, seg, first):
        self.arr, self.seg, self.first = arr, seg, first

    def spec(self):
        seg = self.seg
        return pl.BlockSpec((None, None, 1, self.arr.shape[-1]), lambda b, i: (b, seg(i), 0, 0))


class Glob:
    def __init__(self, arr):
        self.arr = arr

    def spec(self):
        return pl.BlockSpec(self.arr.shape, lambda b, i: (0,) * self.arr.ndim)


def ew_forward(fn, name, bsz, n_tiles, ins, outs):
    n_in = len(ins)

    def body(*refs):
        res = fn(*[r[...] for r in refs[:n_in]])
        for o_ref, o in zip(refs[n_in:], res):
            o_ref[...] = o.astype(o_ref.dtype)

    out_specs = [pl.BlockSpec((None, TOK_TILE, w), lambda b, i: (b, i, 0)) for w, _ in outs]
    out_shape = [jax.ShapeDtypeStruct((bsz, n_tiles * TOK_TILE, w), dt) for w, dt in outs]
    return pl.pallas_call(body, grid=(bsz, n_tiles), in_specs=[d.spec() for d in ins], out_specs=out_specs,
                          out_shape=out_shape, compiler_params=_cparams(("parallel", "parallel")), name=name)(
        *[d.arr for d in ins])


def ew_backward(fn, name, bsz, n_tiles, ins, cts, want, grad_dtypes=None, lead=0):
    n_in, n_ct = len(ins), len(cts)
    diff = [k for k in range(n_in) if want[k]]
    grad_dtypes = grad_dtypes or {}
    assert lead == 0 or not any(isinstance(ins[k], Seg) for k in diff)

    def body(*refs):
        b, i = pl.program_id(0), pl.program_id(1)
        g_refs = refs[n_in + n_ct:]

        def tile_grads():
            vals = [r[...] for r in refs[:n_in]]
            ct_vals = tuple(r[...].astype(f32) for r in refs[n_in:n_in + n_ct])

            def f(*dvals):
                full = list(vals)
                for k, v in zip(diff, dvals):
                    full[k] = v
                return tuple(fn(*full))

            _, vjp = jax.vjp(f, *[vals[k] for k in diff])
            grads = vjp(ct_vals)
            for k, g_ref, g in zip(diff, g_refs, grads):
                d = ins[k]
                if isinstance(d, Tiled):
                    g_ref[...] = g.astype(g_ref.dtype)
                else:
                    zero = d.first(i) if isinstance(d, Seg) else jnp.logical_and(b == 0, i == lead)

                    @pl.when(zero)
                    def _(g_ref=g_ref):
                        g_ref[...] = jnp.zeros_like(g_ref)

                    g_ref[...] += g

        if lead == 0:
            tile_grads()
        else:
            pl.when(i >= lead)(tile_grads)

            @pl.when(i < lead)
            def _():
                for k, g_ref in zip(diff, g_refs):
                    if isinstance(ins[k], Tiled):
                        g_ref[...] = jnp.zeros_like(g_ref)

    out_specs, out_shape = [], []
    for k in diff:
        d = ins[k]
        if isinstance(d, Tiled):
            out_specs.append(pl.BlockSpec((None, TOK_TILE, d.w), lambda b, i: (b, i, 0)))
            out_shape.append(jax.ShapeDtypeStruct((bsz, (n_tiles + lead) * TOK_TILE, d.w), grad_dtypes.get(k, f32)))
        else:
            out_specs.append(d.spec())
            out_shape.append(jax.ShapeDtypeStruct(d.arr.shape, f32))
    return pl.pallas_call(body, grid=(bsz, n_tiles + lead),
                          in_specs=[d.spec() for d in ins] + [c.spec() for c in cts],
                          out_specs=out_specs, out_shape=out_shape,
                          compiler_params=_cparams(("arbitrary", "arbitrary")), name=name)(
        *[d.arr for d in ins], *[c.arr for c in cts])


@jax.custom_vjp
def _mxu_dot(a, b):
    return jnp.dot(a.astype(MXU_DTYPE), b.astype(MXU_DTYPE), preferred_element_type=f32)


def _mxu_dot_fwd(a, b):
    return _mxu_dot(a, b), (a, b)


def _mxu_dot_bwd(res, ct):
    a, b = res
    ct = ct.astype(MXU_DTYPE)
    da = lax.dot_general(ct, b.astype(MXU_DTYPE), (((1,), (1,)), ((), ())), preferred_element_type=f32)
    db = lax.dot_general(a.astype(MXU_DTYPE), ct, (((0,), (0,)), ((), ())), preferred_element_type=f32)
    return da, db


_mxu_dot.defvjp(_mxu_dot_fwd, _mxu_dot_bwd)


def _split_dot_impl(x, ones_mat):
    hi = x.astype(MXU_DTYPE)
    lo = (x - hi.astype(f32)).astype(MXU_DTYPE)
    return jnp.dot(hi, ones_mat, preferred_element_type=f32) + jnp.dot(lo, ones_mat, preferred_element_type=f32)


@jax.custom_vjp
def _split_dot(x, ones_mat):
    return _split_dot_impl(x, ones_mat)


def _split_dot_fwd(x, ones_mat):
    return _split_dot_impl(x, ones_mat), ones_mat


def _split_dot_bwd(ones_mat, ct):
    return _split_dot_impl(ct, ones_mat), None


_split_dot.defvjp(_split_dot_fwd, _split_dot_bwd)


def _block_ones(n, group):
    idx = jnp.arange(n) // group
    return (idx[:, None] == idx[None, :]).astype(MXU_DTYPE)


def _rms(x, g):
    return x * lax.rsqrt(jnp.mean(x * x, axis=-1, keepdims=True) + NORM_EPS) * g


def fn_norm_mod(h, shift, scale, g):
    return (_rms(h, g) * (1.0 + scale) + shift,)


def fn_rwkv_prepare(ks, lora, w0_f, w0_b, a0_f, a0_b, w_up_f, w_up_b, a_up_f, a_up_b, g_up, k_k, k_a, ones64):
    kkr = ks * k_k
    kk = kkr * lax.rsqrt(_split_dot(kkr * kkr, ones64) + 1e-12)
    outs = [kk]
    th = jnp.tanh(lora)
    for w0, a0, w_up, a_up in ((w0_f, a0_f, w_up_f, a_up_f), (w0_b, a0_b, w_up_b, a_up_b)):
        w = jnp.exp(-W_DECAY_SCALE * jax.nn.sigmoid(w0 + _mxu_dot(th, w_up)))
        a = jax.nn.sigmoid(a0 + _mxu_dot(lora, a_up))
        kt = ks * (1.0 + (a - 1.0) * k_a)
        outs += [w, a * kk, kt]
    outs.append(_mxu_dot(jax.nn.sigmoid(lora), g_up))
    return tuple(outs)


def fn_merge(o_f, o_b, g_ret, y_f, y_b, r, kt_f, v, g_rw, r_k, ln_w, ln_b, ones64, ones128):
    o = o_f + o_b
    ret = o * lax.rsqrt(_split_dot(o * o, ones128) * (1.0 / RET_DH) + NORM_EPS) * (g_ret * jax.nn.sigmoid(g_ret))
    y = y_f + y_b
    mean = _split_dot(y, ones64) * (1.0 / RW_N)
    yc = y - mean
    var = _split_dot(yc * yc, ones64) * (1.0 / RW_N)
    y_n = yc * lax.rsqrt(var + GN_EPS) * ln_w + ln_b
    bonus = _split_dot(r * kt_f * r_k, ones64) * v
    return ret, (y_n + bonus) * g_rw


def fn_resid_norm_mod(x, mix, gate, shift, scale, g):
    h1 = x + gate * mix
    return h1, _rms(h1, g) * (1.0 + scale) + shift


def relu2(z):
    return jnp.square(jnp.maximum(z, 0.0))


def relu2_backward(act, dact, name):
    bsz, n_tok, width = act.shape

    def body(a_ref, d_ref, du_ref, db_ref):
        du = d_ref[...].astype(f32) * (2.0 * jnp.sqrt(a_ref[...].astype(f32)))
        du_ref[...] = du.astype(du_ref.dtype)

        @pl.when(jnp.logical_and(pl.program_id(0) == 0, pl.program_id(1) == 0))
        def _():
            db_ref[...] = jnp.zeros_like(db_ref)

        db_ref[...] += jnp.sum(du, axis=0, keepdims=True)

    tile = pl.BlockSpec((None, TOK_TILE, width), lambda b, i: (b, i, 0))
    row = pl.BlockSpec((1, width), lambda b, i: (0, 0))
    return pl.pallas_call(body, grid=(bsz, n_tok // TOK_TILE), in_specs=[tile, tile], out_specs=[tile, row],
                          out_shape=[jax.ShapeDtypeStruct(act.shape, MXU_DTYPE), jax.ShapeDtypeStruct((1, width), f32)],
                          compiler_params=_cparams(("arbitrary", "arbitrary")), name=name)(act, dact)


def fn_loss(h1, f, tgt, gate, b2, g):
    y = _rms(h1 + gate * (f + b2), g)
    err = jnp.square(y - tgt)
    return 0.5 * jnp.sum(jnp.mean(err, axis=-1, keepdims=True), axis=0, keepdims=True)


def loss_and_grads(h1, f, tgt, gate, b2, g, bsz, n_tiles):
    def body(h1_ref, f_ref, t_ref, gate_ref, b2_ref, g_ref, loss_ref, dh1_ref, df_ref, dgate_ref, db2_ref, dg_ref):
        b, i = pl.program_id(0), pl.program_id(1)
        tgt_v = t_ref[...]
        loss, vjp = jax.vjp(lambda a, c, e, p, q: fn_loss(a, c, tgt_v, e, p, q),
                            h1_ref[...], f_ref[...], gate_ref[...], b2_ref[...], g_ref[...])
        dh1, df, dgate, db2, dg = vjp(jnp.ones((1, 1), f32))
        dh1_ref[...] = dh1
        df_ref[...] = df.astype(df_ref.dtype)

        @pl.when(i == 0)
        def _():
            dgate_ref[...] = jnp.zeros_like(dgate_ref)

        @pl.when(jnp.logical_and(b == 0, i == 0))
        def _():
            loss_ref[...] = jnp.zeros_like(loss_ref)
            db2_ref[...] = jnp.zeros_like(db2_ref)
            dg_ref[...] = jnp.zeros_like(dg_ref)

        dgate_ref[...] += dgate
        db2_ref[...] += db2
        dg_ref[...] += dg
        loss_ref[...] += jnp.broadcast_to(loss, loss_ref.shape)

    tile = pl.BlockSpec((None, TOK_TILE, D_MODEL), lambda b, i: (b, i, 0))
    row = pl.BlockSpec((1, D_MODEL), lambda b, i: (0, 0))
    seg = pl.BlockSpec((None, None, 1, D_MODEL), lambda b, i: (b, 0, 0, 0))
    t_tok = n_tiles * TOK_TILE
    return pl.pallas_call(
        body, grid=(bsz, n_tiles), in_specs=[tile, tile, tile, seg, row, row],
        out_specs=[pl.BlockSpec((1, 128), lambda b, i: (0, 0)), tile, tile, seg, row, row],
        out_shape=[jax.ShapeDtypeStruct((1, 128), f32), jax.ShapeDtypeStruct((bsz, t_tok, D_MODEL), f32),
                   jax.ShapeDtypeStruct((bsz, t_tok, D_MODEL), MXU_DTYPE),
                   jax.ShapeDtypeStruct((bsz, 1, 1, D_MODEL), f32),
                   jax.ShapeDtypeStruct((1, D_MODEL), f32), jax.ShapeDtypeStruct((1, D_MODEL), f32)],
        compiler_params=_cparams(("arbitrary", "arbitrary")), name="loss_and_grads")(h1, f, tgt, gate, b2, g)


SHIFT_BLOCK = SHIFT_COLS
HALO_ROWS = 8


def _shift_specs(n_tok, col0):
    per_tile = TOK_TILE // HALO_ROWS
    last = n_tok // HALO_ROWS - 1
    tile = pl.BlockSpec((None, TOK_TILE, SHIFT_BLOCK), lambda j, b, i: (b, i, col0 + j))
    prev = pl.BlockSpec((None, HALO_ROWS, SHIFT_BLOCK),
                        lambda j, b, i: (b, jnp.maximum(i * per_tile - 1, 0), col0 + j))
    nxt = pl.BlockSpec((None, HALO_ROWS, SHIFT_BLOCK),
                       lambda j, b, i: (b, jnp.minimum((i + 1) * per_tile, last), col0 + j))
    return tile, prev, nxt


def _shifted(p, prev_ref, next_ref, is_first, is_last):
    row = lax.broadcasted_iota(jnp.int32, p.shape, 0)
    prev_row = jnp.where(is_first, 0.0, prev_ref[HALO_ROWS - 1:HALO_ROWS, :].astype(f32))
    next_row = jnp.where(is_last, 0.0, next_ref[0:1, :].astype(f32))
    prev = jnp.where(row == 0, prev_row, pltpu.roll(p, 1, axis=0))
    nxt = jnp.where(row == TOK_TILE - 1, next_row, pltpu.roll(p, TOK_TILE - 1, axis=0))
    return prev, nxt


def token_shift(px, mu, seg_first, seg_last):
    bsz, n_tok, _ = px.shape
    n_tiles = n_tok // TOK_TILE

    def body(p_ref, prev_ref, next_ref, mu_ref, o_ref):
        i = pl.program_id(2)
        p = p_ref[...]
        prev, nxt = _shifted(p, prev_ref, next_ref, seg_first(i), seg_last(i))
        o_ref[...] = p + mu_ref[0:1, :] * (prev - p) + mu_ref[1:2, :] * (nxt - p)

    tile, prev, nxt = _shift_specs(n_tok, 0)
    return pl.pallas_call(
        body, grid=(SHIFT_COLS // SHIFT_BLOCK, bsz, n_tiles),
        in_specs=[tile, prev, nxt, pl.BlockSpec((2, SHIFT_BLOCK), lambda j, b, i: (0, j))],
        out_specs=pl.BlockSpec((None, TOK_TILE, SHIFT_BLOCK), lambda j, b, i: (b, i, j)),
        out_shape=jax.ShapeDtypeStruct((bsz, n_tok, SHIFT_COLS), f32),
        compiler_params=_cparams(("parallel", "parallel", "parallel")), name="token_shift")(px, px, px, mu)


def token_shift_bwd(dps, px, mu, seg_first, seg_last):
    bsz, n_tok, _ = px.shape
    n_tiles = n_tok // TOK_TILE

    def body(d_ref, dprev_ref, dnext_ref, p_ref, prev_ref, next_ref, mu_ref, dp_ref, dmu_ref):
        b, i = pl.program_id(1), pl.program_id(2)
        first, last = seg_first(i), seg_last(i)
        d, p = d_ref[...], p_ref[...]
        d_prev, d_next = _shifted(d, dprev_ref, dnext_ref, first, last)
        p_prev, p_next = _shifted(p, prev_ref, next_ref, first, last)
        mu0, mu1 = mu_ref[0:1, :], mu_ref[1:2, :]
        dp_ref[...] = (d + mu0 * (d_next - d) + mu1 * (d_prev - d)).astype(dp_ref.dtype)

        @pl.when(jnp.logical_and(b == 0, i == 0))
        def _():
            dmu_ref[...] = jnp.zeros_like(dmu_ref)

        dmu_ref[0:1, :] += jnp.sum(d * (p_prev - p), axis=0, keepdims=True)
        dmu_ref[1:2, :] += jnp.sum(d * (p_next - p), axis=0, keepdims=True)

    dtile, dprev, dnext = _shift_specs(n_tok, 0)
    tile, prev, nxt = _shift_specs(n_tok, 0)
    mu_spec = pl.BlockSpec((2, SHIFT_BLOCK), lambda j, b, i: (0, j))
    return pl.pallas_call(
        body, grid=(SHIFT_COLS // SHIFT_BLOCK, bsz, n_tiles),
        in_specs=[dtile, dprev, dnext, tile, prev, nxt, mu_spec],
        out_specs=[pl.BlockSpec((None, TOK_TILE, SHIFT_BLOCK), lambda j, b, i: (b, i, j)), mu_spec],
        out_shape=[jax.ShapeDtypeStruct((bsz, n_tok, SHIFT_COLS), MXU_DTYPE),
                   jax.ShapeDtypeStruct((2, SHIFT_COLS), f32)],
        compiler_params=_cparams(("arbitrary", "arbitrary", "arbitrary")), name="token_shift_bwd")(
        dps, dps, dps, px, px, px, mu)


def _dg(a, b, ca, cb):
    return lax.dot_general(a.astype(MXU_DTYPE), b.astype(MXU_DTYPE), (((ca,), (cb,)), ((), ())),
                           preferred_element_type=f32)


@jax.custom_vjp
def _mm_nt(a, b):
    return _dg(a, b, 1, 1)


_mm_nt.defvjp(lambda a, b: (_dg(a, b, 1, 1), (a, b)),
              lambda res, ct: (_dg(ct, res[1], 1, 0), _dg(ct, res[0], 0, 0)))


@jax.custom_vjp
def _mm_tn(a, b):
    return _dg(a, b, 0, 0)


_mm_tn.defvjp(lambda a, b: (_dg(a, b, 0, 0), (a, b)),
              lambda res, ct: (_dg(res[1], ct, 1, 1), _dg(res[0], ct, 1, 0)))


ROTARY_PAIR = RET_DH // 4


def _swap_pairs_impl(t):
    lane = lax.broadcasted_iota(jnp.int32, t.shape, 1)
    return jnp.where(lane % (2 * ROTARY_PAIR) < ROTARY_PAIR, pltpu.roll(t, RET_DH - ROTARY_PAIR, axis=1),
                     pltpu.roll(t, ROTARY_PAIR, axis=1))


@jax.custom_vjp
def _swap_pairs(t):
    return _swap_pairs_impl(t)


_swap_pairs.defvjp(lambda t: (_swap_pairs_impl(t), None), lambda _, ct: (_swap_pairs_impl(ct),))


def _ret_chunk(state, q_raw, k_raw, v, cos, sin, ld_row, head, reverse):
    c = RET_CHUNK
    lane = lax.broadcasted_iota(jnp.int32, ld_row.shape, 1)
    lg = -jnp.exp(jnp.sum(jnp.where(lane == head, ld_row, 0.0), axis=-1, keepdims=True))
    rot = lambda t: t * cos + _swap_pairs(t) * sin
    q = rot(q_raw)
    k = rot(k_raw) * (RET_DH ** -0.5)
    ti = lax.broadcasted_iota(jnp.int32, (c, 1), 0).astype(f32)
    tj = lax.broadcasted_iota(jnp.int32, (1, c), 1).astype(f32)
    if not reverse:
        dist, mask, q_exp, k_exp = ti - tj, (ti - tj) >= 0, ti + 1.0, c - 1.0 - ti
    else:
        dist, mask, q_exp, k_exp = tj - ti, (tj - ti) > 0, c - ti, ti
    decay = jnp.where(mask, jnp.exp(lg * jnp.maximum(dist, 0.0)), 0.0)
    scores = _mm_nt(q, k) * decay
    out = _mxu_dot(scores, v) + _mxu_dot(q * jnp.exp(lg * q_exp), state)
    new_state = state * jnp.exp(lg * c) + _mm_tn(k * jnp.exp(lg * k_exp), v)
    return out, new_state


def _ret_specs(bsz, order):
    tok = lambda col=0: pl.BlockSpec((bsz, RET_CHUNK, RET_W), lambda i: (0, order(i), col))
    tab = pl.BlockSpec((RET_CHUNK, RET_DH), lambda i: (order(i), 0))
    ld = pl.BlockSpec((1, RET_DH), lambda i: (0, 0))
    return tok, tab, ld


def retention_fwd(px, cos, sin, ld_row, order, reverse, name):
    bsz, n_tok, _ = px.shape
    n_ch = n_tok // RET_CHUNK

    def body(q_ref, k_ref, v_ref, cos_ref, sin_ref, ld_ref, o_ref, sv_ref, st_ref):
        @pl.when(pl.program_id(0) == 0)
        def _():
            st_ref[...] = jnp.zeros_like(st_ref)

        for b in range(bsz):
            for h in range(RET_HEADS):
                sl = slice(h * RET_DH, (h + 1) * RET_DH)
                s = st_ref[b, h]
                sv_ref[b, h] = s
                o, s_new = _ret_chunk(s, q_ref[b, :, sl], k_ref[b, :, sl], v_ref[b, :, sl], cos_ref[...], sin_ref[...],
                                      ld_ref[...], h, reverse)
                o_ref[b, :, sl] = o
                st_ref[b, h] = s_new

    tok, tab, ld = _ret_specs(bsz, order)
    return pl.pallas_call(
        body, grid=(n_ch,), in_specs=[tok(0), tok(1), tok(2), tab, tab, ld],
        out_specs=[tok(), pl.BlockSpec((bsz, None, RET_HEADS, RET_DH, RET_DH), lambda i: (0, i, 0, 0, 0))],
        out_shape=[jax.ShapeDtypeStruct((bsz, n_tok, RET_W), f32),
                   jax.ShapeDtypeStruct((bsz, n_ch, RET_HEADS, RET_DH, RET_DH), f32)],
        scratch_shapes=[pltpu.VMEM((bsz, RET_HEADS, RET_DH, RET_DH), f32)],
        compiler_params=_cparams(("arbitrary",)), name=name)(px, px, px, cos, sin, ld_row)


def retention_bwd(do, px, states, cos, sin, ld_row, order, reverse, name):
    bsz, n_tok, _ = px.shape
    n_ch = n_tok // RET_CHUNK
    back = lambda i: order(n_ch - 1 - i)

    def body(do_ref, q_ref, k_ref, v_ref, sv_ref, cos_ref, sin_ref, ld_ref,
             dq_ref, dk_ref, dv_ref, dld_ref, dst_ref):
        @pl.when(pl.program_id(0) == 0)
        def _():
            dst_ref[...] = jnp.zeros_like(dst_ref)
            dld_ref[...] = jnp.zeros_like(dld_ref)

        cos_v, sin_v = cos_ref[...], sin_ref[...]
        for b in range(bsz):
            for h in range(RET_HEADS):
                sl = slice(h * RET_DH, (h + 1) * RET_DH)
                f = lambda s, q, k, v, ld, h=h: _ret_chunk(s, q, k, v, cos_v, sin_v, ld, h, reverse)
                _, vjp = jax.vjp(f, sv_ref[b, h], q_ref[b, :, sl], k_ref[b, :, sl], v_ref[b, :, sl], ld_ref[...])
                ds, dq, dk, dv, dld = vjp((do_ref[b, :, sl], dst_ref[b, h]))
                dst_ref[b, h] = ds
                dq_ref[b, :, sl] = dq
                dk_ref[b, :, sl] = dk
                dv_ref[b, :, sl] = dv
                dld_ref[...] += dld

    tok, tab, ld = _ret_specs(bsz, back)
    return pl.pallas_call(
        body, grid=(n_ch,),
        in_specs=[tok(), tok(0), tok(1), tok(2),
                  pl.BlockSpec((bsz, None, RET_HEADS, RET_DH, RET_DH), lambda i: (0, n_ch - 1 - i, 0, 0, 0)),
                  tab, tab, ld],
        out_specs=[tok(), tok(), tok(), ld],
        out_shape=[jax.ShapeDtypeStruct((bsz, n_tok, RET_W), f32)] * 3 + [jax.ShapeDtypeStruct((1, RET_DH), f32)],
        scratch_shapes=[pltpu.VMEM((bsz, RET_HEADS, RET_DH, RET_DH), f32)],
        compiler_params=_cparams(("arbitrary",)), name=name)(
        do, px, px, px, states, cos, sin, ld_row)


HALF_W = RW_W // 2


def _head_sum(x, ones):
    xm = x.astype(MXU_DTYPE)
    return jnp.concatenate([jnp.dot(xm[:, :HALF_W], ones, preferred_element_type=f32),
                            jnp.dot(xm[:, HALF_W:], ones, preferred_element_type=f32)], axis=1)


def _stack(parts):
    return jnp.concatenate(parts, axis=0)


def _row(ref, b, t):
    return ref[b, pl.ds(t, 1), :]


SCAN_DIRS = ((False, True), (True, False))
RW_HEADS = RW_W // RW_N
HEAD_ROWS_PAD = 16


def _head_rows(row, mask):
    return jnp.broadcast_to(row, mask.shape) * mask


def _outer(per_value, row, mask_pad):
    return lax.dot_general(per_value.astype(MXU_DTYPE), _head_rows(row, mask_pad).astype(MXU_DTYPE),
                           (((0,), (0,)), ((), ())), preferred_element_type=f32)


def _read(states, rows, mask, more_rows=()):
    lhs = _stack([_head_rows(r, mask) for r in list(rows) + list(more_rows)])
    return lax.dot_general(lhs.astype(MXU_DTYPE), _stack(states).astype(MXU_DTYPE), (((1,), (1,)), ((), ())),
                           preferred_element_type=f32)


def _own_block(raw, b):
    lanes = raw[:, RW_N * b:RW_N * (b + 1)]
    turned = _stack([lanes[RW_HEADS * b:], lanes[:RW_HEADS * b]]) if b else lanes
    if turned.shape[0] < HEAD_ROWS_PAD:
        turned = _stack([turned, jnp.zeros((HEAD_ROWS_PAD - turned.shape[0], RW_N), f32)])
    return turned[:HEAD_ROWS_PAD]


def _row_from_heads(per_value, state, mask_pad):
    full = jnp.dot(per_value.astype(MXU_DTYPE), state.astype(MXU_DTYPE), preferred_element_type=f32)
    return jnp.sum(full * mask_pad, axis=0, keepdims=True)


def _scan_specs(bsz, order):
    rows = lambda col=0: pl.BlockSpec((bsz, SCAN_CHUNK, RW_W), lambda i: (0, order(i), col))
    per_value = pl.BlockSpec((bsz, SCAN_CHUNK, HEAD_ROWS_PAD, RW_N), lambda i: (0, order(i), 0, 0))
    states = pl.BlockSpec((SCAN_CHUNK, bsz, RW_N, RW_W), lambda i: (order(i), 0, 0, 0))
    blocks = pl.BlockSpec((SCAN_CHUNK, RW_HEADS * bsz, RW_N * bsz), lambda i: (order(i), 0, 0))
    return rows, per_value, states, blocks


def _mxu_operands(states):
    return [s.astype(MXU_DTYPE) for s in states]


def _removed(states_m, kk_t, ones, bsz):
    removed = _head_sum(_stack([states_m[b] * kk_t[b].astype(MXU_DTYPE) for b in range(bsz)]), ones)
    return [removed[b * RW_N:(b + 1) * RW_N] for b in range(bsz)]


def _advance(sp, rem, w_t, b_t, vk, bsz):
    return [sp[b] * w_t[b] - rem[b] * b_t[b] + vk[b] for b in range(bsz)]


def heads_to_rows(a):
    b, t, _ = a.shape
    return jnp.pad(a.astype(MXU_DTYPE).reshape(b, t, RW_HEADS, RW_N),
                   ((0, 0), (0, 0), (0, HEAD_ROWS_PAD - RW_HEADS), (0, 0)))


def _blocks_to_rows(raw_ref, first, row_ref, bsz):
    steps = pl.ds(first, SCAN_CHUNK)
    for b in range(bsz):
        for h in range(RW_HEADS):
            row_ref[b, :, h * RW_N:(h + 1) * RW_N] = raw_ref[steps, RW_HEADS * b + h, RW_N * b:RW_N * (b + 1)]


N_ROWS_FWD = 5
N_ROWS_BWD = 5


def _scan_consts(bsz):
    head = (jnp.arange(RW_W)[None, :] // RW_N == jnp.arange(RW_HEADS)[:, None]).astype(f32)
    return head, jnp.pad(head, ((0, HEAD_ROWS_PAD - RW_HEADS), (0, 0))), _block_ones(HALF_W, RW_N)


def _const_specs(consts):
    return [pl.BlockSpec(c.shape, lambda i: (0, 0)) for c in consts]


def rwkv_scan_fwd(rows_in, v_heads, orders, name):
    bsz, n_tok, _ = rows_in[0][0][0].shape
    n_ch = n_tok // SCAN_CHUNK
    rng = range(bsz)
    consts = _scan_consts(bsz)

    def body(*refs):
        rows = [refs[:N_ROWS_FWD], refs[N_ROWS_FWD:2 * N_ROWS_FWD]]
        (v0, v1, head_ref, pad_ref, ones_ref, y0, y1, h0, h1, f0, f1, m0, m1, s0, s1, late_ref,
         raw_ref) = refs[2 * N_ROWS_FWD:]
        v_refs, y_refs, hist_refs, final_refs, s_refs = (v0, v1), (y0, y1), (h0, h1), (f0, f1), (s0, s1)
        removed_refs = (m0, m1)
        n_blk = RW_HEADS * bsz
        head_v, pad_v, ones_v = head_ref[...], pad_ref[...], ones_ref[...]
        for d in range(2):
            @pl.when(pl.program_id(0) == 0)
            def _(d=d):
                s_refs[d][...] = jnp.zeros_like(s_refs[d])

        def step(j, carry):
            ts = [SCAN_CHUNK - 1 - j if reverse else j for reverse, _ in SCAN_DIRS]
            sps = [[s_refs[d][b] for b in rng] for d in range(2)]
            sps_m = [_mxu_operands(sps[d]) for d in range(2)]
            vks = [[_outer(v_refs[d][b, ts[d]], _row(rows[d][4], b, ts[d]), pad_v) for b in rng] for d in range(2)]
            rems = [_removed(sps_m[d], [_row(rows[d][1], b, ts[d]) for b in rng], ones_v, bsz) for d in range(2)]
            for d, (reverse, inclusive) in enumerate(SCAN_DIRS):
                r_ref = rows[d][0]
                read_at = jnp.maximum(j - 1, 0) if inclusive else ts[d]
                both = _read(sps_m[d], [_row(r_ref, b, read_at) for b in rng], head_v,
                             [_row(rows[d][1], b, ts[d]) for b in rng])
                if inclusive:
                    late_ref[j] = both[:n_blk]
                else:
                    raw_ref[ts[d]] = both[:n_blk]
                removed_refs[d][ts[d]] = both[n_blk:]
            for d in range(2):
                new = _advance(sps[d], rems[d], [_row(rows[d][2], b, ts[d]) for b in rng],
                               [_row(rows[d][3], b, ts[d]) for b in rng], vks[d], bsz)
                for b in rng:
                    hist_refs[d][ts[d], b] = sps_m[d][b]
                    s_refs[d][b] = new[b]
            return carry

        lax.fori_loop(0, SCAN_CHUNK, step, 0, unroll=SCAN_UNROLL)
        for d, (reverse, inclusive) in enumerate(SCAN_DIRS):
            final_refs[d][...] = s_refs[d][...]
            if inclusive:
                assert not reverse
                last = SCAN_CHUNK - 1
                late_ref[SCAN_CHUNK] = _read(_mxu_operands([s_refs[d][b] for b in rng]),
                                             [rows[d][0][b, last:last + 1, :] for b in rng], head_v)
                _blocks_to_rows(late_ref, 1, y_refs[d], bsz)
            else:
                _blocks_to_rows(raw_ref, 0, y_refs[d], bsz)

    specs = [_scan_specs(bsz, orders[d]) for d in range(2)]
    state = pltpu.VMEM((bsz, RW_N, RW_W), f32)
    late = pltpu.VMEM((SCAN_CHUNK + 1, RW_HEADS * bsz, RW_N * bsz), f32)
    raw = pltpu.VMEM((SCAN_CHUNK, RW_HEADS * bsz, RW_N * bsz), f32)
    final_spec = pl.BlockSpec((bsz, RW_N, RW_W), lambda i: (0, 0, 0))
    return pl.pallas_call(
        body, grid=(n_ch,),
        in_specs=[specs[d][0](col) for d in range(2) for _, col in rows_in[d]] + [specs[0][1], specs[1][1]]
        + _const_specs(consts),
        out_specs=[specs[0][0](), specs[1][0](), specs[0][2], specs[1][2], final_spec, final_spec,
                   specs[0][3], specs[1][3]],
        out_shape=[jax.ShapeDtypeStruct((bsz, n_tok, RW_W), f32)] * 2
        + [jax.ShapeDtypeStruct((n_tok, bsz, RW_N, RW_W), MXU_DTYPE)] * 2
        + [jax.ShapeDtypeStruct((bsz, RW_N, RW_W), f32)] * 2
        + [jax.ShapeDtypeStruct((n_tok, RW_HEADS * bsz, RW_N * bsz), f32)] * 2,
        scratch_shapes=[state, state, late, raw],
        compiler_params=_cparams(("arbitrary",)), name=name)(
        *[a for d in range(2) for a, _ in rows_in[d]], v_heads, v_heads, *consts)


def rwkv_scan_bwd(rows_in, v_heads, dy_heads, hists, finals, removed, orders, name):
    bsz, n_tok, _ = rows_in[0][0][0].shape
    n_ch = n_tok // SCAN_CHUNK
    backs = [functools.partial(lambda i, order: order(n_ch - 1 - i), order=orders[d]) for d in range(2)]
    rng = range(bsz)
    consts = _scan_consts(bsz)
    n_out, n_scr = 6, 6

    def body(*refs):
        rows = [refs[:N_ROWS_BWD], refs[N_ROWS_BWD:2 * N_ROWS_BWD]]
        rest = refs[2 * N_ROWS_BWD:]
        v_refs, dy_refs, hist_refs, final_refs, removed_refs = rest[0:2], rest[2:4], rest[4:6], rest[6:8], rest[8:10]
        head_ref, pad_ref, ones_ref = rest[10:13]
        outs = [rest[13:13 + n_out], rest[13 + n_out:13 + 2 * n_out]]
        scr = [rest[13 + 2 * n_out:13 + 2 * n_out + n_scr], rest[13 + 2 * n_out + n_scr:]]
        n_blk = RW_HEADS * bsz
        head_v, pad_v, ones_v = head_ref[...], pad_ref[...], ones_ref[...]
        for d in range(2):
            @pl.when(pl.program_id(0) == 0)
            def _(d=d):
                scr[d][1][...] = jnp.zeros_like(scr[d][1])
                scr[d][0][...] = final_refs[d][...]

        def step_of(j, reverse):
            return j if reverse else SCAN_CHUNK - 1 - j

        for d, (reverse, _) in enumerate(SCAN_DIRS):
            t0 = step_of(0, reverse)
            for b in rng:
                scr[d][3][b] = _outer(dy_refs[d][b, t0], rows[d][0][b, t0:t0 + 1, :], pad_v)

        def bstep(j, carry):
            ts = [step_of(j, reverse) for reverse, _ in SCAN_DIRS]
            reads = [[scr[d][3][b] for b in rng] for d in range(2)]
            dss = []
            for d, (_, inclusive) in enumerate(SCAN_DIRS):
                ds = [scr[d][1][b] for b in rng]
                dss.append([ds[b] + reads[d][b] for b in rng] if inclusive else ds)
            dss_m = [_mxu_operands(dss[d]) for d in range(2)]
            nexts = []
            for d, (reverse, _) in enumerate(SCAN_DIRS):
                t_next = step_of(jnp.minimum(j + 1, SCAN_CHUNK - 1), reverse)
                nexts.append([_outer(dy_refs[d][b, t_next], _row(rows[d][0], b, t_next), pad_v) for b in rng])
            drems = [_removed(dss_m[d], [-_row(rows[d][3], b, ts[d]) for b in rng], ones_v, bsz) for d in range(2)]
            for d in range(2):
                for b in rng:
                    scr[d][3][b] = nexts[d][b]
                both = _read(dss_m[d], [_row(rows[d][4], b, ts[d]) for b in rng], head_v,
                             [-_row(rows[d][3], b, ts[d]) for b in rng])
                scr[d][4][ts[d]] = both[:n_blk]
                scr[d][5][ts[d]] = both[n_blk:]
            for d, (_, inclusive) in enumerate(SCAN_DIRS):
                _, kk_ref, w_ref, _, _ = rows[d]
                _, ds_ref, dsh_ref = scr[d][:3]
                for b in rng:
                    dsh_ref[ts[d], b] = dss[d][b]
                    dsp = dss[d][b] * _row(w_ref, b, ts[d]) + drems[d][b] * _row(kk_ref, b, ts[d])
                    ds_ref[b] = dsp if inclusive else dsp + reads[d][b]
            return carry

        lax.fori_loop(0, SCAN_CHUNK, bstep, 0, unroll=SCAN_UNROLL)

        rsum = lambda z: jnp.sum(z, axis=0, keepdims=True)
        for d, (reverse, inclusive) in enumerate(SCAN_DIRS):
            dr_ref, dkk_ref, dw_ref, db_ref, dkt_ref, dv_ref = outs[d]
            after_ref, _, dsh_ref, _, dv_raw_ref, dremt_ref = scr[d]
            hist_ref, removed_ref = hist_refs[d], removed_refs[d]
            _blocks_to_rows(dv_raw_ref, 0, dv_ref, bsz)
            for t in range(SCAN_CHUNK):
                ts = slice(t, t + 1)
                after = t - 1 if reverse else t + 1
                for b in rng:
                    sp_m, ds = hist_ref[t, b], dsh_ref[t, b]
                    sp = sp_m.astype(f32)
                    if not inclusive:
                        seen = sp_m
                    else:
                        seen = hist_ref[after, b] if 0 <= after < SCAN_CHUNK else after_ref[b]
                    dr_ref[b, ts, :] = _row_from_heads(dy_refs[d][b, t], seen, pad_v)
                    dkt_ref[b, ts, :] = _row_from_heads(v_refs[d][b, t], ds, pad_v)
                    dw_ref[b, ts, :] = rsum(ds * sp)
                    db_ref[b, ts, :] = -_row_from_heads(_own_block(removed_ref[t], b), ds, pad_v)
                    dkk_ref[b, ts, :] = _row_from_heads(_own_block(dremt_ref[t], b), sp_m, pad_v)
            if inclusive:
                first = SCAN_CHUNK - 1 if reverse else 0
                for b in rng:
                    after_ref[b] = hist_ref[first, b].astype(f32)

    specs = [_scan_specs(bsz, backs[d]) for d in range(2)]
    hist = pltpu.VMEM((SCAN_CHUNK, bsz, RW_N, RW_W), f32)
    state = pltpu.VMEM((bsz, RW_N, RW_W), f32)
    final_spec = pl.BlockSpec((bsz, RW_N, RW_W), lambda i: (0, 0, 0))
    raw = pltpu.VMEM((SCAN_CHUNK, RW_HEADS * bsz, RW_N * bsz), f32)
    return pl.pallas_call(
        body, grid=(n_ch,),
        in_specs=[specs[d][0](col) for d in range(2) for _, col in rows_in[d]]
        + [specs[0][1], specs[1][1]] * 2 + [specs[0][2], specs[1][2], final_spec, final_spec, specs[0][3], specs[1][3]]
        + _const_specs(consts),
        out_specs=[specs[d][0]() for d in range(2) for _ in range(n_out)],
        out_shape=[jax.ShapeDtypeStruct((bsz, n_tok, RW_W), f32)] * (2 * n_out),
        scratch_shapes=[state, state, hist, state, raw, raw] * 2,
        compiler_params=_cparams(("arbitrary",)), name=name)(
        *[a for d in range(2) for a, _ in rows_in[d]], v_heads, v_heads, dy_heads, dy_heads, *hists, *finals, *removed, *consts)


MOD_NAMES = ("shift1", "scale1", "gate1", "shift2", "scale2", "gate2")


def _rope_tables(t_ctx, t_x):
    quarter = RET_DH // 4
    pos = jnp.arange(t_x)
    inv = jnp.power(ROPE_BASE, -jnp.arange(0, 2 * quarter, 2, dtype=f32) / (2 * quarter))
    ang_r = (pos // GRID_W).astype(f32)[:, None] * inv[None, :]
    ang_c = (pos % GRID_W).astype(f32)[:, None] * inv[None, :]
    cos = jnp.concatenate([jnp.cos(ang_r)] * 2 + [jnp.cos(ang_c)] * 2, axis=1)
    sin = jnp.concatenate([-jnp.sin(ang_r), jnp.sin(ang_r), -jnp.sin(ang_c), jnp.sin(ang_c)], axis=1)
    cos = jnp.concatenate([jnp.ones((t_ctx, RET_DH), f32), cos], axis=0)
    sin = jnp.concatenate([jnp.zeros((t_ctx, RET_DH), f32), sin], axis=0)
    return cos, sin


def _pad_rows(w, lo, total):
    return jnp.pad(w, ((lo, total - lo - w.shape[0]), (0, 0)))


LATE_WEIGHTS = ("w_out", "w_ff1", "w_ff2")


def layer_step(x, ctx, tgt, mod_x, mod_ctx, wt, late_weights=None, early_grads=None):
    bsz, t_x, _ = x.shape
    t_c = ctx.shape[1]
    t_all = t_c + t_x
    n_ct, n_xt = t_c // TOK_TILE, t_x // TOK_TILE
    n_t = n_ct + n_xt
    assert t_c % TOK_TILE == 0 and t_x % TOK_TILE == 0 and t_c % RET_CHUNK == 0

    seg = lambda i: (i >= n_ct).astype(jnp.int32)
    seg_first = lambda i: jnp.logical_or(i == 0, i == n_ct)
    seg_last = lambda i: jnp.logical_or(i == n_ct - 1, i == n_t - 1)
    mod_all = {n: jnp.stack([jnp.broadcast_to(mod_ctx[k], (bsz, D_MODEL)), mod_x[:, k]], axis=1)[:, :, None, :]
               for k, n in enumerate(MOD_NAMES)}
    mod_lat = {n: mod_x[:, k][:, None, None, :] for k, n in enumerate(MOD_NAMES)}
    both = lambda n: Seg(mod_all[n], seg, seg_first)
    lat = lambda n: Seg(mod_lat[n], lambda i: 0, lambda i: i == 0)
    flat = lambda a: a.reshape(-1, a.shape[-1])

    def chunk_orders(n_ctx_chunks, n_chunks):
        fwd = lambda i: i
        bwd = lambda i: jnp.where(i < n_ctx_chunks, n_ctx_chunks - 1 - i, n_chunks + n_ctx_chunks - 1 - i)
        return fwd, bwd

    ones64, ones128 = _block_ones(RW_W, RW_N), _block_ones(RET_W, RET_DH)
    cos, sin = _rope_tables(t_c, t_x)
    ld_rows = [jnp.pad(wt["ret_log_decay"][d][None, :], ((0, 0), (0, RET_DH - RET_HEADS))) for d in range(2)]
    w_up_pad = [_pad_rows(wt["rwkv_w_up"][d], 0, LORA_W) for d in range(2)]
    a_up_pad = [_pad_rows(wt["rwkv_a_up"][d], DECAY_LORA, LORA_W) for d in range(2)]
    g_up_pad = _pad_rows(wt["rwkv_g_up"], DECAY_LORA + AAA_LORA, LORA_W)
    row = lambda a, d: a[d][None, :]

    h = jnp.concatenate([ctx, x], axis=1)
    norm1_ins = lambda: [Tiled(h), both("shift1"), both("scale1"), Glob(wt["norm1_g"])]
    (n1,) = ew_forward(fn_norm_mod, "norm1", bsz, n_t, norm1_ins(), [(D_MODEL, MXU_DTYPE)])
    px = matmul(flat(n1), wt["w_in"], "nn", "proj_in").reshape(bsz, t_all, IN_COLS)
    px_rw = px[..., RET_COLS:]
    ps = token_shift(px_rw, wt["rwkv_shift_mu"], seg_first, seg_last)

    def prep_ins(toff=0):
        return [Tiled(ps, RW_W, 1), Tiled(ps, LORA_W, 3 * RW_W // LORA_W),
                Glob(row(wt["rwkv_w0"], 0)), Glob(row(wt["rwkv_w0"], 1)),
                Glob(row(wt["rwkv_a0"], 0)), Glob(row(wt["rwkv_a0"], 1)),
                Glob(w_up_pad[0]), Glob(w_up_pad[1]), Glob(a_up_pad[0]), Glob(a_up_pad[1]), Glob(g_up_pad),
                Glob(wt["rwkv_k_k"]), Glob(wt["rwkv_k_a"]), Glob(ones64)]

    kk, w_f, b_f, kt_f, w_b, b_b, kt_b, g_rw = ew_forward(fn_rwkv_prepare, "rwkv_prepare", bsz, n_t, prep_ins(),
                                                           [(RW_W, f32)] * 8)
    rw_order = chunk_orders(t_c // SCAN_CHUNK, t_all // SCAN_CHUNK)
    ret_order = chunk_orders(t_c // RET_CHUNK, t_all // RET_CHUNK)
    scan_rows = [[(ps, 0), (kk, 0), (w_f, 0), (b_f, 0), (kt_f, 0)], [(ps, 0), (kk, 0), (w_b, 0), (b_b, 0), (kt_b, 0)]]
    v_heads = heads_to_rows(ps[..., 2 * RW_W:3 * RW_W])
    y_f, y_b, *kept_states = rwkv_scan_fwd(scan_rows, v_heads, rw_order, "rwkv_scan_fwd")
    y = [y_f, y_b]
    o, ret_states = [], []
    for d in range(2):
        o_d, st_d = retention_fwd(px, cos, sin, ld_rows[d], ret_order[d], SCAN_DIRS[d][0], f"retention_fwd{d}")
        o.append(o_d), ret_states.append(st_d)

    def merge_ins(toff):
        return [Tiled(o[0], toff=toff), Tiled(o[1], toff=toff), Tiled(px, RET_W, 3, toff),
                Tiled(y[0], toff=toff), Tiled(y[1], toff=toff), Tiled(ps, RW_W, 0, toff), Tiled(kt_f, toff=toff),
                Tiled(ps, RW_W, 2, toff), Tiled(g_rw, toff=toff),
                Glob(wt["rwkv_r_k"]), Glob(wt["rwkv_ln_w"]), Glob(wt["rwkv_ln_b"]), Glob(ones64), Glob(ones128)]

    ret_out, rw_out = ew_forward(fn_merge, "merge_heads", bsz, n_xt, merge_ins(n_ct),
                                 [(RET_W, MXU_DTYPE), (RW_W, MXU_DTYPE)])
    merged = jnp.concatenate([ret_out, rw_out], axis=-1)
    if late_weights is not None:
        wt = {**wt, **late_weights(merged)}
    mix = matmul(flat(merged), wt["w_out"], "nn", "proj_out").reshape(bsz, t_x, D_MODEL)
    resid_ins = lambda: [Tiled(x), Tiled(mix), lat("gate1"), lat("shift2"), lat("scale2"), Glob(wt["norm2_g"])]
    h1, n2 = ew_forward(fn_resid_norm_mod, "resid_norm2", bsz, n_xt, resid_ins(), [(D_MODEL, f32), (D_MODEL, MXU_DTYPE)])
    act = matmul(flat(n2), wt["w_ff1"], "nn", "ff1", MXU_DTYPE, wt["b_ff1"], relu2).reshape(bsz, t_x, D_FF)
    ff = matmul(flat(act), wt["w_ff2"], "nn", "ff2").reshape(bsz, t_x, D_MODEL)

    g = {}
    loss, dh1, dff, dgate2, g["b_ff2"], g["final_g"] = loss_and_grads(
        h1, ff, tgt, mod_lat["gate2"], wt["b_ff2"], wt["final_g"], bsz, n_xt)
    dact = matmul(flat(dff), wt["w_ff2"], "nt", "ff2_dx", MXU_DTYPE).reshape(bsz, t_x, D_FF)
    g["w_ff2"] = matmul(flat(act), flat(dff), "tn", "ff2_dw", MXU_DTYPE)
    du, g["b_ff1"] = relu2_backward(act, dact, "relu2_bwd")
    dn2 = matmul(flat(du), wt["w_ff1"], "nt", "ff1_dx").reshape(bsz, t_x, D_MODEL)
    g["w_ff1"] = matmul(flat(n2), flat(du), "tn", "ff1_dw", MXU_DTYPE)
    dx_res, dmix, dgate1, dshift2, dscale2, g["norm2_g"] = ew_backward(
        fn_resid_norm_mod, "resid_norm2_bwd", bsz, n_xt, resid_ins(), [Tiled(dh1), Tiled(dn2)], [True] * 6,
        {1: MXU_DTYPE})
    dmerged = matmul(flat(dmix), wt["w_out"], "nt", "proj_out_dx").reshape(bsz, t_x, D_MODEL)
    g["w_out"] = matmul(flat(merged), flat(dmix), "tn", "proj_out_dw", MXU_DTYPE)
    if early_grads is not None:
        token = early_grads({n: g.pop(n) for n in LATE_WEIGHTS})
        wt = {**wt, "rwkv_r_k": wt["rwkv_r_k"] + token[:1, :1]}
    (do, dg_ret, dy, dr_m, dkt_m, dv_m, dg_rw, g["rwkv_r_k"], g["rwkv_ln_w"], g["rwkv_ln_b"]) = ew_backward(
        fn_merge, "merge_heads_bwd", bsz, n_xt, merge_ins(0),
        [Tiled(dmerged, RET_W, 0, -n_ct), Tiled(dmerged, RW_W, 1, -n_ct)],
        [True, False, True, True, False, True, True, True, True, True, True, True, False, False], lead=n_ct)

    dqkv, dld = [], []
    for d in range(2):
        *dqkv_d, dld_d = retention_bwd(do, px, ret_states[d], cos, sin, ld_rows[d], ret_order[d],
                                       SCAN_DIRS[d][0], f"retention_bwd{d}")
        dqkv.append(dqkv_d), dld.append(dld_d[0, :RET_HEADS])
    g["ret_log_decay"] = jnp.stack(dld)
    (dr_f, dkk_f, dw_f, db_f, dkt_f, dv_f, dr_b, dkk_b, dw_b, db_b, dkt_b, dv_b) = rwkv_scan_bwd(
        scan_rows, v_heads, heads_to_rows(dy), kept_states[:2], kept_states[2:4], kept_states[4:], rw_order, "rwkv_scan_bwd")
    prep_cts = [dkk_f + dkk_b, dw_f, db_f, dkt_f + dkt_m, dw_b, db_b, dkt_b, dg_rw]
    (dks, dlora, dw0_f, dw0_b, da0_f, da0_b, dwup_f, dwup_b, daup_f, daup_b, dgup, g["rwkv_k_k"],
     g["rwkv_k_a"]) = ew_backward(fn_rwkv_prepare, "rwkv_prepare_bwd", bsz, n_t, prep_ins(),
                                  [Tiled(c) for c in prep_cts], [True] * 13 + [False])
    g["rwkv_w0"] = jnp.concatenate([dw0_f, dw0_b], axis=0)
    g["rwkv_a0"] = jnp.concatenate([da0_f, da0_b], axis=0)
    g["rwkv_w_up"] = jnp.stack([dwup_f[:DECAY_LORA], dwup_b[:DECAY_LORA]])
    g["rwkv_a_up"] = jnp.stack([daup_f[DECAY_LORA:DECAY_LORA + AAA_LORA], daup_b[DECAY_LORA:DECAY_LORA + AAA_LORA]])
    g["rwkv_g_up"] = dgup[DECAY_LORA + AAA_LORA:]
    dps = jnp.concatenate([dr_f + dr_b + dr_m, dks, dv_f + dv_b + dv_m, dlora], axis=-1)
    dp_rw, g["rwkv_shift_mu"] = token_shift_bwd(dps, px_rw, wt["rwkv_shift_mu"], seg_first, seg_last)
    dpx = jnp.concatenate([(dqkv[0][k] + dqkv[1][k]).astype(MXU_DTYPE) for k in range(3)]
                          + [dg_ret.astype(MXU_DTYPE), dp_rw], axis=-1)
    dn1 = matmul(flat(dpx), wt["w_in"], "nt", "proj_in_dx").reshape(bsz, t_all, D_MODEL)
    g["w_in"] = matmul(flat(n1), flat(dpx), "tn", "proj_in_dw", MXU_DTYPE)
    dh, dshift1, dscale1, g["norm1_g"] = ew_backward(fn_norm_mod, "norm1_bwd", bsz, n_t, norm1_ins(), [Tiled(dn1)],
                                                     [True] * 4)
    grad_x = dh[:, t_c:] + dx_res
    zeros = jnp.zeros((D_MODEL,), f32)
    g["mod_x"] = jnp.stack([dshift1[:, 1, 0], dscale1[:, 1, 0], dgate1[:, 0, 0], dshift2[:, 0, 0], dscale2[:, 0, 0],
                            dgate2[:, 0, 0]], axis=1)
    g["mod_ctx"] = jnp.stack([dshift1[:, 0, 0].sum(0), dscale1[:, 0, 0].sum(0), zeros, zeros, zeros, zeros])
    return loss, grad_x, g


MESH_ID = pl.DeviceIdType.MESH
ALL_PEERS = [(dx, dy, dc) for dx in (0, 1) for dy in (0, 1) for dc in (0, 1)][1:]
CHIP_PEERS = [(1, 0, 0), (0, 1, 0), (1, 1, 0)]
CHIP_SLOTS = (0, 2, 4, 6)


def _mesh_pos():
    return lax.axis_index("x"), lax.axis_index("y"), lax.axis_index("c")


def _device_slot():
    x, y, c = _mesh_pos()
    return 4 * x + 2 * y + c


def sibling_swap(arrs, name):
    n = len(arrs)

    def body(*refs):
        in_refs, out_refs = refs[:n], refs[n:2 * n]
        send_sems, recv_sems = refs[2 * n:]
        x, y, c = _mesh_pos()
        copies = [pltpu.make_async_remote_copy(src_ref=in_refs[a], dst_ref=out_refs[a], send_sem=send_sems.at[a],
                                               recv_sem=recv_sems.at[a], device_id=(x, y, 1 - c),
                                               device_id_type=MESH_ID) for a in range(n)]
        for cp in copies:
            cp.start()
        for cp in copies:
            cp.wait()

    any_spec = pl.BlockSpec(memory_space=pl.ANY)
    res = pl.pallas_call(
        body, in_specs=[any_spec] * n, out_specs=[any_spec] * n,
        out_shape=[jax.ShapeDtypeStruct(a.shape, a.dtype) for a in arrs],
        scratch_shapes=[pltpu.SemaphoreType.DMA((n,)), pltpu.SemaphoreType.DMA((n,))],
        name=name)(*arrs)
    return list(res)


def exchange(arrs, gather, peers, name, by_chip=False, own=True):
    n, n_peers = len(arrs), len(peers)
    n_slots = N_SHARDS if by_chip else N_DEV
    slot = (lambda x, y, c: 2 * x + y) if by_chip else (lambda x, y, c: 4 * x + 2 * y + c)

    def body(*refs):
        in_refs, out_refs = refs[:n], refs[n:2 * n]
        send_sems, recv_sems, local_sems = refs[2 * n:]
        x, y, c = _mesh_pos()
        me = slot(x, y, c)
        copies, locals_ = [], []
        for a in range(n):
            if own:
                mine = in_refs[a] if gather else in_refs[a].at[me]
                loc = pltpu.make_async_copy(mine, out_refs[a].at[me], local_sems.at[a])
                loc.start()
                locals_.append(loc)
            for k, (dx, dy, dc) in enumerate(peers):
                peer = (1 - x if dx else x, 1 - y if dy else y, 1 - c if dc else c)
                src = in_refs[a] if gather else in_refs[a].at[slot(*peer)]
                sem = a * n_peers + k
                cp = pltpu.make_async_remote_copy(src_ref=src, dst_ref=out_refs[a].at[me], send_sem=send_sems.at[sem],
                                                  recv_sem=recv_sems.at[sem], device_id=peer, device_id_type=MESH_ID)
                cp.start()
                copies.append(cp)
        for cp in copies:
            cp.wait()
        for loc in locals_:
            loc.wait()

    any_spec = pl.BlockSpec(memory_space=pl.ANY)
    out_shape = [jax.ShapeDtypeStruct((n_slots,) + (a.shape if gather else a.shape[1:]), a.dtype) for a in arrs]
    n_sems = n * n_peers
    res = pl.pallas_call(
        body, in_specs=[any_spec] * n, out_specs=[any_spec] * n, out_shape=out_shape,
        scratch_shapes=[pltpu.SemaphoreType.DMA((n_sems,)), pltpu.SemaphoreType.DMA((n_sems,)),
                        pltpu.SemaphoreType.DMA((n,))],
        name=name)(*arrs)
    return list(res)


HBM_SPEC = pl.BlockSpec(memory_space=pltpu.HBM)
SEM_SPEC = pl.BlockSpec(memory_space=pltpu.SEMAPHORE)
DATAFLOW = pltpu.SideEffectType.DATAFLOW_SIDE_EFFECTING


def _peer_copies(src_refs, land_refs, send_sems, recv_sems, gather):
    x, y, c = _mesh_pos()
    me = 4 * x + 2 * y + c
    copies = []
    for a, (src_ref, land_ref) in enumerate(zip(src_refs, land_refs)):
        for k, (dx, dy, dc) in enumerate(ALL_PEERS):
            peer = (1 - x if dx else x, 1 - y if dy else y, 1 - c if dc else c)
            src = src_ref if gather else src_ref.at[4 * peer[0] + 2 * peer[1] + peer[2]]
            sem = a * len(ALL_PEERS) + k
            copies.append(pltpu.make_async_remote_copy(src_ref=src, dst_ref=land_ref.at[me], send_sem=send_sems.at[sem],
                                                       recv_sem=recv_sems.at[sem], device_id=peer,
                                                       device_id_type=MESH_ID))
    return copies


def exchange_start(arrs, gather, name):
    n = len(arrs)
    lands = [lax.empty((N_DEV,) + (a.shape if gather else a.shape[1:]), a.dtype) for a in arrs]

    def body(*refs):
        for cp in _peer_copies(refs[:n], refs[n:2 * n], refs[2 * n], refs[2 * n + 1], gather):
            cp.start()
        refs[-1][...] = jnp.zeros_like(refs[-1])

    sems = pltpu.SemaphoreType.DMA((n * len(ALL_PEERS),))
    hbm = [pltpu.HBM(a.shape, a.dtype) for a in arrs + lands]
    res = pl.pallas_call(
        body, name=name, out_shape=(sems, sems, *hbm, jax.ShapeDtypeStruct((8, 128), f32)),
        in_specs=[HBM_SPEC] * (2 * n),
        out_specs=(SEM_SPEC, SEM_SPEC, *[HBM_SPEC] * (2 * n), pl.BlockSpec(memory_space=pltpu.VMEM)),
        input_output_aliases={i: 2 + i for i in range(2 * n)},
        compiler_params=pltpu.CompilerParams(has_side_effects=DATAFLOW))(
        *[pltpu.with_memory_space_constraint(a, pltpu.HBM) for a in arrs + lands])
    return res[0], res[1], list(res[2:2 + n]), list(res[2 + n:2 + 2 * n]), res[-1]


def exchange_wait(started, after, gather, name):
    send_sems, recv_sems, srcs, lands, _ = started
    n = len(srcs)

    def body(*refs):
        for cp in _peer_copies(refs[:n], refs[n:2 * n], refs[2 * n], refs[2 * n + 1], gather):
            cp.wait_send()
            cp.wait_recv()

    res = pl.pallas_call(
        body, name=name, out_shape=tuple(pltpu.HBM(a.shape, a.dtype) for a in srcs + lands),
        in_specs=[HBM_SPEC] * (2 * n) + [SEM_SPEC, SEM_SPEC, pl.BlockSpec(memory_space=pl.ANY)],
        out_specs=tuple([HBM_SPEC] * (2 * n)), input_output_aliases={i: i for i in range(2 * n)},
        compiler_params=pltpu.CompilerParams(has_side_effects=DATAFLOW))(*srcs, *lands, send_sems, recv_sems, after)
    return list(res[n:])


def add_arrays(parts, name, out_dtype=f32):
    r, c = parts[0].shape
    tr = r
    for cand in (512, 256, 128, 64, 32, 16):
        if r % cand == 0:
            tr = cand
            break

    def body(*refs):
        acc = refs[0][...].astype(f32)
        for p_ref in refs[1:-1]:
            acc = acc + p_ref[...].astype(f32)
        refs[-1][...] = acc.astype(out_dtype)

    spec = pl.BlockSpec((tr, c), lambda i: (i, 0))
    return pl.pallas_call(body, grid=(r // tr,), in_specs=[spec] * len(parts), out_specs=spec,
                          out_shape=jax.ShapeDtypeStruct((r, c), out_dtype),
                          compiler_params=_cparams(("parallel",)), name=name)(*parts)


def gather_two_level(arrs, name):
    n = len(arrs)
    per = 7

    def body(*refs):
        in_refs, out_refs = refs[:n], refs[n:2 * n]
        send_sems, recv_sems = refs[2 * n:]
        x, y, c = _mesh_pos()
        me, sibling = (x, y, c), (x, y, 1 - c)
        chips = [(1 - x, y), (x, 1 - y), (1 - x, 1 - y)]

        def copy(a, k, block, to, src=None):
            rows = out_refs[a].at[4 * block[0] + 2 * block[1] + block[2]]
            return pltpu.make_async_remote_copy(src_ref=rows if src is None else src, dst_ref=rows,
                                                send_sem=send_sems.at[a * per + k], recv_sem=recv_sems.at[a * per + k],
                                                device_id=to, device_id_type=MESH_ID)

        first, passed = [], []
        for a in range(n):
            first.append(copy(a, 0, me, sibling, src=in_refs[a]))
            first += [copy(a, 1 + j, me, (*chip, c), src=in_refs[a]) for j, chip in enumerate(chips)]
        for cp in first:
            cp.start()
        for a in range(n):
            for j, chip in enumerate(chips):
                copy(a, 1 + j, (*chip, c), me).wait_recv()
                fwd = copy(a, 4 + j, (*chip, c), sibling)
                fwd.start()
                passed.append(fwd)
        for a in range(n):
            copy(a, 0, sibling, me).wait_recv()
            for j, chip in enumerate(chips):
                copy(a, 4 + j, (*chip, 1 - c), me).wait_recv()
        for cp in first + passed:
            cp.wait_send()

    any_spec = pl.BlockSpec(memory_space=pl.ANY)
    res = pl.pallas_call(
        body, in_specs=[any_spec] * n, out_specs=[any_spec] * n,
        out_shape=[jax.ShapeDtypeStruct((N_DEV,) + a.shape, a.dtype) for a in arrs],
        scratch_shapes=[pltpu.SemaphoreType.DMA((n * per,)), pltpu.SemaphoreType.DMA((n * per,))],
        name=name)(*arrs)
    return list(res)


def sum_slots(parts, slots, name):
    _, r, c = parts.shape
    tr = r
    for cand in (512, 256, 128, 64, 32, 16, 8):
        if r % cand == 0 and cand * c * 4 * len(slots) <= 8 * 1024 * 1024:
            tr = cand
            break

    def body(p_ref, o_ref):
        acc = p_ref[slots[0]].astype(f32)
        for s in slots[1:]:
            acc = acc + p_ref[s].astype(f32)
        o_ref[...] = acc

    return pl.pallas_call(body, grid=(r // tr,), in_specs=[pl.BlockSpec((parts.shape[0], tr, c), lambda i: (0, i, 0))],
                          out_specs=pl.BlockSpec((tr, c), lambda i: (i, 0)),
                          out_shape=jax.ShapeDtypeStruct((r, c), f32),
                          compiler_params=_cparams(("parallel",)), name=name)(parts)


def column_sum(a, name):
    def body(a_ref, o_ref):
        o_ref[...] = jnp.sum(a_ref[...], axis=0, keepdims=True)

    return pl.pallas_call(body, out_shape=jax.ShapeDtypeStruct((1, a.shape[1]), f32), name=name)(a)


def adamw(w, g, m, v, name):
    r, c = w.shape
    tr = r
    for cand in (256, 128, 64, 32, 16, 8):
        if r % cand == 0:
            tr = cand
            break

    def body(w_ref, g_ref, m_ref, v_ref, d_ref, mo_ref, vo_ref):
        gv = g_ref[...]
        m_new = ADAM_B1 * m_ref[...] + (1.0 - ADAM_B1) * gv
        v_new = ADAM_B2 * v_ref[...] + (1.0 - ADAM_B2) * jnp.square(gv)
        m_hat = m_new / (1.0 - ADAM_B1 ** ADAM_STEP)
        v_hat = v_new / (1.0 - ADAM_B2 ** ADAM_STEP)
        d_ref[...] = -ADAM_LR * (m_hat / (jnp.sqrt(v_hat) + ADAM_EPS) + ADAM_WD * w_ref[...])
        mo_ref[...] = m_new
        vo_ref[...] = v_new

    spec = pl.BlockSpec((tr, c), lambda i: (i, 0))
    return pl.pallas_call(body, grid=(r // tr,), in_specs=[spec] * 4, out_specs=[spec] * 3,
                          out_shape=[jax.ShapeDtypeStruct((r, c), f32)] * 3,
                          compiler_params=_cparams(("parallel",)), name=name)(w, g, m, v)


def adaln_fwd(c_rows, w, b):
    def body(c_ref, w_ref, b_ref, o_ref):
        cv = c_ref[...]
        o_ref[...] = _mxu_dot(cv * jax.nn.sigmoid(cv), w_ref[...]) + b_ref[...]

    return pl.pallas_call(body, out_shape=jax.ShapeDtypeStruct((c_rows.shape[0], w.shape[1]), f32),
                          compiler_params=pltpu.CompilerParams(vmem_limit_bytes=VMEM_LIMIT), name="adaln_fwd")(c_rows, w, b)


def adaln_bwd(c_rows, dm, w):
    def body(c_ref, dm_ref, w_ref, gw_ref, ds_ref):
        cv = c_ref[...]
        gw_ref[...] = _dg(cv * jax.nn.sigmoid(cv), dm_ref[...], 0, 0)
        ds_ref[...] = _dg(dm_ref[...], w_ref[...], 1, 1)

    return pl.pallas_call(body, out_shape=[jax.ShapeDtypeStruct(w.shape, f32),
                                           jax.ShapeDtypeStruct(c_rows.shape, f32)],
                          compiler_params=pltpu.CompilerParams(vmem_limit_bytes=VMEM_LIMIT), name="adaln_bwd")(c_rows, dm, w)


def c_ctx_grad(parts, c_ctx_row):
    def body(p_ref, c_ref, o_ref):
        total = p_ref[0, 0:1, :]
        for s in range(1, N_SHARDS):
            total = total + p_ref[s, 0:1, :]
        _, vjp = jax.vjp(jax.nn.silu, c_ref[...])
        o_ref[...] = vjp(total)[0]

    return pl.pallas_call(body, out_shape=jax.ShapeDtypeStruct((1, D_MODEL), f32), name="c_ctx_grad")(parts, c_ctx_row)


PACK_W = 1024
PACK_ROWS = 8


def _pack(arrs):
    pieces, layout, r0 = [], [], 0
    for a in arrs:
        size = math.prod(a.shape)
        rows = -(-size // (PACK_W * PACK_ROWS)) * PACK_ROWS
        pieces.append(jnp.pad(a.reshape(-1).astype(f32), (0, rows * PACK_W - size)).reshape(rows, PACK_W))
        layout.append((r0, rows, a.shape))
        r0 += rows
    return jnp.concatenate(pieces, axis=0), layout


def _unpack(pack, layout, lead=()):
    n_lead = len(lead)
    outs = []
    for r0, rows, shape in layout:
        piece = pack[(slice(None),) * n_lead + (slice(r0, r0 + rows),)].reshape(lead + (-1,))
        outs.append(piece[..., :math.prod(shape)].reshape(lead + tuple(shape)))
    return outs


W_NAMES = ("c_ctx", "w_ada", "b_ada", "norm1_g", "norm2_g", "w_in", "ret_log_decay", "rwkv_shift_mu", "rwkv_w0",
           "rwkv_w_up", "rwkv_a0", "rwkv_a_up", "rwkv_g_up", "rwkv_k_k", "rwkv_k_a", "rwkv_r_k", "rwkv_ln_w",
           "rwkv_ln_b", "w_out", "w_ff1", "b_ff1", "w_ff2", "b_ff2", "final_g")
COL_SHARDED = ("w_in", "w_ff1")
ROW_SHARDED = ("w_out", "w_ff2")
LAST_SHARDED = ("rwkv_shift_mu", "rwkv_w0", "rwkv_w_up", "rwkv_a0", "rwkv_a_up", "rwkv_g_up")
REPLICATED = ("c_ctx", "b_ada", "norm1_g", "norm2_g", "ret_log_decay", "rwkv_k_k", "rwkv_k_a", "rwkv_r_k",
              "rwkv_ln_w", "rwkv_ln_b", "b_ff1", "b_ff2", "final_g")
N_SHARDS = 4


def _train_step(a):
    x, c, ctx, tgt = a["x"], a["c"], a["ctx"], a["loss_target"]
    bsz = x.shape[0]
    mx, my, mc = _mesh_pos()
    shard = 2 * mx + my
    dev = _device_slot()

    (c_all,) = exchange([jnp.pad(c, ((0, PACK_ROWS - bsz), (0, 0)))], True, ALL_PEERS, "gather_c")
    n_ex = N_DEV * bsz
    c_rows = jnp.concatenate([c_all[:, :bsz].reshape(n_ex, D_MODEL), a["c_ctx"][None, :],
                              jnp.zeros((PACK_ROWS - 1, D_MODEL), f32)], axis=0)
    ada_cols = a["w_ada"].shape[-1]
    b_ada_cols = lax.dynamic_slice_in_dim(a["b_ada"], shard * ada_cols, ada_cols, axis=1)
    mod_cols = adaln_fwd(c_rows, a["w_ada"][0], b_ada_cols)

    def own_half(n):
        w = a[n][0].astype(MXU_DTYPE)
        return lax.dynamic_slice_in_dim(w, mc * (w.shape[0] // 2), w.shape[0] // 2, axis=0)

    def whole_weight(n, gth, own):
        per_chip = lax.dynamic_update_index_in_dim(gth, own, dev, 0).reshape(N_SHARDS, -1, gth.shape[-1])
        return (per_chip.transpose(1, 0, 2).reshape(per_chip.shape[1], -1) if n in COL_SHARDED
                else per_chip.reshape(-1, per_chip.shape[-1]))

    small_pack, small_layout = _pack([a[n][0] for n in LAST_SHARDED])
    own_blocks = [mod_cols, own_half("w_in"), small_pack]
    gathered = gather_two_level(own_blocks, "gather_weights")
    late_own = [own_half(n) for n in LATE_WEIGHTS]
    late_started = exchange_start(late_own, True, "gather_late_start")
    mod_own = lax.dynamic_update_index_in_dim(gathered[0], mod_cols, dev, 0)
    mod_all = jnp.stack([mod_own[s] for s in CHIP_SLOTS], axis=1).reshape(c_rows.shape[0], -1)
    mod_all = mod_all + late_started[-1][0, 0]
    mod_x = lax.dynamic_slice_in_dim(mod_all, dev * bsz, bsz, axis=0).reshape(bsz, 6, D_MODEL)
    mod_ctx = mod_all[n_ex].reshape(6, D_MODEL)
    wt = {"w_in": whole_weight("w_in", gathered[1], own_blocks[1])}

    def late_weights(after):
        lands = exchange_wait(late_started, after, True, "gather_late_wait")
        return {n: whole_weight(n, land, own) for n, land, own in zip(LATE_WEIGHTS, lands, late_own)}

    def grad_blocks(n, gw):
        if n in COL_SHARDED:
            gw = gw.reshape(gw.shape[0], N_SHARDS, -1).transpose(1, 0, 2)
        return gw.reshape(N_DEV, -1, gw.shape[-1]).astype(MXU_DTYPE)

    late_sent = {}

    def early_grads(late_g):
        late_sent["blocks"] = [grad_blocks(n, late_g[n]) for n in LATE_WEIGHTS]
        late_sent["started"] = exchange_start(late_sent["blocks"], False, "scatter_late_start")
        return late_sent["started"][-1]

    small_own = lax.dynamic_update_index_in_dim(gathered[2], small_pack, dev, 0)
    small_by_chip = _unpack(jnp.stack([small_own[s] for s in CHIP_SLOTS]), small_layout, (N_SHARDS,))
    for n, parts in zip(LAST_SHARDED, small_by_chip):
        wt[n] = jnp.concatenate([parts[s] for s in range(N_SHARDS)], axis=-1)
    for n in ("norm1_g", "norm2_g", "rwkv_k_k", "rwkv_k_a", "rwkv_r_k", "rwkv_ln_w", "rwkv_ln_b", "b_ff1", "b_ff2"):
        wt[n] = a[n]
    wt["ret_log_decay"] = a["ret_log_decay"][0]
    wt["final_g"] = a["final_g"][None, :]

    loss, grad_x, g = layer_step(x, ctx, tgt, mod_x, mod_ctx, wt, late_weights, early_grads)

    small_names = [n for n in REPLICATED if n not in ("c_ctx", "b_ada")]
    g_pack, g_layout = _pack([jnp.pad(loss, ((0, 0), (0, PACK_W - loss.shape[1])))] + [g[n] for n in small_names]
                             + [g["mod_x"], g["mod_ctx"]])
    (g_packs,) = gather_two_level([g_pack], "gather_small_grads")
    g_packs = lax.dynamic_update_index_in_dim(g_packs, g_pack, dev, 0)
    g_sum = _unpack(sum_slots(g_packs, tuple(range(N_DEV)), "sum_small_grads"), g_layout)
    loss_total = g_sum[0][0, 0]
    grads = dict(zip(small_names, g_sum[1:1 + len(small_names)]))
    dmod_ctx = g_sum[-1].reshape(1, -1)
    dmod_x = _unpack(g_packs, g_layout, (N_DEV,))[-2].reshape(n_ex, -1)
    dmod = jnp.concatenate([dmod_x, dmod_ctx, jnp.zeros((PACK_ROWS - 1, dmod_x.shape[1]), f32)], axis=0)
    grads["b_ada"] = column_sum(dmod, "b_ada_grad")
    dmod_cols = lax.dynamic_slice_in_dim(dmod, shard * ada_cols, ada_cols, axis=1)
    grads["w_ada"], dsilu = adaln_bwd(c_rows, dmod_cols, a["w_ada"][0])

    blocks = [grad_blocks("w_in", g["w_in"])]
    shard_packs = []
    for s in range(N_SHARDS):
        pieces_s = [lax.slice_in_dim(g[n], s * a[n].shape[-1], (s + 1) * a[n].shape[-1], axis=g[n].ndim - 1)
                    for n in LAST_SHARDED]
        pack_s, shard_layout = _pack(pieces_s)
        shard_packs.append(jnp.pad(pack_s, ((0, -pack_s.shape[0] % (2 * PACK_ROWS)), (0, 0))))
    blocks.append(jnp.stack(shard_packs).reshape(N_DEV, -1, PACK_W))
    scattered = ("w_in", "small_shards")
    halves_of = lambda blk, core: lax.dynamic_index_in_dim(
        blk.reshape(N_SHARDS, 2, *blk.shape[1:]), core, axis=1, keepdims=False).reshape(-1, blk.shape[-1])
    from_sibling = sibling_swap([halves_of(blk, 1 - mc) for blk in blocks], "prereduce_swap")
    chip_sums = [add_arrays([halves_of(blk, mc), got], f"prereduce_{n}", blk.dtype).reshape(N_SHARDS, -1, blk.shape[-1])
                 for n, blk, got in zip(scattered, blocks, from_sibling)]
    dsilu_rows = jnp.broadcast_to(jnp.pad(dsilu[n_ex:n_ex + 1], ((0, PACK_ROWS - 1), (0, 0)))[None],
                                  (N_SHARDS, PACK_ROWS, D_MODEL))
    to_chips = [dsilu_rows] + chip_sums
    received = exchange(to_chips, False, CHIP_PEERS, "scatter_big_grads", by_chip=True, own=False)
    received = [lax.dynamic_update_index_in_dim(got, lax.dynamic_index_in_dim(sent, shard, 0, keepdims=False), shard, 0)
                for got, sent in zip(received, to_chips)]
    grads["c_ctx"] = c_ctx_grad(received[0], a["c_ctx"][None, :])
    half_sums = [sum_slots(p, tuple(range(N_SHARDS)), f"sum_{n}") for n, p in zip(scattered, received[1:])]
    late_lands = exchange_wait(late_sent["started"], half_sums[0], False, "scatter_late_wait")
    for n, land, sent in zip(LATE_WEIGHTS, late_lands, late_sent["blocks"]):
        land = lax.dynamic_update_index_in_dim(land, lax.dynamic_index_in_dim(sent, dev, 0, keepdims=False), dev, 0)
        half_sums.append(sum_slots(land, tuple(range(N_DEV)), f"sum_{n}"))
    scattered = scattered + LATE_WEIGHTS
    other_halves = sibling_swap(half_sums, "swap_halves")
    for n, mine, other in zip(scattered, half_sums, other_halves):
        rows = mine.shape[0]
        whole = jnp.zeros((2 * rows, mine.shape[1]), f32)
        whole = lax.dynamic_update_slice_in_dim(whole, mine, mc * rows, axis=0)
        grads[n] = lax.dynamic_update_slice_in_dim(whole, other, (1 - mc) * rows, axis=0)
    grads.update(zip(LAST_SHARDED, _unpack(grads.pop("small_shards"), shard_layout)))

    out_g, out_d, out_m, out_v = {}, {}, {}, {}
    for n in ("w_ada",) + COL_SHARDED + ROW_SHARDED:
        out_g[n] = grads[n].reshape(a[n].shape)
        two_d = lambda z: z.reshape(-1, z.shape[-1])
        d, m, v = adamw(two_d(a[n]), two_d(out_g[n]), two_d(a["m_" + n]), two_d(a["v_" + n]), f"adamw_{n}")
        out_d[n], out_m[n], out_v[n] = d.reshape(a[n].shape), m.reshape(a[n].shape), v.reshape(a[n].shape)
    rest = REPLICATED + LAST_SHARDED
    for n in rest:
        out_g[n] = grads[n].reshape(a[n].shape)
    packs = [_pack([src[n] for n in rest])[0] for src in
             ({n: a[n] for n in rest}, out_g, {n: a["m_" + n] for n in rest}, {n: a["v_" + n] for n in rest})]
    _, rest_layout = _pack([a[n] for n in rest])
    for dst, pack in zip((out_d, out_m, out_v), adamw(*packs, "adamw_small")):
        dst.update(zip(rest, _unpack(pack, rest_layout)))
    return (loss_total, grad_x, *[out_g[n] for n in W_NAMES], *[out_d[n] for n in W_NAMES],
            *[out_m[n] for n in W_NAMES], *[out_v[n] for n in W_NAMES])


def kernel(x, c, ctx, c_ctx, w_ada, b_ada, norm1_g, norm2_g, w_in, ret_log_decay, rwkv_shift_mu, rwkv_w0, rwkv_w_up, rwkv_a0, rwkv_a_up, rwkv_g_up, rwkv_k_k, rwkv_k_a, rwkv_r_k, rwkv_ln_w, rwkv_ln_b, w_out, w_ff1, b_ff1, w_ff2, b_ff2, final_g, loss_target, m_c_ctx, m_w_ada, m_b_ada, m_norm1_g, m_norm2_g, m_w_in, m_ret_log_decay, m_rwkv_shift_mu, m_rwkv_w0, m_rwkv_w_up, m_rwkv_a0, m_rwkv_a_up, m_rwkv_g_up, m_rwkv_k_k, m_rwkv_k_a, m_rwkv_r_k, m_rwkv_ln_w, m_rwkv_ln_b, m_w_out, m_w_ff1, m_b_ff1, m_w_ff2, m_b_ff2, m_final_g, v_c_ctx, v_w_ada, v_b_ada, v_norm1_g, v_norm2_g, v_w_in, v_ret_log_decay, v_rwkv_shift_mu, v_rwkv_w0, v_rwkv_w_up, v_rwkv_a0, v_rwkv_a_up, v_rwkv_g_up, v_rwkv_k_k, v_rwkv_k_a, v_rwkv_r_k, v_rwkv_ln_w, v_rwkv_ln_b, v_w_out, v_w_ff1, v_b_ff1, v_w_ff2, v_b_ff2, v_final_g):
    return _train_step(dict(locals()))
```

```python
import functools
import math

import jax
import jax.numpy as jnp
from jax import lax
from jax.experimental import pallas as pl
from jax.experimental.pallas import tpu as pltpu

f32 = jnp.float32
MXU_DTYPE = jnp.bfloat16

D_MODEL = 1024
RET_W = 512
RET_HEADS = 4
RET_DH = 128
RET_CHUNK = 128
RW_W = 512
RW_N = 64
DECAY_LORA = 64
AAA_LORA = 64
GATE_LORA = 128
LORA_W = DECAY_LORA + AAA_LORA + GATE_LORA
D_FF = 4096
RET_COLS = 4 * RET_W
SHIFT_COLS = 3 * RW_W + LORA_W
IN_COLS = RET_COLS + SHIFT_COLS
GRID_W = 64
ROPE_BASE = 10000.0
NORM_EPS = 1e-6
GN_EPS = 64e-5
W_DECAY_SCALE = math.exp(-0.5)
ADAM_LR, ADAM_B1, ADAM_B2, ADAM_EPS, ADAM_WD, ADAM_STEP = 0.001, 0.9, 0.999, 1e-08, 0.01, 10

TOK_TILE = 256
MATMUL_TILE = 1024
SCAN_CHUNK = 32
SCAN_UNROLL = SCAN_CHUNK
N_DEV = 8
V7X_VMEM_BYTES = 64 * 1024 * 1024
VMEM_LIMIT = V7X_VMEM_BYTES * 7 // 8


def _cparams(sem):
    return pltpu.CompilerParams(dimension_semantics=sem, vmem_limit_bytes=VMEM_LIMIT)


def _tile(n, cap):
    best = None
    for t in range(128, min(n, cap) + 1, 128):
        if n % t == 0:
            best = t
    return best if best is not None else n


def matmul(a, b, mode, name, out_dtype=f32, bias=None, finish=None):
    if mode == "nn":
        (m, k), (k2, n) = a.shape, b.shape
    elif mode == "nt":
        (m, k), (n, k2) = a.shape, b.shape
    else:
        (k, m), (k2, n) = a.shape, b.shape
    assert k == k2, (a.shape, b.shape, mode)
    tm, tn, tk = _tile(m, MATMUL_TILE), _tile(n, MATMUL_TILE), _tile(k, MATMUL_TILE)
    nk = k // tk
    dims = {"nn": ((1,), (0,)), "nt": ((1,), (1,)), "tn": ((0,), (0,))}[mode]

    def body(a_ref, b_ref, *rest):
        o_ref, acc_ref = rest[-2:]
        kk = pl.program_id(2)

        @pl.when(kk == 0)
        def _():
            acc_ref[...] = jnp.zeros_like(acc_ref)

        acc_ref[...] += lax.dot_general(a_ref[...].astype(MXU_DTYPE), b_ref[...].astype(MXU_DTYPE),
                                        (dims, ((), ())), preferred_element_type=f32)

        @pl.when(kk == nk - 1)
        def _():
            res = acc_ref[...]
            if bias is not None:
                res = res + rest[0][...]
            if finish is not None:
                res = finish(res)
            o_ref[...] = res.astype(o_ref.dtype)

    if mode == "nn":
        a_spec = pl.BlockSpec((tm, tk), lambda i, j, q: (i, q))
        b_spec = pl.BlockSpec((tk, tn), lambda i, j, q: (q, j))
    elif mode == "nt":
        a_spec = pl.BlockSpec((tm, tk), lambda i, j, q: (i, q))
        b_spec = pl.BlockSpec((tn, tk), lambda i, j, q: (j, q))
    else:
        a_spec = pl.BlockSpec((tk, tm), lambda i, j, q: (q, i))
        b_spec = pl.BlockSpec((tk, tn), lambda i, j, q: (q, j))
    extra_specs = [] if bias is None else [pl.BlockSpec((1, tn), lambda i, j, q: (0, j))]
    extra = [] if bias is None else [bias]
    return pl.pallas_call(
        body, grid=(m // tm, n // tn, nk), in_specs=[a_spec, b_spec] + extra_specs,
        out_specs=pl.BlockSpec((tm, tn), lambda i, j, q: (i, j)),
        out_shape=jax.ShapeDtypeStruct((m, n), out_dtype),
        scratch_shapes=[pltpu.VMEM((tm, tn), f32)],
        compiler_params=_cparams(("parallel", "parallel", "arbitrary")), name=name)(a, b, *extra)


class Tiled:
    def __init__(self, arr, w=None, cidx=0, toff=0):
        self.arr, self.w, self.cidx, self.toff = arr, (arr.shape[-1] if w is None else w), cidx, toff

    def spec(self):
        cidx, toff = self.cidx, self.toff
        return pl.BlockSpec((None, TOK_TILE, self.w), lambda b, i: (b, jnp.maximum(i + toff, 0), cidx))


class Seg:
    def __init__(self, arr, seg, first):
        self.arr, self.seg, self.first = arr, seg, first

    def spec(self):
        seg = self.seg
        return pl.BlockSpec((None, None, 1, self.arr.shape[-1]), lambda b, i: (b, seg(i), 0, 0))


class Glob:
    def __init__(self, arr):
        self.arr = arr

    def spec(self):
        return pl.BlockSpec(self.arr.shape, lambda b, i: (0,) * self.arr.ndim)


def ew_forward(fn, name, bsz, n_tiles, ins, outs):
    n_in = len(ins)

    def body(*refs):
        res = fn(*[r[...] for r in refs[:n_in]])
        for o_ref, o in zip(refs[n_in:], res):
            o_ref[...] = o.astype(o_ref.dtype)

    out_specs = [pl.BlockSpec((None, TOK_TILE, w), lambda b, i: (b, i, 0)) for w, _ in outs]
    out_shape = [jax.ShapeDtypeStruct((bsz, n_tiles * TOK_TILE, w), dt) for w, dt in outs]
    return pl.pallas_call(body, grid=(bsz, n_tiles), in_specs=[d.spec() for d in ins], out_specs=out_specs,
                          out_shape=out_shape, compiler_params=_cparams(("parallel", "parallel")), name=name)(
        *[d.arr for d in ins])


def ew_backward(fn, name, bsz, n_tiles, ins, cts, want, grad_dtypes=None, lead=0):
    n_in, n_ct = len(ins), len(cts)
    diff = [k for k in range(n_in) if want[k]]
    grad_dtypes = grad_dtypes or {}
    assert lead == 0 or not any(isinstance(ins[k], Seg) for k in diff)

    def body(*refs):
        b, i = pl.program_id(0), pl.program_id(1)
        g_refs = refs[n_in + n_ct:]

        def tile_grads():
            vals = [r[...] for r in refs[:n_in]]
            ct_vals = tuple(r[...].astype(f32) for r in refs[n_in:n_in + n_ct])

            def f(*dvals):
                full = list(vals)
                for k, v in zip(diff, dvals):
                    full[k] = v
                return tuple(fn(*full))

            _, vjp = jax.vjp(f, *[vals[k] for k in diff])
            grads = vjp(ct_vals)
            for k, g_ref, g in zip(diff, g_refs, grads):
                d = ins[k]
                if isinstance(d, Tiled):
                    g_ref[...] = g.astype(g_ref.dtype)
                else:
                    zero = d.first(i) if isinstance(d, Seg) else jnp.logical_and(b == 0, i == lead)

                    @pl.when(zero)
                    def _(g_ref=g_ref):
                        g_ref[...] = jnp.zeros_like(g_ref)

                    g_ref[...] += g

        if lead == 0:
            tile_grads()
        else:
            pl.when(i >= lead)(tile_grads)

            @pl.when(i < lead)
            def _():
                for k, g_ref in zip(diff, g_refs):
                    if isinstance(ins[k], Tiled):
                        g_ref[...] = jnp.zeros_like(g_ref)

    out_specs, out_shape = [], []
    for k in diff:
        d = ins[k]
        if isinstance(d, Tiled):
            out_specs.append(pl.BlockSpec((None, TOK_TILE, d.w), lambda b, i: (b, i, 0)))
            out_shape.append(jax.ShapeDtypeStruct((bsz, (n_tiles + lead) * TOK_TILE, d.w), grad_dtypes.get(k, f32)))
        else:
            out_specs.append(d.spec())
            out_shape.append(jax.ShapeDtypeStruct(d.arr.shape, f32))
    return pl.pallas_call(body, grid=(bsz, n_tiles + lead),
                          in_specs=[d.spec() for d in ins] + [c.spec() for c in cts],
                          out_specs=out_specs, out_shape=out_shape,
                          compiler_params=_cparams(("arbitrary", "arbitrary")), name=name)(
        *[d.arr for d in ins], *[c.arr for c in cts])


@jax.custom_vjp
def _mxu_dot(a, b):
    return jnp.dot(a.astype(MXU_DTYPE), b.astype(MXU_DTYPE), preferred_element_type=f32)


def _mxu_dot_fwd(a, b):
    return _mxu_dot(a, b), (a, b)


def _mxu_dot_bwd(res, ct):
    a, b = res
    ct = ct.astype(MXU_DTYPE)
    da = lax.dot_general(ct, b.astype(MXU_DTYPE), (((1,), (1,)), ((), ())), preferred_element_type=f32)
    db = lax.dot_general(a.astype(MXU_DTYPE), ct, (((0,), (0,)), ((), ())), preferred_element_type=f32)
    return da, db


_mxu_dot.defvjp(_mxu_dot_fwd, _mxu_dot_bwd)


def _split_dot_impl(x, ones_mat):
    hi = x.astype(MXU_DTYPE)
    lo = (x - hi.astype(f32)).astype(MXU_DTYPE)
    return jnp.dot(hi, ones_mat, preferred_element_type=f32) + jnp.dot(lo, ones_mat, preferred_element_type=f32)


@jax.custom_vjp
def _split_dot(x, ones_mat):
    return _split_dot_impl(x, ones_mat)


def _split_dot_fwd(x, ones_mat):
    return _split_dot_impl(x, ones_mat), ones_mat


def _split_dot_bwd(ones_mat, ct):
    return _split_dot_impl(ct, ones_mat), None


_split_dot.defvjp(_split_dot_fwd, _split_dot_bwd)


def _block_ones(n, group):
    idx = jnp.arange(n) // group
    return (idx[:, None] == idx[None, :]).astype(MXU_DTYPE)


def _rms(x, g):
    return x * lax.rsqrt(jnp.mean(x * x, axis=-1, keepdims=True) + NORM_EPS) * g


def fn_norm_mod(h, shift, scale, g):
    return (_rms(h, g) * (1.0 + scale) + shift,)


def fn_rwkv_prepare(ks, lora, w0_f, w0_b, a0_f, a0_b, w_up_f, w_up_b, a_up_f, a_up_b, g_up, k_k, k_a, ones64):
    kkr = ks * k_k
    kk = kkr * lax.rsqrt(_split_dot(kkr * kkr, ones64) + 1e-12)
    outs = [kk]
    th = jnp.tanh(lora)
    for w0, a0, w_up, a_up in ((w0_f, a0_f, w_up_f, a_up_f), (w0_b, a0_b, w_up_b, a_up_b)):
        w = jnp.exp(-W_DECAY_SCALE * jax.nn.sigmoid(w0 + _mxu_dot(th, w_up)))
        a = jax.nn.sigmoid(a0 + _mxu_dot(lora, a_up))
        kt = ks * (1.0 + (a - 1.0) * k_a)
        outs += [w, a * kk, kt]
    outs.append(_mxu_dot(jax.nn.sigmoid(lora), g_up))
    return tuple(outs)


def fn_merge(o_f, o_b, g_ret, y_f, y_b, r, kt_f, v, g_rw, r_k, ln_w, ln_b, ones64, ones128):
    o = o_f + o_b
    ret = o * lax.rsqrt(_split_dot(o * o, ones128) * (1.0 / RET_DH) + NORM_EPS) * (g_ret * jax.nn.sigmoid(g_ret))
    y = y_f + y_b
    mean = _split_dot(y, ones64) * (1.0 / RW_N)
    yc = y - mean
    var = _split_dot(yc * yc, ones64) * (1.0 / RW_N)
    y_n = yc * lax.rsqrt(var + GN_EPS) * ln_w + ln_b
    bonus = _split_dot(r * kt_f * r_k, ones64) * v
    return ret, (y_n + bonus) * g_rw


def fn_resid_norm_mod(x, mix, gate, shift, scale, g):
    h1 = x + gate * mix
    return h1, _rms(h1, g) * (1.0 + scale) + shift


def relu2(z):
    return jnp.square(jnp.maximum(z, 0.0))


def relu2_backward(act, dact, name):
    bsz, n_tok, width = act.shape

    def body(a_ref, d_ref, du_ref, db_ref):
        du = d_ref[...].astype(f32) * (2.0 * jnp.sqrt(a_ref[...].astype(f32)))
        du_ref[...] = du.astype(du_ref.dtype)

        @pl.when(jnp.logical_and(pl.program_id(0) == 0, pl.program_id(1) == 0))
        def _():
            db_ref[...] = jnp.zeros_like(db_ref)

        db_ref[...] += jnp.sum(du, axis=0, keepdims=True)

    tile = pl.BlockSpec((None, TOK_TILE, width), lambda b, i: (b, i, 0))
    row = pl.BlockSpec((1, width), lambda b, i: (0, 0))
    return pl.pallas_call(body, grid=(bsz, n_tok // TOK_TILE), in_specs=[tile, tile], out_specs=[tile, row],
                          out_shape=[jax.ShapeDtypeStruct(act.shape, MXU_DTYPE), jax.ShapeDtypeStruct((1, width), f32)],
                          compiler_params=_cparams(("arbitrary", "arbitrary")), name=name)(act, dact)


def fn_loss(h1, f, tgt, gate, b2, g):
    y = _rms(h1 + gate * (f + b2), g)
    err = jnp.square(y - tgt)
    return 0.5 * jnp.sum(jnp.mean(err, axis=-1, keepdims=True), axis=0, keepdims=True)


def loss_and_grads(h1, f, tgt, gate, b2, g, bsz, n_tiles):
    def body(h1_ref, f_ref, t_ref, gate_ref, b2_ref, g_ref, loss_ref, dh1_ref, df_ref, dgate_ref, db2_ref, dg_ref):
        b, i = pl.program_id(0), pl.program_id(1)
        tgt_v = t_ref[...]
        loss, vjp = jax.vjp(lambda a, c, e, p, q: fn_loss(a, c, tgt_v, e, p, q),
                            h1_ref[...], f_ref[...], gate_ref[...], b2_ref[...], g_ref[...])
        dh1, df, dgate, db2, dg = vjp(jnp.ones((1, 1), f32))
        dh1_ref[...] = dh1
        df_ref[...] = df.astype(df_ref.dtype)

        @pl.when(i == 0)
        def _():
            dgate_ref[...] = jnp.zeros_like(dgate_ref)

        @pl.when(jnp.logical_and(b == 0, i == 0))
        def _():
            loss_ref[...] = jnp.zeros_like(loss_ref)
            db2_ref[...] = jnp.zeros_like(db2_ref)
            dg_ref[...] = jnp.zeros_like(dg_ref)

        dgate_ref[...] += dgate
        db2_ref[...] += db2
        dg_ref[...] += dg
        loss_ref[...] += jnp.broadcast_to(loss, loss_ref.shape)

    tile = pl.BlockSpec((None, TOK_TILE, D_MODEL), lambda b, i: (b, i, 0))
    row = pl.BlockSpec((1, D_MODEL), lambda b, i: (0, 0))
    seg = pl.BlockSpec((None, None, 1, D_MODEL), lambda b, i: (b, 0, 0, 0))
    t_tok = n_tiles * TOK_TILE
    return pl.pallas_call(
        body, grid=(bsz, n_tiles), in_specs=[tile, tile, tile, seg, row, row],
        out_specs=[pl.BlockSpec((1, 128), lambda b, i: (0, 0)), tile, tile, seg, row, row],
        out_shape=[jax.ShapeDtypeStruct((1, 128), f32), jax.ShapeDtypeStruct((bsz, t_tok, D_MODEL), f32),
                   jax.ShapeDtypeStruct((bsz, t_tok, D_MODEL), MXU_DTYPE),
                   jax.ShapeDtypeStruct((bsz, 1, 1, D_MODEL), f32),
                   jax.ShapeDtypeStruct((1, D_MODEL), f32), jax.ShapeDtypeStruct((1, D_MODEL), f32)],
        compiler_params=_cparams(("arbitrary", "arbitrary")), name="loss_and_grads")(h1, f, tgt, gate, b2, g)


SHIFT_BLOCK = SHIFT_COLS
HALO_ROWS = 8


def _shift_specs(n_tok, col0):
    per_tile = TOK_TILE // HALO_ROWS
    last = n_tok // HALO_ROWS - 1
    tile = pl.BlockSpec((None, TOK_TILE, SHIFT_BLOCK), lambda j, b, i: (b, i, col0 + j))
    prev = pl.BlockSpec((None, HALO_ROWS, SHIFT_BLOCK),
                        lambda j, b, i: (b, jnp.maximum(i * per_tile - 1, 0), col0 + j))
    nxt = pl.BlockSpec((None, HALO_ROWS, SHIFT_BLOCK),
                       lambda j, b, i: (b, jnp.minimum((i + 1) * per_tile, last), col0 + j))
    return tile, prev, nxt


def _shifted(p, prev_ref, next_ref, is_first, is_last):
    row = lax.broadcasted_iota(jnp.int32, p.shape, 0)
    prev_row = jnp.where(is_first, 0.0, prev_ref[HALO_ROWS - 1:HALO_ROWS, :].astype(f32))
    next_row = jnp.where(is_last, 0.0, next_ref[0:1, :].astype(f32))
    prev = jnp.where(row == 0, prev_row, pltpu.roll(p, 1, axis=0))
    nxt = jnp.where(row == TOK_TILE - 1, next_row, pltpu.roll(p, TOK_TILE - 1, axis=0))
    return prev, nxt


def token_shift(px, mu, seg_first, seg_last):
    bsz, n_tok, _ = px.shape
    n_tiles = n_tok // TOK_TILE

    def body(p_ref, prev_ref, next_ref, mu_ref, o_ref):
        i = pl.program_id(2)
        p = p_ref[...]
        prev, nxt = _shifted(p, prev_ref, next_ref, seg_first(i), seg_last(i))
        o_ref[...] = p + mu_ref[0:1, :] * (prev - p) + mu_ref[1:2, :] * (nxt - p)

    tile, prev, nxt = _shift_specs(n_tok, 0)
    return pl.pallas_call(
        body, grid=(SHIFT_COLS // SHIFT_BLOCK, bsz, n_tiles),
        in_specs=[tile, prev, nxt, pl.BlockSpec((2, SHIFT_BLOCK), lambda j, b, i: (0, j))],
        out_specs=pl.BlockSpec((None, TOK_TILE, SHIFT_BLOCK), lambda j, b, i: (b, i, j)),
        out_shape=jax.ShapeDtypeStruct((bsz, n_tok, SHIFT_COLS), f32),
        compiler_params=_cparams(("parallel", "parallel", "parallel")), name="token_shift")(px, px, px, mu)


def token_shift_bwd(dps, px, mu, seg_first, seg_last):
    bsz, n_tok, _ = px.shape
    n_tiles = n_tok // TOK_TILE

    def body(d_ref, dprev_ref, dnext_ref, p_ref, prev_ref, next_ref, mu_ref, dp_ref, dmu_ref):
        b, i = pl.program_id(1), pl.program_id(2)
        first, last = seg_first(i), seg_last(i)
        d, p = d_ref[...], p_ref[...]
        d_prev, d_next = _shifted(d, dprev_ref, dnext_ref, first, last)
        p_prev, p_next = _shifted(p, prev_ref, next_ref, first, last)
        mu0, mu1 = mu_ref[0:1, :], mu_ref[1:2, :]
        dp_ref[...] = (d + mu0 * (d_next - d) + mu1 * (d_prev - d)).astype(dp_ref.dtype)

        @pl.when(jnp.logical_and(b == 0, i == 0))
        def _():
            dmu_ref[...] = jnp.zeros_like(dmu_ref)

        dmu_ref[0:1, :] += jnp.sum(d * (p_prev - p), axis=0, keepdims=True)
        dmu_ref[1:2, :] += jnp.sum(d * (p_next - p), axis=0, keepdims=True)

    dtile, dprev, dnext = _shift_specs(n_tok, 0)
    tile, prev, nxt = _shift_specs(n_tok, 0)
    mu_spec = pl.BlockSpec((2, SHIFT_BLOCK), lambda j, b, i: (0, j))
    return pl.pallas_call(
        body, grid=(SHIFT_COLS // SHIFT_BLOCK, bsz, n_tiles),
        in_specs=[dtile, dprev, dnext, tile, prev, nxt, mu_spec],
        out_specs=[pl.BlockSpec((None, TOK_TILE, SHIFT_BLOCK), lambda j, b, i: (b, i, j)), mu_spec],
        out_shape=[jax.ShapeDtypeStruct((bsz, n_tok, SHIFT_COLS), MXU_DTYPE),
                   jax.ShapeDtypeStruct((2, SHIFT_COLS), f32)],
        compiler_params=_cparams(("arbitrary", "arbitrary", "arbitrary")), name="token_shift_bwd")(
        dps, dps, dps, px, px, px, mu)


def _dg(a, b, ca, cb):
    return lax.dot_general(a.astype(MXU_DTYPE), b.astype(MXU_DTYPE), (((ca,), (cb,)), ((), ())),
                           preferred_element_type=f32)


@jax.custom_vjp
def _mm_nt(a, b):
    return _dg(a, b, 1, 1)


_mm_nt.defvjp(lambda a, b: (_dg(a, b, 1, 1), (a, b)),
              lambda res, ct: (_dg(ct, res[1], 1, 0), _dg(ct, res[0], 0, 0)))


@jax.custom_vjp
def _mm_tn(a, b):
    return _dg(a, b, 0, 0)


_mm_tn.defvjp(lambda a, b: (_dg(a, b, 0, 0), (a, b)),
              lambda res, ct: (_dg(res[1], ct, 1, 1), _dg(res[0], ct, 1, 0)))


ROTARY_PAIR = RET_DH // 4


def _swap_pairs_impl(t):
    lane = lax.broadcasted_iota(jnp.int32, t.shape, 1)
    return jnp.where(lane % (2 * ROTARY_PAIR) < ROTARY_PAIR, pltpu.roll(t, RET_DH - ROTARY_PAIR, axis=1),
                     pltpu.roll(t, ROTARY_PAIR, axis=1))


@jax.custom_vjp
def _swap_pairs(t):
    return _swap_pairs_impl(t)


_swap_pairs.defvjp(lambda t: (_swap_pairs_impl(t), None), lambda _, ct: (_swap_pairs_impl(ct),))


def _ret_chunk(state, q_raw, k_raw, v, cos, sin, ld_row, head, reverse):
    c = RET_CHUNK
    lane = lax.broadcasted_iota(jnp.int32, ld_row.shape, 1)
    lg = -jnp.exp(jnp.sum(jnp.where(lane == head, ld_row, 0.0), axis=-1, keepdims=True))
    rot = lambda t: t * cos + _swap_pairs(t) * sin
    q = rot(q_raw)
    k = rot(k_raw) * (RET_DH ** -0.5)
    ti = lax.broadcasted_iota(jnp.int32, (c, 1), 0).astype(f32)
    tj = lax.broadcasted_iota(jnp.int32, (1, c), 1).astype(f32)
    if not reverse:
        dist, mask, q_exp, k_exp = ti - tj, (ti - tj) >= 0, ti + 1.0, c - 1.0 - ti
    else:
        dist, mask, q_exp, k_exp = tj - ti, (tj - ti) > 0, c - ti, ti
    decay = jnp.where(mask, jnp.exp(lg * jnp.maximum(dist, 0.0)), 0.0)
    scores = _mm_nt(q, k) * decay
    out = _mxu_dot(scores, v) + _mxu_dot(q * jnp.exp(lg * q_exp), state)
    new_state = state * jnp.exp(lg * c) + _mm_tn(k * jnp.exp(lg * k_exp), v)
    return out, new_state


def _ret_specs(bsz, order):
    tok = lambda col=0: pl.BlockSpec((bsz, RET_CHUNK, RET_W), lambda i: (0, order(i), col))
    tab = pl.BlockSpec((RET_CHUNK, RET_DH), lambda i: (order(i), 0))
    ld = pl.BlockSpec((1, RET_DH), lambda i: (0, 0))
    return tok, tab, ld


def retention_fwd(px, cos, sin, ld_row, order, reverse, name):
    bsz, n_tok, _ = px.shape
    n_ch = n_tok // RET_CHUNK

    def body(q_ref, k_ref, v_ref, cos_ref, sin_ref, ld_ref, o_ref, sv_ref, st_ref):
        @pl.when(pl.program_id(0) == 0)
        def _():
            st_ref[...] = jnp.zeros_like(st_ref)

        for b in range(bsz):
            for h in range(RET_HEADS):
                sl = slice(h * RET_DH, (h + 1) * RET_DH)
                s = st_ref[b, h]
                sv_ref[b, h] = s
                o, s_new = _ret_chunk(s, q_ref[b, :, sl], k_ref[b, :, sl], v_ref[b, :, sl], cos_ref[...], sin_ref[...],
                                      ld_ref[...], h, reverse)
                o_ref[b, :, sl] = o
                st_ref[b, h] = s_new

    tok, tab, ld = _ret_specs(bsz, order)
    return pl.pallas_call(
        body, grid=(n_ch,), in_specs=[tok(0), tok(1), tok(2), tab, tab, ld],
        out_specs=[tok(), pl.BlockSpec((bsz, None, RET_HEADS, RET_DH, RET_DH), lambda i: (0, i, 0, 0, 0))],
        out_shape=[jax.ShapeDtypeStruct((bsz, n_tok, RET_W), f32),
                   jax.ShapeDtypeStruct((bsz, n_ch, RET_HEADS, RET_DH, RET_DH), f32)],
        scratch_shapes=[pltpu.VMEM((bsz, RET_HEADS, RET_DH, RET_DH), f32)],
        compiler_params=_cparams(("arbitrary",)), name=name)(px, px, px, cos, sin, ld_row)


def retention_bwd(do, px, states, cos, sin, ld_row, order, reverse, name):
    bsz, n_tok, _ = px.shape
    n_ch = n_tok // RET_CHUNK
    back = lambda i: order(n_ch - 1 - i)

    def body(do_ref, q_ref, k_ref, v_ref, sv_ref, cos_ref, sin_ref, ld_ref,
             dq_ref, dk_ref, dv_ref, dld_ref, dst_ref):
        @pl.when(pl.program_id(0) == 0)
        def _():
            dst_ref[...] = jnp.zeros_like(dst_ref)
            dld_ref[...] = jnp.zeros_like(dld_ref)

        cos_v, sin_v = cos_ref[...], sin_ref[...]
        for b in range(bsz):
            for h in range(RET_HEADS):
                sl = slice(h * RET_DH, (h + 1) * RET_DH)
                f = lambda s, q, k, v, ld, h=h: _ret_chunk(s, q, k, v, cos_v, sin_v, ld, h, reverse)
                _, vjp = jax.vjp(f, sv_ref[b, h], q_ref[b, :, sl], k_ref[b, :, sl], v_ref[b, :, sl], ld_ref[...])
                ds, dq, dk, dv, dld = vjp((do_ref[b, :, sl], dst_ref[b, h]))
                dst_ref[b, h] = ds
                dq_ref[b, :, sl] = dq
                dk_ref[b, :, sl] = dk
                dv_ref[b, :, sl] = dv
                dld_ref[...] += dld

    tok, tab, ld = _ret_specs(bsz, back)
    return pl.pallas_call(
        body, grid=(n_ch,),
        in_specs=[tok(), tok(0), tok(1), tok(2),
                  pl.BlockSpec((bsz, None, RET_HEADS, RET_DH, RET_DH), lambda i: (0, n_ch - 1 - i, 0, 0, 0)),
                  tab, tab, ld],
        out_specs=[tok(), tok(), tok(), ld],
        out_shape=[jax.ShapeDtypeStruct((bsz, n_tok, RET_W), f32)] * 3 + [jax.ShapeDtypeStruct((1, RET_DH), f32)],
        scratch_shapes=[pltpu.VMEM((bsz, RET_HEADS, RET_DH, RET_DH), f32)],
        compiler_params=_cparams(("arbitrary",)), name=name)(
        do, px, px, px, states, cos, sin, ld_row)


HALF_W = RW_W // 2


def _head_sum(x, ones):
    xm = x.astype(MXU_DTYPE)
    return jnp.concatenate([jnp.dot(xm[:, :HALF_W], ones, preferred_element_type=f32),
                            jnp.dot(xm[:, HALF_W:], ones, preferred_element_type=f32)], axis=1)


def _stack(parts):
    return jnp.concatenate(parts, axis=0)


def _row(ref, b, t):
    return ref[b, pl.ds(t, 1), :]


SCAN_DIRS = ((False, True), (True, False))
RW_HEADS = RW_W // RW_N
HEAD_ROWS_PAD = 16


def _head_rows(row, mask):
    return jnp.broadcast_to(row, mask.shape) * mask


def _outer(per_value, row, mask_pad):
    return lax.dot_general(per_value.astype(MXU_DTYPE), _head_rows(row, mask_pad).astype(MXU_DTYPE),
                           (((0,), (0,)), ((), ())), preferred_element_type=f32)


def _read(states, rows, mask, more_rows=()):
    lhs = _stack([_head_rows(r, mask) for r in list(rows) + list(more_rows)])
    return lax.dot_general(lhs.astype(MXU_DTYPE), _stack(states).astype(MXU_DTYPE), (((1,), (1,)), ((), ())),
                           preferred_element_type=f32)


def _own_block(raw, b):
    lanes = raw[:, RW_N * b:RW_N * (b + 1)]
    turned = _stack([lanes[RW_HEADS * b:], lanes[:RW_HEADS * b]]) if b else lanes
    if turned.shape[0] < HEAD_ROWS_PAD:
        turned = _stack([turned, jnp.zeros((HEAD_ROWS_PAD - turned.shape[0], RW_N), f32)])
    return turned[:HEAD_ROWS_PAD]


def _row_from_heads(per_value, state, mask_pad):
    full = jnp.dot(per_value.astype(MXU_DTYPE), state.astype(MXU_DTYPE), preferred_element_type=f32)
    return jnp.sum(full * mask_pad, axis=0, keepdims=True)


def _scan_specs(bsz, order):
    rows = lambda col=0: pl.BlockSpec((bsz, SCAN_CHUNK, RW_W), lambda i: (0, order(i), col))
    per_value = pl.BlockSpec((bsz, SCAN_CHUNK, HEAD_ROWS_PAD, RW_N), lambda i: (0, order(i), 0, 0))
    states = pl.BlockSpec((SCAN_CHUNK, bsz, RW_N, RW_W), lambda i: (order(i), 0, 0, 0))
    blocks = pl.BlockSpec((SCAN_CHUNK, RW_HEADS * bsz, RW_N * bsz), lambda i: (order(i), 0, 0))
    return rows, per_value, states, blocks


def _mxu_operands(states):
    return [s.astype(MXU_DTYPE) for s in states]


def _removed(states_m, kk_t, ones, bsz):
    removed = _head_sum(_stack([states_m[b] * kk_t[b].astype(MXU_DTYPE) for b in range(bsz)]), ones)
    return [removed[b * RW_N:(b + 1) * RW_N] for b in range(bsz)]


def _advance(sp, rem, w_t, b_t, vk, bsz):
    return [sp[b] * w_t[b] - rem[b] * b_t[b] + vk[b] for b in range(bsz)]


def heads_to_rows(a):
    b, t, _ = a.shape
    return jnp.pad(a.astype(MXU_DTYPE).reshape(b, t, RW_HEADS, RW_N),
                   ((0, 0), (0, 0), (0, HEAD_ROWS_PAD - RW_HEADS), (0, 0)))


def _blocks_to_rows(raw_ref, first, row_ref, bsz):
    steps = pl.ds(first, SCAN_CHUNK)
    for b in range(bsz):
        for h in range(RW_HEADS):
            row_ref[b, :, h * RW_N:(h + 1) * RW_N] = raw_ref[steps, RW_HEADS * b + h, RW_N * b:RW_N * (b + 1)]


N_ROWS_FWD = 5
N_ROWS_BWD = 5


def _scan_consts(bsz):
    head = (jnp.arange(RW_W)[None, :] // RW_N == jnp.arange(RW_HEADS)[:, None]).astype(f32)
    return head, jnp.pad(head, ((0, HEAD_ROWS_PAD - RW_HEADS), (0, 0))), _block_ones(HALF_W, RW_N)


def _const_specs(consts):
    return [pl.BlockSpec(c.shape, lambda i: (0, 0)) for c in consts]


def rwkv_scan_fwd(rows_in, v_heads, orders, name):
    bsz, n_tok, _ = rows_in[0][0][0].shape
    n_ch = n_tok // SCAN_CHUNK
    rng = range(bsz)
    consts = _scan_consts(bsz)

    def body(*refs):
        rows = [refs[:N_ROWS_FWD], refs[N_ROWS_FWD:2 * N_ROWS_FWD]]
        (v0, v1, head_ref, pad_ref, ones_ref, y0, y1, h0, h1, f0, f1, m0, m1, s0, s1, late_ref,
         raw_ref) = refs[2 * N_ROWS_FWD:]
        v_refs, y_refs, hist_refs, final_refs, s_refs = (v0, v1), (y0, y1), (h0, h1), (f0, f1), (s0, s1)
        removed_refs = (m0, m1)
        n_blk = RW_HEADS * bsz
        head_v, pad_v, ones_v = head_ref[...], pad_ref[...], ones_ref[...]
        for d in range(2):
            @pl.when(pl.program_id(0) == 0)
            def _(d=d):
                s_refs[d][...] = jnp.zeros_like(s_refs[d])

        def step(j, carry):
            ts = [SCAN_CHUNK - 1 - j if reverse else j for reverse, _ in SCAN_DIRS]
            sps = [[s_refs[d][b] for b in rng] for d in range(2)]
            sps_m = [_mxu_operands(sps[d]) for d in range(2)]
            vks = [[_outer(v_refs[d][b, ts[d]], _row(rows[d][4], b, ts[d]), pad_v) for b in rng] for d in range(2)]
            rems = [_removed(sps_m[d], [_row(rows[d][1], b, ts[d]) for b in rng], ones_v, bsz) for d in range(2)]
            for d, (reverse, inclusive) in enumerate(SCAN_DIRS):
                r_ref = rows[d][0]
                read_at = jnp.maximum(j - 1, 0) if inclusive else ts[d]
                both = _read(sps_m[d], [_row(r_ref, b, read_at) for b in rng], head_v,
                             [_row(rows[d][1], b, ts[d]) for b in rng])
                if inclusive:
                    late_ref[j] = both[:n_blk]
                else:
                    raw_ref[ts[d]] = both[:n_blk]
                removed_refs[d][ts[d]] = both[n_blk:]
            for d in range(2):
                new = _advance(sps[d], rems[d], [_row(rows[d][2], b, ts[d]) for b in rng],
                               [_row(rows[d][3], b, ts[d]) for b in rng], vks[d], bsz)
                for b in rng:
                    hist_refs[d][ts[d], b] = sps_m[d][b]
                    s_refs[d][b] = new[b]
            return carry

        lax.fori_loop(0, SCAN_CHUNK, step, 0, unroll=SCAN_UNROLL)
        for d, (reverse, inclusive) in enumerate(SCAN_DIRS):
            final_refs[d][...] = s_refs[d][...]
            if inclusive:
                assert not reverse
                last = SCAN_CHUNK - 1
                late_ref[SCAN_CHUNK] = _read(_mxu_operands([s_refs[d][b] for b in rng]),
                                             [rows[d][0][b, last:last + 1, :] for b in rng], head_v)
                _blocks_to_rows(late_ref, 1, y_refs[d], bsz)
            else:
                _blocks_to_rows(raw_ref, 0, y_refs[d], bsz)

    specs = [_scan_specs(bsz, orders[d]) for d in range(2)]
    state = pltpu.VMEM((bsz, RW_N, RW_W), f32)
    late = pltpu.VMEM((SCAN_CHUNK + 1, RW_HEADS * bsz, RW_N * bsz), f32)
    raw = pltpu.VMEM((SCAN_CHUNK, RW_HEADS * bsz, RW_N * bsz), f32)
    final_spec = pl.BlockSpec((bsz, RW_N, RW_W), lambda i: (0, 0, 0))
    return pl.pallas_call(
        body, grid=(n_ch,),
        in_specs=[specs[d][0](col) for d in range(2) for _, col in rows_in[d]] + [specs[0][1], specs[1][1]]
        + _const_specs(consts),
        out_specs=[specs[0][0](), specs[1][0](), specs[0][2], specs[1][2], final_spec, final_spec,
                   specs[0][3], specs[1][3]],
        out_shape=[jax.ShapeDtypeStruct((bsz, n_tok, RW_W), f32)] * 2
        + [jax.ShapeDtypeStruct((n_tok, bsz, RW_N, RW_W), MXU_DTYPE)] * 2
        + [jax.ShapeDtypeStruct((bsz, RW_N, RW_W), f32)] * 2
        + [jax.ShapeDtypeStruct((n_tok, RW_HEADS * bsz, RW_N * bsz), f32)] * 2,
        scratch_shapes=[state, state, late, raw],
        compiler_params=_cparams(("arbitrary",)), name=name)(
        *[a for d in range(2) for a, _ in rows_in[d]], v_heads, v_heads, *consts)


def rwkv_scan_bwd(rows_in, v_heads, dy_heads, hists, finals, removed, orders, name):
    bsz, n_tok, _ = rows_in[0][0][0].shape
    n_ch = n_tok // SCAN_CHUNK
    backs = [functools.partial(lambda i, order: order(n_ch - 1 - i), order=orders[d]) for d in range(2)]
    rng = range(bsz)
    consts = _scan_consts(bsz)
    n_out, n_scr = 6, 6

    def body(*refs):
        rows = [refs[:N_ROWS_BWD], refs[N_ROWS_BWD:2 * N_ROWS_BWD]]
        rest = refs[2 * N_ROWS_BWD:]
        v_refs, dy_refs, hist_refs, final_refs, removed_refs = rest[0:2], rest[2:4], rest[4:6], rest[6:8], rest[8:10]
        head_ref, pad_ref, ones_ref = rest[10:13]
        outs = [rest[13:13 + n_out], rest[13 + n_out:13 + 2 * n_out]]
        scr = [rest[13 + 2 * n_out:13 + 2 * n_out + n_scr], rest[13 + 2 * n_out + n_scr:]]
        n_blk = RW_HEADS * bsz
        head_v, pad_v, ones_v = head_ref[...], pad_ref[...], ones_ref[...]
        for d in range(2):
            @pl.when(pl.program_id(0) == 0)
            def _(d=d):
                scr[d][1][...] = jnp.zeros_like(scr[d][1])
                scr[d][0][...] = final_refs[d][...]

        def step_of(j, reverse):
            return j if reverse else SCAN_CHUNK - 1 - j

        for d, (reverse, _) in enumerate(SCAN_DIRS):
            t0 = step_of(0, reverse)
            for b in rng:
                scr[d][3][b] = _outer(dy_refs[d][b, t0], rows[d][0][b, t0:t0 + 1, :], pad_v)

        def bstep(j, carry):
            ts = [step_of(j, reverse) for reverse, _ in SCAN_DIRS]
            reads = [[scr[d][3][b] for b in rng] for d in range(2)]
            dss = []
            for d, (_, inclusive) in enumerate(SCAN_DIRS):
                ds = [scr[d][1][b] for b in rng]
                dss.append([ds[b] + reads[d][b] for b in rng] if inclusive else ds)
            dss_m = [_mxu_operands(dss[d]) for d in range(2)]
            nexts = []
            for d, (reverse, _) in enumerate(SCAN_DIRS):
                t_next = step_of(jnp.minimum(j + 1, SCAN_CHUNK - 1), reverse)
                nexts.append([_outer(dy_refs[d][b, t_next], _row(rows[d][0], b, t_next), pad_v) for b in rng])
            drems = [_removed(dss_m[d], [-_row(rows[d][3], b, ts[d]) for b in rng], ones_v, bsz) for d in range(2)]
            for d in range(2):
                for b in rng:
                    scr[d][3][b] = nexts[d][b]
                both = _read(dss_m[d], [_row(rows[d][4], b, ts[d]) for b in rng], head_v,
                             [-_row(rows[d][3], b, ts[d]) for b in rng])
                scr[d][4][ts[d]] = both[:n_blk]
                scr[d][5][ts[d]] = both[n_blk:]
            for d, (_, inclusive) in enumerate(SCAN_DIRS):
                _, kk_ref, w_ref, _, _ = rows[d]
                _, ds_ref, dsh_ref = scr[d][:3]
                for b in rng:
                    dsh_ref[ts[d], b] = dss[d][b]
                    dsp = dss[d][b] * _row(w_ref, b, ts[d]) + drems[d][b] * _row(kk_ref, b, ts[d])
                    ds_ref[b] = dsp if inclusive else dsp + reads[d][b]
            return carry

        lax.fori_loop(0, SCAN_CHUNK, bstep, 0, unroll=SCAN_UNROLL)

        rsum = lambda z: jnp.sum(z, axis=0, keepdims=True)
        for d, (reverse, inclusive) in enumerate(SCAN_DIRS):
            dr_ref, dkk_ref, dw_ref, db_ref, dkt_ref, dv_ref = outs[d]
            after_ref, _, dsh_ref, _, dv_raw_ref, dremt_ref = scr[d]
            hist_ref, removed_ref = hist_refs[d], removed_refs[d]
            _blocks_to_rows(dv_raw_ref, 0, dv_ref, bsz)
            for t in range(SCAN_CHUNK):
                ts = slice(t, t + 1)
                after = t - 1 if reverse else t + 1
                for b in rng:
                    sp_m, ds = hist_ref[t, b], dsh_ref[t, b]
                    sp = sp_m.astype(f32)
                    if not inclusive:
                        seen = sp_m
                    else:
                        seen = hist_ref[after, b] if 0 <= after < SCAN_CHUNK else after_ref[b]
                    dr_ref[b, ts, :] = _row_from_heads(dy_refs[d][b, t], seen, pad_v)
                    dkt_ref[b, ts, :] = _row_from_heads(v_refs[d][b, t], ds, pad_v)
                    dw_ref[b, ts, :] = rsum(ds * sp)
                    db_ref[b, ts, :] = -_row_from_heads(_own_block(removed_ref[t], b), ds, pad_v)
                    dkk_ref[b, ts, :] = _row_from_heads(_own_block(dremt_ref[t], b), sp_m, pad_v)
            if inclusive:
                first = SCAN_CHUNK - 1 if reverse else 0
                for b in rng:
                    after_ref[b] = hist_ref[first, b].astype(f32)

    specs = [_scan_specs(bsz, backs[d]) for d in range(2)]
    hist = pltpu.VMEM((SCAN_CHUNK, bsz, RW_N, RW_W), f32)
    state = pltpu.VMEM((bsz, RW_N, RW_W), f32)
    final_spec = pl.BlockSpec((bsz, RW_N, RW_W), lambda i: (0, 0, 0))
    raw = pltpu.VMEM((SCAN_CHUNK, RW_HEADS * bsz, RW_N * bsz), f32)
    return pl.pallas_call(
        body, grid=(n_ch,),
        in_specs=[specs[d][0](col) for d in range(2) for _, col in rows_in[d]]
        + [specs[0][1], specs[1][1]] * 2 + [specs[0][2], specs[1][2], final_spec, final_spec, specs[0][3], specs[1][3]]
        + _const_specs(consts),
        out_specs=[specs[d][0]() for d in range(2) for _ in range(n_out)],
        out_shape=[jax.ShapeDtypeStruct((bsz, n_tok, RW_W), f32)] * (2 * n_out),
        scratch_shapes=[state, state, hist, state, raw, raw] * 2,
        compiler_params=_cparams(("arbitrary",)), name=name)(
        *[a for d in range(2) for a, _ in rows_in[d]], v_heads, v_heads, dy_heads, dy_heads, *hists, *finals, *removed, *consts)


MOD_NAMES = ("shift1", "scale1", "gate1", "shift2", "scale2", "gate2")


def _rope_tables(t_ctx, t_x):
    quarter = RET_DH // 4
    pos = jnp.arange(t_x)
    inv = jnp.power(ROPE_BASE, -jnp.arange(0, 2 * quarter, 2, dtype=f32) / (2 * quarter))
    ang_r = (pos // GRID_W).astype(f32)[:, None] * inv[None, :]
    ang_c = (pos % GRID_W).astype(f32)[:, None] * inv[None, :]
    cos = jnp.concatenate([jnp.cos(ang_r)] * 2 + [jnp.cos(ang_c)] * 2, axis=1)
    sin = jnp.concatenate([-jnp.sin(ang_r), jnp.sin(ang_r), -jnp.sin(ang_c), jnp.sin(ang_c)], axis=1)
    cos = jnp.concatenate([jnp.ones((t_ctx, RET_DH), f32), cos], axis=0)
    sin = jnp.concatenate([jnp.zeros((t_ctx, RET_DH), f32), sin], axis=0)
    return cos, sin


def _pad_rows(w, lo, total):
    return jnp.pad(w, ((lo, total - lo - w.shape[0]), (0, 0)))


LATE_WEIGHTS = ("w_out", "w_ff1", "w_ff2")


def layer_step(x, ctx, tgt, mod_x, mod_ctx, wt, late_weights=None, early_grads=None, last_grads=None):
    bsz, t_x, _ = x.shape
    t_c = ctx.shape[1]
    t_all = t_c + t_x
    n_ct, n_xt = t_c // TOK_TILE, t_x // TOK_TILE
    n_t = n_ct + n_xt
    assert t_c % TOK_TILE == 0 and t_x % TOK_TILE == 0 and t_c % RET_CHUNK == 0

    seg = lambda i: (i >= n_ct).astype(jnp.int32)
    seg_first = lambda i: jnp.logical_or(i == 0, i == n_ct)
    seg_last = lambda i: jnp.logical_or(i == n_ct - 1, i == n_t - 1)
    mod_all = {n: jnp.stack([jnp.broadcast_to(mod_ctx[k], (bsz, D_MODEL)), mod_x[:, k]], axis=1)[:, :, None, :]
               for k, n in enumerate(MOD_NAMES)}
    mod_lat = {n: mod_x[:, k][:, None, None, :] for k, n in enumerate(MOD_NAMES)}
    both = lambda n: Seg(mod_all[n], seg, seg_first)
    lat = lambda n: Seg(mod_lat[n], lambda i: 0, lambda i: i == 0)
    flat = lambda a: a.reshape(-1, a.shape[-1])

    def chunk_orders(n_ctx_chunks, n_chunks):
        fwd = lambda i: i
        bwd = lambda i: jnp.where(i < n_ctx_chunks, n_ctx_chunks - 1 - i, n_chunks + n_ctx_chunks - 1 - i)
        return fwd, bwd

    ones64, ones128 = _block_ones(RW_W, RW_N), _block_ones(RET_W, RET_DH)
    cos, sin = _rope_tables(t_c, t_x)
    ld_rows = [jnp.pad(wt["ret_log_decay"][d][None, :], ((0, 0), (0, RET_DH - RET_HEADS))) for d in range(2)]
    w_up_pad = [_pad_rows(wt["rwkv_w_up"][d], 0, LORA_W) for d in range(2)]
    a_up_pad = [_pad_rows(wt["rwkv_a_up"][d], DECAY_LORA, LORA_W) for d in range(2)]
    g_up_pad = _pad_rows(wt["rwkv_g_up"], DECAY_LORA + AAA_LORA, LORA_W)
    row = lambda a, d: a[d][None, :]

    h = jnp.concatenate([ctx, x], axis=1)
    norm1_ins = lambda: [Tiled(h), both("shift1"), both("scale1"), Glob(wt["norm1_g"])]
    (n1,) = ew_forward(fn_norm_mod, "norm1", bsz, n_t, norm1_ins(), [(D_MODEL, MXU_DTYPE)])
    px = matmul(flat(n1), wt["w_in"], "nn", "proj_in").reshape(bsz, t_all, IN_COLS)
    px_rw = px[..., RET_COLS:]
    ps = token_shift(px_rw, wt["rwkv_shift_mu"], seg_first, seg_last)

    def prep_ins():
        return [Tiled(ps, RW_W, 1), Tiled(ps, LORA_W, 3 * RW_W // LORA_W),
                Glob(row(wt["rwkv_w0"], 0)), Glob(row(wt["rwkv_w0"], 1)),
                Glob(row(wt["rwkv_a0"], 0)), Glob(row(wt["rwkv_a0"], 1)),
                Glob(w_up_pad[0]), Glob(w_up_pad[1]), Glob(a_up_pad[0]), Glob(a_up_pad[1]), Glob(g_up_pad),
                Glob(wt["rwkv_k_k"]), Glob(wt["rwkv_k_a"]), Glob(ones64)]

    kk, w_f, b_f, kt_f, w_b, b_b, kt_b, g_rw = ew_forward(fn_rwkv_prepare, "rwkv_prepare", bsz, n_t, prep_ins(),
                                                           [(RW_W, f32)] * 8)
    rw_order = chunk_orders(t_c // SCAN_CHUNK, t_all // SCAN_CHUNK)
    ret_order = chunk_orders(t_c // RET_CHUNK, t_all // RET_CHUNK)
    scan_rows = [[(ps, 0), (kk, 0), (w_f, 0), (b_f, 0), (kt_f, 0)], [(ps, 0), (kk, 0), (w_b, 0), (b_b, 0), (kt_b, 0)]]
    v_heads = heads_to_rows(ps[..., 2 * RW_W:3 * RW_W])
    y_f, y_b, *kept_states = rwkv_scan_fwd(scan_rows, v_heads, rw_order, "rwkv_scan_fwd")
    y = [y_f, y_b]
    o, ret_states = [], []
    for d in range(2):
        o_d, st_d = retention_fwd(px, cos, sin, ld_rows[d], ret_order[d], SCAN_DIRS[d][0], f"retention_fwd{d}")
        o.append(o_d), ret_states.append(st_d)

    def merge_ins(toff):
        return [Tiled(o[0], toff=toff), Tiled(o[1], toff=toff), Tiled(px, RET_W, 3, toff),
                Tiled(y[0], toff=toff), Tiled(y[1], toff=toff), Tiled(ps, RW_W, 0, toff), Tiled(kt_f, toff=toff),
                Tiled(ps, RW_W, 2, toff), Tiled(g_rw, toff=toff),
                Glob(wt["rwkv_r_k"]), Glob(wt["rwkv_ln_w"]), Glob(wt["rwkv_ln_b"]), Glob(ones64), Glob(ones128)]

    ret_out, rw_out = ew_forward(fn_merge, "merge_heads", bsz, n_xt, merge_ins(n_ct),
                                 [(RET_W, MXU_DTYPE), (RW_W, MXU_DTYPE)])
    merged = jnp.concatenate([ret_out, rw_out], axis=-1)
    if late_weights is not None:
        wt = {**wt, **late_weights(merged)}
    mix = matmul(flat(merged), wt["w_out"], "nn", "proj_out").reshape(bsz, t_x, D_MODEL)
    resid_ins = lambda: [Tiled(x), Tiled(mix), lat("gate1"), lat("shift2"), lat("scale2"), Glob(wt["norm2_g"])]
    h1, n2 = ew_forward(fn_resid_norm_mod, "resid_norm2", bsz, n_xt, resid_ins(), [(D_MODEL, f32), (D_MODEL, MXU_DTYPE)])
    act = matmul(flat(n2), wt["w_ff1"], "nn", "ff1", MXU_DTYPE, wt["b_ff1"], relu2).reshape(bsz, t_x, D_FF)
    ff = matmul(flat(act), wt["w_ff2"], "nn", "ff2").reshape(bsz, t_x, D_MODEL)

    g = {}
    loss, dh1, dff, dgate2, g["b_ff2"], g["final_g"] = loss_and_grads(
        h1, ff, tgt, mod_lat["gate2"], wt["b_ff2"], wt["final_g"], bsz, n_xt)
    dact = matmul(flat(dff), wt["w_ff2"], "nt", "ff2_dx", MXU_DTYPE).reshape(bsz, t_x, D_FF)
    g["w_ff2"] = matmul(flat(act), flat(dff), "tn", "ff2_dw", MXU_DTYPE)
    du, g["b_ff1"] = relu2_backward(act, dact, "relu2_bwd")
    dn2 = matmul(flat(du), wt["w_ff1"], "nt", "ff1_dx").reshape(bsz, t_x, D_MODEL)
    g["w_ff1"] = matmul(flat(n2), flat(du), "tn", "ff1_dw", MXU_DTYPE)
    dx_res, dmix, dgate1, dshift2, dscale2, g["norm2_g"] = ew_backward(
        fn_resid_norm_mod, "resid_norm2_bwd", bsz, n_xt, resid_ins(), [Tiled(dh1), Tiled(dn2)], [True] * 6,
        {1: MXU_DTYPE})
    dmerged = matmul(flat(dmix), wt["w_out"], "nt", "proj_out_dx").reshape(bsz, t_x, D_MODEL)
    g["w_out"] = matmul(flat(merged), flat(dmix), "tn", "proj_out_dw", MXU_DTYPE)
    if early_grads is not None:
        token = early_grads({n: g.pop(n) for n in LATE_WEIGHTS})
        wt = {**wt, "rwkv_r_k": wt["rwkv_r_k"] + token[:1, :1]}
    (do, dg_ret, dy, dr_m, dkt_m, dv_m, dg_rw, g["rwkv_r_k"], g["rwkv_ln_w"], g["rwkv_ln_b"]) = ew_backward(
        fn_merge, "merge_heads_bwd", bsz, n_xt, merge_ins(0),
        [Tiled(dmerged, RET_W, 0, -n_ct), Tiled(dmerged, RW_W, 1, -n_ct)],
        [True, False, True, True, False, True, True, True, True, True, True, True, False, False], lead=n_ct)

    dqkv, dld = [], []
    for d in range(2):
        *dqkv_d, dld_d = retention_bwd(do, px, ret_states[d], cos, sin, ld_rows[d], ret_order[d],
                                       SCAN_DIRS[d][0], f"retention_bwd{d}")
        dqkv.append(dqkv_d), dld.append(dld_d[0, :RET_HEADS])
    g["ret_log_decay"] = jnp.stack(dld)
    (dr_f, dkk_f, dw_f, db_f, dkt_f, dv_f, dr_b, dkk_b, dw_b, db_b, dkt_b, dv_b) = rwkv_scan_bwd(
        scan_rows, v_heads, heads_to_rows(dy), kept_states[:2], kept_states[2:4], kept_states[4:], rw_order, "rwkv_scan_bwd")
    prep_cts = [dkk_f + dkk_b, dw_f, db_f, dkt_f + dkt_m, dw_b, db_b, dkt_b, dg_rw]
    (dks, dlora, dw0_f, dw0_b, da0_f, da0_b, dwup_f, dwup_b, daup_f, daup_b, dgup, g["rwkv_k_k"],
     g["rwkv_k_a"]) = ew_backward(fn_rwkv_prepare, "rwkv_prepare_bwd", bsz, n_t, prep_ins(),
                                  [Tiled(c) for c in prep_cts], [True] * 13 + [False])
    g["rwkv_w0"] = jnp.concatenate([dw0_f, dw0_b], axis=0)
    g["rwkv_a0"] = jnp.concatenate([da0_f, da0_b], axis=0)
    g["rwkv_w_up"] = jnp.stack([dwup_f[:DECAY_LORA], dwup_b[:DECAY_LORA]])
    g["rwkv_a_up"] = jnp.stack([daup_f[DECAY_LORA:DECAY_LORA + AAA_LORA], daup_b[DECAY_LORA:DECAY_LORA + AAA_LORA]])
    g["rwkv_g_up"] = dgup[DECAY_LORA + AAA_LORA:]
    dps = jnp.concatenate([dr_f + dr_b + dr_m, dks, dv_f + dv_b + dv_m, dlora], axis=-1)
    dp_rw, g["rwkv_shift_mu"] = token_shift_bwd(dps, px_rw, wt["rwkv_shift_mu"], seg_first, seg_last)
    dpx = jnp.concatenate([(dqkv[0][k] + dqkv[1][k]).astype(MXU_DTYPE) for k in range(3)]
                          + [dg_ret.astype(MXU_DTYPE), dp_rw], axis=-1)
    dn1 = matmul(flat(dpx), wt["w_in"], "nt", "proj_in_dx").reshape(bsz, t_all, D_MODEL)
    g["w_in"] = matmul(flat(n1), flat(dpx), "tn", "proj_in_dw", MXU_DTYPE)
    if last_grads is not None:
        token = last_grads(g.pop("w_in"), {n: g.pop(n) for n in LAST_SHARDED})
        wt = {**wt, "norm1_g": wt["norm1_g"] + token[:1, :1]}
    dh, dshift1, dscale1, g["norm1_g"] = ew_backward(fn_norm_mod, "norm1_bwd", bsz, n_t, norm1_ins(), [Tiled(dn1)],
                                                     [True] * 4)
    grad_x = dh[:, t_c:] + dx_res
    zeros = jnp.zeros((D_MODEL,), f32)
    g["mod_x"] = jnp.stack([dshift1[:, 1, 0], dscale1[:, 1, 0], dgate1[:, 0, 0], dshift2[:, 0, 0], dscale2[:, 0, 0],
                            dgate2[:, 0, 0]], axis=1)
    g["mod_ctx"] = jnp.stack([dshift1[:, 0, 0].sum(0), dscale1[:, 0, 0].sum(0), zeros, zeros, zeros, zeros])
    return loss, grad_x, g


MESH_ID = pl.DeviceIdType.MESH
ALL_PEERS = [(dx, dy, dc) for dx in (0, 1) for dy in (0, 1) for dc in (0, 1)][1:]
CHIP_PEERS = [(1, 0, 0), (0, 1, 0), (1, 1, 0)]
CHIP_SLOTS = (0, 2, 4, 6)


def _mesh_pos():
    return lax.axis_index("x"), lax.axis_index("y"), lax.axis_index("c")


def _device_slot():
    x, y, c = _mesh_pos()
    return 4 * x + 2 * y + c


def sibling_swap(arrs, name):
    n = len(arrs)

    def body(*refs):
        in_refs, out_refs = refs[:n], refs[n:2 * n]
        send_sems, recv_sems = refs[2 * n:]
        x, y, c = _mesh_pos()
        copies = [pltpu.make_async_remote_copy(src_ref=in_refs[a], dst_ref=out_refs[a], send_sem=send_sems.at[a],
                                               recv_sem=recv_sems.at[a], device_id=(x, y, 1 - c),
                                               device_id_type=MESH_ID) for a in range(n)]
        for cp in copies:
            cp.start()
        for cp in copies:
            cp.wait()

    any_spec = pl.BlockSpec(memory_space=pl.ANY)
    res = pl.pallas_call(
        body, in_specs=[any_spec] * n, out_specs=[any_spec] * n,
        out_shape=[jax.ShapeDtypeStruct(a.shape, a.dtype) for a in arrs],
        scratch_shapes=[pltpu.SemaphoreType.DMA((n,)), pltpu.SemaphoreType.DMA((n,))],
        name=name)(*arrs)
    return list(res)


def exchange(arrs, gather, peers, name, by_chip=False, own=True):
    n, n_peers = len(arrs), len(peers)
    n_slots = N_SHARDS if by_chip else N_DEV
    slot = (lambda x, y, c: 2 * x + y) if by_chip else (lambda x, y, c: 4 * x + 2 * y + c)

    def body(*refs):
        in_refs, out_refs = refs[:n], refs[n:2 * n]
        send_sems, recv_sems, local_sems = refs[2 * n:]
        x, y, c = _mesh_pos()
        me = slot(x, y, c)
        copies, locals_ = [], []
        for a in range(n):
            if own:
                mine = in_refs[a] if gather else in_refs[a].at[me]
                loc = pltpu.make_async_copy(mine, out_refs[a].at[me], local_sems.at[a])
                loc.start()
                locals_.append(loc)
            for k, (dx, dy, dc) in enumerate(peers):
                peer = (1 - x if dx else x, 1 - y if dy else y, 1 - c if dc else c)
                src = in_refs[a] if gather else in_refs[a].at[slot(*peer)]
                sem = a * n_peers + k
                cp = pltpu.make_async_remote_copy(src_ref=src, dst_ref=out_refs[a].at[me], send_sem=send_sems.at[sem],
                                                  recv_sem=recv_sems.at[sem], device_id=peer, device_id_type=MESH_ID)
                cp.start()
                copies.append(cp)
        for cp in copies:
            cp.wait()
        for loc in locals_:
            loc.wait()

    any_spec = pl.BlockSpec(memory_space=pl.ANY)
    out_shape = [jax.ShapeDtypeStruct((n_slots,) + (a.shape if gather else a.shape[1:]), a.dtype) for a in arrs]
    n_sems = n * n_peers
    res = pl.pallas_call(
        body, in_specs=[any_spec] * n, out_specs=[any_spec] * n, out_shape=out_shape,
        scratch_shapes=[pltpu.SemaphoreType.DMA((n_sems,)), pltpu.SemaphoreType.DMA((n_sems,)),
                        pltpu.SemaphoreType.DMA((n,))],
        name=name)(*arrs)
    return list(res)


HBM_SPEC = pl.BlockSpec(memory_space=pltpu.HBM)
SEM_SPEC = pl.BlockSpec(memory_space=pltpu.SEMAPHORE)
DATAFLOW = pltpu.SideEffectType.DATAFLOW_SIDE_EFFECTING


def _peer_copies(src_refs, land_refs, send_sems, recv_sems, gather):
    x, y, c = _mesh_pos()
    me = 4 * x + 2 * y + c
    copies = []
    for a, (src_ref, land_ref) in enumerate(zip(src_refs, land_refs)):
        for k, (dx, dy, dc) in enumerate(ALL_PEERS):
            peer = (1 - x if dx else x, 1 - y if dy else y, 1 - c if dc else c)
            src = src_ref if gather else src_ref.at[4 * peer[0] + 2 * peer[1] + peer[2]]
            sem = a * len(ALL_PEERS) + k
            copies.append(pltpu.make_async_remote_copy(src_ref=src, dst_ref=land_ref.at[me], send_sem=send_sems.at[sem],
                                                       recv_sem=recv_sems.at[sem], device_id=peer,
                                                       device_id_type=MESH_ID))
    return copies


def exchange_start(arrs, gather, name):
    n = len(arrs)
    lands = [lax.empty((N_DEV,) + (a.shape if gather else a.shape[1:]), a.dtype) for a in arrs]

    def body(*refs):
        for cp in _peer_copies(refs[:n], refs[n:2 * n], refs[2 * n], refs[2 * n + 1], gather):
            cp.start()
        refs[-1][...] = jnp.zeros_like(refs[-1])

    sems = pltpu.SemaphoreType.DMA((n * len(ALL_PEERS),))
    hbm = [pltpu.HBM(a.shape, a.dtype) for a in arrs + lands]
    res = pl.pallas_call(
        body, name=name, out_shape=(sems, sems, *hbm, jax.ShapeDtypeStruct((8, 128), f32)),
        in_specs=[HBM_SPEC] * (2 * n),
        out_specs=(SEM_SPEC, SEM_SPEC, *[HBM_SPEC] * (2 * n), pl.BlockSpec(memory_space=pltpu.VMEM)),
        input_output_aliases={i: 2 + i for i in range(2 * n)},
        compiler_params=pltpu.CompilerParams(has_side_effects=DATAFLOW))(
        *[pltpu.with_memory_space_constraint(a, pltpu.HBM) for a in arrs + lands])
    return res[0], res[1], list(res[2:2 + n]), list(res[2 + n:2 + 2 * n]), res[-1]


def exchange_wait(started, after, gather, name):
    send_sems, recv_sems, srcs, lands, _ = started
    n = len(srcs)

    def body(*refs):
        for cp in _peer_copies(refs[:n], refs[n:2 * n], refs[2 * n], refs[2 * n + 1], gather):
            cp.wait_send()
            cp.wait_recv()

    res = pl.pallas_call(
        body, name=name, out_shape=tuple(pltpu.HBM(a.shape, a.dtype) for a in srcs + lands),
        in_specs=[HBM_SPEC] * (2 * n) + [SEM_SPEC, SEM_SPEC, pl.BlockSpec(memory_space=pl.ANY)],
        out_specs=tuple([HBM_SPEC] * (2 * n)), input_output_aliases={i: i for i in range(2 * n)},
        compiler_params=pltpu.CompilerParams(has_side_effects=DATAFLOW))(*srcs, *lands, send_sems, recv_sems, after)
    return list(res[n:])


def gather_two_level(arrs, name):
    n = len(arrs)
    per = 7

    def body(*refs):
        in_refs, out_refs = refs[:n], refs[n:2 * n]
        send_sems, recv_sems = refs[2 * n:]
        x, y, c = _mesh_pos()
        me, sibling = (x, y, c), (x, y, 1 - c)
        chips = [(1 - x, y), (x, 1 - y), (1 - x, 1 - y)]

        def copy(a, k, block, to, src=None):
            rows = out_refs[a].at[4 * block[0] + 2 * block[1] + block[2]]
            return pltpu.make_async_remote_copy(src_ref=rows if src is None else src, dst_ref=rows,
                                                send_sem=send_sems.at[a * per + k], recv_sem=recv_sems.at[a * per + k],
                                                device_id=to, device_id_type=MESH_ID)

        first, passed = [], []
        for a in range(n):
            first.append(copy(a, 0, me, sibling, src=in_refs[a]))
            first += [copy(a, 1 + j, me, (*chip, c), src=in_refs[a]) for j, chip in enumerate(chips)]
        for cp in first:
            cp.start()
        for a in range(n):
            for j, chip in enumerate(chips):
                copy(a, 1 + j, (*chip, c), me).wait_recv()
                fwd = copy(a, 4 + j, (*chip, c), sibling)
                fwd.start()
                passed.append(fwd)
        for a in range(n):
            copy(a, 0, sibling, me).wait_recv()
            for j, chip in enumerate(chips):
                copy(a, 4 + j, (*chip, 1 - c), me).wait_recv()
        for cp in first + passed:
            cp.wait_send()

    any_spec = pl.BlockSpec(memory_space=pl.ANY)
    res = pl.pallas_call(
        body, in_specs=[any_spec] * n, out_specs=[any_spec] * n,
        out_shape=[jax.ShapeDtypeStruct((N_DEV,) + a.shape, a.dtype) for a in arrs],
        scratch_shapes=[pltpu.SemaphoreType.DMA((n * per,)), pltpu.SemaphoreType.DMA((n * per,))],
        name=name)(*arrs)
    return list(res)


def sum_slots(parts, slots, name):
    _, r, c = parts.shape
    tr = r
    for cand in (512, 256, 128, 64, 32, 16, 8):
        if r % cand == 0 and cand * c * 4 * len(slots) <= 8 * 1024 * 1024:
            tr = cand
            break

    def body(p_ref, o_ref):
        acc = p_ref[slots[0]].astype(f32)
        for s in slots[1:]:
            acc = acc + p_ref[s].astype(f32)
        o_ref[...] = acc

    return pl.pallas_call(body, grid=(r // tr,), in_specs=[pl.BlockSpec((parts.shape[0], tr, c), lambda i: (0, i, 0))],
                          out_specs=pl.BlockSpec((tr, c), lambda i: (i, 0)),
                          out_shape=jax.ShapeDtypeStruct((r, c), f32),
                          compiler_params=_cparams(("parallel",)), name=name)(parts)


def column_sum(a, name):
    def body(a_ref, o_ref):
        o_ref[...] = jnp.sum(a_ref[...], axis=0, keepdims=True)

    return pl.pallas_call(body, out_shape=jax.ShapeDtypeStruct((1, a.shape[1]), f32), name=name)(a)


def adamw(w, g, m, v, name):
    r, c = w.shape
    tr = r
    for cand in (256, 128, 64, 32, 16, 8):
        if r % cand == 0:
            tr = cand
            break

    def body(w_ref, g_ref, m_ref, v_ref, d_ref, mo_ref, vo_ref):
        gv = g_ref[...]
        m_new = ADAM_B1 * m_ref[...] + (1.0 - ADAM_B1) * gv
        v_new = ADAM_B2 * v_ref[...] + (1.0 - ADAM_B2) * jnp.square(gv)
        m_hat = m_new / (1.0 - ADAM_B1 ** ADAM_STEP)
        v_hat = v_new / (1.0 - ADAM_B2 ** ADAM_STEP)
        d_ref[...] = -ADAM_LR * (m_hat / (jnp.sqrt(v_hat) + ADAM_EPS) + ADAM_WD * w_ref[...])
        mo_ref[...] = m_new
        vo_ref[...] = v_new

    spec = pl.BlockSpec((tr, c), lambda i: (i, 0))
    return pl.pallas_call(body, grid=(r // tr,), in_specs=[spec] * 4, out_specs=[spec] * 3,
                          out_shape=[jax.ShapeDtypeStruct((r, c), f32)] * 3,
                          compiler_params=_cparams(("parallel",)), name=name)(w, g, m, v)


def adaln_fwd(c_rows, w, b):
    def body(c_ref, w_ref, b_ref, o_ref):
        cv = c_ref[...]
        o_ref[...] = _mxu_dot(cv * jax.nn.sigmoid(cv), w_ref[...]) + b_ref[...]

    return pl.pallas_call(body, out_shape=jax.ShapeDtypeStruct((c_rows.shape[0], w.shape[1]), f32),
                          compiler_params=pltpu.CompilerParams(vmem_limit_bytes=VMEM_LIMIT), name="adaln_fwd")(c_rows, w, b)


def adaln_bwd(c_rows, dm, w):
    def body(c_ref, dm_ref, w_ref, gw_ref, ds_ref):
        cv = c_ref[...]
        gw_ref[...] = _dg(cv * jax.nn.sigmoid(cv), dm_ref[...], 0, 0)
        ds_ref[...] = _dg(dm_ref[...], w_ref[...], 1, 1)

    return pl.pallas_call(body, out_shape=[jax.ShapeDtypeStruct(w.shape, f32),
                                           jax.ShapeDtypeStruct(c_rows.shape, f32)],
                          compiler_params=pltpu.CompilerParams(vmem_limit_bytes=VMEM_LIMIT), name="adaln_bwd")(c_rows, dm, w)


def c_ctx_grad(parts, c_ctx_row):
    def body(p_ref, c_ref, o_ref):
        total = p_ref[0, 0:1, :]
        for s in range(1, N_SHARDS):
            total = total + p_ref[s, 0:1, :]
        _, vjp = jax.vjp(jax.nn.silu, c_ref[...])
        o_ref[...] = vjp(total)[0]

    return pl.pallas_call(body, out_shape=jax.ShapeDtypeStruct((1, D_MODEL), f32), name="c_ctx_grad")(parts, c_ctx_row)


PACK_W = 1024
PACK_ROWS = 8


def _pack(arrs):
    pieces, layout, r0 = [], [], 0
    for a in arrs:
        size = math.prod(a.shape)
        rows = -(-size // (PACK_W * PACK_ROWS)) * PACK_ROWS
        pieces.append(jnp.pad(a.reshape(-1).astype(f32), (0, rows * PACK_W - size)).reshape(rows, PACK_W))
        layout.append((r0, rows, a.shape))
        r0 += rows
    return jnp.concatenate(pieces, axis=0), layout


def _unpack(pack, layout, lead=()):
    n_lead = len(lead)
    outs = []
    for r0, rows, shape in layout:
        piece = pack[(slice(None),) * n_lead + (slice(r0, r0 + rows),)].reshape(lead + (-1,))
        outs.append(piece[..., :math.prod(shape)].reshape(lead + tuple(shape)))
    return outs


W_NAMES = ("c_ctx", "w_ada", "b_ada", "norm1_g", "norm2_g", "w_in", "ret_log_decay", "rwkv_shift_mu", "rwkv_w0",
           "rwkv_w_up", "rwkv_a0", "rwkv_a_up", "rwkv_g_up", "rwkv_k_k", "rwkv_k_a", "rwkv_r_k", "rwkv_ln_w",
           "rwkv_ln_b", "w_out", "w_ff1", "b_ff1", "w_ff2", "b_ff2", "final_g")
COL_SHARDED = ("w_in", "w_ff1")
ROW_SHARDED = ("w_out", "w_ff2")
LAST_SHARDED = ("rwkv_shift_mu", "rwkv_w0", "rwkv_w_up", "rwkv_a0", "rwkv_a_up", "rwkv_g_up")
REPLICATED = ("c_ctx", "b_ada", "norm1_g", "norm2_g", "ret_log_decay", "rwkv_k_k", "rwkv_k_a", "rwkv_r_k",
              "rwkv_ln_w", "rwkv_ln_b", "b_ff1", "b_ff2", "final_g")
N_SHARDS = 4


def _train_step(a):
    x, c, ctx, tgt = a["x"], a["c"], a["ctx"], a["loss_target"]
    bsz = x.shape[0]
    mx, my, mc = _mesh_pos()
    shard = 2 * mx + my
    dev = _device_slot()

    (c_all,) = exchange([jnp.pad(c, ((0, PACK_ROWS - bsz), (0, 0)))], True, ALL_PEERS, "gather_c")
    n_ex = N_DEV * bsz
    c_rows = jnp.concatenate([c_all[:, :bsz].reshape(n_ex, D_MODEL), a["c_ctx"][None, :],
                              jnp.zeros((PACK_ROWS - 1, D_MODEL), f32)], axis=0)
    ada_cols = a["w_ada"].shape[-1]
    b_ada_cols = lax.dynamic_slice_in_dim(a["b_ada"], shard * ada_cols, ada_cols, axis=1)
    mod_cols = adaln_fwd(c_rows, a["w_ada"][0], b_ada_cols)

    def own_half(n):
        w = a[n][0].astype(MXU_DTYPE)
        return lax.dynamic_slice_in_dim(w, mc * (w.shape[0] // 2), w.shape[0] // 2, axis=0)

    def whole_weight(n, gth, own):
        per_chip = lax.dynamic_update_index_in_dim(gth, own, dev, 0).reshape(N_SHARDS, -1, gth.shape[-1])
        return (per_chip.transpose(1, 0, 2).reshape(per_chip.shape[1], -1) if n in COL_SHARDED
                else per_chip.reshape(-1, per_chip.shape[-1]))

    small_pack, small_layout = _pack([a[n][0] for n in LAST_SHARDED])
    own_blocks = [mod_cols, own_half("w_in"), small_pack]
    gathered = gather_two_level(own_blocks, "gather_weights")
    late_own = [own_half(n) for n in LATE_WEIGHTS]
    late_started = exchange_start(late_own, True, "gather_late_start")
    mod_own = lax.dynamic_update_index_in_dim(gathered[0], mod_cols, dev, 0)
    mod_all = jnp.stack([mod_own[s] for s in CHIP_SLOTS], axis=1).reshape(c_rows.shape[0], -1)
    mod_all = mod_all + late_started[-1][0, 0]
    mod_x = lax.dynamic_slice_in_dim(mod_all, dev * bsz, bsz, axis=0).reshape(bsz, 6, D_MODEL)
    mod_ctx = mod_all[n_ex].reshape(6, D_MODEL)
    wt = {"w_in": whole_weight("w_in", gathered[1], own_blocks[1])}

    def late_weights(after):
        lands = exchange_wait(late_started, after, True, "gather_late_wait")
        return {n: whole_weight(n, land, own) for n, land, own in zip(LATE_WEIGHTS, lands, late_own)}

    def grad_blocks(n, gw):
        if n in COL_SHARDED:
            gw = gw.reshape(gw.shape[0], N_SHARDS, -1).transpose(1, 0, 2)
        return gw.reshape(N_DEV, -1, gw.shape[-1]).astype(MXU_DTYPE)

    late_sent, last_sent = {}, {}

    def early_grads(late_g):
        late_sent["blocks"] = [grad_blocks(n, late_g[n]) for n in LATE_WEIGHTS]
        late_sent["started"] = exchange_start(late_sent["blocks"], False, "scatter_late_start")
        return late_sent["started"][-1]

    def last_grads(g_w_in, g_small):
        shard_packs = []
        for s in range(N_SHARDS):
            pieces_s = [lax.slice_in_dim(g_small[n], s * a[n].shape[-1], (s + 1) * a[n].shape[-1],
                                         axis=g_small[n].ndim - 1) for n in LAST_SHARDED]
            pack_s, last_sent["layout"] = _pack(pieces_s)
            shard_packs.append(jnp.pad(pack_s, ((0, -pack_s.shape[0] % (2 * PACK_ROWS)), (0, 0))))
        last_sent["blocks"] = [grad_blocks("w_in", g_w_in), jnp.stack(shard_packs).reshape(N_DEV, -1, PACK_W)]
        last_sent["started"] = exchange_start(last_sent["blocks"], False, "scatter_last_start")
        return last_sent["started"][-1]

    small_own = lax.dynamic_update_index_in_dim(gathered[2], small_pack, dev, 0)
    small_by_chip = _unpack(jnp.stack([small_own[s] for s in CHIP_SLOTS]), small_layout, (N_SHARDS,))
    for n, parts in zip(LAST_SHARDED, small_by_chip):
        wt[n] = jnp.concatenate([parts[s] for s in range(N_SHARDS)], axis=-1)
    for n in ("norm1_g", "norm2_g", "rwkv_k_k", "rwkv_k_a", "rwkv_r_k", "rwkv_ln_w", "rwkv_ln_b", "b_ff1", "b_ff2"):
        wt[n] = a[n]
    wt["ret_log_decay"] = a["ret_log_decay"][0]
    wt["final_g"] = a["final_g"][None, :]

    loss, grad_x, g = layer_step(x, ctx, tgt, mod_x, mod_ctx, wt, late_weights, early_grads, last_grads)

    small_names = [n for n in REPLICATED if n not in ("c_ctx", "b_ada")]
    g_pack, g_layout = _pack([jnp.pad(loss, ((0, 0), (0, PACK_W - loss.shape[1])))] + [g[n] for n in small_names]
                             + [g["mod_x"], g["mod_ctx"]])
    (g_packs,) = gather_two_level([g_pack], "gather_small_grads")
    g_packs = lax.dynamic_update_index_in_dim(g_packs, g_pack, dev, 0)
    g_sum = _unpack(sum_slots(g_packs, tuple(range(N_DEV)), "sum_small_grads"), g_layout)
    loss_total = g_sum[0][0, 0]
    grads = dict(zip(small_names, g_sum[1:1 + len(small_names)]))
    dmod_ctx = g_sum[-1].reshape(1, -1)
    dmod_x = _unpack(g_packs, g_layout, (N_DEV,))[-2].reshape(n_ex, -1)
    dmod = jnp.concatenate([dmod_x, dmod_ctx, jnp.zeros((PACK_ROWS - 1, dmod_x.shape[1]), f32)], axis=0)
    grads["b_ada"] = column_sum(dmod, "b_ada_grad")
    dmod_cols = lax.dynamic_slice_in_dim(dmod, shard * ada_cols, ada_cols, axis=1)
    grads["w_ada"], dsilu = adaln_bwd(c_rows, dmod_cols, a["w_ada"][0])

    dsilu_rows = jnp.broadcast_to(jnp.pad(dsilu[n_ex:n_ex + 1], ((0, PACK_ROWS - 1), (0, 0)))[None],
                                  (N_SHARDS, PACK_ROWS, D_MODEL))
    (shares,) = exchange([dsilu_rows], False, CHIP_PEERS, "share_c_ctx_grad", by_chip=True, own=False)
    shares = lax.dynamic_update_index_in_dim(shares, dsilu_rows[0], shard, 0)
    grads["c_ctx"] = c_ctx_grad(shares, a["c_ctx"][None, :])

    scattered, half_sums = ("w_in", "small_shards") + LATE_WEIGHTS, []
    for sent, wait_name, after in ((last_sent, "scatter_last_wait", grads["c_ctx"]),
                                   (late_sent, "scatter_late_wait", grads["c_ctx"])):
        for land, block in zip(exchange_wait(sent["started"], after, False, wait_name), sent["blocks"]):
            land = lax.dynamic_update_index_in_dim(land, lax.dynamic_index_in_dim(block, dev, 0, keepdims=False), dev, 0)
            half_sums.append(sum_slots(land, tuple(range(N_DEV)), f"sum_{scattered[len(half_sums)]}"))
    other_halves = sibling_swap(half_sums, "swap_halves")
    for n, mine, other in zip(scattered, half_sums, other_halves):
        rows = mine.shape[0]
        whole = jnp.zeros((2 * rows, mine.shape[1]), f32)
        whole = lax.dynamic_update_slice_in_dim(whole, mine, mc * rows, axis=0)
        grads[n] = lax.dynamic_update_slice_in_dim(whole, other, (1 - mc) * rows, axis=0)
    grads.update(zip(LAST_SHARDED, _unpack(grads.pop("small_shards"), last_sent["layout"])))

    out_g, out_d, out_m, out_v = {}, {}, {}, {}
    for n in ("w_ada",) + COL_SHARDED + ROW_SHARDED:
        out_g[n] = grads[n].reshape(a[n].shape)
        two_d = lambda z: z.reshape(-1, z.shape[-1])
        d, m, v = adamw(two_d(a[n]), two_d(out_g[n]), two_d(a["m_" + n]), two_d(a["v_" + n]), f"adamw_{n}")
        out_d[n], out_m[n], out_v[n] = d.reshape(a[n].shape), m.reshape(a[n].shape), v.reshape(a[n].shape)
    rest = REPLICATED + LAST_SHARDED
    for n in rest:
        out_g[n] = grads[n].reshape(a[n].shape)
    packs = [_pack([src[n] for n in rest])[0] for src in
             ({n: a[n] for n in rest}, out_g, {n: a["m_" + n] for n in rest}, {n: a["v_" + n] for n in rest})]
    _, rest_layout = _pack([a[n] for n in rest])
    for dst, pack in zip((out_d, out_m, out_v), adamw(*packs, "adamw_small")):
        dst.update(zip(rest, _unpack(pack, rest_layout)))
    return (loss_total, grad_x, *[out_g[n] for n in W_NAMES], *[out_d[n] for n in W_NAMES],
            *[out_m[n] for n in W_NAMES], *[out_v[n] for n in W_NAMES])


def kernel(x, c, ctx, c_ctx, w_ada, b_ada, norm1_g, norm2_g, w_in, ret_log_decay, rwkv_shift_mu, rwkv_w0, rwkv_w_up, rwkv_a0, rwkv_a_up, rwkv_g_up, rwkv_k_k, rwkv_k_a, rwkv_r_k, rwkv_ln_w, rwkv_ln_b, w_out, w_ff1, b_ff1, w_ff2, b_ff2, final_g, loss_target, m_c_ctx, m_w_ada, m_b_ada, m_norm1_g, m_norm2_g, m_w_in, m_ret_log_decay, m_rwkv_shift_mu, m_rwkv_w0, m_rwkv_w_up, m_rwkv_a0, m_rwkv_a_up, m_rwkv_g_up, m_rwkv_k_k, m_rwkv_k_a, m_rwkv_r_k, m_rwkv_ln_w, m_rwkv_ln_b, m_w_out, m_w_ff1, m_b_ff1, m_w_ff2, m_b_ff2, m_final_g, v_c_ctx, v_w_ada, v_b_ada, v_norm1_g, v_norm2_g, v_w_in, v_ret_log_decay, v_rwkv_shift_mu, v_rwkv_w0, v_rwkv_w_up, v_rwkv_a0, v_rwkv_a_up, v_rwkv_g_up, v_rwkv_k_k, v_rwkv_k_a, v_rwkv_r_k, v_rwkv_ln_w, v_rwkv_ln_b, v_w_out, v_w_ff1, v_b_ff1, v_w_ff2, v_b_ff2, v_final_g):
    return _train_step(dict(locals()))
```

```python
import functools
import math

import jax
import jax.numpy as jnp
from jax import lax
from jax.experimental import pallas as pl
from jax.experimental.pallas import tpu as pltpu

f32 = jnp.float32
MXU_DTYPE = jnp.bfloat16

D_MODEL = 1024
RET_W = 512
RET_HEADS = 4
RET_DH = 128
RET_CHUNK = 128
RW_W = 512
RW_N = 64
DECAY_LORA = 64
AAA_LORA = 64
GATE_LORA = 128
LORA_W = DECAY_LORA + AAA_LORA + GATE_LORA
D_FF = 4096
RET_COLS = 4 * RET_W
SHIFT_COLS = 3 * RW_W + LORA_W
IN_COLS = RET_COLS + SHIFT_COLS
GRID_W = 64
ROPE_BASE = 10000.0
NORM_EPS = 1e-6
GN_EPS = 64e-5
W_DECAY_SCALE = math.exp(-0.5)
ADAM_LR, ADAM_B1, ADAM_B2, ADAM_EPS, ADAM_WD, ADAM_STEP = 0.001, 0.9, 0.999, 1e-08, 0.01, 10

TOK_TILE = 256
MATMUL_TILE = 1024
SCAN_CHUNK = 32
SCAN_UNROLL = SCAN_CHUNK
N_DEV = 8
V7X_VMEM_BYTES = 64 * 1024 * 1024
VMEM_LIMIT = V7X_VMEM_BYTES * 7 // 8


def _cparams(sem):
    return pltpu.CompilerParams(dimension_semantics=sem, vmem_limit_bytes=VMEM_LIMIT)


def _tile(n, cap):
    best = None
    for t in range(128, min(n, cap) + 1, 128):
        if n % t == 0:
            best = t
    return best if best is not None else n


def matmul(a, b, mode, name, out_dtype=f32, bias=None, finish=None):
    if mode == "nn":
        (m, k), (k2, n) = a.shape, b.shape
    elif mode == "nt":
        (m, k), (n, k2) = a.shape, b.shape
    else:
        (k, m), (k2, n) = a.shape, b.shape
    assert k == k2, (a.shape, b.shape, mode)
    tm, tn, tk = _tile(m, MATMUL_TILE), _tile(n, MATMUL_TILE), _tile(k, MATMUL_TILE)
    nk = k // tk
    dims = {"nn": ((1,), (0,)), "nt": ((1,), (1,)), "tn": ((0,), (0,))}[mode]

    def body(a_ref, b_ref, *rest):
        o_ref, acc_ref = rest[-2:]
        kk = pl.program_id(2)

        @pl.when(kk == 0)
        def _():
            acc_ref[...] = jnp.zeros_like(acc_ref)

        acc_ref[...] += lax.dot_general(a_ref[...].astype(MXU_DTYPE), b_ref[...].astype(MXU_DTYPE),
                                        (dims, ((), ())), preferred_element_type=f32)

        @pl.when(kk == nk - 1)
        def _():
            res = acc_ref[...]
            if bias is not None:
                res = res + rest[0][...]
            if finish is not None:
                res = finish(res)
            o_ref[...] = res.astype(o_ref.dtype)

    if mode == "nn":
        a_spec = pl.BlockSpec((tm, tk), lambda i, j, q: (i, q))
        b_spec = pl.BlockSpec((tk, tn), lambda i, j, q: (q, j))
    elif mode == "nt":
        a_spec = pl.BlockSpec((tm, tk), lambda i, j, q: (i, q))
        b_spec = pl.BlockSpec((tn, tk), lambda i, j, q: (j, q))
    else:
        a_spec = pl.BlockSpec((tk, tm), lambda i, j, q: (q, i))
        b_spec = pl.BlockSpec((tk, tn), lambda i, j, q: (q, j))
    extra_specs = [] if bias is None else [pl.BlockSpec((1, tn), lambda i, j, q: (0, j))]
    extra = [] if bias is None else [bias]
    return pl.pallas_call(
        body, grid=(m // tm, n // tn, nk), in_specs=[a_spec, b_spec] + extra_specs,
        out_specs=pl.BlockSpec((tm, tn), lambda i, j, q: (i, j)),
        out_shape=jax.ShapeDtypeStruct((m, n), out_dtype),
        scratch_shapes=[pltpu.VMEM((tm, tn), f32)],
        compiler_params=_cparams(("parallel", "parallel", "arbitrary")), name=name)(a, b, *extra)


class Tiled:
    def __init__(self, arr, w=None, cidx=0, toff=0):
        self.arr, self.w, self.cidx, self.toff = arr, (arr.shape[-1] if w is None else w), cidx, toff

    def spec(self):
        cidx, toff = self.cidx, self.toff
        return pl.BlockSpec((None, TOK_TILE, self.w), lambda b, i: (b, jnp.maximum(i + toff, 0), cidx))


class Seg:
    def __init__(self, arr, seg, first):
        self.arr, self.seg, self.first = arr, seg, first

    def spec(self):
        seg = self.seg
        return pl.BlockSpec((None, None, 1, self.arr.shape[-1]), lambda b, i: (b, seg(i), 0, 0))


class Glob:
    def __init__(self, arr):
        self.arr = arr

    def spec(self):
        return pl.BlockSpec(self.arr.shape, lambda b, i: (0,) * self.arr.ndim)


def ew_forward(fn, name, bsz, n_tiles, ins, outs):
    n_in = len(ins)

    def body(*refs):
        res = fn(*[r[...] for r in refs[:n_in]])
        for o_ref, o in zip(refs[n_in:], res):
            o_ref[...] = o.astype(o_ref.dtype)

    out_specs = [pl.BlockSpec((None, TOK_TILE, w), lambda b, i: (b, i, 0)) for w, _ in outs]
    out_shape = [jax.ShapeDtypeStruct((bsz, n_tiles * TOK_TILE, w), dt) for w, dt in outs]
    return pl.pallas_call(body, grid=(bsz, n_tiles), in_specs=[d.spec() for d in ins], out_specs=out_specs,
                          out_shape=out_shape, compiler_params=_cparams(("parallel", "parallel")), name=name)(
        *[d.arr for d in ins])


def ew_backward(fn, name, bsz, n_tiles, ins, cts, want, grad_dtypes=None, lead=0):
    n_in, n_ct = len(ins), len(cts)
    diff = [k for k in range(n_in) if want[k]]
    grad_dtypes = grad_dtypes or {}
    assert lead == 0 or not any(isinstance(ins[k], Seg) for k in diff)

    def body(*refs):
        b, i = pl.program_id(0), pl.program_id(1)
        g_refs = refs[n_in + n_ct:]

        def tile_grads():
            vals = [r[...] for r in refs[:n_in]]
            ct_vals = tuple(r[...].astype(f32) for r in refs[n_in:n_in + n_ct])

            def f(*dvals):
                full = list(vals)
                for k, v in zip(diff, dvals):
                    full[k] = v
                return tuple(fn(*full))

            _, vjp = jax.vjp(f, *[vals[k] for k in diff])
            grads = vjp(ct_vals)
            for k, g_ref, g in zip(diff, g_refs, grads):
                d = ins[k]
                if isinstance(d, Tiled):
                    g_ref[...] = g.astype(g_ref.dtype)
                else:
                    zero = d.first(i) if isinstance(d, Seg) else jnp.logical_and(b == 0, i == lead)

                    @pl.when(zero)
                    def _(g_ref=g_ref):
                        g_ref[...] = jnp.zeros_like(g_ref)

                    g_ref[...] += g

        if lead == 0:
            tile_grads()
        else:
            pl.when(i >= lead)(tile_grads)

            @pl.when(i < lead)
            def _():
                for k, g_ref in zip(diff, g_refs):
                    if isinstance(ins[k], Tiled):
                        g_ref[...] = jnp.zeros_like(g_ref)

    out_specs, out_shape = [], []
    for k in diff:
        d = ins[k]
        if isinstance(d, Tiled):
            out_specs.append(pl.BlockSpec((None, TOK_TILE, d.w), lambda b, i: (b, i, 0)))
            out_shape.append(jax.ShapeDtypeStruct((bsz, (n_tiles + lead) * TOK_TILE, d.w), grad_dtypes.get(k, f32)))
        else:
            out_specs.append(d.spec())
            out_shape.append(jax.ShapeDtypeStruct(d.arr.shape, f32))
    return pl.pallas_call(body, grid=(bsz, n_tiles + lead),
                          in_specs=[d.spec() for d in ins] + [c.spec() for c in cts],
                          out_specs=out_specs, out_shape=out_shape,
                          compiler_params=_cparams(("arbitrary", "arbitrary")), name=name)(
        *[d.arr for d in ins], *[c.arr for c in cts])


@jax.custom_vjp
def _mxu_dot(a, b):
    return jnp.dot(a.astype(MXU_DTYPE), b.astype(MXU_DTYPE), preferred_element_type=f32)


def _mxu_dot_fwd(a, b):
    return _mxu_dot(a, b), (a, b)


def _mxu_dot_bwd(res, ct):
    a, b = res
    ct = ct.astype(MXU_DTYPE)
    da = lax.dot_general(ct, b.astype(MXU_DTYPE), (((1,), (1,)), ((), ())), preferred_element_type=f32)
    db = lax.dot_general(a.astype(MXU_DTYPE), ct, (((0,), (0,)), ((), ())), preferred_element_type=f32)
    return da, db


_mxu_dot.defvjp(_mxu_dot_fwd, _mxu_dot_bwd)


def _split_dot_impl(x, ones_mat):
    hi = x.astype(MXU_DTYPE)
    lo = (x - hi.astype(f32)).astype(MXU_DTYPE)
    return jnp.dot(hi, ones_mat, preferred_element_type=f32) + jnp.dot(lo, ones_mat, preferred_element_type=f32)


@jax.custom_vjp
def _split_dot(x, ones_mat):
    return _split_dot_impl(x, ones_mat)


def _split_dot_fwd(x, ones_mat):
    return _split_dot_impl(x, ones_mat), ones_mat


def _split_dot_bwd(ones_mat, ct):
    return _split_dot_impl(ct, ones_mat), None


_split_dot.defvjp(_split_dot_fwd, _split_dot_bwd)


def _block_ones(n, group):
    idx = jnp.arange(n) // group
    return (idx[:, None] == idx[None, :]).astype(MXU_DTYPE)


def _rms(x, g):
    return x * lax.rsqrt(jnp.mean(x * x, axis=-1, keepdims=True) + NORM_EPS) * g


def fn_norm_mod(h, shift, scale, g):
    return (_rms(h, g) * (1.0 + scale) + shift,)


def fn_rwkv_prepare(ks, lora, w0_f, w0_b, a0_f, a0_b, w_up_f, w_up_b, a_up_f, a_up_b, g_up, k_k, k_a, ones64):
    kkr = ks * k_k
    kk = kkr * lax.rsqrt(_split_dot(kkr * kkr, ones64) + 1e-12)
    outs = [kk]
    th = jnp.tanh(lora)
    for w0, a0, w_up, a_up in ((w0_f, a0_f, w_up_f, a_up_f), (w0_b, a0_b, w_up_b, a_up_b)):
        w = jnp.exp(-W_DECAY_SCALE * jax.nn.sigmoid(w0 + _mxu_dot(th, w_up)))
        a = jax.nn.sigmoid(a0 + _mxu_dot(lora, a_up))
        kt = ks * (1.0 + (a - 1.0) * k_a)
        outs += [w, a * kk, kt]
    outs.append(_mxu_dot(jax.nn.sigmoid(lora), g_up))
    return tuple(outs)


def fn_merge(o_f, o_b, g_ret, y_f, y_b, r, kt_f, v, g_rw, r_k, ln_w, ln_b, ones64, ones128):
    o = o_f + o_b
    ret = o * lax.rsqrt(_split_dot(o * o, ones128) * (1.0 / RET_DH) + NORM_EPS) * (g_ret * jax.nn.sigmoid(g_ret))
    y = y_f + y_b
    mean = _split_dot(y, ones64) * (1.0 / RW_N)
    yc = y - mean
    var = _split_dot(yc * yc, ones64) * (1.0 / RW_N)
    y_n = yc * lax.rsqrt(var + GN_EPS) * ln_w + ln_b
    bonus = _split_dot(r * kt_f * r_k, ones64) * v
    return ret, (y_n + bonus) * g_rw


def fn_resid_norm_mod(x, mix, gate, shift, scale, g):
    h1 = x + gate * mix
    return h1, _rms(h1, g) * (1.0 + scale) + shift


def relu2(z):
    return jnp.square(jnp.maximum(z, 0.0))


def relu2_backward(act, dact, name):
    bsz, n_tok, width = act.shape

    def body(a_ref, d_ref, du_ref, db_ref):
        du = d_ref[...].astype(f32) * (2.0 * jnp.sqrt(a_ref[...].astype(f32)))
        du_ref[...] = du.astype(du_ref.dtype)

        @pl.when(jnp.logical_and(pl.program_id(0) == 0, pl.program_id(1) == 0))
        def _():
            db_ref[...] = jnp.zeros_like(db_ref)

        db_ref[...] += jnp.sum(du, axis=0, keepdims=True)

    tile = pl.BlockSpec((None, TOK_TILE, width), lambda b, i: (b, i, 0))
    row = pl.BlockSpec((1, width), lambda b, i: (0, 0))
    return pl.pallas_call(body, grid=(bsz, n_tok // TOK_TILE), in_specs=[tile, tile], out_specs=[tile, row],
                          out_shape=[jax.ShapeDtypeStruct(act.shape, MXU_DTYPE), jax.ShapeDtypeStruct((1, width), f32)],
                          compiler_params=_cparams(("arbitrary", "arbitrary")), name=name)(act, dact)


def fn_loss(h1, f, tgt, gate, b2, g):
    y = _rms(h1 + gate * (f + b2), g)
    err = jnp.square(y - tgt)
    return 0.5 * jnp.sum(jnp.mean(err, axis=-1, keepdims=True), axis=0, keepdims=True)


def loss_and_grads(h1, f, tgt, gate, b2, g, bsz, n_tiles):
    def body(h1_ref, f_ref, t_ref, gate_ref, b2_ref, g_ref, loss_ref, dh1_ref, df_ref, dgate_ref, db2_ref, dg_ref):
        b, i = pl.program_id(0), pl.program_id(1)
        tgt_v = t_ref[...]
        loss, vjp = jax.vjp(lambda a, c, e, p, q: fn_loss(a, c, tgt_v, e, p, q),
                            h1_ref[...], f_ref[...], gate_ref[...], b2_ref[...], g_ref[...])
        dh1, df, dgate, db2, dg = vjp(jnp.ones((1, 1), f32))
        dh1_ref[...] = dh1
        df_ref[...] = df.astype(df_ref.dtype)

        @pl.when(i == 0)
        def _():
            dgate_ref[...] = jnp.zeros_like(dgate_ref)

        @pl.when(jnp.logical_and(b == 0, i == 0))
        def _():
            loss_ref[...] = jnp.zeros_like(loss_ref)
            db2_ref[...] = jnp.zeros_like(db2_ref)
            dg_ref[...] = jnp.zeros_like(dg_ref)

        dgate_ref[...] += dgate
        db2_ref[...] += db2
        dg_ref[...] += dg
        loss_ref[...] += jnp.broadcast_to(loss, loss_ref.shape)

    tile = pl.BlockSpec((None, TOK_TILE, D_MODEL), lambda b, i: (b, i, 0))
    row = pl.BlockSpec((1, D_MODEL), lambda b, i: (0, 0))
    seg = pl.BlockSpec((None, None, 1, D_MODEL), lambda b, i: (b, 0, 0, 0))
    t_tok = n_tiles * TOK_TILE
    return pl.pallas_call(
        body, grid=(bsz, n_tiles), in_specs=[tile, tile, tile, seg, row, row],
        out_specs=[pl.BlockSpec((1, 128), lambda b, i: (0, 0)), tile, tile, seg, row, row],
        out_shape=[jax.ShapeDtypeStruct((1, 128), f32), jax.ShapeDtypeStruct((bsz, t_tok, D_MODEL), f32),
                   jax.ShapeDtypeStruct((bsz, t_tok, D_MODEL), MXU_DTYPE),
                   jax.ShapeDtypeStruct((bsz, 1, 1, D_MODEL), f32),
                   jax.ShapeDtypeStruct((1, D_MODEL), f32), jax.ShapeDtypeStruct((1, D_MODEL), f32)],
        compiler_params=_cparams(("arbitrary", "arbitrary")), name="loss_and_grads")(h1, f, tgt, gate, b2, g)


SHIFT_BLOCK = SHIFT_COLS
HALO_ROWS = 8


def _shift_specs(n_tok, col0):
    per_tile = TOK_TILE // HALO_ROWS
    last = n_tok // HALO_ROWS - 1
    tile = pl.BlockSpec((None, TOK_TILE, SHIFT_BLOCK), lambda j, b, i: (b, i, col0 + j))
    prev = pl.BlockSpec((None, HALO_ROWS, SHIFT_BLOCK),
                        lambda j, b, i: (b, jnp.maximum(i * per_tile - 1, 0), col0 + j))
    nxt = pl.BlockSpec((None, HALO_ROWS, SHIFT_BLOCK),
                       lambda j, b, i: (b, jnp.minimum((i + 1) * per_tile, last), col0 + j))
    return tile, prev, nxt


def _shifted(p, prev_ref, next_ref, is_first, is_last):
    row = lax.broadcasted_iota(jnp.int32, p.shape, 0)
    prev_row = jnp.where(is_first, 0.0, prev_ref[HALO_ROWS - 1:HALO_ROWS, :].astype(f32))
    next_row = jnp.where(is_last, 0.0, next_ref[0:1, :].astype(f32))
    prev = jnp.where(row == 0, prev_row, pltpu.roll(p, 1, axis=0))
    nxt = jnp.where(row == TOK_TILE - 1, next_row, pltpu.roll(p, TOK_TILE - 1, axis=0))
    return prev, nxt


def token_shift(px, mu, seg_first, seg_last):
    bsz, n_tok, _ = px.shape
    n_tiles = n_tok // TOK_TILE

    def body(p_ref, prev_ref, next_ref, mu_ref, o_ref):
        i = pl.program_id(2)
        p = p_ref[...]
        prev, nxt = _shifted(p, prev_ref, next_ref, seg_first(i), seg_last(i))
        o_ref[...] = p + mu_ref[0:1, :] * (prev - p) + mu_ref[1:2, :] * (nxt - p)

    tile, prev, nxt = _shift_specs(n_tok, 0)
    return pl.pallas_call(
        body, grid=(SHIFT_COLS // SHIFT_BLOCK, bsz, n_tiles),
        in_specs=[tile, prev, nxt, pl.BlockSpec((2, SHIFT_BLOCK), lambda j, b, i: (0, j))],
        out_specs=pl.BlockSpec((None, TOK_TILE, SHIFT_BLOCK), lambda j, b, i: (b, i, j)),
        out_shape=jax.ShapeDtypeStruct((bsz, n_tok, SHIFT_COLS), f32),
        compiler_params=_cparams(("parallel", "parallel", "parallel")), name="token_shift")(px, px, px, mu)


def token_shift_bwd(dps, px, mu, seg_first, seg_last):
    bsz, n_tok, _ = px.shape
    n_tiles = n_tok // TOK_TILE

    def body(d_ref, dprev_ref, dnext_ref, p_ref, prev_ref, next_ref, mu_ref, dp_ref, dmu_ref):
        b, i = pl.program_id(1), pl.program_id(2)
        first, last = seg_first(i), seg_last(i)
        d, p = d_ref[...], p_ref[...]
        d_prev, d_next = _shifted(d, dprev_ref, dnext_ref, first, last)
        p_prev, p_next = _shifted(p, prev_ref, next_ref, first, last)
        mu0, mu1 = mu_ref[0:1, :], mu_ref[1:2, :]
        dp_ref[...] = (d + mu0 * (d_next - d) + mu1 * (d_prev - d)).astype(dp_ref.dtype)

        @pl.when(jnp.logical_and(b == 0, i == 0))
        def _():
            dmu_ref[...] = jnp.zeros_like(dmu_ref)

        dmu_ref[0:1, :] += jnp.sum(d * (p_prev - p), axis=0, keepdims=True)
        dmu_ref[1:2, :] += jnp.sum(d * (p_next - p), axis=0, keepdims=True)

    dtile, dprev, dnext = _shift_specs(n_tok, 0)
    tile, prev, nxt = _shift_specs(n_tok, 0)
    mu_spec = pl.BlockSpec((2, SHIFT_BLOCK), lambda j, b, i: (0, j))
    return pl.pallas_call(
        body, grid=(SHIFT_COLS // SHIFT_BLOCK, bsz, n_tiles),
        in_specs=[dtile, dprev, dnext, tile, prev, nxt, mu_spec],
        out_specs=[pl.BlockSpec((None, TOK_TILE, SHIFT_BLOCK), lambda j, b, i: (b, i, j)), mu_spec],
        out_shape=[jax.ShapeDtypeStruct((bsz, n_tok, SHIFT_COLS), MXU_DTYPE),
                   jax.ShapeDtypeStruct((2, SHIFT_COLS), f32)],
        compiler_params=_cparams(("arbitrary", "arbitrary", "arbitrary")), name="token_shift_bwd")(
        dps, dps, dps, px, px, px, mu)


def _dg(a, b, ca, cb):
    return lax.dot_general(a.astype(MXU_DTYPE), b.astype(MXU_DTYPE), (((ca,), (cb,)), ((), ())),
                           preferred_element_type=f32)


@jax.custom_vjp
def _mm_nt(a, b):
    return _dg(a, b, 1, 1)


_mm_nt.defvjp(lambda a, b: (_dg(a, b, 1, 1), (a, b)),
              lambda res, ct: (_dg(ct, res[1], 1, 0), _dg(ct, res[0], 0, 0)))


@jax.custom_vjp
def _mm_tn(a, b):
    return _dg(a, b, 0, 0)


_mm_tn.defvjp(lambda a, b: (_dg(a, b, 0, 0), (a, b)),
              lambda res, ct: (_dg(res[1], ct, 1, 1), _dg(res[0], ct, 1, 0)))


ROTARY_PAIR = RET_DH // 4


def _swap_pairs_impl(t):
    lane = lax.broadcasted_iota(jnp.int32, t.shape, 1)
    return jnp.where(lane % (2 * ROTARY_PAIR) < ROTARY_PAIR, pltpu.roll(t, RET_DH - ROTARY_PAIR, axis=1),
                     pltpu.roll(t, ROTARY_PAIR, axis=1))


@jax.custom_vjp
def _swap_pairs(t):
    return _swap_pairs_impl(t)


_swap_pairs.defvjp(lambda t: (_swap_pairs_impl(t), None), lambda _, ct: (_swap_pairs_impl(ct),))


def _ret_chunk(state, q_raw, k_raw, v, cos, sin, ld_row, head, reverse):
    c = RET_CHUNK
    lane = lax.broadcasted_iota(jnp.int32, ld_row.shape, 1)
    lg = -jnp.exp(jnp.sum(jnp.where(lane == head, ld_row, 0.0), axis=-1, keepdims=True))
    rot = lambda t: t * cos + _swap_pairs(t) * sin
    q = rot(q_raw)
    k = rot(k_raw) * (RET_DH ** -0.5)
    ti = lax.broadcasted_iota(jnp.int32, (c, 1), 0).astype(f32)
    tj = lax.broadcasted_iota(jnp.int32, (1, c), 1).astype(f32)
    if not reverse:
        dist, mask, q_exp, k_exp = ti - tj, (ti - tj) >= 0, ti + 1.0, c - 1.0 - ti
    else:
        dist, mask, q_exp, k_exp = tj - ti, (tj - ti) > 0, c - ti, ti
    decay = jnp.where(mask, jnp.exp(lg * jnp.maximum(dist, 0.0)), 0.0)
    scores = _mm_nt(q, k) * decay
    out = _mxu_dot(scores, v) + _mxu_dot(q * jnp.exp(lg * q_exp), state)
    new_state = state * jnp.exp(lg * c) + _mm_tn(k * jnp.exp(lg * k_exp), v)
    return out, new_state


def _ret_specs(bsz, order):
    tok = lambda col=0: pl.BlockSpec((bsz, RET_CHUNK, RET_W), lambda i: (0, order(i), col))
    tab = pl.BlockSpec((RET_CHUNK, RET_DH), lambda i: (order(i), 0))
    ld = pl.BlockSpec((1, RET_DH), lambda i: (0, 0))
    return tok, tab, ld


def retention_fwd(px, cos, sin, ld_row, order, reverse, name):
    bsz, n_tok, _ = px.shape
    n_ch = n_tok // RET_CHUNK

    def body(q_ref, k_ref, v_ref, cos_ref, sin_ref, ld_ref, o_ref, sv_ref, st_ref):
        @pl.when(pl.program_id(0) == 0)
        def _():
            st_ref[...] = jnp.zeros_like(st_ref)

        for b in range(bsz):
            for h in range(RET_HEADS):
                sl = slice(h * RET_DH, (h + 1) * RET_DH)
                s = st_ref[b, h]
                sv_ref[b, h] = s
                o, s_new = _ret_chunk(s, q_ref[b, :, sl], k_ref[b, :, sl], v_ref[b, :, sl], cos_ref[...], sin_ref[...],
                                      ld_ref[...], h, reverse)
                o_ref[b, :, sl] = o
                st_ref[b, h] = s_new

    tok, tab, ld = _ret_specs(bsz, order)
    return pl.pallas_call(
        body, grid=(n_ch,), in_specs=[tok(0), tok(1), tok(2), tab, tab, ld],
        out_specs=[tok(), pl.BlockSpec((bsz, None, RET_HEADS, RET_DH, RET_DH), lambda i: (0, i, 0, 0, 0))],
        out_shape=[jax.ShapeDtypeStruct((bsz, n_tok, RET_W), f32),
                   jax.ShapeDtypeStruct((bsz, n_ch, RET_HEADS, RET_DH, RET_DH), f32)],
        scratch_shapes=[pltpu.VMEM((bsz, RET_HEADS, RET_DH, RET_DH), f32)],
        compiler_params=_cparams(("arbitrary",)), name=name)(px, px, px, cos, sin, ld_row)


def retention_bwd(do, px, states, cos, sin, ld_row, order, reverse, name):
    bsz, n_tok, _ = px.shape
    n_ch = n_tok // RET_CHUNK
    back = lambda i: order(n_ch - 1 - i)

    def body(do_ref, q_ref, k_ref, v_ref, sv_ref, cos_ref, sin_ref, ld_ref,
             dq_ref, dk_ref, dv_ref, dld_ref, dst_ref):
        @pl.when(pl.program_id(0) == 0)
        def _():
            dst_ref[...] = jnp.zeros_like(dst_ref)
            dld_ref[...] = jnp.zeros_like(dld_ref)

        cos_v, sin_v = cos_ref[...], sin_ref[...]
        for b in range(bsz):
            for h in range(RET_HEADS):
                sl = slice(h * RET_DH, (h + 1) * RET_DH)
                f = lambda s, q, k, v, ld, h=h: _ret_chunk(s, q, k, v, cos_v, sin_v, ld, h, reverse)
                _, vjp = jax.vjp(f, sv_ref[b, h], q_ref[b, :, sl], k_ref[b, :, sl], v_ref[b, :, sl], ld_ref[...])
                ds, dq, dk, dv, dld = vjp((do_ref[b, :, sl], dst_ref[b, h]))
                dst_ref[b, h] = ds
                dq_ref[b, :, sl] = dq
                dk_ref[b, :, sl] = dk
                dv_ref[b, :, sl] = dv
                dld_ref[...] += dld

    tok, tab, ld = _ret_specs(bsz, back)
    return pl.pallas_call(
        body, grid=(n_ch,),
        in_specs=[tok(), tok(0), tok(1), tok(2),
                  pl.BlockSpec((bsz, None, RET_HEADS, RET_DH, RET_DH), lambda i: (0, n_ch - 1 - i, 0, 0, 0)),
                  tab, tab, ld],
        out_specs=[tok(), tok(), tok(), ld],
        out_shape=[jax.ShapeDtypeStruct((bsz, n_tok, RET_W), f32)] * 3 + [jax.ShapeDtypeStruct((1, RET_DH), f32)],
        scratch_shapes=[pltpu.VMEM((bsz, RET_HEADS, RET_DH, RET_DH), f32)],
        compiler_params=_cparams(("arbitrary",)), name=name)(
        do, px, px, px, states, cos, sin, ld_row)


HALF_W = RW_W // 2


def _head_sum(x, ones):
    xm = x.astype(MXU_DTYPE)
    return jnp.concatenate([jnp.dot(xm[:, :HALF_W], ones, preferred_element_type=f32),
                            jnp.dot(xm[:, HALF_W:], ones, preferred_element_type=f32)], axis=1)


def _stack(parts):
    return jnp.concatenate(parts, axis=0)


def _row(ref, b, t):
    return ref[b, pl.ds(t, 1), :]


SCAN_DIRS = ((False, True), (True, False))
RW_HEADS = RW_W // RW_N
HEAD_ROWS_PAD = 16


def _head_rows(row, mask):
    return jnp.broadcast_to(row, mask.shape) * mask


def _outer(per_value, row, mask_pad):
    return lax.dot_general(per_value.astype(MXU_DTYPE), _head_rows(row, mask_pad).astype(MXU_DTYPE),
                           (((0,), (0,)), ((), ())), preferred_element_type=f32)


def _read(states, rows, mask, more_rows=()):
    lhs = _stack([_head_rows(r, mask) for r in list(rows) + list(more_rows)])
    return lax.dot_general(lhs.astype(MXU_DTYPE), _stack(states).astype(MXU_DTYPE), (((1,), (1,)), ((), ())),
                           preferred_element_type=f32)


def _own_block(raw, b):
    lanes = raw[:, RW_N * b:RW_N * (b + 1)]
    turned = _stack([lanes[RW_HEADS * b:], lanes[:RW_HEADS * b]]) if b else lanes
    if turned.shape[0] < HEAD_ROWS_PAD:
        turned = _stack([turned, jnp.zeros((HEAD_ROWS_PAD - turned.shape[0], RW_N), f32)])
    return turned[:HEAD_ROWS_PAD]


def _row_from_heads(per_value, state, mask_pad):
    full = jnp.dot(per_value.astype(MXU_DTYPE), state.astype(MXU_DTYPE), preferred_element_type=f32)
    return jnp.sum(full * mask_pad, axis=0, keepdims=True)


def _scan_specs(bsz, order):
    rows = lambda col=0: pl.BlockSpec((bsz, SCAN_CHUNK, RW_W), lambda i: (0, order(i), col))
    per_value = pl.BlockSpec((bsz, SCAN_CHUNK, HEAD_ROWS_PAD, RW_N), lambda i: (0, order(i), 0, 0))
    states = pl.BlockSpec((SCAN_CHUNK, bsz, RW_N, RW_W), lambda i: (order(i), 0, 0, 0))
    blocks = pl.BlockSpec((SCAN_CHUNK, RW_HEADS * bsz, RW_N * bsz), lambda i: (order(i), 0, 0))
    return rows, per_value, states, blocks


def _mxu_operands(states):
    return [s.astype(MXU_DTYPE) for s in states]


def _removed(states_m, kk_t, ones, bsz):
    removed = _head_sum(_stack([states_m[b] * kk_t[b].astype(MXU_DTYPE) for b in range(bsz)]), ones)
    return [removed[b * RW_N:(b + 1) * RW_N] for b in range(bsz)]


def _advance(sp, rem, w_t, b_t, vk, bsz):
    return [sp[b] * w_t[b] - rem[b] * b_t[b] + vk[b] for b in range(bsz)]


def heads_to_rows(a):
    b, t, _ = a.shape
    return jnp.pad(a.astype(MXU_DTYPE).reshape(b, t, RW_HEADS, RW_N),
                   ((0, 0), (0, 0), (0, HEAD_ROWS_PAD - RW_HEADS), (0, 0)))


def _blocks_to_rows(raw_ref, first, row_ref, bsz):
    steps = pl.ds(first, SCAN_CHUNK)
    for b in range(bsz):
        for h in range(RW_HEADS):
            row_ref[b, :, h * RW_N:(h + 1) * RW_N] = raw_ref[steps, RW_HEADS * b + h, RW_N * b:RW_N * (b + 1)]


N_ROWS_FWD = 5
N_ROWS_BWD = 5


def _scan_consts(bsz):
    head = (jnp.arange(RW_W)[None, :] // RW_N == jnp.arange(RW_HEADS)[:, None]).astype(f32)
    return head, jnp.pad(head, ((0, HEAD_ROWS_PAD - RW_HEADS), (0, 0))), _block_ones(HALF_W, RW_N)


def _const_specs(consts):
    return [pl.BlockSpec(c.shape, lambda i: (0, 0)) for c in consts]


def rwkv_scan_fwd(rows_in, v_heads, orders, name):
    bsz, n_tok, _ = rows_in[0][0][0].shape
    n_ch = n_tok // SCAN_CHUNK
    rng = range(bsz)
    consts = _scan_consts(bsz)

    def body(*refs):
        rows = [refs[:N_ROWS_FWD], refs[N_ROWS_FWD:2 * N_ROWS_FWD]]
        (v0, v1, head_ref, pad_ref, ones_ref, y0, y1, h0, h1, f0, f1, m0, m1, s0, s1, late_ref,
         raw_ref) = refs[2 * N_ROWS_FWD:]
        v_refs, y_refs, hist_refs, final_refs, s_refs = (v0, v1), (y0, y1), (h0, h1), (f0, f1), (s0, s1)
        removed_refs = (m0, m1)
        n_blk = RW_HEADS * bsz
        head_v, pad_v, ones_v = head_ref[...], pad_ref[...], ones_ref[...]
        for d in range(2):
            @pl.when(pl.program_id(0) == 0)
            def _(d=d):
                s_refs[d][...] = jnp.zeros_like(s_refs[d])

        def step(j, carry):
            ts = [SCAN_CHUNK - 1 - j if reverse else j for reverse, _ in SCAN_DIRS]
            sps = [[s_refs[d][b] for b in rng] for d in range(2)]
            sps_m = [_mxu_operands(sps[d]) for d in range(2)]
            vks = [[_outer(v_refs[d][b, ts[d]], _row(rows[d][4], b, ts[d]), pad_v) for b in rng] for d in range(2)]
            rems = [_removed(sps_m[d], [_row(rows[d][1], b, ts[d]) for b in rng], ones_v, bsz) for d in range(2)]
            for d, (reverse, inclusive) in enumerate(SCAN_DIRS):
                r_ref = rows[d][0]
                read_at = jnp.maximum(j - 1, 0) if inclusive else ts[d]
                both = _read(sps_m[d], [_row(r_ref, b, read_at) for b in rng], head_v,
                             [_row(rows[d][1], b, ts[d]) for b in rng])
                if inclusive:
                    late_ref[j] = both[:n_blk]
                else:
                    raw_ref[ts[d]] = both[:n_blk]
                removed_refs[d][ts[d]] = both[n_blk:]
            for d in range(2):
                new = _advance(sps[d], rems[d], [_row(rows[d][2], b, ts[d]) for b in rng],
                               [_row(rows[d][3], b, ts[d]) for b in rng], vks[d], bsz)
                for b in rng:
                    hist_refs[d][ts[d], b] = sps_m[d][b]
                    s_refs[d][b] = new[b]
            return carry

        lax.fori_loop(0, SCAN_CHUNK, step, 0, unroll=SCAN_UNROLL)
        for d, (reverse, inclusive) in enumerate(SCAN_DIRS):
            final_refs[d][...] = s_refs[d][...]
            if inclusive:
                assert not reverse
                last = SCAN_CHUNK - 1
                late_ref[SCAN_CHUNK] = _read(_mxu_operands([s_refs[d][b] for b in rng]),
                                             [rows[d][0][b, last:last + 1, :] for b in rng], head_v)
                _blocks_to_rows(late_ref, 1, y_refs[d], bsz)
            else:
                _blocks_to_rows(raw_ref, 0, y_refs[d], bsz)

    specs = [_scan_specs(bsz, orders[d]) for d in range(2)]
    state = pltpu.VMEM((bsz, RW_N, RW_W), f32)
    late = pltpu.VMEM((SCAN_CHUNK + 1, RW_HEADS * bsz, RW_N * bsz), f32)
    raw = pltpu.VMEM((SCAN_CHUNK, RW_HEADS * bsz, RW_N * bsz), f32)
    final_spec = pl.BlockSpec((bsz, RW_N, RW_W), lambda i: (0, 0, 0))
    return pl.pallas_call(
        body, grid=(n_ch,),
        in_specs=[specs[d][0](col) for d in range(2) for _, col in rows_in[d]] + [specs[0][1], specs[1][1]]
        + _const_specs(consts),
        out_specs=[specs[0][0](), specs[1][0](), specs[0][2], specs[1][2], final_spec, final_spec,
                   specs[0][3], specs[1][3]],
        out_shape=[jax.ShapeDtypeStruct((bsz, n_tok, RW_W), f32)] * 2
        + [jax.ShapeDtypeStruct((n_tok, bsz, RW_N, RW_W), MXU_DTYPE)] * 2
        + [jax.ShapeDtypeStruct((bsz, RW_N, RW_W), f32)] * 2
        + [jax.ShapeDtypeStruct((n_tok, RW_HEADS * bsz, RW_N * bsz), f32)] * 2,
        scratch_shapes=[state, state, late, raw],
        compiler_params=_cparams(("arbitrary",)), name=name)(
        *[a for d in range(2) for a, _ in rows_in[d]], v_heads, v_heads, *consts)


def rwkv_scan_bwd(rows_in, v_heads, dy_heads, hists, finals, removed, orders, name):
    bsz, n_tok, _ = rows_in[0][0][0].shape
    n_ch = n_tok // SCAN_CHUNK
    backs = [functools.partial(lambda i, order: order(n_ch - 1 - i), order=orders[d]) for d in range(2)]
    rng = range(bsz)
    consts = _scan_consts(bsz)
    n_out, n_scr = 6, 6

    def body(*refs):
        rows = [refs[:N_ROWS_BWD], refs[N_ROWS_BWD:2 * N_ROWS_BWD]]
        rest = refs[2 * N_ROWS_BWD:]
        v_refs, dy_refs, hist_refs, final_refs, removed_refs = rest[0:2], rest[2:4], rest[4:6], rest[6:8], rest[8:10]
        head_ref, pad_ref, ones_ref = rest[10:13]
        outs = [rest[13:13 + n_out], rest[13 + n_out:13 + 2 * n_out]]
        scr = [rest[13 + 2 * n_out:13 + 2 * n_out + n_scr], rest[13 + 2 * n_out + n_scr:]]
        n_blk = RW_HEADS * bsz
        head_v, pad_v, ones_v = head_ref[...], pad_ref[...], ones_ref[...]
        for d in range(2):
            @pl.when(pl.program_id(0) == 0)
            def _(d=d):
                scr[d][1][...] = jnp.zeros_like(scr[d][1])
                scr[d][0][...] = final_refs[d][...]

        def step_of(j, reverse):
            return j if reverse else SCAN_CHUNK - 1 - j

        for d, (reverse, _) in enumerate(SCAN_DIRS):
            t0 = step_of(0, reverse)
            for b in rng:
                scr[d][3][b] = _outer(dy_refs[d][b, t0], rows[d][0][b, t0:t0 + 1, :], pad_v)

        def bstep(j, carry):
            ts = [step_of(j, reverse) for reverse, _ in SCAN_DIRS]
            reads = [[scr[d][3][b] for b in rng] for d in range(2)]
            dss = []
            for d, (_, inclusive) in enumerate(SCAN_DIRS):
                ds = [scr[d][1][b] for b in rng]
                dss.append([ds[b] + reads[d][b] for b in rng] if inclusive else ds)
            dss_m = [_mxu_operands(dss[d]) for d in range(2)]
            nexts = []
            for d, (reverse, _) in enumerate(SCAN_DIRS):
                t_next = step_of(jnp.minimum(j + 1, SCAN_CHUNK - 1), reverse)
                nexts.append([_outer(dy_refs[d][b, t_next], _row(rows[d][0], b, t_next), pad_v) for b in rng])
            drems = [_removed(dss_m[d], [-_row(rows[d][3], b, ts[d]) for b in rng], ones_v, bsz) for d in range(2)]
            for d in range(2):
                for b in rng:
                    scr[d][3][b] = nexts[d][b]
                both = _read(dss_m[d], [_row(rows[d][4], b, ts[d]) for b in rng], head_v,
                             [-_row(rows[d][3], b, ts[d]) for b in rng])
                scr[d][4][ts[d]] = both[:n_blk]
                scr[d][5][ts[d]] = both[n_blk:]
            for d, (_, inclusive) in enumerate(SCAN_DIRS):
                _, kk_ref, w_ref, _, _ = rows[d]
                _, ds_ref, dsh_ref = scr[d][:3]
                for b in rng:
                    dsh_ref[ts[d], b] = dss[d][b]
                    dsp = dss[d][b] * _row(w_ref, b, ts[d]) + drems[d][b] * _row(kk_ref, b, ts[d])
                    ds_ref[b] = dsp if inclusive else dsp + reads[d][b]
            return carry

        lax.fori_loop(0, SCAN_CHUNK, bstep, 0, unroll=SCAN_UNROLL)

        rsum = lambda z: jnp.sum(z, axis=0, keepdims=True)
        for d, (reverse, inclusive) in enumerate(SCAN_DIRS):
            dr_ref, dkk_ref, dw_ref, db_ref, dkt_ref, dv_ref = outs[d]
            after_ref, _, dsh_ref, _, dv_raw_ref, dremt_ref = scr[d]
            hist_ref, removed_ref = hist_refs[d], removed_refs[d]
            _blocks_to_rows(dv_raw_ref, 0, dv_ref, bsz)
            for t in range(SCAN_CHUNK):
                ts = slice(t, t + 1)
                after = t - 1 if reverse else t + 1
                for b in rng:
                    sp_m, ds = hist_ref[t, b], dsh_ref[t, b]
                    sp = sp_m.astype(f32)
                    if not inclusive:
                        seen = sp_m
                    else:
                        seen = hist_ref[after, b] if 0 <= after < SCAN_CHUNK else after_ref[b]
                    dr_ref[b, ts, :] = _row_from_heads(dy_refs[d][b, t], seen, pad_v)
                    dkt_ref[b, ts, :] = _row_from_heads(v_refs[d][b, t], ds, pad_v)
                    dw_ref[b, ts, :] = rsum(ds * sp)
                    db_ref[b, ts, :] = -_row_from_heads(_own_block(removed_ref[t], b), ds, pad_v)
                    dkk_ref[b, ts, :] = _row_from_heads(_own_block(dremt_ref[t], b), sp_m, pad_v)
            if inclusive:
                first = SCAN_CHUNK - 1 if reverse else 0
                for b in rng:
                    after_ref[b] = hist_ref[first, b].astype(f32)

    specs = [_scan_specs(bsz, backs[d]) for d in range(2)]
    hist = pltpu.VMEM((SCAN_CHUNK, bsz, RW_N, RW_W), f32)
    state = pltpu.VMEM((bsz, RW_N, RW_W), f32)
    final_spec = pl.BlockSpec((bsz, RW_N, RW_W), lambda i: (0, 0, 0))
    raw = pltpu.VMEM((SCAN_CHUNK, RW_HEADS * bsz, RW_N * bsz), f32)
    return pl.pallas_call(
        body, grid=(n_ch,),
        in_specs=[specs[d][0](col) for d in range(2) for _, col in rows_in[d]]
        + [specs[0][1], specs[1][1]] * 2 + [specs[0][2], specs[1][2], final_spec, final_spec, specs[0][3], specs[1][3]]
        + _const_specs(consts),
        out_specs=[specs[d][0]() for d in range(2) for _ in range(n_out)],
        out_shape=[jax.ShapeDtypeStruct((bsz, n_tok, RW_W), f32)] * (2 * n_out),
        scratch_shapes=[state, state, hist, state, raw, raw] * 2,
        compiler_params=_cparams(("arbitrary",)), name=name)(
        *[a for d in range(2) for a, _ in rows_in[d]], v_heads, v_heads, dy_heads, dy_heads, *hists, *finals, *removed, *consts)


MOD_NAMES = ("shift1", "scale1", "gate1", "shift2", "scale2", "gate2")


def _rope_tables(t_ctx, t_x):
    quarter = RET_DH // 4
    pos = jnp.arange(t_x)
    inv = jnp.power(ROPE_BASE, -jnp.arange(0, 2 * quarter, 2, dtype=f32) / (2 * quarter))
    ang_r = (pos // GRID_W).astype(f32)[:, None] * inv[None, :]
    ang_c = (pos % GRID_W).astype(f32)[:, None] * inv[None, :]
    cos = jnp.concatenate([jnp.cos(ang_r)] * 2 + [jnp.cos(ang_c)] * 2, axis=1)
    sin = jnp.concatenate([-jnp.sin(ang_r), jnp.sin(ang_r), -jnp.sin(ang_c), jnp.sin(ang_c)], axis=1)
    cos = jnp.concatenate([jnp.ones((t_ctx, RET_DH), f32), cos], axis=0)
    sin = jnp.concatenate([jnp.zeros((t_ctx, RET_DH), f32), sin], axis=0)
    return cos, sin


def _pad_rows(w, lo, total):
    return jnp.pad(w, ((lo, total - lo - w.shape[0]), (0, 0)))


LATE_WEIGHTS = ("w_out", "w_ff1", "w_ff2")


def layer_step(x, ctx, tgt, mod_x, mod_ctx, wt, first_weights=None, late_weights=None, early_grads=None,
               last_grads=None):
    bsz, t_x, _ = x.shape
    t_c = ctx.shape[1]
    t_all = t_c + t_x
    n_ct, n_xt = t_c // TOK_TILE, t_x // TOK_TILE
    n_t = n_ct + n_xt
    assert t_c % TOK_TILE == 0 and t_x % TOK_TILE == 0 and t_c % RET_CHUNK == 0

    seg = lambda i: (i >= n_ct).astype(jnp.int32)
    seg_first = lambda i: jnp.logical_or(i == 0, i == n_ct)
    seg_last = lambda i: jnp.logical_or(i == n_ct - 1, i == n_t - 1)
    mod_all = {n: jnp.stack([jnp.broadcast_to(mod_ctx[k], (bsz, D_MODEL)), mod_x[:, k]], axis=1)[:, :, None, :]
               for k, n in enumerate(MOD_NAMES)}
    mod_lat = {n: mod_x[:, k][:, None, None, :] for k, n in enumerate(MOD_NAMES)}
    both = lambda n: Seg(mod_all[n], seg, seg_first)
    lat = lambda n: Seg(mod_lat[n], lambda i: 0, lambda i: i == 0)
    flat = lambda a: a.reshape(-1, a.shape[-1])

    def chunk_orders(n_ctx_chunks, n_chunks):
        fwd = lambda i: i
        bwd = lambda i: jnp.where(i < n_ctx_chunks, n_ctx_chunks - 1 - i, n_chunks + n_ctx_chunks - 1 - i)
        return fwd, bwd

    ones64, ones128 = _block_ones(RW_W, RW_N), _block_ones(RET_W, RET_DH)
    cos, sin = _rope_tables(t_c, t_x)
    ld_rows = [jnp.pad(wt["ret_log_decay"][d][None, :], ((0, 0), (0, RET_DH - RET_HEADS))) for d in range(2)]
    row = lambda a, d: a[d][None, :]

    h = jnp.concatenate([ctx, x], axis=1)
    norm1_ins = lambda: [Tiled(h), both("shift1"), both("scale1"), Glob(wt["norm1_g"])]
    (n1,) = ew_forward(fn_norm_mod, "norm1", bsz, n_t, norm1_ins(), [(D_MODEL, MXU_DTYPE)])
    if first_weights is not None:
        wt = {**wt, **first_weights(n1)}
    w_up_pad = [_pad_rows(wt["rwkv_w_up"][d], 0, LORA_W) for d in range(2)]
    a_up_pad = [_pad_rows(wt["rwkv_a_up"][d], DECAY_LORA, LORA_W) for d in range(2)]
    g_up_pad = _pad_rows(wt["rwkv_g_up"], DECAY_LORA + AAA_LORA, LORA_W)
    px = matmul(flat(n1), wt["w_in"], "nn", "proj_in").reshape(bsz, t_all, IN_COLS)
    px_rw = px[..., RET_COLS:]
    ps = token_shift(px_rw, wt["rwkv_shift_mu"], seg_first, seg_last)

    def prep_ins():
        return [Tiled(ps, RW_W, 1), Tiled(ps, LORA_W, 3 * RW_W // LORA_W),
                Glob(row(wt["rwkv_w0"], 0)), Glob(row(wt["rwkv_w0"], 1)),
                Glob(row(wt["rwkv_a0"], 0)), Glob(row(wt["rwkv_a0"], 1)),
                Glob(w_up_pad[0]), Glob(w_up_pad[1]), Glob(a_up_pad[0]), Glob(a_up_pad[1]), Glob(g_up_pad),
                Glob(wt["rwkv_k_k"]), Glob(wt["rwkv_k_a"]), Glob(ones64)]

    kk, w_f, b_f, kt_f, w_b, b_b, kt_b, g_rw = ew_forward(fn_rwkv_prepare, "rwkv_prepare", bsz, n_t, prep_ins(),
                                                           [(RW_W, f32)] * 8)
    rw_order = chunk_orders(t_c // SCAN_CHUNK, t_all // SCAN_CHUNK)
    ret_order = chunk_orders(t_c // RET_CHUNK, t_all // RET_CHUNK)
    scan_rows = [[(ps, 0), (kk, 0), (w_f, 0), (b_f, 0), (kt_f, 0)], [(ps, 0), (kk, 0), (w_b, 0), (b_b, 0), (kt_b, 0)]]
    v_heads = heads_to_rows(ps[..., 2 * RW_W:3 * RW_W])
    y_f, y_b, *kept_states = rwkv_scan_fwd(scan_rows, v_heads, rw_order, "rwkv_scan_fwd")
    y = [y_f, y_b]
    o, ret_states = [], []
    for d in range(2):
        o_d, st_d = retention_fwd(px, cos, sin, ld_rows[d], ret_order[d], SCAN_DIRS[d][0], f"retention_fwd{d}")
        o.append(o_d), ret_states.append(st_d)

    def merge_ins(toff):
        return [Tiled(o[0], toff=toff), Tiled(o[1], toff=toff), Tiled(px, RET_W, 3, toff),
                Tiled(y[0], toff=toff), Tiled(y[1], toff=toff), Tiled(ps, RW_W, 0, toff), Tiled(kt_f, toff=toff),
                Tiled(ps, RW_W, 2, toff), Tiled(g_rw, toff=toff),
                Glob(wt["rwkv_r_k"]), Glob(wt["rwkv_ln_w"]), Glob(wt["rwkv_ln_b"]), Glob(ones64), Glob(ones128)]

    ret_out, rw_out = ew_forward(fn_merge, "merge_heads", bsz, n_xt, merge_ins(n_ct),
                                 [(RET_W, MXU_DTYPE), (RW_W, MXU_DTYPE)])
    merged = jnp.concatenate([ret_out, rw_out], axis=-1)
    if late_weights is not None:
        wt = {**wt, **late_weights(merged)}
    mix = matmul(flat(merged), wt["w_out"], "nn", "proj_out").reshape(bsz, t_x, D_MODEL)
    resid_ins = lambda: [Tiled(x), Tiled(mix), lat("gate1"), lat("shift2"), lat("scale2"), Glob(wt["norm2_g"])]
    h1, n2 = ew_forward(fn_resid_norm_mod, "resid_norm2", bsz, n_xt, resid_ins(), [(D_MODEL, f32), (D_MODEL, MXU_DTYPE)])
    act = matmul(flat(n2), wt["w_ff1"], "nn", "ff1", MXU_DTYPE, wt["b_ff1"], relu2).reshape(bsz, t_x, D_FF)
    ff = matmul(flat(act), wt["w_ff2"], "nn", "ff2").reshape(bsz, t_x, D_MODEL)

    g = {}
    loss, dh1, dff, dgate2, g["b_ff2"], g["final_g"] = loss_and_grads(
        h1, ff, tgt, mod_lat["gate2"], wt["b_ff2"], wt["final_g"], bsz, n_xt)
    dact = matmul(flat(dff), wt["w_ff2"], "nt", "ff2_dx", MXU_DTYPE).reshape(bsz, t_x, D_FF)
    g["w_ff2"] = matmul(flat(act), flat(dff), "tn", "ff2_dw", MXU_DTYPE)
    du, g["b_ff1"] = relu2_backward(act, dact, "relu2_bwd")
    dn2 = matmul(flat(du), wt["w_ff1"], "nt", "ff1_dx").reshape(bsz, t_x, D_MODEL)
    g["w_ff1"] = matmul(flat(n2), flat(du), "tn", "ff1_dw", MXU_DTYPE)
    dx_res, dmix, dgate1, dshift2, dscale2, g["norm2_g"] = ew_backward(
        fn_resid_norm_mod, "resid_norm2_bwd", bsz, n_xt, resid_ins(), [Tiled(dh1), Tiled(dn2)], [True] * 6,
        {1: MXU_DTYPE})
    dmerged = matmul(flat(dmix), wt["w_out"], "nt", "proj_out_dx").reshape(bsz, t_x, D_MODEL)
    g["w_out"] = matmul(flat(merged), flat(dmix), "tn", "proj_out_dw", MXU_DTYPE)
    if early_grads is not None:
        token = early_grads({n: g.pop(n) for n in LATE_WEIGHTS})
        wt = {**wt, "rwkv_r_k": wt["rwkv_r_k"] + token[:1, :1]}
    (do, dg_ret, dy, dr_m, dkt_m, dv_m, dg_rw, g["rwkv_r_k"], g["rwkv_ln_w"], g["rwkv_ln_b"]) = ew_backward(
        fn_merge, "merge_heads_bwd", bsz, n_xt, merge_ins(0),
        [Tiled(dmerged, RET_W, 0, -n_ct), Tiled(dmerged, RW_W, 1, -n_ct)],
        [True, False, True, True, False, True, True, True, True, True, True, True, False, False], lead=n_ct)

    dqkv, dld = [], []
    for d in range(2):
        *dqkv_d, dld_d = retention_bwd(do, px, ret_states[d], cos, sin, ld_rows[d], ret_order[d],
                                       SCAN_DIRS[d][0], f"retention_bwd{d}")
        dqkv.append(dqkv_d), dld.append(dld_d[0, :RET_HEADS])
    g["ret_log_decay"] = jnp.stack(dld)
    (dr_f, dkk_f, dw_f, db_f, dkt_f, dv_f, dr_b, dkk_b, dw_b, db_b, dkt_b, dv_b) = rwkv_scan_bwd(
        scan_rows, v_heads, heads_to_rows(dy), kept_states[:2], kept_states[2:4], kept_states[4:], rw_order, "rwkv_scan_bwd")
    prep_cts = [dkk_f + dkk_b, dw_f, db_f, dkt_f + dkt_m, dw_b, db_b, dkt_b, dg_rw]
    (dks, dlora, dw0_f, dw0_b, da0_f, da0_b, dwup_f, dwup_b, daup_f, daup_b, dgup, g["rwkv_k_k"],
     g["rwkv_k_a"]) = ew_backward(fn_rwkv_prepare, "rwkv_prepare_bwd", bsz, n_t, prep_ins(),
                                  [Tiled(c) for c in prep_cts], [True] * 13 + [False])
    g["rwkv_w0"] = jnp.concatenate([dw0_f, dw0_b], axis=0)
    g["rwkv_a0"] = jnp.concatenate([da0_f, da0_b], axis=0)
    g["rwkv_w_up"] = jnp.stack([dwup_f[:DECAY_LORA], dwup_b[:DECAY_LORA]])
    g["rwkv_a_up"] = jnp.stack([daup_f[DECAY_LORA:DECAY_LORA + AAA_LORA], daup_b[DECAY_LORA:DECAY_LORA + AAA_LORA]])
    g["rwkv_g_up"] = dgup[DECAY_LORA + AAA_LORA:]
    dps = jnp.concatenate([dr_f + dr_b + dr_m, dks, dv_f + dv_b + dv_m, dlora], axis=-1)
    dp_rw, g["rwkv_shift_mu"] = token_shift_bwd(dps, px_rw, wt["rwkv_shift_mu"], seg_first, seg_last)
    dpx = jnp.concatenate([(dqkv[0][k] + dqkv[1][k]).astype(MXU_DTYPE) for k in range(3)]
                          + [dg_ret.astype(MXU_DTYPE), dp_rw], axis=-1)
    dn1 = matmul(flat(dpx), wt["w_in"], "nt", "proj_in_dx").reshape(bsz, t_all, D_MODEL)
    g["w_in"] = matmul(flat(n1), flat(dpx), "tn", "proj_in_dw", MXU_DTYPE)
    if last_grads is not None:
        token = last_grads(g.pop("w_in"), {n: g.pop(n) for n in LAST_SHARDED})
        wt = {**wt, "norm1_g": wt["norm1_g"] + token[:1, :1]}
    dh, dshift1, dscale1, g["norm1_g"] = ew_backward(fn_norm_mod, "norm1_bwd", bsz, n_t, norm1_ins(), [Tiled(dn1)],
                                                     [True] * 4)
    grad_x = dh[:, t_c:] + dx_res
    zeros = jnp.zeros((D_MODEL,), f32)
    g["mod_x"] = jnp.stack([dshift1[:, 1, 0], dscale1[:, 1, 0], dgate1[:, 0, 0], dshift2[:, 0, 0], dscale2[:, 0, 0],
                            dgate2[:, 0, 0]], axis=1)
    g["mod_ctx"] = jnp.stack([dshift1[:, 0, 0].sum(0), dscale1[:, 0, 0].sum(0), zeros, zeros, zeros, zeros])
    return loss, grad_x, g


MESH_ID = pl.DeviceIdType.MESH
ALL_PEERS = [(dx, dy, dc) for dx in (0, 1) for dy in (0, 1) for dc in (0, 1)][1:]
CHIP_PEERS = [(1, 0, 0), (0, 1, 0), (1, 1, 0)]
CHIP_SLOTS = (0, 2, 4, 6)


def _mesh_pos():
    return lax.axis_index("x"), lax.axis_index("y"), lax.axis_index("c")


def _device_slot():
    x, y, c = _mesh_pos()
    return 4 * x + 2 * y + c


def sibling_swap(arrs, name):
    n = len(arrs)

    def body(*refs):
        in_refs, out_refs = refs[:n], refs[n:2 * n]
        send_sems, recv_sems = refs[2 * n:]
        x, y, c = _mesh_pos()
        copies = [pltpu.make_async_remote_copy(src_ref=in_refs[a], dst_ref=out_refs[a], send_sem=send_sems.at[a],
                                               recv_sem=recv_sems.at[a], device_id=(x, y, 1 - c),
                                               device_id_type=MESH_ID) for a in range(n)]
        for cp in copies:
            cp.start()
        for cp in copies:
            cp.wait()

    any_spec = pl.BlockSpec(memory_space=pl.ANY)
    res = pl.pallas_call(
        body, in_specs=[any_spec] * n, out_specs=[any_spec] * n,
        out_shape=[jax.ShapeDtypeStruct(a.shape, a.dtype) for a in arrs],
        scratch_shapes=[pltpu.SemaphoreType.DMA((n,)), pltpu.SemaphoreType.DMA((n,))],
        name=name)(*arrs)
    return list(res)


def exchange(arrs, gather, peers, name, by_chip=False, own=True):
    n, n_peers = len(arrs), len(peers)
    n_slots = N_SHARDS if by_chip else N_DEV
    slot = (lambda x, y, c: 2 * x + y) if by_chip else (lambda x, y, c: 4 * x + 2 * y + c)

    def body(*refs):
        in_refs, out_refs = refs[:n], refs[n:2 * n]
        send_sems, recv_sems, local_sems = refs[2 * n:]
        x, y, c = _mesh_pos()
        me = slot(x, y, c)
        copies, locals_ = [], []
        for a in range(n):
            if own:
                mine = in_refs[a] if gather else in_refs[a].at[me]
                loc = pltpu.make_async_copy(mine, out_refs[a].at[me], local_sems.at[a])
                loc.start()
                locals_.append(loc)
            for k, (dx, dy, dc) in enumerate(peers):
                peer = (1 - x if dx else x, 1 - y if dy else y, 1 - c if dc else c)
                src = in_refs[a] if gather else in_refs[a].at[slot(*peer)]
                sem = a * n_peers + k
                cp = pltpu.make_async_remote_copy(src_ref=src, dst_ref=out_refs[a].at[me], send_sem=send_sems.at[sem],
                                                  recv_sem=recv_sems.at[sem], device_id=peer, device_id_type=MESH_ID)
                cp.start()
                copies.append(cp)
        for cp in copies:
            cp.wait()
        for loc in locals_:
            loc.wait()

    any_spec = pl.BlockSpec(memory_space=pl.ANY)
    out_shape = [jax.ShapeDtypeStruct((n_slots,) + (a.shape if gather else a.shape[1:]), a.dtype) for a in arrs]
    n_sems = n * n_peers
    res = pl.pallas_call(
        body, in_specs=[any_spec] * n, out_specs=[any_spec] * n, out_shape=out_shape,
        scratch_shapes=[pltpu.SemaphoreType.DMA((n_sems,)), pltpu.SemaphoreType.DMA((n_sems,)),
                        pltpu.SemaphoreType.DMA((n,))],
        name=name)(*arrs)
    return list(res)


HBM_SPEC = pl.BlockSpec(memory_space=pltpu.HBM)
SEM_SPEC = pl.BlockSpec(memory_space=pltpu.SEMAPHORE)
DATAFLOW = pltpu.SideEffectType.DATAFLOW_SIDE_EFFECTING


def _peer_copies(src_refs, land_refs, send_sems, recv_sems, gather):
    x, y, c = _mesh_pos()
    me = 4 * x + 2 * y + c
    copies = []
    for a, (src_ref, land_ref) in enumerate(zip(src_refs, land_refs)):
        for k, (dx, dy, dc) in enumerate(ALL_PEERS):
            peer = (1 - x if dx else x, 1 - y if dy else y, 1 - c if dc else c)
            src = src_ref if gather else src_ref.at[4 * peer[0] + 2 * peer[1] + peer[2]]
            sem = a * len(ALL_PEERS) + k
            copies.append(pltpu.make_async_remote_copy(src_ref=src, dst_ref=land_ref.at[me], send_sem=send_sems.at[sem],
                                                       recv_sem=recv_sems.at[sem], device_id=peer,
                                                       device_id_type=MESH_ID))
    return copies


def exchange_start(arrs, gather, name):
    n = len(arrs)
    lands = [lax.empty((N_DEV,) + (a.shape if gather else a.shape[1:]), a.dtype) for a in arrs]

    def body(*refs):
        for cp in _peer_copies(refs[:n], refs[n:2 * n], refs[2 * n], refs[2 * n + 1], gather):
            cp.start()
        refs[-1][...] = jnp.zeros_like(refs[-1])

    sems = pltpu.SemaphoreType.DMA((n * len(ALL_PEERS),))
    hbm = [pltpu.HBM(a.shape, a.dtype) for a in arrs + lands]
    res = pl.pallas_call(
        body, name=name, out_shape=(sems, sems, *hbm, jax.ShapeDtypeStruct((8, 128), f32)),
        in_specs=[HBM_SPEC] * (2 * n),
        out_specs=(SEM_SPEC, SEM_SPEC, *[HBM_SPEC] * (2 * n), pl.BlockSpec(memory_space=pltpu.VMEM)),
        input_output_aliases={i: 2 + i for i in range(2 * n)},
        compiler_params=pltpu.CompilerParams(has_side_effects=DATAFLOW))(
        *[pltpu.with_memory_space_constraint(a, pltpu.HBM) for a in arrs + lands])
    return res[0], res[1], list(res[2:2 + n]), list(res[2 + n:2 + 2 * n]), res[-1]


def exchange_wait(started, after, gather, name):
    send_sems, recv_sems, srcs, lands, _ = started
    n = len(srcs)

    def body(*refs):
        for cp in _peer_copies(refs[:n], refs[n:2 * n], refs[2 * n], refs[2 * n + 1], gather):
            cp.wait_send()
            cp.wait_recv()

    res = pl.pallas_call(
        body, name=name, out_shape=tuple(pltpu.HBM(a.shape, a.dtype) for a in srcs + lands),
        in_specs=[HBM_SPEC] * (2 * n) + [SEM_SPEC, SEM_SPEC, pl.BlockSpec(memory_space=pl.ANY)],
        out_specs=tuple([HBM_SPEC] * (2 * n)), input_output_aliases={i: i for i in range(2 * n)},
        compiler_params=pltpu.CompilerParams(has_side_effects=DATAFLOW))(*srcs, *lands, send_sems, recv_sems, after)
    return list(res[n:])


def gather_two_level(arrs, name):
    n = len(arrs)
    per = 7

    def body(*refs):
        in_refs, out_refs = refs[:n], refs[n:2 * n]
        send_sems, recv_sems = refs[2 * n:]
        x, y, c = _mesh_pos()
        me, sibling = (x, y, c), (x, y, 1 - c)
        chips = [(1 - x, y), (x, 1 - y), (1 - x, 1 - y)]

        def copy(a, k, block, to, src=None):
            rows = out_refs[a].at[4 * block[0] + 2 * block[1] + block[2]]
            return pltpu.make_async_remote_copy(src_ref=rows if src is None else src, dst_ref=rows,
                                                send_sem=send_sems.at[a * per + k], recv_sem=recv_sems.at[a * per + k],
                                                device_id=to, device_id_type=MESH_ID)

        first, passed = [], []
        for a in range(n):
            first.append(copy(a, 0, me, sibling, src=in_refs[a]))
            first += [copy(a, 1 + j, me, (*chip, c), src=in_refs[a]) for j, chip in enumerate(chips)]
        for cp in first:
            cp.start()
        for a in range(n):
            for j, chip in enumerate(chips):
                copy(a, 1 + j, (*chip, c), me).wait_recv()
                fwd = copy(a, 4 + j, (*chip, c), sibling)
                fwd.start()
                passed.append(fwd)
        for a in range(n):
            copy(a, 0, sibling, me).wait_recv()
            for j, chip in enumerate(chips):
                copy(a, 4 + j, (*chip, 1 - c), me).wait_recv()
        for cp in first + passed:
            cp.wait_send()

    any_spec = pl.BlockSpec(memory_space=pl.ANY)
    res = pl.pallas_call(
        body, in_specs=[any_spec] * n, out_specs=[any_spec] * n,
        out_shape=[jax.ShapeDtypeStruct((N_DEV,) + a.shape, a.dtype) for a in arrs],
        scratch_shapes=[pltpu.SemaphoreType.DMA((n * per,)), pltpu.SemaphoreType.DMA((n * per,))],
        name=name)(*arrs)
    return list(res)


def sum_slots(parts, slots, name):
    _, r, c = parts.shape
    tr = r
    for cand in (512, 256, 128, 64, 32, 16, 8):
        if r % cand == 0 and cand * c * 4 * len(slots) <= 8 * 1024 * 1024:
            tr = cand
            break

    def body(p_ref, o_ref):
        acc = p_ref[slots[0]].astype(f32)
        for s in slots[1:]:
            acc = acc + p_ref[s].astype(f32)
        o_ref[...] = acc

    return pl.pallas_call(body, grid=(r // tr,), in_specs=[pl.BlockSpec((parts.shape[0], tr, c), lambda i: (0, i, 0))],
                          out_specs=pl.BlockSpec((tr, c), lambda i: (i, 0)),
                          out_shape=jax.ShapeDtypeStruct((r, c), f32),
                          compiler_params=_cparams(("parallel",)), name=name)(parts)


def column_sum(a, name):
    def body(a_ref, o_ref):
        o_ref[...] = jnp.sum(a_ref[...], axis=0, keepdims=True)

    return pl.pallas_call(body, out_shape=jax.ShapeDtypeStruct((1, a.shape[1]), f32), name=name)(a)


def adamw(w, g, m, v, name):
    r, c = w.shape
    tr = r
    for cand in (256, 128, 64, 32, 16, 8):
        if r % cand == 0:
            tr = cand
            break

    def body(w_ref, g_ref, m_ref, v_ref, d_ref, mo_ref, vo_ref):
        gv = g_ref[...]
        m_new = ADAM_B1 * m_ref[...] + (1.0 - ADAM_B1) * gv
        v_new = ADAM_B2 * v_ref[...] + (1.0 - ADAM_B2) * jnp.square(gv)
        m_hat = m_new / (1.0 - ADAM_B1 ** ADAM_STEP)
        v_hat = v_new / (1.0 - ADAM_B2 ** ADAM_STEP)
        d_ref[...] = -ADAM_LR * (m_hat / (jnp.sqrt(v_hat) + ADAM_EPS) + ADAM_WD * w_ref[...])
        mo_ref[...] = m_new
        vo_ref[...] = v_new

    spec = pl.BlockSpec((tr, c), lambda i: (i, 0))
    return pl.pallas_call(body, grid=(r // tr,), in_specs=[spec] * 4, out_specs=[spec] * 3,
                          out_shape=[jax.ShapeDtypeStruct((r, c), f32)] * 3,
                          compiler_params=_cparams(("parallel",)), name=name)(w, g, m, v)


def adaln_fwd(c_rows, w, b):
    def body(c_ref, w_ref, b_ref, o_ref):
        cv = c_ref[...]
        o_ref[...] = _mxu_dot(cv * jax.nn.sigmoid(cv), w_ref[...]) + b_ref[...]

    return pl.pallas_call(body, out_shape=jax.ShapeDtypeStruct((c_rows.shape[0], w.shape[1]), f32),
                          compiler_params=pltpu.CompilerParams(vmem_limit_bytes=VMEM_LIMIT), name="adaln_fwd")(c_rows, w, b)


def adaln_bwd(c_rows, dm, w):
    def body(c_ref, dm_ref, w_ref, gw_ref, ds_ref):
        cv = c_ref[...]
        gw_ref[...] = _dg(cv * jax.nn.sigmoid(cv), dm_ref[...], 0, 0)
        ds_ref[...] = _dg(dm_ref[...], w_ref[...], 1, 1)

    return pl.pallas_call(body, out_shape=[jax.ShapeDtypeStruct(w.shape, f32),
                                           jax.ShapeDtypeStruct(c_rows.shape, f32)],
                          compiler_params=pltpu.CompilerParams(vmem_limit_bytes=VMEM_LIMIT), name="adaln_bwd")(c_rows, dm, w)


def c_ctx_grad(parts, c_ctx_row):
    def body(p_ref, c_ref, o_ref):
        total = p_ref[0, 0:1, :]
        for s in range(1, N_SHARDS):
            total = total + p_ref[s, 0:1, :]
        _, vjp = jax.vjp(jax.nn.silu, c_ref[...])
        o_ref[...] = vjp(total)[0]

    return pl.pallas_call(body, out_shape=jax.ShapeDtypeStruct((1, D_MODEL), f32), name="c_ctx_grad")(parts, c_ctx_row)


PACK_W = 1024
PACK_ROWS = 8


def _pack(arrs):
    pieces, layout, r0 = [], [], 0
    for a in arrs:
        size = math.prod(a.shape)
        rows = -(-size // (PACK_W * PACK_ROWS)) * PACK_ROWS
        pieces.append(jnp.pad(a.reshape(-1).astype(f32), (0, rows * PACK_W - size)).reshape(rows, PACK_W))
        layout.append((r0, rows, a.shape))
        r0 += rows
    return jnp.concatenate(pieces, axis=0), layout


def _unpack(pack, layout, lead=()):
    n_lead = len(lead)
    outs = []
    for r0, rows, shape in layout:
        piece = pack[(slice(None),) * n_lead + (slice(r0, r0 + rows),)].reshape(lead + (-1,))
        outs.append(piece[..., :math.prod(shape)].reshape(lead + tuple(shape)))
    return outs


W_NAMES = ("c_ctx", "w_ada", "b_ada", "norm1_g", "norm2_g", "w_in", "ret_log_decay", "rwkv_shift_mu", "rwkv_w0",
           "rwkv_w_up", "rwkv_a0", "rwkv_a_up", "rwkv_g_up", "rwkv_k_k", "rwkv_k_a", "rwkv_r_k", "rwkv_ln_w",
           "rwkv_ln_b", "w_out", "w_ff1", "b_ff1", "w_ff2", "b_ff2", "final_g")
COL_SHARDED = ("w_in", "w_ff1")
ROW_SHARDED = ("w_out", "w_ff2")
LAST_SHARDED = ("rwkv_shift_mu", "rwkv_w0", "rwkv_w_up", "rwkv_a0", "rwkv_a_up", "rwkv_g_up")
REPLICATED = ("c_ctx", "b_ada", "norm1_g", "norm2_g", "ret_log_decay", "rwkv_k_k", "rwkv_k_a", "rwkv_r_k",
              "rwkv_ln_w", "rwkv_ln_b", "b_ff1", "b_ff2", "final_g")
N_SHARDS = 4


def _train_step(a):
    x, c, ctx, tgt = a["x"], a["c"], a["ctx"], a["loss_target"]
    bsz = x.shape[0]
    mx, my, mc = _mesh_pos()
    shard = 2 * mx + my
    dev = _device_slot()

    def own_half(n):
        w = a[n][0].astype(MXU_DTYPE)
        return lax.dynamic_slice_in_dim(w, mc * (w.shape[0] // 2), w.shape[0] // 2, axis=0)

    def whole_weight(n, gth, own):
        per_chip = lax.dynamic_update_index_in_dim(gth, own, dev, 0).reshape(N_SHARDS, -1, gth.shape[-1])
        return (per_chip.transpose(1, 0, 2).reshape(per_chip.shape[1], -1) if n in COL_SHARDED
                else per_chip.reshape(-1, per_chip.shape[-1]))

    small_pack, small_layout = _pack([a[n][0] for n in LAST_SHARDED])
    first_own = [own_half("w_in"), small_pack]
    first_started = exchange_start(first_own, True, "gather_first_start")
    late_own = [own_half(n) for n in LATE_WEIGHTS]
    late_started = exchange_start(late_own, True, "gather_late_start")
    started = first_started[-1][0, 0] + late_started[-1][0, 0]

    (c_all,) = exchange([jnp.pad(c, ((0, PACK_ROWS - bsz), (0, 0))) + started], True, ALL_PEERS, "gather_c")
    n_ex = N_DEV * bsz
    c_rows = jnp.concatenate([c_all[:, :bsz].reshape(n_ex, D_MODEL), a["c_ctx"][None, :],
                              jnp.zeros((PACK_ROWS - 1, D_MODEL), f32)], axis=0)
    ada_cols = a["w_ada"].shape[-1]
    b_ada_cols = lax.dynamic_slice_in_dim(a["b_ada"], shard * ada_cols, ada_cols, axis=1)
    mod_cols = adaln_fwd(c_rows, a["w_ada"][0], b_ada_cols)
    (mod_got,) = gather_two_level([mod_cols], "gather_mod")
    mod_own = lax.dynamic_update_index_in_dim(mod_got, mod_cols, dev, 0)
    mod_all = jnp.stack([mod_own[s] for s in CHIP_SLOTS], axis=1).reshape(c_rows.shape[0], -1)
    mod_x = lax.dynamic_slice_in_dim(mod_all, dev * bsz, bsz, axis=0).reshape(bsz, 6, D_MODEL)
    mod_ctx = mod_all[n_ex].reshape(6, D_MODEL)
    wt = {}

    def first_weights(after):
        w_in_land, small_land = exchange_wait(first_started, after, True, "gather_first_wait")
        small_own = lax.dynamic_update_index_in_dim(small_land, small_pack, dev, 0)
        small_by_chip = _unpack(jnp.stack([small_own[s] for s in CHIP_SLOTS]), small_layout, (N_SHARDS,))
        got = {n: jnp.concatenate([parts[s] for s in range(N_SHARDS)], axis=-1)
               for n, parts in zip(LAST_SHARDED, small_by_chip)}
        got["w_in"] = whole_weight("w_in", w_in_land, first_own[0])
        return got

    def late_weights(after):
        lands = exchange_wait(late_started, after, True, "gather_late_wait")
        return {n: whole_weight(n, land, own) for n, land, own in zip(LATE_WEIGHTS, lands, late_own)}

    def grad_blocks(n, gw):
        if n in COL_SHARDED:
            gw = gw.reshape(gw.shape[0], N_SHARDS, -1).transpose(1, 0, 2)
        return gw.reshape(N_DEV, -1, gw.shape[-1]).astype(MXU_DTYPE)

    late_sent, last_sent = {}, {}

    def early_grads(late_g):
        late_sent["blocks"] = [grad_blocks(n, late_g[n]) for n in LATE_WEIGHTS]
        late_sent["started"] = exchange_start(late_sent["blocks"], False, "scatter_late_start")
        return late_sent["started"][-1]

    def last_grads(g_w_in, g_small):
        shard_packs = []
        for s in range(N_SHARDS):
            pieces_s = [lax.slice_in_dim(g_small[n], s * a[n].shape[-1], (s + 1) * a[n].shape[-1],
                                         axis=g_small[n].ndim - 1) for n in LAST_SHARDED]
            pack_s, last_sent["layout"] = _pack(pieces_s)
            shard_packs.append(jnp.pad(pack_s, ((0, -pack_s.shape[0] % (2 * PACK_ROWS)), (0, 0))))
        last_sent["blocks"] = [grad_blocks("w_in", g_w_in), jnp.stack(shard_packs).reshape(N_DEV, -1, PACK_W)]
        last_sent["started"] = exchange_start(last_sent["blocks"], False, "scatter_last_start")
        return last_sent["started"][-1]

    for n in ("norm1_g", "norm2_g", "rwkv_k_k", "rwkv_k_a", "rwkv_r_k", "rwkv_ln_w", "rwkv_ln_b", "b_ff1", "b_ff2"):
        wt[n] = a[n]
    wt["ret_log_decay"] = a["ret_log_decay"][0]
    wt["final_g"] = a["final_g"][None, :]

    loss, grad_x, g = layer_step(x, ctx, tgt, mod_x, mod_ctx, wt, first_weights, late_weights, early_grads, last_grads)

    small_names = [n for n in REPLICATED if n not in ("c_ctx", "b_ada")]
    g_pack, g_layout = _pack([jnp.pad(loss, ((0, 0), (0, PACK_W - loss.shape[1])))] + [g[n] for n in small_names]
                             + [g["mod_x"], g["mod_ctx"]])
    (g_packs,) = gather_two_level([g_pack], "gather_small_grads")
    g_packs = lax.dynamic_update_index_in_dim(g_packs, g_pack, dev, 0)
    g_sum = _unpack(sum_slots(g_packs, tuple(range(N_DEV)), "sum_small_grads"), g_layout)
    loss_total = g_sum[0][0, 0]
    grads = dict(zip(small_names, g_sum[1:1 + len(small_names)]))
    dmod_ctx = g_sum[-1].reshape(1, -1)
    dmod_x = _unpack(g_packs, g_layout, (N_DEV,))[-2].reshape(n_ex, -1)
    dmod = jnp.concatenate([dmod_x, dmod_ctx, jnp.zeros((PACK_ROWS - 1, dmod_x.shape[1]), f32)], axis=0)
    grads["b_ada"] = column_sum(dmod, "b_ada_grad")
    dmod_cols = lax.dynamic_slice_in_dim(dmod, shard * ada_cols, ada_cols, axis=1)
    grads["w_ada"], dsilu = adaln_bwd(c_rows, dmod_cols, a["w_ada"][0])

    dsilu_rows = jnp.broadcast_to(jnp.pad(dsilu[n_ex:n_ex + 1], ((0, PACK_ROWS - 1), (0, 0)))[None],
                                  (N_SHARDS, PACK_ROWS, D_MODEL))
    (shares,) = exchange([dsilu_rows], False, CHIP_PEERS, "share_c_ctx_grad", by_chip=True, own=False)
    shares = lax.dynamic_update_index_in_dim(shares, dsilu_rows[0], shard, 0)
    grads["c_ctx"] = c_ctx_grad(shares, a["c_ctx"][None, :])

    scattered, half_sums = ("w_in", "small_shards") + LATE_WEIGHTS, []
    for sent, wait_name, after in ((last_sent, "scatter_last_wait", grads["c_ctx"]),
                                   (late_sent, "scatter_late_wait", grads["c_ctx"])):
        for land, block in zip(exchange_wait(sent["started"], after, False, wait_name), sent["blocks"]):
            land = lax.dynamic_update_index_in_dim(land, lax.dynamic_index_in_dim(block, dev, 0, keepdims=False), dev, 0)
            half_sums.append(sum_slots(land, tuple(range(N_DEV)), f"sum_{scattered[len(half_sums)]}"))
    other_halves = sibling_swap(half_sums, "swap_halves")
    for n, mine, other in zip(scattered, half_sums, other_halves):
        rows = mine.shape[0]
        whole = jnp.zeros((2 * rows, mine.shape[1]), f32)
        whole = lax.dynamic_update_slice_in_dim(whole, mine, mc * rows, axis=0)
        grads[n] = lax.dynamic_update_slice_in_dim(whole, other, (1 - mc) * rows, axis=0)
    grads.update(zip(LAST_SHARDED, _unpack(grads.pop("small_shards"), last_sent["layout"])))

    out_g, out_d, out_m, out_v = {}, {}, {}, {}
    for n in ("w_ada",) + COL_SHARDED + ROW_SHARDED:
        out_g[n] = grads[n].reshape(a[n].shape)
        two_d = lambda z: z.reshape(-1, z.shape[-1])
        d, m, v = adamw(two_d(a[n]), two_d(out_g[n]), two_d(a["m_" + n]), two_d(a["v_" + n]), f"adamw_{n}")
        out_d[n], out_m[n], out_v[n] = d.reshape(a[n].shape), m.reshape(a[n].shape), v.reshape(a[n].shape)
    rest = REPLICATED + LAST_SHARDED
    for n in rest:
        out_g[n] = grads[n].reshape(a[n].shape)
    packs = [_pack([src[n] for n in rest])[0] for src in
             ({n: a[n] for n in rest}, out_g, {n: a["m_" + n] for n in rest}, {n: a["v_" + n] for n in rest})]
    _, rest_layout = _pack([a[n] for n in rest])
    for dst, pack in zip((out_d, out_m, out_v), adamw(*packs, "adamw_small")):
        dst.update(zip(rest, _unpack(pack, rest_layout)))
    return (loss_total, grad_x, *[out_g[n] for n in W_NAMES], *[out_d[n] for n in W_NAMES],
            *[out_m[n] for n in W_NAMES], *[out_v[n] for n in W_NAMES])


def kernel(x, c, ctx, c_ctx, w_ada, b_ada, norm1_g, norm2_g, w_in, ret_log_decay, rwkv_shift_mu, rwkv_w0, rwkv_w_up, rwkv_a0, rwkv_a_up, rwkv_g_up, rwkv_k_k, rwkv_k_a, rwkv_r_k, rwkv_ln_w, rwkv_ln_b, w_out, w_ff1, b_ff1, w_ff2, b_ff2, final_g, loss_target, m_c_ctx, m_w_ada, m_b_ada, m_norm1_g, m_norm2_g, m_w_in, m_ret_log_decay, m_rwkv_shift_mu, m_rwkv_w0, m_rwkv_w_up, m_rwkv_a0, m_rwkv_a_up, m_rwkv_g_up, m_rwkv_k_k, m_rwkv_k_a, m_rwkv_r_k, m_rwkv_ln_w, m_rwkv_ln_b, m_w_out, m_w_ff1, m_b_ff1, m_w_ff2, m_b_ff2, m_final_g, v_c_ctx, v_w_ada, v_b_ada, v_norm1_g, v_norm2_g, v_w_in, v_ret_log_decay, v_rwkv_shift_mu, v_rwkv_w0, v_rwkv_w_up, v_rwkv_a0, v_rwkv_a_up, v_rwkv_g_up, v_rwkv_k_k, v_rwkv_k_a, v_rwkv_r_k, v_rwkv_ln_w, v_rwkv_ln_b, v_w_out, v_w_ff1, v_b_ff1, v_w_ff2, v_b_ff2, v_final_g):
    return _train_step(dict(locals()))
```

```python
import functools
import math

import jax
import jax.numpy as jnp
from jax import lax
from jax.experimental import pallas as pl
from jax.experimental.pallas import tpu as pltpu

f32 = jnp.float32
MXU_DTYPE = jnp.bfloat16

D_MODEL = 1024
RET_W = 512
RET_HEADS = 4
RET_DH = 128
RET_CHUNK = 128
RW_W = 512
RW_N = 64
DECAY_LORA = 64
AAA_LORA = 64
GATE_LORA = 128
LORA_W = DECAY_LORA + AAA_LORA + GATE_LORA
D_FF = 4096
RET_COLS = 4 * RET_W
SHIFT_COLS = 3 * RW_W + LORA_W
IN_COLS = RET_COLS + SHIFT_COLS
GRID_W = 64
ROPE_BASE = 10000.0
NORM_EPS = 1e-6
GN_EPS = 64e-5
W_DECAY_SCALE = math.exp(-0.5)
ADAM_LR, ADAM_B1, ADAM_B2, ADAM_EPS, ADAM_WD, ADAM_STEP = 0.001, 0.9, 0.999, 1e-08, 0.01, 10

TOK_TILE = 256
MATMUL_TILE = 1024
SCAN_CHUNK = 32
SCAN_UNROLL = SCAN_CHUNK
N_DEV = 8
V7X_VMEM_BYTES = 64 * 1024 * 1024
VMEM_LIMIT = V7X_VMEM_BYTES * 7 // 8


def _cparams(sem):
    return pltpu.CompilerParams(dimension_semantics=sem, vmem_limit_bytes=VMEM_LIMIT)


def _tile(n, cap):
    best = None
    for t in range(128, min(n, cap) + 1, 128):
        if n % t == 0:
            best = t
    return best if best is not None else n


def matmul(a, b, mode, name, out_dtype=f32, bias=None, finish=None):
    if mode == "nn":
        (m, k), (k2, n) = a.shape, b.shape
    elif mode == "nt":
        (m, k), (n, k2) = a.shape, b.shape
    else:
        (k, m), (k2, n) = a.shape, b.shape
    assert k == k2, (a.shape, b.shape, mode)
    tm, tn, tk = _tile(m, MATMUL_TILE), _tile(n, MATMUL_TILE), _tile(k, MATMUL_TILE)
    nk = k // tk
    dims = {"nn": ((1,), (0,)), "nt": ((1,), (1,)), "tn": ((0,), (0,))}[mode]

    def body(a_ref, b_ref, *rest):
        o_ref, acc_ref = rest[-2:]
        kk = pl.program_id(2)

        @pl.when(kk == 0)
        def _():
            acc_ref[...] = jnp.zeros_like(acc_ref)

        acc_ref[...] += lax.dot_general(a_ref[...].astype(MXU_DTYPE), b_ref[...].astype(MXU_DTYPE),
                                        (dims, ((), ())), preferred_element_type=f32)

        @pl.when(kk == nk - 1)
        def _():
            res = acc_ref[...]
            if bias is not None:
                res = res + rest[0][...]
            if finish is not None:
                res = finish(res)
            o_ref[...] = res.astype(o_ref.dtype)

    if mode == "nn":
        a_spec = pl.BlockSpec((tm, tk), lambda i, j, q: (i, q))
        b_spec = pl.BlockSpec((tk, tn), lambda i, j, q: (q, j))
    elif mode == "nt":
        a_spec = pl.BlockSpec((tm, tk), lambda i, j, q: (i, q))
        b_spec = pl.BlockSpec((tn, tk), lambda i, j, q: (j, q))
    else:
        a_spec = pl.BlockSpec((tk, tm), lambda i, j, q: (q, i))
        b_spec = pl.BlockSpec((tk, tn), lambda i, j, q: (q, j))
    extra_specs = [] if bias is None else [pl.BlockSpec((1, tn), lambda i, j, q: (0, j))]
    extra = [] if bias is None else [bias]
    return pl.pallas_call(
        body, grid=(m // tm, n // tn, nk), in_specs=[a_spec, b_spec] + extra_specs,
        out_specs=pl.BlockSpec((tm, tn), lambda i, j, q: (i, j)),
        out_shape=jax.ShapeDtypeStruct((m, n), out_dtype),
        scratch_shapes=[pltpu.VMEM((tm, tn), f32)],
        compiler_params=_cparams(("parallel", "parallel", "arbitrary")), name=name)(a, b, *extra)


class Tiled:
    def __init__(self, arr, w=None, cidx=0, toff=0):
        self.arr, self.w, self.cidx, self.toff = arr, (arr.shape[-1] if w is None else w), cidx, toff

    def spec(self):
        cidx, toff = self.cidx, self.toff
        return pl.BlockSpec((None, TOK_TILE, self.w), lambda b, i: (b, jnp.maximum(i + toff, 0), cidx))


class Seg:
    def __init__(self, arr, seg, first):
        self.arr, self.seg, self.first = arr, seg, first

    def spec(self):
        seg = self.seg
        return pl.BlockSpec((None, None, 1, self.arr.shape[-1]), lambda b, i: (b, seg(i), 0, 0))


class Glob:
    def __init__(self, arr):
        self.arr = arr

    def spec(self):
        return pl.BlockSpec(self.arr.shape, lambda b, i: (0,) * self.arr.ndim)


def ew_forward(fn, name, bsz, n_tiles, ins, outs):
    n_in = len(ins)

    def body(*refs):
        res = fn(*[r[...] for r in refs[:n_in]])
        for o_ref, o in zip(refs[n_in:], res):
            o_ref[...] = o.astype(o_ref.dtype)

    out_specs = [pl.BlockSpec((None, TOK_TILE, w), lambda b, i: (b, i, 0)) for w, _ in outs]
    out_shape = [jax.ShapeDtypeStruct((bsz, n_tiles * TOK_TILE, w), dt) for w, dt in outs]
    return pl.pallas_call(body, grid=(bsz, n_tiles), in_specs=[d.spec() for d in ins], out_specs=out_specs,
                          out_shape=out_shape, compiler_params=_cparams(("parallel", "parallel")), name=name)(
        *[d.arr for d in ins])


def ew_backward(fn, name, bsz, n_tiles, ins, cts, want, grad_dtypes=None, lead=0):
    n_in, n_ct = len(ins), len(cts)
    diff = [k for k in range(n_in) if want[k]]
    grad_dtypes = grad_dtypes or {}
    assert lead == 0 or not any(isinstance(ins[k], Seg) for k in diff)

    def body(*refs):
        b, i = pl.program_id(0), pl.program_id(1)
        g_refs = refs[n_in + n_ct:]

        def tile_grads():
            vals = [r[...] for r in refs[:n_in]]
            ct_vals = tuple(r[...].astype(f32) for r in refs[n_in:n_in + n_ct])

            def f(*dvals):
                full = list(vals)
                for k, v in zip(diff, dvals):
                    full[k] = v
                return tuple(fn(*full))

            _, vjp = jax.vjp(f, *[vals[k] for k in diff])
            grads = vjp(ct_vals)
            for k, g_ref, g in zip(diff, g_refs, grads):
                d = ins[k]
                if isinstance(d, Tiled):
                    g_ref[...] = g.astype(g_ref.dtype)
                else:
                    zero = d.first(i) if isinstance(d, Seg) else jnp.logical_and(b == 0, i == lead)

                    @pl.when(zero)
                    def _(g_ref=g_ref):
                        g_ref[...] = jnp.zeros_like(g_ref)

                    g_ref[...] += g

        if lead == 0:
            tile_grads()
        else:
            pl.when(i >= lead)(tile_grads)

            @pl.when(i < lead)
            def _():
                for k, g_ref in zip(diff, g_refs):
                    if isinstance(ins[k], Tiled):
                        g_ref[...] = jnp.zeros_like(g_ref)

    out_specs, out_shape = [], []
    for k in diff:
        d = ins[k]
        if isinstance(d, Tiled):
            out_specs.append(pl.BlockSpec((None, TOK_TILE, d.w), lambda b, i: (b, i, 0)))
            out_shape.append(jax.ShapeDtypeStruct((bsz, (n_tiles + lead) * TOK_TILE, d.w), grad_dtypes.get(k, f32)))
        else:
            out_specs.append(d.spec())
            out_shape.append(jax.ShapeDtypeStruct(d.arr.shape, f32))
    return pl.pallas_call(body, grid=(bsz, n_tiles + lead),
                          in_specs=[d.spec() for d in ins] + [c.spec() for c in cts],
                          out_specs=out_specs, out_shape=out_shape,
                          compiler_params=_cparams(("arbitrary", "arbitrary")), name=name)(
        *[d.arr for d in ins], *[c.arr for c in cts])


@jax.custom_vjp
def _mxu_dot(a, b):
    return jnp.dot(a.astype(MXU_DTYPE), b.astype(MXU_DTYPE), preferred_element_type=f32)


def _mxu_dot_fwd(a, b):
    return _mxu_dot(a, b), (a, b)


def _mxu_dot_bwd(res, ct):
    a, b = res
    ct = ct.astype(MXU_DTYPE)
    da = lax.dot_general(ct, b.astype(MXU_DTYPE), (((1,), (1,)), ((), ())), preferred_element_type=f32)
    db = lax.dot_general(a.astype(MXU_DTYPE), ct, (((0,), (0,)), ((), ())), preferred_element_type=f32)
    return da, db


_mxu_dot.defvjp(_mxu_dot_fwd, _mxu_dot_bwd)


def _split_dot_impl(x, ones_mat):
    hi = x.astype(MXU_DTYPE)
    lo = (x - hi.astype(f32)).astype(MXU_DTYPE)
    return jnp.dot(hi, ones_mat, preferred_element_type=f32) + jnp.dot(lo, ones_mat, preferred_element_type=f32)


@jax.custom_vjp
def _split_dot(x, ones_mat):
    return _split_dot_impl(x, ones_mat)


def _split_dot_fwd(x, ones_mat):
    return _split_dot_impl(x, ones_mat), ones_mat


def _split_dot_bwd(ones_mat, ct):
    return _split_dot_impl(ct, ones_mat), None


_split_dot.defvjp(_split_dot_fwd, _split_dot_bwd)


def _block_ones(n, group):
    idx = jnp.arange(n) // group
    return (idx[:, None] == idx[None, :]).astype(MXU_DTYPE)


def _rms(x, g):
    return x * lax.rsqrt(jnp.mean(x * x, axis=-1, keepdims=True) + NORM_EPS) * g


def fn_norm_mod(h, shift, scale, g):
    return (_rms(h, g) * (1.0 + scale) + shift,)


def fn_rwkv_prepare(ks, lora, w0_f, w0_b, a0_f, a0_b, w_up_f, w_up_b, a_up_f, a_up_b, g_up, k_k, k_a, ones64):
    kkr = ks * k_k
    kk = kkr * lax.rsqrt(_split_dot(kkr * kkr, ones64) + 1e-12)
    outs = [kk]
    th = jnp.tanh(lora)
    for w0, a0, w_up, a_up in ((w0_f, a0_f, w_up_f, a_up_f), (w0_b, a0_b, w_up_b, a_up_b)):
        w = jnp.exp(-W_DECAY_SCALE * jax.nn.sigmoid(w0 + _mxu_dot(th, w_up)))
        a = jax.nn.sigmoid(a0 + _mxu_dot(lora, a_up))
        kt = ks * (1.0 + (a - 1.0) * k_a)
        outs += [w, a * kk, kt]
    outs.append(_mxu_dot(jax.nn.sigmoid(lora), g_up))
    return tuple(outs)


def fn_merge(o_f, o_b, g_ret, y_f, y_b, r, kt_f, v, g_rw, r_k, ln_w, ln_b, ones64, ones128):
    o = o_f + o_b
    ret = o * lax.rsqrt(_split_dot(o * o, ones128) * (1.0 / RET_DH) + NORM_EPS) * (g_ret * jax.nn.sigmoid(g_ret))
    y = y_f + y_b
    mean = _split_dot(y, ones64) * (1.0 / RW_N)
    yc = y - mean
    var = _split_dot(yc * yc, ones64) * (1.0 / RW_N)
    y_n = yc * lax.rsqrt(var + GN_EPS) * ln_w + ln_b
    bonus = _split_dot(r * kt_f * r_k, ones64) * v
    return ret, (y_n + bonus) * g_rw


def fn_resid_norm_mod(x, mix, gate, shift, scale, g):
    h1 = x + gate * mix
    return h1, _rms(h1, g) * (1.0 + scale) + shift


def relu2(z):
    return jnp.square(jnp.maximum(z, 0.0))


def relu2_backward(act, dact, name):
    bsz, n_tok, width = act.shape

    def body(a_ref, d_ref, du_ref, db_ref):
        du = d_ref[...].astype(f32) * (2.0 * jnp.sqrt(a_ref[...].astype(f32)))
        du_ref[...] = du.astype(du_ref.dtype)

        @pl.when(jnp.logical_and(pl.program_id(0) == 0, pl.program_id(1) == 0))
        def _():
            db_ref[...] = jnp.zeros_like(db_ref)

        db_ref[...] += jnp.sum(du, axis=0, keepdims=True)

    tile = pl.BlockSpec((None, TOK_TILE, width), lambda b, i: (b, i, 0))
    row = pl.BlockSpec((1, width), lambda b, i: (0, 0))
    return pl.pallas_call(body, grid=(bsz, n_tok // TOK_TILE), in_specs=[tile, tile], out_specs=[tile, row],
                          out_shape=[jax.ShapeDtypeStruct(act.shape, MXU_DTYPE), jax.ShapeDtypeStruct((1, width), f32)],
                          compiler_params=_cparams(("arbitrary", "arbitrary")), name=name)(act, dact)


def fn_loss(h1, f, tgt, gate, b2, g):
    y = _rms(h1 + gate * (f + b2), g)
    err = jnp.square(y - tgt)
    return 0.5 * jnp.sum(jnp.mean(err, axis=-1, keepdims=True), axis=0, keepdims=True)


def loss_and_grads(h1, f, tgt, gate, b2, g, bsz, n_tiles):
    def body(h1_ref, f_ref, t_ref, gate_ref, b2_ref, g_ref, loss_ref, dh1_ref, df_ref, dgate_ref, db2_ref, dg_ref):
        b, i = pl.program_id(0), pl.program_id(1)
        tgt_v = t_ref[...]
        loss, vjp = jax.vjp(lambda a, c, e, p, q: fn_loss(a, c, tgt_v, e, p, q),
                            h1_ref[...], f_ref[...], gate_ref[...], b2_ref[...], g_ref[...])
        dh1, df, dgate, db2, dg = vjp(jnp.ones((1, 1), f32))
        dh1_ref[...] = dh1
        df_ref[...] = df.astype(df_ref.dtype)

        @pl.when(i == 0)
        def _():
            dgate_ref[...] = jnp.zeros_like(dgate_ref)

        @pl.when(jnp.logical_and(b == 0, i == 0))
        def _():
            loss_ref[...] = jnp.zeros_like(loss_ref)
            db2_ref[...] = jnp.zeros_like(db2_ref)
            dg_ref[...] = jnp.zeros_like(dg_ref)

        dgate_ref[...] += dgate
        db2_ref[...] += db2
        dg_ref[...] += dg
        loss_ref[...] += jnp.broadcast_to(loss, loss_ref.shape)

    tile = pl.BlockSpec((None, TOK_TILE, D_MODEL), lambda b, i: (b, i, 0))
    row = pl.BlockSpec((1, D_MODEL), lambda b, i: (0, 0))
    seg = pl.BlockSpec((None, None, 1, D_MODEL), lambda b, i: (b, 0, 0, 0))
    t_tok = n_tiles * TOK_TILE
    return pl.pallas_call(
        body, grid=(bsz, n_tiles), in_specs=[tile, tile, tile, seg, row, row],
        out_specs=[pl.BlockSpec((1, 128), lambda b, i: (0, 0)), tile, tile, seg, row, row],
        out_shape=[jax.ShapeDtypeStruct((1, 128), f32), jax.ShapeDtypeStruct((bsz, t_tok, D_MODEL), f32),
                   jax.ShapeDtypeStruct((bsz, t_tok, D_MODEL), MXU_DTYPE),
                   jax.ShapeDtypeStruct((bsz, 1, 1, D_MODEL), f32),
                   jax.ShapeDtypeStruct((1, D_MODEL), f32), jax.ShapeDtypeStruct((1, D_MODEL), f32)],
        compiler_params=_cparams(("arbitrary", "arbitrary")), name="loss_and_grads")(h1, f, tgt, gate, b2, g)


SHIFT_BLOCK = SHIFT_COLS
HALO_ROWS = 8


def _shift_specs(n_tok, col0):
    per_tile = TOK_TILE // HALO_ROWS
    last = n_tok // HALO_ROWS - 1
    tile = pl.BlockSpec((None, TOK_TILE, SHIFT_BLOCK), lambda j, b, i: (b, i, col0 + j))
    prev = pl.BlockSpec((None, HALO_ROWS, SHIFT_BLOCK),
                        lambda j, b, i: (b, jnp.maximum(i * per_tile - 1, 0), col0 + j))
    nxt = pl.BlockSpec((None, HALO_ROWS, SHIFT_BLOCK),
                       lambda j, b, i: (b, jnp.minimum((i + 1) * per_tile, last), col0 + j))
    return tile, prev, nxt


def _shifted(p, prev_ref, next_ref, is_first, is_last):
    row = lax.broadcasted_iota(jnp.int32, p.shape, 0)
    prev_row = jnp.where(is_first, 0.0, prev_ref[HALO_ROWS - 1:HALO_ROWS, :].astype(f32))
    next_row = jnp.where(is_last, 0.0, next_ref[0:1, :].astype(f32))
    prev = jnp.where(row == 0, prev_row, pltpu.roll(p, 1, axis=0))
    nxt = jnp.where(row == TOK_TILE - 1, next_row, pltpu.roll(p, TOK_TILE - 1, axis=0))
    return prev, nxt


def token_shift(px, mu, seg_first, seg_last):
    bsz, n_tok, _ = px.shape
    n_tiles = n_tok // TOK_TILE

    def body(p_ref, prev_ref, next_ref, mu_ref, o_ref):
        i = pl.program_id(2)
        p = p_ref[...]
        prev, nxt = _shifted(p, prev_ref, next_ref, seg_first(i), seg_last(i))
        o_ref[...] = p + mu_ref[0:1, :] * (prev - p) + mu_ref[1:2, :] * (nxt - p)

    tile, prev, nxt = _shift_specs(n_tok, 0)
    return pl.pallas_call(
        body, grid=(SHIFT_COLS // SHIFT_BLOCK, bsz, n_tiles),
        in_specs=[tile, prev, nxt, pl.BlockSpec((2, SHIFT_BLOCK), lambda j, b, i: (0, j))],
        out_specs=pl.BlockSpec((None, TOK_TILE, SHIFT_BLOCK), lambda j, b, i: (b, i, j)),
        out_shape=jax.ShapeDtypeStruct((bsz, n_tok, SHIFT_COLS), f32),
        compiler_params=_cparams(("parallel", "parallel", "parallel")), name="token_shift")(px, px, px, mu)


def token_shift_bwd(dps, px, mu, seg_first, seg_last):
    bsz, n_tok, _ = px.shape
    n_tiles = n_tok // TOK_TILE

    def body(d_ref, dprev_ref, dnext_ref, p_ref, prev_ref, next_ref, mu_ref, dp_ref, dmu_ref):
        b, i = pl.program_id(1), pl.program_id(2)
        first, last = seg_first(i), seg_last(i)
        d, p = d_ref[...], p_ref[...]
        d_prev, d_next = _shifted(d, dprev_ref, dnext_ref, first, last)
        p_prev, p_next = _shifted(p, prev_ref, next_ref, first, last)
        mu0, mu1 = mu_ref[0:1, :], mu_ref[1:2, :]
        dp_ref[...] = (d + mu0 * (d_next - d) + mu1 * (d_prev - d)).astype(dp_ref.dtype)

        @pl.when(jnp.logical_and(b == 0, i == 0))
        def _():
            dmu_ref[...] = jnp.zeros_like(dmu_ref)

        dmu_ref[0:1, :] += jnp.sum(d * (p_prev - p), axis=0, keepdims=True)
        dmu_ref[1:2, :] += jnp.sum(d * (p_next - p), axis=0, keepdims=True)

    dtile, dprev, dnext = _shift_specs(n_tok, 0)
    tile, prev, nxt = _shift_specs(n_tok, 0)
    mu_spec = pl.BlockSpec((2, SHIFT_BLOCK), lambda j, b, i: (0, j))
    return pl.pallas_call(
        body, grid=(SHIFT_COLS // SHIFT_BLOCK, bsz, n_tiles),
        in_specs=[dtile, dprev, dnext, tile, prev, nxt, mu_spec],
        out_specs=[pl.BlockSpec((None, TOK_TILE, SHIFT_BLOCK), lambda j, b, i: (b, i, j)), mu_spec],
        out_shape=[jax.ShapeDtypeStruct((bsz, n_tok, SHIFT_COLS), MXU_DTYPE),
                   jax.ShapeDtypeStruct((2, SHIFT_COLS), f32)],
        compiler_params=_cparams(("arbitrary", "arbitrary", "arbitrary")), name="token_shift_bwd")(
        dps, dps, dps, px, px, px, mu)


def _dg(a, b, ca, cb):
    return lax.dot_general(a.astype(MXU_DTYPE), b.astype(MXU_DTYPE), (((ca,), (cb,)), ((), ())),
                           preferred_element_type=f32)


@jax.custom_vjp
def _mm_nt(a, b):
    return _dg(a, b, 1, 1)


_mm_nt.defvjp(lambda a, b: (_dg(a, b, 1, 1), (a, b)),
              lambda res, ct: (_dg(ct, res[1], 1, 0), _dg(ct, res[0], 0, 0)))


@jax.custom_vjp
def _mm_tn(a, b):
    return _dg(a, b, 0, 0)


_mm_tn.defvjp(lambda a, b: (_dg(a, b, 0, 0), (a, b)),
              lambda res, ct: (_dg(res[1], ct, 1, 1), _dg(res[0], ct, 1, 0)))


ROTARY_PAIR = RET_DH // 4


def _swap_pairs_impl(t):
    lane = lax.broadcasted_iota(jnp.int32, t.shape, 1)
    return jnp.where(lane % (2 * ROTARY_PAIR) < ROTARY_PAIR, pltpu.roll(t, RET_DH - ROTARY_PAIR, axis=1),
                     pltpu.roll(t, ROTARY_PAIR, axis=1))


@jax.custom_vjp
def _swap_pairs(t):
    return _swap_pairs_impl(t)


_swap_pairs.defvjp(lambda t: (_swap_pairs_impl(t), None), lambda _, ct: (_swap_pairs_impl(ct),))


def _ret_chunk(state, q_raw, k_raw, v, cos, sin, ld_row, head, reverse):
    c = RET_CHUNK
    lane = lax.broadcasted_iota(jnp.int32, ld_row.shape, 1)
    lg = -jnp.exp(jnp.sum(jnp.where(lane == head, ld_row, 0.0), axis=-1, keepdims=True))
    rot = lambda t: t * cos + _swap_pairs(t) * sin
    q = rot(q_raw)
    k = rot(k_raw) * (RET_DH ** -0.5)
    ti = lax.broadcasted_iota(jnp.int32, (c, 1), 0).astype(f32)
    tj = lax.broadcasted_iota(jnp.int32, (1, c), 1).astype(f32)
    if not reverse:
        dist, mask, q_exp, k_exp = ti - tj, (ti - tj) >= 0, ti + 1.0, c - 1.0 - ti
    else:
        dist, mask, q_exp, k_exp = tj - ti, (tj - ti) > 0, c - ti, ti
    decay = jnp.where(mask, jnp.exp(lg * jnp.maximum(dist, 0.0)), 0.0)
    scores = _mm_nt(q, k) * decay
    out = _mxu_dot(scores, v) + _mxu_dot(q * jnp.exp(lg * q_exp), state)
    new_state = state * jnp.exp(lg * c) + _mm_tn(k * jnp.exp(lg * k_exp), v)
    return out, new_state


def _ret_specs(bsz, order):
    tok = lambda col=0: pl.BlockSpec((bsz, RET_CHUNK, RET_W), lambda i: (0, order(i), col))
    tab = pl.BlockSpec((RET_CHUNK, RET_DH), lambda i: (order(i), 0))
    ld = pl.BlockSpec((1, RET_DH), lambda i: (0, 0))
    return tok, tab, ld


def retention_fwd(px, cos, sin, ld_row, order, reverse, name):
    bsz, n_tok, _ = px.shape
    n_ch = n_tok // RET_CHUNK

    def body(q_ref, k_ref, v_ref, cos_ref, sin_ref, ld_ref, o_ref, sv_ref, st_ref):
        @pl.when(pl.program_id(0) == 0)
        def _():
            st_ref[...] = jnp.zeros_like(st_ref)

        for b in range(bsz):
            for h in range(RET_HEADS):
                sl = slice(h * RET_DH, (h + 1) * RET_DH)
                s = st_ref[b, h]
                sv_ref[b, h] = s
                o, s_new = _ret_chunk(s, q_ref[b, :, sl], k_ref[b, :, sl], v_ref[b, :, sl], cos_ref[...], sin_ref[...],
                                      ld_ref[...], h, reverse)
                o_ref[b, :, sl] = o
                st_ref[b, h] = s_new

    tok, tab, ld = _ret_specs(bsz, order)
    return pl.pallas_call(
        body, grid=(n_ch,), in_specs=[tok(0), tok(1), tok(2), tab, tab, ld],
        out_specs=[tok(), pl.BlockSpec((bsz, None, RET_HEADS, RET_DH, RET_DH), lambda i: (0, i, 0, 0, 0))],
        out_shape=[jax.ShapeDtypeStruct((bsz, n_tok, RET_W), f32),
                   jax.ShapeDtypeStruct((bsz, n_ch, RET_HEADS, RET_DH, RET_DH), f32)],
        scratch_shapes=[pltpu.VMEM((bsz, RET_HEADS, RET_DH, RET_DH), f32)],
        compiler_params=_cparams(("arbitrary",)), name=name)(px, px, px, cos, sin, ld_row)


def retention_bwd(do, px, states, cos, sin, ld_row, order, reverse, name):
    bsz, n_tok, _ = px.shape
    n_ch = n_tok // RET_CHUNK
    back = lambda i: order(n_ch - 1 - i)

    def body(do_ref, q_ref, k_ref, v_ref, sv_ref, cos_ref, sin_ref, ld_ref,
             dq_ref, dk_ref, dv_ref, dld_ref, dst_ref):
        @pl.when(pl.program_id(0) == 0)
        def _():
            dst_ref[...] = jnp.zeros_like(dst_ref)
            dld_ref[...] = jnp.zeros_like(dld_ref)

        cos_v, sin_v = cos_ref[...], sin_ref[...]
        for b in range(bsz):
            for h in range(RET_HEADS):
                sl = slice(h * RET_DH, (h + 1) * RET_DH)
                f = lambda s, q, k, v, ld, h=h: _ret_chunk(s, q, k, v, cos_v, sin_v, ld, h, reverse)
                _, vjp = jax.vjp(f, sv_ref[b, h], q_ref[b, :, sl], k_ref[b, :, sl], v_ref[b, :, sl], ld_ref[...])
                ds, dq, dk, dv, dld = vjp((do_ref[b, :, sl], dst_ref[b, h]))
                dst_ref[b, h] = ds
                dq_ref[b, :, sl] = dq
                dk_ref[b, :, sl] = dk
                dv_ref[b, :, sl] = dv
                dld_ref[...] += dld

    tok, tab, ld = _ret_specs(bsz, back)
    return pl.pallas_call(
        body, grid=(n_ch,),
        in_specs=[tok(), tok(0), tok(1), tok(2),
                  pl.BlockSpec((bsz, None, RET_HEADS, RET_DH, RET_DH), lambda i: (0, n_ch - 1 - i, 0, 0, 0)),
                  tab, tab, ld],
        out_specs=[tok(), tok(), tok(), ld],
        out_shape=[jax.ShapeDtypeStruct((bsz, n_tok, RET_W), f32)] * 3 + [jax.ShapeDtypeStruct((1, RET_DH), f32)],
        scratch_shapes=[pltpu.VMEM((bsz, RET_HEADS, RET_DH, RET_DH), f32)],
        compiler_params=_cparams(("arbitrary",)), name=name)(
        do, px, px, px, states, cos, sin, ld_row)


HALF_W = RW_W // 2


def _head_sum(x, ones):
    xm = x.astype(MXU_DTYPE)
    return jnp.concatenate([jnp.dot(xm[:, :HALF_W], ones, preferred_element_type=f32),
                            jnp.dot(xm[:, HALF_W:], ones, preferred_element_type=f32)], axis=1)


def _stack(parts):
    return jnp.concatenate(parts, axis=0)


def _row(ref, b, t):
    return ref[b, pl.ds(t, 1), :]


SCAN_DIRS = ((False, True), (True, False))
RW_HEADS = RW_W // RW_N
HEAD_ROWS_PAD = 16


def _head_rows(row, mask):
    return jnp.broadcast_to(row, mask.shape) * mask


def _outer(per_value, row, mask_pad):
    return lax.dot_general(per_value.astype(MXU_DTYPE), _head_rows(row, mask_pad).astype(MXU_DTYPE),
                           (((0,), (0,)), ((), ())), preferred_element_type=f32)


def _read(states, rows, mask, more_rows=()):
    lhs = _stack([_head_rows(r, mask) for r in list(rows) + list(more_rows)])
    return lax.dot_general(lhs.astype(MXU_DTYPE), _stack(states).astype(MXU_DTYPE), (((1,), (1,)), ((), ())),
                           preferred_element_type=f32)


def _own_block(raw, b):
    lanes = raw[:, RW_N * b:RW_N * (b + 1)]
    turned = _stack([lanes[RW_HEADS * b:], lanes[:RW_HEADS * b]]) if b else lanes
    if turned.shape[0] < HEAD_ROWS_PAD:
        turned = _stack([turned, jnp.zeros((HEAD_ROWS_PAD - turned.shape[0], RW_N), f32)])
    return turned[:HEAD_ROWS_PAD]


def _row_from_heads(per_value, state, mask_pad):
    full = jnp.dot(per_value.astype(MXU_DTYPE), state.astype(MXU_DTYPE), preferred_element_type=f32)
    return jnp.sum(full * mask_pad, axis=0, keepdims=True)


def _scan_specs(bsz, order):
    rows = lambda col=0: pl.BlockSpec((bsz, SCAN_CHUNK, RW_W), lambda i: (0, order(i), col))
    per_value = pl.BlockSpec((bsz, SCAN_CHUNK, HEAD_ROWS_PAD, RW_N), lambda i: (0, order(i), 0, 0))
    states = pl.BlockSpec((SCAN_CHUNK, bsz, RW_N, RW_W), lambda i: (order(i), 0, 0, 0))
    blocks = pl.BlockSpec((SCAN_CHUNK, RW_HEADS * bsz, RW_N * bsz), lambda i: (order(i), 0, 0))
    return rows, per_value, states, blocks


def _mxu_operands(states):
    return [s.astype(MXU_DTYPE) for s in states]


def _removed(states_m, kk_t, ones, bsz):
    removed = _head_sum(_stack([states_m[b] * kk_t[b].astype(MXU_DTYPE) for b in range(bsz)]), ones)
    return [removed[b * RW_N:(b + 1) * RW_N] for b in range(bsz)]


def _advance(sp, rem, w_t, b_t, vk, bsz):
    return [sp[b] * w_t[b] - rem[b] * b_t[b] + vk[b] for b in range(bsz)]


def heads_to_rows(a):
    b, t, _ = a.shape
    return jnp.pad(a.astype(MXU_DTYPE).reshape(b, t, RW_HEADS, RW_N),
                   ((0, 0), (0, 0), (0, HEAD_ROWS_PAD - RW_HEADS), (0, 0)))


def _blocks_to_rows(raw_ref, first, row_ref, bsz):
    steps = pl.ds(first, SCAN_CHUNK)
    for b in range(bsz):
        for h in range(RW_HEADS):
            row_ref[b, :, h * RW_N:(h + 1) * RW_N] = raw_ref[steps, RW_HEADS * b + h, RW_N * b:RW_N * (b + 1)]


N_ROWS_FWD = 5
N_ROWS_BWD = 5


def _scan_consts(bsz):
    head = (jnp.arange(RW_W)[None, :] // RW_N == jnp.arange(RW_HEADS)[:, None]).astype(f32)
    return head, jnp.pad(head, ((0, HEAD_ROWS_PAD - RW_HEADS), (0, 0))), _block_ones(HALF_W, RW_N)


def _const_specs(consts):
    return [pl.BlockSpec(c.shape, lambda i: (0, 0)) for c in consts]


def rwkv_scan_fwd(rows_in, v_heads, orders, name):
    bsz, n_tok, _ = rows_in[0][0][0].shape
    n_ch = n_tok // SCAN_CHUNK
    rng = range(bsz)
    consts = _scan_consts(bsz)

    def body(*refs):
        rows = [refs[:N_ROWS_FWD], refs[N_ROWS_FWD:2 * N_ROWS_FWD]]
        (v0, v1, head_ref, pad_ref, ones_ref, y0, y1, h0, h1, f0, f1, m0, m1, s0, s1, late_ref,
         raw_ref) = refs[2 * N_ROWS_FWD:]
        v_refs, y_refs, hist_refs, final_refs, s_refs = (v0, v1), (y0, y1), (h0, h1), (f0, f1), (s0, s1)
        removed_refs = (m0, m1)
        n_blk = RW_HEADS * bsz
        head_v, pad_v, ones_v = head_ref[...], pad_ref[...], ones_ref[...]
        for d in range(2):
            @pl.when(pl.program_id(0) == 0)
            def _(d=d):
                s_refs[d][...] = jnp.zeros_like(s_refs[d])

        def step(j, carry):
            ts = [SCAN_CHUNK - 1 - j if reverse else j for reverse, _ in SCAN_DIRS]
            sps = [[s_refs[d][b] for b in rng] for d in range(2)]
            sps_m = [_mxu_operands(sps[d]) for d in range(2)]
            vks = [[_outer(v_refs[d][b, ts[d]], _row(rows[d][4], b, ts[d]), pad_v) for b in rng] for d in range(2)]
            rems = [_removed(sps_m[d], [_row(rows[d][1], b, ts[d]) for b in rng], ones_v, bsz) for d in range(2)]
            for d, (reverse, inclusive) in enumerate(SCAN_DIRS):
                r_ref = rows[d][0]
                read_at = jnp.maximum(j - 1, 0) if inclusive else ts[d]
                both = _read(sps_m[d], [_row(r_ref, b, read_at) for b in rng], head_v,
                             [_row(rows[d][1], b, ts[d]) for b in rng])
                if inclusive:
                    late_ref[j] = both[:n_blk]
                else:
                    raw_ref[ts[d]] = both[:n_blk]
                removed_refs[d][ts[d]] = both[n_blk:]
            for d in range(2):
                new = _advance(sps[d], rems[d], [_row(rows[d][2], b, ts[d]) for b in rng],
                               [_row(rows[d][3], b, ts[d]) for b in rng], vks[d], bsz)
                for b in rng:
                    hist_refs[d][ts[d], b] = sps_m[d][b]
                    s_refs[d][b] = new[b]
            return carry

        lax.fori_loop(0, SCAN_CHUNK, step, 0, unroll=SCAN_UNROLL)
        for d, (reverse, inclusive) in enumerate(SCAN_DIRS):
            final_refs[d][...] = s_refs[d][...]
            if inclusive:
                assert not reverse
                last = SCAN_CHUNK - 1
                late_ref[SCAN_CHUNK] = _read(_mxu_operands([s_refs[d][b] for b in rng]),
                                             [rows[d][0][b, last:last + 1, :] for b in rng], head_v)
                _blocks_to_rows(late_ref, 1, y_refs[d], bsz)
            else:
                _blocks_to_rows(raw_ref, 0, y_refs[d], bsz)

    specs = [_scan_specs(bsz, orders[d]) for d in range(2)]
    state = pltpu.VMEM((bsz, RW_N, RW_W), f32)
    late = pltpu.VMEM((SCAN_CHUNK + 1, RW_HEADS * bsz, RW_N * bsz), f32)
    raw = pltpu.VMEM((SCAN_CHUNK, RW_HEADS * bsz, RW_N * bsz), f32)
    final_spec = pl.BlockSpec((bsz, RW_N, RW_W), lambda i: (0, 0, 0))
    return pl.pallas_call(
        body, grid=(n_ch,),
        in_specs=[specs[d][0](col) for d in range(2) for _, col in rows_in[d]] + [specs[0][1], specs[1][1]]
        + _const_specs(consts),
        out_specs=[specs[0][0](), specs[1][0](), specs[0][2], specs[1][2], final_spec, final_spec,
                   specs[0][3], specs[1][3]],
        out_shape=[jax.ShapeDtypeStruct((bsz, n_tok, RW_W), f32)] * 2
        + [jax.ShapeDtypeStruct((n_tok, bsz, RW_N, RW_W), MXU_DTYPE)] * 2
        + [jax.ShapeDtypeStruct((bsz, RW_N, RW_W), f32)] * 2
        + [jax.ShapeDtypeStruct((n_tok, RW_HEADS * bsz, RW_N * bsz), f32)] * 2,
        scratch_shapes=[state, state, late, raw],
        compiler_params=_cparams(("arbitrary",)), name=name)(
        *[a for d in range(2) for a, _ in rows_in[d]], v_heads, v_heads, *consts)


def rwkv_scan_bwd(rows_in, v_heads, dy_heads, hists, finals, removed, orders, name):
    bsz, n_tok, _ = rows_in[0][0][0].shape
    n_ch = n_tok // SCAN_CHUNK
    backs = [functools.partial(lambda i, order: order(n_ch - 1 - i), order=orders[d]) for d in range(2)]
    rng = range(bsz)
    consts = _scan_consts(bsz)
    n_out, n_scr = 6, 6

    def body(*refs):
        rows = [refs[:N_ROWS_BWD], refs[N_ROWS_BWD:2 * N_ROWS_BWD]]
        rest = refs[2 * N_ROWS_BWD:]
        v_refs, dy_refs, hist_refs, final_refs, removed_refs = rest[0:2], rest[2:4], rest[4:6], rest[6:8], rest[8:10]
        head_ref, pad_ref, ones_ref = rest[10:13]
        outs = [rest[13:13 + n_out], rest[13 + n_out:13 + 2 * n_out]]
        scr = [rest[13 + 2 * n_out:13 + 2 * n_out + n_scr], rest[13 + 2 * n_out + n_scr:]]
        n_blk = RW_HEADS * bsz
        head_v, pad_v, ones_v = head_ref[...], pad_ref[...], ones_ref[...]
        for d in range(2):
            @pl.when(pl.program_id(0) == 0)
            def _(d=d):
                scr[d][1][...] = jnp.zeros_like(scr[d][1])
                scr[d][0][...] = final_refs[d][...]

        def step_of(j, reverse):
            return j if reverse else SCAN_CHUNK - 1 - j

        for d, (reverse, _) in enumerate(SCAN_DIRS):
            t0 = step_of(0, reverse)
            for b in rng:
                scr[d][3][b] = _outer(dy_refs[d][b, t0], rows[d][0][b, t0:t0 + 1, :], pad_v)

        def bstep(j, carry):
            ts = [step_of(j, reverse) for reverse, _ in SCAN_DIRS]
            reads = [[scr[d][3][b] for b in rng] for d in range(2)]
            dss = []
            for d, (_, inclusive) in enumerate(SCAN_DIRS):
                ds = [scr[d][1][b] for b in rng]
                dss.append([ds[b] + reads[d][b] for b in rng] if inclusive else ds)
            dss_m = [_mxu_operands(dss[d]) for d in range(2)]
            nexts = []
            for d, (reverse, _) in enumerate(SCAN_DIRS):
                t_next = step_of(jnp.minimum(j + 1, SCAN_CHUNK - 1), reverse)
                nexts.append([_outer(dy_refs[d][b, t_next], _row(rows[d][0], b, t_next), pad_v) for b in rng])
            drems = [_removed(dss_m[d], [-_row(rows[d][3], b, ts[d]) for b in rng], ones_v, bsz) for d in range(2)]
            for d in range(2):
                for b in rng:
                    scr[d][3][b] = nexts[d][b]
                both = _read(dss_m[d], [_row(rows[d][4], b, ts[d]) for b in rng], head_v,
                             [-_row(rows[d][3], b, ts[d]) for b in rng])
                scr[d][4][ts[d]] = both[:n_blk]
                scr[d][5][ts[d]] = both[n_blk:]
            for d, (_, inclusive) in enumerate(SCAN_DIRS):
                _, kk_ref, w_ref, _, _ = rows[d]
                _, ds_ref, dsh_ref = scr[d][:3]
                for b in rng:
                    dsh_ref[ts[d], b] = dss[d][b]
                    dsp = dss[d][b] * _row(w_ref, b, ts[d]) + drems[d][b] * _row(kk_ref, b, ts[d])
                    ds_ref[b] = dsp if inclusive else dsp + reads[d][b]
            return carry

        lax.fori_loop(0, SCAN_CHUNK, bstep, 0, unroll=SCAN_UNROLL)

        rsum = lambda z: jnp.sum(z, axis=0, keepdims=True)
        for d, (reverse, inclusive) in enumerate(SCAN_DIRS):
            dr_ref, dkk_ref, dw_ref, db_ref, dkt_ref, dv_ref = outs[d]
            after_ref, _, dsh_ref, _, dv_raw_ref, dremt_ref = scr[d]
            hist_ref, removed_ref = hist_refs[d], removed_refs[d]
            _blocks_to_rows(dv_raw_ref, 0, dv_ref, bsz)
            for t in range(SCAN_CHUNK):
                ts = slice(t, t + 1)
                after = t - 1 if reverse else t + 1
                for b in rng:
                    sp_m, ds = hist_ref[t, b], dsh_ref[t, b]
                    sp = sp_m.astype(f32)
                    if not inclusive:
                        seen = sp_m
                    else:
                        seen = hist_ref[after, b] if 0 <= after < SCAN_CHUNK else after_ref[b]
                    dr_ref[b, ts, :] = _row_from_heads(dy_refs[d][b, t], seen, pad_v)
                    dkt_ref[b, ts, :] = _row_from_heads(v_refs[d][b, t], ds, pad_v)
                    dw_ref[b, ts, :] = rsum(ds * sp)
                    db_ref[b, ts, :] = -_row_from_heads(_own_block(removed_ref[t], b), ds, pad_v)
                    dkk_ref[b, ts, :] = _row_from_heads(_own_block(dremt_ref[t], b), sp_m, pad_v)
            if inclusive:
                first = SCAN_CHUNK - 1 if reverse else 0
                for b in rng:
                    after_ref[b] = hist_ref[first, b].astype(f32)

    specs = [_scan_specs(bsz, backs[d]) for d in range(2)]
    hist = pltpu.VMEM((SCAN_CHUNK, bsz, RW_N, RW_W), f32)
    state = pltpu.VMEM((bsz, RW_N, RW_W), f32)
    final_spec = pl.BlockSpec((bsz, RW_N, RW_W), lambda i: (0, 0, 0))
    raw = pltpu.VMEM((SCAN_CHUNK, RW_HEADS * bsz, RW_N * bsz), f32)
    return pl.pallas_call(
        body, grid=(n_ch,),
        in_specs=[specs[d][0](col) for d in range(2) for _, col in rows_in[d]]
        + [specs[0][1], specs[1][1]] * 2 + [specs[0][2], specs[1][2], final_spec, final_spec, specs[0][3], specs[1][3]]
        + _const_specs(consts),
        out_specs=[specs[d][0]() for d in range(2) for _ in range(n_out)],
        out_shape=[jax.ShapeDtypeStruct((bsz, n_tok, RW_W), f32)] * (2 * n_out),
        scratch_shapes=[state, state, hist, state, raw, raw] * 2,
        compiler_params=_cparams(("arbitrary",)), name=name)(
        *[a for d in range(2) for a, _ in rows_in[d]], v_heads, v_heads, dy_heads, dy_heads, *hists, *finals, *removed, *consts)


MOD_NAMES = ("shift1", "scale1", "gate1", "shift2", "scale2", "gate2")


def _rope_tables(t_ctx, t_x):
    quarter = RET_DH // 4
    pos = jnp.arange(t_x)
    inv = jnp.power(ROPE_BASE, -jnp.arange(0, 2 * quarter, 2, dtype=f32) / (2 * quarter))
    ang_r = (pos // GRID_W).astype(f32)[:, None] * inv[None, :]
    ang_c = (pos % GRID_W).astype(f32)[:, None] * inv[None, :]
    cos = jnp.concatenate([jnp.cos(ang_r)] * 2 + [jnp.cos(ang_c)] * 2, axis=1)
    sin = jnp.concatenate([-jnp.sin(ang_r), jnp.sin(ang_r), -jnp.sin(ang_c), jnp.sin(ang_c)], axis=1)
    cos = jnp.concatenate([jnp.ones((t_ctx, RET_DH), f32), cos], axis=0)
    sin = jnp.concatenate([jnp.zeros((t_ctx, RET_DH), f32), sin], axis=0)
    return cos, sin


def _pad_rows(w, lo, total):
    return jnp.pad(w, ((lo, total - lo - w.shape[0]), (0, 0)))


LATE_WEIGHTS = ("w_out", "w_ff1", "w_ff2")


def layer_step(x, ctx, tgt, mod_x, mod_ctx, wt, late_weights=None, early_grads=None, last_grads=None):
    bsz, t_x, _ = x.shape
    t_c = ctx.shape[1]
    t_all = t_c + t_x
    n_ct, n_xt = t_c // TOK_TILE, t_x // TOK_TILE
    n_t = n_ct + n_xt
    assert t_c % TOK_TILE == 0 and t_x % TOK_TILE == 0 and t_c % RET_CHUNK == 0

    seg = lambda i: (i >= n_ct).astype(jnp.int32)
    seg_first = lambda i: jnp.logical_or(i == 0, i == n_ct)
    seg_last = lambda i: jnp.logical_or(i == n_ct - 1, i == n_t - 1)
    mod_all = {n: jnp.stack([jnp.broadcast_to(mod_ctx[k], (bsz, D_MODEL)), mod_x[:, k]], axis=1)[:, :, None, :]
               for k, n in enumerate(MOD_NAMES)}
    mod_lat = {n: mod_x[:, k][:, None, None, :] for k, n in enumerate(MOD_NAMES)}
    both = lambda n: Seg(mod_all[n], seg, seg_first)
    lat = lambda n: Seg(mod_lat[n], lambda i: 0, lambda i: i == 0)
    flat = lambda a: a.reshape(-1, a.shape[-1])

    def chunk_orders(n_ctx_chunks, n_chunks):
        fwd = lambda i: i
        bwd = lambda i: jnp.where(i < n_ctx_chunks, n_ctx_chunks - 1 - i, n_chunks + n_ctx_chunks - 1 - i)
        return fwd, bwd

    ones64, ones128 = _block_ones(RW_W, RW_N), _block_ones(RET_W, RET_DH)
    cos, sin = _rope_tables(t_c, t_x)
    ld_rows = [jnp.pad(wt["ret_log_decay"][d][None, :], ((0, 0), (0, RET_DH - RET_HEADS))) for d in range(2)]
    w_up_pad = [_pad_rows(wt["rwkv_w_up"][d], 0, LORA_W) for d in range(2)]
    a_up_pad = [_pad_rows(wt["rwkv_a_up"][d], DECAY_LORA, LORA_W) for d in range(2)]
    g_up_pad = _pad_rows(wt["rwkv_g_up"], DECAY_LORA + AAA_LORA, LORA_W)
    row = lambda a, d: a[d][None, :]

    h = jnp.concatenate([ctx, x], axis=1)
    norm1_ins = lambda: [Tiled(h), both("shift1"), both("scale1"), Glob(wt["norm1_g"])]
    (n1,) = ew_forward(fn_norm_mod, "norm1", bsz, n_t, norm1_ins(), [(D_MODEL, MXU_DTYPE)])
    px = matmul(flat(n1), wt["w_in"], "nn", "proj_in").reshape(bsz, t_all, IN_COLS)
    px_rw = px[..., RET_COLS:]
    ps = token_shift(px_rw, wt["rwkv_shift_mu"], seg_first, seg_last)

    def prep_ins():
        return [Tiled(ps, RW_W, 1), Tiled(ps, LORA_W, 3 * RW_W // LORA_W),
                Glob(row(wt["rwkv_w0"], 0)), Glob(row(wt["rwkv_w0"], 1)),
                Glob(row(wt["rwkv_a0"], 0)), Glob(row(wt["rwkv_a0"], 1)),
                Glob(w_up_pad[0]), Glob(w_up_pad[1]), Glob(a_up_pad[0]), Glob(a_up_pad[1]), Glob(g_up_pad),
                Glob(wt["rwkv_k_k"]), Glob(wt["rwkv_k_a"]), Glob(ones64)]

    kk, w_f, b_f, kt_f, w_b, b_b, kt_b, g_rw = ew_forward(fn_rwkv_prepare, "rwkv_prepare", bsz, n_t, prep_ins(),
                                                           [(RW_W, f32)] * 8)
    rw_order = chunk_orders(t_c // SCAN_CHUNK, t_all // SCAN_CHUNK)
    ret_order = chunk_orders(t_c // RET_CHUNK, t_all // RET_CHUNK)
    scan_rows = [[(ps, 0), (kk, 0), (w_f, 0), (b_f, 0), (kt_f, 0)], [(ps, 0), (kk, 0), (w_b, 0), (b_b, 0), (kt_b, 0)]]
    v_heads = heads_to_rows(ps[..., 2 * RW_W:3 * RW_W])
    y_f, y_b, *kept_states = rwkv_scan_fwd(scan_rows, v_heads, rw_order, "rwkv_scan_fwd")
    y = [y_f, y_b]
    o, ret_states = [], []
    for d in range(2):
        o_d, st_d = retention_fwd(px, cos, sin, ld_rows[d], ret_order[d], SCAN_DIRS[d][0], f"retention_fwd{d}")
        o.append(o_d), ret_states.append(st_d)

    def merge_ins(toff):
        return [Tiled(o[0], toff=toff), Tiled(o[1], toff=toff), Tiled(px, RET_W, 3, toff),
                Tiled(y[0], toff=toff), Tiled(y[1], toff=toff), Tiled(ps, RW_W, 0, toff), Tiled(kt_f, toff=toff),
                Tiled(ps, RW_W, 2, toff), Tiled(g_rw, toff=toff),
                Glob(wt["rwkv_r_k"]), Glob(wt["rwkv_ln_w"]), Glob(wt["rwkv_ln_b"]), Glob(ones64), Glob(ones128)]

    ret_out, rw_out = ew_forward(fn_merge, "merge_heads", bsz, n_xt, merge_ins(n_ct),
                                 [(RET_W, MXU_DTYPE), (RW_W, MXU_DTYPE)])
    merged = jnp.concatenate([ret_out, rw_out], axis=-1)
    if late_weights is not None:
        wt = {**wt, **late_weights(merged)}
    mix = matmul(flat(merged), wt["w_out"], "nn", "proj_out").reshape(bsz, t_x, D_MODEL)
    resid_ins = lambda: [Tiled(x), Tiled(mix), lat("gate1"), lat("shift2"), lat("scale2"), Glob(wt["norm2_g"])]
    h1, n2 = ew_forward(fn_resid_norm_mod, "resid_norm2", bsz, n_xt, resid_ins(), [(D_MODEL, f32), (D_MODEL, MXU_DTYPE)])
    act = matmul(flat(n2), wt["w_ff1"], "nn", "ff1", MXU_DTYPE, wt["b_ff1"], relu2).reshape(bsz, t_x, D_FF)
    ff = matmul(flat(act), wt["w_ff2"], "nn", "ff2").reshape(bsz, t_x, D_MODEL)

    g = {}
    loss, dh1, dff, dgate2, g["b_ff2"], g["final_g"] = loss_and_grads(
        h1, ff, tgt, mod_lat["gate2"], wt["b_ff2"], wt["final_g"], bsz, n_xt)
    dact = matmul(flat(dff), wt["w_ff2"], "nt", "ff2_dx", MXU_DTYPE).reshape(bsz, t_x, D_FF)
    g["w_ff2"] = matmul(flat(act), flat(dff), "tn", "ff2_dw", MXU_DTYPE)
    du, g["b_ff1"] = relu2_backward(act, dact, "relu2_bwd")
    dn2 = matmul(flat(du), wt["w_ff1"], "nt", "ff1_dx").reshape(bsz, t_x, D_MODEL)
    g["w_ff1"] = matmul(flat(n2), flat(du), "tn", "ff1_dw", MXU_DTYPE)
    dx_res, dmix, dgate1, dshift2, dscale2, g["norm2_g"] = ew_backward(
        fn_resid_norm_mod, "resid_norm2_bwd", bsz, n_xt, resid_ins(), [Tiled(dh1), Tiled(dn2)], [True] * 6,
        {1: MXU_DTYPE})
    dmerged = matmul(flat(dmix), wt["w_out"], "nt", "proj_out_dx").reshape(bsz, t_x, D_MODEL)
    g["w_out"] = matmul(flat(merged), flat(dmix), "tn", "proj_out_dw", MXU_DTYPE)
    if early_grads is not None:
        token = early_grads({n: g.pop(n) for n in LATE_WEIGHTS})
        wt = {**wt, "rwkv_r_k": wt["rwkv_r_k"] + token[:1, :1]}
    (do, dg_ret, dy, dr_m, dkt_m, dv_m, dg_rw, g["rwkv_r_k"], g["rwkv_ln_w"], g["rwkv_ln_b"]) = ew_backward(
        fn_merge, "merge_heads_bwd", bsz, n_xt, merge_ins(0),
        [Tiled(dmerged, RET_W, 0, -n_ct), Tiled(dmerged, RW_W, 1, -n_ct)],
        [True, False, True, True, False, True, True, True, True, True, True, True, False, False], lead=n_ct)

    dqkv, dld = [], []
    for d in range(2):
        *dqkv_d, dld_d = retention_bwd(do, px, ret_states[d], cos, sin, ld_rows[d], ret_order[d],
                                       SCAN_DIRS[d][0], f"retention_bwd{d}")
        dqkv.append(dqkv_d), dld.append(dld_d[0, :RET_HEADS])
    g["ret_log_decay"] = jnp.stack(dld)
    (dr_f, dkk_f, dw_f, db_f, dkt_f, dv_f, dr_b, dkk_b, dw_b, db_b, dkt_b, dv_b) = rwkv_scan_bwd(
        scan_rows, v_heads, heads_to_rows(dy), kept_states[:2], kept_states[2:4], kept_states[4:], rw_order, "rwkv_scan_bwd")
    prep_cts = [dkk_f + dkk_b, dw_f, db_f, dkt_f + dkt_m, dw_b, db_b, dkt_b, dg_rw]
    (dks, dlora, dw0_f, dw0_b, da0_f, da0_b, dwup_f, dwup_b, daup_f, daup_b, dgup, g["rwkv_k_k"],
     g["rwkv_k_a"]) = ew_backward(fn_rwkv_prepare, "rwkv_prepare_bwd", bsz, n_t, prep_ins(),
                                  [Tiled(c) for c in prep_cts], [True] * 13 + [False])
    g["rwkv_w0"] = jnp.concatenate([dw0_f, dw0_b], axis=0)
    g["rwkv_a0"] = jnp.concatenate([da0_f, da0_b], axis=0)
    g["rwkv_w_up"] = jnp.stack([dwup_f[:DECAY_LORA], dwup_b[:DECAY_LORA]])
    g["rwkv_a_up"] = jnp.stack([daup_f[DECAY_LORA:DECAY_LORA + AAA_LORA], daup_b[DECAY_LORA:DECAY_LORA + AAA_LORA]])
    g["rwkv_g_up"] = dgup[DECAY_LORA + AAA_LORA:]
    dps = jnp.concatenate([dr_f + dr_b + dr_m, dks, dv_f + dv_b + dv_m, dlora], axis=-1)
    dp_rw, g["rwkv_shift_mu"] = token_shift_bwd(dps, px_rw, wt["rwkv_shift_mu"], seg_first, seg_last)
    dpx = jnp.concatenate([(dqkv[0][k] + dqkv[1][k]).astype(MXU_DTYPE) for k in range(3)]
                          + [dg_ret.astype(MXU_DTYPE), dp_rw], axis=-1)
    g["w_in"] = matmul(flat(n1), flat(dpx), "tn", "proj_in_dw", MXU_DTYPE)
    after_start = None
    if last_grads is not None:
        token = last_grads(g.pop("w_in"), {n: g.pop(n) for n in LAST_SHARDED})
        after_start = jnp.zeros((1, D_MODEL), f32) + token[:1, :1]
    dn1 = matmul(flat(dpx), wt["w_in"], "nt", "proj_in_dx", bias=after_start).reshape(bsz, t_all, D_MODEL)
    dh, dshift1, dscale1, g["norm1_g"] = ew_backward(fn_norm_mod, "norm1_bwd", bsz, n_t, norm1_ins(), [Tiled(dn1)],
                                                     [True] * 4)
    grad_x = dh[:, t_c:] + dx_res
    zeros = jnp.zeros((D_MODEL,), f32)
    g["mod_x"] = jnp.stack([dshift1[:, 1, 0], dscale1[:, 1, 0], dgate1[:, 0, 0], dshift2[:, 0, 0], dscale2[:, 0, 0],
                            dgate2[:, 0, 0]], axis=1)
    g["mod_ctx"] = jnp.stack([dshift1[:, 0, 0].sum(0), dscale1[:, 0, 0].sum(0), zeros, zeros, zeros, zeros])
    return loss, grad_x, g


MESH_ID = pl.DeviceIdType.MESH
ALL_PEERS = [(dx, dy, dc) for dx in (0, 1) for dy in (0, 1) for dc in (0, 1)][1:]
CHIP_PEERS = [(1, 0, 0), (0, 1, 0), (1, 1, 0)]
CHIP_SLOTS = (0, 2, 4, 6)


def _mesh_pos():
    return lax.axis_index("x"), lax.axis_index("y"), lax.axis_index("c")


def _device_slot():
    x, y, c = _mesh_pos()
    return 4 * x + 2 * y + c


def sibling_swap(arrs, name):
    n = len(arrs)

    def body(*refs):
        in_refs, out_refs = refs[:n], refs[n:2 * n]
        send_sems, recv_sems = refs[2 * n:]
        x, y, c = _mesh_pos()
        copies = [pltpu.make_async_remote_copy(src_ref=in_refs[a], dst_ref=out_refs[a], send_sem=send_sems.at[a],
                                               recv_sem=recv_sems.at[a], device_id=(x, y, 1 - c),
                                               device_id_type=MESH_ID) for a in range(n)]
        for cp in copies:
            cp.start()
        for cp in copies:
            cp.wait()

    any_spec = pl.BlockSpec(memory_space=pl.ANY)
    res = pl.pallas_call(
        body, in_specs=[any_spec] * n, out_specs=[any_spec] * n,
        out_shape=[jax.ShapeDtypeStruct(a.shape, a.dtype) for a in arrs],
        scratch_shapes=[pltpu.SemaphoreType.DMA((n,)), pltpu.SemaphoreType.DMA((n,))],
        name=name)(*arrs)
    return list(res)


def exchange(arrs, gather, peers, name, by_chip=False, own=True):
    n, n_peers = len(arrs), len(peers)
    n_slots = N_SHARDS if by_chip else N_DEV
    slot = (lambda x, y, c: 2 * x + y) if by_chip else (lambda x, y, c: 4 * x + 2 * y + c)

    def body(*refs):
        in_refs, out_refs = refs[:n], refs[n:2 * n]
        send_sems, recv_sems, local_sems = refs[2 * n:]
        x, y, c = _mesh_pos()
        me = slot(x, y, c)
        copies, locals_ = [], []
        for a in range(n):
            if own:
                mine = in_refs[a] if gather else in_refs[a].at[me]
                loc = pltpu.make_async_copy(mine, out_refs[a].at[me], local_sems.at[a])
                loc.start()
                locals_.append(loc)
            for k, (dx, dy, dc) in enumerate(peers):
                peer = (1 - x if dx else x, 1 - y if dy else y, 1 - c if dc else c)
                src = in_refs[a] if gather else in_refs[a].at[slot(*peer)]
                sem = a * n_peers + k
                cp = pltpu.make_async_remote_copy(src_ref=src, dst_ref=out_refs[a].at[me], send_sem=send_sems.at[sem],
                                                  recv_sem=recv_sems.at[sem], device_id=peer, device_id_type=MESH_ID)
                cp.start()
                copies.append(cp)
        for cp in copies:
            cp.wait()
        for loc in locals_:
            loc.wait()

    any_spec = pl.BlockSpec(memory_space=pl.ANY)
    out_shape = [jax.ShapeDtypeStruct((n_slots,) + (a.shape if gather else a.shape[1:]), a.dtype) for a in arrs]
    n_sems = n * n_peers
    res = pl.pallas_call(
        body, in_specs=[any_spec] * n, out_specs=[any_spec] * n, out_shape=out_shape,
        scratch_shapes=[pltpu.SemaphoreType.DMA((n_sems,)), pltpu.SemaphoreType.DMA((n_sems,)),
                        pltpu.SemaphoreType.DMA((n,))],
        name=name)(*arrs)
    return list(res)


HBM_SPEC = pl.BlockSpec(memory_space=pltpu.HBM)
SEM_SPEC = pl.BlockSpec(memory_space=pltpu.SEMAPHORE)
DATAFLOW = pltpu.SideEffectType.DATAFLOW_SIDE_EFFECTING


def _peer_copies(src_refs, land_refs, send_sems, recv_sems, gather):
    x, y, c = _mesh_pos()
    me = 4 * x + 2 * y + c
    copies = []
    for a, (src_ref, land_ref) in enumerate(zip(src_refs, land_refs)):
        for k, (dx, dy, dc) in enumerate(ALL_PEERS):
            peer = (1 - x if dx else x, 1 - y if dy else y, 1 - c if dc else c)
            src = src_ref if gather else src_ref.at[4 * peer[0] + 2 * peer[1] + peer[2]]
            sem = a * len(ALL_PEERS) + k
            copies.append(pltpu.make_async_remote_copy(src_ref=src, dst_ref=land_ref.at[me], send_sem=send_sems.at[sem],
                                                       recv_sem=recv_sems.at[sem], device_id=peer,
                                                       device_id_type=MESH_ID))
    return copies


def exchange_start(arrs, gather, name):
    n = len(arrs)
    lands = [lax.empty((N_DEV,) + (a.shape if gather else a.shape[1:]), a.dtype) for a in arrs]

    def body(*refs):
        for cp in _peer_copies(refs[:n], refs[n:2 * n], refs[2 * n], refs[2 * n + 1], gather):
            cp.start()
        refs[-1][...] = jnp.zeros_like(refs[-1])

    sems = pltpu.SemaphoreType.DMA((n * len(ALL_PEERS),))
    hbm = [pltpu.HBM(a.shape, a.dtype) for a in arrs + lands]
    res = pl.pallas_call(
        body, name=name, out_shape=(sems, sems, *hbm, jax.ShapeDtypeStruct((8, 128), f32)),
        in_specs=[HBM_SPEC] * (2 * n),
        out_specs=(SEM_SPEC, SEM_SPEC, *[HBM_SPEC] * (2 * n), pl.BlockSpec(memory_space=pltpu.VMEM)),
        input_output_aliases={i: 2 + i for i in range(2 * n)},
        compiler_params=pltpu.CompilerParams(has_side_effects=DATAFLOW))(
        *[pltpu.with_memory_space_constraint(a, pltpu.HBM) for a in arrs + lands])
    return res[0], res[1], list(res[2:2 + n]), list(res[2 + n:2 + 2 * n]), res[-1]


def exchange_wait(started, after, gather, name):
    send_sems, recv_sems, srcs, lands, _ = started
    n = len(srcs)

    def body(*refs):
        for cp in _peer_copies(refs[:n], refs[n:2 * n], refs[2 * n], refs[2 * n + 1], gather):
            cp.wait_send()
            cp.wait_recv()

    res = pl.pallas_call(
        body, name=name, out_shape=tuple(pltpu.HBM(a.shape, a.dtype) for a in srcs + lands),
        in_specs=[HBM_SPEC] * (2 * n) + [SEM_SPEC, SEM_SPEC, pl.BlockSpec(memory_space=pl.ANY)],
        out_specs=tuple([HBM_SPEC] * (2 * n)), input_output_aliases={i: i for i in range(2 * n)},
        compiler_params=pltpu.CompilerParams(has_side_effects=DATAFLOW))(*srcs, *lands, send_sems, recv_sems, after)
    return list(res[n:])


def gather_two_level(arrs, name):
    n = len(arrs)
    per = 7

    def body(*refs):
        in_refs, out_refs = refs[:n], refs[n:2 * n]
        send_sems, recv_sems = refs[2 * n:]
        x, y, c = _mesh_pos()
        me, sibling = (x, y, c), (x, y, 1 - c)
        chips = [(1 - x, y), (x, 1 - y), (1 - x, 1 - y)]

        def copy(a, k, block, to, src=None):
            rows = out_refs[a].at[4 * block[0] + 2 * block[1] + block[2]]
            return pltpu.make_async_remote_copy(src_ref=rows if src is None else src, dst_ref=rows,
                                                send_sem=send_sems.at[a * per + k], recv_sem=recv_sems.at[a * per + k],
                                                device_id=to, device_id_type=MESH_ID)

        first, passed = [], []
        for a in range(n):
            first.append(copy(a, 0, me, sibling, src=in_refs[a]))
            first += [copy(a, 1 + j, me, (*chip, c), src=in_refs[a]) for j, chip in enumerate(chips)]
        for cp in first:
            cp.start()
        for a in range(n):
            for j, chip in enumerate(chips):
                copy(a, 1 + j, (*chip, c), me).wait_recv()
                fwd = copy(a, 4 + j, (*chip, c), sibling)
                fwd.start()
                passed.append(fwd)
        for a in range(n):
            copy(a, 0, sibling, me).wait_recv()
            for j, chip in enumerate(chips):
                copy(a, 4 + j, (*chip, 1 - c), me).wait_recv()
        for cp in first + passed:
            cp.wait_send()

    any_spec = pl.BlockSpec(memory_space=pl.ANY)
    res = pl.pallas_call(
        body, in_specs=[any_spec] * n, out_specs=[any_spec] * n,
        out_shape=[jax.ShapeDtypeStruct((N_DEV,) + a.shape, a.dtype) for a in arrs],
        scratch_shapes=[pltpu.SemaphoreType.DMA((n * per,)), pltpu.SemaphoreType.DMA((n * per,))],
        name=name)(*arrs)
    return list(res)


def sum_slots(parts, slots, name):
    _, r, c = parts.shape
    tr = r
    for cand in (512, 256, 128, 64, 32, 16, 8):
        if r % cand == 0 and cand * c * 4 * len(slots) <= 8 * 1024 * 1024:
            tr = cand
            break

    def body(p_ref, o_ref):
        acc = p_ref[slots[0]].astype(f32)
        for s in slots[1:]:
            acc = acc + p_ref[s].astype(f32)
        o_ref[...] = acc

    return pl.pallas_call(body, grid=(r // tr,), in_specs=[pl.BlockSpec((parts.shape[0], tr, c), lambda i: (0, i, 0))],
                          out_specs=pl.BlockSpec((tr, c), lambda i: (i, 0)),
                          out_shape=jax.ShapeDtypeStruct((r, c), f32),
                          compiler_params=_cparams(("parallel",)), name=name)(parts)


def column_sum(a, name):
    def body(a_ref, o_ref):
        o_ref[...] = jnp.sum(a_ref[...], axis=0, keepdims=True)

    return pl.pallas_call(body, out_shape=jax.ShapeDtypeStruct((1, a.shape[1]), f32), name=name)(a)


def adamw(w, g, m, v, name):
    r, c = w.shape
    tr = r
    for cand in (256, 128, 64, 32, 16, 8):
        if r % cand == 0:
            tr = cand
            break

    def body(w_ref, g_ref, m_ref, v_ref, d_ref, mo_ref, vo_ref):
        gv = g_ref[...]
        m_new = ADAM_B1 * m_ref[...] + (1.0 - ADAM_B1) * gv
        v_new = ADAM_B2 * v_ref[...] + (1.0 - ADAM_B2) * jnp.square(gv)
        m_hat = m_new / (1.0 - ADAM_B1 ** ADAM_STEP)
        v_hat = v_new / (1.0 - ADAM_B2 ** ADAM_STEP)
        d_ref[...] = -ADAM_LR * (m_hat / (jnp.sqrt(v_hat) + ADAM_EPS) + ADAM_WD * w_ref[...])
        mo_ref[...] = m_new
        vo_ref[...] = v_new

    spec = pl.BlockSpec((tr, c), lambda i: (i, 0))
    return pl.pallas_call(body, grid=(r // tr,), in_specs=[spec] * 4, out_specs=[spec] * 3,
                          out_shape=[jax.ShapeDtypeStruct((r, c), f32)] * 3,
                          compiler_params=_cparams(("parallel",)), name=name)(w, g, m, v)


def adaln_fwd(c_rows, w, b):
    def body(c_ref, w_ref, b_ref, o_ref):
        cv = c_ref[...]
        o_ref[...] = _mxu_dot(cv * jax.nn.sigmoid(cv), w_ref[...]) + b_ref[...]

    return pl.pallas_call(body, out_shape=jax.ShapeDtypeStruct((c_rows.shape[0], w.shape[1]), f32),
                          compiler_params=pltpu.CompilerParams(vmem_limit_bytes=VMEM_LIMIT), name="adaln_fwd")(c_rows, w, b)


def adaln_bwd(c_rows, dm, w):
    def body(c_ref, dm_ref, w_ref, gw_ref, ds_ref):
        cv = c_ref[...]
        gw_ref[...] = _dg(cv * jax.nn.sigmoid(cv), dm_ref[...], 0, 0)
        ds_ref[...] = _dg(dm_ref[...], w_ref[...], 1, 1)

    return pl.pallas_call(body, out_shape=[jax.ShapeDtypeStruct(w.shape, f32),
                                           jax.ShapeDtypeStruct(c_rows.shape, f32)],
                          compiler_params=pltpu.CompilerParams(vmem_limit_bytes=VMEM_LIMIT), name="adaln_bwd")(c_rows, dm, w)


def c_ctx_grad(parts, c_ctx_row):
    def body(p_ref, c_ref, o_ref):
        total = p_ref[0, 0:1, :]
        for s in range(1, N_SHARDS):
            total = total + p_ref[s, 0:1, :]
        _, vjp = jax.vjp(jax.nn.silu, c_ref[...])
        o_ref[...] = vjp(total)[0]

    return pl.pallas_call(body, out_shape=jax.ShapeDtypeStruct((1, D_MODEL), f32), name="c_ctx_grad")(parts, c_ctx_row)


PACK_W = 1024
PACK_ROWS = 8


def _pack(arrs):
    pieces, layout, r0 = [], [], 0
    for a in arrs:
        size = math.prod(a.shape)
        rows = -(-size // (PACK_W * PACK_ROWS)) * PACK_ROWS
        pieces.append(jnp.pad(a.reshape(-1).astype(f32), (0, rows * PACK_W - size)).reshape(rows, PACK_W))
        layout.append((r0, rows, a.shape))
        r0 += rows
    return jnp.concatenate(pieces, axis=0), layout


def _unpack(pack, layout, lead=()):
    n_lead = len(lead)
    outs = []
    for r0, rows, shape in layout:
        piece = pack[(slice(None),) * n_lead + (slice(r0, r0 + rows),)].reshape(lead + (-1,))
        outs.append(piece[..., :math.prod(shape)].reshape(lead + tuple(shape)))
    return outs


W_NAMES = ("c_ctx", "w_ada", "b_ada", "norm1_g", "norm2_g", "w_in", "ret_log_decay", "rwkv_shift_mu", "rwkv_w0",
           "rwkv_w_up", "rwkv_a0", "rwkv_a_up", "rwkv_g_up", "rwkv_k_k", "rwkv_k_a", "rwkv_r_k", "rwkv_ln_w",
           "rwkv_ln_b", "w_out", "w_ff1", "b_ff1", "w_ff2", "b_ff2", "final_g")
COL_SHARDED = ("w_in", "w_ff1")
ROW_SHARDED = ("w_out", "w_ff2")
LAST_SHARDED = ("rwkv_shift_mu", "rwkv_w0", "rwkv_w_up", "rwkv_a0", "rwkv_a_up", "rwkv_g_up")
REPLICATED = ("c_ctx", "b_ada", "norm1_g", "norm2_g", "ret_log_decay", "rwkv_k_k", "rwkv_k_a", "rwkv_r_k",
              "rwkv_ln_w", "rwkv_ln_b", "b_ff1", "b_ff2", "final_g")
N_SHARDS = 4


def _train_step(a):
    x, c, ctx, tgt = a["x"], a["c"], a["ctx"], a["loss_target"]
    bsz = x.shape[0]
    mx, my, mc = _mesh_pos()
    shard = 2 * mx + my
    dev = _device_slot()

    (c_all,) = exchange([jnp.pad(c, ((0, PACK_ROWS - bsz), (0, 0)))], True, ALL_PEERS, "gather_c")
    n_ex = N_DEV * bsz
    c_rows = jnp.concatenate([c_all[:, :bsz].reshape(n_ex, D_MODEL), a["c_ctx"][None, :],
                              jnp.zeros((PACK_ROWS - 1, D_MODEL), f32)], axis=0)
    ada_cols = a["w_ada"].shape[-1]
    b_ada_cols = lax.dynamic_slice_in_dim(a["b_ada"], shard * ada_cols, ada_cols, axis=1)
    mod_cols = adaln_fwd(c_rows, a["w_ada"][0], b_ada_cols)

    def own_half(n):
        w = a[n][0].astype(MXU_DTYPE)
        return lax.dynamic_slice_in_dim(w, mc * (w.shape[0] // 2), w.shape[0] // 2, axis=0)

    def whole_weight(n, gth, own):
        per_chip = lax.dynamic_update_index_in_dim(gth, own, dev, 0).reshape(N_SHARDS, -1, gth.shape[-1])
        return (per_chip.transpose(1, 0, 2).reshape(per_chip.shape[1], -1) if n in COL_SHARDED
                else per_chip.reshape(-1, per_chip.shape[-1]))

    small_pack, small_layout = _pack([a[n][0] for n in LAST_SHARDED])
    own_blocks = [mod_cols, own_half("w_in"), small_pack]
    gathered = gather_two_level(own_blocks, "gather_weights")
    late_own = [own_half(n) for n in LATE_WEIGHTS]
    late_started = exchange_start(late_own, True, "gather_late_start")
    mod_own = lax.dynamic_update_index_in_dim(gathered[0], mod_cols, dev, 0)
    mod_all = jnp.stack([mod_own[s] for s in CHIP_SLOTS], axis=1).reshape(c_rows.shape[0], -1)
    mod_all = mod_all + late_started[-1][0, 0]
    mod_x = lax.dynamic_slice_in_dim(mod_all, dev * bsz, bsz, axis=0).reshape(bsz, 6, D_MODEL)
    mod_ctx = mod_all[n_ex].reshape(6, D_MODEL)
    wt = {"w_in": whole_weight("w_in", gathered[1], own_blocks[1])}

    def late_weights(after):
        lands = exchange_wait(late_started, after, True, "gather_late_wait")
        return {n: whole_weight(n, land, own) for n, land, own in zip(LATE_WEIGHTS, lands, late_own)}

    def grad_blocks(n, gw):
        if n in COL_SHARDED:
            gw = gw.reshape(gw.shape[0], N_SHARDS, -1).transpose(1, 0, 2)
        return gw.reshape(N_DEV, -1, gw.shape[-1]).astype(MXU_DTYPE)

    late_sent, last_sent = {}, {}

    def early_grads(late_g):
        late_sent["blocks"] = [grad_blocks(n, late_g[n]) for n in LATE_WEIGHTS]
        late_sent["started"] = exchange_start(late_sent["blocks"], False, "scatter_late_start")
        return late_sent["started"][-1]

    def last_grads(g_w_in, g_small):
        shard_packs = []
        for s in range(N_SHARDS):
            pieces_s = [lax.slice_in_dim(g_small[n], s * a[n].shape[-1], (s + 1) * a[n].shape[-1],
                                         axis=g_small[n].ndim - 1) for n in LAST_SHARDED]
            pack_s, last_sent["layout"] = _pack(pieces_s)
            shard_packs.append(jnp.pad(pack_s, ((0, -pack_s.shape[0] % (2 * PACK_ROWS)), (0, 0))))
        last_sent["blocks"] = [grad_blocks("w_in", g_w_in), jnp.stack(shard_packs).reshape(N_DEV, -1, PACK_W)]
        last_sent["started"] = exchange_start(last_sent["blocks"], False, "scatter_last_start")
        return last_sent["started"][-1]

    small_own = lax.dynamic_update_index_in_dim(gathered[2], small_pack, dev, 0)
    small_by_chip = _unpack(jnp.stack([small_own[s] for s in CHIP_SLOTS]), small_layout, (N_SHARDS,))
    for n, parts in zip(LAST_SHARDED, small_by_chip):
        wt[n] = jnp.concatenate([parts[s] for s in range(N_SHARDS)], axis=-1)
    for n in ("norm1_g", "norm2_g", "rwkv_k_k", "rwkv_k_a", "rwkv_r_k", "rwkv_ln_w", "rwkv_ln_b", "b_ff1", "b_ff2"):
        wt[n] = a[n]
    wt["ret_log_decay"] = a["ret_log_decay"][0]
    wt["final_g"] = a["final_g"][None, :]

    loss, grad_x, g = layer_step(x, ctx, tgt, mod_x, mod_ctx, wt, late_weights, early_grads, last_grads)

    small_names = [n for n in REPLICATED if n not in ("c_ctx", "b_ada")]
    g_pack, g_layout = _pack([jnp.pad(loss, ((0, 0), (0, PACK_W - loss.shape[1])))] + [g[n] for n in small_names]
                             + [g["mod_x"], g["mod_ctx"]])
    (g_packs,) = gather_two_level([g_pack], "gather_small_grads")
    g_packs = lax.dynamic_update_index_in_dim(g_packs, g_pack, dev, 0)
    g_sum = _unpack(sum_slots(g_packs, tuple(range(N_DEV)), "sum_small_grads"), g_layout)
    loss_total = g_sum[0][0, 0]
    grads = dict(zip(small_names, g_sum[1:1 + len(small_names)]))
    dmod_ctx = g_sum[-1].reshape(1, -1)
    dmod_x = _unpack(g_packs, g_layout, (N_DEV,))[-2].reshape(n_ex, -1)
    dmod = jnp.concatenate([dmod_x, dmod_ctx, jnp.zeros((PACK_ROWS - 1, dmod_x.shape[1]), f32)], axis=0)
    grads["b_ada"] = column_sum(dmod, "b_ada_grad")
    dmod_cols = lax.dynamic_slice_in_dim(dmod, shard * ada_cols, ada_cols, axis=1)
    grads["w_ada"], dsilu = adaln_bwd(c_rows, dmod_cols, a["w_ada"][0])

    dsilu_rows = jnp.broadcast_to(jnp.pad(dsilu[n_ex:n_ex + 1], ((0, PACK_ROWS - 1), (0, 0)))[None],
                                  (N_SHARDS, PACK_ROWS, D_MODEL))
    (shares,) = exchange([dsilu_rows], False, CHIP_PEERS, "share_c_ctx_grad", by_chip=True, own=False)
    shares = lax.dynamic_update_index_in_dim(shares, dsilu_rows[0], shard, 0)
    grads["c_ctx"] = c_ctx_grad(shares, a["c_ctx"][None, :])

    scattered, half_sums = ("w_in", "small_shards") + LATE_WEIGHTS, []
    for sent, wait_name, after in ((last_sent, "scatter_last_wait", grads["c_ctx"]),
                                   (late_sent, "scatter_late_wait", grads["c_ctx"])):
        for land, block in zip(exchange_wait(sent["started"], after, False, wait_name), sent["blocks"]):
            land = lax.dynamic_update_index_in_dim(land, lax.dynamic_index_in_dim(block, dev, 0, keepdims=False), dev, 0)
            half_sums.append(sum_slots(land, tuple(range(N_DEV)), f"sum_{scattered[len(half_sums)]}"))
    other_halves = sibling_swap(half_sums, "swap_halves")
    for n, mine, other in zip(scattered, half_sums, other_halves):
        rows = mine.shape[0]
        whole = jnp.zeros((2 * rows, mine.shape[1]), f32)
        whole = lax.dynamic_update_slice_in_dim(whole, mine, mc * rows, axis=0)
        grads[n] = lax.dynamic_update_slice_in_dim(whole, other, (1 - mc) * rows, axis=0)
    grads.update(zip(LAST_SHARDED, _unpack(grads.pop("small_shards"), last_sent["layout"])))

    out_g, out_d, out_m, out_v = {}, {}, {}, {}
    for n in ("w_ada",) + COL_SHARDED + ROW_SHARDED:
        out_g[n] = grads[n].reshape(a[n].shape)
        two_d = lambda z: z.reshape(-1, z.shape[-1])
        d, m, v = adamw(two_d(a[n]), two_d(out_g[n]), two_d(a["m_" + n]), two_d(a["v_" + n]), f"adamw_{n}")
        out_d[n], out_m[n], out_v[n] = d.reshape(a[n].shape), m.reshape(a[n].shape), v.reshape(a[n].shape)
    rest = REPLICATED + LAST_SHARDED
    for n in rest:
        out_g[n] = grads[n].reshape(a[n].shape)
    packs = [_pack([src[n] for n in rest])[0] for src in
             ({n: a[n] for n in rest}, out_g, {n: a["m_" + n] for n in rest}, {n: a["v_" + n] for n in rest})]
    _, rest_layout = _pack([a[n] for n in rest])
    for dst, pack in zip((out_d, out_m, out_v), adamw(*packs, "adamw_small")):
        dst.update(zip(rest, _unpack(pack, rest_layout)))
    return (loss_total, grad_x, *[out_g[n] for n in W_NAMES], *[out_d[n] for n in W_NAMES],
            *[out_m[n] for n in W_NAMES], *[out_v[n] for n in W_NAMES])


def kernel(x, c, ctx, c_ctx, w_ada, b_ada, norm1_g, norm2_g, w_in, ret_log_decay, rwkv_shift_mu, rwkv_w0, rwkv_w_up, rwkv_a0, rwkv_a_up, rwkv_g_up, rwkv_k_k, rwkv_k_a, rwkv_r_k, rwkv_ln_w, rwkv_ln_b, w_out, w_ff1, b_ff1, w_ff2, b_ff2, final_g, loss_target, m_c_ctx, m_w_ada, m_b_ada, m_norm1_g, m_norm2_g, m_w_in, m_ret_log_decay, m_rwkv_shift_mu, m_rwkv_w0, m_rwkv_w_up, m_rwkv_a0, m_rwkv_a_up, m_rwkv_g_up, m_rwkv_k_k, m_rwkv_k_a, m_rwkv_r_k, m_rwkv_ln_w, m_rwkv_ln_b, m_w_out, m_w_ff1, m_b_ff1, m_w_ff2, m_b_ff2, m_final_g, v_c_ctx, v_w_ada, v_b_ada, v_norm1_g, v_norm2_g, v_w_in, v_ret_log_decay, v_rwkv_shift_mu, v_rwkv_w0, v_rwkv_w_up, v_rwkv_a0, v_rwkv_a_up, v_rwkv_g_up, v_rwkv_k_k, v_rwkv_k_a, v_rwkv_r_k, v_rwkv_ln_w, v_rwkv_ln_b, v_w_out, v_w_ff1, v_b_ff1, v_w_ff2, v_b_ff2, v_final_g):
    return _train_step(dict(locals()))
```

```python
import functools
import math

import jax
import jax.numpy as jnp
from jax import lax
from jax.experimental import pallas as pl
from jax.experimental.pallas import tpu as pltpu

f32 = jnp.float32
MXU_DTYPE = jnp.bfloat16

D_MODEL = 1024
RET_W = 512
RET_HEADS = 4
RET_DH = 128
RET_CHUNK = 128
RW_W = 512
RW_N = 64
DECAY_LORA = 64
AAA_LORA = 64
GATE_LORA = 128
LORA_W = DECAY_LORA + AAA_LORA + GATE_LORA
D_FF = 4096
RET_COLS = 4 * RET_W
SHIFT_COLS = 3 * RW_W + LORA_W
IN_COLS = RET_COLS + SHIFT_COLS
GRID_W = 64
ROPE_BASE = 10000.0
NORM_EPS = 1e-6
GN_EPS = 64e-5
W_DECAY_SCALE = math.exp(-0.5)
ADAM_LR, ADAM_B1, ADAM_B2, ADAM_EPS, ADAM_WD, ADAM_STEP = 0.001, 0.9, 0.999, 1e-08, 0.01, 10

TOK_TILE = 256
MATMUL_TILE = 1024
SCAN_CHUNK = 32
SCAN_UNROLL = SCAN_CHUNK
N_DEV = 8
V7X_VMEM_BYTES = 64 * 1024 * 1024
VMEM_LIMIT = V7X_VMEM_BYTES * 7 // 8


def _cparams(sem):
    return pltpu.CompilerParams(dimension_semantics=sem, vmem_limit_bytes=VMEM_LIMIT)


def _tile(n, cap):
    best = None
    for t in range(128, min(n, cap) + 1, 128):
        if n % t == 0:
            best = t
    return best if best is not None else n


def matmul(a, b, mode, name, out_dtype=f32, bias=None, finish=None):
    if mode == "nn":
        (m, k), (k2, n) = a.shape, b.shape
    elif mode == "nt":
        (m, k), (n, k2) = a.shape, b.shape
    else:
        (k, m), (k2, n) = a.shape, b.shape
    assert k == k2, (a.shape, b.shape, mode)
    tm, tn, tk = _tile(m, MATMUL_TILE), _tile(n, MATMUL_TILE), _tile(k, MATMUL_TILE)
    nk = k // tk
    dims = {"nn": ((1,), (0,)), "nt": ((1,), (1,)), "tn": ((0,), (0,))}[mode]

    def body(a_ref, b_ref, *rest):
        o_ref, acc_ref = rest[-2:]
        kk = pl.program_id(2)

        @pl.when(kk == 0)
        def _():
            acc_ref[...] = jnp.zeros_like(acc_ref)

        acc_ref[...] += lax.dot_general(a_ref[...].astype(MXU_DTYPE), b_ref[...].astype(MXU_DTYPE),
                                        (dims, ((), ())), preferred_element_type=f32)

        @pl.when(kk == nk - 1)
        def _():
            res = acc_ref[...]
            if bias is not None:
                res = res + rest[0][...]
            if finish is not None:
                res = finish(res)
            o_ref[...] = res.astype(o_ref.dtype)

    if mode == "nn":
        a_spec = pl.BlockSpec((tm, tk), lambda i, j, q: (i, q))
        b_spec = pl.BlockSpec((tk, tn), lambda i, j, q: (q, j))
    elif mode == "nt":
        a_spec = pl.BlockSpec((tm, tk), lambda i, j, q: (i, q))
        b_spec = pl.BlockSpec((tn, tk), lambda i, j, q: (j, q))
    else:
        a_spec = pl.BlockSpec((tk, tm), lambda i, j, q: (q, i))
        b_spec = pl.BlockSpec((tk, tn), lambda i, j, q: (q, j))
    extra_specs = [] if bias is None else [pl.BlockSpec((1, tn), lambda i, j, q: (0, j))]
    extra = [] if bias is None else [bias]
    return pl.pallas_call(
        body, grid=(m // tm, n // tn, nk), in_specs=[a_spec, b_spec] + extra_specs,
        out_specs=pl.BlockSpec((tm, tn), lambda i, j, q: (i, j)),
        out_shape=jax.ShapeDtypeStruct((m, n), out_dtype),
        scratch_shapes=[pltpu.VMEM((tm, tn), f32)],
        compiler_params=_cparams(("parallel", "parallel", "arbitrary")), name=name)(a, b, *extra)


class Tiled:
    def __init__(self, arr, w=None, cidx=0, toff=0):
        self.arr, self.w, self.cidx, self.toff = arr, (arr.shape[-1] if w is None else w), cidx, toff

    def spec(self):
        cidx, toff = self.cidx, self.toff
        return pl.BlockSpec((None, TOK_TILE, self.w), lambda b, i: (b, jnp.maximum(i + toff, 0), cidx))


class Seg:
    def __init__(self, arr, seg, first):
        self.arr, self.seg, self.first = arr, seg, first

    def spec(self):
        seg = self.seg
        return pl.BlockSpec((None, None, 1, self.arr.shape[-1]), lambda b, i: (b, seg(i), 0, 0))


class Glob:
    def __init__(self, arr):
        self.arr = arr

    def spec(self):
        return pl.BlockSpec(self.arr.shape, lambda b, i: (0,) * self.arr.ndim)


def ew_forward(fn, name, bsz, n_tiles, ins, outs):
    n_in = len(ins)

    def body(*refs):
        res = fn(*[r[...] for r in refs[:n_in]])
        for o_ref, o in zip(refs[n_in:], res):
            o_ref[...] = o.astype(o_ref.dtype)

    out_specs = [pl.BlockSpec((None, TOK_TILE, w), lambda b, i: (b, i, 0)) for w, _ in outs]
    out_shape = [jax.ShapeDtypeStruct((bsz, n_tiles * TOK_TILE, w), dt) for w, dt in outs]
    return pl.pallas_call(body, grid=(bsz, n_tiles), in_specs=[d.spec() for d in ins], out_specs=out_specs,
                          out_shape=out_shape, compiler_params=_cparams(("parallel", "parallel")), name=name)(
        *[d.arr for d in ins])


def ew_backward(fn, name, bsz, n_tiles, ins, cts, want, grad_dtypes=None, lead=0):
    ct_parts = [c if isinstance(c, tuple) else (c,) for c in cts]
    cts = [part for parts in ct_parts for part in parts]
    n_in, n_ct = len(ins), len(cts)
    diff = [k for k in range(n_in) if want[k]]
    grad_dtypes = grad_dtypes or {}
    assert lead == 0 or not any(isinstance(ins[k], Seg) for k in diff)

    def body(*refs):
        b, i = pl.program_id(0), pl.program_id(1)
        g_refs = refs[n_in + n_ct:]

        def tile_grads():
            vals = [r[...] for r in refs[:n_in]]
            ct_refs = iter(refs[n_in:n_in + n_ct])
            ct_vals = tuple(functools.reduce(lambda s, t: s + t, [next(ct_refs)[...].astype(f32) for _ in parts])
                            for parts in ct_parts)

            def f(*dvals):
                full = list(vals)
                for k, v in zip(diff, dvals):
                    full[k] = v
                return tuple(fn(*full))

            _, vjp = jax.vjp(f, *[vals[k] for k in diff])
            grads = vjp(ct_vals)
            for k, g_ref, g in zip(diff, g_refs, grads):
                d = ins[k]
                if isinstance(d, Tiled):
                    g_ref[...] = g.astype(g_ref.dtype)
                else:
                    zero = d.first(i) if isinstance(d, Seg) else jnp.logical_and(b == 0, i == lead)

                    @pl.when(zero)
                    def _(g_ref=g_ref):
                        g_ref[...] = jnp.zeros_like(g_ref)

                    g_ref[...] += g

        if lead == 0:
            tile_grads()
        else:
            pl.when(i >= lead)(tile_grads)

            @pl.when(i < lead)
            def _():
                for k, g_ref in zip(diff, g_refs):
                    if isinstance(ins[k], Tiled):
                        g_ref[...] = jnp.zeros_like(g_ref)

    out_specs, out_shape = [], []
    for k in diff:
        d = ins[k]
        if isinstance(d, Tiled):
            out_specs.append(pl.BlockSpec((None, TOK_TILE, d.w), lambda b, i: (b, i, 0)))
            out_shape.append(jax.ShapeDtypeStruct((bsz, (n_tiles + lead) * TOK_TILE, d.w), grad_dtypes.get(k, f32)))
        else:
            out_specs.append(d.spec())
            out_shape.append(jax.ShapeDtypeStruct(d.arr.shape, f32))
    return pl.pallas_call(body, grid=(bsz, n_tiles + lead),
                          in_specs=[d.spec() for d in ins] + [c.spec() for c in cts],
                          out_specs=out_specs, out_shape=out_shape,
                          compiler_params=_cparams(("arbitrary", "arbitrary")), name=name)(
        *[d.arr for d in ins], *[c.arr for c in cts])


@jax.custom_vjp
def _mxu_dot(a, b):
    return jnp.dot(a.astype(MXU_DTYPE), b.astype(MXU_DTYPE), preferred_element_type=f32)


def _mxu_dot_fwd(a, b):
    return _mxu_dot(a, b), (a, b)


def _mxu_dot_bwd(res, ct):
    a, b = res
    ct = ct.astype(MXU_DTYPE)
    da = lax.dot_general(ct, b.astype(MXU_DTYPE), (((1,), (1,)), ((), ())), preferred_element_type=f32)
    db = lax.dot_general(a.astype(MXU_DTYPE), ct, (((0,), (0,)), ((), ())), preferred_element_type=f32)
    return da, db


_mxu_dot.defvjp(_mxu_dot_fwd, _mxu_dot_bwd)


def _split_dot_impl(x, ones_mat):
    hi = x.astype(MXU_DTYPE)
    lo = (x - hi.astype(f32)).astype(MXU_DTYPE)
    return jnp.dot(hi, ones_mat, preferred_element_type=f32) + jnp.dot(lo, ones_mat, preferred_element_type=f32)


@jax.custom_vjp
def _split_dot(x, ones_mat):
    return _split_dot_impl(x, ones_mat)


def _split_dot_fwd(x, ones_mat):
    return _split_dot_impl(x, ones_mat), ones_mat


def _split_dot_bwd(ones_mat, ct):
    return _split_dot_impl(ct, ones_mat), None


_split_dot.defvjp(_split_dot_fwd, _split_dot_bwd)


def _block_ones(n, group):
    idx = jnp.arange(n) // group
    return (idx[:, None] == idx[None, :]).astype(MXU_DTYPE)


def _rms(x, g):
    return x * lax.rsqrt(jnp.mean(x * x, axis=-1, keepdims=True) + NORM_EPS) * g


def fn_norm_mod(h, shift, scale, g):
    return (_rms(h, g) * (1.0 + scale) + shift,)


def fn_rwkv_prepare(ks, lora, w0_f, w0_b, a0_f, a0_b, w_up_f, w_up_b, a_up_f, a_up_b, g_up, k_k, k_a, ones64):
    kkr = ks * k_k
    kk = kkr * lax.rsqrt(_split_dot(kkr * kkr, ones64) + 1e-12)
    outs = [kk]
    th = jnp.tanh(lora)
    for w0, a0, w_up, a_up in ((w0_f, a0_f, w_up_f, a_up_f), (w0_b, a0_b, w_up_b, a_up_b)):
        w = jnp.exp(-W_DECAY_SCALE * jax.nn.sigmoid(w0 + _mxu_dot(th, w_up)))
        a = jax.nn.sigmoid(a0 + _mxu_dot(lora, a_up))
        kt = ks * (1.0 + (a - 1.0) * k_a)
        outs += [w, a * kk, kt]
    outs.append(_mxu_dot(jax.nn.sigmoid(lora), g_up))
    return tuple(outs)


def fn_merge(o_f, o_b, g_ret, y_f, y_b, r, kt_f, v, g_rw, r_k, ln_w, ln_b, ones64, ones128):
    o = o_f + o_b
    ret = o * lax.rsqrt(_split_dot(o * o, ones128) * (1.0 / RET_DH) + NORM_EPS) * (g_ret * jax.nn.sigmoid(g_ret))
    y = y_f + y_b
    mean = _split_dot(y, ones64) * (1.0 / RW_N)
    yc = y - mean
    var = _split_dot(yc * yc, ones64) * (1.0 / RW_N)
    y_n = yc * lax.rsqrt(var + GN_EPS) * ln_w + ln_b
    bonus = _split_dot(r * kt_f * r_k, ones64) * v
    return ret, (y_n + bonus) * g_rw


def fn_resid_norm_mod(x, mix, gate, shift, scale, g):
    h1 = x + gate * mix
    return h1, _rms(h1, g) * (1.0 + scale) + shift


def relu2(z):
    return jnp.square(jnp.maximum(z, 0.0))


def relu2_backward(act, dact, name):
    bsz, n_tok, width = act.shape

    def body(a_ref, d_ref, du_ref, db_ref):
        du = d_ref[...].astype(f32) * (2.0 * jnp.sqrt(a_ref[...].astype(f32)))
        du_ref[...] = du.astype(du_ref.dtype)

        @pl.when(jnp.logical_and(pl.program_id(0) == 0, pl.program_id(1) == 0))
        def _():
            db_ref[...] = jnp.zeros_like(db_ref)

        db_ref[...] += jnp.sum(du, axis=0, keepdims=True)

    tile = pl.BlockSpec((None, TOK_TILE, width), lambda b, i: (b, i, 0))
    row = pl.BlockSpec((1, width), lambda b, i: (0, 0))
    return pl.pallas_call(body, grid=(bsz, n_tok // TOK_TILE), in_specs=[tile, tile], out_specs=[tile, row],
                          out_shape=[jax.ShapeDtypeStruct(act.shape, MXU_DTYPE), jax.ShapeDtypeStruct((1, width), f32)],
                          compiler_params=_cparams(("arbitrary", "arbitrary")), name=name)(act, dact)


def fn_loss(h1, f, tgt, gate, b2, g):
    y = _rms(h1 + gate * (f + b2), g)
    err = jnp.square(y - tgt)
    return 0.5 * jnp.sum(jnp.mean(err, axis=-1, keepdims=True), axis=0, keepdims=True)


def loss_and_grads(h1, f, tgt, gate, b2, g, bsz, n_tiles):
    def body(h1_ref, f_ref, t_ref, gate_ref, b2_ref, g_ref, loss_ref, dh1_ref, df_ref, dgate_ref, db2_ref, dg_ref):
        b, i = pl.program_id(0), pl.program_id(1)
        tgt_v = t_ref[...]
        loss, vjp = jax.vjp(lambda a, c, e, p, q: fn_loss(a, c, tgt_v, e, p, q),
                            h1_ref[...], f_ref[...], gate_ref[...], b2_ref[...], g_ref[...])
        dh1, df, dgate, db2, dg = vjp(jnp.ones((1, 1), f32))
        dh1_ref[...] = dh1
        df_ref[...] = df.astype(df_ref.dtype)

        @pl.when(i == 0)
        def _():
            dgate_ref[...] = jnp.zeros_like(dgate_ref)

        @pl.when(jnp.logical_and(b == 0, i == 0))
        def _():
            loss_ref[...] = jnp.zeros_like(loss_ref)
            db2_ref[...] = jnp.zeros_like(db2_ref)
            dg_ref[...] = jnp.zeros_like(dg_ref)

        dgate_ref[...] += dgate
        db2_ref[...] += db2
        dg_ref[...] += dg
        loss_ref[...] += jnp.broadcast_to(loss, loss_ref.shape)

    tile = pl.BlockSpec((None, TOK_TILE, D_MODEL), lambda b, i: (b, i, 0))
    row = pl.BlockSpec((1, D_MODEL), lambda b, i: (0, 0))
    seg = pl.BlockSpec((None, None, 1, D_MODEL), lambda b, i: (b, 0, 0, 0))
    t_tok = n_tiles * TOK_TILE
    return pl.pallas_call(
        body, grid=(bsz, n_tiles), in_specs=[tile, tile, tile, seg, row, row],
        out_specs=[pl.BlockSpec((1, 128), lambda b, i: (0, 0)), tile, tile, seg, row, row],
        out_shape=[jax.ShapeDtypeStruct((1, 128), f32), jax.ShapeDtypeStruct((bsz, t_tok, D_MODEL), f32),
                   jax.ShapeDtypeStruct((bsz, t_tok, D_MODEL), MXU_DTYPE),
                   jax.ShapeDtypeStruct((bsz, 1, 1, D_MODEL), f32),
                   jax.ShapeDtypeStruct((1, D_MODEL), f32), jax.ShapeDtypeStruct((1, D_MODEL), f32)],
        compiler_params=_cparams(("arbitrary", "arbitrary")), name="loss_and_grads")(h1, f, tgt, gate, b2, g)


SHIFT_BLOCK = SHIFT_COLS
HALO_ROWS = 8


def _shift_specs(n_tok, col0):
    per_tile = TOK_TILE // HALO_ROWS
    last = n_tok // HALO_ROWS - 1
    tile = pl.BlockSpec((None, TOK_TILE, SHIFT_BLOCK), lambda j, b, i: (b, i, col0 + j))
    prev = pl.BlockSpec((None, HALO_ROWS, SHIFT_BLOCK),
                        lambda j, b, i: (b, jnp.maximum(i * per_tile - 1, 0), col0 + j))
    nxt = pl.BlockSpec((None, HALO_ROWS, SHIFT_BLOCK),
                       lambda j, b, i: (b, jnp.minimum((i + 1) * per_tile, last), col0 + j))
    return tile, prev, nxt


def _shifted(p, prev_ref, next_ref, is_first, is_last):
    row = lax.broadcasted_iota(jnp.int32, p.shape, 0)
    prev_row = jnp.where(is_first, 0.0, prev_ref[HALO_ROWS - 1:HALO_ROWS, :].astype(f32))
    next_row = jnp.where(is_last, 0.0, next_ref[0:1, :].astype(f32))
    prev = jnp.where(row == 0, prev_row, pltpu.roll(p, 1, axis=0))
    nxt = jnp.where(row == TOK_TILE - 1, next_row, pltpu.roll(p, TOK_TILE - 1, axis=0))
    return prev, nxt


def token_shift(px, mu, seg_first, seg_last):
    bsz, n_tok, _ = px.shape
    n_tiles = n_tok // TOK_TILE

    def body(p_ref, prev_ref, next_ref, mu_ref, o_ref):
        i = pl.program_id(2)
        p = p_ref[...]
        prev, nxt = _shifted(p, prev_ref, next_ref, seg_first(i), seg_last(i))
        o_ref[...] = p + mu_ref[0:1, :] * (prev - p) + mu_ref[1:2, :] * (nxt - p)

    tile, prev, nxt = _shift_specs(n_tok, 0)
    return pl.pallas_call(
        body, grid=(SHIFT_COLS // SHIFT_BLOCK, bsz, n_tiles),
        in_specs=[tile, prev, nxt, pl.BlockSpec((2, SHIFT_BLOCK), lambda j, b, i: (0, j))],
        out_specs=pl.BlockSpec((None, TOK_TILE, SHIFT_BLOCK), lambda j, b, i: (b, i, j)),
        out_shape=jax.ShapeDtypeStruct((bsz, n_tok, SHIFT_COLS), f32),
        compiler_params=_cparams(("parallel", "parallel", "parallel")), name="token_shift")(px, px, px, mu)


def token_shift_bwd(dps, px, mu, seg_first, seg_last):
    bsz, n_tok, _ = px.shape
    n_tiles = n_tok // TOK_TILE

    def body(d_ref, dprev_ref, dnext_ref, p_ref, prev_ref, next_ref, mu_ref, dp_ref, dmu_ref):
        b, i = pl.program_id(1), pl.program_id(2)
        first, last = seg_first(i), seg_last(i)
        d, p = d_ref[...], p_ref[...]
        d_prev, d_next = _shifted(d, dprev_ref, dnext_ref, first, last)
        p_prev, p_next = _shifted(p, prev_ref, next_ref, first, last)
        mu0, mu1 = mu_ref[0:1, :], mu_ref[1:2, :]
        dp_ref[...] = (d + mu0 * (d_next - d) + mu1 * (d_prev - d)).astype(dp_ref.dtype)

        @pl.when(jnp.logical_and(b == 0, i == 0))
        def _():
            dmu_ref[...] = jnp.zeros_like(dmu_ref)

        dmu_ref[0:1, :] += jnp.sum(d * (p_prev - p), axis=0, keepdims=True)
        dmu_ref[1:2, :] += jnp.sum(d * (p_next - p), axis=0, keepdims=True)

    dtile, dprev, dnext = _shift_specs(n_tok, 0)
    tile, prev, nxt = _shift_specs(n_tok, 0)
    mu_spec = pl.BlockSpec((2, SHIFT_BLOCK), lambda j, b, i: (0, j))
    return pl.pallas_call(
        body, grid=(SHIFT_COLS // SHIFT_BLOCK, bsz, n_tiles),
        in_specs=[dtile, dprev, dnext, tile, prev, nxt, mu_spec],
        out_specs=[pl.BlockSpec((None, TOK_TILE, SHIFT_BLOCK), lambda j, b, i: (b, i, j)), mu_spec],
        out_shape=[jax.ShapeDtypeStruct((bsz, n_tok, SHIFT_COLS), MXU_DTYPE),
                   jax.ShapeDtypeStruct((2, SHIFT_COLS), f32)],
        compiler_params=_cparams(("arbitrary", "arbitrary", "arbitrary")), name="token_shift_bwd")(
        dps, dps, dps, px, px, px, mu)


def _dg(a, b, ca, cb):
    return lax.dot_general(a.astype(MXU_DTYPE), b.astype(MXU_DTYPE), (((ca,), (cb,)), ((), ())),
                           preferred_element_type=f32)


@jax.custom_vjp
def _mm_nt(a, b):
    return _dg(a, b, 1, 1)


_mm_nt.defvjp(lambda a, b: (_dg(a, b, 1, 1), (a, b)),
              lambda res, ct: (_dg(ct, res[1], 1, 0), _dg(ct, res[0], 0, 0)))


@jax.custom_vjp
def _mm_tn(a, b):
    return _dg(a, b, 0, 0)


_mm_tn.defvjp(lambda a, b: (_dg(a, b, 0, 0), (a, b)),
              lambda res, ct: (_dg(res[1], ct, 1, 1), _dg(res[0], ct, 1, 0)))


ROTARY_PAIR = RET_DH // 4


def _swap_pairs_impl(t):
    lane = lax.broadcasted_iota(jnp.int32, t.shape, 1)
    return jnp.where(lane % (2 * ROTARY_PAIR) < ROTARY_PAIR, pltpu.roll(t, RET_DH - ROTARY_PAIR, axis=1),
                     pltpu.roll(t, ROTARY_PAIR, axis=1))


@jax.custom_vjp
def _swap_pairs(t):
    return _swap_pairs_impl(t)


_swap_pairs.defvjp(lambda t: (_swap_pairs_impl(t), None), lambda _, ct: (_swap_pairs_impl(ct),))


def _ret_chunk(state, q_raw, k_raw, v, cos, sin, ld_row, head, reverse):
    c = RET_CHUNK
    lane = lax.broadcasted_iota(jnp.int32, ld_row.shape, 1)
    lg = -jnp.exp(jnp.sum(jnp.where(lane == head, ld_row, 0.0), axis=-1, keepdims=True))
    rot = lambda t: t * cos + _swap_pairs(t) * sin
    q = rot(q_raw)
    k = rot(k_raw) * (RET_DH ** -0.5)
    ti = lax.broadcasted_iota(jnp.int32, (c, 1), 0).astype(f32)
    tj = lax.broadcasted_iota(jnp.int32, (1, c), 1).astype(f32)
    if not reverse:
        dist, mask, q_exp, k_exp = ti - tj, (ti - tj) >= 0, ti + 1.0, c - 1.0 - ti
    else:
        dist, mask, q_exp, k_exp = tj - ti, (tj - ti) > 0, c - ti, ti
    decay = jnp.where(mask, jnp.exp(lg * jnp.maximum(dist, 0.0)), 0.0)
    scores = _mm_nt(q, k) * decay
    out = _mxu_dot(scores, v) + _mxu_dot(q * jnp.exp(lg * q_exp), state)
    new_state = state * jnp.exp(lg * c) + _mm_tn(k * jnp.exp(lg * k_exp), v)
    return out, new_state


def _ret_specs(bsz, order):
    tok = lambda col=0: pl.BlockSpec((bsz, RET_CHUNK, RET_W), lambda i: (0, order(i), col))
    tab = pl.BlockSpec((RET_CHUNK, RET_DH), lambda i: (order(i), 0))
    ld = pl.BlockSpec((1, RET_DH), lambda i: (0, 0))
    return tok, tab, ld


def retention_fwd(px, cos, sin, ld_row, order, reverse, name):
    bsz, n_tok, _ = px.shape
    n_ch = n_tok // RET_CHUNK

    def body(q_ref, k_ref, v_ref, cos_ref, sin_ref, ld_ref, o_ref, sv_ref, st_ref):
        @pl.when(pl.program_id(0) == 0)
        def _():
            st_ref[...] = jnp.zeros_like(st_ref)

        for b in range(bsz):
            for h in range(RET_HEADS):
                sl = slice(h * RET_DH, (h + 1) * RET_DH)
                s = st_ref[b, h]
                sv_ref[b, h] = s
                o, s_new = _ret_chunk(s, q_ref[b, :, sl], k_ref[b, :, sl], v_ref[b, :, sl], cos_ref[...], sin_ref[...],
                                      ld_ref[...], h, reverse)
                o_ref[b, :, sl] = o
                st_ref[b, h] = s_new

    tok, tab, ld = _ret_specs(bsz, order)
    return pl.pallas_call(
        body, grid=(n_ch,), in_specs=[tok(0), tok(1), tok(2), tab, tab, ld],
        out_specs=[tok(), pl.BlockSpec((bsz, None, RET_HEADS, RET_DH, RET_DH), lambda i: (0, i, 0, 0, 0))],
        out_shape=[jax.ShapeDtypeStruct((bsz, n_tok, RET_W), f32),
                   jax.ShapeDtypeStruct((bsz, n_ch, RET_HEADS, RET_DH, RET_DH), f32)],
        scratch_shapes=[pltpu.VMEM((bsz, RET_HEADS, RET_DH, RET_DH), f32)],
        compiler_params=_cparams(("arbitrary",)), name=name)(px, px, px, cos, sin, ld_row)


def retention_bwd(do, px, states, cos, sin, ld_row, order, reverse, name):
    bsz, n_tok, _ = px.shape
    n_ch = n_tok // RET_CHUNK
    back = lambda i: order(n_ch - 1 - i)

    def body(do_ref, q_ref, k_ref, v_ref, sv_ref, cos_ref, sin_ref, ld_ref,
             dq_ref, dk_ref, dv_ref, dld_ref, dst_ref):
        @pl.when(pl.program_id(0) == 0)
        def _():
            dst_ref[...] = jnp.zeros_like(dst_ref)
            dld_ref[...] = jnp.zeros_like(dld_ref)

        cos_v, sin_v = cos_ref[...], sin_ref[...]
        for b in range(bsz):
            for h in range(RET_HEADS):
                sl = slice(h * RET_DH, (h + 1) * RET_DH)
                f = lambda s, q, k, v, ld, h=h: _ret_chunk(s, q, k, v, cos_v, sin_v, ld, h, reverse)
                _, vjp = jax.vjp(f, sv_ref[b, h], q_ref[b, :, sl], k_ref[b, :, sl], v_ref[b, :, sl], ld_ref[...])
                ds, dq, dk, dv, dld = vjp((do_ref[b, :, sl], dst_ref[b, h]))
                dst_ref[b, h] = ds
                dq_ref[b, :, sl] = dq
                dk_ref[b, :, sl] = dk
                dv_ref[b, :, sl] = dv
                dld_ref[...] += dld

    tok, tab, ld = _ret_specs(bsz, back)
    return pl.pallas_call(
        body, grid=(n_ch,),
        in_specs=[tok(), tok(0), tok(1), tok(2),
                  pl.BlockSpec((bsz, None, RET_HEADS, RET_DH, RET_DH), lambda i: (0, n_ch - 1 - i, 0, 0, 0)),
                  tab, tab, ld],
        out_specs=[tok(), tok(), tok(), ld],
        out_shape=[jax.ShapeDtypeStruct((bsz, n_tok, RET_W), f32)] * 3 + [jax.ShapeDtypeStruct((1, RET_DH), f32)],
        scratch_shapes=[pltpu.VMEM((bsz, RET_HEADS, RET_DH, RET_DH), f32)],
        compiler_params=_cparams(("arbitrary",)), name=name)(
        do, px, px, px, states, cos, sin, ld_row)


HALF_W = RW_W // 2


def _head_sum(x, ones):
    xm = x.astype(MXU_DTYPE)
    return jnp.concatenate([jnp.dot(xm[:, :HALF_W], ones, preferred_element_type=f32),
                            jnp.dot(xm[:, HALF_W:], ones, preferred_element_type=f32)], axis=1)


def _stack(parts):
    return jnp.concatenate(parts, axis=0)


def _row(ref, b, t):
    return ref[b, pl.ds(t, 1), :]


SCAN_DIRS = ((False, True), (True, False))
RW_HEADS = RW_W // RW_N
HEAD_ROWS_PAD = 16


def _head_rows(row, mask):
    return jnp.broadcast_to(row, mask.shape) * mask


def _outer(per_value, row, mask_pad):
    return lax.dot_general(per_value.astype(MXU_DTYPE), _head_rows(row, mask_pad).astype(MXU_DTYPE),
                           (((0,), (0,)), ((), ())), preferred_element_type=f32)


def _read(states, rows, mask, more_rows=()):
    lhs = _stack([_head_rows(r, mask) for r in list(rows) + list(more_rows)])
    return lax.dot_general(lhs.astype(MXU_DTYPE), _stack(states).astype(MXU_DTYPE), (((1,), (1,)), ((), ())),
                           preferred_element_type=f32)


def _own_block(raw, b):
    lanes = raw[:, RW_N * b:RW_N * (b + 1)]
    turned = _stack([lanes[RW_HEADS * b:], lanes[:RW_HEADS * b]]) if b else lanes
    if turned.shape[0] < HEAD_ROWS_PAD:
        turned = _stack([turned, jnp.zeros((HEAD_ROWS_PAD - turned.shape[0], RW_N), f32)])
    return turned[:HEAD_ROWS_PAD]


def _row_from_heads(per_value, state, mask_pad):
    full = jnp.dot(per_value.astype(MXU_DTYPE), state.astype(MXU_DTYPE), preferred_element_type=f32)
    return jnp.sum(full * mask_pad, axis=0, keepdims=True)


def _scan_specs(bsz, order):
    rows = lambda col=0: pl.BlockSpec((bsz, SCAN_CHUNK, RW_W), lambda i: (0, order(i), col))
    per_value = pl.BlockSpec((bsz, SCAN_CHUNK, HEAD_ROWS_PAD, RW_N), lambda i: (0, order(i), 0, 0))
    states = pl.BlockSpec((SCAN_CHUNK, bsz, RW_N, RW_W), lambda i: (order(i), 0, 0, 0))
    blocks = pl.BlockSpec((SCAN_CHUNK, RW_HEADS * bsz, RW_N * bsz), lambda i: (order(i), 0, 0))
    return rows, per_value, states, blocks


def _mxu_operands(states):
    return [s.astype(MXU_DTYPE) for s in states]


def _removed(states_m, kk_t, ones, bsz):
    removed = _head_sum(_stack([states_m[b] * kk_t[b].astype(MXU_DTYPE) for b in range(bsz)]), ones)
    return [removed[b * RW_N:(b + 1) * RW_N] for b in range(bsz)]


def _advance(sp, rem, w_t, b_t, vk, bsz):
    return [sp[b] * w_t[b] - rem[b] * b_t[b] + vk[b] for b in range(bsz)]


def heads_to_rows(a):
    b, t, _ = a.shape
    return jnp.pad(a.astype(MXU_DTYPE).reshape(b, t, RW_HEADS, RW_N),
                   ((0, 0), (0, 0), (0, HEAD_ROWS_PAD - RW_HEADS), (0, 0)))


def _blocks_to_rows(raw_ref, first, row_ref, bsz):
    steps = pl.ds(first, SCAN_CHUNK)
    for b in range(bsz):
        for h in range(RW_HEADS):
            row_ref[b, :, h * RW_N:(h + 1) * RW_N] = raw_ref[steps, RW_HEADS * b + h, RW_N * b:RW_N * (b + 1)]


N_ROWS_FWD = 5
N_ROWS_BWD = 5


def _scan_consts(bsz):
    head = (jnp.arange(RW_W)[None, :] // RW_N == jnp.arange(RW_HEADS)[:, None]).astype(f32)
    return head, jnp.pad(head, ((0, HEAD_ROWS_PAD - RW_HEADS), (0, 0))), _block_ones(HALF_W, RW_N)


def _const_specs(consts):
    return [pl.BlockSpec(c.shape, lambda i: (0, 0)) for c in consts]


def rwkv_scan_fwd(rows_in, v_heads, orders, name):
    bsz, n_tok, _ = rows_in[0][0][0].shape
    n_ch = n_tok // SCAN_CHUNK
    rng = range(bsz)
    consts = _scan_consts(bsz)

    def body(*refs):
        rows = [refs[:N_ROWS_FWD], refs[N_ROWS_FWD:2 * N_ROWS_FWD]]
        (v0, v1, head_ref, pad_ref, ones_ref, y0, y1, h0, h1, f0, f1, m0, m1, s0, s1, late_ref,
         raw_ref) = refs[2 * N_ROWS_FWD:]
        v_refs, y_refs, hist_refs, final_refs, s_refs = (v0, v1), (y0, y1), (h0, h1), (f0, f1), (s0, s1)
        removed_refs = (m0, m1)
        n_blk = RW_HEADS * bsz
        head_v, pad_v, ones_v = head_ref[...], pad_ref[...], ones_ref[...]
        for d in range(2):
            @pl.when(pl.program_id(0) == 0)
            def _(d=d):
                s_refs[d][...] = jnp.zeros_like(s_refs[d])

        def step(j, carry):
            ts = [SCAN_CHUNK - 1 - j if reverse else j for reverse, _ in SCAN_DIRS]
            sps = [[s_refs[d][b] for b in rng] for d in range(2)]
            sps_m = [_mxu_operands(sps[d]) for d in range(2)]
            vks = [[_outer(v_refs[d][b, ts[d]], _row(rows[d][4], b, ts[d]), pad_v) for b in rng] for d in range(2)]
            rems = [_removed(sps_m[d], [_row(rows[d][1], b, ts[d]) for b in rng], ones_v, bsz) for d in range(2)]
            for d, (reverse, inclusive) in enumerate(SCAN_DIRS):
                r_ref = rows[d][0]
                read_at = jnp.maximum(j - 1, 0) if inclusive else ts[d]
                both = _read(sps_m[d], [_row(r_ref, b, read_at) for b in rng], head_v,
                             [_row(rows[d][1], b, ts[d]) for b in rng])
                if inclusive:
                    late_ref[j] = both[:n_blk]
                else:
                    raw_ref[ts[d]] = both[:n_blk]
                removed_refs[d][ts[d]] = both[n_blk:]
            for d in range(2):
                new = _advance(sps[d], rems[d], [_row(rows[d][2], b, ts[d]) for b in rng],
                               [_row(rows[d][3], b, ts[d]) for b in rng], vks[d], bsz)
                for b in rng:
                    hist_refs[d][ts[d], b] = sps_m[d][b]
                    s_refs[d][b] = new[b]
            return carry

        lax.fori_loop(0, SCAN_CHUNK, step, 0, unroll=SCAN_UNROLL)
        for d, (reverse, inclusive) in enumerate(SCAN_DIRS):
            final_refs[d][...] = s_refs[d][...]
            if inclusive:
                assert not reverse
                last = SCAN_CHUNK - 1
                late_ref[SCAN_CHUNK] = _read(_mxu_operands([s_refs[d][b] for b in rng]),
                                             [rows[d][0][b, last:last + 1, :] for b in rng], head_v)
                _blocks_to_rows(late_ref, 1, y_refs[d], bsz)
            else:
                _blocks_to_rows(raw_ref, 0, y_refs[d], bsz)

    specs = [_scan_specs(bsz, orders[d]) for d in range(2)]
    state = pltpu.VMEM((bsz, RW_N, RW_W), f32)
    late = pltpu.VMEM((SCAN_CHUNK + 1, RW_HEADS * bsz, RW_N * bsz), f32)
    raw = pltpu.VMEM((SCAN_CHUNK, RW_HEADS * bsz, RW_N * bsz), f32)
    final_spec = pl.BlockSpec((bsz, RW_N, RW_W), lambda i: (0, 0, 0))
    return pl.pallas_call(
        body, grid=(n_ch,),
        in_specs=[specs[d][0](col) for d in range(2) for _, col in rows_in[d]] + [specs[0][1], specs[1][1]]
        + _const_specs(consts),
        out_specs=[specs[0][0](), specs[1][0](), specs[0][2], specs[1][2], final_spec, final_spec,
                   specs[0][3], specs[1][3]],
        out_shape=[jax.ShapeDtypeStruct((bsz, n_tok, RW_W), f32)] * 2
        + [jax.ShapeDtypeStruct((n_tok, bsz, RW_N, RW_W), MXU_DTYPE)] * 2
        + [jax.ShapeDtypeStruct((bsz, RW_N, RW_W), f32)] * 2
        + [jax.ShapeDtypeStruct((n_tok, RW_HEADS * bsz, RW_N * bsz), f32)] * 2,
        scratch_shapes=[state, state, late, raw],
        compiler_params=_cparams(("arbitrary",)), name=name)(
        *[a for d in range(2) for a, _ in rows_in[d]], v_heads, v_heads, *consts)


def rwkv_scan_bwd(rows_in, v_heads, dy_heads, hists, finals, removed, orders, name):
    bsz, n_tok, _ = rows_in[0][0][0].shape
    n_ch = n_tok // SCAN_CHUNK
    backs = [functools.partial(lambda i, order: order(n_ch - 1 - i), order=orders[d]) for d in range(2)]
    rng = range(bsz)
    consts = _scan_consts(bsz)
    n_out, n_scr = 6, 6

    def body(*refs):
        rows = [refs[:N_ROWS_BWD], refs[N_ROWS_BWD:2 * N_ROWS_BWD]]
        rest = refs[2 * N_ROWS_BWD:]
        v_refs, dy_refs, hist_refs, final_refs, removed_refs = rest[0:2], rest[2:4], rest[4:6], rest[6:8], rest[8:10]
        head_ref, pad_ref, ones_ref = rest[10:13]
        outs = [rest[13:13 + n_out], rest[13 + n_out:13 + 2 * n_out]]
        scr = [rest[13 + 2 * n_out:13 + 2 * n_out + n_scr], rest[13 + 2 * n_out + n_scr:]]
        n_blk = RW_HEADS * bsz
        head_v, pad_v, ones_v = head_ref[...], pad_ref[...], ones_ref[...]
        for d in range(2):
            @pl.when(pl.program_id(0) == 0)
            def _(d=d):
                scr[d][1][...] = jnp.zeros_like(scr[d][1])
                scr[d][0][...] = final_refs[d][...]

        def step_of(j, reverse):
            return j if reverse else SCAN_CHUNK - 1 - j

        for d, (reverse, _) in enumerate(SCAN_DIRS):
            t0 = step_of(0, reverse)
            for b in rng:
                scr[d][3][b] = _outer(dy_refs[d][b, t0], rows[d][0][b, t0:t0 + 1, :], pad_v)

        def bstep(j, carry):
            ts = [step_of(j, reverse) for reverse, _ in SCAN_DIRS]
            reads = [[scr[d][3][b] for b in rng] for d in range(2)]
            dss = []
            for d, (_, inclusive) in enumerate(SCAN_DIRS):
                ds = [scr[d][1][b] for b in rng]
                dss.append([ds[b] + reads[d][b] for b in rng] if inclusive else ds)
            dss_m = [_mxu_operands(dss[d]) for d in range(2)]
            nexts = []
            for d, (reverse, _) in enumerate(SCAN_DIRS):
                t_next = step_of(jnp.minimum(j + 1, SCAN_CHUNK - 1), reverse)
                nexts.append([_outer(dy_refs[d][b, t_next], _row(rows[d][0], b, t_next), pad_v) for b in rng])
            drems = [_removed(dss_m[d], [-_row(rows[d][3], b, ts[d]) for b in rng], ones_v, bsz) for d in range(2)]
            for d in range(2):
                for b in rng:
                    scr[d][3][b] = nexts[d][b]
                both = _read(dss_m[d], [_row(rows[d][4], b, ts[d]) for b in rng], head_v,
                             [-_row(rows[d][3], b, ts[d]) for b in rng])
                scr[d][4][ts[d]] = both[:n_blk]
                scr[d][5][ts[d]] = both[n_blk:]
            for d, (_, inclusive) in enumerate(SCAN_DIRS):
                _, kk_ref, w_ref, _, _ = rows[d]
                _, ds_ref, dsh_ref = scr[d][:3]
                for b in rng:
                    dsh_ref[ts[d], b] = dss[d][b]
                    dsp = dss[d][b] * _row(w_ref, b, ts[d]) + drems[d][b] * _row(kk_ref, b, ts[d])
                    ds_ref[b] = dsp if inclusive else dsp + reads[d][b]
            return carry

        lax.fori_loop(0, SCAN_CHUNK, bstep, 0, unroll=SCAN_UNROLL)

        rsum = lambda z: jnp.sum(z, axis=0, keepdims=True)
        for d, (reverse, inclusive) in enumerate(SCAN_DIRS):
            dr_ref, dkk_ref, dw_ref, db_ref, dkt_ref, dv_ref = outs[d]
            after_ref, _, dsh_ref, _, dv_raw_ref, dremt_ref = scr[d]
            hist_ref, removed_ref = hist_refs[d], removed_refs[d]
            _blocks_to_rows(dv_raw_ref, 0, dv_ref, bsz)
            for t in range(SCAN_CHUNK):
                ts = slice(t, t + 1)
                after = t - 1 if reverse else t + 1
                for b in rng:
                    sp_m, ds = hist_ref[t, b], dsh_ref[t, b]
                    sp = sp_m.astype(f32)
                    if not inclusive:
                        seen = sp_m
                    else:
                        seen = hist_ref[after, b] if 0 <= after < SCAN_CHUNK else after_ref[b]
                    dr_ref[b, ts, :] = _row_from_heads(dy_refs[d][b, t], seen, pad_v)
                    dkt_ref[b, ts, :] = _row_from_heads(v_refs[d][b, t], ds, pad_v)
                    dw_ref[b, ts, :] = rsum(ds * sp)
                    db_ref[b, ts, :] = -_row_from_heads(_own_block(removed_ref[t], b), ds, pad_v)
                    dkk_ref[b, ts, :] = _row_from_heads(_own_block(dremt_ref[t], b), sp_m, pad_v)
            if inclusive:
                first = SCAN_CHUNK - 1 if reverse else 0
                for b in rng:
                    after_ref[b] = hist_ref[first, b].astype(f32)

    specs = [_scan_specs(bsz, backs[d]) for d in range(2)]
    hist = pltpu.VMEM((SCAN_CHUNK, bsz, RW_N, RW_W), f32)
    state = pltpu.VMEM((bsz, RW_N, RW_W), f32)
    final_spec = pl.BlockSpec((bsz, RW_N, RW_W), lambda i: (0, 0, 0))
    raw = pltpu.VMEM((SCAN_CHUNK, RW_HEADS * bsz, RW_N * bsz), f32)
    return pl.pallas_call(
        body, grid=(n_ch,),
        in_specs=[specs[d][0](col) for d in range(2) for _, col in rows_in[d]]
        + [specs[0][1], specs[1][1]] * 2 + [specs[0][2], specs[1][2], final_spec, final_spec, specs[0][3], specs[1][3]]
        + _const_specs(consts),
        out_specs=[specs[d][0]() for d in range(2) for _ in range(n_out)],
        out_shape=[jax.ShapeDtypeStruct((bsz, n_tok, RW_W), f32)] * (2 * n_out),
        scratch_shapes=[state, state, hist, state, raw, raw] * 2,
        compiler_params=_cparams(("arbitrary",)), name=name)(
        *[a for d in range(2) for a, _ in rows_in[d]], v_heads, v_heads, dy_heads, dy_heads, *hists, *finals, *removed, *consts)


MOD_NAMES = ("shift1", "scale1", "gate1", "shift2", "scale2", "gate2")


def _rope_tables(t_ctx, t_x):
    quarter = RET_DH // 4
    pos = jnp.arange(t_x)
    inv = jnp.power(ROPE_BASE, -jnp.arange(0, 2 * quarter, 2, dtype=f32) / (2 * quarter))
    ang_r = (pos // GRID_W).astype(f32)[:, None] * inv[None, :]
    ang_c = (pos % GRID_W).astype(f32)[:, None] * inv[None, :]
    cos = jnp.concatenate([jnp.cos(ang_r)] * 2 + [jnp.cos(ang_c)] * 2, axis=1)
    sin = jnp.concatenate([-jnp.sin(ang_r), jnp.sin(ang_r), -jnp.sin(ang_c), jnp.sin(ang_c)], axis=1)
    cos = jnp.concatenate([jnp.ones((t_ctx, RET_DH), f32), cos], axis=0)
    sin = jnp.concatenate([jnp.zeros((t_ctx, RET_DH), f32), sin], axis=0)
    return cos, sin


def _pad_rows(w, lo, total):
    return jnp.pad(w, ((lo, total - lo - w.shape[0]), (0, 0)))


LATE_WEIGHTS = ("w_out", "w_ff1", "w_ff2")


def layer_step(x, ctx, tgt, mod_x, mod_ctx, wt, late_weights=None, early_grads=None, last_grads=None):
    bsz, t_x, _ = x.shape
    t_c = ctx.shape[1]
    t_all = t_c + t_x
    n_ct, n_xt = t_c // TOK_TILE, t_x // TOK_TILE
    n_t = n_ct + n_xt
    assert t_c % TOK_TILE == 0 and t_x % TOK_TILE == 0 and t_c % RET_CHUNK == 0

    seg = lambda i: (i >= n_ct).astype(jnp.int32)
    seg_first = lambda i: jnp.logical_or(i == 0, i == n_ct)
    seg_last = lambda i: jnp.logical_or(i == n_ct - 1, i == n_t - 1)
    mod_all = {n: jnp.stack([jnp.broadcast_to(mod_ctx[k], (bsz, D_MODEL)), mod_x[:, k]], axis=1)[:, :, None, :]
               for k, n in enumerate(MOD_NAMES)}
    mod_lat = {n: mod_x[:, k][:, None, None, :] for k, n in enumerate(MOD_NAMES)}
    both = lambda n: Seg(mod_all[n], seg, seg_first)
    lat = lambda n: Seg(mod_lat[n], lambda i: 0, lambda i: i == 0)
    flat = lambda a: a.reshape(-1, a.shape[-1])

    def chunk_orders(n_ctx_chunks, n_chunks):
        fwd = lambda i: i
        bwd = lambda i: jnp.where(i < n_ctx_chunks, n_ctx_chunks - 1 - i, n_chunks + n_ctx_chunks - 1 - i)
        return fwd, bwd

    ones64, ones128 = _block_ones(RW_W, RW_N), _block_ones(RET_W, RET_DH)
    cos, sin = _rope_tables(t_c, t_x)
    ld_rows = [jnp.pad(wt["ret_log_decay"][d][None, :], ((0, 0), (0, RET_DH - RET_HEADS))) for d in range(2)]
    w_up_pad = [_pad_rows(wt["rwkv_w_up"][d], 0, LORA_W) for d in range(2)]
    a_up_pad = [_pad_rows(wt["rwkv_a_up"][d], DECAY_LORA, LORA_W) for d in range(2)]
    g_up_pad = _pad_rows(wt["rwkv_g_up"], DECAY_LORA + AAA_LORA, LORA_W)
    row = lambda a, d: a[d][None, :]

    h = jnp.concatenate([ctx, x], axis=1)
    norm1_ins = lambda: [Tiled(h), both("shift1"), both("scale1"), Glob(wt["norm1_g"])]
    (n1,) = ew_forward(fn_norm_mod, "norm1", bsz, n_t, norm1_ins(), [(D_MODEL, MXU_DTYPE)])
    px = matmul(flat(n1), wt["w_in"], "nn", "proj_in").reshape(bsz, t_all, IN_COLS)
    px_rw = px[..., RET_COLS:]
    ps = token_shift(px_rw, wt["rwkv_shift_mu"], seg_first, seg_last)

    def prep_ins():
        return [Tiled(ps, RW_W, 1), Tiled(ps, LORA_W, 3 * RW_W // LORA_W),
                Glob(row(wt["rwkv_w0"], 0)), Glob(row(wt["rwkv_w0"], 1)),
                Glob(row(wt["rwkv_a0"], 0)), Glob(row(wt["rwkv_a0"], 1)),
                Glob(w_up_pad[0]), Glob(w_up_pad[1]), Glob(a_up_pad[0]), Glob(a_up_pad[1]), Glob(g_up_pad),
                Glob(wt["rwkv_k_k"]), Glob(wt["rwkv_k_a"]), Glob(ones64)]

    kk, w_f, b_f, kt_f, w_b, b_b, kt_b, g_rw = ew_forward(fn_rwkv_prepare, "rwkv_prepare", bsz, n_t, prep_ins(),
                                                           [(RW_W, f32)] * 8)
    rw_order = chunk_orders(t_c // SCAN_CHUNK, t_all // SCAN_CHUNK)
    ret_order = chunk_orders(t_c // RET_CHUNK, t_all // RET_CHUNK)
    scan_rows = [[(ps, 0), (kk, 0), (w_f, 0), (b_f, 0), (kt_f, 0)], [(ps, 0), (kk, 0), (w_b, 0), (b_b, 0), (kt_b, 0)]]
    v_heads = heads_to_rows(ps[..., 2 * RW_W:3 * RW_W])
    y_f, y_b, *kept_states = rwkv_scan_fwd(scan_rows, v_heads, rw_order, "rwkv_scan_fwd")
    y = [y_f, y_b]
    o, ret_states = [], []
    for d in range(2):
        o_d, st_d = retention_fwd(px, cos, sin, ld_rows[d], ret_order[d], SCAN_DIRS[d][0], f"retention_fwd{d}")
        o.append(o_d), ret_states.append(st_d)

    def merge_ins(toff):
        return [Tiled(o[0], toff=toff), Tiled(o[1], toff=toff), Tiled(px, RET_W, 3, toff),
                Tiled(y[0], toff=toff), Tiled(y[1], toff=toff), Tiled(ps, RW_W, 0, toff), Tiled(kt_f, toff=toff),
                Tiled(ps, RW_W, 2, toff), Tiled(g_rw, toff=toff),
                Glob(wt["rwkv_r_k"]), Glob(wt["rwkv_ln_w"]), Glob(wt["rwkv_ln_b"]), Glob(ones64), Glob(ones128)]

    ret_out, rw_out = ew_forward(fn_merge, "merge_heads", bsz, n_xt, merge_ins(n_ct),
                                 [(RET_W, MXU_DTYPE), (RW_W, MXU_DTYPE)])
    merged = jnp.concatenate([ret_out, rw_out], axis=-1)
    if late_weights is not None:
        wt = {**wt, **late_weights(merged)}
    mix = matmul(flat(merged), wt["w_out"], "nn", "proj_out").reshape(bsz, t_x, D_MODEL)
    resid_ins = lambda: [Tiled(x), Tiled(mix), lat("gate1"), lat("shift2"), lat("scale2"), Glob(wt["norm2_g"])]
    h1, n2 = ew_forward(fn_resid_norm_mod, "resid_norm2", bsz, n_xt, resid_ins(), [(D_MODEL, f32), (D_MODEL, MXU_DTYPE)])
    act = matmul(flat(n2), wt["w_ff1"], "nn", "ff1", MXU_DTYPE, wt["b_ff1"], relu2).reshape(bsz, t_x, D_FF)
    ff = matmul(flat(act), wt["w_ff2"], "nn", "ff2").reshape(bsz, t_x, D_MODEL)

    g = {}
    loss, dh1, dff, dgate2, g["b_ff2"], g["final_g"] = loss_and_grads(
        h1, ff, tgt, mod_lat["gate2"], wt["b_ff2"], wt["final_g"], bsz, n_xt)
    dact = matmul(flat(dff), wt["w_ff2"], "nt", "ff2_dx", MXU_DTYPE).reshape(bsz, t_x, D_FF)
    g["w_ff2"] = matmul(flat(act), flat(dff), "tn", "ff2_dw", MXU_DTYPE)
    du, g["b_ff1"] = relu2_backward(act, dact, "relu2_bwd")
    dn2 = matmul(flat(du), wt["w_ff1"], "nt", "ff1_dx").reshape(bsz, t_x, D_MODEL)
    g["w_ff1"] = matmul(flat(n2), flat(du), "tn", "ff1_dw", MXU_DTYPE)
    dx_res, dmix, dgate1, dshift2, dscale2, g["norm2_g"] = ew_backward(
        fn_resid_norm_mod, "resid_norm2_bwd", bsz, n_xt, resid_ins(), [Tiled(dh1), Tiled(dn2)], [True] * 6,
        {1: MXU_DTYPE})
    dmerged = matmul(flat(dmix), wt["w_out"], "nt", "proj_out_dx").reshape(bsz, t_x, D_MODEL)
    g["w_out"] = matmul(flat(merged), flat(dmix), "tn", "proj_out_dw", MXU_DTYPE)
    if early_grads is not None:
        token = early_grads({n: g.pop(n) for n in LATE_WEIGHTS})
        wt = {**wt, "rwkv_r_k": wt["rwkv_r_k"] + token[:1, :1]}
    (do, dg_ret, dy, dr_m, dkt_m, dv_m, dg_rw, g["rwkv_r_k"], g["rwkv_ln_w"], g["rwkv_ln_b"]) = ew_backward(
        fn_merge, "merge_heads_bwd", bsz, n_xt, merge_ins(0),
        [Tiled(dmerged, RET_W, 0, -n_ct), Tiled(dmerged, RW_W, 1, -n_ct)],
        [True, False, True, True, False, True, True, True, True, True, True, True, False, False], lead=n_ct)

    dqkv, dld = [], []
    for d in range(2):
        *dqkv_d, dld_d = retention_bwd(do, px, ret_states[d], cos, sin, ld_rows[d], ret_order[d],
                                       SCAN_DIRS[d][0], f"retention_bwd{d}")
        dqkv.append(dqkv_d), dld.append(dld_d[0, :RET_HEADS])
    g["ret_log_decay"] = jnp.stack(dld)
    (dr_f, dkk_f, dw_f, db_f, dkt_f, dv_f, dr_b, dkk_b, dw_b, db_b, dkt_b, dv_b) = rwkv_scan_bwd(
        scan_rows, v_heads, heads_to_rows(dy), kept_states[:2], kept_states[2:4], kept_states[4:], rw_order, "rwkv_scan_bwd")
    prep_cts = [(dkk_f, dkk_b), dw_f, db_f, (dkt_f, dkt_m), dw_b, db_b, dkt_b, dg_rw]
    (dks, dlora, dw0_f, dw0_b, da0_f, da0_b, dwup_f, dwup_b, daup_f, daup_b, dgup, g["rwkv_k_k"],
     g["rwkv_k_a"]) = ew_backward(fn_rwkv_prepare, "rwkv_prepare_bwd", bsz, n_t, prep_ins(),
                                  [tuple(map(Tiled, c)) if isinstance(c, tuple) else Tiled(c) for c in prep_cts],
                                  [True] * 13 + [False])
    g["rwkv_w0"] = jnp.concatenate([dw0_f, dw0_b], axis=0)
    g["rwkv_a0"] = jnp.concatenate([da0_f, da0_b], axis=0)
    g["rwkv_w_up"] = jnp.stack([dwup_f[:DECAY_LORA], dwup_b[:DECAY_LORA]])
    g["rwkv_a_up"] = jnp.stack([daup_f[DECAY_LORA:DECAY_LORA + AAA_LORA], daup_b[DECAY_LORA:DECAY_LORA + AAA_LORA]])
    g["rwkv_g_up"] = dgup[DECAY_LORA + AAA_LORA:]
    dps = jnp.concatenate([dr_f + dr_b + dr_m, dks, dv_f + dv_b + dv_m, dlora], axis=-1)
    dp_rw, g["rwkv_shift_mu"] = token_shift_bwd(dps, px_rw, wt["rwkv_shift_mu"], seg_first, seg_last)
    dpx = jnp.concatenate([(dqkv[0][k] + dqkv[1][k]).astype(MXU_DTYPE) for k in range(3)]
                          + [dg_ret.astype(MXU_DTYPE), dp_rw], axis=-1)
    g["w_in"] = matmul(flat(n1), flat(dpx), "tn", "proj_in_dw", MXU_DTYPE)
    after_start = None
    if last_grads is not None:
        token = last_grads(g.pop("w_in"), {n: g.pop(n) for n in LAST_SHARDED})
        after_start = jnp.zeros((1, D_MODEL), f32) + token[:1, :1]
    dn1 = matmul(flat(dpx), wt["w_in"], "nt", "proj_in_dx", bias=after_start).reshape(bsz, t_all, D_MODEL)
    dh, dshift1, dscale1, g["norm1_g"] = ew_backward(fn_norm_mod, "norm1_bwd", bsz, n_t, norm1_ins(), [Tiled(dn1)],
                                                     [True] * 4)
    grad_x = dh[:, t_c:] + dx_res
    zeros = jnp.zeros((D_MODEL,), f32)
    g["mod_x"] = jnp.stack([dshift1[:, 1, 0], dscale1[:, 1, 0], dgate1[:, 0, 0], dshift2[:, 0, 0], dscale2[:, 0, 0],
                            dgate2[:, 0, 0]], axis=1)
    g["mod_ctx"] = jnp.stack([dshift1[:, 0, 0].sum(0), dscale1[:, 0, 0].sum(0), zeros, zeros, zeros, zeros])
    return loss, grad_x, g


MESH_ID = pl.DeviceIdType.MESH
ALL_PEERS = [(dx, dy, dc) for dx in (0, 1) for dy in (0, 1) for dc in (0, 1)][1:]
CHIP_PEERS = [(1, 0, 0), (0, 1, 0), (1, 1, 0)]
CHIP_SLOTS = (0, 2, 4, 6)


def _mesh_pos():
    return lax.axis_index("x"), lax.axis_index("y"), lax.axis_index("c")


def _device_slot():
    x, y, c = _mesh_pos()
    return 4 * x + 2 * y + c


def sibling_swap(arrs, name):
    n = len(arrs)

    def body(*refs):
        in_refs, out_refs = refs[:n], refs[n:2 * n]
        send_sems, recv_sems = refs[2 * n:]
        x, y, c = _mesh_pos()
        copies = [pltpu.make_async_remote_copy(src_ref=in_refs[a], dst_ref=out_refs[a], send_sem=send_sems.at[a],
                                               recv_sem=recv_sems.at[a], device_id=(x, y, 1 - c),
                                               device_id_type=MESH_ID) for a in range(n)]
        for cp in copies:
            cp.start()
        for cp in copies:
            cp.wait()

    any_spec = pl.BlockSpec(memory_space=pl.ANY)
    res = pl.pallas_call(
        body, in_specs=[any_spec] * n, out_specs=[any_spec] * n,
        out_shape=[jax.ShapeDtypeStruct(a.shape, a.dtype) for a in arrs],
        scratch_shapes=[pltpu.SemaphoreType.DMA((n,)), pltpu.SemaphoreType.DMA((n,))],
        name=name)(*arrs)
    return list(res)


def exchange(arrs, gather, peers, name, by_chip=False, own=True):
    n, n_peers = len(arrs), len(peers)
    n_slots = N_SHARDS if by_chip else N_DEV
    slot = (lambda x, y, c: 2 * x + y) if by_chip else (lambda x, y, c: 4 * x + 2 * y + c)

    def body(*refs):
        in_refs, out_refs = refs[:n], refs[n:2 * n]
        send_sems, recv_sems, local_sems = refs[2 * n:]
        x, y, c = _mesh_pos()
        me = slot(x, y, c)
        copies, locals_ = [], []
        for a in range(n):
            if own:
                mine = in_refs[a] if gather else in_refs[a].at[me]
                loc = pltpu.make_async_copy(mine, out_refs[a].at[me], local_sems.at[a])
                loc.start()
                locals_.append(loc)
            for k, (dx, dy, dc) in enumerate(peers):
                peer = (1 - x if dx else x, 1 - y if dy else y, 1 - c if dc else c)
                src = in_refs[a] if gather else in_refs[a].at[slot(*peer)]
                sem = a * n_peers + k
                cp = pltpu.make_async_remote_copy(src_ref=src, dst_ref=out_refs[a].at[me], send_sem=send_sems.at[sem],
                                                  recv_sem=recv_sems.at[sem], device_id=peer, device_id_type=MESH_ID)
                cp.start()
                copies.append(cp)
        for cp in copies:
            cp.wait()
        for loc in locals_:
            loc.wait()

    any_spec = pl.BlockSpec(memory_space=pl.ANY)
    out_shape = [jax.ShapeDtypeStruct((n_slots,) + (a.shape if gather else a.shape[1:]), a.dtype) for a in arrs]
    n_sems = n * n_peers
    res = pl.pallas_call(
        body, in_specs=[any_spec] * n, out_specs=[any_spec] * n, out_shape=out_shape,
        scratch_shapes=[pltpu.SemaphoreType.DMA((n_sems,)), pltpu.SemaphoreType.DMA((n_sems,)),
                        pltpu.SemaphoreType.DMA((n,))],
        name=name)(*arrs)
    return list(res)


HBM_SPEC = pl.BlockSpec(memory_space=pltpu.HBM)
SEM_SPEC = pl.BlockSpec(memory_space=pltpu.SEMAPHORE)
DATAFLOW = pltpu.SideEffectType.DATAFLOW_SIDE_EFFECTING


def _peer_copies(src_refs, land_refs, send_sems, recv_sems, gather):
    x, y, c = _mesh_pos()
    me = 4 * x + 2 * y + c
    copies = []
    for a, (src_ref, land_ref) in enumerate(zip(src_refs, land_refs)):
        for k, (dx, dy, dc) in enumerate(ALL_PEERS):
            peer = (1 - x if dx else x, 1 - y if dy else y, 1 - c if dc else c)
            src = src_ref if gather else src_ref.at[4 * peer[0] + 2 * peer[1] + peer[2]]
            sem = a * len(ALL_PEERS) + k
            copies.append(pltpu.make_async_remote_copy(src_ref=src, dst_ref=land_ref.at[me], send_sem=send_sems.at[sem],
                                                       recv_sem=recv_sems.at[sem], device_id=peer,
                                                       device_id_type=MESH_ID))
    return copies


def exchange_start(arrs, gather, name):
    n = len(arrs)
    lands = [lax.empty((N_DEV,) + (a.shape if gather else a.shape[1:]), a.dtype) for a in arrs]

    def body(*refs):
        for cp in _peer_copies(refs[:n], refs[n:2 * n], refs[2 * n], refs[2 * n + 1], gather):
            cp.start()
        refs[-1][...] = jnp.zeros_like(refs[-1])

    sems = pltpu.SemaphoreType.DMA((n * len(ALL_PEERS),))
    hbm = [pltpu.HBM(a.shape, a.dtype) for a in arrs + lands]
    res = pl.pallas_call(
        body, name=name, out_shape=(sems, sems, *hbm, jax.ShapeDtypeStruct((8, 128), f32)),
        in_specs=[HBM_SPEC] * (2 * n),
        out_specs=(SEM_SPEC, SEM_SPEC, *[HBM_SPEC] * (2 * n), pl.BlockSpec(memory_space=pltpu.VMEM)),
        input_output_aliases={i: 2 + i for i in range(2 * n)},
        compiler_params=pltpu.CompilerParams(has_side_effects=DATAFLOW))(
        *[pltpu.with_memory_space_constraint(a, pltpu.HBM) for a in arrs + lands])
    return res[0], res[1], list(res[2:2 + n]), list(res[2 + n:2 + 2 * n]), res[-1]


def exchange_wait(started, after, gather, name):
    send_sems, recv_sems, srcs, lands, _ = started
    n = len(srcs)

    def body(*refs):
        for cp in _peer_copies(refs[:n], refs[n:2 * n], refs[2 * n], refs[2 * n + 1], gather):
            cp.wait_send()
            cp.wait_recv()

    res = pl.pallas_call(
        body, name=name, out_shape=tuple(pltpu.HBM(a.shape, a.dtype) for a in srcs + lands),
        in_specs=[HBM_SPEC] * (2 * n) + [SEM_SPEC, SEM_SPEC, pl.BlockSpec(memory_space=pl.ANY)],
        out_specs=tuple([HBM_SPEC] * (2 * n)), input_output_aliases={i: i for i in range(2 * n)},
        compiler_params=pltpu.CompilerParams(has_side_effects=DATAFLOW))(*srcs, *lands, send_sems, recv_sems, after)
    return list(res[n:])


def gather_two_level(arrs, name):
    n = len(arrs)
    per = 7

    def body(*refs):
        in_refs, out_refs = refs[:n], refs[n:2 * n]
        send_sems, recv_sems = refs[2 * n:]
        x, y, c = _mesh_pos()
        me, sibling = (x, y, c), (x, y, 1 - c)
        chips = [(1 - x, y), (x, 1 - y), (1 - x, 1 - y)]

        def copy(a, k, block, to, src=None):
            rows = out_refs[a].at[4 * block[0] + 2 * block[1] + block[2]]
            return pltpu.make_async_remote_copy(src_ref=rows if src is None else src, dst_ref=rows,
                                                send_sem=send_sems.at[a * per + k], recv_sem=recv_sems.at[a * per + k],
                                                device_id=to, device_id_type=MESH_ID)

        first, passed = [], []
        for a in range(n):
            first.append(copy(a, 0, me, sibling, src=in_refs[a]))
            first += [copy(a, 1 + j, me, (*chip, c), src=in_refs[a]) for j, chip in enumerate(chips)]
        for cp in first:
            cp.start()
        for a in range(n):
            for j, chip in enumerate(chips):
                copy(a, 1 + j, (*chip, c), me).wait_recv()
                fwd = copy(a, 4 + j, (*chip, c), sibling)
                fwd.start()
                passed.append(fwd)
        for a in range(n):
            copy(a, 0, sibling, me).wait_recv()
            for j, chip in enumerate(chips):
                copy(a, 4 + j, (*chip, 1 - c), me).wait_recv()
        for cp in first + passed:
            cp.wait_send()

    any_spec = pl.BlockSpec(memory_space=pl.ANY)
    res = pl.pallas_call(
        body, in_specs=[any_spec] * n, out_specs=[any_spec] * n,
        out_shape=[jax.ShapeDtypeStruct((N_DEV,) + a.shape, a.dtype) for a in arrs],
        scratch_shapes=[pltpu.SemaphoreType.DMA((n * per,)), pltpu.SemaphoreType.DMA((n * per,))],
        name=name)(*arrs)
    return list(res)


def sum_slots(parts, slots, name):
    _, r, c = parts.shape
    tr = r
    for cand in (512, 256, 128, 64, 32, 16, 8):
        if r % cand == 0 and cand * c * 4 * len(slots) <= 8 * 1024 * 1024:
            tr = cand
            break

    def body(p_ref, o_ref):
        acc = p_ref[slots[0]].astype(f32)
        for s in slots[1:]:
            acc = acc + p_ref[s].astype(f32)
        o_ref[...] = acc

    return pl.pallas_call(body, grid=(r // tr,), in_specs=[pl.BlockSpec((parts.shape[0], tr, c), lambda i: (0, i, 0))],
                          out_specs=pl.BlockSpec((tr, c), lambda i: (i, 0)),
                          out_shape=jax.ShapeDtypeStruct((r, c), f32),
                          compiler_params=_cparams(("parallel",)), name=name)(parts)


def column_sum(a, name):
    def body(a_ref, o_ref):
        o_ref[...] = jnp.sum(a_ref[...], axis=0, keepdims=True)

    return pl.pallas_call(body, out_shape=jax.ShapeDtypeStruct((1, a.shape[1]), f32), name=name)(a)


def adamw(w, g, m, v, name):
    r, c = w.shape
    tr = r
    for cand in (256, 128, 64, 32, 16, 8):
        if r % cand == 0:
            tr = cand
            break

    def body(w_ref, g_ref, m_ref, v_ref, d_ref, mo_ref, vo_ref):
        gv = g_ref[...]
        m_new = ADAM_B1 * m_ref[...] + (1.0 - ADAM_B1) * gv
        v_new = ADAM_B2 * v_ref[...] + (1.0 - ADAM_B2) * jnp.square(gv)
        m_hat = m_new / (1.0 - ADAM_B1 ** ADAM_STEP)
        v_hat = v_new / (1.0 - ADAM_B2 ** ADAM_STEP)
        d_ref[...] = -ADAM_LR * (m_hat / (jnp.sqrt(v_hat) + ADAM_EPS) + ADAM_WD * w_ref[...])
        mo_ref[...] = m_new
        vo_ref[...] = v_new

    spec = pl.BlockSpec((tr, c), lambda i: (i, 0))
    return pl.pallas_call(body, grid=(r // tr,), in_specs=[spec] * 4, out_specs=[spec] * 3,
                          out_shape=[jax.ShapeDtypeStruct((r, c), f32)] * 3,
                          compiler_params=_cparams(("parallel",)), name=name)(w, g, m, v)


def adaln_fwd(c_rows, w, b):
    def body(c_ref, w_ref, b_ref, o_ref):
        cv = c_ref[...]
        o_ref[...] = _mxu_dot(cv * jax.nn.sigmoid(cv), w_ref[...]) + b_ref[...]

    return pl.pallas_call(body, out_shape=jax.ShapeDtypeStruct((c_rows.shape[0], w.shape[1]), f32),
                          compiler_params=pltpu.CompilerParams(vmem_limit_bytes=VMEM_LIMIT), name="adaln_fwd")(c_rows, w, b)


def adaln_bwd(c_rows, dm, w):
    def body(c_ref, dm_ref, w_ref, gw_ref, ds_ref):
        cv = c_ref[...]
        gw_ref[...] = _dg(cv * jax.nn.sigmoid(cv), dm_ref[...], 0, 0)
        ds_ref[...] = _dg(dm_ref[...], w_ref[...], 1, 1)

    return pl.pallas_call(body, out_shape=[jax.ShapeDtypeStruct(w.shape, f32),
                                           jax.ShapeDtypeStruct(c_rows.shape, f32)],
                          compiler_params=pltpu.CompilerParams(vmem_limit_bytes=VMEM_LIMIT), name="adaln_bwd")(c_rows, dm, w)


def c_ctx_grad(parts, c_ctx_row):
    def body(p_ref, c_ref, o_ref):
        total = p_ref[0, 0:1, :]
        for s in range(1, N_SHARDS):
            total = total + p_ref[s, 0:1, :]
        _, vjp = jax.vjp(jax.nn.silu, c_ref[...])
        o_ref[...] = vjp(total)[0]

    return pl.pallas_call(body, out_shape=jax.ShapeDtypeStruct((1, D_MODEL), f32), name="c_ctx_grad")(parts, c_ctx_row)


PACK_W = 1024
PACK_ROWS = 8


def _pack(arrs):
    pieces, layout, r0 = [], [], 0
    for a in arrs:
        size = math.prod(a.shape)
        rows = -(-size // (PACK_W * PACK_ROWS)) * PACK_ROWS
        pieces.append(jnp.pad(a.reshape(-1).astype(f32), (0, rows * PACK_W - size)).reshape(rows, PACK_W))
        layout.append((r0, rows, a.shape))
        r0 += rows
    return jnp.concatenate(pieces, axis=0), layout


def _unpack(pack, layout, lead=()):
    n_lead = len(lead)
    outs = []
    for r0, rows, shape in layout:
        piece = pack[(slice(None),) * n_lead + (slice(r0, r0 + rows),)].reshape(lead + (-1,))
        outs.append(piece[..., :math.prod(shape)].reshape(lead + tuple(shape)))
    return outs


W_NAMES = ("c_ctx", "w_ada", "b_ada", "norm1_g", "norm2_g", "w_in", "ret_log_decay", "rwkv_shift_mu", "rwkv_w0",
           "rwkv_w_up", "rwkv_a0", "rwkv_a_up", "rwkv_g_up", "rwkv_k_k", "rwkv_k_a", "rwkv_r_k", "rwkv_ln_w",
           "rwkv_ln_b", "w_out", "w_ff1", "b_ff1", "w_ff2", "b_ff2", "final_g")
COL_SHARDED = ("w_in", "w_ff1")
ROW_SHARDED = ("w_out", "w_ff2")
LAST_SHARDED = ("rwkv_shift_mu", "rwkv_w0", "rwkv_w_up", "rwkv_a0", "rwkv_a_up", "rwkv_g_up")
REPLICATED = ("c_ctx", "b_ada", "norm1_g", "norm2_g", "ret_log_decay", "rwkv_k_k", "rwkv_k_a", "rwkv_r_k",
              "rwkv_ln_w", "rwkv_ln_b", "b_ff1", "b_ff2", "final_g")
N_SHARDS = 4


def _train_step(a):
    x, c, ctx, tgt = a["x"], a["c"], a["ctx"], a["loss_target"]
    bsz = x.shape[0]
    mx, my, mc = _mesh_pos()
    shard = 2 * mx + my
    dev = _device_slot()

    (c_all,) = exchange([jnp.pad(c, ((0, PACK_ROWS - bsz), (0, 0)))], True, ALL_PEERS, "gather_c")
    n_ex = N_DEV * bsz
    c_rows = jnp.concatenate([c_all[:, :bsz].reshape(n_ex, D_MODEL), a["c_ctx"][None, :],
                              jnp.zeros((PACK_ROWS - 1, D_MODEL), f32)], axis=0)
    ada_cols = a["w_ada"].shape[-1]
    b_ada_cols = lax.dynamic_slice_in_dim(a["b_ada"], shard * ada_cols, ada_cols, axis=1)
    mod_cols = adaln_fwd(c_rows, a["w_ada"][0], b_ada_cols)

    def own_half(n):
        w = a[n][0].astype(MXU_DTYPE)
        return lax.dynamic_slice_in_dim(w, mc * (w.shape[0] // 2), w.shape[0] // 2, axis=0)

    def whole_weight(n, gth, own):
        per_chip = lax.dynamic_update_index_in_dim(gth, own, dev, 0).reshape(N_SHARDS, -1, gth.shape[-1])
        return (per_chip.transpose(1, 0, 2).reshape(per_chip.shape[1], -1) if n in COL_SHARDED
                else per_chip.reshape(-1, per_chip.shape[-1]))

    small_pack, small_layout = _pack([a[n][0] for n in LAST_SHARDED])
    own_blocks = [mod_cols, own_half("w_in"), small_pack]
    gathered = gather_two_level(own_blocks, "gather_weights")
    late_own = [own_half(n) for n in LATE_WEIGHTS]
    late_started = exchange_start(late_own, True, "gather_late_start")
    mod_own = lax.dynamic_update_index_in_dim(gathered[0], mod_cols, dev, 0)
    mod_all = jnp.stack([mod_own[s] for s in CHIP_SLOTS], axis=1).reshape(c_rows.shape[0], -1)
    mod_all = mod_all + late_started[-1][0, 0]
    mod_x = lax.dynamic_slice_in_dim(mod_all, dev * bsz, bsz, axis=0).reshape(bsz, 6, D_MODEL)
    mod_ctx = mod_all[n_ex].reshape(6, D_MODEL)
    wt = {"w_in": whole_weight("w_in", gathered[1], own_blocks[1])}

    def late_weights(after):
        lands = exchange_wait(late_started, after, True, "gather_late_wait")
        return {n: whole_weight(n, land, own) for n, land, own in zip(LATE_WEIGHTS, lands, late_own)}

    def grad_blocks(n, gw):
        if n in COL_SHARDED:
            gw = gw.reshape(gw.shape[0], N_SHARDS, -1).transpose(1, 0, 2)
        return gw.reshape(N_DEV, -1, gw.shape[-1]).astype(MXU_DTYPE)

    late_sent, last_sent = {}, {}

    def early_grads(late_g):
        late_sent["blocks"] = [grad_blocks(n, late_g[n]) for n in LATE_WEIGHTS]
        late_sent["started"] = exchange_start(late_sent["blocks"], False, "scatter_late_start")
        return late_sent["started"][-1]

    def last_grads(g_w_in, g_small):
        shard_packs = []
        for s in range(N_SHARDS):
            pieces_s = [lax.slice_in_dim(g_small[n], s * a[n].shape[-1], (s + 1) * a[n].shape[-1],
                                         axis=g_small[n].ndim - 1) for n in LAST_SHARDED]
            pack_s, last_sent["layout"] = _pack(pieces_s)
            shard_packs.append(jnp.pad(pack_s, ((0, -pack_s.shape[0] % (2 * PACK_ROWS)), (0, 0))))
        last_sent["blocks"] = [grad_blocks("w_in", g_w_in), jnp.stack(shard_packs).reshape(N_DEV, -1, PACK_W)]
        last_sent["started"] = exchange_start(last_sent["blocks"], False, "scatter_last_start")
        return last_sent["started"][-1]

    small_own = lax.dynamic_update_index_in_dim(gathered[2], small_pack, dev, 0)
    small_by_chip = _unpack(jnp.stack([small_own[s] for s in CHIP_SLOTS]), small_layout, (N_SHARDS,))
    for n, parts in zip(LAST_SHARDED, small_by_chip):
        wt[n] = jnp.concatenate([parts[s] for s in range(N_SHARDS)], axis=-1)
    for n in ("norm1_g", "norm2_g", "rwkv_k_k", "rwkv_k_a", "rwkv_r_k", "rwkv_ln_w", "rwkv_ln_b", "b_ff1", "b_ff2"):
        wt[n] = a[n]
    wt["ret_log_decay"] = a["ret_log_decay"][0]
    wt["final_g"] = a["final_g"][None, :]

    loss, grad_x, g = layer_step(x, ctx, tgt, mod_x, mod_ctx, wt, late_weights, early_grads, last_grads)

    small_names = [n for n in REPLICATED if n not in ("c_ctx", "b_ada")]
    g_pack, g_layout = _pack([jnp.pad(loss, ((0, 0), (0, PACK_W - loss.shape[1])))] + [g[n] for n in small_names]
                             + [g["mod_x"], g["mod_ctx"]])
    (g_packs,) = gather_two_level([g_pack], "gather_small_grads")
    g_packs = lax.dynamic_update_index_in_dim(g_packs, g_pack, dev, 0)
    g_sum = _unpack(sum_slots(g_packs, tuple(range(N_DEV)), "sum_small_grads"), g_layout)
    loss_total = g_sum[0][0, 0]
    grads = dict(zip(small_names, g_sum[1:1 + len(small_names)]))
    dmod_ctx = g_sum[-1].reshape(1, -1)
    dmod_x = _unpack(g_packs, g_layout, (N_DEV,))[-2].reshape(n_ex, -1)
    dmod = jnp.concatenate([dmod_x, dmod_ctx, jnp.zeros((PACK_ROWS - 1, dmod_x.shape[1]), f32)], axis=0)
    grads["b_ada"] = column_sum(dmod, "b_ada_grad")
    dmod_cols = lax.dynamic_slice_in_dim(dmod, shard * ada_cols, ada_cols, axis=1)
    grads["w_ada"], dsilu = adaln_bwd(c_rows, dmod_cols, a["w_ada"][0])

    dsilu_rows = jnp.broadcast_to(jnp.pad(dsilu[n_ex:n_ex + 1], ((0, PACK_ROWS - 1), (0, 0)))[None],
                                  (N_SHARDS, PACK_ROWS, D_MODEL))
    (shares,) = exchange([dsilu_rows], False, CHIP_PEERS, "share_c_ctx_grad", by_chip=True, own=False)
    shares = lax.dynamic_update_index_in_dim(shares, dsilu_rows[0], shard, 0)
    grads["c_ctx"] = c_ctx_grad(shares, a["c_ctx"][None, :])

    scattered, half_sums = ("w_in", "small_shards") + LATE_WEIGHTS, []
    for sent, wait_name, after in ((last_sent, "scatter_last_wait", grads["c_ctx"]),
                                   (late_sent, "scatter_late_wait", grads["c_ctx"])):
        for land, block in zip(exchange_wait(sent["started"], after, False, wait_name), sent["blocks"]):
            land = lax.dynamic_update_index_in_dim(land, lax.dynamic_index_in_dim(block, dev, 0, keepdims=False), dev, 0)
            half_sums.append(sum_slots(land, tuple(range(N_DEV)), f"sum_{scattered[len(half_sums)]}"))
    other_halves = sibling_swap(half_sums, "swap_halves")
    for n, mine, other in zip(scattered, half_sums, other_halves):
        rows = mine.shape[0]
        whole = jnp.zeros((2 * rows, mine.shape[1]), f32)
        whole = lax.dynamic_update_slice_in_dim(whole, mine, mc * rows, axis=0)
        grads[n] = lax.dynamic_update_slice_in_dim(whole, other, (1 - mc) * rows, axis=0)
    grads.update(zip(LAST_SHARDED, _unpack(grads.pop("small_shards"), last_sent["layout"])))

    out_g, out_d, out_m, out_v = {}, {}, {}, {}
    for n in ("w_ada",) + COL_SHARDED + ROW_SHARDED:
        out_g[n] = grads[n].reshape(a[n].shape)
        two_d = lambda z: z.reshape(-1, z.shape[-1])
        d, m, v = adamw(two_d(a[n]), two_d(out_g[n]), two_d(a["m_" + n]), two_d(a["v_" + n]), f"adamw_{n}")
        out_d[n], out_m[n], out_v[n] = d.reshape(a[n].shape), m.reshape(a[n].shape), v.reshape(a[n].shape)
    rest = REPLICATED + LAST_SHARDED
    for n in rest:
        out_g[n] = grads[n].reshape(a[n].shape)
    packs = [_pack([src[n] for n in rest])[0] for src in
             ({n: a[n] for n in rest}, out_g, {n: a["m_" + n] for n in rest}, {n: a["v_" + n] for n in rest})]
    _, rest_layout = _pack([a[n] for n in rest])
    for dst, pack in zip((out_d, out_m, out_v), adamw(*packs, "adamw_small")):
        dst.update(zip(rest, _unpack(pack, rest_layout)))
    return (loss_total, grad_x, *[out_g[n] for n in W_NAMES], *[out_d[n] for n in W_NAMES],
            *[out_m[n] for n in W_NAMES], *[out_v[n] for n in W_NAMES])


def kernel(x, c, ctx, c_ctx, w_ada, b_ada, norm1_g, norm2_g, w_in, ret_log_decay, rwkv_shift_mu, rwkv_w0, rwkv_w_up, rwkv_a0, rwkv_a_up, rwkv_g_up, rwkv_k_k, rwkv_k_a, rwkv_r_k, rwkv_ln_w, rwkv_ln_b, w_out, w_ff1, b_ff1, w_ff2, b_ff2, final_g, loss_target, m_c_ctx, m_w_ada, m_b_ada, m_norm1_g, m_norm2_g, m_w_in, m_ret_log_decay, m_rwkv_shift_mu, m_rwkv_w0, m_rwkv_w_up, m_rwkv_a0, m_rwkv_a_up, m_rwkv_g_up, m_rwkv_k_k, m_rwkv_k_a, m_rwkv_r_k, m_rwkv_ln_w, m_rwkv_ln_b, m_w_out, m_w_ff1, m_b_ff1, m_w_ff2, m_b_ff2, m_final_g, v_c_ctx, v_w_ada, v_b_ada, v_norm1_g, v_norm2_g, v_w_in, v_ret_log_decay, v_rwkv_shift_mu, v_rwkv_w0, v_rwkv_w_up, v_rwkv_a0, v_rwkv_a_up, v_rwkv_g_up, v_rwkv_k_k, v_rwkv_k_a, v_rwkv_r_k, v_rwkv_ln_w, v_rwkv_ln_b, v_w_out, v_w_ff1, v_b_ff1, v_w_ff2, v_b_ff2, v_final_g):
    return _train_step(dict(locals()))
```

```python
import functools
import math

import jax
import jax.numpy as jnp
from jax import lax
from jax.experimental import pallas as pl
from jax.experimental.pallas import tpu as pltpu

f32 = jnp.float32
MXU_DTYPE = jnp.bfloat16

D_MODEL = 1024
RET_W = 512
RET_HEADS = 4
RET_DH = 128
RET_CHUNK = 128
RW_W = 512
RW_N = 64
DECAY_LORA = 64
AAA_LORA = 64
GATE_LORA = 128
LORA_W = DECAY_LORA + AAA_LORA + GATE_LORA
D_FF = 4096
RET_COLS = 4 * RET_W
SHIFT_COLS = 3 * RW_W + LORA_W
IN_COLS = RET_COLS + SHIFT_COLS
GRID_W = 64
ROPE_BASE = 10000.0
NORM_EPS = 1e-6
GN_EPS = 64e-5
W_DECAY_SCALE = math.exp(-0.5)
ADAM_LR, ADAM_B1, ADAM_B2, ADAM_EPS, ADAM_WD, ADAM_STEP = 0.001, 0.9, 0.999, 1e-08, 0.01, 10

TOK_TILE = 256
MATMUL_TILE = 1024
SCAN_CHUNK = 32
SCAN_UNROLL = SCAN_CHUNK
N_DEV = 8
V7X_VMEM_BYTES = 64 * 1024 * 1024
VMEM_LIMIT = V7X_VMEM_BYTES * 7 // 8


def _cparams(sem):
    return pltpu.CompilerParams(dimension_semantics=sem, vmem_limit_bytes=VMEM_LIMIT)


def _tile(n, cap):
    best = None
    for t in range(128, min(n, cap) + 1, 128):
        if n % t == 0:
            best = t
    return best if best is not None else n


def matmul(a, b, mode, name, out_dtype=f32, bias=None, finish=None):
    if mode == "nn":
        (m, k), (k2, n) = a.shape, b.shape
    elif mode == "nt":
        (m, k), (n, k2) = a.shape, b.shape
    else:
        (k, m), (k2, n) = a.shape, b.shape
    assert k == k2, (a.shape, b.shape, mode)
    tm, tn, tk = _tile(m, MATMUL_TILE), _tile(n, MATMUL_TILE), _tile(k, MATMUL_TILE)
    nk = k // tk
    dims = {"nn": ((1,), (0,)), "nt": ((1,), (1,)), "tn": ((0,), (0,))}[mode]

    def body(a_ref, b_ref, *rest):
        o_ref, acc_ref = rest[-2:]
        kk = pl.program_id(2)

        @pl.when(kk == 0)
        def _():
            acc_ref[...] = jnp.zeros_like(acc_ref)

        acc_ref[...] += lax.dot_general(a_ref[...].astype(MXU_DTYPE), b_ref[...].astype(MXU_DTYPE),
                                        (dims, ((), ())), preferred_element_type=f32)

        @pl.when(kk == nk - 1)
        def _():
            res = acc_ref[...]
            if bias is not None:
                res = res + rest[0][...]
            if finish is not None:
                res = finish(res)
            o_ref[...] = res.astype(o_ref.dtype)

    if mode == "nn":
        a_spec = pl.BlockSpec((tm, tk), lambda i, j, q: (i, q))
        b_spec = pl.BlockSpec((tk, tn), lambda i, j, q: (q, j))
    elif mode == "nt":
        a_spec = pl.BlockSpec((tm, tk), lambda i, j, q: (i, q))
        b_spec = pl.BlockSpec((tn, tk), lambda i, j, q: (j, q))
    else:
        a_spec = pl.BlockSpec((tk, tm), lambda i, j, q: (q, i))
        b_spec = pl.BlockSpec((tk, tn), lambda i, j, q: (q, j))
    extra_specs = [] if bias is None else [pl.BlockSpec((1, tn), lambda i, j, q: (0, j))]
    extra = [] if bias is None else [bias]
    return pl.pallas_call(
        body, grid=(m // tm, n // tn, nk), in_specs=[a_spec, b_spec] + extra_specs,
        out_specs=pl.BlockSpec((tm, tn), lambda i, j, q: (i, j)),
        out_shape=jax.ShapeDtypeStruct((m, n), out_dtype),
        scratch_shapes=[pltpu.VMEM((tm, tn), f32)],
        compiler_params=_cparams(("parallel", "parallel", "arbitrary")), name=name)(a, b, *extra)


class Tiled:
    def __init__(self, arr, w=None, cidx=0, toff=0):
        self.arr, self.w, self.cidx, self.toff = arr, (arr.shape[-1] if w is None else w), cidx, toff

    def spec(self):
        cidx, toff = self.cidx, self.toff
        return pl.BlockSpec((None, TOK_TILE, self.w), lambda b, i: (b, jnp.maximum(i + toff, 0), cidx))


class Seg:
    def __init__(self, arr, seg, first):
        self.arr, self.seg, self.first = arr, seg, first

    def spec(self):
        seg = self.seg
        return pl.BlockSpec((None, None, 1, self.arr.shape[-1]), lambda b, i: (b, seg(i), 0, 0))


class Glob:
    def __init__(self, arr):
        self.arr = arr

    def spec(self):
        return pl.BlockSpec(self.arr.shape, lambda b, i: (0,) * self.arr.ndim)


def ew_forward(fn, name, bsz, n_tiles, ins, outs):
    n_in = len(ins)

    def body(*refs):
        res = fn(*[r[...] for r in refs[:n_in]])
        for o_ref, o in zip(refs[n_in:], res):
            o_ref[...] = o.astype(o_ref.dtype)

    out_specs = [pl.BlockSpec((None, TOK_TILE, w), lambda b, i: (b, i, 0)) for w, _ in outs]
    out_shape = [jax.ShapeDtypeStruct((bsz, n_tiles * TOK_TILE, w), dt) for w, dt in outs]
    return pl.pallas_call(body, grid=(bsz, n_tiles), in_specs=[d.spec() for d in ins], out_specs=out_specs,
                          out_shape=out_shape, compiler_params=_cparams(("parallel", "parallel")), name=name)(
        *[d.arr for d in ins])


def ew_backward(fn, name, bsz, n_tiles, ins, cts, want, grad_dtypes=None, lead=0):
    ct_parts = [c if isinstance(c, tuple) else (c,) for c in cts]
    cts = [part for parts in ct_parts for part in parts]
    n_in, n_ct = len(ins), len(cts)
    diff = [k for k in range(n_in) if want[k]]
    grad_dtypes = grad_dtypes or {}
    assert lead == 0 or not any(isinstance(ins[k], Seg) for k in diff)

    def body(*refs):
        b, i = pl.program_id(0), pl.program_id(1)
        g_refs = refs[n_in + n_ct:]

        def tile_grads():
            vals = [r[...] for r in refs[:n_in]]
            ct_refs = iter(refs[n_in:n_in + n_ct])
            ct_vals = tuple(functools.reduce(lambda s, t: s + t, [next(ct_refs)[...].astype(f32) for _ in parts])
                            for parts in ct_parts)

            def f(*dvals):
                full = list(vals)
                for k, v in zip(diff, dvals):
                    full[k] = v
                return tuple(fn(*full))

            _, vjp = jax.vjp(f, *[vals[k] for k in diff])
            grads = vjp(ct_vals)
            for k, g_ref, g in zip(diff, g_refs, grads):
                d = ins[k]
                if isinstance(d, Tiled):
                    g_ref[...] = g.astype(g_ref.dtype)
                else:
                    zero = d.first(i) if isinstance(d, Seg) else jnp.logical_and(b == 0, i == lead)

                    @pl.when(zero)
                    def _(g_ref=g_ref):
                        g_ref[...] = jnp.zeros_like(g_ref)

                    g_ref[...] += g

        if lead == 0:
            tile_grads()
        else:
            pl.when(i >= lead)(tile_grads)

            @pl.when(i < lead)
            def _():
                for k, g_ref in zip(diff, g_refs):
                    if isinstance(ins[k], Tiled):
                        g_ref[...] = jnp.zeros_like(g_ref)

    out_specs, out_shape = [], []
    for k in diff:
        d = ins[k]
        if isinstance(d, Tiled):
            out_specs.append(pl.BlockSpec((None, TOK_TILE, d.w), lambda b, i: (b, i, 0)))
            out_shape.append(jax.ShapeDtypeStruct((bsz, (n_tiles + lead) * TOK_TILE, d.w), grad_dtypes.get(k, f32)))
        else:
            out_specs.append(d.spec())
            out_shape.append(jax.ShapeDtypeStruct(d.arr.shape, f32))
    return pl.pallas_call(body, grid=(bsz, n_tiles + lead),
                          in_specs=[d.spec() for d in ins] + [c.spec() for c in cts],
                          out_specs=out_specs, out_shape=out_shape,
                          compiler_params=_cparams(("arbitrary", "arbitrary")), name=name)(
        *[d.arr for d in ins], *[c.arr for c in cts])


@jax.custom_vjp
def _mxu_dot(a, b):
    return jnp.dot(a.astype(MXU_DTYPE), b.astype(MXU_DTYPE), preferred_element_type=f32)


def _mxu_dot_fwd(a, b):
    return _mxu_dot(a, b), (a, b)


def _mxu_dot_bwd(res, ct):
    a, b = res
    ct = ct.astype(MXU_DTYPE)
    da = lax.dot_general(ct, b.astype(MXU_DTYPE), (((1,), (1,)), ((), ())), preferred_element_type=f32)
    db = lax.dot_general(a.astype(MXU_DTYPE), ct, (((0,), (0,)), ((), ())), preferred_element_type=f32)
    return da, db


_mxu_dot.defvjp(_mxu_dot_fwd, _mxu_dot_bwd)


def _split_dot_impl(x, ones_mat):
    hi = x.astype(MXU_DTYPE)
    lo = (x - hi.astype(f32)).astype(MXU_DTYPE)
    return jnp.dot(hi, ones_mat, preferred_element_type=f32) + jnp.dot(lo, ones_mat, preferred_element_type=f32)


@jax.custom_vjp
def _split_dot(x, ones_mat):
    return _split_dot_impl(x, ones_mat)


def _split_dot_fwd(x, ones_mat):
    return _split_dot_impl(x, ones_mat), ones_mat


def _split_dot_bwd(ones_mat, ct):
    return _split_dot_impl(ct, ones_mat), None


_split_dot.defvjp(_split_dot_fwd, _split_dot_bwd)


def _block_ones(n, group):
    idx = jnp.arange(n) // group
    return (idx[:, None] == idx[None, :]).astype(MXU_DTYPE)


def _rms(x, g):
    return x * lax.rsqrt(jnp.mean(x * x, axis=-1, keepdims=True) + NORM_EPS) * g


def fn_norm_mod(h, shift, scale, g):
    return (_rms(h, g) * (1.0 + scale) + shift,)


def fn_rwkv_prepare(ks, lora, w0_f, w0_b, a0_f, a0_b, w_up_f, w_up_b, a_up_f, a_up_b, g_up, k_k, k_a, ones64):
    kkr = ks * k_k
    kk = kkr * lax.rsqrt(_split_dot(kkr * kkr, ones64) + 1e-12)
    outs = [kk]
    th = jnp.tanh(lora)
    for w0, a0, w_up, a_up in ((w0_f, a0_f, w_up_f, a_up_f), (w0_b, a0_b, w_up_b, a_up_b)):
        w = jnp.exp(-W_DECAY_SCALE * jax.nn.sigmoid(w0 + _mxu_dot(th, w_up)))
        a = jax.nn.sigmoid(a0 + _mxu_dot(lora, a_up))
        kt = ks * (1.0 + (a - 1.0) * k_a)
        outs += [w, a * kk, kt]
    outs.append(_mxu_dot(jax.nn.sigmoid(lora), g_up))
    return tuple(outs)


def fn_merge(o_f, o_b, g_ret, y_f, y_b, r, kt_f, v, g_rw, r_k, ln_w, ln_b, ones64, ones128):
    o = o_f + o_b
    ret = o * lax.rsqrt(_split_dot(o * o, ones128) * (1.0 / RET_DH) + NORM_EPS) * (g_ret * jax.nn.sigmoid(g_ret))
    y = y_f + y_b
    mean = _split_dot(y, ones64) * (1.0 / RW_N)
    yc = y - mean
    var = _split_dot(yc * yc, ones64) * (1.0 / RW_N)
    y_n = yc * lax.rsqrt(var + GN_EPS) * ln_w + ln_b
    bonus = _split_dot(r * kt_f * r_k, ones64) * v
    return ret, (y_n + bonus) * g_rw


def fn_resid_norm_mod(x, mix, gate, shift, scale, g):
    h1 = x + gate * mix
    return h1, _rms(h1, g) * (1.0 + scale) + shift


def relu2(z):
    return jnp.square(jnp.maximum(z, 0.0))


def relu2_backward(act, dact, name):
    bsz, n_tok, width = act.shape

    def body(a_ref, d_ref, du_ref, db_ref):
        du = d_ref[...].astype(f32) * (2.0 * jnp.sqrt(a_ref[...].astype(f32)))
        du_ref[...] = du.astype(du_ref.dtype)

        @pl.when(jnp.logical_and(pl.program_id(0) == 0, pl.program_id(1) == 0))
        def _():
            db_ref[...] = jnp.zeros_like(db_ref)

        db_ref[...] += jnp.sum(du, axis=0, keepdims=True)

    tile = pl.BlockSpec((None, TOK_TILE, width), lambda b, i: (b, i, 0))
    row = pl.BlockSpec((1, width), lambda b, i: (0, 0))
    return pl.pallas_call(body, grid=(bsz, n_tok // TOK_TILE), in_specs=[tile, tile], out_specs=[tile, row],
                          out_shape=[jax.ShapeDtypeStruct(act.shape, MXU_DTYPE), jax.ShapeDtypeStruct((1, width), f32)],
                          compiler_params=_cparams(("arbitrary", "arbitrary")), name=name)(act, dact)


def fn_loss(h1, f, tgt, gate, b2, g):
    y = _rms(h1 + gate * (f + b2), g)
    err = jnp.square(y - tgt)
    return 0.5 * jnp.sum(jnp.mean(err, axis=-1, keepdims=True), axis=0, keepdims=True)


def loss_and_grads(h1, f, tgt, gate, b2, g, bsz, n_tiles):
    def body(h1_ref, f_ref, t_ref, gate_ref, b2_ref, g_ref, loss_ref, dh1_ref, df_ref, dgate_ref, db2_ref, dg_ref):
        b, i = pl.program_id(0), pl.program_id(1)
        tgt_v = t_ref[...]
        loss, vjp = jax.vjp(lambda a, c, e, p, q: fn_loss(a, c, tgt_v, e, p, q),
                            h1_ref[...], f_ref[...], gate_ref[...], b2_ref[...], g_ref[...])
        dh1, df, dgate, db2, dg = vjp(jnp.ones((1, 1), f32))
        dh1_ref[...] = dh1
        df_ref[...] = df.astype(df_ref.dtype)

        @pl.when(i == 0)
        def _():
            dgate_ref[...] = jnp.zeros_like(dgate_ref)

        @pl.when(jnp.logical_and(b == 0, i == 0))
        def _():
            loss_ref[...] = jnp.zeros_like(loss_ref)
            db2_ref[...] = jnp.zeros_like(db2_ref)
            dg_ref[...] = jnp.zeros_like(dg_ref)

        dgate_ref[...] += dgate
        db2_ref[...] += db2
        dg_ref[...] += dg
        loss_ref[...] += jnp.broadcast_to(loss, loss_ref.shape)

    tile = pl.BlockSpec((None, TOK_TILE, D_MODEL), lambda b, i: (b, i, 0))
    row = pl.BlockSpec((1, D_MODEL), lambda b, i: (0, 0))
    seg = pl.BlockSpec((None, None, 1, D_MODEL), lambda b, i: (b, 0, 0, 0))
    t_tok = n_tiles * TOK_TILE
    return pl.pallas_call(
        body, grid=(bsz, n_tiles), in_specs=[tile, tile, tile, seg, row, row],
        out_specs=[pl.BlockSpec((1, 128), lambda b, i: (0, 0)), tile, tile, seg, row, row],
        out_shape=[jax.ShapeDtypeStruct((1, 128), f32), jax.ShapeDtypeStruct((bsz, t_tok, D_MODEL), f32),
                   jax.ShapeDtypeStruct((bsz, t_tok, D_MODEL), MXU_DTYPE),
                   jax.ShapeDtypeStruct((bsz, 1, 1, D_MODEL), f32),
                   jax.ShapeDtypeStruct((1, D_MODEL), f32), jax.ShapeDtypeStruct((1, D_MODEL), f32)],
        compiler_params=_cparams(("arbitrary", "arbitrary")), name="loss_and_grads")(h1, f, tgt, gate, b2, g)


SHIFT_BLOCK = SHIFT_COLS
HALO_ROWS = 8


def _shift_specs(n_tok, col0, width=SHIFT_BLOCK):
    per_tile = TOK_TILE // HALO_ROWS
    last = n_tok // HALO_ROWS - 1
    tile = pl.BlockSpec((None, TOK_TILE, width), lambda j, b, i: (b, i, col0 + j))
    prev = pl.BlockSpec((None, HALO_ROWS, width),
                        lambda j, b, i: (b, jnp.maximum(i * per_tile - 1, 0), col0 + j))
    nxt = pl.BlockSpec((None, HALO_ROWS, width),
                       lambda j, b, i: (b, jnp.minimum((i + 1) * per_tile, last), col0 + j))
    return tile, prev, nxt


def _shifted(p, prev_ref, next_ref, is_first, is_last):
    row = lax.broadcasted_iota(jnp.int32, p.shape, 0)
    prev_row = jnp.where(is_first, 0.0, prev_ref[HALO_ROWS - 1:HALO_ROWS, :].astype(f32))
    next_row = jnp.where(is_last, 0.0, next_ref[0:1, :].astype(f32))
    prev = jnp.where(row == 0, prev_row, pltpu.roll(p, 1, axis=0))
    nxt = jnp.where(row == TOK_TILE - 1, next_row, pltpu.roll(p, TOK_TILE - 1, axis=0))
    return prev, nxt


def token_shift(px, mu, seg_first, seg_last):
    bsz, n_tok, _ = px.shape
    n_tiles = n_tok // TOK_TILE

    def body(p_ref, prev_ref, next_ref, mu_ref, o_ref):
        i = pl.program_id(2)
        p = p_ref[...]
        prev, nxt = _shifted(p, prev_ref, next_ref, seg_first(i), seg_last(i))
        o_ref[...] = p + mu_ref[0:1, :] * (prev - p) + mu_ref[1:2, :] * (nxt - p)

    tile, prev, nxt = _shift_specs(n_tok, 0)
    return pl.pallas_call(
        body, grid=(SHIFT_COLS // SHIFT_BLOCK, bsz, n_tiles),
        in_specs=[tile, prev, nxt, pl.BlockSpec((2, SHIFT_BLOCK), lambda j, b, i: (0, j))],
        out_specs=pl.BlockSpec((None, TOK_TILE, SHIFT_BLOCK), lambda j, b, i: (b, i, j)),
        out_shape=jax.ShapeDtypeStruct((bsz, n_tok, SHIFT_COLS), f32),
        compiler_params=_cparams(("parallel", "parallel", "parallel")), name="token_shift")(px, px, px, mu)


def token_shift_bwd(d_sections, px, mu, seg_first, seg_last):
    bsz, n_tok, _ = px.shape
    n_tiles = n_tok // TOK_TILE
    d_arrs = [part for section in d_sections for part in section]
    assert SHIFT_BLOCK == SHIFT_COLS == sum(section[0].shape[-1] for section in d_sections)

    def body(*refs):
        d_refs, (p_ref, prev_ref, next_ref, mu_ref, dp_ref, dmu_ref) = refs[:3 * len(d_arrs)], refs[3 * len(d_arrs):]
        b, i = pl.program_id(1), pl.program_id(2)
        first, last = seg_first(i), seg_last(i)

        def summed(which):
            part_refs = iter(d_refs[which::3])
            return jnp.concatenate([functools.reduce(lambda s, t: s + t, [next(part_refs)[...] for _ in section])
                                    for section in d_sections], axis=-1)

        d, p = summed(0), p_ref[...]
        d_prev, d_next = _shifted(d, summed(1), summed(2), first, last)
        p_prev, p_next = _shifted(p, prev_ref, next_ref, first, last)
        mu0, mu1 = mu_ref[0:1, :], mu_ref[1:2, :]
        dp_ref[...] = (d + mu0 * (d_next - d) + mu1 * (d_prev - d)).astype(dp_ref.dtype)

        @pl.when(jnp.logical_and(b == 0, i == 0))
        def _():
            dmu_ref[...] = jnp.zeros_like(dmu_ref)

        dmu_ref[0:1, :] += jnp.sum(d * (p_prev - p), axis=0, keepdims=True)
        dmu_ref[1:2, :] += jnp.sum(d * (p_next - p), axis=0, keepdims=True)

    d_specs = [spec for arr in d_arrs for spec in _shift_specs(n_tok, 0, arr.shape[-1])]
    tile, prev, nxt = _shift_specs(n_tok, 0)
    mu_spec = pl.BlockSpec((2, SHIFT_BLOCK), lambda j, b, i: (0, j))
    return pl.pallas_call(
        body, grid=(SHIFT_COLS // SHIFT_BLOCK, bsz, n_tiles),
        in_specs=d_specs + [tile, prev, nxt, mu_spec],
        out_specs=[pl.BlockSpec((None, TOK_TILE, SHIFT_BLOCK), lambda j, b, i: (b, i, j)), mu_spec],
        out_shape=[jax.ShapeDtypeStruct((bsz, n_tok, SHIFT_COLS), MXU_DTYPE),
                   jax.ShapeDtypeStruct((2, SHIFT_COLS), f32)],
        compiler_params=_cparams(("arbitrary", "arbitrary", "arbitrary")), name="token_shift_bwd")(
        *[arr for arr in d_arrs for _ in range(3)], px, px, px, mu)


def _dg(a, b, ca, cb):
    return lax.dot_general(a.astype(MXU_DTYPE), b.astype(MXU_DTYPE), (((ca,), (cb,)), ((), ())),
                           preferred_element_type=f32)


@jax.custom_vjp
def _mm_nt(a, b):
    return _dg(a, b, 1, 1)


_mm_nt.defvjp(lambda a, b: (_dg(a, b, 1, 1), (a, b)),
              lambda res, ct: (_dg(ct, res[1], 1, 0), _dg(ct, res[0], 0, 0)))


@jax.custom_vjp
def _mm_tn(a, b):
    return _dg(a, b, 0, 0)


_mm_tn.defvjp(lambda a, b: (_dg(a, b, 0, 0), (a, b)),
              lambda res, ct: (_dg(res[1], ct, 1, 1), _dg(res[0], ct, 1, 0)))


ROTARY_PAIR = RET_DH // 4


def _swap_pairs_impl(t):
    lane = lax.broadcasted_iota(jnp.int32, t.shape, 1)
    return jnp.where(lane % (2 * ROTARY_PAIR) < ROTARY_PAIR, pltpu.roll(t, RET_DH - ROTARY_PAIR, axis=1),
                     pltpu.roll(t, ROTARY_PAIR, axis=1))


@jax.custom_vjp
def _swap_pairs(t):
    return _swap_pairs_impl(t)


_swap_pairs.defvjp(lambda t: (_swap_pairs_impl(t), None), lambda _, ct: (_swap_pairs_impl(ct),))


def _ret_chunk(state, q_raw, k_raw, v, cos, sin, ld_row, head, reverse):
    c = RET_CHUNK
    lane = lax.broadcasted_iota(jnp.int32, ld_row.shape, 1)
    lg = -jnp.exp(jnp.sum(jnp.where(lane == head, ld_row, 0.0), axis=-1, keepdims=True))
    rot = lambda t: t * cos + _swap_pairs(t) * sin
    q = rot(q_raw)
    k = rot(k_raw) * (RET_DH ** -0.5)
    ti = lax.broadcasted_iota(jnp.int32, (c, 1), 0).astype(f32)
    tj = lax.broadcasted_iota(jnp.int32, (1, c), 1).astype(f32)
    if not reverse:
        dist, mask, q_exp, k_exp = ti - tj, (ti - tj) >= 0, ti + 1.0, c - 1.0 - ti
    else:
        dist, mask, q_exp, k_exp = tj - ti, (tj - ti) > 0, c - ti, ti
    decay = jnp.where(mask, jnp.exp(lg * jnp.maximum(dist, 0.0)), 0.0)
    scores = _mm_nt(q, k) * decay
    out = _mxu_dot(scores, v) + _mxu_dot(q * jnp.exp(lg * q_exp), state)
    new_state = state * jnp.exp(lg * c) + _mm_tn(k * jnp.exp(lg * k_exp), v)
    return out, new_state


def _ret_specs(bsz, order):
    tok = lambda col=0: pl.BlockSpec((bsz, RET_CHUNK, RET_W), lambda i: (0, order(i), col))
    tab = pl.BlockSpec((RET_CHUNK, RET_DH), lambda i: (order(i), 0))
    ld = pl.BlockSpec((1, RET_DH), lambda i: (0, 0))
    return tok, tab, ld


def retention_fwd(px, cos, sin, ld_row, order, reverse, name):
    bsz, n_tok, _ = px.shape
    n_ch = n_tok // RET_CHUNK

    def body(q_ref, k_ref, v_ref, cos_ref, sin_ref, ld_ref, o_ref, sv_ref, st_ref):
        @pl.when(pl.program_id(0) == 0)
        def _():
            st_ref[...] = jnp.zeros_like(st_ref)

        for b in range(bsz):
            for h in range(RET_HEADS):
                sl = slice(h * RET_DH, (h + 1) * RET_DH)
                s = st_ref[b, h]
                sv_ref[b, h] = s
                o, s_new = _ret_chunk(s, q_ref[b, :, sl], k_ref[b, :, sl], v_ref[b, :, sl], cos_ref[...], sin_ref[...],
                                      ld_ref[...], h, reverse)
                o_ref[b, :, sl] = o
                st_ref[b, h] = s_new

    tok, tab, ld = _ret_specs(bsz, order)
    return pl.pallas_call(
        body, grid=(n_ch,), in_specs=[tok(0), tok(1), tok(2), tab, tab, ld],
        out_specs=[tok(), pl.BlockSpec((bsz, None, RET_HEADS, RET_DH, RET_DH), lambda i: (0, i, 0, 0, 0))],
        out_shape=[jax.ShapeDtypeStruct((bsz, n_tok, RET_W), f32),
                   jax.ShapeDtypeStruct((bsz, n_ch, RET_HEADS, RET_DH, RET_DH), f32)],
        scratch_shapes=[pltpu.VMEM((bsz, RET_HEADS, RET_DH, RET_DH), f32)],
        compiler_params=_cparams(("arbitrary",)), name=name)(px, px, px, cos, sin, ld_row)


def retention_bwd(do, px, states, cos, sin, ld_row, order, reverse, name):
    bsz, n_tok, _ = px.shape
    n_ch = n_tok // RET_CHUNK
    back = lambda i: order(n_ch - 1 - i)

    def body(do_ref, q_ref, k_ref, v_ref, sv_ref, cos_ref, sin_ref, ld_ref,
             dq_ref, dk_ref, dv_ref, dld_ref, dst_ref):
        @pl.when(pl.program_id(0) == 0)
        def _():
            dst_ref[...] = jnp.zeros_like(dst_ref)
            dld_ref[...] = jnp.zeros_like(dld_ref)

        cos_v, sin_v = cos_ref[...], sin_ref[...]
        for b in range(bsz):
            for h in range(RET_HEADS):
                sl = slice(h * RET_DH, (h + 1) * RET_DH)
                f = lambda s, q, k, v, ld, h=h: _ret_chunk(s, q, k, v, cos_v, sin_v, ld, h, reverse)
                _, vjp = jax.vjp(f, sv_ref[b, h], q_ref[b, :, sl], k_ref[b, :, sl], v_ref[b, :, sl], ld_ref[...])
                ds, dq, dk, dv, dld = vjp((do_ref[b, :, sl], dst_ref[b, h]))
                dst_ref[b, h] = ds
                dq_ref[b, :, sl] = dq
                dk_ref[b, :, sl] = dk
                dv_ref[b, :, sl] = dv
                dld_ref[...] += dld

    tok, tab, ld = _ret_specs(bsz, back)
    return pl.pallas_call(
        body, grid=(n_ch,),
        in_specs=[tok(), tok(0), tok(1), tok(2),
                  pl.BlockSpec((bsz, None, RET_HEADS, RET_DH, RET_DH), lambda i: (0, n_ch - 1 - i, 0, 0, 0)),
                  tab, tab, ld],
        out_specs=[tok(), tok(), tok(), ld],
        out_shape=[jax.ShapeDtypeStruct((bsz, n_tok, RET_W), f32)] * 3 + [jax.ShapeDtypeStruct((1, RET_DH), f32)],
        scratch_shapes=[pltpu.VMEM((bsz, RET_HEADS, RET_DH, RET_DH), f32)],
        compiler_params=_cparams(("arbitrary",)), name=name)(
        do, px, px, px, states, cos, sin, ld_row)


HALF_W = RW_W // 2


def _head_sum(x, ones):
    xm = x.astype(MXU_DTYPE)
    return jnp.concatenate([jnp.dot(xm[:, :HALF_W], ones, preferred_element_type=f32),
                            jnp.dot(xm[:, HALF_W:], ones, preferred_element_type=f32)], axis=1)


def _stack(parts):
    return jnp.concatenate(parts, axis=0)


def _row(ref, b, t):
    return ref[b, pl.ds(t, 1), :]


SCAN_DIRS = ((False, True), (True, False))
RW_HEADS = RW_W // RW_N
HEAD_ROWS_PAD = 16


def _head_rows(row, mask):
    return jnp.broadcast_to(row, mask.shape) * mask


def _outer(per_value, row, mask_pad):
    return lax.dot_general(per_value.astype(MXU_DTYPE), _head_rows(row, mask_pad).astype(MXU_DTYPE),
                           (((0,), (0,)), ((), ())), preferred_element_type=f32)


def _read(states, rows, mask, more_rows=()):
    lhs = _stack([_head_rows(r, mask) for r in list(rows) + list(more_rows)])
    return lax.dot_general(lhs.astype(MXU_DTYPE), _stack(states).astype(MXU_DTYPE), (((1,), (1,)), ((), ())),
                           preferred_element_type=f32)


def _own_block(raw, b):
    lanes = raw[:, RW_N * b:RW_N * (b + 1)]
    turned = _stack([lanes[RW_HEADS * b:], lanes[:RW_HEADS * b]]) if b else lanes
    if turned.shape[0] < HEAD_ROWS_PAD:
        turned = _stack([turned, jnp.zeros((HEAD_ROWS_PAD - turned.shape[0], RW_N), f32)])
    return turned[:HEAD_ROWS_PAD]


def _row_from_heads(per_value, state, mask_pad):
    full = jnp.dot(per_value.astype(MXU_DTYPE), state.astype(MXU_DTYPE), preferred_element_type=f32)
    return jnp.sum(full * mask_pad, axis=0, keepdims=True)


def _scan_specs(bsz, order):
    rows = lambda col=0: pl.BlockSpec((bsz, SCAN_CHUNK, RW_W), lambda i: (0, order(i), col))
    per_value = pl.BlockSpec((bsz, SCAN_CHUNK, HEAD_ROWS_PAD, RW_N), lambda i: (0, order(i), 0, 0))
    states = pl.BlockSpec((SCAN_CHUNK, bsz, RW_N, RW_W), lambda i: (order(i), 0, 0, 0))
    blocks = pl.BlockSpec((SCAN_CHUNK, RW_HEADS * bsz, RW_N * bsz), lambda i: (order(i), 0, 0))
    return rows, per_value, states, blocks


def _mxu_operands(states):
    return [s.astype(MXU_DTYPE) for s in states]


def _removed(states_m, kk_t, ones, bsz):
    removed = _head_sum(_stack([states_m[b] * kk_t[b].astype(MXU_DTYPE) for b in range(bsz)]), ones)
    return [removed[b * RW_N:(b + 1) * RW_N] for b in range(bsz)]


def _advance(sp, rem, w_t, b_t, vk, bsz):
    return [sp[b] * w_t[b] - rem[b] * b_t[b] + vk[b] for b in range(bsz)]


def heads_to_rows(a):
    b, t, _ = a.shape
    return jnp.pad(a.astype(MXU_DTYPE).reshape(b, t, RW_HEADS, RW_N),
                   ((0, 0), (0, 0), (0, HEAD_ROWS_PAD - RW_HEADS), (0, 0)))


def _blocks_to_rows(raw_ref, first, row_ref, bsz):
    steps = pl.ds(first, SCAN_CHUNK)
    for b in range(bsz):
        for h in range(RW_HEADS):
            row_ref[b, :, h * RW_N:(h + 1) * RW_N] = raw_ref[steps, RW_HEADS * b + h, RW_N * b:RW_N * (b + 1)]


N_ROWS_FWD = 5
N_ROWS_BWD = 5


def _scan_consts(bsz):
    head = (jnp.arange(RW_W)[None, :] // RW_N == jnp.arange(RW_HEADS)[:, None]).astype(f32)
    return head, jnp.pad(head, ((0, HEAD_ROWS_PAD - RW_HEADS), (0, 0))), _block_ones(HALF_W, RW_N)


def _const_specs(consts):
    return [pl.BlockSpec(c.shape, lambda i: (0, 0)) for c in consts]


def rwkv_scan_fwd(rows_in, v_heads, orders, name):
    bsz, n_tok, _ = rows_in[0][0][0].shape
    n_ch = n_tok // SCAN_CHUNK
    rng = range(bsz)
    consts = _scan_consts(bsz)

    def body(*refs):
        rows = [refs[:N_ROWS_FWD], refs[N_ROWS_FWD:2 * N_ROWS_FWD]]
        (v0, v1, head_ref, pad_ref, ones_ref, y0, y1, h0, h1, f0, f1, m0, m1, s0, s1, late_ref,
         raw_ref) = refs[2 * N_ROWS_FWD:]
        v_refs, y_refs, hist_refs, final_refs, s_refs = (v0, v1), (y0, y1), (h0, h1), (f0, f1), (s0, s1)
        removed_refs = (m0, m1)
        n_blk = RW_HEADS * bsz
        head_v, pad_v, ones_v = head_ref[...], pad_ref[...], ones_ref[...]
        for d in range(2):
            @pl.when(pl.program_id(0) == 0)
            def _(d=d):
                s_refs[d][...] = jnp.zeros_like(s_refs[d])

        def step(j, carry):
            ts = [SCAN_CHUNK - 1 - j if reverse else j for reverse, _ in SCAN_DIRS]
            sps = [[s_refs[d][b] for b in rng] for d in range(2)]
            sps_m = [_mxu_operands(sps[d]) for d in range(2)]
            vks = [[_outer(v_refs[d][b, ts[d]], _row(rows[d][4], b, ts[d]), pad_v) for b in rng] for d in range(2)]
            rems = [_removed(sps_m[d], [_row(rows[d][1], b, ts[d]) for b in rng], ones_v, bsz) for d in range(2)]
            for d, (reverse, inclusive) in enumerate(SCAN_DIRS):
                r_ref = rows[d][0]
                read_at = jnp.maximum(j - 1, 0) if inclusive else ts[d]
                both = _read(sps_m[d], [_row(r_ref, b, read_at) for b in rng], head_v,
                             [_row(rows[d][1], b, ts[d]) for b in rng])
                if inclusive:
                    late_ref[j] = both[:n_blk]
                else:
                    raw_ref[ts[d]] = both[:n_blk]
                removed_refs[d][ts[d]] = both[n_blk:]
            for d in range(2):
                new = _advance(sps[d], rems[d], [_row(rows[d][2], b, ts[d]) for b in rng],
                               [_row(rows[d][3], b, ts[d]) for b in rng], vks[d], bsz)
                for b in rng:
                    hist_refs[d][ts[d], b] = sps_m[d][b]
                    s_refs[d][b] = new[b]
            return carry

        lax.fori_loop(0, SCAN_CHUNK, step, 0, unroll=SCAN_UNROLL)
        for d, (reverse, inclusive) in enumerate(SCAN_DIRS):
            final_refs[d][...] = s_refs[d][...]
            if inclusive:
                assert not reverse
                last = SCAN_CHUNK - 1
                late_ref[SCAN_CHUNK] = _read(_mxu_operands([s_refs[d][b] for b in rng]),
                                             [rows[d][0][b, last:last + 1, :] for b in rng], head_v)
                _blocks_to_rows(late_ref, 1, y_refs[d], bsz)
            else:
                _blocks_to_rows(raw_ref, 0, y_refs[d], bsz)

    specs = [_scan_specs(bsz, orders[d]) for d in range(2)]
    state = pltpu.VMEM((bsz, RW_N, RW_W), f32)
    late = pltpu.VMEM((SCAN_CHUNK + 1, RW_HEADS * bsz, RW_N * bsz), f32)
    raw = pltpu.VMEM((SCAN_CHUNK, RW_HEADS * bsz, RW_N * bsz), f32)
    final_spec = pl.BlockSpec((bsz, RW_N, RW_W), lambda i: (0, 0, 0))
    return pl.pallas_call(
        body, grid=(n_ch,),
        in_specs=[specs[d][0](col) for d in range(2) for _, col in rows_in[d]] + [specs[0][1], specs[1][1]]
        + _const_specs(consts),
        out_specs=[specs[0][0](), specs[1][0](), specs[0][2], specs[1][2], final_spec, final_spec,
                   specs[0][3], specs[1][3]],
        out_shape=[jax.ShapeDtypeStruct((bsz, n_tok, RW_W), f32)] * 2
        + [jax.ShapeDtypeStruct((n_tok, bsz, RW_N, RW_W), MXU_DTYPE)] * 2
        + [jax.ShapeDtypeStruct((bsz, RW_N, RW_W), f32)] * 2
        + [jax.ShapeDtypeStruct((n_tok, RW_HEADS * bsz, RW_N * bsz), f32)] * 2,
        scratch_shapes=[state, state, late, raw],
        compiler_params=_cparams(("arbitrary",)), name=name)(
        *[a for d in range(2) for a, _ in rows_in[d]], v_heads, v_heads, *consts)


def rwkv_scan_bwd(rows_in, v_heads, dy_heads, hists, finals, removed, orders, name):
    bsz, n_tok, _ = rows_in[0][0][0].shape
    n_ch = n_tok // SCAN_CHUNK
    backs = [functools.partial(lambda i, order: order(n_ch - 1 - i), order=orders[d]) for d in range(2)]
    rng = range(bsz)
    consts = _scan_consts(bsz)
    n_out, n_scr = 6, 6

    def body(*refs):
        rows = [refs[:N_ROWS_BWD], refs[N_ROWS_BWD:2 * N_ROWS_BWD]]
        rest = refs[2 * N_ROWS_BWD:]
        v_refs, dy_refs, hist_refs, final_refs, removed_refs = rest[0:2], rest[2:4], rest[4:6], rest[6:8], rest[8:10]
        head_ref, pad_ref, ones_ref = rest[10:13]
        outs = [rest[13:13 + n_out], rest[13 + n_out:13 + 2 * n_out]]
        scr = [rest[13 + 2 * n_out:13 + 2 * n_out + n_scr], rest[13 + 2 * n_out + n_scr:]]
        n_blk = RW_HEADS * bsz
        head_v, pad_v, ones_v = head_ref[...], pad_ref[...], ones_ref[...]
        for d in range(2):
            @pl.when(pl.program_id(0) == 0)
            def _(d=d):
                scr[d][1][...] = jnp.zeros_like(scr[d][1])
                scr[d][0][...] = final_refs[d][...]

        def step_of(j, reverse):
            return j if reverse else SCAN_CHUNK - 1 - j

        for d, (reverse, _) in enumerate(SCAN_DIRS):
            t0 = step_of(0, reverse)
            for b in rng:
                scr[d][3][b] = _outer(dy_refs[d][b, t0], rows[d][0][b, t0:t0 + 1, :], pad_v)

        def bstep(j, carry):
            ts = [step_of(j, reverse) for reverse, _ in SCAN_DIRS]
            reads = [[scr[d][3][b] for b in rng] for d in range(2)]
            dss = []
            for d, (_, inclusive) in enumerate(SCAN_DIRS):
                ds = [scr[d][1][b] for b in rng]
                dss.append([ds[b] + reads[d][b] for b in rng] if inclusive else ds)
            dss_m = [_mxu_operands(dss[d]) for d in range(2)]
            nexts = []
            for d, (reverse, _) in enumerate(SCAN_DIRS):
                t_next = step_of(jnp.minimum(j + 1, SCAN_CHUNK - 1), reverse)
                nexts.append([_outer(dy_refs[d][b, t_next], _row(rows[d][0], b, t_next), pad_v) for b in rng])
            drems = [_removed(dss_m[d], [-_row(rows[d][3], b, ts[d]) for b in rng], ones_v, bsz) for d in range(2)]
            for d in range(2):
                for b in rng:
                    scr[d][3][b] = nexts[d][b]
                both = _read(dss_m[d], [_row(rows[d][4], b, ts[d]) for b in rng], head_v,
                             [-_row(rows[d][3], b, ts[d]) for b in rng])
                scr[d][4][ts[d]] = both[:n_blk]
                scr[d][5][ts[d]] = both[n_blk:]
            for d, (_, inclusive) in enumerate(SCAN_DIRS):
                _, kk_ref, w_ref, _, _ = rows[d]
                _, ds_ref, dsh_ref = scr[d][:3]
                for b in rng:
                    dsh_ref[ts[d], b] = dss[d][b]
                    dsp = dss[d][b] * _row(w_ref, b, ts[d]) + drems[d][b] * _row(kk_ref, b, ts[d])
                    ds_ref[b] = dsp if inclusive else dsp + reads[d][b]
            return carry

        lax.fori_loop(0, SCAN_CHUNK, bstep, 0, unroll=SCAN_UNROLL)

        rsum = lambda z: jnp.sum(z, axis=0, keepdims=True)
        for d, (reverse, inclusive) in enumerate(SCAN_DIRS):
            dr_ref, dkk_ref, dw_ref, db_ref, dkt_ref, dv_ref = outs[d]
            after_ref, _, dsh_ref, _, dv_raw_ref, dremt_ref = scr[d]
            hist_ref, removed_ref = hist_refs[d], removed_refs[d]
            _blocks_to_rows(dv_raw_ref, 0, dv_ref, bsz)
            for t in range(SCAN_CHUNK):
                ts = slice(t, t + 1)
                after = t - 1 if reverse else t + 1
                for b in rng:
                    sp_m, ds = hist_ref[t, b], dsh_ref[t, b]
                    sp = sp_m.astype(f32)
                    if not inclusive:
                        seen = sp_m
                    else:
                        seen = hist_ref[after, b] if 0 <= after < SCAN_CHUNK else after_ref[b]
                    dr_ref[b, ts, :] = _row_from_heads(dy_refs[d][b, t], seen, pad_v)
                    dkt_ref[b, ts, :] = _row_from_heads(v_refs[d][b, t], ds, pad_v)
                    dw_ref[b, ts, :] = rsum(ds * sp)
                    db_ref[b, ts, :] = -_row_from_heads(_own_block(removed_ref[t], b), ds, pad_v)
                    dkk_ref[b, ts, :] = _row_from_heads(_own_block(dremt_ref[t], b), sp_m, pad_v)
            if inclusive:
                first = SCAN_CHUNK - 1 if reverse else 0
                for b in rng:
                    after_ref[b] = hist_ref[first, b].astype(f32)

    specs = [_scan_specs(bsz, backs[d]) for d in range(2)]
    hist = pltpu.VMEM((SCAN_CHUNK, bsz, RW_N, RW_W), f32)
    state = pltpu.VMEM((bsz, RW_N, RW_W), f32)
    final_spec = pl.BlockSpec((bsz, RW_N, RW_W), lambda i: (0, 0, 0))
    raw = pltpu.VMEM((SCAN_CHUNK, RW_HEADS * bsz, RW_N * bsz), f32)
    return pl.pallas_call(
        body, grid=(n_ch,),
        in_specs=[specs[d][0](col) for d in range(2) for _, col in rows_in[d]]
        + [specs[0][1], specs[1][1]] * 2 + [specs[0][2], specs[1][2], final_spec, final_spec, specs[0][3], specs[1][3]]
        + _const_specs(consts),
        out_specs=[specs[d][0]() for d in range(2) for _ in range(n_out)],
        out_shape=[jax.ShapeDtypeStruct((bsz, n_tok, RW_W), f32)] * (2 * n_out),
        scratch_shapes=[state, state, hist, state, raw, raw] * 2,
        compiler_params=_cparams(("arbitrary",)), name=name)(
        *[a for d in range(2) for a, _ in rows_in[d]], v_heads, v_heads, dy_heads, dy_heads, *hists, *finals, *removed, *consts)


MOD_NAMES = ("shift1", "scale1", "gate1", "shift2", "scale2", "gate2")


def _rope_tables(t_ctx, t_x):
    quarter = RET_DH // 4
    pos = jnp.arange(t_x)
    inv = jnp.power(ROPE_BASE, -jnp.arange(0, 2 * quarter, 2, dtype=f32) / (2 * quarter))
    ang_r = (pos // GRID_W).astype(f32)[:, None] * inv[None, :]
    ang_c = (pos % GRID_W).astype(f32)[:, None] * inv[None, :]
    cos = jnp.concatenate([jnp.cos(ang_r)] * 2 + [jnp.cos(ang_c)] * 2, axis=1)
    sin = jnp.concatenate([-jnp.sin(ang_r), jnp.sin(ang_r), -jnp.sin(ang_c), jnp.sin(ang_c)], axis=1)
    cos = jnp.concatenate([jnp.ones((t_ctx, RET_DH), f32), cos], axis=0)
    sin = jnp.concatenate([jnp.zeros((t_ctx, RET_DH), f32), sin], axis=0)
    return cos, sin


def _pad_rows(w, lo, total):
    return jnp.pad(w, ((lo, total - lo - w.shape[0]), (0, 0)))


LATE_WEIGHTS = ("w_out", "w_ff1", "w_ff2")


def layer_step(x, ctx, tgt, mod_x, mod_ctx, wt, late_weights=None, early_grads=None, last_grads=None):
    bsz, t_x, _ = x.shape
    t_c = ctx.shape[1]
    t_all = t_c + t_x
    n_ct, n_xt = t_c // TOK_TILE, t_x // TOK_TILE
    n_t = n_ct + n_xt
    assert t_c % TOK_TILE == 0 and t_x % TOK_TILE == 0 and t_c % RET_CHUNK == 0

    seg = lambda i: (i >= n_ct).astype(jnp.int32)
    seg_first = lambda i: jnp.logical_or(i == 0, i == n_ct)
    seg_last = lambda i: jnp.logical_or(i == n_ct - 1, i == n_t - 1)
    mod_all = {n: jnp.stack([jnp.broadcast_to(mod_ctx[k], (bsz, D_MODEL)), mod_x[:, k]], axis=1)[:, :, None, :]
               for k, n in enumerate(MOD_NAMES)}
    mod_lat = {n: mod_x[:, k][:, None, None, :] for k, n in enumerate(MOD_NAMES)}
    both = lambda n: Seg(mod_all[n], seg, seg_first)
    lat = lambda n: Seg(mod_lat[n], lambda i: 0, lambda i: i == 0)
    flat = lambda a: a.reshape(-1, a.shape[-1])

    def chunk_orders(n_ctx_chunks, n_chunks):
        fwd = lambda i: i
        bwd = lambda i: jnp.where(i < n_ctx_chunks, n_ctx_chunks - 1 - i, n_chunks + n_ctx_chunks - 1 - i)
        return fwd, bwd

    ones64, ones128 = _block_ones(RW_W, RW_N), _block_ones(RET_W, RET_DH)
    cos, sin = _rope_tables(t_c, t_x)
    ld_rows = [jnp.pad(wt["ret_log_decay"][d][None, :], ((0, 0), (0, RET_DH - RET_HEADS))) for d in range(2)]
    w_up_pad = [_pad_rows(wt["rwkv_w_up"][d], 0, LORA_W) for d in range(2)]
    a_up_pad = [_pad_rows(wt["rwkv_a_up"][d], DECAY_LORA, LORA_W) for d in range(2)]
    g_up_pad = _pad_rows(wt["rwkv_g_up"], DECAY_LORA + AAA_LORA, LORA_W)
    row = lambda a, d: a[d][None, :]

    h = jnp.concatenate([ctx, x], axis=1)
    norm1_ins = lambda: [Tiled(h), both("shift1"), both("scale1"), Glob(wt["norm1_g"])]
    (n1,) = ew_forward(fn_norm_mod, "norm1", bsz, n_t, norm1_ins(), [(D_MODEL, MXU_DTYPE)])
    px = matmul(flat(n1), wt["w_in"], "nn", "proj_in").reshape(bsz, t_all, IN_COLS)
    px_rw = px[..., RET_COLS:]
    ps = token_shift(px_rw, wt["rwkv_shift_mu"], seg_first, seg_last)

    def prep_ins():
        return [Tiled(ps, RW_W, 1), Tiled(ps, LORA_W, 3 * RW_W // LORA_W),
                Glob(row(wt["rwkv_w0"], 0)), Glob(row(wt["rwkv_w0"], 1)),
                Glob(row(wt["rwkv_a0"], 0)), Glob(row(wt["rwkv_a0"], 1)),
                Glob(w_up_pad[0]), Glob(w_up_pad[1]), Glob(a_up_pad[0]), Glob(a_up_pad[1]), Glob(g_up_pad),
                Glob(wt["rwkv_k_k"]), Glob(wt["rwkv_k_a"]), Glob(ones64)]

    kk, w_f, b_f, kt_f, w_b, b_b, kt_b, g_rw = ew_forward(fn_rwkv_prepare, "rwkv_prepare", bsz, n_t, prep_ins(),
                                                           [(RW_W, f32)] * 8)
    rw_order = chunk_orders(t_c // SCAN_CHUNK, t_all // SCAN_CHUNK)
    ret_order = chunk_orders(t_c // RET_CHUNK, t_all // RET_CHUNK)
    scan_rows = [[(ps, 0), (kk, 0), (w_f, 0), (b_f, 0), (kt_f, 0)], [(ps, 0), (kk, 0), (w_b, 0), (b_b, 0), (kt_b, 0)]]
    v_heads = heads_to_rows(ps[..., 2 * RW_W:3 * RW_W])
    y_f, y_b, *kept_states = rwkv_scan_fwd(scan_rows, v_heads, rw_order, "rwkv_scan_fwd")
    y = [y_f, y_b]
    o, ret_states = [], []
    for d in range(2):
        o_d, st_d = retention_fwd(px, cos, sin, ld_rows[d], ret_order[d], SCAN_DIRS[d][0], f"retention_fwd{d}")
        o.append(o_d), ret_states.append(st_d)

    def merge_ins(toff):
        return [Tiled(o[0], toff=toff), Tiled(o[1], toff=toff), Tiled(px, RET_W, 3, toff),
                Tiled(y[0], toff=toff), Tiled(y[1], toff=toff), Tiled(ps, RW_W, 0, toff), Tiled(kt_f, toff=toff),
                Tiled(ps, RW_W, 2, toff), Tiled(g_rw, toff=toff),
                Glob(wt["rwkv_r_k"]), Glob(wt["rwkv_ln_w"]), Glob(wt["rwkv_ln_b"]), Glob(ones64), Glob(ones128)]

    ret_out, rw_out = ew_forward(fn_merge, "merge_heads", bsz, n_xt, merge_ins(n_ct),
                                 [(RET_W, MXU_DTYPE), (RW_W, MXU_DTYPE)])
    merged = jnp.concatenate([ret_out, rw_out], axis=-1)
    if late_weights is not None:
        wt = {**wt, **late_weights(merged)}
    mix = matmul(flat(merged), wt["w_out"], "nn", "proj_out").reshape(bsz, t_x, D_MODEL)
    resid_ins = lambda: [Tiled(x), Tiled(mix), lat("gate1"), lat("shift2"), lat("scale2"), Glob(wt["norm2_g"])]
    h1, n2 = ew_forward(fn_resid_norm_mod, "resid_norm2", bsz, n_xt, resid_ins(), [(D_MODEL, f32), (D_MODEL, MXU_DTYPE)])
    act = matmul(flat(n2), wt["w_ff1"], "nn", "ff1", MXU_DTYPE, wt["b_ff1"], relu2).reshape(bsz, t_x, D_FF)
    ff = matmul(flat(act), wt["w_ff2"], "nn", "ff2").reshape(bsz, t_x, D_MODEL)

    g = {}
    loss, dh1, dff, dgate2, g["b_ff2"], g["final_g"] = loss_and_grads(
        h1, ff, tgt, mod_lat["gate2"], wt["b_ff2"], wt["final_g"], bsz, n_xt)
    dact = matmul(flat(dff), wt["w_ff2"], "nt", "ff2_dx", MXU_DTYPE).reshape(bsz, t_x, D_FF)
    g["w_ff2"] = matmul(flat(act), flat(dff), "tn", "ff2_dw", MXU_DTYPE)
    du, g["b_ff1"] = relu2_backward(act, dact, "relu2_bwd")
    dn2 = matmul(flat(du), wt["w_ff1"], "nt", "ff1_dx").reshape(bsz, t_x, D_MODEL)
    g["w_ff1"] = matmul(flat(n2), flat(du), "tn", "ff1_dw", MXU_DTYPE)
    dx_res, dmix, dgate1, dshift2, dscale2, g["norm2_g"] = ew_backward(
        fn_resid_norm_mod, "resid_norm2_bwd", bsz, n_xt, resid_ins(), [Tiled(dh1), Tiled(dn2)], [True] * 6,
        {1: MXU_DTYPE})
    dmerged = matmul(flat(dmix), wt["w_out"], "nt", "proj_out_dx").reshape(bsz, t_x, D_MODEL)
    g["w_out"] = matmul(flat(merged), flat(dmix), "tn", "proj_out_dw", MXU_DTYPE)
    if early_grads is not None:
        token = early_grads({n: g.pop(n) for n in LATE_WEIGHTS})
        wt = {**wt, "rwkv_r_k": wt["rwkv_r_k"] + token[:1, :1]}
    (do, dg_ret, dy, dr_m, dkt_m, dv_m, dg_rw, g["rwkv_r_k"], g["rwkv_ln_w"], g["rwkv_ln_b"]) = ew_backward(
        fn_merge, "merge_heads_bwd", bsz, n_xt, merge_ins(0),
        [Tiled(dmerged, RET_W, 0, -n_ct), Tiled(dmerged, RW_W, 1, -n_ct)],
        [True, False, True, True, False, True, True, True, True, True, True, True, False, False], lead=n_ct)

    dqkv, dld = [], []
    for d in range(2):
        *dqkv_d, dld_d = retention_bwd(do, px, ret_states[d], cos, sin, ld_rows[d], ret_order[d],
                                       SCAN_DIRS[d][0], f"retention_bwd{d}")
        dqkv.append(dqkv_d), dld.append(dld_d[0, :RET_HEADS])
    g["ret_log_decay"] = jnp.stack(dld)
    (dr_f, dkk_f, dw_f, db_f, dkt_f, dv_f, dr_b, dkk_b, dw_b, db_b, dkt_b, dv_b) = rwkv_scan_bwd(
        scan_rows, v_heads, heads_to_rows(dy), kept_states[:2], kept_states[2:4], kept_states[4:], rw_order, "rwkv_scan_bwd")
    prep_cts = [(dkk_f, dkk_b), dw_f, db_f, (dkt_f, dkt_m), dw_b, db_b, dkt_b, dg_rw]
    (dks, dlora, dw0_f, dw0_b, da0_f, da0_b, dwup_f, dwup_b, daup_f, daup_b, dgup, g["rwkv_k_k"],
     g["rwkv_k_a"]) = ew_backward(fn_rwkv_prepare, "rwkv_prepare_bwd", bsz, n_t, prep_ins(),
                                  [tuple(map(Tiled, c)) if isinstance(c, tuple) else Tiled(c) for c in prep_cts],
                                  [True] * 13 + [False])
    g["rwkv_w0"] = jnp.concatenate([dw0_f, dw0_b], axis=0)
    g["rwkv_a0"] = jnp.concatenate([da0_f, da0_b], axis=0)
    g["rwkv_w_up"] = jnp.stack([dwup_f[:DECAY_LORA], dwup_b[:DECAY_LORA]])
    g["rwkv_a_up"] = jnp.stack([daup_f[DECAY_LORA:DECAY_LORA + AAA_LORA], daup_b[DECAY_LORA:DECAY_LORA + AAA_LORA]])
    g["rwkv_g_up"] = dgup[DECAY_LORA + AAA_LORA:]
    dp_rw, g["rwkv_shift_mu"] = token_shift_bwd([(dr_f, dr_b, dr_m), (dks,), (dv_f, dv_b, dv_m), (dlora,)], px_rw,
                                                 wt["rwkv_shift_mu"], seg_first, seg_last)
    dpx = jnp.concatenate([(dqkv[0][k] + dqkv[1][k]).astype(MXU_DTYPE) for k in range(3)]
                          + [dg_ret.astype(MXU_DTYPE), dp_rw], axis=-1)
    g["w_in"] = matmul(flat(n1), flat(dpx), "tn", "proj_in_dw", MXU_DTYPE)
    after_start = None
    if last_grads is not None:
        token = last_grads(g.pop("w_in"), {n: g.pop(n) for n in LAST_SHARDED})
        after_start = jnp.zeros((1, D_MODEL), f32) + token[:1, :1]
    dn1 = matmul(flat(dpx), wt["w_in"], "nt", "proj_in_dx", bias=after_start).reshape(bsz, t_all, D_MODEL)
    dh, dshift1, dscale1, g["norm1_g"] = ew_backward(fn_norm_mod, "norm1_bwd", bsz, n_t, norm1_ins(), [Tiled(dn1)],
                                                     [True] * 4)
    grad_x = dh[:, t_c:] + dx_res
    zeros = jnp.zeros((D_MODEL,), f32)
    g["mod_x"] = jnp.stack([dshift1[:, 1, 0], dscale1[:, 1, 0], dgate1[:, 0, 0], dshift2[:, 0, 0], dscale2[:, 0, 0],
                            dgate2[:, 0, 0]], axis=1)
    g["mod_ctx"] = jnp.stack([dshift1[:, 0, 0].sum(0), dscale1[:, 0, 0].sum(0), zeros, zeros, zeros, zeros])
    return loss, grad_x, g


MESH_ID = pl.DeviceIdType.MESH
ALL_PEERS = [(dx, dy, dc) for dx in (0, 1) for dy in (0, 1) for dc in (0, 1)][1:]
CHIP_PEERS = [(1, 0, 0), (0, 1, 0), (1, 1, 0)]
CHIP_SLOTS = (0, 2, 4, 6)


def _mesh_pos():
    return lax.axis_index("x"), lax.axis_index("y"), lax.axis_index("c")


def _device_slot():
    x, y, c = _mesh_pos()
    return 4 * x + 2 * y + c


def sibling_swap(arrs, name):
    n = len(arrs)

    def body(*refs):
        in_refs, out_refs = refs[:n], refs[n:2 * n]
        send_sems, recv_sems = refs[2 * n:]
        x, y, c = _mesh_pos()
        copies = [pltpu.make_async_remote_copy(src_ref=in_refs[a], dst_ref=out_refs[a], send_sem=send_sems.at[a],
                                               recv_sem=recv_sems.at[a], device_id=(x, y, 1 - c),
                                               device_id_type=MESH_ID) for a in range(n)]
        for cp in copies:
            cp.start()
        for cp in copies:
            cp.wait()

    any_spec = pl.BlockSpec(memory_space=pl.ANY)
    res = pl.pallas_call(
        body, in_specs=[any_spec] * n, out_specs=[any_spec] * n,
        out_shape=[jax.ShapeDtypeStruct(a.shape, a.dtype) for a in arrs],
        scratch_shapes=[pltpu.SemaphoreType.DMA((n,)), pltpu.SemaphoreType.DMA((n,))],
        name=name)(*arrs)
    return list(res)


def exchange(arrs, gather, peers, name, by_chip=False, own=True):
    n, n_peers = len(arrs), len(peers)
    n_slots = N_SHARDS if by_chip else N_DEV
    slot = (lambda x, y, c: 2 * x + y) if by_chip else (lambda x, y, c: 4 * x + 2 * y + c)

    def body(*refs):
        in_refs, out_refs = refs[:n], refs[n:2 * n]
        send_sems, recv_sems, local_sems = refs[2 * n:]
        x, y, c = _mesh_pos()
        me = slot(x, y, c)
        copies, locals_ = [], []
        for a in range(n):
            if own:
                mine = in_refs[a] if gather else in_refs[a].at[me]
                loc = pltpu.make_async_copy(mine, out_refs[a].at[me], local_sems.at[a])
                loc.start()
                locals_.append(loc)
            for k, (dx, dy, dc) in enumerate(peers):
                peer = (1 - x if dx else x, 1 - y if dy else y, 1 - c if dc else c)
                src = in_refs[a] if gather else in_refs[a].at[slot(*peer)]
                sem = a * n_peers + k
                cp = pltpu.make_async_remote_copy(src_ref=src, dst_ref=out_refs[a].at[me], send_sem=send_sems.at[sem],
                                                  recv_sem=recv_sems.at[sem], device_id=peer, device_id_type=MESH_ID)
                cp.start()
                copies.append(cp)
        for cp in copies:
            cp.wait()
        for loc in locals_:
            loc.wait()

    any_spec = pl.BlockSpec(memory_space=pl.ANY)
    out_shape = [jax.ShapeDtypeStruct((n_slots,) + (a.shape if gather else a.shape[1:]), a.dtype) for a in arrs]
    n_sems = n * n_peers
    res = pl.pallas_call(
        body, in_specs=[any_spec] * n, out_specs=[any_spec] * n, out_shape=out_shape,
        scratch_shapes=[pltpu.SemaphoreType.DMA((n_sems,)), pltpu.SemaphoreType.DMA((n_sems,)),
                        pltpu.SemaphoreType.DMA((n,))],
        name=name)(*arrs)
    return list(res)


HBM_SPEC = pl.BlockSpec(memory_space=pltpu.HBM)
SEM_SPEC = pl.BlockSpec(memory_space=pltpu.SEMAPHORE)
DATAFLOW = pltpu.SideEffectType.DATAFLOW_SIDE_EFFECTING


def _peer_copies(src_refs, land_refs, send_sems, recv_sems, gather):
    x, y, c = _mesh_pos()
    me = 4 * x + 2 * y + c
    copies = []
    for a, (src_ref, land_ref) in enumerate(zip(src_refs, land_refs)):
        for k, (dx, dy, dc) in enumerate(ALL_PEERS):
            peer = (1 - x if dx else x, 1 - y if dy else y, 1 - c if dc else c)
            src = src_ref if gather else src_ref.at[4 * peer[0] + 2 * peer[1] + peer[2]]
            sem = a * len(ALL_PEERS) + k
            copies.append(pltpu.make_async_remote_copy(src_ref=src, dst_ref=land_ref.at[me], send_sem=send_sems.at[sem],
                                                       recv_sem=recv_sems.at[sem], device_id=peer,
                                                       device_id_type=MESH_ID))
    return copies


def exchange_start(arrs, gather, name):
    n = len(arrs)
    lands = [lax.empty((N_DEV,) + (a.shape if gather else a.shape[1:]), a.dtype) for a in arrs]

    def body(*refs):
        for cp in _peer_copies(refs[:n], refs[n:2 * n], refs[2 * n], refs[2 * n + 1], gather):
            cp.start()
        refs[-1][...] = jnp.zeros_like(refs[-1])

    sems = pltpu.SemaphoreType.DMA((n * len(ALL_PEERS),))
    hbm = [pltpu.HBM(a.shape, a.dtype) for a in arrs + lands]
    res = pl.pallas_call(
        body, name=name, out_shape=(sems, sems, *hbm, jax.ShapeDtypeStruct((8, 128), f32)),
        in_specs=[HBM_SPEC] * (2 * n),
        out_specs=(SEM_SPEC, SEM_SPEC, *[HBM_SPEC] * (2 * n), pl.BlockSpec(memory_space=pltpu.VMEM)),
        input_output_aliases={i: 2 + i for i in range(2 * n)},
        compiler_params=pltpu.CompilerParams(has_side_effects=DATAFLOW))(
        *[pltpu.with_memory_space_constraint(a, pltpu.HBM) for a in arrs + lands])
    return res[0], res[1], list(res[2:2 + n]), list(res[2 + n:2 + 2 * n]), res[-1]


def exchange_wait(started, after, gather, name):
    send_sems, recv_sems, srcs, lands, _ = started
    n = len(srcs)

    def body(*refs):
        for cp in _peer_copies(refs[:n], refs[n:2 * n], refs[2 * n], refs[2 * n + 1], gather):
            cp.wait_send()
            cp.wait_recv()

    res = pl.pallas_call(
        body, name=name, out_shape=tuple(pltpu.HBM(a.shape, a.dtype) for a in srcs + lands),
        in_specs=[HBM_SPEC] * (2 * n) + [SEM_SPEC, SEM_SPEC, pl.BlockSpec(memory_space=pl.ANY)],
        out_specs=tuple([HBM_SPEC] * (2 * n)), input_output_aliases={i: i for i in range(2 * n)},
        compiler_params=pltpu.CompilerParams(has_side_effects=DATAFLOW))(*srcs, *lands, send_sems, recv_sems, after)
    return list(res[n:])


def gather_two_level(arrs, name):
    n = len(arrs)
    per = 7

    def body(*refs):
        in_refs, out_refs = refs[:n], refs[n:2 * n]
        send_sems, recv_sems = refs[2 * n:]
        x, y, c = _mesh_pos()
        me, sibling = (x, y, c), (x, y, 1 - c)
        chips = [(1 - x, y), (x, 1 - y), (1 - x, 1 - y)]

        def copy(a, k, block, to, src=None):
            rows = out_refs[a].at[4 * block[0] + 2 * block[1] + block[2]]
            return pltpu.make_async_remote_copy(src_ref=rows if src is None else src, dst_ref=rows,
                                                send_sem=send_sems.at[a * per + k], recv_sem=recv_sems.at[a * per + k],
                                                device_id=to, device_id_type=MESH_ID)

        first, passed = [], []
        for a in range(n):
            first.append(copy(a, 0, me, sibling, src=in_refs[a]))
            first += [copy(a, 1 + j, me, (*chip, c), src=in_refs[a]) for j, chip in enumerate(chips)]
        for cp in first:
            cp.start()
        for a in range(n):
            for j, chip in enumerate(chips):
                copy(a, 1 + j, (*chip, c), me).wait_recv()
                fwd = copy(a, 4 + j, (*chip, c), sibling)
                fwd.start()
                passed.append(fwd)
        for a in range(n):
            copy(a, 0, sibling, me).wait_recv()
            for j, chip in enumerate(chips):
                copy(a, 4 + j, (*chip, 1 - c), me).wait_recv()
        for cp in first + passed:
            cp.wait_send()

    any_spec = pl.BlockSpec(memory_space=pl.ANY)
    res = pl.pallas_call(
        body, in_specs=[any_spec] * n, out_specs=[any_spec] * n,
        out_shape=[jax.ShapeDtypeStruct((N_DEV,) + a.shape, a.dtype) for a in arrs],
        scratch_shapes=[pltpu.SemaphoreType.DMA((n * per,)), pltpu.SemaphoreType.DMA((n * per,))],
        name=name)(*arrs)
    return list(res)


def sum_slots(parts, slots, name):
    _, r, c = parts.shape
    tr = r
    for cand in (512, 256, 128, 64, 32, 16, 8):
        if r % cand == 0 and cand * c * 4 * len(slots) <= 8 * 1024 * 1024:
            tr = cand
            break

    def body(p_ref, o_ref):
        acc = p_ref[slots[0]].astype(f32)
        for s in slots[1:]:
            acc = acc + p_ref[s].astype(f32)
        o_ref[...] = acc

    return pl.pallas_call(body, grid=(r // tr,), in_specs=[pl.BlockSpec((parts.shape[0], tr, c), lambda i: (0, i, 0))],
                          out_specs=pl.BlockSpec((tr, c), lambda i: (i, 0)),
                          out_shape=jax.ShapeDtypeStruct((r, c), f32),
                          compiler_params=_cparams(("parallel",)), name=name)(parts)


def column_sum(a, name):
    def body(a_ref, o_ref):
        o_ref[...] = jnp.sum(a_ref[...], axis=0, keepdims=True)

    return pl.pallas_call(body, out_shape=jax.ShapeDtypeStruct((1, a.shape[1]), f32), name=name)(a)


def adamw(w, g, m, v, name):
    r, c = w.shape
    tr = r
    for cand in (256, 128, 64, 32, 16, 8):
        if r % cand == 0:
            tr = cand
            break

    def body(w_ref, g_ref, m_ref, v_ref, d_ref, mo_ref, vo_ref):
        gv = g_ref[...]
        m_new = ADAM_B1 * m_ref[...] + (1.0 - ADAM_B1) * gv
        v_new = ADAM_B2 * v_ref[...] + (1.0 - ADAM_B2) * jnp.square(gv)
        m_hat = m_new / (1.0 - ADAM_B1 ** ADAM_STEP)
        v_hat = v_new / (1.0 - ADAM_B2 ** ADAM_STEP)
        d_ref[...] = -ADAM_LR * (m_hat / (jnp.sqrt(v_hat) + ADAM_EPS) + ADAM_WD * w_ref[...])
        mo_ref[...] = m_new
        vo_ref[...] = v_new

    spec = pl.BlockSpec((tr, c), lambda i: (i, 0))
    return pl.pallas_call(body, grid=(r // tr,), in_specs=[spec] * 4, out_specs=[spec] * 3,
                          out_shape=[jax.ShapeDtypeStruct((r, c), f32)] * 3,
                          compiler_params=_cparams(("parallel",)), name=name)(w, g, m, v)


def adaln_fwd(c_rows, w, b):
    def body(c_ref, w_ref, b_ref, o_ref):
        cv = c_ref[...]
        o_ref[...] = _mxu_dot(cv * jax.nn.sigmoid(cv), w_ref[...]) + b_ref[...]

    return pl.pallas_call(body, out_shape=jax.ShapeDtypeStruct((c_rows.shape[0], w.shape[1]), f32),
                          compiler_params=pltpu.CompilerParams(vmem_limit_bytes=VMEM_LIMIT), name="adaln_fwd")(c_rows, w, b)


def adaln_bwd(c_rows, dm, w):
    def body(c_ref, dm_ref, w_ref, gw_ref, ds_ref):
        cv = c_ref[...]
        gw_ref[...] = _dg(cv * jax.nn.sigmoid(cv), dm_ref[...], 0, 0)
        ds_ref[...] = _dg(dm_ref[...], w_ref[...], 1, 1)

    return pl.pallas_call(body, out_shape=[jax.ShapeDtypeStruct(w.shape, f32),
                                           jax.ShapeDtypeStruct(c_rows.shape, f32)],
                          compiler_params=pltpu.CompilerParams(vmem_limit_bytes=VMEM_LIMIT), name="adaln_bwd")(c_rows, dm, w)


def c_ctx_grad(parts, c_ctx_row):
    def body(p_ref, c_ref, o_ref):
        total = p_ref[0, 0:1, :]
        for s in range(1, N_SHARDS):
            total = total + p_ref[s, 0:1, :]
        _, vjp = jax.vjp(jax.nn.silu, c_ref[...])
        o_ref[...] = vjp(total)[0]

    return pl.pallas_call(body, out_shape=jax.ShapeDtypeStruct((1, D_MODEL), f32), name="c_ctx_grad")(parts, c_ctx_row)


PACK_W = 1024
PACK_ROWS = 8


def _pack(arrs):
    pieces, layout, r0 = [], [], 0
    for a in arrs:
        size = math.prod(a.shape)
        rows = -(-size // (PACK_W * PACK_ROWS)) * PACK_ROWS
        pieces.append(jnp.pad(a.reshape(-1).astype(f32), (0, rows * PACK_W - size)).reshape(rows, PACK_W))
        layout.append((r0, rows, a.shape))
        r0 += rows
    return jnp.concatenate(pieces, axis=0), layout


def _unpack(pack, layout, lead=()):
    n_lead = len(lead)
    outs = []
    for r0, rows, shape in layout:
        piece = pack[(slice(None),) * n_lead + (slice(r0, r0 + rows),)].reshape(lead + (-1,))
        outs.append(piece[..., :math.prod(shape)].reshape(lead + tuple(shape)))
    return outs


W_NAMES = ("c_ctx", "w_ada", "b_ada", "norm1_g", "norm2_g", "w_in", "ret_log_decay", "rwkv_shift_mu", "rwkv_w0",
           "rwkv_w_up", "rwkv_a0", "rwkv_a_up", "rwkv_g_up", "rwkv_k_k", "rwkv_k_a", "rwkv_r_k", "rwkv_ln_w",
           "rwkv_ln_b", "w_out", "w_ff1", "b_ff1", "w_ff2", "b_ff2", "final_g")
COL_SHARDED = ("w_in", "w_ff1")
ROW_SHARDED = ("w_out", "w_ff2")
LAST_SHARDED = ("rwkv_shift_mu", "rwkv_w0", "rwkv_w_up", "rwkv_a0", "rwkv_a_up", "rwkv_g_up")
REPLICATED = ("c_ctx", "b_ada", "norm1_g", "norm2_g", "ret_log_decay", "rwkv_k_k", "rwkv_k_a", "rwkv_r_k",
              "rwkv_ln_w", "rwkv_ln_b", "b_ff1", "b_ff2", "final_g")
N_SHARDS = 4


def _train_step(a):
    x, c, ctx, tgt = a["x"], a["c"], a["ctx"], a["loss_target"]
    bsz = x.shape[0]
    mx, my, mc = _mesh_pos()
    shard = 2 * mx + my
    dev = _device_slot()

    (c_all,) = exchange([jnp.pad(c, ((0, PACK_ROWS - bsz), (0, 0)))], True, ALL_PEERS, "gather_c")
    n_ex = N_DEV * bsz
    c_rows = jnp.concatenate([c_all[:, :bsz].reshape(n_ex, D_MODEL), a["c_ctx"][None, :],
                              jnp.zeros((PACK_ROWS - 1, D_MODEL), f32)], axis=0)
    ada_cols = a["w_ada"].shape[-1]
    b_ada_cols = lax.dynamic_slice_in_dim(a["b_ada"], shard * ada_cols, ada_cols, axis=1)
    mod_cols = adaln_fwd(c_rows, a["w_ada"][0], b_ada_cols)

    def own_half(n):
        w = a[n][0].astype(MXU_DTYPE)
        return lax.dynamic_slice_in_dim(w, mc * (w.shape[0] // 2), w.shape[0] // 2, axis=0)

    def whole_weight(n, gth, own):
        per_chip = lax.dynamic_update_index_in_dim(gth, own, dev, 0).reshape(N_SHARDS, -1, gth.shape[-1])
        return (per_chip.transpose(1, 0, 2).reshape(per_chip.shape[1], -1) if n in COL_SHARDED
                else per_chip.reshape(-1, per_chip.shape[-1]))

    small_pack, small_layout = _pack([a[n][0] for n in LAST_SHARDED])
    own_blocks = [mod_cols, own_half("w_in"), small_pack]
    gathered = gather_two_level(own_blocks, "gather_weights")
    late_own = [own_half(n) for n in LATE_WEIGHTS]
    late_started = exchange_start(late_own, True, "gather_late_start")
    mod_own = lax.dynamic_update_index_in_dim(gathered[0], mod_cols, dev, 0)
    mod_all = jnp.stack([mod_own[s] for s in CHIP_SLOTS], axis=1).reshape(c_rows.shape[0], -1)
    mod_all = mod_all + late_started[-1][0, 0]
    mod_x = lax.dynamic_slice_in_dim(mod_all, dev * bsz, bsz, axis=0).reshape(bsz, 6, D_MODEL)
    mod_ctx = mod_all[n_ex].reshape(6, D_MODEL)
    wt = {"w_in": whole_weight("w_in", gathered[1], own_blocks[1])}

    def late_weights(after):
        lands = exchange_wait(late_started, after, True, "gather_late_wait")
        return {n: whole_weight(n, land, own) for n, land, own in zip(LATE_WEIGHTS, lands, late_own)}

    def grad_blocks(n, gw):
        if n in COL_SHARDED:
            gw = gw.reshape(gw.shape[0], N_SHARDS, -1).transpose(1, 0, 2)
        return gw.reshape(N_DEV, -1, gw.shape[-1]).astype(MXU_DTYPE)

    late_sent, last_sent = {}, {}

    def early_grads(late_g):
        late_sent["blocks"] = [grad_blocks(n, late_g[n]) for n in LATE_WEIGHTS]
        late_sent["started"] = exchange_start(late_sent["blocks"], False, "scatter_late_start")
        return late_sent["started"][-1]

    def last_grads(g_w_in, g_small):
        shard_packs = []
        for s in range(N_SHARDS):
            pieces_s = [lax.slice_in_dim(g_small[n], s * a[n].shape[-1], (s + 1) * a[n].shape[-1],
                                         axis=g_small[n].ndim - 1) for n in LAST_SHARDED]
            pack_s, last_sent["layout"] = _pack(pieces_s)
            shard_packs.append(jnp.pad(pack_s, ((0, -pack_s.shape[0] % (2 * PACK_ROWS)), (0, 0))))
        last_sent["blocks"] = [grad_blocks("w_in", g_w_in), jnp.stack(shard_packs).reshape(N_DEV, -1, PACK_W)]
        last_sent["started"] = exchange_start(last_sent["blocks"], False, "scatter_last_start")
        return last_sent["started"][-1]

    small_own = lax.dynamic_update_index_in_dim(gathered[2], small_pack, dev, 0)
    small_by_chip = _unpack(jnp.stack([small_own[s] for s in CHIP_SLOTS]), small_layout, (N_SHARDS,))
    for n, parts in zip(LAST_SHARDED, small_by_chip):
        wt[n] = jnp.concatenate([parts[s] for s in range(N_SHARDS)], axis=-1)
    for n in ("norm1_g", "norm2_g", "rwkv_k_k", "rwkv_k_a", "rwkv_r_k", "rwkv_ln_w", "rwkv_ln_b", "b_ff1", "b_ff2"):
        wt[n] = a[n]
    wt["ret_log_decay"] = a["ret_log_decay"][0]
    wt["final_g"] = a["final_g"][None, :]

    loss, grad_x, g = layer_step(x, ctx, tgt, mod_x, mod_ctx, wt, late_weights, early_grads, last_grads)

    small_names = [n for n in REPLICATED if n not in ("c_ctx", "b_ada")]
    g_pack, g_layout = _pack([jnp.pad(loss, ((0, 0), (0, PACK_W - loss.shape[1])))] + [g[n] for n in small_names]
                             + [g["mod_x"], g["mod_ctx"]])
    (g_packs,) = gather_two_level([g_pack], "gather_small_grads")
    g_packs = lax.dynamic_update_index_in_dim(g_packs, g_pack, dev, 0)
    g_sum = _unpack(sum_slots(g_packs, tuple(range(N_DEV)), "sum_small_grads"), g_layout)
    loss_total = g_sum[0][0, 0]
    grads = dict(zip(small_names, g_sum[1:1 + len(small_names)]))
    dmod_ctx = g_sum[-1].reshape(1, -1)
    dmod_x = _unpack(g_packs, g_layout, (N_DEV,))[-2].reshape(n_ex, -1)
    dmod = jnp.concatenate([dmod_x, dmod_ctx, jnp.zeros((PACK_ROWS - 1, dmod_x.shape[1]), f32)], axis=0)
    grads["b_ada"] = column_sum(dmod, "b_ada_grad")
    dmod_cols = lax.dynamic_slice_in_dim(dmod, shard * ada_cols, ada_cols, axis=1)
    grads["w_ada"], dsilu = adaln_bwd(c_rows, dmod_cols, a["w_ada"][0])

    dsilu_rows = jnp.broadcast_to(jnp.pad(dsilu[n_ex:n_ex + 1], ((0, PACK_ROWS - 1), (0, 0)))[None],
                                  (N_SHARDS, PACK_ROWS, D_MODEL))
    (shares,) = exchange([dsilu_rows], False, CHIP_PEERS, "share_c_ctx_grad", by_chip=True, own=False)
    shares = lax.dynamic_update_index_in_dim(shares, dsilu_rows[0], shard, 0)
    grads["c_ctx"] = c_ctx_grad(shares, a["c_ctx"][None, :])

    scattered, half_sums = ("w_in", "small_shards") + LATE_WEIGHTS, []
    for sent, wait_name, after in ((last_sent, "scatter_last_wait", grads["c_ctx"]),
                                   (late_sent, "scatter_late_wait", grads["c_ctx"])):
        for land, block in zip(exchange_wait(sent["started"], after, False, wait_name), sent["blocks"]):
            land = lax.dynamic_update_index_in_dim(land, lax.dynamic_index_in_dim(block, dev, 0, keepdims=False), dev, 0)
            half_sums.append(sum_slots(land, tuple(range(N_DEV)), f"sum_{scattered[len(half_sums)]}"))
    other_halves = sibling_swap(half_sums, "swap_halves")
    for n, mine, other in zip(scattered, half_sums, other_halves):
        rows = mine.shape[0]
        whole = jnp.zeros((2 * rows, mine.shape[1]), f32)
        whole = lax.dynamic_update_slice_in_dim(whole, mine, mc * rows, axis=0)
        grads[n] = lax.dynamic_update_slice_in_dim(whole, other, (1 - mc) * rows, axis=0)
    grads.update(zip(LAST_SHARDED, _unpack(grads.pop("small_shards"), last_sent["layout"])))

    out_g, out_d, out_m, out_v = {}, {}, {}, {}
    for n in ("w_ada",) + COL_SHARDED + ROW_SHARDED:
        out_g[n] = grads[n].reshape(a[n].shape)
        two_d = lambda z: z.reshape(-1, z.shape[-1])
        d, m, v = adamw(two_d(a[n]), two_d(out_g[n]), two_d(a["m_" + n]), two_d(a["v_" + n]), f"adamw_{n}")
        out_d[n], out_m[n], out_v[n] = d.reshape(a[n].shape), m.reshape(a[n].shape), v.reshape(a[n].shape)
    rest = REPLICATED + LAST_SHARDED
    for n in rest:
        out_g[n] = grads[n].reshape(a[n].shape)
    packs = [_pack([src[n] for n in rest])[0] for src in
             ({n: a[n] for n in rest}, out_g, {n: a["m_" + n] for n in rest}, {n: a["v_" + n] for n in rest})]
    _, rest_layout = _pack([a[n] for n in rest])
    for dst, pack in zip((out_d, out_m, out_v), adamw(*packs, "adamw_small")):
        dst.update(zip(rest, _unpack(pack, rest_layout)))
    return (loss_total, grad_x, *[out_g[n] for n in W_NAMES], *[out_d[n] for n in W_NAMES],
            *[out_m[n] for n in W_NAMES], *[out_v[n] for n in W_NAMES])


def kernel(x, c, ctx, c_ctx, w_ada, b_ada, norm1_g, norm2_g, w_in, ret_log_decay, rwkv_shift_mu, rwkv_w0, rwkv_w_up, rwkv_a0, rwkv_a_up, rwkv_g_up, rwkv_k_k, rwkv_k_a, rwkv_r_k, rwkv_ln_w, rwkv_ln_b, w_out, w_ff1, b_ff1, w_ff2, b_ff2, final_g, loss_target, m_c_ctx, m_w_ada, m_b_ada, m_norm1_g, m_norm2_g, m_w_in, m_ret_log_decay, m_rwkv_shift_mu, m_rwkv_w0, m_rwkv_w_up, m_rwkv_a0, m_rwkv_a_up, m_rwkv_g_up, m_rwkv_k_k, m_rwkv_k_a, m_rwkv_r_k, m_rwkv_ln_w, m_rwkv_ln_b, m_w_out, m_w_ff1, m_b_ff1, m_w_ff2, m_b_ff2, m_final_g, v_c_ctx, v_w_ada, v_b_ada, v_norm1_g, v_norm2_g, v_w_in, v_ret_log_decay, v_rwkv_shift_mu, v_rwkv_w0, v_rwkv_w_up, v_rwkv_a0, v_rwkv_a_up, v_rwkv_g_up, v_rwkv_k_k, v_rwkv_k_a, v_rwkv_r_k, v_rwkv_ln_w, v_rwkv_ln_b, v_w_out, v_w_ff1, v_b_ff1, v_w_ff2, v_b_ff2, v_final_g):
    return _train_step(dict(locals()))
```

```python
import functools
import math

import jax
import jax.numpy as jnp
from jax import lax
from jax.experimental import pallas as pl
from jax.experimental.pallas import tpu as pltpu

f32 = jnp.float32
MXU_DTYPE = jnp.bfloat16

D_MODEL = 1024
RET_W = 512
RET_HEADS = 4
RET_DH = 128
RET_CHUNK = 128
RW_W = 512
RW_N = 64
DECAY_LORA = 64
AAA_LORA = 64
GATE_LORA = 128
LORA_W = DECAY_LORA + AAA_LORA + GATE_LORA
D_FF = 4096
RET_COLS = 4 * RET_W
SHIFT_COLS = 3 * RW_W + LORA_W
IN_COLS = RET_COLS + SHIFT_COLS
GRID_W = 64
ROPE_BASE = 10000.0
NORM_EPS = 1e-6
GN_EPS = 64e-5
W_DECAY_SCALE = math.exp(-0.5)
ADAM_LR, ADAM_B1, ADAM_B2, ADAM_EPS, ADAM_WD, ADAM_STEP = 0.001, 0.9, 0.999, 1e-08, 0.01, 10

TOK_TILE = 256
MATMUL_TILE = 1024
SCAN_CHUNK = 32
SCAN_UNROLL = SCAN_CHUNK
N_DEV = 8
V7X_VMEM_BYTES = 64 * 1024 * 1024
VMEM_LIMIT = V7X_VMEM_BYTES * 7 // 8


def _cparams(sem):
    return pltpu.CompilerParams(dimension_semantics=sem, vmem_limit_bytes=VMEM_LIMIT)


def _tile(n, cap):
    best = None
    for t in range(128, min(n, cap) + 1, 128):
        if n % t == 0:
            best = t
    return best if best is not None else n


def matmul(a, b, mode, name, out_dtype=f32, bias=None, finish=None):
    if mode == "nn":
        (m, k), (k2, n) = a.shape, b.shape
    elif mode == "nt":
        (m, k), (n, k2) = a.shape, b.shape
    else:
        (k, m), (k2, n) = a.shape, b.shape
    assert k == k2, (a.shape, b.shape, mode)
    tm, tn, tk = _tile(m, MATMUL_TILE), _tile(n, MATMUL_TILE), _tile(k, MATMUL_TILE)
    nk = k // tk
    dims = {"nn": ((1,), (0,)), "nt": ((1,), (1,)), "tn": ((0,), (0,))}[mode]

    def body(a_ref, b_ref, *rest):
        o_ref, acc_ref = rest[-2:]
        kk = pl.program_id(2)

        @pl.when(kk == 0)
        def _():
            acc_ref[...] = jnp.zeros_like(acc_ref)

        acc_ref[...] += lax.dot_general(a_ref[...].astype(MXU_DTYPE), b_ref[...].astype(MXU_DTYPE),
                                        (dims, ((), ())), preferred_element_type=f32)

        @pl.when(kk == nk - 1)
        def _():
            res = acc_ref[...]
            if bias is not None:
                res = res + rest[0][...]
            if finish is not None:
                res = finish(res)
            o_ref[...] = res.astype(o_ref.dtype)

    if mode == "nn":
        a_spec = pl.BlockSpec((tm, tk), lambda i, j, q: (i, q))
        b_spec = pl.BlockSpec((tk, tn), lambda i, j, q: (q, j))
    elif mode == "nt":
        a_spec = pl.BlockSpec((tm, tk), lambda i, j, q: (i, q))
        b_spec = pl.BlockSpec((tn, tk), lambda i, j, q: (j, q))
    else:
        a_spec = pl.BlockSpec((tk, tm), lambda i, j, q: (q, i))
        b_spec = pl.BlockSpec((tk, tn), lambda i, j, q: (q, j))
    extra_specs = [] if bias is None else [pl.BlockSpec((1, tn), lambda i, j, q: (0, j))]
    extra = [] if bias is None else [bias]
    return pl.pallas_call(
        body, grid=(m // tm, n // tn, nk), in_specs=[a_spec, b_spec] + extra_specs,
        out_specs=pl.BlockSpec((tm, tn), lambda i, j, q: (i, j)),
        out_shape=jax.ShapeDtypeStruct((m, n), out_dtype),
        scratch_shapes=[pltpu.VMEM((tm, tn), f32)],
        compiler_params=_cparams(("parallel", "parallel", "arbitrary")), name=name)(a, b, *extra)


class Tiled:
    def __init__(self, arr, w=None, cidx=0, toff=0):
        self.arr, self.w, self.cidx, self.toff = arr, (arr.shape[-1] if w is None else w), cidx, toff

    def spec(self):
        cidx, toff = self.cidx, self.toff
        return pl.BlockSpec((None, TOK_TILE, self.w), lambda b, i: (b, jnp.maximum(i + toff, 0), cidx))


class Seg:
    def __init__(self, arr, seg, first):
        self.arr, self.seg, self.first = arr, seg, first

    def spec(self):
        seg = self.seg
        return pl.BlockSpec((None, None, 1, self.arr.shape[-1]), lambda b, i: (b, seg(i), 0, 0))


class Glob:
    def __init__(self, arr):
        self.arr = arr

    def spec(self):
        return pl.BlockSpec(self.arr.shape, lambda b, i: (0,) * self.arr.ndim)


def ew_forward(fn, name, bsz, n_tiles, ins, outs):
    n_in = len(ins)

    def body(*refs):
        res = fn(*[r[...] for r in refs[:n_in]])
        for o_ref, o in zip(refs[n_in:], res):
            o_ref[...] = o.astype(o_ref.dtype)

    out_specs = [pl.BlockSpec((None, TOK_TILE, w), lambda b, i: (b, i, 0)) for w, _ in outs]
    out_shape = [jax.ShapeDtypeStruct((bsz, n_tiles * TOK_TILE, w), dt) for w, dt in outs]
    return pl.pallas_call(body, grid=(bsz, n_tiles), in_specs=[d.spec() for d in ins], out_specs=out_specs,
                          out_shape=out_shape, compiler_params=_cparams(("parallel", "parallel")), name=name)(
        *[d.arr for d in ins])


def ew_backward(fn, name, bsz, n_tiles, ins, cts, want, grad_dtypes=None, lead=0):
    ct_parts = [c if isinstance(c, tuple) else (c,) for c in cts]
    cts = [part for parts in ct_parts for part in parts]
    n_in, n_ct = len(ins), len(cts)
    diff = [k for k in range(n_in) if want[k]]
    grad_dtypes = grad_dtypes or {}
    assert lead == 0 or not any(isinstance(ins[k], Seg) for k in diff)

    def body(*refs):
        b, i = pl.program_id(0), pl.program_id(1)
        g_refs = refs[n_in + n_ct:]

        def tile_grads():
            vals = [r[...] for r in refs[:n_in]]
            ct_refs = iter(refs[n_in:n_in + n_ct])
            ct_vals = tuple(functools.reduce(lambda s, t: s + t, [next(ct_refs)[...].astype(f32) for _ in parts])
                            for parts in ct_parts)

            def f(*dvals):
                full = list(vals)
                for k, v in zip(diff, dvals):
                    full[k] = v
                return tuple(fn(*full))

            _, vjp = jax.vjp(f, *[vals[k] for k in diff])
            grads = vjp(ct_vals)
            for k, g_ref, g in zip(diff, g_refs, grads):
                d = ins[k]
                if isinstance(d, Tiled):
                    g_ref[...] = g.astype(g_ref.dtype)
                else:
                    zero = d.first(i) if isinstance(d, Seg) else jnp.logical_and(b == 0, i == lead)

                    @pl.when(zero)
                    def _(g_ref=g_ref):
                        g_ref[...] = jnp.zeros_like(g_ref)

                    g_ref[...] += g

        if lead == 0:
            tile_grads()
        else:
            pl.when(i >= lead)(tile_grads)

            @pl.when(i < lead)
            def _():
                for k, g_ref in zip(diff, g_refs):
                    if isinstance(ins[k], Tiled):
                        g_ref[...] = jnp.zeros_like(g_ref)

    out_specs, out_shape = [], []
    for k in diff:
        d = ins[k]
        if isinstance(d, Tiled):
            out_specs.append(pl.BlockSpec((None, TOK_TILE, d.w), lambda b, i: (b, i, 0)))
            out_shape.append(jax.ShapeDtypeStruct((bsz, (n_tiles + lead) * TOK_TILE, d.w), grad_dtypes.get(k, f32)))
        else:
            out_specs.append(d.spec())
            out_shape.append(jax.ShapeDtypeStruct(d.arr.shape, f32))
    return pl.pallas_call(body, grid=(bsz, n_tiles + lead),
                          in_specs=[d.spec() for d in ins] + [c.spec() for c in cts],
                          out_specs=out_specs, out_shape=out_shape,
                          compiler_params=_cparams(("arbitrary", "arbitrary")), name=name)(
        *[d.arr for d in ins], *[c.arr for c in cts])


@jax.custom_vjp
def _mxu_dot(a, b):
    return jnp.dot(a.astype(MXU_DTYPE), b.astype(MXU_DTYPE), preferred_element_type=f32)


def _mxu_dot_fwd(a, b):
    return _mxu_dot(a, b), (a, b)


def _mxu_dot_bwd(res, ct):
    a, b = res
    ct = ct.astype(MXU_DTYPE)
    da = lax.dot_general(ct, b.astype(MXU_DTYPE), (((1,), (1,)), ((), ())), preferred_element_type=f32)
    db = lax.dot_general(a.astype(MXU_DTYPE), ct, (((0,), (0,)), ((), ())), preferred_element_type=f32)
    return da, db


_mxu_dot.defvjp(_mxu_dot_fwd, _mxu_dot_bwd)


def _split_dot_impl(x, ones_mat):
    hi = x.astype(MXU_DTYPE)
    lo = (x - hi.astype(f32)).astype(MXU_DTYPE)
    return jnp.dot(hi, ones_mat, preferred_element_type=f32) + jnp.dot(lo, ones_mat, preferred_element_type=f32)


@jax.custom_vjp
def _split_dot(x, ones_mat):
    return _split_dot_impl(x, ones_mat)


def _split_dot_fwd(x, ones_mat):
    return _split_dot_impl(x, ones_mat), ones_mat


def _split_dot_bwd(ones_mat, ct):
    return _split_dot_impl(ct, ones_mat), None


_split_dot.defvjp(_split_dot_fwd, _split_dot_bwd)


def _block_ones(n, group):
    idx = jnp.arange(n) // group
    return (idx[:, None] == idx[None, :]).astype(MXU_DTYPE)


def _rms(x, g):
    return x * lax.rsqrt(jnp.mean(x * x, axis=-1, keepdims=True) + NORM_EPS) * g


def fn_norm_mod(h, shift, scale, g):
    return (_rms(h, g) * (1.0 + scale) + shift,)


def fn_rwkv_prepare(ks, lora, w0_f, w0_b, a0_f, a0_b, w_up_f, w_up_b, a_up_f, a_up_b, g_up, k_k, k_a, ones64):
    kkr = ks * k_k
    kk = kkr * lax.rsqrt(_split_dot(kkr * kkr, ones64) + 1e-12)
    outs = [kk]
    th = jnp.tanh(lora)
    for w0, a0, w_up, a_up in ((w0_f, a0_f, w_up_f, a_up_f), (w0_b, a0_b, w_up_b, a_up_b)):
        w = jnp.exp(-W_DECAY_SCALE * jax.nn.sigmoid(w0 + _mxu_dot(th, w_up)))
        a = jax.nn.sigmoid(a0 + _mxu_dot(lora, a_up))
        kt = ks * (1.0 + (a - 1.0) * k_a)
        outs += [w, a * kk, kt]
    outs.append(_mxu_dot(jax.nn.sigmoid(lora), g_up))
    return tuple(outs)


def fn_merge(o_f, o_b, g_ret, y_f, y_b, r, kt_f, v, g_rw, r_k, ln_w, ln_b, ones64, ones128):
    o = o_f + o_b
    ret = o * lax.rsqrt(_split_dot(o * o, ones128) * (1.0 / RET_DH) + NORM_EPS) * (g_ret * jax.nn.sigmoid(g_ret))
    y = y_f + y_b
    mean = _split_dot(y, ones64) * (1.0 / RW_N)
    yc = y - mean
    var = _split_dot(yc * yc, ones64) * (1.0 / RW_N)
    y_n = yc * lax.rsqrt(var + GN_EPS) * ln_w + ln_b
    bonus = _split_dot(r * kt_f * r_k, ones64) * v
    return ret, (y_n + bonus) * g_rw


def fn_resid_norm_mod(x, mix, gate, shift, scale, g):
    h1 = x + gate * mix
    return h1, _rms(h1, g) * (1.0 + scale) + shift


def relu2(z):
    return jnp.square(jnp.maximum(z, 0.0))


def relu2_backward(act, dact, name):
    bsz, n_tok, width = act.shape

    def body(a_ref, d_ref, du_ref, db_ref):
        du = d_ref[...].astype(f32) * (2.0 * jnp.sqrt(a_ref[...].astype(f32)))
        du_ref[...] = du.astype(du_ref.dtype)

        @pl.when(jnp.logical_and(pl.program_id(0) == 0, pl.program_id(1) == 0))
        def _():
            db_ref[...] = jnp.zeros_like(db_ref)

        db_ref[...] += jnp.sum(du, axis=0, keepdims=True)

    tile = pl.BlockSpec((None, TOK_TILE, width), lambda b, i: (b, i, 0))
    row = pl.BlockSpec((1, width), lambda b, i: (0, 0))
    return pl.pallas_call(body, grid=(bsz, n_tok // TOK_TILE), in_specs=[tile, tile], out_specs=[tile, row],
                          out_shape=[jax.ShapeDtypeStruct(act.shape, MXU_DTYPE), jax.ShapeDtypeStruct((1, width), f32)],
                          compiler_params=_cparams(("arbitrary", "arbitrary")), name=name)(act, dact)


def fn_loss(h1, f, tgt, gate, b2, g):
    y = _rms(h1 + gate * (f + b2), g)
    err = jnp.square(y - tgt)
    return 0.5 * jnp.sum(jnp.mean(err, axis=-1, keepdims=True), axis=0, keepdims=True)


def loss_and_grads(h1, f, tgt, gate, b2, g, bsz, n_tiles):
    def body(h1_ref, f_ref, t_ref, gate_ref, b2_ref, g_ref, loss_ref, dh1_ref, df_ref, dgate_ref, db2_ref, dg_ref):
        b, i = pl.program_id(0), pl.program_id(1)
        tgt_v = t_ref[...]
        loss, vjp = jax.vjp(lambda a, c, e, p, q: fn_loss(a, c, tgt_v, e, p, q),
                            h1_ref[...], f_ref[...], gate_ref[...], b2_ref[...], g_ref[...])
        dh1, df, dgate, db2, dg = vjp(jnp.ones((1, 1), f32))
        dh1_ref[...] = dh1
        df_ref[...] = df.astype(df_ref.dtype)

        @pl.when(i == 0)
        def _():
            dgate_ref[...] = jnp.zeros_like(dgate_ref)

        @pl.when(jnp.logical_and(b == 0, i == 0))
        def _():
            loss_ref[...] = jnp.zeros_like(loss_ref)
            db2_ref[...] = jnp.zeros_like(db2_ref)
            dg_ref[...] = jnp.zeros_like(dg_ref)

        dgate_ref[...] += dgate
        db2_ref[...] += db2
        dg_ref[...] += dg
        loss_ref[...] += jnp.broadcast_to(loss, loss_ref.shape)

    tile = pl.BlockSpec((None, TOK_TILE, D_MODEL), lambda b, i: (b, i, 0))
    row = pl.BlockSpec((1, D_MODEL), lambda b, i: (0, 0))
    seg = pl.BlockSpec((None, None, 1, D_MODEL), lambda b, i: (b, 0, 0, 0))
    t_tok = n_tiles * TOK_TILE
    return pl.pallas_call(
        body, grid=(bsz, n_tiles), in_specs=[tile, tile, tile, seg, row, row],
        out_specs=[pl.BlockSpec((1, 128), lambda b, i: (0, 0)), tile, tile, seg, row, row],
        out_shape=[jax.ShapeDtypeStruct((1, 128), f32), jax.ShapeDtypeStruct((bsz, t_tok, D_MODEL), f32),
                   jax.ShapeDtypeStruct((bsz, t_tok, D_MODEL), MXU_DTYPE),
                   jax.ShapeDtypeStruct((bsz, 1, 1, D_MODEL), f32),
                   jax.ShapeDtypeStruct((1, D_MODEL), f32), jax.ShapeDtypeStruct((1, D_MODEL), f32)],
        compiler_params=_cparams(("arbitrary", "arbitrary")), name="loss_and_grads")(h1, f, tgt, gate, b2, g)


SHIFT_BLOCK = SHIFT_COLS
HALO_ROWS = 8


def _shift_specs(n_tok, col0, width=SHIFT_BLOCK):
    per_tile = TOK_TILE // HALO_ROWS
    last = n_tok // HALO_ROWS - 1
    tile = pl.BlockSpec((None, TOK_TILE, width), lambda j, b, i: (b, i, col0 + j))
    prev = pl.BlockSpec((None, HALO_ROWS, width),
                        lambda j, b, i: (b, jnp.maximum(i * per_tile - 1, 0), col0 + j))
    nxt = pl.BlockSpec((None, HALO_ROWS, width),
                       lambda j, b, i: (b, jnp.minimum((i + 1) * per_tile, last), col0 + j))
    return tile, prev, nxt


def _shifted(p, prev_ref, next_ref, is_first, is_last):
    row = lax.broadcasted_iota(jnp.int32, p.shape, 0)
    prev_row = jnp.where(is_first, 0.0, prev_ref[HALO_ROWS - 1:HALO_ROWS, :].astype(f32))
    next_row = jnp.where(is_last, 0.0, next_ref[0:1, :].astype(f32))
    prev = jnp.where(row == 0, prev_row, pltpu.roll(p, 1, axis=0))
    nxt = jnp.where(row == TOK_TILE - 1, next_row, pltpu.roll(p, TOK_TILE - 1, axis=0))
    return prev, nxt


def token_shift(px, mu, seg_first, seg_last):
    bsz, n_tok, _ = px.shape
    n_tiles = n_tok // TOK_TILE

    def body(p_ref, prev_ref, next_ref, mu_ref, o_ref):
        i = pl.program_id(2)
        p = p_ref[...]
        prev, nxt = _shifted(p, prev_ref, next_ref, seg_first(i), seg_last(i))
        o_ref[...] = p + mu_ref[0:1, :] * (prev - p) + mu_ref[1:2, :] * (nxt - p)

    tile, prev, nxt = _shift_specs(n_tok, 0)
    return pl.pallas_call(
        body, grid=(SHIFT_COLS // SHIFT_BLOCK, bsz, n_tiles),
        in_specs=[tile, prev, nxt, pl.BlockSpec((2, SHIFT_BLOCK), lambda j, b, i: (0, j))],
        out_specs=pl.BlockSpec((None, TOK_TILE, SHIFT_BLOCK), lambda j, b, i: (b, i, j)),
        out_shape=jax.ShapeDtypeStruct((bsz, n_tok, SHIFT_COLS), f32),
        compiler_params=_cparams(("parallel", "parallel", "parallel")), name="token_shift")(px, px, px, mu)


def token_shift_bwd(d_sections, px, mu, seg_first, seg_last):
    bsz, n_tok, _ = px.shape
    n_tiles = n_tok // TOK_TILE
    d_arrs = [part for section in d_sections for part in section]
    assert SHIFT_BLOCK == SHIFT_COLS == sum(section[0].shape[-1] for section in d_sections)

    def body(*refs):
        d_refs, (p_ref, prev_ref, next_ref, mu_ref, dp_ref, dmu_ref) = refs[:3 * len(d_arrs)], refs[3 * len(d_arrs):]
        b, i = pl.program_id(1), pl.program_id(2)
        first, last = seg_first(i), seg_last(i)

        def summed(which):
            part_refs = iter(d_refs[which::3])
            return jnp.concatenate([functools.reduce(lambda s, t: s + t, [next(part_refs)[...] for _ in section])
                                    for section in d_sections], axis=-1)

        d, p = summed(0), p_ref[...]
        d_prev, d_next = _shifted(d, summed(1), summed(2), first, last)
        p_prev, p_next = _shifted(p, prev_ref, next_ref, first, last)
        mu0, mu1 = mu_ref[0:1, :], mu_ref[1:2, :]
        dp_ref[...] = (d + mu0 * (d_next - d) + mu1 * (d_prev - d)).astype(dp_ref.dtype)

        @pl.when(jnp.logical_and(b == 0, i == 0))
        def _():
            dmu_ref[...] = jnp.zeros_like(dmu_ref)

        dmu_ref[0:1, :] += jnp.sum(d * (p_prev - p), axis=0, keepdims=True)
        dmu_ref[1:2, :] += jnp.sum(d * (p_next - p), axis=0, keepdims=True)

    d_specs = [spec for arr in d_arrs for spec in _shift_specs(n_tok, 0, arr.shape[-1])]
    tile, prev, nxt = _shift_specs(n_tok, 0)
    mu_spec = pl.BlockSpec((2, SHIFT_BLOCK), lambda j, b, i: (0, j))
    return pl.pallas_call(
        body, grid=(SHIFT_COLS // SHIFT_BLOCK, bsz, n_tiles),
        in_specs=d_specs + [tile, prev, nxt, mu_spec],
        out_specs=[pl.BlockSpec((None, TOK_TILE, SHIFT_BLOCK), lambda j, b, i: (b, i, j)), mu_spec],
        out_shape=[jax.ShapeDtypeStruct((bsz, n_tok, SHIFT_COLS), MXU_DTYPE),
                   jax.ShapeDtypeStruct((2, SHIFT_COLS), f32)],
        compiler_params=_cparams(("arbitrary", "arbitrary", "arbitrary")), name="token_shift_bwd")(
        *[arr for arr in d_arrs for _ in range(3)], px, px, px, mu)


def _dg(a, b, ca, cb):
    return lax.dot_general(a.astype(MXU_DTYPE), b.astype(MXU_DTYPE), (((ca,), (cb,)), ((), ())),
                           preferred_element_type=f32)


@jax.custom_vjp
def _mm_nt(a, b):
    return _dg(a, b, 1, 1)


_mm_nt.defvjp(lambda a, b: (_dg(a, b, 1, 1), (a, b)),
              lambda res, ct: (_dg(ct, res[1], 1, 0), _dg(ct, res[0], 0, 0)))


@jax.custom_vjp
def _mm_tn(a, b):
    return _dg(a, b, 0, 0)


_mm_tn.defvjp(lambda a, b: (_dg(a, b, 0, 0), (a, b)),
              lambda res, ct: (_dg(res[1], ct, 1, 1), _dg(res[0], ct, 1, 0)))


ROTARY_PAIR = RET_DH // 4


def _swap_pairs_impl(t):
    lane = lax.broadcasted_iota(jnp.int32, t.shape, 1)
    return jnp.where(lane % (2 * ROTARY_PAIR) < ROTARY_PAIR, pltpu.roll(t, RET_DH - ROTARY_PAIR, axis=1),
                     pltpu.roll(t, ROTARY_PAIR, axis=1))


@jax.custom_vjp
def _swap_pairs(t):
    return _swap_pairs_impl(t)


_swap_pairs.defvjp(lambda t: (_swap_pairs_impl(t), None), lambda _, ct: (_swap_pairs_impl(ct),))


def _ret_chunk(state, q_raw, k_raw, v, cos, sin, ld_row, head, reverse):
    c = RET_CHUNK
    lane = lax.broadcasted_iota(jnp.int32, ld_row.shape, 1)
    lg = -jnp.exp(jnp.sum(jnp.where(lane == head, ld_row, 0.0), axis=-1, keepdims=True))
    rot = lambda t: t * cos + _swap_pairs(t) * sin
    q = rot(q_raw)
    k = rot(k_raw) * (RET_DH ** -0.5)
    ti = lax.broadcasted_iota(jnp.int32, (c, 1), 0).astype(f32)
    tj = lax.broadcasted_iota(jnp.int32, (1, c), 1).astype(f32)
    if not reverse:
        dist, mask, q_exp, k_exp = ti - tj, (ti - tj) >= 0, ti + 1.0, c - 1.0 - ti
    else:
        dist, mask, q_exp, k_exp = tj - ti, (tj - ti) > 0, c - ti, ti
    decay = jnp.where(mask, jnp.exp(lg * jnp.maximum(dist, 0.0)), 0.0)
    scores = _mm_nt(q, k) * decay
    out = _mxu_dot(scores, v) + _mxu_dot(q * jnp.exp(lg * q_exp), state)
    new_state = state * jnp.exp(lg * c) + _mm_tn(k * jnp.exp(lg * k_exp), v)
    return out, new_state


def _ret_specs(bsz, order):
    tok = lambda col=0: pl.BlockSpec((bsz, RET_CHUNK, RET_W), lambda i: (0, order(i), col))
    tab = pl.BlockSpec((RET_CHUNK, RET_DH), lambda i: (order(i), 0))
    ld = pl.BlockSpec((1, RET_DH), lambda i: (0, 0))
    return tok, tab, ld


def retention_fwd(px, cos, sin, ld_row, order, reverse, name):
    bsz, n_tok, _ = px.shape
    n_ch = n_tok // RET_CHUNK

    def body(q_ref, k_ref, v_ref, cos_ref, sin_ref, ld_ref, o_ref, sv_ref, st_ref):
        @pl.when(pl.program_id(0) == 0)
        def _():
            st_ref[...] = jnp.zeros_like(st_ref)

        for b in range(bsz):
            for h in range(RET_HEADS):
                sl = slice(h * RET_DH, (h + 1) * RET_DH)
                s = st_ref[b, h]
                sv_ref[b, h] = s
                o, s_new = _ret_chunk(s, q_ref[b, :, sl], k_ref[b, :, sl], v_ref[b, :, sl], cos_ref[...], sin_ref[...],
                                      ld_ref[...], h, reverse)
                o_ref[b, :, sl] = o
                st_ref[b, h] = s_new

    tok, tab, ld = _ret_specs(bsz, order)
    return pl.pallas_call(
        body, grid=(n_ch,), in_specs=[tok(0), tok(1), tok(2), tab, tab, ld],
        out_specs=[tok(), pl.BlockSpec((bsz, None, RET_HEADS, RET_DH, RET_DH), lambda i: (0, i, 0, 0, 0))],
        out_shape=[jax.ShapeDtypeStruct((bsz, n_tok, RET_W), f32),
                   jax.ShapeDtypeStruct((bsz, n_ch, RET_HEADS, RET_DH, RET_DH), f32)],
        scratch_shapes=[pltpu.VMEM((bsz, RET_HEADS, RET_DH, RET_DH), f32)],
        compiler_params=_cparams(("arbitrary",)), name=name)(px, px, px, cos, sin, ld_row)


def retention_bwd(do, px, states, cos, sin, ld_row, order, reverse, name, add_to=()):
    bsz, n_tok, _ = px.shape
    n_ch = n_tok // RET_CHUNK
    back = lambda i: order(n_ch - 1 - i)

    def body(do_ref, q_ref, k_ref, v_ref, sv_ref, cos_ref, sin_ref, ld_ref, *rest):
        add_refs, (dq_ref, dk_ref, dv_ref, dld_ref, dst_ref) = rest[:-5] or (None,) * 3, rest[-5:]

        @pl.when(pl.program_id(0) == 0)
        def _():
            dst_ref[...] = jnp.zeros_like(dst_ref)
            dld_ref[...] = jnp.zeros_like(dld_ref)

        cos_v, sin_v = cos_ref[...], sin_ref[...]
        for b in range(bsz):
            for h in range(RET_HEADS):
                sl = slice(h * RET_DH, (h + 1) * RET_DH)
                f = lambda s, q, k, v, ld, h=h: _ret_chunk(s, q, k, v, cos_v, sin_v, ld, h, reverse)
                _, vjp = jax.vjp(f, sv_ref[b, h], q_ref[b, :, sl], k_ref[b, :, sl], v_ref[b, :, sl], ld_ref[...])
                ds, dq, dk, dv, dld = vjp((do_ref[b, :, sl], dst_ref[b, h]))
                dst_ref[b, h] = ds
                for o_ref, add_ref, val in zip((dq_ref, dk_ref, dv_ref), add_refs, (dq, dk, dv)):
                    if add_ref is not None:
                        val = add_ref[b, :, sl] + val
                    o_ref[b, :, sl] = val.astype(o_ref.dtype)
                dld_ref[...] += dld

    tok, tab, ld = _ret_specs(bsz, back)
    return pl.pallas_call(
        body, grid=(n_ch,),
        in_specs=[tok(), tok(0), tok(1), tok(2),
                  pl.BlockSpec((bsz, None, RET_HEADS, RET_DH, RET_DH), lambda i: (0, n_ch - 1 - i, 0, 0, 0)),
                  tab, tab, ld] + [tok() for _ in add_to],
        out_specs=[tok(), tok(), tok(), ld],
        out_shape=[jax.ShapeDtypeStruct((bsz, n_tok, RET_W), MXU_DTYPE if add_to else f32)] * 3
        + [jax.ShapeDtypeStruct((1, RET_DH), f32)],
        scratch_shapes=[pltpu.VMEM((bsz, RET_HEADS, RET_DH, RET_DH), f32)],
        compiler_params=_cparams(("arbitrary",)), name=name)(
        do, px, px, px, states, cos, sin, ld_row, *add_to)


HALF_W = RW_W // 2


def _head_sum(x, ones):
    xm = x.astype(MXU_DTYPE)
    return jnp.concatenate([jnp.dot(xm[:, :HALF_W], ones, preferred_element_type=f32),
                            jnp.dot(xm[:, HALF_W:], ones, preferred_element_type=f32)], axis=1)


def _stack(parts):
    return jnp.concatenate(parts, axis=0)


def _row(ref, b, t):
    return ref[b, pl.ds(t, 1), :]


SCAN_DIRS = ((False, True), (True, False))
RW_HEADS = RW_W // RW_N
HEAD_ROWS_PAD = 16


def _head_rows(row, mask):
    return jnp.broadcast_to(row, mask.shape) * mask


def _outer(per_value, row, mask_pad):
    return lax.dot_general(per_value.astype(MXU_DTYPE), _head_rows(row, mask_pad).astype(MXU_DTYPE),
                           (((0,), (0,)), ((), ())), preferred_element_type=f32)


def _read(states, rows, mask, more_rows=()):
    lhs = _stack([_head_rows(r, mask) for r in list(rows) + list(more_rows)])
    return lax.dot_general(lhs.astype(MXU_DTYPE), _stack(states).astype(MXU_DTYPE), (((1,), (1,)), ((), ())),
                           preferred_element_type=f32)


def _own_block(raw, b):
    lanes = raw[:, RW_N * b:RW_N * (b + 1)]
    turned = _stack([lanes[RW_HEADS * b:], lanes[:RW_HEADS * b]]) if b else lanes
    if turned.shape[0] < HEAD_ROWS_PAD:
        turned = _stack([turned, jnp.zeros((HEAD_ROWS_PAD - turned.shape[0], RW_N), f32)])
    return turned[:HEAD_ROWS_PAD]


def _row_from_heads(per_value, state, mask_pad):
    full = jnp.dot(per_value.astype(MXU_DTYPE), state.astype(MXU_DTYPE), preferred_element_type=f32)
    return jnp.sum(full * mask_pad, axis=0, keepdims=True)


def _scan_specs(bsz, order):
    rows = lambda col=0: pl.BlockSpec((bsz, SCAN_CHUNK, RW_W), lambda i: (0, order(i), col))
    per_value = pl.BlockSpec((bsz, SCAN_CHUNK, HEAD_ROWS_PAD, RW_N), lambda i: (0, order(i), 0, 0))
    states = pl.BlockSpec((SCAN_CHUNK, bsz, RW_N, RW_W), lambda i: (order(i), 0, 0, 0))
    blocks = pl.BlockSpec((SCAN_CHUNK, RW_HEADS * bsz, RW_N * bsz), lambda i: (order(i), 0, 0))
    return rows, per_value, states, blocks


def _mxu_operands(states):
    return [s.astype(MXU_DTYPE) for s in states]


def _removed(states_m, kk_t, ones, bsz):
    removed = _head_sum(_stack([states_m[b] * kk_t[b].astype(MXU_DTYPE) for b in range(bsz)]), ones)
    return [removed[b * RW_N:(b + 1) * RW_N] for b in range(bsz)]


def _advance(sp, rem, w_t, b_t, vk, bsz):
    return [sp[b] * w_t[b] - rem[b] * b_t[b] + vk[b] for b in range(bsz)]


def heads_to_rows(a):
    b, t, _ = a.shape
    return jnp.pad(a.astype(MXU_DTYPE).reshape(b, t, RW_HEADS, RW_N),
                   ((0, 0), (0, 0), (0, HEAD_ROWS_PAD - RW_HEADS), (0, 0)))


def _blocks_to_rows(raw_ref, first, row_ref, bsz):
    steps = pl.ds(first, SCAN_CHUNK)
    for b in range(bsz):
        for h in range(RW_HEADS):
            row_ref[b, :, h * RW_N:(h + 1) * RW_N] = raw_ref[steps, RW_HEADS * b + h, RW_N * b:RW_N * (b + 1)]


N_ROWS_FWD = 5
N_ROWS_BWD = 5


def _scan_consts(bsz):
    head = (jnp.arange(RW_W)[None, :] // RW_N == jnp.arange(RW_HEADS)[:, None]).astype(f32)
    return head, jnp.pad(head, ((0, HEAD_ROWS_PAD - RW_HEADS), (0, 0))), _block_ones(HALF_W, RW_N)


def _const_specs(consts):
    return [pl.BlockSpec(c.shape, lambda i: (0, 0)) for c in consts]


def rwkv_scan_fwd(rows_in, v_heads, orders, name):
    bsz, n_tok, _ = rows_in[0][0][0].shape
    n_ch = n_tok // SCAN_CHUNK
    rng = range(bsz)
    consts = _scan_consts(bsz)

    def body(*refs):
        rows = [refs[:N_ROWS_FWD], refs[N_ROWS_FWD:2 * N_ROWS_FWD]]
        (v0, v1, head_ref, pad_ref, ones_ref, y0, y1, h0, h1, f0, f1, m0, m1, s0, s1, late_ref,
         raw_ref) = refs[2 * N_ROWS_FWD:]
        v_refs, y_refs, hist_refs, final_refs, s_refs = (v0, v1), (y0, y1), (h0, h1), (f0, f1), (s0, s1)
        removed_refs = (m0, m1)
        n_blk = RW_HEADS * bsz
        head_v, pad_v, ones_v = head_ref[...], pad_ref[...], ones_ref[...]
        for d in range(2):
            @pl.when(pl.program_id(0) == 0)
            def _(d=d):
                s_refs[d][...] = jnp.zeros_like(s_refs[d])

        def step(j, carry):
            ts = [SCAN_CHUNK - 1 - j if reverse else j for reverse, _ in SCAN_DIRS]
            sps = [[s_refs[d][b] for b in rng] for d in range(2)]
            sps_m = [_mxu_operands(sps[d]) for d in range(2)]
            vks = [[_outer(v_refs[d][b, ts[d]], _row(rows[d][4], b, ts[d]), pad_v) for b in rng] for d in range(2)]
            rems = [_removed(sps_m[d], [_row(rows[d][1], b, ts[d]) for b in rng], ones_v, bsz) for d in range(2)]
            for d, (reverse, inclusive) in enumerate(SCAN_DIRS):
                r_ref = rows[d][0]
                read_at = jnp.maximum(j - 1, 0) if inclusive else ts[d]
                both = _read(sps_m[d], [_row(r_ref, b, read_at) for b in rng], head_v,
                             [_row(rows[d][1], b, ts[d]) for b in rng])
                if inclusive:
                    late_ref[j] = both[:n_blk]
                else:
                    raw_ref[ts[d]] = both[:n_blk]
                removed_refs[d][ts[d]] = both[n_blk:]
            for d in range(2):
                new = _advance(sps[d], rems[d], [_row(rows[d][2], b, ts[d]) for b in rng],
                               [_row(rows[d][3], b, ts[d]) for b in rng], vks[d], bsz)
                for b in rng:
                    hist_refs[d][ts[d], b] = sps_m[d][b]
                    s_refs[d][b] = new[b]
            return carry

        lax.fori_loop(0, SCAN_CHUNK, step, 0, unroll=SCAN_UNROLL)
        for d, (reverse, inclusive) in enumerate(SCAN_DIRS):
            final_refs[d][...] = s_refs[d][...]
            if inclusive:
                assert not reverse
                last = SCAN_CHUNK - 1
                late_ref[SCAN_CHUNK] = _read(_mxu_operands([s_refs[d][b] for b in rng]),
                                             [rows[d][0][b, last:last + 1, :] for b in rng], head_v)
                _blocks_to_rows(late_ref, 1, y_refs[d], bsz)
            else:
                _blocks_to_rows(raw_ref, 0, y_refs[d], bsz)

    specs = [_scan_specs(bsz, orders[d]) for d in range(2)]
    state = pltpu.VMEM((bsz, RW_N, RW_W), f32)
    late = pltpu.VMEM((SCAN_CHUNK + 1, RW_HEADS * bsz, RW_N * bsz), f32)
    raw = pltpu.VMEM((SCAN_CHUNK, RW_HEADS * bsz, RW_N * bsz), f32)
    final_spec = pl.BlockSpec((bsz, RW_N, RW_W), lambda i: (0, 0, 0))
    return pl.pallas_call(
        body, grid=(n_ch,),
        in_specs=[specs[d][0](col) for d in range(2) for _, col in rows_in[d]] + [specs[0][1], specs[1][1]]
        + _const_specs(consts),
        out_specs=[specs[0][0](), specs[1][0](), specs[0][2], specs[1][2], final_spec, final_spec,
                   specs[0][3], specs[1][3]],
        out_shape=[jax.ShapeDtypeStruct((bsz, n_tok, RW_W), f32)] * 2
        + [jax.ShapeDtypeStruct((n_tok, bsz, RW_N, RW_W), MXU_DTYPE)] * 2
        + [jax.ShapeDtypeStruct((bsz, RW_N, RW_W), f32)] * 2
        + [jax.ShapeDtypeStruct((n_tok, RW_HEADS * bsz, RW_N * bsz), f32)] * 2,
        scratch_shapes=[state, state, late, raw],
        compiler_params=_cparams(("arbitrary",)), name=name)(
        *[a for d in range(2) for a, _ in rows_in[d]], v_heads, v_heads, *consts)


def rwkv_scan_bwd(rows_in, v_heads, dy_heads, hists, finals, removed, orders, name):
    bsz, n_tok, _ = rows_in[0][0][0].shape
    n_ch = n_tok // SCAN_CHUNK
    backs = [functools.partial(lambda i, order: order(n_ch - 1 - i), order=orders[d]) for d in range(2)]
    rng = range(bsz)
    consts = _scan_consts(bsz)
    n_out, n_scr = 6, 6

    def body(*refs):
        rows = [refs[:N_ROWS_BWD], refs[N_ROWS_BWD:2 * N_ROWS_BWD]]
        rest = refs[2 * N_ROWS_BWD:]
        v_refs, dy_refs, hist_refs, final_refs, removed_refs = rest[0:2], rest[2:4], rest[4:6], rest[6:8], rest[8:10]
        head_ref, pad_ref, ones_ref = rest[10:13]
        outs = [rest[13:13 + n_out], rest[13 + n_out:13 + 2 * n_out]]
        scr = [rest[13 + 2 * n_out:13 + 2 * n_out + n_scr], rest[13 + 2 * n_out + n_scr:]]
        n_blk = RW_HEADS * bsz
        head_v, pad_v, ones_v = head_ref[...], pad_ref[...], ones_ref[...]
        for d in range(2):
            @pl.when(pl.program_id(0) == 0)
            def _(d=d):
                scr[d][1][...] = jnp.zeros_like(scr[d][1])
                scr[d][0][...] = final_refs[d][...]

        def step_of(j, reverse):
            return j if reverse else SCAN_CHUNK - 1 - j

        for d, (reverse, _) in enumerate(SCAN_DIRS):
            t0 = step_of(0, reverse)
            for b in rng:
                scr[d][3][b] = _outer(dy_refs[d][b, t0], rows[d][0][b, t0:t0 + 1, :], pad_v)

        def bstep(j, carry):
            ts = [step_of(j, reverse) for reverse, _ in SCAN_DIRS]
            reads = [[scr[d][3][b] for b in rng] for d in range(2)]
            dss = []
            for d, (_, inclusive) in enumerate(SCAN_DIRS):
                ds = [scr[d][1][b] for b in rng]
                dss.append([ds[b] + reads[d][b] for b in rng] if inclusive else ds)
            dss_m = [_mxu_operands(dss[d]) for d in range(2)]
            nexts = []
            for d, (reverse, _) in enumerate(SCAN_DIRS):
                t_next = step_of(jnp.minimum(j + 1, SCAN_CHUNK - 1), reverse)
                nexts.append([_outer(dy_refs[d][b, t_next], _row(rows[d][0], b, t_next), pad_v) for b in rng])
            drems = [_removed(dss_m[d], [-_row(rows[d][3], b, ts[d]) for b in rng], ones_v, bsz) for d in range(2)]
            for d in range(2):
                for b in rng:
                    scr[d][3][b] = nexts[d][b]
                both = _read(dss_m[d], [_row(rows[d][4], b, ts[d]) for b in rng], head_v,
                             [-_row(rows[d][3], b, ts[d]) for b in rng])
                scr[d][4][ts[d]] = both[:n_blk]
                scr[d][5][ts[d]] = both[n_blk:]
            for d, (_, inclusive) in enumerate(SCAN_DIRS):
                _, kk_ref, w_ref, _, _ = rows[d]
                _, ds_ref, dsh_ref = scr[d][:3]
                for b in rng:
                    dsh_ref[ts[d], b] = dss[d][b]
                    dsp = dss[d][b] * _row(w_ref, b, ts[d]) + drems[d][b] * _row(kk_ref, b, ts[d])
                    ds_ref[b] = dsp if inclusive else dsp + reads[d][b]
            return carry

        lax.fori_loop(0, SCAN_CHUNK, bstep, 0, unroll=SCAN_UNROLL)

        rsum = lambda z: jnp.sum(z, axis=0, keepdims=True)
        for d, (reverse, inclusive) in enumerate(SCAN_DIRS):
            dr_ref, dkk_ref, dw_ref, db_ref, dkt_ref, dv_ref = outs[d]
            after_ref, _, dsh_ref, _, dv_raw_ref, dremt_ref = scr[d]
            hist_ref, removed_ref = hist_refs[d], removed_refs[d]
            _blocks_to_rows(dv_raw_ref, 0, dv_ref, bsz)
            for t in range(SCAN_CHUNK):
                ts = slice(t, t + 1)
                after = t - 1 if reverse else t + 1
                for b in rng:
                    sp_m, ds = hist_ref[t, b], dsh_ref[t, b]
                    sp = sp_m.astype(f32)
                    if not inclusive:
                        seen = sp_m
                    else:
                        seen = hist_ref[after, b] if 0 <= after < SCAN_CHUNK else after_ref[b]
                    dr_ref[b, ts, :] = _row_from_heads(dy_refs[d][b, t], seen, pad_v)
                    dkt_ref[b, ts, :] = _row_from_heads(v_refs[d][b, t], ds, pad_v)
                    dw_ref[b, ts, :] = rsum(ds * sp)
                    db_ref[b, ts, :] = -_row_from_heads(_own_block(removed_ref[t], b), ds, pad_v)
                    dkk_ref[b, ts, :] = _row_from_heads(_own_block(dremt_ref[t], b), sp_m, pad_v)
            if inclusive:
                first = SCAN_CHUNK - 1 if reverse else 0
                for b in rng:
                    after_ref[b] = hist_ref[first, b].astype(f32)

    specs = [_scan_specs(bsz, backs[d]) for d in range(2)]
    hist = pltpu.VMEM((SCAN_CHUNK, bsz, RW_N, RW_W), f32)
    state = pltpu.VMEM((bsz, RW_N, RW_W), f32)
    final_spec = pl.BlockSpec((bsz, RW_N, RW_W), lambda i: (0, 0, 0))
    raw = pltpu.VMEM((SCAN_CHUNK, RW_HEADS * bsz, RW_N * bsz), f32)
    return pl.pallas_call(
        body, grid=(n_ch,),
        in_specs=[specs[d][0](col) for d in range(2) for _, col in rows_in[d]]
        + [specs[0][1], specs[1][1]] * 2 + [specs[0][2], specs[1][2], final_spec, final_spec, specs[0][3], specs[1][3]]
        + _const_specs(consts),
        out_specs=[specs[d][0]() for d in range(2) for _ in range(n_out)],
        out_shape=[jax.ShapeDtypeStruct((bsz, n_tok, RW_W), f32)] * (2 * n_out),
        scratch_shapes=[state, state, hist, state, raw, raw] * 2,
        compiler_params=_cparams(("arbitrary",)), name=name)(
        *[a for d in range(2) for a, _ in rows_in[d]], v_heads, v_heads, dy_heads, dy_heads, *hists, *finals, *removed, *consts)


MOD_NAMES = ("shift1", "scale1", "gate1", "shift2", "scale2", "gate2")


def _rope_tables(t_ctx, t_x):
    quarter = RET_DH // 4
    pos = jnp.arange(t_x)
    inv = jnp.power(ROPE_BASE, -jnp.arange(0, 2 * quarter, 2, dtype=f32) / (2 * quarter))
    ang_r = (pos // GRID_W).astype(f32)[:, None] * inv[None, :]
    ang_c = (pos % GRID_W).astype(f32)[:, None] * inv[None, :]
    cos = jnp.concatenate([jnp.cos(ang_r)] * 2 + [jnp.cos(ang_c)] * 2, axis=1)
    sin = jnp.concatenate([-jnp.sin(ang_r), jnp.sin(ang_r), -jnp.sin(ang_c), jnp.sin(ang_c)], axis=1)
    cos = jnp.concatenate([jnp.ones((t_ctx, RET_DH), f32), cos], axis=0)
    sin = jnp.concatenate([jnp.zeros((t_ctx, RET_DH), f32), sin], axis=0)
    return cos, sin


def _pad_rows(w, lo, total):
    return jnp.pad(w, ((lo, total - lo - w.shape[0]), (0, 0)))


LATE_WEIGHTS = ("w_out", "w_ff1", "w_ff2")


def layer_step(x, ctx, tgt, mod_x, mod_ctx, wt, late_weights=None, early_grads=None, last_grads=None):
    bsz, t_x, _ = x.shape
    t_c = ctx.shape[1]
    t_all = t_c + t_x
    n_ct, n_xt = t_c // TOK_TILE, t_x // TOK_TILE
    n_t = n_ct + n_xt
    assert t_c % TOK_TILE == 0 and t_x % TOK_TILE == 0 and t_c % RET_CHUNK == 0

    seg = lambda i: (i >= n_ct).astype(jnp.int32)
    seg_first = lambda i: jnp.logical_or(i == 0, i == n_ct)
    seg_last = lambda i: jnp.logical_or(i == n_ct - 1, i == n_t - 1)
    mod_all = {n: jnp.stack([jnp.broadcast_to(mod_ctx[k], (bsz, D_MODEL)), mod_x[:, k]], axis=1)[:, :, None, :]
               for k, n in enumerate(MOD_NAMES)}
    mod_lat = {n: mod_x[:, k][:, None, None, :] for k, n in enumerate(MOD_NAMES)}
    both = lambda n: Seg(mod_all[n], seg, seg_first)
    lat = lambda n: Seg(mod_lat[n], lambda i: 0, lambda i: i == 0)
    flat = lambda a: a.reshape(-1, a.shape[-1])

    def chunk_orders(n_ctx_chunks, n_chunks):
        fwd = lambda i: i
        bwd = lambda i: jnp.where(i < n_ctx_chunks, n_ctx_chunks - 1 - i, n_chunks + n_ctx_chunks - 1 - i)
        return fwd, bwd

    ones64, ones128 = _block_ones(RW_W, RW_N), _block_ones(RET_W, RET_DH)
    cos, sin = _rope_tables(t_c, t_x)
    ld_rows = [jnp.pad(wt["ret_log_decay"][d][None, :], ((0, 0), (0, RET_DH - RET_HEADS))) for d in range(2)]
    w_up_pad = [_pad_rows(wt["rwkv_w_up"][d], 0, LORA_W) for d in range(2)]
    a_up_pad = [_pad_rows(wt["rwkv_a_up"][d], DECAY_LORA, LORA_W) for d in range(2)]
    g_up_pad = _pad_rows(wt["rwkv_g_up"], DECAY_LORA + AAA_LORA, LORA_W)
    row = lambda a, d: a[d][None, :]

    h = jnp.concatenate([ctx, x], axis=1)
    norm1_ins = lambda: [Tiled(h), both("shift1"), both("scale1"), Glob(wt["norm1_g"])]
    (n1,) = ew_forward(fn_norm_mod, "norm1", bsz, n_t, norm1_ins(), [(D_MODEL, MXU_DTYPE)])
    px = matmul(flat(n1), wt["w_in"], "nn", "proj_in").reshape(bsz, t_all, IN_COLS)
    px_rw = px[..., RET_COLS:]
    ps = token_shift(px_rw, wt["rwkv_shift_mu"], seg_first, seg_last)

    def prep_ins():
        return [Tiled(ps, RW_W, 1), Tiled(ps, LORA_W, 3 * RW_W // LORA_W),
                Glob(row(wt["rwkv_w0"], 0)), Glob(row(wt["rwkv_w0"], 1)),
                Glob(row(wt["rwkv_a0"], 0)), Glob(row(wt["rwkv_a0"], 1)),
                Glob(w_up_pad[0]), Glob(w_up_pad[1]), Glob(a_up_pad[0]), Glob(a_up_pad[1]), Glob(g_up_pad),
                Glob(wt["rwkv_k_k"]), Glob(wt["rwkv_k_a"]), Glob(ones64)]

    kk, w_f, b_f, kt_f, w_b, b_b, kt_b, g_rw = ew_forward(fn_rwkv_prepare, "rwkv_prepare", bsz, n_t, prep_ins(),
                                                           [(RW_W, f32)] * 8)
    rw_order = chunk_orders(t_c // SCAN_CHUNK, t_all // SCAN_CHUNK)
    ret_order = chunk_orders(t_c // RET_CHUNK, t_all // RET_CHUNK)
    scan_rows = [[(ps, 0), (kk, 0), (w_f, 0), (b_f, 0), (kt_f, 0)], [(ps, 0), (kk, 0), (w_b, 0), (b_b, 0), (kt_b, 0)]]
    v_heads = heads_to_rows(ps[..., 2 * RW_W:3 * RW_W])
    y_f, y_b, *kept_states = rwkv_scan_fwd(scan_rows, v_heads, rw_order, "rwkv_scan_fwd")
    y = [y_f, y_b]
    o, ret_states = [], []
    for d in range(2):
        o_d, st_d = retention_fwd(px, cos, sin, ld_rows[d], ret_order[d], SCAN_DIRS[d][0], f"retention_fwd{d}")
        o.append(o_d), ret_states.append(st_d)

    def merge_ins(toff):
        return [Tiled(o[0], toff=toff), Tiled(o[1], toff=toff), Tiled(px, RET_W, 3, toff),
                Tiled(y[0], toff=toff), Tiled(y[1], toff=toff), Tiled(ps, RW_W, 0, toff), Tiled(kt_f, toff=toff),
                Tiled(ps, RW_W, 2, toff), Tiled(g_rw, toff=toff),
                Glob(wt["rwkv_r_k"]), Glob(wt["rwkv_ln_w"]), Glob(wt["rwkv_ln_b"]), Glob(ones64), Glob(ones128)]

    ret_out, rw_out = ew_forward(fn_merge, "merge_heads", bsz, n_xt, merge_ins(n_ct),
                                 [(RET_W, MXU_DTYPE), (RW_W, MXU_DTYPE)])
    merged = jnp.concatenate([ret_out, rw_out], axis=-1)
    if late_weights is not None:
        wt = {**wt, **late_weights(merged)}
    mix = matmul(flat(merged), wt["w_out"], "nn", "proj_out").reshape(bsz, t_x, D_MODEL)
    resid_ins = lambda: [Tiled(x), Tiled(mix), lat("gate1"), lat("shift2"), lat("scale2"), Glob(wt["norm2_g"])]
    h1, n2 = ew_forward(fn_resid_norm_mod, "resid_norm2", bsz, n_xt, resid_ins(), [(D_MODEL, f32), (D_MODEL, MXU_DTYPE)])
    act = matmul(flat(n2), wt["w_ff1"], "nn", "ff1", MXU_DTYPE, wt["b_ff1"], relu2).reshape(bsz, t_x, D_FF)
    ff = matmul(flat(act), wt["w_ff2"], "nn", "ff2").reshape(bsz, t_x, D_MODEL)

    g = {}
    loss, dh1, dff, dgate2, g["b_ff2"], g["final_g"] = loss_and_grads(
        h1, ff, tgt, mod_lat["gate2"], wt["b_ff2"], wt["final_g"], bsz, n_xt)
    dact = matmul(flat(dff), wt["w_ff2"], "nt", "ff2_dx", MXU_DTYPE).reshape(bsz, t_x, D_FF)
    g["w_ff2"] = matmul(flat(act), flat(dff), "tn", "ff2_dw", MXU_DTYPE)
    du, g["b_ff1"] = relu2_backward(act, dact, "relu2_bwd")
    dn2 = matmul(flat(du), wt["w_ff1"], "nt", "ff1_dx").reshape(bsz, t_x, D_MODEL)
    g["w_ff1"] = matmul(flat(n2), flat(du), "tn", "ff1_dw", MXU_DTYPE)
    dx_res, dmix, dgate1, dshift2, dscale2, g["norm2_g"] = ew_backward(
        fn_resid_norm_mod, "resid_norm2_bwd", bsz, n_xt, resid_ins(), [Tiled(dh1), Tiled(dn2)], [True] * 6,
        {1: MXU_DTYPE})
    dmerged = matmul(flat(dmix), wt["w_out"], "nt", "proj_out_dx").reshape(bsz, t_x, D_MODEL)
    g["w_out"] = matmul(flat(merged), flat(dmix), "tn", "proj_out_dw", MXU_DTYPE)
    if early_grads is not None:
        token = early_grads({n: g.pop(n) for n in LATE_WEIGHTS})
        wt = {**wt, "rwkv_r_k": wt["rwkv_r_k"] + token[:1, :1]}
    (do, dg_ret, dy, dr_m, dkt_m, dv_m, dg_rw, g["rwkv_r_k"], g["rwkv_ln_w"], g["rwkv_ln_b"]) = ew_backward(
        fn_merge, "merge_heads_bwd", bsz, n_xt, merge_ins(0),
        [Tiled(dmerged, RET_W, 0, -n_ct), Tiled(dmerged, RW_W, 1, -n_ct)],
        [True, False, True, True, False, True, True, True, True, True, True, True, False, False], lead=n_ct)

    dqkv, dld = (), []
    for d in range(2):
        *dqkv, dld_d = retention_bwd(do, px, ret_states[d], cos, sin, ld_rows[d], ret_order[d],
                                     SCAN_DIRS[d][0], f"retention_bwd{d}", add_to=dqkv)
        dld.append(dld_d[0, :RET_HEADS])
    g["ret_log_decay"] = jnp.stack(dld)
    (dr_f, dkk_f, dw_f, db_f, dkt_f, dv_f, dr_b, dkk_b, dw_b, db_b, dkt_b, dv_b) = rwkv_scan_bwd(
        scan_rows, v_heads, heads_to_rows(dy), kept_states[:2], kept_states[2:4], kept_states[4:], rw_order, "rwkv_scan_bwd")
    prep_cts = [(dkk_f, dkk_b), dw_f, db_f, (dkt_f, dkt_m), dw_b, db_b, dkt_b, dg_rw]
    (dks, dlora, dw0_f, dw0_b, da0_f, da0_b, dwup_f, dwup_b, daup_f, daup_b, dgup, g["rwkv_k_k"],
     g["rwkv_k_a"]) = ew_backward(fn_rwkv_prepare, "rwkv_prepare_bwd", bsz, n_t, prep_ins(),
                                  [tuple(map(Tiled, c)) if isinstance(c, tuple) else Tiled(c) for c in prep_cts],
                                  [True] * 13 + [False])
    g["rwkv_w0"] = jnp.concatenate([dw0_f, dw0_b], axis=0)
    g["rwkv_a0"] = jnp.concatenate([da0_f, da0_b], axis=0)
    g["rwkv_w_up"] = jnp.stack([dwup_f[:DECAY_LORA], dwup_b[:DECAY_LORA]])
    g["rwkv_a_up"] = jnp.stack([daup_f[DECAY_LORA:DECAY_LORA + AAA_LORA], daup_b[DECAY_LORA:DECAY_LORA + AAA_LORA]])
    g["rwkv_g_up"] = dgup[DECAY_LORA + AAA_LORA:]
    dp_rw, g["rwkv_shift_mu"] = token_shift_bwd([(dr_f, dr_b, dr_m), (dks,), (dv_f, dv_b, dv_m), (dlora,)], px_rw,
                                                 wt["rwkv_shift_mu"], seg_first, seg_last)
    dpx = jnp.concatenate(dqkv + [dg_ret.astype(MXU_DTYPE), dp_rw], axis=-1)
    g["w_in"] = matmul(flat(n1), flat(dpx), "tn", "proj_in_dw", MXU_DTYPE)
    after_start = None
    if last_grads is not None:
        token = last_grads(g.pop("w_in"), {n: g.pop(n) for n in LAST_SHARDED})
        after_start = jnp.zeros((1, D_MODEL), f32) + token[:1, :1]
    dn1 = matmul(flat(dpx), wt["w_in"], "nt", "proj_in_dx", bias=after_start).reshape(bsz, t_all, D_MODEL)
    dh, dshift1, dscale1, g["norm1_g"] = ew_backward(fn_norm_mod, "norm1_bwd", bsz, n_t, norm1_ins(), [Tiled(dn1)],
                                                     [True] * 4)
    grad_x = dh[:, t_c:] + dx_res
    zeros = jnp.zeros((D_MODEL,), f32)
    g["mod_x"] = jnp.stack([dshift1[:, 1, 0], dscale1[:, 1, 0], dgate1[:, 0, 0], dshift2[:, 0, 0], dscale2[:, 0, 0],
                            dgate2[:, 0, 0]], axis=1)
    g["mod_ctx"] = jnp.stack([dshift1[:, 0, 0].sum(0), dscale1[:, 0, 0].sum(0), zeros, zeros, zeros, zeros])
    return loss, grad_x, g


MESH_ID = pl.DeviceIdType.MESH
ALL_PEERS = [(dx, dy, dc) for dx in (0, 1) for dy in (0, 1) for dc in (0, 1)][1:]
CHIP_PEERS = [(1, 0, 0), (0, 1, 0), (1, 1, 0)]
CHIP_SLOTS = (0, 2, 4, 6)


def _mesh_pos():
    return lax.axis_index("x"), lax.axis_index("y"), lax.axis_index("c")


def _device_slot():
    x, y, c = _mesh_pos()
    return 4 * x + 2 * y + c


def sibling_swap(arrs, name):
    n = len(arrs)

    def body(*refs):
        in_refs, out_refs = refs[:n], refs[n:2 * n]
        send_sems, recv_sems = refs[2 * n:]
        x, y, c = _mesh_pos()
        copies = [pltpu.make_async_remote_copy(src_ref=in_refs[a], dst_ref=out_refs[a], send_sem=send_sems.at[a],
                                               recv_sem=recv_sems.at[a], device_id=(x, y, 1 - c),
                                               device_id_type=MESH_ID) for a in range(n)]
        for cp in copies:
            cp.start()
        for cp in copies:
            cp.wait()

    any_spec = pl.BlockSpec(memory_space=pl.ANY)
    res = pl.pallas_call(
        body, in_specs=[any_spec] * n, out_specs=[any_spec] * n,
        out_shape=[jax.ShapeDtypeStruct(a.shape, a.dtype) for a in arrs],
        scratch_shapes=[pltpu.SemaphoreType.DMA((n,)), pltpu.SemaphoreType.DMA((n,))],
        name=name)(*arrs)
    return list(res)


def exchange(arrs, gather, peers, name, by_chip=False, own=True):
    n, n_peers = len(arrs), len(peers)
    n_slots = N_SHARDS if by_chip else N_DEV
    slot = (lambda x, y, c: 2 * x + y) if by_chip else (lambda x, y, c: 4 * x + 2 * y + c)

    def body(*refs):
        in_refs, out_refs = refs[:n], refs[n:2 * n]
        send_sems, recv_sems, local_sems = refs[2 * n:]
        x, y, c = _mesh_pos()
        me = slot(x, y, c)
        copies, locals_ = [], []
        for a in range(n):
            if own:
                mine = in_refs[a] if gather else in_refs[a].at[me]
                loc = pltpu.make_async_copy(mine, out_refs[a].at[me], local_sems.at[a])
                loc.start()
                locals_.append(loc)
            for k, (dx, dy, dc) in enumerate(peers):
                peer = (1 - x if dx else x, 1 - y if dy else y, 1 - c if dc else c)
                src = in_refs[a] if gather else in_refs[a].at[slot(*peer)]
                sem = a * n_peers + k
                cp = pltpu.make_async_remote_copy(src_ref=src, dst_ref=out_refs[a].at[me], send_sem=send_sems.at[sem],
                                                  recv_sem=recv_sems.at[sem], device_id=peer, device_id_type=MESH_ID)
                cp.start()
                copies.append(cp)
        for cp in copies:
            cp.wait()
        for loc in locals_:
            loc.wait()

    any_spec = pl.BlockSpec(memory_space=pl.ANY)
    out_shape = [jax.ShapeDtypeStruct((n_slots,) + (a.shape if gather else a.shape[1:]), a.dtype) for a in arrs]
    n_sems = n * n_peers
    res = pl.pallas_call(
        body, in_specs=[any_spec] * n, out_specs=[any_spec] * n, out_shape=out_shape,
        scratch_shapes=[pltpu.SemaphoreType.DMA((n_sems,)), pltpu.SemaphoreType.DMA((n_sems,)),
                        pltpu.SemaphoreType.DMA((n,))],
        name=name)(*arrs)
    return list(res)


HBM_SPEC = pl.BlockSpec(memory_space=pltpu.HBM)
SEM_SPEC = pl.BlockSpec(memory_space=pltpu.SEMAPHORE)
DATAFLOW = pltpu.SideEffectType.DATAFLOW_SIDE_EFFECTING


def _peer_copies(src_refs, land_refs, send_sems, recv_sems, gather):
    x, y, c = _mesh_pos()
    me = 4 * x + 2 * y + c
    copies = []
    for a, (src_ref, land_ref) in enumerate(zip(src_refs, land_refs)):
        for k, (dx, dy, dc) in enumerate(ALL_PEERS):
            peer = (1 - x if dx else x, 1 - y if dy else y, 1 - c if dc else c)
            src = src_ref if gather else src_ref.at[4 * peer[0] + 2 * peer[1] + peer[2]]
            sem = a * len(ALL_PEERS) + k
            copies.append(pltpu.make_async_remote_copy(src_ref=src, dst_ref=land_ref.at[me], send_sem=send_sems.at[sem],
                                                       recv_sem=recv_sems.at[sem], device_id=peer,
                                                       device_id_type=MESH_ID))
    return copies


def exchange_start(arrs, gather, name):
    n = len(arrs)
    lands = [lax.empty((N_DEV,) + (a.shape if gather else a.shape[1:]), a.dtype) for a in arrs]

    def body(*refs):
        for cp in _peer_copies(refs[:n], refs[n:2 * n], refs[2 * n], refs[2 * n + 1], gather):
            cp.start()
        refs[-1][...] = jnp.zeros_like(refs[-1])

    sems = pltpu.SemaphoreType.DMA((n * len(ALL_PEERS),))
    hbm = [pltpu.HBM(a.shape, a.dtype) for a in arrs + lands]
    res = pl.pallas_call(
        body, name=name, out_shape=(sems, sems, *hbm, jax.ShapeDtypeStruct((8, 128), f32)),
        in_specs=[HBM_SPEC] * (2 * n),
        out_specs=(SEM_SPEC, SEM_SPEC, *[HBM_SPEC] * (2 * n), pl.BlockSpec(memory_space=pltpu.VMEM)),
        input_output_aliases={i: 2 + i for i in range(2 * n)},
        compiler_params=pltpu.CompilerParams(has_side_effects=DATAFLOW))(
        *[pltpu.with_memory_space_constraint(a, pltpu.HBM) for a in arrs + lands])
    return res[0], res[1], list(res[2:2 + n]), list(res[2 + n:2 + 2 * n]), res[-1]


def exchange_wait(started, after, gather, name):
    send_sems, recv_sems, srcs, lands, _ = started
    n = len(srcs)

    def body(*refs):
        for cp in _peer_copies(refs[:n], refs[n:2 * n], refs[2 * n], refs[2 * n + 1], gather):
            cp.wait_send()
            cp.wait_recv()

    res = pl.pallas_call(
        body, name=name, out_shape=tuple(pltpu.HBM(a.shape, a.dtype) for a in srcs + lands),
        in_specs=[HBM_SPEC] * (2 * n) + [SEM_SPEC, SEM_SPEC, pl.BlockSpec(memory_space=pl.ANY)],
        out_specs=tuple([HBM_SPEC] * (2 * n)), input_output_aliases={i: i for i in range(2 * n)},
        compiler_params=pltpu.CompilerParams(has_side_effects=DATAFLOW))(*srcs, *lands, send_sems, recv_sems, after)
    return list(res[n:])


def gather_two_level(arrs, name):
    n = len(arrs)
    per = 7

    def body(*refs):
        in_refs, out_refs = refs[:n], refs[n:2 * n]
        send_sems, recv_sems = refs[2 * n:]
        x, y, c = _mesh_pos()
        me, sibling = (x, y, c), (x, y, 1 - c)
        chips = [(1 - x, y), (x, 1 - y), (1 - x, 1 - y)]

        def copy(a, k, block, to, src=None):
            rows = out_refs[a].at[4 * block[0] + 2 * block[1] + block[2]]
            return pltpu.make_async_remote_copy(src_ref=rows if src is None else src, dst_ref=rows,
                                                send_sem=send_sems.at[a * per + k], recv_sem=recv_sems.at[a * per + k],
                                                device_id=to, device_id_type=MESH_ID)

        first, passed = [], []
        for a in range(n):
            first.append(copy(a, 0, me, sibling, src=in_refs[a]))
            first += [copy(a, 1 + j, me, (*chip, c), src=in_refs[a]) for j, chip in enumerate(chips)]
        for cp in first:
            cp.start()
        for a in range(n):
            for j, chip in enumerate(chips):
                copy(a, 1 + j, (*chip, c), me).wait_recv()
                fwd = copy(a, 4 + j, (*chip, c), sibling)
                fwd.start()
                passed.append(fwd)
        for a in range(n):
            copy(a, 0, sibling, me).wait_recv()
            for j, chip in enumerate(chips):
                copy(a, 4 + j, (*chip, 1 - c), me).wait_recv()
        for cp in first + passed:
            cp.wait_send()

    any_spec = pl.BlockSpec(memory_space=pl.ANY)
    res = pl.pallas_call(
        body, in_specs=[any_spec] * n, out_specs=[any_spec] * n,
        out_shape=[jax.ShapeDtypeStruct((N_DEV,) + a.shape, a.dtype) for a in arrs],
        scratch_shapes=[pltpu.SemaphoreType.DMA((n * per,)), pltpu.SemaphoreType.DMA((n * per,))],
        name=name)(*arrs)
    return list(res)


def sum_slots(parts, slots, name):
    _, r, c = parts.shape
    tr = r
    for cand in (512, 256, 128, 64, 32, 16, 8):
        if r % cand == 0 and cand * c * 4 * len(slots) <= 8 * 1024 * 1024:
            tr = cand
            break

    def body(p_ref, o_ref):
        acc = p_ref[slots[0]].astype(f32)
        for s in slots[1:]:
            acc = acc + p_ref[s].astype(f32)
        o_ref[...] = acc

    return pl.pallas_call(body, grid=(r // tr,), in_specs=[pl.BlockSpec((parts.shape[0], tr, c), lambda i: (0, i, 0))],
                          out_specs=pl.BlockSpec((tr, c), lambda i: (i, 0)),
                          out_shape=jax.ShapeDtypeStruct((r, c), f32),
                          compiler_params=_cparams(("parallel",)), name=name)(parts)


def column_sum(a, name):
    def body(a_ref, o_ref):
        o_ref[...] = jnp.sum(a_ref[...], axis=0, keepdims=True)

    return pl.pallas_call(body, out_shape=jax.ShapeDtypeStruct((1, a.shape[1]), f32), name=name)(a)


def adamw(w, g, m, v, name):
    r, c = w.shape
    tr = r
    for cand in (256, 128, 64, 32, 16, 8):
        if r % cand == 0:
            tr = cand
            break

    def body(w_ref, g_ref, m_ref, v_ref, d_ref, mo_ref, vo_ref):
        gv = g_ref[...]
        m_new = ADAM_B1 * m_ref[...] + (1.0 - ADAM_B1) * gv
        v_new = ADAM_B2 * v_ref[...] + (1.0 - ADAM_B2) * jnp.square(gv)
        m_hat = m_new / (1.0 - ADAM_B1 ** ADAM_STEP)
        v_hat = v_new / (1.0 - ADAM_B2 ** ADAM_STEP)
        d_ref[...] = -ADAM_LR * (m_hat / (jnp.sqrt(v_hat) + ADAM_EPS) + ADAM_WD * w_ref[...])
        mo_ref[...] = m_new
        vo_ref[...] = v_new

    spec = pl.BlockSpec((tr, c), lambda i: (i, 0))
    return pl.pallas_call(body, grid=(r // tr,), in_specs=[spec] * 4, out_specs=[spec] * 3,
                          out_shape=[jax.ShapeDtypeStruct((r, c), f32)] * 3,
                          compiler_params=_cparams(("parallel",)), name=name)(w, g, m, v)


def adaln_fwd(c_rows, w, b):
    def body(c_ref, w_ref, b_ref, o_ref):
        cv = c_ref[...]
        o_ref[...] = _mxu_dot(cv * jax.nn.sigmoid(cv), w_ref[...]) + b_ref[...]

    return pl.pallas_call(body, out_shape=jax.ShapeDtypeStruct((c_rows.shape[0], w.shape[1]), f32),
                          compiler_params=pltpu.CompilerParams(vmem_limit_bytes=VMEM_LIMIT), name="adaln_fwd")(c_rows, w, b)


def adaln_bwd(c_rows, dm, w):
    def body(c_ref, dm_ref, w_ref, gw_ref, ds_ref):
        cv = c_ref[...]
        gw_ref[...] = _dg(cv * jax.nn.sigmoid(cv), dm_ref[...], 0, 0)
        ds_ref[...] = _dg(dm_ref[...], w_ref[...], 1, 1)

    return pl.pallas_call(body, out_shape=[jax.ShapeDtypeStruct(w.shape, f32),
                                           jax.ShapeDtypeStruct(c_rows.shape, f32)],
                          compiler_params=pltpu.CompilerParams(vmem_limit_bytes=VMEM_LIMIT), name="adaln_bwd")(c_rows, dm, w)


def c_ctx_grad(parts, c_ctx_row):
    def body(p_ref, c_ref, o_ref):
        total = p_ref[0, 0:1, :]
        for s in range(1, N_SHARDS):
            total = total + p_ref[s, 0:1, :]
        _, vjp = jax.vjp(jax.nn.silu, c_ref[...])
        o_ref[...] = vjp(total)[0]

    return pl.pallas_call(body, out_shape=jax.ShapeDtypeStruct((1, D_MODEL), f32), name="c_ctx_grad")(parts, c_ctx_row)


PACK_W = 1024
PACK_ROWS = 8


def _pack(arrs):
    pieces, layout, r0 = [], [], 0
    for a in arrs:
        size = math.prod(a.shape)
        rows = -(-size // (PACK_W * PACK_ROWS)) * PACK_ROWS
        pieces.append(jnp.pad(a.reshape(-1).astype(f32), (0, rows * PACK_W - size)).reshape(rows, PACK_W))
        layout.append((r0, rows, a.shape))
        r0 += rows
    return jnp.concatenate(pieces, axis=0), layout


def _unpack(pack, layout, lead=()):
    n_lead = len(lead)
    outs = []
    for r0, rows, shape in layout:
        piece = pack[(slice(None),) * n_lead + (slice(r0, r0 + rows),)].reshape(lead + (-1,))
        outs.append(piece[..., :math.prod(shape)].reshape(lead + tuple(shape)))
    return outs


W_NAMES = ("c_ctx", "w_ada", "b_ada", "norm1_g", "norm2_g", "w_in", "ret_log_decay", "rwkv_shift_mu", "rwkv_w0",
           "rwkv_w_up", "rwkv_a0", "rwkv_a_up", "rwkv_g_up", "rwkv_k_k", "rwkv_k_a", "rwkv_r_k", "rwkv_ln_w",
           "rwkv_ln_b", "w_out", "w_ff1", "b_ff1", "w_ff2", "b_ff2", "final_g")
COL_SHARDED = ("w_in", "w_ff1")
ROW_SHARDED = ("w_out", "w_ff2")
LAST_SHARDED = ("rwkv_shift_mu", "rwkv_w0", "rwkv_w_up", "rwkv_a0", "rwkv_a_up", "rwkv_g_up")
REPLICATED = ("c_ctx", "b_ada", "norm1_g", "norm2_g", "ret_log_decay", "rwkv_k_k", "rwkv_k_a", "rwkv_r_k",
              "rwkv_ln_w", "rwkv_ln_b", "b_ff1", "b_ff2", "final_g")
N_SHARDS = 4


def _train_step(a):
    x, c, ctx, tgt = a["x"], a["c"], a["ctx"], a["loss_target"]
    bsz = x.shape[0]
    mx, my, mc = _mesh_pos()
    shard = 2 * mx + my
    dev = _device_slot()

    (c_all,) = exchange([jnp.pad(c, ((0, PACK_ROWS - bsz), (0, 0)))], True, ALL_PEERS, "gather_c")
    n_ex = N_DEV * bsz
    c_rows = jnp.concatenate([c_all[:, :bsz].reshape(n_ex, D_MODEL), a["c_ctx"][None, :],
                              jnp.zeros((PACK_ROWS - 1, D_MODEL), f32)], axis=0)
    ada_cols = a["w_ada"].shape[-1]
    b_ada_cols = lax.dynamic_slice_in_dim(a["b_ada"], shard * ada_cols, ada_cols, axis=1)
    mod_cols = adaln_fwd(c_rows, a["w_ada"][0], b_ada_cols)

    def own_half(n):
        w = a[n][0].astype(MXU_DTYPE)
        return lax.dynamic_slice_in_dim(w, mc * (w.shape[0] // 2), w.shape[0] // 2, axis=0)

    def whole_weight(n, gth, own):
        per_chip = lax.dynamic_update_index_in_dim(gth, own, dev, 0).reshape(N_SHARDS, -1, gth.shape[-1])
        return (per_chip.transpose(1, 0, 2).reshape(per_chip.shape[1], -1) if n in COL_SHARDED
                else per_chip.reshape(-1, per_chip.shape[-1]))

    small_pack, small_layout = _pack([a[n][0] for n in LAST_SHARDED])
    own_blocks = [mod_cols, own_half("w_in"), small_pack]
    gathered = gather_two_level(own_blocks, "gather_weights")
    late_own = [own_half(n) for n in LATE_WEIGHTS]
    late_started = exchange_start(late_own, True, "gather_late_start")
    mod_own = lax.dynamic_update_index_in_dim(gathered[0], mod_cols, dev, 0)
    mod_all = jnp.stack([mod_own[s] for s in CHIP_SLOTS], axis=1).reshape(c_rows.shape[0], -1)
    mod_all = mod_all + late_started[-1][0, 0]
    mod_x = lax.dynamic_slice_in_dim(mod_all, dev * bsz, bsz, axis=0).reshape(bsz, 6, D_MODEL)
    mod_ctx = mod_all[n_ex].reshape(6, D_MODEL)
    wt = {"w_in": whole_weight("w_in", gathered[1], own_blocks[1])}

    def late_weights(after):
        lands = exchange_wait(late_started, after, True, "gather_late_wait")
        return {n: whole_weight(n, land, own) for n, land, own in zip(LATE_WEIGHTS, lands, late_own)}

    def grad_blocks(n, gw):
        if n in COL_SHARDED:
            gw = gw.reshape(gw.shape[0], N_SHARDS, -1).transpose(1, 0, 2)
        return gw.reshape(N_DEV, -1, gw.shape[-1]).astype(MXU_DTYPE)

    late_sent, last_sent = {}, {}

    def early_grads(late_g):
        late_sent["blocks"] = [grad_blocks(n, late_g[n]) for n in LATE_WEIGHTS]
        late_sent["started"] = exchange_start(late_sent["blocks"], False, "scatter_late_start")
        return late_sent["started"][-1]

    def last_grads(g_w_in, g_small):
        shard_packs = []
        for s in range(N_SHARDS):
            pieces_s = [lax.slice_in_dim(g_small[n], s * a[n].shape[-1], (s + 1) * a[n].shape[-1],
                                         axis=g_small[n].ndim - 1) for n in LAST_SHARDED]
            pack_s, last_sent["layout"] = _pack(pieces_s)
            shard_packs.append(jnp.pad(pack_s, ((0, -pack_s.shape[0] % (2 * PACK_ROWS)), (0, 0))))
        last_sent["blocks"] = [grad_blocks("w_in", g_w_in), jnp.stack(shard_packs).reshape(N_DEV, -1, PACK_W)]
        last_sent["started"] = exchange_start(last_sent["blocks"], False, "scatter_last_start")
        return last_sent["started"][-1]

    small_own = lax.dynamic_update_index_in_dim(gathered[2], small_pack, dev, 0)
    small_by_chip = _unpack(jnp.stack([small_own[s] for s in CHIP_SLOTS]), small_layout, (N_SHARDS,))
    for n, parts in zip(LAST_SHARDED, small_by_chip):
        wt[n] = jnp.concatenate([parts[s] for s in range(N_SHARDS)], axis=-1)
    for n in ("norm1_g", "norm2_g", "rwkv_k_k", "rwkv_k_a", "rwkv_r_k", "rwkv_ln_w", "rwkv_ln_b", "b_ff1", "b_ff2"):
        wt[n] = a[n]
    wt["ret_log_decay"] = a["ret_log_decay"][0]
    wt["final_g"] = a["final_g"][None, :]

    loss, grad_x, g = layer_step(x, ctx, tgt, mod_x, mod_ctx, wt, late_weights, early_grads, last_grads)

    small_names = [n for n in REPLICATED if n not in ("c_ctx", "b_ada")]
    g_pack, g_layout = _pack([jnp.pad(loss, ((0, 0), (0, PACK_W - loss.shape[1])))] + [g[n] for n in small_names]
                             + [g["mod_x"], g["mod_ctx"]])
    (g_packs,) = gather_two_level([g_pack], "gather_small_grads")
    g_packs = lax.dynamic_update_index_in_dim(g_packs, g_pack, dev, 0)
    g_sum = _unpack(sum_slots(g_packs, tuple(range(N_DEV)), "sum_small_grads"), g_layout)
    loss_total = g_sum[0][0, 0]
    grads = dict(zip(small_names, g_sum[1:1 + len(small_names)]))
    dmod_ctx = g_sum[-1].reshape(1, -1)
    dmod_x = _unpack(g_packs, g_layout, (N_DEV,))[-2].reshape(n_ex, -1)
    dmod = jnp.concatenate([dmod_x, dmod_ctx, jnp.zeros((PACK_ROWS - 1, dmod_x.shape[1]), f32)], axis=0)
    grads["b_ada"] = column_sum(dmod, "b_ada_grad")
    dmod_cols = lax.dynamic_slice_in_dim(dmod, shard * ada_cols, ada_cols, axis=1)
    grads["w_ada"], dsilu = adaln_bwd(c_rows, dmod_cols, a["w_ada"][0])

    dsilu_rows = jnp.broadcast_to(jnp.pad(dsilu[n_ex:n_ex + 1], ((0, PACK_ROWS - 1), (0, 0)))[None],
                                  (N_SHARDS, PACK_ROWS, D_MODEL))
    (shares,) = exchange([dsilu_rows], False, CHIP_PEERS, "share_c_ctx_grad", by_chip=True, own=False)
    shares = lax.dynamic_update_index_in_dim(shares, dsilu_rows[0], shard, 0)
    grads["c_ctx"] = c_ctx_grad(shares, a["c_ctx"][None, :])

    scattered, half_sums = ("w_in", "small_shards") + LATE_WEIGHTS, []
    for sent, wait_name, after in ((last_sent, "scatter_last_wait", grads["c_ctx"]),
                                   (late_sent, "scatter_late_wait", grads["c_ctx"])):
        for land, block in zip(exchange_wait(sent["started"], after, False, wait_name), sent["blocks"]):
            land = lax.dynamic_update_index_in_dim(land, lax.dynamic_index_in_dim(block, dev, 0, keepdims=False), dev, 0)
            half_sums.append(sum_slots(land, tuple(range(N_DEV)), f"sum_{scattered[len(half_sums)]}"))
    other_halves = sibling_swap(half_sums, "swap_halves")
    for n, mine, other in zip(scattered, half_sums, other_halves):
        rows = mine.shape[0]
        whole = jnp.zeros((2 * rows, mine.shape[1]), f32)
        whole = lax.dynamic_update_slice_in_dim(whole, mine, mc * rows, axis=0)
        grads[n] = lax.dynamic_update_slice_in_dim(whole, other, (1 - mc) * rows, axis=0)
    grads.update(zip(LAST_SHARDED, _unpack(grads.pop("small_shards"), last_sent["layout"])))

    out_g, out_d, out_m, out_v = {}, {}, {}, {}
    for n in ("w_ada",) + COL_SHARDED + ROW_SHARDED:
        out_g[n] = grads[n].reshape(a[n].shape)
        two_d = lambda z: z.reshape(-1, z.shape[-1])
        d, m, v = adamw(two_d(a[n]), two_d(out_g[n]), two_d(a["m_" + n]), two_d(a["v_" + n]), f"adamw_{n}")
        out_d[n], out_m[n], out_v[n] = d.reshape(a[n].shape), m.reshape(a[n].shape), v.reshape(a[n].shape)
    rest = REPLICATED + LAST_SHARDED
    for n in rest:
        out_g[n] = grads[n].reshape(a[n].shape)
    packs = [_pack([src[n] for n in rest])[0] for src in
             ({n: a[n] for n in rest}, out_g, {n: a["m_" + n] for n in rest}, {n: a["v_" + n] for n in rest})]
    _, rest_layout = _pack([a[n] for n in rest])
    for dst, pack in zip((out_d, out_m, out_v), adamw(*packs, "adamw_small")):
        dst.update(zip(rest, _unpack(pack, rest_layout)))
    return (loss_total, grad_x, *[out_g[n] for n in W_NAMES], *[out_d[n] for n in W_NAMES],
            *[out_m[n] for n in W_NAMES], *[out_v[n] for n in W_NAMES])


def kernel(x, c, ctx, c_ctx, w_ada, b_ada, norm1_g, norm2_g, w_in, ret_log_decay, rwkv_shift_mu, rwkv_w0, rwkv_w_up, rwkv_a0, rwkv_a_up, rwkv_g_up, rwkv_k_k, rwkv_k_a, rwkv_r_k, rwkv_ln_w, rwkv_ln_b, w_out, w_ff1, b_ff1, w_ff2, b_ff2, final_g, loss_target, m_c_ctx, m_w_ada, m_b_ada, m_norm1_g, m_norm2_g, m_w_in, m_ret_log_decay, m_rwkv_shift_mu, m_rwkv_w0, m_rwkv_w_up, m_rwkv_a0, m_rwkv_a_up, m_rwkv_g_up, m_rwkv_k_k, m_rwkv_k_a, m_rwkv_r_k, m_rwkv_ln_w, m_rwkv_ln_b, m_w_out, m_w_ff1, m_b_ff1, m_w_ff2, m_b_ff2, m_final_g, v_c_ctx, v_w_ada, v_b_ada, v_norm1_g, v_norm2_g, v_w_in, v_ret_log_decay, v_rwkv_shift_mu, v_rwkv_w0, v_rwkv_w_up, v_rwkv_a0, v_rwkv_a_up, v_rwkv_g_up, v_rwkv_k_k, v_rwkv_k_a, v_rwkv_r_k, v_rwkv_ln_w, v_rwkv_ln_b, v_w_out, v_w_ff1, v_b_ff1, v_w_ff2, v_b_ff2, v_final_g):
    return _train_step(dict(locals()))
```

```python
import functools
import math

import jax
import jax.numpy as jnp
from jax import lax
from jax.experimental import pallas as pl
from jax.experimental.pallas import tpu as pltpu

f32 = jnp.float32
MXU_DTYPE = jnp.bfloat16

D_MODEL = 1024
RET_W = 512
RET_HEADS = 4
RET_DH = 128
RET_CHUNK = 128
RW_W = 512
RW_N = 64
DECAY_LORA = 64
AAA_LORA = 64
GATE_LORA = 128
LORA_W = DECAY_LORA + AAA_LORA + GATE_LORA
D_FF = 4096
RET_COLS = 4 * RET_W
SHIFT_COLS = 3 * RW_W + LORA_W
IN_COLS = RET_COLS + SHIFT_COLS
GRID_W = 64
ROPE_BASE = 10000.0
NORM_EPS = 1e-6
GN_EPS = 64e-5
W_DECAY_SCALE = math.exp(-0.5)
ADAM_LR, ADAM_B1, ADAM_B2, ADAM_EPS, ADAM_WD, ADAM_STEP = 0.001, 0.9, 0.999, 1e-08, 0.01, 10

TOK_TILE = 256
MATMUL_TILE = 1024
SCAN_CHUNK = 32
SCAN_UNROLL = SCAN_CHUNK
N_DEV = 8
V7X_VMEM_BYTES = 64 * 1024 * 1024
VMEM_LIMIT = V7X_VMEM_BYTES * 7 // 8


def _cparams(sem):
    return pltpu.CompilerParams(dimension_semantics=sem, vmem_limit_bytes=VMEM_LIMIT)


def _tile(n, cap):
    best = None
    for t in range(128, min(n, cap) + 1, 128):
        if n % t == 0:
            best = t
    return best if best is not None else n


def matmul(a, b, mode, name, out_dtype=f32, bias=None, finish=None):
    if mode == "nn":
        (m, k), (k2, n) = a.shape, b.shape
    elif mode == "nt":
        (m, k), (n, k2) = a.shape, b.shape
    else:
        (k, m), (k2, n) = a.shape, b.shape
    assert k == k2, (a.shape, b.shape, mode)
    tm, tn, tk = _tile(m, MATMUL_TILE), _tile(n, MATMUL_TILE), _tile(k, MATMUL_TILE)
    nk = k // tk
    dims = {"nn": ((1,), (0,)), "nt": ((1,), (1,)), "tn": ((0,), (0,))}[mode]

    def body(a_ref, b_ref, *rest):
        o_ref, acc_ref = rest[-2:]
        kk = pl.program_id(2)

        @pl.when(kk == 0)
        def _():
            acc_ref[...] = jnp.zeros_like(acc_ref)

        acc_ref[...] += lax.dot_general(a_ref[...].astype(MXU_DTYPE), b_ref[...].astype(MXU_DTYPE),
                                        (dims, ((), ())), preferred_element_type=f32)

        @pl.when(kk == nk - 1)
        def _():
            res = acc_ref[...]
            if bias is not None:
                res = res + rest[0][...]
            if finish is not None:
                res = finish(res)
            o_ref[...] = res.astype(o_ref.dtype)

    if mode == "nn":
        a_spec = pl.BlockSpec((tm, tk), lambda i, j, q: (i, q))
        b_spec = pl.BlockSpec((tk, tn), lambda i, j, q: (q, j))
    elif mode == "nt":
        a_spec = pl.BlockSpec((tm, tk), lambda i, j, q: (i, q))
        b_spec = pl.BlockSpec((tn, tk), lambda i, j, q: (j, q))
    else:
        a_spec = pl.BlockSpec((tk, tm), lambda i, j, q: (q, i))
        b_spec = pl.BlockSpec((tk, tn), lambda i, j, q: (q, j))
    extra_specs = [] if bias is None else [pl.BlockSpec((1, tn), lambda i, j, q: (0, j))]
    extra = [] if bias is None else [bias]
    return pl.pallas_call(
        body, grid=(m // tm, n // tn, nk), in_specs=[a_spec, b_spec] + extra_specs,
        out_specs=pl.BlockSpec((tm, tn), lambda i, j, q: (i, j)),
        out_shape=jax.ShapeDtypeStruct((m, n), out_dtype),
        scratch_shapes=[pltpu.VMEM((tm, tn), f32)],
        compiler_params=_cparams(("parallel", "parallel", "arbitrary")), name=name)(a, b, *extra)


class Tiled:
    def __init__(self, arr, w=None, cidx=0, toff=0):
        self.arr, self.w, self.cidx, self.toff = arr, (arr.shape[-1] if w is None else w), cidx, toff

    def spec(self):
        cidx, toff = self.cidx, self.toff
        return pl.BlockSpec((None, TOK_TILE, self.w), lambda b, i: (b, jnp.maximum(i + toff, 0), cidx))


class Seg:
    def __init__(self, arr, seg, first):
        self.arr, self.seg, self.first = arr, seg, first

    def spec(self):
        seg = self.seg
        return pl.BlockSpec((None, None, 1, self.arr.shape[-1]), lambda b, i: (b, seg(i), 0, 0))


class Glob:
    def __init__(self, arr):
        self.arr = arr

    def spec(self):
        return pl.BlockSpec(self.arr.shape, lambda b, i: (0,) * self.arr.ndim)


def ew_forward(fn, name, bsz, n_tiles, ins, outs):
    n_in = len(ins)

    def body(*refs):
        res = fn(*[r[...] for r in refs[:n_in]])
        for o_ref, o in zip(refs[n_in:], res):
            o_ref[...] = o.astype(o_ref.dtype)

    out_specs = [pl.BlockSpec((None, TOK_TILE, w), lambda b, i: (b, i, 0)) for w, _ in outs]
    out_shape = [jax.ShapeDtypeStruct((bsz, n_tiles * TOK_TILE, w), dt) for w, dt in outs]
    return pl.pallas_call(body, grid=(bsz, n_tiles), in_specs=[d.spec() for d in ins], out_specs=out_specs,
                          out_shape=out_shape, compiler_params=_cparams(("parallel", "parallel")), name=name)(
        *[d.arr for d in ins])


def ew_backward(fn, name, bsz, n_tiles, ins, cts, want, grad_dtypes=None, lead=0):
    ct_parts = [c if isinstance(c, tuple) else (c,) for c in cts]
    cts = [part for parts in ct_parts for part in parts]
    n_in, n_ct = len(ins), len(cts)
    diff = [k for k in range(n_in) if want[k]]
    grad_dtypes = grad_dtypes or {}
    assert lead == 0 or not any(isinstance(ins[k], Seg) for k in diff)

    def body(*refs):
        b, i = pl.program_id(0), pl.program_id(1)
        g_refs = refs[n_in + n_ct:]

        def tile_grads():
            vals = [r[...] for r in refs[:n_in]]
            ct_refs = iter(refs[n_in:n_in + n_ct])
            ct_vals = tuple(functools.reduce(lambda s, t: s + t, [next(ct_refs)[...].astype(f32) for _ in parts])
                            for parts in ct_parts)

            def f(*dvals):
                full = list(vals)
                for k, v in zip(diff, dvals):
                    full[k] = v
                return tuple(fn(*full))

            _, vjp = jax.vjp(f, *[vals[k] for k in diff])
            grads = vjp(ct_vals)
            for k, g_ref, g in zip(diff, g_refs, grads):
                d = ins[k]
                if isinstance(d, Tiled):
                    g_ref[...] = g.astype(g_ref.dtype)
                else:
                    zero = d.first(i) if isinstance(d, Seg) else jnp.logical_and(b == 0, i == lead)

                    @pl.when(zero)
                    def _(g_ref=g_ref):
                        g_ref[...] = jnp.zeros_like(g_ref)

                    g_ref[...] += g

        if lead == 0:
            tile_grads()
        else:
            pl.when(i >= lead)(tile_grads)

            @pl.when(i < lead)
            def _():
                for k, g_ref in zip(diff, g_refs):
                    if isinstance(ins[k], Tiled):
                        g_ref[...] = jnp.zeros_like(g_ref)

    out_specs, out_shape = [], []
    for k in diff:
        d = ins[k]
        if isinstance(d, Tiled):
            out_specs.append(pl.BlockSpec((None, TOK_TILE, d.w), lambda b, i: (b, i, 0)))
            out_shape.append(jax.ShapeDtypeStruct((bsz, (n_tiles + lead) * TOK_TILE, d.w), grad_dtypes.get(k, f32)))
        else:
            out_specs.append(d.spec())
            out_shape.append(jax.ShapeDtypeStruct(d.arr.shape, f32))
    return pl.pallas_call(body, grid=(bsz, n_tiles + lead),
                          in_specs=[d.spec() for d in ins] + [c.spec() for c in cts],
                          out_specs=out_specs, out_shape=out_shape,
                          compiler_params=_cparams(("arbitrary", "arbitrary")), name=name)(
        *[d.arr for d in ins], *[c.arr for c in cts])


@jax.custom_vjp
def _mxu_dot(a, b):
    return jnp.dot(a.astype(MXU_DTYPE), b.astype(MXU_DTYPE), preferred_element_type=f32)


def _mxu_dot_fwd(a, b):
    return _mxu_dot(a, b), (a, b)


def _mxu_dot_bwd(res, ct):
    a, b = res
    ct = ct.astype(MXU_DTYPE)
    da = lax.dot_general(ct, b.astype(MXU_DTYPE), (((1,), (1,)), ((), ())), preferred_element_type=f32)
    db = lax.dot_general(a.astype(MXU_DTYPE), ct, (((0,), (0,)), ((), ())), preferred_element_type=f32)
    return da, db


_mxu_dot.defvjp(_mxu_dot_fwd, _mxu_dot_bwd)


def _split_dot_impl(x, ones_mat):
    hi = x.astype(MXU_DTYPE)
    lo = (x - hi.astype(f32)).astype(MXU_DTYPE)
    return jnp.dot(hi, ones_mat, preferred_element_type=f32) + jnp.dot(lo, ones_mat, preferred_element_type=f32)


@jax.custom_vjp
def _split_dot(x, ones_mat):
    return _split_dot_impl(x, ones_mat)


def _split_dot_fwd(x, ones_mat):
    return _split_dot_impl(x, ones_mat), ones_mat


def _split_dot_bwd(ones_mat, ct):
    return _split_dot_impl(ct, ones_mat), None


_split_dot.defvjp(_split_dot_fwd, _split_dot_bwd)


def _block_ones(n, group):
    idx = jnp.arange(n) // group
    return (idx[:, None] == idx[None, :]).astype(MXU_DTYPE)


def _rms(x, g):
    return x * lax.rsqrt(jnp.mean(x * x, axis=-1, keepdims=True) + NORM_EPS) * g


def fn_norm_mod(h, shift, scale, g):
    return (_rms(h, g) * (1.0 + scale) + shift,)


def fn_rwkv_prepare(ks, lora, w0_f, w0_b, a0_f, a0_b, w_up_f, w_up_b, a_up_f, a_up_b, g_up, k_k, k_a, ones64):
    kkr = ks * k_k
    kk = kkr * lax.rsqrt(_split_dot(kkr * kkr, ones64) + 1e-12)
    outs = [kk]
    th = jnp.tanh(lora)
    for w0, a0, w_up, a_up in ((w0_f, a0_f, w_up_f, a_up_f), (w0_b, a0_b, w_up_b, a_up_b)):
        w = jnp.exp(-W_DECAY_SCALE * jax.nn.sigmoid(w0 + _mxu_dot(th, w_up)))
        a = jax.nn.sigmoid(a0 + _mxu_dot(lora, a_up))
        kt = ks * (1.0 + (a - 1.0) * k_a)
        outs += [w, a * kk, kt]
    outs.append(_mxu_dot(jax.nn.sigmoid(lora), g_up))
    return tuple(outs)


def fn_merge(o_f, o_b, g_ret, y_f, y_b, r, kt_f, v, g_rw, r_k, ln_w, ln_b, ones64, ones128):
    o = o_f + o_b
    ret = o * lax.rsqrt(_split_dot(o * o, ones128) * (1.0 / RET_DH) + NORM_EPS) * (g_ret * jax.nn.sigmoid(g_ret))
    y = y_f + y_b
    mean = _split_dot(y, ones64) * (1.0 / RW_N)
    yc = y - mean
    var = _split_dot(yc * yc, ones64) * (1.0 / RW_N)
    y_n = yc * lax.rsqrt(var + GN_EPS) * ln_w + ln_b
    bonus = _split_dot(r * kt_f * r_k, ones64) * v
    return ret, (y_n + bonus) * g_rw


def fn_resid_norm_mod(x, mix, gate, shift, scale, g):
    h1 = x + gate * mix
    return h1, _rms(h1, g) * (1.0 + scale) + shift


def relu2(z):
    return jnp.square(jnp.maximum(z, 0.0))


def relu2_backward(act, dact, name):
    bsz, n_tok, width = act.shape

    def body(a_ref, d_ref, du_ref, db_ref):
        du = d_ref[...].astype(f32) * (2.0 * jnp.sqrt(a_ref[...].astype(f32)))
        du_ref[...] = du.astype(du_ref.dtype)

        @pl.when(jnp.logical_and(pl.program_id(0) == 0, pl.program_id(1) == 0))
        def _():
            db_ref[...] = jnp.zeros_like(db_ref)

        db_ref[...] += jnp.sum(du, axis=0, keepdims=True)

    tile = pl.BlockSpec((None, TOK_TILE, width), lambda b, i: (b, i, 0))
    row = pl.BlockSpec((1, width), lambda b, i: (0, 0))
    return pl.pallas_call(body, grid=(bsz, n_tok // TOK_TILE), in_specs=[tile, tile], out_specs=[tile, row],
                          out_shape=[jax.ShapeDtypeStruct(act.shape, MXU_DTYPE), jax.ShapeDtypeStruct((1, width), f32)],
                          compiler_params=_cparams(("arbitrary", "arbitrary")), name=name)(act, dact)


def fn_loss(h1, f, tgt, gate, b2, g):
    y = _rms(h1 + gate * (f + b2), g)
    err = jnp.square(y - tgt)
    return 0.5 * jnp.sum(jnp.mean(err, axis=-1, keepdims=True), axis=0, keepdims=True)


def loss_and_grads(h1, f, tgt, gate, b2, g, bsz, n_tiles):
    def body(h1_ref, f_ref, t_ref, gate_ref, b2_ref, g_ref, loss_ref, dh1_ref, df_ref, dgate_ref, db2_ref, dg_ref):
        b, i = pl.program_id(0), pl.program_id(1)
        tgt_v = t_ref[...]
        loss, vjp = jax.vjp(lambda a, c, e, p, q: fn_loss(a, c, tgt_v, e, p, q),
                            h1_ref[...], f_ref[...], gate_ref[...], b2_ref[...], g_ref[...])
        dh1, df, dgate, db2, dg = vjp(jnp.ones((1, 1), f32))
        dh1_ref[...] = dh1
        df_ref[...] = df.astype(df_ref.dtype)

        @pl.when(i == 0)
        def _():
            dgate_ref[...] = jnp.zeros_like(dgate_ref)

        @pl.when(jnp.logical_and(b == 0, i == 0))
        def _():
            loss_ref[...] = jnp.zeros_like(loss_ref)
            db2_ref[...] = jnp.zeros_like(db2_ref)
            dg_ref[...] = jnp.zeros_like(dg_ref)

        dgate_ref[...] += dgate
        db2_ref[...] += db2
        dg_ref[...] += dg
        loss_ref[...] += jnp.broadcast_to(loss, loss_ref.shape)

    tile = pl.BlockSpec((None, TOK_TILE, D_MODEL), lambda b, i: (b, i, 0))
    row = pl.BlockSpec((1, D_MODEL), lambda b, i: (0, 0))
    seg = pl.BlockSpec((None, None, 1, D_MODEL), lambda b, i: (b, 0, 0, 0))
    t_tok = n_tiles * TOK_TILE
    return pl.pallas_call(
        body, grid=(bsz, n_tiles), in_specs=[tile, tile, tile, seg, row, row],
        out_specs=[pl.BlockSpec((1, 128), lambda b, i: (0, 0)), tile, tile, seg, row, row],
        out_shape=[jax.ShapeDtypeStruct((1, 128), f32), jax.ShapeDtypeStruct((bsz, t_tok, D_MODEL), f32),
                   jax.ShapeDtypeStruct((bsz, t_tok, D_MODEL), MXU_DTYPE),
                   jax.ShapeDtypeStruct((bsz, 1, 1, D_MODEL), f32),
                   jax.ShapeDtypeStruct((1, D_MODEL), f32), jax.ShapeDtypeStruct((1, D_MODEL), f32)],
        compiler_params=_cparams(("arbitrary", "arbitrary")), name="loss_and_grads")(h1, f, tgt, gate, b2, g)


SHIFT_BLOCK = SHIFT_COLS
HALO_ROWS = 8


def _shift_specs(n_tok, col0, width=SHIFT_BLOCK):
    per_tile = TOK_TILE // HALO_ROWS
    last = n_tok // HALO_ROWS - 1
    tile = pl.BlockSpec((None, TOK_TILE, width), lambda j, b, i: (b, i, col0 + j))
    prev = pl.BlockSpec((None, HALO_ROWS, width),
                        lambda j, b, i: (b, jnp.maximum(i * per_tile - 1, 0), col0 + j))
    nxt = pl.BlockSpec((None, HALO_ROWS, width),
                       lambda j, b, i: (b, jnp.minimum((i + 1) * per_tile, last), col0 + j))
    return tile, prev, nxt


def _shift_specs_at(n_tok, col):
    one, halo, cols = pl.Element(1), pl.Element(HALO_ROWS), pl.Element(SHIFT_COLS)
    tile = pl.BlockSpec((one, pl.Element(TOK_TILE), cols),
                        lambda j, b, i: (b, pl.multiple_of(i * TOK_TILE, TOK_TILE), col))
    prev = pl.BlockSpec((one, halo, cols), lambda j, b, i: (
        b, pl.multiple_of(jnp.maximum(i * TOK_TILE - HALO_ROWS, 0), HALO_ROWS), col))
    nxt = pl.BlockSpec((one, halo, cols), lambda j, b, i: (
        b, pl.multiple_of(jnp.minimum((i + 1) * TOK_TILE, n_tok - HALO_ROWS), HALO_ROWS), col))
    return tile, prev, nxt


def _shifted(p, prev_ref, next_ref, is_first, is_last):
    row = lax.broadcasted_iota(jnp.int32, p.shape, 0)
    prev_row = jnp.where(is_first, 0.0, prev_ref[HALO_ROWS - 1:HALO_ROWS, :].astype(f32))
    next_row = jnp.where(is_last, 0.0, next_ref[0:1, :].astype(f32))
    prev = jnp.where(row == 0, prev_row, pltpu.roll(p, 1, axis=0))
    nxt = jnp.where(row == TOK_TILE - 1, next_row, pltpu.roll(p, TOK_TILE - 1, axis=0))
    return prev, nxt


def token_shift(px, mu, seg_first, seg_last):
    bsz, n_tok, _ = px.shape
    n_tiles = n_tok // TOK_TILE
    assert SHIFT_BLOCK == SHIFT_COLS

    def body(p_ref, prev_ref, next_ref, mu_ref, o_ref):
        i = pl.program_id(2)
        p = p_ref[0]
        prev, nxt = _shifted(p, prev_ref[0], next_ref[0], seg_first(i), seg_last(i))
        o_ref[...] = p + mu_ref[0:1, :] * (prev - p) + mu_ref[1:2, :] * (nxt - p)

    tile, prev, nxt = _shift_specs_at(n_tok, RET_COLS)
    return pl.pallas_call(
        body, grid=(SHIFT_COLS // SHIFT_BLOCK, bsz, n_tiles),
        in_specs=[tile, prev, nxt, pl.BlockSpec((2, SHIFT_BLOCK), lambda j, b, i: (0, j))],
        out_specs=pl.BlockSpec((None, TOK_TILE, SHIFT_BLOCK), lambda j, b, i: (b, i, j)),
        out_shape=jax.ShapeDtypeStruct((bsz, n_tok, SHIFT_COLS), f32),
        compiler_params=_cparams(("parallel", "parallel", "parallel")), name="token_shift")(px, px, px, mu)


def token_shift_bwd(d_sections, px, mu, seg_first, seg_last):
    bsz, n_tok, _ = px.shape
    n_tiles = n_tok // TOK_TILE
    d_arrs = [part for section in d_sections for part in section]
    assert SHIFT_BLOCK == SHIFT_COLS == sum(section[0].shape[-1] for section in d_sections)

    def body(*refs):
        d_refs, (p_ref, prev_ref, next_ref, mu_ref, dp_ref, dmu_ref) = refs[:3 * len(d_arrs)], refs[3 * len(d_arrs):]
        b, i = pl.program_id(1), pl.program_id(2)
        first, last = seg_first(i), seg_last(i)

        def summed(which):
            part_refs = iter(d_refs[which::3])
            return jnp.concatenate([functools.reduce(lambda s, t: s + t, [next(part_refs)[...] for _ in section])
                                    for section in d_sections], axis=-1)

        d, p = summed(0), p_ref[0]
        d_prev, d_next = _shifted(d, summed(1), summed(2), first, last)
        p_prev, p_next = _shifted(p, prev_ref[0], next_ref[0], first, last)
        mu0, mu1 = mu_ref[0:1, :], mu_ref[1:2, :]
        dp_ref[...] = (d + mu0 * (d_next - d) + mu1 * (d_prev - d)).astype(dp_ref.dtype)

        @pl.when(jnp.logical_and(b == 0, i == 0))
        def _():
            dmu_ref[...] = jnp.zeros_like(dmu_ref)

        dmu_ref[0:1, :] += jnp.sum(d * (p_prev - p), axis=0, keepdims=True)
        dmu_ref[1:2, :] += jnp.sum(d * (p_next - p), axis=0, keepdims=True)

    d_specs = [spec for arr in d_arrs for spec in _shift_specs(n_tok, 0, arr.shape[-1])]
    tile, prev, nxt = _shift_specs_at(n_tok, RET_COLS)
    mu_spec = pl.BlockSpec((2, SHIFT_BLOCK), lambda j, b, i: (0, j))
    return pl.pallas_call(
        body, grid=(SHIFT_COLS // SHIFT_BLOCK, bsz, n_tiles),
        in_specs=d_specs + [tile, prev, nxt, mu_spec],
        out_specs=[pl.BlockSpec((None, TOK_TILE, SHIFT_BLOCK), lambda j, b, i: (b, i, j)), mu_spec],
        out_shape=[jax.ShapeDtypeStruct((bsz, n_tok, SHIFT_COLS), MXU_DTYPE),
                   jax.ShapeDtypeStruct((2, SHIFT_COLS), f32)],
        compiler_params=_cparams(("arbitrary", "arbitrary", "arbitrary")), name="token_shift_bwd")(
        *[arr for arr in d_arrs for _ in range(3)], px, px, px, mu)


def _dg(a, b, ca, cb):
    return lax.dot_general(a.astype(MXU_DTYPE), b.astype(MXU_DTYPE), (((ca,), (cb,)), ((), ())),
                           preferred_element_type=f32)


@jax.custom_vjp
def _mm_nt(a, b):
    return _dg(a, b, 1, 1)


_mm_nt.defvjp(lambda a, b: (_dg(a, b, 1, 1), (a, b)),
              lambda res, ct: (_dg(ct, res[1], 1, 0), _dg(ct, res[0], 0, 0)))


@jax.custom_vjp
def _mm_tn(a, b):
    return _dg(a, b, 0, 0)


_mm_tn.defvjp(lambda a, b: (_dg(a, b, 0, 0), (a, b)),
              lambda res, ct: (_dg(res[1], ct, 1, 1), _dg(res[0], ct, 1, 0)))


ROTARY_PAIR = RET_DH // 4


def _swap_pairs_impl(t):
    lane = lax.broadcasted_iota(jnp.int32, t.shape, 1)
    return jnp.where(lane % (2 * ROTARY_PAIR) < ROTARY_PAIR, pltpu.roll(t, RET_DH - ROTARY_PAIR, axis=1),
                     pltpu.roll(t, ROTARY_PAIR, axis=1))


@jax.custom_vjp
def _swap_pairs(t):
    return _swap_pairs_impl(t)


_swap_pairs.defvjp(lambda t: (_swap_pairs_impl(t), None), lambda _, ct: (_swap_pairs_impl(ct),))


def _ret_chunk(state, q_raw, k_raw, v, cos, sin, ld_row, head, reverse):
    c = RET_CHUNK
    lane = lax.broadcasted_iota(jnp.int32, ld_row.shape, 1)
    lg = -jnp.exp(jnp.sum(jnp.where(lane == head, ld_row, 0.0), axis=-1, keepdims=True))
    rot = lambda t: t * cos + _swap_pairs(t) * sin
    q = rot(q_raw)
    k = rot(k_raw) * (RET_DH ** -0.5)
    ti = lax.broadcasted_iota(jnp.int32, (c, 1), 0).astype(f32)
    tj = lax.broadcasted_iota(jnp.int32, (1, c), 1).astype(f32)
    if not reverse:
        dist, mask, q_exp, k_exp = ti - tj, (ti - tj) >= 0, ti + 1.0, c - 1.0 - ti
    else:
        dist, mask, q_exp, k_exp = tj - ti, (tj - ti) > 0, c - ti, ti
    decay = jnp.where(mask, jnp.exp(lg * jnp.maximum(dist, 0.0)), 0.0)
    scores = _mm_nt(q, k) * decay
    out = _mxu_dot(scores, v) + _mxu_dot(q * jnp.exp(lg * q_exp), state)
    new_state = state * jnp.exp(lg * c) + _mm_tn(k * jnp.exp(lg * k_exp), v)
    return out, new_state


def _ret_specs(bsz, order):
    tok = lambda col=0: pl.BlockSpec((bsz, RET_CHUNK, RET_W), lambda i: (0, order(i), col))
    tab = pl.BlockSpec((RET_CHUNK, RET_DH), lambda i: (order(i), 0))
    ld = pl.BlockSpec((1, RET_DH), lambda i: (0, 0))
    return tok, tab, ld


def retention_fwd(px, cos, sin, ld_row, order, reverse, name):
    bsz, n_tok, _ = px.shape
    n_ch = n_tok // RET_CHUNK

    def body(q_ref, k_ref, v_ref, cos_ref, sin_ref, ld_ref, o_ref, sv_ref, st_ref):
        @pl.when(pl.program_id(0) == 0)
        def _():
            st_ref[...] = jnp.zeros_like(st_ref)

        for b in range(bsz):
            for h in range(RET_HEADS):
                sl = slice(h * RET_DH, (h + 1) * RET_DH)
                s = st_ref[b, h]
                sv_ref[b, h] = s
                o, s_new = _ret_chunk(s, q_ref[b, :, sl], k_ref[b, :, sl], v_ref[b, :, sl], cos_ref[...], sin_ref[...],
                                      ld_ref[...], h, reverse)
                o_ref[b, :, sl] = o
                st_ref[b, h] = s_new

    tok, tab, ld = _ret_specs(bsz, order)
    return pl.pallas_call(
        body, grid=(n_ch,), in_specs=[tok(0), tok(1), tok(2), tab, tab, ld],
        out_specs=[tok(), pl.BlockSpec((bsz, None, RET_HEADS, RET_DH, RET_DH), lambda i: (0, i, 0, 0, 0))],
        out_shape=[jax.ShapeDtypeStruct((bsz, n_tok, RET_W), f32),
                   jax.ShapeDtypeStruct((bsz, n_ch, RET_HEADS, RET_DH, RET_DH), f32)],
        scratch_shapes=[pltpu.VMEM((bsz, RET_HEADS, RET_DH, RET_DH), f32)],
        compiler_params=_cparams(("arbitrary",)), name=name)(px, px, px, cos, sin, ld_row)


def retention_bwd(do, px, states, cos, sin, ld_row, order, reverse, name, add_to=()):
    bsz, n_tok, _ = px.shape
    n_ch = n_tok // RET_CHUNK
    back = lambda i: order(n_ch - 1 - i)

    def body(do_ref, q_ref, k_ref, v_ref, sv_ref, cos_ref, sin_ref, ld_ref, *rest):
        add_refs, (dq_ref, dk_ref, dv_ref, dld_ref, dst_ref) = rest[:-5] or (None,) * 3, rest[-5:]

        @pl.when(pl.program_id(0) == 0)
        def _():
            dst_ref[...] = jnp.zeros_like(dst_ref)
            dld_ref[...] = jnp.zeros_like(dld_ref)

        cos_v, sin_v = cos_ref[...], sin_ref[...]
        for b in range(bsz):
            for h in range(RET_HEADS):
                sl = slice(h * RET_DH, (h + 1) * RET_DH)
                f = lambda s, q, k, v, ld, h=h: _ret_chunk(s, q, k, v, cos_v, sin_v, ld, h, reverse)
                _, vjp = jax.vjp(f, sv_ref[b, h], q_ref[b, :, sl], k_ref[b, :, sl], v_ref[b, :, sl], ld_ref[...])
                ds, dq, dk, dv, dld = vjp((do_ref[b, :, sl], dst_ref[b, h]))
                dst_ref[b, h] = ds
                for o_ref, add_ref, val in zip((dq_ref, dk_ref, dv_ref), add_refs, (dq, dk, dv)):
                    if add_ref is not None:
                        val = add_ref[b, :, sl] + val
                    o_ref[b, :, sl] = val.astype(o_ref.dtype)
                dld_ref[...] += dld

    tok, tab, ld = _ret_specs(bsz, back)
    return pl.pallas_call(
        body, grid=(n_ch,),
        in_specs=[tok(), tok(0), tok(1), tok(2),
                  pl.BlockSpec((bsz, None, RET_HEADS, RET_DH, RET_DH), lambda i: (0, n_ch - 1 - i, 0, 0, 0)),
                  tab, tab, ld] + [tok() for _ in add_to],
        out_specs=[tok(), tok(), tok(), ld],
        out_shape=[jax.ShapeDtypeStruct((bsz, n_tok, RET_W), MXU_DTYPE if add_to else f32)] * 3
        + [jax.ShapeDtypeStruct((1, RET_DH), f32)],
        scratch_shapes=[pltpu.VMEM((bsz, RET_HEADS, RET_DH, RET_DH), f32)],
        compiler_params=_cparams(("arbitrary",)), name=name)(
        do, px, px, px, states, cos, sin, ld_row, *add_to)


HALF_W = RW_W // 2


def _head_sum(x, ones):
    xm = x.astype(MXU_DTYPE)
    return jnp.concatenate([jnp.dot(xm[:, :HALF_W], ones, preferred_element_type=f32),
                            jnp.dot(xm[:, HALF_W:], ones, preferred_element_type=f32)], axis=1)


def _stack(parts):
    return jnp.concatenate(parts, axis=0)


def _row(ref, b, t):
    return ref[b, pl.ds(t, 1), :]


SCAN_DIRS = ((False, True), (True, False))
RW_HEADS = RW_W // RW_N
HEAD_ROWS_PAD = 16


def _head_rows(row, mask):
    return jnp.broadcast_to(row, mask.shape) * mask


def _outer(per_value, row, mask_pad):
    return lax.dot_general(per_value.astype(MXU_DTYPE), _head_rows(row, mask_pad).astype(MXU_DTYPE),
                           (((0,), (0,)), ((), ())), preferred_element_type=f32)


def _read(states, rows, mask, more_rows=()):
    lhs = _stack([_head_rows(r, mask) for r in list(rows) + list(more_rows)])
    return lax.dot_general(lhs.astype(MXU_DTYPE), _stack(states).astype(MXU_DTYPE), (((1,), (1,)), ((), ())),
                           preferred_element_type=f32)


def _own_block(raw, b):
    lanes = raw[:, RW_N * b:RW_N * (b + 1)]
    turned = _stack([lanes[RW_HEADS * b:], lanes[:RW_HEADS * b]]) if b else lanes
    if turned.shape[0] < HEAD_ROWS_PAD:
        turned = _stack([turned, jnp.zeros((HEAD_ROWS_PAD - turned.shape[0], RW_N), f32)])
    return turned[:HEAD_ROWS_PAD]


def _row_from_heads(per_value, state, mask_pad):
    full = jnp.dot(per_value.astype(MXU_DTYPE), state.astype(MXU_DTYPE), preferred_element_type=f32)
    return jnp.sum(full * mask_pad, axis=0, keepdims=True)


def _scan_specs(bsz, order):
    rows = lambda col=0: pl.BlockSpec((bsz, SCAN_CHUNK, RW_W), lambda i: (0, order(i), col))
    per_value = pl.BlockSpec((bsz, SCAN_CHUNK, HEAD_ROWS_PAD, RW_N), lambda i: (0, order(i), 0, 0))
    states = pl.BlockSpec((SCAN_CHUNK, bsz, RW_N, RW_W), lambda i: (order(i), 0, 0, 0))
    blocks = pl.BlockSpec((SCAN_CHUNK, RW_HEADS * bsz, RW_N * bsz), lambda i: (order(i), 0, 0))
    return rows, per_value, states, blocks


def _mxu_operands(states):
    return [s.astype(MXU_DTYPE) for s in states]


def _removed(states_m, kk_t, ones, bsz):
    removed = _head_sum(_stack([states_m[b] * kk_t[b].astype(MXU_DTYPE) for b in range(bsz)]), ones)
    return [removed[b * RW_N:(b + 1) * RW_N] for b in range(bsz)]


def _advance(sp, rem, w_t, b_t, vk, bsz):
    return [sp[b] * w_t[b] - rem[b] * b_t[b] + vk[b] for b in range(bsz)]


def heads_to_rows(a):
    b, t, _ = a.shape
    return jnp.pad(a.astype(MXU_DTYPE).reshape(b, t, RW_HEADS, RW_N),
                   ((0, 0), (0, 0), (0, HEAD_ROWS_PAD - RW_HEADS), (0, 0)))


def _blocks_to_rows(raw_ref, first, row_ref, bsz):
    steps = pl.ds(first, SCAN_CHUNK)
    for b in range(bsz):
        for h in range(RW_HEADS):
            row_ref[b, :, h * RW_N:(h + 1) * RW_N] = raw_ref[steps, RW_HEADS * b + h, RW_N * b:RW_N * (b + 1)]


N_ROWS_FWD = 5
N_ROWS_BWD = 5


def _scan_consts(bsz):
    head = (jnp.arange(RW_W)[None, :] // RW_N == jnp.arange(RW_HEADS)[:, None]).astype(f32)
    return head, jnp.pad(head, ((0, HEAD_ROWS_PAD - RW_HEADS), (0, 0))), _block_ones(HALF_W, RW_N)


def _const_specs(consts):
    return [pl.BlockSpec(c.shape, lambda i: (0, 0)) for c in consts]


def rwkv_scan_fwd(rows_in, v_heads, orders, name):
    bsz, n_tok, _ = rows_in[0][0][0].shape
    n_ch = n_tok // SCAN_CHUNK
    rng = range(bsz)
    consts = _scan_consts(bsz)

    def body(*refs):
        rows = [refs[:N_ROWS_FWD], refs[N_ROWS_FWD:2 * N_ROWS_FWD]]
        (v0, v1, head_ref, pad_ref, ones_ref, y0, y1, h0, h1, f0, f1, m0, m1, s0, s1, late_ref,
         raw_ref) = refs[2 * N_ROWS_FWD:]
        v_refs, y_refs, hist_refs, final_refs, s_refs = (v0, v1), (y0, y1), (h0, h1), (f0, f1), (s0, s1)
        removed_refs = (m0, m1)
        n_blk = RW_HEADS * bsz
        head_v, pad_v, ones_v = head_ref[...], pad_ref[...], ones_ref[...]
        for d in range(2):
            @pl.when(pl.program_id(0) == 0)
            def _(d=d):
                s_refs[d][...] = jnp.zeros_like(s_refs[d])

        def step(j, carry):
            ts = [SCAN_CHUNK - 1 - j if reverse else j for reverse, _ in SCAN_DIRS]
            sps = [[s_refs[d][b] for b in rng] for d in range(2)]
            sps_m = [_mxu_operands(sps[d]) for d in range(2)]
            vks = [[_outer(v_refs[d][b, ts[d]], _row(rows[d][4], b, ts[d]), pad_v) for b in rng] for d in range(2)]
            rems = [_removed(sps_m[d], [_row(rows[d][1], b, ts[d]) for b in rng], ones_v, bsz) for d in range(2)]
            for d, (reverse, inclusive) in enumerate(SCAN_DIRS):
                r_ref = rows[d][0]
                read_at = jnp.maximum(j - 1, 0) if inclusive else ts[d]
                both = _read(sps_m[d], [_row(r_ref, b, read_at) for b in rng], head_v,
                             [_row(rows[d][1], b, ts[d]) for b in rng])
                if inclusive:
                    late_ref[j] = both[:n_blk]
                else:
                    raw_ref[ts[d]] = both[:n_blk]
                removed_refs[d][ts[d]] = both[n_blk:]
            for d in range(2):
                new = _advance(sps[d], rems[d], [_row(rows[d][2], b, ts[d]) for b in rng],
                               [_row(rows[d][3], b, ts[d]) for b in rng], vks[d], bsz)
                for b in rng:
                    hist_refs[d][ts[d], b] = sps_m[d][b]
                    s_refs[d][b] = new[b]
            return carry

        lax.fori_loop(0, SCAN_CHUNK, step, 0, unroll=SCAN_UNROLL)
        for d, (reverse, inclusive) in enumerate(SCAN_DIRS):
            final_refs[d][...] = s_refs[d][...]
            if inclusive:
                assert not reverse
                last = SCAN_CHUNK - 1
                late_ref[SCAN_CHUNK] = _read(_mxu_operands([s_refs[d][b] for b in rng]),
                                             [rows[d][0][b, last:last + 1, :] for b in rng], head_v)
                _blocks_to_rows(late_ref, 1, y_refs[d], bsz)
            else:
                _blocks_to_rows(raw_ref, 0, y_refs[d], bsz)

    specs = [_scan_specs(bsz, orders[d]) for d in range(2)]
    state = pltpu.VMEM((bsz, RW_N, RW_W), f32)
    late = pltpu.VMEM((SCAN_CHUNK + 1, RW_HEADS * bsz, RW_N * bsz), f32)
    raw = pltpu.VMEM((SCAN_CHUNK, RW_HEADS * bsz, RW_N * bsz), f32)
    final_spec = pl.BlockSpec((bsz, RW_N, RW_W), lambda i: (0, 0, 0))
    return pl.pallas_call(
        body, grid=(n_ch,),
        in_specs=[specs[d][0](col) for d in range(2) for _, col in rows_in[d]] + [specs[0][1], specs[1][1]]
        + _const_specs(consts),
        out_specs=[specs[0][0](), specs[1][0](), specs[0][2], specs[1][2], final_spec, final_spec,
                   specs[0][3], specs[1][3]],
        out_shape=[jax.ShapeDtypeStruct((bsz, n_tok, RW_W), f32)] * 2
        + [jax.ShapeDtypeStruct((n_tok, bsz, RW_N, RW_W), MXU_DTYPE)] * 2
        + [jax.ShapeDtypeStruct((bsz, RW_N, RW_W), f32)] * 2
        + [jax.ShapeDtypeStruct((n_tok, RW_HEADS * bsz, RW_N * bsz), f32)] * 2,
        scratch_shapes=[state, state, late, raw],
        compiler_params=_cparams(("arbitrary",)), name=name)(
        *[a for d in range(2) for a, _ in rows_in[d]], v_heads, v_heads, *consts)


def rwkv_scan_bwd(rows_in, v_heads, dy_heads, hists, finals, removed, orders, name):
    bsz, n_tok, _ = rows_in[0][0][0].shape
    n_ch = n_tok // SCAN_CHUNK
    backs = [functools.partial(lambda i, order: order(n_ch - 1 - i), order=orders[d]) for d in range(2)]
    rng = range(bsz)
    consts = _scan_consts(bsz)
    n_out, n_scr = 6, 6

    def body(*refs):
        rows = [refs[:N_ROWS_BWD], refs[N_ROWS_BWD:2 * N_ROWS_BWD]]
        rest = refs[2 * N_ROWS_BWD:]
        v_refs, dy_refs, hist_refs, final_refs, removed_refs = rest[0:2], rest[2:4], rest[4:6], rest[6:8], rest[8:10]
        head_ref, pad_ref, ones_ref = rest[10:13]
        outs = [rest[13:13 + n_out], rest[13 + n_out:13 + 2 * n_out]]
        scr = [rest[13 + 2 * n_out:13 + 2 * n_out + n_scr], rest[13 + 2 * n_out + n_scr:]]
        n_blk = RW_HEADS * bsz
        head_v, pad_v, ones_v = head_ref[...], pad_ref[...], ones_ref[...]
        for d in range(2):
            @pl.when(pl.program_id(0) == 0)
            def _(d=d):
                scr[d][1][...] = jnp.zeros_like(scr[d][1])
                scr[d][0][...] = final_refs[d][...]

        def step_of(j, reverse):
            return j if reverse else SCAN_CHUNK - 1 - j

        for d, (reverse, _) in enumerate(SCAN_DIRS):
            t0 = step_of(0, reverse)
            for b in rng:
                scr[d][3][b] = _outer(dy_refs[d][b, t0], rows[d][0][b, t0:t0 + 1, :], pad_v)

        def bstep(j, carry):
            ts = [step_of(j, reverse) for reverse, _ in SCAN_DIRS]
            reads = [[scr[d][3][b] for b in rng] for d in range(2)]
            dss = []
            for d, (_, inclusive) in enumerate(SCAN_DIRS):
                ds = [scr[d][1][b] for b in rng]
                dss.append([ds[b] + reads[d][b] for b in rng] if inclusive else ds)
            dss_m = [_mxu_operands(dss[d]) for d in range(2)]
            nexts = []
            for d, (reverse, _) in enumerate(SCAN_DIRS):
                t_next = step_of(jnp.minimum(j + 1, SCAN_CHUNK - 1), reverse)
                nexts.append([_outer(dy_refs[d][b, t_next], _row(rows[d][0], b, t_next), pad_v) for b in rng])
            drems = [_removed(dss_m[d], [-_row(rows[d][3], b, ts[d]) for b in rng], ones_v, bsz) for d in range(2)]
            for d in range(2):
                for b in rng:
                    scr[d][3][b] = nexts[d][b]
                both = _read(dss_m[d], [_row(rows[d][4], b, ts[d]) for b in rng], head_v,
                             [-_row(rows[d][3], b, ts[d]) for b in rng])
                scr[d][4][ts[d]] = both[:n_blk]
                scr[d][5][ts[d]] = both[n_blk:]
            for d, (_, inclusive) in enumerate(SCAN_DIRS):
                _, kk_ref, w_ref, _, _ = rows[d]
                _, ds_ref, dsh_ref = scr[d][:3]
                for b in rng:
                    dsh_ref[ts[d], b] = dss[d][b]
                    dsp = dss[d][b] * _row(w_ref, b, ts[d]) + drems[d][b] * _row(kk_ref, b, ts[d])
                    ds_ref[b] = dsp if inclusive else dsp + reads[d][b]
            return carry

        lax.fori_loop(0, SCAN_CHUNK, bstep, 0, unroll=SCAN_UNROLL)

        rsum = lambda z: jnp.sum(z, axis=0, keepdims=True)
        for d, (reverse, inclusive) in enumerate(SCAN_DIRS):
            dr_ref, dkk_ref, dw_ref, db_ref, dkt_ref, dv_ref = outs[d]
            after_ref, _, dsh_ref, _, dv_raw_ref, dremt_ref = scr[d]
            hist_ref, removed_ref = hist_refs[d], removed_refs[d]
            _blocks_to_rows(dv_raw_ref, 0, dv_ref, bsz)
            for t in range(SCAN_CHUNK):
                ts = slice(t, t + 1)
                after = t - 1 if reverse else t + 1
                for b in rng:
                    sp_m, ds = hist_ref[t, b], dsh_ref[t, b]
                    sp = sp_m.astype(f32)
                    if not inclusive:
                        seen = sp_m
                    else:
                        seen = hist_ref[after, b] if 0 <= after < SCAN_CHUNK else after_ref[b]
                    dr_ref[b, ts, :] = _row_from_heads(dy_refs[d][b, t], seen, pad_v)
                    dkt_ref[b, ts, :] = _row_from_heads(v_refs[d][b, t], ds, pad_v)
                    dw_ref[b, ts, :] = rsum(ds * sp)
                    db_ref[b, ts, :] = -_row_from_heads(_own_block(removed_ref[t], b), ds, pad_v)
                    dkk_ref[b, ts, :] = _row_from_heads(_own_block(dremt_ref[t], b), sp_m, pad_v)
            if inclusive:
                first = SCAN_CHUNK - 1 if reverse else 0
                for b in rng:
                    after_ref[b] = hist_ref[first, b].astype(f32)

    specs = [_scan_specs(bsz, backs[d]) for d in range(2)]
    hist = pltpu.VMEM((SCAN_CHUNK, bsz, RW_N, RW_W), f32)
    state = pltpu.VMEM((bsz, RW_N, RW_W), f32)
    final_spec = pl.BlockSpec((bsz, RW_N, RW_W), lambda i: (0, 0, 0))
    raw = pltpu.VMEM((SCAN_CHUNK, RW_HEADS * bsz, RW_N * bsz), f32)
    return pl.pallas_call(
        body, grid=(n_ch,),
        in_specs=[specs[d][0](col) for d in range(2) for _, col in rows_in[d]]
        + [specs[0][1], specs[1][1]] * 2 + [specs[0][2], specs[1][2], final_spec, final_spec, specs[0][3], specs[1][3]]
        + _const_specs(consts),
        out_specs=[specs[d][0]() for d in range(2) for _ in range(n_out)],
        out_shape=[jax.ShapeDtypeStruct((bsz, n_tok, RW_W), f32)] * (2 * n_out),
        scratch_shapes=[state, state, hist, state, raw, raw] * 2,
        compiler_params=_cparams(("arbitrary",)), name=name)(
        *[a for d in range(2) for a, _ in rows_in[d]], v_heads, v_heads, dy_heads, dy_heads, *hists, *finals, *removed, *consts)


MOD_NAMES = ("shift1", "scale1", "gate1", "shift2", "scale2", "gate2")


def _rope_tables(t_ctx, t_x):
    quarter = RET_DH // 4
    pos = jnp.arange(t_x)
    inv = jnp.power(ROPE_BASE, -jnp.arange(0, 2 * quarter, 2, dtype=f32) / (2 * quarter))
    ang_r = (pos // GRID_W).astype(f32)[:, None] * inv[None, :]
    ang_c = (pos % GRID_W).astype(f32)[:, None] * inv[None, :]
    cos = jnp.concatenate([jnp.cos(ang_r)] * 2 + [jnp.cos(ang_c)] * 2, axis=1)
    sin = jnp.concatenate([-jnp.sin(ang_r), jnp.sin(ang_r), -jnp.sin(ang_c), jnp.sin(ang_c)], axis=1)
    cos = jnp.concatenate([jnp.ones((t_ctx, RET_DH), f32), cos], axis=0)
    sin = jnp.concatenate([jnp.zeros((t_ctx, RET_DH), f32), sin], axis=0)
    return cos, sin


def _pad_rows(w, lo, total):
    return jnp.pad(w, ((lo, total - lo - w.shape[0]), (0, 0)))


LATE_WEIGHTS = ("w_out", "w_ff1", "w_ff2")


def layer_step(x, ctx, tgt, mod_x, mod_ctx, wt, late_weights=None, early_grads=None, last_grads=None):
    bsz, t_x, _ = x.shape
    t_c = ctx.shape[1]
    t_all = t_c + t_x
    n_ct, n_xt = t_c // TOK_TILE, t_x // TOK_TILE
    n_t = n_ct + n_xt
    assert t_c % TOK_TILE == 0 and t_x % TOK_TILE == 0 and t_c % RET_CHUNK == 0

    seg = lambda i: (i >= n_ct).astype(jnp.int32)
    seg_first = lambda i: jnp.logical_or(i == 0, i == n_ct)
    seg_last = lambda i: jnp.logical_or(i == n_ct - 1, i == n_t - 1)
    mod_all = {n: jnp.stack([jnp.broadcast_to(mod_ctx[k], (bsz, D_MODEL)), mod_x[:, k]], axis=1)[:, :, None, :]
               for k, n in enumerate(MOD_NAMES)}
    mod_lat = {n: mod_x[:, k][:, None, None, :] for k, n in enumerate(MOD_NAMES)}
    both = lambda n: Seg(mod_all[n], seg, seg_first)
    lat = lambda n: Seg(mod_lat[n], lambda i: 0, lambda i: i == 0)
    flat = lambda a: a.reshape(-1, a.shape[-1])

    def chunk_orders(n_ctx_chunks, n_chunks):
        fwd = lambda i: i
        bwd = lambda i: jnp.where(i < n_ctx_chunks, n_ctx_chunks - 1 - i, n_chunks + n_ctx_chunks - 1 - i)
        return fwd, bwd

    ones64, ones128 = _block_ones(RW_W, RW_N), _block_ones(RET_W, RET_DH)
    cos, sin = _rope_tables(t_c, t_x)
    ld_rows = [jnp.pad(wt["ret_log_decay"][d][None, :], ((0, 0), (0, RET_DH - RET_HEADS))) for d in range(2)]
    w_up_pad = [_pad_rows(wt["rwkv_w_up"][d], 0, LORA_W) for d in range(2)]
    a_up_pad = [_pad_rows(wt["rwkv_a_up"][d], DECAY_LORA, LORA_W) for d in range(2)]
    g_up_pad = _pad_rows(wt["rwkv_g_up"], DECAY_LORA + AAA_LORA, LORA_W)
    row = lambda a, d: a[d][None, :]

    h = jnp.concatenate([ctx, x], axis=1)
    norm1_ins = lambda: [Tiled(h), both("shift1"), both("scale1"), Glob(wt["norm1_g"])]
    (n1,) = ew_forward(fn_norm_mod, "norm1", bsz, n_t, norm1_ins(), [(D_MODEL, MXU_DTYPE)])
    px = matmul(flat(n1), wt["w_in"], "nn", "proj_in").reshape(bsz, t_all, IN_COLS)
    ps = token_shift(px, wt["rwkv_shift_mu"], seg_first, seg_last)

    def prep_ins():
        return [Tiled(ps, RW_W, 1), Tiled(ps, LORA_W, 3 * RW_W // LORA_W),
                Glob(row(wt["rwkv_w0"], 0)), Glob(row(wt["rwkv_w0"], 1)),
                Glob(row(wt["rwkv_a0"], 0)), Glob(row(wt["rwkv_a0"], 1)),
                Glob(w_up_pad[0]), Glob(w_up_pad[1]), Glob(a_up_pad[0]), Glob(a_up_pad[1]), Glob(g_up_pad),
                Glob(wt["rwkv_k_k"]), Glob(wt["rwkv_k_a"]), Glob(ones64)]

    kk, w_f, b_f, kt_f, w_b, b_b, kt_b, g_rw = ew_forward(fn_rwkv_prepare, "rwkv_prepare", bsz, n_t, prep_ins(),
                                                           [(RW_W, f32)] * 8)
    rw_order = chunk_orders(t_c // SCAN_CHUNK, t_all // SCAN_CHUNK)
    ret_order = chunk_orders(t_c // RET_CHUNK, t_all // RET_CHUNK)
    scan_rows = [[(ps, 0), (kk, 0), (w_f, 0), (b_f, 0), (kt_f, 0)], [(ps, 0), (kk, 0), (w_b, 0), (b_b, 0), (kt_b, 0)]]
    v_heads = heads_to_rows(ps[..., 2 * RW_W:3 * RW_W])
    y_f, y_b, *kept_states = rwkv_scan_fwd(scan_rows, v_heads, rw_order, "rwkv_scan_fwd")
    y = [y_f, y_b]
    o, ret_states = [], []
    for d in range(2):
        o_d, st_d = retention_fwd(px, cos, sin, ld_rows[d], ret_order[d], SCAN_DIRS[d][0], f"retention_fwd{d}")
        o.append(o_d), ret_states.append(st_d)

    def merge_ins(toff):
        return [Tiled(o[0], toff=toff), Tiled(o[1], toff=toff), Tiled(px, RET_W, 3, toff),
                Tiled(y[0], toff=toff), Tiled(y[1], toff=toff), Tiled(ps, RW_W, 0, toff), Tiled(kt_f, toff=toff),
                Tiled(ps, RW_W, 2, toff), Tiled(g_rw, toff=toff),
                Glob(wt["rwkv_r_k"]), Glob(wt["rwkv_ln_w"]), Glob(wt["rwkv_ln_b"]), Glob(ones64), Glob(ones128)]

    ret_out, rw_out = ew_forward(fn_merge, "merge_heads", bsz, n_xt, merge_ins(n_ct),
                                 [(RET_W, MXU_DTYPE), (RW_W, MXU_DTYPE)])
    merged = jnp.concatenate([ret_out, rw_out], axis=-1)
    if late_weights is not None:
        wt = {**wt, **late_weights(merged)}
    mix = matmul(flat(merged), wt["w_out"], "nn", "proj_out").reshape(bsz, t_x, D_MODEL)
    resid_ins = lambda: [Tiled(x), Tiled(mix), lat("gate1"), lat("shift2"), lat("scale2"), Glob(wt["norm2_g"])]
    h1, n2 = ew_forward(fn_resid_norm_mod, "resid_norm2", bsz, n_xt, resid_ins(), [(D_MODEL, f32), (D_MODEL, MXU_DTYPE)])
    act = matmul(flat(n2), wt["w_ff1"], "nn", "ff1", MXU_DTYPE, wt["b_ff1"], relu2).reshape(bsz, t_x, D_FF)
    ff = matmul(flat(act), wt["w_ff2"], "nn", "ff2").reshape(bsz, t_x, D_MODEL)

    g = {}
    loss, dh1, dff, dgate2, g["b_ff2"], g["final_g"] = loss_and_grads(
        h1, ff, tgt, mod_lat["gate2"], wt["b_ff2"], wt["final_g"], bsz, n_xt)
    dact = matmul(flat(dff), wt["w_ff2"], "nt", "ff2_dx", MXU_DTYPE).reshape(bsz, t_x, D_FF)
    g["w_ff2"] = matmul(flat(act), flat(dff), "tn", "ff2_dw", MXU_DTYPE)
    du, g["b_ff1"] = relu2_backward(act, dact, "relu2_bwd")
    dn2 = matmul(flat(du), wt["w_ff1"], "nt", "ff1_dx").reshape(bsz, t_x, D_MODEL)
    g["w_ff1"] = matmul(flat(n2), flat(du), "tn", "ff1_dw", MXU_DTYPE)
    dx_res, dmix, dgate1, dshift2, dscale2, g["norm2_g"] = ew_backward(
        fn_resid_norm_mod, "resid_norm2_bwd", bsz, n_xt, resid_ins(), [Tiled(dh1), Tiled(dn2)], [True] * 6,
        {1: MXU_DTYPE})
    dmerged = matmul(flat(dmix), wt["w_out"], "nt", "proj_out_dx").reshape(bsz, t_x, D_MODEL)
    g["w_out"] = matmul(flat(merged), flat(dmix), "tn", "proj_out_dw", MXU_DTYPE)
    if early_grads is not None:
        token = early_grads({n: g.pop(n) for n in LATE_WEIGHTS})
        wt = {**wt, "rwkv_r_k": wt["rwkv_r_k"] + token[:1, :1]}
    (do, dg_ret, dy, dr_m, dkt_m, dv_m, dg_rw, g["rwkv_r_k"], g["rwkv_ln_w"], g["rwkv_ln_b"]) = ew_backward(
        fn_merge, "merge_heads_bwd", bsz, n_xt, merge_ins(0),
        [Tiled(dmerged, RET_W, 0, -n_ct), Tiled(dmerged, RW_W, 1, -n_ct)],
        [True, False, True, True, False, True, True, True, True, True, True, True, False, False], lead=n_ct)

    dqkv, dld = (), []
    for d in range(2):
        *dqkv, dld_d = retention_bwd(do, px, ret_states[d], cos, sin, ld_rows[d], ret_order[d],
                                     SCAN_DIRS[d][0], f"retention_bwd{d}", add_to=dqkv)
        dld.append(dld_d[0, :RET_HEADS])
    g["ret_log_decay"] = jnp.stack(dld)
    (dr_f, dkk_f, dw_f, db_f, dkt_f, dv_f, dr_b, dkk_b, dw_b, db_b, dkt_b, dv_b) = rwkv_scan_bwd(
        scan_rows, v_heads, heads_to_rows(dy), kept_states[:2], kept_states[2:4], kept_states[4:], rw_order, "rwkv_scan_bwd")
    prep_cts = [(dkk_f, dkk_b), dw_f, db_f, (dkt_f, dkt_m), dw_b, db_b, dkt_b, dg_rw]
    (dks, dlora, dw0_f, dw0_b, da0_f, da0_b, dwup_f, dwup_b, daup_f, daup_b, dgup, g["rwkv_k_k"],
     g["rwkv_k_a"]) = ew_backward(fn_rwkv_prepare, "rwkv_prepare_bwd", bsz, n_t, prep_ins(),
                                  [tuple(map(Tiled, c)) if isinstance(c, tuple) else Tiled(c) for c in prep_cts],
                                  [True] * 13 + [False])
    g["rwkv_w0"] = jnp.concatenate([dw0_f, dw0_b], axis=0)
    g["rwkv_a0"] = jnp.concatenate([da0_f, da0_b], axis=0)
    g["rwkv_w_up"] = jnp.stack([dwup_f[:DECAY_LORA], dwup_b[:DECAY_LORA]])
    g["rwkv_a_up"] = jnp.stack([daup_f[DECAY_LORA:DECAY_LORA + AAA_LORA], daup_b[DECAY_LORA:DECAY_LORA + AAA_LORA]])
    g["rwkv_g_up"] = dgup[DECAY_LORA + AAA_LORA:]
    dp_rw, g["rwkv_shift_mu"] = token_shift_bwd([(dr_f, dr_b, dr_m), (dks,), (dv_f, dv_b, dv_m), (dlora,)], px,
                                                 wt["rwkv_shift_mu"], seg_first, seg_last)
    dpx = jnp.concatenate(dqkv + [dg_ret.astype(MXU_DTYPE), dp_rw], axis=-1)
    g["w_in"] = matmul(flat(n1), flat(dpx), "tn", "proj_in_dw", MXU_DTYPE)
    after_start = None
    if last_grads is not None:
        token = last_grads(g.pop("w_in"), {n: g.pop(n) for n in LAST_SHARDED})
        after_start = jnp.zeros((1, D_MODEL), f32) + token[:1, :1]
    dn1 = matmul(flat(dpx), wt["w_in"], "nt", "proj_in_dx", bias=after_start).reshape(bsz, t_all, D_MODEL)
    dh, dshift1, dscale1, g["norm1_g"] = ew_backward(fn_norm_mod, "norm1_bwd", bsz, n_t, norm1_ins(), [Tiled(dn1)],
                                                     [True] * 4)
    grad_x = dh[:, t_c:] + dx_res
    zeros = jnp.zeros((D_MODEL,), f32)
    g["mod_x"] = jnp.stack([dshift1[:, 1, 0], dscale1[:, 1, 0], dgate1[:, 0, 0], dshift2[:, 0, 0], dscale2[:, 0, 0],
                            dgate2[:, 0, 0]], axis=1)
    g["mod_ctx"] = jnp.stack([dshift1[:, 0, 0].sum(0), dscale1[:, 0, 0].sum(0), zeros, zeros, zeros, zeros])
    return loss, grad_x, g


MESH_ID = pl.DeviceIdType.MESH
ALL_PEERS = [(dx, dy, dc) for dx in (0, 1) for dy in (0, 1) for dc in (0, 1)][1:]
CHIP_PEERS = [(1, 0, 0), (0, 1, 0), (1, 1, 0)]
CHIP_SLOTS = (0, 2, 4, 6)


def _mesh_pos():
    return lax.axis_index("x"), lax.axis_index("y"), lax.axis_index("c")


def _device_slot():
    x, y, c = _mesh_pos()
    return 4 * x + 2 * y + c


def sibling_swap(arrs, name):
    n = len(arrs)

    def body(*refs):
        in_refs, out_refs = refs[:n], refs[n:2 * n]
        send_sems, recv_sems = refs[2 * n:]
        x, y, c = _mesh_pos()
        copies = [pltpu.make_async_remote_copy(src_ref=in_refs[a], dst_ref=out_refs[a], send_sem=send_sems.at[a],
                                               recv_sem=recv_sems.at[a], device_id=(x, y, 1 - c),
                                               device_id_type=MESH_ID) for a in range(n)]
        for cp in copies:
            cp.start()
        for cp in copies:
            cp.wait()

    any_spec = pl.BlockSpec(memory_space=pl.ANY)
    res = pl.pallas_call(
        body, in_specs=[any_spec] * n, out_specs=[any_spec] * n,
        out_shape=[jax.ShapeDtypeStruct(a.shape, a.dtype) for a in arrs],
        scratch_shapes=[pltpu.SemaphoreType.DMA((n,)), pltpu.SemaphoreType.DMA((n,))],
        name=name)(*arrs)
    return list(res)


def exchange(arrs, gather, peers, name, by_chip=False, own=True):
    n, n_peers = len(arrs), len(peers)
    n_slots = N_SHARDS if by_chip else N_DEV
    slot = (lambda x, y, c: 2 * x + y) if by_chip else (lambda x, y, c: 4 * x + 2 * y + c)

    def body(*refs):
        in_refs, out_refs = refs[:n], refs[n:2 * n]
        send_sems, recv_sems, local_sems = refs[2 * n:]
        x, y, c = _mesh_pos()
        me = slot(x, y, c)
        copies, locals_ = [], []
        for a in range(n):
            if own:
                mine = in_refs[a] if gather else in_refs[a].at[me]
                loc = pltpu.make_async_copy(mine, out_refs[a].at[me], local_sems.at[a])
                loc.start()
                locals_.append(loc)
            for k, (dx, dy, dc) in enumerate(peers):
                peer = (1 - x if dx else x, 1 - y if dy else y, 1 - c if dc else c)
                src = in_refs[a] if gather else in_refs[a].at[slot(*peer)]
                sem = a * n_peers + k
                cp = pltpu.make_async_remote_copy(src_ref=src, dst_ref=out_refs[a].at[me], send_sem=send_sems.at[sem],
                                                  recv_sem=recv_sems.at[sem], device_id=peer, device_id_type=MESH_ID)
                cp.start()
                copies.append(cp)
        for cp in copies:
            cp.wait()
        for loc in locals_:
            loc.wait()

    any_spec = pl.BlockSpec(memory_space=pl.ANY)
    out_shape = [jax.ShapeDtypeStruct((n_slots,) + (a.shape if gather else a.shape[1:]), a.dtype) for a in arrs]
    n_sems = n * n_peers
    res = pl.pallas_call(
        body, in_specs=[any_spec] * n, out_specs=[any_spec] * n, out_shape=out_shape,
        scratch_shapes=[pltpu.SemaphoreType.DMA((n_sems,)), pltpu.SemaphoreType.DMA((n_sems,)),
                        pltpu.SemaphoreType.DMA((n,))],
        name=name)(*arrs)
    return list(res)


HBM_SPEC = pl.BlockSpec(memory_space=pltpu.HBM)
SEM_SPEC = pl.BlockSpec(memory_space=pltpu.SEMAPHORE)
DATAFLOW = pltpu.SideEffectType.DATAFLOW_SIDE_EFFECTING


def _peer_copies(src_refs, land_refs, send_sems, recv_sems, gather):
    x, y, c = _mesh_pos()
    me = 4 * x + 2 * y + c
    copies = []
    for a, (src_ref, land_ref) in enumerate(zip(src_refs, land_refs)):
        for k, (dx, dy, dc) in enumerate(ALL_PEERS):
            peer = (1 - x if dx else x, 1 - y if dy else y, 1 - c if dc else c)
            src = src_ref if gather else src_ref.at[4 * peer[0] + 2 * peer[1] + peer[2]]
            sem = a * len(ALL_PEERS) + k
            copies.append(pltpu.make_async_remote_copy(src_ref=src, dst_ref=land_ref.at[me], send_sem=send_sems.at[sem],
                                                       recv_sem=recv_sems.at[sem], device_id=peer,
                                                       device_id_type=MESH_ID))
    return copies


def exchange_start(arrs, gather, name):
    n = len(arrs)
    lands = [lax.empty((N_DEV,) + (a.shape if gather else a.shape[1:]), a.dtype) for a in arrs]

    def body(*refs):
        for cp in _peer_copies(refs[:n], refs[n:2 * n], refs[2 * n], refs[2 * n + 1], gather):
            cp.start()
        refs[-1][...] = jnp.zeros_like(refs[-1])

    sems = pltpu.SemaphoreType.DMA((n * len(ALL_PEERS),))
    hbm = [pltpu.HBM(a.shape, a.dtype) for a in arrs + lands]
    res = pl.pallas_call(
        body, name=name, out_shape=(sems, sems, *hbm, jax.ShapeDtypeStruct((8, 128), f32)),
        in_specs=[HBM_SPEC] * (2 * n),
        out_specs=(SEM_SPEC, SEM_SPEC, *[HBM_SPEC] * (2 * n), pl.BlockSpec(memory_space=pltpu.VMEM)),
        input_output_aliases={i: 2 + i for i in range(2 * n)},
        compiler_params=pltpu.CompilerParams(has_side_effects=DATAFLOW))(
        *[pltpu.with_memory_space_constraint(a, pltpu.HBM) for a in arrs + lands])
    return res[0], res[1], list(res[2:2 + n]), list(res[2 + n:2 + 2 * n]), res[-1]


def exchange_wait(started, after, gather, name):
    send_sems, recv_sems, srcs, lands, _ = started
    n = len(srcs)

    def body(*refs):
        for cp in _peer_copies(refs[:n], refs[n:2 * n], refs[2 * n], refs[2 * n + 1], gather):
            cp.wait_send()
            cp.wait_recv()

    res = pl.pallas_call(
        body, name=name, out_shape=tuple(pltpu.HBM(a.shape, a.dtype) for a in srcs + lands),
        in_specs=[HBM_SPEC] * (2 * n) + [SEM_SPEC, SEM_SPEC, pl.BlockSpec(memory_space=pl.ANY)],
        out_specs=tuple([HBM_SPEC] * (2 * n)), input_output_aliases={i: i for i in range(2 * n)},
        compiler_params=pltpu.CompilerParams(has_side_effects=DATAFLOW))(*srcs, *lands, send_sems, recv_sems, after)
    return list(res[n:])


def gather_two_level(arrs, name):
    n = len(arrs)
    per = 7

    def body(*refs):
        in_refs, out_refs = refs[:n], refs[n:2 * n]
        send_sems, recv_sems = refs[2 * n:]
        x, y, c = _mesh_pos()
        me, sibling = (x, y, c), (x, y, 1 - c)
        chips = [(1 - x, y), (x, 1 - y), (1 - x, 1 - y)]

        def copy(a, k, block, to, src=None):
            rows = out_refs[a].at[4 * block[0] + 2 * block[1] + block[2]]
            return pltpu.make_async_remote_copy(src_ref=rows if src is None else src, dst_ref=rows,
                                                send_sem=send_sems.at[a * per + k], recv_sem=recv_sems.at[a * per + k],
                                                device_id=to, device_id_type=MESH_ID)

        first, passed = [], []
        for a in range(n):
            first.append(copy(a, 0, me, sibling, src=in_refs[a]))
            first += [copy(a, 1 + j, me, (*chip, c), src=in_refs[a]) for j, chip in enumerate(chips)]
        for cp in first:
            cp.start()
        for a in range(n):
            for j, chip in enumerate(chips):
                copy(a, 1 + j, (*chip, c), me).wait_recv()
                fwd = copy(a, 4 + j, (*chip, c), sibling)
                fwd.start()
                passed.append(fwd)
        for a in range(n):
            copy(a, 0, sibling, me).wait_recv()
            for j, chip in enumerate(chips):
                copy(a, 4 + j, (*chip, 1 - c), me).wait_recv()
        for cp in first + passed:
            cp.wait_send()

    any_spec = pl.BlockSpec(memory_space=pl.ANY)
    res = pl.pallas_call(
        body, in_specs=[any_spec] * n, out_specs=[any_spec] * n,
        out_shape=[jax.ShapeDtypeStruct((N_DEV,) + a.shape, a.dtype) for a in arrs],
        scratch_shapes=[pltpu.SemaphoreType.DMA((n * per,)), pltpu.SemaphoreType.DMA((n * per,))],
        name=name)(*arrs)
    return list(res)


def sum_slots(parts, slots, name):
    _, r, c = parts.shape
    tr = r
    for cand in (512, 256, 128, 64, 32, 16, 8):
        if r % cand == 0 and cand * c * 4 * len(slots) <= 8 * 1024 * 1024:
            tr = cand
            break

    def body(p_ref, o_ref):
        acc = p_ref[slots[0]].astype(f32)
        for s in slots[1:]:
            acc = acc + p_ref[s].astype(f32)
        o_ref[...] = acc

    return pl.pallas_call(body, grid=(r // tr,), in_specs=[pl.BlockSpec((parts.shape[0], tr, c), lambda i: (0, i, 0))],
                          out_specs=pl.BlockSpec((tr, c), lambda i: (i, 0)),
                          out_shape=jax.ShapeDtypeStruct((r, c), f32),
                          compiler_params=_cparams(("parallel",)), name=name)(parts)


def column_sum(a, name):
    def body(a_ref, o_ref):
        o_ref[...] = jnp.sum(a_ref[...], axis=0, keepdims=True)

    return pl.pallas_call(body, out_shape=jax.ShapeDtypeStruct((1, a.shape[1]), f32), name=name)(a)


def adamw(w, g, m, v, name):
    r, c = w.shape
    tr = r
    for cand in (256, 128, 64, 32, 16, 8):
        if r % cand == 0:
            tr = cand
            break

    def body(w_ref, g_ref, m_ref, v_ref, d_ref, mo_ref, vo_ref):
        gv = g_ref[...]
        m_new = ADAM_B1 * m_ref[...] + (1.0 - ADAM_B1) * gv
        v_new = ADAM_B2 * v_ref[...] + (1.0 - ADAM_B2) * jnp.square(gv)
        m_hat = m_new / (1.0 - ADAM_B1 ** ADAM_STEP)
        v_hat = v_new / (1.0 - ADAM_B2 ** ADAM_STEP)
        d_ref[...] = -ADAM_LR * (m_hat / (jnp.sqrt(v_hat) + ADAM_EPS) + ADAM_WD * w_ref[...])
        mo_ref[...] = m_new
        vo_ref[...] = v_new

    spec = pl.BlockSpec((tr, c), lambda i: (i, 0))
    return pl.pallas_call(body, grid=(r // tr,), in_specs=[spec] * 4, out_specs=[spec] * 3,
                          out_shape=[jax.ShapeDtypeStruct((r, c), f32)] * 3,
                          compiler_params=_cparams(("parallel",)), name=name)(w, g, m, v)


def adaln_fwd(c_rows, w, b):
    def body(c_ref, w_ref, b_ref, o_ref):
        cv = c_ref[...]
        o_ref[...] = _mxu_dot(cv * jax.nn.sigmoid(cv), w_ref[...]) + b_ref[...]

    return pl.pallas_call(body, out_shape=jax.ShapeDtypeStruct((c_rows.shape[0], w.shape[1]), f32),
                          compiler_params=pltpu.CompilerParams(vmem_limit_bytes=VMEM_LIMIT), name="adaln_fwd")(c_rows, w, b)


def adaln_bwd(c_rows, dm, w):
    def body(c_ref, dm_ref, w_ref, gw_ref, ds_ref):
        cv = c_ref[...]
        gw_ref[...] = _dg(cv * jax.nn.sigmoid(cv), dm_ref[...], 0, 0)
        ds_ref[...] = _dg(dm_ref[...], w_ref[...], 1, 1)

    return pl.pallas_call(body, out_shape=[jax.ShapeDtypeStruct(w.shape, f32),
                                           jax.ShapeDtypeStruct(c_rows.shape, f32)],
                          compiler_params=pltpu.CompilerParams(vmem_limit_bytes=VMEM_LIMIT), name="adaln_bwd")(c_rows, dm, w)


def c_ctx_grad(parts, c_ctx_row):
    def body(p_ref, c_ref, o_ref):
        total = p_ref[0, 0:1, :]
        for s in range(1, N_SHARDS):
            total = total + p_ref[s, 0:1, :]
        _, vjp = jax.vjp(jax.nn.silu, c_ref[...])
        o_ref[...] = vjp(total)[0]

    return pl.pallas_call(body, out_shape=jax.ShapeDtypeStruct((1, D_MODEL), f32), name="c_ctx_grad")(parts, c_ctx_row)


PACK_W = 1024
PACK_ROWS = 8


def _pack(arrs):
    pieces, layout, r0 = [], [], 0
    for a in arrs:
        size = math.prod(a.shape)
        rows = -(-size // (PACK_W * PACK_ROWS)) * PACK_ROWS
        pieces.append(jnp.pad(a.reshape(-1).astype(f32), (0, rows * PACK_W - size)).reshape(rows, PACK_W))
        layout.append((r0, rows, a.shape))
        r0 += rows
    return jnp.concatenate(pieces, axis=0), layout


def _unpack(pack, layout, lead=()):
    n_lead = len(lead)
    outs = []
    for r0, rows, shape in layout:
        piece = pack[(slice(None),) * n_lead + (slice(r0, r0 + rows),)].reshape(lead + (-1,))
        outs.append(piece[..., :math.prod(shape)].reshape(lead + tuple(shape)))
    return outs


W_NAMES = ("c_ctx", "w_ada", "b_ada", "norm1_g", "norm2_g", "w_in", "ret_log_decay", "rwkv_shift_mu", "rwkv_w0",
           "rwkv_w_up", "rwkv_a0", "rwkv_a_up", "rwkv_g_up", "rwkv_k_k", "rwkv_k_a", "rwkv_r_k", "rwkv_ln_w",
           "rwkv_ln_b", "w_out", "w_ff1", "b_ff1", "w_ff2", "b_ff2", "final_g")
COL_SHARDED = ("w_in", "w_ff1")
ROW_SHARDED = ("w_out", "w_ff2")
LAST_SHARDED = ("rwkv_shift_mu", "rwkv_w0", "rwkv_w_up", "rwkv_a0", "rwkv_a_up", "rwkv_g_up")
REPLICATED = ("c_ctx", "b_ada", "norm1_g", "norm2_g", "ret_log_decay", "rwkv_k_k", "rwkv_k_a", "rwkv_r_k",
              "rwkv_ln_w", "rwkv_ln_b", "b_ff1", "b_ff2", "final_g")
N_SHARDS = 4


def _train_step(a):
    x, c, ctx, tgt = a["x"], a["c"], a["ctx"], a["loss_target"]
    bsz = x.shape[0]
    mx, my, mc = _mesh_pos()
    shard = 2 * mx + my
    dev = _device_slot()

    (c_all,) = exchange([jnp.pad(c, ((0, PACK_ROWS - bsz), (0, 0)))], True, ALL_PEERS, "gather_c")
    n_ex = N_DEV * bsz
    c_rows = jnp.concatenate([c_all[:, :bsz].reshape(n_ex, D_MODEL), a["c_ctx"][None, :],
                              jnp.zeros((PACK_ROWS - 1, D_MODEL), f32)], axis=0)
    ada_cols = a["w_ada"].shape[-1]
    b_ada_cols = lax.dynamic_slice_in_dim(a["b_ada"], shard * ada_cols, ada_cols, axis=1)
    mod_cols = adaln_fwd(c_rows, a["w_ada"][0], b_ada_cols)

    def own_half(n):
        w = a[n][0].astype(MXU_DTYPE)
        return lax.dynamic_slice_in_dim(w, mc * (w.shape[0] // 2), w.shape[0] // 2, axis=0)

    def whole_weight(n, gth, own):
        per_chip = lax.dynamic_update_index_in_dim(gth, own, dev, 0).reshape(N_SHARDS, -1, gth.shape[-1])
        return (per_chip.transpose(1, 0, 2).reshape(per_chip.shape[1], -1) if n in COL_SHARDED
                else per_chip.reshape(-1, per_chip.shape[-1]))

    small_pack, small_layout = _pack([a[n][0] for n in LAST_SHARDED])
    own_blocks = [mod_cols, own_half("w_in"), small_pack]
    gathered = gather_two_level(own_blocks, "gather_weights")
    late_own = [own_half(n) for n in LATE_WEIGHTS]
    late_started = exchange_start(late_own, True, "gather_late_start")
    mod_own = lax.dynamic_update_index_in_dim(gathered[0], mod_cols, dev, 0)
    mod_all = jnp.stack([mod_own[s] for s in CHIP_SLOTS], axis=1).reshape(c_rows.shape[0], -1)
    mod_all = mod_all + late_started[-1][0, 0]
    mod_x = lax.dynamic_slice_in_dim(mod_all, dev * bsz, bsz, axis=0).reshape(bsz, 6, D_MODEL)
    mod_ctx = mod_all[n_ex].reshape(6, D_MODEL)
    wt = {"w_in": whole_weight("w_in", gathered[1], own_blocks[1])}

    def late_weights(after):
        lands = exchange_wait(late_started, after, True, "gather_late_wait")
        return {n: whole_weight(n, land, own) for n, land, own in zip(LATE_WEIGHTS, lands, late_own)}

    def grad_blocks(n, gw):
        if n in COL_SHARDED:
            gw = gw.reshape(gw.shape[0], N_SHARDS, -1).transpose(1, 0, 2)
        return gw.reshape(N_DEV, -1, gw.shape[-1]).astype(MXU_DTYPE)

    late_sent, last_sent = {}, {}

    def early_grads(late_g):
        late_sent["blocks"] = [grad_blocks(n, late_g[n]) for n in LATE_WEIGHTS]
        late_sent["started"] = exchange_start(late_sent["blocks"], False, "scatter_late_start")
        return late_sent["started"][-1]

    def last_grads(g_w_in, g_small):
        shard_packs = []
        for s in range(N_SHARDS):
            pieces_s = [lax.slice_in_dim(g_small[n], s * a[n].shape[-1], (s + 1) * a[n].shape[-1],
                                         axis=g_small[n].ndim - 1) for n in LAST_SHARDED]
            pack_s, last_sent["layout"] = _pack(pieces_s)
            shard_packs.append(jnp.pad(pack_s, ((0, -pack_s.shape[0] % (2 * PACK_ROWS)), (0, 0))))
        last_sent["blocks"] = [grad_blocks("w_in", g_w_in), jnp.stack(shard_packs).reshape(N_DEV, -1, PACK_W)]
        last_sent["started"] = exchange_start(last_sent["blocks"], False, "scatter_last_start")
        return last_sent["started"][-1]

    small_own = lax.dynamic_update_index_in_dim(gathered[2], small_pack, dev, 0)
    small_by_chip = _unpack(jnp.stack([small_own[s] for s in CHIP_SLOTS]), small_layout, (N_SHARDS,))
    for n, parts in zip(LAST_SHARDED, small_by_chip):
        wt[n] = jnp.concatenate([parts[s] for s in range(N_SHARDS)], axis=-1)
    for n in ("norm1_g", "norm2_g", "rwkv_k_k", "rwkv_k_a", "rwkv_r_k", "rwkv_ln_w", "rwkv_ln_b", "b_ff1", "b_ff2"):
        wt[n] = a[n]
    wt["ret_log_decay"] = a["ret_log_decay"][0]
    wt["final_g"] = a["final_g"][None, :]

    loss, grad_x, g = layer_step(x, ctx, tgt, mod_x, mod_ctx, wt, late_weights, early_grads, last_grads)

    small_names = [n for n in REPLICATED if n not in ("c_ctx", "b_ada")]
    g_pack, g_layout = _pack([jnp.pad(loss, ((0, 0), (0, PACK_W - loss.shape[1])))] + [g[n] for n in small_names]
                             + [g["mod_x"], g["mod_ctx"]])
    (g_packs,) = gather_two_level([g_pack], "gather_small_grads")
    g_packs = lax.dynamic_update_index_in_dim(g_packs, g_pack, dev, 0)
    g_sum = _unpack(sum_slots(g_packs, tuple(range(N_DEV)), "sum_small_grads"), g_layout)
    loss_total = g_sum[0][0, 0]
    grads = dict(zip(small_names, g_sum[1:1 + len(small_names)]))
    dmod_ctx = g_sum[-1].reshape(1, -1)
    dmod_x = _unpack(g_packs, g_layout, (N_DEV,))[-2].reshape(n_ex, -1)
    dmod = jnp.concatenate([dmod_x, dmod_ctx, jnp.zeros((PACK_ROWS - 1, dmod_x.shape[1]), f32)], axis=0)
    grads["b_ada"] = column_sum(dmod, "b_ada_grad")
    dmod_cols = lax.dynamic_slice_in_dim(dmod, shard * ada_cols, ada_cols, axis=1)
    grads["w_ada"], dsilu = adaln_bwd(c_rows, dmod_cols, a["w_ada"][0])

    dsilu_rows = jnp.broadcast_to(jnp.pad(dsilu[n_ex:n_ex + 1], ((0, PACK_ROWS - 1), (0, 0)))[None],
                                  (N_SHARDS, PACK_ROWS, D_MODEL))
    (shares,) = exchange([dsilu_rows], False, CHIP_PEERS, "share_c_ctx_grad", by_chip=True, own=False)
    shares = lax.dynamic_update_index_in_dim(shares, dsilu_rows[0], shard, 0)
    grads["c_ctx"] = c_ctx_grad(shares, a["c_ctx"][None, :])

    scattered, half_sums = ("w_in", "small_shards") + LATE_WEIGHTS, []
    for sent, wait_name, after in ((last_sent, "scatter_last_wait", grads["c_ctx"]),
                                   (late_sent, "scatter_late_wait", grads["c_ctx"])):
        for land, block in zip(exchange_wait(sent["started"], after, False, wait_name), sent["blocks"]):
            land = lax.dynamic_update_index_in_dim(land, lax.dynamic_index_in_dim(block, dev, 0, keepdims=False), dev, 0)
            half_sums.append(sum_slots(land, tuple(range(N_DEV)), f"sum_{scattered[len(half_sums)]}"))
    other_halves = sibling_swap(half_sums, "swap_halves")
    for n, mine, other in zip(scattered, half_sums, other_halves):
        rows = mine.shape[0]
        whole = jnp.zeros((2 * rows, mine.shape[1]), f32)
        whole = lax.dynamic_update_slice_in_dim(whole, mine, mc * rows, axis=0)
        grads[n] = lax.dynamic_update_slice_in_dim(whole, other, (1 - mc) * rows, axis=0)
    grads.update(zip(LAST_SHARDED, _unpack(grads.pop("small_shards"), last_sent["layout"])))

    out_g, out_d, out_m, out_v = {}, {}, {}, {}
    for n in ("w_ada",) + COL_SHARDED + ROW_SHARDED:
        out_g[n] = grads[n].reshape(a[n].shape)
        two_d = lambda z: z.reshape(-1, z.shape[-1])
        d, m, v = adamw(two_d(a[n]), two_d(out_g[n]), two_d(a["m_" + n]), two_d(a["v_" + n]), f"adamw_{n}")
        out_d[n], out_m[n], out_v[n] = d.reshape(a[n].shape), m.reshape(a[n].shape), v.reshape(a[n].shape)
    rest = REPLICATED + LAST_SHARDED
    for n in rest:
        out_g[n] = grads[n].reshape(a[n].shape)
    packs = [_pack([src[n] for n in rest])[0] for src in
             ({n: a[n] for n in rest}, out_g, {n: a["m_" + n] for n in rest}, {n: a["v_" + n] for n in rest})]
    _, rest_layout = _pack([a[n] for n in rest])
    for dst, pack in zip((out_d, out_m, out_v), adamw(*packs, "adamw_small")):
        dst.update(zip(rest, _unpack(pack, rest_layout)))
    return (loss_total, grad_x, *[out_g[n] for n in W_NAMES], *[out_d[n] for n in W_NAMES],
            *[out_m[n] for n in W_NAMES], *[out_v[n] for n in W_NAMES])


def kernel(x, c, ctx, c_ctx, w_ada, b_ada, norm1_g, norm2_g, w_in, ret_log_decay, rwkv_shift_mu, rwkv_w0, rwkv_w_up, rwkv_a0, rwkv_a_up, rwkv_g_up, rwkv_k_k, rwkv_k_a, rwkv_r_k, rwkv_ln_w, rwkv_ln_b, w_out, w_ff1, b_ff1, w_ff2, b_ff2, final_g, loss_target, m_c_ctx, m_w_ada, m_b_ada, m_norm1_g, m_norm2_g, m_w_in, m_ret_log_decay, m_rwkv_shift_mu, m_rwkv_w0, m_rwkv_w_up, m_rwkv_a0, m_rwkv_a_up, m_rwkv_g_up, m_rwkv_k_k, m_rwkv_k_a, m_rwkv_r_k, m_rwkv_ln_w, m_rwkv_ln_b, m_w_out, m_w_ff1, m_b_ff1, m_w_ff2, m_b_ff2, m_final_g, v_c_ctx, v_w_ada, v_b_ada, v_norm1_g, v_norm2_g, v_w_in, v_ret_log_decay, v_rwkv_shift_mu, v_rwkv_w0, v_rwkv_w_up, v_rwkv_a0, v_rwkv_a_up, v_rwkv_g_up, v_rwkv_k_k, v_rwkv_k_a, v_rwkv_r_k, v_rwkv_ln_w, v_rwkv_ln_b, v_w_out, v_w_ff1, v_b_ff1, v_w_ff2, v_b_ff2, v_final_g):
    return _train_step(dict(locals()))
```

```python
import functools
import math

import jax
import jax.numpy as jnp
from jax import lax
from jax.experimental import pallas as pl
from jax.experimental.pallas import tpu as pltpu

f32 = jnp.float32
MXU_DTYPE = jnp.bfloat16

D_MODEL = 1024
RET_W = 512
RET_HEADS = 4
RET_DH = 128
RET_CHUNK = 128
RW_W = 512
RW_N = 64
DECAY_LORA = 64
AAA_LORA = 64
GATE_LORA = 128
LORA_W = DECAY_LORA + AAA_LORA + GATE_LORA
D_FF = 4096
RET_COLS = 4 * RET_W
SHIFT_COLS = 3 * RW_W + LORA_W
IN_COLS = RET_COLS + SHIFT_COLS
GRID_W = 64
ROPE_BASE = 10000.0
NORM_EPS = 1e-6
GN_EPS = 64e-5
W_DECAY_SCALE = math.exp(-0.5)
ADAM_LR, ADAM_B1, ADAM_B2, ADAM_EPS, ADAM_WD, ADAM_STEP = 0.001, 0.9, 0.999, 1e-08, 0.01, 10

TOK_TILE = 256
EW_BWD_PARTS = 2
MATMUL_TILE = 1024
SCAN_CHUNK = 32
SCAN_UNROLL = SCAN_CHUNK
N_DEV = 8
V7X_VMEM_BYTES = 64 * 1024 * 1024
VMEM_LIMIT = V7X_VMEM_BYTES * 7 // 8


def _cparams(sem):
    return pltpu.CompilerParams(dimension_semantics=sem, vmem_limit_bytes=VMEM_LIMIT)


def _tile(n, cap):
    best = None
    for t in range(128, min(n, cap) + 1, 128):
        if n % t == 0:
            best = t
    return best if best is not None else n


def matmul(a, b, mode, name, out_dtype=f32, bias=None, finish=None):
    if mode == "nn":
        (m, k), (k2, n) = a.shape, b.shape
    elif mode == "nt":
        (m, k), (n, k2) = a.shape, b.shape
    else:
        (k, m), (k2, n) = a.shape, b.shape
    assert k == k2, (a.shape, b.shape, mode)
    tm, tn, tk = _tile(m, MATMUL_TILE), _tile(n, MATMUL_TILE), _tile(k, MATMUL_TILE)
    nk = k // tk
    dims = {"nn": ((1,), (0,)), "nt": ((1,), (1,)), "tn": ((0,), (0,))}[mode]

    def body(a_ref, b_ref, *rest):
        o_ref, acc_ref = rest[-2:]
        kk = pl.program_id(2)

        @pl.when(kk == 0)
        def _():
            acc_ref[...] = jnp.zeros_like(acc_ref)

        acc_ref[...] += lax.dot_general(a_ref[...].astype(MXU_DTYPE), b_ref[...].astype(MXU_DTYPE),
                                        (dims, ((), ())), preferred_element_type=f32)

        @pl.when(kk == nk - 1)
        def _():
            res = acc_ref[...]
            if bias is not None:
                res = res + rest[0][...]
            if finish is not None:
                res = finish(res)
            o_ref[...] = res.astype(o_ref.dtype)

    if mode == "nn":
        a_spec = pl.BlockSpec((tm, tk), lambda i, j, q: (i, q))
        b_spec = pl.BlockSpec((tk, tn), lambda i, j, q: (q, j))
    elif mode == "nt":
        a_spec = pl.BlockSpec((tm, tk), lambda i, j, q: (i, q))
        b_spec = pl.BlockSpec((tn, tk), lambda i, j, q: (j, q))
    else:
        a_spec = pl.BlockSpec((tk, tm), lambda i, j, q: (q, i))
        b_spec = pl.BlockSpec((tk, tn), lambda i, j, q: (q, j))
    extra_specs = [] if bias is None else [pl.BlockSpec((1, tn), lambda i, j, q: (0, j))]
    extra = [] if bias is None else [bias]
    return pl.pallas_call(
        body, grid=(m // tm, n // tn, nk), in_specs=[a_spec, b_spec] + extra_specs,
        out_specs=pl.BlockSpec((tm, tn), lambda i, j, q: (i, j)),
        out_shape=jax.ShapeDtypeStruct((m, n), out_dtype),
        scratch_shapes=[pltpu.VMEM((tm, tn), f32)],
        compiler_params=_cparams(("parallel", "parallel", "arbitrary")), name=name)(a, b, *extra)


class Tiled:
    def __init__(self, arr, w=None, cidx=0, toff=0):
        self.arr, self.w, self.cidx, self.toff = arr, (arr.shape[-1] if w is None else w), cidx, toff

    def spec(self):
        cidx, toff = self.cidx, self.toff
        return pl.BlockSpec((None, TOK_TILE, self.w), lambda b, i: (b, jnp.maximum(i + toff, 0), cidx))


class Seg:
    def __init__(self, arr, seg, first):
        self.arr, self.seg, self.first = arr, seg, first

    def spec(self):
        seg = self.seg
        return pl.BlockSpec((None, None, 1, self.arr.shape[-1]), lambda b, i: (b, seg(i), 0, 0))


class Glob:
    def __init__(self, arr):
        self.arr = arr

    def spec(self):
        return pl.BlockSpec(self.arr.shape, lambda b, i: (0,) * self.arr.ndim)


def ew_forward(fn, name, bsz, n_tiles, ins, outs):
    n_in = len(ins)

    def body(*refs):
        res = fn(*[r[...] for r in refs[:n_in]])
        for o_ref, o in zip(refs[n_in:], res):
            o_ref[...] = o.astype(o_ref.dtype)

    out_specs = [pl.BlockSpec((None, TOK_TILE, w), lambda b, i: (b, i, 0)) for w, _ in outs]
    out_shape = [jax.ShapeDtypeStruct((bsz, n_tiles * TOK_TILE, w), dt) for w, dt in outs]
    return pl.pallas_call(body, grid=(bsz, n_tiles), in_specs=[d.spec() for d in ins], out_specs=out_specs,
                          out_shape=out_shape, compiler_params=_cparams(("parallel", "parallel")), name=name)(
        *[d.arr for d in ins])


def ew_backward(fn, name, bsz, n_tiles, ins, cts, want, grad_dtypes=None, lead=0):
    ct_parts = [c if isinstance(c, tuple) else (c,) for c in cts]
    cts = [part for parts in ct_parts for part in parts]
    n_in, n_ct = len(ins), len(cts)
    diff = [k for k in range(n_in) if want[k]]
    grad_dtypes = grad_dtypes or {}
    assert lead == 0 or not any(isinstance(ins[k], Seg) for k in diff)

    def body(*refs):
        b, i = pl.program_id(0), pl.program_id(1)
        g_refs = refs[n_in + n_ct:]

        def tile_grads():
            for part in range(EW_BWD_PARTS):
                part_grads(slice(part * TOK_TILE // EW_BWD_PARTS, (part + 1) * TOK_TILE // EW_BWD_PARTS), part == 0)

        def part_grads(rows, first_part):
            vals = [r[rows, :] if isinstance(d, Tiled) else r[...] for r, d in zip(refs[:n_in], ins)]
            ct_refs = iter(refs[n_in:n_in + n_ct])
            ct_vals = tuple(functools.reduce(lambda s, t: s + t, [next(ct_refs)[rows, :].astype(f32) for _ in parts])
                            for parts in ct_parts)

            def f(*dvals):
                full = list(vals)
                for k, v in zip(diff, dvals):
                    full[k] = v
                return tuple(fn(*full))

            _, vjp = jax.vjp(f, *[vals[k] for k in diff])
            grads = vjp(ct_vals)
            for k, g_ref, g in zip(diff, g_refs, grads):
                d = ins[k]
                if isinstance(d, Tiled):
                    g_ref[rows, :] = g.astype(g_ref.dtype)
                else:
                    if first_part:
                        zero = d.first(i) if isinstance(d, Seg) else jnp.logical_and(b == 0, i == lead)

                        @pl.when(zero)
                        def _(g_ref=g_ref):
                            g_ref[...] = jnp.zeros_like(g_ref)

                    g_ref[...] += g

        if lead == 0:
            tile_grads()
        else:
            pl.when(i >= lead)(tile_grads)

            @pl.when(i < lead)
            def _():
                for k, g_ref in zip(diff, g_refs):
                    if isinstance(ins[k], Tiled):
                        g_ref[...] = jnp.zeros_like(g_ref)

    out_specs, out_shape = [], []
    for k in diff:
        d = ins[k]
        if isinstance(d, Tiled):
            out_specs.append(pl.BlockSpec((None, TOK_TILE, d.w), lambda b, i: (b, i, 0)))
            out_shape.append(jax.ShapeDtypeStruct((bsz, (n_tiles + lead) * TOK_TILE, d.w), grad_dtypes.get(k, f32)))
        else:
            out_specs.append(d.spec())
            out_shape.append(jax.ShapeDtypeStruct(d.arr.shape, f32))
    return pl.pallas_call(body, grid=(bsz, n_tiles + lead),
                          in_specs=[d.spec() for d in ins] + [c.spec() for c in cts],
                          out_specs=out_specs, out_shape=out_shape,
                          compiler_params=_cparams(("arbitrary", "arbitrary")), name=name)(
        *[d.arr for d in ins], *[c.arr for c in cts])


@jax.custom_vjp
def _mxu_dot(a, b):
    return jnp.dot(a.astype(MXU_DTYPE), b.astype(MXU_DTYPE), preferred_element_type=f32)


def _mxu_dot_fwd(a, b):
    return _mxu_dot(a, b), (a, b)


def _mxu_dot_bwd(res, ct):
    a, b = res
    ct = ct.astype(MXU_DTYPE)
    da = lax.dot_general(ct, b.astype(MXU_DTYPE), (((1,), (1,)), ((), ())), preferred_element_type=f32)
    db = lax.dot_general(a.astype(MXU_DTYPE), ct, (((0,), (0,)), ((), ())), preferred_element_type=f32)
    return da, db


_mxu_dot.defvjp(_mxu_dot_fwd, _mxu_dot_bwd)


def _split_dot_impl(x, ones_mat):
    hi = x.astype(MXU_DTYPE)
    lo = (x - hi.astype(f32)).astype(MXU_DTYPE)
    return jnp.dot(hi, ones_mat, preferred_element_type=f32) + jnp.dot(lo, ones_mat, preferred_element_type=f32)


@jax.custom_vjp
def _split_dot(x, ones_mat):
    return _split_dot_impl(x, ones_mat)


def _split_dot_fwd(x, ones_mat):
    return _split_dot_impl(x, ones_mat), ones_mat


def _split_dot_bwd(ones_mat, ct):
    return _split_dot_impl(ct, ones_mat), None


_split_dot.defvjp(_split_dot_fwd, _split_dot_bwd)


def _block_ones(n, group):
    idx = jnp.arange(n) // group
    return (idx[:, None] == idx[None, :]).astype(MXU_DTYPE)


def _rms(x, g):
    return x * lax.rsqrt(jnp.mean(x * x, axis=-1, keepdims=True) + NORM_EPS) * g


def fn_norm_mod(h, shift, scale, g):
    return (_rms(h, g) * (1.0 + scale) + shift,)


def fn_rwkv_prepare(ks, lora, w0_f, w0_b, a0_f, a0_b, w_up_f, w_up_b, a_up_f, a_up_b, g_up, k_k, k_a, ones64):
    kkr = ks * k_k
    kk = kkr * lax.rsqrt(_split_dot(kkr * kkr, ones64) + 1e-12)
    outs = [kk]
    th = jnp.tanh(lora)
    for w0, a0, w_up, a_up in ((w0_f, a0_f, w_up_f, a_up_f), (w0_b, a0_b, w_up_b, a_up_b)):
        w = jnp.exp(-W_DECAY_SCALE * jax.nn.sigmoid(w0 + _mxu_dot(th, w_up)))
        a = jax.nn.sigmoid(a0 + _mxu_dot(lora, a_up))
        kt = ks * (1.0 + (a - 1.0) * k_a)
        outs += [w, a * kk, kt]
    outs.append(_mxu_dot(jax.nn.sigmoid(lora), g_up))
    return tuple(outs)


def fn_merge(o_f, o_b, g_ret, y_f, y_b, r, kt_f, v, g_rw, r_k, ln_w, ln_b, ones64, ones128):
    o = o_f + o_b
    ret = o * lax.rsqrt(_split_dot(o * o, ones128) * (1.0 / RET_DH) + NORM_EPS) * (g_ret * jax.nn.sigmoid(g_ret))
    y = y_f + y_b
    mean = _split_dot(y, ones64) * (1.0 / RW_N)
    yc = y - mean
    var = _split_dot(yc * yc, ones64) * (1.0 / RW_N)
    y_n = yc * lax.rsqrt(var + GN_EPS) * ln_w + ln_b
    bonus = _split_dot(r * kt_f * r_k, ones64) * v
    return ret, (y_n + bonus) * g_rw


def fn_resid_norm_mod(x, mix, gate, shift, scale, g):
    h1 = x + gate * mix
    return h1, _rms(h1, g) * (1.0 + scale) + shift


def relu2(z):
    return jnp.square(jnp.maximum(z, 0.0))


def relu2_backward(act, dact, name):
    bsz, n_tok, width = act.shape

    def body(a_ref, d_ref, du_ref, db_ref):
        du = d_ref[...].astype(f32) * (2.0 * jnp.sqrt(a_ref[...].astype(f32)))
        du_ref[...] = du.astype(du_ref.dtype)

        @pl.when(jnp.logical_and(pl.program_id(0) == 0, pl.program_id(1) == 0))
        def _():
            db_ref[...] = jnp.zeros_like(db_ref)

        db_ref[...] += jnp.sum(du, axis=0, keepdims=True)

    tile = pl.BlockSpec((None, TOK_TILE, width), lambda b, i: (b, i, 0))
    row = pl.BlockSpec((1, width), lambda b, i: (0, 0))
    return pl.pallas_call(body, grid=(bsz, n_tok // TOK_TILE), in_specs=[tile, tile], out_specs=[tile, row],
                          out_shape=[jax.ShapeDtypeStruct(act.shape, MXU_DTYPE), jax.ShapeDtypeStruct((1, width), f32)],
                          compiler_params=_cparams(("arbitrary", "arbitrary")), name=name)(act, dact)


def fn_loss(h1, f, tgt, gate, b2, g):
    y = _rms(h1 + gate * (f + b2), g)
    err = jnp.square(y - tgt)
    return 0.5 * jnp.sum(jnp.mean(err, axis=-1, keepdims=True), axis=0, keepdims=True)


def loss_and_grads(h1, f, tgt, gate, b2, g, bsz, n_tiles):
    def body(h1_ref, f_ref, t_ref, gate_ref, b2_ref, g_ref, loss_ref, dh1_ref, df_ref, dgate_ref, db2_ref, dg_ref):
        b, i = pl.program_id(0), pl.program_id(1)
        tgt_v = t_ref[...]
        loss, vjp = jax.vjp(lambda a, c, e, p, q: fn_loss(a, c, tgt_v, e, p, q),
                            h1_ref[...], f_ref[...], gate_ref[...], b2_ref[...], g_ref[...])
        dh1, df, dgate, db2, dg = vjp(jnp.ones((1, 1), f32))
        dh1_ref[...] = dh1
        df_ref[...] = df.astype(df_ref.dtype)

        @pl.when(i == 0)
        def _():
            dgate_ref[...] = jnp.zeros_like(dgate_ref)

        @pl.when(jnp.logical_and(b == 0, i == 0))
        def _():
            loss_ref[...] = jnp.zeros_like(loss_ref)
            db2_ref[...] = jnp.zeros_like(db2_ref)
            dg_ref[...] = jnp.zeros_like(dg_ref)

        dgate_ref[...] += dgate
        db2_ref[...] += db2
        dg_ref[...] += dg
        loss_ref[...] += jnp.broadcast_to(loss, loss_ref.shape)

    tile = pl.BlockSpec((None, TOK_TILE, D_MODEL), lambda b, i: (b, i, 0))
    row = pl.BlockSpec((1, D_MODEL), lambda b, i: (0, 0))
    seg = pl.BlockSpec((None, None, 1, D_MODEL), lambda b, i: (b, 0, 0, 0))
    t_tok = n_tiles * TOK_TILE
    return pl.pallas_call(
        body, grid=(bsz, n_tiles), in_specs=[tile, tile, tile, seg, row, row],
        out_specs=[pl.BlockSpec((1, 128), lambda b, i: (0, 0)), tile, tile, seg, row, row],
        out_shape=[jax.ShapeDtypeStruct((1, 128), f32), jax.ShapeDtypeStruct((bsz, t_tok, D_MODEL), f32),
                   jax.ShapeDtypeStruct((bsz, t_tok, D_MODEL), MXU_DTYPE),
                   jax.ShapeDtypeStruct((bsz, 1, 1, D_MODEL), f32),
                   jax.ShapeDtypeStruct((1, D_MODEL), f32), jax.ShapeDtypeStruct((1, D_MODEL), f32)],
        compiler_params=_cparams(("arbitrary", "arbitrary")), name="loss_and_grads")(h1, f, tgt, gate, b2, g)


SHIFT_BLOCK = SHIFT_COLS
HALO_ROWS = 8


def _shift_specs(n_tok, col0, width=SHIFT_BLOCK):
    per_tile = TOK_TILE // HALO_ROWS
    last = n_tok // HALO_ROWS - 1
    tile = pl.BlockSpec((None, TOK_TILE, width), lambda j, b, i: (b, i, col0 + j))
    prev = pl.BlockSpec((None, HALO_ROWS, width),
                        lambda j, b, i: (b, jnp.maximum(i * per_tile - 1, 0), col0 + j))
    nxt = pl.BlockSpec((None, HALO_ROWS, width),
                       lambda j, b, i: (b, jnp.minimum((i + 1) * per_tile, last), col0 + j))
    return tile, prev, nxt


def _shift_specs_at(n_tok, col):
    one, halo, cols = pl.Element(1), pl.Element(HALO_ROWS), pl.Element(SHIFT_COLS)
    tile = pl.BlockSpec((one, pl.Element(TOK_TILE), cols),
                        lambda j, b, i: (b, pl.multiple_of(i * TOK_TILE, TOK_TILE), col))
    prev = pl.BlockSpec((one, halo, cols), lambda j, b, i: (
        b, pl.multiple_of(jnp.maximum(i * TOK_TILE - HALO_ROWS, 0), HALO_ROWS), col))
    nxt = pl.BlockSpec((one, halo, cols), lambda j, b, i: (
        b, pl.multiple_of(jnp.minimum((i + 1) * TOK_TILE, n_tok - HALO_ROWS), HALO_ROWS), col))
    return tile, prev, nxt


def _shifted(p, prev_ref, next_ref, is_first, is_last):
    row = lax.broadcasted_iota(jnp.int32, p.shape, 0)
    prev_row = jnp.where(is_first, 0.0, prev_ref[HALO_ROWS - 1:HALO_ROWS, :].astype(f32))
    next_row = jnp.where(is_last, 0.0, next_ref[0:1, :].astype(f32))
    prev = jnp.where(row == 0, prev_row, pltpu.roll(p, 1, axis=0))
    nxt = jnp.where(row == TOK_TILE - 1, next_row, pltpu.roll(p, TOK_TILE - 1, axis=0))
    return prev, nxt


def token_shift(px, mu, seg_first, seg_last):
    bsz, n_tok, _ = px.shape
    n_tiles = n_tok // TOK_TILE
    assert SHIFT_BLOCK == SHIFT_COLS

    def body(p_ref, prev_ref, next_ref, mu_ref, o_ref):
        i = pl.program_id(2)
        p = p_ref[0]
        prev, nxt = _shifted(p, prev_ref[0], next_ref[0], seg_first(i), seg_last(i))
        o_ref[...] = p + mu_ref[0:1, :] * (prev - p) + mu_ref[1:2, :] * (nxt - p)

    tile, prev, nxt = _shift_specs_at(n_tok, RET_COLS)
    return pl.pallas_call(
        body, grid=(SHIFT_COLS // SHIFT_BLOCK, bsz, n_tiles),
        in_specs=[tile, prev, nxt, pl.BlockSpec((2, SHIFT_BLOCK), lambda j, b, i: (0, j))],
        out_specs=pl.BlockSpec((None, TOK_TILE, SHIFT_BLOCK), lambda j, b, i: (b, i, j)),
        out_shape=jax.ShapeDtypeStruct((bsz, n_tok, SHIFT_COLS), f32),
        compiler_params=_cparams(("parallel", "parallel", "parallel")), name="token_shift")(px, px, px, mu)


def token_shift_bwd(d_sections, px, mu, seg_first, seg_last):
    bsz, n_tok, _ = px.shape
    n_tiles = n_tok // TOK_TILE
    d_arrs = [part for section in d_sections for part in section]
    assert SHIFT_BLOCK == SHIFT_COLS == sum(section[0].shape[-1] for section in d_sections)

    def body(*refs):
        d_refs, (p_ref, prev_ref, next_ref, mu_ref, dp_ref, dmu_ref) = refs[:3 * len(d_arrs)], refs[3 * len(d_arrs):]
        b, i = pl.program_id(1), pl.program_id(2)
        first, last = seg_first(i), seg_last(i)

        def summed(which):
            part_refs = iter(d_refs[which::3])
            return jnp.concatenate([functools.reduce(lambda s, t: s + t, [next(part_refs)[...] for _ in section])
                                    for section in d_sections], axis=-1)

        d, p = summed(0), p_ref[0]
        d_prev, d_next = _shifted(d, summed(1), summed(2), first, last)
        p_prev, p_next = _shifted(p, prev_ref[0], next_ref[0], first, last)
        mu0, mu1 = mu_ref[0:1, :], mu_ref[1:2, :]
        dp_ref[...] = (d + mu0 * (d_next - d) + mu1 * (d_prev - d)).astype(dp_ref.dtype)

        @pl.when(jnp.logical_and(b == 0, i == 0))
        def _():
            dmu_ref[...] = jnp.zeros_like(dmu_ref)

        dmu_ref[0:1, :] += jnp.sum(d * (p_prev - p), axis=0, keepdims=True)
        dmu_ref[1:2, :] += jnp.sum(d * (p_next - p), axis=0, keepdims=True)

    d_specs = [spec for arr in d_arrs for spec in _shift_specs(n_tok, 0, arr.shape[-1])]
    tile, prev, nxt = _shift_specs_at(n_tok, RET_COLS)
    mu_spec = pl.BlockSpec((2, SHIFT_BLOCK), lambda j, b, i: (0, j))
    return pl.pallas_call(
        body, grid=(SHIFT_COLS // SHIFT_BLOCK, bsz, n_tiles),
        in_specs=d_specs + [tile, prev, nxt, mu_spec],
        out_specs=[pl.BlockSpec((None, TOK_TILE, SHIFT_BLOCK), lambda j, b, i: (b, i, j)), mu_spec],
        out_shape=[jax.ShapeDtypeStruct((bsz, n_tok, SHIFT_COLS), MXU_DTYPE),
                   jax.ShapeDtypeStruct((2, SHIFT_COLS), f32)],
        compiler_params=_cparams(("arbitrary", "arbitrary", "arbitrary")), name="token_shift_bwd")(
        *[arr for arr in d_arrs for _ in range(3)], px, px, px, mu)


def _dg(a, b, ca, cb):
    return lax.dot_general(a.astype(MXU_DTYPE), b.astype(MXU_DTYPE), (((ca,), (cb,)), ((), ())),
                           preferred_element_type=f32)


@jax.custom_vjp
def _mm_nt(a, b):
    return _dg(a, b, 1, 1)


_mm_nt.defvjp(lambda a, b: (_dg(a, b, 1, 1), (a, b)),
              lambda res, ct: (_dg(ct, res[1], 1, 0), _dg(ct, res[0], 0, 0)))


@jax.custom_vjp
def _mm_tn(a, b):
    return _dg(a, b, 0, 0)


_mm_tn.defvjp(lambda a, b: (_dg(a, b, 0, 0), (a, b)),
              lambda res, ct: (_dg(res[1], ct, 1, 1), _dg(res[0], ct, 1, 0)))


ROTARY_PAIR = RET_DH // 4


def _swap_pairs_impl(t):
    lane = lax.broadcasted_iota(jnp.int32, t.shape, 1)
    return jnp.where(lane % (2 * ROTARY_PAIR) < ROTARY_PAIR, pltpu.roll(t, RET_DH - ROTARY_PAIR, axis=1),
                     pltpu.roll(t, ROTARY_PAIR, axis=1))


@jax.custom_vjp
def _swap_pairs(t):
    return _swap_pairs_impl(t)


_swap_pairs.defvjp(lambda t: (_swap_pairs_impl(t), None), lambda _, ct: (_swap_pairs_impl(ct),))


def _ret_chunk(state, q_raw, k_raw, v, cos, sin, ld_row, head, reverse):
    c = RET_CHUNK
    lane = lax.broadcasted_iota(jnp.int32, ld_row.shape, 1)
    lg = -jnp.exp(jnp.sum(jnp.where(lane == head, ld_row, 0.0), axis=-1, keepdims=True))
    rot = lambda t: t * cos + _swap_pairs(t) * sin
    q = rot(q_raw)
    k = rot(k_raw) * (RET_DH ** -0.5)
    ti = lax.broadcasted_iota(jnp.int32, (c, 1), 0).astype(f32)
    tj = lax.broadcasted_iota(jnp.int32, (1, c), 1).astype(f32)
    if not reverse:
        dist, mask, q_exp, k_exp = ti - tj, (ti - tj) >= 0, ti + 1.0, c - 1.0 - ti
    else:
        dist, mask, q_exp, k_exp = tj - ti, (tj - ti) > 0, c - ti, ti
    decay = jnp.where(mask, jnp.exp(lg * jnp.maximum(dist, 0.0)), 0.0)
    scores = _mm_nt(q, k) * decay
    out = _mxu_dot(scores, v) + _mxu_dot(q * jnp.exp(lg * q_exp), state)
    new_state = state * jnp.exp(lg * c) + _mm_tn(k * jnp.exp(lg * k_exp), v)
    return out, new_state


def _ret_specs(bsz, order):
    tok = lambda col=0: pl.BlockSpec((bsz, RET_CHUNK, RET_W), lambda i: (0, order(i), col))
    tab = pl.BlockSpec((RET_CHUNK, RET_DH), lambda i: (order(i), 0))
    ld = pl.BlockSpec((1, RET_DH), lambda i: (0, 0))
    return tok, tab, ld


def retention_fwd(px, cos, sin, ld_row, order, reverse, name):
    bsz, n_tok, _ = px.shape
    n_ch = n_tok // RET_CHUNK

    def body(q_ref, k_ref, v_ref, cos_ref, sin_ref, ld_ref, o_ref, sv_ref, st_ref):
        @pl.when(pl.program_id(0) == 0)
        def _():
            st_ref[...] = jnp.zeros_like(st_ref)

        for b in range(bsz):
            for h in range(RET_HEADS):
                sl = slice(h * RET_DH, (h + 1) * RET_DH)
                s = st_ref[b, h]
                sv_ref[b, h] = s
                o, s_new = _ret_chunk(s, q_ref[b, :, sl], k_ref[b, :, sl], v_ref[b, :, sl], cos_ref[...], sin_ref[...],
                                      ld_ref[...], h, reverse)
                o_ref[b, :, sl] = o
                st_ref[b, h] = s_new

    tok, tab, ld = _ret_specs(bsz, order)
    return pl.pallas_call(
        body, grid=(n_ch,), in_specs=[tok(0), tok(1), tok(2), tab, tab, ld],
        out_specs=[tok(), pl.BlockSpec((bsz, None, RET_HEADS, RET_DH, RET_DH), lambda i: (0, i, 0, 0, 0))],
        out_shape=[jax.ShapeDtypeStruct((bsz, n_tok, RET_W), f32),
                   jax.ShapeDtypeStruct((bsz, n_ch, RET_HEADS, RET_DH, RET_DH), f32)],
        scratch_shapes=[pltpu.VMEM((bsz, RET_HEADS, RET_DH, RET_DH), f32)],
        compiler_params=_cparams(("arbitrary",)), name=name)(px, px, px, cos, sin, ld_row)


def retention_bwd(do, px, states, cos, sin, ld_row, order, reverse, name, add_to=()):
    bsz, n_tok, _ = px.shape
    n_ch = n_tok // RET_CHUNK
    back = lambda i: order(n_ch - 1 - i)

    def body(do_ref, q_ref, k_ref, v_ref, sv_ref, cos_ref, sin_ref, ld_ref, *rest):
        add_refs, (dq_ref, dk_ref, dv_ref, dld_ref, dst_ref) = rest[:-5] or (None,) * 3, rest[-5:]

        @pl.when(pl.program_id(0) == 0)
        def _():
            dst_ref[...] = jnp.zeros_like(dst_ref)
            dld_ref[...] = jnp.zeros_like(dld_ref)

        cos_v, sin_v = cos_ref[...], sin_ref[...]
        for b in range(bsz):
            for h in range(RET_HEADS):
                sl = slice(h * RET_DH, (h + 1) * RET_DH)
                f = lambda s, q, k, v, ld, h=h: _ret_chunk(s, q, k, v, cos_v, sin_v, ld, h, reverse)
                _, vjp = jax.vjp(f, sv_ref[b, h], q_ref[b, :, sl], k_ref[b, :, sl], v_ref[b, :, sl], ld_ref[...])
                ds, dq, dk, dv, dld = vjp((do_ref[b, :, sl], dst_ref[b, h]))
                dst_ref[b, h] = ds
                for o_ref, add_ref, val in zip((dq_ref, dk_ref, dv_ref), add_refs, (dq, dk, dv)):
                    if add_ref is not None:
                        val = add_ref[b, :, sl] + val
                    o_ref[b, :, sl] = val.astype(o_ref.dtype)
                dld_ref[...] += dld

    tok, tab, ld = _ret_specs(bsz, back)
    return pl.pallas_call(
        body, grid=(n_ch,),
        in_specs=[tok(), tok(0), tok(1), tok(2),
                  pl.BlockSpec((bsz, None, RET_HEADS, RET_DH, RET_DH), lambda i: (0, n_ch - 1 - i, 0, 0, 0)),
                  tab, tab, ld] + [tok() for _ in add_to],
        out_specs=[tok(), tok(), tok(), ld],
        out_shape=[jax.ShapeDtypeStruct((bsz, n_tok, RET_W), MXU_DTYPE if add_to else f32)] * 3
        + [jax.ShapeDtypeStruct((1, RET_DH), f32)],
        scratch_shapes=[pltpu.VMEM((bsz, RET_HEADS, RET_DH, RET_DH), f32)],
        compiler_params=_cparams(("arbitrary",)), name=name)(
        do, px, px, px, states, cos, sin, ld_row, *add_to)


HALF_W = RW_W // 2


def _head_sum(x, ones):
    xm = x.astype(MXU_DTYPE)
    return jnp.concatenate([jnp.dot(xm[:, :HALF_W], ones, preferred_element_type=f32),
                            jnp.dot(xm[:, HALF_W:], ones, preferred_element_type=f32)], axis=1)


def _stack(parts):
    return jnp.concatenate(parts, axis=0)


def _row(ref, b, t):
    return ref[b, pl.ds(t, 1), :]


SCAN_DIRS = ((False, True), (True, False))
RW_HEADS = RW_W // RW_N
HEAD_ROWS_PAD = 16


def _head_rows(row, mask):
    return jnp.broadcast_to(row, mask.shape) * mask


def _outer(per_value, row, mask_pad):
    return lax.dot_general(per_value.astype(MXU_DTYPE), _head_rows(row, mask_pad).astype(MXU_DTYPE),
                           (((0,), (0,)), ((), ())), preferred_element_type=f32)


def _read(states, rows, mask, more_rows=()):
    lhs = _stack([_head_rows(r, mask) for r in list(rows) + list(more_rows)])
    return lax.dot_general(lhs.astype(MXU_DTYPE), _stack(states).astype(MXU_DTYPE), (((1,), (1,)), ((), ())),
                           preferred_element_type=f32)


def _own_block(raw, b):
    lanes = raw[:, RW_N * b:RW_N * (b + 1)]
    turned = _stack([lanes[RW_HEADS * b:], lanes[:RW_HEADS * b]]) if b else lanes
    if turned.shape[0] < HEAD_ROWS_PAD:
        turned = _stack([turned, jnp.zeros((HEAD_ROWS_PAD - turned.shape[0], RW_N), f32)])
    return turned[:HEAD_ROWS_PAD]


def _row_from_heads(per_value, state, mask_pad):
    full = jnp.dot(per_value.astype(MXU_DTYPE), state.astype(MXU_DTYPE), preferred_element_type=f32)
    return jnp.sum(full * mask_pad, axis=0, keepdims=True)


def _scan_specs(bsz, order):
    rows = lambda col=0: pl.BlockSpec((bsz, SCAN_CHUNK, RW_W), lambda i: (0, order(i), col))
    per_value = pl.BlockSpec((bsz, SCAN_CHUNK, HEAD_ROWS_PAD, RW_N), lambda i: (0, order(i), 0, 0))
    states = pl.BlockSpec((SCAN_CHUNK, bsz, RW_N, RW_W), lambda i: (order(i), 0, 0, 0))
    blocks = pl.BlockSpec((SCAN_CHUNK, RW_HEADS * bsz, RW_N * bsz), lambda i: (order(i), 0, 0))
    return rows, per_value, states, blocks


def _mxu_operands(states):
    return [s.astype(MXU_DTYPE) for s in states]


def _removed(states_m, kk_t, ones, bsz):
    removed = _head_sum(_stack([states_m[b] * kk_t[b].astype(MXU_DTYPE) for b in range(bsz)]), ones)
    return [removed[b * RW_N:(b + 1) * RW_N] for b in range(bsz)]


def _advance(sp, rem, w_t, b_t, vk, bsz):
    return [sp[b] * w_t[b] - rem[b] * b_t[b] + vk[b] for b in range(bsz)]


def heads_to_rows(a):
    b, t, _ = a.shape
    return jnp.pad(a.astype(MXU_DTYPE).reshape(b, t, RW_HEADS, RW_N),
                   ((0, 0), (0, 0), (0, HEAD_ROWS_PAD - RW_HEADS), (0, 0)))


def _blocks_to_rows(raw_ref, first, row_ref, bsz):
    steps = pl.ds(first, SCAN_CHUNK)
    for b in range(bsz):
        for h in range(RW_HEADS):
            row_ref[b, :, h * RW_N:(h + 1) * RW_N] = raw_ref[steps, RW_HEADS * b + h, RW_N * b:RW_N * (b + 1)]


N_ROWS_FWD = 5
N_ROWS_BWD = 5


def _scan_consts(bsz):
    head = (jnp.arange(RW_W)[None, :] // RW_N == jnp.arange(RW_HEADS)[:, None]).astype(f32)
    return head, jnp.pad(head, ((0, HEAD_ROWS_PAD - RW_HEADS), (0, 0))), _block_ones(HALF_W, RW_N)


def _const_specs(consts):
    return [pl.BlockSpec(c.shape, lambda i: (0, 0)) for c in consts]


def rwkv_scan_fwd(rows_in, v_heads, orders, name):
    bsz, n_tok, _ = rows_in[0][0][0].shape
    n_ch = n_tok // SCAN_CHUNK
    rng = range(bsz)
    consts = _scan_consts(bsz)

    def body(*refs):
        rows = [refs[:N_ROWS_FWD], refs[N_ROWS_FWD:2 * N_ROWS_FWD]]
        (v0, v1, head_ref, pad_ref, ones_ref, y0, y1, h0, h1, f0, f1, m0, m1, s0, s1, late_ref,
         raw_ref) = refs[2 * N_ROWS_FWD:]
        v_refs, y_refs, hist_refs, final_refs, s_refs = (v0, v1), (y0, y1), (h0, h1), (f0, f1), (s0, s1)
        removed_refs = (m0, m1)
        n_blk = RW_HEADS * bsz
        head_v, pad_v, ones_v = head_ref[...], pad_ref[...], ones_ref[...]
        for d in range(2):
            @pl.when(pl.program_id(0) == 0)
            def _(d=d):
                s_refs[d][...] = jnp.zeros_like(s_refs[d])

        def step(j, carry):
            ts = [SCAN_CHUNK - 1 - j if reverse else j for reverse, _ in SCAN_DIRS]
            sps = [[s_refs[d][b] for b in rng] for d in range(2)]
            sps_m = [_mxu_operands(sps[d]) for d in range(2)]
            vks = [[_outer(v_refs[d][b, ts[d]], _row(rows[d][4], b, ts[d]), pad_v) for b in rng] for d in range(2)]
            rems = [_removed(sps_m[d], [_row(rows[d][1], b, ts[d]) for b in rng], ones_v, bsz) for d in range(2)]
            for d, (reverse, inclusive) in enumerate(SCAN_DIRS):
                r_ref = rows[d][0]
                read_at = jnp.maximum(j - 1, 0) if inclusive else ts[d]
                both = _read(sps_m[d], [_row(r_ref, b, read_at) for b in rng], head_v,
                             [_row(rows[d][1], b, ts[d]) for b in rng])
                if inclusive:
                    late_ref[j] = both[:n_blk]
                else:
                    raw_ref[ts[d]] = both[:n_blk]
                removed_refs[d][ts[d]] = both[n_blk:]
            for d in range(2):
                new = _advance(sps[d], rems[d], [_row(rows[d][2], b, ts[d]) for b in rng],
                               [_row(rows[d][3], b, ts[d]) for b in rng], vks[d], bsz)
                for b in rng:
                    hist_refs[d][ts[d], b] = sps_m[d][b]
                    s_refs[d][b] = new[b]
            return carry

        lax.fori_loop(0, SCAN_CHUNK, step, 0, unroll=SCAN_UNROLL)
        for d, (reverse, inclusive) in enumerate(SCAN_DIRS):
            final_refs[d][...] = s_refs[d][...]
            if inclusive:
                assert not reverse
                last = SCAN_CHUNK - 1
                late_ref[SCAN_CHUNK] = _read(_mxu_operands([s_refs[d][b] for b in rng]),
                                             [rows[d][0][b, last:last + 1, :] for b in rng], head_v)
                _blocks_to_rows(late_ref, 1, y_refs[d], bsz)
            else:
                _blocks_to_rows(raw_ref, 0, y_refs[d], bsz)

    specs = [_scan_specs(bsz, orders[d]) for d in range(2)]
    state = pltpu.VMEM((bsz, RW_N, RW_W), f32)
    late = pltpu.VMEM((SCAN_CHUNK + 1, RW_HEADS * bsz, RW_N * bsz), f32)
    raw = pltpu.VMEM((SCAN_CHUNK, RW_HEADS * bsz, RW_N * bsz), f32)
    final_spec = pl.BlockSpec((bsz, RW_N, RW_W), lambda i: (0, 0, 0))
    return pl.pallas_call(
        body, grid=(n_ch,),
        in_specs=[specs[d][0](col) for d in range(2) for _, col in rows_in[d]] + [specs[0][1], specs[1][1]]
        + _const_specs(consts),
        out_specs=[specs[0][0](), specs[1][0](), specs[0][2], specs[1][2], final_spec, final_spec,
                   specs[0][3], specs[1][3]],
        out_shape=[jax.ShapeDtypeStruct((bsz, n_tok, RW_W), f32)] * 2
        + [jax.ShapeDtypeStruct((n_tok, bsz, RW_N, RW_W), MXU_DTYPE)] * 2
        + [jax.ShapeDtypeStruct((bsz, RW_N, RW_W), f32)] * 2
        + [jax.ShapeDtypeStruct((n_tok, RW_HEADS * bsz, RW_N * bsz), f32)] * 2,
        scratch_shapes=[state, state, late, raw],
        compiler_params=_cparams(("arbitrary",)), name=name)(
        *[a for d in range(2) for a, _ in rows_in[d]], v_heads, v_heads, *consts)


def rwkv_scan_bwd(rows_in, v_heads, dy_heads, hists, finals, removed, orders, name):
    bsz, n_tok, _ = rows_in[0][0][0].shape
    n_ch = n_tok // SCAN_CHUNK
    backs = [functools.partial(lambda i, order: order(n_ch - 1 - i), order=orders[d]) for d in range(2)]
    rng = range(bsz)
    consts = _scan_consts(bsz)
    n_out, n_scr = 6, 6

    def body(*refs):
        rows = [refs[:N_ROWS_BWD], refs[N_ROWS_BWD:2 * N_ROWS_BWD]]
        rest = refs[2 * N_ROWS_BWD:]
        v_refs, dy_refs, hist_refs, final_refs, removed_refs = rest[0:2], rest[2:4], rest[4:6], rest[6:8], rest[8:10]
        head_ref, pad_ref, ones_ref = rest[10:13]
        outs = [rest[13:13 + n_out], rest[13 + n_out:13 + 2 * n_out]]
        scr = [rest[13 + 2 * n_out:13 + 2 * n_out + n_scr], rest[13 + 2 * n_out + n_scr:]]
        n_blk = RW_HEADS * bsz
        head_v, pad_v, ones_v = head_ref[...], pad_ref[...], ones_ref[...]
        for d in range(2):
            @pl.when(pl.program_id(0) == 0)
            def _(d=d):
                scr[d][1][...] = jnp.zeros_like(scr[d][1])
                scr[d][0][...] = final_refs[d][...]

        def step_of(j, reverse):
            return j if reverse else SCAN_CHUNK - 1 - j

        for d, (reverse, _) in enumerate(SCAN_DIRS):
            t0 = step_of(0, reverse)
            for b in rng:
                scr[d][3][b] = _outer(dy_refs[d][b, t0], rows[d][0][b, t0:t0 + 1, :], pad_v)

        def bstep(j, carry):
            ts = [step_of(j, reverse) for reverse, _ in SCAN_DIRS]
            reads = [[scr[d][3][b] for b in rng] for d in range(2)]
            dss = []
            for d, (_, inclusive) in enumerate(SCAN_DIRS):
                ds = [scr[d][1][b] for b in rng]
                dss.append([ds[b] + reads[d][b] for b in rng] if inclusive else ds)
            dss_m = [_mxu_operands(dss[d]) for d in range(2)]
            nexts = []
            for d, (reverse, _) in enumerate(SCAN_DIRS):
                t_next = step_of(jnp.minimum(j + 1, SCAN_CHUNK - 1), reverse)
                nexts.append([_outer(dy_refs[d][b, t_next], _row(rows[d][0], b, t_next), pad_v) for b in rng])
            drems = [_removed(dss_m[d], [-_row(rows[d][3], b, ts[d]) for b in rng], ones_v, bsz) for d in range(2)]
            for d in range(2):
                for b in rng:
                    scr[d][3][b] = nexts[d][b]
                both = _read(dss_m[d], [_row(rows[d][4], b, ts[d]) for b in rng], head_v,
                             [-_row(rows[d][3], b, ts[d]) for b in rng])
                scr[d][4][ts[d]] = both[:n_blk]
                scr[d][5][ts[d]] = both[n_blk:]
            for d, (_, inclusive) in enumerate(SCAN_DIRS):
                _, kk_ref, w_ref, _, _ = rows[d]
                _, ds_ref, dsh_ref = scr[d][:3]
                for b in rng:
                    dsh_ref[ts[d], b] = dss[d][b]
                    dsp = dss[d][b] * _row(w_ref, b, ts[d]) + drems[d][b] * _row(kk_ref, b, ts[d])
                    ds_ref[b] = dsp if inclusive else dsp + reads[d][b]
            return carry

        lax.fori_loop(0, SCAN_CHUNK, bstep, 0, unroll=SCAN_UNROLL)

        rsum = lambda z: jnp.sum(z, axis=0, keepdims=True)
        for d, (reverse, inclusive) in enumerate(SCAN_DIRS):
            dr_ref, dkk_ref, dw_ref, db_ref, dkt_ref, dv_ref = outs[d]
            after_ref, _, dsh_ref, _, dv_raw_ref, dremt_ref = scr[d]
            hist_ref, removed_ref = hist_refs[d], removed_refs[d]
            _blocks_to_rows(dv_raw_ref, 0, dv_ref, bsz)
            for t in range(SCAN_CHUNK):
                ts = slice(t, t + 1)
                after = t - 1 if reverse else t + 1
                for b in rng:
                    sp_m, ds = hist_ref[t, b], dsh_ref[t, b]
                    sp = sp_m.astype(f32)
                    if not inclusive:
                        seen = sp_m
                    else:
                        seen = hist_ref[after, b] if 0 <= after < SCAN_CHUNK else after_ref[b]
                    dr_ref[b, ts, :] = _row_from_heads(dy_refs[d][b, t], seen, pad_v)
                    dkt_ref[b, ts, :] = _row_from_heads(v_refs[d][b, t], ds, pad_v)
                    dw_ref[b, ts, :] = rsum(ds * sp)
                    db_ref[b, ts, :] = -_row_from_heads(_own_block(removed_ref[t], b), ds, pad_v)
                    dkk_ref[b, ts, :] = _row_from_heads(_own_block(dremt_ref[t], b), sp_m, pad_v)
            if inclusive:
                first = SCAN_CHUNK - 1 if reverse else 0
                for b in rng:
                    after_ref[b] = hist_ref[first, b].astype(f32)

    specs = [_scan_specs(bsz, backs[d]) for d in range(2)]
    hist = pltpu.VMEM((SCAN_CHUNK, bsz, RW_N, RW_W), f32)
    state = pltpu.VMEM((bsz, RW_N, RW_W), f32)
    final_spec = pl.BlockSpec((bsz, RW_N, RW_W), lambda i: (0, 0, 0))
    raw = pltpu.VMEM((SCAN_CHUNK, RW_HEADS * bsz, RW_N * bsz), f32)
    return pl.pallas_call(
        body, grid=(n_ch,),
        in_specs=[specs[d][0](col) for d in range(2) for _, col in rows_in[d]]
        + [specs[0][1], specs[1][1]] * 2 + [specs[0][2], specs[1][2], final_spec, final_spec, specs[0][3], specs[1][3]]
        + _const_specs(consts),
        out_specs=[specs[d][0]() for d in range(2) for _ in range(n_out)],
        out_shape=[jax.ShapeDtypeStruct((bsz, n_tok, RW_W), f32)] * (2 * n_out),
        scratch_shapes=[state, state, hist, state, raw, raw] * 2,
        compiler_params=_cparams(("arbitrary",)), name=name)(
        *[a for d in range(2) for a, _ in rows_in[d]], v_heads, v_heads, dy_heads, dy_heads, *hists, *finals, *removed, *consts)


MOD_NAMES = ("shift1", "scale1", "gate1", "shift2", "scale2", "gate2")


def _rope_tables(t_ctx, t_x):
    quarter = RET_DH // 4
    pos = jnp.arange(t_x)
    inv = jnp.power(ROPE_BASE, -jnp.arange(0, 2 * quarter, 2, dtype=f32) / (2 * quarter))
    ang_r = (pos // GRID_W).astype(f32)[:, None] * inv[None, :]
    ang_c = (pos % GRID_W).astype(f32)[:, None] * inv[None, :]
    cos = jnp.concatenate([jnp.cos(ang_r)] * 2 + [jnp.cos(ang_c)] * 2, axis=1)
    sin = jnp.concatenate([-jnp.sin(ang_r), jnp.sin(ang_r), -jnp.sin(ang_c), jnp.sin(ang_c)], axis=1)
    cos = jnp.concatenate([jnp.ones((t_ctx, RET_DH), f32), cos], axis=0)
    sin = jnp.concatenate([jnp.zeros((t_ctx, RET_DH), f32), sin], axis=0)
    return cos, sin


def _pad_rows(w, lo, total):
    return jnp.pad(w, ((lo, total - lo - w.shape[0]), (0, 0)))


LATE_WEIGHTS = ("w_out", "w_ff1", "w_ff2")


def layer_step(x, ctx, tgt, mod_x, mod_ctx, wt, late_weights=None, early_grads=None, last_grads=None):
    bsz, t_x, _ = x.shape
    t_c = ctx.shape[1]
    t_all = t_c + t_x
    n_ct, n_xt = t_c // TOK_TILE, t_x // TOK_TILE
    n_t = n_ct + n_xt
    assert t_c % TOK_TILE == 0 and t_x % TOK_TILE == 0 and t_c % RET_CHUNK == 0

    seg = lambda i: (i >= n_ct).astype(jnp.int32)
    seg_first = lambda i: jnp.logical_or(i == 0, i == n_ct)
    seg_last = lambda i: jnp.logical_or(i == n_ct - 1, i == n_t - 1)
    mod_all = {n: jnp.stack([jnp.broadcast_to(mod_ctx[k], (bsz, D_MODEL)), mod_x[:, k]], axis=1)[:, :, None, :]
               for k, n in enumerate(MOD_NAMES)}
    mod_lat = {n: mod_x[:, k][:, None, None, :] for k, n in enumerate(MOD_NAMES)}
    both = lambda n: Seg(mod_all[n], seg, seg_first)
    lat = lambda n: Seg(mod_lat[n], lambda i: 0, lambda i: i == 0)
    flat = lambda a: a.reshape(-1, a.shape[-1])

    def chunk_orders(n_ctx_chunks, n_chunks):
        fwd = lambda i: i
        bwd = lambda i: jnp.where(i < n_ctx_chunks, n_ctx_chunks - 1 - i, n_chunks + n_ctx_chunks - 1 - i)
        return fwd, bwd

    ones64, ones128 = _block_ones(RW_W, RW_N), _block_ones(RET_W, RET_DH)
    cos, sin = _rope_tables(t_c, t_x)
    ld_rows = [jnp.pad(wt["ret_log_decay"][d][None, :], ((0, 0), (0, RET_DH - RET_HEADS))) for d in range(2)]
    w_up_pad = [_pad_rows(wt["rwkv_w_up"][d], 0, LORA_W) for d in range(2)]
    a_up_pad = [_pad_rows(wt["rwkv_a_up"][d], DECAY_LORA, LORA_W) for d in range(2)]
    g_up_pad = _pad_rows(wt["rwkv_g_up"], DECAY_LORA + AAA_LORA, LORA_W)
    row = lambda a, d: a[d][None, :]

    h = jnp.concatenate([ctx, x], axis=1)
    norm1_ins = lambda: [Tiled(h), both("shift1"), both("scale1"), Glob(wt["norm1_g"])]
    (n1,) = ew_forward(fn_norm_mod, "norm1", bsz, n_t, norm1_ins(), [(D_MODEL, MXU_DTYPE)])
    px = matmul(flat(n1), wt["w_in"], "nn", "proj_in").reshape(bsz, t_all, IN_COLS)
    ps = token_shift(px, wt["rwkv_shift_mu"], seg_first, seg_last)

    def prep_ins():
        return [Tiled(ps, RW_W, 1), Tiled(ps, LORA_W, 3 * RW_W // LORA_W),
                Glob(row(wt["rwkv_w0"], 0)), Glob(row(wt["rwkv_w0"], 1)),
                Glob(row(wt["rwkv_a0"], 0)), Glob(row(wt["rwkv_a0"], 1)),
                Glob(w_up_pad[0]), Glob(w_up_pad[1]), Glob(a_up_pad[0]), Glob(a_up_pad[1]), Glob(g_up_pad),
                Glob(wt["rwkv_k_k"]), Glob(wt["rwkv_k_a"]), Glob(ones64)]

    kk, w_f, b_f, kt_f, w_b, b_b, kt_b, g_rw = ew_forward(fn_rwkv_prepare, "rwkv_prepare", bsz, n_t, prep_ins(),
                                                           [(RW_W, f32)] * 8)
    rw_order = chunk_orders(t_c // SCAN_CHUNK, t_all // SCAN_CHUNK)
    ret_order = chunk_orders(t_c // RET_CHUNK, t_all // RET_CHUNK)
    scan_rows = [[(ps, 0), (kk, 0), (w_f, 0), (b_f, 0), (kt_f, 0)], [(ps, 0), (kk, 0), (w_b, 0), (b_b, 0), (kt_b, 0)]]
    v_heads = heads_to_rows(ps[..., 2 * RW_W:3 * RW_W])
    y_f, y_b, *kept_states = rwkv_scan_fwd(scan_rows, v_heads, rw_order, "rwkv_scan_fwd")
    y = [y_f, y_b]
    o, ret_states = [], []
    for d in range(2):
        o_d, st_d = retention_fwd(px, cos, sin, ld_rows[d], ret_order[d], SCAN_DIRS[d][0], f"retention_fwd{d}")
        o.append(o_d), ret_states.append(st_d)

    def merge_ins(toff):
        return [Tiled(o[0], toff=toff), Tiled(o[1], toff=toff), Tiled(px, RET_W, 3, toff),
                Tiled(y[0], toff=toff), Tiled(y[1], toff=toff), Tiled(ps, RW_W, 0, toff), Tiled(kt_f, toff=toff),
                Tiled(ps, RW_W, 2, toff), Tiled(g_rw, toff=toff),
                Glob(wt["rwkv_r_k"]), Glob(wt["rwkv_ln_w"]), Glob(wt["rwkv_ln_b"]), Glob(ones64), Glob(ones128)]

    ret_out, rw_out = ew_forward(fn_merge, "merge_heads", bsz, n_xt, merge_ins(n_ct),
                                 [(RET_W, MXU_DTYPE), (RW_W, MXU_DTYPE)])
    merged = jnp.concatenate([ret_out, rw_out], axis=-1)
    if late_weights is not None:
        wt = {**wt, **late_weights(merged)}
    mix = matmul(flat(merged), wt["w_out"], "nn", "proj_out").reshape(bsz, t_x, D_MODEL)
    resid_ins = lambda: [Tiled(x), Tiled(mix), lat("gate1"), lat("shift2"), lat("scale2"), Glob(wt["norm2_g"])]
    h1, n2 = ew_forward(fn_resid_norm_mod, "resid_norm2", bsz, n_xt, resid_ins(), [(D_MODEL, f32), (D_MODEL, MXU_DTYPE)])
    act = matmul(flat(n2), wt["w_ff1"], "nn", "ff1", MXU_DTYPE, wt["b_ff1"], relu2).reshape(bsz, t_x, D_FF)
    ff = matmul(flat(act), wt["w_ff2"], "nn", "ff2").reshape(bsz, t_x, D_MODEL)

    g = {}
    loss, dh1, dff, dgate2, g["b_ff2"], g["final_g"] = loss_and_grads(
        h1, ff, tgt, mod_lat["gate2"], wt["b_ff2"], wt["final_g"], bsz, n_xt)
    dact = matmul(flat(dff), wt["w_ff2"], "nt", "ff2_dx", MXU_DTYPE).reshape(bsz, t_x, D_FF)
    g["w_ff2"] = matmul(flat(act), flat(dff), "tn", "ff2_dw", MXU_DTYPE)
    du, g["b_ff1"] = relu2_backward(act, dact, "relu2_bwd")
    dn2 = matmul(flat(du), wt["w_ff1"], "nt", "ff1_dx").reshape(bsz, t_x, D_MODEL)
    g["w_ff1"] = matmul(flat(n2), flat(du), "tn", "ff1_dw", MXU_DTYPE)
    dx_res, dmix, dgate1, dshift2, dscale2, g["norm2_g"] = ew_backward(
        fn_resid_norm_mod, "resid_norm2_bwd", bsz, n_xt, resid_ins(), [Tiled(dh1), Tiled(dn2)], [True] * 6,
        {1: MXU_DTYPE})
    dmerged = matmul(flat(dmix), wt["w_out"], "nt", "proj_out_dx").reshape(bsz, t_x, D_MODEL)
    g["w_out"] = matmul(flat(merged), flat(dmix), "tn", "proj_out_dw", MXU_DTYPE)
    if early_grads is not None:
        token = early_grads({n: g.pop(n) for n in LATE_WEIGHTS})
        wt = {**wt, "rwkv_r_k": wt["rwkv_r_k"] + token[:1, :1]}
    (do, dg_ret, dy, dr_m, dkt_m, dv_m, dg_rw, g["rwkv_r_k"], g["rwkv_ln_w"], g["rwkv_ln_b"]) = ew_backward(
        fn_merge, "merge_heads_bwd", bsz, n_xt, merge_ins(0),
        [Tiled(dmerged, RET_W, 0, -n_ct), Tiled(dmerged, RW_W, 1, -n_ct)],
        [True, False, True, True, False, True, True, True, True, True, True, True, False, False], lead=n_ct)

    dqkv, dld = (), []
    for d in range(2):
        *dqkv, dld_d = retention_bwd(do, px, ret_states[d], cos, sin, ld_rows[d], ret_order[d],
                                     SCAN_DIRS[d][0], f"retention_bwd{d}", add_to=dqkv)
        dld.append(dld_d[0, :RET_HEADS])
    g["ret_log_decay"] = jnp.stack(dld)
    (dr_f, dkk_f, dw_f, db_f, dkt_f, dv_f, dr_b, dkk_b, dw_b, db_b, dkt_b, dv_b) = rwkv_scan_bwd(
        scan_rows, v_heads, heads_to_rows(dy), kept_states[:2], kept_states[2:4], kept_states[4:], rw_order, "rwkv_scan_bwd")
    prep_cts = [(dkk_f, dkk_b), dw_f, db_f, (dkt_f, dkt_m), dw_b, db_b, dkt_b, dg_rw]
    (dks, dlora, dw0_f, dw0_b, da0_f, da0_b, dwup_f, dwup_b, daup_f, daup_b, dgup, g["rwkv_k_k"],
     g["rwkv_k_a"]) = ew_backward(fn_rwkv_prepare, "rwkv_prepare_bwd", bsz, n_t, prep_ins(),
                                  [tuple(map(Tiled, c)) if isinstance(c, tuple) else Tiled(c) for c in prep_cts],
                                  [True] * 13 + [False])
    g["rwkv_w0"] = jnp.concatenate([dw0_f, dw0_b], axis=0)
    g["rwkv_a0"] = jnp.concatenate([da0_f, da0_b], axis=0)
    g["rwkv_w_up"] = jnp.stack([dwup_f[:DECAY_LORA], dwup_b[:DECAY_LORA]])
    g["rwkv_a_up"] = jnp.stack([daup_f[DECAY_LORA:DECAY_LORA + AAA_LORA], daup_b[DECAY_LORA:DECAY_LORA + AAA_LORA]])
    g["rwkv_g_up"] = dgup[DECAY_LORA + AAA_LORA:]
    dp_rw, g["rwkv_shift_mu"] = token_shift_bwd([(dr_f, dr_b, dr_m), (dks,), (dv_f, dv_b, dv_m), (dlora,)], px,
                                                 wt["rwkv_shift_mu"], seg_first, seg_last)
    dpx = jnp.concatenate(dqkv + [dg_ret.astype(MXU_DTYPE), dp_rw], axis=-1)
    g["w_in"] = matmul(flat(n1), flat(dpx), "tn", "proj_in_dw", MXU_DTYPE)
    after_start = None
    if last_grads is not None:
        token = last_grads(g.pop("w_in"), {n: g.pop(n) for n in LAST_SHARDED})
        after_start = jnp.zeros((1, D_MODEL), f32) + token[:1, :1]
    dn1 = matmul(flat(dpx), wt["w_in"], "nt", "proj_in_dx", bias=after_start).reshape(bsz, t_all, D_MODEL)
    dh, dshift1, dscale1, g["norm1_g"] = ew_backward(fn_norm_mod, "norm1_bwd", bsz, n_t, norm1_ins(), [Tiled(dn1)],
                                                     [True] * 4)
    grad_x = dh[:, t_c:] + dx_res
    zeros = jnp.zeros((D_MODEL,), f32)
    g["mod_x"] = jnp.stack([dshift1[:, 1, 0], dscale1[:, 1, 0], dgate1[:, 0, 0], dshift2[:, 0, 0], dscale2[:, 0, 0],
                            dgate2[:, 0, 0]], axis=1)
    g["mod_ctx"] = jnp.stack([dshift1[:, 0, 0].sum(0), dscale1[:, 0, 0].sum(0), zeros, zeros, zeros, zeros])
    return loss, grad_x, g


MESH_ID = pl.DeviceIdType.MESH
ALL_PEERS = [(dx, dy, dc) for dx in (0, 1) for dy in (0, 1) for dc in (0, 1)][1:]
CHIP_PEERS = [(1, 0, 0), (0, 1, 0), (1, 1, 0)]
CHIP_SLOTS = (0, 2, 4, 6)


def _mesh_pos():
    return lax.axis_index("x"), lax.axis_index("y"), lax.axis_index("c")


def _device_slot():
    x, y, c = _mesh_pos()
    return 4 * x + 2 * y + c


def sibling_swap(arrs, name):
    n = len(arrs)

    def body(*refs):
        in_refs, out_refs = refs[:n], refs[n:2 * n]
        send_sems, recv_sems = refs[2 * n:]
        x, y, c = _mesh_pos()
        copies = [pltpu.make_async_remote_copy(src_ref=in_refs[a], dst_ref=out_refs[a], send_sem=send_sems.at[a],
                                               recv_sem=recv_sems.at[a], device_id=(x, y, 1 - c),
                                               device_id_type=MESH_ID) for a in range(n)]
        for cp in copies:
            cp.start()
        for cp in copies:
            cp.wait()

    any_spec = pl.BlockSpec(memory_space=pl.ANY)
    res = pl.pallas_call(
        body, in_specs=[any_spec] * n, out_specs=[any_spec] * n,
        out_shape=[jax.ShapeDtypeStruct(a.shape, a.dtype) for a in arrs],
        scratch_shapes=[pltpu.SemaphoreType.DMA((n,)), pltpu.SemaphoreType.DMA((n,))],
        name=name)(*arrs)
    return list(res)


def exchange(arrs, gather, peers, name, by_chip=False, own=True):
    n, n_peers = len(arrs), len(peers)
    n_slots = N_SHARDS if by_chip else N_DEV
    slot = (lambda x, y, c: 2 * x + y) if by_chip else (lambda x, y, c: 4 * x + 2 * y + c)

    def body(*refs):
        in_refs, out_refs = refs[:n], refs[n:2 * n]
        send_sems, recv_sems, local_sems = refs[2 * n:]
        x, y, c = _mesh_pos()
        me = slot(x, y, c)
        copies, locals_ = [], []
        for a in range(n):
            if own:
                mine = in_refs[a] if gather else in_refs[a].at[me]
                loc = pltpu.make_async_copy(mine, out_refs[a].at[me], local_sems.at[a])
                loc.start()
                locals_.append(loc)
            for k, (dx, dy, dc) in enumerate(peers):
                peer = (1 - x if dx else x, 1 - y if dy else y, 1 - c if dc else c)
                src = in_refs[a] if gather else in_refs[a].at[slot(*peer)]
                sem = a * n_peers + k
                cp = pltpu.make_async_remote_copy(src_ref=src, dst_ref=out_refs[a].at[me], send_sem=send_sems.at[sem],
                                                  recv_sem=recv_sems.at[sem], device_id=peer, device_id_type=MESH_ID)
                cp.start()
                copies.append(cp)
        for cp in copies:
            cp.wait()
        for loc in locals_:
            loc.wait()

    any_spec = pl.BlockSpec(memory_space=pl.ANY)
    out_shape = [jax.ShapeDtypeStruct((n_slots,) + (a.shape if gather else a.shape[1:]), a.dtype) for a in arrs]
    n_sems = n * n_peers
    res = pl.pallas_call(
        body, in_specs=[any_spec] * n, out_specs=[any_spec] * n, out_shape=out_shape,
        scratch_shapes=[pltpu.SemaphoreType.DMA((n_sems,)), pltpu.SemaphoreType.DMA((n_sems,)),
                        pltpu.SemaphoreType.DMA((n,))],
        name=name)(*arrs)
    return list(res)


HBM_SPEC = pl.BlockSpec(memory_space=pltpu.HBM)
SEM_SPEC = pl.BlockSpec(memory_space=pltpu.SEMAPHORE)
DATAFLOW = pltpu.SideEffectType.DATAFLOW_SIDE_EFFECTING


def _peer_copies(src_refs, land_refs, send_sems, recv_sems, gather):
    x, y, c = _mesh_pos()
    me = 4 * x + 2 * y + c
    copies = []
    for a, (src_ref, land_ref) in enumerate(zip(src_refs, land_refs)):
        for k, (dx, dy, dc) in enumerate(ALL_PEERS):
            peer = (1 - x if dx else x, 1 - y if dy else y, 1 - c if dc else c)
            src = src_ref if gather else src_ref.at[4 * peer[0] + 2 * peer[1] + peer[2]]
            sem = a * len(ALL_PEERS) + k
            copies.append(pltpu.make_async_remote_copy(src_ref=src, dst_ref=land_ref.at[me], send_sem=send_sems.at[sem],
                                                       recv_sem=recv_sems.at[sem], device_id=peer,
                                                       device_id_type=MESH_ID))
    return copies


def exchange_start(arrs, gather, name):
    n = len(arrs)
    lands = [lax.empty((N_DEV,) + (a.shape if gather else a.shape[1:]), a.dtype) for a in arrs]

    def body(*refs):
        for cp in _peer_copies(refs[:n], refs[n:2 * n], refs[2 * n], refs[2 * n + 1], gather):
            cp.start()
        refs[-1][...] = jnp.zeros_like(refs[-1])

    sems = pltpu.SemaphoreType.DMA((n * len(ALL_PEERS),))
    hbm = [pltpu.HBM(a.shape, a.dtype) for a in arrs + lands]
    res = pl.pallas_call(
        body, name=name, out_shape=(sems, sems, *hbm, jax.ShapeDtypeStruct((8, 128), f32)),
        in_specs=[HBM_SPEC] * (2 * n),
        out_specs=(SEM_SPEC, SEM_SPEC, *[HBM_SPEC] * (2 * n), pl.BlockSpec(memory_space=pltpu.VMEM)),
        input_output_aliases={i: 2 + i for i in range(2 * n)},
        compiler_params=pltpu.CompilerParams(has_side_effects=DATAFLOW))(
        *[pltpu.with_memory_space_constraint(a, pltpu.HBM) for a in arrs + lands])
    return res[0], res[1], list(res[2:2 + n]), list(res[2 + n:2 + 2 * n]), res[-1]


def exchange_wait(started, after, gather, name):
    send_sems, recv_sems, srcs, lands, _ = started
    n = len(srcs)

    def body(*refs):
        for cp in _peer_copies(refs[:n], refs[n:2 * n], refs[2 * n], refs[2 * n + 1], gather):
            cp.wait_send()
            cp.wait_recv()

    res = pl.pallas_call(
        body, name=name, out_shape=tuple(pltpu.HBM(a.shape, a.dtype) for a in srcs + lands),
        in_specs=[HBM_SPEC] * (2 * n) + [SEM_SPEC, SEM_SPEC, pl.BlockSpec(memory_space=pl.ANY)],
        out_specs=tuple([HBM_SPEC] * (2 * n)), input_output_aliases={i: i for i in range(2 * n)},
        compiler_params=pltpu.CompilerParams(has_side_effects=DATAFLOW))(*srcs, *lands, send_sems, recv_sems, after)
    return list(res[n:])


def gather_two_level(arrs, name):
    n = len(arrs)
    per = 7

    def body(*refs):
        in_refs, out_refs = refs[:n], refs[n:2 * n]
        send_sems, recv_sems = refs[2 * n:]
        x, y, c = _mesh_pos()
        me, sibling = (x, y, c), (x, y, 1 - c)
        chips = [(1 - x, y), (x, 1 - y), (1 - x, 1 - y)]

        def copy(a, k, block, to, src=None):
            rows = out_refs[a].at[4 * block[0] + 2 * block[1] + block[2]]
            return pltpu.make_async_remote_copy(src_ref=rows if src is None else src, dst_ref=rows,
                                                send_sem=send_sems.at[a * per + k], recv_sem=recv_sems.at[a * per + k],
                                                device_id=to, device_id_type=MESH_ID)

        first, passed = [], []
        for a in range(n):
            first.append(copy(a, 0, me, sibling, src=in_refs[a]))
            first += [copy(a, 1 + j, me, (*chip, c), src=in_refs[a]) for j, chip in enumerate(chips)]
        for cp in first:
            cp.start()
        for a in range(n):
            for j, chip in enumerate(chips):
                copy(a, 1 + j, (*chip, c), me).wait_recv()
                fwd = copy(a, 4 + j, (*chip, c), sibling)
                fwd.start()
                passed.append(fwd)
        for a in range(n):
            copy(a, 0, sibling, me).wait_recv()
            for j, chip in enumerate(chips):
                copy(a, 4 + j, (*chip, 1 - c), me).wait_recv()
        for cp in first + passed:
            cp.wait_send()

    any_spec = pl.BlockSpec(memory_space=pl.ANY)
    res = pl.pallas_call(
        body, in_specs=[any_spec] * n, out_specs=[any_spec] * n,
        out_shape=[jax.ShapeDtypeStruct((N_DEV,) + a.shape, a.dtype) for a in arrs],
        scratch_shapes=[pltpu.SemaphoreType.DMA((n * per,)), pltpu.SemaphoreType.DMA((n * per,))],
        name=name)(*arrs)
    return list(res)


def sum_slots(parts, slots, name):
    _, r, c = parts.shape
    tr = r
    for cand in (512, 256, 128, 64, 32, 16, 8):
        if r % cand == 0 and cand * c * 4 * len(slots) <= 8 * 1024 * 1024:
            tr = cand
            break

    def body(p_ref, o_ref):
        acc = p_ref[slots[0]].astype(f32)
        for s in slots[1:]:
            acc = acc + p_ref[s].astype(f32)
        o_ref[...] = acc

    return pl.pallas_call(body, grid=(r // tr,), in_specs=[pl.BlockSpec((parts.shape[0], tr, c), lambda i: (0, i, 0))],
                          out_specs=pl.BlockSpec((tr, c), lambda i: (i, 0)),
                          out_shape=jax.ShapeDtypeStruct((r, c), f32),
                          compiler_params=_cparams(("parallel",)), name=name)(parts)


def column_sum(a, name):
    def body(a_ref, o_ref):
        o_ref[...] = jnp.sum(a_ref[...], axis=0, keepdims=True)

    return pl.pallas_call(body, out_shape=jax.ShapeDtypeStruct((1, a.shape[1]), f32), name=name)(a)


def adamw(w, g, m, v, name):
    r, c = w.shape
    tr = r
    for cand in (256, 128, 64, 32, 16, 8):
        if r % cand == 0:
            tr = cand
            break

    def body(w_ref, g_ref, m_ref, v_ref, d_ref, mo_ref, vo_ref):
        gv = g_ref[...]
        m_new = ADAM_B1 * m_ref[...] + (1.0 - ADAM_B1) * gv
        v_new = ADAM_B2 * v_ref[...] + (1.0 - ADAM_B2) * jnp.square(gv)
        m_hat = m_new / (1.0 - ADAM_B1 ** ADAM_STEP)
        v_hat = v_new / (1.0 - ADAM_B2 ** ADAM_STEP)
        d_ref[...] = -ADAM_LR * (m_hat / (jnp.sqrt(v_hat) + ADAM_EPS) + ADAM_WD * w_ref[...])
        mo_ref[...] = m_new
        vo_ref[...] = v_new

    spec = pl.BlockSpec((tr, c), lambda i: (i, 0))
    return pl.pallas_call(body, grid=(r // tr,), in_specs=[spec] * 4, out_specs=[spec] * 3,
                          out_shape=[jax.ShapeDtypeStruct((r, c), f32)] * 3,
                          compiler_params=_cparams(("parallel",)), name=name)(w, g, m, v)


def adaln_fwd(c_rows, w, b):
    def body(c_ref, w_ref, b_ref, o_ref):
        cv = c_ref[...]
        o_ref[...] = _mxu_dot(cv * jax.nn.sigmoid(cv), w_ref[...]) + b_ref[...]

    return pl.pallas_call(body, out_shape=jax.ShapeDtypeStruct((c_rows.shape[0], w.shape[1]), f32),
                          compiler_params=pltpu.CompilerParams(vmem_limit_bytes=VMEM_LIMIT), name="adaln_fwd")(c_rows, w, b)


def adaln_bwd(c_rows, dm, w):
    def body(c_ref, dm_ref, w_ref, gw_ref, ds_ref):
        cv = c_ref[...]
        gw_ref[...] = _dg(cv * jax.nn.sigmoid(cv), dm_ref[...], 0, 0)
        ds_ref[...] = _dg(dm_ref[...], w_ref[...], 1, 1)

    return pl.pallas_call(body, out_shape=[jax.ShapeDtypeStruct(w.shape, f32),
                                           jax.ShapeDtypeStruct(c_rows.shape, f32)],
                          compiler_params=pltpu.CompilerParams(vmem_limit_bytes=VMEM_LIMIT), name="adaln_bwd")(c_rows, dm, w)


def c_ctx_grad(parts, c_ctx_row):
    def body(p_ref, c_ref, o_ref):
        total = p_ref[0, 0:1, :]
        for s in range(1, N_SHARDS):
            total = total + p_ref[s, 0:1, :]
        _, vjp = jax.vjp(jax.nn.silu, c_ref[...])
        o_ref[...] = vjp(total)[0]

    return pl.pallas_call(body, out_shape=jax.ShapeDtypeStruct((1, D_MODEL), f32), name="c_ctx_grad")(parts, c_ctx_row)


PACK_W = 1024
PACK_ROWS = 8


def _pack(arrs):
    pieces, layout, r0 = [], [], 0
    for a in arrs:
        size = math.prod(a.shape)
        rows = -(-size // (PACK_W * PACK_ROWS)) * PACK_ROWS
        pieces.append(jnp.pad(a.reshape(-1).astype(f32), (0, rows * PACK_W - size)).reshape(rows, PACK_W))
        layout.append((r0, rows, a.shape))
        r0 += rows
    return jnp.concatenate(pieces, axis=0), layout


def _unpack(pack, layout, lead=()):
    n_lead = len(lead)
    outs = []
    for r0, rows, shape in layout:
        piece = pack[(slice(None),) * n_lead + (slice(r0, r0 + rows),)].reshape(lead + (-1,))
        outs.append(piece[..., :math.prod(shape)].reshape(lead + tuple(shape)))
    return outs


W_NAMES = ("c_ctx", "w_ada", "b_ada", "norm1_g", "norm2_g", "w_in", "ret_log_decay", "rwkv_shift_mu", "rwkv_w0",
           "rwkv_w_up", "rwkv_a0", "rwkv_a_up", "rwkv_g_up", "rwkv_k_k", "rwkv_k_a", "rwkv_r_k", "rwkv_ln_w",
           "rwkv_ln_b", "w_out", "w_ff1", "b_ff1", "w_ff2", "b_ff2", "final_g")
COL_SHARDED = ("w_in", "w_ff1")
ROW_SHARDED = ("w_out", "w_ff2")
LAST_SHARDED = ("rwkv_shift_mu", "rwkv_w0", "rwkv_w_up", "rwkv_a0", "rwkv_a_up", "rwkv_g_up")
REPLICATED = ("c_ctx", "b_ada", "norm1_g", "norm2_g", "ret_log_decay", "rwkv_k_k", "rwkv_k_a", "rwkv_r_k",
              "rwkv_ln_w", "rwkv_ln_b", "b_ff1", "b_ff2", "final_g")
N_SHARDS = 4


def _train_step(a):
    x, c, ctx, tgt = a["x"], a["c"], a["ctx"], a["loss_target"]
    bsz = x.shape[0]
    mx, my, mc = _mesh_pos()
    shard = 2 * mx + my
    dev = _device_slot()

    (c_all,) = exchange([jnp.pad(c, ((0, PACK_ROWS - bsz), (0, 0)))], True, ALL_PEERS, "gather_c")
    n_ex = N_DEV * bsz
    c_rows = jnp.concatenate([c_all[:, :bsz].reshape(n_ex, D_MODEL), a["c_ctx"][None, :],
                              jnp.zeros((PACK_ROWS - 1, D_MODEL), f32)], axis=0)
    ada_cols = a["w_ada"].shape[-1]
    b_ada_cols = lax.dynamic_slice_in_dim(a["b_ada"], shard * ada_cols, ada_cols, axis=1)
    mod_cols = adaln_fwd(c_rows, a["w_ada"][0], b_ada_cols)

    def own_half(n):
        w = a[n][0].astype(MXU_DTYPE)
        return lax.dynamic_slice_in_dim(w, mc * (w.shape[0] // 2), w.shape[0] // 2, axis=0)

    def whole_weight(n, gth, own):
        per_chip = lax.dynamic_update_index_in_dim(gth, own, dev, 0).reshape(N_SHARDS, -1, gth.shape[-1])
        return (per_chip.transpose(1, 0, 2).reshape(per_chip.shape[1], -1) if n in COL_SHARDED
                else per_chip.reshape(-1, per_chip.shape[-1]))

    small_pack, small_layout = _pack([a[n][0] for n in LAST_SHARDED])
    own_blocks = [mod_cols, own_half("w_in"), small_pack]
    gathered = gather_two_level(own_blocks, "gather_weights")
    late_own = [own_half(n) for n in LATE_WEIGHTS]
    late_started = exchange_start(late_own, True, "gather_late_start")
    mod_own = lax.dynamic_update_index_in_dim(gathered[0], mod_cols, dev, 0)
    mod_all = jnp.stack([mod_own[s] for s in CHIP_SLOTS], axis=1).reshape(c_rows.shape[0], -1)
    mod_all = mod_all + late_started[-1][0, 0]
    mod_x = lax.dynamic_slice_in_dim(mod_all, dev * bsz, bsz, axis=0).reshape(bsz, 6, D_MODEL)
    mod_ctx = mod_all[n_ex].reshape(6, D_MODEL)
    wt = {"w_in": whole_weight("w_in", gathered[1], own_blocks[1])}

    def late_weights(after):
        lands = exchange_wait(late_started, after, True, "gather_late_wait")
        return {n: whole_weight(n, land, own) for n, land, own in zip(LATE_WEIGHTS, lands, late_own)}

    def grad_blocks(n, gw):
        if n in COL_SHARDED:
            gw = gw.reshape(gw.shape[0], N_SHARDS, -1).transpose(1, 0, 2)
        return gw.reshape(N_DEV, -1, gw.shape[-1]).astype(MXU_DTYPE)

    late_sent, last_sent = {}, {}

    def early_grads(late_g):
        late_sent["blocks"] = [grad_blocks(n, late_g[n]) for n in LATE_WEIGHTS]
        late_sent["started"] = exchange_start(late_sent["blocks"], False, "scatter_late_start")
        return late_sent["started"][-1]

    def last_grads(g_w_in, g_small):
        shard_packs = []
        for s in range(N_SHARDS):
            pieces_s = [lax.slice_in_dim(g_small[n], s * a[n].shape[-1], (s + 1) * a[n].shape[-1],
                                         axis=g_small[n].ndim - 1) for n in LAST_SHARDED]
            pack_s, last_sent["layout"] = _pack(pieces_s)
            shard_packs.append(jnp.pad(pack_s, ((0, -pack_s.shape[0] % (2 * PACK_ROWS)), (0, 0))))
        last_sent["blocks"] = [grad_blocks("w_in", g_w_in), jnp.stack(shard_packs).reshape(N_DEV, -1, PACK_W)]
        last_sent["started"] = exchange_start(last_sent["blocks"], False, "scatter_last_start")
        return last_sent["started"][-1]

    small_own = lax.dynamic_update_index_in_dim(gathered[2], small_pack, dev, 0)
    small_by_chip = _unpack(jnp.stack([small_own[s] for s in CHIP_SLOTS]), small_layout, (N_SHARDS,))
    for n, parts in zip(LAST_SHARDED, small_by_chip):
        wt[n] = jnp.concatenate([parts[s] for s in range(N_SHARDS)], axis=-1)
    for n in ("norm1_g", "norm2_g", "rwkv_k_k", "rwkv_k_a", "rwkv_r_k", "rwkv_ln_w", "rwkv_ln_b", "b_ff1", "b_ff2"):
        wt[n] = a[n]
    wt["ret_log_decay"] = a["ret_log_decay"][0]
    wt["final_g"] = a["final_g"][None, :]

    loss, grad_x, g = layer_step(x, ctx, tgt, mod_x, mod_ctx, wt, late_weights, early_grads, last_grads)

    small_names = [n for n in REPLICATED if n not in ("c_ctx", "b_ada")]
    g_pack, g_layout = _pack([jnp.pad(loss, ((0, 0), (0, PACK_W - loss.shape[1])))] + [g[n] for n in small_names]
                             + [g["mod_x"], g["mod_ctx"]])
    (g_packs,) = gather_two_level([g_pack], "gather_small_grads")
    g_packs = lax.dynamic_update_index_in_dim(g_packs, g_pack, dev, 0)
    g_sum = _unpack(sum_slots(g_packs, tuple(range(N_DEV)), "sum_small_grads"), g_layout)
    loss_total = g_sum[0][0, 0]
    grads = dict(zip(small_names, g_sum[1:1 + len(small_names)]))
    dmod_ctx = g_sum[-1].reshape(1, -1)
    dmod_x = _unpack(g_packs, g_layout, (N_DEV,))[-2].reshape(n_ex, -1)
    dmod = jnp.concatenate([dmod_x, dmod_ctx, jnp.zeros((PACK_ROWS - 1, dmod_x.shape[1]), f32)], axis=0)
    grads["b_ada"] = column_sum(dmod, "b_ada_grad")
    dmod_cols = lax.dynamic_slice_in_dim(dmod, shard * ada_cols, ada_cols, axis=1)
    grads["w_ada"], dsilu = adaln_bwd(c_rows, dmod_cols, a["w_ada"][0])

    dsilu_rows = jnp.broadcast_to(jnp.pad(dsilu[n_ex:n_ex + 1], ((0, PACK_ROWS - 1), (0, 0)))[None],
                                  (N_SHARDS, PACK_ROWS, D_MODEL))
    (shares,) = exchange([dsilu_rows], False, CHIP_PEERS, "share_c_ctx_grad", by_chip=True, own=False)
    shares = lax.dynamic_update_index_in_dim(shares, dsilu_rows[0], shard, 0)
    grads["c_ctx"] = c_ctx_grad(shares, a["c_ctx"][None, :])

    scattered, half_sums = ("w_in", "small_shards") + LATE_WEIGHTS, []
    for sent, wait_name, after in ((last_sent, "scatter_last_wait", grads["c_ctx"]),
                                   (late_sent, "scatter_late_wait", grads["c_ctx"])):
        for land, block in zip(exchange_wait(sent["started"], after, False, wait_name), sent["blocks"]):
            land = lax.dynamic_update_index_in_dim(land, lax.dynamic_index_in_dim(block, dev, 0, keepdims=False), dev, 0)
            half_sums.append(sum_slots(land, tuple(range(N_DEV)), f"sum_{scattered[len(half_sums)]}"))
    other_halves = sibling_swap(half_sums, "swap_halves")
    for n, mine, other in zip(scattered, half_sums, other_halves):
        rows = mine.shape[0]
        whole = jnp.zeros((2 * rows, mine.shape[1]), f32)
        whole = lax.dynamic_update_slice_in_dim(whole, mine, mc * rows, axis=0)
        grads[n] = lax.dynamic_update_slice_in_dim(whole, other, (1 - mc) * rows, axis=0)
    grads.update(zip(LAST_SHARDED, _unpack(grads.pop("small_shards"), last_sent["layout"])))

    out_g, out_d, out_m, out_v = {}, {}, {}, {}
    for n in ("w_ada",) + COL_SHARDED + ROW_SHARDED:
        out_g[n] = grads[n].reshape(a[n].shape)
        two_d = lambda z: z.reshape(-1, z.shape[-1])
        d, m, v = adamw(two_d(a[n]), two_d(out_g[n]), two_d(a["m_" + n]), two_d(a["v_" + n]), f"adamw_{n}")
        out_d[n], out_m[n], out_v[n] = d.reshape(a[n].shape), m.reshape(a[n].shape), v.reshape(a[n].shape)
    rest = REPLICATED + LAST_SHARDED
    for n in rest:
        out_g[n] = grads[n].reshape(a[n].shape)
    packs = [_pack([src[n] for n in rest])[0] for src in
             ({n: a[n] for n in rest}, out_g, {n: a["m_" + n] for n in rest}, {n: a["v_" + n] for n in rest})]
    _, rest_layout = _pack([a[n] for n in rest])
    for dst, pack in zip((out_d, out_m, out_v), adamw(*packs, "adamw_small")):
        dst.update(zip(rest, _unpack(pack, rest_layout)))
    return (loss_total, grad_x, *[out_g[n] for n in W_NAMES], *[out_d[n] for n in W_NAMES],
            *[out_m[n] for n in W_NAMES], *[out_v[n] for n in W_NAMES])


def kernel(x, c, ctx, c_ctx, w_ada, b_ada, norm1_g, norm2_g, w_in, ret_log_decay, rwkv_shift_mu, rwkv_w0, rwkv_w_up, rwkv_a0, rwkv_a_up, rwkv_g_up, rwkv_k_k, rwkv_k_a, rwkv_r_k, rwkv_ln_w, rwkv_ln_b, w_out, w_ff1, b_ff1, w_ff2, b_ff2, final_g, loss_target, m_c_ctx, m_w_ada, m_b_ada, m_norm1_g, m_norm2_g, m_w_in, m_ret_log_decay, m_rwkv_shift_mu, m_rwkv_w0, m_rwkv_w_up, m_rwkv_a0, m_rwkv_a_up, m_rwkv_g_up, m_rwkv_k_k, m_rwkv_k_a, m_rwkv_r_k, m_rwkv_ln_w, m_rwkv_ln_b, m_w_out, m_w_ff1, m_b_ff1, m_w_ff2, m_b_ff2, m_final_g, v_c_ctx, v_w_ada, v_b_ada, v_norm1_g, v_norm2_g, v_w_in, v_ret_log_decay, v_rwkv_shift_mu, v_rwkv_w0, v_rwkv_w_up, v_rwkv_a0, v_rwkv_a_up, v_rwkv_g_up, v_rwkv_k_k, v_rwkv_k_a, v_rwkv_r_k, v_rwkv_ln_w, v_rwkv_ln_b, v_w_out, v_w_ff1, v_b_ff1, v_w_ff2, v_b_ff2, v_final_g):
    return _train_step(dict(locals()))
```

```python
import functools
import math

import jax
import jax.numpy as jnp
from jax import lax
from jax.experimental import pallas as pl
from jax.experimental.pallas import tpu as pltpu

f32 = jnp.float32
MXU_DTYPE = jnp.bfloat16

D_MODEL = 1024
RET_W = 512
RET_HEADS = 4
RET_DH = 128
RET_CHUNK = 128
RW_W = 512
RW_N = 64
DECAY_LORA = 64
AAA_LORA = 64
GATE_LORA = 128
LORA_W = DECAY_LORA + AAA_LORA + GATE_LORA
D_FF = 4096
RET_COLS = 4 * RET_W
SHIFT_COLS = 3 * RW_W + LORA_W
IN_COLS = RET_COLS + SHIFT_COLS
GRID_W = 64
ROPE_BASE = 10000.0
NORM_EPS = 1e-6
GN_EPS = 64e-5
W_DECAY_SCALE = math.exp(-0.5)
ADAM_LR, ADAM_B1, ADAM_B2, ADAM_EPS, ADAM_WD, ADAM_STEP = 0.001, 0.9, 0.999, 1e-08, 0.01, 10

TOK_TILE = 256
MATMUL_TILE = 1024
SCAN_CHUNK = 32
SCAN_UNROLL = SCAN_CHUNK
N_DEV = 8
V7X_VMEM_BYTES = 64 * 1024 * 1024
VMEM_LIMIT = V7X_VMEM_BYTES * 7 // 8


def _cparams(sem):
    return pltpu.CompilerParams(dimension_semantics=sem, vmem_limit_bytes=VMEM_LIMIT)


def _tile(n, cap):
    best = None
    for t in range(128, min(n, cap) + 1, 128):
        if n % t == 0:
            best = t
    return best if best is not None else n


def matmul(a, b, mode, name, out_dtype=f32, bias=None, finish=None):
    if mode == "nn":
        (m, k), (k2, n) = a.shape, b.shape
    elif mode == "nt":
        (m, k), (n, k2) = a.shape, b.shape
    else:
        (k, m), (k2, n) = a.shape, b.shape
    assert k == k2, (a.shape, b.shape, mode)
    tm, tn, tk = _tile(m, MATMUL_TILE), _tile(n, MATMUL_TILE), _tile(k, MATMUL_TILE)
    nk = k // tk
    dims = {"nn": ((1,), (0,)), "nt": ((1,), (1,)), "tn": ((0,), (0,))}[mode]

    def body(a_ref, b_ref, *rest):
        o_ref, acc_ref = rest[-2:]
        kk = pl.program_id(2)

        @pl.when(kk == 0)
        def _():
            acc_ref[...] = jnp.zeros_like(acc_ref)

        acc_ref[...] += lax.dot_general(a_ref[...].astype(MXU_DTYPE), b_ref[...].astype(MXU_DTYPE),
                                        (dims, ((), ())), preferred_element_type=f32)

        @pl.when(kk == nk - 1)
        def _():
            res = acc_ref[...]
            if bias is not None:
                res = res + rest[0][...]
            if finish is not None:
                res = finish(res)
            o_ref[...] = res.astype(o_ref.dtype)

    if mode == "nn":
        a_spec = pl.BlockSpec((tm, tk), lambda i, j, q: (i, q))
        b_spec = pl.BlockSpec((tk, tn), lambda i, j, q: (q, j))
    elif mode == "nt":
        a_spec = pl.BlockSpec((tm, tk), lambda i, j, q: (i, q))
        b_spec = pl.BlockSpec((tn, tk), lambda i, j, q: (j, q))
    else:
        a_spec = pl.BlockSpec((tk, tm), lambda i, j, q: (q, i))
        b_spec = pl.BlockSpec((tk, tn), lambda i, j, q: (q, j))
    extra_specs = [] if bias is None else [pl.BlockSpec((1, tn), lambda i, j, q: (0, j))]
    extra = [] if bias is None else [bias]
    return pl.pallas_call(
        body, grid=(m // tm, n // tn, nk), in_specs=[a_spec, b_spec] + extra_specs,
        out_specs=pl.BlockSpec((tm, tn), lambda i, j, q: (i, j)),
        out_shape=jax.ShapeDtypeStruct((m, n), out_dtype),
        scratch_shapes=[pltpu.VMEM((tm, tn), f32)],
        compiler_params=_cparams(("parallel", "parallel", "arbitrary")), name=name)(a, b, *extra)


class Tiled:
    def __init__(self, arr, w=None, cidx=0, toff=0):
        self.arr, self.w, self.cidx, self.toff = arr, (arr.shape[-1] if w is None else w), cidx, toff

    def spec(self):
        cidx, toff = self.cidx, self.toff
        return pl.BlockSpec((None, TOK_TILE, self.w), lambda b, i: (b, jnp.maximum(i + toff, 0), cidx))


class Seg:
    def __init__(self, arr, seg, first):
        self.arr, self.seg, self.first = arr, seg, first

    def spec(self):
        seg = self.seg
        return pl.BlockSpec((None, None, 1, self.arr.shape[-1]), lambda b, i: (b, seg(i), 0, 0))


class Glob:
    def __init__(self, arr):
        self.arr = arr

    def spec(self):
        return pl.BlockSpec(self.arr.shape, lambda b, i: (0,) * self.arr.ndim)


def ew_forward(fn, name, bsz, n_tiles, ins, outs):
    n_in = len(ins)

    def body(*refs):
        res = fn(*[r[...] for r in refs[:n_in]])
        for o_ref, o in zip(refs[n_in:], res):
            o_ref[...] = o.astype(o_ref.dtype)

    out_specs = [pl.BlockSpec((None, TOK_TILE, w), lambda b, i: (b, i, 0)) for w, _ in outs]
    out_shape = [jax.ShapeDtypeStruct((bsz, n_tiles * TOK_TILE, w), dt) for w, dt in outs]
    return pl.pallas_call(body, grid=(bsz, n_tiles), in_specs=[d.spec() for d in ins], out_specs=out_specs,
                          out_shape=out_shape, compiler_params=_cparams(("parallel", "parallel")), name=name)(
        *[d.arr for d in ins])


def ew_backward(fn, name, bsz, n_tiles, ins, cts, want, grad_dtypes=None, lead=0):
    ct_parts = [c if isinstance(c, tuple) else (c,) for c in cts]
    cts = [part for parts in ct_parts for part in parts]
    n_in, n_ct = len(ins), len(cts)
    diff = [k for k in range(n_in) if want[k]]
    grad_dtypes = grad_dtypes or {}
    assert lead == 0 or not any(isinstance(ins[k], Seg) for k in diff)

    def body(*refs):
        b, i = pl.program_id(0), pl.program_id(1)
        g_refs = refs[n_in + n_ct:]

        def tile_grads():
            vals = [r[...] for r in refs[:n_in]]
            ct_refs = iter(refs[n_in:n_in + n_ct])
            ct_vals = tuple(functools.reduce(lambda s, t: s + t, [next(ct_refs)[...].astype(f32) for _ in parts])
                            for parts in ct_parts)

            def f(*dvals):
                full = list(vals)
                for k, v in zip(diff, dvals):
                    full[k] = v
                return tuple(fn(*full))

            _, vjp = jax.vjp(f, *[vals[k] for k in diff])
            grads = vjp(ct_vals)
            for k, g_ref, g in zip(diff, g_refs, grads):
                d = ins[k]
                if isinstance(d, Tiled):
                    g_ref[...] = g.astype(g_ref.dtype)
                else:
                    zero = d.first(i) if isinstance(d, Seg) else jnp.logical_and(b == 0, i == lead)

                    @pl.when(zero)
                    def _(g_ref=g_ref):
                        g_ref[...] = jnp.zeros_like(g_ref)

                    g_ref[...] += g

        if lead == 0:
            tile_grads()
        else:
            pl.when(i >= lead)(tile_grads)

            @pl.when(i < lead)
            def _():
                for k, g_ref in zip(diff, g_refs):
                    if isinstance(ins[k], Tiled):
                        g_ref[...] = jnp.zeros_like(g_ref)

    out_specs, out_shape = [], []
    for k in diff:
        d = ins[k]
        if isinstance(d, Tiled):
            out_specs.append(pl.BlockSpec((None, TOK_TILE, d.w), lambda b, i: (b, i, 0)))
            out_shape.append(jax.ShapeDtypeStruct((bsz, (n_tiles + lead) * TOK_TILE, d.w), grad_dtypes.get(k, f32)))
        else:
            out_specs.append(d.spec())
            out_shape.append(jax.ShapeDtypeStruct(d.arr.shape, f32))
    return pl.pallas_call(body, grid=(bsz, n_tiles + lead),
                          in_specs=[d.spec() for d in ins] + [c.spec() for c in cts],
                          out_specs=out_specs, out_shape=out_shape,
                          compiler_params=_cparams(("arbitrary", "arbitrary")), name=name)(
        *[d.arr for d in ins], *[c.arr for c in cts])


@jax.custom_vjp
def _mxu_dot(a, b):
    return jnp.dot(a.astype(MXU_DTYPE), b.astype(MXU_DTYPE), preferred_element_type=f32)


def _mxu_dot_fwd(a, b):
    return _mxu_dot(a, b), (a, b)


def _mxu_dot_bwd(res, ct):
    a, b = res
    ct = ct.astype(MXU_DTYPE)
    da = lax.dot_general(ct, b.astype(MXU_DTYPE), (((1,), (1,)), ((), ())), preferred_element_type=f32)
    db = lax.dot_general(a.astype(MXU_DTYPE), ct, (((0,), (0,)), ((), ())), preferred_element_type=f32)
    return da, db


_mxu_dot.defvjp(_mxu_dot_fwd, _mxu_dot_bwd)


def _split_dot_impl(x, ones_mat):
    hi = x.astype(MXU_DTYPE)
    lo = (x - hi.astype(f32)).astype(MXU_DTYPE)
    return jnp.dot(hi, ones_mat, preferred_element_type=f32) + jnp.dot(lo, ones_mat, preferred_element_type=f32)


@jax.custom_vjp
def _split_dot(x, ones_mat):
    return _split_dot_impl(x, ones_mat)


def _split_dot_fwd(x, ones_mat):
    return _split_dot_impl(x, ones_mat), ones_mat


def _split_dot_bwd(ones_mat, ct):
    return _split_dot_impl(ct, ones_mat), None


_split_dot.defvjp(_split_dot_fwd, _split_dot_bwd)


def _block_ones(n, group):
    idx = jnp.arange(n) // group
    return (idx[:, None] == idx[None, :]).astype(MXU_DTYPE)


def _rms(x, g):
    return x * lax.rsqrt(jnp.mean(x * x, axis=-1, keepdims=True) + NORM_EPS) * g


def fn_norm_mod(h, shift, scale, g):
    return (_rms(h, g) * (1.0 + scale) + shift,)


def fn_rwkv_prepare(ks, lora, w0_f, w0_b, a0_f, a0_b, w_up_f, w_up_b, a_up_f, a_up_b, g_up, k_k, k_a, ones64):
    kkr = ks * k_k
    kk = kkr * lax.rsqrt(_split_dot(kkr * kkr, ones64) + 1e-12)
    outs = [kk]
    th = jnp.tanh(lora)
    for w0, a0, w_up, a_up in ((w0_f, a0_f, w_up_f, a_up_f), (w0_b, a0_b, w_up_b, a_up_b)):
        w = jnp.exp(-W_DECAY_SCALE * jax.nn.sigmoid(w0 + _mxu_dot(th, w_up)))
        a = jax.nn.sigmoid(a0 + _mxu_dot(lora, a_up))
        kt = ks * (1.0 + (a - 1.0) * k_a)
        outs += [w, a * kk, kt]
    outs.append(_mxu_dot(jax.nn.sigmoid(lora), g_up))
    return tuple(outs)


def fn_merge(o_f, o_b, g_ret, y_f, y_b, r, kt_f, v, g_rw, r_k, ln_w, ln_b, ones64, ones128):
    o = o_f + o_b
    ret = o * lax.rsqrt(_split_dot(o * o, ones128) * (1.0 / RET_DH) + NORM_EPS) * (g_ret * jax.nn.sigmoid(g_ret))
    y = y_f + y_b
    mean = _split_dot(y, ones64) * (1.0 / RW_N)
    yc = y - mean
    var = _split_dot(yc * yc, ones64) * (1.0 / RW_N)
    y_n = yc * lax.rsqrt(var + GN_EPS) * ln_w + ln_b
    bonus = _split_dot(r * kt_f * r_k, ones64) * v
    return ret, (y_n + bonus) * g_rw


def fn_resid_norm_mod(x, mix, gate, shift, scale, g):
    h1 = x + gate * mix
    return h1, _rms(h1, g) * (1.0 + scale) + shift


def relu2(z):
    return jnp.square(jnp.maximum(z, 0.0))


def relu2_backward(act, dact, name):
    bsz, n_tok, width = act.shape

    def body(a_ref, d_ref, du_ref, db_ref):
        du = d_ref[...].astype(f32) * (2.0 * jnp.sqrt(a_ref[...].astype(f32)))
        du_ref[...] = du.astype(du_ref.dtype)

        @pl.when(jnp.logical_and(pl.program_id(0) == 0, pl.program_id(1) == 0))
        def _():
            db_ref[...] = jnp.zeros_like(db_ref)

        db_ref[...] += jnp.sum(du, axis=0, keepdims=True)

    tile = pl.BlockSpec((None, TOK_TILE, width), lambda b, i: (b, i, 0))
    row = pl.BlockSpec((1, width), lambda b, i: (0, 0))
    return pl.pallas_call(body, grid=(bsz, n_tok // TOK_TILE), in_specs=[tile, tile], out_specs=[tile, row],
                          out_shape=[jax.ShapeDtypeStruct(act.shape, MXU_DTYPE), jax.ShapeDtypeStruct((1, width), f32)],
                          compiler_params=_cparams(("arbitrary", "arbitrary")), name=name)(act, dact)


def fn_loss(h1, f, tgt, gate, b2, g):
    y = _rms(h1 + gate * (f + b2), g)
    err = jnp.square(y - tgt)
    return 0.5 * jnp.sum(jnp.mean(err, axis=-1, keepdims=True), axis=0, keepdims=True)


def loss_and_grads(h1, f, tgt, gate, b2, g, bsz, n_tiles):
    def body(h1_ref, f_ref, t_ref, gate_ref, b2_ref, g_ref, loss_ref, dh1_ref, df_ref, dgate_ref, db2_ref, dg_ref):
        b, i = pl.program_id(0), pl.program_id(1)
        tgt_v = t_ref[...]
        loss, vjp = jax.vjp(lambda a, c, e, p, q: fn_loss(a, c, tgt_v, e, p, q),
                            h1_ref[...], f_ref[...], gate_ref[...], b2_ref[...], g_ref[...])
        dh1, df, dgate, db2, dg = vjp(jnp.ones((1, 1), f32))
        dh1_ref[...] = dh1
        df_ref[...] = df.astype(df_ref.dtype)

        @pl.when(i == 0)
        def _():
            dgate_ref[...] = jnp.zeros_like(dgate_ref)

        @pl.when(jnp.logical_and(b == 0, i == 0))
        def _():
            loss_ref[...] = jnp.zeros_like(loss_ref)
            db2_ref[...] = jnp.zeros_like(db2_ref)
            dg_ref[...] = jnp.zeros_like(dg_ref)

        dgate_ref[...] += dgate
        db2_ref[...] += db2
        dg_ref[...] += dg
        loss_ref[...] += jnp.broadcast_to(loss, loss_ref.shape)

    tile = pl.BlockSpec((None, TOK_TILE, D_MODEL), lambda b, i: (b, i, 0))
    row = pl.BlockSpec((1, D_MODEL), lambda b, i: (0, 0))
    seg = pl.BlockSpec((None, None, 1, D_MODEL), lambda b, i: (b, 0, 0, 0))
    t_tok = n_tiles * TOK_TILE
    return pl.pallas_call(
        body, grid=(bsz, n_tiles), in_specs=[tile, tile, tile, seg, row, row],
        out_specs=[pl.BlockSpec((1, 128), lambda b, i: (0, 0)), tile, tile, seg, row, row],
        out_shape=[jax.ShapeDtypeStruct((1, 128), f32), jax.ShapeDtypeStruct((bsz, t_tok, D_MODEL), f32),
                   jax.ShapeDtypeStruct((bsz, t_tok, D_MODEL), MXU_DTYPE),
                   jax.ShapeDtypeStruct((bsz, 1, 1, D_MODEL), f32),
                   jax.ShapeDtypeStruct((1, D_MODEL), f32), jax.ShapeDtypeStruct((1, D_MODEL), f32)],
        compiler_params=_cparams(("arbitrary", "arbitrary")), name="loss_and_grads")(h1, f, tgt, gate, b2, g)


SHIFT_BLOCK = SHIFT_COLS
HALO_ROWS = 8


def _shift_specs(n_tok, col0, width=SHIFT_BLOCK):
    per_tile = TOK_TILE // HALO_ROWS
    last = n_tok // HALO_ROWS - 1
    tile = pl.BlockSpec((None, TOK_TILE, width), lambda j, b, i: (b, i, col0 + j))
    prev = pl.BlockSpec((None, HALO_ROWS, width),
                        lambda j, b, i: (b, jnp.maximum(i * per_tile - 1, 0), col0 + j))
    nxt = pl.BlockSpec((None, HALO_ROWS, width),
                       lambda j, b, i: (b, jnp.minimum((i + 1) * per_tile, last), col0 + j))
    return tile, prev, nxt


def _shift_specs_at(n_tok, col):
    one, halo, cols = pl.Element(1), pl.Element(HALO_ROWS), pl.Element(SHIFT_COLS)
    tile = pl.BlockSpec((one, pl.Element(TOK_TILE), cols),
                        lambda j, b, i: (b, pl.multiple_of(i * TOK_TILE, TOK_TILE), col))
    prev = pl.BlockSpec((one, halo, cols), lambda j, b, i: (
        b, pl.multiple_of(jnp.maximum(i * TOK_TILE - HALO_ROWS, 0), HALO_ROWS), col))
    nxt = pl.BlockSpec((one, halo, cols), lambda j, b, i: (
        b, pl.multiple_of(jnp.minimum((i + 1) * TOK_TILE, n_tok - HALO_ROWS), HALO_ROWS), col))
    return tile, prev, nxt


def _shifted(p, prev_ref, next_ref, is_first, is_last):
    row = lax.broadcasted_iota(jnp.int32, p.shape, 0)
    prev_row = jnp.where(is_first, 0.0, prev_ref[HALO_ROWS - 1:HALO_ROWS, :].astype(f32))
    next_row = jnp.where(is_last, 0.0, next_ref[0:1, :].astype(f32))
    prev = jnp.where(row == 0, prev_row, pltpu.roll(p, 1, axis=0))
    nxt = jnp.where(row == TOK_TILE - 1, next_row, pltpu.roll(p, TOK_TILE - 1, axis=0))
    return prev, nxt


def token_shift(px, mu, seg_first, seg_last):
    bsz, n_tok, _ = px.shape
    n_tiles = n_tok // TOK_TILE
    assert SHIFT_BLOCK == SHIFT_COLS

    def body(p_ref, prev_ref, next_ref, mu_ref, o_ref):
        i = pl.program_id(2)
        p = p_ref[0]
        prev, nxt = _shifted(p, prev_ref[0], next_ref[0], seg_first(i), seg_last(i))
        o_ref[...] = p + mu_ref[0:1, :] * (prev - p) + mu_ref[1:2, :] * (nxt - p)

    tile, prev, nxt = _shift_specs_at(n_tok, RET_COLS)
    return pl.pallas_call(
        body, grid=(SHIFT_COLS // SHIFT_BLOCK, bsz, n_tiles),
        in_specs=[tile, prev, nxt, pl.BlockSpec((2, SHIFT_BLOCK), lambda j, b, i: (0, j))],
        out_specs=pl.BlockSpec((None, TOK_TILE, SHIFT_BLOCK), lambda j, b, i: (b, i, j)),
        out_shape=jax.ShapeDtypeStruct((bsz, n_tok, SHIFT_COLS), f32),
        compiler_params=_cparams(("parallel", "parallel", "parallel")), name="token_shift")(px, px, px, mu)


def token_shift_bwd(d_sections, px, mu, seg_first, seg_last):
    bsz, n_tok, _ = px.shape
    n_tiles = n_tok // TOK_TILE
    d_arrs = [part for section in d_sections for part in section]
    assert SHIFT_BLOCK == SHIFT_COLS == sum(section[0].shape[-1] for section in d_sections)

    def body(*refs):
        d_refs, (p_ref, prev_ref, next_ref, mu_ref, dp_ref, dmu_ref) = refs[:3 * len(d_arrs)], refs[3 * len(d_arrs):]
        b, i = pl.program_id(1), pl.program_id(2)
        first, last = seg_first(i), seg_last(i)

        def summed(which):
            part_refs = iter(d_refs[which::3])
            return jnp.concatenate([functools.reduce(lambda s, t: s + t, [next(part_refs)[...] for _ in section])
                                    for section in d_sections], axis=-1)

        d, p = summed(0), p_ref[0]
        d_prev, d_next = _shifted(d, summed(1), summed(2), first, last)
        p_prev, p_next = _shifted(p, prev_ref[0], next_ref[0], first, last)
        mu0, mu1 = mu_ref[0:1, :], mu_ref[1:2, :]
        dp_ref[...] = (d + mu0 * (d_next - d) + mu1 * (d_prev - d)).astype(dp_ref.dtype)

        @pl.when(jnp.logical_and(b == 0, i == 0))
        def _():
            dmu_ref[...] = jnp.zeros_like(dmu_ref)

        dmu_ref[0:1, :] += jnp.sum(d * (p_prev - p), axis=0, keepdims=True)
        dmu_ref[1:2, :] += jnp.sum(d * (p_next - p), axis=0, keepdims=True)

    d_specs = [spec for arr in d_arrs for spec in _shift_specs(n_tok, 0, arr.shape[-1])]
    tile, prev, nxt = _shift_specs_at(n_tok, RET_COLS)
    mu_spec = pl.BlockSpec((2, SHIFT_BLOCK), lambda j, b, i: (0, j))
    return pl.pallas_call(
        body, grid=(SHIFT_COLS // SHIFT_BLOCK, bsz, n_tiles),
        in_specs=d_specs + [tile, prev, nxt, mu_spec],
        out_specs=[pl.BlockSpec((None, TOK_TILE, SHIFT_BLOCK), lambda j, b, i: (b, i, j)), mu_spec],
        out_shape=[jax.ShapeDtypeStruct((bsz, n_tok, SHIFT_COLS), MXU_DTYPE),
                   jax.ShapeDtypeStruct((2, SHIFT_COLS), f32)],
        compiler_params=_cparams(("arbitrary", "arbitrary", "arbitrary")), name="token_shift_bwd")(
        *[arr for arr in d_arrs for _ in range(3)], px, px, px, mu)


def _dg(a, b, ca, cb):
    return lax.dot_general(a.astype(MXU_DTYPE), b.astype(MXU_DTYPE), (((ca,), (cb,)), ((), ())),
                           preferred_element_type=f32)


@jax.custom_vjp
def _mm_nt(a, b):
    return _dg(a, b, 1, 1)


_mm_nt.defvjp(lambda a, b: (_dg(a, b, 1, 1), (a, b)),
              lambda res, ct: (_dg(ct, res[1], 1, 0), _dg(ct, res[0], 0, 0)))


@jax.custom_vjp
def _mm_tn(a, b):
    return _dg(a, b, 0, 0)


_mm_tn.defvjp(lambda a, b: (_dg(a, b, 0, 0), (a, b)),
              lambda res, ct: (_dg(res[1], ct, 1, 1), _dg(res[0], ct, 1, 0)))


ROTARY_PAIR = RET_DH // 4


def _swap_pairs_impl(t):
    lane = lax.broadcasted_iota(jnp.int32, t.shape, 1)
    return jnp.where(lane % (2 * ROTARY_PAIR) < ROTARY_PAIR, pltpu.roll(t, RET_DH - ROTARY_PAIR, axis=1),
                     pltpu.roll(t, ROTARY_PAIR, axis=1))


@jax.custom_vjp
def _swap_pairs(t):
    return _swap_pairs_impl(t)


_swap_pairs.defvjp(lambda t: (_swap_pairs_impl(t), None), lambda _, ct: (_swap_pairs_impl(ct),))


def _ret_chunk(state, q_raw, k_raw, v, cos, sin, ld_row, head, reverse):
    c = RET_CHUNK
    lane = lax.broadcasted_iota(jnp.int32, ld_row.shape, 1)
    lg = -jnp.exp(jnp.sum(jnp.where(lane == head, ld_row, 0.0), axis=-1, keepdims=True))
    rot = lambda t: t * cos + _swap_pairs(t) * sin
    q = rot(q_raw)
    k = rot(k_raw) * (RET_DH ** -0.5)
    ti = lax.broadcasted_iota(jnp.int32, (c, 1), 0).astype(f32)
    tj = lax.broadcasted_iota(jnp.int32, (1, c), 1).astype(f32)
    if not reverse:
        dist, mask, q_exp, k_exp = ti - tj, (ti - tj) >= 0, ti + 1.0, c - 1.0 - ti
    else:
        dist, mask, q_exp, k_exp = tj - ti, (tj - ti) > 0, c - ti, ti
    decay = jnp.where(mask, jnp.exp(lg * jnp.maximum(dist, 0.0)), 0.0)
    scores = _mm_nt(q, k) * decay
    out = _mxu_dot(scores, v) + _mxu_dot(q * jnp.exp(lg * q_exp), state)
    new_state = state * jnp.exp(lg * c) + _mm_tn(k * jnp.exp(lg * k_exp), v)
    return out, new_state


def _ret_specs(bsz, order):
    tok = lambda col=0: pl.BlockSpec((bsz, RET_CHUNK, RET_W), lambda i: (0, order(i), col))
    tab = pl.BlockSpec((RET_CHUNK, RET_DH), lambda i: (order(i), 0))
    ld = pl.BlockSpec((1, RET_DH), lambda i: (0, 0))
    return tok, tab, ld


def retention_fwd(px, cos, sin, ld_row, order, reverse, name):
    bsz, n_tok, _ = px.shape
    n_ch = n_tok // RET_CHUNK

    def body(q_ref, k_ref, v_ref, cos_ref, sin_ref, ld_ref, o_ref, sv_ref, st_ref):
        @pl.when(pl.program_id(0) == 0)
        def _():
            st_ref[...] = jnp.zeros_like(st_ref)

        for b in range(bsz):
            for h in range(RET_HEADS):
                sl = slice(h * RET_DH, (h + 1) * RET_DH)
                s = st_ref[b, h]
                sv_ref[b, h] = s
                o, s_new = _ret_chunk(s, q_ref[b, :, sl], k_ref[b, :, sl], v_ref[b, :, sl], cos_ref[...], sin_ref[...],
                                      ld_ref[...], h, reverse)
                o_ref[b, :, sl] = o
                st_ref[b, h] = s_new

    tok, tab, ld = _ret_specs(bsz, order)
    return pl.pallas_call(
        body, grid=(n_ch,), in_specs=[tok(0), tok(1), tok(2), tab, tab, ld],
        out_specs=[tok(), pl.BlockSpec((bsz, None, RET_HEADS, RET_DH, RET_DH), lambda i: (0, i, 0, 0, 0))],
        out_shape=[jax.ShapeDtypeStruct((bsz, n_tok, RET_W), f32),
                   jax.ShapeDtypeStruct((bsz, n_ch, RET_HEADS, RET_DH, RET_DH), f32)],
        scratch_shapes=[pltpu.VMEM((bsz, RET_HEADS, RET_DH, RET_DH), f32)],
        compiler_params=_cparams(("arbitrary",)), name=name)(px, px, px, cos, sin, ld_row)


def retention_bwd(do, px, states, cos, sin, ld_row, order, reverse, name, add_to=()):
    bsz, n_tok, _ = px.shape
    n_ch = n_tok // RET_CHUNK
    back = lambda i: order(n_ch - 1 - i)

    def body(do_ref, q_ref, k_ref, v_ref, sv_ref, cos_ref, sin_ref, ld_ref, *rest):
        add_refs, (dq_ref, dk_ref, dv_ref, dld_ref, dst_ref) = rest[:-5] or (None,) * 3, rest[-5:]

        @pl.when(pl.program_id(0) == 0)
        def _():
            dst_ref[...] = jnp.zeros_like(dst_ref)
            dld_ref[...] = jnp.zeros_like(dld_ref)

        cos_v, sin_v = cos_ref[...], sin_ref[...]
        for b in range(bsz):
            for h in range(RET_HEADS):
                sl = slice(h * RET_DH, (h + 1) * RET_DH)
                f = lambda s, q, k, v, ld, h=h: _ret_chunk(s, q, k, v, cos_v, sin_v, ld, h, reverse)
                _, vjp = jax.vjp(f, sv_ref[b, h], q_ref[b, :, sl], k_ref[b, :, sl], v_ref[b, :, sl], ld_ref[...])
                ds, dq, dk, dv, dld = vjp((do_ref[b, :, sl], dst_ref[b, h]))
                dst_ref[b, h] = ds
                for o_ref, add_ref, val in zip((dq_ref, dk_ref, dv_ref), add_refs, (dq, dk, dv)):
                    if add_ref is not None:
                        val = add_ref[b, :, sl] + val
                    o_ref[b, :, sl] = val.astype(o_ref.dtype)
                dld_ref[...] += dld

    tok, tab, ld = _ret_specs(bsz, back)
    return pl.pallas_call(
        body, grid=(n_ch,),
        in_specs=[tok(), tok(0), tok(1), tok(2),
                  pl.BlockSpec((bsz, None, RET_HEADS, RET_DH, RET_DH), lambda i: (0, n_ch - 1 - i, 0, 0, 0)),
                  tab, tab, ld] + [tok() for _ in add_to],
        out_specs=[tok(), tok(), tok(), ld],
        out_shape=[jax.ShapeDtypeStruct((bsz, n_tok, RET_W), MXU_DTYPE if add_to else f32)] * 3
        + [jax.ShapeDtypeStruct((1, RET_DH), f32)],
        scratch_shapes=[pltpu.VMEM((bsz, RET_HEADS, RET_DH, RET_DH), f32)],
        compiler_params=_cparams(("arbitrary",)), name=name)(
        do, px, px, px, states, cos, sin, ld_row, *add_to)


HALF_W = RW_W // 2


def _head_sum(x, ones):
    xm = x.astype(MXU_DTYPE)
    return jnp.concatenate([jnp.dot(xm[:, :HALF_W], ones, preferred_element_type=f32),
                            jnp.dot(xm[:, HALF_W:], ones, preferred_element_type=f32)], axis=1)


def _stack(parts):
    return jnp.concatenate(parts, axis=0)


def _row(ref, b, t):
    return ref[b, pl.ds(t, 1), :]


SCAN_DIRS = ((False, True), (True, False))
RW_HEADS = RW_W // RW_N
HEAD_ROWS_PAD = 16


def _head_rows(row, mask):
    return jnp.broadcast_to(row, mask.shape) * mask


def _outer(per_value, row, mask_pad):
    return lax.dot_general(per_value.astype(MXU_DTYPE), _head_rows(row, mask_pad).astype(MXU_DTYPE),
                           (((0,), (0,)), ((), ())), preferred_element_type=f32)


def _read(states, rows, mask, more_rows=()):
    lhs = _stack([_head_rows(r, mask) for r in list(rows) + list(more_rows)])
    return lax.dot_general(lhs.astype(MXU_DTYPE), _stack(states).astype(MXU_DTYPE), (((1,), (1,)), ((), ())),
                           preferred_element_type=f32)


def _own_block(raw, b):
    lanes = raw[:, RW_N * b:RW_N * (b + 1)]
    turned = _stack([lanes[RW_HEADS * b:], lanes[:RW_HEADS * b]]) if b else lanes
    if turned.shape[0] < HEAD_ROWS_PAD:
        turned = _stack([turned, jnp.zeros((HEAD_ROWS_PAD - turned.shape[0], RW_N), f32)])
    return turned[:HEAD_ROWS_PAD]


def _row_from_heads(per_value, state, mask_pad):
    full = jnp.dot(per_value.astype(MXU_DTYPE), state.astype(MXU_DTYPE), preferred_element_type=f32)
    return jnp.sum(full * mask_pad, axis=0, keepdims=True)


def _scan_specs(bsz, order):
    rows = lambda col=0: pl.BlockSpec((bsz, SCAN_CHUNK, RW_W), lambda i: (0, order(i), col))
    per_value = pl.BlockSpec((bsz, SCAN_CHUNK, HEAD_ROWS_PAD, RW_N), lambda i: (0, order(i), 0, 0))
    states = pl.BlockSpec((SCAN_CHUNK, bsz, RW_N, RW_W), lambda i: (order(i), 0, 0, 0))
    blocks = pl.BlockSpec((SCAN_CHUNK, RW_HEADS * bsz, RW_N * bsz), lambda i: (order(i), 0, 0))
    return rows, per_value, states, blocks


def _mxu_operands(states):
    return [s.astype(MXU_DTYPE) for s in states]


def _removed(states_m, kk_t, ones, bsz):
    removed = _head_sum(_stack([states_m[b] * kk_t[b].astype(MXU_DTYPE) for b in range(bsz)]), ones)
    return [removed[b * RW_N:(b + 1) * RW_N] for b in range(bsz)]


def _advance(sp, rem, w_t, b_t, vk, bsz):
    return [sp[b] * w_t[b] - rem[b] * b_t[b] + vk[b] for b in range(bsz)]


def heads_to_rows(a):
    b, t, _ = a.shape
    return jnp.pad(a.astype(MXU_DTYPE).reshape(b, t, RW_HEADS, RW_N),
                   ((0, 0), (0, 0), (0, HEAD_ROWS_PAD - RW_HEADS), (0, 0)))


def _blocks_to_rows(raw_ref, first, row_ref, bsz):
    steps = pl.ds(first, SCAN_CHUNK)
    for b in range(bsz):
        for h in range(RW_HEADS):
            row_ref[b, :, h * RW_N:(h + 1) * RW_N] = raw_ref[steps, RW_HEADS * b + h, RW_N * b:RW_N * (b + 1)]


N_ROWS_FWD = 5
N_ROWS_BWD = 5


def _scan_consts(bsz):
    head = (jnp.arange(RW_W)[None, :] // RW_N == jnp.arange(RW_HEADS)[:, None]).astype(f32)
    return head, jnp.pad(head, ((0, HEAD_ROWS_PAD - RW_HEADS), (0, 0))), _block_ones(HALF_W, RW_N)


def _const_specs(consts):
    return [pl.BlockSpec(c.shape, lambda i: (0, 0)) for c in consts]


def rwkv_scan_fwd(rows_in, v_heads, orders, name):
    bsz, n_tok, _ = rows_in[0][0][0].shape
    n_ch = n_tok // SCAN_CHUNK
    rng = range(bsz)
    consts = _scan_consts(bsz)

    def body(*refs):
        rows = [refs[:N_ROWS_FWD], refs[N_ROWS_FWD:2 * N_ROWS_FWD]]
        (v0, v1, head_ref, pad_ref, ones_ref, y0, y1, h0, h1, f0, f1, m0, m1, s0, s1, late_ref,
         raw_ref) = refs[2 * N_ROWS_FWD:]
        v_refs, y_refs, hist_refs, final_refs, s_refs = (v0, v1), (y0, y1), (h0, h1), (f0, f1), (s0, s1)
        removed_refs = (m0, m1)
        n_blk = RW_HEADS * bsz
        head_v, pad_v, ones_v = head_ref[...], pad_ref[...], ones_ref[...]
        for d in range(2):
            @pl.when(pl.program_id(0) == 0)
            def _(d=d):
                s_refs[d][...] = jnp.zeros_like(s_refs[d])

        def step(j, carry):
            ts = [SCAN_CHUNK - 1 - j if reverse else j for reverse, _ in SCAN_DIRS]
            sps = [[s_refs[d][b] for b in rng] for d in range(2)]
            sps_m = [_mxu_operands(sps[d]) for d in range(2)]
            vks = [[_outer(v_refs[d][b, ts[d]], _row(rows[d][4], b, ts[d]), pad_v) for b in rng] for d in range(2)]
            rems = [_removed(sps_m[d], [_row(rows[d][1], b, ts[d]) for b in rng], ones_v, bsz) for d in range(2)]
            for d, (reverse, inclusive) in enumerate(SCAN_DIRS):
                r_ref = rows[d][0]
                read_at = jnp.maximum(j - 1, 0) if inclusive else ts[d]
                both = _read(sps_m[d], [_row(r_ref, b, read_at) for b in rng], head_v,
                             [_row(rows[d][1], b, ts[d]) for b in rng])
                if inclusive:
                    late_ref[j] = both[:n_blk]
                else:
                    raw_ref[ts[d]] = both[:n_blk]
                removed_refs[d][ts[d]] = both[n_blk:]
            for d in range(2):
                new = _advance(sps[d], rems[d], [_row(rows[d][2], b, ts[d]) for b in rng],
                               [_row(rows[d][3], b, ts[d]) for b in rng], vks[d], bsz)
                for b in rng:
                    hist_refs[d][ts[d], b] = sps_m[d][b]
                    s_refs[d][b] = new[b]
            return carry

        lax.fori_loop(0, SCAN_CHUNK, step, 0, unroll=SCAN_UNROLL)
        for d, (reverse, inclusive) in enumerate(SCAN_DIRS):
            final_refs[d][...] = s_refs[d][...]
            if inclusive:
                assert not reverse
                last = SCAN_CHUNK - 1
                late_ref[SCAN_CHUNK] = _read(_mxu_operands([s_refs[d][b] for b in rng]),
                                             [rows[d][0][b, last:last + 1, :] for b in rng], head_v)
                _blocks_to_rows(late_ref, 1, y_refs[d], bsz)
            else:
                _blocks_to_rows(raw_ref, 0, y_refs[d], bsz)

    specs = [_scan_specs(bsz, orders[d]) for d in range(2)]
    state = pltpu.VMEM((bsz, RW_N, RW_W), f32)
    late = pltpu.VMEM((SCAN_CHUNK + 1, RW_HEADS * bsz, RW_N * bsz), f32)
    raw = pltpu.VMEM((SCAN_CHUNK, RW_HEADS * bsz, RW_N * bsz), f32)
    final_spec = pl.BlockSpec((bsz, RW_N, RW_W), lambda i: (0, 0, 0))
    return pl.pallas_call(
        body, grid=(n_ch,),
        in_specs=[specs[d][0](col) for d in range(2) for _, col in rows_in[d]] + [specs[0][1], specs[1][1]]
        + _const_specs(consts),
        out_specs=[specs[0][0](), specs[1][0](), specs[0][2], specs[1][2], final_spec, final_spec,
                   specs[0][3], specs[1][3]],
        out_shape=[jax.ShapeDtypeStruct((bsz, n_tok, RW_W), f32)] * 2
        + [jax.ShapeDtypeStruct((n_tok, bsz, RW_N, RW_W), MXU_DTYPE)] * 2
        + [jax.ShapeDtypeStruct((bsz, RW_N, RW_W), f32)] * 2
        + [jax.ShapeDtypeStruct((n_tok, RW_HEADS * bsz, RW_N * bsz), f32)] * 2,
        scratch_shapes=[state, state, late, raw],
        compiler_params=_cparams(("arbitrary",)), name=name)(
        *[a for d in range(2) for a, _ in rows_in[d]], v_heads, v_heads, *consts)


def rwkv_scan_bwd(rows_in, v_heads, dy_heads, hists, finals, removed, orders, name):
    bsz, n_tok, _ = rows_in[0][0][0].shape
    n_ch = n_tok // SCAN_CHUNK
    backs = [functools.partial(lambda i, order: order(n_ch - 1 - i), order=orders[d]) for d in range(2)]
    rng = range(bsz)
    consts = _scan_consts(bsz)
    n_out, n_scr = 6, 6

    def body(*refs):
        rows = [refs[:N_ROWS_BWD], refs[N_ROWS_BWD:2 * N_ROWS_BWD]]
        rest = refs[2 * N_ROWS_BWD:]
        v_refs, dy_refs, hist_refs, final_refs, removed_refs = rest[0:2], rest[2:4], rest[4:6], rest[6:8], rest[8:10]
        head_ref, pad_ref, ones_ref = rest[10:13]
        outs = [rest[13:13 + n_out], rest[13 + n_out:13 + 2 * n_out]]
        scr = [rest[13 + 2 * n_out:13 + 2 * n_out + n_scr], rest[13 + 2 * n_out + n_scr:]]
        n_blk = RW_HEADS * bsz
        head_v, pad_v, ones_v = head_ref[...], pad_ref[...], ones_ref[...]
        for d in range(2):
            @pl.when(pl.program_id(0) == 0)
            def _(d=d):
                scr[d][1][...] = jnp.zeros_like(scr[d][1])
                scr[d][0][...] = final_refs[d][...]

        def step_of(j, reverse):
            return j if reverse else SCAN_CHUNK - 1 - j

        for d, (reverse, _) in enumerate(SCAN_DIRS):
            t0 = step_of(0, reverse)
            for b in rng:
                scr[d][3][b] = _outer(dy_refs[d][b, t0], rows[d][0][b, t0:t0 + 1, :], pad_v)

        def bstep(j, carry):
            ts = [step_of(j, reverse) for reverse, _ in SCAN_DIRS]
            reads = [[scr[d][3][b] for b in rng] for d in range(2)]
            dss = []
            for d, (_, inclusive) in enumerate(SCAN_DIRS):
                ds = [scr[d][1][b] for b in rng]
                dss.append([ds[b] + reads[d][b] for b in rng] if inclusive else ds)
            dss_m = [_mxu_operands(dss[d]) for d in range(2)]
            nexts = []
            for d, (reverse, _) in enumerate(SCAN_DIRS):
                t_next = step_of(jnp.minimum(j + 1, SCAN_CHUNK - 1), reverse)
                nexts.append([_outer(dy_refs[d][b, t_next], _row(rows[d][0], b, t_next), pad_v) for b in rng])
            drems = [_removed(dss_m[d], [-_row(rows[d][3], b, ts[d]) for b in rng], ones_v, bsz) for d in range(2)]
            for d in range(2):
                for b in rng:
                    scr[d][3][b] = nexts[d][b]
                both = _read(dss_m[d], [_row(rows[d][4], b, ts[d]) for b in rng], head_v,
                             [-_row(rows[d][3], b, ts[d]) for b in rng])
                scr[d][4][ts[d]] = both[:n_blk]
                scr[d][5][ts[d]] = both[n_blk:]
            for d, (_, inclusive) in enumerate(SCAN_DIRS):
                _, kk_ref, w_ref, _, _ = rows[d]
                _, ds_ref, dsh_ref = scr[d][:3]
                for b in rng:
                    dsh_ref[ts[d], b] = dss[d][b]
                    dsp = dss[d][b] * _row(w_ref, b, ts[d]) + drems[d][b] * _row(kk_ref, b, ts[d])
                    ds_ref[b] = dsp if inclusive else dsp + reads[d][b]
            return carry

        lax.fori_loop(0, SCAN_CHUNK, bstep, 0, unroll=SCAN_UNROLL)

        rsum = lambda z: jnp.sum(z, axis=0, keepdims=True)
        for d, (reverse, inclusive) in enumerate(SCAN_DIRS):
            dr_ref, dkk_ref, dw_ref, db_ref, dkt_ref, dv_ref = outs[d]
            after_ref, _, dsh_ref, _, dv_raw_ref, dremt_ref = scr[d]
            hist_ref, removed_ref = hist_refs[d], removed_refs[d]
            _blocks_to_rows(dv_raw_ref, 0, dv_ref, bsz)
            for t in range(SCAN_CHUNK):
                ts = slice(t, t + 1)
                after = t - 1 if reverse else t + 1
                for b in rng:
                    sp_m, ds = hist_ref[t, b], dsh_ref[t, b]
                    sp = sp_m.astype(f32)
                    if not inclusive:
                        seen = sp_m
                    else:
                        seen = hist_ref[after, b] if 0 <= after < SCAN_CHUNK else after_ref[b]
                    dr_ref[b, ts, :] = _row_from_heads(dy_refs[d][b, t], seen, pad_v)
                    dkt_ref[b, ts, :] = _row_from_heads(v_refs[d][b, t], ds, pad_v)
                    dw_ref[b, ts, :] = rsum(ds * sp)
                    db_ref[b, ts, :] = -_row_from_heads(_own_block(removed_ref[t], b), ds, pad_v)
                    dkk_ref[b, ts, :] = _row_from_heads(_own_block(dremt_ref[t], b), sp_m, pad_v)
            if inclusive:
                first = SCAN_CHUNK - 1 if reverse else 0
                for b in rng:
                    after_ref[b] = hist_ref[first, b].astype(f32)

    specs = [_scan_specs(bsz, backs[d]) for d in range(2)]
    hist = pltpu.VMEM((SCAN_CHUNK, bsz, RW_N, RW_W), f32)
    state = pltpu.VMEM((bsz, RW_N, RW_W), f32)
    final_spec = pl.BlockSpec((bsz, RW_N, RW_W), lambda i: (0, 0, 0))
    raw = pltpu.VMEM((SCAN_CHUNK, RW_HEADS * bsz, RW_N * bsz), f32)
    return pl.pallas_call(
        body, grid=(n_ch,),
        in_specs=[specs[d][0](col) for d in range(2) for _, col in rows_in[d]]
        + [specs[0][1], specs[1][1]] * 2 + [specs[0][2], specs[1][2], final_spec, final_spec, specs[0][3], specs[1][3]]
        + _const_specs(consts),
        out_specs=[specs[d][0]() for d in range(2) for _ in range(n_out)],
        out_shape=[jax.ShapeDtypeStruct((bsz, n_tok, RW_W), f32)] * (2 * n_out),
        scratch_shapes=[state, state, hist, state, raw, raw] * 2,
        compiler_params=_cparams(("arbitrary",)), name=name)(
        *[a for d in range(2) for a, _ in rows_in[d]], v_heads, v_heads, dy_heads, dy_heads, *hists, *finals, *removed, *consts)


MOD_NAMES = ("shift1", "scale1", "gate1", "shift2", "scale2", "gate2")


def _rope_tables(t_ctx, t_x):
    quarter = RET_DH // 4
    pos = jnp.arange(t_x)
    inv = jnp.power(ROPE_BASE, -jnp.arange(0, 2 * quarter, 2, dtype=f32) / (2 * quarter))
    ang_r = (pos // GRID_W).astype(f32)[:, None] * inv[None, :]
    ang_c = (pos % GRID_W).astype(f32)[:, None] * inv[None, :]
    cos = jnp.concatenate([jnp.cos(ang_r)] * 2 + [jnp.cos(ang_c)] * 2, axis=1)
    sin = jnp.concatenate([-jnp.sin(ang_r), jnp.sin(ang_r), -jnp.sin(ang_c), jnp.sin(ang_c)], axis=1)
    cos = jnp.concatenate([jnp.ones((t_ctx, RET_DH), f32), cos], axis=0)
    sin = jnp.concatenate([jnp.zeros((t_ctx, RET_DH), f32), sin], axis=0)
    return cos, sin


def _pad_rows(w, lo, total):
    return jnp.pad(w, ((lo, total - lo - w.shape[0]), (0, 0)))


LATE_WEIGHTS = ("w_out", "w_ff1", "w_ff2")


def layer_step(x, ctx, tgt, mod_x, mod_ctx, wt, late_weights=None, early_grads=None, last_grads=None):
    bsz, t_x, _ = x.shape
    t_c = ctx.shape[1]
    t_all = t_c + t_x
    n_ct, n_xt = t_c // TOK_TILE, t_x // TOK_TILE
    n_t = n_ct + n_xt
    assert t_c % TOK_TILE == 0 and t_x % TOK_TILE == 0 and t_c % RET_CHUNK == 0

    seg = lambda i: (i >= n_ct).astype(jnp.int32)
    seg_first = lambda i: jnp.logical_or(i == 0, i == n_ct)
    seg_last = lambda i: jnp.logical_or(i == n_ct - 1, i == n_t - 1)
    mod_all = {n: jnp.stack([jnp.broadcast_to(mod_ctx[k], (bsz, D_MODEL)), mod_x[:, k]], axis=1)[:, :, None, :]
               for k, n in enumerate(MOD_NAMES)}
    mod_lat = {n: mod_x[:, k][:, None, None, :] for k, n in enumerate(MOD_NAMES)}
    both = lambda n: Seg(mod_all[n], seg, seg_first)
    lat = lambda n: Seg(mod_lat[n], lambda i: 0, lambda i: i == 0)
    flat = lambda a: a.reshape(-1, a.shape[-1])

    def chunk_orders(n_ctx_chunks, n_chunks):
        fwd = lambda i: i
        bwd = lambda i: jnp.where(i < n_ctx_chunks, n_ctx_chunks - 1 - i, n_chunks + n_ctx_chunks - 1 - i)
        return fwd, bwd

    ones64, ones128 = _block_ones(RW_W, RW_N), _block_ones(RET_W, RET_DH)
    cos, sin = _rope_tables(t_c, t_x)
    ld_rows = [jnp.pad(wt["ret_log_decay"][d][None, :], ((0, 0), (0, RET_DH - RET_HEADS))) for d in range(2)]
    w_up_pad = [_pad_rows(wt["rwkv_w_up"][d], 0, LORA_W) for d in range(2)]
    a_up_pad = [_pad_rows(wt["rwkv_a_up"][d], DECAY_LORA, LORA_W) for d in range(2)]
    g_up_pad = _pad_rows(wt["rwkv_g_up"], DECAY_LORA + AAA_LORA, LORA_W)
    row = lambda a, d: a[d][None, :]

    h = jnp.concatenate([ctx, x], axis=1)
    norm1_ins = lambda: [Tiled(h), both("shift1"), both("scale1"), Glob(wt["norm1_g"])]
    (n1,) = ew_forward(fn_norm_mod, "norm1", bsz, n_t, norm1_ins(), [(D_MODEL, MXU_DTYPE)])
    px = matmul(flat(n1), wt["w_in"], "nn", "proj_in").reshape(bsz, t_all, IN_COLS)
    ps = token_shift(px, wt["rwkv_shift_mu"], seg_first, seg_last)

    def prep_ins():
        return [Tiled(ps, RW_W, 1), Tiled(ps, LORA_W, 3 * RW_W // LORA_W),
                Glob(row(wt["rwkv_w0"], 0)), Glob(row(wt["rwkv_w0"], 1)),
                Glob(row(wt["rwkv_a0"], 0)), Glob(row(wt["rwkv_a0"], 1)),
                Glob(w_up_pad[0]), Glob(w_up_pad[1]), Glob(a_up_pad[0]), Glob(a_up_pad[1]), Glob(g_up_pad),
                Glob(wt["rwkv_k_k"]), Glob(wt["rwkv_k_a"]), Glob(ones64)]

    kk, w_f, b_f, kt_f, w_b, b_b, kt_b, g_rw = ew_forward(fn_rwkv_prepare, "rwkv_prepare", bsz, n_t, prep_ins(),
                                                           [(RW_W, f32)] * 8)
    rw_order = chunk_orders(t_c // SCAN_CHUNK, t_all // SCAN_CHUNK)
    ret_order = chunk_orders(t_c // RET_CHUNK, t_all // RET_CHUNK)
    scan_rows = [[(ps, 0), (kk, 0), (w_f, 0), (b_f, 0), (kt_f, 0)], [(ps, 0), (kk, 0), (w_b, 0), (b_b, 0), (kt_b, 0)]]
    v_heads = heads_to_rows(ps[..., 2 * RW_W:3 * RW_W])
    y_f, y_b, *kept_states = rwkv_scan_fwd(scan_rows, v_heads, rw_order, "rwkv_scan_fwd")
    y = [y_f, y_b]
    o, ret_states = [], []
    for d in range(2):
        o_d, st_d = retention_fwd(px, cos, sin, ld_rows[d], ret_order[d], SCAN_DIRS[d][0], f"retention_fwd{d}")
        o.append(o_d), ret_states.append(st_d)

    def merge_ins(toff):
        return [Tiled(o[0], toff=toff), Tiled(o[1], toff=toff), Tiled(px, RET_W, 3, toff),
                Tiled(y[0], toff=toff), Tiled(y[1], toff=toff), Tiled(ps, RW_W, 0, toff), Tiled(kt_f, toff=toff),
                Tiled(ps, RW_W, 2, toff), Tiled(g_rw, toff=toff),
                Glob(wt["rwkv_r_k"]), Glob(wt["rwkv_ln_w"]), Glob(wt["rwkv_ln_b"]), Glob(ones64), Glob(ones128)]

    ret_out, rw_out = ew_forward(fn_merge, "merge_heads", bsz, n_xt, merge_ins(n_ct),
                                 [(RET_W, MXU_DTYPE), (RW_W, MXU_DTYPE)])
    merged = jnp.concatenate([ret_out, rw_out], axis=-1)
    if late_weights is not None:
        wt = {**wt, **late_weights(merged)}
    mix = matmul(flat(merged), wt["w_out"], "nn", "proj_out").reshape(bsz, t_x, D_MODEL)
    resid_ins = lambda: [Tiled(x), Tiled(mix), lat("gate1"), lat("shift2"), lat("scale2"), Glob(wt["norm2_g"])]
    h1, n2 = ew_forward(fn_resid_norm_mod, "resid_norm2", bsz, n_xt, resid_ins(), [(D_MODEL, f32), (D_MODEL, MXU_DTYPE)])
    act = matmul(flat(n2), wt["w_ff1"], "nn", "ff1", MXU_DTYPE, wt["b_ff1"], relu2).reshape(bsz, t_x, D_FF)
    ff = matmul(flat(act), wt["w_ff2"], "nn", "ff2").reshape(bsz, t_x, D_MODEL)

    g = {}
    loss, dh1, dff, dgate2, g["b_ff2"], g["final_g"] = loss_and_grads(
        h1, ff, tgt, mod_lat["gate2"], wt["b_ff2"], wt["final_g"], bsz, n_xt)
    dact = matmul(flat(dff), wt["w_ff2"], "nt", "ff2_dx", MXU_DTYPE).reshape(bsz, t_x, D_FF)
    g["w_ff2"] = matmul(flat(act), flat(dff), "tn", "ff2_dw", MXU_DTYPE)
    du, g["b_ff1"] = relu2_backward(act, dact, "relu2_bwd")
    dn2 = matmul(flat(du), wt["w_ff1"], "nt", "ff1_dx").reshape(bsz, t_x, D_MODEL)
    g["w_ff1"] = matmul(flat(n2), flat(du), "tn", "ff1_dw", MXU_DTYPE)
    dx_res, dmix, dgate1, dshift2, dscale2, g["norm2_g"] = ew_backward(
        fn_resid_norm_mod, "resid_norm2_bwd", bsz, n_xt, resid_ins(), [Tiled(dh1), Tiled(dn2)], [True] * 6,
        {1: MXU_DTYPE})
    dmerged = matmul(flat(dmix), wt["w_out"], "nt", "proj_out_dx").reshape(bsz, t_x, D_MODEL)
    g["w_out"] = matmul(flat(merged), flat(dmix), "tn", "proj_out_dw", MXU_DTYPE)
    if early_grads is not None:
        token = early_grads({n: g.pop(n) for n in LATE_WEIGHTS})
        wt = {**wt, "rwkv_r_k": wt["rwkv_r_k"] + token[:1, :1]}
    (do, dg_ret, dy, dr_m, dkt_m, dv_m, dg_rw, g["rwkv_r_k"], g["rwkv_ln_w"], g["rwkv_ln_b"]) = ew_backward(
        fn_merge, "merge_heads_bwd", bsz, n_xt, merge_ins(0),
        [Tiled(dmerged, RET_W, 0, -n_ct), Tiled(dmerged, RW_W, 1, -n_ct)],
        [True, False, True, True, False, True, True, True, True, True, True, True, False, False], lead=n_ct)

    dqkv, dld = (), []
    for d in range(2):
        *dqkv, dld_d = retention_bwd(do, px, ret_states[d], cos, sin, ld_rows[d], ret_order[d],
                                     SCAN_DIRS[d][0], f"retention_bwd{d}", add_to=dqkv)
        dld.append(dld_d[0, :RET_HEADS])
    g["ret_log_decay"] = jnp.stack(dld)
    (dr_f, dkk_f, dw_f, db_f, dkt_f, dv_f, dr_b, dkk_b, dw_b, db_b, dkt_b, dv_b) = rwkv_scan_bwd(
        scan_rows, v_heads, heads_to_rows(dy), kept_states[:2], kept_states[2:4], kept_states[4:], rw_order, "rwkv_scan_bwd")
    prep_cts = [(dkk_f, dkk_b), dw_f, db_f, (dkt_f, dkt_m), dw_b, db_b, dkt_b, dg_rw]
    (dks, dlora, dw0_f, dw0_b, da0_f, da0_b, dwup_f, dwup_b, daup_f, daup_b, dgup, g["rwkv_k_k"],
     g["rwkv_k_a"]) = ew_backward(fn_rwkv_prepare, "rwkv_prepare_bwd", bsz, n_t, prep_ins(),
                                  [tuple(map(Tiled, c)) if isinstance(c, tuple) else Tiled(c) for c in prep_cts],
                                  [True] * 13 + [False])
    g["rwkv_w0"] = jnp.concatenate([dw0_f, dw0_b], axis=0)
    g["rwkv_a0"] = jnp.concatenate([da0_f, da0_b], axis=0)
    g["rwkv_w_up"] = jnp.stack([dwup_f[:DECAY_LORA], dwup_b[:DECAY_LORA]])
    g["rwkv_a_up"] = jnp.stack([daup_f[DECAY_LORA:DECAY_LORA + AAA_LORA], daup_b[DECAY_LORA:DECAY_LORA + AAA_LORA]])
    g["rwkv_g_up"] = dgup[DECAY_LORA + AAA_LORA:]
    dp_rw, g["rwkv_shift_mu"] = token_shift_bwd([(dr_f, dr_b, dr_m), (dks,), (dv_f, dv_b, dv_m), (dlora,)], px,
                                                 wt["rwkv_shift_mu"], seg_first, seg_last)
    dpx = jnp.concatenate(dqkv + [dg_ret.astype(MXU_DTYPE), dp_rw], axis=-1)
    g["w_in"] = matmul(flat(n1), flat(dpx), "tn", "proj_in_dw", MXU_DTYPE)
    after_start = None
    if last_grads is not None:
        token = last_grads(g.pop("w_in"), {n: g.pop(n) for n in LAST_SHARDED})
        after_start = jnp.zeros((1, D_MODEL), f32) + token[:1, :1]
    dn1 = matmul(flat(dpx), wt["w_in"], "nt", "proj_in_dx", bias=after_start).reshape(bsz, t_all, D_MODEL)
    dh, dshift1, dscale1, g["norm1_g"] = ew_backward(fn_norm_mod, "norm1_bwd", bsz, n_t, norm1_ins(), [Tiled(dn1)],
                                                     [True] * 4)
    grad_x = dh[:, t_c:] + dx_res
    zeros = jnp.zeros((D_MODEL,), f32)
    g["mod_x"] = jnp.stack([dshift1[:, 1, 0], dscale1[:, 1, 0], dgate1[:, 0, 0], dshift2[:, 0, 0], dscale2[:, 0, 0],
                            dgate2[:, 0, 0]], axis=1)
    g["mod_ctx"] = jnp.stack([dshift1[:, 0, 0].sum(0), dscale1[:, 0, 0].sum(0), zeros, zeros, zeros, zeros])
    return loss, grad_x, g


MESH_ID = pl.DeviceIdType.MESH
ALL_PEERS = [(dx, dy, dc) for dx in (0, 1) for dy in (0, 1) for dc in (0, 1)][1:]
CHIP_PEERS = [(1, 0, 0), (0, 1, 0), (1, 1, 0)]
CHIP_SLOTS = (0, 2, 4, 6)


def _mesh_pos():
    return lax.axis_index("x"), lax.axis_index("y"), lax.axis_index("c")


def _device_slot():
    x, y, c = _mesh_pos()
    return 4 * x + 2 * y + c


def sibling_swap(arrs, name):
    n = len(arrs)

    def body(*refs):
        in_refs, out_refs = refs[:n], refs[n:2 * n]
        send_sems, recv_sems = refs[2 * n:]
        x, y, c = _mesh_pos()
        copies = [pltpu.make_async_remote_copy(src_ref=in_refs[a], dst_ref=out_refs[a], send_sem=send_sems.at[a],
                                               recv_sem=recv_sems.at[a], device_id=(x, y, 1 - c),
                                               device_id_type=MESH_ID) for a in range(n)]
        for cp in copies:
            cp.start()
        for cp in copies:
            cp.wait()

    any_spec = pl.BlockSpec(memory_space=pl.ANY)
    res = pl.pallas_call(
        body, in_specs=[any_spec] * n, out_specs=[any_spec] * n,
        out_shape=[jax.ShapeDtypeStruct(a.shape, a.dtype) for a in arrs],
        scratch_shapes=[pltpu.SemaphoreType.DMA((n,)), pltpu.SemaphoreType.DMA((n,))],
        name=name)(*arrs)
    return list(res)


def exchange(arrs, gather, peers, name, by_chip=False, own=True):
    n, n_peers = len(arrs), len(peers)
    n_slots = N_SHARDS if by_chip else N_DEV
    slot = (lambda x, y, c: 2 * x + y) if by_chip else (lambda x, y, c: 4 * x + 2 * y + c)

    def body(*refs):
        in_refs, out_refs = refs[:n], refs[n:2 * n]
        send_sems, recv_sems, local_sems = refs[2 * n:]
        x, y, c = _mesh_pos()
        me = slot(x, y, c)
        copies, locals_ = [], []
        for a in range(n):
            if own:
                mine = in_refs[a] if gather else in_refs[a].at[me]
                loc = pltpu.make_async_copy(mine, out_refs[a].at[me], local_sems.at[a])
                loc.start()
                locals_.append(loc)
            for k, (dx, dy, dc) in enumerate(peers):
                peer = (1 - x if dx else x, 1 - y if dy else y, 1 - c if dc else c)
                src = in_refs[a] if gather else in_refs[a].at[slot(*peer)]
                sem = a * n_peers + k
                cp = pltpu.make_async_remote_copy(src_ref=src, dst_ref=out_refs[a].at[me], send_sem=send_sems.at[sem],
                                                  recv_sem=recv_sems.at[sem], device_id=peer, device_id_type=MESH_ID)
                cp.start()
                copies.append(cp)
        for cp in copies:
            cp.wait()
        for loc in locals_:
            loc.wait()

    any_spec = pl.BlockSpec(memory_space=pl.ANY)
    out_shape = [jax.ShapeDtypeStruct((n_slots,) + (a.shape if gather else a.shape[1:]), a.dtype) for a in arrs]
    n_sems = n * n_peers
    res = pl.pallas_call(
        body, in_specs=[any_spec] * n, out_specs=[any_spec] * n, out_shape=out_shape,
        scratch_shapes=[pltpu.SemaphoreType.DMA((n_sems,)), pltpu.SemaphoreType.DMA((n_sems,)),
                        pltpu.SemaphoreType.DMA((n,))],
        name=name)(*arrs)
    return list(res)


HBM_SPEC = pl.BlockSpec(memory_space=pltpu.HBM)
SEM_SPEC = pl.BlockSpec(memory_space=pltpu.SEMAPHORE)
DATAFLOW = pltpu.SideEffectType.DATAFLOW_SIDE_EFFECTING


def _peer_copies(src_refs, land_refs, send_sems, recv_sems, gather):
    x, y, c = _mesh_pos()
    me = 4 * x + 2 * y + c
    copies = []
    for a, (src_ref, land_ref) in enumerate(zip(src_refs, land_refs)):
        for k, (dx, dy, dc) in enumerate(ALL_PEERS):
            peer = (1 - x if dx else x, 1 - y if dy else y, 1 - c if dc else c)
            src = src_ref if gather else src_ref.at[4 * peer[0] + 2 * peer[1] + peer[2]]
            sem = a * len(ALL_PEERS) + k
            copies.append(pltpu.make_async_remote_copy(src_ref=src, dst_ref=land_ref.at[me], send_sem=send_sems.at[sem],
                                                       recv_sem=recv_sems.at[sem], device_id=peer,
                                                       device_id_type=MESH_ID))
    return copies


def exchange_start(arrs, gather, name):
    n = len(arrs)
    lands = [lax.empty((N_DEV,) + (a.shape if gather else a.shape[1:]), a.dtype) for a in arrs]

    def body(*refs):
        for cp in _peer_copies(refs[:n], refs[n:2 * n], refs[2 * n], refs[2 * n + 1], gather):
            cp.start()
        refs[-1][...] = jnp.zeros_like(refs[-1])

    sems = pltpu.SemaphoreType.DMA((n * len(ALL_PEERS),))
    hbm = [pltpu.HBM(a.shape, a.dtype) for a in arrs + lands]
    res = pl.pallas_call(
        body, name=name, out_shape=(sems, sems, *hbm, jax.ShapeDtypeStruct((8, 128), f32)),
        in_specs=[HBM_SPEC] * (2 * n),
        out_specs=(SEM_SPEC, SEM_SPEC, *[HBM_SPEC] * (2 * n), pl.BlockSpec(memory_space=pltpu.VMEM)),
        input_output_aliases={i: 2 + i for i in range(2 * n)},
        compiler_params=pltpu.CompilerParams(has_side_effects=DATAFLOW))(
        *[pltpu.with_memory_space_constraint(a, pltpu.HBM) for a in arrs + lands])
    return res[0], res[1], list(res[2:2 + n]), list(res[2 + n:2 + 2 * n]), res[-1]


def exchange_wait(started, after, gather, name):
    send_sems, recv_sems, srcs, lands, _ = started
    n = len(srcs)

    def body(*refs):
        for cp in _peer_copies(refs[:n], refs[n:2 * n], refs[2 * n], refs[2 * n + 1], gather):
            cp.wait_send()
            cp.wait_recv()

    res = pl.pallas_call(
        body, name=name, out_shape=tuple(pltpu.HBM(a.shape, a.dtype) for a in srcs + lands),
        in_specs=[HBM_SPEC] * (2 * n) + [SEM_SPEC, SEM_SPEC, pl.BlockSpec(memory_space=pl.ANY)],
        out_specs=tuple([HBM_SPEC] * (2 * n)), input_output_aliases={i: i for i in range(2 * n)},
        compiler_params=pltpu.CompilerParams(has_side_effects=DATAFLOW))(*srcs, *lands, send_sems, recv_sems, after)
    return list(res[n:])


def gather_two_level(arrs, name):
    n = len(arrs)
    per = 7

    def body(*refs):
        in_refs, out_refs = refs[:n], refs[n:2 * n]
        send_sems, recv_sems = refs[2 * n:]
        x, y, c = _mesh_pos()
        me, sibling = (x, y, c), (x, y, 1 - c)
        chips = [(1 - x, y), (x, 1 - y), (1 - x, 1 - y)]

        def copy(a, k, block, to, src=None):
            rows = out_refs[a].at[4 * block[0] + 2 * block[1] + block[2]]
            return pltpu.make_async_remote_copy(src_ref=rows if src is None else src, dst_ref=rows,
                                                send_sem=send_sems.at[a * per + k], recv_sem=recv_sems.at[a * per + k],
                                                device_id=to, device_id_type=MESH_ID)

        first, passed = [], []
        for a in range(n):
            first.append(copy(a, 0, me, sibling, src=in_refs[a]))
            first += [copy(a, 1 + j, me, (*chip, c), src=in_refs[a]) for j, chip in enumerate(chips)]
        for cp in first:
            cp.start()
        for a in range(n):
            for j, chip in enumerate(chips):
                copy(a, 1 + j, (*chip, c), me).wait_recv()
                fwd = copy(a, 4 + j, (*chip, c), sibling)
                fwd.start()
                passed.append(fwd)
        for a in range(n):
            copy(a, 0, sibling, me).wait_recv()
            for j, chip in enumerate(chips):
                copy(a, 4 + j, (*chip, 1 - c), me).wait_recv()
        for cp in first + passed:
            cp.wait_send()

    any_spec = pl.BlockSpec(memory_space=pl.ANY)
    res = pl.pallas_call(
        body, in_specs=[any_spec] * n, out_specs=[any_spec] * n,
        out_shape=[jax.ShapeDtypeStruct((N_DEV,) + a.shape, a.dtype) for a in arrs],
        scratch_shapes=[pltpu.SemaphoreType.DMA((n * per,)), pltpu.SemaphoreType.DMA((n * per,))],
        name=name)(*arrs)
    return list(res)


def sum_slots(parts, slots, name):
    _, r, c = parts.shape
    tr = r
    for cand in (512, 256, 128, 64, 32, 16, 8):
        if r % cand == 0 and cand * c * 4 * len(slots) <= 8 * 1024 * 1024:
            tr = cand
            break

    def body(p_ref, o_ref):
        acc = p_ref[slots[0]].astype(f32)
        for s in slots[1:]:
            acc = acc + p_ref[s].astype(f32)
        o_ref[...] = acc

    return pl.pallas_call(body, grid=(r // tr,), in_specs=[pl.BlockSpec((parts.shape[0], tr, c), lambda i: (0, i, 0))],
                          out_specs=pl.BlockSpec((tr, c), lambda i: (i, 0)),
                          out_shape=jax.ShapeDtypeStruct((r, c), f32),
                          compiler_params=_cparams(("parallel",)), name=name)(parts)


def column_sum(a, name):
    def body(a_ref, o_ref):
        o_ref[...] = jnp.sum(a_ref[...], axis=0, keepdims=True)

    return pl.pallas_call(body, out_shape=jax.ShapeDtypeStruct((1, a.shape[1]), f32), name=name)(a)


def adamw(w, g, m, v, name):
    r, c = w.shape
    tr = r
    for cand in (256, 128, 64, 32, 16, 8):
        if r % cand == 0:
            tr = cand
            break

    def body(w_ref, g_ref, m_ref, v_ref, d_ref, mo_ref, vo_ref):
        gv = g_ref[...]
        m_new = ADAM_B1 * m_ref[...] + (1.0 - ADAM_B1) * gv
        v_new = ADAM_B2 * v_ref[...] + (1.0 - ADAM_B2) * jnp.square(gv)
        m_hat = m_new / (1.0 - ADAM_B1 ** ADAM_STEP)
        v_hat = v_new / (1.0 - ADAM_B2 ** ADAM_STEP)
        d_ref[...] = -ADAM_LR * (m_hat / (jnp.sqrt(v_hat) + ADAM_EPS) + ADAM_WD * w_ref[...])
        mo_ref[...] = m_new
        vo_ref[...] = v_new

    spec = pl.BlockSpec((tr, c), lambda i: (i, 0))
    return pl.pallas_call(body, grid=(r // tr,), in_specs=[spec] * 4, out_specs=[spec] * 3,
                          out_shape=[jax.ShapeDtypeStruct((r, c), f32)] * 3,
                          compiler_params=_cparams(("parallel",)), name=name)(w, g, m, v)


def adaln_fwd(c_rows, w, b):
    def body(c_ref, w_ref, b_ref, o_ref):
        cv = c_ref[...]
        o_ref[...] = _mxu_dot(cv * jax.nn.sigmoid(cv), w_ref[...]) + b_ref[...]

    return pl.pallas_call(body, out_shape=jax.ShapeDtypeStruct((c_rows.shape[0], w.shape[1]), f32),
                          compiler_params=pltpu.CompilerParams(vmem_limit_bytes=VMEM_LIMIT), name="adaln_fwd")(c_rows, w, b)


def adaln_bwd(c_rows, dm, w):
    def body(c_ref, dm_ref, w_ref, gw_ref, ds_ref):
        cv = c_ref[...]
        gw_ref[...] = _dg(cv * jax.nn.sigmoid(cv), dm_ref[...], 0, 0)
        ds_ref[...] = _dg(dm_ref[...], w_ref[...], 1, 1)

    return pl.pallas_call(body, out_shape=[jax.ShapeDtypeStruct(w.shape, f32),
                                           jax.ShapeDtypeStruct(c_rows.shape, f32)],
                          compiler_params=pltpu.CompilerParams(vmem_limit_bytes=VMEM_LIMIT), name="adaln_bwd")(c_rows, dm, w)


def c_ctx_grad(parts, c_ctx_row):
    def body(p_ref, c_ref, o_ref):
        total = p_ref[0, 0:1, :]
        for s in range(1, N_SHARDS):
            total = total + p_ref[s, 0:1, :]
        _, vjp = jax.vjp(jax.nn.silu, c_ref[...])
        o_ref[...] = vjp(total)[0]

    return pl.pallas_call(body, out_shape=jax.ShapeDtypeStruct((1, D_MODEL), f32), name="c_ctx_grad")(parts, c_ctx_row)


PACK_W = 1024
PACK_ROWS = 8


def _pack(arrs):
    pieces, layout, r0 = [], [], 0
    for a in arrs:
        size = math.prod(a.shape)
        rows = -(-size // (PACK_W * PACK_ROWS)) * PACK_ROWS
        pieces.append(jnp.pad(a.reshape(-1).astype(f32), (0, rows * PACK_W - size)).reshape(rows, PACK_W))
        layout.append((r0, rows, a.shape))
        r0 += rows
    return jnp.concatenate(pieces, axis=0), layout


def _unpack(pack, layout, lead=()):
    n_lead = len(lead)
    outs = []
    for r0, rows, shape in layout:
        piece = pack[(slice(None),) * n_lead + (slice(r0, r0 + rows),)].reshape(lead + (-1,))
        outs.append(piece[..., :math.prod(shape)].reshape(lead + tuple(shape)))
    return outs


W_NAMES = ("c_ctx", "w_ada", "b_ada", "norm1_g", "norm2_g", "w_in", "ret_log_decay", "rwkv_shift_mu", "rwkv_w0",
           "rwkv_w_up", "rwkv_a0", "rwkv_a_up", "rwkv_g_up", "rwkv_k_k", "rwkv_k_a", "rwkv_r_k", "rwkv_ln_w",
           "rwkv_ln_b", "w_out", "w_ff1", "b_ff1", "w_ff2", "b_ff2", "final_g")
COL_SHARDED = ("w_in", "w_ff1")
ROW_SHARDED = ("w_out", "w_ff2")
LAST_SHARDED = ("rwkv_shift_mu", "rwkv_w0", "rwkv_w_up", "rwkv_a0", "rwkv_a_up", "rwkv_g_up")
REPLICATED = ("c_ctx", "b_ada", "norm1_g", "norm2_g", "ret_log_decay", "rwkv_k_k", "rwkv_k_a", "rwkv_r_k",
              "rwkv_ln_w", "rwkv_ln_b", "b_ff1", "b_ff2", "final_g")
N_SHARDS = 4


def _train_step(a):
    x, c, ctx, tgt = a["x"], a["c"], a["ctx"], a["loss_target"]
    bsz = x.shape[0]
    mx, my, mc = _mesh_pos()
    shard = 2 * mx + my
    dev = _device_slot()

    (c_all,) = exchange([jnp.pad(c, ((0, PACK_ROWS - bsz), (0, 0)))], True, ALL_PEERS, "gather_c")
    n_ex = N_DEV * bsz
    c_rows = jnp.concatenate([c_all[:, :bsz].reshape(n_ex, D_MODEL), a["c_ctx"][None, :],
                              jnp.zeros((PACK_ROWS - 1, D_MODEL), f32)], axis=0)
    ada_cols = a["w_ada"].shape[-1]
    b_ada_cols = lax.dynamic_slice_in_dim(a["b_ada"], shard * ada_cols, ada_cols, axis=1)
    mod_cols = adaln_fwd(c_rows, a["w_ada"][0], b_ada_cols)

    def own_half(n):
        w = a[n][0].astype(MXU_DTYPE)
        return lax.dynamic_slice_in_dim(w, mc * (w.shape[0] // 2), w.shape[0] // 2, axis=0)

    def whole_weight(n, gth, own):
        per_chip = lax.dynamic_update_index_in_dim(gth, own, dev, 0).reshape(N_SHARDS, -1, gth.shape[-1])
        return (per_chip.transpose(1, 0, 2).reshape(per_chip.shape[1], -1) if n in COL_SHARDED
                else per_chip.reshape(-1, per_chip.shape[-1]))

    small_pack, small_layout = _pack([a[n][0] for n in LAST_SHARDED])
    own_blocks = [mod_cols, own_half("w_in"), small_pack]
    gathered = gather_two_level(own_blocks, "gather_weights")
    late_own = [own_half(n) for n in LATE_WEIGHTS]
    late_started = exchange_start(late_own, True, "gather_late_start")
    mod_own = lax.dynamic_update_index_in_dim(gathered[0], mod_cols, dev, 0)
    mod_all = jnp.stack([mod_own[s] for s in CHIP_SLOTS], axis=1).reshape(c_rows.shape[0], -1)
    mod_all = mod_all + late_started[-1][0, 0]
    mod_x = lax.dynamic_slice_in_dim(mod_all, dev * bsz, bsz, axis=0).reshape(bsz, 6, D_MODEL)
    mod_ctx = mod_all[n_ex].reshape(6, D_MODEL)
    wt = {"w_in": whole_weight("w_in", gathered[1], own_blocks[1])}

    def late_weights(after):
        lands = exchange_wait(late_started, after, True, "gather_late_wait")
        return {n: whole_weight(n, land, own) for n, land, own in zip(LATE_WEIGHTS, lands, late_own)}

    def grad_blocks(n, gw):
        if n in COL_SHARDED:
            gw = gw.reshape(gw.shape[0], N_SHARDS, -1).transpose(1, 0, 2)
        return gw.reshape(N_DEV, -1, gw.shape[-1]).astype(MXU_DTYPE)

    late_sent, last_sent = {}, {}

    def early_grads(late_g):
        late_sent["blocks"] = [grad_blocks(n, late_g[n]) for n in LATE_WEIGHTS]
        late_sent["started"] = exchange_start(late_sent["blocks"], False, "scatter_late_start")
        return late_sent["started"][-1]

    def last_grads(g_w_in, g_small):
        shard_packs = []
        for s in range(N_SHARDS):
            pieces_s = [lax.slice_in_dim(g_small[n], s * a[n].shape[-1], (s + 1) * a[n].shape[-1],
                                         axis=g_small[n].ndim - 1) for n in LAST_SHARDED]
            pack_s, last_sent["layout"] = _pack(pieces_s)
            shard_packs.append(jnp.pad(pack_s, ((0, -pack_s.shape[0] % (2 * PACK_ROWS)), (0, 0))))
        last_sent["blocks"] = [grad_blocks("w_in", g_w_in), jnp.stack(shard_packs).reshape(N_DEV, -1, PACK_W)]
        last_sent["started"] = exchange_start(last_sent["blocks"], False, "scatter_last_start")
        return last_sent["started"][-1]

    small_own = lax.dynamic_update_index_in_dim(gathered[2], small_pack, dev, 0)
    small_by_chip = _unpack(jnp.stack([small_own[s] for s in CHIP_SLOTS]), small_layout, (N_SHARDS,))
    for n, parts in zip(LAST_SHARDED, small_by_chip):
        wt[n] = jnp.concatenate([parts[s] for s in range(N_SHARDS)], axis=-1)
    for n in ("norm1_g", "norm2_g", "rwkv_k_k", "rwkv_k_a", "rwkv_r_k", "rwkv_ln_w", "rwkv_ln_b", "b_ff1", "b_ff2"):
        wt[n] = a[n]
    wt["ret_log_decay"] = a["ret_log_decay"][0]
    wt["final_g"] = a["final_g"][None, :]

    loss, grad_x, g = layer_step(x, ctx, tgt, mod_x, mod_ctx, wt, late_weights, early_grads, last_grads)

    small_names = [n for n in REPLICATED if n not in ("c_ctx", "b_ada")]
    g_pack, g_layout = _pack([jnp.pad(loss, ((0, 0), (0, PACK_W - loss.shape[1])))] + [g[n] for n in small_names]
                             + [g["mod_x"], g["mod_ctx"]])
    (g_packs,) = gather_two_level([g_pack], "gather_small_grads")
    g_packs = lax.dynamic_update_index_in_dim(g_packs, g_pack, dev, 0)
    g_sum = _unpack(sum_slots(g_packs, tuple(range(N_DEV)), "sum_small_grads"), g_layout)
    loss_total = g_sum[0][0, 0]
    grads = dict(zip(small_names, g_sum[1:1 + len(small_names)]))
    dmod_ctx = g_sum[-1].reshape(1, -1)
    dmod_x = _unpack(g_packs, g_layout, (N_DEV,))[-2].reshape(n_ex, -1)
    dmod = jnp.concatenate([dmod_x, dmod_ctx, jnp.zeros((PACK_ROWS - 1, dmod_x.shape[1]), f32)], axis=0)
    grads["b_ada"] = column_sum(dmod, "b_ada_grad")
    dmod_cols = lax.dynamic_slice_in_dim(dmod, shard * ada_cols, ada_cols, axis=1)
    grads["w_ada"], dsilu = adaln_bwd(c_rows, dmod_cols, a["w_ada"][0])

    dsilu_rows = jnp.broadcast_to(jnp.pad(dsilu[n_ex:n_ex + 1], ((0, PACK_ROWS - 1), (0, 0)))[None],
                                  (N_SHARDS, PACK_ROWS, D_MODEL))
    (shares,) = exchange([dsilu_rows], False, CHIP_PEERS, "share_c_ctx_grad", by_chip=True, own=False)
    shares = lax.dynamic_update_index_in_dim(shares, dsilu_rows[0], shard, 0)
    grads["c_ctx"] = c_ctx_grad(shares, a["c_ctx"][None, :])

    scattered, half_sums = ("w_in", "small_shards") + LATE_WEIGHTS, []
    for sent, wait_name, after in ((last_sent, "scatter_last_wait", grads["c_ctx"]),
                                   (late_sent, "scatter_late_wait", grads["c_ctx"])):
        for land, block in zip(exchange_wait(sent["started"], after, False, wait_name), sent["blocks"]):
            land = lax.dynamic_update_index_in_dim(land, lax.dynamic_index_in_dim(block, dev, 0, keepdims=False), dev, 0)
            half_sums.append(sum_slots(land, tuple(range(N_DEV)), f"sum_{scattered[len(half_sums)]}"))
    other_halves = sibling_swap(half_sums, "swap_halves")
    for n, mine, other in zip(scattered, half_sums, other_halves):
        grads[n] = jnp.concatenate([jnp.where(mc == 0, mine, other), jnp.where(mc == 0, other, mine)], axis=0)
    grads.update(zip(LAST_SHARDED, _unpack(grads.pop("small_shards"), last_sent["layout"])))

    out_g, out_d, out_m, out_v = {}, {}, {}, {}
    for n in ("w_ada",) + COL_SHARDED + ROW_SHARDED:
        out_g[n] = grads[n].reshape(a[n].shape)
        two_d = lambda z: z.reshape(-1, z.shape[-1])
        d, m, v = adamw(two_d(a[n]), two_d(out_g[n]), two_d(a["m_" + n]), two_d(a["v_" + n]), f"adamw_{n}")
        out_d[n], out_m[n], out_v[n] = d.reshape(a[n].shape), m.reshape(a[n].shape), v.reshape(a[n].shape)
    rest = REPLICATED + LAST_SHARDED
    for n in rest:
        out_g[n] = grads[n].reshape(a[n].shape)
    packs = [_pack([src[n] for n in rest])[0] for src in
             ({n: a[n] for n in rest}, out_g, {n: a["m_" + n] for n in rest}, {n: a["v_" + n] for n in rest})]
    _, rest_layout = _pack([a[n] for n in rest])
    for dst, pack in zip((out_d, out_m, out_v), adamw(*packs, "adamw_small")):
        dst.update(zip(rest, _unpack(pack, rest_layout)))
    return (loss_total, grad_x, *[out_g[n] for n in W_NAMES], *[out_d[n] for n in W_NAMES],
            *[out_m[n] for n in W_NAMES], *[out_v[n] for n in W_NAMES])


def kernel(x, c, ctx, c_ctx, w_ada, b_ada, norm1_g, norm2_g, w_in, ret_log_decay, rwkv_shift_mu, rwkv_w0, rwkv_w_up, rwkv_a0, rwkv_a_up, rwkv_g_up, rwkv_k_k, rwkv_k_a, rwkv_r_k, rwkv_ln_w, rwkv_ln_b, w_out, w_ff1, b_ff1, w_ff2, b_ff2, final_g, loss_target, m_c_ctx, m_w_ada, m_b_ada, m_norm1_g, m_norm2_g, m_w_in, m_ret_log_decay, m_rwkv_shift_mu, m_rwkv_w0, m_rwkv_w_up, m_rwkv_a0, m_rwkv_a_up, m_rwkv_g_up, m_rwkv_k_k, m_rwkv_k_a, m_rwkv_r_k, m_rwkv_ln_w, m_rwkv_ln_b, m_w_out, m_w_ff1, m_b_ff1, m_w_ff2, m_b_ff2, m_final_g, v_c_ctx, v_w_ada, v_b_ada, v_norm1_g, v_norm2_g, v_w_in, v_ret_log_decay, v_rwkv_shift_mu, v_rwkv_w0, v_rwkv_w_up, v_rwkv_a0, v_rwkv_a_up, v_rwkv_g_up, v_rwkv_k_k, v_rwkv_k_a, v_rwkv_r_k, v_rwkv_ln_w, v_rwkv_ln_b, v_w_out, v_w_ff1, v_b_ff1, v_w_ff2, v_b_ff2, v_final_g):
    return _train_step(dict(locals()))
```
